```python
import jax, jax.numpy as jnp
from jax import lax
import numpy as np

D_MODEL = 1024
BATCH = 32
SEQ = 2048
DEPTH = 1

N_META = 16
D_CONV = D_MODEL
CONV_WIDTH = 3
GLA_HEADS = 4
DK = D_MODEL // 2
DV = D_MODEL
HEAD_K = DK // GLA_HEADS
HEAD_V = DV // GLA_HEADS
GATE_RANK = 16
GATE_NORMALIZER = 16.0
CHUNK = 64
EPS = 1e-6
IN_SPLITS = (D_CONV, D_CONV, D_CONV, D_CONV, DK, DK, DV, DV, GATE_RANK, GATE_RANK, D_MODEL, D_MODEL)
N_IN = sum(IN_SPLITS)

kernel_name = "hybrid_gated_shortconv_bigla_block"


def rms_norm(x, g):
    xf = x.astype(jnp.float32)
    y = xf * lax.rsqrt(jnp.mean(xf * xf, axis=-1, keepdims=True) + EPS)
    return (y * g.astype(jnp.float32)).astype(x.dtype)


def short_conv_centred(u, w):
    half = CONV_WIDTH // 2
    L = u.shape[1]
    up = jnp.pad(u, ((0, 0), (half, half), (0, 0)))
    return sum(up[:, i:i + L] * w[i] for i in range(CONV_WIDTH))


def to_chunks(t, pad_front, pad_back, n_heads, head_dim):
    t = jnp.pad(t, ((0, 0), (pad_front, pad_back), (0, 0)))
    bn, lp, _ = t.shape
    t = t.reshape(bn, lp // CHUNK, CHUNK, n_heads, head_dim)
    return t.transpose(0, 3, 1, 2, 4)


def gla_chunked(q, k, v, g, strict):
    bn, nh, _, c, dk = q.shape
    dv = v.shape[-1]
    b = jnp.cumsum(g.astype(jnp.float32), axis=3)
    q_in = q * jnp.exp(b)
    k_in = k * jnp.exp(-b)
    scores = jnp.einsum('bhncd,bhnjd->bhncj', q_in, k_in)
    mask = jnp.tril(jnp.ones((c, c), dtype=bool), k=-1 if strict else 0)
    scores = jnp.where(mask, scores, 0.0)
    o_intra = jnp.einsum('bhncj,bhnje->bhnce', scores, v)
    b_last = b[..., -1:, :]
    k_dec = k * jnp.exp(b_last - b)
    decay = jnp.exp(b_last[..., 0, :])

    def step(state, xs):
        q_n, k_n, v_n, d_n = xs
        o_n = jnp.einsum('bhcd,bhde->bhce', q_n, state)
        state = state * d_n[..., None] + jnp.einsum('bhcd,bhce->bhde', k_n, v_n)
        return state, o_n

    xs = tuple(jnp.moveaxis(t, 2, 0) for t in (q_in, k_dec, v, decay))
    s0 = jnp.zeros((bn, nh, dk, dv), jnp.float32)
    _, o_inter = lax.scan(step, s0, xs)
    return o_intra + jnp.moveaxis(o_inter, 0, 2)


def hybrid_layer(h, g_pre, w_in, conv_w, w_gate_f, b_gate_f, w_gate_b, b_gate_b,
                 gla_g, w_out_c, w_out_g, w_out, g_post):
    bn, L, _ = h.shape
    pad_front = (-N_META) % CHUNK
    pad_back = (-(L - N_META)) % CHUNK
    u = rms_norm(h, g_pre)
    proj = jnp.einsum('bld,dn->bln', u, w_in)
    split_idx = np.cumsum(IN_SPLITS)[:-1].tolist()
    (c_b, c_c, c_x, c_z, q, k, v, r, lr_f, lr_b, m_a, m_b) = jnp.split(proj, split_idx, axis=-1)

    y_conv = c_b * short_conv_centred(c_c * c_x, conv_w) * jax.nn.silu(c_z)
    p_conv = jnp.einsum('blc,cd->bld', y_conv, w_out_c)

    g_f = jax.nn.log_sigmoid((lr_f @ w_gate_f + b_gate_f).astype(jnp.float32)) / GATE_NORMALIZER
    g_b = jax.nn.log_sigmoid((lr_b @ w_gate_b + b_gate_b).astype(jnp.float32)) / GATE_NORMALIZER
    qc = to_chunks(q * (HEAD_K ** -0.5), pad_front, pad_back, GLA_HEADS, HEAD_K)
    kc = to_chunks(k, pad_front, pad_back, GLA_HEADS, HEAD_K)
    vc = to_chunks(v, pad_front, pad_back, GLA_HEADS, HEAD_V)
    gfc = to_chunks(g_f, pad_front, pad_back, GLA_HEADS, HEAD_K)
    gbc = to_chunks(g_b, pad_front, pad_back, GLA_HEADS, HEAD_K)
    rev = lambda t: jnp.flip(t, axis=(2, 3))
    o_f = gla_chunked(qc, kc, vc, gfc, strict=False)
    o_b = rev(gla_chunked(rev(qc), rev(kc), rev(vc), rev(gbc), strict=True))
    o = (o_f + o_b).transpose(0, 2, 3, 1, 4).reshape(bn, -1, GLA_HEADS, HEAD_V)
    o = o[:, pad_front:pad_front + L]
    o = rms_norm(o, gla_g).reshape(bn, L, DV).astype(h.dtype)
    y_gla = o * jax.nn.silu(r)
    p_gla = jnp.einsum('blc,cd->bld', y_gla, w_out_g)

    merged = jax.nn.sigmoid(m_a) * p_conv + jax.nn.sigmoid(m_b) * p_gla
    out = jnp.einsum('bld,de->ble', merged, w_out)
    return h + rms_norm(out, g_post)


def _fwd_setup_inputs(seed: int = 0) -> dict:
    key = jax.random.key(seed)
    ks = jax.random.split(key, 16)
    nrm = lambda k, shape, scale: jax.random.normal(k, shape, jnp.float32) * scale
    return {
        "x": nrm(ks[0], (BATCH, SEQ, D_MODEL), 1.0),
        "meta_tokens": nrm(ks[1], (N_META, D_MODEL), 1.0),
        "norm_pre": 1.0 + nrm(ks[2], (DEPTH, D_MODEL), 0.05),
        "w_in": nrm(ks[3], (DEPTH, D_MODEL, N_IN), D_MODEL ** -0.5),
        "conv_w": nrm(ks[4], (DEPTH, CONV_WIDTH, D_CONV), CONV_WIDTH ** -0.5),
        "w_gate_fwd": nrm(ks[5], (DEPTH, GATE_RANK, DK), GATE_RANK ** -0.5),
        "b_gate_fwd": nrm(ks[6], (DEPTH, DK), 0.1),
        "w_gate_bwd": nrm(ks[7], (DEPTH, GATE_RANK, DK), GATE_RANK ** -0.5),
        "b_gate_bwd": nrm(ks[8], (DEPTH, DK), 0.1),
        "gla_norm": 1.0 + nrm(ks[9], (DEPTH, HEAD_V), 0.05),
        "w_out_conv": nrm(ks[10], (DEPTH, D_CONV, D_MODEL), D_CONV ** -0.5),
        "w_out_gla": nrm(ks[11], (DEPTH, DV, D_MODEL), DV ** -0.5),
        "w_merge_out": nrm(ks[12], (DEPTH, D_MODEL, D_MODEL), D_MODEL ** -0.5),
        "norm_post": 1.0 + nrm(ks[13], (DEPTH, D_MODEL), 0.05),
    }


def _fwd_reference(x, meta_tokens, norm_pre, w_in, conv_w, w_gate_fwd, b_gate_fwd, w_gate_bwd,
              b_gate_bwd, gla_norm, w_out_conv, w_out_gla, w_merge_out, norm_post):
    bn = x.shape[0]
    meta = jnp.broadcast_to(meta_tokens[None].astype(x.dtype), (bn, N_META, D_MODEL))
    h = jnp.concatenate([meta, x], axis=1)
    for l in range(DEPTH):
        h = hybrid_layer(h, norm_pre[l], w_in[l], conv_w[l], w_gate_fwd[l], b_gate_fwd[l],
                         w_gate_bwd[l], b_gate_bwd[l], gla_norm[l], w_out_conv[l],
                         w_out_gla[l], w_merge_out[l], norm_post[l])
    return h[:, N_META:]


import jax as _jax
import jax.numpy as _jnp

TWIN_FORMAT = 'train_step'
FWD_PARAMS = ['x', 'meta_tokens', 'norm_pre', 'w_in', 'conv_w', 'w_gate_fwd', 'b_gate_fwd', 'w_gate_bwd', 'b_gate_bwd', 'gla_norm', 'w_out_conv', 'w_out_gla', 'w_merge_out', 'norm_post']
TWIN_WEIGHTS = ['meta_tokens', 'norm_pre', 'w_in', 'conv_w', 'w_gate_fwd', 'b_gate_fwd', 'w_gate_bwd', 'b_gate_bwd', 'gla_norm', 'w_out_conv', 'w_out_gla', 'w_merge_out', 'norm_post']
TWIN_DIFF_INPUT = 'x'
TWIN_INPUTS = ['x', 'meta_tokens', 'norm_pre', 'w_in', 'conv_w', 'w_gate_fwd', 'b_gate_fwd', 'w_gate_bwd', 'b_gate_bwd', 'gla_norm', 'w_out_conv', 'w_out_gla', 'w_merge_out', 'norm_post', 'loss_target', 'm_meta_tokens', 'm_norm_pre', 'm_w_in', 'm_conv_w', 'm_w_gate_fwd', 'm_b_gate_fwd', 'm_w_gate_bwd', 'm_b_gate_bwd', 'm_gla_norm', 'm_w_out_conv', 'm_w_out_gla', 'm_w_merge_out', 'm_norm_post', 'v_meta_tokens', 'v_norm_pre', 'v_w_in', 'v_conv_w', 'v_w_gate_fwd', 'v_b_gate_fwd', 'v_w_gate_bwd', 'v_b_gate_bwd', 'v_gla_norm', 'v_w_out_conv', 'v_w_out_gla', 'v_w_merge_out', 'v_norm_post']
TWIN_OUTPUTS = ['loss', 'grad_x', 'grad_meta_tokens', 'grad_norm_pre', 'grad_w_in', 'grad_conv_w', 'grad_w_gate_fwd', 'grad_b_gate_fwd', 'grad_w_gate_bwd', 'grad_b_gate_bwd', 'grad_gla_norm', 'grad_w_out_conv', 'grad_w_out_gla', 'grad_w_merge_out', 'grad_norm_post', 'delta_meta_tokens', 'delta_norm_pre', 'delta_w_in', 'delta_conv_w', 'delta_w_gate_fwd', 'delta_b_gate_fwd', 'delta_w_gate_bwd', 'delta_b_gate_bwd', 'delta_gla_norm', 'delta_w_out_conv', 'delta_w_out_gla', 'delta_w_merge_out', 'delta_norm_post', 'new_m_meta_tokens', 'new_m_norm_pre', 'new_m_w_in', 'new_m_conv_w', 'new_m_w_gate_fwd', 'new_m_b_gate_fwd', 'new_m_w_gate_bwd', 'new_m_b_gate_bwd', 'new_m_gla_norm', 'new_m_w_out_conv', 'new_m_w_out_gla', 'new_m_w_merge_out', 'new_m_norm_post', 'new_v_meta_tokens', 'new_v_norm_pre', 'new_v_w_in', 'new_v_conv_w', 'new_v_w_gate_fwd', 'new_v_b_gate_fwd', 'new_v_w_gate_bwd', 'new_v_b_gate_bwd', 'new_v_gla_norm', 'new_v_w_out_conv', 'new_v_w_out_gla', 'new_v_w_merge_out', 'new_v_norm_post']
TWIN_LEAF_KINDS = {'loss': 'loss', 'grad_x': 'grad_x', 'grad_meta_tokens': 'grad_w', 'grad_norm_pre': 'grad_w', 'grad_w_in': 'grad_w', 'grad_conv_w': 'grad_w', 'grad_w_gate_fwd': 'grad_w', 'grad_b_gate_fwd': 'grad_w', 'grad_w_gate_bwd': 'grad_w', 'grad_b_gate_bwd': 'grad_w', 'grad_gla_norm': 'grad_w', 'grad_w_out_conv': 'grad_w', 'grad_w_out_gla': 'grad_w', 'grad_w_merge_out': 'grad_w', 'grad_norm_post': 'grad_w', 'delta_meta_tokens': 'delta_w', 'delta_norm_pre': 'delta_w', 'delta_w_in': 'delta_w', 'delta_conv_w': 'delta_w', 'delta_w_gate_fwd': 'delta_w', 'delta_b_gate_fwd': 'delta_w', 'delta_w_gate_bwd': 'delta_w', 'delta_b_gate_bwd': 'delta_w', 'delta_gla_norm': 'delta_w', 'delta_w_out_conv': 'delta_w', 'delta_w_out_gla': 'delta_w', 'delta_w_merge_out': 'delta_w', 'delta_norm_post': 'delta_w', 'new_m_meta_tokens': 'new_m', 'new_m_norm_pre': 'new_m', 'new_m_w_in': 'new_m', 'new_m_conv_w': 'new_m', 'new_m_w_gate_fwd': 'new_m', 'new_m_b_gate_fwd': 'new_m', 'new_m_w_gate_bwd': 'new_m', 'new_m_b_gate_bwd': 'new_m', 'new_m_gla_norm': 'new_m', 'new_m_w_out_conv': 'new_m', 'new_m_w_out_gla': 'new_m', 'new_m_w_merge_out': 'new_m', 'new_m_norm_post': 'new_m', 'new_v_meta_tokens': 'new_v', 'new_v_norm_pre': 'new_v', 'new_v_w_in': 'new_v', 'new_v_conv_w': 'new_v', 'new_v_w_gate_fwd': 'new_v', 'new_v_b_gate_fwd': 'new_v', 'new_v_w_gate_bwd': 'new_v', 'new_v_b_gate_bwd': 'new_v', 'new_v_gla_norm': 'new_v', 'new_v_w_out_conv': 'new_v', 'new_v_w_out_gla': 'new_v', 'new_v_w_merge_out': 'new_v', 'new_v_norm_post': 'new_v'}


def _forward(args):
    return _fwd_reference(*[args[k] for k in FWD_PARAMS])


def _output_shape():
    out = _jax.eval_shape(lambda: _forward(_fwd_setup_inputs(0)))
    return out.shape, out.dtype

N_MICROBATCH = 1
ADAM_LR = 0.001
ADAM_B1 = 0.9
ADAM_B2 = 0.999
ADAM_EPS = 1e-08
ADAM_WD = 0.01
ADAM_STEP = 10
PER_EXAMPLE_BATCH_AXIS = {'x': 0, 'loss_target': 0}
SHARED_INPUTS = []
_WEIGHT_DTYPES = {'meta_tokens': _jnp.float32, 'norm_pre': _jnp.float32, 'w_in': _jnp.float32, 'conv_w': _jnp.float32, 'w_gate_fwd': _jnp.float32, 'b_gate_fwd': _jnp.float32, 'w_gate_bwd': _jnp.float32, 'b_gate_bwd': _jnp.float32, 'gla_norm': _jnp.float32, 'w_out_conv': _jnp.float32, 'w_out_gla': _jnp.float32, 'w_merge_out': _jnp.float32, 'norm_post': _jnp.float32}
MOMENT_SCALE = {'meta_tokens': 1.213864e-02, 'norm_pre': 7.650143e-01, 'w_in': 2.472889e-01, 'conv_w': 2.582877e-01, 'w_gate_fwd': 2.956991e-02, 'b_gate_fwd': 1.410115e-01, 'w_gate_bwd': 3.178755e-02, 'b_gate_bwd': 1.299412e-01, 'gla_norm': 5.962552e-01, 'w_out_conv': 2.495692e-01, 'w_out_gla': 2.619280e-01, 'w_merge_out': 3.737949e-01, 'norm_post': 6.387640e+01}


def _to_microbatches(a, axis):
    t = _jnp.moveaxis(a, axis, 0)
    t = t.reshape((N_MICROBATCH, t.shape[0] // N_MICROBATCH) + t.shape[1:])
    return _jnp.moveaxis(t, 1, axis + 1)


def setup_inputs(seed: int = 0) -> dict:
    inp = _fwd_setup_inputs(seed)
    key = _jax.random.fold_in(_jax.random.key(seed), 7919)
    shape, _ = _output_shape()
    out = dict(inp)
    out["loss_target"] = _jax.random.normal(_jax.random.fold_in(key, 0), shape, _jnp.float32)
    for i, name in enumerate(TWIN_WEIGHTS):
        w = inp[name].astype(_jnp.float32)
        if MOMENT_SCALE is None:
            s = _jnp.sqrt(_jnp.mean(_jnp.square(w)) + 1e-30)
        else:
            s = MOMENT_SCALE[name]
        km, kv = _jax.random.split(_jax.random.fold_in(key, i + 1))
        out[name] = w
        out["m_" + name] = s * _jax.random.normal(km, w.shape, _jnp.float32)
        out["v_" + name] = (s * s) * _jax.random.uniform(kv, w.shape, _jnp.float32, 0.5, 1.5)
    if N_MICROBATCH > 1:
        for name, axis in PER_EXAMPLE_BATCH_AXIS.items():
            out[name] = _to_microbatches(out[name], axis)
    return {'x': out['x'], 'meta_tokens': out['meta_tokens'], 'norm_pre': out['norm_pre'], 'w_in': out['w_in'], 'conv_w': out['conv_w'], 'w_gate_fwd': out['w_gate_fwd'], 'b_gate_fwd': out['b_gate_fwd'], 'w_gate_bwd': out['w_gate_bwd'], 'b_gate_bwd': out['b_gate_bwd'], 'gla_norm': out['gla_norm'], 'w_out_conv': out['w_out_conv'], 'w_out_gla': out['w_out_gla'], 'w_merge_out': out['w_merge_out'], 'norm_post': out['norm_post'], 'loss_target': out['loss_target'], 'm_meta_tokens': out['m_meta_tokens'], 'm_norm_pre': out['m_norm_pre'], 'm_w_in': out['m_w_in'], 'm_conv_w': out['m_conv_w'], 'm_w_gate_fwd': out['m_w_gate_fwd'], 'm_b_gate_fwd': out['m_b_gate_fwd'], 'm_w_gate_bwd': out['m_w_gate_bwd'], 'm_b_gate_bwd': out['m_b_gate_bwd'], 'm_gla_norm': out['m_gla_norm'], 'm_w_out_conv': out['m_w_out_conv'], 'm_w_out_gla': out['m_w_out_gla'], 'm_w_merge_out': out['m_w_merge_out'], 'm_norm_post': out['m_norm_post'], 'v_meta_tokens': out['v_meta_tokens'], 'v_norm_pre': out['v_norm_pre'], 'v_w_in': out['v_w_in'], 'v_conv_w': out['v_conv_w'], 'v_w_gate_fwd': out['v_w_gate_fwd'], 'v_b_gate_fwd': out['v_b_gate_fwd'], 'v_w_gate_bwd': out['v_w_gate_bwd'], 'v_b_gate_bwd': out['v_b_gate_bwd'], 'v_gla_norm': out['v_gla_norm'], 'v_w_out_conv': out['v_w_out_conv'], 'v_w_out_gla': out['v_w_out_gla'], 'v_w_merge_out': out['v_w_merge_out'], 'v_norm_post': out['v_norm_post']}


def _loss(weights, diff, rest, loss_target):
    with _jax.named_scope("forward"):
        args = {**rest, TWIN_DIFF_INPUT: diff, **{k: w.astype(_WEIGHT_DTYPES[k]) for k, w in weights.items()}}
        y = _forward(args)
    with _jax.named_scope("loss_head"):
        err = _jnp.square(y.astype(_jnp.float32) - loss_target)
        return 0.5 * _jnp.sum(_jnp.mean(err, axis=-1)) if err.ndim else 0.5 * err


def _adamw(w, g, m, v):
    m = ADAM_B1 * m + (1.0 - ADAM_B1) * g
    v = ADAM_B2 * v + (1.0 - ADAM_B2) * _jnp.square(g)
    m_hat = m / (1.0 - ADAM_B1 ** ADAM_STEP)
    v_hat = v / (1.0 - ADAM_B2 ** ADAM_STEP)
    delta = -ADAM_LR * (m_hat / (_jnp.sqrt(v_hat) + ADAM_EPS) + ADAM_WD * w)
    return delta, m, v


def reference(x, meta_tokens, norm_pre, w_in, conv_w, w_gate_fwd, b_gate_fwd, w_gate_bwd, b_gate_bwd, gla_norm, w_out_conv, w_out_gla, w_merge_out, norm_post, loss_target, m_meta_tokens, m_norm_pre, m_w_in, m_conv_w, m_w_gate_fwd, m_b_gate_fwd, m_w_gate_bwd, m_b_gate_bwd, m_gla_norm, m_w_out_conv, m_w_out_gla, m_w_merge_out, m_norm_post, v_meta_tokens, v_norm_pre, v_w_in, v_conv_w, v_w_gate_fwd, v_b_gate_fwd, v_w_gate_bwd, v_b_gate_bwd, v_gla_norm, v_w_out_conv, v_w_out_gla, v_w_merge_out, v_norm_post):
    given = dict(x=x, meta_tokens=meta_tokens, norm_pre=norm_pre, w_in=w_in, conv_w=conv_w, w_gate_fwd=w_gate_fwd, b_gate_fwd=b_gate_fwd, w_gate_bwd=w_gate_bwd, b_gate_bwd=b_gate_bwd, gla_norm=gla_norm, w_out_conv=w_out_conv, w_out_gla=w_out_gla, w_merge_out=w_merge_out, norm_post=norm_post, loss_target=loss_target, m_meta_tokens=m_meta_tokens, m_norm_pre=m_norm_pre, m_w_in=m_w_in, m_conv_w=m_conv_w, m_w_gate_fwd=m_w_gate_fwd, m_b_gate_fwd=m_b_gate_fwd, m_w_gate_bwd=m_w_gate_bwd, m_b_gate_bwd=m_b_gate_bwd, m_gla_norm=m_gla_norm, m_w_out_conv=m_w_out_conv, m_w_out_gla=m_w_out_gla, m_w_merge_out=m_w_merge_out, m_norm_post=m_norm_post, v_meta_tokens=v_meta_tokens, v_norm_pre=v_norm_pre, v_w_in=v_w_in, v_conv_w=v_conv_w, v_w_gate_fwd=v_w_gate_fwd, v_b_gate_fwd=v_b_gate_fwd, v_w_gate_bwd=v_w_gate_bwd, v_b_gate_bwd=v_b_gate_bwd, v_gla_norm=v_gla_norm, v_w_out_conv=v_w_out_conv, v_w_out_gla=v_w_out_gla, v_w_merge_out=v_w_merge_out, v_norm_post=v_norm_post)
    weights = {n: given[n] for n in TWIN_WEIGHTS}
    shared = {n: given[n] for n in SHARED_INPUTS}
    per_example = {n: given[n] for n in ['x']}
    grad_fn = _jax.value_and_grad(_loss, argnums=(0, 1))

    def one_microbatch(ex, loss_target):
        ex = dict(ex)
        diff = ex.pop(TWIN_DIFF_INPUT)
        return grad_fn(weights, diff, {**shared, **ex}, loss_target)

    if N_MICROBATCH == 1:
        loss, (grad_w, grad_x) = one_microbatch(per_example, given["loss_target"])
    else:
        def body(carry, xs):
            loss_sum, grad_sum = carry
            l_k, (gw_k, gx_k) = one_microbatch(xs[0], xs[1])
            with _jax.named_scope("update"):
                return (loss_sum + l_k, _jax.tree.map(_jnp.add, grad_sum, gw_k)), gx_k

        init = (_jnp.zeros((), _jnp.float32), _jax.tree.map(_jnp.zeros_like, weights))
        (loss, grad_w), grad_x = _jax.lax.scan(body, init, (per_example, given["loss_target"]))
    with _jax.named_scope("update"):
        delta_w, new_m, new_v = {}, {}, {}
        for n in TWIN_WEIGHTS:
            delta_w[n], new_m[n], new_v[n] = _adamw(weights[n], grad_w[n], given["m_" + n], given["v_" + n])
    return (loss, grad_x, *[grad_w[n] for n in TWIN_WEIGHTS], *[delta_w[n] for n in TWIN_WEIGHTS],
            *[new_m[n] for n in TWIN_WEIGHTS], *[new_v[n] for n in TWIN_WEIGHTS])
```

```python
import functools

import jax
import jax.numpy as jnp
import numpy as np
from jax import lax
from jax.experimental import pallas as pl
from jax.experimental.pallas import tpu as pltpu

F32 = jnp.float32
BF16 = jnp.bfloat16
MESH = pl.DeviceIdType.MESH

N_META = 16
CHUNK = 64
HEADS = 4
RANK = 16
LR_LANES = 128
PAD_ROWS = CHUNK - N_META
EPS = 1e-6
GATE_NORMALIZER = 16.0
N_DEV = 8
ADAM_LR, ADAM_B1, ADAM_B2, ADAM_EPS, ADAM_WD, ADAM_STEP = 0.001, 0.9, 0.999, 1e-08, 0.01, 10
VMEM_LIMIT_BYTES = 56 * 1024 * 1024
BLOB_LANES = 512
BIG_ROWS, SMALL_ROWS = 128, 16


class _Dims:
    def __init__(self, bl, s, d):
        self.Bl, self.S, self.D = bl, s, d
        self.TM = 256 if s % 256 == 0 else CHUNK
        self.LP = self.TM + s
        self.T = bl * self.LP
        self.TPS = self.LP // self.TM
        self.NC = self.LP // CHUNK
        self.C0 = (self.TM - CHUNK) // CHUNK
        self.DK, self.DV = d // 2, d
        self.HK, self.HV = self.DK // HEADS, self.DV // HEADS
        self.HW = 2 * self.HK + 2 * self.HV
        self.CW = 256 if d % 256 == 0 and d > 256 else d // 4
        self.NJ = d // self.CW


def _pick(n, target, mult):
    t = min(n, target)
    while t >= mult:
        if n % t == 0 and t % mult == 0:
            return t
        t -= mult
    return n


def _cp(n_axes):
    return pltpu.CompilerParams(dimension_semantics=("arbitrary",) * n_axes, vmem_limit_bytes=VMEM_LIMIT_BYTES)


def _sigmoid(x):
    return 1.0 / (1.0 + jnp.exp(-x))


def _dot(a, b):
    return jnp.dot(a, b, preferred_element_type=F32)


def _dot_nt(a, b):
    return lax.dot_general(a, b, (((1,), (1,)), ((), ())), preferred_element_type=F32)


def _dot_tn(a, b):
    return lax.dot_general(a, b, (((0,), (0,)), ((), ())), preferred_element_type=F32)


def _dot_exact01(m01, x):
    hi = x.astype(BF16)
    lo = (x - hi.astype(F32)).astype(BF16)
    return _dot(m01, hi) + _dot(m01, lo)


def _exchange(gathers, scatters, name):
    arrays = list(gathers) + list(scatters)
    n, ng = len(arrays), len(gathers)

    def body(*refs):
        ins, outs = refs[:n], refs[n:2 * n]
        send_sems, recv_sems, local_sems = refs[2 * n:]
        x, y, c = lax.axis_index("x"), lax.axis_index("y"), lax.axis_index("c")
        me = 4 * x + 2 * y + c
        started = []
        for t in range(n):
            src, dst = ins[t], outs[t]
            own = pltpu.make_async_copy(src if t < ng else src.at[me], dst.at[me], local_sems.at[t])
            own.start()
            started.append(own)
            for k in range(1, N_DEV):
                px = 1 - x if (k >> 2) & 1 else x
                py = 1 - y if (k >> 1) & 1 else y
                pc = 1 - c if k & 1 else c
                peer = 4 * px + 2 * py + pc
                cp = pltpu.make_async_remote_copy(
                    src_ref=src if t < ng else src.at[peer], dst_ref=dst.at[me],
                    send_sem=send_sems.at[t * (N_DEV - 1) + k - 1], recv_sem=recv_sems.at[t * (N_DEV - 1) + k - 1],
                    device_id=(px, py, pc), device_id_type=MESH)
                cp.start()
                started.append(cp)
        for cp in started:
            cp.wait()

    out_shape = [jax.ShapeDtypeStruct((N_DEV,) + a.shape[-2:], a.dtype) for a in arrays]
    any_spec = pl.BlockSpec(memory_space=pl.ANY)
    return pl.pallas_call(
        body, name=name, out_shape=out_shape, in_specs=[any_spec] * n, out_specs=[any_spec] * n,
        scratch_shapes=[pltpu.SemaphoreType.DMA((n * (N_DEV - 1),)), pltpu.SemaphoreType.DMA((n * (N_DEV - 1),)),
                        pltpu.SemaphoreType.DMA((n,))],
        compiler_params=pltpu.CompilerParams(has_side_effects=True),
    )(*arrays)


def _prenorm(x, metapad, g_pre, dm):
    tm, tps, d = dm.TM, dm.TPS, dm.D

    def body(x_ref, mp_ref, g_ref, u_ref):
        j = pl.program_id(0) % tps
        h = jnp.where(j == 0, mp_ref[...], x_ref[0])
        r = lax.rsqrt(jnp.mean(h * h, axis=-1, keepdims=True) + EPS)
        u_ref[...] = (h * r * g_ref[...]).astype(BF16)

    return pl.pallas_call(
        body, name="prenorm", grid=(dm.Bl * tps,),
        in_specs=[pl.BlockSpec((1, tm, d), lambda i: (i // tps, jnp.maximum(i % tps - 1, 0), 0)),
                  pl.BlockSpec((tm, d), lambda i: (0, 0)),
                  pl.BlockSpec((1, d), lambda i: (0, 0))],
        out_specs=pl.BlockSpec((tm, d), lambda i: (i, 0)),
        out_shape=jax.ShapeDtypeStruct((dm.T, d), BF16), compiler_params=_cp(1),
    )(x, metapad, g_pre)


def _matmul(a, b, out_dtype, name, tm=1024, tn=1024):
    m, k = a.shape
    n = b.shape[1]
    tm, tn = _pick(m, tm, 16), _pick(n, tn, 128)

    def body(a_ref, b_ref, o_ref):
        o_ref[...] = _dot(a_ref[...].astype(BF16), b_ref[...]).astype(out_dtype)

    return pl.pallas_call(
        body, name=name, grid=(n // tn, m // tm),
        in_specs=[pl.BlockSpec((tm, k), lambda j, i: (i, 0)), pl.BlockSpec((k, tn), lambda j, i: (0, j))],
        out_specs=pl.BlockSpec((tm, tn), lambda j, i: (i, j)),
        out_shape=jax.ShapeDtypeStruct((m, n), out_dtype), compiler_params=_cp(2),
    )(a, b)


def _matmul_parts(parts, name, tm=512, tk=1024):
    m = parts[0][0].shape[0]
    n = parts[0][1].shape[1]
    tm = _pick(m, tm, 16)
    tks = [_pick(a.shape[1], tk, 128) for a, _ in parts]
    counts = [a.shape[1] // t for (a, _), t in zip(parts, tks)]
    starts = [int(s) for s in np.cumsum([0] + counts[:-1])]
    total = sum(counts)

    def body(*refs):
        acc = refs[-1]
        o_ref = refs[-2]
        k = pl.program_id(1)

        @pl.when(k == 0)
        def _():
            acc[...] = jnp.zeros_like(acc)

        for p in range(len(parts)):
            @pl.when((k >= starts[p]) & (k < starts[p] + counts[p]))
            def _(p=p):
                acc[...] += _dot(refs[2 * p][...].astype(BF16), refs[2 * p + 1][...])

        @pl.when(k == total - 1)
        def _():
            o_ref[...] = acc[...]

    in_specs, operands = [], []
    for (a, b), t, s, cnt in zip(parts, tks, starts, counts):
        in_specs.append(pl.BlockSpec((tm, t), lambda i, k, s=s, cnt=cnt: (i, jnp.clip(k - s, 0, cnt - 1))))
        in_specs.append(pl.BlockSpec((t, n), lambda i, k, s=s, cnt=cnt: (jnp.clip(k - s, 0, cnt - 1), 0)))
        operands += [a, b]
    return pl.pallas_call(
        body, name=name, grid=(m // tm, total), in_specs=in_specs,
        out_specs=pl.BlockSpec((tm, n), lambda i, k: (i, 0)),
        out_shape=jax.ShapeDtypeStruct((m, n), F32), scratch_shapes=[pltpu.VMEM((tm, n), F32)],
        compiler_params=_cp(2),
    )(*operands)


def _matmul_tn(a, b, name, tt=768, tn=1024):
    t, k = a.shape
    n = b.shape[1]
    tt, tn = _pick(t, tt, 16), _pick(n, tn, 128)

    def body(a_ref, b_ref, o_ref):
        p = _dot_tn(a_ref[...].astype(BF16), b_ref[...].astype(BF16))

        @pl.when(pl.program_id(1) == 0)
        def _():
            o_ref[...] = p

        @pl.when(pl.program_id(1) > 0)
        def _():
            o_ref[...] += p

    return pl.pallas_call(
        body, name=name, grid=(n // tn, t // tt),
        in_specs=[pl.BlockSpec((tt, k), lambda j, i: (i, 0)), pl.BlockSpec((tt, tn), lambda j, i: (i, j))],
        out_specs=pl.BlockSpec((k, tn), lambda j, i: (0, j)),
        out_shape=jax.ShapeDtypeStruct((k, n), F32), compiler_params=_cp(2),
    )(a, b)


def _conv_rows(dm):
    return _pick(dm.LP, 256, 16)


def _shifted(m, prev_row, next_row, rows):
    row = lax.broadcasted_iota(jnp.int32, m.shape, 0)
    m_prev = jnp.where(row == 0, prev_row, pltpu.roll(m, 1, 0))
    m_next = jnp.where(row == rows - 1, next_row, pltpu.roll(m, rows - 1, 0))
    return m_prev, m_next


def _conv_fwd(proj_a, conv_w, dm):
    lp, cw, rc = dm.LP, dm.CW, _conv_rows(dm)
    nchunk = lp // rc

    def body(p_ref, w_ref, y_ref):
        w0, w1, w2 = w_ref[0:1, :], w_ref[1:2, :], w_ref[2:3, :]

        def chunk(ci, carry):
            r0 = pl.multiple_of(ci * rc, rc)
            blk = p_ref[pl.ds(r0, rc), :].astype(F32)
            cb, cc, cx, cz = (blk[:, i * cw:(i + 1) * cw] for i in range(4))
            m = cc * cx
            rp = pl.multiple_of(jnp.maximum(r0 - 16, 0), 16)
            rn = pl.multiple_of(jnp.minimum(r0 + rc, lp - 16), 16)
            pv = p_ref[pl.ds(rp, 16), cw:3 * cw].astype(F32)
            nx = p_ref[pl.ds(rn, 16), cw:3 * cw].astype(F32)
            prev_row = jnp.where(ci > 0, pv[15:16, :cw] * pv[15:16, cw:], 0.0)
            next_row = jnp.where(ci < nchunk - 1, nx[0:1, :cw] * nx[0:1, cw:], 0.0)
            m_prev, m_next = _shifted(m, prev_row, next_row, rc)
            s = w0 * m_prev + w1 * m + w2 * m_next
            y_ref[pl.ds(r0, rc), :] = (cb * s * (cz * _sigmoid(cz))).astype(BF16)
            return carry

        lax.fori_loop(0, nchunk, chunk, 0)

    return pl.pallas_call(
        body, name="conv_fwd", grid=(dm.Bl, dm.NJ),
        in_specs=[pl.BlockSpec((lp, 4 * cw), lambda s, j: (s, j)), pl.BlockSpec((3, cw), lambda s, j: (0, j))],
        out_specs=pl.BlockSpec((lp, cw), lambda s, j: (s, j)),
        out_shape=jax.ShapeDtypeStruct((dm.T, dm.D), BF16), compiler_params=_cp(2),
    )(proj_a, conv_w)


def _conv_bwd(proj_a, dy_conv, conv_w, dm):
    lp, cw, rc = dm.LP, dm.CW, _conv_rows(dm)
    nchunk = lp // rc

    def body(p_ref, dy_ref, w_ref, d_ref, gw_ref):
        w0, w1, w2 = w_ref[0:1, :], w_ref[1:2, :], w_ref[2:3, :]

        def ds_of(p4, dy):
            cb, cz = p4[:, :cw], p4[:, 3 * cw:]
            return dy * cb * (cz * _sigmoid(cz))

        def chunk(ci, carry):
            g0, g1, g2 = carry
            r0 = pl.multiple_of(ci * rc, rc)
            blk = p_ref[pl.ds(r0, rc), :].astype(F32)
            dy = dy_ref[pl.ds(r0, rc), :].astype(F32)
            cb, cc, cx, cz = (blk[:, i * cw:(i + 1) * cw] for i in range(4))
            rp = pl.multiple_of(jnp.maximum(r0 - 16, 0), 16)
            rn = pl.multiple_of(jnp.minimum(r0 + rc, lp - 16), 16)
            pv = p_ref[pl.ds(rp, 16), :].astype(F32)[15:16]
            nx = p_ref[pl.ds(rn, 16), :].astype(F32)[0:1]
            dpv = dy_ref[pl.ds(rp, 16), :].astype(F32)[15:16]
            dnx = dy_ref[pl.ds(rn, 16), :].astype(F32)[0:1]
            has_prev, has_next = ci > 0, ci < nchunk - 1
            m = cc * cx
            m_prev, m_next = _shifted(m, jnp.where(has_prev, pv[:, cw:2 * cw] * pv[:, 2 * cw:3 * cw], 0.0),
                                      jnp.where(has_next, nx[:, cw:2 * cw] * nx[:, 2 * cw:3 * cw], 0.0), rc)
            s = w0 * m_prev + w1 * m + w2 * m_next
            sg = _sigmoid(cz)
            silu = cz * sg
            ds = dy * cb * silu
            ds_prev, ds_next = _shifted(ds, jnp.where(has_prev, ds_of(pv, dpv), 0.0),
                                        jnp.where(has_next, ds_of(nx, dnx), 0.0), rc)
            dm_ = w0 * ds_next + w1 * ds + w2 * ds_prev
            d_ref[pl.ds(r0, rc), 0:cw] = (dy * s * silu).astype(BF16)
            d_ref[pl.ds(r0, rc), cw:2 * cw] = (dm_ * cx).astype(BF16)
            d_ref[pl.ds(r0, rc), 2 * cw:3 * cw] = (dm_ * cc).astype(BF16)
            d_ref[pl.ds(r0, rc), 3 * cw:4 * cw] = (dy * cb * s * (sg * (1.0 + cz * (1.0 - sg)))).astype(BF16)
            return (g0 + jnp.sum(ds * m_prev, axis=0, keepdims=True), g1 + jnp.sum(ds * m, axis=0, keepdims=True),
                    g2 + jnp.sum(ds * m_next, axis=0, keepdims=True))

        z = jnp.zeros((1, cw), F32)
        g0, g1, g2 = lax.fori_loop(0, nchunk, chunk, (z, z, z))

        @pl.when(pl.program_id(1) == 0)
        def _():
            gw_ref[...] = jnp.zeros_like(gw_ref)

        gw_ref[0:1, :] += g0
        gw_ref[1:2, :] += g1
        gw_ref[2:3, :] += g2

    return pl.pallas_call(
        body, name="conv_bwd", grid=(dm.NJ, dm.Bl),
        in_specs=[pl.BlockSpec((lp, 4 * cw), lambda j, s: (s, j)), pl.BlockSpec((lp, cw), lambda j, s: (s, j)),
                  pl.BlockSpec((3, cw), lambda j, s: (0, j))],
        out_specs=[pl.BlockSpec((lp, 4 * cw), lambda j, s: (s, j)), pl.BlockSpec((8, cw), lambda j, s: (0, j))],
        out_shape=[jax.ShapeDtypeStruct((dm.T, 4 * dm.D), BF16), jax.ShapeDtypeStruct((8, dm.D), F32)],
        compiler_params=_cp(2),
    )(proj_a, dy_conv, conv_w)


def _chunk_masks():
    ii = lax.broadcasted_iota(jnp.int32, (CHUNK, CHUNK), 0)
    jj = lax.broadcasted_iota(jnp.int32, (CHUNK, CHUNK), 1)
    return jj <= ii, jj > ii, (jj <= ii).astype(BF16), (jj >= ii).astype(BF16)


def _log_gate(lr_rows, w_ref, b_ref, first_chunk, hk):
    z = _dot(lr_rows, w_ref[...]) + b_ref[...]
    e = jnp.exp(-jnp.abs(z))
    g = (jnp.minimum(z, 0.0) - jnp.log(1.0 + e)) * (1.0 / GATE_NORMALIZER)
    dg_dz = jnp.where(z >= 0.0, e, 1.0) / (1.0 + e) * (1.0 / GATE_NORMALIZER)
    row = lax.broadcasted_iota(jnp.int32, (CHUNK, hk), 0)
    pad = first_chunk & (row < PAD_ROWS)
    return jnp.where(pad, 0.0, g), jnp.where(pad, 0.0, dg_dz)


def _gla_fwd(proj_b, lr, wg_f, bg_f, wg_b, bg_b, gla_g, dm):
    lp, hk, hv, nc, c0, hw = dm.LP, dm.HK, dm.HV, dm.NC, dm.C0, dm.HW
    scale = hk ** -0.5

    def body(p_ref, lr_ref, wf_ref, bf_ref, wb_ref, bb_ref, gg_ref, o_ref, y_ref, oacc, st):
        low_incl, up_strict, ones_low, ones_up = _chunk_masks()
        if c0 > 0:
            oacc[0:c0 * CHUNK, :] = jnp.zeros((c0 * CHUNK, hv), F32)

        def run(fwd):
            w_ref, b_ref = (wf_ref, bf_ref) if fwd else (wb_ref, bb_ref)
            st[...] = jnp.zeros_like(st)

            def step(i, carry):
                n = c0 + i if fwd else nc - 1 - i
                r0 = pl.multiple_of(n * CHUNK, CHUNK)
                blk = p_ref[pl.ds(r0, CHUNK), :]
                q = blk[:, :hk].astype(F32) * scale
                k = blk[:, hk:2 * hk].astype(F32)
                v = blk[:, 2 * hk:2 * hk + hv]
                g, _ = _log_gate(lr_ref[pl.ds(r0, CHUNK), :], w_ref, b_ref, n == c0, hk)
                b = _dot_exact01(ones_low if fwd else ones_up, g)
                btot = b[CHUNK - 1:CHUNK] if fwd else b[0:1]
                qi = (q * jnp.exp(b)).astype(BF16)
                ki = (k * jnp.exp(-b)).astype(BF16)
                kd = (k * jnp.exp(btot - b)).astype(BF16)
                a = jnp.where(low_incl if fwd else up_strict, _dot_nt(qi, ki), 0.0)
                o = _dot(a.astype(BF16), v) + _dot_nt(qi, st[...].astype(BF16))
                if fwd:
                    oacc[pl.ds(r0, CHUNK), :] = o
                else:
                    oacc[pl.ds(r0, CHUNK), :] += o
                st[...] = st[...] * jnp.exp(btot) + _dot_tn(v, kd)
                return carry

            lax.fori_loop(0, nc - c0, step, 0)

        run(True)
        run(False)

        def finish(i, carry):
            r0 = pl.multiple_of(i * CHUNK, CHUNK)
            o = oacc[pl.ds(r0, CHUNK), :]
            r = p_ref[pl.ds(r0, CHUNK), 2 * hk + hv:].astype(F32)
            on = o * lax.rsqrt(jnp.mean(o * o, axis=-1, keepdims=True) + EPS) * gg_ref[...]
            o_ref[pl.ds(r0, CHUNK), :] = o.astype(BF16)
            y_ref[pl.ds(r0, CHUNK), :] = (on * r * _sigmoid(r)).astype(BF16)
            return carry

        lax.fori_loop(0, nc, finish, 0)

    head = lambda s, h: (s, h)
    wspec = pl.BlockSpec((LR_LANES, hk), lambda s, h: (0, h))
    bspec = pl.BlockSpec((1, hk), lambda s, h: (0, h))
    return pl.pallas_call(
        body, name="gla_fwd", grid=(dm.Bl, HEADS),
        in_specs=[pl.BlockSpec((lp, hw), head), pl.BlockSpec((lp, LR_LANES), lambda s, h: (s, 0)),
                  wspec, bspec, wspec, bspec, pl.BlockSpec((1, hv), lambda s, h: (0, 0))],
        out_specs=[pl.BlockSpec((lp, hv), head), pl.BlockSpec((lp, hv), head)],
        out_shape=[jax.ShapeDtypeStruct((dm.T, dm.DV), BF16), jax.ShapeDtypeStruct((dm.T, dm.DV), BF16)],
        scratch_shapes=[pltpu.VMEM((lp, hv), F32), pltpu.VMEM((hv, hk), F32)],
        compiler_params=_cp(2),
    )(proj_b, lr, wg_f, bg_f, wg_b, bg_b, gla_g)


def _gla_bwd(proj_b, lr, o_all, dy_gla, wg_f, bg_f, wg_b, bg_b, gla_g, dm):
    lp, hk, hv, nc, c0, hw = dm.LP, dm.HK, dm.HV, dm.NC, dm.C0, dm.HW
    scale = hk ** -0.5
    ncu = nc - c0

    def body(p_ref, lr_ref, o_ref, dy_ref, wf_ref, bf_ref, wb_ref, bb_ref, gg_ref,
             d_ref, dlr_ref, gwf_ref, gbf_ref, gwb_ref, gbb_ref, ggg_ref,
             do_s, s_all, b_s, gs_s, dq_s, dk_s, dv_s, dst):
        low_incl, up_strict, ones_low, ones_up = _chunk_masks()
        h = pl.program_id(1)

        @pl.when(h == 0)
        def _():
            dlr_ref[...] = jnp.zeros_like(dlr_ref)

        if c0 > 0:
            zr = c0 * CHUNK
            d_ref[0:zr, :] = jnp.zeros((zr, hw), BF16)

        def norm_bwd(i, ggg):
            r0 = pl.multiple_of(i * CHUNK, CHUNK)
            o = o_ref[pl.ds(r0, CHUNK), :].astype(F32)
            dy = dy_ref[pl.ds(r0, CHUNK), :].astype(F32)
            r = p_ref[pl.ds(r0, CHUNK), 2 * hk + hv:].astype(F32)
            rstd = lax.rsqrt(jnp.mean(o * o, axis=-1, keepdims=True) + EPS)
            ohat = o * rstd
            sg = _sigmoid(r)
            d_on = dy * (r * sg)
            d_ref[pl.ds(r0, CHUNK), 2 * hk + hv:] = (dy * ohat * gg_ref[...] * (sg * (1.0 + r * (1.0 - sg)))).astype(BF16)
            d_oh = d_on * gg_ref[...]
            do_s[pl.ds(r0, CHUNK), :] = (rstd * (d_oh - ohat * jnp.mean(d_oh * ohat, axis=-1, keepdims=True))).astype(BF16)
            return ggg + jnp.sum(d_on * ohat, axis=0, keepdims=True)

        ggg = lax.fori_loop(c0, nc, norm_bwd, jnp.zeros((1, hv), F32))

        @pl.when((pl.program_id(0) == 0) & (h == 0))
        def _():
            ggg_ref[...] = jnp.zeros_like(ggg_ref)

        ggg_ref[0:1, :] += ggg

        def run(fwd):
            w_ref, b_ref = (wf_ref, bf_ref) if fwd else (wb_ref, bb_ref)
            gw_ref, gb_ref = (gwf_ref, gbf_ref) if fwd else (gwb_ref, gbb_ref)
            cum, cum_t = (ones_low, ones_up) if fwd else (ones_up, ones_low)
            mask = low_incl if fwd else up_strict

            def load(n):
                r0 = pl.multiple_of(n * CHUNK, CHUNK)
                blk = p_ref[pl.ds(r0, CHUNK), :]
                return r0, blk[:, :hk].astype(F32) * scale, blk[:, hk:2 * hk].astype(F32), blk[:, 2 * hk:2 * hk + hv]

            dst[...] = jnp.zeros_like(dst)

            def record(i, carry):
                n = c0 + i if fwd else nc - 1 - i
                r0, q, k, v = load(n)
                g, dg_dz = _log_gate(lr_ref[pl.ds(r0, CHUNK), :], w_ref, b_ref, n == c0, hk)
                b = _dot_exact01(cum, g)
                btot = b[CHUNK - 1:CHUNK] if fwd else b[0:1]
                b_s[pl.ds(r0, CHUNK), :] = b
                gs_s[pl.ds(r0, CHUNK), :] = dg_dz
                s_all[n] = dst[...]
                kd = (k * jnp.exp(btot - b)).astype(BF16)
                dst[...] = dst[...] * jnp.exp(btot) + _dot_tn(v, kd)
                return carry

            lax.fori_loop(0, ncu, record, 0)

            dst[...] = jnp.zeros_like(dst)

            def grad(i, carry):
                gw, gb = carry
                n = nc - 1 - i if fwd else c0 + i
                r0, q, k, v = load(n)
                b = b_s[pl.ds(r0, CHUNK), :]
                btot = b[CHUNK - 1:CHUNK] if fwd else b[0:1]
                eb, enb, edb, dec = jnp.exp(b), jnp.exp(-b), jnp.exp(btot - b), jnp.exp(btot)
                qi_f, ki_f, kd_f = q * eb, k * enb, k * edb
                qi, ki, kd = qi_f.astype(BF16), ki_f.astype(BF16), kd_f.astype(BF16)
                do = do_s[pl.ds(r0, CHUNK), :]
                st = s_all[n]
                dsn = dst[...]
                st_b, dsn_b = st.astype(BF16), dsn.astype(BF16)
                a = jnp.where(mask, _dot_nt(qi, ki), 0.0).astype(BF16)
                da = jnp.where(mask, _dot_nt(do, v), 0.0).astype(BF16)
                dv = _dot_tn(a, do) + _dot_nt(kd, dsn_b)
                dqi = _dot(da, ki) + _dot(do, st_b)
                dki = _dot_tn(da, qi)
                dkd = _dot(v, dsn_b)
                ddec = jnp.sum(st * dsn, axis=0, keepdims=True)
                dst[...] = dsn * dec + _dot_tn(do, qi)
                dq = dqi * eb * scale
                dk = dki * enb + dkd * edb
                db = dqi * qi_f - dki * ki_f - dkd * kd_f
                extra = jnp.sum(dkd * kd_f, axis=0, keepdims=True) + ddec * dec
                dg = _dot_exact01(cum_t, db) + extra
                dz = dg * gs_s[pl.ds(r0, CHUNK), :]
                dz_b = dz.astype(BF16)
                lrc = lr_ref[pl.ds(r0, CHUNK), :]
                dlr_ref[pl.ds(r0, CHUNK), :] += _dot_nt(dz_b, w_ref[...])
                if fwd:
                    dq_s[pl.ds(r0, CHUNK), :] = dq
                    dk_s[pl.ds(r0, CHUNK), :] = dk
                    dv_s[pl.ds(r0, CHUNK), :] = dv
                else:
                    d_ref[pl.ds(r0, CHUNK), 0:hk] = (dq_s[pl.ds(r0, CHUNK), :] + dq).astype(BF16)
                    d_ref[pl.ds(r0, CHUNK), hk:2 * hk] = (dk_s[pl.ds(r0, CHUNK), :] + dk).astype(BF16)
                    d_ref[pl.ds(r0, CHUNK), 2 * hk:2 * hk + hv] = (dv_s[pl.ds(r0, CHUNK), :] + dv).astype(BF16)
                return gw + _dot_tn(lrc, dz_b), gb + jnp.sum(dz, axis=0, keepdims=True)

            gw, gb = lax.fori_loop(0, ncu, grad, (jnp.zeros((LR_LANES, hk), F32), jnp.zeros((1, hk), F32)))
            gw_ref[0] = gw
            gb_ref[0] = jnp.zeros((8, hk), F32)
            gb_ref[0, 0:1, :] = gb

        run(True)
        run(False)

    head = lambda s, h: (s, h)
    wspec = pl.BlockSpec((LR_LANES, hk), lambda s, h: (0, h))
    bspec = pl.BlockSpec((1, hk), lambda s, h: (0, h))
    gwspec = pl.BlockSpec((1, LR_LANES, hk), lambda s, h: (s, 0, h))
    gbspec = pl.BlockSpec((1, 8, hk), lambda s, h: (s, 0, h))
    gw_shape = jax.ShapeDtypeStruct((dm.Bl, LR_LANES, dm.DK), F32)
    gb_shape = jax.ShapeDtypeStruct((dm.Bl, 8, dm.DK), F32)
    return pl.pallas_call(
        body, name="gla_bwd", grid=(dm.Bl, HEADS),
        in_specs=[pl.BlockSpec((lp, hw), head), pl.BlockSpec((lp, LR_LANES), lambda s, h: (s, 0)),
                  pl.BlockSpec((lp, hv), head), pl.BlockSpec((lp, hv), head),
                  wspec, bspec, wspec, bspec, pl.BlockSpec((1, hv), lambda s, h: (0, 0))],
        out_specs=[pl.BlockSpec((lp, hw), head), pl.BlockSpec((lp, LR_LANES), lambda s, h: (s, 0)),
                   gwspec, gbspec, gwspec, gbspec, pl.BlockSpec((8, hv), lambda s, h: (0, 0))],
        out_shape=[jax.ShapeDtypeStruct((dm.T, HEADS * hw), BF16), jax.ShapeDtypeStruct((dm.T, LR_LANES), F32),
                   gw_shape, gb_shape, gw_shape, gb_shape, jax.ShapeDtypeStruct((8, hv), F32)],
        scratch_shapes=[pltpu.VMEM((lp, hv), BF16), pltpu.VMEM((nc, hv, hk), F32), pltpu.VMEM((lp, hk), F32),
                        pltpu.VMEM((lp, hk), F32), pltpu.VMEM((lp, hk), F32), pltpu.VMEM((lp, hk), F32),
                        pltpu.VMEM((lp, hv), F32), pltpu.VMEM((hv, hk), F32)],
        compiler_params=_cp(2),
    )(proj_b, lr, o_all, dy_gla, wg_f, bg_f, wg_b, bg_b, gla_g)


def _out_merge(y_conv, y_gla, proj_c, w_oc, w_og, dm):
    d = dm.D
    tm = _pick(dm.T, 512, 16)

    def body(yc_ref, yg_ref, c_ref, woc_ref, wog_ref, pc_ref, pg_ref, m_ref):
        pc = _dot(yc_ref[...], woc_ref[...])
        pg = _dot(yg_ref[...], wog_ref[...])
        pc_ref[...] = pc.astype(BF16)
        pg_ref[...] = pg.astype(BF16)
        ma = c_ref[:, :d].astype(F32)
        mb = c_ref[:, d:].astype(F32)
        m_ref[...] = (_sigmoid(ma) * pc + _sigmoid(mb) * pg).astype(BF16)

    row = pl.BlockSpec((tm, d), lambda i: (i, 0))
    full = pl.BlockSpec((d, d), lambda i: (0, 0))
    act = jax.ShapeDtypeStruct((dm.T, d), BF16)
    return pl.pallas_call(
        body, name="out_merge", grid=(dm.T // tm,),
        in_specs=[row, row, pl.BlockSpec((tm, 2 * d), lambda i: (i, 0)), full, full],
        out_specs=[row, row, row], out_shape=[act, act, act], compiler_params=_cp(1),
    )(y_conv, y_gla, proj_c, w_oc, w_og)


def _final_fwd(merged, w_out, x, metapad, target, g_post, dm):
    tm, tps, d = dm.TM, dm.TPS, dm.D

    def body(m_ref, w_ref, x_ref, mp_ref, t_ref, g_ref, dout_ref, dy_ref, st_ref):
        i = pl.program_id(0)
        j = i % tps
        out = _dot(m_ref[...], w_ref[...])
        rstd = lax.rsqrt(jnp.mean(out * out, axis=-1, keepdims=True) + EPS)
        ohat = out * rstd
        h = jnp.where(j == 0, mp_ref[...], x_ref[0])
        y = h + ohat * g_ref[...]
        err = jnp.where(j == 0, 0.0, y - t_ref[0])
        dy = err * (1.0 / d)
        d_oh = dy * g_ref[...]
        dout_ref[...] = (rstd * (d_oh - ohat * jnp.mean(d_oh * ohat, axis=-1, keepdims=True))).astype(BF16)
        dy_ref[...] = dy

        @pl.when(i == 0)
        def _():
            st_ref[...] = jnp.zeros_like(st_ref)

        st_ref[0:1, :] += jnp.sum(dy * ohat, axis=0, keepdims=True)
        st_ref[1:2, :] += jnp.sum(err * err, axis=0, keepdims=True)

    row = pl.BlockSpec((tm, d), lambda i: (i, 0))
    tok = pl.BlockSpec((1, tm, d), lambda i: (i // tps, jnp.maximum(i % tps - 1, 0), 0))
    const = lambda r: pl.BlockSpec((r, d), lambda i: (0, 0))
    return pl.pallas_call(
        body, name="final_fwd", grid=(dm.Bl * tps,),
        in_specs=[row, const(d), tok, const(tm), tok, const(1)],
        out_specs=[row, row, const(8)],
        out_shape=[jax.ShapeDtypeStruct((dm.T, d), BF16), jax.ShapeDtypeStruct((dm.T, d), F32),
                   jax.ShapeDtypeStruct((8, d), F32)],
        compiler_params=_cp(1),
    )(merged, w_out, x, metapad, target, g_post)


def _merge_bwd(d_out, proj_c, p_conv, p_gla, wt_out, wt_oc, wt_og, dm):
    d = dm.D
    tm = _pick(dm.T, 512, 16)

    def body(do_ref, c_ref, pc_ref, pg_ref, wo_ref, woc_ref, wog_ref, dpc_ref, dpg_ref, dc_ref, dyc_ref, dyg_ref):
        dmg = _dot(do_ref[...], wo_ref[...])
        sa = _sigmoid(c_ref[:, :d].astype(F32))
        sb = _sigmoid(c_ref[:, d:].astype(F32))
        dpc = (dmg * sa).astype(BF16)
        dpg = (dmg * sb).astype(BF16)
        dpc_ref[...] = dpc
        dpg_ref[...] = dpg
        dc_ref[:, :d] = (dmg * pc_ref[...].astype(F32) * sa * (1.0 - sa)).astype(BF16)
        dc_ref[:, d:] = (dmg * pg_ref[...].astype(F32) * sb * (1.0 - sb)).astype(BF16)
        dyc_ref[...] = _dot(dpc, woc_ref[...]).astype(BF16)
        dyg_ref[...] = _dot(dpg, wog_ref[...]).astype(BF16)

    row = pl.BlockSpec((tm, d), lambda i: (i, 0))
    row2 = pl.BlockSpec((tm, 2 * d), lambda i: (i, 0))
    full = pl.BlockSpec((d, d), lambda i: (0, 0))
    act = jax.ShapeDtypeStruct((dm.T, d), BF16)
    return pl.pallas_call(
        body, name="merge_bwd", grid=(dm.T // tm,),
        in_specs=[row, row2, row, row, full, full, full],
        out_specs=[row, row, row2, row, row],
        out_shape=[act, act, jax.ShapeDtypeStruct((dm.T, 2 * d), BF16), act, act],
        compiler_params=_cp(1),
    )(d_out, proj_c, p_conv, p_gla, wt_out, wt_oc, wt_og)


def _prenorm_bwd(du, dy, x, metapad, g_pre, dm):
    tm, tps, d = dm.TM, dm.TPS, dm.D

    def body(du_ref, dy_ref, x_ref, mp_ref, g_ref, gx_ref, dmeta_ref, gg_ref):
        i = pl.program_id(0)
        j = i % tps
        h = jnp.where(j == 0, mp_ref[...], x_ref[0])
        rstd = lax.rsqrt(jnp.mean(h * h, axis=-1, keepdims=True) + EPS)
        hhat = h * rstd
        dug = du_ref[...] * g_ref[...]
        dh = dy_ref[...] + rstd * (dug - hhat * jnp.mean(dug * hhat, axis=-1, keepdims=True))

        @pl.when(j == 0)
        def _():
            dmeta_ref[0] = dh

        @pl.when(j > 0)
        def _():
            gx_ref[0] = dh

        @pl.when(i == 0)
        def _():
            gg_ref[...] = jnp.zeros_like(gg_ref)

        gg_ref[0:1, :] += jnp.sum(du_ref[...] * hhat, axis=0, keepdims=True)

    row = pl.BlockSpec((tm, d), lambda i: (i, 0))
    tok = pl.BlockSpec((1, tm, d), lambda i: (i // tps, jnp.maximum(i % tps - 1, 0), 0))
    const = lambda r: pl.BlockSpec((r, d), lambda i: (0, 0))
    return pl.pallas_call(
        body, name="prenorm_bwd", grid=(dm.Bl * tps,),
        in_specs=[row, row, tok, const(tm), const(1)],
        out_specs=[tok, pl.BlockSpec((1, tm, d), lambda i: (i // tps, 0, 0)), const(8)],
        out_shape=[jax.ShapeDtypeStruct((dm.Bl, dm.S, d), F32), jax.ShapeDtypeStruct((dm.Bl, tm, d), F32),
                   jax.ShapeDtypeStruct((8, d), F32)],
        compiler_params=_cp(1),
    )(du, dy, x, metapad, g_pre)


def _adamw(partials, w, m, v, name):
    r, c = w.shape
    tr = _pick(r, 256, 16)

    def body(p_ref, w_ref, m_ref, v_ref, g_ref, d_ref, nm_ref, nv_ref):
        g = p_ref[0].astype(F32)
        for j in range(1, N_DEV):
            g = g + p_ref[j].astype(F32)
        m2 = ADAM_B1 * m_ref[...] + (1.0 - ADAM_B1) * g
        v2 = ADAM_B2 * v_ref[...] + (1.0 - ADAM_B2) * (g * g)
        m_hat = m2 / (1.0 - ADAM_B1 ** ADAM_STEP)
        v_hat = v2 / (1.0 - ADAM_B2 ** ADAM_STEP)
        g_ref[...] = g
        d_ref[...] = -ADAM_LR * (m_hat / (jnp.sqrt(v_hat) + ADAM_EPS) + ADAM_WD * w_ref[...])
        nm_ref[...] = m2
        nv_ref[...] = v2

    row = pl.BlockSpec((tr, c), lambda i: (i, 0))
    out = jax.ShapeDtypeStruct((r, c), F32)
    return pl.pallas_call(
        body, name=name, grid=(r // tr,),
        in_specs=[pl.BlockSpec((N_DEV, tr, c), lambda i: (0, i, 0)), row, row, row],
        out_specs=[row, row, row, row], out_shape=[out, out, out, out], compiler_params=_cp(1),
    )(partials, w, m, v)


def _pack_cols(w, dm):
    d, dk, hk, hv, cw, nj = dm.D, dm.DK, dm.HK, dm.HV, dm.CW, dm.NJ
    rows = w.shape[0]
    a = w[:, :4 * d].reshape(rows, 4, nj, cw).transpose(0, 2, 1, 3).reshape(rows, 4 * d)
    q, k, v, r = w[:, 4 * d:4 * d + dk], w[:, 4 * d + dk:5 * d], w[:, 5 * d:6 * d], w[:, 6 * d:7 * d]
    b = jnp.concatenate([t for h in range(HEADS) for t in (q[:, h * hk:(h + 1) * hk], k[:, h * hk:(h + 1) * hk],
                                                           v[:, h * hv:(h + 1) * hv], r[:, h * hv:(h + 1) * hv])], axis=1)
    c = w[:, 7 * d + 2 * RANK:]
    lr = jnp.pad(w[:, 7 * d:7 * d + 2 * RANK], ((0, 0), (0, LR_LANES - 2 * RANK)))
    return a, b, c, lr


def _unpack_cols(a, b, c, lr, dm):
    d, hk, hv, cw, nj, hw = dm.D, dm.HK, dm.HV, dm.CW, dm.NJ, dm.HW
    rows = a.shape[0]
    conv = a.reshape(rows, nj, 4, cw).transpose(0, 2, 1, 3).reshape(rows, 4 * d)
    heads = [b[:, h * hw:(h + 1) * hw] for h in range(HEADS)]
    q = jnp.concatenate([t[:, :hk] for t in heads], axis=1)
    k = jnp.concatenate([t[:, hk:2 * hk] for t in heads], axis=1)
    v = jnp.concatenate([t[:, 2 * hk:2 * hk + hv] for t in heads], axis=1)
    r = jnp.concatenate([t[:, 2 * hk + hv:] for t in heads], axis=1)
    return jnp.concatenate([conv, q, k, v, r, lr[:, :2 * RANK], c], axis=1)


def _to_blob(pieces, dtype, row_mult):
    lead = pieces[0].shape[0]
    flat = jnp.concatenate([p.reshape(lead, -1).astype(dtype) for p in pieces], axis=1)
    unit = row_mult * BLOB_LANES
    padded = -(-flat.shape[1] // unit) * unit
    flat = jnp.pad(flat, ((0, 0), (0, padded - flat.shape[1])))
    return flat.reshape(lead, padded // BLOB_LANES, BLOB_LANES)


def _from_blob(blob, shapes):
    lead = blob.shape[0]
    flat = blob.reshape(lead, -1)
    out, off = [], 0
    for shp in shapes:
        size = int(np.prod(shp))
        out.append(flat[:, off:off + size].reshape((lead,) + tuple(shp)))
        off += size
    return out


def _local_step(x, target, meta, g_pre, w_in, conv_w, wg_f, bg_f, wg_b, bg_b, gla_g, w_oc, w_og, w_out, g_post):
    bl, s, d = x.shape
    dm = _Dims(bl, s, d)
    metapad = jnp.concatenate([jnp.zeros((dm.TM - N_META, d), F32), meta], axis=0)
    wa, wb, wc, wlr = _pack_cols(w_in, dm)
    wgp_f = jnp.pad(wg_f, ((0, LR_LANES - RANK), (0, 0))).astype(BF16)
    wgp_b = jnp.pad(wg_b, ((RANK, LR_LANES - 2 * RANK), (0, 0))).astype(BF16)

    u = _prenorm(x, metapad, g_pre, dm)
    proj_a = _matmul(u, wa, BF16, "inproj_conv")
    proj_b = _matmul(u, wb, BF16, "inproj_gla")
    proj_c = _matmul(u, wc, BF16, "inproj_merge")
    lr = _matmul(u, wlr, BF16, "inproj_gate")
    y_conv = _conv_fwd(proj_a, conv_w, dm)
    o_all, y_gla = _gla_fwd(proj_b, lr, wgp_f, bg_f, wgp_b, bg_b, gla_g, dm)
    p_conv, p_gla, merged = _out_merge(y_conv, y_gla, proj_c, w_oc, w_og, dm)
    d_out, dy, stats = _final_fwd(merged, w_out, x, metapad, target, g_post, dm)
    loss = 0.5 / d * jnp.sum(stats[1])

    d_pc, d_pg, d_c, dy_conv, dy_gla = _merge_bwd(d_out, proj_c, p_conv, p_gla, w_out.T, w_oc.T, w_og.T, dm)
    g_out = _matmul_tn(merged, d_out, "grad_w_out")
    g_oc = _matmul_tn(y_conv, d_pc, "grad_w_out_conv")
    g_og = _matmul_tn(y_gla, d_pg, "grad_w_out_gla")
    d_a, g_conv = _conv_bwd(proj_a, dy_conv, conv_w, dm)
    d_b, d_lr, gwp_f, gbp_f, gwp_b, gbp_b, g_gla = _gla_bwd(proj_b, lr, o_all, dy_gla, wgp_f, bg_f, wgp_b, bg_b, gla_g, dm)
    du = _matmul_parts([(d_a, wa.T), (d_b, wb.T), (d_c, wc.T), (d_lr, wlr.T)], "grad_u")
    g_in = _unpack_cols(_matmul_tn(u, d_a, "grad_w_in_conv"), _matmul_tn(u, d_b, "grad_w_in_gla"),
                        _matmul_tn(u, d_c, "grad_w_in_merge"), _matmul_tn(u, d_lr, "grad_w_in_gate"), dm)
    grad_x, d_meta, g_pre_rows = _prenorm_bwd(du, dy, x, metapad, g_pre, dm)

    grads = dict(
        meta_tokens=jnp.sum(d_meta[:, dm.TM - N_META:, :], axis=0), norm_pre=g_pre_rows[0:1], w_in=g_in,
        conv_w=g_conv[0:3], w_gate_fwd=jnp.sum(gwp_f, axis=0)[:RANK], b_gate_fwd=jnp.sum(gbp_f, axis=0)[0:1],
        w_gate_bwd=jnp.sum(gwp_b, axis=0)[RANK:2 * RANK], b_gate_bwd=jnp.sum(gbp_b, axis=0)[0:1],
        gla_norm=g_gla[0:1], w_out_conv=g_oc, w_out_gla=g_og, w_merge_out=g_out, norm_post=stats[0:1])
    return loss, grad_x, grads


SHARDED = ("w_in", "w_out_conv", "w_out_gla", "w_merge_out", "meta_tokens", "conv_w", "w_gate_fwd", "w_gate_bwd")
REPLICATED = ("norm_pre", "b_gate_fwd", "b_gate_bwd", "gla_norm", "norm_post")
NAMES = ("meta_tokens", "norm_pre", "w_in", "conv_w", "w_gate_fwd", "b_gate_fwd", "w_gate_bwd", "b_gate_bwd", "gla_norm",
         "w_out_conv", "w_out_gla", "w_merge_out", "norm_post")


def _split_for_devices(name, g):
    if name in ("w_out_conv", "w_out_gla", "w_merge_out"):
        return g.reshape(N_DEV, g.shape[0] // N_DEV, g.shape[1])
    r, c = g.shape
    return g.reshape(r, N_DEV, c // N_DEV).transpose(1, 0, 2)


def _join_from_devices(name, parts):
    if name in ("w_out_conv", "w_out_gla", "w_merge_out"):
        return parts.reshape(parts.shape[0] * parts.shape[1], parts.shape[2])
    n, r, c = parts.shape
    return parts.transpose(1, 0, 2).reshape(r, n * c)


def kernel(x, meta_tokens, norm_pre, w_in, conv_w, w_gate_fwd, b_gate_fwd, w_gate_bwd, b_gate_bwd, gla_norm, w_out_conv, w_out_gla, w_merge_out, norm_post, loss_target, m_meta_tokens, m_norm_pre, m_w_in, m_conv_w, m_w_gate_fwd, m_b_gate_fwd, m_w_gate_bwd, m_b_gate_bwd, m_gla_norm, m_w_out_conv, m_w_out_gla, m_w_merge_out, m_norm_post, v_meta_tokens, v_norm_pre, v_w_in, v_conv_w, v_w_gate_fwd, v_b_gate_fwd, v_w_gate_bwd, v_b_gate_bwd, v_gla_norm, v_w_out_conv, v_w_out_gla, v_w_merge_out, v_norm_post):
    w = dict(meta_tokens=meta_tokens, norm_pre=norm_pre, w_in=w_in[0], conv_w=conv_w[0], w_gate_fwd=w_gate_fwd[0],
             b_gate_fwd=b_gate_fwd, w_gate_bwd=w_gate_bwd[0], b_gate_bwd=b_gate_bwd, gla_norm=gla_norm,
             w_out_conv=w_out_conv[0], w_out_gla=w_out_gla[0], w_merge_out=w_merge_out[0], norm_post=norm_post)
    m = dict(meta_tokens=m_meta_tokens, norm_pre=m_norm_pre, w_in=m_w_in[0], conv_w=m_conv_w[0], w_gate_fwd=m_w_gate_fwd[0],
             b_gate_fwd=m_b_gate_fwd, w_gate_bwd=m_w_gate_bwd[0], b_gate_bwd=m_b_gate_bwd, gla_norm=m_gla_norm,
             w_out_conv=m_w_out_conv[0], w_out_gla=m_w_out_gla[0], w_merge_out=m_w_merge_out[0], norm_post=m_norm_post)
    v = dict(meta_tokens=v_meta_tokens, norm_pre=v_norm_pre, w_in=v_w_in[0], conv_w=v_conv_w[0], w_gate_fwd=v_w_gate_fwd[0],
             b_gate_fwd=v_b_gate_fwd, w_gate_bwd=v_w_gate_bwd[0], b_gate_bwd=v_b_gate_bwd, gla_norm=v_gla_norm,
             w_out_conv=v_w_out_conv[0], w_out_gla=v_w_out_gla[0], w_merge_out=v_w_merge_out[0], norm_post=v_norm_post)
    big, small = SHARDED[:4], SHARDED[4:]

    big_blob = _to_blob([w[n][None] for n in big], BF16, BIG_ROWS)[0]
    small_blob = _to_blob([w[n][None] for n in small], F32, SMALL_ROWS)[0]
    big_all, small_all = _exchange([big_blob, small_blob], [], "gather_weights")
    full = {n: _join_from_devices(n, p) for n, p in zip(big, _from_blob(big_all, [w[n].shape for n in big]))}
    full.update({n: _join_from_devices(n, p) for n, p in zip(small, _from_blob(small_all, [w[n].shape for n in small]))})

    loss, grad_x, grads = _local_step(
        x, loss_target, full["meta_tokens"], norm_pre, full["w_in"], full["conv_w"], full["w_gate_fwd"], b_gate_fwd,
        full["w_gate_bwd"], b_gate_bwd, gla_norm, full["w_out_conv"], full["w_out_gla"], full["w_merge_out"], norm_post)
    loss = lax.psum(loss, ("x", "y", "c"))

    shard_blob = _to_blob([_split_for_devices(n, grads[n]) for n in SHARDED], BF16, BIG_ROWS)
    repl_blob = _to_blob([grads[n][None] for n in REPLICATED], F32, SMALL_ROWS)[0]
    repl_all, shard_all = _exchange([repl_blob], [shard_blob], "exchange_grads")

    def update(names, partials, rows, tag):
        blobs = [_to_blob([t[n][None] for n in names], F32, rows)[0] for t in (w, m, v)]
        res = _adamw(partials, *blobs, name="adamw_" + tag)
        return [dict(zip(names, [p[0] for p in _from_blob(r[None], [w[n].shape for n in names])])) for r in res]

    outs = [dict(a, **b) for a, b in zip(update(SHARDED, shard_all, BIG_ROWS, "sharded"),
                                         update(REPLICATED, repl_all, SMALL_ROWS, "replicated"))]
    lead = lambda n, t: t[None] if n in ("w_in", "conv_w", "w_gate_fwd", "w_gate_bwd", "w_out_conv", "w_out_gla", "w_merge_out") else t
    return (loss, grad_x, *[lead(n, o[n]) for o in outs for n in NAMES])
```

```python
import functools

import jax
import jax.numpy as jnp
import numpy as np
from jax import lax
from jax.experimental import pallas as pl
from jax.experimental.pallas import tpu as pltpu

F32 = jnp.float32
BF16 = jnp.bfloat16
MESH = pl.DeviceIdType.MESH

N_META = 16
CHUNK = 64
HEADS = 4
RANK = 16
LR_LANES = 128
PAD_ROWS = CHUNK - N_META
EPS = 1e-6
GATE_NORMALIZER = 16.0
N_DEV = 8
ADAM_LR, ADAM_B1, ADAM_B2, ADAM_EPS, ADAM_WD, ADAM_STEP = 0.001, 0.9, 0.999, 1e-08, 0.01, 10
VMEM_LIMIT_BYTES = 56 * 1024 * 1024
BLOB_LANES = 512
SMALL_ROWS = 16


class _Dims:
    def __init__(self, bl, s, d):
        self.Bl, self.S, self.D = bl, s, d
        self.TM = 256 if s % 256 == 0 else CHUNK
        self.LP = self.TM + s
        self.T = bl * self.LP
        self.TPS = self.LP // self.TM
        self.NC = self.LP // CHUNK
        self.C0 = (self.TM - CHUNK) // CHUNK
        self.DK, self.DV = d // 2, d
        self.HK, self.HV = self.DK // HEADS, self.DV // HEADS
        self.HW = 2 * self.HK + 2 * self.HV
        self.CW = 256 if d % 256 == 0 and d > 256 else d // 4
        self.NJ = d // self.CW


def _pick(n, target, mult):
    t = min(n, target)
    while t >= mult:
        if n % t == 0 and t % mult == 0:
            return t
        t -= mult
    return n


def _cp(n_axes):
    return pltpu.CompilerParams(dimension_semantics=("arbitrary",) * n_axes, vmem_limit_bytes=VMEM_LIMIT_BYTES)


def _sigmoid(x):
    return 1.0 / (1.0 + jnp.exp(-x))


def _dot(a, b):
    return jnp.dot(a, b, preferred_element_type=F32)


def _dot_nt(a, b):
    return lax.dot_general(a, b, (((1,), (1,)), ((), ())), preferred_element_type=F32)


def _dot_tn(a, b):
    return lax.dot_general(a, b, (((0,), (0,)), ((), ())), preferred_element_type=F32)


def _dot_exact01(m01, x):
    hi = x.astype(BF16)
    lo = (x - hi.astype(F32)).astype(BF16)
    return _dot(m01, hi) + _dot(m01, lo)


def _exchange(gathers, scatters, name):
    arrays = list(gathers) + list(scatters)
    n, ng = len(arrays), len(gathers)

    def body(*refs):
        ins, outs = refs[:n], refs[n:2 * n]
        send_sems, recv_sems, local_sems = refs[2 * n:]
        x, y, c = lax.axis_index("x"), lax.axis_index("y"), lax.axis_index("c")
        me = 4 * x + 2 * y + c
        started = []
        for t in range(n):
            src, dst = ins[t], outs[t]
            own = pltpu.make_async_copy(src if t < ng else src.at[me], dst.at[me], local_sems.at[t])
            own.start()
            started.append(own)
            for k in range(1, N_DEV):
                px = 1 - x if (k >> 2) & 1 else x
                py = 1 - y if (k >> 1) & 1 else y
                pc = 1 - c if k & 1 else c
                peer = 4 * px + 2 * py + pc
                cp = pltpu.make_async_remote_copy(
                    src_ref=src if t < ng else src.at[peer], dst_ref=dst.at[me],
                    send_sem=send_sems.at[t * (N_DEV - 1) + k - 1], recv_sem=recv_sems.at[t * (N_DEV - 1) + k - 1],
                    device_id=(px, py, pc), device_id_type=MESH)
                cp.start()
                started.append(cp)
        for cp in started:
            cp.wait()

    out_shape = [jax.ShapeDtypeStruct((N_DEV,) + a.shape[-2:], a.dtype) for a in arrays]
    any_spec = pl.BlockSpec(memory_space=pl.ANY)
    return pl.pallas_call(
        body, name=name, out_shape=out_shape, in_specs=[any_spec] * n, out_specs=[any_spec] * n,
        scratch_shapes=[pltpu.SemaphoreType.DMA((n * (N_DEV - 1),)), pltpu.SemaphoreType.DMA((n * (N_DEV - 1),)),
                        pltpu.SemaphoreType.DMA((n,))],
        compiler_params=pltpu.CompilerParams(has_side_effects=True),
    )(*arrays)


def _prenorm(x, metapad, g_pre, dm):
    tm, tps, d = dm.TM, dm.TPS, dm.D

    def body(x_ref, mp_ref, g_ref, u_ref):
        j = pl.program_id(0) % tps
        h = jnp.where(j == 0, mp_ref[...], x_ref[0])
        r = lax.rsqrt(jnp.mean(h * h, axis=-1, keepdims=True) + EPS)
        u_ref[...] = (h * r * g_ref[...]).astype(BF16)

    return pl.pallas_call(
        body, name="prenorm", grid=(dm.Bl * tps,),
        in_specs=[pl.BlockSpec((1, tm, d), lambda i: (i // tps, jnp.maximum(i % tps - 1, 0), 0)),
                  pl.BlockSpec((tm, d), lambda i: (0, 0)),
                  pl.BlockSpec((1, d), lambda i: (0, 0))],
        out_specs=pl.BlockSpec((tm, d), lambda i: (i, 0)),
        out_shape=jax.ShapeDtypeStruct((dm.T, d), BF16), compiler_params=_cp(1),
    )(x, metapad, g_pre)


def _matmul(a, b, out_dtype, name, tm=1024, tn=1024):
    m, k = a.shape
    n = b.shape[1]
    tm, tn = _pick(m, tm, 16), _pick(n, tn, 128)

    def body(a_ref, b_ref, o_ref):
        o_ref[...] = _dot(a_ref[...].astype(BF16), b_ref[...]).astype(out_dtype)

    return pl.pallas_call(
        body, name=name, grid=(n // tn, m // tm),
        in_specs=[pl.BlockSpec((tm, k), lambda j, i: (i, 0)), pl.BlockSpec((k, tn), lambda j, i: (0, j))],
        out_specs=pl.BlockSpec((tm, tn), lambda j, i: (i, j)),
        out_shape=jax.ShapeDtypeStruct((m, n), out_dtype), compiler_params=_cp(2),
    )(a, b)


def _matmul_parts(parts, name, tm=512, tk=1024):
    m = parts[0][0].shape[0]
    n = parts[0][1].shape[1]
    tm = _pick(m, tm, 16)
    tks = [_pick(a.shape[1], tk, 128) for a, _ in parts]
    counts = [a.shape[1] // t for (a, _), t in zip(parts, tks)]
    starts = [int(s) for s in np.cumsum([0] + counts[:-1])]
    total = sum(counts)

    def body(*refs):
        acc = refs[-1]
        o_ref = refs[-2]
        k = pl.program_id(1)

        @pl.when(k == 0)
        def _():
            acc[...] = jnp.zeros_like(acc)

        for p in range(len(parts)):
            @pl.when((k >= starts[p]) & (k < starts[p] + counts[p]))
            def _(p=p):
                acc[...] += _dot(refs[2 * p][...].astype(BF16), refs[2 * p + 1][...])

        @pl.when(k == total - 1)
        def _():
            o_ref[...] = acc[...]

    in_specs, operands = [], []
    for (a, b), t, s, cnt in zip(parts, tks, starts, counts):
        in_specs.append(pl.BlockSpec((tm, t), lambda i, k, s=s, cnt=cnt: (i, jnp.clip(k - s, 0, cnt - 1))))
        in_specs.append(pl.BlockSpec((t, n), lambda i, k, s=s, cnt=cnt: (jnp.clip(k - s, 0, cnt - 1), 0)))
        operands += [a, b]
    return pl.pallas_call(
        body, name=name, grid=(m // tm, total), in_specs=in_specs,
        out_specs=pl.BlockSpec((tm, n), lambda i, k: (i, 0)),
        out_shape=jax.ShapeDtypeStruct((m, n), F32), scratch_shapes=[pltpu.VMEM((tm, n), F32)],
        compiler_params=_cp(2),
    )(*operands)


def _matmul_tn(a, b, name, tt=768, tn=1024):
    t, k = a.shape
    n = b.shape[1]
    tt, tn, tk = _pick(t, tt, 16), _pick(n, tn, 128), _pick(k, 1024, 128)

    def body(a_ref, b_ref, o_ref):
        p = _dot_tn(a_ref[...].astype(BF16), b_ref[...].astype(BF16))

        @pl.when(pl.program_id(2) == 0)
        def _():
            o_ref[...] = p

        @pl.when(pl.program_id(2) > 0)
        def _():
            o_ref[...] += p

    return pl.pallas_call(
        body, name=name, grid=(k // tk, n // tn, t // tt),
        in_specs=[pl.BlockSpec((tt, tk), lambda kk, j, i: (i, kk)), pl.BlockSpec((tt, tn), lambda kk, j, i: (i, j))],
        out_specs=pl.BlockSpec((tk, tn), lambda kk, j, i: (kk, j)),
        out_shape=jax.ShapeDtypeStruct((k, n), F32), compiler_params=_cp(3),
    )(a, b)


def _conv_rows(dm):
    return _pick(dm.LP, 256, 16)


def _shifted(m, prev_row, next_row, rows):
    row = lax.broadcasted_iota(jnp.int32, m.shape, 0)
    m_prev = jnp.where(row == 0, prev_row, pltpu.roll(m, 1, 0))
    m_next = jnp.where(row == rows - 1, next_row, pltpu.roll(m, rows - 1, 0))
    return m_prev, m_next


def _conv_fwd(proj_a, conv_w, dm):
    lp, cw, rc = dm.LP, dm.CW, _conv_rows(dm)
    nchunk = lp // rc

    def body(p_ref, w_ref, y_ref):
        w0, w1, w2 = w_ref[0:1, :], w_ref[1:2, :], w_ref[2:3, :]

        def chunk(ci, carry):
            r0 = pl.multiple_of(ci * rc, rc)
            blk = p_ref[pl.ds(r0, rc), :].astype(F32)
            cb, cc, cx, cz = (blk[:, i * cw:(i + 1) * cw] for i in range(4))
            m = cc * cx
            rp = pl.multiple_of(jnp.maximum(r0 - 16, 0), 16)
            rn = pl.multiple_of(jnp.minimum(r0 + rc, lp - 16), 16)
            pv = p_ref[pl.ds(rp, 16), cw:3 * cw].astype(F32)
            nx = p_ref[pl.ds(rn, 16), cw:3 * cw].astype(F32)
            prev_row = jnp.where(ci > 0, pv[15:16, :cw] * pv[15:16, cw:], 0.0)
            next_row = jnp.where(ci < nchunk - 1, nx[0:1, :cw] * nx[0:1, cw:], 0.0)
            m_prev, m_next = _shifted(m, prev_row, next_row, rc)
            s = w0 * m_prev + w1 * m + w2 * m_next
            y_ref[pl.ds(r0, rc), :] = (cb * s * (cz * _sigmoid(cz))).astype(BF16)
            return carry

        lax.fori_loop(0, nchunk, chunk, 0)

    return pl.pallas_call(
        body, name="conv_fwd", grid=(dm.Bl, dm.NJ),
        in_specs=[pl.BlockSpec((lp, 4 * cw), lambda s, j: (s, j)), pl.BlockSpec((3, cw), lambda s, j: (0, j))],
        out_specs=pl.BlockSpec((lp, cw), lambda s, j: (s, j)),
        out_shape=jax.ShapeDtypeStruct((dm.T, dm.D), BF16), compiler_params=_cp(2),
    )(proj_a, conv_w)


def _conv_bwd(proj_a, dy_conv, conv_w, dm):
    lp, cw, rc = dm.LP, dm.CW, _conv_rows(dm)
    nchunk = lp // rc

    def body(p_ref, dy_ref, w_ref, d_ref, gw_ref):
        w0, w1, w2 = w_ref[0:1, :], w_ref[1:2, :], w_ref[2:3, :]

        def ds_of(p4, dy):
            cb, cz = p4[:, :cw], p4[:, 3 * cw:]
            return dy * cb * (cz * _sigmoid(cz))

        def chunk(ci, carry):
            g0, g1, g2 = carry
            r0 = pl.multiple_of(ci * rc, rc)
            blk = p_ref[pl.ds(r0, rc), :].astype(F32)
            dy = dy_ref[pl.ds(r0, rc), :].astype(F32)
            cb, cc, cx, cz = (blk[:, i * cw:(i + 1) * cw] for i in range(4))
            rp = pl.multiple_of(jnp.maximum(r0 - 16, 0), 16)
            rn = pl.multiple_of(jnp.minimum(r0 + rc, lp - 16), 16)
            pv = p_ref[pl.ds(rp, 16), :].astype(F32)[15:16]
            nx = p_ref[pl.ds(rn, 16), :].astype(F32)[0:1]
            dpv = dy_ref[pl.ds(rp, 16), :].astype(F32)[15:16]
            dnx = dy_ref[pl.ds(rn, 16), :].astype(F32)[0:1]
            has_prev, has_next = ci > 0, ci < nchunk - 1
            m = cc * cx
            m_prev, m_next = _shifted(m, jnp.where(has_prev, pv[:, cw:2 * cw] * pv[:, 2 * cw:3 * cw], 0.0),
                                      jnp.where(has_next, nx[:, cw:2 * cw] * nx[:, 2 * cw:3 * cw], 0.0), rc)
            s = w0 * m_prev + w1 * m + w2 * m_next
            sg = _sigmoid(cz)
            silu = cz * sg
            ds = dy * cb * silu
            ds_prev, ds_next = _shifted(ds, jnp.where(has_prev, ds_of(pv, dpv), 0.0),
                                        jnp.where(has_next, ds_of(nx, dnx), 0.0), rc)
            dm_ = w0 * ds_next + w1 * ds + w2 * ds_prev
            d_ref[pl.ds(r0, rc), 0:cw] = (dy * s * silu).astype(BF16)
            d_ref[pl.ds(r0, rc), cw:2 * cw] = (dm_ * cx).astype(BF16)
            d_ref[pl.ds(r0, rc), 2 * cw:3 * cw] = (dm_ * cc).astype(BF16)
            d_ref[pl.ds(r0, rc), 3 * cw:4 * cw] = (dy * cb * s * (sg * (1.0 + cz * (1.0 - sg)))).astype(BF16)
            return (g0 + jnp.sum(ds * m_prev, axis=0, keepdims=True), g1 + jnp.sum(ds * m, axis=0, keepdims=True),
                    g2 + jnp.sum(ds * m_next, axis=0, keepdims=True))

        z = jnp.zeros((1, cw), F32)
        g0, g1, g2 = lax.fori_loop(0, nchunk, chunk, (z, z, z))

        @pl.when(pl.program_id(1) == 0)
        def _():
            gw_ref[...] = jnp.zeros_like(gw_ref)

        gw_ref[0:1, :] += g0
        gw_ref[1:2, :] += g1
        gw_ref[2:3, :] += g2

    return pl.pallas_call(
        body, name="conv_bwd", grid=(dm.NJ, dm.Bl),
        in_specs=[pl.BlockSpec((lp, 4 * cw), lambda j, s: (s, j)), pl.BlockSpec((lp, cw), lambda j, s: (s, j)),
                  pl.BlockSpec((3, cw), lambda j, s: (0, j))],
        out_specs=[pl.BlockSpec((lp, 4 * cw), lambda j, s: (s, j)), pl.BlockSpec((8, cw), lambda j, s: (0, j))],
        out_shape=[jax.ShapeDtypeStruct((dm.T, 4 * dm.D), BF16), jax.ShapeDtypeStruct((8, dm.D), F32)],
        compiler_params=_cp(2),
    )(proj_a, dy_conv, conv_w)


def _chunk_masks():
    ii = lax.broadcasted_iota(jnp.int32, (CHUNK, CHUNK), 0)
    jj = lax.broadcasted_iota(jnp.int32, (CHUNK, CHUNK), 1)
    return jj <= ii, jj > ii, (jj <= ii).astype(BF16), (jj >= ii).astype(BF16)


def _log_gate(lr_rows, w_ref, b_ref, first_chunk, hk):
    z = _dot(lr_rows, w_ref[...]) + b_ref[...]
    e = jnp.exp(-jnp.abs(z))
    g = (jnp.minimum(z, 0.0) - jnp.log(1.0 + e)) * (1.0 / GATE_NORMALIZER)
    dg_dz = jnp.where(z >= 0.0, e, 1.0) / (1.0 + e) * (1.0 / GATE_NORMALIZER)
    row = lax.broadcasted_iota(jnp.int32, (CHUNK, hk), 0)
    pad = first_chunk & (row < PAD_ROWS)
    return jnp.where(pad, 0.0, g), jnp.where(pad, 0.0, dg_dz)


def _gla_fwd(proj_b, lr, wg_f, bg_f, wg_b, bg_b, gla_g, dm):
    lp, hk, hv, nc, c0, hw = dm.LP, dm.HK, dm.HV, dm.NC, dm.C0, dm.HW
    scale = hk ** -0.5

    def body(p_ref, lr_ref, wf_ref, bf_ref, wb_ref, bb_ref, gg_ref, o_ref, y_ref, oacc, st):
        low_incl, up_strict, ones_low, ones_up = _chunk_masks()
        if c0 > 0:
            oacc[0:c0 * CHUNK, :] = jnp.zeros((c0 * CHUNK, hv), F32)

        def run(fwd):
            w_ref, b_ref = (wf_ref, bf_ref) if fwd else (wb_ref, bb_ref)
            st[...] = jnp.zeros_like(st)

            def step(i, carry):
                n = c0 + i if fwd else nc - 1 - i
                r0 = pl.multiple_of(n * CHUNK, CHUNK)
                blk = p_ref[pl.ds(r0, CHUNK), :]
                q = blk[:, :hk].astype(F32) * scale
                k = blk[:, hk:2 * hk].astype(F32)
                v = blk[:, 2 * hk:2 * hk + hv]
                g, _ = _log_gate(lr_ref[pl.ds(r0, CHUNK), :], w_ref, b_ref, n == c0, hk)
                b = _dot_exact01(ones_low if fwd else ones_up, g)
                btot = b[CHUNK - 1:CHUNK] if fwd else b[0:1]
                qi = (q * jnp.exp(b)).astype(BF16)
                ki = (k * jnp.exp(-b)).astype(BF16)
                kd = (k * jnp.exp(btot - b)).astype(BF16)
                a = jnp.where(low_incl if fwd else up_strict, _dot_nt(qi, ki), 0.0)
                o = _dot(a.astype(BF16), v) + _dot_nt(qi, st[...].astype(BF16))
                if fwd:
                    oacc[pl.ds(r0, CHUNK), :] = o
                else:
                    oacc[pl.ds(r0, CHUNK), :] += o
                st[...] = st[...] * jnp.exp(btot) + _dot_tn(v, kd)
                return carry

            lax.fori_loop(0, nc - c0, step, 0)

        run(True)
        run(False)

        def finish(i, carry):
            r0 = pl.multiple_of(i * CHUNK, CHUNK)
            o = oacc[pl.ds(r0, CHUNK), :]
            r = p_ref[pl.ds(r0, CHUNK), 2 * hk + hv:].astype(F32)
            on = o * lax.rsqrt(jnp.mean(o * o, axis=-1, keepdims=True) + EPS) * gg_ref[...]
            o_ref[pl.ds(r0, CHUNK), :] = o.astype(BF16)
            y_ref[pl.ds(r0, CHUNK), :] = (on * r * _sigmoid(r)).astype(BF16)
            return carry

        lax.fori_loop(0, nc, finish, 0)

    head = lambda s, h: (s, h)
    wspec = pl.BlockSpec((LR_LANES, hk), lambda s, h: (0, h))
    bspec = pl.BlockSpec((1, hk), lambda s, h: (0, h))
    return pl.pallas_call(
        body, name="gla_fwd", grid=(dm.Bl, HEADS),
        in_specs=[pl.BlockSpec((lp, hw), head), pl.BlockSpec((lp, LR_LANES), lambda s, h: (s, 0)),
                  wspec, bspec, wspec, bspec, pl.BlockSpec((1, hv), lambda s, h: (0, 0))],
        out_specs=[pl.BlockSpec((lp, hv), head), pl.BlockSpec((lp, hv), head)],
        out_shape=[jax.ShapeDtypeStruct((dm.T, dm.DV), BF16), jax.ShapeDtypeStruct((dm.T, dm.DV), BF16)],
        scratch_shapes=[pltpu.VMEM((lp, hv), F32), pltpu.VMEM((hv, hk), F32)],
        compiler_params=_cp(2),
    )(proj_b, lr, wg_f, bg_f, wg_b, bg_b, gla_g)


def _gla_bwd(proj_b, lr, o_all, dy_gla, wg_f, bg_f, wg_b, bg_b, gla_g, dm):
    lp, hk, hv, nc, c0, hw = dm.LP, dm.HK, dm.HV, dm.NC, dm.C0, dm.HW
    scale = hk ** -0.5
    ncu = nc - c0

    def body(p_ref, lr_ref, o_ref, dy_ref, wf_ref, bf_ref, wb_ref, bb_ref, gg_ref,
             d_ref, dlr_ref, gwf_ref, gbf_ref, gwb_ref, gbb_ref, ggg_ref,
             do_s, s_all, b_s, gs_s, dq_s, dk_s, dv_s, dst):
        low_incl, up_strict, ones_low, ones_up = _chunk_masks()
        h = pl.program_id(1)

        @pl.when(h == 0)
        def _():
            dlr_ref[...] = jnp.zeros_like(dlr_ref)

        if c0 > 0:
            zr = c0 * CHUNK
            d_ref[0:zr, :] = jnp.zeros((zr, hw), BF16)

        def norm_bwd(i, ggg):
            r0 = pl.multiple_of(i * CHUNK, CHUNK)
            o = o_ref[pl.ds(r0, CHUNK), :].astype(F32)
            dy = dy_ref[pl.ds(r0, CHUNK), :].astype(F32)
            r = p_ref[pl.ds(r0, CHUNK), 2 * hk + hv:].astype(F32)
            rstd = lax.rsqrt(jnp.mean(o * o, axis=-1, keepdims=True) + EPS)
            ohat = o * rstd
            sg = _sigmoid(r)
            d_on = dy * (r * sg)
            d_ref[pl.ds(r0, CHUNK), 2 * hk + hv:] = (dy * ohat * gg_ref[...] * (sg * (1.0 + r * (1.0 - sg)))).astype(BF16)
            d_oh = d_on * gg_ref[...]
            do_s[pl.ds(r0, CHUNK), :] = (rstd * (d_oh - ohat * jnp.mean(d_oh * ohat, axis=-1, keepdims=True))).astype(BF16)
            return ggg + jnp.sum(d_on * ohat, axis=0, keepdims=True)

        ggg = lax.fori_loop(c0, nc, norm_bwd, jnp.zeros((1, hv), F32))

        @pl.when((pl.program_id(0) == 0) & (h == 0))
        def _():
            ggg_ref[...] = jnp.zeros_like(ggg_ref)

        ggg_ref[0:1, :] += ggg

        def run(fwd):
            w_ref, b_ref = (wf_ref, bf_ref) if fwd else (wb_ref, bb_ref)
            gw_ref, gb_ref = (gwf_ref, gbf_ref) if fwd else (gwb_ref, gbb_ref)
            cum, cum_t = (ones_low, ones_up) if fwd else (ones_up, ones_low)
            mask = low_incl if fwd else up_strict

            def load(n):
                r0 = pl.multiple_of(n * CHUNK, CHUNK)
                blk = p_ref[pl.ds(r0, CHUNK), :]
                return r0, blk[:, :hk].astype(F32) * scale, blk[:, hk:2 * hk].astype(F32), blk[:, 2 * hk:2 * hk + hv]

            dst[...] = jnp.zeros_like(dst)

            def record(i, carry):
                n = c0 + i if fwd else nc - 1 - i
                r0, q, k, v = load(n)
                g, dg_dz = _log_gate(lr_ref[pl.ds(r0, CHUNK), :], w_ref, b_ref, n == c0, hk)
                b = _dot_exact01(cum, g)
                btot = b[CHUNK - 1:CHUNK] if fwd else b[0:1]
                b_s[pl.ds(r0, CHUNK), :] = b
                gs_s[pl.ds(r0, CHUNK), :] = dg_dz
                s_all[n] = dst[...]
                kd = (k * jnp.exp(btot - b)).astype(BF16)
                dst[...] = dst[...] * jnp.exp(btot) + _dot_tn(v, kd)
                return carry

            lax.fori_loop(0, ncu, record, 0)

            dst[...] = jnp.zeros_like(dst)

            def grad(i, carry):
                gw, gb = carry
                n = nc - 1 - i if fwd else c0 + i
                r0, q, k, v = load(n)
                b = b_s[pl.ds(r0, CHUNK), :]
                btot = b[CHUNK - 1:CHUNK] if fwd else b[0:1]
                eb, enb, edb, dec = jnp.exp(b), jnp.exp(-b), jnp.exp(btot - b), jnp.exp(btot)
                qi_f, ki_f, kd_f = q * eb, k * enb, k * edb
                qi, ki, kd = qi_f.astype(BF16), ki_f.astype(BF16), kd_f.astype(BF16)
                do = do_s[pl.ds(r0, CHUNK), :]
                st = s_all[n]
                dsn = dst[...]
                st_b, dsn_b = st.astype(BF16), dsn.astype(BF16)
                a = jnp.where(mask, _dot_nt(qi, ki), 0.0).astype(BF16)
                da = jnp.where(mask, _dot_nt(do, v), 0.0).astype(BF16)
                dv = _dot_tn(a, do) + _dot_nt(kd, dsn_b)
                dqi = _dot(da, ki) + _dot(do, st_b)
                dki = _dot_tn(da, qi)
                dkd = _dot(v, dsn_b)
                ddec = jnp.sum(st * dsn, axis=0, keepdims=True)
                dst[...] = dsn * dec + _dot_tn(do, qi)
                dq = dqi * eb * scale
                dk = dki * enb + dkd * edb
                db = dqi * qi_f - dki * ki_f - dkd * kd_f
                extra = jnp.sum(dkd * kd_f, axis=0, keepdims=True) + ddec * dec
                dg = _dot_exact01(cum_t, db) + extra
                dz = dg * gs_s[pl.ds(r0, CHUNK), :]
                dz_b = dz.astype(BF16)
                lrc = lr_ref[pl.ds(r0, CHUNK), :]
                dlr_ref[pl.ds(r0, CHUNK), :] += _dot_nt(dz_b, w_ref[...])
                if fwd:
                    dq_s[pl.ds(r0, CHUNK), :] = dq
                    dk_s[pl.ds(r0, CHUNK), :] = dk
                    dv_s[pl.ds(r0, CHUNK), :] = dv
                else:
                    d_ref[pl.ds(r0, CHUNK), 0:hk] = (dq_s[pl.ds(r0, CHUNK), :] + dq).astype(BF16)
                    d_ref[pl.ds(r0, CHUNK), hk:2 * hk] = (dk_s[pl.ds(r0, CHUNK), :] + dk).astype(BF16)
                    d_ref[pl.ds(r0, CHUNK), 2 * hk:2 * hk + hv] = (dv_s[pl.ds(r0, CHUNK), :] + dv).astype(BF16)
                return gw + _dot_tn(lrc, dz_b), gb + jnp.sum(dz, axis=0, keepdims=True)

            gw, gb = lax.fori_loop(0, ncu, grad, (jnp.zeros((LR_LANES, hk), F32), jnp.zeros((1, hk), F32)))
            gw_ref[0] = gw
            gb_ref[0] = jnp.zeros((8, hk), F32)
            gb_ref[0, 0:1, :] = gb

        run(True)
        run(False)

    head = lambda s, h: (s, h)
    wspec = pl.BlockSpec((LR_LANES, hk), lambda s, h: (0, h))
    bspec = pl.BlockSpec((1, hk), lambda s, h: (0, h))
    gwspec = pl.BlockSpec((1, LR_LANES, hk), lambda s, h: (s, 0, h))
    gbspec = pl.BlockSpec((1, 8, hk), lambda s, h: (s, 0, h))
    gw_shape = jax.ShapeDtypeStruct((dm.Bl, LR_LANES, dm.DK), F32)
    gb_shape = jax.ShapeDtypeStruct((dm.Bl, 8, dm.DK), F32)
    return pl.pallas_call(
        body, name="gla_bwd", grid=(dm.Bl, HEADS),
        in_specs=[pl.BlockSpec((lp, hw), head), pl.BlockSpec((lp, LR_LANES), lambda s, h: (s, 0)),
                  pl.BlockSpec((lp, hv), head), pl.BlockSpec((lp, hv), head),
                  wspec, bspec, wspec, bspec, pl.BlockSpec((1, hv), lambda s, h: (0, 0))],
        out_specs=[pl.BlockSpec((lp, hw), head), pl.BlockSpec((lp, LR_LANES), lambda s, h: (s, 0)),
                   gwspec, gbspec, gwspec, gbspec, pl.BlockSpec((8, hv), lambda s, h: (0, 0))],
        out_shape=[jax.ShapeDtypeStruct((dm.T, HEADS * hw), BF16), jax.ShapeDtypeStruct((dm.T, LR_LANES), F32),
                   gw_shape, gb_shape, gw_shape, gb_shape, jax.ShapeDtypeStruct((8, hv), F32)],
        scratch_shapes=[pltpu.VMEM((lp, hv), BF16), pltpu.VMEM((nc, hv, hk), F32), pltpu.VMEM((lp, hk), F32),
                        pltpu.VMEM((lp, hk), F32), pltpu.VMEM((lp, hk), F32), pltpu.VMEM((lp, hk), F32),
                        pltpu.VMEM((lp, hv), F32), pltpu.VMEM((hv, hk), F32)],
        compiler_params=_cp(2),
    )(proj_b, lr, o_all, dy_gla, wg_f, bg_f, wg_b, bg_b, gla_g)


def _out_merge(y_conv, y_gla, proj_c, w_oc, w_og, dm):
    d = dm.D
    tm = _pick(dm.T, 512, 16)

    def body(yc_ref, yg_ref, c_ref, woc_ref, wog_ref, pc_ref, pg_ref, m_ref):
        pc = _dot(yc_ref[...], woc_ref[...])
        pg = _dot(yg_ref[...], wog_ref[...])
        pc_ref[...] = pc.astype(BF16)
        pg_ref[...] = pg.astype(BF16)
        ma = c_ref[:, :d].astype(F32)
        mb = c_ref[:, d:].astype(F32)
        m_ref[...] = (_sigmoid(ma) * pc + _sigmoid(mb) * pg).astype(BF16)

    row = pl.BlockSpec((tm, d), lambda i: (i, 0))
    full = pl.BlockSpec((d, d), lambda i: (0, 0))
    act = jax.ShapeDtypeStruct((dm.T, d), BF16)
    return pl.pallas_call(
        body, name="out_merge", grid=(dm.T // tm,),
        in_specs=[row, row, pl.BlockSpec((tm, 2 * d), lambda i: (i, 0)), full, full],
        out_specs=[row, row, row], out_shape=[act, act, act], compiler_params=_cp(1),
    )(y_conv, y_gla, proj_c, w_oc, w_og)


def _final_fwd(merged, w_out, x, metapad, target, g_post, dm):
    tm, tps, d = dm.TM, dm.TPS, dm.D

    def body(m_ref, w_ref, x_ref, mp_ref, t_ref, g_ref, dout_ref, dy_ref, st_ref):
        i = pl.program_id(0)
        j = i % tps
        out = _dot(m_ref[...], w_ref[...])
        rstd = lax.rsqrt(jnp.mean(out * out, axis=-1, keepdims=True) + EPS)
        ohat = out * rstd
        h = jnp.where(j == 0, mp_ref[...], x_ref[0])
        y = h + ohat * g_ref[...]
        err = jnp.where(j == 0, 0.0, y - t_ref[0])
        dy = err * (1.0 / d)
        d_oh = dy * g_ref[...]
        dout_ref[...] = (rstd * (d_oh - ohat * jnp.mean(d_oh * ohat, axis=-1, keepdims=True))).astype(BF16)
        dy_ref[...] = dy

        @pl.when(i == 0)
        def _():
            st_ref[...] = jnp.zeros_like(st_ref)

        st_ref[0:1, :] += jnp.sum(dy * ohat, axis=0, keepdims=True)
        st_ref[1:2, :] += jnp.sum(err * err, axis=0, keepdims=True)

    row = pl.BlockSpec((tm, d), lambda i: (i, 0))
    tok = pl.BlockSpec((1, tm, d), lambda i: (i // tps, jnp.maximum(i % tps - 1, 0), 0))
    const = lambda r: pl.BlockSpec((r, d), lambda i: (0, 0))
    return pl.pallas_call(
        body, name="final_fwd", grid=(dm.Bl * tps,),
        in_specs=[row, const(d), tok, const(tm), tok, const(1)],
        out_specs=[row, row, const(8)],
        out_shape=[jax.ShapeDtypeStruct((dm.T, d), BF16), jax.ShapeDtypeStruct((dm.T, d), F32),
                   jax.ShapeDtypeStruct((8, d), F32)],
        compiler_params=_cp(1),
    )(merged, w_out, x, metapad, target, g_post)


def _merge_bwd(d_out, proj_c, p_conv, p_gla, wt_out, wt_oc, wt_og, dm):
    d = dm.D
    tm = _pick(dm.T, 512, 16)

    def body(do_ref, c_ref, pc_ref, pg_ref, wo_ref, woc_ref, wog_ref, dpc_ref, dpg_ref, dc_ref, dyc_ref, dyg_ref):
        dmg = _dot(do_ref[...], wo_ref[...])
        sa = _sigmoid(c_ref[:, :d].astype(F32))
        sb = _sigmoid(c_ref[:, d:].astype(F32))
        dpc = (dmg * sa).astype(BF16)
        dpg = (dmg * sb).astype(BF16)
        dpc_ref[...] = dpc
        dpg_ref[...] = dpg
        dc_ref[:, :d] = (dmg * pc_ref[...].astype(F32) * sa * (1.0 - sa)).astype(BF16)
        dc_ref[:, d:] = (dmg * pg_ref[...].astype(F32) * sb * (1.0 - sb)).astype(BF16)
        dyc_ref[...] = _dot(dpc, woc_ref[...]).astype(BF16)
        dyg_ref[...] = _dot(dpg, wog_ref[...]).astype(BF16)

    row = pl.BlockSpec((tm, d), lambda i: (i, 0))
    row2 = pl.BlockSpec((tm, 2 * d), lambda i: (i, 0))
    full = pl.BlockSpec((d, d), lambda i: (0, 0))
    act = jax.ShapeDtypeStruct((dm.T, d), BF16)
    return pl.pallas_call(
        body, name="merge_bwd", grid=(dm.T // tm,),
        in_specs=[row, row2, row, row, full, full, full],
        out_specs=[row, row, row2, row, row],
        out_shape=[act, act, jax.ShapeDtypeStruct((dm.T, 2 * d), BF16), act, act],
        compiler_params=_cp(1),
    )(d_out, proj_c, p_conv, p_gla, wt_out, wt_oc, wt_og)


def _prenorm_bwd(du, dy, x, metapad, g_pre, dm):
    tm, tps, d = dm.TM, dm.TPS, dm.D

    def body(du_ref, dy_ref, x_ref, mp_ref, g_ref, gx_ref, dmeta_ref, gg_ref):
        i = pl.program_id(0)
        j = i % tps
        h = jnp.where(j == 0, mp_ref[...], x_ref[0])
        rstd = lax.rsqrt(jnp.mean(h * h, axis=-1, keepdims=True) + EPS)
        hhat = h * rstd
        dug = du_ref[...] * g_ref[...]
        dh = dy_ref[...] + rstd * (dug - hhat * jnp.mean(dug * hhat, axis=-1, keepdims=True))

        @pl.when(j == 0)
        def _():
            dmeta_ref[0] = dh

        @pl.when(j > 0)
        def _():
            gx_ref[0] = dh

        @pl.when(i == 0)
        def _():
            gg_ref[...] = jnp.zeros_like(gg_ref)

        gg_ref[0:1, :] += jnp.sum(du_ref[...] * hhat, axis=0, keepdims=True)

    row = pl.BlockSpec((tm, d), lambda i: (i, 0))
    tok = pl.BlockSpec((1, tm, d), lambda i: (i // tps, jnp.maximum(i % tps - 1, 0), 0))
    const = lambda r: pl.BlockSpec((r, d), lambda i: (0, 0))
    return pl.pallas_call(
        body, name="prenorm_bwd", grid=(dm.Bl * tps,),
        in_specs=[row, row, tok, const(tm), const(1)],
        out_specs=[tok, pl.BlockSpec((1, tm, d), lambda i: (i // tps, 0, 0)), const(8)],
        out_shape=[jax.ShapeDtypeStruct((dm.Bl, dm.S, d), F32), jax.ShapeDtypeStruct((dm.Bl, tm, d), F32),
                   jax.ShapeDtypeStruct((8, d), F32)],
        compiler_params=_cp(1),
    )(du, dy, x, metapad, g_pre)


def _sum_partials(partials, name):
    p, r, c = partials.shape
    tc = _pick(c, 256, 128)

    def body(p_ref, o_ref):
        g = p_ref[0].astype(F32)
        for j in range(1, p):
            g = g + p_ref[j].astype(F32)
        o_ref[...] = g

    return pl.pallas_call(
        body, name=name, grid=(c // tc,), in_specs=[pl.BlockSpec((p, r, tc), lambda i: (0, 0, i))],
        out_specs=pl.BlockSpec((r, tc), lambda i: (0, i)), out_shape=jax.ShapeDtypeStruct((r, c), F32),
        compiler_params=_cp(1),
    )(partials)


def _adamw(partials, w, m, v, name):
    r, c = w.shape
    n_parts = partials.shape[0]
    tr = _pick(r, 256, 16)

    def body(p_ref, w_ref, m_ref, v_ref, g_ref, d_ref, nm_ref, nv_ref):
        g = p_ref[0].astype(F32)
        for j in range(1, n_parts):
            g = g + p_ref[j].astype(F32)
        m2 = ADAM_B1 * m_ref[...] + (1.0 - ADAM_B1) * g
        v2 = ADAM_B2 * v_ref[...] + (1.0 - ADAM_B2) * (g * g)
        m_hat = m2 / (1.0 - ADAM_B1 ** ADAM_STEP)
        v_hat = v2 / (1.0 - ADAM_B2 ** ADAM_STEP)
        g_ref[...] = g
        d_ref[...] = -ADAM_LR * (m_hat / (jnp.sqrt(v_hat) + ADAM_EPS) + ADAM_WD * w_ref[...])
        nm_ref[...] = m2
        nv_ref[...] = v2

    row = pl.BlockSpec((tr, c), lambda i: (i, 0))
    out = jax.ShapeDtypeStruct((r, c), F32)
    return pl.pallas_call(
        body, name=name, grid=(r // tr,),
        in_specs=[pl.BlockSpec((n_parts, tr, c), lambda i: (0, i, 0)), row, row, row],
        out_specs=[row, row, row, row], out_shape=[out, out, out, out], compiler_params=_cp(1),
    )(partials, w, m, v)


def _pack_rows(wt, dm):
    d, dk, hk, hv, cw, nj = dm.D, dm.DK, dm.HK, dm.HV, dm.CW, dm.NJ
    a = wt[:4 * d].reshape(4, nj, cw, d).transpose(1, 0, 2, 3).reshape(4 * d, d)
    q, k, v, r = wt[4 * d:4 * d + dk], wt[4 * d + dk:5 * d], wt[5 * d:6 * d], wt[6 * d:7 * d]
    b = jnp.concatenate([t for h in range(HEADS) for t in (q[h * hk:(h + 1) * hk], k[h * hk:(h + 1) * hk],
                                                           v[h * hv:(h + 1) * hv], r[h * hv:(h + 1) * hv])], axis=0)
    c = wt[7 * d + 2 * RANK:]
    lr = jnp.pad(wt[7 * d:7 * d + 2 * RANK], ((0, LR_LANES - 2 * RANK), (0, 0)))
    return a, b, c, lr


def _unpack_rows(a, b, c, lr, dm):
    d, hk, hv, cw, nj, hw = dm.D, dm.HK, dm.HV, dm.CW, dm.NJ, dm.HW
    conv = a.reshape(nj, 4, cw, d).transpose(1, 0, 2, 3).reshape(4 * d, d)
    heads = [b[h * hw:(h + 1) * hw] for h in range(HEADS)]
    q = jnp.concatenate([t[:hk] for t in heads], axis=0)
    k = jnp.concatenate([t[hk:2 * hk] for t in heads], axis=0)
    v = jnp.concatenate([t[2 * hk:2 * hk + hv] for t in heads], axis=0)
    r = jnp.concatenate([t[2 * hk + hv:] for t in heads], axis=0)
    return jnp.concatenate([conv, q, k, v, r, lr[:2 * RANK], c], axis=0)


def _to_blob(pieces, dtype, row_mult):
    lead = pieces[0].shape[0]
    flat = jnp.concatenate([p.reshape(lead, -1).astype(dtype) for p in pieces], axis=1)
    unit = row_mult * BLOB_LANES
    padded = -(-flat.shape[1] // unit) * unit
    flat = jnp.pad(flat, ((0, 0), (0, padded - flat.shape[1])))
    return flat.reshape(lead, padded // BLOB_LANES, BLOB_LANES)


def _from_blob(blob, shapes):
    lead = blob.shape[0]
    flat = blob.reshape(lead, -1)
    out, off = [], 0
    for shp in shapes:
        size = int(np.prod(shp))
        out.append(flat[:, off:off + size].reshape((lead,) + tuple(shp)))
        off += size
    return out


def _local_step(x, target, meta, g_pre, wt_in, conv_w, wg_f, bg_f, wg_b, bg_b, gla_g, w_oc, w_og, w_out, g_post):
    bl, s, d = x.shape
    dm = _Dims(bl, s, d)
    metapad = jnp.concatenate([jnp.zeros((dm.TM - N_META, d), F32), meta], axis=0)
    wta, wtb, wtc, wtlr = _pack_rows(wt_in, dm)
    wa, wb, wc, wlr = wta.T, wtb.T, wtc.T, wtlr.T
    wgp_f = jnp.pad(wg_f, ((0, LR_LANES - RANK), (0, 0))).astype(BF16)
    wgp_b = jnp.pad(wg_b, ((RANK, LR_LANES - 2 * RANK), (0, 0))).astype(BF16)

    u = _prenorm(x, metapad, g_pre, dm)
    proj_a = _matmul(u, wa, BF16, "inproj_conv")
    proj_b = _matmul(u, wb, BF16, "inproj_gla")
    proj_c = _matmul(u, wc, BF16, "inproj_merge")
    lr = _matmul(u, wlr, BF16, "inproj_gate")
    y_conv = _conv_fwd(proj_a, conv_w, dm)
    o_all, y_gla = _gla_fwd(proj_b, lr, wgp_f, bg_f, wgp_b, bg_b, gla_g, dm)
    p_conv, p_gla, merged = _out_merge(y_conv, y_gla, proj_c, w_oc, w_og, dm)
    d_out, dy, stats = _final_fwd(merged, w_out, x, metapad, target, g_post, dm)
    loss = 0.5 / d * jnp.sum(stats[1])

    d_pc, d_pg, d_c, dy_conv, dy_gla = _merge_bwd(d_out, proj_c, p_conv, p_gla, w_out.T, w_oc.T, w_og.T, dm)
    g_out = _matmul_tn(merged, d_out, "grad_w_out")
    g_oc = _matmul_tn(y_conv, d_pc, "grad_w_out_conv")
    g_og = _matmul_tn(y_gla, d_pg, "grad_w_out_gla")
    d_a, g_conv = _conv_bwd(proj_a, dy_conv, conv_w, dm)
    d_b, d_lr, gwp_f, gbp_f, gwp_b, gbp_b, g_gla = _gla_bwd(proj_b, lr, o_all, dy_gla, wgp_f, bg_f, wgp_b, bg_b, gla_g, dm)
    du = _matmul_parts([(d_a, wta), (d_b, wtb), (d_c, wtc), (d_lr, wtlr)], "grad_u")
    g_in = _unpack_rows(_matmul_tn(d_a, u, "grad_w_in_conv"), _matmul_tn(d_b, u, "grad_w_in_gla"),
                        _matmul_tn(d_c, u, "grad_w_in_merge"), _matmul_tn(d_lr, u, "grad_w_in_gate"), dm)
    grad_x, d_meta, g_pre_rows = _prenorm_bwd(du, dy, x, metapad, g_pre, dm)

    grads = dict(
        meta_tokens=jnp.sum(d_meta[:, dm.TM - N_META:, :], axis=0), norm_pre=g_pre_rows[0:1], w_in=g_in,
        conv_w=g_conv[0:3], w_gate_fwd=jnp.sum(gwp_f, axis=0)[:RANK], b_gate_fwd=jnp.sum(gbp_f, axis=0)[0:1],
        w_gate_bwd=jnp.sum(gwp_b, axis=0)[RANK:2 * RANK], b_gate_bwd=jnp.sum(gbp_b, axis=0)[0:1],
        gla_norm=g_gla[0:1], w_out_conv=g_oc, w_out_gla=g_og, w_merge_out=g_out, norm_post=stats[0:1])
    return loss, grad_x, grads


MATRICES = ("w_out_conv", "w_out_gla", "w_merge_out")
SMALL_SHARDED = ("meta_tokens", "conv_w", "w_gate_fwd", "w_gate_bwd")
REPLICATED = ("norm_pre", "b_gate_fwd", "b_gate_bwd", "gla_norm", "norm_post")
NAMES = ("meta_tokens", "norm_pre", "w_in", "conv_w", "w_gate_fwd", "b_gate_fwd", "w_gate_bwd", "b_gate_bwd", "gla_norm",
         "w_out_conv", "w_out_gla", "w_merge_out", "norm_post")
DEPTH_AXIS = ("w_in", "conv_w", "w_gate_fwd", "w_gate_bwd") + MATRICES


def _cols_to_devices(g):
    r, c = g.shape
    return g.reshape(r, N_DEV, c // N_DEV).transpose(1, 0, 2)


def _cols_from_devices(parts):
    n, r, c = parts.shape
    return parts.transpose(1, 0, 2).reshape(r, n * c)


def kernel(x, meta_tokens, norm_pre, w_in, conv_w, w_gate_fwd, b_gate_fwd, w_gate_bwd, b_gate_bwd, gla_norm, w_out_conv, w_out_gla, w_merge_out, norm_post, loss_target, m_meta_tokens, m_norm_pre, m_w_in, m_conv_w, m_w_gate_fwd, m_b_gate_fwd, m_w_gate_bwd, m_b_gate_bwd, m_gla_norm, m_w_out_conv, m_w_out_gla, m_w_merge_out, m_norm_post, v_meta_tokens, v_norm_pre, v_w_in, v_conv_w, v_w_gate_fwd, v_b_gate_fwd, v_w_gate_bwd, v_b_gate_bwd, v_gla_norm, v_w_out_conv, v_w_out_gla, v_w_merge_out, v_norm_post):
    w = dict(meta_tokens=meta_tokens, norm_pre=norm_pre, w_in=w_in[0], conv_w=conv_w[0], w_gate_fwd=w_gate_fwd[0],
             b_gate_fwd=b_gate_fwd, w_gate_bwd=w_gate_bwd[0], b_gate_bwd=b_gate_bwd, gla_norm=gla_norm,
             w_out_conv=w_out_conv[0], w_out_gla=w_out_gla[0], w_merge_out=w_merge_out[0], norm_post=norm_post)
    m = dict(meta_tokens=m_meta_tokens, norm_pre=m_norm_pre, w_in=m_w_in[0], conv_w=m_conv_w[0], w_gate_fwd=m_w_gate_fwd[0],
             b_gate_fwd=m_b_gate_fwd, w_gate_bwd=m_w_gate_bwd[0], b_gate_bwd=m_b_gate_bwd, gla_norm=m_gla_norm,
             w_out_conv=m_w_out_conv[0], w_out_gla=m_w_out_gla[0], w_merge_out=m_w_merge_out[0], norm_post=m_norm_post)
    v = dict(meta_tokens=v_meta_tokens, norm_pre=v_norm_pre, w_in=v_w_in[0], conv_w=v_conv_w[0], w_gate_fwd=v_w_gate_fwd[0],
             b_gate_fwd=v_b_gate_fwd, w_gate_bwd=v_w_gate_bwd[0], b_gate_bwd=v_b_gate_bwd, gla_norm=v_gla_norm,
             w_out_conv=v_w_out_conv[0], w_out_gla=v_w_out_gla[0], w_merge_out=v_w_merge_out[0], norm_post=v_norm_post)
    d = x.shape[-1]

    small_blob = _to_blob([w[n][None] for n in SMALL_SHARDED], F32, SMALL_ROWS)[0]
    gathered = _exchange([w["w_in"].T.astype(BF16)] + [w[n].astype(BF16) for n in MATRICES] + [small_blob], [],
                         "gather_weights")
    wt_in, w_oc, w_og, w_out = (a.reshape(-1, d) for a in gathered[:4])
    small = {n: _cols_from_devices(p) for n, p in zip(SMALL_SHARDED, _from_blob(gathered[4], [w[n].shape for n in SMALL_SHARDED]))}

    loss, grad_x, grads = _local_step(
        x, loss_target, small["meta_tokens"], norm_pre, wt_in, small["conv_w"], small["w_gate_fwd"], b_gate_fwd,
        small["w_gate_bwd"], b_gate_bwd, gla_norm, w_oc, w_og, w_out, norm_post)
    loss = lax.psum(loss, ("x", "y", "c"))

    to_send = [grads[n].astype(BF16).reshape(N_DEV, -1, d) for n in ("w_in",) + MATRICES]
    to_send.append(_to_blob([_cols_to_devices(grads[n]) for n in SMALL_SHARDED], F32, SMALL_ROWS))
    repl_blob = _to_blob([grads[n][None] for n in REPLICATED], F32, SMALL_ROWS)[0]
    received = _exchange([repl_blob], to_send, "exchange_grads")

    results = {"w_in": _adamw(_sum_partials(received[1], "sum_grad_w_in").T[None], w["w_in"], m["w_in"], v["w_in"], "adamw_w_in")}
    for n, partials in zip(MATRICES, received[2:5]):
        results[n] = _adamw(partials, w[n], m[n], v[n], "adamw_" + n)
    for names, partials, tag in ((SMALL_SHARDED, received[5], "small"), (REPLICATED, received[0], "replicated")):
        blobs = [_to_blob([t[n][None] for n in names], F32, SMALL_ROWS)[0] for t in (w, m, v)]
        res = [_from_blob(r[None], [w[n].shape for n in names]) for r in _adamw(partials, *blobs, name="adamw_" + tag)]
        for i, n in enumerate(names):
            results[n] = [r[i][0] for r in res]
    lead = lambda n, t: t[None] if n in DEPTH_AXIS else t
    return (loss, grad_x, *[lead(n, results[n][i]) for i in range(4) for n in NAMES])
```

```python
import functools

import jax
import jax.numpy as jnp
import numpy as np
from jax import lax
from jax.experimental import pallas as pl
from jax.experimental.pallas import tpu as pltpu

F32 = jnp.float32
BF16 = jnp.bfloat16
MESH = pl.DeviceIdType.MESH

N_META = 16
CHUNK = 64
CHUNK_SHIFT = 6
HEADS = 4
RANK = 16
LR_LANES = 128
PAD_ROWS = CHUNK - N_META
EPS = 1e-6
GATE_NORMALIZER = 16.0
N_DEV = 8
ADAM_LR, ADAM_B1, ADAM_B2, ADAM_EPS, ADAM_WD, ADAM_STEP = 0.001, 0.9, 0.999, 1e-08, 0.01, 10
VMEM_LIMIT_BYTES = 56 * 1024 * 1024
BLOB_LANES = 512
SMALL_ROWS = 16


class _Dims:
    def __init__(self, bl, s, d):
        self.Bl, self.S, self.D = bl, s, d
        self.TM = 256 if s % 256 == 0 else CHUNK
        self.LP = self.TM + s
        self.T = bl * self.LP
        self.TPS = self.LP // self.TM
        self.NC = self.LP // CHUNK
        self.C0 = (self.TM - CHUNK) // CHUNK
        self.DK, self.DV = d // 2, d
        self.HK, self.HV = self.DK // HEADS, self.DV // HEADS
        self.HW = 2 * self.HK + 2 * self.HV
        self.CW = 256 if d % 256 == 0 and d > 256 else d // 4
        self.NJ = d // self.CW


def _pick(n, target, mult):
    t = min(n, target)
    while t >= mult:
        if n % t == 0 and t % mult == 0:
            return t
        t -= mult
    return n


def _cp(n_axes):
    return pltpu.CompilerParams(dimension_semantics=("arbitrary",) * n_axes, vmem_limit_bytes=VMEM_LIMIT_BYTES)


def _sigmoid(x):
    return 1.0 / (1.0 + jnp.exp(-x))


def _dot(a, b):
    return jnp.dot(a, b, preferred_element_type=F32)


def _dot_nt(a, b):
    return lax.dot_general(a, b, (((1,), (1,)), ((), ())), preferred_element_type=F32)


def _dot_tn(a, b):
    return lax.dot_general(a, b, (((0,), (0,)), ((), ())), preferred_element_type=F32)


def _dot_exact01(m01, x):
    hi = x.astype(BF16)
    lo = (x - hi.astype(F32)).astype(BF16)
    return _dot(m01, hi) + _dot(m01, lo)


def _exchange(gathers, scatters, name):
    arrays = list(gathers) + list(scatters)
    n, ng = len(arrays), len(gathers)

    def body(*refs):
        ins, outs = refs[:n], refs[n:2 * n]
        send_sems, recv_sems, local_sems = refs[2 * n:]
        x, y, c = lax.axis_index("x"), lax.axis_index("y"), lax.axis_index("c")
        me = 4 * x + 2 * y + c
        started = []
        for t in range(n):
            src, dst = ins[t], outs[t]
            own = pltpu.make_async_copy(src if t < ng else src.at[me], dst.at[me], local_sems.at[t])
            own.start()
            started.append(own)
            for k in range(1, N_DEV):
                px = 1 - x if (k >> 2) & 1 else x
                py = 1 - y if (k >> 1) & 1 else y
                pc = 1 - c if k & 1 else c
                peer = 4 * px + 2 * py + pc
                cp = pltpu.make_async_remote_copy(
                    src_ref=src if t < ng else src.at[peer], dst_ref=dst.at[me],
                    send_sem=send_sems.at[t * (N_DEV - 1) + k - 1], recv_sem=recv_sems.at[t * (N_DEV - 1) + k - 1],
                    device_id=(px, py, pc), device_id_type=MESH)
                cp.start()
                started.append(cp)
        for cp in started:
            cp.wait()

    out_shape = [jax.ShapeDtypeStruct((N_DEV,) + a.shape[-2:], a.dtype) for a in arrays]
    any_spec = pl.BlockSpec(memory_space=pl.ANY)
    return pl.pallas_call(
        body, name=name, out_shape=out_shape, in_specs=[any_spec] * n, out_specs=[any_spec] * n,
        scratch_shapes=[pltpu.SemaphoreType.DMA((n * (N_DEV - 1),)), pltpu.SemaphoreType.DMA((n * (N_DEV - 1),)),
                        pltpu.SemaphoreType.DMA((n,))],
        compiler_params=pltpu.CompilerParams(has_side_effects=True),
    )(*arrays)


def _prenorm(x, metapad, g_pre, dm):
    tm, tps, d = dm.TM, dm.TPS, dm.D

    def body(x_ref, mp_ref, g_ref, u_ref):
        j = pl.program_id(0) % tps
        h = jnp.where(j == 0, mp_ref[...], x_ref[0])
        r = lax.rsqrt(jnp.mean(h * h, axis=-1, keepdims=True) + EPS)
        u_ref[...] = (h * r * g_ref[...]).astype(BF16)

    return pl.pallas_call(
        body, name="prenorm", grid=(dm.Bl * tps,),
        in_specs=[pl.BlockSpec((1, tm, d), lambda i: (i // tps, jnp.maximum(i % tps - 1, 0), 0)),
                  pl.BlockSpec((tm, d), lambda i: (0, 0)),
                  pl.BlockSpec((1, d), lambda i: (0, 0))],
        out_specs=pl.BlockSpec((tm, d), lambda i: (i, 0)),
        out_shape=jax.ShapeDtypeStruct((dm.T, d), BF16), compiler_params=_cp(1),
    )(x, metapad, g_pre)


def _matmul(a, b, out_dtype, name, tm=1024, tn=1024):
    m, k = a.shape
    n = b.shape[1]
    tm, tn = _pick(m, tm, 16), _pick(n, tn, 128)

    def body(a_ref, b_ref, o_ref):
        o_ref[...] = _dot(a_ref[...].astype(BF16), b_ref[...]).astype(out_dtype)

    return pl.pallas_call(
        body, name=name, grid=(n // tn, m // tm),
        in_specs=[pl.BlockSpec((tm, k), lambda j, i: (i, 0)), pl.BlockSpec((k, tn), lambda j, i: (0, j))],
        out_specs=pl.BlockSpec((tm, tn), lambda j, i: (i, j)),
        out_shape=jax.ShapeDtypeStruct((m, n), out_dtype), compiler_params=_cp(2),
    )(a, b)


def _matmul_parts(parts, name, tm=512, tk=1024):
    m = parts[0][0].shape[0]
    n = parts[0][1].shape[1]
    tm = _pick(m, tm, 16)
    tks = [_pick(a.shape[1], tk, 128) for a, _ in parts]
    counts = [a.shape[1] // t for (a, _), t in zip(parts, tks)]
    starts = [int(s) for s in np.cumsum([0] + counts[:-1])]
    total = sum(counts)

    def body(*refs):
        acc = refs[-1]
        o_ref = refs[-2]
        k = pl.program_id(1)

        @pl.when(k == 0)
        def _():
            acc[...] = jnp.zeros_like(acc)

        for p in range(len(parts)):
            @pl.when((k >= starts[p]) & (k < starts[p] + counts[p]))
            def _(p=p):
                acc[...] += _dot(refs[2 * p][...].astype(BF16), refs[2 * p + 1][...])

        @pl.when(k == total - 1)
        def _():
            o_ref[...] = acc[...]

    in_specs, operands = [], []
    for (a, b), t, s, cnt in zip(parts, tks, starts, counts):
        in_specs.append(pl.BlockSpec((tm, t), lambda i, k, s=s, cnt=cnt: (i, jnp.clip(k - s, 0, cnt - 1))))
        in_specs.append(pl.BlockSpec((t, n), lambda i, k, s=s, cnt=cnt: (jnp.clip(k - s, 0, cnt - 1), 0)))
        operands += [a, b]
    return pl.pallas_call(
        body, name=name, grid=(m // tm, total), in_specs=in_specs,
        out_specs=pl.BlockSpec((tm, n), lambda i, k: (i, 0)),
        out_shape=jax.ShapeDtypeStruct((m, n), F32), scratch_shapes=[pltpu.VMEM((tm, n), F32)],
        compiler_params=_cp(2),
    )(*operands)


def _matmul_tn(a, b, name, tt=768, tn=1024):
    t, k = a.shape
    n = b.shape[1]
    tt, tn, tk = _pick(t, tt, 16), _pick(n, tn, 128), _pick(k, 1024, 128)

    def body(a_ref, b_ref, o_ref):
        p = _dot_tn(a_ref[...].astype(BF16), b_ref[...].astype(BF16))

        @pl.when(pl.program_id(2) == 0)
        def _():
            o_ref[...] = p

        @pl.when(pl.program_id(2) > 0)
        def _():
            o_ref[...] += p

    return pl.pallas_call(
        body, name=name, grid=(k // tk, n // tn, t // tt),
        in_specs=[pl.BlockSpec((tt, tk), lambda kk, j, i: (i, kk)), pl.BlockSpec((tt, tn), lambda kk, j, i: (i, j))],
        out_specs=pl.BlockSpec((tk, tn), lambda kk, j, i: (kk, j)),
        out_shape=jax.ShapeDtypeStruct((k, n), F32), compiler_params=_cp(3),
    )(a, b)


def _conv_rows(dm):
    return _pick(dm.LP, 256, 16)


def _shifted(m, prev_row, next_row, rows):
    row = lax.broadcasted_iota(jnp.int32, m.shape, 0)
    m_prev = jnp.where(row == 0, prev_row, pltpu.roll(m, 1, 0))
    m_next = jnp.where(row == rows - 1, next_row, pltpu.roll(m, rows - 1, 0))
    return m_prev, m_next


def _conv_fwd(proj_a, conv_w, dm):
    lp, cw, rc = dm.LP, dm.CW, _conv_rows(dm)
    nchunk = lp // rc

    def body(p_ref, w_ref, y_ref):
        w0, w1, w2 = w_ref[0:1, :], w_ref[1:2, :], w_ref[2:3, :]

        def chunk(ci, carry):
            r0 = pl.multiple_of(ci * rc, rc)
            blk = p_ref[pl.ds(r0, rc), :].astype(F32)
            cb, cc, cx, cz = (blk[:, i * cw:(i + 1) * cw] for i in range(4))
            m = cc * cx
            rp = pl.multiple_of(jnp.maximum(r0 - 16, 0), 16)
            rn = pl.multiple_of(jnp.minimum(r0 + rc, lp - 16), 16)
            pv = p_ref[pl.ds(rp, 16), cw:3 * cw].astype(F32)
            nx = p_ref[pl.ds(rn, 16), cw:3 * cw].astype(F32)
            prev_row = jnp.where(ci > 0, pv[15:16, :cw] * pv[15:16, cw:], 0.0)
            next_row = jnp.where(ci < nchunk - 1, nx[0:1, :cw] * nx[0:1, cw:], 0.0)
            m_prev, m_next = _shifted(m, prev_row, next_row, rc)
            s = w0 * m_prev + w1 * m + w2 * m_next
            y_ref[pl.ds(r0, rc), :] = (cb * s * (cz * _sigmoid(cz))).astype(BF16)
            return carry

        lax.fori_loop(0, nchunk, chunk, 0)

    return pl.pallas_call(
        body, name="conv_fwd", grid=(dm.Bl, dm.NJ),
        in_specs=[pl.BlockSpec((lp, 4 * cw), lambda s, j: (s, j)), pl.BlockSpec((3, cw), lambda s, j: (0, j))],
        out_specs=pl.BlockSpec((lp, cw), lambda s, j: (s, j)),
        out_shape=jax.ShapeDtypeStruct((dm.T, dm.D), BF16), compiler_params=_cp(2),
    )(proj_a, conv_w)


def _conv_bwd(proj_a, dy_conv, conv_w, dm):
    lp, cw, rc = dm.LP, dm.CW, _conv_rows(dm)
    nchunk = lp // rc

    def body(p_ref, dy_ref, w_ref, d_ref, gw_ref):
        w0, w1, w2 = w_ref[0:1, :], w_ref[1:2, :], w_ref[2:3, :]

        def ds_of(p4, dy):
            cb, cz = p4[:, :cw], p4[:, 3 * cw:]
            return dy * cb * (cz * _sigmoid(cz))

        def chunk(ci, carry):
            g0, g1, g2 = carry
            r0 = pl.multiple_of(ci * rc, rc)
            blk = p_ref[pl.ds(r0, rc), :].astype(F32)
            dy = dy_ref[pl.ds(r0, rc), :].astype(F32)
            cb, cc, cx, cz = (blk[:, i * cw:(i + 1) * cw] for i in range(4))
            rp = pl.multiple_of(jnp.maximum(r0 - 16, 0), 16)
            rn = pl.multiple_of(jnp.minimum(r0 + rc, lp - 16), 16)
            pv = p_ref[pl.ds(rp, 16), :].astype(F32)[15:16]
            nx = p_ref[pl.ds(rn, 16), :].astype(F32)[0:1]
            dpv = dy_ref[pl.ds(rp, 16), :].astype(F32)[15:16]
            dnx = dy_ref[pl.ds(rn, 16), :].astype(F32)[0:1]
            has_prev, has_next = ci > 0, ci < nchunk - 1
            m = cc * cx
            m_prev, m_next = _shifted(m, jnp.where(has_prev, pv[:, cw:2 * cw] * pv[:, 2 * cw:3 * cw], 0.0),
                                      jnp.where(has_next, nx[:, cw:2 * cw] * nx[:, 2 * cw:3 * cw], 0.0), rc)
            s = w0 * m_prev + w1 * m + w2 * m_next
            sg = _sigmoid(cz)
            silu = cz * sg
            ds = dy * cb * silu
            ds_prev, ds_next = _shifted(ds, jnp.where(has_prev, ds_of(pv, dpv), 0.0),
                                        jnp.where(has_next, ds_of(nx, dnx), 0.0), rc)
            dm_ = w0 * ds_next + w1 * ds + w2 * ds_prev
            d_ref[pl.ds(r0, rc), 0:cw] = (dy * s * silu).astype(BF16)
            d_ref[pl.ds(r0, rc), cw:2 * cw] = (dm_ * cx).astype(BF16)
            d_ref[pl.ds(r0, rc), 2 * cw:3 * cw] = (dm_ * cc).astype(BF16)
            d_ref[pl.ds(r0, rc), 3 * cw:4 * cw] = (dy * cb * s * (sg * (1.0 + cz * (1.0 - sg)))).astype(BF16)
            return (g0 + jnp.sum(ds * m_prev, axis=0, keepdims=True), g1 + jnp.sum(ds * m, axis=0, keepdims=True),
                    g2 + jnp.sum(ds * m_next, axis=0, keepdims=True))

        z = jnp.zeros((1, cw), F32)
        g0, g1, g2 = lax.fori_loop(0, nchunk, chunk, (z, z, z))

        @pl.when(pl.program_id(1) == 0)
        def _():
            gw_ref[...] = jnp.zeros_like(gw_ref)

        gw_ref[0:1, :] += g0
        gw_ref[1:2, :] += g1
        gw_ref[2:3, :] += g2

    return pl.pallas_call(
        body, name="conv_bwd", grid=(dm.NJ, dm.Bl),
        in_specs=[pl.BlockSpec((lp, 4 * cw), lambda j, s: (s, j)), pl.BlockSpec((lp, cw), lambda j, s: (s, j)),
                  pl.BlockSpec((3, cw), lambda j, s: (0, j))],
        out_specs=[pl.BlockSpec((lp, 4 * cw), lambda j, s: (s, j)), pl.BlockSpec((8, cw), lambda j, s: (0, j))],
        out_shape=[jax.ShapeDtypeStruct((dm.T, 4 * dm.D), BF16), jax.ShapeDtypeStruct((8, dm.D), F32)],
        compiler_params=_cp(2),
    )(proj_a, dy_conv, conv_w)


def _group_chunks(dm):
    n = dm.NC - dm.C0
    return 3 if n % 3 == 0 else 1


def _group_masks(rows):
    ii = lax.broadcasted_iota(jnp.int32, (rows, rows), 0)
    jj = lax.broadcasted_iota(jnp.int32, (rows, rows), 1)
    same = jnp.right_shift(ii, CHUNK_SHIFT) == jnp.right_shift(jj, CHUNK_SHIFT)
    low, up = same & (jj <= ii), same & (jj >= ii)
    return low, same & (jj > ii), low.astype(BF16), up.astype(BF16), same.astype(BF16)


def _log_gate(lr_rows, w_ref, b_ref, first_group, hk):
    z = _dot(lr_rows, w_ref[...]) + b_ref[...]
    e = jnp.exp(-jnp.abs(z))
    g = (jnp.minimum(z, 0.0) - jnp.log(1.0 + e)) * (1.0 / GATE_NORMALIZER)
    dg_dz = jnp.where(z >= 0.0, e, 1.0) / (1.0 + e) * (1.0 / GATE_NORMALIZER)
    row = lax.broadcasted_iota(jnp.int32, (lr_rows.shape[0], hk), 0)
    pad = first_group & (row < PAD_ROWS)
    return jnp.where(pad, 0.0, g), jnp.where(pad, 0.0, dg_dz)


def _gla_fwd(proj_b, lr, wg_f, bg_f, wg_b, bg_b, gla_g, dm):
    lp, hk, hv, nc, c0, hw = dm.LP, dm.HK, dm.HV, dm.NC, dm.C0, dm.HW
    scale = hk ** -0.5
    gc = _group_chunks(dm)
    gr, ng = gc * CHUNK, (nc - c0) // gc

    def body(p_ref, lr_ref, wf_ref, bf_ref, wb_ref, bb_ref, gg_ref, o_ref, y_ref, oacc):
        low_incl, up_strict, ones_low, ones_up, ones_same = _group_masks(gr)
        if c0 > 0:
            oacc[0:c0 * CHUNK, :] = jnp.zeros((c0 * CHUNK, hv), F32)

        def run(fwd):
            w_ref, b_ref = (wf_ref, bf_ref) if fwd else (wb_ref, bb_ref)

            def step(i, st):
                gi = i if fwd else ng - 1 - i
                r0 = pl.multiple_of((c0 + gi * gc) * CHUNK, CHUNK)
                blk = p_ref[pl.ds(r0, gr), :]
                q = blk[:, :hk].astype(F32) * scale
                k = blk[:, hk:2 * hk].astype(F32)
                v = blk[:, 2 * hk:2 * hk + hv]
                g, _ = _log_gate(lr_ref[pl.ds(r0, gr), :], w_ref, b_ref, gi == 0, hk)
                b = _dot_exact01(ones_low if fwd else ones_up, g)
                btot = _dot_exact01(ones_same, g)
                qi = (q * jnp.exp(b)).astype(BF16)
                ki = (k * jnp.exp(-b)).astype(BF16)
                kd = (k * jnp.exp(btot - b)).astype(BF16)
                dec = jnp.exp(btot)
                a = jnp.where(low_incl if fwd else up_strict, _dot_nt(qi, ki), 0.0)
                o = _dot(a.astype(BF16), v)
                for c in (range(gc) if fwd else reversed(range(gc))):
                    rows = slice(c * CHUNK, (c + 1) * CHUNK)
                    o_c = o[rows] + _dot_nt(qi[rows], st.astype(BF16))
                    st = st * dec[c * CHUNK:c * CHUNK + 1] + _dot_tn(v[rows], kd[rows])
                    if fwd:
                        oacc[pl.ds(r0 + c * CHUNK, CHUNK), :] = o_c
                    else:
                        oacc[pl.ds(r0 + c * CHUNK, CHUNK), :] += o_c
                return st

            lax.fori_loop(0, ng, step, jnp.zeros((hv, hk), F32))

        run(True)
        run(False)

        def finish(i, carry):
            r0 = pl.multiple_of(i * CHUNK, CHUNK)
            o = oacc[pl.ds(r0, CHUNK), :]
            r = p_ref[pl.ds(r0, CHUNK), 2 * hk + hv:].astype(F32)
            on = o * lax.rsqrt(jnp.mean(o * o, axis=-1, keepdims=True) + EPS) * gg_ref[...]
            o_ref[pl.ds(r0, CHUNK), :] = o.astype(BF16)
            y_ref[pl.ds(r0, CHUNK), :] = (on * r * _sigmoid(r)).astype(BF16)
            return carry

        lax.fori_loop(0, nc, finish, 0, unroll=2)

    head = lambda s, h: (s, h)
    wspec = pl.BlockSpec((LR_LANES, hk), lambda s, h: (0, h))
    bspec = pl.BlockSpec((1, hk), lambda s, h: (0, h))
    return pl.pallas_call(
        body, name="gla_fwd", grid=(dm.Bl, HEADS),
        in_specs=[pl.BlockSpec((lp, hw), head), pl.BlockSpec((lp, LR_LANES), lambda s, h: (s, 0)),
                  wspec, bspec, wspec, bspec, pl.BlockSpec((1, hv), lambda s, h: (0, 0))],
        out_specs=[pl.BlockSpec((lp, hv), head), pl.BlockSpec((lp, hv), head)],
        out_shape=[jax.ShapeDtypeStruct((dm.T, dm.DV), BF16), jax.ShapeDtypeStruct((dm.T, dm.DV), BF16)],
        scratch_shapes=[pltpu.VMEM((lp, hv), F32)],
        compiler_params=_cp(2),
    )(proj_b, lr, wg_f, bg_f, wg_b, bg_b, gla_g)


def _gla_bwd(proj_b, lr, o_all, dy_gla, wg_f, bg_f, wg_b, bg_b, gla_g, dm):
    lp, hk, hv, nc, c0, hw = dm.LP, dm.HK, dm.HV, dm.NC, dm.C0, dm.HW
    scale = hk ** -0.5
    gc = _group_chunks(dm)
    gr, ng = gc * CHUNK, (nc - c0) // gc

    def body(p_ref, lr_ref, o_ref, dy_ref, wf_ref, bf_ref, wb_ref, bb_ref, gg_ref,
             d_ref, dlr_ref, gwf_ref, gbf_ref, gwb_ref, gbb_ref, ggg_ref,
             do_s, s_all, b_s, bt_s, gs_s, dq_s, dk_s, dv_s):
        low_incl, up_strict, ones_low, ones_up, ones_same = _group_masks(gr)
        h = pl.program_id(1)

        @pl.when(h == 0)
        def _():
            dlr_ref[...] = jnp.zeros_like(dlr_ref)

        if c0 > 0:
            zr = c0 * CHUNK
            d_ref[0:zr, :] = jnp.zeros((zr, hw), BF16)

        def norm_bwd(i, ggg):
            r0 = pl.multiple_of(i * CHUNK, CHUNK)
            o = o_ref[pl.ds(r0, CHUNK), :].astype(F32)
            dy = dy_ref[pl.ds(r0, CHUNK), :].astype(F32)
            r = p_ref[pl.ds(r0, CHUNK), 2 * hk + hv:].astype(F32)
            rstd = lax.rsqrt(jnp.mean(o * o, axis=-1, keepdims=True) + EPS)
            ohat = o * rstd
            sg = _sigmoid(r)
            d_on = dy * (r * sg)
            d_ref[pl.ds(r0, CHUNK), 2 * hk + hv:] = (dy * ohat * gg_ref[...] * (sg * (1.0 + r * (1.0 - sg)))).astype(BF16)
            d_oh = d_on * gg_ref[...]
            do_s[pl.ds(r0, CHUNK), :] = (rstd * (d_oh - ohat * jnp.mean(d_oh * ohat, axis=-1, keepdims=True))).astype(BF16)
            return ggg + jnp.sum(d_on * ohat, axis=0, keepdims=True)

        ggg = lax.fori_loop(c0, nc, norm_bwd, jnp.zeros((1, hv), F32))

        @pl.when((pl.program_id(0) == 0) & (h == 0))
        def _():
            ggg_ref[...] = jnp.zeros_like(ggg_ref)

        ggg_ref[0:1, :] += ggg

        def run(fwd):
            w_ref, b_ref = (wf_ref, bf_ref) if fwd else (wb_ref, bb_ref)
            gw_ref, gb_ref = (gwf_ref, gbf_ref) if fwd else (gwb_ref, gbb_ref)
            cum, cum_t = (ones_low, ones_up) if fwd else (ones_up, ones_low)
            mask = low_incl if fwd else up_strict
            state_order = list(range(gc)) if fwd else list(reversed(range(gc)))

            def load(gi):
                r0 = pl.multiple_of((c0 + gi * gc) * CHUNK, CHUNK)
                blk = p_ref[pl.ds(r0, gr), :]
                return r0, blk[:, :hk].astype(F32) * scale, blk[:, hk:2 * hk].astype(F32), blk[:, 2 * hk:2 * hk + hv]

            def record(i, st):
                gi = i if fwd else ng - 1 - i
                r0, q, k, v = load(gi)
                g, dg_dz = _log_gate(lr_ref[pl.ds(r0, gr), :], w_ref, b_ref, gi == 0, hk)
                b = _dot_exact01(cum, g)
                btot = _dot_exact01(ones_same, g)
                b_s[pl.ds(r0, gr), :] = b
                bt_s[pl.ds(r0, gr), :] = btot
                gs_s[pl.ds(r0, gr), :] = dg_dz
                kd = (k * jnp.exp(btot - b)).astype(BF16)
                dec = jnp.exp(btot)
                for c in state_order:
                    rows = slice(c * CHUNK, (c + 1) * CHUNK)
                    s_all[c0 + gi * gc + c] = st
                    st = st * dec[c * CHUNK:c * CHUNK + 1] + _dot_tn(v[rows], kd[rows])
                return st

            lax.fori_loop(0, ng, record, jnp.zeros((hv, hk), F32))

            def grad(i, carry):
                dst, gw, gb = carry
                gi = ng - 1 - i if fwd else i
                r0, q, k, v = load(gi)
                b = b_s[pl.ds(r0, gr), :]
                btot = bt_s[pl.ds(r0, gr), :]
                eb, enb, edb, dec = jnp.exp(b), jnp.exp(-b), jnp.exp(btot - b), jnp.exp(btot)
                qi_f, ki_f, kd_f = q * eb, k * enb, k * edb
                qi, ki, kd = qi_f.astype(BF16), ki_f.astype(BF16), kd_f.astype(BF16)
                do = do_s[pl.ds(r0, gr), :]
                a = jnp.where(mask, _dot_nt(qi, ki), 0.0).astype(BF16)
                da = jnp.where(mask, _dot_nt(do, v), 0.0).astype(BF16)
                dv = _dot_tn(a, do)
                dqi = _dot(da, ki)
                dki = _dot_tn(da, qi)
                dv_c, dqi_c, dkd_c, extra_c = [None] * gc, [None] * gc, [None] * gc, [None] * gc
                for c in reversed(state_order):
                    rows = slice(c * CHUNK, (c + 1) * CHUNK)
                    st = s_all[c0 + gi * gc + c]
                    dsn_b = dst.astype(BF16)
                    dec_c = dec[c * CHUNK:c * CHUNK + 1]
                    dv_c[c] = dv[rows] + _dot_nt(kd[rows], dsn_b)
                    dqi_c[c] = dqi[rows] + _dot(do[rows], st.astype(BF16))
                    dkd_c[c] = _dot(v[rows], dsn_b)
                    ddec = jnp.sum(st * dst, axis=0, keepdims=True)
                    extra = jnp.sum(dkd_c[c] * kd_f[rows], axis=0, keepdims=True) + ddec * dec_c
                    extra_c[c] = jnp.broadcast_to(extra, (CHUNK, hk))
                    dst = dst * dec_c + _dot_tn(do[rows], qi[rows])
                dv, dqi = jnp.concatenate(dv_c, axis=0), jnp.concatenate(dqi_c, axis=0)
                dkd, extra = jnp.concatenate(dkd_c, axis=0), jnp.concatenate(extra_c, axis=0)
                dq = dqi * eb * scale
                dk = dki * enb + dkd * edb
                db = dqi * qi_f - dki * ki_f - dkd * kd_f
                dg = _dot_exact01(cum_t, db) + extra
                dz = dg * gs_s[pl.ds(r0, gr), :]
                dz_b = dz.astype(BF16)
                lrc = lr_ref[pl.ds(r0, gr), :]
                dlr_ref[pl.ds(r0, gr), :] += _dot_nt(dz_b, w_ref[...])
                if fwd:
                    dq_s[pl.ds(r0, gr), :] = dq
                    dk_s[pl.ds(r0, gr), :] = dk
                    dv_s[pl.ds(r0, gr), :] = dv
                else:
                    d_ref[pl.ds(r0, gr), 0:hk] = (dq_s[pl.ds(r0, gr), :] + dq).astype(BF16)
                    d_ref[pl.ds(r0, gr), hk:2 * hk] = (dk_s[pl.ds(r0, gr), :] + dk).astype(BF16)
                    d_ref[pl.ds(r0, gr), 2 * hk:2 * hk + hv] = (dv_s[pl.ds(r0, gr), :] + dv).astype(BF16)
                return dst, gw + _dot_tn(lrc, dz_b), gb + jnp.sum(dz, axis=0, keepdims=True)

            _, gw, gb = lax.fori_loop(0, ng, grad, (jnp.zeros((hv, hk), F32), jnp.zeros((LR_LANES, hk), F32),
                                                    jnp.zeros((1, hk), F32)))
            gw_ref[0] = gw
            gb_ref[0] = jnp.zeros((8, hk), F32)
            gb_ref[0, 0:1, :] = gb

        run(True)
        run(False)

    head = lambda s, h: (s, h)
    wspec = pl.BlockSpec((LR_LANES, hk), lambda s, h: (0, h))
    bspec = pl.BlockSpec((1, hk), lambda s, h: (0, h))
    gwspec = pl.BlockSpec((1, LR_LANES, hk), lambda s, h: (s, 0, h))
    gbspec = pl.BlockSpec((1, 8, hk), lambda s, h: (s, 0, h))
    gw_shape = jax.ShapeDtypeStruct((dm.Bl, LR_LANES, dm.DK), F32)
    gb_shape = jax.ShapeDtypeStruct((dm.Bl, 8, dm.DK), F32)
    return pl.pallas_call(
        body, name="gla_bwd", grid=(dm.Bl, HEADS),
        in_specs=[pl.BlockSpec((lp, hw), head), pl.BlockSpec((lp, LR_LANES), lambda s, h: (s, 0)),
                  pl.BlockSpec((lp, hv), head), pl.BlockSpec((lp, hv), head),
                  wspec, bspec, wspec, bspec, pl.BlockSpec((1, hv), lambda s, h: (0, 0))],
        out_specs=[pl.BlockSpec((lp, hw), head), pl.BlockSpec((lp, LR_LANES), lambda s, h: (s, 0)),
                   gwspec, gbspec, gwspec, gbspec, pl.BlockSpec((8, hv), lambda s, h: (0, 0))],
        out_shape=[jax.ShapeDtypeStruct((dm.T, HEADS * hw), BF16), jax.ShapeDtypeStruct((dm.T, LR_LANES), F32),
                   gw_shape, gb_shape, gw_shape, gb_shape, jax.ShapeDtypeStruct((8, hv), F32)],
        scratch_shapes=[pltpu.VMEM((lp, hv), BF16), pltpu.VMEM((nc, hv, hk), F32), pltpu.VMEM((lp, hk), F32),
                        pltpu.VMEM((lp, hk), F32), pltpu.VMEM((lp, hk), F32), pltpu.VMEM((lp, hk), F32),
                        pltpu.VMEM((lp, hk), F32), pltpu.VMEM((lp, hv), F32)],
        compiler_params=_cp(2),
    )(proj_b, lr, o_all, dy_gla, wg_f, bg_f, wg_b, bg_b, gla_g)


def _out_merge(y_conv, y_gla, proj_c, w_oc, w_og, dm):
    d = dm.D
    tm = _pick(dm.T, 512, 16)

    def body(yc_ref, yg_ref, c_ref, woc_ref, wog_ref, pc_ref, pg_ref, m_ref):
        pc = _dot(yc_ref[...], woc_ref[...])
        pg = _dot(yg_ref[...], wog_ref[...])
        pc_ref[...] = pc.astype(BF16)
        pg_ref[...] = pg.astype(BF16)
        ma = c_ref[:, :d].astype(F32)
        mb = c_ref[:, d:].astype(F32)
        m_ref[...] = (_sigmoid(ma) * pc + _sigmoid(mb) * pg).astype(BF16)

    row = pl.BlockSpec((tm, d), lambda i: (i, 0))
    full = pl.BlockSpec((d, d), lambda i: (0, 0))
    act = jax.ShapeDtypeStruct((dm.T, d), BF16)
    return pl.pallas_call(
        body, name="out_merge", grid=(dm.T // tm,),
        in_specs=[row, row, pl.BlockSpec((tm, 2 * d), lambda i: (i, 0)), full, full],
        out_specs=[row, row, row], out_shape=[act, act, act], compiler_params=_cp(1),
    )(y_conv, y_gla, proj_c, w_oc, w_og)


def _final_fwd(merged, w_out, x, metapad, target, g_post, dm):
    tm, tps, d = dm.TM, dm.TPS, dm.D

    def body(m_ref, w_ref, x_ref, mp_ref, t_ref, g_ref, dout_ref, dy_ref, st_ref):
        i = pl.program_id(0)
        j = i % tps
        out = _dot(m_ref[...], w_ref[...])
        rstd = lax.rsqrt(jnp.mean(out * out, axis=-1, keepdims=True) + EPS)
        ohat = out * rstd
        h = jnp.where(j == 0, mp_ref[...], x_ref[0])
        y = h + ohat * g_ref[...]
        err = jnp.where(j == 0, 0.0, y - t_ref[0])
        dy = err * (1.0 / d)
        d_oh = dy * g_ref[...]
        dout_ref[...] = (rstd * (d_oh - ohat * jnp.mean(d_oh * ohat, axis=-1, keepdims=True))).astype(BF16)
        dy_ref[...] = dy

        @pl.when(i == 0)
        def _():
            st_ref[...] = jnp.zeros_like(st_ref)

        st_ref[0:1, :] += jnp.sum(dy * ohat, axis=0, keepdims=True)
        st_ref[1:2, :] += jnp.sum(err * err, axis=0, keepdims=True)

    row = pl.BlockSpec((tm, d), lambda i: (i, 0))
    tok = pl.BlockSpec((1, tm, d), lambda i: (i // tps, jnp.maximum(i % tps - 1, 0), 0))
    const = lambda r: pl.BlockSpec((r, d), lambda i: (0, 0))
    return pl.pallas_call(
        body, name="final_fwd", grid=(dm.Bl * tps,),
        in_specs=[row, const(d), tok, const(tm), tok, const(1)],
        out_specs=[row, row, const(8)],
        out_shape=[jax.ShapeDtypeStruct((dm.T, d), BF16), jax.ShapeDtypeStruct((dm.T, d), F32),
                   jax.ShapeDtypeStruct((8, d), F32)],
        compiler_params=_cp(1),
    )(merged, w_out, x, metapad, target, g_post)


def _merge_bwd(d_out, proj_c, p_conv, p_gla, wt_out, wt_oc, wt_og, dm):
    d = dm.D
    tm = _pick(dm.T, 512, 16)

    def body(do_ref, c_ref, pc_ref, pg_ref, wo_ref, woc_ref, wog_ref, dpc_ref, dpg_ref, dc_ref, dyc_ref, dyg_ref):
        dmg = _dot(do_ref[...], wo_ref[...])
        sa = _sigmoid(c_ref[:, :d].astype(F32))
        sb = _sigmoid(c_ref[:, d:].astype(F32))
        dpc = (dmg * sa).astype(BF16)
        dpg = (dmg * sb).astype(BF16)
        dpc_ref[...] = dpc
        dpg_ref[...] = dpg
        dc_ref[:, :d] = (dmg * pc_ref[...].astype(F32) * sa * (1.0 - sa)).astype(BF16)
        dc_ref[:, d:] = (dmg * pg_ref[...].astype(F32) * sb * (1.0 - sb)).astype(BF16)
        dyc_ref[...] = _dot(dpc, woc_ref[...]).astype(BF16)
        dyg_ref[...] = _dot(dpg, wog_ref[...]).astype(BF16)

    row = pl.BlockSpec((tm, d), lambda i: (i, 0))
    row2 = pl.BlockSpec((tm, 2 * d), lambda i: (i, 0))
    full = pl.BlockSpec((d, d), lambda i: (0, 0))
    act = jax.ShapeDtypeStruct((dm.T, d), BF16)
    return pl.pallas_call(
        body, name="merge_bwd", grid=(dm.T // tm,),
        in_specs=[row, row2, row, row, full, full, full],
        out_specs=[row, row, row2, row, row],
        out_shape=[act, act, jax.ShapeDtypeStruct((dm.T, 2 * d), BF16), act, act],
        compiler_params=_cp(1),
    )(d_out, proj_c, p_conv, p_gla, wt_out, wt_oc, wt_og)


def _prenorm_bwd(du, dy, x, metapad, g_pre, dm):
    tm, tps, d = dm.TM, dm.TPS, dm.D

    def body(du_ref, dy_ref, x_ref, mp_ref, g_ref, gx_ref, dmeta_ref, gg_ref):
        i = pl.program_id(0)
        j = i % tps
        h = jnp.where(j == 0, mp_ref[...], x_ref[0])
        rstd = lax.rsqrt(jnp.mean(h * h, axis=-1, keepdims=True) + EPS)
        hhat = h * rstd
        dug = du_ref[...] * g_ref[...]
        dh = dy_ref[...] + rstd * (dug - hhat * jnp.mean(dug * hhat, axis=-1, keepdims=True))

        @pl.when(j == 0)
        def _():
            dmeta_ref[0] = dh

        @pl.when(j > 0)
        def _():
            gx_ref[0] = dh

        @pl.when(i == 0)
        def _():
            gg_ref[...] = jnp.zeros_like(gg_ref)

        gg_ref[0:1, :] += jnp.sum(du_ref[...] * hhat, axis=0, keepdims=True)

    row = pl.BlockSpec((tm, d), lambda i: (i, 0))
    tok = pl.BlockSpec((1, tm, d), lambda i: (i // tps, jnp.maximum(i % tps - 1, 0), 0))
    const = lambda r: pl.BlockSpec((r, d), lambda i: (0, 0))
    return pl.pallas_call(
        body, name="prenorm_bwd", grid=(dm.Bl * tps,),
        in_specs=[row, row, tok, const(tm), const(1)],
        out_specs=[tok, pl.BlockSpec((1, tm, d), lambda i: (i // tps, 0, 0)), const(8)],
        out_shape=[jax.ShapeDtypeStruct((dm.Bl, dm.S, d), F32), jax.ShapeDtypeStruct((dm.Bl, tm, d), F32),
                   jax.ShapeDtypeStruct((8, d), F32)],
        compiler_params=_cp(1),
    )(du, dy, x, metapad, g_pre)


def _sum_partials(partials, name):
    p, r, c = partials.shape
    tc = _pick(c, 256, 128)

    def body(p_ref, o_ref):
        g = p_ref[0].astype(F32)
        for j in range(1, p):
            g = g + p_ref[j].astype(F32)
        o_ref[...] = g

    return pl.pallas_call(
        body, name=name, grid=(c // tc,), in_specs=[pl.BlockSpec((p, r, tc), lambda i: (0, 0, i))],
        out_specs=pl.BlockSpec((r, tc), lambda i: (0, i)), out_shape=jax.ShapeDtypeStruct((r, c), F32),
        compiler_params=_cp(1),
    )(partials)


def _adamw(partials, w, m, v, name):
    r, c = w.shape
    n_parts = partials.shape[0]
    tr = _pick(r, 256, 16)

    def body(p_ref, w_ref, m_ref, v_ref, g_ref, d_ref, nm_ref, nv_ref):
        g = p_ref[0].astype(F32)
        for j in range(1, n_parts):
            g = g + p_ref[j].astype(F32)
        m2 = ADAM_B1 * m_ref[...] + (1.0 - ADAM_B1) * g
        v2 = ADAM_B2 * v_ref[...] + (1.0 - ADAM_B2) * (g * g)
        m_hat = m2 / (1.0 - ADAM_B1 ** ADAM_STEP)
        v_hat = v2 / (1.0 - ADAM_B2 ** ADAM_STEP)
        g_ref[...] = g
        d_ref[...] = -ADAM_LR * (m_hat / (jnp.sqrt(v_hat) + ADAM_EPS) + ADAM_WD * w_ref[...])
        nm_ref[...] = m2
        nv_ref[...] = v2

    row = pl.BlockSpec((tr, c), lambda i: (i, 0))
    out = jax.ShapeDtypeStruct((r, c), F32)
    return pl.pallas_call(
        body, name=name, grid=(r // tr,),
        in_specs=[pl.BlockSpec((n_parts, tr, c), lambda i: (0, i, 0)), row, row, row],
        out_specs=[row, row, row, row], out_shape=[out, out, out, out], compiler_params=_cp(1),
    )(partials, w, m, v)


def _pack_rows(wt, dm):
    d, dk, hk, hv, cw, nj = dm.D, dm.DK, dm.HK, dm.HV, dm.CW, dm.NJ
    a = wt[:4 * d].reshape(4, nj, cw, d).transpose(1, 0, 2, 3).reshape(4 * d, d)
    q, k, v, r = wt[4 * d:4 * d + dk], wt[4 * d + dk:5 * d], wt[5 * d:6 * d], wt[6 * d:7 * d]
    b = jnp.concatenate([t for h in range(HEADS) for t in (q[h * hk:(h + 1) * hk], k[h * hk:(h + 1) * hk],
                                                           v[h * hv:(h + 1) * hv], r[h * hv:(h + 1) * hv])], axis=0)
    c = wt[7 * d + 2 * RANK:]
    lr = jnp.pad(wt[7 * d:7 * d + 2 * RANK], ((0, LR_LANES - 2 * RANK), (0, 0)))
    return a, b, c, lr


def _unpack_rows(a, b, c, lr, dm):
    d, hk, hv, cw, nj, hw = dm.D, dm.HK, dm.HV, dm.CW, dm.NJ, dm.HW
    conv = a.reshape(nj, 4, cw, d).transpose(1, 0, 2, 3).reshape(4 * d, d)
    heads = [b[h * hw:(h + 1) * hw] for h in range(HEADS)]
    q = jnp.concatenate([t[:hk] for t in heads], axis=0)
    k = jnp.concatenate([t[hk:2 * hk] for t in heads], axis=0)
    v = jnp.concatenate([t[2 * hk:2 * hk + hv] for t in heads], axis=0)
    r = jnp.concatenate([t[2 * hk + hv:] for t in heads], axis=0)
    return jnp.concatenate([conv, q, k, v, r, lr[:2 * RANK], c], axis=0)


def _to_blob(pieces, dtype, row_mult):
    lead = pieces[0].shape[0]
    flat = jnp.concatenate([p.reshape(lead, -1).astype(dtype) for p in pieces], axis=1)
    unit = row_mult * BLOB_LANES
    padded = -(-flat.shape[1] // unit) * unit
    flat = jnp.pad(flat, ((0, 0), (0, padded - flat.shape[1])))
    return flat.reshape(lead, padded // BLOB_LANES, BLOB_LANES)


def _from_blob(blob, shapes):
    lead = blob.shape[0]
    flat = blob.reshape(lead, -1)
    out, off = [], 0
    for shp in shapes:
        size = int(np.prod(shp))
        out.append(flat[:, off:off + size].reshape((lead,) + tuple(shp)))
        off += size
    return out


def _local_step(x, target, meta, g_pre, wt_in, conv_w, wg_f, bg_f, wg_b, bg_b, gla_g, w_oc, w_og, w_out, g_post):
    bl, s, d = x.shape
    dm = _Dims(bl, s, d)
    metapad = jnp.concatenate([jnp.zeros((dm.TM - N_META, d), F32), meta], axis=0)
    wta, wtb, wtc, wtlr = _pack_rows(wt_in, dm)
    wa, wb, wc, wlr = wta.T, wtb.T, wtc.T, wtlr.T
    wgp_f = jnp.pad(wg_f, ((0, LR_LANES - RANK), (0, 0))).astype(BF16)
    wgp_b = jnp.pad(wg_b, ((RANK, LR_LANES - 2 * RANK), (0, 0))).astype(BF16)

    u = _prenorm(x, metapad, g_pre, dm)
    proj_a = _matmul(u, wa, BF16, "inproj_conv")
    proj_b = _matmul(u, wb, BF16, "inproj_gla")
    proj_c = _matmul(u, wc, BF16, "inproj_merge")
    lr = _matmul(u, wlr, BF16, "inproj_gate")
    y_conv = _conv_fwd(proj_a, conv_w, dm)
    o_all, y_gla = _gla_fwd(proj_b, lr, wgp_f, bg_f, wgp_b, bg_b, gla_g, dm)
    p_conv, p_gla, merged = _out_merge(y_conv, y_gla, proj_c, w_oc, w_og, dm)
    d_out, dy, stats = _final_fwd(merged, w_out, x, metapad, target, g_post, dm)
    loss = 0.5 / d * jnp.sum(stats[1])

    d_pc, d_pg, d_c, dy_conv, dy_gla = _merge_bwd(d_out, proj_c, p_conv, p_gla, w_out.T, w_oc.T, w_og.T, dm)
    g_out = _matmul_tn(merged, d_out, "grad_w_out")
    g_oc = _matmul_tn(y_conv, d_pc, "grad_w_out_conv")
    g_og = _matmul_tn(y_gla, d_pg, "grad_w_out_gla")
    d_a, g_conv = _conv_bwd(proj_a, dy_conv, conv_w, dm)
    d_b, d_lr, gwp_f, gbp_f, gwp_b, gbp_b, g_gla = _gla_bwd(proj_b, lr, o_all, dy_gla, wgp_f, bg_f, wgp_b, bg_b, gla_g, dm)
    du = _matmul_parts([(d_a, wta), (d_b, wtb), (d_c, wtc), (d_lr, wtlr)], "grad_u")
    g_in = _unpack_rows(_matmul_tn(d_a, u, "grad_w_in_conv"), _matmul_tn(d_b, u, "grad_w_in_gla"),
                        _matmul_tn(d_c, u, "grad_w_in_merge"), _matmul_tn(d_lr, u, "grad_w_in_gate"), dm)
    grad_x, d_meta, g_pre_rows = _prenorm_bwd(du, dy, x, metapad, g_pre, dm)

    grads = dict(
        meta_tokens=jnp.sum(d_meta[:, dm.TM - N_META:, :], axis=0), norm_pre=g_pre_rows[0:1], w_in=g_in,
        conv_w=g_conv[0:3], w_gate_fwd=jnp.sum(gwp_f, axis=0)[:RANK], b_gate_fwd=jnp.sum(gbp_f, axis=0)[0:1],
        w_gate_bwd=jnp.sum(gwp_b, axis=0)[RANK:2 * RANK], b_gate_bwd=jnp.sum(gbp_b, axis=0)[0:1],
        gla_norm=g_gla[0:1], w_out_conv=g_oc, w_out_gla=g_og, w_merge_out=g_out, norm_post=stats[0:1])
    return loss, grad_x, grads


MATRICES = ("w_out_conv", "w_out_gla", "w_merge_out")
SMALL_SHARDED = ("meta_tokens", "conv_w", "w_gate_fwd", "w_gate_bwd")
REPLICATED = ("norm_pre", "b_gate_fwd", "b_gate_bwd", "gla_norm", "norm_post")
NAMES = ("meta_tokens", "norm_pre", "w_in", "conv_w", "w_gate_fwd", "b_gate_fwd", "w_gate_bwd", "b_gate_bwd", "gla_norm",
         "w_out_conv", "w_out_gla", "w_merge_out", "norm_post")
DEPTH_AXIS = ("w_in", "conv_w", "w_gate_fwd", "w_gate_bwd") + MATRICES


def _cols_to_devices(g):
    r, c = g.shape
    return g.reshape(r, N_DEV, c // N_DEV).transpose(1, 0, 2)


def _cols_from_devices(parts):
    n, r, c = parts.shape
    return parts.transpose(1, 0, 2).reshape(r, n * c)


def kernel(x, meta_tokens, norm_pre, w_in, conv_w, w_gate_fwd, b_gate_fwd, w_gate_bwd, b_gate_bwd, gla_norm, w_out_conv, w_out_gla, w_merge_out, norm_post, loss_target, m_meta_tokens, m_norm_pre, m_w_in, m_conv_w, m_w_gate_fwd, m_b_gate_fwd, m_w_gate_bwd, m_b_gate_bwd, m_gla_norm, m_w_out_conv, m_w_out_gla, m_w_merge_out, m_norm_post, v_meta_tokens, v_norm_pre, v_w_in, v_conv_w, v_w_gate_fwd, v_b_gate_fwd, v_w_gate_bwd, v_b_gate_bwd, v_gla_norm, v_w_out_conv, v_w_out_gla, v_w_merge_out, v_norm_post):
    w = dict(meta_tokens=meta_tokens, norm_pre=norm_pre, w_in=w_in[0], conv_w=conv_w[0], w_gate_fwd=w_gate_fwd[0],
             b_gate_fwd=b_gate_fwd, w_gate_bwd=w_gate_bwd[0], b_gate_bwd=b_gate_bwd, gla_norm=gla_norm,
             w_out_conv=w_out_conv[0], w_out_gla=w_out_gla[0], w_merge_out=w_merge_out[0], norm_post=norm_post)
    m = dict(meta_tokens=m_meta_tokens, norm_pre=m_norm_pre, w_in=m_w_in[0], conv_w=m_conv_w[0], w_gate_fwd=m_w_gate_fwd[0],
             b_gate_fwd=m_b_gate_fwd, w_gate_bwd=m_w_gate_bwd[0], b_gate_bwd=m_b_gate_bwd, gla_norm=m_gla_norm,
             w_out_conv=m_w_out_conv[0], w_out_gla=m_w_out_gla[0], w_merge_out=m_w_merge_out[0], norm_post=m_norm_post)
    v = dict(meta_tokens=v_meta_tokens, norm_pre=v_norm_pre, w_in=v_w_in[0], conv_w=v_conv_w[0], w_gate_fwd=v_w_gate_fwd[0],
             b_gate_fwd=v_b_gate_fwd, w_gate_bwd=v_w_gate_bwd[0], b_gate_bwd=v_b_gate_bwd, gla_norm=v_gla_norm,
             w_out_conv=v_w_out_conv[0], w_out_gla=v_w_out_gla[0], w_merge_out=v_w_merge_out[0], norm_post=v_norm_post)
    d = x.shape[-1]

    small_blob = _to_blob([w[n][None] for n in SMALL_SHARDED], F32, SMALL_ROWS)[0]
    gathered = _exchange([w["w_in"].T.astype(BF16)] + [w[n].astype(BF16) for n in MATRICES] + [small_blob], [],
                         "gather_weights")
    wt_in, w_oc, w_og, w_out = (a.reshape(-1, d) for a in gathered[:4])
    small = {n: _cols_from_devices(p) for n, p in zip(SMALL_SHARDED, _from_blob(gathered[4], [w[n].shape for n in SMALL_SHARDED]))}

    loss, grad_x, grads = _local_step(
        x, loss_target, small["meta_tokens"], norm_pre, wt_in, small["conv_w"], small["w_gate_fwd"], b_gate_fwd,
        small["w_gate_bwd"], b_gate_bwd, gla_norm, w_oc, w_og, w_out, norm_post)
    loss = lax.psum(loss, ("x", "y", "c"))

    to_send = [grads[n].astype(BF16).reshape(N_DEV, -1, d) for n in ("w_in",) + MATRICES]
    to_send.append(_to_blob([_cols_to_devices(grads[n]) for n in SMALL_SHARDED], F32, SMALL_ROWS))
    repl_blob = _to_blob([grads[n][None] for n in REPLICATED], F32, SMALL_ROWS)[0]
    received = _exchange([repl_blob], to_send, "exchange_grads")

    results = {"w_in": _adamw(_sum_partials(received[1], "sum_grad_w_in").T[None], w["w_in"], m["w_in"], v["w_in"], "adamw_w_in")}
    for n, partials in zip(MATRICES, received[2:5]):
        results[n] = _adamw(partials, w[n], m[n], v[n], "adamw_" + n)
    for names, partials, tag in ((SMALL_SHARDED, received[5], "small"), (REPLICATED, received[0], "replicated")):
        blobs = [_to_blob([t[n][None] for n in names], F32, SMALL_ROWS)[0] for t in (w, m, v)]
        res = [_from_blob(r[None], [w[n].shape for n in names]) for r in _adamw(partials, *blobs, name="adamw_" + tag)]
        for i, n in enumerate(names):
            results[n] = [r[i][0] for r in res]
    lead = lambda n, t: t[None] if n in DEPTH_AXIS else t
    return (loss, grad_x, *[lead(n, results[n][i]) for i in range(4) for n in NAMES])
```

```python
import functools

import jax
import jax.numpy as jnp
import numpy as np
from jax import lax
from jax.experimental import pallas as pl
from jax.experimental.pallas import tpu as pltpu

F32 = jnp.float32
BF16 = jnp.bfloat16
MESH = pl.DeviceIdType.MESH

N_META = 16
CHUNK = 64
CHUNK_SHIFT = 6
HEADS = 4
RANK = 16
LR_LANES = 128
PAD_ROWS = CHUNK - N_META
EPS = 1e-6
GATE_NORMALIZER = 16.0
N_DEV = 8
ADAM_LR, ADAM_B1, ADAM_B2, ADAM_EPS, ADAM_WD, ADAM_STEP = 0.001, 0.9, 0.999, 1e-08, 0.01, 10
VMEM_LIMIT_BYTES = 56 * 1024 * 1024
BLOB_LANES = 512
SMALL_ROWS = 16


class _Dims:
    def __init__(self, bl, s, d):
        self.Bl, self.S, self.D = bl, s, d
        self.TM = 256 if s % 256 == 0 else CHUNK
        self.LP = self.TM + s
        self.T = bl * self.LP
        self.TPS = self.LP // self.TM
        self.NC = self.LP // CHUNK
        self.C0 = (self.TM - CHUNK) // CHUNK
        self.DK, self.DV = d // 2, d
        self.HK, self.HV = self.DK // HEADS, self.DV // HEADS
        self.HW = 2 * self.HK + 2 * self.HV
        self.CW = 256 if d % 256 == 0 and d > 256 else d // 4
        self.NJ = d // self.CW


def _pick(n, target, mult):
    t = min(n, target)
    while t >= mult:
        if n % t == 0 and t % mult == 0:
            return t
        t -= mult
    return n


def _cp(n_axes):
    return pltpu.CompilerParams(dimension_semantics=("arbitrary",) * n_axes, vmem_limit_bytes=VMEM_LIMIT_BYTES)


def _sigmoid(x):
    return 1.0 / (1.0 + jnp.exp(-x))


def _dot(a, b):
    return jnp.dot(a, b, preferred_element_type=F32)


def _dot_nt(a, b):
    return lax.dot_general(a, b, (((1,), (1,)), ((), ())), preferred_element_type=F32)


def _dot_tn(a, b):
    return lax.dot_general(a, b, (((0,), (0,)), ((), ())), preferred_element_type=F32)


def _dot_exact01(m01, x):
    hi = x.astype(BF16)
    lo = (x - hi.astype(F32)).astype(BF16)
    return _dot(m01, hi) + _dot(m01, lo)


def _exchange(gathers, scatters, name):
    arrays = list(gathers) + list(scatters)
    n, ng = len(arrays), len(gathers)

    def body(*refs):
        ins, outs = refs[:n], refs[n:2 * n]
        send_sems, recv_sems, local_sems = refs[2 * n:]
        x, y, c = lax.axis_index("x"), lax.axis_index("y"), lax.axis_index("c")
        me = 4 * x + 2 * y + c
        started = []
        for t in range(n):
            src, dst = ins[t], outs[t]
            own = pltpu.make_async_copy(src if t < ng else src.at[me], dst.at[me], local_sems.at[t])
            own.start()
            started.append(own)
            for k, pos, peer in _peers(x, y, c):
                cp = pltpu.make_async_remote_copy(
                    src_ref=src if t < ng else src.at[peer], dst_ref=dst.at[me],
                    send_sem=send_sems.at[t * (N_DEV - 1) + k - 1], recv_sem=recv_sems.at[t * (N_DEV - 1) + k - 1],
                    device_id=pos, device_id_type=MESH)
                cp.start()
                started.append(cp)
        for cp in started:
            cp.wait()

    out_shape = [jax.ShapeDtypeStruct((N_DEV,) + a.shape[-2:], a.dtype) for a in arrays]
    any_spec = pl.BlockSpec(memory_space=pl.ANY)
    return pl.pallas_call(
        body, name=name, out_shape=out_shape, in_specs=[any_spec] * n, out_specs=[any_spec] * n,
        scratch_shapes=[pltpu.SemaphoreType.DMA((n * (N_DEV - 1),)), pltpu.SemaphoreType.DMA((n * (N_DEV - 1),)),
                        pltpu.SemaphoreType.DMA((n,))],
        compiler_params=pltpu.CompilerParams(has_side_effects=True),
    )(*arrays)


def _peers(x, y, c):
    out = []
    for k in range(1, N_DEV):
        px = 1 - x if (k >> 2) & 1 else x
        py = 1 - y if (k >> 1) & 1 else y
        pc = 1 - c if k & 1 else c
        out.append((k, (px, py, pc), 4 * px + 2 * py + pc))
    return out


def _exchange_start(gathers, scatters, after, name):
    arrays = list(gathers) + list(scatters)
    n, ng = len(arrays), len(gathers)
    hbm = pl.BlockSpec(memory_space=pltpu.HBM)
    sem = pl.BlockSpec(memory_space=pltpu.SEMAPHORE)

    def body(*refs):
        ins, lands = refs[:n], refs[n:2 * n]
        send_sems, recv_sems = refs[2 * n + 1], refs[2 * n + 2]
        token = refs[4 * n + 3]
        x, y, c = lax.axis_index("x"), lax.axis_index("y"), lax.axis_index("c")
        me = 4 * x + 2 * y + c
        for t in range(n):
            for k, pos, peer in _peers(x, y, c):
                pltpu.make_async_remote_copy(
                    src_ref=ins[t] if t < ng else ins[t].at[peer], dst_ref=lands[t].at[me],
                    send_sem=send_sems.at[t * (N_DEV - 1) + k - 1], recv_sem=recv_sems.at[t * (N_DEV - 1) + k - 1],
                    device_id=pos, device_id_type=MESH).start()
        token[...] = jnp.zeros_like(token)

    me = 4 * lax.axis_index("x") + 2 * lax.axis_index("y") + lax.axis_index("c")
    lands = [lax.dynamic_update_index_in_dim(lax.empty((N_DEV,) + a.shape[-2:], a.dtype),
                                             a if t < ng else lax.dynamic_index_in_dim(a, me, 0, keepdims=False), me, 0)
             for t, a in enumerate(arrays)]
    operands = [pltpu.with_memory_space_constraint(a, pltpu.HBM) for a in arrays + lands]
    sems = pltpu.SemaphoreType.DMA((n * (N_DEV - 1),))
    res = pl.pallas_call(
        body, name=name,
        out_shape=(sems, sems, *[pltpu.HBM(a.shape, a.dtype) for a in arrays + lands], jax.ShapeDtypeStruct((8, 128), F32)),
        in_specs=[hbm] * (2 * n) + [pl.BlockSpec(memory_space=pl.ANY)],
        out_specs=(sem, sem, *[hbm] * (2 * n), pl.BlockSpec(memory_space=pltpu.VMEM)),
        input_output_aliases={i: 2 + i for i in range(2 * n)},
        compiler_params=pltpu.CompilerParams(has_side_effects=pltpu.SideEffectType.DATAFLOW_SIDE_EFFECTING),
    )(*operands, after)
    return res[-1][0, 0], (ng, res[0], res[1], list(res[2:2 + n]), list(res[2 + n:2 + 2 * n]))


def _exchange_wait(state, after, name):
    ng, send_sems, recv_sems, sent, lands = state
    n = len(sent)
    hbm = pl.BlockSpec(memory_space=pltpu.HBM)
    sem = pl.BlockSpec(memory_space=pltpu.SEMAPHORE)

    def body(*refs):
        ins, land_refs = refs[:n], refs[n:2 * n]
        send_ref, recv_ref = refs[2 * n], refs[2 * n + 1]
        x, y, c = lax.axis_index("x"), lax.axis_index("y"), lax.axis_index("c")
        me = 4 * x + 2 * y + c
        for t in range(n):
            for k, pos, peer in _peers(x, y, c):
                cp = pltpu.make_async_remote_copy(
                    src_ref=ins[t] if t < ng else ins[t].at[peer], dst_ref=land_refs[t].at[me],
                    send_sem=send_ref.at[t * (N_DEV - 1) + k - 1], recv_sem=recv_ref.at[t * (N_DEV - 1) + k - 1],
                    device_id=pos, device_id_type=MESH)
                cp.wait_send()
                cp.wait_recv()

    res = pl.pallas_call(
        body, name=name, out_shape=tuple(pltpu.HBM(a.shape, a.dtype) for a in sent + lands),
        in_specs=[hbm] * (2 * n) + [sem, sem, pl.BlockSpec(memory_space=pl.ANY)], out_specs=tuple([hbm] * (2 * n)),
        input_output_aliases={i: i for i in range(2 * n)},
        compiler_params=pltpu.CompilerParams(has_side_effects=pltpu.SideEffectType.DATAFLOW_SIDE_EFFECTING),
    )(*sent, *lands, send_sems, recv_sems, after)
    return list(res[n:])


def _prenorm(x, metapad, g_pre, dm):
    tm, tps, d = dm.TM, dm.TPS, dm.D

    def body(x_ref, mp_ref, g_ref, u_ref):
        j = pl.program_id(0) % tps
        h = jnp.where(j == 0, mp_ref[...], x_ref[0])
        r = lax.rsqrt(jnp.mean(h * h, axis=-1, keepdims=True) + EPS)
        u_ref[...] = (h * r * g_ref[...]).astype(BF16)

    return pl.pallas_call(
        body, name="prenorm", grid=(dm.Bl * tps,),
        in_specs=[pl.BlockSpec((1, tm, d), lambda i: (i // tps, jnp.maximum(i % tps - 1, 0), 0)),
                  pl.BlockSpec((tm, d), lambda i: (0, 0)),
                  pl.BlockSpec((1, d), lambda i: (0, 0))],
        out_specs=pl.BlockSpec((tm, d), lambda i: (i, 0)),
        out_shape=jax.ShapeDtypeStruct((dm.T, d), BF16), compiler_params=_cp(1),
    )(x, metapad, g_pre)


def _matmul_nt(a, bt, out_dtype, name, tm=1024, tn=1024):
    m, k = a.shape
    n = bt.shape[0]
    tm, tn = _pick(m, tm, 16), _pick(n, tn, 128)

    def body(a_ref, b_ref, o_ref):
        o_ref[...] = _dot_nt(a_ref[...].astype(BF16), b_ref[...]).astype(out_dtype)

    return pl.pallas_call(
        body, name=name, grid=(n // tn, m // tm),
        in_specs=[pl.BlockSpec((tm, k), lambda j, i: (i, 0)), pl.BlockSpec((tn, k), lambda j, i: (j, 0))],
        out_specs=pl.BlockSpec((tm, tn), lambda j, i: (i, j)),
        out_shape=jax.ShapeDtypeStruct((m, n), out_dtype), compiler_params=_cp(2),
    )(a, bt)


def _matmul_parts(parts, name, tm=1024, tk=1024):
    m = parts[0][0].shape[0]
    n = parts[0][1].shape[1]
    tm = _pick(m, tm, 16)
    tks = [_pick(a.shape[1], tk, 128) for a, _ in parts]
    counts = [a.shape[1] // t for (a, _), t in zip(parts, tks)]
    starts = [int(s) for s in np.cumsum([0] + counts[:-1])]
    total = sum(counts)

    def body(*refs):
        acc = refs[-1]
        o_ref = refs[-2]
        k = pl.program_id(1)

        @pl.when(k == 0)
        def _():
            acc[...] = jnp.zeros_like(acc)

        for p in range(len(parts)):
            @pl.when((k >= starts[p]) & (k < starts[p] + counts[p]))
            def _(p=p):
                acc[...] += _dot(refs[2 * p][...].astype(BF16), refs[2 * p + 1][...])

        @pl.when(k == total - 1)
        def _():
            o_ref[...] = acc[...]

    in_specs, operands = [], []
    for (a, b), t, s, cnt in zip(parts, tks, starts, counts):
        in_specs.append(pl.BlockSpec((tm, t), lambda i, k, s=s, cnt=cnt: (i, jnp.clip(k - s, 0, cnt - 1))))
        in_specs.append(pl.BlockSpec((t, n), lambda i, k, s=s, cnt=cnt: (jnp.clip(k - s, 0, cnt - 1), 0)))
        operands += [a, b]
    return pl.pallas_call(
        body, name=name, grid=(m // tm, total), in_specs=in_specs,
        out_specs=pl.BlockSpec((tm, n), lambda i, k: (i, 0)),
        out_shape=jax.ShapeDtypeStruct((m, n), F32), scratch_shapes=[pltpu.VMEM((tm, n), F32)],
        compiler_params=_cp(2),
    )(*operands)


def _matmul_tn(a, b, name, tt=768, tn=1024):
    t, k = a.shape
    n = b.shape[1]
    tt, tn, tk = _pick(t, tt, 16), _pick(n, tn, 128), _pick(k, 1024, 128)

    def body(a_ref, b_ref, o_ref):
        p = _dot_tn(a_ref[...].astype(BF16), b_ref[...].astype(BF16))

        @pl.when(pl.program_id(2) == 0)
        def _():
            o_ref[...] = p

        @pl.when(pl.program_id(2) > 0)
        def _():
            o_ref[...] += p

    return pl.pallas_call(
        body, name=name, grid=(k // tk, n // tn, t // tt),
        in_specs=[pl.BlockSpec((tt, tk), lambda kk, j, i: (i, kk)), pl.BlockSpec((tt, tn), lambda kk, j, i: (i, j))],
        out_specs=pl.BlockSpec((tk, tn), lambda kk, j, i: (kk, j)),
        out_shape=jax.ShapeDtypeStruct((k, n), F32), compiler_params=_cp(3),
    )(a, b)


def _conv_rows(dm):
    return _pick(dm.LP, 256, 16)


def _shifted(m, prev_row, next_row, rows):
    row = lax.broadcasted_iota(jnp.int32, m.shape, 0)
    m_prev = jnp.where(row == 0, prev_row, pltpu.roll(m, 1, 0))
    m_next = jnp.where(row == rows - 1, next_row, pltpu.roll(m, rows - 1, 0))
    return m_prev, m_next


def _conv_fwd(proj_a, conv_w, dm):
    lp, cw, rc = dm.LP, dm.CW, _conv_rows(dm)
    nchunk = lp // rc

    def body(p_ref, w_ref, y_ref):
        w0, w1, w2 = w_ref[0:1, :], w_ref[1:2, :], w_ref[2:3, :]

        def chunk(ci, carry):
            r0 = pl.multiple_of(ci * rc, rc)
            blk = p_ref[pl.ds(r0, rc), :].astype(F32)
            cb, cc, cx, cz = (blk[:, i * cw:(i + 1) * cw] for i in range(4))
            m = cc * cx
            rp = pl.multiple_of(jnp.maximum(r0 - 16, 0), 16)
            rn = pl.multiple_of(jnp.minimum(r0 + rc, lp - 16), 16)
            pv = p_ref[pl.ds(rp, 16), cw:3 * cw].astype(F32)
            nx = p_ref[pl.ds(rn, 16), cw:3 * cw].astype(F32)
            prev_row = jnp.where(ci > 0, pv[15:16, :cw] * pv[15:16, cw:], 0.0)
            next_row = jnp.where(ci < nchunk - 1, nx[0:1, :cw] * nx[0:1, cw:], 0.0)
            m_prev, m_next = _shifted(m, prev_row, next_row, rc)
            s = w0 * m_prev + w1 * m + w2 * m_next
            y_ref[pl.ds(r0, rc), :] = (cb * s * (cz * _sigmoid(cz))).astype(BF16)
            return carry

        lax.fori_loop(0, nchunk, chunk, 0)

    return pl.pallas_call(
        body, name="conv_fwd", grid=(dm.Bl, dm.NJ),
        in_specs=[pl.BlockSpec((lp, 4 * cw), lambda s, j: (s, j)), pl.BlockSpec((3, cw), lambda s, j: (0, j))],
        out_specs=pl.BlockSpec((lp, cw), lambda s, j: (s, j)),
        out_shape=jax.ShapeDtypeStruct((dm.T, dm.D), BF16), compiler_params=_cp(2),
    )(proj_a, conv_w)


def _conv_bwd(proj_a, dy_conv, conv_w, dm):
    lp, cw, rc = dm.LP, dm.CW, _conv_rows(dm)
    nchunk = lp // rc

    def body(p_ref, dy_ref, w_ref, d_ref, gw_ref):
        w0, w1, w2 = w_ref[0:1, :], w_ref[1:2, :], w_ref[2:3, :]

        def ds_of(p4, dy):
            cb, cz = p4[:, :cw], p4[:, 3 * cw:]
            return dy * cb * (cz * _sigmoid(cz))

        def chunk(ci, carry):
            g0, g1, g2 = carry
            r0 = pl.multiple_of(ci * rc, rc)
            blk = p_ref[pl.ds(r0, rc), :].astype(F32)
            dy = dy_ref[pl.ds(r0, rc), :].astype(F32)
            cb, cc, cx, cz = (blk[:, i * cw:(i + 1) * cw] for i in range(4))
            rp = pl.multiple_of(jnp.maximum(r0 - 16, 0), 16)
            rn = pl.multiple_of(jnp.minimum(r0 + rc, lp - 16), 16)
            pv = p_ref[pl.ds(rp, 16), :].astype(F32)[15:16]
            nx = p_ref[pl.ds(rn, 16), :].astype(F32)[0:1]
            dpv = dy_ref[pl.ds(rp, 16), :].astype(F32)[15:16]
            dnx = dy_ref[pl.ds(rn, 16), :].astype(F32)[0:1]
            has_prev, has_next = ci > 0, ci < nchunk - 1
            m = cc * cx
            m_prev, m_next = _shifted(m, jnp.where(has_prev, pv[:, cw:2 * cw] * pv[:, 2 * cw:3 * cw], 0.0),
                                      jnp.where(has_next, nx[:, cw:2 * cw] * nx[:, 2 * cw:3 * cw], 0.0), rc)
            s = w0 * m_prev + w1 * m + w2 * m_next
            sg = _sigmoid(cz)
            silu = cz * sg
            ds = dy * cb * silu
            ds_prev, ds_next = _shifted(ds, jnp.where(has_prev, ds_of(pv, dpv), 0.0),
                                        jnp.where(has_next, ds_of(nx, dnx), 0.0), rc)
            dm_ = w0 * ds_next + w1 * ds + w2 * ds_prev
            d_ref[pl.ds(r0, rc), 0:cw] = (dy * s * silu).astype(BF16)
            d_ref[pl.ds(r0, rc), cw:2 * cw] = (dm_ * cx).astype(BF16)
            d_ref[pl.ds(r0, rc), 2 * cw:3 * cw] = (dm_ * cc).astype(BF16)
            d_ref[pl.ds(r0, rc), 3 * cw:4 * cw] = (dy * cb * s * (sg * (1.0 + cz * (1.0 - sg)))).astype(BF16)
            return (g0 + jnp.sum(ds * m_prev, axis=0, keepdims=True), g1 + jnp.sum(ds * m, axis=0, keepdims=True),
                    g2 + jnp.sum(ds * m_next, axis=0, keepdims=True))

        z = jnp.zeros((1, cw), F32)
        g0, g1, g2 = lax.fori_loop(0, nchunk, chunk, (z, z, z))

        @pl.when(pl.program_id(1) == 0)
        def _():
            gw_ref[...] = jnp.zeros_like(gw_ref)

        gw_ref[0:1, :] += g0
        gw_ref[1:2, :] += g1
        gw_ref[2:3, :] += g2

    return pl.pallas_call(
        body, name="conv_bwd", grid=(dm.NJ, dm.Bl),
        in_specs=[pl.BlockSpec((lp, 4 * cw), lambda j, s: (s, j)), pl.BlockSpec((lp, cw), lambda j, s: (s, j)),
                  pl.BlockSpec((3, cw), lambda j, s: (0, j))],
        out_specs=[pl.BlockSpec((lp, 4 * cw), lambda j, s: (s, j)), pl.BlockSpec((8, cw), lambda j, s: (0, j))],
        out_shape=[jax.ShapeDtypeStruct((dm.T, 4 * dm.D), BF16), jax.ShapeDtypeStruct((8, dm.D), F32)],
        compiler_params=_cp(2),
    )(proj_a, dy_conv, conv_w)


def _group_chunks(dm):
    n = dm.NC - dm.C0
    return 3 if n % 3 == 0 else 1


def _group_masks(rows):
    ii = lax.broadcasted_iota(jnp.int32, (rows, rows), 0)
    jj = lax.broadcasted_iota(jnp.int32, (rows, rows), 1)
    same = jnp.right_shift(ii, CHUNK_SHIFT) == jnp.right_shift(jj, CHUNK_SHIFT)
    low, up = same & (jj <= ii), same & (jj >= ii)
    return low, same & (jj > ii), low.astype(BF16), up.astype(BF16), same.astype(BF16)


def _log_gate(lr_rows, w_ref, b_ref, first_group, hk):
    z = _dot(lr_rows, w_ref[...]) + b_ref[...]
    e = jnp.exp(-jnp.abs(z))
    g = (jnp.minimum(z, 0.0) - jnp.log(1.0 + e)) * (1.0 / GATE_NORMALIZER)
    dg_dz = jnp.where(z >= 0.0, e, 1.0) / (1.0 + e) * (1.0 / GATE_NORMALIZER)
    row = lax.broadcasted_iota(jnp.int32, (lr_rows.shape[0], hk), 0)
    pad = first_group & (row < PAD_ROWS)
    return jnp.where(pad, 0.0, g), jnp.where(pad, 0.0, dg_dz)


def _gla_fwd(proj_b, lr, wg_f, bg_f, wg_b, bg_b, gla_g, dm):
    lp, hk, hv, nc, c0, hw = dm.LP, dm.HK, dm.HV, dm.NC, dm.C0, dm.HW
    scale = hk ** -0.5
    gc = _group_chunks(dm)
    gr, ng = gc * CHUNK, (nc - c0) // gc

    def body(p_ref, lr_ref, wf_ref, bf_ref, wb_ref, bb_ref, gg_ref, o_ref, y_ref, oacc):
        low_incl, up_strict, ones_low, ones_up, ones_same = _group_masks(gr)
        if c0 > 0:
            oacc[0:c0 * CHUNK, :] = jnp.zeros((c0 * CHUNK, hv), F32)

        def run(fwd):
            w_ref, b_ref = (wf_ref, bf_ref) if fwd else (wb_ref, bb_ref)

            def step(i, st):
                gi = i if fwd else ng - 1 - i
                r0 = pl.multiple_of((c0 + gi * gc) * CHUNK, CHUNK)
                blk = p_ref[pl.ds(r0, gr), :]
                q = blk[:, :hk].astype(F32) * scale
                k = blk[:, hk:2 * hk].astype(F32)
                v = blk[:, 2 * hk:2 * hk + hv]
                g, _ = _log_gate(lr_ref[pl.ds(r0, gr), :], w_ref, b_ref, gi == 0, hk)
                b = _dot_exact01(ones_low if fwd else ones_up, g)
                btot = _dot_exact01(ones_same, g)
                qi = (q * jnp.exp(b)).astype(BF16)
                ki = (k * jnp.exp(-b)).astype(BF16)
                kd = (k * jnp.exp(btot - b)).astype(BF16)
                dec = jnp.exp(btot)
                a = jnp.where(low_incl if fwd else up_strict, _dot_nt(qi, ki), 0.0)
                o = _dot(a.astype(BF16), v)
                for c in (range(gc) if fwd else reversed(range(gc))):
                    rows = slice(c * CHUNK, (c + 1) * CHUNK)
                    o_c = o[rows] + _dot_nt(qi[rows], st.astype(BF16))
                    st = st * dec[c * CHUNK:c * CHUNK + 1] + _dot_tn(v[rows], kd[rows])
                    if fwd:
                        oacc[pl.ds(r0 + c * CHUNK, CHUNK), :] = o_c
                    else:
                        oacc[pl.ds(r0 + c * CHUNK, CHUNK), :] += o_c
                return st

            lax.fori_loop(0, ng, step, jnp.zeros((hv, hk), F32))

        run(True)
        run(False)

        def finish(i, carry):
            r0 = pl.multiple_of(i * CHUNK, CHUNK)
            o = oacc[pl.ds(r0, CHUNK), :]
            r = p_ref[pl.ds(r0, CHUNK), 2 * hk + hv:].astype(F32)
            on = o * lax.rsqrt(jnp.mean(o * o, axis=-1, keepdims=True) + EPS) * gg_ref[...]
            o_ref[pl.ds(r0, CHUNK), :] = o.astype(BF16)
            y_ref[pl.ds(r0, CHUNK), :] = (on * r * _sigmoid(r)).astype(BF16)
            return carry

        lax.fori_loop(0, nc, finish, 0, unroll=2)

    head = lambda s, h: (s, h)
    wspec = pl.BlockSpec((LR_LANES, hk), lambda s, h: (0, h))
    bspec = pl.BlockSpec((1, hk), lambda s, h: (0, h))
    return pl.pallas_call(
        body, name="gla_fwd", grid=(dm.Bl, HEADS),
        in_specs=[pl.BlockSpec((lp, hw), head), pl.BlockSpec((lp, LR_LANES), lambda s, h: (s, 0)),
                  wspec, bspec, wspec, bspec, pl.BlockSpec((1, hv), lambda s, h: (0, 0))],
        out_specs=[pl.BlockSpec((lp, hv), head), pl.BlockSpec((lp, hv), head)],
        out_shape=[jax.ShapeDtypeStruct((dm.T, dm.DV), BF16), jax.ShapeDtypeStruct((dm.T, dm.DV), BF16)],
        scratch_shapes=[pltpu.VMEM((lp, hv), F32)],
        compiler_params=_cp(2),
    )(proj_b, lr, wg_f, bg_f, wg_b, bg_b, gla_g)


def _gla_bwd(proj_b, lr, o_all, dy_gla, wg_f, bg_f, wg_b, bg_b, gla_g, dm):
    lp, hk, hv, nc, c0, hw = dm.LP, dm.HK, dm.HV, dm.NC, dm.C0, dm.HW
    scale = hk ** -0.5
    gc = _group_chunks(dm)
    gr, ng = gc * CHUNK, (nc - c0) // gc

    def body(p_ref, lr_ref, o_ref, dy_ref, wf_ref, bf_ref, wb_ref, bb_ref, gg_ref,
             d_ref, dlr_ref, gwf_ref, gbf_ref, gwb_ref, gbb_ref, ggg_ref,
             do_s, s_all, b_s, bt_s, gs_s, dq_s, dk_s, dv_s):
        low_incl, up_strict, ones_low, ones_up, ones_same = _group_masks(gr)
        h = pl.program_id(1)

        @pl.when(h == 0)
        def _():
            dlr_ref[...] = jnp.zeros_like(dlr_ref)

        if c0 > 0:
            zr = c0 * CHUNK
            d_ref[0:zr, :] = jnp.zeros((zr, hw), BF16)

        def norm_bwd(i, ggg):
            r0 = pl.multiple_of(i * CHUNK, CHUNK)
            o = o_ref[pl.ds(r0, CHUNK), :].astype(F32)
            dy = dy_ref[pl.ds(r0, CHUNK), :].astype(F32)
            r = p_ref[pl.ds(r0, CHUNK), 2 * hk + hv:].astype(F32)
            rstd = lax.rsqrt(jnp.mean(o * o, axis=-1, keepdims=True) + EPS)
            ohat = o * rstd
            sg = _sigmoid(r)
            d_on = dy * (r * sg)
            d_ref[pl.ds(r0, CHUNK), 2 * hk + hv:] = (dy * ohat * gg_ref[...] * (sg * (1.0 + r * (1.0 - sg)))).astype(BF16)
            d_oh = d_on * gg_ref[...]
            do_s[pl.ds(r0, CHUNK), :] = (rstd * (d_oh - ohat * jnp.mean(d_oh * ohat, axis=-1, keepdims=True))).astype(BF16)
            return ggg + jnp.sum(d_on * ohat, axis=0, keepdims=True)

        ggg = lax.fori_loop(c0, nc, norm_bwd, jnp.zeros((1, hv), F32))

        @pl.when((pl.program_id(0) == 0) & (h == 0))
        def _():
            ggg_ref[...] = jnp.zeros_like(ggg_ref)

        ggg_ref[0:1, :] += ggg

        def run(fwd):
            w_ref, b_ref = (wf_ref, bf_ref) if fwd else (wb_ref, bb_ref)
            gw_ref, gb_ref = (gwf_ref, gbf_ref) if fwd else (gwb_ref, gbb_ref)
            cum, cum_t = (ones_low, ones_up) if fwd else (ones_up, ones_low)
            mask = low_incl if fwd else up_strict
            state_order = list(range(gc)) if fwd else list(reversed(range(gc)))

            def load(gi):
                r0 = pl.multiple_of((c0 + gi * gc) * CHUNK, CHUNK)
                blk = p_ref[pl.ds(r0, gr), :]
                return r0, blk[:, :hk].astype(F32) * scale, blk[:, hk:2 * hk].astype(F32), blk[:, 2 * hk:2 * hk + hv]

            def record(i, st):
                gi = i if fwd else ng - 1 - i
                r0, q, k, v = load(gi)
                g, dg_dz = _log_gate(lr_ref[pl.ds(r0, gr), :], w_ref, b_ref, gi == 0, hk)
                b = _dot_exact01(cum, g)
                btot = _dot_exact01(ones_same, g)
                b_s[pl.ds(r0, gr), :] = b
                bt_s[pl.ds(r0, gr), :] = btot
                gs_s[pl.ds(r0, gr), :] = dg_dz
                kd = (k * jnp.exp(btot - b)).astype(BF16)
                dec = jnp.exp(btot)
                for c in state_order:
                    rows = slice(c * CHUNK, (c + 1) * CHUNK)
                    s_all[c0 + gi * gc + c] = st
                    st = st * dec[c * CHUNK:c * CHUNK + 1] + _dot_tn(v[rows], kd[rows])
                return st

            lax.fori_loop(0, ng, record, jnp.zeros((hv, hk), F32))

            def grad(i, carry):
                dst, gw, gb = carry
                gi = ng - 1 - i if fwd else i
                r0, q, k, v = load(gi)
                b = b_s[pl.ds(r0, gr), :]
                btot = bt_s[pl.ds(r0, gr), :]
                eb, enb, edb, dec = jnp.exp(b), jnp.exp(-b), jnp.exp(btot - b), jnp.exp(btot)
                qi_f, ki_f, kd_f = q * eb, k * enb, k * edb
                qi, ki, kd = qi_f.astype(BF16), ki_f.astype(BF16), kd_f.astype(BF16)
                do = do_s[pl.ds(r0, gr), :]
                a = jnp.where(mask, _dot_nt(qi, ki), 0.0).astype(BF16)
                da = jnp.where(mask, _dot_nt(do, v), 0.0).astype(BF16)
                dv = _dot_tn(a, do)
                dqi = _dot(da, ki)
                dki = _dot_tn(da, qi)
                dv_c, dqi_c, dkd_c, extra_c = [None] * gc, [None] * gc, [None] * gc, [None] * gc
                for c in reversed(state_order):
                    rows = slice(c * CHUNK, (c + 1) * CHUNK)
                    st = s_all[c0 + gi * gc + c]
                    dsn_b = dst.astype(BF16)
                    dec_c = dec[c * CHUNK:c * CHUNK + 1]
                    dv_c[c] = dv[rows] + _dot_nt(kd[rows], dsn_b)
                    dqi_c[c] = dqi[rows] + _dot(do[rows], st.astype(BF16))
                    dkd_c[c] = _dot(v[rows], dsn_b)
                    ddec = jnp.sum(st * dst, axis=0, keepdims=True)
                    extra = jnp.sum(dkd_c[c] * kd_f[rows], axis=0, keepdims=True) + ddec * dec_c
                    extra_c[c] = jnp.broadcast_to(extra, (CHUNK, hk))
                    dst = dst * dec_c + _dot_tn(do[rows], qi[rows])
                dv, dqi = jnp.concatenate(dv_c, axis=0), jnp.concatenate(dqi_c, axis=0)
                dkd, extra = jnp.concatenate(dkd_c, axis=0), jnp.concatenate(extra_c, axis=0)
                dq = dqi * eb * scale
                dk = dki * enb + dkd * edb
                db = dqi * qi_f - dki * ki_f - dkd * kd_f
                dg = _dot_exact01(cum_t, db) + extra
                dz = dg * gs_s[pl.ds(r0, gr), :]
                dz_b = dz.astype(BF16)
                lrc = lr_ref[pl.ds(r0, gr), :]
                dlr_ref[pl.ds(r0, gr), :] += _dot_nt(dz_b, w_ref[...])
                if fwd:
                    dq_s[pl.ds(r0, gr), :] = dq
                    dk_s[pl.ds(r0, gr), :] = dk
                    dv_s[pl.ds(r0, gr), :] = dv
                else:
                    d_ref[pl.ds(r0, gr), 0:hk] = (dq_s[pl.ds(r0, gr), :] + dq).astype(BF16)
                    d_ref[pl.ds(r0, gr), hk:2 * hk] = (dk_s[pl.ds(r0, gr), :] + dk).astype(BF16)
                    d_ref[pl.ds(r0, gr), 2 * hk:2 * hk + hv] = (dv_s[pl.ds(r0, gr), :] + dv).astype(BF16)
                return dst, gw + _dot_tn(lrc, dz_b), gb + jnp.sum(dz, axis=0, keepdims=True)

            _, gw, gb = lax.fori_loop(0, ng, grad, (jnp.zeros((hv, hk), F32), jnp.zeros((LR_LANES, hk), F32),
                                                    jnp.zeros((1, hk), F32)))
            gw_ref[0] = gw
            gb_ref[0] = jnp.zeros((8, hk), F32)
            gb_ref[0, 0:1, :] = gb

        run(True)
        run(False)

    head = lambda s, h: (s, h)
    wspec = pl.BlockSpec((LR_LANES, hk), lambda s, h: (0, h))
    bspec = pl.BlockSpec((1, hk), lambda s, h: (0, h))
    gwspec = pl.BlockSpec((1, LR_LANES, hk), lambda s, h: (s, 0, h))
    gbspec = pl.BlockSpec((1, 8, hk), lambda s, h: (s, 0, h))
    gw_shape = jax.ShapeDtypeStruct((dm.Bl, LR_LANES, dm.DK), F32)
    gb_shape = jax.ShapeDtypeStruct((dm.Bl, 8, dm.DK), F32)
    return pl.pallas_call(
        body, name="gla_bwd", grid=(dm.Bl, HEADS),
        in_specs=[pl.BlockSpec((lp, hw), head), pl.BlockSpec((lp, LR_LANES), lambda s, h: (s, 0)),
                  pl.BlockSpec((lp, hv), head), pl.BlockSpec((lp, hv), head),
                  wspec, bspec, wspec, bspec, pl.BlockSpec((1, hv), lambda s, h: (0, 0))],
        out_specs=[pl.BlockSpec((lp, hw), head), pl.BlockSpec((lp, LR_LANES), lambda s, h: (s, 0)),
                   gwspec, gbspec, gwspec, gbspec, pl.BlockSpec((8, hv), lambda s, h: (0, 0))],
        out_shape=[jax.ShapeDtypeStruct((dm.T, HEADS * hw), BF16), jax.ShapeDtypeStruct((dm.T, LR_LANES), F32),
                   gw_shape, gb_shape, gw_shape, gb_shape, jax.ShapeDtypeStruct((8, hv), F32)],
        scratch_shapes=[pltpu.VMEM((lp, hv), BF16), pltpu.VMEM((nc, hv, hk), F32), pltpu.VMEM((lp, hk), F32),
                        pltpu.VMEM((lp, hk), F32), pltpu.VMEM((lp, hk), F32), pltpu.VMEM((lp, hk), F32),
                        pltpu.VMEM((lp, hk), F32), pltpu.VMEM((lp, hv), F32)],
        compiler_params=_cp(2),
    )(proj_b, lr, o_all, dy_gla, wg_f, bg_f, wg_b, bg_b, gla_g)


def _out_merge(y_conv, y_gla, proj_c, w_oc, w_og, dm):
    d = dm.D
    tm = _pick(dm.T, 512, 16)

    def body(yc_ref, yg_ref, c_ref, woc_ref, wog_ref, pc_ref, pg_ref, m_ref):
        pc = _dot(yc_ref[...], woc_ref[...])
        pg = _dot(yg_ref[...], wog_ref[...])
        pc_ref[...] = pc.astype(BF16)
        pg_ref[...] = pg.astype(BF16)
        ma = c_ref[:, :d].astype(F32)
        mb = c_ref[:, d:].astype(F32)
        m_ref[...] = (_sigmoid(ma) * pc + _sigmoid(mb) * pg).astype(BF16)

    row = pl.BlockSpec((tm, d), lambda i: (i, 0))
    full = pl.BlockSpec((d, d), lambda i: (0, 0))
    act = jax.ShapeDtypeStruct((dm.T, d), BF16)
    return pl.pallas_call(
        body, name="out_merge", grid=(dm.T // tm,),
        in_specs=[row, row, pl.BlockSpec((tm, 2 * d), lambda i: (i, 0)), full, full],
        out_specs=[row, row, row], out_shape=[act, act, act], compiler_params=_cp(1),
    )(y_conv, y_gla, proj_c, w_oc, w_og)


def _final_fwd(merged, w_out, x, metapad, target, g_post, dm):
    tm, tps, d = dm.TM, dm.TPS, dm.D

    def body(m_ref, w_ref, x_ref, mp_ref, t_ref, g_ref, dout_ref, dy_ref, st_ref):
        i = pl.program_id(0)
        j = i % tps
        out = _dot(m_ref[...], w_ref[...])
        rstd = lax.rsqrt(jnp.mean(out * out, axis=-1, keepdims=True) + EPS)
        ohat = out * rstd
        h = jnp.where(j == 0, mp_ref[...], x_ref[0])
        y = h + ohat * g_ref[...]
        err = jnp.where(j == 0, 0.0, y - t_ref[0])
        dy = err * (1.0 / d)
        d_oh = dy * g_ref[...]
        dout_ref[...] = (rstd * (d_oh - ohat * jnp.mean(d_oh * ohat, axis=-1, keepdims=True))).astype(BF16)
        dy_ref[...] = dy

        @pl.when(i == 0)
        def _():
            st_ref[...] = jnp.zeros_like(st_ref)

        st_ref[0:1, :] += jnp.sum(dy * ohat, axis=0, keepdims=True)
        st_ref[1:2, :] += jnp.sum(err * err, axis=0, keepdims=True)

    row = pl.BlockSpec((tm, d), lambda i: (i, 0))
    tok = pl.BlockSpec((1, tm, d), lambda i: (i // tps, jnp.maximum(i % tps - 1, 0), 0))
    const = lambda r: pl.BlockSpec((r, d), lambda i: (0, 0))
    return pl.pallas_call(
        body, name="final_fwd", grid=(dm.Bl * tps,),
        in_specs=[row, const(d), tok, const(tm), tok, const(1)],
        out_specs=[row, row, const(8)],
        out_shape=[jax.ShapeDtypeStruct((dm.T, d), BF16), jax.ShapeDtypeStruct((dm.T, d), F32),
                   jax.ShapeDtypeStruct((8, d), F32)],
        compiler_params=_cp(1),
    )(merged, w_out, x, metapad, target, g_post)


def _merge_bwd(d_out, proj_c, p_conv, p_gla, w_out, w_oc, w_og, dm):
    d = dm.D
    tm = _pick(dm.T, 512, 16)

    def body(do_ref, c_ref, pc_ref, pg_ref, wo_ref, woc_ref, wog_ref, dpc_ref, dpg_ref, dc_ref, dyc_ref, dyg_ref):
        dmg = _dot_nt(do_ref[...], wo_ref[...])
        sa = _sigmoid(c_ref[:, :d].astype(F32))
        sb = _sigmoid(c_ref[:, d:].astype(F32))
        dpc = (dmg * sa).astype(BF16)
        dpg = (dmg * sb).astype(BF16)
        dpc_ref[...] = dpc
        dpg_ref[...] = dpg
        dc_ref[:, :d] = (dmg * pc_ref[...].astype(F32) * sa * (1.0 - sa)).astype(BF16)
        dc_ref[:, d:] = (dmg * pg_ref[...].astype(F32) * sb * (1.0 - sb)).astype(BF16)
        dyc_ref[...] = _dot_nt(dpc, woc_ref[...]).astype(BF16)
        dyg_ref[...] = _dot_nt(dpg, wog_ref[...]).astype(BF16)

    row = pl.BlockSpec((tm, d), lambda i: (i, 0))
    row2 = pl.BlockSpec((tm, 2 * d), lambda i: (i, 0))
    full = pl.BlockSpec((d, d), lambda i: (0, 0))
    act = jax.ShapeDtypeStruct((dm.T, d), BF16)
    return pl.pallas_call(
        body, name="merge_bwd", grid=(dm.T // tm,),
        in_specs=[row, row2, row, row, full, full, full],
        out_specs=[row, row, row2, row, row],
        out_shape=[act, act, jax.ShapeDtypeStruct((dm.T, 2 * d), BF16), act, act],
        compiler_params=_cp(1),
    )(d_out, proj_c, p_conv, p_gla, w_out, w_oc, w_og)


def _prenorm_bwd(du, dy, x, metapad, g_pre, dm):
    tm, tps, d = dm.TM, dm.TPS, dm.D

    def body(du_ref, dy_ref, x_ref, mp_ref, g_ref, gx_ref, dmeta_ref, gg_ref):
        i = pl.program_id(0)
        j = i % tps
        h = jnp.where(j == 0, mp_ref[...], x_ref[0])
        rstd = lax.rsqrt(jnp.mean(h * h, axis=-1, keepdims=True) + EPS)
        hhat = h * rstd
        dug = du_ref[...] * g_ref[...]
        dh = dy_ref[...] + rstd * (dug - hhat * jnp.mean(dug * hhat, axis=-1, keepdims=True))

        @pl.when(j == 0)
        def _():
            dmeta_ref[0] = dh

        @pl.when(j > 0)
        def _():
            gx_ref[0] = dh

        @pl.when(i == 0)
        def _():
            gg_ref[...] = jnp.zeros_like(gg_ref)

        gg_ref[0:1, :] += jnp.sum(du_ref[...] * hhat, axis=0, keepdims=True)

    row = pl.BlockSpec((tm, d), lambda i: (i, 0))
    tok = pl.BlockSpec((1, tm, d), lambda i: (i // tps, jnp.maximum(i % tps - 1, 0), 0))
    const = lambda r: pl.BlockSpec((r, d), lambda i: (0, 0))
    return pl.pallas_call(
        body, name="prenorm_bwd", grid=(dm.Bl * tps,),
        in_specs=[row, row, tok, const(tm), const(1)],
        out_specs=[tok, pl.BlockSpec((1, tm, d), lambda i: (i // tps, 0, 0)), const(8)],
        out_shape=[jax.ShapeDtypeStruct((dm.Bl, dm.S, d), F32), jax.ShapeDtypeStruct((dm.Bl, tm, d), F32),
                   jax.ShapeDtypeStruct((8, d), F32)],
        compiler_params=_cp(1),
    )(du, dy, x, metapad, g_pre)


def _sum_partials(partials, name):
    p, r, c = partials.shape
    tc = _pick(c, 256, 128)

    def body(p_ref, o_ref):
        g = p_ref[0].astype(F32)
        for j in range(1, p):
            g = g + p_ref[j].astype(F32)
        o_ref[...] = g

    return pl.pallas_call(
        body, name=name, grid=(c // tc,), in_specs=[pl.BlockSpec((p, r, tc), lambda i: (0, 0, i))],
        out_specs=pl.BlockSpec((r, tc), lambda i: (0, i)), out_shape=jax.ShapeDtypeStruct((r, c), F32),
        compiler_params=_cp(1),
    )(partials)


def _adamw(partials, w, m, v, name):
    r, c = w.shape
    n_parts = partials.shape[0]
    tr = _pick(r, 256, 16)

    def body(p_ref, w_ref, m_ref, v_ref, g_ref, d_ref, nm_ref, nv_ref):
        g = p_ref[0].astype(F32)
        for j in range(1, n_parts):
            g = g + p_ref[j].astype(F32)
        m2 = ADAM_B1 * m_ref[...] + (1.0 - ADAM_B1) * g
        v2 = ADAM_B2 * v_ref[...] + (1.0 - ADAM_B2) * (g * g)
        m_hat = m2 / (1.0 - ADAM_B1 ** ADAM_STEP)
        v_hat = v2 / (1.0 - ADAM_B2 ** ADAM_STEP)
        g_ref[...] = g
        d_ref[...] = -ADAM_LR * (m_hat / (jnp.sqrt(v_hat) + ADAM_EPS) + ADAM_WD * w_ref[...])
        nm_ref[...] = m2
        nv_ref[...] = v2

    row = pl.BlockSpec((tr, c), lambda i: (i, 0))
    out = jax.ShapeDtypeStruct((r, c), F32)
    return pl.pallas_call(
        body, name=name, grid=(r // tr,),
        in_specs=[pl.BlockSpec((n_parts, tr, c), lambda i: (0, i, 0)), row, row, row],
        out_specs=[row, row, row, row], out_shape=[out, out, out, out], compiler_params=_cp(1),
    )(partials, w, m, v)


def _pack_rows(wt, dm):
    d, dk, hk, hv, cw, nj = dm.D, dm.DK, dm.HK, dm.HV, dm.CW, dm.NJ
    a = wt[:4 * d].reshape(4, nj, cw, d).transpose(1, 0, 2, 3).reshape(4 * d, d)
    q, k, v, r = wt[4 * d:4 * d + dk], wt[4 * d + dk:5 * d], wt[5 * d:6 * d], wt[6 * d:7 * d]
    b = jnp.concatenate([t for h in range(HEADS) for t in (q[h * hk:(h + 1) * hk], k[h * hk:(h + 1) * hk],
                                                           v[h * hv:(h + 1) * hv], r[h * hv:(h + 1) * hv])], axis=0)
    c = wt[7 * d + 2 * RANK:]
    lr = jnp.pad(wt[7 * d:7 * d + 2 * RANK], ((0, LR_LANES - 2 * RANK), (0, 0)))
    return a, b, c, lr


def _unpack_rows(a, b, c, lr, dm):
    d, hk, hv, cw, nj, hw = dm.D, dm.HK, dm.HV, dm.CW, dm.NJ, dm.HW
    conv = a.reshape(nj, 4, cw, d).transpose(1, 0, 2, 3).reshape(4 * d, d)
    heads = [b[h * hw:(h + 1) * hw] for h in range(HEADS)]
    q = jnp.concatenate([t[:hk] for t in heads], axis=0)
    k = jnp.concatenate([t[hk:2 * hk] for t in heads], axis=0)
    v = jnp.concatenate([t[2 * hk:2 * hk + hv] for t in heads], axis=0)
    r = jnp.concatenate([t[2 * hk + hv:] for t in heads], axis=0)
    return jnp.concatenate([conv, q, k, v, r, lr[:2 * RANK], c], axis=0)


def _to_blob(pieces, dtype, row_mult):
    lead = pieces[0].shape[0]
    flat = jnp.concatenate([p.reshape(lead, -1).astype(dtype) for p in pieces], axis=1)
    unit = row_mult * BLOB_LANES
    padded = -(-flat.shape[1] // unit) * unit
    flat = jnp.pad(flat, ((0, 0), (0, padded - flat.shape[1])))
    return flat.reshape(lead, padded // BLOB_LANES, BLOB_LANES)


def _from_blob(blob, shapes):
    lead = blob.shape[0]
    flat = blob.reshape(lead, -1)
    out, off = [], 0
    for shp in shapes:
        size = int(np.prod(shp))
        out.append(flat[:, off:off + size].reshape((lead,) + tuple(shp)))
        off += size
    return out


def _local_step(x, target, meta, g_pre, wt_in, conv_w, wg_f, bg_f, wg_b, bg_b, gla_g, out_weights, g_post,
                on_matrix_grads=None):
    bl, s, d = x.shape
    dm = _Dims(bl, s, d)
    metapad = jnp.concatenate([jnp.zeros((dm.TM - N_META, d), F32), meta], axis=0)
    wta, wtb, wtc, wtlr = _pack_rows(wt_in, dm)
    wgp_f = jnp.pad(wg_f, ((0, LR_LANES - RANK), (0, 0))).astype(BF16)
    wgp_b = jnp.pad(wg_b, ((RANK, LR_LANES - 2 * RANK), (0, 0))).astype(BF16)

    u = _prenorm(x, metapad, g_pre, dm)
    proj_a = _matmul_nt(u, wta, BF16, "inproj_conv")
    proj_b = _matmul_nt(u, wtb, BF16, "inproj_gla")
    proj_c = _matmul_nt(u, wtc, BF16, "inproj_merge")
    lr = _matmul_nt(u, wtlr, BF16, "inproj_gate")
    y_conv = _conv_fwd(proj_a, conv_w, dm)
    o_all, y_gla = _gla_fwd(proj_b, lr, wgp_f, bg_f, wgp_b, bg_b, gla_g, dm)
    w_oc, w_og, w_out = out_weights(y_conv) if callable(out_weights) else out_weights
    p_conv, p_gla, merged = _out_merge(y_conv, y_gla, proj_c, w_oc, w_og, dm)
    d_out, dy, stats = _final_fwd(merged, w_out, x, metapad, target, g_post, dm)
    loss = 0.5 / d * jnp.sum(stats[1])

    d_pc, d_pg, d_c, dy_conv, dy_gla = _merge_bwd(d_out, proj_c, p_conv, p_gla, w_out, w_oc, w_og, dm)
    g_out = _matmul_tn(merged, d_out, "grad_w_out")
    g_oc = _matmul_tn(y_conv, d_pc, "grad_w_out_conv")
    g_og = _matmul_tn(y_gla, d_pg, "grad_w_out_gla")
    d_a, g_conv = _conv_bwd(proj_a, dy_conv, conv_w, dm)
    d_b, d_lr, gwp_f, gbp_f, gwp_b, gbp_b, g_gla = _gla_bwd(proj_b, lr, o_all, dy_gla, wgp_f, bg_f, wgp_b, bg_b, gla_g, dm)
    g_in = _unpack_rows(_matmul_tn(d_a, u, "grad_w_in_conv"), _matmul_tn(d_b, u, "grad_w_in_gla"),
                        _matmul_tn(d_c, u, "grad_w_in_merge"), _matmul_tn(d_lr, u, "grad_w_in_gate"), dm)
    if on_matrix_grads is not None:
        wtlr = wtlr + on_matrix_grads(dict(w_in=g_in, w_out_conv=g_oc, w_out_gla=g_og, w_merge_out=g_out)).astype(BF16)
    du = _matmul_parts([(d_a, wta), (d_b, wtb), (d_c, wtc), (d_lr, wtlr)], "grad_u")
    grad_x, d_meta, g_pre_rows = _prenorm_bwd(du, dy, x, metapad, g_pre, dm)

    grads = dict(
        meta_tokens=jnp.sum(d_meta[:, dm.TM - N_META:, :], axis=0), norm_pre=g_pre_rows[0:1], w_in=g_in,
        conv_w=g_conv[0:3], w_gate_fwd=jnp.sum(gwp_f, axis=0)[:RANK], b_gate_fwd=jnp.sum(gbp_f, axis=0)[0:1],
        w_gate_bwd=jnp.sum(gwp_b, axis=0)[RANK:2 * RANK], b_gate_bwd=jnp.sum(gbp_b, axis=0)[0:1],
        gla_norm=g_gla[0:1], w_out_conv=g_oc, w_out_gla=g_og, w_merge_out=g_out, norm_post=stats[0:1])
    return loss, grad_x, grads


MATRICES = ("w_out_conv", "w_out_gla", "w_merge_out")
SMALL_SHARDED = ("meta_tokens", "conv_w", "w_gate_fwd", "w_gate_bwd")
REPLICATED = ("norm_pre", "b_gate_fwd", "b_gate_bwd", "gla_norm", "norm_post")
NAMES = ("meta_tokens", "norm_pre", "w_in", "conv_w", "w_gate_fwd", "b_gate_fwd", "w_gate_bwd", "b_gate_bwd", "gla_norm",
         "w_out_conv", "w_out_gla", "w_merge_out", "norm_post")
DEPTH_AXIS = ("w_in", "conv_w", "w_gate_fwd", "w_gate_bwd") + MATRICES


def _cols_to_devices(g):
    r, c = g.shape
    return g.reshape(r, N_DEV, c // N_DEV).transpose(1, 0, 2)


def _cols_from_devices(parts):
    n, r, c = parts.shape
    return parts.transpose(1, 0, 2).reshape(r, n * c)


def kernel(x, meta_tokens, norm_pre, w_in, conv_w, w_gate_fwd, b_gate_fwd, w_gate_bwd, b_gate_bwd, gla_norm, w_out_conv, w_out_gla, w_merge_out, norm_post, loss_target, m_meta_tokens, m_norm_pre, m_w_in, m_conv_w, m_w_gate_fwd, m_b_gate_fwd, m_w_gate_bwd, m_b_gate_bwd, m_gla_norm, m_w_out_conv, m_w_out_gla, m_w_merge_out, m_norm_post, v_meta_tokens, v_norm_pre, v_w_in, v_conv_w, v_w_gate_fwd, v_b_gate_fwd, v_w_gate_bwd, v_b_gate_bwd, v_gla_norm, v_w_out_conv, v_w_out_gla, v_w_merge_out, v_norm_post):
    w = dict(meta_tokens=meta_tokens, norm_pre=norm_pre, w_in=w_in[0], conv_w=conv_w[0], w_gate_fwd=w_gate_fwd[0],
             b_gate_fwd=b_gate_fwd, w_gate_bwd=w_gate_bwd[0], b_gate_bwd=b_gate_bwd, gla_norm=gla_norm,
             w_out_conv=w_out_conv[0], w_out_gla=w_out_gla[0], w_merge_out=w_merge_out[0], norm_post=norm_post)
    m = dict(meta_tokens=m_meta_tokens, norm_pre=m_norm_pre, w_in=m_w_in[0], conv_w=m_conv_w[0], w_gate_fwd=m_w_gate_fwd[0],
             b_gate_fwd=m_b_gate_fwd, w_gate_bwd=m_w_gate_bwd[0], b_gate_bwd=m_b_gate_bwd, gla_norm=m_gla_norm,
             w_out_conv=m_w_out_conv[0], w_out_gla=m_w_out_gla[0], w_merge_out=m_w_merge_out[0], norm_post=m_norm_post)
    v = dict(meta_tokens=v_meta_tokens, norm_pre=v_norm_pre, w_in=v_w_in[0], conv_w=v_conv_w[0], w_gate_fwd=v_w_gate_fwd[0],
             b_gate_fwd=v_b_gate_fwd, w_gate_bwd=v_w_gate_bwd[0], b_gate_bwd=v_b_gate_bwd, gla_norm=v_gla_norm,
             w_out_conv=v_w_out_conv[0], w_out_gla=v_w_out_gla[0], w_merge_out=v_w_merge_out[0], norm_post=v_norm_post)
    d = x.shape[-1]

    small_blob = _to_blob([w[n][None] for n in SMALL_SHARDED], F32, SMALL_ROWS)[0]
    wt_all, small_all = _exchange([w["w_in"].T.astype(BF16), small_blob], [], "gather_weights")
    _, late_weights = _exchange_start([w[n].astype(BF16) for n in MATRICES], [], small_all, "gather_out_weights_start")
    wt_in = wt_all.reshape(-1, d)
    small = {n: _cols_from_devices(p) for n, p in zip(SMALL_SHARDED, _from_blob(small_all, [w[n].shape for n in SMALL_SHARDED]))}

    def out_weights(after):
        return tuple(a.reshape(-1, d) for a in _exchange_wait(late_weights, after, "gather_out_weights_wait"))

    pending = []

    def on_matrix_grads(g):
        to_send = [g[n].astype(BF16).reshape(N_DEV, -1, d) for n in ("w_in",) + MATRICES]
        token, state = _exchange_start([], to_send, g["w_out_conv"], "exchange_grads_start")
        pending.append(state)
        return token

    loss, grad_x, grads = _local_step(
        x, loss_target, small["meta_tokens"], norm_pre, wt_in, small["conv_w"], small["w_gate_fwd"], b_gate_fwd,
        small["w_gate_bwd"], b_gate_bwd, gla_norm, out_weights, norm_post, on_matrix_grads)
    loss = lax.psum(loss, ("x", "y", "c"))
    received = _exchange_wait(pending[0], grad_x, "exchange_grads_wait")

    small_send = _to_blob([_cols_to_devices(grads[n]) for n in SMALL_SHARDED], F32, SMALL_ROWS)
    repl_blob = _to_blob([grads[n][None] for n in REPLICATED], F32, SMALL_ROWS)[0]
    repl_all, small_recv = _exchange([repl_blob], [small_send], "exchange_small_grads")

    results = {"w_in": _adamw(_sum_partials(received[0], "sum_grad_w_in").T[None], w["w_in"], m["w_in"], v["w_in"], "adamw_w_in")}
    for n, partials in zip(MATRICES, received[1:4]):
        results[n] = _adamw(partials, w[n], m[n], v[n], "adamw_" + n)
    for names, partials, tag in ((SMALL_SHARDED, small_recv, "small"), (REPLICATED, repl_all, "replicated")):
        blobs = [_to_blob([t[n][None] for n in names], F32, SMALL_ROWS)[0] for t in (w, m, v)]
        res = [_from_blob(r[None], [w[n].shape for n in names]) for r in _adamw(partials, *blobs, name="adamw_" + tag)]
        for i, n in enumerate(names):
            results[n] = [r[i][0] for r in res]
    lead = lambda n, t: t[None] if n in DEPTH_AXIS else t
    return (loss, grad_x, *[lead(n, results[n][i]) for i in range(4) for n in NAMES])
```

```python
import functools

import jax
import jax.numpy as jnp
import numpy as np
from jax import lax
from jax.experimental import pallas as pl
from jax.experimental.pallas import tpu as pltpu

F32 = jnp.float32
BF16 = jnp.bfloat16
MESH = pl.DeviceIdType.MESH

N_META = 16
CHUNK = 64
CHUNK_SHIFT = 6
HEADS = 4
RANK = 16
LR_LANES = 128
PAD_ROWS = CHUNK - N_META
EPS = 1e-6
GATE_NORMALIZER = 16.0
N_DEV = 8
ADAM_LR, ADAM_B1, ADAM_B2, ADAM_EPS, ADAM_WD, ADAM_STEP = 0.001, 0.9, 0.999, 1e-08, 0.01, 10
VMEM_LIMIT_BYTES = 56 * 1024 * 1024
BLOB_LANES = 512
SMALL_ROWS = 16


class _Dims:
    def __init__(self, bl, s, d):
        self.Bl, self.S, self.D = bl, s, d
        self.TM = 256 if s % 256 == 0 else CHUNK
        self.LP = self.TM + s
        self.T = bl * self.LP
        self.TPS = self.LP // self.TM
        self.NC = self.LP // CHUNK
        self.C0 = (self.TM - CHUNK) // CHUNK
        self.DK, self.DV = d // 2, d
        self.HK, self.HV = self.DK // HEADS, self.DV // HEADS
        self.HW = 2 * self.HK + 2 * self.HV
        self.CW = 256 if d % 256 == 0 and d > 256 else d // 4
        self.NJ = d // self.CW


def _pick(n, target, mult):
    t = min(n, target)
    while t >= mult:
        if n % t == 0 and t % mult == 0:
            return t
        t -= mult
    return n


def _cp(n_axes):
    return pltpu.CompilerParams(dimension_semantics=("arbitrary",) * n_axes, vmem_limit_bytes=VMEM_LIMIT_BYTES)


def _sigmoid(x):
    return 1.0 / (1.0 + jnp.exp(-x))


def _dot(a, b):
    return jnp.dot(a, b, preferred_element_type=F32)


def _dot_nt(a, b):
    return lax.dot_general(a, b, (((1,), (1,)), ((), ())), preferred_element_type=F32)


def _dot_tn(a, b):
    return lax.dot_general(a, b, (((0,), (0,)), ((), ())), preferred_element_type=F32)


def _dot_exact01(m01, x):
    hi = x.astype(BF16)
    lo = (x - hi.astype(F32)).astype(BF16)
    return _dot(m01, hi) + _dot(m01, lo)


def _exchange(gathers, scatters, name):
    arrays = list(gathers) + list(scatters)
    n, ng = len(arrays), len(gathers)

    def body(*refs):
        ins, outs = refs[:n], refs[n:2 * n]
        send_sems, recv_sems, local_sems = refs[2 * n:]
        x, y, c = lax.axis_index("x"), lax.axis_index("y"), lax.axis_index("c")
        me = 4 * x + 2 * y + c
        started = []
        for t in range(n):
            src, dst = ins[t], outs[t]
            own = pltpu.make_async_copy(src if t < ng else src.at[me], dst.at[me], local_sems.at[t])
            own.start()
            started.append(own)
            for k, pos, peer in _peers(x, y, c):
                cp = pltpu.make_async_remote_copy(
                    src_ref=src if t < ng else src.at[peer], dst_ref=dst.at[me],
                    send_sem=send_sems.at[t * (N_DEV - 1) + k - 1], recv_sem=recv_sems.at[t * (N_DEV - 1) + k - 1],
                    device_id=pos, device_id_type=MESH)
                cp.start()
                started.append(cp)
        for cp in started:
            cp.wait()

    out_shape = [jax.ShapeDtypeStruct((N_DEV,) + a.shape[-2:], a.dtype) for a in arrays]
    any_spec = pl.BlockSpec(memory_space=pl.ANY)
    return pl.pallas_call(
        body, name=name, out_shape=out_shape, in_specs=[any_spec] * n, out_specs=[any_spec] * n,
        scratch_shapes=[pltpu.SemaphoreType.DMA((n * (N_DEV - 1),)), pltpu.SemaphoreType.DMA((n * (N_DEV - 1),)),
                        pltpu.SemaphoreType.DMA((n,))],
        compiler_params=pltpu.CompilerParams(has_side_effects=True),
    )(*arrays)


def _gather_two_level(arrays, name):
    n = len(arrays)
    per = N_DEV - 1

    def body(*refs):
        ins, outs = refs[:n], refs[n:2 * n]
        send_sems, recv_sems, local_sems = refs[2 * n:]
        x, y, c = lax.axis_index("x"), lax.axis_index("y"), lax.axis_index("c")
        sibling = (x, y, 1 - c)
        chips = [(1 - x, y), (x, 1 - y), (1 - x, 1 - y)]
        index = lambda px, py, pc: 4 * px + 2 * py + pc

        def copy(t, k, block, to, from_input=False):
            slab = outs[t].at[index(*block)]
            return pltpu.make_async_remote_copy(
                src_ref=ins[t] if from_input else slab, dst_ref=slab, send_sem=send_sems.at[t * per + k],
                recv_sem=recv_sems.at[t * per + k], device_id=to, device_id_type=MESH)

        own, sent = [], []
        for t in range(n):
            own.append(pltpu.make_async_copy(ins[t], outs[t].at[index(x, y, c)], local_sems.at[t]))
            own[-1].start()
            first = [copy(t, 0, (x, y, c), sibling, True)]
            first += [copy(t, 1 + j, (x, y, c), (*chip, c), True) for j, chip in enumerate(chips)]
            for cp in first:
                cp.start()
            sent += first
        for t in range(n):
            for j, chip in enumerate(chips):
                copy(t, 1 + j, (*chip, c), (x, y, c)).wait_recv()
                sent.append(copy(t, 4 + j, (*chip, c), sibling))
                sent[-1].start()
        for t in range(n):
            copy(t, 0, sibling, (x, y, c)).wait_recv()
            for j, chip in enumerate(chips):
                copy(t, 4 + j, (*chip, 1 - c), (x, y, c)).wait_recv()
        for cp in sent:
            cp.wait_send()
        for cp in own:
            cp.wait()

    out_shape = [jax.ShapeDtypeStruct((N_DEV,) + a.shape, a.dtype) for a in arrays]
    any_spec = pl.BlockSpec(memory_space=pl.ANY)
    return pl.pallas_call(
        body, name=name, out_shape=out_shape, in_specs=[any_spec] * n, out_specs=[any_spec] * n,
        scratch_shapes=[pltpu.SemaphoreType.DMA((n * per,)), pltpu.SemaphoreType.DMA((n * per,)),
                        pltpu.SemaphoreType.DMA((n,))],
        compiler_params=pltpu.CompilerParams(has_side_effects=True),
    )(*arrays)


def _peers(x, y, c):
    out = []
    for k in range(1, N_DEV):
        px = 1 - x if (k >> 2) & 1 else x
        py = 1 - y if (k >> 1) & 1 else y
        pc = 1 - c if k & 1 else c
        out.append((k, (px, py, pc), 4 * px + 2 * py + pc))
    return out


def _exchange_start(gathers, scatters, after, name):
    arrays = list(gathers) + list(scatters)
    n, ng = len(arrays), len(gathers)
    hbm = pl.BlockSpec(memory_space=pltpu.HBM)
    sem = pl.BlockSpec(memory_space=pltpu.SEMAPHORE)

    def body(*refs):
        ins, lands = refs[:n], refs[n:2 * n]
        send_sems, recv_sems = refs[2 * n + 1], refs[2 * n + 2]
        token = refs[4 * n + 3]
        x, y, c = lax.axis_index("x"), lax.axis_index("y"), lax.axis_index("c")
        me = 4 * x + 2 * y + c
        for t in range(n):
            for k, pos, peer in _peers(x, y, c):
                pltpu.make_async_remote_copy(
                    src_ref=ins[t] if t < ng else ins[t].at[peer], dst_ref=lands[t].at[me],
                    send_sem=send_sems.at[t * (N_DEV - 1) + k - 1], recv_sem=recv_sems.at[t * (N_DEV - 1) + k - 1],
                    device_id=pos, device_id_type=MESH).start()
        token[...] = jnp.zeros_like(token)

    me = 4 * lax.axis_index("x") + 2 * lax.axis_index("y") + lax.axis_index("c")
    lands = [lax.dynamic_update_index_in_dim(lax.empty((N_DEV,) + a.shape[-2:], a.dtype),
                                             a if t < ng else lax.dynamic_index_in_dim(a, me, 0, keepdims=False), me, 0)
             for t, a in enumerate(arrays)]
    operands = [pltpu.with_memory_space_constraint(a, pltpu.HBM) for a in arrays + lands]
    sems = pltpu.SemaphoreType.DMA((n * (N_DEV - 1),))
    res = pl.pallas_call(
        body, name=name,
        out_shape=(sems, sems, *[pltpu.HBM(a.shape, a.dtype) for a in arrays + lands], jax.ShapeDtypeStruct((8, 128), F32)),
        in_specs=[hbm] * (2 * n) + [pl.BlockSpec(memory_space=pl.ANY)],
        out_specs=(sem, sem, *[hbm] * (2 * n), pl.BlockSpec(memory_space=pltpu.VMEM)),
        input_output_aliases={i: 2 + i for i in range(2 * n)},
        compiler_params=pltpu.CompilerParams(has_side_effects=pltpu.SideEffectType.DATAFLOW_SIDE_EFFECTING),
    )(*operands, after)
    return res[-1][0, 0], (ng, res[0], res[1], list(res[2:2 + n]), list(res[2 + n:2 + 2 * n]))


def _exchange_wait(state, after, name):
    ng, send_sems, recv_sems, sent, lands = state
    n = len(sent)
    hbm = pl.BlockSpec(memory_space=pltpu.HBM)
    sem = pl.BlockSpec(memory_space=pltpu.SEMAPHORE)

    def body(*refs):
        ins, land_refs = refs[:n], refs[n:2 * n]
        send_ref, recv_ref = refs[2 * n], refs[2 * n + 1]
        x, y, c = lax.axis_index("x"), lax.axis_index("y"), lax.axis_index("c")
        me = 4 * x + 2 * y + c
        for t in range(n):
            for k, pos, peer in _peers(x, y, c):
                cp = pltpu.make_async_remote_copy(
                    src_ref=ins[t] if t < ng else ins[t].at[peer], dst_ref=land_refs[t].at[me],
                    send_sem=send_ref.at[t * (N_DEV - 1) + k - 1], recv_sem=recv_ref.at[t * (N_DEV - 1) + k - 1],
                    device_id=pos, device_id_type=MESH)
                cp.wait_send()
                cp.wait_recv()

    res = pl.pallas_call(
        body, name=name, out_shape=tuple(pltpu.HBM(a.shape, a.dtype) for a in sent + lands),
        in_specs=[hbm] * (2 * n) + [sem, sem, pl.BlockSpec(memory_space=pl.ANY)], out_specs=tuple([hbm] * (2 * n)),
        input_output_aliases={i: i for i in range(2 * n)},
        compiler_params=pltpu.CompilerParams(has_side_effects=pltpu.SideEffectType.DATAFLOW_SIDE_EFFECTING),
    )(*sent, *lands, send_sems, recv_sems, after)
    return list(res[n:])


def _prenorm(x, metapad, g_pre, dm):
    tm, tps, d = dm.TM, dm.TPS, dm.D

    def body(x_ref, mp_ref, g_ref, u_ref):
        j = pl.program_id(0) % tps
        h = jnp.where(j == 0, mp_ref[...], x_ref[0])
        r = lax.rsqrt(jnp.mean(h * h, axis=-1, keepdims=True) + EPS)
        u_ref[...] = (h * r * g_ref[...]).astype(BF16)

    return pl.pallas_call(
        body, name="prenorm", grid=(dm.Bl * tps,),
        in_specs=[pl.BlockSpec((1, tm, d), lambda i: (i // tps, jnp.maximum(i % tps - 1, 0), 0)),
                  pl.BlockSpec((tm, d), lambda i: (0, 0)),
                  pl.BlockSpec((1, d), lambda i: (0, 0))],
        out_specs=pl.BlockSpec((tm, d), lambda i: (i, 0)),
        out_shape=jax.ShapeDtypeStruct((dm.T, d), BF16), compiler_params=_cp(1),
    )(x, metapad, g_pre)


def _matmul_nt(a, bt, out_dtype, name, tm=1024, tn=1024):
    m, k = a.shape
    n = bt.shape[0]
    tm, tn = _pick(m, tm, 16), _pick(n, tn, 128)

    def body(a_ref, b_ref, o_ref):
        o_ref[...] = _dot_nt(a_ref[...].astype(BF16), b_ref[...]).astype(out_dtype)

    return pl.pallas_call(
        body, name=name, grid=(n // tn, m // tm),
        in_specs=[pl.BlockSpec((tm, k), lambda j, i: (i, 0)), pl.BlockSpec((tn, k), lambda j, i: (j, 0))],
        out_specs=pl.BlockSpec((tm, tn), lambda j, i: (i, j)),
        out_shape=jax.ShapeDtypeStruct((m, n), out_dtype), compiler_params=_cp(2),
    )(a, bt)


def _matmul_parts(parts, name, tm=1024, tk=1024):
    m = parts[0][0].shape[0]
    n = parts[0][1].shape[1]
    tm = _pick(m, tm, 16)
    tks = [_pick(a.shape[1], tk, 128) for a, _ in parts]
    counts = [a.shape[1] // t for (a, _), t in zip(parts, tks)]
    starts = [int(s) for s in np.cumsum([0] + counts[:-1])]
    total = sum(counts)

    def body(*refs):
        acc = refs[-1]
        o_ref = refs[-2]
        k = pl.program_id(1)

        @pl.when(k == 0)
        def _():
            acc[...] = jnp.zeros_like(acc)

        for p in range(len(parts)):
            @pl.when((k >= starts[p]) & (k < starts[p] + counts[p]))
            def _(p=p):
                acc[...] += _dot(refs[2 * p][...].astype(BF16), refs[2 * p + 1][...])

        @pl.when(k == total - 1)
        def _():
            o_ref[...] = acc[...]

    in_specs, operands = [], []
    for (a, b), t, s, cnt in zip(parts, tks, starts, counts):
        in_specs.append(pl.BlockSpec((tm, t), lambda i, k, s=s, cnt=cnt: (i, jnp.clip(k - s, 0, cnt - 1))))
        in_specs.append(pl.BlockSpec((t, n), lambda i, k, s=s, cnt=cnt: (jnp.clip(k - s, 0, cnt - 1), 0)))
        operands += [a, b]
    return pl.pallas_call(
        body, name=name, grid=(m // tm, total), in_specs=in_specs,
        out_specs=pl.BlockSpec((tm, n), lambda i, k: (i, 0)),
        out_shape=jax.ShapeDtypeStruct((m, n), F32), scratch_shapes=[pltpu.VMEM((tm, n), F32)],
        compiler_params=_cp(2),
    )(*operands)


def _matmul_tn(a, b, name, tt=768, tn=1024):
    t, k = a.shape
    n = b.shape[1]
    tt, tn, tk = _pick(t, tt, 16), _pick(n, tn, 128), _pick(k, 1024, 128)

    def body(a_ref, b_ref, o_ref):
        p = _dot_tn(a_ref[...].astype(BF16), b_ref[...].astype(BF16))

        @pl.when(pl.program_id(2) == 0)
        def _():
            o_ref[...] = p

        @pl.when(pl.program_id(2) > 0)
        def _():
            o_ref[...] += p

    return pl.pallas_call(
        body, name=name, grid=(k // tk, n // tn, t // tt),
        in_specs=[pl.BlockSpec((tt, tk), lambda kk, j, i: (i, kk)), pl.BlockSpec((tt, tn), lambda kk, j, i: (i, j))],
        out_specs=pl.BlockSpec((tk, tn), lambda kk, j, i: (kk, j)),
        out_shape=jax.ShapeDtypeStruct((k, n), F32), compiler_params=_cp(3),
    )(a, b)


def _conv_rows(dm):
    return _pick(dm.LP, 256, 16)


def _shifted(m, prev_row, next_row, rows):
    row = lax.broadcasted_iota(jnp.int32, m.shape, 0)
    m_prev = jnp.where(row == 0, prev_row, pltpu.roll(m, 1, 0))
    m_next = jnp.where(row == rows - 1, next_row, pltpu.roll(m, rows - 1, 0))
    return m_prev, m_next


def _conv_fwd(proj_a, conv_w, dm):
    lp, cw, rc = dm.LP, dm.CW, _conv_rows(dm)
    nchunk = lp // rc

    def body(p_ref, w_ref, y_ref):
        w0, w1, w2 = w_ref[0:1, :], w_ref[1:2, :], w_ref[2:3, :]

        def chunk(ci, carry):
            r0 = pl.multiple_of(ci * rc, rc)
            blk = p_ref[pl.ds(r0, rc), :].astype(F32)
            cb, cc, cx, cz = (blk[:, i * cw:(i + 1) * cw] for i in range(4))
            m = cc * cx
            rp = pl.multiple_of(jnp.maximum(r0 - 16, 0), 16)
            rn = pl.multiple_of(jnp.minimum(r0 + rc, lp - 16), 16)
            pv = p_ref[pl.ds(rp, 16), cw:3 * cw].astype(F32)
            nx = p_ref[pl.ds(rn, 16), cw:3 * cw].astype(F32)
            prev_row = jnp.where(ci > 0, pv[15:16, :cw] * pv[15:16, cw:], 0.0)
            next_row = jnp.where(ci < nchunk - 1, nx[0:1, :cw] * nx[0:1, cw:], 0.0)
            m_prev, m_next = _shifted(m, prev_row, next_row, rc)
            s = w0 * m_prev + w1 * m + w2 * m_next
            y_ref[pl.ds(r0, rc), :] = (cb * s * (cz * _sigmoid(cz))).astype(BF16)
            return carry

        lax.fori_loop(0, nchunk, chunk, 0)

    return pl.pallas_call(
        body, name="conv_fwd", grid=(dm.Bl, dm.NJ),
        in_specs=[pl.BlockSpec((lp, 4 * cw), lambda s, j: (s, j)), pl.BlockSpec((3, cw), lambda s, j: (0, j))],
        out_specs=pl.BlockSpec((lp, cw), lambda s, j: (s, j)),
        out_shape=jax.ShapeDtypeStruct((dm.T, dm.D), BF16), compiler_params=_cp(2),
    )(proj_a, conv_w)


def _conv_bwd(proj_a, dy_conv, conv_w, dm):
    lp, cw, rc = dm.LP, dm.CW, _conv_rows(dm)
    nchunk = lp // rc

    def body(p_ref, dy_ref, w_ref, d_ref, gw_ref):
        w0, w1, w2 = w_ref[0:1, :], w_ref[1:2, :], w_ref[2:3, :]

        def ds_of(p4, dy):
            cb, cz = p4[:, :cw], p4[:, 3 * cw:]
            return dy * cb * (cz * _sigmoid(cz))

        def chunk(ci, carry):
            g0, g1, g2 = carry
            r0 = pl.multiple_of(ci * rc, rc)
            blk = p_ref[pl.ds(r0, rc), :].astype(F32)
            dy = dy_ref[pl.ds(r0, rc), :].astype(F32)
            cb, cc, cx, cz = (blk[:, i * cw:(i + 1) * cw] for i in range(4))
            rp = pl.multiple_of(jnp.maximum(r0 - 16, 0), 16)
            rn = pl.multiple_of(jnp.minimum(r0 + rc, lp - 16), 16)
            pv = p_ref[pl.ds(rp, 16), :].astype(F32)[15:16]
            nx = p_ref[pl.ds(rn, 16), :].astype(F32)[0:1]
            dpv = dy_ref[pl.ds(rp, 16), :].astype(F32)[15:16]
            dnx = dy_ref[pl.ds(rn, 16), :].astype(F32)[0:1]
            has_prev, has_next = ci > 0, ci < nchunk - 1
            m = cc * cx
            m_prev, m_next = _shifted(m, jnp.where(has_prev, pv[:, cw:2 * cw] * pv[:, 2 * cw:3 * cw], 0.0),
                                      jnp.where(has_next, nx[:, cw:2 * cw] * nx[:, 2 * cw:3 * cw], 0.0), rc)
            s = w0 * m_prev + w1 * m + w2 * m_next
            sg = _sigmoid(cz)
            silu = cz * sg
            ds = dy * cb * silu
            ds_prev, ds_next = _shifted(ds, jnp.where(has_prev, ds_of(pv, dpv), 0.0),
                                        jnp.where(has_next, ds_of(nx, dnx), 0.0), rc)
            dm_ = w0 * ds_next + w1 * ds + w2 * ds_prev
            d_ref[pl.ds(r0, rc), 0:cw] = (dy * s * silu).astype(BF16)
            d_ref[pl.ds(r0, rc), cw:2 * cw] = (dm_ * cx).astype(BF16)
            d_ref[pl.ds(r0, rc), 2 * cw:3 * cw] = (dm_ * cc).astype(BF16)
            d_ref[pl.ds(r0, rc), 3 * cw:4 * cw] = (dy * cb * s * (sg * (1.0 + cz * (1.0 - sg)))).astype(BF16)
            return (g0 + jnp.sum(ds * m_prev, axis=0, keepdims=True), g1 + jnp.sum(ds * m, axis=0, keepdims=True),
                    g2 + jnp.sum(ds * m_next, axis=0, keepdims=True))

        z = jnp.zeros((1, cw), F32)
        g0, g1, g2 = lax.fori_loop(0, nchunk, chunk, (z, z, z))

        @pl.when(pl.program_id(1) == 0)
        def _():
            gw_ref[...] = jnp.zeros_like(gw_ref)

        gw_ref[0:1, :] += g0
        gw_ref[1:2, :] += g1
        gw_ref[2:3, :] += g2

    return pl.pallas_call(
        body, name="conv_bwd", grid=(dm.NJ, dm.Bl),
        in_specs=[pl.BlockSpec((lp, 4 * cw), lambda j, s: (s, j)), pl.BlockSpec((lp, cw), lambda j, s: (s, j)),
                  pl.BlockSpec((3, cw), lambda j, s: (0, j))],
        out_specs=[pl.BlockSpec((lp, 4 * cw), lambda j, s: (s, j)), pl.BlockSpec((8, cw), lambda j, s: (0, j))],
        out_shape=[jax.ShapeDtypeStruct((dm.T, 4 * dm.D), BF16), jax.ShapeDtypeStruct((8, dm.D), F32)],
        compiler_params=_cp(2),
    )(proj_a, dy_conv, conv_w)


def _interleave(gens):
    results = [None] * len(gens)
    live = list(range(len(gens)))
    while live:
        for idx in list(live):
            try:
                next(gens[idx])
            except StopIteration as done:
                results[idx] = done.value
                live.remove(idx)
    return results


def _group_chunks(dm):
    n = dm.NC - dm.C0
    return 3 if n % 3 == 0 else 1


def _group_masks(rows):
    ii = lax.broadcasted_iota(jnp.int32, (rows, rows), 0)
    jj = lax.broadcasted_iota(jnp.int32, (rows, rows), 1)
    same = jnp.right_shift(ii, CHUNK_SHIFT) == jnp.right_shift(jj, CHUNK_SHIFT)
    low, up = same & (jj <= ii), same & (jj >= ii)
    return low, same & (jj > ii), low.astype(BF16), up.astype(BF16)


def _chunk_totals(b, fwd):
    hk = b.shape[1]
    rows = [b[c * CHUNK + CHUNK - 1:(c + 1) * CHUNK] if fwd else b[c * CHUNK:c * CHUNK + 1]
            for c in range(b.shape[0] // CHUNK)]
    return jnp.concatenate([jnp.broadcast_to(r, (CHUNK, hk)) for r in rows], axis=0)


def _log_gate(lr_rows, w_ref, b_ref, first_group, hk):
    z = _dot(lr_rows, w_ref[...]) + b_ref[...]
    e = jnp.exp(-jnp.abs(z))
    g = (jnp.minimum(z, 0.0) - jnp.log(1.0 + e)) * (1.0 / GATE_NORMALIZER)
    dg_dz = jnp.where(z >= 0.0, e, 1.0) / (1.0 + e) * (1.0 / GATE_NORMALIZER)
    row = lax.broadcasted_iota(jnp.int32, (lr_rows.shape[0], hk), 0)
    pad = first_group & (row < PAD_ROWS)
    return jnp.where(pad, 0.0, g), jnp.where(pad, 0.0, dg_dz)


def _gla_fwd(proj_b, lr, wg_f, bg_f, wg_b, bg_b, gla_g, dm):
    lp, hk, hv, nc, c0, hw = dm.LP, dm.HK, dm.HV, dm.NC, dm.C0, dm.HW
    scale = hk ** -0.5
    gc = _group_chunks(dm)
    gr, ng = gc * CHUNK, (nc - c0) // gc

    def body(p_ref, lr_ref, wf_ref, bf_ref, wb_ref, bb_ref, gg_ref, o_ref, y_ref, oacc_f, oacc_b):
        low_incl, up_strict, ones_low, ones_up = _group_masks(gr)
        if c0 > 0:
            o_ref[0:c0 * CHUNK, :] = jnp.zeros((c0 * CHUNK, hv), BF16)
            y_ref[0:c0 * CHUNK, :] = jnp.zeros((c0 * CHUNK, hv), BF16)

        def group(gi, st, fwd):
            w_ref, b_ref, oacc = (wf_ref, bf_ref, oacc_f) if fwd else (wb_ref, bb_ref, oacc_b)
            r0 = pl.multiple_of((c0 + gi * gc) * CHUNK, CHUNK)
            blk = p_ref[pl.ds(r0, gr), :]
            q = blk[:, :hk].astype(F32) * scale
            k = blk[:, hk:2 * hk].astype(F32)
            v = blk[:, 2 * hk:2 * hk + hv]
            yield
            g, _ = _log_gate(lr_ref[pl.ds(r0, gr), :], w_ref, b_ref, gi == 0, hk)
            yield
            b = _dot_exact01(ones_low if fwd else ones_up, g)
            yield
            btot = _chunk_totals(b, fwd)
            qi = (q * jnp.exp(b)).astype(BF16)
            ki = (k * jnp.exp(-b)).astype(BF16)
            kd = (k * jnp.exp(btot - b)).astype(BF16)
            dec = jnp.exp(btot)
            a = _dot_nt(qi, ki)
            yield
            o = _dot(jnp.where(low_incl if fwd else up_strict, a, 0.0).astype(BF16), v)
            for c in (range(gc) if fwd else reversed(range(gc))):
                yield
                rows = slice(c * CHUNK, (c + 1) * CHUNK)
                oacc[pl.ds(r0 + c * CHUNK, CHUNK), :] = o[rows] + _dot_nt(qi[rows], st.astype(BF16))
                st = st * dec[c * CHUNK:c * CHUNK + 1] + _dot_tn(v[rows], kd[rows])
            return st

        def step(i, carry):
            st_f, st_b = carry
            return tuple(_interleave([group(i, st_f, True), group(ng - 1 - i, st_b, False)]))

        zero = jnp.zeros((hv, hk), F32)
        lax.fori_loop(0, ng, step, (zero, zero))

        def finish(i, carry):
            r0 = pl.multiple_of((c0 + i * gc) * CHUNK, CHUNK)
            o = oacc_f[pl.ds(r0, gr), :] + oacc_b[pl.ds(r0, gr), :]
            r = p_ref[pl.ds(r0, gr), 2 * hk + hv:].astype(F32)
            on = o * lax.rsqrt(jnp.mean(o * o, axis=-1, keepdims=True) + EPS) * gg_ref[...]
            o_ref[pl.ds(r0, gr), :] = o.astype(BF16)
            y_ref[pl.ds(r0, gr), :] = (on * r * _sigmoid(r)).astype(BF16)
            return carry

        lax.fori_loop(0, ng, finish, 0)

    head = lambda s, h: (s, h)
    wspec = pl.BlockSpec((LR_LANES, hk), lambda s, h: (0, h))
    bspec = pl.BlockSpec((1, hk), lambda s, h: (0, h))
    return pl.pallas_call(
        body, name="gla_fwd", grid=(dm.Bl, HEADS),
        in_specs=[pl.BlockSpec((lp, hw), head), pl.BlockSpec((lp, LR_LANES), lambda s, h: (s, 0)),
                  wspec, bspec, wspec, bspec, pl.BlockSpec((1, hv), lambda s, h: (0, 0))],
        out_specs=[pl.BlockSpec((lp, hv), head), pl.BlockSpec((lp, hv), head)],
        out_shape=[jax.ShapeDtypeStruct((dm.T, dm.DV), BF16), jax.ShapeDtypeStruct((dm.T, dm.DV), BF16)],
        scratch_shapes=[pltpu.VMEM((lp, hv), F32), pltpu.VMEM((lp, hv), F32)],
        compiler_params=_cp(2),
    )(proj_b, lr, wg_f, bg_f, wg_b, bg_b, gla_g)


def _gla_bwd(proj_b, lr, o_all, dy_gla, wg_f, bg_f, wg_b, bg_b, gla_g, dm):
    lp, hk, hv, nc, c0, hw = dm.LP, dm.HK, dm.HV, dm.NC, dm.C0, dm.HW
    scale = hk ** -0.5
    gc = _group_chunks(dm)
    gr, ng = gc * CHUNK, (nc - c0) // gc

    def body(p_ref, lr_ref, o_ref, dy_ref, wf_ref, bf_ref, wb_ref, bb_ref, gg_ref,
             d_ref, dlr_ref, gwf_ref, gbf_ref, gwb_ref, gbb_ref, ggg_ref,
             do_s, sf_all, sb_all, bf_s, bb_s, gsf_s, gsb_s, dqf_s, dqb_s, dkf_s, dkb_s, dvf_s, dvb_s, dlrf_s, dlrb_s):
        low_incl, up_strict, ones_low, ones_up = _group_masks(gr)
        h = pl.program_id(1)

        @pl.when(h == 0)
        def _():
            dlr_ref[...] = jnp.zeros_like(dlr_ref)

        if c0 > 0:
            zr = c0 * CHUNK
            d_ref[0:zr, :] = jnp.zeros((zr, hw), BF16)

        def norm_bwd(i, ggg):
            r0 = pl.multiple_of((c0 + i * gc) * CHUNK, CHUNK)
            o = o_ref[pl.ds(r0, gr), :].astype(F32)
            dy = dy_ref[pl.ds(r0, gr), :].astype(F32)
            r = p_ref[pl.ds(r0, gr), 2 * hk + hv:].astype(F32)
            rstd = lax.rsqrt(jnp.mean(o * o, axis=-1, keepdims=True) + EPS)
            ohat = o * rstd
            sg = _sigmoid(r)
            d_on = dy * (r * sg)
            d_ref[pl.ds(r0, gr), 2 * hk + hv:] = (dy * ohat * gg_ref[...] * (sg * (1.0 + r * (1.0 - sg)))).astype(BF16)
            d_oh = d_on * gg_ref[...]
            do_s[pl.ds(r0, gr), :] = (rstd * (d_oh - ohat * jnp.mean(d_oh * ohat, axis=-1, keepdims=True))).astype(BF16)
            return ggg + jnp.sum(d_on * ohat, axis=0, keepdims=True)

        ggg = lax.fori_loop(0, ng, norm_bwd, jnp.zeros((1, hv), F32))

        @pl.when((pl.program_id(0) == 0) & (h == 0))
        def _():
            ggg_ref[...] = jnp.zeros_like(ggg_ref)

        ggg_ref[0:1, :] += ggg

        def load(gi):
            r0 = pl.multiple_of((c0 + gi * gc) * CHUNK, CHUNK)
            blk = p_ref[pl.ds(r0, gr), :]
            return r0, blk[:, :hk].astype(F32) * scale, blk[:, hk:2 * hk].astype(F32), blk[:, 2 * hk:2 * hk + hv]

        chunk_totals = _chunk_totals

        def record(gi, st, fwd):
            w_ref, b_ref, s_all, b_s, gs_s = (wf_ref, bf_ref, sf_all, bf_s, gsf_s) if fwd else (wb_ref, bb_ref, sb_all, bb_s, gsb_s)
            r0, q, k, v = load(gi)
            yield
            g, dg_dz = _log_gate(lr_ref[pl.ds(r0, gr), :], w_ref, b_ref, gi == 0, hk)
            gs_s[pl.ds(r0, gr), :] = dg_dz
            yield
            b = _dot_exact01(ones_low if fwd else ones_up, g)
            yield
            b_s[pl.ds(r0, gr), :] = b
            btot = chunk_totals(b, fwd)
            kd = (k * jnp.exp(btot - b)).astype(BF16)
            dec = jnp.exp(btot)
            for c in (range(gc) if fwd else reversed(range(gc))):
                yield
                rows = slice(c * CHUNK, (c + 1) * CHUNK)
                s_all[c0 + gi * gc + c] = st
                st = st * dec[c * CHUNK:c * CHUNK + 1] + _dot_tn(v[rows], kd[rows])
            return st

        def record_step(i, carry):
            return tuple(_interleave([record(i, carry[0], True), record(ng - 1 - i, carry[1], False)]))

        zero = jnp.zeros((hv, hk), F32)
        lax.fori_loop(0, ng, record_step, (zero, zero))

        def grad(gi, carry, fwd):
            dst, gw, gb = carry
            w_ref, s_all, b_s, gs_s = (wf_ref, sf_all, bf_s, gsf_s) if fwd else (wb_ref, sb_all, bb_s, gsb_s)
            dq_s, dk_s, dv_s, dlr_s = (dqf_s, dkf_s, dvf_s, dlrf_s) if fwd else (dqb_s, dkb_s, dvb_s, dlrb_s)
            mask = low_incl if fwd else up_strict
            r0, q, k, v = load(gi)
            b = b_s[pl.ds(r0, gr), :]
            btot = chunk_totals(b, fwd)
            eb, enb, edb, dec = jnp.exp(b), jnp.exp(-b), jnp.exp(btot - b), jnp.exp(btot)
            qi_f, ki_f, kd_f = q * eb, k * enb, k * edb
            qi, ki, kd = qi_f.astype(BF16), ki_f.astype(BF16), kd_f.astype(BF16)
            do = do_s[pl.ds(r0, gr), :]
            a = _dot_nt(qi, ki)
            da = _dot_nt(do, v)
            yield
            a = jnp.where(mask, a, 0.0).astype(BF16)
            da = jnp.where(mask, da, 0.0).astype(BF16)
            dv = _dot_tn(a, do)
            dqi = _dot(da, ki)
            dki = _dot_tn(da, qi)
            dv_c, dqi_c, dkd_c, extra_c = [None] * gc, [None] * gc, [None] * gc, [None] * gc
            for c in (reversed(range(gc)) if fwd else range(gc)):
                yield
                rows = slice(c * CHUNK, (c + 1) * CHUNK)
                st = s_all[c0 + gi * gc + c]
                dsn_b = dst.astype(BF16)
                dec_c = dec[c * CHUNK:c * CHUNK + 1]
                dv_c[c] = dv[rows] + _dot_nt(kd[rows], dsn_b)
                dqi_c[c] = dqi[rows] + _dot(do[rows], st.astype(BF16))
                dkd_c[c] = _dot(v[rows], dsn_b)
                ddec = jnp.sum(st * dst, axis=0, keepdims=True)
                extra = jnp.sum(dkd_c[c] * kd_f[rows], axis=0, keepdims=True) + ddec * dec_c
                extra_c[c] = jnp.broadcast_to(extra, (CHUNK, hk))
                dst = dst * dec_c + _dot_tn(do[rows], qi[rows])
            yield
            dv, dqi = jnp.concatenate(dv_c, axis=0), jnp.concatenate(dqi_c, axis=0)
            dkd, extra = jnp.concatenate(dkd_c, axis=0), jnp.concatenate(extra_c, axis=0)
            dq_s[pl.ds(r0, gr), :] = dqi * eb * scale
            dk_s[pl.ds(r0, gr), :] = dki * enb + dkd * edb
            dv_s[pl.ds(r0, gr), :] = dv
            db = dqi * qi_f - dki * ki_f - dkd * kd_f
            dg = _dot_exact01(ones_up if fwd else ones_low, db) + extra
            yield
            dz = dg * gs_s[pl.ds(r0, gr), :]
            dz_b = dz.astype(BF16)
            dlr_s[pl.ds(r0, gr), :] = _dot_nt(dz_b, w_ref[...])
            return dst, gw + _dot_tn(lr_ref[pl.ds(r0, gr), :], dz_b), gb + jnp.sum(dz, axis=0, keepdims=True)

        def grad_step(i, carry):
            return tuple(_interleave([grad(ng - 1 - i, carry[0], True), grad(i, carry[1], False)]))

        init = (zero, jnp.zeros((LR_LANES, hk), F32), jnp.zeros((1, hk), F32))
        (_, gw_f, gb_f), (_, gw_b, gb_b) = lax.fori_loop(0, ng, grad_step, (init, init))
        for gw_ref, gb_ref, gw, gb in ((gwf_ref, gbf_ref, gw_f, gb_f), (gwb_ref, gbb_ref, gw_b, gb_b)):
            gw_ref[0] = gw
            gb_ref[0] = jnp.zeros((8, hk), F32)
            gb_ref[0, 0:1, :] = gb

        def combine(i, carry):
            r0 = pl.multiple_of((c0 + i * gc) * CHUNK, CHUNK)
            d_ref[pl.ds(r0, gr), 0:hk] = (dqf_s[pl.ds(r0, gr), :] + dqb_s[pl.ds(r0, gr), :]).astype(BF16)
            d_ref[pl.ds(r0, gr), hk:2 * hk] = (dkf_s[pl.ds(r0, gr), :] + dkb_s[pl.ds(r0, gr), :]).astype(BF16)
            d_ref[pl.ds(r0, gr), 2 * hk:2 * hk + hv] = (dvf_s[pl.ds(r0, gr), :] + dvb_s[pl.ds(r0, gr), :]).astype(BF16)
            dlr_ref[pl.ds(r0, gr), :] += dlrf_s[pl.ds(r0, gr), :] + dlrb_s[pl.ds(r0, gr), :]
            return carry

        lax.fori_loop(0, ng, combine, 0)

    head = lambda s, h: (s, h)
    wspec = pl.BlockSpec((LR_LANES, hk), lambda s, h: (0, h))
    bspec = pl.BlockSpec((1, hk), lambda s, h: (0, h))
    gwspec = pl.BlockSpec((1, LR_LANES, hk), lambda s, h: (s, 0, h))
    gbspec = pl.BlockSpec((1, 8, hk), lambda s, h: (s, 0, h))
    gw_shape = jax.ShapeDtypeStruct((dm.Bl, LR_LANES, dm.DK), F32)
    gb_shape = jax.ShapeDtypeStruct((dm.Bl, 8, dm.DK), F32)
    return pl.pallas_call(
        body, name="gla_bwd", grid=(dm.Bl, HEADS),
        in_specs=[pl.BlockSpec((lp, hw), head), pl.BlockSpec((lp, LR_LANES), lambda s, h: (s, 0)),
                  pl.BlockSpec((lp, hv), head), pl.BlockSpec((lp, hv), head),
                  wspec, bspec, wspec, bspec, pl.BlockSpec((1, hv), lambda s, h: (0, 0))],
        out_specs=[pl.BlockSpec((lp, hw), head), pl.BlockSpec((lp, LR_LANES), lambda s, h: (s, 0)),
                   gwspec, gbspec, gwspec, gbspec, pl.BlockSpec((8, hv), lambda s, h: (0, 0))],
        out_shape=[jax.ShapeDtypeStruct((dm.T, HEADS * hw), BF16), jax.ShapeDtypeStruct((dm.T, LR_LANES), F32),
                   gw_shape, gb_shape, gw_shape, gb_shape, jax.ShapeDtypeStruct((8, hv), F32)],
        scratch_shapes=[pltpu.VMEM((lp, hv), BF16), pltpu.VMEM((nc, hv, hk), F32), pltpu.VMEM((nc, hv, hk), F32)]
        + [pltpu.VMEM((lp, hk), F32)] * 8 + [pltpu.VMEM((lp, hv), F32)] * 2 + [pltpu.VMEM((lp, LR_LANES), F32)] * 2,
        compiler_params=_cp(2),
    )(proj_b, lr, o_all, dy_gla, wg_f, bg_f, wg_b, bg_b, gla_g)


def _out_merge(y_conv, y_gla, proj_c, w_oc, w_og, dm):
    d = dm.D
    tm = _pick(dm.T, 512, 16)

    def body(yc_ref, yg_ref, c_ref, woc_ref, wog_ref, pc_ref, pg_ref, m_ref):
        pc = _dot(yc_ref[...], woc_ref[...])
        pg = _dot(yg_ref[...], wog_ref[...])
        pc_ref[...] = pc.astype(BF16)
        pg_ref[...] = pg.astype(BF16)
        ma = c_ref[:, :d].astype(F32)
        mb = c_ref[:, d:].astype(F32)
        m_ref[...] = (_sigmoid(ma) * pc + _sigmoid(mb) * pg).astype(BF16)

    row = pl.BlockSpec((tm, d), lambda i: (i, 0))
    full = pl.BlockSpec((d, d), lambda i: (0, 0))
    act = jax.ShapeDtypeStruct((dm.T, d), BF16)
    return pl.pallas_call(
        body, name="out_merge", grid=(dm.T // tm,),
        in_specs=[row, row, pl.BlockSpec((tm, 2 * d), lambda i: (i, 0)), full, full],
        out_specs=[row, row, row], out_shape=[act, act, act], compiler_params=_cp(1),
    )(y_conv, y_gla, proj_c, w_oc, w_og)


def _final_fwd(merged, w_out, x, metapad, target, g_post, dm):
    tm, tps, d = dm.TM, dm.TPS, dm.D

    def body(m_ref, w_ref, x_ref, mp_ref, t_ref, g_ref, dout_ref, dy_ref, st_ref):
        i = pl.program_id(0)
        j = i % tps
        out = _dot(m_ref[...], w_ref[...])
        rstd = lax.rsqrt(jnp.mean(out * out, axis=-1, keepdims=True) + EPS)
        ohat = out * rstd
        h = jnp.where(j == 0, mp_ref[...], x_ref[0])
        y = h + ohat * g_ref[...]
        err = jnp.where(j == 0, 0.0, y - t_ref[0])
        dy = err * (1.0 / d)
        d_oh = dy * g_ref[...]
        dout_ref[...] = (rstd * (d_oh - ohat * jnp.mean(d_oh * ohat, axis=-1, keepdims=True))).astype(BF16)
        dy_ref[...] = dy

        @pl.when(i == 0)
        def _():
            st_ref[...] = jnp.zeros_like(st_ref)

        st_ref[0:1, :] += jnp.sum(dy * ohat, axis=0, keepdims=True)
        st_ref[1:2, :] += jnp.sum(err * err, axis=0, keepdims=True)

    row = pl.BlockSpec((tm, d), lambda i: (i, 0))
    tok = pl.BlockSpec((1, tm, d), lambda i: (i // tps, jnp.maximum(i % tps - 1, 0), 0))
    const = lambda r: pl.BlockSpec((r, d), lambda i: (0, 0))
    return pl.pallas_call(
        body, name="final_fwd", grid=(dm.Bl * tps,),
        in_specs=[row, const(d), tok, const(tm), tok, const(1)],
        out_specs=[row, row, const(8)],
        out_shape=[jax.ShapeDtypeStruct((dm.T, d), BF16), jax.ShapeDtypeStruct((dm.T, d), F32),
                   jax.ShapeDtypeStruct((8, d), F32)],
        compiler_params=_cp(1),
    )(merged, w_out, x, metapad, target, g_post)


def _merge_bwd(d_out, proj_c, p_conv, p_gla, w_out, w_oc, w_og, dm):
    d = dm.D
    tm = _pick(dm.T, 512, 16)

    def body(do_ref, c_ref, pc_ref, pg_ref, wo_ref, woc_ref, wog_ref, dpc_ref, dpg_ref, dc_ref, dyc_ref, dyg_ref):
        dmg = _dot_nt(do_ref[...], wo_ref[...])
        sa = _sigmoid(c_ref[:, :d].astype(F32))
        sb = _sigmoid(c_ref[:, d:].astype(F32))
        dpc = (dmg * sa).astype(BF16)
        dpg = (dmg * sb).astype(BF16)
        dpc_ref[...] = dpc
        dpg_ref[...] = dpg
        dc_ref[:, :d] = (dmg * pc_ref[...].astype(F32) * sa * (1.0 - sa)).astype(BF16)
        dc_ref[:, d:] = (dmg * pg_ref[...].astype(F32) * sb * (1.0 - sb)).astype(BF16)
        dyc_ref[...] = _dot_nt(dpc, woc_ref[...]).astype(BF16)
        dyg_ref[...] = _dot_nt(dpg, wog_ref[...]).astype(BF16)

    row = pl.BlockSpec((tm, d), lambda i: (i, 0))
    row2 = pl.BlockSpec((tm, 2 * d), lambda i: (i, 0))
    full = pl.BlockSpec((d, d), lambda i: (0, 0))
    act = jax.ShapeDtypeStruct((dm.T, d), BF16)
    return pl.pallas_call(
        body, name="merge_bwd", grid=(dm.T // tm,),
        in_specs=[row, row2, row, row, full, full, full],
        out_specs=[row, row, row2, row, row],
        out_shape=[act, act, jax.ShapeDtypeStruct((dm.T, 2 * d), BF16), act, act],
        compiler_params=_cp(1),
    )(d_out, proj_c, p_conv, p_gla, w_out, w_oc, w_og)


def _prenorm_bwd(du, dy, x, metapad, g_pre, dm):
    tm, tps, d = dm.TM, dm.TPS, dm.D

    def body(du_ref, dy_ref, x_ref, mp_ref, g_ref, gx_ref, dmeta_ref, gg_ref):
        i = pl.program_id(0)
        j = i % tps
        h = jnp.where(j == 0, mp_ref[...], x_ref[0])
        rstd = lax.rsqrt(jnp.mean(h * h, axis=-1, keepdims=True) + EPS)
        hhat = h * rstd
        dug = du_ref[...] * g_ref[...]
        dh = dy_ref[...] + rstd * (dug - hhat * jnp.mean(dug * hhat, axis=-1, keepdims=True))

        @pl.when(j == 0)
        def _():
            dmeta_ref[0] = dh

        @pl.when(j > 0)
        def _():
            gx_ref[0] = dh

        @pl.when(i == 0)
        def _():
            gg_ref[...] = jnp.zeros_like(gg_ref)

        gg_ref[0:1, :] += jnp.sum(du_ref[...] * hhat, axis=0, keepdims=True)

    row = pl.BlockSpec((tm, d), lambda i: (i, 0))
    tok = pl.BlockSpec((1, tm, d), lambda i: (i // tps, jnp.maximum(i % tps - 1, 0), 0))
    const = lambda r: pl.BlockSpec((r, d), lambda i: (0, 0))
    return pl.pallas_call(
        body, name="prenorm_bwd", grid=(dm.Bl * tps,),
        in_specs=[row, row, tok, const(tm), const(1)],
        out_specs=[tok, pl.BlockSpec((1, tm, d), lambda i: (i // tps, 0, 0)), const(8)],
        out_shape=[jax.ShapeDtypeStruct((dm.Bl, dm.S, d), F32), jax.ShapeDtypeStruct((dm.Bl, tm, d), F32),
                   jax.ShapeDtypeStruct((8, d), F32)],
        compiler_params=_cp(1),
    )(du, dy, x, metapad, g_pre)


def _sum_partials(partials, name):
    p, r, c = partials.shape
    tc = _pick(c, 256, 128)

    def body(p_ref, o_ref):
        g = p_ref[0].astype(F32)
        for j in range(1, p):
            g = g + p_ref[j].astype(F32)
        o_ref[...] = g

    return pl.pallas_call(
        body, name=name, grid=(c // tc,), in_specs=[pl.BlockSpec((p, r, tc), lambda i: (0, 0, i))],
        out_specs=pl.BlockSpec((r, tc), lambda i: (0, i)), out_shape=jax.ShapeDtypeStruct((r, c), F32),
        compiler_params=_cp(1),
    )(partials)


def _adamw(partials, w, m, v, name):
    r, c = w.shape
    n_parts = partials.shape[0]
    tr = _pick(r, 256, 16)

    def body(p_ref, w_ref, m_ref, v_ref, g_ref, d_ref, nm_ref, nv_ref):
        g = p_ref[0].astype(F32)
        for j in range(1, n_parts):
            g = g + p_ref[j].astype(F32)
        m2 = ADAM_B1 * m_ref[...] + (1.0 - ADAM_B1) * g
        v2 = ADAM_B2 * v_ref[...] + (1.0 - ADAM_B2) * (g * g)
        m_hat = m2 / (1.0 - ADAM_B1 ** ADAM_STEP)
        v_hat = v2 / (1.0 - ADAM_B2 ** ADAM_STEP)
        g_ref[...] = g
        d_ref[...] = -ADAM_LR * (m_hat / (jnp.sqrt(v_hat) + ADAM_EPS) + ADAM_WD * w_ref[...])
        nm_ref[...] = m2
        nv_ref[...] = v2

    row = pl.BlockSpec((tr, c), lambda i: (i, 0))
    out = jax.ShapeDtypeStruct((r, c), F32)
    return pl.pallas_call(
        body, name=name, grid=(r // tr,),
        in_specs=[pl.BlockSpec((n_parts, tr, c), lambda i: (0, i, 0)), row, row, row],
        out_specs=[row, row, row, row], out_shape=[out, out, out, out], compiler_params=_cp(1),
    )(partials, w, m, v)


def _pack_rows(wt, dm):
    d, dk, hk, hv, cw, nj = dm.D, dm.DK, dm.HK, dm.HV, dm.CW, dm.NJ
    a = wt[:4 * d].reshape(4, nj, cw, d).transpose(1, 0, 2, 3).reshape(4 * d, d)
    q, k, v, r = wt[4 * d:4 * d + dk], wt[4 * d + dk:5 * d], wt[5 * d:6 * d], wt[6 * d:7 * d]
    b = jnp.concatenate([t for h in range(HEADS) for t in (q[h * hk:(h + 1) * hk], k[h * hk:(h + 1) * hk],
                                                           v[h * hv:(h + 1) * hv], r[h * hv:(h + 1) * hv])], axis=0)
    c = wt[7 * d + 2 * RANK:]
    lr = jnp.pad(wt[7 * d:7 * d + 2 * RANK], ((0, LR_LANES - 2 * RANK), (0, 0)))
    return a, b, c, lr


def _unpack_rows(a, b, c, lr, dm):
    d, hk, hv, cw, nj, hw = dm.D, dm.HK, dm.HV, dm.CW, dm.NJ, dm.HW
    conv = a.reshape(nj, 4, cw, d).transpose(1, 0, 2, 3).reshape(4 * d, d)
    heads = [b[h * hw:(h + 1) * hw] for h in range(HEADS)]
    q = jnp.concatenate([t[:hk] for t in heads], axis=0)
    k = jnp.concatenate([t[hk:2 * hk] for t in heads], axis=0)
    v = jnp.concatenate([t[2 * hk:2 * hk + hv] for t in heads], axis=0)
    r = jnp.concatenate([t[2 * hk + hv:] for t in heads], axis=0)
    return jnp.concatenate([conv, q, k, v, r, lr[:2 * RANK], c], axis=0)


def _to_blob(pieces, dtype, row_mult):
    lead = pieces[0].shape[0]
    flat = jnp.concatenate([p.reshape(lead, -1).astype(dtype) for p in pieces], axis=1)
    unit = row_mult * BLOB_LANES
    padded = -(-flat.shape[1] // unit) * unit
    flat = jnp.pad(flat, ((0, 0), (0, padded - flat.shape[1])))
    return flat.reshape(lead, padded // BLOB_LANES, BLOB_LANES)


def _from_blob(blob, shapes):
    lead = blob.shape[0]
    flat = blob.reshape(lead, -1)
    out, off = [], 0
    for shp in shapes:
        size = int(np.prod(shp))
        out.append(flat[:, off:off + size].reshape((lead,) + tuple(shp)))
        off += size
    return out


def _local_step(x, target, meta, g_pre, wt_in, conv_w, wg_f, bg_f, wg_b, bg_b, gla_g, out_weights, g_post,
                on_matrix_grads=None):
    bl, s, d = x.shape
    dm = _Dims(bl, s, d)
    metapad = jnp.concatenate([jnp.zeros((dm.TM - N_META, d), F32), meta], axis=0)
    wta, wtb, wtc, wtlr = _pack_rows(wt_in, dm)
    wgp_f = jnp.pad(wg_f, ((0, LR_LANES - RANK), (0, 0))).astype(BF16)
    wgp_b = jnp.pad(wg_b, ((RANK, LR_LANES - 2 * RANK), (0, 0))).astype(BF16)

    u = _prenorm(x, metapad, g_pre, dm)
    proj_a = _matmul_nt(u, wta, BF16, "inproj_conv")
    proj_b = _matmul_nt(u, wtb, BF16, "inproj_gla")
    proj_c = _matmul_nt(u, wtc, BF16, "inproj_merge")
    lr = _matmul_nt(u, wtlr, BF16, "inproj_gate")
    y_conv = _conv_fwd(proj_a, conv_w, dm)
    o_all, y_gla = _gla_fwd(proj_b, lr, wgp_f, bg_f, wgp_b, bg_b, gla_g, dm)
    w_oc, w_og, w_out = out_weights(y_conv) if callable(out_weights) else out_weights
    p_conv, p_gla, merged = _out_merge(y_conv, y_gla, proj_c, w_oc, w_og, dm)
    d_out, dy, stats = _final_fwd(merged, w_out, x, metapad, target, g_post, dm)
    loss = 0.5 / d * jnp.sum(stats[1])

    d_pc, d_pg, d_c, dy_conv, dy_gla = _merge_bwd(d_out, proj_c, p_conv, p_gla, w_out, w_oc, w_og, dm)
    g_out = _matmul_tn(merged, d_out, "grad_w_out")
    g_oc = _matmul_tn(y_conv, d_pc, "grad_w_out_conv")
    g_og = _matmul_tn(y_gla, d_pg, "grad_w_out_gla")
    d_a, g_conv = _conv_bwd(proj_a, dy_conv, conv_w, dm)
    d_b, d_lr, gwp_f, gbp_f, gwp_b, gbp_b, g_gla = _gla_bwd(proj_b, lr, o_all, dy_gla, wgp_f, bg_f, wgp_b, bg_b, gla_g, dm)
    g_in = _unpack_rows(_matmul_tn(d_a, u, "grad_w_in_conv"), _matmul_tn(d_b, u, "grad_w_in_gla"),
                        _matmul_tn(d_c, u, "grad_w_in_merge"), _matmul_tn(d_lr, u, "grad_w_in_gate"), dm)
    if on_matrix_grads is not None:
        wtlr = wtlr + on_matrix_grads(dict(w_in=g_in, w_out_conv=g_oc, w_out_gla=g_og, w_merge_out=g_out)).astype(BF16)
    du = _matmul_parts([(d_a, wta), (d_b, wtb), (d_c, wtc), (d_lr, wtlr)], "grad_u")
    grad_x, d_meta, g_pre_rows = _prenorm_bwd(du, dy, x, metapad, g_pre, dm)

    grads = dict(
        meta_tokens=jnp.sum(d_meta[:, dm.TM - N_META:, :], axis=0), norm_pre=g_pre_rows[0:1], w_in=g_in,
        conv_w=g_conv[0:3], w_gate_fwd=jnp.sum(gwp_f, axis=0)[:RANK], b_gate_fwd=jnp.sum(gbp_f, axis=0)[0:1],
        w_gate_bwd=jnp.sum(gwp_b, axis=0)[RANK:2 * RANK], b_gate_bwd=jnp.sum(gbp_b, axis=0)[0:1],
        gla_norm=g_gla[0:1], w_out_conv=g_oc, w_out_gla=g_og, w_merge_out=g_out, norm_post=stats[0:1])
    return loss, grad_x, grads


MATRICES = ("w_out_conv", "w_out_gla", "w_merge_out")
SMALL_SHARDED = ("meta_tokens", "conv_w", "w_gate_fwd", "w_gate_bwd")
REPLICATED = ("norm_pre", "b_gate_fwd", "b_gate_bwd", "gla_norm", "norm_post")
NAMES = ("meta_tokens", "norm_pre", "w_in", "conv_w", "w_gate_fwd", "b_gate_fwd", "w_gate_bwd", "b_gate_bwd", "gla_norm",
         "w_out_conv", "w_out_gla", "w_merge_out", "norm_post")
DEPTH_AXIS = ("w_in", "conv_w", "w_gate_fwd", "w_gate_bwd") + MATRICES


def _cols_to_devices(g):
    r, c = g.shape
    return g.reshape(r, N_DEV, c // N_DEV).transpose(1, 0, 2)


def _cols_from_devices(parts):
    n, r, c = parts.shape
    return parts.transpose(1, 0, 2).reshape(r, n * c)


def kernel(x, meta_tokens, norm_pre, w_in, conv_w, w_gate_fwd, b_gate_fwd, w_gate_bwd, b_gate_bwd, gla_norm, w_out_conv, w_out_gla, w_merge_out, norm_post, loss_target, m_meta_tokens, m_norm_pre, m_w_in, m_conv_w, m_w_gate_fwd, m_b_gate_fwd, m_w_gate_bwd, m_b_gate_bwd, m_gla_norm, m_w_out_conv, m_w_out_gla, m_w_merge_out, m_norm_post, v_meta_tokens, v_norm_pre, v_w_in, v_conv_w, v_w_gate_fwd, v_b_gate_fwd, v_w_gate_bwd, v_b_gate_bwd, v_gla_norm, v_w_out_conv, v_w_out_gla, v_w_merge_out, v_norm_post):
    w = dict(meta_tokens=meta_tokens, norm_pre=norm_pre, w_in=w_in[0], conv_w=conv_w[0], w_gate_fwd=w_gate_fwd[0],
             b_gate_fwd=b_gate_fwd, w_gate_bwd=w_gate_bwd[0], b_gate_bwd=b_gate_bwd, gla_norm=gla_norm,
             w_out_conv=w_out_conv[0], w_out_gla=w_out_gla[0], w_merge_out=w_merge_out[0], norm_post=norm_post)
    m = dict(meta_tokens=m_meta_tokens, norm_pre=m_norm_pre, w_in=m_w_in[0], conv_w=m_conv_w[0], w_gate_fwd=m_w_gate_fwd[0],
             b_gate_fwd=m_b_gate_fwd, w_gate_bwd=m_w_gate_bwd[0], b_gate_bwd=m_b_gate_bwd, gla_norm=m_gla_norm,
             w_out_conv=m_w_out_conv[0], w_out_gla=m_w_out_gla[0], w_merge_out=m_w_merge_out[0], norm_post=m_norm_post)
    v = dict(meta_tokens=v_meta_tokens, norm_pre=v_norm_pre, w_in=v_w_in[0], conv_w=v_conv_w[0], w_gate_fwd=v_w_gate_fwd[0],
             b_gate_fwd=v_b_gate_fwd, w_gate_bwd=v_w_gate_bwd[0], b_gate_bwd=v_b_gate_bwd, gla_norm=v_gla_norm,
             w_out_conv=v_w_out_conv[0], w_out_gla=v_w_out_gla[0], w_merge_out=v_w_merge_out[0], norm_post=v_norm_post)
    d = x.shape[-1]

    small_blob = _to_blob([w[n][None] for n in SMALL_SHARDED], F32, SMALL_ROWS)[0]
    wt_all, small_all = _gather_two_level([w["w_in"].T.astype(BF16), small_blob], "gather_weights")
    _, late_weights = _exchange_start([w[n].astype(BF16) for n in MATRICES], [], small_all, "gather_out_weights_start")
    wt_in = wt_all.reshape(-1, d)
    small = {n: _cols_from_devices(p) for n, p in zip(SMALL_SHARDED, _from_blob(small_all, [w[n].shape for n in SMALL_SHARDED]))}

    def out_weights(after):
        return tuple(a.reshape(-1, d) for a in _exchange_wait(late_weights, after, "gather_out_weights_wait"))

    pending = []

    def on_matrix_grads(g):
        to_send = [g[n].astype(BF16).reshape(N_DEV, -1, d) for n in ("w_in",) + MATRICES]
        token, state = _exchange_start([], to_send, g["w_out_conv"], "exchange_grads_start")
        pending.append(state)
        return token

    loss, grad_x, grads = _local_step(
        x, loss_target, small["meta_tokens"], norm_pre, wt_in, small["conv_w"], small["w_gate_fwd"], b_gate_fwd,
        small["w_gate_bwd"], b_gate_bwd, gla_norm, out_weights, norm_post, on_matrix_grads)
    loss = lax.psum(loss, ("x", "y", "c"))
    received = _exchange_wait(pending[0], grad_x, "exchange_grads_wait")

    small_send = _to_blob([_cols_to_devices(grads[n]) for n in SMALL_SHARDED], F32, SMALL_ROWS)
    repl_blob = _to_blob([grads[n][None] for n in REPLICATED], F32, SMALL_ROWS)[0]
    repl_all, small_recv = _exchange([repl_blob], [small_send], "exchange_small_grads")

    results = {"w_in": _adamw(_sum_partials(received[0], "sum_grad_w_in").T[None], w["w_in"], m["w_in"], v["w_in"], "adamw_w_in")}
    for n, partials in zip(MATRICES, received[1:4]):
        results[n] = _adamw(partials, w[n], m[n], v[n], "adamw_" + n)
    for names, partials, tag in ((SMALL_SHARDED, small_recv, "small"), (REPLICATED, repl_all, "replicated")):
        blobs = [_to_blob([t[n][None] for n in names], F32, SMALL_ROWS)[0] for t in (w, m, v)]
        res = [_from_blob(r[None], [w[n].shape for n in names]) for r in _adamw(partials, *blobs, name="adamw_" + tag)]
        for i, n in enumerate(names):
            results[n] = [r[i][0] for r in res]
    lead = lambda n, t: t[None] if n in DEPTH_AXIS else t
    return (loss, grad_x, *[lead(n, results[n][i]) for i in range(4) for n in NAMES])
```

```python
import functools

import jax
import jax.numpy as jnp
import numpy as np
from jax import lax
from jax.experimental import pallas as pl
from jax.experimental.pallas import tpu as pltpu

F32 = jnp.float32
BF16 = jnp.bfloat16
MESH = pl.DeviceIdType.MESH

N_META = 16
CHUNK = 64
CHUNK_SHIFT = 6
HEADS = 4
RANK = 16
LR_LANES = 128
PAD_ROWS = CHUNK - N_META
EPS = 1e-6
GATE_NORMALIZER = 16.0
N_DEV = 8
ADAM_LR, ADAM_B1, ADAM_B2, ADAM_EPS, ADAM_WD, ADAM_STEP = 0.001, 0.9, 0.999, 1e-08, 0.01, 10
VMEM_LIMIT_BYTES = 56 * 1024 * 1024
BLOB_LANES = 512
SMALL_ROWS = 16


class _Dims:
    def __init__(self, bl, s, d):
        self.Bl, self.S, self.D = bl, s, d
        self.TM = 256 if s % 256 == 0 else CHUNK
        self.LP = self.TM + s
        self.T = bl * self.LP
        self.TPS = self.LP // self.TM
        self.NC = self.LP // CHUNK
        self.C0 = (self.TM - CHUNK) // CHUNK
        self.DK, self.DV = d // 2, d
        self.HK, self.HV = self.DK // HEADS, self.DV // HEADS
        self.HW = 2 * self.HK + 2 * self.HV
        self.CW = 256 if d % 256 == 0 and d > 256 else d // 4
        self.NJ = d // self.CW


def _pick(n, target, mult):
    t = min(n, target)
    while t >= mult:
        if n % t == 0 and t % mult == 0:
            return t
        t -= mult
    return n


def _cp(n_axes):
    return pltpu.CompilerParams(dimension_semantics=("arbitrary",) * n_axes, vmem_limit_bytes=VMEM_LIMIT_BYTES)


def _sigmoid(x):
    return 1.0 / (1.0 + jnp.exp(-x))


def _dot(a, b):
    return jnp.dot(a, b, preferred_element_type=F32)


def _dot_nt(a, b):
    return lax.dot_general(a, b, (((1,), (1,)), ((), ())), preferred_element_type=F32)


def _dot_tn(a, b):
    return lax.dot_general(a, b, (((0,), (0,)), ((), ())), preferred_element_type=F32)


def _dot_exact01(m01, x):
    hi = x.astype(BF16)
    lo = (x - hi.astype(F32)).astype(BF16)
    return _dot(m01, hi) + _dot(m01, lo)


def _exchange(gathers, scatters, name):
    arrays = list(gathers) + list(scatters)
    n, ng = len(arrays), len(gathers)

    def body(*refs):
        ins, outs = refs[:n], refs[n:2 * n]
        send_sems, recv_sems, local_sems = refs[2 * n:]
        x, y, c = lax.axis_index("x"), lax.axis_index("y"), lax.axis_index("c")
        me = 4 * x + 2 * y + c
        started = []
        for t in range(n):
            src, dst = ins[t], outs[t]
            own = pltpu.make_async_copy(src if t < ng else src.at[me], dst.at[me], local_sems.at[t])
            own.start()
            started.append(own)
            for k, pos, peer in _peers(x, y, c):
                cp = pltpu.make_async_remote_copy(
                    src_ref=src if t < ng else src.at[peer], dst_ref=dst.at[me],
                    send_sem=send_sems.at[t * (N_DEV - 1) + k - 1], recv_sem=recv_sems.at[t * (N_DEV - 1) + k - 1],
                    device_id=pos, device_id_type=MESH)
                cp.start()
                started.append(cp)
        for cp in started:
            cp.wait()

    out_shape = [jax.ShapeDtypeStruct((N_DEV,) + a.shape[-2:], a.dtype) for a in arrays]
    any_spec = pl.BlockSpec(memory_space=pl.ANY)
    return pl.pallas_call(
        body, name=name, out_shape=out_shape, in_specs=[any_spec] * n, out_specs=[any_spec] * n,
        scratch_shapes=[pltpu.SemaphoreType.DMA((n * (N_DEV - 1),)), pltpu.SemaphoreType.DMA((n * (N_DEV - 1),)),
                        pltpu.SemaphoreType.DMA((n,))],
        compiler_params=pltpu.CompilerParams(has_side_effects=True),
    )(*arrays)


def _gather_two_level(arrays, name):
    n = len(arrays)
    per = N_DEV - 1

    def body(*refs):
        ins, outs = refs[:n], refs[n:2 * n]
        send_sems, recv_sems, local_sems = refs[2 * n:]
        x, y, c = lax.axis_index("x"), lax.axis_index("y"), lax.axis_index("c")
        sibling = (x, y, 1 - c)
        chips = [(1 - x, y), (x, 1 - y), (1 - x, 1 - y)]
        index = lambda px, py, pc: 4 * px + 2 * py + pc

        def copy(t, k, block, to, from_input=False):
            slab = outs[t].at[index(*block)]
            return pltpu.make_async_remote_copy(
                src_ref=ins[t] if from_input else slab, dst_ref=slab, send_sem=send_sems.at[t * per + k],
                recv_sem=recv_sems.at[t * per + k], device_id=to, device_id_type=MESH)

        own, sent = [], []
        for t in range(n):
            own.append(pltpu.make_async_copy(ins[t], outs[t].at[index(x, y, c)], local_sems.at[t]))
            own[-1].start()
            first = [copy(t, 0, (x, y, c), sibling, True)]
            first += [copy(t, 1 + j, (x, y, c), (*chip, c), True) for j, chip in enumerate(chips)]
            for cp in first:
                cp.start()
            sent += first
        for t in range(n):
            for j, chip in enumerate(chips):
                copy(t, 1 + j, (*chip, c), (x, y, c)).wait_recv()
                sent.append(copy(t, 4 + j, (*chip, c), sibling))
                sent[-1].start()
        for t in range(n):
            copy(t, 0, sibling, (x, y, c)).wait_recv()
            for j, chip in enumerate(chips):
                copy(t, 4 + j, (*chip, 1 - c), (x, y, c)).wait_recv()
        for cp in sent:
            cp.wait_send()
        for cp in own:
            cp.wait()

    out_shape = [jax.ShapeDtypeStruct((N_DEV,) + a.shape, a.dtype) for a in arrays]
    any_spec = pl.BlockSpec(memory_space=pl.ANY)
    return pl.pallas_call(
        body, name=name, out_shape=out_shape, in_specs=[any_spec] * n, out_specs=[any_spec] * n,
        scratch_shapes=[pltpu.SemaphoreType.DMA((n * per,)), pltpu.SemaphoreType.DMA((n * per,)),
                        pltpu.SemaphoreType.DMA((n,))],
        compiler_params=pltpu.CompilerParams(has_side_effects=True),
    )(*arrays)


def _peers(x, y, c):
    out = []
    for k in range(1, N_DEV):
        px = 1 - x if (k >> 2) & 1 else x
        py = 1 - y if (k >> 1) & 1 else y
        pc = 1 - c if k & 1 else c
        out.append((k, (px, py, pc), 4 * px + 2 * py + pc))
    return out


def _exchange_start(gathers, scatters, after, name):
    arrays = list(gathers) + list(scatters)
    n, ng = len(arrays), len(gathers)
    hbm = pl.BlockSpec(memory_space=pltpu.HBM)
    sem = pl.BlockSpec(memory_space=pltpu.SEMAPHORE)

    extra = [] if after is None else [after]
    ne = len(extra)

    def body(*refs):
        ins, lands = refs[:n], refs[n:2 * n]
        send_sems, recv_sems = refs[2 * n + ne], refs[2 * n + ne + 1]
        token = refs[4 * n + ne + 2]
        x, y, c = lax.axis_index("x"), lax.axis_index("y"), lax.axis_index("c")
        me = 4 * x + 2 * y + c
        for t in range(n):
            for k, pos, peer in _peers(x, y, c):
                pltpu.make_async_remote_copy(
                    src_ref=ins[t] if t < ng else ins[t].at[peer], dst_ref=lands[t].at[me],
                    send_sem=send_sems.at[t * (N_DEV - 1) + k - 1], recv_sem=recv_sems.at[t * (N_DEV - 1) + k - 1],
                    device_id=pos, device_id_type=MESH).start()
        token[...] = jnp.zeros_like(token)

    me = 4 * lax.axis_index("x") + 2 * lax.axis_index("y") + lax.axis_index("c")
    lands = [lax.dynamic_update_index_in_dim(lax.empty((N_DEV,) + a.shape[-2:], a.dtype),
                                             a if t < ng else lax.dynamic_index_in_dim(a, me, 0, keepdims=False), me, 0)
             for t, a in enumerate(arrays)]
    operands = [pltpu.with_memory_space_constraint(a, pltpu.HBM) for a in arrays + lands]
    sems = pltpu.SemaphoreType.DMA((n * (N_DEV - 1),))
    res = pl.pallas_call(
        body, name=name,
        out_shape=(sems, sems, *[pltpu.HBM(a.shape, a.dtype) for a in arrays + lands], jax.ShapeDtypeStruct((8, 128), F32)),
        in_specs=[hbm] * (2 * n) + [pl.BlockSpec(memory_space=pl.ANY)] * ne,
        out_specs=(sem, sem, *[hbm] * (2 * n), pl.BlockSpec(memory_space=pltpu.VMEM)),
        input_output_aliases={i: 2 + i for i in range(2 * n)},
        compiler_params=pltpu.CompilerParams(has_side_effects=pltpu.SideEffectType.DATAFLOW_SIDE_EFFECTING),
    )(*operands, *extra)
    return res[-1][0, 0], (ng, res[0], res[1], list(res[2:2 + n]), list(res[2 + n:2 + 2 * n]))


def _exchange_wait(state, after, name):
    ng, send_sems, recv_sems, sent, lands = state
    n = len(sent)
    hbm = pl.BlockSpec(memory_space=pltpu.HBM)
    sem = pl.BlockSpec(memory_space=pltpu.SEMAPHORE)

    def body(*refs):
        ins, land_refs = refs[:n], refs[n:2 * n]
        send_ref, recv_ref = refs[2 * n], refs[2 * n + 1]
        x, y, c = lax.axis_index("x"), lax.axis_index("y"), lax.axis_index("c")
        me = 4 * x + 2 * y + c
        for t in range(n):
            for k, pos, peer in _peers(x, y, c):
                cp = pltpu.make_async_remote_copy(
                    src_ref=ins[t] if t < ng else ins[t].at[peer], dst_ref=land_refs[t].at[me],
                    send_sem=send_ref.at[t * (N_DEV - 1) + k - 1], recv_sem=recv_ref.at[t * (N_DEV - 1) + k - 1],
                    device_id=pos, device_id_type=MESH)
                cp.wait_send()
                cp.wait_recv()

    res = pl.pallas_call(
        body, name=name, out_shape=tuple(pltpu.HBM(a.shape, a.dtype) for a in sent + lands),
        in_specs=[hbm] * (2 * n) + [sem, sem, pl.BlockSpec(memory_space=pl.ANY)], out_specs=tuple([hbm] * (2 * n)),
        input_output_aliases={i: i for i in range(2 * n)},
        compiler_params=pltpu.CompilerParams(has_side_effects=pltpu.SideEffectType.DATAFLOW_SIDE_EFFECTING),
    )(*sent, *lands, send_sems, recv_sems, after)
    return list(res[n:])


def _prenorm(x, metapad, g_pre, dm):
    tm, tps, d = dm.TM, dm.TPS, dm.D

    def body(x_ref, mp_ref, g_ref, u_ref):
        j = pl.program_id(0) % tps
        h = jnp.where(j == 0, mp_ref[...], x_ref[0])
        r = lax.rsqrt(jnp.mean(h * h, axis=-1, keepdims=True) + EPS)
        u_ref[...] = (h * r * g_ref[...]).astype(BF16)

    return pl.pallas_call(
        body, name="prenorm", grid=(dm.Bl * tps,),
        in_specs=[pl.BlockSpec((1, tm, d), lambda i: (i // tps, jnp.maximum(i % tps - 1, 0), 0)),
                  pl.BlockSpec((tm, d), lambda i: (0, 0)),
                  pl.BlockSpec((1, d), lambda i: (0, 0))],
        out_specs=pl.BlockSpec((tm, d), lambda i: (i, 0)),
        out_shape=jax.ShapeDtypeStruct((dm.T, d), BF16), compiler_params=_cp(1),
    )(x, metapad, g_pre)


def _matmul_tn(a, b, out_dtype, name, tt=512, tn=1024, tk=1024):
    t, k = a.shape
    n = b.shape[1]
    tt, tn, tk = _pick(t, tt, 16), _pick(n, tn, 128), _pick(k, tk, 128)
    nt = t // tt

    def body(a_ref, b_ref, o_ref, acc):
        p = _dot_tn(a_ref[...].astype(BF16), b_ref[...].astype(BF16))
        i = pl.program_id(2)

        @pl.when(i == 0)
        def _():
            acc[...] = p

        @pl.when(i > 0)
        def _():
            acc[...] += p

        @pl.when(i == nt - 1)
        def _():
            o_ref[...] = acc[...].astype(out_dtype)

    return pl.pallas_call(
        body, name=name, grid=(k // tk, n // tn, nt),
        in_specs=[pl.BlockSpec((tt, tk), lambda kk, j, i: (i, kk)), pl.BlockSpec((tt, tn), lambda kk, j, i: (i, j))],
        out_specs=pl.BlockSpec((tk, tn), lambda kk, j, i: (kk, j)),
        out_shape=jax.ShapeDtypeStruct((k, n), out_dtype), scratch_shapes=[pltpu.VMEM((tk, tn), F32)],
        compiler_params=_cp(3),
    )(a, b)


def _load_resident(hbm_refs, vmem_refs, sems):
    @pl.when(pl.program_id(0) == 0)
    def _():
        copies = [pltpu.make_async_copy(h, v, sems.at[i]) for i, (h, v) in enumerate(zip(hbm_refs, vmem_refs))]
        for cp in copies:
            cp.start()
        for cp in copies:
            cp.wait()


def _inproj(u, wts, dm):
    t, d = u.shape
    tm = _pick(t, 512, 16)
    np_ = len(wts)
    cn = 1024

    def body(*refs):
        u_ref, w_hbm, outs = refs[0], refs[1:1 + np_], refs[1 + np_:1 + 2 * np_]
        w_vmem, sems = refs[1 + 2 * np_:1 + 3 * np_], refs[1 + 3 * np_]
        _load_resident(w_hbm, w_vmem, sems)
        ut = u_ref[...]
        for w, o_ref in zip(w_vmem, outs):
            n = w.shape[0]
            step = cn if n % cn == 0 else n
            for j in range(0, n, step):
                o_ref[:, j:j + step] = _dot_nt(ut, w[j:j + step, :]).astype(BF16)

    return pl.pallas_call(
        body, name="inproj", grid=(t // tm,),
        in_specs=[pl.BlockSpec((tm, d), lambda i: (i, 0))] + [pl.BlockSpec(memory_space=pl.ANY)] * np_,
        out_specs=[pl.BlockSpec((tm, w.shape[0]), lambda i: (i, 0)) for w in wts],
        out_shape=[jax.ShapeDtypeStruct((t, w.shape[0]), BF16) for w in wts],
        scratch_shapes=[pltpu.VMEM(w.shape, BF16) for w in wts] + [pltpu.SemaphoreType.DMA((np_,))],
        compiler_params=_cp(1),
    )(u, *wts)


def _grad_u(d_parts, wts, dm):
    t = d_parts[0].shape[0]
    d = wts[0].shape[1]
    tm = _pick(t, 512, 16)
    np_ = len(wts)

    def body(*refs):
        d_refs, w_hbm, o_ref = refs[:np_], refs[np_:2 * np_], refs[2 * np_]
        w_vmem, sems = refs[2 * np_ + 1:3 * np_ + 1], refs[3 * np_ + 1]
        _load_resident(w_hbm, w_vmem, sems)
        o_ref[...] = _dot(d_refs[0][...].astype(BF16), w_vmem[0][...])
        for a_ref, w in zip(d_refs[1:], w_vmem[1:]):
            o_ref[...] += _dot(a_ref[...].astype(BF16), w[...])

    return pl.pallas_call(
        body, name="grad_u", grid=(t // tm,),
        in_specs=[pl.BlockSpec((tm, a.shape[1]), lambda i: (i, 0)) for a in d_parts] + [pl.BlockSpec(memory_space=pl.ANY)] * np_,
        out_specs=pl.BlockSpec((tm, d), lambda i: (i, 0)), out_shape=jax.ShapeDtypeStruct((t, d), F32),
        scratch_shapes=[pltpu.VMEM(w.shape, BF16) for w in wts] + [pltpu.SemaphoreType.DMA((np_,))],
        compiler_params=_cp(1),
    )(*d_parts, *wts)


def _conv_rows(dm):
    return _pick(dm.LP, 256, 16)


def _shifted(m, prev_row, next_row, rows):
    row = lax.broadcasted_iota(jnp.int32, m.shape, 0)
    m_prev = jnp.where(row == 0, prev_row, pltpu.roll(m, 1, 0))
    m_next = jnp.where(row == rows - 1, next_row, pltpu.roll(m, rows - 1, 0))
    return m_prev, m_next


def _conv_fwd(proj_a, conv_w, dm):
    lp, cw, rc = dm.LP, dm.CW, _conv_rows(dm)
    nchunk = lp // rc

    def body(p_ref, w_ref, y_ref):
        w0, w1, w2 = w_ref[0:1, :], w_ref[1:2, :], w_ref[2:3, :]

        def chunk(ci, carry):
            r0 = pl.multiple_of(ci * rc, rc)
            blk = p_ref[pl.ds(r0, rc), :].astype(F32)
            cb, cc, cx, cz = (blk[:, i * cw:(i + 1) * cw] for i in range(4))
            m = cc * cx
            rp = pl.multiple_of(jnp.maximum(r0 - 16, 0), 16)
            rn = pl.multiple_of(jnp.minimum(r0 + rc, lp - 16), 16)
            pv = p_ref[pl.ds(rp, 16), cw:3 * cw].astype(F32)
            nx = p_ref[pl.ds(rn, 16), cw:3 * cw].astype(F32)
            prev_row = jnp.where(ci > 0, pv[15:16, :cw] * pv[15:16, cw:], 0.0)
            next_row = jnp.where(ci < nchunk - 1, nx[0:1, :cw] * nx[0:1, cw:], 0.0)
            m_prev, m_next = _shifted(m, prev_row, next_row, rc)
            s = w0 * m_prev + w1 * m + w2 * m_next
            y_ref[pl.ds(r0, rc), :] = (cb * s * (cz * _sigmoid(cz))).astype(BF16)
            return carry

        lax.fori_loop(0, nchunk, chunk, 0)

    return pl.pallas_call(
        body, name="conv_fwd", grid=(dm.Bl, dm.NJ),
        in_specs=[pl.BlockSpec((lp, 4 * cw), lambda s, j: (s, j)), pl.BlockSpec((3, cw), lambda s, j: (0, j))],
        out_specs=pl.BlockSpec((lp, cw), lambda s, j: (s, j)),
        out_shape=jax.ShapeDtypeStruct((dm.T, dm.D), BF16), compiler_params=_cp(2),
    )(proj_a, conv_w)


def _conv_bwd(proj_a, dy_conv, conv_w, dm):
    lp, cw, rc = dm.LP, dm.CW, _conv_rows(dm)
    nchunk = lp // rc

    def body(p_ref, dy_ref, w_ref, d_ref, gw_ref):
        w0, w1, w2 = w_ref[0:1, :], w_ref[1:2, :], w_ref[2:3, :]

        def ds_of(p4, dy):
            cb, cz = p4[:, :cw], p4[:, 3 * cw:]
            return dy * cb * (cz * _sigmoid(cz))

        def chunk(ci, carry):
            g0, g1, g2 = carry
            r0 = pl.multiple_of(ci * rc, rc)
            blk = p_ref[pl.ds(r0, rc), :].astype(F32)
            dy = dy_ref[pl.ds(r0, rc), :].astype(F32)
            cb, cc, cx, cz = (blk[:, i * cw:(i + 1) * cw] for i in range(4))
            rp = pl.multiple_of(jnp.maximum(r0 - 16, 0), 16)
            rn = pl.multiple_of(jnp.minimum(r0 + rc, lp - 16), 16)
            pv = p_ref[pl.ds(rp, 16), :].astype(F32)[15:16]
            nx = p_ref[pl.ds(rn, 16), :].astype(F32)[0:1]
            dpv = dy_ref[pl.ds(rp, 16), :].astype(F32)[15:16]
            dnx = dy_ref[pl.ds(rn, 16), :].astype(F32)[0:1]
            has_prev, has_next = ci > 0, ci < nchunk - 1
            m = cc * cx
            m_prev, m_next = _shifted(m, jnp.where(has_prev, pv[:, cw:2 * cw] * pv[:, 2 * cw:3 * cw], 0.0),
                                      jnp.where(has_next, nx[:, cw:2 * cw] * nx[:, 2 * cw:3 * cw], 0.0), rc)
            s = w0 * m_prev + w1 * m + w2 * m_next
            sg = _sigmoid(cz)
            silu = cz * sg
            ds = dy * cb * silu
            ds_prev, ds_next = _shifted(ds, jnp.where(has_prev, ds_of(pv, dpv), 0.0),
                                        jnp.where(has_next, ds_of(nx, dnx), 0.0), rc)
            dm_ = w0 * ds_next + w1 * ds + w2 * ds_prev
            d_ref[pl.ds(r0, rc), 0:cw] = (dy * s * silu).astype(BF16)
            d_ref[pl.ds(r0, rc), cw:2 * cw] = (dm_ * cx).astype(BF16)
            d_ref[pl.ds(r0, rc), 2 * cw:3 * cw] = (dm_ * cc).astype(BF16)
            d_ref[pl.ds(r0, rc), 3 * cw:4 * cw] = (dy * cb * s * (sg * (1.0 + cz * (1.0 - sg)))).astype(BF16)
            return (g0 + jnp.sum(ds * m_prev, axis=0, keepdims=True), g1 + jnp.sum(ds * m, axis=0, keepdims=True),
                    g2 + jnp.sum(ds * m_next, axis=0, keepdims=True))

        z = jnp.zeros((1, cw), F32)
        g0, g1, g2 = lax.fori_loop(0, nchunk, chunk, (z, z, z))

        @pl.when(pl.program_id(1) == 0)
        def _():
            gw_ref[...] = jnp.zeros_like(gw_ref)

        gw_ref[0:1, :] += g0
        gw_ref[1:2, :] += g1
        gw_ref[2:3, :] += g2

    return pl.pallas_call(
        body, name="conv_bwd", grid=(dm.NJ, dm.Bl),
        in_specs=[pl.BlockSpec((lp, 4 * cw), lambda j, s: (s, j)), pl.BlockSpec((lp, cw), lambda j, s: (s, j)),
                  pl.BlockSpec((3, cw), lambda j, s: (0, j))],
        out_specs=[pl.BlockSpec((lp, 4 * cw), lambda j, s: (s, j)), pl.BlockSpec((8, cw), lambda j, s: (0, j))],
        out_shape=[jax.ShapeDtypeStruct((dm.T, 4 * dm.D), BF16), jax.ShapeDtypeStruct((8, dm.D), F32)],
        compiler_params=_cp(2),
    )(proj_a, dy_conv, conv_w)


def _interleave(gens):
    results = [None] * len(gens)
    live = list(range(len(gens)))
    while live:
        for idx in list(live):
            try:
                next(gens[idx])
            except StopIteration as done:
                results[idx] = done.value
                live.remove(idx)
    return results


def _group_chunks(dm):
    n = dm.NC - dm.C0
    return 3 if n % 3 == 0 else 1


def _group_masks(rows):
    ii = lax.broadcasted_iota(jnp.int32, (rows, rows), 0)
    jj = lax.broadcasted_iota(jnp.int32, (rows, rows), 1)
    same = jnp.right_shift(ii, CHUNK_SHIFT) == jnp.right_shift(jj, CHUNK_SHIFT)
    low, up = same & (jj <= ii), same & (jj >= ii)
    return low, same & (jj > ii), low.astype(BF16), up.astype(BF16)


def _chunk_totals(b, fwd):
    hk = b.shape[1]
    rows = [b[c * CHUNK + CHUNK - 1:(c + 1) * CHUNK] if fwd else b[c * CHUNK:c * CHUNK + 1]
            for c in range(b.shape[0] // CHUNK)]
    return jnp.concatenate([jnp.broadcast_to(r, (CHUNK, hk)) for r in rows], axis=0)


def _log_gate(lr_rows, w_ref, b_ref, first_group, hk):
    z = _dot(lr_rows, w_ref[...]) + b_ref[...]
    e = jnp.exp(-jnp.abs(z))
    g = (jnp.minimum(z, 0.0) - jnp.log(1.0 + e)) * (1.0 / GATE_NORMALIZER)
    dg_dz = jnp.where(z >= 0.0, e, 1.0) / (1.0 + e) * (1.0 / GATE_NORMALIZER)
    row = lax.broadcasted_iota(jnp.int32, (lr_rows.shape[0], hk), 0)
    pad = first_group & (row < PAD_ROWS)
    return jnp.where(pad, 0.0, g), jnp.where(pad, 0.0, dg_dz)


def _gla_fwd(proj_b, lr, wg_f, bg_f, wg_b, bg_b, gla_g, dm):
    lp, hk, hv, nc, c0, hw = dm.LP, dm.HK, dm.HV, dm.NC, dm.C0, dm.HW
    scale = hk ** -0.5
    gc = _group_chunks(dm)
    gr, ng = gc * CHUNK, (nc - c0) // gc

    def body(p_ref, lr_ref, wf_ref, bf_ref, wb_ref, bb_ref, gg_ref, o_ref, y_ref, oacc_f, oacc_b):
        low_incl, up_strict, ones_low, ones_up = _group_masks(gr)
        if c0 > 0:
            o_ref[0:c0 * CHUNK, :] = jnp.zeros((c0 * CHUNK, hv), BF16)
            y_ref[0:c0 * CHUNK, :] = jnp.zeros((c0 * CHUNK, hv), BF16)

        def group(gi, st, fwd):
            w_ref, b_ref, oacc = (wf_ref, bf_ref, oacc_f) if fwd else (wb_ref, bb_ref, oacc_b)
            r0 = pl.multiple_of((c0 + gi * gc) * CHUNK, CHUNK)
            blk = p_ref[pl.ds(r0, gr), :]
            q = blk[:, :hk].astype(F32) * scale
            k = blk[:, hk:2 * hk].astype(F32)
            v = blk[:, 2 * hk:2 * hk + hv]
            yield
            g, _ = _log_gate(lr_ref[pl.ds(r0, gr), :], w_ref, b_ref, gi == 0, hk)
            yield
            b = _dot_exact01(ones_low if fwd else ones_up, g)
            yield
            btot = _chunk_totals(b, fwd)
            qi = (q * jnp.exp(b)).astype(BF16)
            ki = (k * jnp.exp(-b)).astype(BF16)
            kd = (k * jnp.exp(btot - b)).astype(BF16)
            dec = jnp.exp(btot)
            a = _dot_nt(qi, ki)
            yield
            o = _dot(jnp.where(low_incl if fwd else up_strict, a, 0.0).astype(BF16), v)
            for c in (range(gc) if fwd else reversed(range(gc))):
                yield
                rows = slice(c * CHUNK, (c + 1) * CHUNK)
                oacc[pl.ds(r0 + c * CHUNK, CHUNK), :] = o[rows] + _dot_nt(qi[rows], st.astype(BF16))
                st = st * dec[c * CHUNK:c * CHUNK + 1] + _dot_tn(v[rows], kd[rows])
            return st

        def step(i, carry):
            st_f, st_b = carry
            return tuple(_interleave([group(i, st_f, True), group(ng - 1 - i, st_b, False)]))

        zero = jnp.zeros((hv, hk), F32)
        lax.fori_loop(0, ng, step, (zero, zero))

        def finish(i, carry):
            r0 = pl.multiple_of((c0 + i * gc) * CHUNK, CHUNK)
            o = oacc_f[pl.ds(r0, gr), :] + oacc_b[pl.ds(r0, gr), :]
            r = p_ref[pl.ds(r0, gr), 2 * hk + hv:].astype(F32)
            on = o * lax.rsqrt(jnp.mean(o * o, axis=-1, keepdims=True) + EPS) * gg_ref[...]
            o_ref[pl.ds(r0, gr), :] = o.astype(BF16)
            y_ref[pl.ds(r0, gr), :] = (on * r * _sigmoid(r)).astype(BF16)
            return carry

        lax.fori_loop(0, ng, finish, 0)

    head = lambda s, h: (s, h)
    wspec = pl.BlockSpec((LR_LANES, hk), lambda s, h: (0, h))
    bspec = pl.BlockSpec((1, hk), lambda s, h: (0, h))
    return pl.pallas_call(
        body, name="gla_fwd", grid=(dm.Bl, HEADS),
        in_specs=[pl.BlockSpec((lp, hw), head), pl.BlockSpec((lp, LR_LANES), lambda s, h: (s, 0)),
                  wspec, bspec, wspec, bspec, pl.BlockSpec((1, hv), lambda s, h: (0, 0))],
        out_specs=[pl.BlockSpec((lp, hv), head), pl.BlockSpec((lp, hv), head)],
        out_shape=[jax.ShapeDtypeStruct((dm.T, dm.DV), BF16), jax.ShapeDtypeStruct((dm.T, dm.DV), BF16)],
        scratch_shapes=[pltpu.VMEM((lp, hv), F32), pltpu.VMEM((lp, hv), F32)],
        compiler_params=_cp(2),
    )(proj_b, lr, wg_f, bg_f, wg_b, bg_b, gla_g)


def _gla_bwd(proj_b, lr, o_all, dy_gla, wg_f, bg_f, wg_b, bg_b, gla_g, dm):
    lp, hk, hv, nc, c0, hw = dm.LP, dm.HK, dm.HV, dm.NC, dm.C0, dm.HW
    scale = hk ** -0.5
    gc = _group_chunks(dm)
    gr, ng = gc * CHUNK, (nc - c0) // gc

    def body(p_ref, lr_ref, o_ref, dy_ref, wf_ref, bf_ref, wb_ref, bb_ref, gg_ref,
             d_ref, dlr_ref, gwf_ref, gbf_ref, gwb_ref, gbb_ref, ggg_ref,
             do_s, sf_all, sb_all, bf_s, bb_s, gsf_s, gsb_s, dqf_s, dqb_s, dkf_s, dkb_s, dvf_s, dvb_s, dlrf_s, dlrb_s):
        low_incl, up_strict, ones_low, ones_up = _group_masks(gr)
        h = pl.program_id(1)

        @pl.when(h == 0)
        def _():
            dlr_ref[...] = jnp.zeros_like(dlr_ref)

        if c0 > 0:
            zr = c0 * CHUNK
            d_ref[0:zr, :] = jnp.zeros((zr, hw), BF16)

        def norm_bwd(i, ggg):
            r0 = pl.multiple_of((c0 + i * gc) * CHUNK, CHUNK)
            o = o_ref[pl.ds(r0, gr), :].astype(F32)
            dy = dy_ref[pl.ds(r0, gr), :].astype(F32)
            r = p_ref[pl.ds(r0, gr), 2 * hk + hv:].astype(F32)
            rstd = lax.rsqrt(jnp.mean(o * o, axis=-1, keepdims=True) + EPS)
            ohat = o * rstd
            sg = _sigmoid(r)
            d_on = dy * (r * sg)
            d_ref[pl.ds(r0, gr), 2 * hk + hv:] = (dy * ohat * gg_ref[...] * (sg * (1.0 + r * (1.0 - sg)))).astype(BF16)
            d_oh = d_on * gg_ref[...]
            do_s[pl.ds(r0, gr), :] = (rstd * (d_oh - ohat * jnp.mean(d_oh * ohat, axis=-1, keepdims=True))).astype(BF16)
            return ggg + jnp.sum(d_on * ohat, axis=0, keepdims=True)

        ggg = lax.fori_loop(0, ng, norm_bwd, jnp.zeros((1, hv), F32))

        @pl.when((pl.program_id(0) == 0) & (h == 0))
        def _():
            ggg_ref[...] = jnp.zeros_like(ggg_ref)

        ggg_ref[0:1, :] += ggg

        def load(gi):
            r0 = pl.multiple_of((c0 + gi * gc) * CHUNK, CHUNK)
            blk = p_ref[pl.ds(r0, gr), :]
            return r0, blk[:, :hk].astype(F32) * scale, blk[:, hk:2 * hk].astype(F32), blk[:, 2 * hk:2 * hk + hv]

        chunk_totals = _chunk_totals

        def record(gi, st, fwd):
            w_ref, b_ref, s_all, b_s, gs_s = (wf_ref, bf_ref, sf_all, bf_s, gsf_s) if fwd else (wb_ref, bb_ref, sb_all, bb_s, gsb_s)
            r0, q, k, v = load(gi)
            yield
            g, dg_dz = _log_gate(lr_ref[pl.ds(r0, gr), :], w_ref, b_ref, gi == 0, hk)
            gs_s[pl.ds(r0, gr), :] = dg_dz
            yield
            b = _dot_exact01(ones_low if fwd else ones_up, g)
            yield
            b_s[pl.ds(r0, gr), :] = b
            btot = chunk_totals(b, fwd)
            kd = (k * jnp.exp(btot - b)).astype(BF16)
            dec = jnp.exp(btot)
            for c in (range(gc) if fwd else reversed(range(gc))):
                yield
                rows = slice(c * CHUNK, (c + 1) * CHUNK)
                s_all[c0 + gi * gc + c] = st
                st = st * dec[c * CHUNK:c * CHUNK + 1] + _dot_tn(v[rows], kd[rows])
            return st

        def record_step(i, carry):
            return tuple(_interleave([record(i, carry[0], True), record(ng - 1 - i, carry[1], False)]))

        zero = jnp.zeros((hv, hk), F32)
        lax.fori_loop(0, ng, record_step, (zero, zero))

        def grad(gi, carry, fwd):
            dst, gw, gb = carry
            w_ref, s_all, b_s, gs_s = (wf_ref, sf_all, bf_s, gsf_s) if fwd else (wb_ref, sb_all, bb_s, gsb_s)
            dq_s, dk_s, dv_s, dlr_s = (dqf_s, dkf_s, dvf_s, dlrf_s) if fwd else (dqb_s, dkb_s, dvb_s, dlrb_s)
            mask = low_incl if fwd else up_strict
            r0, q, k, v = load(gi)
            b = b_s[pl.ds(r0, gr), :]
            btot = chunk_totals(b, fwd)
            eb, enb, edb, dec = jnp.exp(b), jnp.exp(-b), jnp.exp(btot - b), jnp.exp(btot)
            qi_f, ki_f, kd_f = q * eb, k * enb, k * edb
            qi, ki, kd = qi_f.astype(BF16), ki_f.astype(BF16), kd_f.astype(BF16)
            do = do_s[pl.ds(r0, gr), :]
            a = _dot_nt(qi, ki)
            da = _dot_nt(do, v)
            yield
            a = jnp.where(mask, a, 0.0).astype(BF16)
            da = jnp.where(mask, da, 0.0).astype(BF16)
            dv = _dot_tn(a, do)
            dqi = _dot(da, ki)
            dki = _dot_tn(da, qi)
            dv_c, dqi_c, dkd_c, extra_c = [None] * gc, [None] * gc, [None] * gc, [None] * gc
            for c in (reversed(range(gc)) if fwd else range(gc)):
                yield
                rows = slice(c * CHUNK, (c + 1) * CHUNK)
                st = s_all[c0 + gi * gc + c]
                dsn_b = dst.astype(BF16)
                dec_c = dec[c * CHUNK:c * CHUNK + 1]
                dv_c[c] = dv[rows] + _dot_nt(kd[rows], dsn_b)
                dqi_c[c] = dqi[rows] + _dot(do[rows], st.astype(BF16))
                dkd_c[c] = _dot(v[rows], dsn_b)
                ddec = jnp.sum(st * dst, axis=0, keepdims=True)
                extra = jnp.sum(dkd_c[c] * kd_f[rows], axis=0, keepdims=True) + ddec * dec_c
                extra_c[c] = jnp.broadcast_to(extra, (CHUNK, hk))
                dst = dst * dec_c + _dot_tn(do[rows], qi[rows])
            yield
            dv, dqi = jnp.concatenate(dv_c, axis=0), jnp.concatenate(dqi_c, axis=0)
            dkd, extra = jnp.concatenate(dkd_c, axis=0), jnp.concatenate(extra_c, axis=0)
            dq_s[pl.ds(r0, gr), :] = dqi * eb * scale
            dk_s[pl.ds(r0, gr), :] = dki * enb + dkd * edb
            dv_s[pl.ds(r0, gr), :] = dv
            db = dqi * qi_f - dki * ki_f - dkd * kd_f
            dg = _dot_exact01(ones_up if fwd else ones_low, db) + extra
            yield
            dz = dg * gs_s[pl.ds(r0, gr), :]
            dz_b = dz.astype(BF16)
            dlr_s[pl.ds(r0, gr), :] = _dot_nt(dz_b, w_ref[...])
            return dst, gw + _dot_tn(lr_ref[pl.ds(r0, gr), :], dz_b), gb + jnp.sum(dz, axis=0, keepdims=True)

        def grad_step(i, carry):
            return tuple(_interleave([grad(ng - 1 - i, carry[0], True), grad(i, carry[1], False)]))

        init = (zero, jnp.zeros((LR_LANES, hk), F32), jnp.zeros((1, hk), F32))
        (_, gw_f, gb_f), (_, gw_b, gb_b) = lax.fori_loop(0, ng, grad_step, (init, init))
        for gw_ref, gb_ref, gw, gb in ((gwf_ref, gbf_ref, gw_f, gb_f), (gwb_ref, gbb_ref, gw_b, gb_b)):
            gw_ref[0] = gw
            gb_ref[0] = jnp.zeros((8, hk), F32)
            gb_ref[0, 0:1, :] = gb

        def combine(i, carry):
            r0 = pl.multiple_of((c0 + i * gc) * CHUNK, CHUNK)
            d_ref[pl.ds(r0, gr), 0:hk] = (dqf_s[pl.ds(r0, gr), :] + dqb_s[pl.ds(r0, gr), :]).astype(BF16)
            d_ref[pl.ds(r0, gr), hk:2 * hk] = (dkf_s[pl.ds(r0, gr), :] + dkb_s[pl.ds(r0, gr), :]).astype(BF16)
            d_ref[pl.ds(r0, gr), 2 * hk:2 * hk + hv] = (dvf_s[pl.ds(r0, gr), :] + dvb_s[pl.ds(r0, gr), :]).astype(BF16)
            dlr_ref[pl.ds(r0, gr), :] += dlrf_s[pl.ds(r0, gr), :] + dlrb_s[pl.ds(r0, gr), :]
            return carry

        lax.fori_loop(0, ng, combine, 0)

    head = lambda s, h: (s, h)
    wspec = pl.BlockSpec((LR_LANES, hk), lambda s, h: (0, h))
    bspec = pl.BlockSpec((1, hk), lambda s, h: (0, h))
    gwspec = pl.BlockSpec((1, LR_LANES, hk), lambda s, h: (s, 0, h))
    gbspec = pl.BlockSpec((1, 8, hk), lambda s, h: (s, 0, h))
    gw_shape = jax.ShapeDtypeStruct((dm.Bl, LR_LANES, dm.DK), F32)
    gb_shape = jax.ShapeDtypeStruct((dm.Bl, 8, dm.DK), F32)
    return pl.pallas_call(
        body, name="gla_bwd", grid=(dm.Bl, HEADS),
        in_specs=[pl.BlockSpec((lp, hw), head), pl.BlockSpec((lp, LR_LANES), lambda s, h: (s, 0)),
                  pl.BlockSpec((lp, hv), head), pl.BlockSpec((lp, hv), head),
                  wspec, bspec, wspec, bspec, pl.BlockSpec((1, hv), lambda s, h: (0, 0))],
        out_specs=[pl.BlockSpec((lp, hw), head), pl.BlockSpec((lp, LR_LANES), lambda s, h: (s, 0)),
                   gwspec, gbspec, gwspec, gbspec, pl.BlockSpec((8, hv), lambda s, h: (0, 0))],
        out_shape=[jax.ShapeDtypeStruct((dm.T, HEADS * hw), BF16), jax.ShapeDtypeStruct((dm.T, LR_LANES), F32),
                   gw_shape, gb_shape, gw_shape, gb_shape, jax.ShapeDtypeStruct((8, hv), F32)],
        scratch_shapes=[pltpu.VMEM((lp, hv), BF16), pltpu.VMEM((nc, hv, hk), F32), pltpu.VMEM((nc, hv, hk), F32)]
        + [pltpu.VMEM((lp, hk), F32)] * 8 + [pltpu.VMEM((lp, hv), F32)] * 2 + [pltpu.VMEM((lp, LR_LANES), F32)] * 2,
        compiler_params=_cp(2),
    )(proj_b, lr, o_all, dy_gla, wg_f, bg_f, wg_b, bg_b, gla_g)


def _out_merge(y_conv, y_gla, proj_c, w_oc, w_og, dm):
    d = dm.D
    tm = _pick(dm.T, 512, 16)

    def body(yc_ref, yg_ref, c_ref, woc_ref, wog_ref, pc_ref, pg_ref, m_ref):
        pc = _dot(yc_ref[...], woc_ref[...])
        pg = _dot(yg_ref[...], wog_ref[...])
        pc_ref[...] = pc.astype(BF16)
        pg_ref[...] = pg.astype(BF16)
        ma = c_ref[:, :d].astype(F32)
        mb = c_ref[:, d:].astype(F32)
        m_ref[...] = (_sigmoid(ma) * pc + _sigmoid(mb) * pg).astype(BF16)

    row = pl.BlockSpec((tm, d), lambda i: (i, 0))
    full = pl.BlockSpec((d, d), lambda i: (0, 0))
    act = jax.ShapeDtypeStruct((dm.T, d), BF16)
    return pl.pallas_call(
        body, name="out_merge", grid=(dm.T // tm,),
        in_specs=[row, row, pl.BlockSpec((tm, 2 * d), lambda i: (i, 0)), full, full],
        out_specs=[row, row, row], out_shape=[act, act, act], compiler_params=_cp(1),
    )(y_conv, y_gla, proj_c, w_oc, w_og)


def _final_fwd(merged, w_out, x, metapad, target, g_post, dm):
    tm, tps, d = dm.TM, dm.TPS, dm.D

    def body(m_ref, w_ref, x_ref, mp_ref, t_ref, g_ref, dout_ref, dy_ref, st_ref):
        i = pl.program_id(0)
        j = i % tps
        out = _dot(m_ref[...], w_ref[...])
        rstd = lax.rsqrt(jnp.mean(out * out, axis=-1, keepdims=True) + EPS)
        ohat = out * rstd
        h = jnp.where(j == 0, mp_ref[...], x_ref[0])
        y = h + ohat * g_ref[...]
        err = jnp.where(j == 0, 0.0, y - t_ref[0])
        dy = err * (1.0 / d)
        d_oh = dy * g_ref[...]
        dout_ref[...] = (rstd * (d_oh - ohat * jnp.mean(d_oh * ohat, axis=-1, keepdims=True))).astype(BF16)
        dy_ref[...] = dy

        @pl.when(i == 0)
        def _():
            st_ref[...] = jnp.zeros_like(st_ref)

        st_ref[0:1, :] += jnp.sum(dy * ohat, axis=0, keepdims=True)
        st_ref[1:2, :] += jnp.sum(err * err, axis=0, keepdims=True)

    row = pl.BlockSpec((tm, d), lambda i: (i, 0))
    tok = pl.BlockSpec((1, tm, d), lambda i: (i // tps, jnp.maximum(i % tps - 1, 0), 0))
    const = lambda r: pl.BlockSpec((r, d), lambda i: (0, 0))
    return pl.pallas_call(
        body, name="final_fwd", grid=(dm.Bl * tps,),
        in_specs=[row, const(d), tok, const(tm), tok, const(1)],
        out_specs=[row, row, const(8)],
        out_shape=[jax.ShapeDtypeStruct((dm.T, d), BF16), jax.ShapeDtypeStruct((dm.T, d), F32),
                   jax.ShapeDtypeStruct((8, d), F32)],
        compiler_params=_cp(1),
    )(merged, w_out, x, metapad, target, g_post)


def _merge_bwd(d_out, proj_c, p_conv, p_gla, w_out, w_oc, w_og, dm):
    d = dm.D
    tm = _pick(dm.T, 512, 16)

    def body(do_ref, c_ref, pc_ref, pg_ref, wo_ref, woc_ref, wog_ref, dpc_ref, dpg_ref, dc_ref, dyc_ref, dyg_ref):
        dmg = _dot_nt(do_ref[...], wo_ref[...])
        sa = _sigmoid(c_ref[:, :d].astype(F32))
        sb = _sigmoid(c_ref[:, d:].astype(F32))
        dpc = (dmg * sa).astype(BF16)
        dpg = (dmg * sb).astype(BF16)
        dpc_ref[...] = dpc
        dpg_ref[...] = dpg
        dc_ref[:, :d] = (dmg * pc_ref[...].astype(F32) * sa * (1.0 - sa)).astype(BF16)
        dc_ref[:, d:] = (dmg * pg_ref[...].astype(F32) * sb * (1.0 - sb)).astype(BF16)
        dyc_ref[...] = _dot_nt(dpc, woc_ref[...]).astype(BF16)
        dyg_ref[...] = _dot_nt(dpg, wog_ref[...]).astype(BF16)

    row = pl.BlockSpec((tm, d), lambda i: (i, 0))
    row2 = pl.BlockSpec((tm, 2 * d), lambda i: (i, 0))
    full = pl.BlockSpec((d, d), lambda i: (0, 0))
    act = jax.ShapeDtypeStruct((dm.T, d), BF16)
    return pl.pallas_call(
        body, name="merge_bwd", grid=(dm.T // tm,),
        in_specs=[row, row2, row, row, full, full, full],
        out_specs=[row, row, row2, row, row],
        out_shape=[act, act, jax.ShapeDtypeStruct((dm.T, 2 * d), BF16), act, act],
        compiler_params=_cp(1),
    )(d_out, proj_c, p_conv, p_gla, w_out, w_oc, w_og)


def _prenorm_bwd(du, dy, x, metapad, g_pre, dm):
    tm, tps, d = dm.TM, dm.TPS, dm.D

    def body(du_ref, dy_ref, x_ref, mp_ref, g_ref, gx_ref, dmeta_ref, gg_ref):
        i = pl.program_id(0)
        j = i % tps
        h = jnp.where(j == 0, mp_ref[...], x_ref[0])
        rstd = lax.rsqrt(jnp.mean(h * h, axis=-1, keepdims=True) + EPS)
        hhat = h * rstd
        dug = du_ref[...] * g_ref[...]
        dh = dy_ref[...] + rstd * (dug - hhat * jnp.mean(dug * hhat, axis=-1, keepdims=True))

        @pl.when(j == 0)
        def _():
            dmeta_ref[0] = dh

        @pl.when(j > 0)
        def _():
            gx_ref[0] = dh

        @pl.when(i == 0)
        def _():
            gg_ref[...] = jnp.zeros_like(gg_ref)

        gg_ref[0:1, :] += jnp.sum(du_ref[...] * hhat, axis=0, keepdims=True)

    row = pl.BlockSpec((tm, d), lambda i: (i, 0))
    tok = pl.BlockSpec((1, tm, d), lambda i: (i // tps, jnp.maximum(i % tps - 1, 0), 0))
    const = lambda r: pl.BlockSpec((r, d), lambda i: (0, 0))
    return pl.pallas_call(
        body, name="prenorm_bwd", grid=(dm.Bl * tps,),
        in_specs=[row, row, tok, const(tm), const(1)],
        out_specs=[tok, pl.BlockSpec((1, tm, d), lambda i: (i // tps, 0, 0)), const(8)],
        out_shape=[jax.ShapeDtypeStruct((dm.Bl, dm.S, d), F32), jax.ShapeDtypeStruct((dm.Bl, tm, d), F32),
                   jax.ShapeDtypeStruct((8, d), F32)],
        compiler_params=_cp(1),
    )(du, dy, x, metapad, g_pre)


def _sum_partials(partials, name):
    p, r, c = partials.shape
    tc = _pick(c, 256, 128)

    def body(p_ref, o_ref):
        g = p_ref[0].astype(F32)
        for j in range(1, p):
            g = g + p_ref[j].astype(F32)
        o_ref[...] = g

    return pl.pallas_call(
        body, name=name, grid=(c // tc,), in_specs=[pl.BlockSpec((p, r, tc), lambda i: (0, 0, i))],
        out_specs=pl.BlockSpec((r, tc), lambda i: (0, i)), out_shape=jax.ShapeDtypeStruct((r, c), F32),
        compiler_params=_cp(1),
    )(partials)


def _adamw(partials, w, m, v, name):
    r, c = w.shape
    n_parts = partials.shape[0]
    tr = _pick(r, 256, 16)

    def body(p_ref, w_ref, m_ref, v_ref, g_ref, d_ref, nm_ref, nv_ref):
        g = p_ref[0].astype(F32)
        for j in range(1, n_parts):
            g = g + p_ref[j].astype(F32)
        m2 = ADAM_B1 * m_ref[...] + (1.0 - ADAM_B1) * g
        v2 = ADAM_B2 * v_ref[...] + (1.0 - ADAM_B2) * (g * g)
        m_hat = m2 / (1.0 - ADAM_B1 ** ADAM_STEP)
        v_hat = v2 / (1.0 - ADAM_B2 ** ADAM_STEP)
        g_ref[...] = g
        d_ref[...] = -ADAM_LR * (m_hat / (jnp.sqrt(v_hat) + ADAM_EPS) + ADAM_WD * w_ref[...])
        nm_ref[...] = m2
        nv_ref[...] = v2

    row = pl.BlockSpec((tr, c), lambda i: (i, 0))
    out = jax.ShapeDtypeStruct((r, c), F32)
    return pl.pallas_call(
        body, name=name, grid=(r // tr,),
        in_specs=[pl.BlockSpec((n_parts, tr, c), lambda i: (0, i, 0)), row, row, row],
        out_specs=[row, row, row, row], out_shape=[out, out, out, out], compiler_params=_cp(1),
    )(partials, w, m, v)


def _pack_rows(wt, dm):
    d, dk, hk, hv, cw, nj = dm.D, dm.DK, dm.HK, dm.HV, dm.CW, dm.NJ
    a = wt[:4 * d].reshape(4, nj, cw, d).transpose(1, 0, 2, 3).reshape(4 * d, d)
    b = jnp.concatenate([wt[4 * d:4 * d + dk].reshape(HEADS, hk, d), wt[4 * d + dk:5 * d].reshape(HEADS, hk, d),
                         wt[5 * d:6 * d].reshape(HEADS, hv, d), wt[6 * d:7 * d].reshape(HEADS, hv, d)],
                        axis=1).reshape(3 * d, d)
    c = wt[7 * d + 2 * RANK:]
    lr = jnp.pad(wt[7 * d:7 * d + 2 * RANK], ((0, LR_LANES - 2 * RANK), (0, 0)))
    return a, b, c, lr


def _unpack_rows(a, b, c, lr, dm):
    d, hk, hv, cw, nj, hw = dm.D, dm.HK, dm.HV, dm.CW, dm.NJ, dm.HW
    conv = a.reshape(nj, 4, cw, d).transpose(1, 0, 2, 3).reshape(4 * d, d)
    heads = b.reshape(HEADS, hw, d)
    q = heads[:, :hk].reshape(HEADS * hk, d)
    k = heads[:, hk:2 * hk].reshape(HEADS * hk, d)
    v = heads[:, 2 * hk:2 * hk + hv].reshape(HEADS * hv, d)
    r = heads[:, 2 * hk + hv:].reshape(HEADS * hv, d)
    return jnp.concatenate([conv, q, k, v, r, lr[:2 * RANK], c], axis=0)


def _to_blob(pieces, dtype, row_mult):
    lead = pieces[0].shape[0]
    flat = jnp.concatenate([p.reshape(lead, -1).astype(dtype) for p in pieces], axis=1)
    unit = row_mult * BLOB_LANES
    padded = -(-flat.shape[1] // unit) * unit
    flat = jnp.pad(flat, ((0, 0), (0, padded - flat.shape[1])))
    return flat.reshape(lead, padded // BLOB_LANES, BLOB_LANES)


def _from_blob(blob, shapes):
    lead = blob.shape[0]
    flat = blob.reshape(lead, -1)
    out, off = [], 0
    for shp in shapes:
        size = int(np.prod(shp))
        out.append(flat[:, off:off + size].reshape((lead,) + tuple(shp)))
        off += size
    return out


def _local_step(x, target, meta, g_pre, wt_in, conv_w, wg_f, bg_f, wg_b, bg_b, gla_g, out_weights, g_post,
                on_matrix_grads=None):
    bl, s, d = x.shape
    dm = _Dims(bl, s, d)
    metapad = jnp.concatenate([jnp.zeros((dm.TM - N_META, d), F32), meta], axis=0)
    wta, wtb, wtc, wtlr = _pack_rows(wt_in, dm)
    wgp_f = jnp.pad(wg_f, ((0, LR_LANES - RANK), (0, 0))).astype(BF16)
    wgp_b = jnp.pad(wg_b, ((RANK, LR_LANES - 2 * RANK), (0, 0))).astype(BF16)

    u = _prenorm(x, metapad, g_pre, dm)
    proj_a, proj_b, proj_c, lr = _inproj(u, [wta, wtb, wtc, wtlr], dm)
    y_conv = _conv_fwd(proj_a, conv_w, dm)
    o_all, y_gla = _gla_fwd(proj_b, lr, wgp_f, bg_f, wgp_b, bg_b, gla_g, dm)
    w_oc, w_og, w_out = out_weights(y_conv) if callable(out_weights) else out_weights
    p_conv, p_gla, merged = _out_merge(y_conv, y_gla, proj_c, w_oc, w_og, dm)
    d_out, dy, stats = _final_fwd(merged, w_out, x, metapad, target, g_post, dm)
    loss = 0.5 / d * jnp.sum(stats[1])

    d_pc, d_pg, d_c, dy_conv, dy_gla = _merge_bwd(d_out, proj_c, p_conv, p_gla, w_out, w_oc, w_og, dm)
    g_out = _matmul_tn(merged, d_out, BF16, "grad_w_out")
    g_oc = _matmul_tn(y_conv, d_pc, BF16, "grad_w_out_conv")
    g_og = _matmul_tn(y_gla, d_pg, BF16, "grad_w_out_gla")
    d_a, g_conv = _conv_bwd(proj_a, dy_conv, conv_w, dm)
    d_b, d_lr, gwp_f, gbp_f, gwp_b, gbp_b, g_gla = _gla_bwd(proj_b, lr, o_all, dy_gla, wgp_f, bg_f, wgp_b, bg_b, gla_g, dm)
    g_in = _unpack_rows(_matmul_tn(d_a, u, BF16, "grad_w_in_conv", tk=2048), _matmul_tn(d_b, u, BF16, "grad_w_in_gla", tk=3072),
                        _matmul_tn(d_c, u, BF16, "grad_w_in_merge", tk=2048), _matmul_tn(d_lr, u, BF16, "grad_w_in_gate"), dm)
    if on_matrix_grads is not None:
        wtlr = wtlr + on_matrix_grads(dict(w_in=g_in, w_out_conv=g_oc, w_out_gla=g_og, w_merge_out=g_out)).astype(BF16)
    du = _grad_u([d_a, d_b, d_c, d_lr], [wta, wtb, wtc, wtlr], dm)
    grad_x, d_meta, g_pre_rows = _prenorm_bwd(du, dy, x, metapad, g_pre, dm)

    grads = dict(
        meta_tokens=jnp.sum(d_meta[:, dm.TM - N_META:, :], axis=0), norm_pre=g_pre_rows[0:1], w_in=g_in,
        conv_w=g_conv[0:3], w_gate_fwd=jnp.sum(gwp_f, axis=0)[:RANK], b_gate_fwd=jnp.sum(gbp_f, axis=0)[0:1],
        w_gate_bwd=jnp.sum(gwp_b, axis=0)[RANK:2 * RANK], b_gate_bwd=jnp.sum(gbp_b, axis=0)[0:1],
        gla_norm=g_gla[0:1], w_out_conv=g_oc, w_out_gla=g_og, w_merge_out=g_out, norm_post=stats[0:1])
    return loss, grad_x, grads


MATRICES = ("w_out_conv", "w_out_gla", "w_merge_out")
SMALL_SHARDED = ("meta_tokens", "conv_w", "w_gate_fwd", "w_gate_bwd")
REPLICATED = ("norm_pre", "b_gate_fwd", "b_gate_bwd", "gla_norm", "norm_post")
NAMES = ("meta_tokens", "norm_pre", "w_in", "conv_w", "w_gate_fwd", "b_gate_fwd", "w_gate_bwd", "b_gate_bwd", "gla_norm",
         "w_out_conv", "w_out_gla", "w_merge_out", "norm_post")
DEPTH_AXIS = ("w_in", "conv_w", "w_gate_fwd", "w_gate_bwd") + MATRICES


def _cols_to_devices(g):
    r, c = g.shape
    return g.reshape(r, N_DEV, c // N_DEV).transpose(1, 0, 2)


def _cols_from_devices(parts):
    n, r, c = parts.shape
    return parts.transpose(1, 0, 2).reshape(r, n * c)


def kernel(x, meta_tokens, norm_pre, w_in, conv_w, w_gate_fwd, b_gate_fwd, w_gate_bwd, b_gate_bwd, gla_norm, w_out_conv, w_out_gla, w_merge_out, norm_post, loss_target, m_meta_tokens, m_norm_pre, m_w_in, m_conv_w, m_w_gate_fwd, m_b_gate_fwd, m_w_gate_bwd, m_b_gate_bwd, m_gla_norm, m_w_out_conv, m_w_out_gla, m_w_merge_out, m_norm_post, v_meta_tokens, v_norm_pre, v_w_in, v_conv_w, v_w_gate_fwd, v_b_gate_fwd, v_w_gate_bwd, v_b_gate_bwd, v_gla_norm, v_w_out_conv, v_w_out_gla, v_w_merge_out, v_norm_post):
    w = dict(meta_tokens=meta_tokens, norm_pre=norm_pre, w_in=w_in[0], conv_w=conv_w[0], w_gate_fwd=w_gate_fwd[0],
             b_gate_fwd=b_gate_fwd, w_gate_bwd=w_gate_bwd[0], b_gate_bwd=b_gate_bwd, gla_norm=gla_norm,
             w_out_conv=w_out_conv[0], w_out_gla=w_out_gla[0], w_merge_out=w_merge_out[0], norm_post=norm_post)
    m = dict(meta_tokens=m_meta_tokens, norm_pre=m_norm_pre, w_in=m_w_in[0], conv_w=m_conv_w[0], w_gate_fwd=m_w_gate_fwd[0],
             b_gate_fwd=m_b_gate_fwd, w_gate_bwd=m_w_gate_bwd[0], b_gate_bwd=m_b_gate_bwd, gla_norm=m_gla_norm,
             w_out_conv=m_w_out_conv[0], w_out_gla=m_w_out_gla[0], w_merge_out=m_w_merge_out[0], norm_post=m_norm_post)
    v = dict(meta_tokens=v_meta_tokens, norm_pre=v_norm_pre, w_in=v_w_in[0], conv_w=v_conv_w[0], w_gate_fwd=v_w_gate_fwd[0],
             b_gate_fwd=v_b_gate_fwd, w_gate_bwd=v_w_gate_bwd[0], b_gate_bwd=v_b_gate_bwd, gla_norm=v_gla_norm,
             w_out_conv=v_w_out_conv[0], w_out_gla=v_w_out_gla[0], w_merge_out=v_w_merge_out[0], norm_post=v_norm_post)
    d = x.shape[-1]

    small_blob = _to_blob([w[n][None] for n in SMALL_SHARDED], F32, SMALL_ROWS)[0]
    wt_all, small_all = _gather_two_level([w["w_in"].T.astype(BF16), small_blob], "gather_weights")
    _, late_weights = _exchange_start([w[n].astype(BF16) for n in MATRICES], [], small_all, "gather_out_weights_start")
    wt_in = wt_all.reshape(-1, d)
    small = {n: _cols_from_devices(p) for n, p in zip(SMALL_SHARDED, _from_blob(small_all, [w[n].shape for n in SMALL_SHARDED]))}

    def out_weights(after):
        return tuple(a.reshape(-1, d) for a in _exchange_wait(late_weights, after, "gather_out_weights_wait"))

    pending = []

    def on_matrix_grads(g):
        to_send = [g[n].astype(BF16).reshape(N_DEV, -1, d) for n in ("w_in",) + MATRICES]
        token, state = _exchange_start([], to_send, None, "exchange_grads_start")
        pending.append(state)
        return token

    loss, grad_x, grads = _local_step(
        x, loss_target, small["meta_tokens"], norm_pre, wt_in, small["conv_w"], small["w_gate_fwd"], b_gate_fwd,
        small["w_gate_bwd"], b_gate_bwd, gla_norm, out_weights, norm_post, on_matrix_grads)
    loss = lax.psum(loss, ("x", "y", "c"))
    received = _exchange_wait(pending[0], grad_x, "exchange_grads_wait")

    small_send = _to_blob([_cols_to_devices(grads[n]) for n in SMALL_SHARDED], F32, SMALL_ROWS)
    repl_blob = _to_blob([grads[n][None] for n in REPLICATED], F32, SMALL_ROWS)[0]
    repl_all, small_recv = _exchange([repl_blob], [small_send], "exchange_small_grads")

    results = {"w_in": _adamw(_sum_partials(received[0], "sum_grad_w_in").T[None], w["w_in"], m["w_in"], v["w_in"], "adamw_w_in")}
    for n, partials in zip(MATRICES, received[1:4]):
        results[n] = _adamw(partials, w[n], m[n], v[n], "adamw_" + n)
    for names, partials, tag in ((SMALL_SHARDED, small_recv, "small"), (REPLICATED, repl_all, "replicated")):
        blobs = [_to_blob([t[n][None] for n in names], F32, SMALL_ROWS)[0] for t in (w, m, v)]
        res = [_from_blob(r[None], [w[n].shape for n in names]) for r in _adamw(partials, *blobs, name="adamw_" + tag)]
        for i, n in enumerate(names):
            results[n] = [r[i][0] for r in res]
    lead = lambda n, t: t[None] if n in DEPTH_AXIS else t
    return (loss, grad_x, *[lead(n, results[n][i]) for i in range(4) for n in NAMES])
```

```python
import functools

import jax
import jax.numpy as jnp
import numpy as np
from jax import lax
from jax.experimental import pallas as pl
from jax.experimental.pallas import tpu as pltpu

F32 = jnp.float32
BF16 = jnp.bfloat16
MESH = pl.DeviceIdType.MESH

N_META = 16
CHUNK = 64
CHUNK_SHIFT = 6
HEADS = 4
RANK = 16
LR_LANES = 128
PAD_ROWS = CHUNK - N_META
EPS = 1e-6
GATE_NORMALIZER = 16.0
N_DEV = 8
ADAM_LR, ADAM_B1, ADAM_B2, ADAM_EPS, ADAM_WD, ADAM_STEP = 0.001, 0.9, 0.999, 1e-08, 0.01, 10
VMEM_LIMIT_BYTES = 56 * 1024 * 1024
BLOB_LANES = 512
SMALL_ROWS = 16


class _Dims:
    def __init__(self, bl, s, d):
        self.Bl, self.S, self.D = bl, s, d
        self.TM = 128 if s % 128 == 0 else CHUNK
        self.LP = self.TM + s
        self.T = bl * self.LP
        self.TPS = self.LP // self.TM
        self.NC = self.LP // CHUNK
        self.C0 = (self.TM - CHUNK) // CHUNK
        self.DK, self.DV = d // 2, d
        self.HK, self.HV = self.DK // HEADS, self.DV // HEADS
        self.HW = 2 * self.HK + 2 * self.HV
        self.CW = 256 if d % 256 == 0 and d > 256 else d // 4
        self.NJ = d // self.CW


def _pick(n, target, mult):
    t = min(n, target)
    while t >= mult:
        if n % t == 0 and t % mult == 0:
            return t
        t -= mult
    return n


def _cp(n_axes):
    return pltpu.CompilerParams(dimension_semantics=("arbitrary",) * n_axes, vmem_limit_bytes=VMEM_LIMIT_BYTES)


def _sigmoid(x):
    return 1.0 / (1.0 + jnp.exp(-x))


def _dot(a, b):
    return jnp.dot(a, b, preferred_element_type=F32)


def _dot_nt(a, b):
    return lax.dot_general(a, b, (((1,), (1,)), ((), ())), preferred_element_type=F32)


def _dot_tn(a, b):
    return lax.dot_general(a, b, (((0,), (0,)), ((), ())), preferred_element_type=F32)


def _dot_exact01(m01, x):
    hi = x.astype(BF16)
    lo = (x - hi.astype(F32)).astype(BF16)
    return _dot(m01, hi) + _dot(m01, lo)


def _exchange(gathers, scatters, name):
    arrays = list(gathers) + list(scatters)
    n, ng = len(arrays), len(gathers)

    def body(*refs):
        ins, outs = refs[:n], refs[n:2 * n]
        send_sems, recv_sems, local_sems = refs[2 * n:]
        x, y, c = lax.axis_index("x"), lax.axis_index("y"), lax.axis_index("c")
        me = 4 * x + 2 * y + c
        started = []
        for t in range(n):
            src, dst = ins[t], outs[t]
            own = pltpu.make_async_copy(src if t < ng else src.at[me], dst.at[me], local_sems.at[t])
            own.start()
            started.append(own)
            for k, pos, peer in _peers(x, y, c):
                cp = pltpu.make_async_remote_copy(
                    src_ref=src if t < ng else src.at[peer], dst_ref=dst.at[me],
                    send_sem=send_sems.at[t * (N_DEV - 1) + k - 1], recv_sem=recv_sems.at[t * (N_DEV - 1) + k - 1],
                    device_id=pos, device_id_type=MESH)
                cp.start()
                started.append(cp)
        for cp in started:
            cp.wait()

    out_shape = [jax.ShapeDtypeStruct((N_DEV,) + a.shape[-2:], a.dtype) for a in arrays]
    any_spec = pl.BlockSpec(memory_space=pl.ANY)
    return pl.pallas_call(
        body, name=name, out_shape=out_shape, in_specs=[any_spec] * n, out_specs=[any_spec] * n,
        scratch_shapes=[pltpu.SemaphoreType.DMA((n * (N_DEV - 1),)), pltpu.SemaphoreType.DMA((n * (N_DEV - 1),)),
                        pltpu.SemaphoreType.DMA((n,))],
        compiler_params=pltpu.CompilerParams(has_side_effects=True),
    )(*arrays)


def _gather_two_level(arrays, name):
    n = len(arrays)
    per = N_DEV - 1

    def body(*refs):
        ins, outs = refs[:n], refs[n:2 * n]
        send_sems, recv_sems, local_sems = refs[2 * n:]
        x, y, c = lax.axis_index("x"), lax.axis_index("y"), lax.axis_index("c")
        sibling = (x, y, 1 - c)
        chips = [(1 - x, y), (x, 1 - y), (1 - x, 1 - y)]
        index = lambda px, py, pc: 4 * px + 2 * py + pc

        def copy(t, k, block, to, from_input=False):
            slab = outs[t].at[index(*block)]
            return pltpu.make_async_remote_copy(
                src_ref=ins[t] if from_input else slab, dst_ref=slab, send_sem=send_sems.at[t * per + k],
                recv_sem=recv_sems.at[t * per + k], device_id=to, device_id_type=MESH)

        own, sent = [], []
        for t in range(n):
            own.append(pltpu.make_async_copy(ins[t], outs[t].at[index(x, y, c)], local_sems.at[t]))
            own[-1].start()
            first = [copy(t, 0, (x, y, c), sibling, True)]
            first += [copy(t, 1 + j, (x, y, c), (*chip, c), True) for j, chip in enumerate(chips)]
            for cp in first:
                cp.start()
            sent += first
        for t in range(n):
            for j, chip in enumerate(chips):
                copy(t, 1 + j, (*chip, c), (x, y, c)).wait_recv()
                sent.append(copy(t, 4 + j, (*chip, c), sibling))
                sent[-1].start()
        for t in range(n):
            copy(t, 0, sibling, (x, y, c)).wait_recv()
            for j, chip in enumerate(chips):
                copy(t, 4 + j, (*chip, 1 - c), (x, y, c)).wait_recv()
        for cp in sent:
            cp.wait_send()
        for cp in own:
            cp.wait()

    out_shape = [jax.ShapeDtypeStruct((N_DEV,) + a.shape, a.dtype) for a in arrays]
    any_spec = pl.BlockSpec(memory_space=pl.ANY)
    return pl.pallas_call(
        body, name=name, out_shape=out_shape, in_specs=[any_spec] * n, out_specs=[any_spec] * n,
        scratch_shapes=[pltpu.SemaphoreType.DMA((n * per,)), pltpu.SemaphoreType.DMA((n * per,)),
                        pltpu.SemaphoreType.DMA((n,))],
        compiler_params=pltpu.CompilerParams(has_side_effects=True),
    )(*arrays)


def _peers(x, y, c):
    out = []
    for k in range(1, N_DEV):
        px = 1 - x if (k >> 2) & 1 else x
        py = 1 - y if (k >> 1) & 1 else y
        pc = 1 - c if k & 1 else c
        out.append((k, (px, py, pc), 4 * px + 2 * py + pc))
    return out


def _exchange_start(gathers, scatters, after, name):
    arrays = list(gathers) + list(scatters)
    n, ng = len(arrays), len(gathers)
    hbm = pl.BlockSpec(memory_space=pltpu.HBM)
    sem = pl.BlockSpec(memory_space=pltpu.SEMAPHORE)

    extra = [] if after is None else [after]
    ne = len(extra)

    def body(*refs):
        ins, lands = refs[:n], refs[n:2 * n]
        send_sems, recv_sems = refs[2 * n + ne], refs[2 * n + ne + 1]
        token = refs[4 * n + ne + 2]
        x, y, c = lax.axis_index("x"), lax.axis_index("y"), lax.axis_index("c")
        me = 4 * x + 2 * y + c
        for t in range(n):
            for k, pos, peer in _peers(x, y, c):
                pltpu.make_async_remote_copy(
                    src_ref=ins[t] if t < ng else ins[t].at[peer], dst_ref=lands[t].at[me],
                    send_sem=send_sems.at[t * (N_DEV - 1) + k - 1], recv_sem=recv_sems.at[t * (N_DEV - 1) + k - 1],
                    device_id=pos, device_id_type=MESH).start()
        token[...] = jnp.zeros_like(token)

    me = 4 * lax.axis_index("x") + 2 * lax.axis_index("y") + lax.axis_index("c")
    lands = [lax.dynamic_update_index_in_dim(lax.empty((N_DEV,) + a.shape[-2:], a.dtype),
                                             a if t < ng else lax.dynamic_index_in_dim(a, me, 0, keepdims=False), me, 0)
             for t, a in enumerate(arrays)]
    operands = [pltpu.with_memory_space_constraint(a, pltpu.HBM) for a in arrays + lands]
    sems = pltpu.SemaphoreType.DMA((n * (N_DEV - 1),))
    res = pl.pallas_call(
        body, name=name,
        out_shape=(sems, sems, *[pltpu.HBM(a.shape, a.dtype) for a in arrays + lands], jax.ShapeDtypeStruct((8, 128), F32)),
        in_specs=[hbm] * (2 * n) + [pl.BlockSpec(memory_space=pl.ANY)] * ne,
        out_specs=(sem, sem, *[hbm] * (2 * n), pl.BlockSpec(memory_space=pltpu.VMEM)),
        input_output_aliases={i: 2 + i for i in range(2 * n)},
        compiler_params=pltpu.CompilerParams(has_side_effects=pltpu.SideEffectType.DATAFLOW_SIDE_EFFECTING),
    )(*operands, *extra)
    return res[-1][0, 0], (ng, res[0], res[1], list(res[2:2 + n]), list(res[2 + n:2 + 2 * n]))


def _exchange_wait(state, after, name):
    ng, send_sems, recv_sems, sent, lands = state
    n = len(sent)
    hbm = pl.BlockSpec(memory_space=pltpu.HBM)
    sem = pl.BlockSpec(memory_space=pltpu.SEMAPHORE)

    def body(*refs):
        ins, land_refs = refs[:n], refs[n:2 * n]
        send_ref, recv_ref = refs[2 * n], refs[2 * n + 1]
        x, y, c = lax.axis_index("x"), lax.axis_index("y"), lax.axis_index("c")
        me = 4 * x + 2 * y + c
        for t in range(n):
            for k, pos, peer in _peers(x, y, c):
                cp = pltpu.make_async_remote_copy(
                    src_ref=ins[t] if t < ng else ins[t].at[peer], dst_ref=land_refs[t].at[me],
                    send_sem=send_ref.at[t * (N_DEV - 1) + k - 1], recv_sem=recv_ref.at[t * (N_DEV - 1) + k - 1],
                    device_id=pos, device_id_type=MESH)
                cp.wait_send()
                cp.wait_recv()

    res = pl.pallas_call(
        body, name=name, out_shape=tuple(pltpu.HBM(a.shape, a.dtype) for a in sent + lands),
        in_specs=[hbm] * (2 * n) + [sem, sem, pl.BlockSpec(memory_space=pl.ANY)], out_specs=tuple([hbm] * (2 * n)),
        input_output_aliases={i: i for i in range(2 * n)},
        compiler_params=pltpu.CompilerParams(has_side_effects=pltpu.SideEffectType.DATAFLOW_SIDE_EFFECTING),
    )(*sent, *lands, send_sems, recv_sems, after)
    return list(res[n:])


def _prenorm(x, metapad, g_pre, dm):
    tm, tps, d = dm.TM, dm.TPS, dm.D

    def body(x_ref, mp_ref, g_ref, u_ref):
        j = pl.program_id(0) % tps
        h = jnp.where(j == 0, mp_ref[...], x_ref[0])
        r = lax.rsqrt(jnp.mean(h * h, axis=-1, keepdims=True) + EPS)
        u_ref[...] = (h * r * g_ref[...]).astype(BF16)

    return pl.pallas_call(
        body, name="prenorm", grid=(dm.Bl * tps,),
        in_specs=[pl.BlockSpec((1, tm, d), lambda i: (i // tps, jnp.maximum(i % tps - 1, 0), 0)),
                  pl.BlockSpec((tm, d), lambda i: (0, 0)),
                  pl.BlockSpec((1, d), lambda i: (0, 0))],
        out_specs=pl.BlockSpec((tm, d), lambda i: (i, 0)),
        out_shape=jax.ShapeDtypeStruct((dm.T, d), BF16), compiler_params=_cp(1),
    )(x, metapad, g_pre)


def _matmul_tn(a, b, out_dtype, name, tt=2304, tn=1024, tk=1024):
    t, k = a.shape
    n = b.shape[1]
    tt, tn, tk = _pick(t, tt, 16), _pick(n, tn, 128), _pick(k, tk, 128)
    nt = t // tt

    def body(a_ref, b_ref, o_ref, acc):
        p = _dot_tn(a_ref[...].astype(BF16), b_ref[...].astype(BF16))
        i = pl.program_id(2)

        @pl.when(i == 0)
        def _():
            acc[...] = p

        @pl.when(i > 0)
        def _():
            acc[...] += p

        @pl.when(i == nt - 1)
        def _():
            o_ref[...] = acc[...].astype(out_dtype)

    return pl.pallas_call(
        body, name=name, grid=(k // tk, n // tn, nt),
        in_specs=[pl.BlockSpec((tt, tk), lambda kk, j, i: (i, kk)), pl.BlockSpec((tt, tn), lambda kk, j, i: (i, j))],
        out_specs=pl.BlockSpec((tk, tn), lambda kk, j, i: (kk, j)),
        out_shape=jax.ShapeDtypeStruct((k, n), out_dtype), scratch_shapes=[pltpu.VMEM((tk, tn), F32)],
        compiler_params=_cp(3),
    )(a, b)


def _load_resident(hbm_refs, vmem_refs, sems):
    @pl.when(pl.program_id(0) == 0)
    def _():
        copies = [pltpu.make_async_copy(h, v, sems.at[i]) for i, (h, v) in enumerate(zip(hbm_refs, vmem_refs))]
        for cp in copies:
            cp.start()
        for cp in copies:
            cp.wait()


def _inproj(u, wts, dm):
    t, d = u.shape
    tm = _pick(t, 512, 16)
    np_ = len(wts)
    cn = 1024

    def body(*refs):
        u_ref, w_hbm, outs = refs[0], refs[1:1 + np_], refs[1 + np_:1 + 2 * np_]
        w_vmem, sems = refs[1 + 2 * np_:1 + 3 * np_], refs[1 + 3 * np_]
        _load_resident(w_hbm, w_vmem, sems)
        ut = u_ref[...]
        for w, o_ref in zip(w_vmem, outs):
            n = w.shape[0]
            step = cn if n % cn == 0 else n
            for j in range(0, n, step):
                o_ref[:, j:j + step] = _dot_nt(ut, w[j:j + step, :]).astype(BF16)

    return pl.pallas_call(
        body, name="inproj", grid=(t // tm,),
        in_specs=[pl.BlockSpec((tm, d), lambda i: (i, 0))] + [pl.BlockSpec(memory_space=pl.ANY)] * np_,
        out_specs=[pl.BlockSpec((tm, w.shape[0]), lambda i: (i, 0)) for w in wts],
        out_shape=[jax.ShapeDtypeStruct((t, w.shape[0]), BF16) for w in wts],
        scratch_shapes=[pltpu.VMEM(w.shape, BF16) for w in wts] + [pltpu.SemaphoreType.DMA((np_,))],
        compiler_params=_cp(1),
    )(u, *wts)


def _grad_u(d_parts, wts, dm):
    t = d_parts[0].shape[0]
    d = wts[0].shape[1]
    tm = _pick(t, 512, 16)
    np_ = len(wts)

    def body(*refs):
        d_refs, w_hbm, o_ref = refs[:np_], refs[np_:2 * np_], refs[2 * np_]
        w_vmem, sems = refs[2 * np_ + 1:3 * np_ + 1], refs[3 * np_ + 1]
        _load_resident(w_hbm, w_vmem, sems)
        o_ref[...] = _dot(d_refs[0][...].astype(BF16), w_vmem[0][...])
        for a_ref, w in zip(d_refs[1:], w_vmem[1:]):
            o_ref[...] += _dot(a_ref[...].astype(BF16), w[...])

    return pl.pallas_call(
        body, name="grad_u", grid=(t // tm,),
        in_specs=[pl.BlockSpec((tm, a.shape[1]), lambda i: (i, 0)) for a in d_parts] + [pl.BlockSpec(memory_space=pl.ANY)] * np_,
        out_specs=pl.BlockSpec((tm, d), lambda i: (i, 0)), out_shape=jax.ShapeDtypeStruct((t, d), F32),
        scratch_shapes=[pltpu.VMEM(w.shape, BF16) for w in wts] + [pltpu.SemaphoreType.DMA((np_,))],
        compiler_params=_cp(1),
    )(*d_parts, *wts)


def _conv_rows(dm):
    return _pick(dm.LP, 256, 16)


def _shifted(m, prev_row, next_row, rows):
    row = lax.broadcasted_iota(jnp.int32, m.shape, 0)
    m_prev = jnp.where(row == 0, prev_row, pltpu.roll(m, 1, 0))
    m_next = jnp.where(row == rows - 1, next_row, pltpu.roll(m, rows - 1, 0))
    return m_prev, m_next


def _conv_fwd(proj_a, conv_w, dm):
    lp, cw, rc = dm.LP, dm.CW, _conv_rows(dm)
    nchunk = lp // rc

    def body(p_ref, w_ref, y_ref):
        w0, w1, w2 = w_ref[0:1, :], w_ref[1:2, :], w_ref[2:3, :]

        def chunk(ci, carry):
            r0 = pl.multiple_of(ci * rc, rc)
            blk = p_ref[pl.ds(r0, rc), :].astype(F32)
            cb, cc, cx, cz = (blk[:, i * cw:(i + 1) * cw] for i in range(4))
            m = cc * cx
            rp = pl.multiple_of(jnp.maximum(r0 - 16, 0), 16)
            rn = pl.multiple_of(jnp.minimum(r0 + rc, lp - 16), 16)
            pv = p_ref[pl.ds(rp, 16), cw:3 * cw].astype(F32)
            nx = p_ref[pl.ds(rn, 16), cw:3 * cw].astype(F32)
            prev_row = jnp.where(ci > 0, pv[15:16, :cw] * pv[15:16, cw:], 0.0)
            next_row = jnp.where(ci < nchunk - 1, nx[0:1, :cw] * nx[0:1, cw:], 0.0)
            m_prev, m_next = _shifted(m, prev_row, next_row, rc)
            s = w0 * m_prev + w1 * m + w2 * m_next
            y_ref[pl.ds(r0, rc), :] = (cb * s * (cz * _sigmoid(cz))).astype(BF16)
            return carry

        lax.fori_loop(0, nchunk, chunk, 0)

    return pl.pallas_call(
        body, name="conv_fwd", grid=(dm.Bl, dm.NJ),
        in_specs=[pl.BlockSpec((lp, 4 * cw), lambda s, j: (s, j)), pl.BlockSpec((3, cw), lambda s, j: (0, j))],
        out_specs=pl.BlockSpec((lp, cw), lambda s, j: (s, j)),
        out_shape=jax.ShapeDtypeStruct((dm.T, dm.D), BF16), compiler_params=_cp(2),
    )(proj_a, conv_w)


def _conv_bwd(proj_a, dy_conv, conv_w, dm):
    lp, cw, rc = dm.LP, dm.CW, _conv_rows(dm)
    nchunk = lp // rc

    def body(p_ref, dy_ref, w_ref, d_ref, gw_ref):
        w0, w1, w2 = w_ref[0:1, :], w_ref[1:2, :], w_ref[2:3, :]

        def ds_of(p4, dy):
            cb, cz = p4[:, :cw], p4[:, 3 * cw:]
            return dy * cb * (cz * _sigmoid(cz))

        def chunk(ci, carry):
            g0, g1, g2 = carry
            r0 = pl.multiple_of(ci * rc, rc)
            blk = p_ref[pl.ds(r0, rc), :].astype(F32)
            dy = dy_ref[pl.ds(r0, rc), :].astype(F32)
            cb, cc, cx, cz = (blk[:, i * cw:(i + 1) * cw] for i in range(4))
            rp = pl.multiple_of(jnp.maximum(r0 - 16, 0), 16)
            rn = pl.multiple_of(jnp.minimum(r0 + rc, lp - 16), 16)
            pv = p_ref[pl.ds(rp, 16), :].astype(F32)[15:16]
            nx = p_ref[pl.ds(rn, 16), :].astype(F32)[0:1]
            dpv = dy_ref[pl.ds(rp, 16), :].astype(F32)[15:16]
            dnx = dy_ref[pl.ds(rn, 16), :].astype(F32)[0:1]
            has_prev, has_next = ci > 0, ci < nchunk - 1
            m = cc * cx
            m_prev, m_next = _shifted(m, jnp.where(has_prev, pv[:, cw:2 * cw] * pv[:, 2 * cw:3 * cw], 0.0),
                                      jnp.where(has_next, nx[:, cw:2 * cw] * nx[:, 2 * cw:3 * cw], 0.0), rc)
            s = w0 * m_prev + w1 * m + w2 * m_next
            sg = _sigmoid(cz)
            silu = cz * sg
            ds = dy * cb * silu
            ds_prev, ds_next = _shifted(ds, jnp.where(has_prev, ds_of(pv, dpv), 0.0),
                                        jnp.where(has_next, ds_of(nx, dnx), 0.0), rc)
            dm_ = w0 * ds_next + w1 * ds + w2 * ds_prev
            d_ref[pl.ds(r0, rc), 0:cw] = (dy * s * silu).astype(BF16)
            d_ref[pl.ds(r0, rc), cw:2 * cw] = (dm_ * cx).astype(BF16)
            d_ref[pl.ds(r0, rc), 2 * cw:3 * cw] = (dm_ * cc).astype(BF16)
            d_ref[pl.ds(r0, rc), 3 * cw:4 * cw] = (dy * cb * s * (sg * (1.0 + cz * (1.0 - sg)))).astype(BF16)
            return (g0 + jnp.sum(ds * m_prev, axis=0, keepdims=True), g1 + jnp.sum(ds * m, axis=0, keepdims=True),
                    g2 + jnp.sum(ds * m_next, axis=0, keepdims=True))

        z = jnp.zeros((1, cw), F32)
        g0, g1, g2 = lax.fori_loop(0, nchunk, chunk, (z, z, z))

        @pl.when(pl.program_id(1) == 0)
        def _():
            gw_ref[...] = jnp.zeros_like(gw_ref)

        gw_ref[0:1, :] += g0
        gw_ref[1:2, :] += g1
        gw_ref[2:3, :] += g2

    return pl.pallas_call(
        body, name="conv_bwd", grid=(dm.NJ, dm.Bl),
        in_specs=[pl.BlockSpec((lp, 4 * cw), lambda j, s: (s, j)), pl.BlockSpec((lp, cw), lambda j, s: (s, j)),
                  pl.BlockSpec((3, cw), lambda j, s: (0, j))],
        out_specs=[pl.BlockSpec((lp, 4 * cw), lambda j, s: (s, j)), pl.BlockSpec((8, cw), lambda j, s: (0, j))],
        out_shape=[jax.ShapeDtypeStruct((dm.T, 4 * dm.D), BF16), jax.ShapeDtypeStruct((8, dm.D), F32)],
        compiler_params=_cp(2),
    )(proj_a, dy_conv, conv_w)


def _interleave(gens):
    results = [None] * len(gens)
    live = list(range(len(gens)))
    while live:
        for idx in list(live):
            try:
                next(gens[idx])
            except StopIteration as done:
                results[idx] = done.value
                live.remove(idx)
    return results


def _group_chunks(dm):
    n = dm.NC - dm.C0
    return 3 if n % 3 == 0 else 1


def _group_masks(rows):
    ii = lax.broadcasted_iota(jnp.int32, (rows, rows), 0)
    jj = lax.broadcasted_iota(jnp.int32, (rows, rows), 1)
    same = jnp.right_shift(ii, CHUNK_SHIFT) == jnp.right_shift(jj, CHUNK_SHIFT)
    low, up = same & (jj <= ii), same & (jj >= ii)
    return low, same & (jj > ii), low.astype(BF16), up.astype(BF16)


def _chunk_totals(b, fwd):
    hk = b.shape[1]
    rows = [b[c * CHUNK + CHUNK - 1:(c + 1) * CHUNK] if fwd else b[c * CHUNK:c * CHUNK + 1]
            for c in range(b.shape[0] // CHUNK)]
    return jnp.concatenate([jnp.broadcast_to(r, (CHUNK, hk)) for r in rows], axis=0)


def _log_gate(lr_rows, w_ref, b_ref, first_group, hk):
    z = _dot(lr_rows, w_ref[...]) + b_ref[...]
    e = jnp.exp(-jnp.abs(z))
    g = (jnp.minimum(z, 0.0) - jnp.log(1.0 + e)) * (1.0 / GATE_NORMALIZER)
    dg_dz = jnp.where(z >= 0.0, e, 1.0) / (1.0 + e) * (1.0 / GATE_NORMALIZER)
    row = lax.broadcasted_iota(jnp.int32, (lr_rows.shape[0], hk), 0)
    pad = first_group & (row < PAD_ROWS)
    return jnp.where(pad, 0.0, g), jnp.where(pad, 0.0, dg_dz)


def _gla_fwd(proj_b, lr, wg_f, bg_f, wg_b, bg_b, gla_g, dm):
    lp, hk, hv, nc, c0, hw = dm.LP, dm.HK, dm.HV, dm.NC, dm.C0, dm.HW
    scale = hk ** -0.5
    gc = _group_chunks(dm)
    gr, ng = gc * CHUNK, (nc - c0) // gc

    def body(p_ref, lr_ref, wf_ref, bf_ref, wb_ref, bb_ref, gg_ref, o_ref, y_ref, oacc_f, oacc_b):
        low_incl, up_strict, ones_low, ones_up = _group_masks(gr)
        if c0 > 0:
            o_ref[0:c0 * CHUNK, :] = jnp.zeros((c0 * CHUNK, hv), BF16)
            y_ref[0:c0 * CHUNK, :] = jnp.zeros((c0 * CHUNK, hv), BF16)

        def group(gi, st, fwd):
            w_ref, b_ref, oacc = (wf_ref, bf_ref, oacc_f) if fwd else (wb_ref, bb_ref, oacc_b)
            r0 = pl.multiple_of((c0 + gi * gc) * CHUNK, CHUNK)
            blk = p_ref[pl.ds(r0, gr), :]
            q = blk[:, :hk].astype(F32) * scale
            k = blk[:, hk:2 * hk].astype(F32)
            v = blk[:, 2 * hk:2 * hk + hv]
            yield
            g, _ = _log_gate(lr_ref[pl.ds(r0, gr), :], w_ref, b_ref, gi == 0, hk)
            yield
            b = _dot_exact01(ones_low if fwd else ones_up, g)
            yield
            btot = _chunk_totals(b, fwd)
            qi = (q * jnp.exp(b)).astype(BF16)
            ki = (k * jnp.exp(-b)).astype(BF16)
            kd = (k * jnp.exp(btot - b)).astype(BF16)
            dec = jnp.exp(btot)
            a = _dot_nt(qi, ki)
            yield
            o = _dot(jnp.where(low_incl if fwd else up_strict, a, 0.0).astype(BF16), v)
            for c in (range(gc) if fwd else reversed(range(gc))):
                yield
                rows = slice(c * CHUNK, (c + 1) * CHUNK)
                oacc[pl.ds(r0 + c * CHUNK, CHUNK), :] = o[rows] + _dot_nt(qi[rows], st.astype(BF16))
                st = st * dec[c * CHUNK:c * CHUNK + 1] + _dot_tn(v[rows], kd[rows])
            return st

        def step(i, carry):
            st_f, st_b = carry
            return tuple(_interleave([group(i, st_f, True), group(ng - 1 - i, st_b, False)]))

        zero = jnp.zeros((hv, hk), F32)
        lax.fori_loop(0, ng, step, (zero, zero))

        def finish(i, carry):
            r0 = pl.multiple_of((c0 + i * gc) * CHUNK, CHUNK)
            o = oacc_f[pl.ds(r0, gr), :] + oacc_b[pl.ds(r0, gr), :]
            r = p_ref[pl.ds(r0, gr), 2 * hk + hv:].astype(F32)
            on = o * lax.rsqrt(jnp.mean(o * o, axis=-1, keepdims=True) + EPS) * gg_ref[...]
            o_ref[pl.ds(r0, gr), :] = o.astype(BF16)
            y_ref[pl.ds(r0, gr), :] = (on * r * _sigmoid(r)).astype(BF16)
            return carry

        lax.fori_loop(0, ng, finish, 0)

    head = lambda s, h: (s, h)
    wspec = pl.BlockSpec((LR_LANES, hk), lambda s, h: (0, h))
    bspec = pl.BlockSpec((1, hk), lambda s, h: (0, h))
    return pl.pallas_call(
        body, name="gla_fwd", grid=(dm.Bl, HEADS),
        in_specs=[pl.BlockSpec((lp, hw), head), pl.BlockSpec((lp, LR_LANES), lambda s, h: (s, 0)),
                  wspec, bspec, wspec, bspec, pl.BlockSpec((1, hv), lambda s, h: (0, 0))],
        out_specs=[pl.BlockSpec((lp, hv), head), pl.BlockSpec((lp, hv), head)],
        out_shape=[jax.ShapeDtypeStruct((dm.T, dm.DV), BF16), jax.ShapeDtypeStruct((dm.T, dm.DV), BF16)],
        scratch_shapes=[pltpu.VMEM((lp, hv), F32), pltpu.VMEM((lp, hv), F32)],
        compiler_params=_cp(2),
    )(proj_b, lr, wg_f, bg_f, wg_b, bg_b, gla_g)


def _gla_bwd(proj_b, lr, o_all, dy_gla, wg_f, bg_f, wg_b, bg_b, gla_g, dm):
    lp, hk, hv, nc, c0, hw = dm.LP, dm.HK, dm.HV, dm.NC, dm.C0, dm.HW
    scale = hk ** -0.5
    gc = _group_chunks(dm)
    gr, ng = gc * CHUNK, (nc - c0) // gc

    def body(p_ref, lr_ref, o_ref, dy_ref, wf_ref, bf_ref, wb_ref, bb_ref, gg_ref,
             d_ref, dlr_ref, gwf_ref, gbf_ref, gwb_ref, gbb_ref, ggg_ref,
             do_s, sf_all, sb_all, bf_s, bb_s, gsf_s, gsb_s, dqf_s, dqb_s, dkf_s, dkb_s, dvf_s, dvb_s, dlrf_s, dlrb_s):
        low_incl, up_strict, ones_low, ones_up = _group_masks(gr)
        h = pl.program_id(1)

        @pl.when(h == 0)
        def _():
            dlr_ref[...] = jnp.zeros_like(dlr_ref)

        if c0 > 0:
            zr = c0 * CHUNK
            d_ref[0:zr, :] = jnp.zeros((zr, hw), BF16)

        def norm_bwd(i, ggg):
            r0 = pl.multiple_of((c0 + i * gc) * CHUNK, CHUNK)
            o = o_ref[pl.ds(r0, gr), :].astype(F32)
            dy = dy_ref[pl.ds(r0, gr), :].astype(F32)
            r = p_ref[pl.ds(r0, gr), 2 * hk + hv:].astype(F32)
            rstd = lax.rsqrt(jnp.mean(o * o, axis=-1, keepdims=True) + EPS)
            ohat = o * rstd
            sg = _sigmoid(r)
            d_on = dy * (r * sg)
            d_ref[pl.ds(r0, gr), 2 * hk + hv:] = (dy * ohat * gg_ref[...] * (sg * (1.0 + r * (1.0 - sg)))).astype(BF16)
            d_oh = d_on * gg_ref[...]
            do_s[pl.ds(r0, gr), :] = (rstd * (d_oh - ohat * jnp.mean(d_oh * ohat, axis=-1, keepdims=True))).astype(BF16)
            return ggg + jnp.sum(d_on * ohat, axis=0, keepdims=True)

        ggg = lax.fori_loop(0, ng, norm_bwd, jnp.zeros((1, hv), F32))

        @pl.when((pl.program_id(0) == 0) & (h == 0))
        def _():
            ggg_ref[...] = jnp.zeros_like(ggg_ref)

        ggg_ref[0:1, :] += ggg

        def load(gi):
            r0 = pl.multiple_of((c0 + gi * gc) * CHUNK, CHUNK)
            blk = p_ref[pl.ds(r0, gr), :]
            return r0, blk[:, :hk].astype(F32) * scale, blk[:, hk:2 * hk].astype(F32), blk[:, 2 * hk:2 * hk + hv]

        chunk_totals = _chunk_totals

        def record(gi, st, fwd):
            w_ref, b_ref, s_all, b_s, gs_s = (wf_ref, bf_ref, sf_all, bf_s, gsf_s) if fwd else (wb_ref, bb_ref, sb_all, bb_s, gsb_s)
            r0, q, k, v = load(gi)
            yield
            g, dg_dz = _log_gate(lr_ref[pl.ds(r0, gr), :], w_ref, b_ref, gi == 0, hk)
            gs_s[pl.ds(r0, gr), :] = dg_dz
            yield
            b = _dot_exact01(ones_low if fwd else ones_up, g)
            yield
            b_s[pl.ds(r0, gr), :] = b
            btot = chunk_totals(b, fwd)
            kd = (k * jnp.exp(btot - b)).astype(BF16)
            dec = jnp.exp(btot)
            for c in (range(gc) if fwd else reversed(range(gc))):
                yield
                rows = slice(c * CHUNK, (c + 1) * CHUNK)
                s_all[c0 + gi * gc + c] = st
                st = st * dec[c * CHUNK:c * CHUNK + 1] + _dot_tn(v[rows], kd[rows])
            return st

        def record_step(i, carry):
            return tuple(_interleave([record(i, carry[0], True), record(ng - 1 - i, carry[1], False)]))

        zero = jnp.zeros((hv, hk), F32)
        lax.fori_loop(0, ng, record_step, (zero, zero))

        def grad(gi, carry, fwd):
            dst, gw, gb = carry
            w_ref, s_all, b_s, gs_s = (wf_ref, sf_all, bf_s, gsf_s) if fwd else (wb_ref, sb_all, bb_s, gsb_s)
            dq_s, dk_s, dv_s, dlr_s = (dqf_s, dkf_s, dvf_s, dlrf_s) if fwd else (dqb_s, dkb_s, dvb_s, dlrb_s)
            mask = low_incl if fwd else up_strict
            r0, q, k, v = load(gi)
            b = b_s[pl.ds(r0, gr), :]
            btot = chunk_totals(b, fwd)
            eb, enb, edb, dec = jnp.exp(b), jnp.exp(-b), jnp.exp(btot - b), jnp.exp(btot)
            qi_f, ki_f, kd_f = q * eb, k * enb, k * edb
            qi, ki, kd = qi_f.astype(BF16), ki_f.astype(BF16), kd_f.astype(BF16)
            do = do_s[pl.ds(r0, gr), :]
            a = _dot_nt(qi, ki)
            da = _dot_nt(do, v)
            yield
            a = jnp.where(mask, a, 0.0).astype(BF16)
            da = jnp.where(mask, da, 0.0).astype(BF16)
            dv = _dot_tn(a, do)
            dqi = _dot(da, ki)
            dki = _dot_tn(da, qi)
            dv_c, dqi_c, dkd_c, extra_c = [None] * gc, [None] * gc, [None] * gc, [None] * gc
            for c in (reversed(range(gc)) if fwd else range(gc)):
                yield
                rows = slice(c * CHUNK, (c + 1) * CHUNK)
                st = s_all[c0 + gi * gc + c]
                dsn_b = dst.astype(BF16)
                dec_c = dec[c * CHUNK:c * CHUNK + 1]
                dv_c[c] = dv[rows] + _dot_nt(kd[rows], dsn_b)
                dqi_c[c] = dqi[rows] + _dot(do[rows], st.astype(BF16))
                dkd_c[c] = _dot(v[rows], dsn_b)
                ddec = jnp.sum(st * dst, axis=0, keepdims=True)
                extra = jnp.sum(dkd_c[c] * kd_f[rows], axis=0, keepdims=True) + ddec * dec_c
                extra_c[c] = jnp.broadcast_to(extra, (CHUNK, hk))
                dst = dst * dec_c + _dot_tn(do[rows], qi[rows])
            yield
            dv, dqi = jnp.concatenate(dv_c, axis=0), jnp.concatenate(dqi_c, axis=0)
            dkd, extra = jnp.concatenate(dkd_c, axis=0), jnp.concatenate(extra_c, axis=0)
            dq_s[pl.ds(r0, gr), :] = dqi * eb * scale
            dk_s[pl.ds(r0, gr), :] = dki * enb + dkd * edb
            dv_s[pl.ds(r0, gr), :] = dv
            db = dqi * qi_f - dki * ki_f - dkd * kd_f
            dg = _dot_exact01(ones_up if fwd else ones_low, db) + extra
            yield
            dz = dg * gs_s[pl.ds(r0, gr), :]
            dz_b = dz.astype(BF16)
            dlr_s[pl.ds(r0, gr), :] = _dot_nt(dz_b, w_ref[...])
            return dst, gw + _dot_tn(lr_ref[pl.ds(r0, gr), :], dz_b), gb + jnp.sum(dz, axis=0, keepdims=True)

        def grad_step(i, carry):
            return tuple(_interleave([grad(ng - 1 - i, carry[0], True), grad(i, carry[1], False)]))

        init = (zero, jnp.zeros((LR_LANES, hk), F32), jnp.zeros((1, hk), F32))
        (_, gw_f, gb_f), (_, gw_b, gb_b) = lax.fori_loop(0, ng, grad_step, (init, init))
        for gw_ref, gb_ref, gw, gb in ((gwf_ref, gbf_ref, gw_f, gb_f), (gwb_ref, gbb_ref, gw_b, gb_b)):
            gw_ref[0] = gw
            gb_ref[0] = jnp.zeros((8, hk), F32)
            gb_ref[0, 0:1, :] = gb

        def combine(i, carry):
            r0 = pl.multiple_of((c0 + i * gc) * CHUNK, CHUNK)
            d_ref[pl.ds(r0, gr), 0:hk] = (dqf_s[pl.ds(r0, gr), :] + dqb_s[pl.ds(r0, gr), :]).astype(BF16)
            d_ref[pl.ds(r0, gr), hk:2 * hk] = (dkf_s[pl.ds(r0, gr), :] + dkb_s[pl.ds(r0, gr), :]).astype(BF16)
            d_ref[pl.ds(r0, gr), 2 * hk:2 * hk + hv] = (dvf_s[pl.ds(r0, gr), :] + dvb_s[pl.ds(r0, gr), :]).astype(BF16)
            dlr_ref[pl.ds(r0, gr), :] += dlrf_s[pl.ds(r0, gr), :] + dlrb_s[pl.ds(r0, gr), :]
            return carry

        lax.fori_loop(0, ng, combine, 0)

    head = lambda s, h: (s, h)
    wspec = pl.BlockSpec((LR_LANES, hk), lambda s, h: (0, h))
    bspec = pl.BlockSpec((1, hk), lambda s, h: (0, h))
    gwspec = pl.BlockSpec((1, LR_LANES, hk), lambda s, h: (s, 0, h))
    gbspec = pl.BlockSpec((1, 8, hk), lambda s, h: (s, 0, h))
    gw_shape = jax.ShapeDtypeStruct((dm.Bl, LR_LANES, dm.DK), F32)
    gb_shape = jax.ShapeDtypeStruct((dm.Bl, 8, dm.DK), F32)
    return pl.pallas_call(
        body, name="gla_bwd", grid=(dm.Bl, HEADS),
        in_specs=[pl.BlockSpec((lp, hw), head), pl.BlockSpec((lp, LR_LANES), lambda s, h: (s, 0)),
                  pl.BlockSpec((lp, hv), head), pl.BlockSpec((lp, hv), head),
                  wspec, bspec, wspec, bspec, pl.BlockSpec((1, hv), lambda s, h: (0, 0))],
        out_specs=[pl.BlockSpec((lp, hw), head), pl.BlockSpec((lp, LR_LANES), lambda s, h: (s, 0)),
                   gwspec, gbspec, gwspec, gbspec, pl.BlockSpec((8, hv), lambda s, h: (0, 0))],
        out_shape=[jax.ShapeDtypeStruct((dm.T, HEADS * hw), BF16), jax.ShapeDtypeStruct((dm.T, LR_LANES), F32),
                   gw_shape, gb_shape, gw_shape, gb_shape, jax.ShapeDtypeStruct((8, hv), F32)],
        scratch_shapes=[pltpu.VMEM((lp, hv), BF16), pltpu.VMEM((nc, hv, hk), F32), pltpu.VMEM((nc, hv, hk), F32)]
        + [pltpu.VMEM((lp, hk), F32)] * 8 + [pltpu.VMEM((lp, hv), F32)] * 2 + [pltpu.VMEM((lp, LR_LANES), F32)] * 2,
        compiler_params=_cp(2),
    )(proj_b, lr, o_all, dy_gla, wg_f, bg_f, wg_b, bg_b, gla_g)


def _out_merge(y_conv, y_gla, proj_c, w_oc, w_og, dm):
    d = dm.D
    tm = _pick(dm.T, 512, 16)

    def body(yc_ref, yg_ref, c_ref, woc_ref, wog_ref, pc_ref, pg_ref, m_ref):
        pc = _dot(yc_ref[...], woc_ref[...])
        pg = _dot(yg_ref[...], wog_ref[...])
        pc_ref[...] = pc.astype(BF16)
        pg_ref[...] = pg.astype(BF16)
        ma = c_ref[:, :d].astype(F32)
        mb = c_ref[:, d:].astype(F32)
        m_ref[...] = (_sigmoid(ma) * pc + _sigmoid(mb) * pg).astype(BF16)

    row = pl.BlockSpec((tm, d), lambda i: (i, 0))
    full = pl.BlockSpec((d, d), lambda i: (0, 0))
    act = jax.ShapeDtypeStruct((dm.T, d), BF16)
    return pl.pallas_call(
        body, name="out_merge", grid=(dm.T // tm,),
        in_specs=[row, row, pl.BlockSpec((tm, 2 * d), lambda i: (i, 0)), full, full],
        out_specs=[row, row, row], out_shape=[act, act, act], compiler_params=_cp(1),
    )(y_conv, y_gla, proj_c, w_oc, w_og)


def _final_fwd(merged, w_out, x, metapad, target, g_post, dm):
    tm, tps, d = dm.TM, dm.TPS, dm.D

    def body(m_ref, w_ref, x_ref, mp_ref, t_ref, g_ref, dout_ref, dy_ref, st_ref):
        i = pl.program_id(0)
        j = i % tps
        out = _dot(m_ref[...], w_ref[...])
        rstd = lax.rsqrt(jnp.mean(out * out, axis=-1, keepdims=True) + EPS)
        ohat = out * rstd
        h = jnp.where(j == 0, mp_ref[...], x_ref[0])
        y = h + ohat * g_ref[...]
        err = jnp.where(j == 0, 0.0, y - t_ref[0])
        dy = err * (1.0 / d)
        d_oh = dy * g_ref[...]
        dout_ref[...] = (rstd * (d_oh - ohat * jnp.mean(d_oh * ohat, axis=-1, keepdims=True))).astype(BF16)
        dy_ref[...] = dy

        @pl.when(i == 0)
        def _():
            st_ref[...] = jnp.zeros_like(st_ref)

        st_ref[0:1, :] += jnp.sum(dy * ohat, axis=0, keepdims=True)
        st_ref[1:2, :] += jnp.sum(err * err, axis=0, keepdims=True)

    row = pl.BlockSpec((tm, d), lambda i: (i, 0))
    tok = pl.BlockSpec((1, tm, d), lambda i: (i // tps, jnp.maximum(i % tps - 1, 0), 0))
    const = lambda r: pl.BlockSpec((r, d), lambda i: (0, 0))
    return pl.pallas_call(
        body, name="final_fwd", grid=(dm.Bl * tps,),
        in_specs=[row, const(d), tok, const(tm), tok, const(1)],
        out_specs=[row, row, const(8)],
        out_shape=[jax.ShapeDtypeStruct((dm.T, d), BF16), jax.ShapeDtypeStruct((dm.T, d), F32),
                   jax.ShapeDtypeStruct((8, d), F32)],
        compiler_params=_cp(1),
    )(merged, w_out, x, metapad, target, g_post)


def _merge_bwd(d_out, proj_c, p_conv, p_gla, w_out, w_oc, w_og, dm):
    d = dm.D
    tm = _pick(dm.T, 512, 16)

    def body(do_ref, c_ref, pc_ref, pg_ref, wo_ref, woc_ref, wog_ref, dpc_ref, dpg_ref, dc_ref, dyc_ref, dyg_ref):
        dmg = _dot_nt(do_ref[...], wo_ref[...])
        sa = _sigmoid(c_ref[:, :d].astype(F32))
        sb = _sigmoid(c_ref[:, d:].astype(F32))
        dpc = (dmg * sa).astype(BF16)
        dpg = (dmg * sb).astype(BF16)
        dpc_ref[...] = dpc
        dpg_ref[...] = dpg
        dc_ref[:, :d] = (dmg * pc_ref[...].astype(F32) * sa * (1.0 - sa)).astype(BF16)
        dc_ref[:, d:] = (dmg * pg_ref[...].astype(F32) * sb * (1.0 - sb)).astype(BF16)
        dyc_ref[...] = _dot_nt(dpc, woc_ref[...]).astype(BF16)
        dyg_ref[...] = _dot_nt(dpg, wog_ref[...]).astype(BF16)

    row = pl.BlockSpec((tm, d), lambda i: (i, 0))
    row2 = pl.BlockSpec((tm, 2 * d), lambda i: (i, 0))
    full = pl.BlockSpec((d, d), lambda i: (0, 0))
    act = jax.ShapeDtypeStruct((dm.T, d), BF16)
    return pl.pallas_call(
        body, name="merge_bwd", grid=(dm.T // tm,),
        in_specs=[row, row2, row, row, full, full, full],
        out_specs=[row, row, row2, row, row],
        out_shape=[act, act, jax.ShapeDtypeStruct((dm.T, 2 * d), BF16), act, act],
        compiler_params=_cp(1),
    )(d_out, proj_c, p_conv, p_gla, w_out, w_oc, w_og)


def _prenorm_bwd(du, dy, x, metapad, g_pre, dm):
    tm, tps, d = dm.TM, dm.TPS, dm.D

    def body(du_ref, dy_ref, x_ref, mp_ref, g_ref, gx_ref, dmeta_ref, gg_ref):
        i = pl.program_id(0)
        j = i % tps
        h = jnp.where(j == 0, mp_ref[...], x_ref[0])
        rstd = lax.rsqrt(jnp.mean(h * h, axis=-1, keepdims=True) + EPS)
        hhat = h * rstd
        dug = du_ref[...] * g_ref[...]
        dh = dy_ref[...] + rstd * (dug - hhat * jnp.mean(dug * hhat, axis=-1, keepdims=True))

        @pl.when(j == 0)
        def _():
            dmeta_ref[0] = dh

        @pl.when(j > 0)
        def _():
            gx_ref[0] = dh

        @pl.when(i == 0)
        def _():
            gg_ref[...] = jnp.zeros_like(gg_ref)

        gg_ref[0:1, :] += jnp.sum(du_ref[...] * hhat, axis=0, keepdims=True)

    row = pl.BlockSpec((tm, d), lambda i: (i, 0))
    tok = pl.BlockSpec((1, tm, d), lambda i: (i // tps, jnp.maximum(i % tps - 1, 0), 0))
    const = lambda r: pl.BlockSpec((r, d), lambda i: (0, 0))
    return pl.pallas_call(
        body, name="prenorm_bwd", grid=(dm.Bl * tps,),
        in_specs=[row, row, tok, const(tm), const(1)],
        out_specs=[tok, pl.BlockSpec((1, tm, d), lambda i: (i // tps, 0, 0)), const(8)],
        out_shape=[jax.ShapeDtypeStruct((dm.Bl, dm.S, d), F32), jax.ShapeDtypeStruct((dm.Bl, tm, d), F32),
                   jax.ShapeDtypeStruct((8, d), F32)],
        compiler_params=_cp(1),
    )(du, dy, x, metapad, g_pre)


def _adamw(partials, w, m, v, name, by_columns=False):
    r, c = w.shape
    n_parts = partials.shape[0]
    tr, tc = (r, _pick(c, 128, 128)) if by_columns else (_pick(r, 256, 16), c)

    def body(p_ref, w_ref, m_ref, v_ref, g_ref, d_ref, nm_ref, nv_ref):
        g = p_ref[0].astype(F32)
        for j in range(1, n_parts):
            g = g + p_ref[j].astype(F32)
        m2 = ADAM_B1 * m_ref[...] + (1.0 - ADAM_B1) * g
        v2 = ADAM_B2 * v_ref[...] + (1.0 - ADAM_B2) * (g * g)
        m_hat = m2 / (1.0 - ADAM_B1 ** ADAM_STEP)
        v_hat = v2 / (1.0 - ADAM_B2 ** ADAM_STEP)
        g_ref[...] = g
        d_ref[...] = -ADAM_LR * (m_hat / (jnp.sqrt(v_hat) + ADAM_EPS) + ADAM_WD * w_ref[...])
        nm_ref[...] = m2
        nv_ref[...] = v2

    at = (lambda i: (0, i)) if by_columns else (lambda i: (i, 0))
    tile = pl.BlockSpec((tr, tc), at)
    out = jax.ShapeDtypeStruct((r, c), F32)
    return pl.pallas_call(
        body, name=name, grid=(c // tc if by_columns else r // tr,),
        in_specs=[pl.BlockSpec((n_parts, tr, tc), lambda i: (0,) + at(i)), tile, tile, tile],
        out_specs=[tile, tile, tile, tile], out_shape=[out, out, out, out], compiler_params=_cp(1),
    )(partials, w, m, v)


def _pack_rows(wt, dm):
    d, dk, hk, hv, cw, nj = dm.D, dm.DK, dm.HK, dm.HV, dm.CW, dm.NJ
    a = wt[:4 * d].reshape(4, nj, cw, d).transpose(1, 0, 2, 3).reshape(4 * d, d)
    b = jnp.concatenate([wt[4 * d:4 * d + dk].reshape(HEADS, hk, d), wt[4 * d + dk:5 * d].reshape(HEADS, hk, d),
                         wt[5 * d:6 * d].reshape(HEADS, hv, d), wt[6 * d:7 * d].reshape(HEADS, hv, d)],
                        axis=1).reshape(3 * d, d)
    c = wt[7 * d + 2 * RANK:]
    lr = jnp.pad(wt[7 * d:7 * d + 2 * RANK], ((0, LR_LANES - 2 * RANK), (0, 0)))
    return a, b, c, lr


def _unpack_rows(a, b, c, lr, dm):
    d, hk, hv, cw, nj, hw = dm.D, dm.HK, dm.HV, dm.CW, dm.NJ, dm.HW
    conv = a.reshape(nj, 4, cw, d).transpose(1, 0, 2, 3).reshape(4 * d, d)
    heads = b.reshape(HEADS, hw, d)
    q = heads[:, :hk].reshape(HEADS * hk, d)
    k = heads[:, hk:2 * hk].reshape(HEADS * hk, d)
    v = heads[:, 2 * hk:2 * hk + hv].reshape(HEADS * hv, d)
    r = heads[:, 2 * hk + hv:].reshape(HEADS * hv, d)
    return jnp.concatenate([conv, q, k, v, r, lr[:2 * RANK], c], axis=0)


def _to_blob(pieces, dtype, row_mult):
    lead = pieces[0].shape[0]
    flat = jnp.concatenate([p.reshape(lead, -1).astype(dtype) for p in pieces], axis=1)
    unit = row_mult * BLOB_LANES
    padded = -(-flat.shape[1] // unit) * unit
    flat = jnp.pad(flat, ((0, 0), (0, padded - flat.shape[1])))
    return flat.reshape(lead, padded // BLOB_LANES, BLOB_LANES)


def _from_blob(blob, shapes):
    lead = blob.shape[0]
    flat = blob.reshape(lead, -1)
    out, off = [], 0
    for shp in shapes:
        size = int(np.prod(shp))
        out.append(flat[:, off:off + size].reshape((lead,) + tuple(shp)))
        off += size
    return out


def _local_step(x, target, meta, g_pre, wt_in, conv_w, wg_f, bg_f, wg_b, bg_b, gla_g, out_weights, g_post,
                on_matrix_grads=None):
    bl, s, d = x.shape
    dm = _Dims(bl, s, d)
    metapad = jnp.concatenate([jnp.zeros((dm.TM - N_META, d), F32), meta], axis=0)
    wta, wtb, wtc, wtlr = _pack_rows(wt_in, dm)
    wgp_f = jnp.pad(wg_f, ((0, LR_LANES - RANK), (0, 0))).astype(BF16)
    wgp_b = jnp.pad(wg_b, ((RANK, LR_LANES - 2 * RANK), (0, 0))).astype(BF16)

    u = _prenorm(x, metapad, g_pre, dm)
    proj_a, proj_b, proj_c, lr = _inproj(u, [wta, wtb, wtc, wtlr], dm)
    y_conv = _conv_fwd(proj_a, conv_w, dm)
    o_all, y_gla = _gla_fwd(proj_b, lr, wgp_f, bg_f, wgp_b, bg_b, gla_g, dm)
    w_oc, w_og, w_out = out_weights(y_conv) if callable(out_weights) else out_weights
    p_conv, p_gla, merged = _out_merge(y_conv, y_gla, proj_c, w_oc, w_og, dm)
    d_out, dy, stats = _final_fwd(merged, w_out, x, metapad, target, g_post, dm)
    loss = 0.5 / d * jnp.sum(stats[1])

    d_pc, d_pg, d_c, dy_conv, dy_gla = _merge_bwd(d_out, proj_c, p_conv, p_gla, w_out, w_oc, w_og, dm)
    g_out = _matmul_tn(merged, d_out, BF16, "grad_w_out")
    g_oc = _matmul_tn(y_conv, d_pc, BF16, "grad_w_out_conv")
    g_og = _matmul_tn(y_gla, d_pg, BF16, "grad_w_out_gla")
    d_a, g_conv = _conv_bwd(proj_a, dy_conv, conv_w, dm)
    d_b, d_lr, gwp_f, gbp_f, gwp_b, gbp_b, g_gla = _gla_bwd(proj_b, lr, o_all, dy_gla, wgp_f, bg_f, wgp_b, bg_b, gla_g, dm)
    g_in = _unpack_rows(_matmul_tn(d_a, u, BF16, "grad_w_in_conv"), _matmul_tn(d_b, u, BF16, "grad_w_in_gla"),
                        _matmul_tn(d_c, u, BF16, "grad_w_in_merge"), _matmul_tn(d_lr, u, BF16, "grad_w_in_gate"), dm)
    if on_matrix_grads is not None:
        wtlr = wtlr + on_matrix_grads(dict(w_in=g_in, w_out_conv=g_oc, w_out_gla=g_og, w_merge_out=g_out)).astype(BF16)
    du = _grad_u([d_a, d_b, d_c, d_lr], [wta, wtb, wtc, wtlr], dm)
    grad_x, d_meta, g_pre_rows = _prenorm_bwd(du, dy, x, metapad, g_pre, dm)

    grads = dict(
        meta_tokens=jnp.sum(d_meta[:, dm.TM - N_META:, :], axis=0), norm_pre=g_pre_rows[0:1], w_in=g_in,
        conv_w=g_conv[0:3], w_gate_fwd=jnp.sum(gwp_f, axis=0)[:RANK], b_gate_fwd=jnp.sum(gbp_f, axis=0)[0:1],
        w_gate_bwd=jnp.sum(gwp_b, axis=0)[RANK:2 * RANK], b_gate_bwd=jnp.sum(gbp_b, axis=0)[0:1],
        gla_norm=g_gla[0:1], w_out_conv=g_oc, w_out_gla=g_og, w_merge_out=g_out, norm_post=stats[0:1])
    return loss, grad_x, grads


MATRICES = ("w_out_conv", "w_out_gla", "w_merge_out")
SMALL_SHARDED = ("meta_tokens", "conv_w", "w_gate_fwd", "w_gate_bwd")
REPLICATED = ("norm_pre", "b_gate_fwd", "b_gate_bwd", "gla_norm", "norm_post")
NAMES = ("meta_tokens", "norm_pre", "w_in", "conv_w", "w_gate_fwd", "b_gate_fwd", "w_gate_bwd", "b_gate_bwd", "gla_norm",
         "w_out_conv", "w_out_gla", "w_merge_out", "norm_post")
DEPTH_AXIS = ("w_in", "conv_w", "w_gate_fwd", "w_gate_bwd") + MATRICES


def _cols_to_devices(g):
    r, c = g.shape
    return g.reshape(r, N_DEV, c // N_DEV).transpose(1, 0, 2)


def _cols_from_devices(parts):
    n, r, c = parts.shape
    return parts.transpose(1, 0, 2).reshape(r, n * c)


def kernel(x, meta_tokens, norm_pre, w_in, conv_w, w_gate_fwd, b_gate_fwd, w_gate_bwd, b_gate_bwd, gla_norm, w_out_conv, w_out_gla, w_merge_out, norm_post, loss_target, m_meta_tokens, m_norm_pre, m_w_in, m_conv_w, m_w_gate_fwd, m_b_gate_fwd, m_w_gate_bwd, m_b_gate_bwd, m_gla_norm, m_w_out_conv, m_w_out_gla, m_w_merge_out, m_norm_post, v_meta_tokens, v_norm_pre, v_w_in, v_conv_w, v_w_gate_fwd, v_b_gate_fwd, v_w_gate_bwd, v_b_gate_bwd, v_gla_norm, v_w_out_conv, v_w_out_gla, v_w_merge_out, v_norm_post):
    w = dict(meta_tokens=meta_tokens, norm_pre=norm_pre, w_in=w_in[0], conv_w=conv_w[0], w_gate_fwd=w_gate_fwd[0],
             b_gate_fwd=b_gate_fwd, w_gate_bwd=w_gate_bwd[0], b_gate_bwd=b_gate_bwd, gla_norm=gla_norm,
             w_out_conv=w_out_conv[0], w_out_gla=w_out_gla[0], w_merge_out=w_merge_out[0], norm_post=norm_post)
    m = dict(meta_tokens=m_meta_tokens, norm_pre=m_norm_pre, w_in=m_w_in[0], conv_w=m_conv_w[0], w_gate_fwd=m_w_gate_fwd[0],
             b_gate_fwd=m_b_gate_fwd, w_gate_bwd=m_w_gate_bwd[0], b_gate_bwd=m_b_gate_bwd, gla_norm=m_gla_norm,
             w_out_conv=m_w_out_conv[0], w_out_gla=m_w_out_gla[0], w_merge_out=m_w_merge_out[0], norm_post=m_norm_post)
    v = dict(meta_tokens=v_meta_tokens, norm_pre=v_norm_pre, w_in=v_w_in[0], conv_w=v_conv_w[0], w_gate_fwd=v_w_gate_fwd[0],
             b_gate_fwd=v_b_gate_fwd, w_gate_bwd=v_w_gate_bwd[0], b_gate_bwd=v_b_gate_bwd, gla_norm=v_gla_norm,
             w_out_conv=v_w_out_conv[0], w_out_gla=v_w_out_gla[0], w_merge_out=v_w_merge_out[0], norm_post=v_norm_post)
    d = x.shape[-1]

    small_blob = _to_blob([w[n][None] for n in SMALL_SHARDED], F32, SMALL_ROWS)[0]
    wt_all, small_all = _gather_two_level([w["w_in"].T.astype(BF16), small_blob], "gather_weights")
    _, late_weights = _exchange_start([w[n].astype(BF16) for n in MATRICES], [], small_all, "gather_out_weights_start")
    wt_in = wt_all.reshape(-1, d)
    small = {n: _cols_from_devices(p) for n, p in zip(SMALL_SHARDED, _from_blob(small_all, [w[n].shape for n in SMALL_SHARDED]))}

    def out_weights(after):
        return tuple(a.reshape(-1, d) for a in _exchange_wait(late_weights, after, "gather_out_weights_wait"))

    pending = []

    def on_matrix_grads(g):
        to_send = [g[n].astype(BF16).reshape(N_DEV, -1, d) for n in ("w_in",) + MATRICES]
        token, state = _exchange_start([], to_send, None, "exchange_grads_start")
        pending.append(state)
        return token

    loss, grad_x, grads = _local_step(
        x, loss_target, small["meta_tokens"], norm_pre, wt_in, small["conv_w"], small["w_gate_fwd"], b_gate_fwd,
        small["w_gate_bwd"], b_gate_bwd, gla_norm, out_weights, norm_post, on_matrix_grads)
    loss = lax.psum(loss, ("x", "y", "c"))
    received = _exchange_wait(pending[0], grad_x, "exchange_grads_wait")

    small_send = _to_blob([_cols_to_devices(grads[n]) for n in SMALL_SHARDED], F32, SMALL_ROWS)
    repl_blob = _to_blob([grads[n][None] for n in REPLICATED], F32, SMALL_ROWS)[0]
    repl_all, small_recv = _exchange([repl_blob], [small_send], "exchange_small_grads")

    results = {"w_in": [r.T for r in _adamw(received[0], w["w_in"].T, m["w_in"].T, v["w_in"].T, "adamw_w_in", by_columns=True)]}
    for n, partials in zip(MATRICES, received[1:4]):
        results[n] = _adamw(partials, w[n], m[n], v[n], "adamw_" + n)
    for names, partials, tag in ((SMALL_SHARDED, small_recv, "small"), (REPLICATED, repl_all, "replicated")):
        blobs = [_to_blob([t[n][None] for n in names], F32, SMALL_ROWS)[0] for t in (w, m, v)]
        res = [_from_blob(r[None], [w[n].shape for n in names]) for r in _adamw(partials, *blobs, name="adamw_" + tag)]
        for i, n in enumerate(names):
            results[n] = [r[i][0] for r in res]
    lead = lambda n, t: t[None] if n in DEPTH_AXIS else t
    return (loss, grad_x, *[lead(n, results[n][i]) for i in range(4) for n in NAMES])
```

```python
import functools

import jax
import jax.numpy as jnp
import numpy as np
from jax import lax
from jax.experimental import pallas as pl
from jax.experimental.pallas import tpu as pltpu

F32 = jnp.float32
BF16 = jnp.bfloat16
MESH = pl.DeviceIdType.MESH

N_META = 16
CHUNK = 64
CHUNK_SHIFT = 6
HEADS = 4
RANK = 16
LR_LANES = 128
PAD_ROWS = CHUNK - N_META
EPS = 1e-6
GATE_NORMALIZER = 16.0
N_DEV = 8
ADAM_LR, ADAM_B1, ADAM_B2, ADAM_EPS, ADAM_WD, ADAM_STEP = 0.001, 0.9, 0.999, 1e-08, 0.01, 10
VMEM_LIMIT_BYTES = 56 * 1024 * 1024
BLOB_LANES = 512
SMALL_ROWS = 16


class _Dims:
    def __init__(self, bl, s, d):
        self.Bl, self.S, self.D = bl, s, d
        self.TM = 128 if s % 128 == 0 else CHUNK
        self.LP = self.TM + s
        self.T = bl * self.LP
        self.TPS = self.LP // self.TM
        self.NC = self.LP // CHUNK
        self.C0 = (self.TM - CHUNK) // CHUNK
        self.DK, self.DV = d // 2, d
        self.HK, self.HV = self.DK // HEADS, self.DV // HEADS
        self.HW = 2 * self.HK + 2 * self.HV
        self.CW = 256 if d % 256 == 0 and d > 256 else d // 4
        self.NJ = d // self.CW


def _pick(n, target, mult):
    t = min(n, target)
    while t >= mult:
        if n % t == 0 and t % mult == 0:
            return t
        t -= mult
    return n


def _cp(n_axes):
    return pltpu.CompilerParams(dimension_semantics=("arbitrary",) * n_axes, vmem_limit_bytes=VMEM_LIMIT_BYTES)


def _sigmoid(x):
    return 1.0 / (1.0 + jnp.exp(-x))


def _dot(a, b):
    return jnp.dot(a, b, preferred_element_type=F32)


def _dot_nt(a, b):
    return lax.dot_general(a, b, (((1,), (1,)), ((), ())), preferred_element_type=F32)


def _dot_tn(a, b):
    return lax.dot_general(a, b, (((0,), (0,)), ((), ())), preferred_element_type=F32)


def _dot_exact01(m01, x):
    hi = x.astype(BF16)
    lo = (x - hi.astype(F32)).astype(BF16)
    return _dot(m01, hi) + _dot(m01, lo)


def _exchange(gathers, scatters, name):
    arrays = list(gathers) + list(scatters)
    n, ng = len(arrays), len(gathers)

    def body(*refs):
        ins, outs = refs[:n], refs[n:2 * n]
        send_sems, recv_sems, local_sems = refs[2 * n:]
        x, y, c = lax.axis_index("x"), lax.axis_index("y"), lax.axis_index("c")
        me = 4 * x + 2 * y + c
        started = []
        for t in range(n):
            src, dst = ins[t], outs[t]
            own = pltpu.make_async_copy(src if t < ng else src.at[me], dst.at[me], local_sems.at[t])
            own.start()
            started.append(own)
            for k, pos, peer in _peers(x, y, c):
                cp = pltpu.make_async_remote_copy(
                    src_ref=src if t < ng else src.at[peer], dst_ref=dst.at[me],
                    send_sem=send_sems.at[t * (N_DEV - 1) + k - 1], recv_sem=recv_sems.at[t * (N_DEV - 1) + k - 1],
                    device_id=pos, device_id_type=MESH)
                cp.start()
                started.append(cp)
        for cp in started:
            cp.wait()

    out_shape = [jax.ShapeDtypeStruct((N_DEV,) + a.shape[-2:], a.dtype) for a in arrays]
    any_spec = pl.BlockSpec(memory_space=pl.ANY)
    return pl.pallas_call(
        body, name=name, out_shape=out_shape, in_specs=[any_spec] * n, out_specs=[any_spec] * n,
        scratch_shapes=[pltpu.SemaphoreType.DMA((n * (N_DEV - 1),)), pltpu.SemaphoreType.DMA((n * (N_DEV - 1),)),
                        pltpu.SemaphoreType.DMA((n,))],
        compiler_params=pltpu.CompilerParams(has_side_effects=True),
    )(*arrays)


def _gather_two_level(arrays, name):
    n = len(arrays)
    per = N_DEV - 1

    def body(*refs):
        ins, outs = refs[:n], refs[n:2 * n]
        send_sems, recv_sems, local_sems = refs[2 * n:]
        x, y, c = lax.axis_index("x"), lax.axis_index("y"), lax.axis_index("c")
        sibling = (x, y, 1 - c)
        chips = [(1 - x, y), (x, 1 - y), (1 - x, 1 - y)]
        index = lambda px, py, pc: 4 * px + 2 * py + pc

        def copy(t, k, block, to, from_input=False):
            slab = outs[t].at[index(*block)]
            return pltpu.make_async_remote_copy(
                src_ref=ins[t] if from_input else slab, dst_ref=slab, send_sem=send_sems.at[t * per + k],
                recv_sem=recv_sems.at[t * per + k], device_id=to, device_id_type=MESH)

        own, sent = [], []
        for t in range(n):
            own.append(pltpu.make_async_copy(ins[t], outs[t].at[index(x, y, c)], local_sems.at[t]))
            own[-1].start()
            first = [copy(t, 0, (x, y, c), sibling, True)]
            first += [copy(t, 1 + j, (x, y, c), (*chip, c), True) for j, chip in enumerate(chips)]
            for cp in first:
                cp.start()
            sent += first
        for t in range(n):
            for j, chip in enumerate(chips):
                copy(t, 1 + j, (*chip, c), (x, y, c)).wait_recv()
                sent.append(copy(t, 4 + j, (*chip, c), sibling))
                sent[-1].start()
        for t in range(n):
            copy(t, 0, sibling, (x, y, c)).wait_recv()
            for j, chip in enumerate(chips):
                copy(t, 4 + j, (*chip, 1 - c), (x, y, c)).wait_recv()
        for cp in sent:
            cp.wait_send()
        for cp in own:
            cp.wait()

    out_shape = [jax.ShapeDtypeStruct((N_DEV,) + a.shape, a.dtype) for a in arrays]
    any_spec = pl.BlockSpec(memory_space=pl.ANY)
    return pl.pallas_call(
        body, name=name, out_shape=out_shape, in_specs=[any_spec] * n, out_specs=[any_spec] * n,
        scratch_shapes=[pltpu.SemaphoreType.DMA((n * per,)), pltpu.SemaphoreType.DMA((n * per,)),
                        pltpu.SemaphoreType.DMA((n,))],
        compiler_params=pltpu.CompilerParams(has_side_effects=True),
    )(*arrays)


def _peers(x, y, c):
    out = []
    for k in range(1, N_DEV):
        px = 1 - x if (k >> 2) & 1 else x
        py = 1 - y if (k >> 1) & 1 else y
        pc = 1 - c if k & 1 else c
        out.append((k, (px, py, pc), 4 * px + 2 * py + pc))
    return out


def _exchange_start(gathers, scatters, after, name):
    arrays = list(gathers) + list(scatters)
    n, ng = len(arrays), len(gathers)
    hbm = pl.BlockSpec(memory_space=pltpu.HBM)
    sem = pl.BlockSpec(memory_space=pltpu.SEMAPHORE)

    extra = [] if after is None else [after]
    ne = len(extra)

    def body(*refs):
        ins, lands = refs[:n], refs[n:2 * n]
        send_sems, recv_sems = refs[2 * n + ne], refs[2 * n + ne + 1]
        token = refs[4 * n + ne + 2]
        x, y, c = lax.axis_index("x"), lax.axis_index("y"), lax.axis_index("c")
        me = 4 * x + 2 * y + c
        for t in range(n):
            for k, pos, peer in _peers(x, y, c):
                pltpu.make_async_remote_copy(
                    src_ref=ins[t] if t < ng else ins[t].at[peer], dst_ref=lands[t].at[me],
                    send_sem=send_sems.at[t * (N_DEV - 1) + k - 1], recv_sem=recv_sems.at[t * (N_DEV - 1) + k - 1],
                    device_id=pos, device_id_type=MESH).start()
        token[...] = jnp.zeros_like(token)

    me = 4 * lax.axis_index("x") + 2 * lax.axis_index("y") + lax.axis_index("c")
    lands = [lax.dynamic_update_index_in_dim(lax.empty((N_DEV,) + a.shape[-2:], a.dtype),
                                             a if t < ng else lax.dynamic_index_in_dim(a, me, 0, keepdims=False), me, 0)
             for t, a in enumerate(arrays)]
    operands = [pltpu.with_memory_space_constraint(a, pltpu.HBM) for a in arrays + lands]
    sems = pltpu.SemaphoreType.DMA((n * (N_DEV - 1),))
    res = pl.pallas_call(
        body, name=name,
        out_shape=(sems, sems, *[pltpu.HBM(a.shape, a.dtype) for a in arrays + lands], jax.ShapeDtypeStruct((8, 128), F32)),
        in_specs=[hbm] * (2 * n) + [pl.BlockSpec(memory_space=pl.ANY)] * ne,
        out_specs=(sem, sem, *[hbm] * (2 * n), pl.BlockSpec(memory_space=pltpu.VMEM)),
        input_output_aliases={i: 2 + i for i in range(2 * n)},
        compiler_params=pltpu.CompilerParams(has_side_effects=pltpu.SideEffectType.DATAFLOW_SIDE_EFFECTING),
    )(*operands, *extra)
    return res[-1][0, 0], (ng, res[0], res[1], list(res[2:2 + n]), list(res[2 + n:2 + 2 * n]))


def _exchange_wait(state, after, name):
    ng, send_sems, recv_sems, sent, lands = state
    n = len(sent)
    hbm = pl.BlockSpec(memory_space=pltpu.HBM)
    sem = pl.BlockSpec(memory_space=pltpu.SEMAPHORE)

    def body(*refs):
        ins, land_refs = refs[:n], refs[n:2 * n]
        send_ref, recv_ref = refs[2 * n], refs[2 * n + 1]
        x, y, c = lax.axis_index("x"), lax.axis_index("y"), lax.axis_index("c")
        me = 4 * x + 2 * y + c
        for t in range(n):
            for k, pos, peer in _peers(x, y, c):
                cp = pltpu.make_async_remote_copy(
                    src_ref=ins[t] if t < ng else ins[t].at[peer], dst_ref=land_refs[t].at[me],
                    send_sem=send_ref.at[t * (N_DEV - 1) + k - 1], recv_sem=recv_ref.at[t * (N_DEV - 1) + k - 1],
                    device_id=pos, device_id_type=MESH)
                cp.wait_send()
                cp.wait_recv()

    res = pl.pallas_call(
        body, name=name, out_shape=tuple(pltpu.HBM(a.shape, a.dtype) for a in sent + lands),
        in_specs=[hbm] * (2 * n) + [sem, sem, pl.BlockSpec(memory_space=pl.ANY)], out_specs=tuple([hbm] * (2 * n)),
        input_output_aliases={i: i for i in range(2 * n)},
        compiler_params=pltpu.CompilerParams(has_side_effects=pltpu.SideEffectType.DATAFLOW_SIDE_EFFECTING),
    )(*sent, *lands, send_sems, recv_sems, after)
    return list(res[n:])


def _prenorm(x, metapad, g_pre, dm):
    tm, tps, d = dm.TM, dm.TPS, dm.D

    def body(x_ref, mp_ref, g_ref, u_ref):
        j = pl.program_id(0) % tps
        h = jnp.where(j == 0, mp_ref[...], x_ref[0])
        r = lax.rsqrt(jnp.mean(h * h, axis=-1, keepdims=True) + EPS)
        u_ref[...] = (h * r * g_ref[...]).astype(BF16)

    return pl.pallas_call(
        body, name="prenorm", grid=(dm.Bl * tps,),
        in_specs=[pl.BlockSpec((1, tm, d), lambda i: (i // tps, jnp.maximum(i % tps - 1, 0), 0)),
                  pl.BlockSpec((tm, d), lambda i: (0, 0)),
                  pl.BlockSpec((1, d), lambda i: (0, 0))],
        out_specs=pl.BlockSpec((tm, d), lambda i: (i, 0)),
        out_shape=jax.ShapeDtypeStruct((dm.T, d), BF16), compiler_params=_cp(1),
    )(x, metapad, g_pre)


def _matmul_tn(a, b, out_dtype, name, tt=2304, tn=1024, tk=1024):
    t, k = a.shape
    n = b.shape[1]
    tt, tn, tk = _pick(t, tt, 16), _pick(n, tn, 128), _pick(k, tk, 128)
    nt = t // tt

    def body(a_ref, b_ref, o_ref, acc):
        p = _dot_tn(a_ref[...].astype(BF16), b_ref[...].astype(BF16))
        i = pl.program_id(2)

        @pl.when(i == 0)
        def _():
            acc[...] = p

        @pl.when(i > 0)
        def _():
            acc[...] += p

        @pl.when(i == nt - 1)
        def _():
            o_ref[...] = acc[...].astype(out_dtype)

    return pl.pallas_call(
        body, name=name, grid=(k // tk, n // tn, nt),
        in_specs=[pl.BlockSpec((tt, tk), lambda kk, j, i: (i, kk)), pl.BlockSpec((tt, tn), lambda kk, j, i: (i, j))],
        out_specs=pl.BlockSpec((tk, tn), lambda kk, j, i: (kk, j)),
        out_shape=jax.ShapeDtypeStruct((k, n), out_dtype), scratch_shapes=[pltpu.VMEM((tk, tn), F32)],
        compiler_params=_cp(3),
    )(a, b)


def _load_resident(hbm_refs, vmem_refs, sems):
    @pl.when(pl.program_id(0) == 0)
    def _():
        copies = [pltpu.make_async_copy(h, v, sems.at[i]) for i, (h, v) in enumerate(zip(hbm_refs, vmem_refs))]
        for cp in copies:
            cp.start()
        for cp in copies:
            cp.wait()


def _inproj(u, wts, dm):
    t, d = u.shape
    tm = _pick(t, 512, 16)
    np_ = len(wts)
    cn = 1024

    def body(*refs):
        u_ref, w_hbm, outs = refs[0], refs[1:1 + np_], refs[1 + np_:1 + 2 * np_]
        w_vmem, sems = refs[1 + 2 * np_:1 + 3 * np_], refs[1 + 3 * np_]
        _load_resident(w_hbm, w_vmem, sems)
        ut = u_ref[...]
        for w, o_ref in zip(w_vmem, outs):
            n = w.shape[0]
            step = cn if n % cn == 0 else n
            for j in range(0, n, step):
                o_ref[:, j:j + step] = _dot_nt(ut, w[j:j + step, :]).astype(BF16)

    return pl.pallas_call(
        body, name="inproj", grid=(t // tm,),
        in_specs=[pl.BlockSpec((tm, d), lambda i: (i, 0))] + [pl.BlockSpec(memory_space=pl.ANY)] * np_,
        out_specs=[pl.BlockSpec((tm, w.shape[0]), lambda i: (i, 0)) for w in wts],
        out_shape=[jax.ShapeDtypeStruct((t, w.shape[0]), BF16) for w in wts],
        scratch_shapes=[pltpu.VMEM(w.shape, BF16) for w in wts] + [pltpu.SemaphoreType.DMA((np_,))],
        compiler_params=_cp(1),
    )(u, *wts)


def _grad_u(d_parts, wts, dm):
    t = d_parts[0].shape[0]
    d = wts[0].shape[1]
    tm = _pick(t, 512, 16)
    np_ = len(wts)

    def body(*refs):
        d_refs, w_hbm, o_ref = refs[:np_], refs[np_:2 * np_], refs[2 * np_]
        w_vmem, sems = refs[2 * np_ + 1:3 * np_ + 1], refs[3 * np_ + 1]
        _load_resident(w_hbm, w_vmem, sems)
        o_ref[...] = _dot(d_refs[0][...].astype(BF16), w_vmem[0][...])
        for a_ref, w in zip(d_refs[1:], w_vmem[1:]):
            o_ref[...] += _dot(a_ref[...].astype(BF16), w[...])

    return pl.pallas_call(
        body, name="grad_u", grid=(t // tm,),
        in_specs=[pl.BlockSpec((tm, a.shape[1]), lambda i: (i, 0)) for a in d_parts] + [pl.BlockSpec(memory_space=pl.ANY)] * np_,
        out_specs=pl.BlockSpec((tm, d), lambda i: (i, 0)), out_shape=jax.ShapeDtypeStruct((t, d), F32),
        scratch_shapes=[pltpu.VMEM(w.shape, BF16) for w in wts] + [pltpu.SemaphoreType.DMA((np_,))],
        compiler_params=_cp(1),
    )(*d_parts, *wts)


def _conv_rows(dm):
    return _pick(dm.LP, 256, 16)


def _shifted(m, prev_row, next_row, rows):
    row = lax.broadcasted_iota(jnp.int32, m.shape, 0)
    m_prev = jnp.where(row == 0, prev_row, pltpu.roll(m, 1, 0))
    m_next = jnp.where(row == rows - 1, next_row, pltpu.roll(m, rows - 1, 0))
    return m_prev, m_next


def _conv_fwd(proj_a, conv_w, dm):
    lp, cw, rc = dm.LP, dm.CW, _conv_rows(dm)
    nchunk = lp // rc

    def body(p_ref, w_ref, y_ref):
        w0, w1, w2 = w_ref[0:1, :], w_ref[1:2, :], w_ref[2:3, :]

        def chunk(ci, carry):
            r0 = pl.multiple_of(ci * rc, rc)
            blk = p_ref[pl.ds(r0, rc), :].astype(F32)
            cb, cc, cx, cz = (blk[:, i * cw:(i + 1) * cw] for i in range(4))
            m = cc * cx
            rp = pl.multiple_of(jnp.maximum(r0 - 16, 0), 16)
            rn = pl.multiple_of(jnp.minimum(r0 + rc, lp - 16), 16)
            pv = p_ref[pl.ds(rp, 16), cw:3 * cw].astype(F32)
            nx = p_ref[pl.ds(rn, 16), cw:3 * cw].astype(F32)
            prev_row = jnp.where(ci > 0, pv[15:16, :cw] * pv[15:16, cw:], 0.0)
            next_row = jnp.where(ci < nchunk - 1, nx[0:1, :cw] * nx[0:1, cw:], 0.0)
            m_prev, m_next = _shifted(m, prev_row, next_row, rc)
            s = w0 * m_prev + w1 * m + w2 * m_next
            y_ref[pl.ds(r0, rc), :] = (cb * s * (cz * _sigmoid(cz))).astype(BF16)
            return carry

        lax.fori_loop(0, nchunk, chunk, 0)

    return pl.pallas_call(
        body, name="conv_fwd", grid=(dm.Bl, dm.NJ),
        in_specs=[pl.BlockSpec((lp, 4 * cw), lambda s, j: (s, j)), pl.BlockSpec((3, cw), lambda s, j: (0, j))],
        out_specs=pl.BlockSpec((lp, cw), lambda s, j: (s, j)),
        out_shape=jax.ShapeDtypeStruct((dm.T, dm.D), BF16), compiler_params=_cp(2),
    )(proj_a, conv_w)


def _conv_bwd(proj_a, dy_conv, conv_w, dm):
    lp, cw, rc = dm.LP, dm.CW, _conv_rows(dm)
    nchunk = lp // rc

    def body(p_ref, dy_ref, w_ref, d_ref, gw_ref):
        w0, w1, w2 = w_ref[0:1, :], w_ref[1:2, :], w_ref[2:3, :]

        def ds_of(p4, dy):
            cb, cz = p4[:, :cw], p4[:, 3 * cw:]
            return dy * cb * (cz * _sigmoid(cz))

        def chunk(ci, carry):
            g0, g1, g2 = carry
            r0 = pl.multiple_of(ci * rc, rc)
            blk = p_ref[pl.ds(r0, rc), :].astype(F32)
            dy = dy_ref[pl.ds(r0, rc), :].astype(F32)
            cb, cc, cx, cz = (blk[:, i * cw:(i + 1) * cw] for i in range(4))
            rp = pl.multiple_of(jnp.maximum(r0 - 16, 0), 16)
            rn = pl.multiple_of(jnp.minimum(r0 + rc, lp - 16), 16)
            pv = p_ref[pl.ds(rp, 16), :].astype(F32)[15:16]
            nx = p_ref[pl.ds(rn, 16), :].astype(F32)[0:1]
            dpv = dy_ref[pl.ds(rp, 16), :].astype(F32)[15:16]
            dnx = dy_ref[pl.ds(rn, 16), :].astype(F32)[0:1]
            has_prev, has_next = ci > 0, ci < nchunk - 1
            m = cc * cx
            m_prev, m_next = _shifted(m, jnp.where(has_prev, pv[:, cw:2 * cw] * pv[:, 2 * cw:3 * cw], 0.0),
                                      jnp.where(has_next, nx[:, cw:2 * cw] * nx[:, 2 * cw:3 * cw], 0.0), rc)
            s = w0 * m_prev + w1 * m + w2 * m_next
            sg = _sigmoid(cz)
            silu = cz * sg
            ds = dy * cb * silu
            ds_prev, ds_next = _shifted(ds, jnp.where(has_prev, ds_of(pv, dpv), 0.0),
                                        jnp.where(has_next, ds_of(nx, dnx), 0.0), rc)
            dm_ = w0 * ds_next + w1 * ds + w2 * ds_prev
            d_ref[pl.ds(r0, rc), 0:cw] = (dy * s * silu).astype(BF16)
            d_ref[pl.ds(r0, rc), cw:2 * cw] = (dm_ * cx).astype(BF16)
            d_ref[pl.ds(r0, rc), 2 * cw:3 * cw] = (dm_ * cc).astype(BF16)
            d_ref[pl.ds(r0, rc), 3 * cw:4 * cw] = (dy * cb * s * (sg * (1.0 + cz * (1.0 - sg)))).astype(BF16)
            return (g0 + jnp.sum(ds * m_prev, axis=0, keepdims=True), g1 + jnp.sum(ds * m, axis=0, keepdims=True),
                    g2 + jnp.sum(ds * m_next, axis=0, keepdims=True))

        z = jnp.zeros((1, cw), F32)
        g0, g1, g2 = lax.fori_loop(0, nchunk, chunk, (z, z, z))

        @pl.when(pl.program_id(1) == 0)
        def _():
            gw_ref[...] = jnp.zeros_like(gw_ref)

        gw_ref[0:1, :] += g0
        gw_ref[1:2, :] += g1
        gw_ref[2:3, :] += g2

    return pl.pallas_call(
        body, name="conv_bwd", grid=(dm.NJ, dm.Bl),
        in_specs=[pl.BlockSpec((lp, 4 * cw), lambda j, s: (s, j)), pl.BlockSpec((lp, cw), lambda j, s: (s, j)),
                  pl.BlockSpec((3, cw), lambda j, s: (0, j))],
        out_specs=[pl.BlockSpec((lp, 4 * cw), lambda j, s: (s, j)), pl.BlockSpec((8, cw), lambda j, s: (0, j))],
        out_shape=[jax.ShapeDtypeStruct((dm.T, 4 * dm.D), BF16), jax.ShapeDtypeStruct((8, dm.D), F32)],
        compiler_params=_cp(2),
    )(proj_a, dy_conv, conv_w)


def _interleave(gens):
    results = [None] * len(gens)
    live = list(range(len(gens)))
    while live:
        for idx in list(live):
            try:
                next(gens[idx])
            except StopIteration as done:
                results[idx] = done.value
                live.remove(idx)
    return results


def _group_chunks(dm):
    n = dm.NC - dm.C0
    return 3 if n % 3 == 0 else 1


def _group_masks(rows):
    ii = lax.broadcasted_iota(jnp.int32, (rows, rows), 0)
    jj = lax.broadcasted_iota(jnp.int32, (rows, rows), 1)
    same = jnp.right_shift(ii, CHUNK_SHIFT) == jnp.right_shift(jj, CHUNK_SHIFT)
    low, up = same & (jj <= ii), same & (jj >= ii)
    return low, same & (jj > ii), low.astype(BF16), up.astype(BF16)


def _first_row(chunk):
    return chunk * CHUNK if isinstance(chunk, int) else pl.multiple_of(chunk * CHUNK, CHUNK)


def _chunk_totals(b, fwd):
    hk = b.shape[1]
    rows = [b[c * CHUNK + CHUNK - 1:(c + 1) * CHUNK] if fwd else b[c * CHUNK:c * CHUNK + 1]
            for c in range(b.shape[0] // CHUNK)]
    return jnp.concatenate([jnp.broadcast_to(r, (CHUNK, hk)) for r in rows], axis=0)


def _log_gate(lr_rows, w_ref, b_ref, first_group, hk):
    z = _dot(lr_rows, w_ref[...]) + b_ref[...]
    e = jnp.exp(-jnp.abs(z))
    g = (jnp.minimum(z, 0.0) - jnp.log(1.0 + e)) * (1.0 / GATE_NORMALIZER)
    dg_dz = jnp.where(z >= 0.0, e, 1.0) / (1.0 + e) * (1.0 / GATE_NORMALIZER)
    row = lax.broadcasted_iota(jnp.int32, (lr_rows.shape[0], hk), 0)
    pad = first_group & (row < PAD_ROWS)
    return jnp.where(pad, 0.0, g), jnp.where(pad, 0.0, dg_dz)


def _gla_fwd(proj_b, lr, wg_f, bg_f, wg_b, bg_b, gla_g, dm):
    lp, hk, hv, nc, c0, hw = dm.LP, dm.HK, dm.HV, dm.NC, dm.C0, dm.HW
    scale = hk ** -0.5
    gc = _group_chunks(dm)
    gr, ng = gc * CHUNK, (nc - c0) // gc

    def body(p_ref, lr_ref, wf_ref, bf_ref, wb_ref, bb_ref, gg_ref, o_ref, y_ref, st_ref, b_out, gs_out, oacc_f, oacc_b):
        low_incl, up_strict, ones_low, ones_up = _group_masks(gr)
        if c0 > 0:
            zr = c0 * CHUNK
            o_ref[0:zr, :] = jnp.zeros((zr, hv), BF16)
            y_ref[0:zr, :] = jnp.zeros((zr, hv), BF16)
            b_out[:, 0:zr, :] = jnp.zeros((2, zr, hk), F32)
            gs_out[:, 0:zr, :] = jnp.zeros((2, zr, hk), F32)
            st_ref[0, 0, :, 0:c0] = jnp.zeros((2, c0, hv, hk), BF16)

        def decay(gi, fwd):
            w_ref, b_ref = (wf_ref, bf_ref) if fwd else (wb_ref, bb_ref)
            r0 = _first_row(c0 + gi * gc)
            yield
            g, dg_dz = _log_gate(lr_ref[pl.ds(r0, gr), :], w_ref, b_ref, gi == 0, hk)
            gs_out[0 if fwd else 1, pl.ds(r0, gr), :] = dg_dz
            yield
            b = _dot_exact01(ones_low if fwd else ones_up, g)
            b_out[0 if fwd else 1, pl.ds(r0, gr), :] = b
            return b

        def group(gi, st, b, fwd):
            oacc = oacc_f if fwd else oacc_b
            r0 = pl.multiple_of((c0 + gi * gc) * CHUNK, CHUNK)
            blk = p_ref[pl.ds(r0, gr), :]
            q = blk[:, :hk].astype(F32) * scale
            k = blk[:, hk:2 * hk].astype(F32)
            v = blk[:, 2 * hk:2 * hk + hv]
            btot = _chunk_totals(b, fwd)
            qi = (q * jnp.exp(b)).astype(BF16)
            ki = (k * jnp.exp(-b)).astype(BF16)
            kd = (k * jnp.exp(btot - b)).astype(BF16)
            dec = jnp.exp(btot)
            a = _dot_nt(qi, ki)
            yield
            o = _dot(jnp.where(low_incl if fwd else up_strict, a, 0.0).astype(BF16), v)
            for c in (range(gc) if fwd else reversed(range(gc))):
                yield
                rows = slice(c * CHUNK, (c + 1) * CHUNK)
                st_b = st.astype(BF16)
                st_ref[0, 0, 0 if fwd else 1, c0 + gi * gc + c] = st_b
                oacc[pl.ds(r0 + c * CHUNK, CHUNK), :] = o[rows] + _dot_nt(qi[rows], st_b)
                st = st * dec[c * CHUNK:c * CHUNK + 1] + _dot_tn(v[rows], kd[rows])
            return st

        def step(i, carry):
            st_f, st_b, b_f, b_b = carry
            gf, gb = i, ng - 1 - i
            return tuple(_interleave([group(gf, st_f, b_f, True), group(gb, st_b, b_b, False),
                                      decay(jnp.minimum(gf + 1, ng - 1), True), decay(jnp.maximum(gb - 1, 0), False)]))

        zero = jnp.zeros((hv, hk), F32)
        lax.fori_loop(0, ng, step, (zero, zero, *_interleave([decay(0, True), decay(ng - 1, False)])))

        def finish(i, carry):
            r0 = pl.multiple_of((c0 + i * gc) * CHUNK, CHUNK)
            o = oacc_f[pl.ds(r0, gr), :] + oacc_b[pl.ds(r0, gr), :]
            r = p_ref[pl.ds(r0, gr), 2 * hk + hv:].astype(F32)
            on = o * lax.rsqrt(jnp.mean(o * o, axis=-1, keepdims=True) + EPS) * gg_ref[...]
            o_ref[pl.ds(r0, gr), :] = o.astype(BF16)
            y_ref[pl.ds(r0, gr), :] = (on * r * _sigmoid(r)).astype(BF16)
            return carry

        lax.fori_loop(0, ng, finish, 0)

    head = lambda s, h: (s, h)
    wspec = pl.BlockSpec((LR_LANES, hk), lambda s, h: (0, h))
    bspec = pl.BlockSpec((1, hk), lambda s, h: (0, h))
    return pl.pallas_call(
        body, name="gla_fwd", grid=(dm.Bl, HEADS),
        in_specs=[pl.BlockSpec((lp, hw), head), pl.BlockSpec((lp, LR_LANES), lambda s, h: (s, 0)),
                  wspec, bspec, wspec, bspec, pl.BlockSpec((1, hv), lambda s, h: (0, 0))],
        out_specs=[pl.BlockSpec((lp, hv), head), pl.BlockSpec((lp, hv), head),
                   pl.BlockSpec((1, 1, 2, nc, hv, hk), lambda s, h: (s, h, 0, 0, 0, 0)),
                   pl.BlockSpec((2, lp, hk), lambda s, h: (0, s, h)), pl.BlockSpec((2, lp, hk), lambda s, h: (0, s, h))],
        out_shape=[jax.ShapeDtypeStruct((dm.T, dm.DV), BF16), jax.ShapeDtypeStruct((dm.T, dm.DV), BF16),
                   jax.ShapeDtypeStruct((dm.Bl, HEADS, 2, nc, hv, hk), BF16),
                   jax.ShapeDtypeStruct((2, dm.T, dm.DK), F32), jax.ShapeDtypeStruct((2, dm.T, dm.DK), F32)],
        scratch_shapes=[pltpu.VMEM((lp, hv), F32), pltpu.VMEM((lp, hv), F32)],
        compiler_params=_cp(2),
    )(proj_b, lr, wg_f, bg_f, wg_b, bg_b, gla_g)


def _gla_bwd(proj_b, lr, o_all, dy_gla, states, decays, gate_slopes, wg_f, wg_b, gla_g, dm):
    lp, hk, hv, nc, c0, hw = dm.LP, dm.HK, dm.HV, dm.NC, dm.C0, dm.HW
    scale = hk ** -0.5
    gc = _group_chunks(dm)
    gr, ng = gc * CHUNK, (nc - c0) // gc

    def body(p_ref, lr_ref, o_ref, dy_ref, st_ref, b_ref, gs_ref, wf_ref, wb_ref, gg_ref,
             d_ref, dlr_ref, gwf_ref, gbf_ref, gwb_ref, gbb_ref, ggg_ref, do_s, dq_s, dk_s, dv_s, dlr_s):
        low_incl, up_strict, ones_low, ones_up = _group_masks(gr)
        h = pl.program_id(1)

        @pl.when(h == 0)
        def _():
            dlr_ref[...] = jnp.zeros_like(dlr_ref)

        if c0 > 0:
            zr = c0 * CHUNK
            d_ref[0:zr, :] = jnp.zeros((zr, hw), BF16)
        for acc in (dq_s, dk_s, dv_s, dlr_s):
            acc[...] = jnp.zeros_like(acc)

        def norm_bwd(i, ggg):
            r0 = pl.multiple_of((c0 + i * gc) * CHUNK, CHUNK)
            o = o_ref[pl.ds(r0, gr), :].astype(F32)
            dy = dy_ref[pl.ds(r0, gr), :].astype(F32)
            r = p_ref[pl.ds(r0, gr), 2 * hk + hv:].astype(F32)
            rstd = lax.rsqrt(jnp.mean(o * o, axis=-1, keepdims=True) + EPS)
            ohat = o * rstd
            sg = _sigmoid(r)
            d_on = dy * (r * sg)
            d_ref[pl.ds(r0, gr), 2 * hk + hv:] = (dy * ohat * gg_ref[...] * (sg * (1.0 + r * (1.0 - sg)))).astype(BF16)
            d_oh = d_on * gg_ref[...]
            do_s[pl.ds(r0, gr), :] = (rstd * (d_oh - ohat * jnp.mean(d_oh * ohat, axis=-1, keepdims=True))).astype(BF16)
            return ggg + jnp.sum(d_on * ohat, axis=0, keepdims=True)

        ggg = lax.fori_loop(0, ng, norm_bwd, jnp.zeros((1, hv), F32))

        @pl.when((pl.program_id(0) == 0) & (h == 0))
        def _():
            ggg_ref[...] = jnp.zeros_like(ggg_ref)

        ggg_ref[0:1, :] += ggg

        def load(gi):
            r0 = pl.multiple_of((c0 + gi * gc) * CHUNK, CHUNK)
            blk = p_ref[pl.ds(r0, gr), :]
            return r0, blk[:, :hk].astype(F32) * scale, blk[:, hk:2 * hk].astype(F32), blk[:, 2 * hk:2 * hk + hv]

        zero = jnp.zeros((hv, hk), F32)

        def grad(gi, carry, fwd):
            dst, gw, gb = carry
            w_ref, way = (wf_ref, 0) if fwd else (wb_ref, 1)
            mask = low_incl if fwd else up_strict
            r0, q, k, v = load(gi)
            b = b_ref[way, pl.ds(r0, gr), :]
            btot = _chunk_totals(b, fwd)
            eb, enb, edb, dec = jnp.exp(b), jnp.exp(-b), jnp.exp(btot - b), jnp.exp(btot)
            qi_f, ki_f, kd_f = q * eb, k * enb, k * edb
            qi, ki, kd = qi_f.astype(BF16), ki_f.astype(BF16), kd_f.astype(BF16)
            do = do_s[pl.ds(r0, gr), :]
            a = _dot_nt(qi, ki)
            da = _dot_nt(do, v)
            yield
            a = jnp.where(mask, a, 0.0).astype(BF16)
            da = jnp.where(mask, da, 0.0).astype(BF16)
            dv = _dot_tn(a, do)
            dqi = _dot(da, ki)
            dki = _dot_tn(da, qi)
            dv_c, dqi_c, dkd_c, extra_c = [None] * gc, [None] * gc, [None] * gc, [None] * gc
            for c in (reversed(range(gc)) if fwd else range(gc)):
                yield
                rows = slice(c * CHUNK, (c + 1) * CHUNK)
                st = st_ref[0, 0, way, c0 + gi * gc + c]
                dsn_b = dst.astype(BF16)
                dec_c = dec[c * CHUNK:c * CHUNK + 1]
                dv_c[c] = dv[rows] + _dot_nt(kd[rows], dsn_b)
                dqi_c[c] = dqi[rows] + _dot(do[rows], st)
                dkd_c[c] = _dot(v[rows], dsn_b)
                ddec = jnp.sum(st.astype(F32) * dst, axis=0, keepdims=True)
                extra = jnp.sum(dkd_c[c] * kd_f[rows], axis=0, keepdims=True) + ddec * dec_c
                extra_c[c] = jnp.broadcast_to(extra, (CHUNK, hk))
                dst = dst * dec_c + _dot_tn(do[rows], qi[rows])
            yield
            dv, dqi = jnp.concatenate(dv_c, axis=0), jnp.concatenate(dqi_c, axis=0)
            dkd, extra = jnp.concatenate(dkd_c, axis=0), jnp.concatenate(extra_c, axis=0)
            dq_s[pl.ds(r0, gr), :] += dqi * eb * scale
            dk_s[pl.ds(r0, gr), :] += dki * enb + dkd * edb
            dv_s[pl.ds(r0, gr), :] += dv
            db = dqi * qi_f - dki * ki_f - dkd * kd_f
            dg = _dot_exact01(ones_up if fwd else ones_low, db) + extra
            yield
            dz = dg * gs_ref[way, pl.ds(r0, gr), :]
            dz_b = dz.astype(BF16)
            dlr_s[pl.ds(r0, gr), :] += _dot_nt(dz_b, w_ref[...])
            return dst, gw + _dot_tn(lr_ref[pl.ds(r0, gr), :], dz_b), gb + jnp.sum(dz, axis=0, keepdims=True)

        def grad_step(i, carry):
            return tuple(_interleave([grad(ng - 1 - i, carry[0], True), grad(i, carry[1], False)]))

        init = (zero, jnp.zeros((LR_LANES, hk), F32), jnp.zeros((1, hk), F32))
        (_, gw_f, gb_f), (_, gw_b, gb_b) = lax.fori_loop(0, ng, grad_step, (init, init))
        for gw_ref, gb_ref, gw, gb in ((gwf_ref, gbf_ref, gw_f, gb_f), (gwb_ref, gbb_ref, gw_b, gb_b)):
            gw_ref[0] = gw
            gb_ref[0] = jnp.zeros((8, hk), F32)
            gb_ref[0, 0:1, :] = gb

        def combine(i, carry):
            r0 = pl.multiple_of((c0 + i * gc) * CHUNK, CHUNK)
            d_ref[pl.ds(r0, gr), 0:hk] = dq_s[pl.ds(r0, gr), :].astype(BF16)
            d_ref[pl.ds(r0, gr), hk:2 * hk] = dk_s[pl.ds(r0, gr), :].astype(BF16)
            d_ref[pl.ds(r0, gr), 2 * hk:2 * hk + hv] = dv_s[pl.ds(r0, gr), :].astype(BF16)
            dlr_ref[pl.ds(r0, gr), :] += dlr_s[pl.ds(r0, gr), :]
            return carry

        lax.fori_loop(0, ng, combine, 0)

    head = lambda s, h: (s, h)
    wspec = pl.BlockSpec((LR_LANES, hk), lambda s, h: (0, h))
    gwspec = pl.BlockSpec((1, LR_LANES, hk), lambda s, h: (s, 0, h))
    gbspec = pl.BlockSpec((1, 8, hk), lambda s, h: (s, 0, h))
    gw_shape = jax.ShapeDtypeStruct((dm.Bl, LR_LANES, dm.DK), F32)
    gb_shape = jax.ShapeDtypeStruct((dm.Bl, 8, dm.DK), F32)
    both = pl.BlockSpec((2, lp, hk), lambda s, h: (0, s, h))
    return pl.pallas_call(
        body, name="gla_bwd", grid=(dm.Bl, HEADS),
        in_specs=[pl.BlockSpec((lp, hw), head), pl.BlockSpec((lp, LR_LANES), lambda s, h: (s, 0)),
                  pl.BlockSpec((lp, hv), head), pl.BlockSpec((lp, hv), head),
                  pl.BlockSpec((1, 1, 2, nc, hv, hk), lambda s, h: (s, h, 0, 0, 0, 0)), both, both,
                  wspec, wspec, pl.BlockSpec((1, hv), lambda s, h: (0, 0))],
        out_specs=[pl.BlockSpec((lp, hw), head), pl.BlockSpec((lp, LR_LANES), lambda s, h: (s, 0)),
                   gwspec, gbspec, gwspec, gbspec, pl.BlockSpec((8, hv), lambda s, h: (0, 0))],
        out_shape=[jax.ShapeDtypeStruct((dm.T, HEADS * hw), BF16), jax.ShapeDtypeStruct((dm.T, LR_LANES), F32),
                   gw_shape, gb_shape, gw_shape, gb_shape, jax.ShapeDtypeStruct((8, hv), F32)],
        scratch_shapes=[pltpu.VMEM((lp, hv), BF16), pltpu.VMEM((lp, hk), F32), pltpu.VMEM((lp, hk), F32),
                        pltpu.VMEM((lp, hv), F32), pltpu.VMEM((lp, LR_LANES), F32)],
        compiler_params=_cp(2),
    )(proj_b, lr, o_all, dy_gla, states, decays, gate_slopes, wg_f, wg_b, gla_g)


def _out_merge(y_conv, y_gla, proj_c, w_oc, w_og, dm):
    d = dm.D
    tm = _pick(dm.T, 512, 16)

    def body(yc_ref, yg_ref, c_ref, woc_ref, wog_ref, pc_ref, pg_ref, m_ref):
        pc = _dot(yc_ref[...], woc_ref[...])
        pg = _dot(yg_ref[...], wog_ref[...])
        pc_ref[...] = pc.astype(BF16)
        pg_ref[...] = pg.astype(BF16)
        ma = c_ref[:, :d].astype(F32)
        mb = c_ref[:, d:].astype(F32)
        m_ref[...] = (_sigmoid(ma) * pc + _sigmoid(mb) * pg).astype(BF16)

    row = pl.BlockSpec((tm, d), lambda i: (i, 0))
    full = pl.BlockSpec((d, d), lambda i: (0, 0))
    act = jax.ShapeDtypeStruct((dm.T, d), BF16)
    return pl.pallas_call(
        body, name="out_merge", grid=(dm.T // tm,),
        in_specs=[row, row, pl.BlockSpec((tm, 2 * d), lambda i: (i, 0)), full, full],
        out_specs=[row, row, row], out_shape=[act, act, act], compiler_params=_cp(1),
    )(y_conv, y_gla, proj_c, w_oc, w_og)


def _final_fwd(merged, w_out, x, metapad, target, g_post, dm):
    tm, tps, d = dm.TM, dm.TPS, dm.D

    def body(m_ref, w_ref, x_ref, mp_ref, t_ref, g_ref, dout_ref, dy_ref, st_ref):
        i = pl.program_id(0)
        j = i % tps
        out = _dot(m_ref[...], w_ref[...])
        rstd = lax.rsqrt(jnp.mean(out * out, axis=-1, keepdims=True) + EPS)
        ohat = out * rstd
        h = jnp.where(j == 0, mp_ref[...], x_ref[0])
        y = h + ohat * g_ref[...]
        err = jnp.where(j == 0, 0.0, y - t_ref[0])
        dy = err * (1.0 / d)
        d_oh = dy * g_ref[...]
        dout_ref[...] = (rstd * (d_oh - ohat * jnp.mean(d_oh * ohat, axis=-1, keepdims=True))).astype(BF16)
        dy_ref[...] = dy

        @pl.when(i == 0)
        def _():
            st_ref[...] = jnp.zeros_like(st_ref)

        st_ref[0:1, :] += jnp.sum(dy * ohat, axis=0, keepdims=True)
        st_ref[1:2, :] += jnp.sum(err * err, axis=0, keepdims=True)

    row = pl.BlockSpec((tm, d), lambda i: (i, 0))
    tok = pl.BlockSpec((1, tm, d), lambda i: (i // tps, jnp.maximum(i % tps - 1, 0), 0))
    const = lambda r: pl.BlockSpec((r, d), lambda i: (0, 0))
    return pl.pallas_call(
        body, name="final_fwd", grid=(dm.Bl * tps,),
        in_specs=[row, const(d), tok, const(tm), tok, const(1)],
        out_specs=[row, row, const(8)],
        out_shape=[jax.ShapeDtypeStruct((dm.T, d), BF16), jax.ShapeDtypeStruct((dm.T, d), F32),
                   jax.ShapeDtypeStruct((8, d), F32)],
        compiler_params=_cp(1),
    )(merged, w_out, x, metapad, target, g_post)


def _merge_bwd(d_out, proj_c, p_conv, p_gla, w_out, w_oc, w_og, dm):
    d = dm.D
    tm = _pick(dm.T, 512, 16)

    def body(do_ref, c_ref, pc_ref, pg_ref, wo_ref, woc_ref, wog_ref, dpc_ref, dpg_ref, dc_ref, dyc_ref, dyg_ref):
        dmg = _dot_nt(do_ref[...], wo_ref[...])
        sa = _sigmoid(c_ref[:, :d].astype(F32))
        sb = _sigmoid(c_ref[:, d:].astype(F32))
        dpc = (dmg * sa).astype(BF16)
        dpg = (dmg * sb).astype(BF16)
        dpc_ref[...] = dpc
        dpg_ref[...] = dpg
        dc_ref[:, :d] = (dmg * pc_ref[...].astype(F32) * sa * (1.0 - sa)).astype(BF16)
        dc_ref[:, d:] = (dmg * pg_ref[...].astype(F32) * sb * (1.0 - sb)).astype(BF16)
        dyc_ref[...] = _dot_nt(dpc, woc_ref[...]).astype(BF16)
        dyg_ref[...] = _dot_nt(dpg, wog_ref[...]).astype(BF16)

    row = pl.BlockSpec((tm, d), lambda i: (i, 0))
    row2 = pl.BlockSpec((tm, 2 * d), lambda i: (i, 0))
    full = pl.BlockSpec((d, d), lambda i: (0, 0))
    act = jax.ShapeDtypeStruct((dm.T, d), BF16)
    return pl.pallas_call(
        body, name="merge_bwd", grid=(dm.T // tm,),
        in_specs=[row, row2, row, row, full, full, full],
        out_specs=[row, row, row2, row, row],
        out_shape=[act, act, jax.ShapeDtypeStruct((dm.T, 2 * d), BF16), act, act],
        compiler_params=_cp(1),
    )(d_out, proj_c, p_conv, p_gla, w_out, w_oc, w_og)


def _prenorm_bwd(du, dy, x, metapad, g_pre, dm):
    tm, tps, d = dm.TM, dm.TPS, dm.D

    def body(du_ref, dy_ref, x_ref, mp_ref, g_ref, gx_ref, dmeta_ref, gg_ref):
        i = pl.program_id(0)
        j = i % tps
        h = jnp.where(j == 0, mp_ref[...], x_ref[0])
        rstd = lax.rsqrt(jnp.mean(h * h, axis=-1, keepdims=True) + EPS)
        hhat = h * rstd
        dug = du_ref[...] * g_ref[...]
        dh = dy_ref[...] + rstd * (dug - hhat * jnp.mean(dug * hhat, axis=-1, keepdims=True))

        @pl.when(j == 0)
        def _():
            dmeta_ref[0] = dh

        @pl.when(j > 0)
        def _():
            gx_ref[0] = dh

        @pl.when(i == 0)
        def _():
            gg_ref[...] = jnp.zeros_like(gg_ref)

        gg_ref[0:1, :] += jnp.sum(du_ref[...] * hhat, axis=0, keepdims=True)

    row = pl.BlockSpec((tm, d), lambda i: (i, 0))
    tok = pl.BlockSpec((1, tm, d), lambda i: (i // tps, jnp.maximum(i % tps - 1, 0), 0))
    const = lambda r: pl.BlockSpec((r, d), lambda i: (0, 0))
    return pl.pallas_call(
        body, name="prenorm_bwd", grid=(dm.Bl * tps,),
        in_specs=[row, row, tok, const(tm), const(1)],
        out_specs=[tok, pl.BlockSpec((1, tm, d), lambda i: (i // tps, 0, 0)), const(8)],
        out_shape=[jax.ShapeDtypeStruct((dm.Bl, dm.S, d), F32), jax.ShapeDtypeStruct((dm.Bl, tm, d), F32),
                   jax.ShapeDtypeStruct((8, d), F32)],
        compiler_params=_cp(1),
    )(du, dy, x, metapad, g_pre)


def _adamw(partials, w, m, v, name, by_columns=False):
    r, c = w.shape
    n_parts = partials.shape[0]
    tr, tc = (r, _pick(c, 128, 128)) if by_columns else (_pick(r, 256, 16), c)

    def body(p_ref, w_ref, m_ref, v_ref, g_ref, d_ref, nm_ref, nv_ref):
        g = p_ref[0].astype(F32)
        for j in range(1, n_parts):
            g = g + p_ref[j].astype(F32)
        m2 = ADAM_B1 * m_ref[...] + (1.0 - ADAM_B1) * g
        v2 = ADAM_B2 * v_ref[...] + (1.0 - ADAM_B2) * (g * g)
        m_hat = m2 / (1.0 - ADAM_B1 ** ADAM_STEP)
        v_hat = v2 / (1.0 - ADAM_B2 ** ADAM_STEP)
        g_ref[...] = g
        d_ref[...] = -ADAM_LR * (m_hat / (jnp.sqrt(v_hat) + ADAM_EPS) + ADAM_WD * w_ref[...])
        nm_ref[...] = m2
        nv_ref[...] = v2

    at = (lambda i: (0, i)) if by_columns else (lambda i: (i, 0))
    tile = pl.BlockSpec((tr, tc), at)
    out = jax.ShapeDtypeStruct((r, c), F32)
    return pl.pallas_call(
        body, name=name, grid=(c // tc if by_columns else r // tr,),
        in_specs=[pl.BlockSpec((n_parts, tr, tc), lambda i: (0,) + at(i)), tile, tile, tile],
        out_specs=[tile, tile, tile, tile], out_shape=[out, out, out, out], compiler_params=_cp(1),
    )(partials, w, m, v)


def _pack_rows(wt, dm):
    d, dk, hk, hv, cw, nj = dm.D, dm.DK, dm.HK, dm.HV, dm.CW, dm.NJ
    a = wt[:4 * d].reshape(4, nj, cw, d).transpose(1, 0, 2, 3).reshape(4 * d, d)
    b = jnp.concatenate([wt[4 * d:4 * d + dk].reshape(HEADS, hk, d), wt[4 * d + dk:5 * d].reshape(HEADS, hk, d),
                         wt[5 * d:6 * d].reshape(HEADS, hv, d), wt[6 * d:7 * d].reshape(HEADS, hv, d)],
                        axis=1).reshape(3 * d, d)
    c = wt[7 * d + 2 * RANK:]
    lr = jnp.pad(wt[7 * d:7 * d + 2 * RANK], ((0, LR_LANES - 2 * RANK), (0, 0)))
    return a, b, c, lr


def _unpack_rows(a, b, c, lr, dm):
    d, hk, hv, cw, nj, hw = dm.D, dm.HK, dm.HV, dm.CW, dm.NJ, dm.HW
    conv = a.reshape(nj, 4, cw, d).transpose(1, 0, 2, 3).reshape(4 * d, d)
    heads = b.reshape(HEADS, hw, d)
    q = heads[:, :hk].reshape(HEADS * hk, d)
    k = heads[:, hk:2 * hk].reshape(HEADS * hk, d)
    v = heads[:, 2 * hk:2 * hk + hv].reshape(HEADS * hv, d)
    r = heads[:, 2 * hk + hv:].reshape(HEADS * hv, d)
    return jnp.concatenate([conv, q, k, v, r, lr[:2 * RANK], c], axis=0)


def _to_blob(pieces, dtype, row_mult):
    lead = pieces[0].shape[0]
    flat = jnp.concatenate([p.reshape(lead, -1).astype(dtype) for p in pieces], axis=1)
    unit = row_mult * BLOB_LANES
    padded = -(-flat.shape[1] // unit) * unit
    flat = jnp.pad(flat, ((0, 0), (0, padded - flat.shape[1])))
    return flat.reshape(lead, padded // BLOB_LANES, BLOB_LANES)


def _from_blob(blob, shapes):
    lead = blob.shape[0]
    flat = blob.reshape(lead, -1)
    out, off = [], 0
    for shp in shapes:
        size = int(np.prod(shp))
        out.append(flat[:, off:off + size].reshape((lead,) + tuple(shp)))
        off += size
    return out


def _local_step(x, target, meta, g_pre, wt_in, conv_w, wg_f, bg_f, wg_b, bg_b, gla_g, out_weights, g_post,
                on_matrix_grads=None):
    bl, s, d = x.shape
    dm = _Dims(bl, s, d)
    metapad = jnp.concatenate([jnp.zeros((dm.TM - N_META, d), F32), meta], axis=0)
    wta, wtb, wtc, wtlr = _pack_rows(wt_in, dm)
    wgp_f = jnp.pad(wg_f, ((0, LR_LANES - RANK), (0, 0))).astype(BF16)
    wgp_b = jnp.pad(wg_b, ((RANK, LR_LANES - 2 * RANK), (0, 0))).astype(BF16)

    u = _prenorm(x, metapad, g_pre, dm)
    proj_a, proj_b, proj_c, lr = _inproj(u, [wta, wtb, wtc, wtlr], dm)
    y_conv = _conv_fwd(proj_a, conv_w, dm)
    o_all, y_gla, states, decays, gate_slopes = _gla_fwd(proj_b, lr, wgp_f, bg_f, wgp_b, bg_b, gla_g, dm)
    w_oc, w_og, w_out = out_weights(y_conv) if callable(out_weights) else out_weights
    p_conv, p_gla, merged = _out_merge(y_conv, y_gla, proj_c, w_oc, w_og, dm)
    d_out, dy, stats = _final_fwd(merged, w_out, x, metapad, target, g_post, dm)
    loss = 0.5 / d * jnp.sum(stats[1])

    d_pc, d_pg, d_c, dy_conv, dy_gla = _merge_bwd(d_out, proj_c, p_conv, p_gla, w_out, w_oc, w_og, dm)
    g_out = _matmul_tn(merged, d_out, BF16, "grad_w_out")
    g_oc = _matmul_tn(y_conv, d_pc, BF16, "grad_w_out_conv")
    g_og = _matmul_tn(y_gla, d_pg, BF16, "grad_w_out_gla")
    d_a, g_conv = _conv_bwd(proj_a, dy_conv, conv_w, dm)
    d_b, d_lr, gwp_f, gbp_f, gwp_b, gbp_b, g_gla = _gla_bwd(proj_b, lr, o_all, dy_gla, states, decays, gate_slopes, wgp_f, wgp_b, gla_g, dm)
    g_in = _unpack_rows(_matmul_tn(d_a, u, BF16, "grad_w_in_conv"), _matmul_tn(d_b, u, BF16, "grad_w_in_gla"),
                        _matmul_tn(d_c, u, BF16, "grad_w_in_merge"), _matmul_tn(d_lr, u, BF16, "grad_w_in_gate"), dm)
    if on_matrix_grads is not None:
        wtlr = wtlr + on_matrix_grads(dict(w_in=g_in, w_out_conv=g_oc, w_out_gla=g_og, w_merge_out=g_out)).astype(BF16)
    du = _grad_u([d_a, d_b, d_c, d_lr], [wta, wtb, wtc, wtlr], dm)
    grad_x, d_meta, g_pre_rows = _prenorm_bwd(du, dy, x, metapad, g_pre, dm)

    grads = dict(
        meta_tokens=jnp.sum(d_meta[:, dm.TM - N_META:, :], axis=0), norm_pre=g_pre_rows[0:1], w_in=g_in,
        conv_w=g_conv[0:3], w_gate_fwd=jnp.sum(gwp_f, axis=0)[:RANK], b_gate_fwd=jnp.sum(gbp_f, axis=0)[0:1],
        w_gate_bwd=jnp.sum(gwp_b, axis=0)[RANK:2 * RANK], b_gate_bwd=jnp.sum(gbp_b, axis=0)[0:1],
        gla_norm=g_gla[0:1], w_out_conv=g_oc, w_out_gla=g_og, w_merge_out=g_out, norm_post=stats[0:1])
    return loss, grad_x, grads


MATRICES = ("w_out_conv", "w_out_gla", "w_merge_out")
SMALL_SHARDED = ("meta_tokens", "conv_w", "w_gate_fwd", "w_gate_bwd")
REPLICATED = ("norm_pre", "b_gate_fwd", "b_gate_bwd", "gla_norm", "norm_post")
NAMES = ("meta_tokens", "norm_pre", "w_in", "conv_w", "w_gate_fwd", "b_gate_fwd", "w_gate_bwd", "b_gate_bwd", "gla_norm",
         "w_out_conv", "w_out_gla", "w_merge_out", "norm_post")
DEPTH_AXIS = ("w_in", "conv_w", "w_gate_fwd", "w_gate_bwd") + MATRICES


def _cols_to_devices(g):
    r, c = g.shape
    return g.reshape(r, N_DEV, c // N_DEV).transpose(1, 0, 2)


def _cols_from_devices(parts):
    n, r, c = parts.shape
    return parts.transpose(1, 0, 2).reshape(r, n * c)


def kernel(x, meta_tokens, norm_pre, w_in, conv_w, w_gate_fwd, b_gate_fwd, w_gate_bwd, b_gate_bwd, gla_norm, w_out_conv, w_out_gla, w_merge_out, norm_post, loss_target, m_meta_tokens, m_norm_pre, m_w_in, m_conv_w, m_w_gate_fwd, m_b_gate_fwd, m_w_gate_bwd, m_b_gate_bwd, m_gla_norm, m_w_out_conv, m_w_out_gla, m_w_merge_out, m_norm_post, v_meta_tokens, v_norm_pre, v_w_in, v_conv_w, v_w_gate_fwd, v_b_gate_fwd, v_w_gate_bwd, v_b_gate_bwd, v_gla_norm, v_w_out_conv, v_w_out_gla, v_w_merge_out, v_norm_post):
    w = dict(meta_tokens=meta_tokens, norm_pre=norm_pre, w_in=w_in[0], conv_w=conv_w[0], w_gate_fwd=w_gate_fwd[0],
             b_gate_fwd=b_gate_fwd, w_gate_bwd=w_gate_bwd[0], b_gate_bwd=b_gate_bwd, gla_norm=gla_norm,
             w_out_conv=w_out_conv[0], w_out_gla=w_out_gla[0], w_merge_out=w_merge_out[0], norm_post=norm_post)
    m = dict(meta_tokens=m_meta_tokens, norm_pre=m_norm_pre, w_in=m_w_in[0], conv_w=m_conv_w[0], w_gate_fwd=m_w_gate_fwd[0],
             b_gate_fwd=m_b_gate_fwd, w_gate_bwd=m_w_gate_bwd[0], b_gate_bwd=m_b_gate_bwd, gla_norm=m_gla_norm,
             w_out_conv=m_w_out_conv[0], w_out_gla=m_w_out_gla[0], w_merge_out=m_w_merge_out[0], norm_post=m_norm_post)
    v = dict(meta_tokens=v_meta_tokens, norm_pre=v_norm_pre, w_in=v_w_in[0], conv_w=v_conv_w[0], w_gate_fwd=v_w_gate_fwd[0],
             b_gate_fwd=v_b_gate_fwd, w_gate_bwd=v_w_gate_bwd[0], b_gate_bwd=v_b_gate_bwd, gla_norm=v_gla_norm,
             w_out_conv=v_w_out_conv[0], w_out_gla=v_w_out_gla[0], w_merge_out=v_w_merge_out[0], norm_post=v_norm_post)
    d = x.shape[-1]

    small_blob = _to_blob([w[n][None] for n in SMALL_SHARDED], F32, SMALL_ROWS)[0]
    wt_all, small_all = _gather_two_level([w["w_in"].T.astype(BF16), small_blob], "gather_weights")
    _, late_weights = _exchange_start([w[n].astype(BF16) for n in MATRICES], [], small_all, "gather_out_weights_start")
    wt_in = wt_all.reshape(-1, d)
    small = {n: _cols_from_devices(p) for n, p in zip(SMALL_SHARDED, _from_blob(small_all, [w[n].shape for n in SMALL_SHARDED]))}

    def out_weights(after):
        return tuple(a.reshape(-1, d) for a in _exchange_wait(late_weights, after, "gather_out_weights_wait"))

    pending = []

    def on_matrix_grads(g):
        to_send = [g[n].astype(BF16).reshape(N_DEV, -1, d) for n in ("w_in",) + MATRICES]
        token, state = _exchange_start([], to_send, None, "exchange_grads_start")
        pending.append(state)
        return token

    loss, grad_x, grads = _local_step(
        x, loss_target, small["meta_tokens"], norm_pre, wt_in, small["conv_w"], small["w_gate_fwd"], b_gate_fwd,
        small["w_gate_bwd"], b_gate_bwd, gla_norm, out_weights, norm_post, on_matrix_grads)
    loss = lax.psum(loss, ("x", "y", "c"))
    received = _exchange_wait(pending[0], grad_x, "exchange_grads_wait")

    small_send = _to_blob([_cols_to_devices(grads[n]) for n in SMALL_SHARDED], F32, SMALL_ROWS)
    repl_blob = _to_blob([grads[n][None] for n in REPLICATED], F32, SMALL_ROWS)[0]
    repl_all, small_recv = _exchange([repl_blob], [small_send], "exchange_small_grads")

    results = {"w_in": [r.T for r in _adamw(received[0], w["w_in"].T, m["w_in"].T, v["w_in"].T, "adamw_w_in", by_columns=True)]}
    for n, partials in zip(MATRICES, received[1:4]):
        results[n] = _adamw(partials, w[n], m[n], v[n], "adamw_" + n)
    for names, partials, tag in ((SMALL_SHARDED, small_recv, "small"), (REPLICATED, repl_all, "replicated")):
        blobs = [_to_blob([t[n][None] for n in names], F32, SMALL_ROWS)[0] for t in (w, m, v)]
        res = [_from_blob(r[None], [w[n].shape for n in names]) for r in _adamw(partials, *blobs, name="adamw_" + tag)]
        for i, n in enumerate(names):
            results[n] = [r[i][0] for r in res]
    lead = lambda n, t: t[None] if n in DEPTH_AXIS else t
    return (loss, grad_x, *[lead(n, results[n][i]) for i in range(4) for n in NAMES])
```

```python
import jax
import jax.numpy as jnp
from jax import lax
from jax.experimental import pallas as pl
from jax.experimental.pallas import tpu as pltpu

F32 = jnp.float32
BF16 = jnp.bfloat16
MESH = pl.DeviceIdType.MESH

N_META = 16
CHUNK = 64
CHUNK_SHIFT = 6
HEADS = 4
RANK = 16
LR_LANES = 128
PAD_ROWS = CHUNK - N_META
EPS = 1e-6
GATE_NORMALIZER = 16.0
N_DEV = 8
ADAM_LR, ADAM_B1, ADAM_B2, ADAM_EPS, ADAM_WD, ADAM_STEP = 0.001, 0.9, 0.999, 1e-08, 0.01, 10
VMEM_LIMIT_BYTES = 56 * 1024 * 1024


class _Dims:
    def __init__(self, bl, s, d):
        self.Bl, self.S, self.D = bl, s, d
        self.TM = 128 if s % 128 == 0 else CHUNK
        self.LP = self.TM + s
        self.T = bl * self.LP
        self.TPS = self.LP // self.TM
        self.NC = self.LP // CHUNK
        self.C0 = (self.TM - CHUNK) // CHUNK
        self.DK, self.DV = d // 2, d
        self.HK, self.HV = self.DK // HEADS, self.DV // HEADS
        self.HW = 2 * self.HK + 2 * self.HV
        self.CW = 256 if d % 256 == 0 and d > 256 else d // 4
        self.NJ = d // self.CW


def _pick(n, target, mult):
    t = min(n, target)
    while t >= mult:
        if n % t == 0 and t % mult == 0:
            return t
        t -= mult
    return n


def _cp(n_axes):
    return pltpu.CompilerParams(dimension_semantics=("arbitrary",) * n_axes, vmem_limit_bytes=VMEM_LIMIT_BYTES)


def _sigmoid(x):
    return 1.0 / (1.0 + jnp.exp(-x))


def _dot(a, b):
    return jnp.dot(a, b, preferred_element_type=F32)


def _dot_nt(a, b):
    return lax.dot_general(a, b, (((1,), (1,)), ((), ())), preferred_element_type=F32)


def _dot_tn(a, b):
    return lax.dot_general(a, b, (((0,), (0,)), ((), ())), preferred_element_type=F32)


def _dot_exact01(m01, x):
    hi = x.astype(BF16)
    lo = (x - hi.astype(F32)).astype(BF16)
    return _dot(m01, hi) + _dot(m01, lo)


def _exchange(gathers, scatters, name):
    arrays = list(gathers) + list(scatters)
    n, ng = len(arrays), len(gathers)

    def body(*refs):
        ins, outs = refs[:n], refs[n:2 * n]
        send_sems, recv_sems, local_sems = refs[2 * n:]
        x, y, c = lax.axis_index("x"), lax.axis_index("y"), lax.axis_index("c")
        me = 4 * x + 2 * y + c
        started = []
        for t in range(n):
            src, dst = ins[t], outs[t]
            own = pltpu.make_async_copy(src if t < ng else src.at[me], dst.at[me], local_sems.at[t])
            own.start()
            started.append(own)
            for k, pos, peer in _peers(x, y, c):
                cp = pltpu.make_async_remote_copy(
                    src_ref=src if t < ng else src.at[peer], dst_ref=dst.at[me],
                    send_sem=send_sems.at[t * (N_DEV - 1) + k - 1], recv_sem=recv_sems.at[t * (N_DEV - 1) + k - 1],
                    device_id=pos, device_id_type=MESH)
                cp.start()
                started.append(cp)
        for cp in started:
            cp.wait()

    out_shape = [jax.ShapeDtypeStruct((N_DEV,) + a.shape if t < ng else a.shape, a.dtype) for t, a in enumerate(arrays)]
    any_spec = pl.BlockSpec(memory_space=pl.ANY)
    return pl.pallas_call(
        body, name=name, out_shape=out_shape, in_specs=[any_spec] * n, out_specs=[any_spec] * n,
        scratch_shapes=[pltpu.SemaphoreType.DMA((n * (N_DEV - 1),)), pltpu.SemaphoreType.DMA((n * (N_DEV - 1),)),
                        pltpu.SemaphoreType.DMA((n,))],
        compiler_params=pltpu.CompilerParams(has_side_effects=True),
    )(*arrays)


def _gather_two_level(arrays, name):
    n = len(arrays)
    per = N_DEV - 1

    def body(*refs):
        ins, outs = refs[:n], refs[n:2 * n]
        send_sems, recv_sems, local_sems = refs[2 * n:]
        x, y, c = lax.axis_index("x"), lax.axis_index("y"), lax.axis_index("c")
        sibling = (x, y, 1 - c)
        chips = [(1 - x, y), (x, 1 - y), (1 - x, 1 - y)]
        index = lambda px, py, pc: 4 * px + 2 * py + pc

        def copy(t, k, block, to, from_input=False):
            slab = outs[t].at[index(*block)]
            return pltpu.make_async_remote_copy(
                src_ref=ins[t] if from_input else slab, dst_ref=slab, send_sem=send_sems.at[t * per + k],
                recv_sem=recv_sems.at[t * per + k], device_id=to, device_id_type=MESH)

        own, sent = [], []
        for t in range(n):
            own.append(pltpu.make_async_copy(ins[t], outs[t].at[index(x, y, c)], local_sems.at[t]))
            own[-1].start()
            first = [copy(t, 0, (x, y, c), sibling, True)]
            first += [copy(t, 1 + j, (x, y, c), (*chip, c), True) for j, chip in enumerate(chips)]
            for cp in first:
                cp.start()
            sent += first
        for t in range(n):
            for j, chip in enumerate(chips):
                copy(t, 1 + j, (*chip, c), (x, y, c)).wait_recv()
                sent.append(copy(t, 4 + j, (*chip, c), sibling))
                sent[-1].start()
        for t in range(n):
            copy(t, 0, sibling, (x, y, c)).wait_recv()
            for j, chip in enumerate(chips):
                copy(t, 4 + j, (*chip, 1 - c), (x, y, c)).wait_recv()
        for cp in sent:
            cp.wait_send()
        for cp in own:
            cp.wait()

    out_shape = [jax.ShapeDtypeStruct((N_DEV,) + a.shape, a.dtype) for a in arrays]
    any_spec = pl.BlockSpec(memory_space=pl.ANY)
    return pl.pallas_call(
        body, name=name, out_shape=out_shape, in_specs=[any_spec] * n, out_specs=[any_spec] * n,
        scratch_shapes=[pltpu.SemaphoreType.DMA((n * per,)), pltpu.SemaphoreType.DMA((n * per,)),
                        pltpu.SemaphoreType.DMA((n,))],
        compiler_params=pltpu.CompilerParams(has_side_effects=True),
    )(*arrays)


def _peers(x, y, c):
    out = []
    for k in range(1, N_DEV):
        px = 1 - x if (k >> 2) & 1 else x
        py = 1 - y if (k >> 1) & 1 else y
        pc = 1 - c if k & 1 else c
        out.append((k, (px, py, pc), 4 * px + 2 * py + pc))
    return out


def _exchange_start(gathers, scatters, after, name):
    arrays = list(gathers) + list(scatters)
    n, ng = len(arrays), len(gathers)
    hbm = pl.BlockSpec(memory_space=pltpu.HBM)
    sem = pl.BlockSpec(memory_space=pltpu.SEMAPHORE)

    extra = [] if after is None else [after]
    ne = len(extra)

    def body(*refs):
        ins, lands = refs[:n], refs[n:2 * n]
        send_sems, recv_sems = refs[2 * n + ne], refs[2 * n + ne + 1]
        token = refs[4 * n + ne + 2]
        x, y, c = lax.axis_index("x"), lax.axis_index("y"), lax.axis_index("c")
        me = 4 * x + 2 * y + c
        for t in range(n):
            for k, pos, peer in _peers(x, y, c):
                pltpu.make_async_remote_copy(
                    src_ref=ins[t] if t < ng else ins[t].at[peer], dst_ref=lands[t].at[me],
                    send_sem=send_sems.at[t * (N_DEV - 1) + k - 1], recv_sem=recv_sems.at[t * (N_DEV - 1) + k - 1],
                    device_id=pos, device_id_type=MESH).start()
        token[...] = jnp.zeros_like(token)

    me = 4 * lax.axis_index("x") + 2 * lax.axis_index("y") + lax.axis_index("c")
    lands = [lax.dynamic_update_index_in_dim(lax.empty((N_DEV,) + a.shape if t < ng else a.shape, a.dtype),
                                             a if t < ng else lax.dynamic_index_in_dim(a, me, 0, keepdims=False), me, 0)
             for t, a in enumerate(arrays)]
    operands = [pltpu.with_memory_space_constraint(a, pltpu.HBM) for a in arrays + lands]
    sems = pltpu.SemaphoreType.DMA((n * (N_DEV - 1),))
    res = pl.pallas_call(
        body, name=name,
        out_shape=(sems, sems, *[pltpu.HBM(a.shape, a.dtype) for a in arrays + lands], jax.ShapeDtypeStruct((8, 128), F32)),
        in_specs=[hbm] * (2 * n) + [pl.BlockSpec(memory_space=pl.ANY)] * ne,
        out_specs=(sem, sem, *[hbm] * (2 * n), pl.BlockSpec(memory_space=pltpu.VMEM)),
        input_output_aliases={i: 2 + i for i in range(2 * n)},
        compiler_params=pltpu.CompilerParams(has_side_effects=pltpu.SideEffectType.DATAFLOW_SIDE_EFFECTING),
    )(*operands, *extra)
    return res[-1][0, 0], (ng, res[0], res[1], list(res[2:2 + n]), list(res[2 + n:2 + 2 * n]))


def _exchange_wait(state, after, name):
    ng, send_sems, recv_sems, sent, lands = state
    n = len(sent)
    hbm = pl.BlockSpec(memory_space=pltpu.HBM)
    sem = pl.BlockSpec(memory_space=pltpu.SEMAPHORE)

    def body(*refs):
        ins, land_refs = refs[:n], refs[n:2 * n]
        send_ref, recv_ref = refs[2 * n], refs[2 * n + 1]
        x, y, c = lax.axis_index("x"), lax.axis_index("y"), lax.axis_index("c")
        me = 4 * x + 2 * y + c
        for t in range(n):
            for k, pos, peer in _peers(x, y, c):
                cp = pltpu.make_async_remote_copy(
                    src_ref=ins[t] if t < ng else ins[t].at[peer], dst_ref=land_refs[t].at[me],
                    send_sem=send_ref.at[t * (N_DEV - 1) + k - 1], recv_sem=recv_ref.at[t * (N_DEV - 1) + k - 1],
                    device_id=pos, device_id_type=MESH)
                cp.wait_send()
                cp.wait_recv()

    res = pl.pallas_call(
        body, name=name, out_shape=tuple(pltpu.HBM(a.shape, a.dtype) for a in sent + lands),
        in_specs=[hbm] * (2 * n) + [sem, sem, pl.BlockSpec(memory_space=pl.ANY)], out_specs=tuple([hbm] * (2 * n)),
        input_output_aliases={i: i for i in range(2 * n)},
        compiler_params=pltpu.CompilerParams(has_side_effects=pltpu.SideEffectType.DATAFLOW_SIDE_EFFECTING),
    )(*sent, *lands, send_sems, recv_sems, after)
    return list(res[n:])


def _prenorm(x, metapad, g_pre, dm):
    tm, tps, d = dm.TM, dm.TPS, dm.D

    def body(x_ref, mp_ref, g_ref, u_ref):
        j = pl.program_id(0) % tps
        h = jnp.where(j == 0, mp_ref[...], x_ref[0])
        r = lax.rsqrt(jnp.mean(h * h, axis=-1, keepdims=True) + EPS)
        u_ref[...] = (h * r * g_ref[...]).astype(BF16)

    return pl.pallas_call(
        body, name="prenorm", grid=(dm.Bl * tps,),
        in_specs=[pl.BlockSpec((1, tm, d), lambda i: (i // tps, jnp.maximum(i % tps - 1, 0), 0)),
                  pl.BlockSpec((tm, d), lambda i: (0, 0)),
                  pl.BlockSpec((1, d), lambda i: (0, 0))],
        out_specs=pl.BlockSpec((tm, d), lambda i: (i, 0)),
        out_shape=jax.ShapeDtypeStruct((dm.T, d), BF16), compiler_params=_cp(1),
    )(x, metapad, g_pre)


def _matmul_tn(a, b, out_dtype, name, tt=2304, tn=1024, tk=1024):
    t, k = a.shape
    n = b.shape[1]
    tt, tn, tk = _pick(t, tt, 16), _pick(n, tn, 128), _pick(k, tk, 128)
    nt = t // tt

    def body(a_ref, b_ref, o_ref, acc):
        p = _dot_tn(a_ref[...].astype(BF16), b_ref[...].astype(BF16))
        i = pl.program_id(2)

        @pl.when(i == 0)
        def _():
            acc[...] = p

        @pl.when(i > 0)
        def _():
            acc[...] += p

        @pl.when(i == nt - 1)
        def _():
            o_ref[...] = acc[...].astype(out_dtype)

    return pl.pallas_call(
        body, name=name, grid=(k // tk, n // tn, nt),
        in_specs=[pl.BlockSpec((tt, tk), lambda kk, j, i: (i, kk)), pl.BlockSpec((tt, tn), lambda kk, j, i: (i, j))],
        out_specs=pl.BlockSpec((tk, tn), lambda kk, j, i: (kk, j)),
        out_shape=jax.ShapeDtypeStruct((k, n), out_dtype), scratch_shapes=[pltpu.VMEM((tk, tn), F32)],
        compiler_params=_cp(3),
    )(a, b)


def _load_resident(hbm_refs, vmem_refs, sems):
    @pl.when(pl.program_id(0) == 0)
    def _():
        copies = [pltpu.make_async_copy(h, v, sems.at[i]) for i, (h, v) in enumerate(zip(hbm_refs, vmem_refs))]
        for cp in copies:
            cp.start()
        for cp in copies:
            cp.wait()


def _inproj(u, wts, dm):
    t, d = u.shape
    tm = _pick(t, 512, 16)
    np_ = len(wts)
    cn = 1024

    def body(*refs):
        u_ref, w_hbm, outs = refs[0], refs[1:1 + np_], refs[1 + np_:1 + 2 * np_]
        w_vmem, sems = refs[1 + 2 * np_:1 + 3 * np_], refs[1 + 3 * np_]
        _load_resident(w_hbm, w_vmem, sems)
        ut = u_ref[...]
        for w, o_ref in zip(w_vmem, outs):
            n = w.shape[0]
            step = cn if n % cn == 0 else n
            for j in range(0, n, step):
                o_ref[:, j:j + step] = _dot_nt(ut, w[j:j + step, :]).astype(BF16)

    return pl.pallas_call(
        body, name="inproj", grid=(t // tm,),
        in_specs=[pl.BlockSpec((tm, d), lambda i: (i, 0))] + [pl.BlockSpec(memory_space=pl.ANY)] * np_,
        out_specs=[pl.BlockSpec((tm, w.shape[0]), lambda i: (i, 0)) for w in wts],
        out_shape=[jax.ShapeDtypeStruct((t, w.shape[0]), BF16) for w in wts],
        scratch_shapes=[pltpu.VMEM(w.shape, BF16) for w in wts] + [pltpu.SemaphoreType.DMA((np_,))],
        compiler_params=_cp(1),
    )(u, *wts)


def _grad_u(d_parts, wts, dm):
    t = d_parts[0].shape[0]
    d = wts[0].shape[1]
    tm = _pick(t, 512, 16)
    np_ = len(wts)

    def body(*refs):
        d_refs, w_hbm, o_ref = refs[:np_], refs[np_:2 * np_], refs[2 * np_]
        w_vmem, sems, acc = refs[2 * np_ + 1:3 * np_ + 1], refs[3 * np_ + 1], refs[3 * np_ + 2]
        _load_resident(w_hbm, w_vmem, sems)
        acc[...] = _dot(d_refs[0][...].astype(BF16), w_vmem[0][...])
        for a_ref, w in zip(d_refs[1:], w_vmem[1:]):
            acc[...] += _dot(a_ref[...].astype(BF16), w[...])
        o_ref[...] = acc[...].astype(BF16)

    return pl.pallas_call(
        body, name="grad_u", grid=(t // tm,),
        in_specs=[pl.BlockSpec((tm, a.shape[1]), lambda i: (i, 0)) for a in d_parts] + [pl.BlockSpec(memory_space=pl.ANY)] * np_,
        out_specs=pl.BlockSpec((tm, d), lambda i: (i, 0)), out_shape=jax.ShapeDtypeStruct((t, d), BF16),
        scratch_shapes=[pltpu.VMEM(w.shape, BF16) for w in wts] + [pltpu.SemaphoreType.DMA((np_,)), pltpu.VMEM((tm, d), F32)],
        compiler_params=_cp(1),
    )(*d_parts, *wts)


def _conv_rows(dm):
    return _pick(dm.LP, 256, 16)


def _shifted(m, prev_row, next_row, rows):
    row = lax.broadcasted_iota(jnp.int32, m.shape, 0)
    m_prev = jnp.where(row == 0, prev_row, pltpu.roll(m, 1, 0))
    m_next = jnp.where(row == rows - 1, next_row, pltpu.roll(m, rows - 1, 0))
    return m_prev, m_next


def _conv_fwd(proj_a, conv_w, dm):
    lp, cw, rc = dm.LP, dm.CW, _conv_rows(dm)
    nchunk = lp // rc

    def body(p_ref, w_ref, y_ref):
        w0, w1, w2 = w_ref[0:1, :], w_ref[1:2, :], w_ref[2:3, :]

        def chunk(ci, carry):
            r0 = pl.multiple_of(ci * rc, rc)
            blk = p_ref[pl.ds(r0, rc), :].astype(F32)
            cb, cc, cx, cz = (blk[:, i * cw:(i + 1) * cw] for i in range(4))
            m = cc * cx
            rp = pl.multiple_of(jnp.maximum(r0 - 16, 0), 16)
            rn = pl.multiple_of(jnp.minimum(r0 + rc, lp - 16), 16)
            pv = p_ref[pl.ds(rp, 16), cw:3 * cw].astype(F32)
            nx = p_ref[pl.ds(rn, 16), cw:3 * cw].astype(F32)
            prev_row = jnp.where(ci > 0, pv[15:16, :cw] * pv[15:16, cw:], 0.0)
            next_row = jnp.where(ci < nchunk - 1, nx[0:1, :cw] * nx[0:1, cw:], 0.0)
            m_prev, m_next = _shifted(m, prev_row, next_row, rc)
            s = w0 * m_prev + w1 * m + w2 * m_next
            y_ref[pl.ds(r0, rc), :] = (cb * s * (cz * _sigmoid(cz))).astype(BF16)
            return carry

        lax.fori_loop(0, nchunk, chunk, 0)

    return pl.pallas_call(
        body, name="conv_fwd", grid=(dm.Bl, dm.NJ),
        in_specs=[pl.BlockSpec((lp, 4 * cw), lambda s, j: (s, j)), pl.BlockSpec((3, cw), lambda s, j: (0, j))],
        out_specs=pl.BlockSpec((lp, cw), lambda s, j: (s, j)),
        out_shape=jax.ShapeDtypeStruct((dm.T, dm.D), BF16), compiler_params=_cp(2),
    )(proj_a, conv_w)


def _conv_bwd(proj_a, dy_conv, conv_w, dm):
    lp, cw, rc = dm.LP, dm.CW, _conv_rows(dm)
    nchunk = lp // rc

    def body(p_ref, dy_ref, w_ref, d_ref, gw_ref):
        w0, w1, w2 = w_ref[0:1, :], w_ref[1:2, :], w_ref[2:3, :]

        def ds_of(p4, dy):
            cb, cz = p4[:, :cw], p4[:, 3 * cw:]
            return dy * cb * (cz * _sigmoid(cz))

        def chunk(ci, carry):
            g0, g1, g2 = carry
            r0 = pl.multiple_of(ci * rc, rc)
            blk = p_ref[pl.ds(r0, rc), :].astype(F32)
            dy = dy_ref[pl.ds(r0, rc), :].astype(F32)
            cb, cc, cx, cz = (blk[:, i * cw:(i + 1) * cw] for i in range(4))
            rp = pl.multiple_of(jnp.maximum(r0 - 16, 0), 16)
            rn = pl.multiple_of(jnp.minimum(r0 + rc, lp - 16), 16)
            pv = p_ref[pl.ds(rp, 16), :].astype(F32)[15:16]
            nx = p_ref[pl.ds(rn, 16), :].astype(F32)[0:1]
            dpv = dy_ref[pl.ds(rp, 16), :].astype(F32)[15:16]
            dnx = dy_ref[pl.ds(rn, 16), :].astype(F32)[0:1]
            has_prev, has_next = ci > 0, ci < nchunk - 1
            m = cc * cx
            m_prev, m_next = _shifted(m, jnp.where(has_prev, pv[:, cw:2 * cw] * pv[:, 2 * cw:3 * cw], 0.0),
                                      jnp.where(has_next, nx[:, cw:2 * cw] * nx[:, 2 * cw:3 * cw], 0.0), rc)
            s = w0 * m_prev + w1 * m + w2 * m_next
            sg = _sigmoid(cz)
            silu = cz * sg
            ds = dy * cb * silu
            ds_prev, ds_next = _shifted(ds, jnp.where(has_prev, ds_of(pv, dpv), 0.0),
                                        jnp.where(has_next, ds_of(nx, dnx), 0.0), rc)
            dm_ = w0 * ds_next + w1 * ds + w2 * ds_prev
            d_ref[pl.ds(r0, rc), 0:cw] = (dy * s * silu).astype(BF16)
            d_ref[pl.ds(r0, rc), cw:2 * cw] = (dm_ * cx).astype(BF16)
            d_ref[pl.ds(r0, rc), 2 * cw:3 * cw] = (dm_ * cc).astype(BF16)
            d_ref[pl.ds(r0, rc), 3 * cw:4 * cw] = (dy * cb * s * (sg * (1.0 + cz * (1.0 - sg)))).astype(BF16)
            return (g0 + jnp.sum(ds * m_prev, axis=0, keepdims=True), g1 + jnp.sum(ds * m, axis=0, keepdims=True),
                    g2 + jnp.sum(ds * m_next, axis=0, keepdims=True))

        z = jnp.zeros((1, cw), F32)
        g0, g1, g2 = lax.fori_loop(0, nchunk, chunk, (z, z, z))

        @pl.when(pl.program_id(1) == 0)
        def _():
            gw_ref[...] = jnp.zeros_like(gw_ref)

        gw_ref[0:1, :] += g0
        gw_ref[1:2, :] += g1
        gw_ref[2:3, :] += g2

    return pl.pallas_call(
        body, name="conv_bwd", grid=(dm.NJ, dm.Bl),
        in_specs=[pl.BlockSpec((lp, 4 * cw), lambda j, s: (s, j)), pl.BlockSpec((lp, cw), lambda j, s: (s, j)),
                  pl.BlockSpec((3, cw), lambda j, s: (0, j))],
        out_specs=[pl.BlockSpec((lp, 4 * cw), lambda j, s: (s, j)), pl.BlockSpec((8, cw), lambda j, s: (0, j))],
        out_shape=[jax.ShapeDtypeStruct((dm.T, 4 * dm.D), BF16), jax.ShapeDtypeStruct((8, dm.D), F32)],
        compiler_params=_cp(2),
    )(proj_a, dy_conv, conv_w)


def _interleave(gens):
    results = [None] * len(gens)
    live = list(range(len(gens)))
    while live:
        for idx in list(live):
            try:
                next(gens[idx])
            except StopIteration as done:
                results[idx] = done.value
                live.remove(idx)
    return results


def _group_chunks(dm):
    n = dm.NC - dm.C0
    return 3 if n % 3 == 0 else 1


def _group_masks(rows):
    ii = lax.broadcasted_iota(jnp.int32, (rows, rows), 0)
    jj = lax.broadcasted_iota(jnp.int32, (rows, rows), 1)
    same = jnp.right_shift(ii, CHUNK_SHIFT) == jnp.right_shift(jj, CHUNK_SHIFT)
    low, up = same & (jj <= ii), same & (jj >= ii)
    return low, same & (jj > ii), low.astype(BF16), up.astype(BF16)


def _first_row(chunk):
    return chunk * CHUNK if isinstance(chunk, int) else pl.multiple_of(chunk * CHUNK, CHUNK)


def _chunk_totals(b, fwd):
    hk = b.shape[1]
    rows = [b[c * CHUNK + CHUNK - 1:(c + 1) * CHUNK] if fwd else b[c * CHUNK:c * CHUNK + 1]
            for c in range(b.shape[0] // CHUNK)]
    return jnp.concatenate([jnp.broadcast_to(r, (CHUNK, hk)) for r in rows], axis=0)


def _log_gate(lr_rows, w_ref, b_ref, first_group, hk):
    z = _dot(lr_rows, w_ref[...]) + b_ref[...]
    e = jnp.exp(-jnp.abs(z))
    g = (jnp.minimum(z, 0.0) - jnp.log(1.0 + e)) * (1.0 / GATE_NORMALIZER)
    dg_dz = jnp.where(z >= 0.0, e, 1.0) / (1.0 + e) * (1.0 / GATE_NORMALIZER)
    row = lax.broadcasted_iota(jnp.int32, (lr_rows.shape[0], hk), 0)
    pad = first_group & (row < PAD_ROWS)
    return jnp.where(pad, 0.0, g), jnp.where(pad, 0.0, dg_dz)


def _gla_fwd(proj_b, lr, wg_f, bg_f, wg_b, bg_b, gla_g, dm):
    lp, hk, hv, nc, c0, hw = dm.LP, dm.HK, dm.HV, dm.NC, dm.C0, dm.HW
    scale = hk ** -0.5
    gc = _group_chunks(dm)
    gr, ng = gc * CHUNK, (nc - c0) // gc

    def body(p_ref, lr_ref, wf_ref, bf_ref, wb_ref, bb_ref, gg_ref, o_ref, y_ref, st_ref, b_out, gs_out, oacc_f, oacc_b):
        low_incl, up_strict, ones_low, ones_up = _group_masks(gr)
        if c0 > 0:
            zr = c0 * CHUNK
            o_ref[0:zr, :] = jnp.zeros((zr, hv), BF16)
            y_ref[0:zr, :] = jnp.zeros((zr, hv), BF16)
            b_out[:, 0:zr, :] = jnp.zeros((2, zr, hk), F32)
            gs_out[:, 0:zr, :] = jnp.zeros((2, zr, hk), F32)
            st_ref[0, 0, :, 0:c0] = jnp.zeros((2, c0, hv, hk), BF16)

        def decay(gi, fwd):
            w_ref, b_ref = (wf_ref, bf_ref) if fwd else (wb_ref, bb_ref)
            r0 = _first_row(c0 + gi * gc)
            yield
            g, dg_dz = _log_gate(lr_ref[pl.ds(r0, gr), :], w_ref, b_ref, gi == 0, hk)
            gs_out[0 if fwd else 1, pl.ds(r0, gr), :] = dg_dz
            yield
            b = _dot_exact01(ones_low if fwd else ones_up, g)
            b_out[0 if fwd else 1, pl.ds(r0, gr), :] = b
            return b

        def group(gi, st, b, fwd):
            oacc = oacc_f if fwd else oacc_b
            r0 = pl.multiple_of((c0 + gi * gc) * CHUNK, CHUNK)
            blk = p_ref[pl.ds(r0, gr), :]
            q = blk[:, :hk].astype(F32) * scale
            k = blk[:, hk:2 * hk].astype(F32)
            v = blk[:, 2 * hk:2 * hk + hv]
            btot = _chunk_totals(b, fwd)
            qi = (q * jnp.exp(b)).astype(BF16)
            ki = (k * jnp.exp(-b)).astype(BF16)
            kd = (k * jnp.exp(btot - b)).astype(BF16)
            dec = jnp.exp(btot)
            a = _dot_nt(qi, ki)
            yield
            o = _dot(jnp.where(low_incl if fwd else up_strict, a, 0.0).astype(BF16), v)
            for c in (range(gc) if fwd else reversed(range(gc))):
                yield
                rows = slice(c * CHUNK, (c + 1) * CHUNK)
                st_b = st.astype(BF16)
                st_ref[0, 0, 0 if fwd else 1, c0 + gi * gc + c] = st_b
                oacc[pl.ds(r0 + c * CHUNK, CHUNK), :] = o[rows] + _dot_nt(qi[rows], st_b)
                st = st * dec[c * CHUNK:c * CHUNK + 1] + _dot_tn(v[rows], kd[rows])
            return st

        def step(i, carry):
            st_f, st_b, b_f, b_b = carry
            gf, gb = i, ng - 1 - i
            return tuple(_interleave([group(gf, st_f, b_f, True), group(gb, st_b, b_b, False),
                                      decay(jnp.minimum(gf + 1, ng - 1), True), decay(jnp.maximum(gb - 1, 0), False)]))

        zero = jnp.zeros((hv, hk), F32)
        lax.fori_loop(0, ng, step, (zero, zero, *_interleave([decay(0, True), decay(ng - 1, False)])))

        def finish(i, carry):
            r0 = pl.multiple_of((c0 + i * gc) * CHUNK, CHUNK)
            o = oacc_f[pl.ds(r0, gr), :] + oacc_b[pl.ds(r0, gr), :]
            r = p_ref[pl.ds(r0, gr), 2 * hk + hv:].astype(F32)
            on = o * lax.rsqrt(jnp.mean(o * o, axis=-1, keepdims=True) + EPS) * gg_ref[...]
            o_ref[pl.ds(r0, gr), :] = o.astype(BF16)
            y_ref[pl.ds(r0, gr), :] = (on * r * _sigmoid(r)).astype(BF16)
            return carry

        lax.fori_loop(0, ng, finish, 0)

    head = lambda s, h: (s, h)
    wspec = pl.BlockSpec((LR_LANES, hk), lambda s, h: (0, h))
    bspec = pl.BlockSpec((1, hk), lambda s, h: (0, h))
    return pl.pallas_call(
        body, name="gla_fwd", grid=(dm.Bl, HEADS),
        in_specs=[pl.BlockSpec((lp, hw), head), pl.BlockSpec((lp, LR_LANES), lambda s, h: (s, 0)),
                  wspec, bspec, wspec, bspec, pl.BlockSpec((1, hv), lambda s, h: (0, 0))],
        out_specs=[pl.BlockSpec((lp, hv), head), pl.BlockSpec((lp, hv), head),
                   pl.BlockSpec((1, 1, 2, nc, hv, hk), lambda s, h: (s, h, 0, 0, 0, 0)),
                   pl.BlockSpec((2, lp, hk), lambda s, h: (0, s, h)), pl.BlockSpec((2, lp, hk), lambda s, h: (0, s, h))],
        out_shape=[jax.ShapeDtypeStruct((dm.T, dm.DV), BF16), jax.ShapeDtypeStruct((dm.T, dm.DV), BF16),
                   jax.ShapeDtypeStruct((dm.Bl, HEADS, 2, nc, hv, hk), BF16),
                   jax.ShapeDtypeStruct((2, dm.T, dm.DK), F32), jax.ShapeDtypeStruct((2, dm.T, dm.DK), F32)],
        scratch_shapes=[pltpu.VMEM((lp, hv), F32), pltpu.VMEM((lp, hv), F32)],
        compiler_params=_cp(2),
    )(proj_b, lr, wg_f, bg_f, wg_b, bg_b, gla_g)


def _gla_bwd(proj_b, lr, o_all, dy_gla, states, decays, gate_slopes, wg_f, wg_b, gla_g, dm):
    lp, hk, hv, nc, c0, hw = dm.LP, dm.HK, dm.HV, dm.NC, dm.C0, dm.HW
    scale = hk ** -0.5
    gc = _group_chunks(dm)
    gr, ng = gc * CHUNK, (nc - c0) // gc

    def body(p_ref, lr_ref, o_ref, dy_ref, st_ref, b_ref, gs_ref, wf_ref, wb_ref, gg_ref,
             d_ref, dlr_ref, gwf_ref, gbf_ref, gwb_ref, gbb_ref, ggg_ref, do_s, dq_s, dk_s, dv_s, dlr_s):
        low_incl, up_strict, ones_low, ones_up = _group_masks(gr)
        h = pl.program_id(1)

        @pl.when(h == 0)
        def _():
            dlr_ref[...] = jnp.zeros_like(dlr_ref)

        if c0 > 0:
            zr = c0 * CHUNK
            d_ref[0:zr, :] = jnp.zeros((zr, hw), BF16)
        for acc in (dq_s, dk_s, dv_s, dlr_s):
            acc[...] = jnp.zeros_like(acc)

        def norm_bwd(i, ggg):
            r0 = pl.multiple_of((c0 + i * gc) * CHUNK, CHUNK)
            o = o_ref[pl.ds(r0, gr), :].astype(F32)
            dy = dy_ref[pl.ds(r0, gr), :].astype(F32)
            r = p_ref[pl.ds(r0, gr), 2 * hk + hv:].astype(F32)
            rstd = lax.rsqrt(jnp.mean(o * o, axis=-1, keepdims=True) + EPS)
            ohat = o * rstd
            sg = _sigmoid(r)
            d_on = dy * (r * sg)
            d_ref[pl.ds(r0, gr), 2 * hk + hv:] = (dy * ohat * gg_ref[...] * (sg * (1.0 + r * (1.0 - sg)))).astype(BF16)
            d_oh = d_on * gg_ref[...]
            do_s[pl.ds(r0, gr), :] = (rstd * (d_oh - ohat * jnp.mean(d_oh * ohat, axis=-1, keepdims=True))).astype(BF16)
            return ggg + jnp.sum(d_on * ohat, axis=0, keepdims=True)

        ggg = lax.fori_loop(0, ng, norm_bwd, jnp.zeros((1, hv), F32))

        @pl.when((pl.program_id(0) == 0) & (h == 0))
        def _():
            ggg_ref[...] = jnp.zeros_like(ggg_ref)

        ggg_ref[0:1, :] += ggg

        def load(gi):
            r0 = pl.multiple_of((c0 + gi * gc) * CHUNK, CHUNK)
            blk = p_ref[pl.ds(r0, gr), :]
            return r0, blk[:, :hk].astype(F32) * scale, blk[:, hk:2 * hk].astype(F32), blk[:, 2 * hk:2 * hk + hv]

        zero = jnp.zeros((hv, hk), F32)

        def grad(gi, carry, fwd):
            dst, gw, gb = carry
            w_ref, way = (wf_ref, 0) if fwd else (wb_ref, 1)
            mask = low_incl if fwd else up_strict
            r0, q, k, v = load(gi)
            b = b_ref[way, pl.ds(r0, gr), :]
            btot = _chunk_totals(b, fwd)
            eb, enb, edb, dec = jnp.exp(b), jnp.exp(-b), jnp.exp(btot - b), jnp.exp(btot)
            qi_f, ki_f, kd_f = q * eb, k * enb, k * edb
            qi, ki, kd = qi_f.astype(BF16), ki_f.astype(BF16), kd_f.astype(BF16)
            do = do_s[pl.ds(r0, gr), :]
            a = _dot_nt(qi, ki)
            da = _dot_nt(do, v)
            yield
            a = jnp.where(mask, a, 0.0).astype(BF16)
            da = jnp.where(mask, da, 0.0).astype(BF16)
            dv = _dot_tn(a, do)
            dqi = _dot(da, ki)
            dki = _dot_tn(da, qi)
            dv_c, dqi_c, dkd_c, extra_c = [None] * gc, [None] * gc, [None] * gc, [None] * gc
            for c in (reversed(range(gc)) if fwd else range(gc)):
                yield
                rows = slice(c * CHUNK, (c + 1) * CHUNK)
                st = st_ref[0, 0, way, c0 + gi * gc + c]
                dsn_b = dst.astype(BF16)
                dec_c = dec[c * CHUNK:c * CHUNK + 1]
                dv_c[c] = dv[rows] + _dot_nt(kd[rows], dsn_b)
                dqi_c[c] = dqi[rows] + _dot(do[rows], st)
                dkd_c[c] = _dot(v[rows], dsn_b)
                ddec = jnp.sum(st.astype(F32) * dst, axis=0, keepdims=True)
                extra = jnp.sum(dkd_c[c] * kd_f[rows], axis=0, keepdims=True) + ddec * dec_c
                extra_c[c] = jnp.broadcast_to(extra, (CHUNK, hk))
                dst = dst * dec_c + _dot_tn(do[rows], qi[rows])
            yield
            dv, dqi = jnp.concatenate(dv_c, axis=0), jnp.concatenate(dqi_c, axis=0)
            dkd, extra = jnp.concatenate(dkd_c, axis=0), jnp.concatenate(extra_c, axis=0)
            dq_s[pl.ds(r0, gr), :] += dqi * eb * scale
            dk_s[pl.ds(r0, gr), :] += dki * enb + dkd * edb
            dv_s[pl.ds(r0, gr), :] += dv
            db = dqi * qi_f - dki * ki_f - dkd * kd_f
            dg = _dot_exact01(ones_up if fwd else ones_low, db) + extra
            yield
            dz = dg * gs_ref[way, pl.ds(r0, gr), :]
            dz_b = dz.astype(BF16)
            dlr_s[pl.ds(r0, gr), :] += _dot_nt(dz_b, w_ref[...])
            return dst, gw + _dot_tn(lr_ref[pl.ds(r0, gr), :], dz_b), gb + jnp.sum(dz, axis=0, keepdims=True)

        def grad_step(i, carry):
            return tuple(_interleave([grad(ng - 1 - i, carry[0], True), grad(i, carry[1], False)]))

        init = (zero, jnp.zeros((LR_LANES, hk), F32), jnp.zeros((1, hk), F32))
        (_, gw_f, gb_f), (_, gw_b, gb_b) = lax.fori_loop(0, ng, grad_step, (init, init))
        for gw_ref, gb_ref, gw, gb in ((gwf_ref, gbf_ref, gw_f, gb_f), (gwb_ref, gbb_ref, gw_b, gb_b)):
            gw_ref[0] = gw
            gb_ref[0] = jnp.zeros((8, hk), F32)
            gb_ref[0, 0:1, :] = gb

        def combine(i, carry):
            r0 = pl.multiple_of((c0 + i * gc) * CHUNK, CHUNK)
            d_ref[pl.ds(r0, gr), 0:hk] = dq_s[pl.ds(r0, gr), :].astype(BF16)
            d_ref[pl.ds(r0, gr), hk:2 * hk] = dk_s[pl.ds(r0, gr), :].astype(BF16)
            d_ref[pl.ds(r0, gr), 2 * hk:2 * hk + hv] = dv_s[pl.ds(r0, gr), :].astype(BF16)
            dlr_ref[pl.ds(r0, gr), :] += dlr_s[pl.ds(r0, gr), :]
            return carry

        lax.fori_loop(0, ng, combine, 0)

    head = lambda s, h: (s, h)
    wspec = pl.BlockSpec((LR_LANES, hk), lambda s, h: (0, h))
    gwspec = pl.BlockSpec((1, LR_LANES, hk), lambda s, h: (s, 0, h))
    gbspec = pl.BlockSpec((1, 8, hk), lambda s, h: (s, 0, h))
    gw_shape = jax.ShapeDtypeStruct((dm.Bl, LR_LANES, dm.DK), F32)
    gb_shape = jax.ShapeDtypeStruct((dm.Bl, 8, dm.DK), F32)
    both = pl.BlockSpec((2, lp, hk), lambda s, h: (0, s, h))
    return pl.pallas_call(
        body, name="gla_bwd", grid=(dm.Bl, HEADS),
        in_specs=[pl.BlockSpec((lp, hw), head), pl.BlockSpec((lp, LR_LANES), lambda s, h: (s, 0)),
                  pl.BlockSpec((lp, hv), head), pl.BlockSpec((lp, hv), head),
                  pl.BlockSpec((1, 1, 2, nc, hv, hk), lambda s, h: (s, h, 0, 0, 0, 0)), both, both,
                  wspec, wspec, pl.BlockSpec((1, hv), lambda s, h: (0, 0))],
        out_specs=[pl.BlockSpec((lp, hw), head), pl.BlockSpec((lp, LR_LANES), lambda s, h: (s, 0)),
                   gwspec, gbspec, gwspec, gbspec, pl.BlockSpec((8, hv), lambda s, h: (0, 0))],
        out_shape=[jax.ShapeDtypeStruct((dm.T, HEADS * hw), BF16), jax.ShapeDtypeStruct((dm.T, LR_LANES), F32),
                   gw_shape, gb_shape, gw_shape, gb_shape, jax.ShapeDtypeStruct((8, hv), F32)],
        scratch_shapes=[pltpu.VMEM((lp, hv), BF16), pltpu.VMEM((lp, hk), F32), pltpu.VMEM((lp, hk), F32),
                        pltpu.VMEM((lp, hv), F32), pltpu.VMEM((lp, LR_LANES), F32)],
        compiler_params=_cp(2),
    )(proj_b, lr, o_all, dy_gla, states, decays, gate_slopes, wg_f, wg_b, gla_g)


def _out_merge(y_conv, y_gla, proj_c, w_oc, w_og, dm):
    d = dm.D
    tm = _pick(dm.T, 512, 16)

    def body(yc_ref, yg_ref, c_ref, woc_ref, wog_ref, pc_ref, pg_ref, m_ref):
        pc = _dot(yc_ref[...], woc_ref[...])
        pg = _dot(yg_ref[...], wog_ref[...])
        pc_ref[...] = pc.astype(BF16)
        pg_ref[...] = pg.astype(BF16)
        ma = c_ref[:, :d].astype(F32)
        mb = c_ref[:, d:].astype(F32)
        m_ref[...] = (_sigmoid(ma) * pc + _sigmoid(mb) * pg).astype(BF16)

    row = pl.BlockSpec((tm, d), lambda i: (i, 0))
    full = pl.BlockSpec((d, d), lambda i: (0, 0))
    act = jax.ShapeDtypeStruct((dm.T, d), BF16)
    return pl.pallas_call(
        body, name="out_merge", grid=(dm.T // tm,),
        in_specs=[row, row, pl.BlockSpec((tm, 2 * d), lambda i: (i, 0)), full, full],
        out_specs=[row, row, row], out_shape=[act, act, act], compiler_params=_cp(1),
    )(y_conv, y_gla, proj_c, w_oc, w_og)


def _final_fwd(merged, w_out, x, metapad, target, g_post, dm):
    tm, tps, d = dm.TM, dm.TPS, dm.D

    def body(m_ref, w_ref, x_ref, mp_ref, t_ref, g_ref, dout_ref, dy_ref, st_ref):
        i = pl.program_id(0)
        j = i % tps
        out = _dot(m_ref[...], w_ref[...])
        rstd = lax.rsqrt(jnp.mean(out * out, axis=-1, keepdims=True) + EPS)
        ohat = out * rstd
        h = jnp.where(j == 0, mp_ref[...], x_ref[0])
        y = h + ohat * g_ref[...]
        err = jnp.where(j == 0, 0.0, y - t_ref[0])
        dy = err * (1.0 / d)
        d_oh = dy * g_ref[...]
        dout_ref[...] = (rstd * (d_oh - ohat * jnp.mean(d_oh * ohat, axis=-1, keepdims=True))).astype(BF16)
        dy_ref[...] = dy.astype(BF16)

        @pl.when(i == 0)
        def _():
            st_ref[...] = jnp.zeros_like(st_ref)

        st_ref[0:1, :] += jnp.sum(dy * ohat, axis=0, keepdims=True)
        st_ref[1:2, :] += jnp.sum(err * err, axis=0, keepdims=True)

    row = pl.BlockSpec((tm, d), lambda i: (i, 0))
    tok = pl.BlockSpec((1, tm, d), lambda i: (i // tps, jnp.maximum(i % tps - 1, 0), 0))
    const = lambda r: pl.BlockSpec((r, d), lambda i: (0, 0))
    return pl.pallas_call(
        body, name="final_fwd", grid=(dm.Bl * tps,),
        in_specs=[row, const(d), tok, const(tm), tok, const(1)],
        out_specs=[row, row, const(8)],
        out_shape=[jax.ShapeDtypeStruct((dm.T, d), BF16), jax.ShapeDtypeStruct((dm.T, d), BF16),
                   jax.ShapeDtypeStruct((8, d), F32)],
        compiler_params=_cp(1),
    )(merged, w_out, x, metapad, target, g_post)


def _merge_bwd(d_out, proj_c, p_conv, p_gla, w_out, w_oc, w_og, dm):
    d = dm.D
    tm = _pick(dm.T, 512, 16)

    def body(do_ref, c_ref, pc_ref, pg_ref, wo_ref, woc_ref, wog_ref, dpc_ref, dpg_ref, dc_ref, dyc_ref, dyg_ref):
        dmg = _dot_nt(do_ref[...], wo_ref[...])
        sa = _sigmoid(c_ref[:, :d].astype(F32))
        sb = _sigmoid(c_ref[:, d:].astype(F32))
        dpc = (dmg * sa).astype(BF16)
        dpg = (dmg * sb).astype(BF16)
        dpc_ref[...] = dpc
        dpg_ref[...] = dpg
        dc_ref[:, :d] = (dmg * pc_ref[...].astype(F32) * sa * (1.0 - sa)).astype(BF16)
        dc_ref[:, d:] = (dmg * pg_ref[...].astype(F32) * sb * (1.0 - sb)).astype(BF16)
        dyc_ref[...] = _dot_nt(dpc, woc_ref[...]).astype(BF16)
        dyg_ref[...] = _dot_nt(dpg, wog_ref[...]).astype(BF16)

    row = pl.BlockSpec((tm, d), lambda i: (i, 0))
    row2 = pl.BlockSpec((tm, 2 * d), lambda i: (i, 0))
    full = pl.BlockSpec((d, d), lambda i: (0, 0))
    act = jax.ShapeDtypeStruct((dm.T, d), BF16)
    return pl.pallas_call(
        body, name="merge_bwd", grid=(dm.T // tm,),
        in_specs=[row, row2, row, row, full, full, full],
        out_specs=[row, row, row2, row, row],
        out_shape=[act, act, jax.ShapeDtypeStruct((dm.T, 2 * d), BF16), act, act],
        compiler_params=_cp(1),
    )(d_out, proj_c, p_conv, p_gla, w_out, w_oc, w_og)


def _prenorm_bwd(du, dy, x, metapad, g_pre, dm):
    tm, tps, d = dm.TM, dm.TPS, dm.D

    def body(du_ref, dy_ref, x_ref, mp_ref, g_ref, gx_ref, dmeta_ref, gg_ref):
        i = pl.program_id(0)
        j = i % tps
        h = jnp.where(j == 0, mp_ref[...], x_ref[0])
        rstd = lax.rsqrt(jnp.mean(h * h, axis=-1, keepdims=True) + EPS)
        hhat = h * rstd
        du = du_ref[...].astype(F32)
        dug = du * g_ref[...]
        dh = dy_ref[...].astype(F32) + rstd * (dug - hhat * jnp.mean(dug * hhat, axis=-1, keepdims=True))

        @pl.when(j == 0)
        def _():
            dmeta_ref[0] = dh

        @pl.when(j > 0)
        def _():
            gx_ref[0] = dh

        @pl.when(i == 0)
        def _():
            gg_ref[...] = jnp.zeros_like(gg_ref)

        gg_ref[0:1, :] += jnp.sum(du * hhat, axis=0, keepdims=True)

    row = pl.BlockSpec((tm, d), lambda i: (i, 0))
    tok = pl.BlockSpec((1, tm, d), lambda i: (i // tps, jnp.maximum(i % tps - 1, 0), 0))
    const = lambda r: pl.BlockSpec((r, d), lambda i: (0, 0))
    return pl.pallas_call(
        body, name="prenorm_bwd", grid=(dm.Bl * tps,),
        in_specs=[row, row, tok, const(tm), const(1)],
        out_specs=[tok, pl.BlockSpec((1, tm, d), lambda i: (i // tps, 0, 0)), const(8)],
        out_shape=[jax.ShapeDtypeStruct((dm.Bl, dm.S, d), F32), jax.ShapeDtypeStruct((dm.Bl, tm, d), F32),
                   jax.ShapeDtypeStruct((8, d), F32)],
        compiler_params=_cp(1),
    )(du, dy, x, metapad, g_pre)


def _adamw(partials, w, m, v, name, by_columns=False):
    r, c = w.shape
    n_parts = partials.shape[0]
    tr, tc = (r, _pick(c, 128, 128)) if by_columns else (_pick(r, 256, 16), c)

    def body(p_ref, w_ref, m_ref, v_ref, g_ref, d_ref, nm_ref, nv_ref):
        g = p_ref[0].astype(F32)
        for j in range(1, n_parts):
            g = g + p_ref[j].astype(F32)
        g_ref[...] = g
        d_ref[...], nm_ref[...], nv_ref[...] = _adam_step(g, w_ref[...], m_ref[...], v_ref[...])

    at = (lambda i: (0, i)) if by_columns else (lambda i: (i, 0))
    tile = pl.BlockSpec((tr, tc), at)
    out = jax.ShapeDtypeStruct((r, c), F32)
    return pl.pallas_call(
        body, name=name, grid=(c // tc if by_columns else r // tr,),
        in_specs=[pl.BlockSpec((n_parts, tr, tc), lambda i: (0,) + at(i)), tile, tile, tile],
        out_specs=[tile, tile, tile, tile], out_shape=[out, out, out, out], compiler_params=_cp(1),
    )(partials, w, m, v)


def _adam_step(g, w, m, v):
    m2 = ADAM_B1 * m + (1.0 - ADAM_B1) * g
    v2 = ADAM_B2 * v + (1.0 - ADAM_B2) * (g * g)
    m_hat = m2 / (1.0 - ADAM_B1 ** ADAM_STEP)
    v_hat = v2 / (1.0 - ADAM_B2 ** ADAM_STEP)
    return -ADAM_LR * (m_hat / (jnp.sqrt(v_hat) + ADAM_EPS) + ADAM_WD * w), m2, v2


def _adamw_small(items, name):
    n = len(items)

    def body(*refs):
        ins, outs = refs[:4 * n], refs[4 * n:]
        for i in range(n):
            p_ref, w_ref, m_ref, v_ref = ins[4 * i:4 * i + 4]
            g = p_ref[0]
            for j in range(1, p_ref.shape[0]):
                g = g + p_ref[j]
            delta, m2, v2 = _adam_step(g, w_ref[...], m_ref[...], v_ref[...])
            for o_ref, val in zip(outs[4 * i:4 * i + 4], (g, delta, m2, v2)):
                o_ref[...] = val

    vmem = pl.BlockSpec(memory_space=pltpu.VMEM)
    res = pl.pallas_call(
        body, name=name, in_specs=[vmem] * (4 * n), out_specs=[vmem] * (4 * n),
        out_shape=[jax.ShapeDtypeStruct(w.shape, F32) for _, w, _, _ in items for _ in range(4)],
    )(*[a for item in items for a in item])
    return [res[4 * i:4 * i + 4] for i in range(n)]


def _pack_rows(wt, dm):
    d, dk, hk, hv, cw, nj = dm.D, dm.DK, dm.HK, dm.HV, dm.CW, dm.NJ
    a = wt[:4 * d].reshape(4, nj, cw, d).transpose(1, 0, 2, 3).reshape(4 * d, d)
    b = jnp.concatenate([wt[4 * d:4 * d + dk].reshape(HEADS, hk, d), wt[4 * d + dk:5 * d].reshape(HEADS, hk, d),
                         wt[5 * d:6 * d].reshape(HEADS, hv, d), wt[6 * d:7 * d].reshape(HEADS, hv, d)],
                        axis=1).reshape(3 * d, d)
    c = wt[7 * d + 2 * RANK:]
    lr = jnp.pad(wt[7 * d:7 * d + 2 * RANK], ((0, LR_LANES - 2 * RANK), (0, 0)))
    return a, b, c, lr


def _unpack_rows(a, b, c, lr, dm):
    d, hk, hv, cw, nj, hw = dm.D, dm.HK, dm.HV, dm.CW, dm.NJ, dm.HW
    conv = a.reshape(nj, 4, cw, d).transpose(1, 0, 2, 3).reshape(4 * d, d)
    heads = b.reshape(HEADS, hw, d)
    q = heads[:, :hk].reshape(HEADS * hk, d)
    k = heads[:, hk:2 * hk].reshape(HEADS * hk, d)
    v = heads[:, 2 * hk:2 * hk + hv].reshape(HEADS * hv, d)
    r = heads[:, 2 * hk + hv:].reshape(HEADS * hv, d)
    return jnp.concatenate([conv, q, k, v, r, lr[:2 * RANK], c], axis=0)


def _column_shards(g, shard_shape):
    r, c = g.shape
    return g.reshape(r, N_DEV, c // N_DEV).transpose(1, 0, 2).reshape((N_DEV,) + tuple(shard_shape))


def _join_column_shards(parts):
    r, c = parts.shape[-2:]
    return parts.reshape(N_DEV, r, c).transpose(1, 0, 2).reshape(r, N_DEV * c)


def _local_step(x, target, meta, g_pre, wt_in, conv_w, wg_f, bg_f, wg_b, bg_b, gla_g, out_weights, g_post,
                on_matrix_grads=None):
    bl, s, d = x.shape
    dm = _Dims(bl, s, d)
    metapad = jnp.concatenate([jnp.zeros((dm.TM - N_META, d), F32), meta], axis=0)
    wta, wtb, wtc, wtlr = _pack_rows(wt_in, dm)
    wgp_f = jnp.pad(wg_f, ((0, LR_LANES - RANK), (0, 0))).astype(BF16)
    wgp_b = jnp.pad(wg_b, ((RANK, LR_LANES - 2 * RANK), (0, 0))).astype(BF16)

    u = _prenorm(x, metapad, g_pre, dm)
    proj_a, proj_b, proj_c, lr = _inproj(u, [wta, wtb, wtc, wtlr], dm)
    y_conv = _conv_fwd(proj_a, conv_w, dm)
    o_all, y_gla, states, decays, gate_slopes = _gla_fwd(proj_b, lr, wgp_f, bg_f, wgp_b, bg_b, gla_g, dm)
    w_oc, w_og, w_out = out_weights(y_conv) if callable(out_weights) else out_weights
    p_conv, p_gla, merged = _out_merge(y_conv, y_gla, proj_c, w_oc, w_og, dm)
    d_out, dy, stats = _final_fwd(merged, w_out, x, metapad, target, g_post, dm)
    loss = 0.5 / d * jnp.sum(stats[1])

    d_pc, d_pg, d_c, dy_conv, dy_gla = _merge_bwd(d_out, proj_c, p_conv, p_gla, w_out, w_oc, w_og, dm)
    g_out = _matmul_tn(merged, d_out, BF16, "grad_w_out")
    g_oc = _matmul_tn(y_conv, d_pc, BF16, "grad_w_out_conv")
    g_og = _matmul_tn(y_gla, d_pg, BF16, "grad_w_out_gla")
    d_a, g_conv = _conv_bwd(proj_a, dy_conv, conv_w, dm)
    d_b, d_lr, gwp_f, gbp_f, gwp_b, gbp_b, g_gla = _gla_bwd(proj_b, lr, o_all, dy_gla, states, decays, gate_slopes, wgp_f, wgp_b, gla_g, dm)
    g_in = _unpack_rows(_matmul_tn(d_a, u, BF16, "grad_w_in_conv"), _matmul_tn(d_b, u, BF16, "grad_w_in_gla"),
                        _matmul_tn(d_c, u, BF16, "grad_w_in_merge"), _matmul_tn(d_lr, u, BF16, "grad_w_in_gate"), dm)
    if on_matrix_grads is not None:
        wtlr = wtlr + on_matrix_grads(dict(w_in=g_in, w_out_conv=g_oc, w_out_gla=g_og, w_merge_out=g_out)).astype(BF16)
    du = _grad_u([d_a, d_b, d_c, d_lr], [wta, wtb, wtc, wtlr], dm)
    grad_x, d_meta, g_pre_rows = _prenorm_bwd(du, dy, x, metapad, g_pre, dm)

    grads = dict(
        meta_tokens=jnp.sum(d_meta[:, dm.TM - N_META:, :], axis=0), norm_pre=g_pre_rows[0:1], w_in=g_in,
        conv_w=g_conv[0:3], w_gate_fwd=jnp.sum(gwp_f, axis=0)[:RANK], b_gate_fwd=jnp.sum(gbp_f, axis=0)[0:1],
        w_gate_bwd=jnp.sum(gwp_b, axis=0)[RANK:2 * RANK], b_gate_bwd=jnp.sum(gbp_b, axis=0)[0:1],
        gla_norm=g_gla[0:1], w_out_conv=g_oc, w_out_gla=g_og, w_merge_out=g_out, norm_post=stats[0:1])
    return loss, grad_x, grads


MATRICES = ("w_out_conv", "w_out_gla", "w_merge_out")
SMALL_SHARDED = ("meta_tokens", "conv_w", "w_gate_fwd", "w_gate_bwd")
REPLICATED = ("norm_pre", "b_gate_fwd", "b_gate_bwd", "gla_norm", "norm_post")
NAMES = ("meta_tokens", "norm_pre", "w_in", "conv_w", "w_gate_fwd", "b_gate_fwd", "w_gate_bwd", "b_gate_bwd", "gla_norm",
         "w_out_conv", "w_out_gla", "w_merge_out", "norm_post")


def kernel(x, meta_tokens, norm_pre, w_in, conv_w, w_gate_fwd, b_gate_fwd, w_gate_bwd, b_gate_bwd, gla_norm, w_out_conv, w_out_gla, w_merge_out, norm_post, loss_target, m_meta_tokens, m_norm_pre, m_w_in, m_conv_w, m_w_gate_fwd, m_b_gate_fwd, m_w_gate_bwd, m_b_gate_bwd, m_gla_norm, m_w_out_conv, m_w_out_gla, m_w_merge_out, m_norm_post, v_meta_tokens, v_norm_pre, v_w_in, v_conv_w, v_w_gate_fwd, v_b_gate_fwd, v_w_gate_bwd, v_b_gate_bwd, v_gla_norm, v_w_out_conv, v_w_out_gla, v_w_merge_out, v_norm_post):
    w = dict(meta_tokens=meta_tokens, norm_pre=norm_pre, w_in=w_in[0], conv_w=conv_w, w_gate_fwd=w_gate_fwd,
             b_gate_fwd=b_gate_fwd, w_gate_bwd=w_gate_bwd, b_gate_bwd=b_gate_bwd, gla_norm=gla_norm,
             w_out_conv=w_out_conv[0], w_out_gla=w_out_gla[0], w_merge_out=w_merge_out[0], norm_post=norm_post)
    m = dict(meta_tokens=m_meta_tokens, norm_pre=m_norm_pre, w_in=m_w_in[0], conv_w=m_conv_w, w_gate_fwd=m_w_gate_fwd,
             b_gate_fwd=m_b_gate_fwd, w_gate_bwd=m_w_gate_bwd, b_gate_bwd=m_b_gate_bwd, gla_norm=m_gla_norm,
             w_out_conv=m_w_out_conv[0], w_out_gla=m_w_out_gla[0], w_merge_out=m_w_merge_out[0], norm_post=m_norm_post)
    v = dict(meta_tokens=v_meta_tokens, norm_pre=v_norm_pre, w_in=v_w_in[0], conv_w=v_conv_w, w_gate_fwd=v_w_gate_fwd,
             b_gate_fwd=v_b_gate_fwd, w_gate_bwd=v_w_gate_bwd, b_gate_bwd=v_b_gate_bwd, gla_norm=v_gla_norm,
             w_out_conv=v_w_out_conv[0], w_out_gla=v_w_out_gla[0], w_merge_out=v_w_merge_out[0], norm_post=v_norm_post)
    d = x.shape[-1]

    wt_all, *small_all = _gather_two_level([w["w_in"].T.astype(BF16)] + [w[n] for n in SMALL_SHARDED], "gather_weights")
    _, late_weights = _exchange_start([w[n].astype(BF16) for n in MATRICES], [], small_all[0], "gather_out_weights_start")
    wt_in = wt_all.reshape(-1, d)
    small = {n: _join_column_shards(p) for n, p in zip(SMALL_SHARDED, small_all)}

    def out_weights(after):
        return tuple(a.reshape(-1, d) for a in _exchange_wait(late_weights, after, "gather_out_weights_wait"))

    pending = []

    def on_matrix_grads(g):
        to_send = [g[n].astype(BF16).reshape(N_DEV, -1, d) for n in ("w_in",) + MATRICES]
        token, state = _exchange_start([], to_send, None, "exchange_grads_start")
        pending.append(state)
        return token

    loss, grad_x, grads = _local_step(
        x, loss_target, small["meta_tokens"], norm_pre, wt_in, small["conv_w"], small["w_gate_fwd"], b_gate_fwd,
        small["w_gate_bwd"], b_gate_bwd, gla_norm, out_weights, norm_post, on_matrix_grads)
    loss = lax.psum(loss, ("x", "y", "c"))
    received = _exchange_wait(pending[0], grad_x, "exchange_grads_wait")

    small_recv = _exchange([grads[n] for n in REPLICATED], [_column_shards(grads[n], w[n].shape) for n in SMALL_SHARDED],
                           "exchange_small_grads")

    results = {"w_in": [r.T[None] for r in _adamw(received[0], w["w_in"].T, m["w_in"].T, v["w_in"].T, "adamw_w_in", by_columns=True)]}
    for n, partials in zip(MATRICES, received[1:4]):
        results[n] = [r[None] for r in _adamw(partials, w[n], m[n], v[n], "adamw_" + n)]
    small_names = REPLICATED + SMALL_SHARDED
    results.update(zip(small_names, _adamw_small([(p, w[n], m[n], v[n]) for n, p in zip(small_names, small_recv)], "adamw_small")))
    return (loss, grad_x, *[results[n][i] for i in range(4) for n in NAMES])
```

```python
import jax
import jax.numpy as jnp
from jax import lax
from jax.experimental import pallas as pl
from jax.experimental.pallas import tpu as pltpu

F32 = jnp.float32
BF16 = jnp.bfloat16
MESH = pl.DeviceIdType.MESH

N_META = 16
CHUNK = 64
CHUNK_SHIFT = 6
HEADS = 4
RANK = 16
LR_LANES = 128
PAD_ROWS = CHUNK - N_META
EPS = 1e-6
GATE_NORMALIZER = 16.0
N_DEV = 8
ADAM_LR, ADAM_B1, ADAM_B2, ADAM_EPS, ADAM_WD, ADAM_STEP = 0.001, 0.9, 0.999, 1e-08, 0.01, 10
VMEM_LIMIT_BYTES = 56 * 1024 * 1024


class _Dims:
    def __init__(self, bl, s, d):
        self.Bl, self.S, self.D = bl, s, d
        self.TM = 128 if s % 128 == 0 else CHUNK
        self.LP = self.TM + s
        self.T = bl * self.LP
        self.TPS = self.LP // self.TM
        self.NC = self.LP // CHUNK
        self.C0 = (self.TM - CHUNK) // CHUNK
        self.DK, self.DV = d // 2, d
        self.HK, self.HV = self.DK // HEADS, self.DV // HEADS
        self.HW = 2 * self.HK + 2 * self.HV
        self.CW = 256 if d % 256 == 0 and d > 256 else d // 4
        self.NJ = d // self.CW


def _pick(n, target, mult):
    t = min(n, target)
    while t >= mult:
        if n % t == 0 and t % mult == 0:
            return t
        t -= mult
    return n


def _cp(n_axes):
    return pltpu.CompilerParams(dimension_semantics=("arbitrary",) * n_axes, vmem_limit_bytes=VMEM_LIMIT_BYTES)


def _sigmoid(x):
    return 1.0 / (1.0 + jnp.exp(-x))


def _dot(a, b):
    return jnp.dot(a, b, preferred_element_type=F32)


def _dot_nt(a, b):
    return lax.dot_general(a, b, (((1,), (1,)), ((), ())), preferred_element_type=F32)


def _dot_tn(a, b):
    return lax.dot_general(a, b, (((0,), (0,)), ((), ())), preferred_element_type=F32)


def _dot_exact01(m01, x):
    hi = x.astype(BF16)
    lo = (x - hi.astype(F32)).astype(BF16)
    return _dot(m01, hi) + _dot(m01, lo)


def _exchange(gathers, scatters, name):
    arrays = list(gathers) + list(scatters)
    n, ng = len(arrays), len(gathers)

    def body(*refs):
        ins, outs = refs[:n], refs[n:2 * n]
        send_sems, recv_sems, local_sems = refs[2 * n:]
        x, y, c = lax.axis_index("x"), lax.axis_index("y"), lax.axis_index("c")
        me = 4 * x + 2 * y + c
        started = []
        for t in range(n):
            src, dst = ins[t], outs[t]
            own = pltpu.make_async_copy(src if t < ng else src.at[me], dst.at[me], local_sems.at[t])
            own.start()
            started.append(own)
            for k, pos, peer in _peers(x, y, c):
                cp = pltpu.make_async_remote_copy(
                    src_ref=src if t < ng else src.at[peer], dst_ref=dst.at[me],
                    send_sem=send_sems.at[t * (N_DEV - 1) + k - 1], recv_sem=recv_sems.at[t * (N_DEV - 1) + k - 1],
                    device_id=pos, device_id_type=MESH)
                cp.start()
                started.append(cp)
        for cp in started:
            cp.wait()

    out_shape = [jax.ShapeDtypeStruct((N_DEV,) + a.shape if t < ng else a.shape, a.dtype) for t, a in enumerate(arrays)]
    any_spec = pl.BlockSpec(memory_space=pl.ANY)
    return pl.pallas_call(
        body, name=name, out_shape=out_shape, in_specs=[any_spec] * n, out_specs=[any_spec] * n,
        scratch_shapes=[pltpu.SemaphoreType.DMA((n * (N_DEV - 1),)), pltpu.SemaphoreType.DMA((n * (N_DEV - 1),)),
                        pltpu.SemaphoreType.DMA((n,))],
        compiler_params=pltpu.CompilerParams(has_side_effects=True),
    )(*arrays)


def _gather_two_level(arrays, name, side=None):
    n = len(arrays)
    per = N_DEV - 1
    work, side_in, side_in_specs, side_out, side_out_specs, side_scratch = side or (None, [], [], [], [], [])
    n_in, n_out = len(side_in), len(side_out)

    def body(*refs):
        ins, outs = refs[:n], refs[n + n_in:2 * n + n_in]
        send_sems, recv_sems, local_sems = refs[2 * n + n_in + n_out:2 * n + n_in + n_out + 3]
        x, y, c = lax.axis_index("x"), lax.axis_index("y"), lax.axis_index("c")
        sibling = (x, y, 1 - c)
        chips = [(1 - x, y), (x, 1 - y), (1 - x, 1 - y)]
        index = lambda px, py, pc: 4 * px + 2 * py + pc

        def copy(t, k, block, to, from_input=False):
            slab = outs[t].at[index(*block)]
            return pltpu.make_async_remote_copy(
                src_ref=ins[t] if from_input else slab, dst_ref=slab, send_sem=send_sems.at[t * per + k],
                recv_sem=recv_sems.at[t * per + k], device_id=to, device_id_type=MESH)

        own, sent = [], []
        for t in range(n):
            own.append(pltpu.make_async_copy(ins[t], outs[t].at[index(x, y, c)], local_sems.at[t]))
            own[-1].start()
            first = [copy(t, 0, (x, y, c), sibling, True)]
            first += [copy(t, 1 + j, (x, y, c), (*chip, c), True) for j, chip in enumerate(chips)]
            for cp in first:
                cp.start()
            sent += first
        if work is not None:
            work(refs[n:n + n_in], refs[2 * n + n_in:2 * n + n_in + n_out], refs[2 * n + n_in + n_out + 3:])
        for t in range(n):
            for j, chip in enumerate(chips):
                copy(t, 1 + j, (*chip, c), (x, y, c)).wait_recv()
                sent.append(copy(t, 4 + j, (*chip, c), sibling))
                sent[-1].start()
        for t in range(n):
            copy(t, 0, sibling, (x, y, c)).wait_recv()
            for j, chip in enumerate(chips):
                copy(t, 4 + j, (*chip, 1 - c), (x, y, c)).wait_recv()
        for cp in sent:
            cp.wait_send()
        for cp in own:
            cp.wait()

    out_shape = [jax.ShapeDtypeStruct((N_DEV,) + a.shape, a.dtype) for a in arrays]
    any_spec = pl.BlockSpec(memory_space=pl.ANY)
    return pl.pallas_call(
        body, name=name, out_shape=out_shape + list(side_out), in_specs=[any_spec] * n + list(side_in_specs),
        out_specs=[any_spec] * n + list(side_out_specs),
        scratch_shapes=[pltpu.SemaphoreType.DMA((n * per,)), pltpu.SemaphoreType.DMA((n * per,)),
                        pltpu.SemaphoreType.DMA((n,))] + list(side_scratch),
        compiler_params=pltpu.CompilerParams(has_side_effects=True, vmem_limit_bytes=VMEM_LIMIT_BYTES),
    )(*arrays, *side_in)


def _peers(x, y, c):
    out = []
    for k in range(1, N_DEV):
        px = 1 - x if (k >> 2) & 1 else x
        py = 1 - y if (k >> 1) & 1 else y
        pc = 1 - c if k & 1 else c
        out.append((k, (px, py, pc), 4 * px + 2 * py + pc))
    return out


def _exchange_start(gathers, scatters, after, name):
    arrays = list(gathers) + list(scatters)
    n, ng = len(arrays), len(gathers)
    hbm = pl.BlockSpec(memory_space=pltpu.HBM)
    sem = pl.BlockSpec(memory_space=pltpu.SEMAPHORE)

    extra = [] if after is None else [after]
    ne = len(extra)

    def body(*refs):
        ins, lands = refs[:n], refs[n:2 * n]
        send_sems, recv_sems = refs[2 * n + ne], refs[2 * n + ne + 1]
        token = refs[4 * n + ne + 2]
        x, y, c = lax.axis_index("x"), lax.axis_index("y"), lax.axis_index("c")
        me = 4 * x + 2 * y + c
        for t in range(n):
            for k, pos, peer in _peers(x, y, c):
                pltpu.make_async_remote_copy(
                    src_ref=ins[t] if t < ng else ins[t].at[peer], dst_ref=lands[t].at[me],
                    send_sem=send_sems.at[t * (N_DEV - 1) + k - 1], recv_sem=recv_sems.at[t * (N_DEV - 1) + k - 1],
                    device_id=pos, device_id_type=MESH).start()
        token[...] = jnp.zeros_like(token)

    me = 4 * lax.axis_index("x") + 2 * lax.axis_index("y") + lax.axis_index("c")
    lands = [lax.dynamic_update_index_in_dim(lax.empty((N_DEV,) + a.shape if t < ng else a.shape, a.dtype),
                                             a if t < ng else lax.dynamic_index_in_dim(a, me, 0, keepdims=False), me, 0)
             for t, a in enumerate(arrays)]
    operands = [pltpu.with_memory_space_constraint(a, pltpu.HBM) for a in arrays + lands]
    sems = pltpu.SemaphoreType.DMA((n * (N_DEV - 1),))
    res = pl.pallas_call(
        body, name=name,
        out_shape=(sems, sems, *[pltpu.HBM(a.shape, a.dtype) for a in arrays + lands], jax.ShapeDtypeStruct((8, 128), F32)),
        in_specs=[hbm] * (2 * n) + [pl.BlockSpec(memory_space=pl.ANY)] * ne,
        out_specs=(sem, sem, *[hbm] * (2 * n), pl.BlockSpec(memory_space=pltpu.VMEM)),
        input_output_aliases={i: 2 + i for i in range(2 * n)},
        compiler_params=pltpu.CompilerParams(has_side_effects=pltpu.SideEffectType.DATAFLOW_SIDE_EFFECTING),
    )(*operands, *extra)
    return res[-1][0, 0], (ng, res[0], res[1], list(res[2:2 + n]), list(res[2 + n:2 + 2 * n]))


def _exchange_wait(state, after, name):
    ng, send_sems, recv_sems, sent, lands = state
    n = len(sent)
    hbm = pl.BlockSpec(memory_space=pltpu.HBM)
    sem = pl.BlockSpec(memory_space=pltpu.SEMAPHORE)

    def body(*refs):
        ins, land_refs = refs[:n], refs[n:2 * n]
        send_ref, recv_ref = refs[2 * n], refs[2 * n + 1]
        x, y, c = lax.axis_index("x"), lax.axis_index("y"), lax.axis_index("c")
        me = 4 * x + 2 * y + c
        for t in range(n):
            for k, pos, peer in _peers(x, y, c):
                cp = pltpu.make_async_remote_copy(
                    src_ref=ins[t] if t < ng else ins[t].at[peer], dst_ref=land_refs[t].at[me],
                    send_sem=send_ref.at[t * (N_DEV - 1) + k - 1], recv_sem=recv_ref.at[t * (N_DEV - 1) + k - 1],
                    device_id=pos, device_id_type=MESH)
                cp.wait_send()
                cp.wait_recv()

    res = pl.pallas_call(
        body, name=name, out_shape=tuple(pltpu.HBM(a.shape, a.dtype) for a in sent + lands),
        in_specs=[hbm] * (2 * n) + [sem, sem, pl.BlockSpec(memory_space=pl.ANY)], out_specs=tuple([hbm] * (2 * n)),
        input_output_aliases={i: i for i in range(2 * n)},
        compiler_params=pltpu.CompilerParams(has_side_effects=pltpu.SideEffectType.DATAFLOW_SIDE_EFFECTING),
    )(*sent, *lands, send_sems, recv_sems, after)
    return list(res[n:])


def _rms_scaled(h, g):
    return (h * lax.rsqrt(jnp.mean(h * h, axis=-1, keepdims=True) + EPS) * g).astype(BF16)


def _prenorm_tokens_side(x, g_pre, dm):
    bl, s, d = x.shape
    rows = _pick(s, 512, 16)
    tiles = [(b, j) for b in range(bl) for j in range(s // rows)]

    def work(ins, outs, scratch):
        (x_ref, g_ref), (u_ref,), (xbuf, ubuf, sem_in, sem_out) = ins, outs, scratch

        def load(t, slot):
            b, j = tiles[t]
            return pltpu.make_async_copy(x_ref.at[b, pl.ds(j * rows, rows), :], xbuf.at[slot], sem_in.at[slot])

        def store(t, slot):
            b, j = tiles[t]
            return pltpu.make_async_copy(ubuf.at[slot], u_ref.at[pl.ds(b * dm.LP + dm.TM + j * rows, rows), :], sem_out.at[slot])

        load(0, 0).start()
        for t in range(len(tiles)):
            slot = t % 2
            if t + 1 < len(tiles):
                load(t + 1, 1 - slot).start()
            load(t, slot).wait()
            if t >= 2:
                store(t - 2, slot).wait()
            ubuf[slot] = _rms_scaled(xbuf[slot], g_ref[...])
            store(t, slot).start()
        for t in range(max(len(tiles) - 2, 0), len(tiles)):
            store(t, t % 2).wait()

    any_spec = pl.BlockSpec(memory_space=pl.ANY)
    return (work, [x, g_pre], [any_spec, pl.BlockSpec(memory_space=pltpu.VMEM)],
            [jax.ShapeDtypeStruct((dm.T, d), BF16)], [any_spec],
            [pltpu.VMEM((2, rows, d), F32), pltpu.VMEM((2, rows, d), BF16), pltpu.SemaphoreType.DMA((2,)),
             pltpu.SemaphoreType.DMA((2,))])


def _prenorm_meta(u, metapad, g_pre, dm):
    tm, tps, d = dm.TM, dm.TPS, dm.D

    def body(u_in, mp_ref, g_ref, u_ref):
        u_ref[...] = _rms_scaled(mp_ref[...], g_ref[...])

    return pl.pallas_call(
        body, name="prenorm_meta", grid=(dm.Bl,),
        in_specs=[pl.BlockSpec(memory_space=pl.ANY), pl.BlockSpec((tm, d), lambda i: (0, 0)),
                  pl.BlockSpec((1, d), lambda i: (0, 0))],
        out_specs=pl.BlockSpec((tm, d), lambda i: (i * tps, 0)),
        out_shape=jax.ShapeDtypeStruct((dm.T, d), BF16), input_output_aliases={0: 0}, compiler_params=_cp(1),
    )(u, metapad, g_pre)


def _matmul_tn(a, b, out_dtype, name, tt=2304, tn=1024, tk=1024):
    t, k = a.shape
    n = b.shape[1]
    tt, tn, tk = _pick(t, tt, 16), _pick(n, tn, 128), _pick(k, tk, 128)
    nt = t // tt

    def body(a_ref, b_ref, o_ref, acc):
        p = _dot_tn(a_ref[...].astype(BF16), b_ref[...].astype(BF16))
        i = pl.program_id(2)

        @pl.when(i == 0)
        def _():
            acc[...] = p

        @pl.when(i > 0)
        def _():
            acc[...] += p

        @pl.when(i == nt - 1)
        def _():
            o_ref[...] = acc[...].astype(out_dtype)

    return pl.pallas_call(
        body, name=name, grid=(k // tk, n // tn, nt),
        in_specs=[pl.BlockSpec((tt, tk), lambda kk, j, i: (i, kk)), pl.BlockSpec((tt, tn), lambda kk, j, i: (i, j))],
        out_specs=pl.BlockSpec((tk, tn), lambda kk, j, i: (kk, j)),
        out_shape=jax.ShapeDtypeStruct((k, n), out_dtype), scratch_shapes=[pltpu.VMEM((tk, tn), F32)],
        compiler_params=_cp(3),
    )(a, b)


def _load_resident(hbm_refs, vmem_refs, sems):
    @pl.when(pl.program_id(0) == 0)
    def _():
        copies = [pltpu.make_async_copy(h, v, sems.at[i]) for i, (h, v) in enumerate(zip(hbm_refs, vmem_refs))]
        for cp in copies:
            cp.start()
        for cp in copies:
            cp.wait()


def _inproj(u, wts, dm):
    t, d = u.shape
    tm = _pick(t, 512, 16)
    np_ = len(wts)
    cn = 1024

    def body(*refs):
        u_ref, w_hbm, outs = refs[0], refs[1:1 + np_], refs[1 + np_:1 + 2 * np_]
        w_vmem, sems = refs[1 + 2 * np_:1 + 3 * np_], refs[1 + 3 * np_]
        _load_resident(w_hbm, w_vmem, sems)
        ut = u_ref[...]
        for w, o_ref in zip(w_vmem, outs):
            n = w.shape[0]
            step = cn if n % cn == 0 else n
            for j in range(0, n, step):
                o_ref[:, j:j + step] = _dot_nt(ut, w[j:j + step, :]).astype(BF16)

    return pl.pallas_call(
        body, name="inproj", grid=(t // tm,),
        in_specs=[pl.BlockSpec((tm, d), lambda i: (i, 0))] + [pl.BlockSpec(memory_space=pl.ANY)] * np_,
        out_specs=[pl.BlockSpec((tm, w.shape[0]), lambda i: (i, 0)) for w in wts],
        out_shape=[jax.ShapeDtypeStruct((t, w.shape[0]), BF16) for w in wts],
        scratch_shapes=[pltpu.VMEM(w.shape, BF16) for w in wts] + [pltpu.SemaphoreType.DMA((np_,))],
        compiler_params=_cp(1),
    )(u, *wts)


def _grad_u(d_parts, wts, dm):
    t = d_parts[0].shape[0]
    d = wts[0].shape[1]
    tm = _pick(t, 512, 16)
    np_ = len(wts)

    def body(*refs):
        d_refs, w_hbm, o_ref = refs[:np_], refs[np_:2 * np_], refs[2 * np_]
        w_vmem, sems, acc = refs[2 * np_ + 1:3 * np_ + 1], refs[3 * np_ + 1], refs[3 * np_ + 2]
        _load_resident(w_hbm, w_vmem, sems)
        acc[...] = _dot(d_refs[0][...].astype(BF16), w_vmem[0][...])
        for a_ref, w in zip(d_refs[1:], w_vmem[1:]):
            acc[...] += _dot(a_ref[...].astype(BF16), w[...])
        o_ref[...] = acc[...].astype(BF16)

    return pl.pallas_call(
        body, name="grad_u", grid=(t // tm,),
        in_specs=[pl.BlockSpec((tm, a.shape[1]), lambda i: (i, 0)) for a in d_parts] + [pl.BlockSpec(memory_space=pl.ANY)] * np_,
        out_specs=pl.BlockSpec((tm, d), lambda i: (i, 0)), out_shape=jax.ShapeDtypeStruct((t, d), BF16),
        scratch_shapes=[pltpu.VMEM(w.shape, BF16) for w in wts] + [pltpu.SemaphoreType.DMA((np_,)), pltpu.VMEM((tm, d), F32)],
        compiler_params=_cp(1),
    )(*d_parts, *wts)


def _conv_rows(dm):
    return _pick(dm.LP, 256, 16)


def _shifted(m, prev_row, next_row, rows):
    row = lax.broadcasted_iota(jnp.int32, m.shape, 0)
    m_prev = jnp.where(row == 0, prev_row, pltpu.roll(m, 1, 0))
    m_next = jnp.where(row == rows - 1, next_row, pltpu.roll(m, rows - 1, 0))
    return m_prev, m_next


def _conv_fwd(proj_a, conv_w, dm):
    lp, cw, rc = dm.LP, dm.CW, _conv_rows(dm)
    nchunk = lp // rc

    def body(p_ref, w_ref, y_ref):
        w0, w1, w2 = w_ref[0:1, :], w_ref[1:2, :], w_ref[2:3, :]

        def chunk(ci, carry):
            r0 = pl.multiple_of(ci * rc, rc)
            blk = p_ref[pl.ds(r0, rc), :].astype(F32)
            cb, cc, cx, cz = (blk[:, i * cw:(i + 1) * cw] for i in range(4))
            m = cc * cx
            rp = pl.multiple_of(jnp.maximum(r0 - 16, 0), 16)
            rn = pl.multiple_of(jnp.minimum(r0 + rc, lp - 16), 16)
            pv = p_ref[pl.ds(rp, 16), cw:3 * cw].astype(F32)
            nx = p_ref[pl.ds(rn, 16), cw:3 * cw].astype(F32)
            prev_row = jnp.where(ci > 0, pv[15:16, :cw] * pv[15:16, cw:], 0.0)
            next_row = jnp.where(ci < nchunk - 1, nx[0:1, :cw] * nx[0:1, cw:], 0.0)
            m_prev, m_next = _shifted(m, prev_row, next_row, rc)
            s = w0 * m_prev + w1 * m + w2 * m_next
            y_ref[pl.ds(r0, rc), :] = (cb * s * (cz * _sigmoid(cz))).astype(BF16)
            return carry

        lax.fori_loop(0, nchunk, chunk, 0)

    return pl.pallas_call(
        body, name="conv_fwd", grid=(dm.Bl, dm.NJ),
        in_specs=[pl.BlockSpec((lp, 4 * cw), lambda s, j: (s, j)), pl.BlockSpec((3, cw), lambda s, j: (0, j))],
        out_specs=pl.BlockSpec((lp, cw), lambda s, j: (s, j)),
        out_shape=jax.ShapeDtypeStruct((dm.T, dm.D), BF16), compiler_params=_cp(2),
    )(proj_a, conv_w)


def _conv_bwd(proj_a, dy_conv, conv_w, dm):
    lp, cw, rc = dm.LP, dm.CW, _conv_rows(dm)
    nchunk = lp // rc

    def body(p_ref, dy_ref, w_ref, d_ref, gw_ref):
        w0, w1, w2 = w_ref[0:1, :], w_ref[1:2, :], w_ref[2:3, :]

        def ds_of(p4, dy):
            cb, cz = p4[:, :cw], p4[:, 3 * cw:]
            return dy * cb * (cz * _sigmoid(cz))

        def chunk(ci, carry):
            g0, g1, g2 = carry
            r0 = pl.multiple_of(ci * rc, rc)
            blk = p_ref[pl.ds(r0, rc), :].astype(F32)
            dy = dy_ref[pl.ds(r0, rc), :].astype(F32)
            cb, cc, cx, cz = (blk[:, i * cw:(i + 1) * cw] for i in range(4))
            rp = pl.multiple_of(jnp.maximum(r0 - 16, 0), 16)
            rn = pl.multiple_of(jnp.minimum(r0 + rc, lp - 16), 16)
            pv = p_ref[pl.ds(rp, 16), :].astype(F32)[15:16]
            nx = p_ref[pl.ds(rn, 16), :].astype(F32)[0:1]
            dpv = dy_ref[pl.ds(rp, 16), :].astype(F32)[15:16]
            dnx = dy_ref[pl.ds(rn, 16), :].astype(F32)[0:1]
            has_prev, has_next = ci > 0, ci < nchunk - 1
            m = cc * cx
            m_prev, m_next = _shifted(m, jnp.where(has_prev, pv[:, cw:2 * cw] * pv[:, 2 * cw:3 * cw], 0.0),
                                      jnp.where(has_next, nx[:, cw:2 * cw] * nx[:, 2 * cw:3 * cw], 0.0), rc)
            s = w0 * m_prev + w1 * m + w2 * m_next
            sg = _sigmoid(cz)
            silu = cz * sg
            ds = dy * cb * silu
            ds_prev, ds_next = _shifted(ds, jnp.where(has_prev, ds_of(pv, dpv), 0.0),
                                        jnp.where(has_next, ds_of(nx, dnx), 0.0), rc)
            dm_ = w0 * ds_next + w1 * ds + w2 * ds_prev
            d_ref[pl.ds(r0, rc), 0:cw] = (dy * s * silu).astype(BF16)
            d_ref[pl.ds(r0, rc), cw:2 * cw] = (dm_ * cx).astype(BF16)
            d_ref[pl.ds(r0, rc), 2 * cw:3 * cw] = (dm_ * cc).astype(BF16)
            d_ref[pl.ds(r0, rc), 3 * cw:4 * cw] = (dy * cb * s * (sg * (1.0 + cz * (1.0 - sg)))).astype(BF16)
            return (g0 + jnp.sum(ds * m_prev, axis=0, keepdims=True), g1 + jnp.sum(ds * m, axis=0, keepdims=True),
                    g2 + jnp.sum(ds * m_next, axis=0, keepdims=True))

        z = jnp.zeros((1, cw), F32)
        g0, g1, g2 = lax.fori_loop(0, nchunk, chunk, (z, z, z))

        @pl.when(pl.program_id(1) == 0)
        def _():
            gw_ref[...] = jnp.zeros_like(gw_ref)

        gw_ref[0:1, :] += g0
        gw_ref[1:2, :] += g1
        gw_ref[2:3, :] += g2

    return pl.pallas_call(
        body, name="conv_bwd", grid=(dm.NJ, dm.Bl),
        in_specs=[pl.BlockSpec((lp, 4 * cw), lambda j, s: (s, j)), pl.BlockSpec((lp, cw), lambda j, s: (s, j)),
                  pl.BlockSpec((3, cw), lambda j, s: (0, j))],
        out_specs=[pl.BlockSpec((lp, 4 * cw), lambda j, s: (s, j)), pl.BlockSpec((8, cw), lambda j, s: (0, j))],
        out_shape=[jax.ShapeDtypeStruct((dm.T, 4 * dm.D), BF16), jax.ShapeDtypeStruct((8, dm.D), F32)],
        compiler_params=_cp(2),
    )(proj_a, dy_conv, conv_w)


def _interleave(gens):
    results = [None] * len(gens)
    live = list(range(len(gens)))
    while live:
        for idx in list(live):
            try:
                next(gens[idx])
            except StopIteration as done:
                results[idx] = done.value
                live.remove(idx)
    return results


def _group_chunks(dm):
    n = dm.NC - dm.C0
    return 3 if n % 3 == 0 else 1


def _group_masks(rows):
    ii = lax.broadcasted_iota(jnp.int32, (rows, rows), 0)
    jj = lax.broadcasted_iota(jnp.int32, (rows, rows), 1)
    same = jnp.right_shift(ii, CHUNK_SHIFT) == jnp.right_shift(jj, CHUNK_SHIFT)
    low, up = same & (jj <= ii), same & (jj >= ii)
    return low, same & (jj > ii), low.astype(BF16), up.astype(BF16)


def _first_row(chunk):
    return chunk * CHUNK if isinstance(chunk, int) else pl.multiple_of(chunk * CHUNK, CHUNK)


def _chunk_totals(b, fwd):
    hk = b.shape[1]
    rows = [b[c * CHUNK + CHUNK - 1:(c + 1) * CHUNK] if fwd else b[c * CHUNK:c * CHUNK + 1]
            for c in range(b.shape[0] // CHUNK)]
    return jnp.concatenate([jnp.broadcast_to(r, (CHUNK, hk)) for r in rows], axis=0)


def _log_gate(lr_rows, w_ref, b_ref, first_group, hk):
    z = _dot(lr_rows, w_ref[...]) + b_ref[...]
    e = jnp.exp(-jnp.abs(z))
    g = (jnp.minimum(z, 0.0) - jnp.log(1.0 + e)) * (1.0 / GATE_NORMALIZER)
    dg_dz = jnp.where(z >= 0.0, e, 1.0) / (1.0 + e) * (1.0 / GATE_NORMALIZER)
    row = lax.broadcasted_iota(jnp.int32, (lr_rows.shape[0], hk), 0)
    pad = first_group & (row < PAD_ROWS)
    return jnp.where(pad, 0.0, g), jnp.where(pad, 0.0, dg_dz)


def _gla_fwd(proj_b, lr, wg_f, bg_f, wg_b, bg_b, gla_g, dm):
    lp, hk, hv, nc, c0, hw = dm.LP, dm.HK, dm.HV, dm.NC, dm.C0, dm.HW
    scale = hk ** -0.5
    gc = _group_chunks(dm)
    gr, ng = gc * CHUNK, (nc - c0) // gc

    def body(p_ref, lr_ref, wf_ref, bf_ref, wb_ref, bb_ref, gg_ref, o_ref, y_ref, st_ref, b_out, gs_out, oacc_f, oacc_b):
        low_incl, up_strict, ones_low, ones_up = _group_masks(gr)
        if c0 > 0:
            zr = c0 * CHUNK
            o_ref[0:zr, :] = jnp.zeros((zr, hv), BF16)
            y_ref[0:zr, :] = jnp.zeros((zr, hv), BF16)
            b_out[:, 0:zr, :] = jnp.zeros((2, zr, hk), F32)
            gs_out[:, 0:zr, :] = jnp.zeros((2, zr, hk), F32)
            st_ref[0, 0, :, 0:c0] = jnp.zeros((2, c0, hv, hk), BF16)

        def decay(gi, fwd):
            w_ref, b_ref = (wf_ref, bf_ref) if fwd else (wb_ref, bb_ref)
            r0 = _first_row(c0 + gi * gc)
            yield
            g, dg_dz = _log_gate(lr_ref[pl.ds(r0, gr), :], w_ref, b_ref, gi == 0, hk)
            gs_out[0 if fwd else 1, pl.ds(r0, gr), :] = dg_dz
            yield
            b = _dot_exact01(ones_low if fwd else ones_up, g)
            b_out[0 if fwd else 1, pl.ds(r0, gr), :] = b
            return b

        def group(gi, st, b, fwd):
            oacc = oacc_f if fwd else oacc_b
            r0 = pl.multiple_of((c0 + gi * gc) * CHUNK, CHUNK)
            blk = p_ref[pl.ds(r0, gr), :]
            q = blk[:, :hk].astype(F32) * scale
            k = blk[:, hk:2 * hk].astype(F32)
            v = blk[:, 2 * hk:2 * hk + hv]
            btot = _chunk_totals(b, fwd)
            qi = (q * jnp.exp(b)).astype(BF16)
            ki = (k * jnp.exp(-b)).astype(BF16)
            kd = (k * jnp.exp(btot - b)).astype(BF16)
            dec = jnp.exp(btot)
            a = _dot_nt(qi, ki)
            yield
            o = _dot(jnp.where(low_incl if fwd else up_strict, a, 0.0).astype(BF16), v)
            chunk_rows = [slice(c * CHUNK, (c + 1) * CHUNK) for c in range(gc)]
            kv = [_dot_tn(v[rows], kd[rows]) for rows in chunk_rows]
            for c in (range(gc) if fwd else reversed(range(gc))):
                yield
                rows = chunk_rows[c]
                st_b = st.astype(BF16)
                st_ref[0, 0, 0 if fwd else 1, c0 + gi * gc + c] = st_b
                oacc[pl.ds(r0 + c * CHUNK, CHUNK), :] = o[rows] + _dot_nt(qi[rows], st_b)
                st = st * dec[c * CHUNK:c * CHUNK + 1] + kv[c]
            return st

        def step(i, carry):
            st_f, st_b, b_f, b_b = carry
            gf, gb = i, ng - 1 - i
            return tuple(_interleave([group(gf, st_f, b_f, True), group(gb, st_b, b_b, False),
                                      decay(jnp.minimum(gf + 1, ng - 1), True), decay(jnp.maximum(gb - 1, 0), False)]))

        zero = jnp.zeros((hv, hk), F32)
        lax.fori_loop(0, ng, step, (zero, zero, *_interleave([decay(0, True), decay(ng - 1, False)])))

        def finish(i, carry):
            r0 = pl.multiple_of((c0 + i * gc) * CHUNK, CHUNK)
            o = oacc_f[pl.ds(r0, gr), :] + oacc_b[pl.ds(r0, gr), :]
            r = p_ref[pl.ds(r0, gr), 2 * hk + hv:].astype(F32)
            on = o * lax.rsqrt(jnp.mean(o * o, axis=-1, keepdims=True) + EPS) * gg_ref[...]
            o_ref[pl.ds(r0, gr), :] = o.astype(BF16)
            y_ref[pl.ds(r0, gr), :] = (on * r * _sigmoid(r)).astype(BF16)
            return carry

        lax.fori_loop(0, ng, finish, 0)

    head = lambda s, h: (s, h)
    wspec = pl.BlockSpec((LR_LANES, hk), lambda s, h: (0, h))
    bspec = pl.BlockSpec((1, hk), lambda s, h: (0, h))
    return pl.pallas_call(
        body, name="gla_fwd", grid=(dm.Bl, HEADS),
        in_specs=[pl.BlockSpec((lp, hw), head), pl.BlockSpec((lp, LR_LANES), lambda s, h: (s, 0)),
                  wspec, bspec, wspec, bspec, pl.BlockSpec((1, hv), lambda s, h: (0, 0))],
        out_specs=[pl.BlockSpec((lp, hv), head), pl.BlockSpec((lp, hv), head),
                   pl.BlockSpec((1, 1, 2, nc, hv, hk), lambda s, h: (s, h, 0, 0, 0, 0)),
                   pl.BlockSpec((2, lp, hk), lambda s, h: (0, s, h)), pl.BlockSpec((2, lp, hk), lambda s, h: (0, s, h))],
        out_shape=[jax.ShapeDtypeStruct((dm.T, dm.DV), BF16), jax.ShapeDtypeStruct((dm.T, dm.DV), BF16),
                   jax.ShapeDtypeStruct((dm.Bl, HEADS, 2, nc, hv, hk), BF16),
                   jax.ShapeDtypeStruct((2, dm.T, dm.DK), F32), jax.ShapeDtypeStruct((2, dm.T, dm.DK), F32)],
        scratch_shapes=[pltpu.VMEM((lp, hv), F32), pltpu.VMEM((lp, hv), F32)],
        compiler_params=_cp(2),
    )(proj_b, lr, wg_f, bg_f, wg_b, bg_b, gla_g)


def _gla_bwd(proj_b, lr, o_all, dy_gla, states, decays, gate_slopes, wg_f, wg_b, gla_g, dm):
    lp, hk, hv, nc, c0, hw = dm.LP, dm.HK, dm.HV, dm.NC, dm.C0, dm.HW
    scale = hk ** -0.5
    gc = _group_chunks(dm)
    gr, ng = gc * CHUNK, (nc - c0) // gc

    def body(p_ref, lr_ref, o_ref, dy_ref, st_ref, b_ref, gs_ref, wf_ref, wb_ref, gg_ref,
             d_ref, dlr_ref, gwf_ref, gbf_ref, gwb_ref, gbb_ref, ggg_ref, do_s, dq_s, dk_s, dv_s, dlr_s):
        low_incl, up_strict, ones_low, ones_up = _group_masks(gr)
        h = pl.program_id(1)

        @pl.when(h == 0)
        def _():
            dlr_ref[...] = jnp.zeros_like(dlr_ref)

        if c0 > 0:
            zr = c0 * CHUNK
            d_ref[0:zr, :] = jnp.zeros((zr, hw), BF16)
        for acc in (dq_s, dk_s, dv_s, dlr_s):
            acc[...] = jnp.zeros_like(acc)

        def norm_bwd(i, ggg):
            r0 = pl.multiple_of((c0 + i * gc) * CHUNK, CHUNK)
            o = o_ref[pl.ds(r0, gr), :].astype(F32)
            dy = dy_ref[pl.ds(r0, gr), :].astype(F32)
            r = p_ref[pl.ds(r0, gr), 2 * hk + hv:].astype(F32)
            rstd = lax.rsqrt(jnp.mean(o * o, axis=-1, keepdims=True) + EPS)
            ohat = o * rstd
            sg = _sigmoid(r)
            d_on = dy * (r * sg)
            d_ref[pl.ds(r0, gr), 2 * hk + hv:] = (dy * ohat * gg_ref[...] * (sg * (1.0 + r * (1.0 - sg)))).astype(BF16)
            d_oh = d_on * gg_ref[...]
            do_s[pl.ds(r0, gr), :] = (rstd * (d_oh - ohat * jnp.mean(d_oh * ohat, axis=-1, keepdims=True))).astype(BF16)
            return ggg + jnp.sum(d_on * ohat, axis=0, keepdims=True)

        ggg = lax.fori_loop(0, ng, norm_bwd, jnp.zeros((1, hv), F32))

        @pl.when((pl.program_id(0) == 0) & (h == 0))
        def _():
            ggg_ref[...] = jnp.zeros_like(ggg_ref)

        ggg_ref[0:1, :] += ggg

        def load(gi):
            r0 = pl.multiple_of((c0 + gi * gc) * CHUNK, CHUNK)
            blk = p_ref[pl.ds(r0, gr), :]
            return r0, blk[:, :hk].astype(F32) * scale, blk[:, hk:2 * hk].astype(F32), blk[:, 2 * hk:2 * hk + hv]

        zero = jnp.zeros((hv, hk), F32)

        def grad(gi, carry, fwd):
            dst, gw, gb = carry
            w_ref, way = (wf_ref, 0) if fwd else (wb_ref, 1)
            mask = low_incl if fwd else up_strict
            r0, q, k, v = load(gi)
            b = b_ref[way, pl.ds(r0, gr), :]
            btot = _chunk_totals(b, fwd)
            eb, enb, edb, dec = jnp.exp(b), jnp.exp(-b), jnp.exp(btot - b), jnp.exp(btot)
            qi_f, ki_f, kd_f = q * eb, k * enb, k * edb
            qi, ki, kd = qi_f.astype(BF16), ki_f.astype(BF16), kd_f.astype(BF16)
            do = do_s[pl.ds(r0, gr), :]
            a = _dot_nt(qi, ki)
            da = _dot_nt(do, v)
            yield
            a = jnp.where(mask, a, 0.0).astype(BF16)
            da = jnp.where(mask, da, 0.0).astype(BF16)
            dv = _dot_tn(a, do)
            dqi = _dot(da, ki)
            dki = _dot_tn(da, qi)
            dv_c, dqi_c, dkd_c, extra_c = [None] * gc, [None] * gc, [None] * gc, [None] * gc
            chunk_rows = [slice(c * CHUNK, (c + 1) * CHUNK) for c in range(gc)]
            qdo = [_dot_tn(do[rows], qi[rows]) for rows in chunk_rows]
            for c in (reversed(range(gc)) if fwd else range(gc)):
                yield
                rows = chunk_rows[c]
                st = st_ref[0, 0, way, c0 + gi * gc + c]
                dsn_b = dst.astype(BF16)
                dec_c = dec[c * CHUNK:c * CHUNK + 1]
                dv_c[c] = dv[rows] + _dot_nt(kd[rows], dsn_b)
                dqi_c[c] = dqi[rows] + _dot(do[rows], st)
                dkd_c[c] = _dot(v[rows], dsn_b)
                ddec = jnp.sum(st.astype(F32) * dst, axis=0, keepdims=True)
                extra = jnp.sum(dkd_c[c] * kd_f[rows], axis=0, keepdims=True) + ddec * dec_c
                extra_c[c] = jnp.broadcast_to(extra, (CHUNK, hk))
                dst = dst * dec_c + qdo[c]
            yield
            dv, dqi = jnp.concatenate(dv_c, axis=0), jnp.concatenate(dqi_c, axis=0)
            dkd, extra = jnp.concatenate(dkd_c, axis=0), jnp.concatenate(extra_c, axis=0)
            dq_s[pl.ds(r0, gr), :] += dqi * eb * scale
            dk_s[pl.ds(r0, gr), :] += dki * enb + dkd * edb
            dv_s[pl.ds(r0, gr), :] += dv
            db = dqi * qi_f - dki * ki_f - dkd * kd_f
            dg = _dot_exact01(ones_up if fwd else ones_low, db) + extra
            yield
            dz = dg * gs_ref[way, pl.ds(r0, gr), :]
            dz_b = dz.astype(BF16)
            dlr_s[pl.ds(r0, gr), :] += _dot_nt(dz_b, w_ref[...])
            return dst, gw + _dot_tn(lr_ref[pl.ds(r0, gr), :], dz_b), gb + jnp.sum(dz, axis=0, keepdims=True)

        def grad_step(i, carry):
            return tuple(_interleave([grad(ng - 1 - i, carry[0], True), grad(i, carry[1], False)]))

        init = (zero, jnp.zeros((LR_LANES, hk), F32), jnp.zeros((1, hk), F32))
        (_, gw_f, gb_f), (_, gw_b, gb_b) = lax.fori_loop(0, ng, grad_step, (init, init))
        for gw_ref, gb_ref, gw, gb in ((gwf_ref, gbf_ref, gw_f, gb_f), (gwb_ref, gbb_ref, gw_b, gb_b)):
            gw_ref[0] = gw
            gb_ref[0] = jnp.zeros((8, hk), F32)
            gb_ref[0, 0:1, :] = gb

        def combine(i, carry):
            r0 = pl.multiple_of((c0 + i * gc) * CHUNK, CHUNK)
            d_ref[pl.ds(r0, gr), 0:hk] = dq_s[pl.ds(r0, gr), :].astype(BF16)
            d_ref[pl.ds(r0, gr), hk:2 * hk] = dk_s[pl.ds(r0, gr), :].astype(BF16)
            d_ref[pl.ds(r0, gr), 2 * hk:2 * hk + hv] = dv_s[pl.ds(r0, gr), :].astype(BF16)
            dlr_ref[pl.ds(r0, gr), :] += dlr_s[pl.ds(r0, gr), :]
            return carry

        lax.fori_loop(0, ng, combine, 0)

    head = lambda s, h: (s, h)
    wspec = pl.BlockSpec((LR_LANES, hk), lambda s, h: (0, h))
    gwspec = pl.BlockSpec((1, LR_LANES, hk), lambda s, h: (s, 0, h))
    gbspec = pl.BlockSpec((1, 8, hk), lambda s, h: (s, 0, h))
    gw_shape = jax.ShapeDtypeStruct((dm.Bl, LR_LANES, dm.DK), F32)
    gb_shape = jax.ShapeDtypeStruct((dm.Bl, 8, dm.DK), F32)
    both = pl.BlockSpec((2, lp, hk), lambda s, h: (0, s, h))
    return pl.pallas_call(
        body, name="gla_bwd", grid=(dm.Bl, HEADS),
        in_specs=[pl.BlockSpec((lp, hw), head), pl.BlockSpec((lp, LR_LANES), lambda s, h: (s, 0)),
                  pl.BlockSpec((lp, hv), head), pl.BlockSpec((lp, hv), head),
                  pl.BlockSpec((1, 1, 2, nc, hv, hk), lambda s, h: (s, h, 0, 0, 0, 0)), both, both,
                  wspec, wspec, pl.BlockSpec((1, hv), lambda s, h: (0, 0))],
        out_specs=[pl.BlockSpec((lp, hw), head), pl.BlockSpec((lp, LR_LANES), lambda s, h: (s, 0)),
                   gwspec, gbspec, gwspec, gbspec, pl.BlockSpec((8, hv), lambda s, h: (0, 0))],
        out_shape=[jax.ShapeDtypeStruct((dm.T, HEADS * hw), BF16), jax.ShapeDtypeStruct((dm.T, LR_LANES), F32),
                   gw_shape, gb_shape, gw_shape, gb_shape, jax.ShapeDtypeStruct((8, hv), F32)],
        scratch_shapes=[pltpu.VMEM((lp, hv), BF16), pltpu.VMEM((lp, hk), F32), pltpu.VMEM((lp, hk), F32),
                        pltpu.VMEM((lp, hv), F32), pltpu.VMEM((lp, LR_LANES), F32)],
        compiler_params=_cp(2),
    )(proj_b, lr, o_all, dy_gla, states, decays, gate_slopes, wg_f, wg_b, gla_g)


def _out_merge(y_conv, y_gla, proj_c, w_oc, w_og, dm):
    d = dm.D
    tm = _pick(dm.T, 512, 16)

    def body(yc_ref, yg_ref, c_ref, woc_ref, wog_ref, pc_ref, pg_ref, m_ref):
        pc = _dot(yc_ref[...], woc_ref[...])
        pg = _dot(yg_ref[...], wog_ref[...])
        pc_ref[...] = pc.astype(BF16)
        pg_ref[...] = pg.astype(BF16)
        ma = c_ref[:, :d].astype(F32)
        mb = c_ref[:, d:].astype(F32)
        m_ref[...] = (_sigmoid(ma) * pc + _sigmoid(mb) * pg).astype(BF16)

    row = pl.BlockSpec((tm, d), lambda i: (i, 0))
    full = pl.BlockSpec((d, d), lambda i: (0, 0))
    act = jax.ShapeDtypeStruct((dm.T, d), BF16)
    return pl.pallas_call(
        body, name="out_merge", grid=(dm.T // tm,),
        in_specs=[row, row, pl.BlockSpec((tm, 2 * d), lambda i: (i, 0)), full, full],
        out_specs=[row, row, row], out_shape=[act, act, act], compiler_params=_cp(1),
    )(y_conv, y_gla, proj_c, w_oc, w_og)


def _final_fwd(merged, w_out, x, metapad, target, g_post, dm):
    tm, tps, d = dm.TM, dm.TPS, dm.D

    def body(m_ref, w_ref, x_ref, mp_ref, t_ref, g_ref, dout_ref, dy_ref, st_ref):
        i = pl.program_id(0)
        j = i % tps
        out = _dot(m_ref[...], w_ref[...])
        rstd = lax.rsqrt(jnp.mean(out * out, axis=-1, keepdims=True) + EPS)
        ohat = out * rstd
        h = jnp.where(j == 0, mp_ref[...], x_ref[0])
        y = h + ohat * g_ref[...]
        err = jnp.where(j == 0, 0.0, y - t_ref[0])
        dy = err * (1.0 / d)
        d_oh = dy * g_ref[...]
        dout_ref[...] = (rstd * (d_oh - ohat * jnp.mean(d_oh * ohat, axis=-1, keepdims=True))).astype(BF16)
        dy_ref[...] = dy.astype(BF16)

        @pl.when(i == 0)
        def _():
            st_ref[...] = jnp.zeros_like(st_ref)

        st_ref[0:1, :] += jnp.sum(dy * ohat, axis=0, keepdims=True)
        st_ref[1:2, :] += jnp.sum(err * err, axis=0, keepdims=True)

    row = pl.BlockSpec((tm, d), lambda i: (i, 0))
    tok = pl.BlockSpec((1, tm, d), lambda i: (i // tps, jnp.maximum(i % tps - 1, 0), 0))
    const = lambda r: pl.BlockSpec((r, d), lambda i: (0, 0))
    return pl.pallas_call(
        body, name="final_fwd", grid=(dm.Bl * tps,),
        in_specs=[row, const(d), tok, const(tm), tok, const(1)],
        out_specs=[row, row, const(8)],
        out_shape=[jax.ShapeDtypeStruct((dm.T, d), BF16), jax.ShapeDtypeStruct((dm.T, d), BF16),
                   jax.ShapeDtypeStruct((8, d), F32)],
        compiler_params=_cp(1),
    )(merged, w_out, x, metapad, target, g_post)


def _merge_bwd(d_out, proj_c, p_conv, p_gla, w_out, w_oc, w_og, dm):
    d = dm.D
    tm = _pick(dm.T, 512, 16)

    def body(do_ref, c_ref, pc_ref, pg_ref, wo_ref, woc_ref, wog_ref, dpc_ref, dpg_ref, dc_ref, dyc_ref, dyg_ref):
        dmg = _dot_nt(do_ref[...], wo_ref[...])
        sa = _sigmoid(c_ref[:, :d].astype(F32))
        sb = _sigmoid(c_ref[:, d:].astype(F32))
        dpc = (dmg * sa).astype(BF16)
        dpg = (dmg * sb).astype(BF16)
        dpc_ref[...] = dpc
        dpg_ref[...] = dpg
        dc_ref[:, :d] = (dmg * pc_ref[...].astype(F32) * sa * (1.0 - sa)).astype(BF16)
        dc_ref[:, d:] = (dmg * pg_ref[...].astype(F32) * sb * (1.0 - sb)).astype(BF16)
        dyc_ref[...] = _dot_nt(dpc, woc_ref[...]).astype(BF16)
        dyg_ref[...] = _dot_nt(dpg, wog_ref[...]).astype(BF16)

    row = pl.BlockSpec((tm, d), lambda i: (i, 0))
    row2 = pl.BlockSpec((tm, 2 * d), lambda i: (i, 0))
    full = pl.BlockSpec((d, d), lambda i: (0, 0))
    act = jax.ShapeDtypeStruct((dm.T, d), BF16)
    return pl.pallas_call(
        body, name="merge_bwd", grid=(dm.T // tm,),
        in_specs=[row, row2, row, row, full, full, full],
        out_specs=[row, row, row2, row, row],
        out_shape=[act, act, jax.ShapeDtypeStruct((dm.T, 2 * d), BF16), act, act],
        compiler_params=_cp(1),
    )(d_out, proj_c, p_conv, p_gla, w_out, w_oc, w_og)


def _prenorm_bwd(du, dy, x, metapad, g_pre, dm):
    tm, tps, d = dm.TM, dm.TPS, dm.D

    def body(du_ref, dy_ref, x_ref, mp_ref, g_ref, gx_ref, dmeta_ref, gg_ref):
        i = pl.program_id(0)
        j = i % tps
        h = jnp.where(j == 0, mp_ref[...], x_ref[0])
        rstd = lax.rsqrt(jnp.mean(h * h, axis=-1, keepdims=True) + EPS)
        hhat = h * rstd
        du = du_ref[...].astype(F32)
        dug = du * g_ref[...]
        dh = dy_ref[...].astype(F32) + rstd * (dug - hhat * jnp.mean(dug * hhat, axis=-1, keepdims=True))

        @pl.when(j == 0)
        def _():
            dmeta_ref[0] = dh

        @pl.when(j > 0)
        def _():
            gx_ref[0] = dh

        @pl.when(i == 0)
        def _():
            gg_ref[...] = jnp.zeros_like(gg_ref)

        gg_ref[0:1, :] += jnp.sum(du * hhat, axis=0, keepdims=True)

    row = pl.BlockSpec((tm, d), lambda i: (i, 0))
    tok = pl.BlockSpec((1, tm, d), lambda i: (i // tps, jnp.maximum(i % tps - 1, 0), 0))
    const = lambda r: pl.BlockSpec((r, d), lambda i: (0, 0))
    return pl.pallas_call(
        body, name="prenorm_bwd", grid=(dm.Bl * tps,),
        in_specs=[row, row, tok, const(tm), const(1)],
        out_specs=[tok, pl.BlockSpec((1, tm, d), lambda i: (i // tps, 0, 0)), const(8)],
        out_shape=[jax.ShapeDtypeStruct((dm.Bl, dm.S, d), F32), jax.ShapeDtypeStruct((dm.Bl, tm, d), F32),
                   jax.ShapeDtypeStruct((8, d), F32)],
        compiler_params=_cp(1),
    )(du, dy, x, metapad, g_pre)


def _adamw(partials, w, m, v, name, by_columns=False):
    r, c = w.shape
    n_parts = partials.shape[0]
    tr, tc = (r, _pick(c, 128, 128)) if by_columns else (_pick(r, 256, 16), c)

    def body(p_ref, w_ref, m_ref, v_ref, g_ref, d_ref, nm_ref, nv_ref):
        g = p_ref[0].astype(F32)
        for j in range(1, n_parts):
            g = g + p_ref[j].astype(F32)
        g_ref[...] = g
        d_ref[...], nm_ref[...], nv_ref[...] = _adam_step(g, w_ref[...], m_ref[...], v_ref[...])

    at = (lambda i: (0, i)) if by_columns else (lambda i: (i, 0))
    tile = pl.BlockSpec((tr, tc), at)
    out = jax.ShapeDtypeStruct((r, c), F32)
    return pl.pallas_call(
        body, name=name, grid=(c // tc if by_columns else r // tr,),
        in_specs=[pl.BlockSpec((n_parts, tr, tc), lambda i: (0,) + at(i)), tile, tile, tile],
        out_specs=[tile, tile, tile, tile], out_shape=[out, out, out, out], compiler_params=_cp(1),
    )(partials, w, m, v)


def _adam_step(g, w, m, v):
    m2 = ADAM_B1 * m + (1.0 - ADAM_B1) * g
    v2 = ADAM_B2 * v + (1.0 - ADAM_B2) * (g * g)
    m_hat = m2 / (1.0 - ADAM_B1 ** ADAM_STEP)
    v_hat = v2 / (1.0 - ADAM_B2 ** ADAM_STEP)
    return -ADAM_LR * (m_hat / (jnp.sqrt(v_hat) + ADAM_EPS) + ADAM_WD * w), m2, v2


def _adamw_small(items, name):
    n = len(items)

    def body(*refs):
        ins, outs = refs[:4 * n], refs[4 * n:]
        for i in range(n):
            p_ref, w_ref, m_ref, v_ref = ins[4 * i:4 * i + 4]
            g = p_ref[0]
            for j in range(1, p_ref.shape[0]):
                g = g + p_ref[j]
            delta, m2, v2 = _adam_step(g, w_ref[...], m_ref[...], v_ref[...])
            for o_ref, val in zip(outs[4 * i:4 * i + 4], (g, delta, m2, v2)):
                o_ref[...] = val

    vmem = pl.BlockSpec(memory_space=pltpu.VMEM)
    res = pl.pallas_call(
        body, name=name, in_specs=[vmem] * (4 * n), out_specs=[vmem] * (4 * n),
        out_shape=[jax.ShapeDtypeStruct(w.shape, F32) for _, w, _, _ in items for _ in range(4)],
    )(*[a for item in items for a in item])
    return [res[4 * i:4 * i + 4] for i in range(n)]


def _pack_rows(wt, dm):
    d, dk, hk, hv, cw, nj = dm.D, dm.DK, dm.HK, dm.HV, dm.CW, dm.NJ
    a = wt[:4 * d].reshape(4, nj, cw, d).transpose(1, 0, 2, 3).reshape(4 * d, d)
    b = jnp.concatenate([wt[4 * d:4 * d + dk].reshape(HEADS, hk, d), wt[4 * d + dk:5 * d].reshape(HEADS, hk, d),
                         wt[5 * d:6 * d].reshape(HEADS, hv, d), wt[6 * d:7 * d].reshape(HEADS, hv, d)],
                        axis=1).reshape(3 * d, d)
    c = wt[7 * d + 2 * RANK:]
    lr = jnp.pad(wt[7 * d:7 * d + 2 * RANK], ((0, LR_LANES - 2 * RANK), (0, 0)))
    return a, b, c, lr


def _unpack_rows(a, b, c, lr, dm):
    d, hk, hv, cw, nj, hw = dm.D, dm.HK, dm.HV, dm.CW, dm.NJ, dm.HW
    conv = a.reshape(nj, 4, cw, d).transpose(1, 0, 2, 3).reshape(4 * d, d)
    heads = b.reshape(HEADS, hw, d)
    q = heads[:, :hk].reshape(HEADS * hk, d)
    k = heads[:, hk:2 * hk].reshape(HEADS * hk, d)
    v = heads[:, 2 * hk:2 * hk + hv].reshape(HEADS * hv, d)
    r = heads[:, 2 * hk + hv:].reshape(HEADS * hv, d)
    return jnp.concatenate([conv, q, k, v, r, lr[:2 * RANK], c], axis=0)


def _column_shards(g, shard_shape):
    r, c = g.shape
    return g.reshape(r, N_DEV, c // N_DEV).transpose(1, 0, 2).reshape((N_DEV,) + tuple(shard_shape))


def _join_column_shards(parts):
    r, c = parts.shape[-2:]
    return parts.reshape(N_DEV, r, c).transpose(1, 0, 2).reshape(r, N_DEV * c)


def _local_step(x, target, meta, g_pre, u, wt_in, conv_w, wg_f, bg_f, wg_b, bg_b, gla_g, out_weights, g_post,
                on_matrix_grads=None):
    bl, s, d = x.shape
    dm = _Dims(bl, s, d)
    metapad = jnp.concatenate([jnp.zeros((dm.TM - N_META, d), F32), meta], axis=0)
    wta, wtb, wtc, wtlr = _pack_rows(wt_in, dm)
    wgp_f = jnp.pad(wg_f, ((0, LR_LANES - RANK), (0, 0))).astype(BF16)
    wgp_b = jnp.pad(wg_b, ((RANK, LR_LANES - 2 * RANK), (0, 0))).astype(BF16)

    u = _prenorm_meta(u, metapad, g_pre, dm)
    proj_a, proj_b, proj_c, lr = _inproj(u, [wta, wtb, wtc, wtlr], dm)
    y_conv = _conv_fwd(proj_a, conv_w, dm)
    o_all, y_gla, states, decays, gate_slopes = _gla_fwd(proj_b, lr, wgp_f, bg_f, wgp_b, bg_b, gla_g, dm)
    w_oc, w_og, w_out = out_weights(y_conv) if callable(out_weights) else out_weights
    p_conv, p_gla, merged = _out_merge(y_conv, y_gla, proj_c, w_oc, w_og, dm)
    d_out, dy, stats = _final_fwd(merged, w_out, x, metapad, target, g_post, dm)
    loss = 0.5 / d * jnp.sum(stats[1])

    d_pc, d_pg, d_c, dy_conv, dy_gla = _merge_bwd(d_out, proj_c, p_conv, p_gla, w_out, w_oc, w_og, dm)
    g_out = _matmul_tn(merged, d_out, BF16, "grad_w_out")
    g_oc = _matmul_tn(y_conv, d_pc, BF16, "grad_w_out_conv")
    g_og = _matmul_tn(y_gla, d_pg, BF16, "grad_w_out_gla")
    d_a, g_conv = _conv_bwd(proj_a, dy_conv, conv_w, dm)
    d_b, d_lr, gwp_f, gbp_f, gwp_b, gbp_b, g_gla = _gla_bwd(proj_b, lr, o_all, dy_gla, states, decays, gate_slopes, wgp_f, wgp_b, gla_g, dm)
    g_in = _unpack_rows(_matmul_tn(d_a, u, BF16, "grad_w_in_conv"), _matmul_tn(d_b, u, BF16, "grad_w_in_gla"),
                        _matmul_tn(d_c, u, BF16, "grad_w_in_merge"), _matmul_tn(d_lr, u, BF16, "grad_w_in_gate"), dm)
    if on_matrix_grads is not None:
        wtlr = wtlr + on_matrix_grads(dict(w_in=g_in, w_out_conv=g_oc, w_out_gla=g_og, w_merge_out=g_out)).astype(BF16)
    du = _grad_u([d_a, d_b, d_c, d_lr], [wta, wtb, wtc, wtlr], dm)
    grad_x, d_meta, g_pre_rows = _prenorm_bwd(du, dy, x, metapad, g_pre, dm)

    grads = dict(
        meta_tokens=jnp.sum(d_meta[:, dm.TM - N_META:, :], axis=0), norm_pre=g_pre_rows[0:1], w_in=g_in,
        conv_w=g_conv[0:3], w_gate_fwd=jnp.sum(gwp_f, axis=0)[:RANK], b_gate_fwd=jnp.sum(gbp_f, axis=0)[0:1],
        w_gate_bwd=jnp.sum(gwp_b, axis=0)[RANK:2 * RANK], b_gate_bwd=jnp.sum(gbp_b, axis=0)[0:1],
        gla_norm=g_gla[0:1], w_out_conv=g_oc, w_out_gla=g_og, w_merge_out=g_out, norm_post=stats[0:1])
    return loss, grad_x, grads


MATRICES = ("w_out_conv", "w_out_gla", "w_merge_out")
SMALL_SHARDED = ("meta_tokens", "conv_w", "w_gate_fwd", "w_gate_bwd")
REPLICATED = ("norm_pre", "b_gate_fwd", "b_gate_bwd", "gla_norm", "norm_post")
NAMES = ("meta_tokens", "norm_pre", "w_in", "conv_w", "w_gate_fwd", "b_gate_fwd", "w_gate_bwd", "b_gate_bwd", "gla_norm",
         "w_out_conv", "w_out_gla", "w_merge_out", "norm_post")


def kernel(x, meta_tokens, norm_pre, w_in, conv_w, w_gate_fwd, b_gate_fwd, w_gate_bwd, b_gate_bwd, gla_norm, w_out_conv, w_out_gla, w_merge_out, norm_post, loss_target, m_meta_tokens, m_norm_pre, m_w_in, m_conv_w, m_w_gate_fwd, m_b_gate_fwd, m_w_gate_bwd, m_b_gate_bwd, m_gla_norm, m_w_out_conv, m_w_out_gla, m_w_merge_out, m_norm_post, v_meta_tokens, v_norm_pre, v_w_in, v_conv_w, v_w_gate_fwd, v_b_gate_fwd, v_w_gate_bwd, v_b_gate_bwd, v_gla_norm, v_w_out_conv, v_w_out_gla, v_w_merge_out, v_norm_post):
    w = dict(meta_tokens=meta_tokens, norm_pre=norm_pre, w_in=w_in[0], conv_w=conv_w, w_gate_fwd=w_gate_fwd,
             b_gate_fwd=b_gate_fwd, w_gate_bwd=w_gate_bwd, b_gate_bwd=b_gate_bwd, gla_norm=gla_norm,
             w_out_conv=w_out_conv[0], w_out_gla=w_out_gla[0], w_merge_out=w_merge_out[0], norm_post=norm_post)
    m = dict(meta_tokens=m_meta_tokens, norm_pre=m_norm_pre, w_in=m_w_in[0], conv_w=m_conv_w, w_gate_fwd=m_w_gate_fwd,
             b_gate_fwd=m_b_gate_fwd, w_gate_bwd=m_w_gate_bwd, b_gate_bwd=m_b_gate_bwd, gla_norm=m_gla_norm,
             w_out_conv=m_w_out_conv[0], w_out_gla=m_w_out_gla[0], w_merge_out=m_w_merge_out[0], norm_post=m_norm_post)
    v = dict(meta_tokens=v_meta_tokens, norm_pre=v_norm_pre, w_in=v_w_in[0], conv_w=v_conv_w, w_gate_fwd=v_w_gate_fwd,
             b_gate_fwd=v_b_gate_fwd, w_gate_bwd=v_w_gate_bwd, b_gate_bwd=v_b_gate_bwd, gla_norm=v_gla_norm,
             w_out_conv=v_w_out_conv[0], w_out_gla=v_w_out_gla[0], w_merge_out=v_w_merge_out[0], norm_post=v_norm_post)
    d = x.shape[-1]

    dm = _Dims(*x.shape)
    wt_all, *small_all, u = _gather_two_level([w["w_in"].T.astype(BF16)] + [w[n] for n in SMALL_SHARDED], "gather_weights",
                                              _prenorm_tokens_side(x, norm_pre, dm))
    _, late_weights = _exchange_start([w[n].astype(BF16) for n in MATRICES], [], small_all[0], "gather_out_weights_start")
    wt_in = wt_all.reshape(-1, d)
    small = {n: _join_column_shards(p) for n, p in zip(SMALL_SHARDED, small_all)}

    def out_weights(after):
        return tuple(a.reshape(-1, d) for a in _exchange_wait(late_weights, after, "gather_out_weights_wait"))

    pending = []

    def on_matrix_grads(g):
        to_send = [g[n].astype(BF16).reshape(N_DEV, -1, d) for n in ("w_in",) + MATRICES]
        token, state = _exchange_start([], to_send, None, "exchange_grads_start")
        pending.append(state)
        return token

    loss, grad_x, grads = _local_step(
        x, loss_target, small["meta_tokens"], norm_pre, u, wt_in, small["conv_w"], small["w_gate_fwd"], b_gate_fwd,
        small["w_gate_bwd"], b_gate_bwd, gla_norm, out_weights, norm_post, on_matrix_grads)
    loss = lax.psum(loss, ("x", "y", "c"))
    received = _exchange_wait(pending[0], grad_x, "exchange_grads_wait")

    small_recv = _exchange([grads[n] for n in REPLICATED], [_column_shards(grads[n], w[n].shape) for n in SMALL_SHARDED],
                           "exchange_small_grads")

    results = {"w_in": [r.T[None] for r in _adamw(received[0], w["w_in"].T, m["w_in"].T, v["w_in"].T, "adamw_w_in", by_columns=True)]}
    for n, partials in zip(MATRICES, received[1:4]):
        results[n] = [r[None] for r in _adamw(partials, w[n], m[n], v[n], "adamw_" + n)]
    small_names = REPLICATED + SMALL_SHARDED
    results.update(zip(small_names, _adamw_small([(p, w[n], m[n], v[n]) for n, p in zip(small_names, small_recv)], "adamw_small")))
    return (loss, grad_x, *[results[n][i] for i in range(4) for n in NAMES])
```

```python
import jax
import jax.numpy as jnp
from jax import lax
from jax.experimental import pallas as pl
from jax.experimental.pallas import tpu as pltpu

F32 = jnp.float32
BF16 = jnp.bfloat16
MESH = pl.DeviceIdType.MESH

N_META = 16
CHUNK = 64
CHUNK_SHIFT = 6
HEADS = 4
RANK = 16
LR_LANES = 128
PAD_ROWS = CHUNK - N_META
EPS = 1e-6
GATE_NORMALIZER = 16.0
N_DEV = 8
ADAM_LR, ADAM_B1, ADAM_B2, ADAM_EPS, ADAM_WD, ADAM_STEP = 0.001, 0.9, 0.999, 1e-08, 0.01, 10
VMEM_LIMIT_BYTES = 56 * 1024 * 1024


class _Dims:
    def __init__(self, bl, s, d):
        self.Bl, self.S, self.D = bl, s, d
        self.TM = CHUNK
        self.LP = self.TM + s
        self.T = bl * self.LP
        self.TPS = self.LP // self.TM
        self.NC = self.LP // CHUNK
        self.C0 = (self.TM - CHUNK) // CHUNK
        self.DK, self.DV = d // 2, d
        self.HK, self.HV = self.DK // HEADS, self.DV // HEADS
        self.HW = 2 * self.HK + 2 * self.HV
        self.CW = 256 if d % 256 == 0 and d > 256 else d // 4
        self.NJ = d // self.CW


def _pick(n, target, mult):
    t = min(n, target)
    while t >= mult:
        if n % t == 0 and t % mult == 0:
            return t
        t -= mult
    return n


def _cp(n_axes):
    return pltpu.CompilerParams(dimension_semantics=("arbitrary",) * n_axes, vmem_limit_bytes=VMEM_LIMIT_BYTES)


def _sigmoid(x):
    return 1.0 / (1.0 + jnp.exp(-x))


def _dot(a, b):
    return jnp.dot(a, b, preferred_element_type=F32)


def _dot_nt(a, b):
    return lax.dot_general(a, b, (((1,), (1,)), ((), ())), preferred_element_type=F32)


def _dot_tn(a, b):
    return lax.dot_general(a, b, (((0,), (0,)), ((), ())), preferred_element_type=F32)


def _dot_exact01(m01, x):
    hi = x.astype(BF16)
    lo = (x - hi.astype(F32)).astype(BF16)
    return _dot(m01, hi) + _dot(m01, lo)


def _exchange(gathers, scatters, name):
    arrays = list(gathers) + list(scatters)
    n, ng = len(arrays), len(gathers)

    def body(*refs):
        ins, outs = refs[:n], refs[n:2 * n]
        send_sems, recv_sems, local_sems = refs[2 * n:]
        x, y, c = lax.axis_index("x"), lax.axis_index("y"), lax.axis_index("c")
        me = 4 * x + 2 * y + c
        started = []
        for t in range(n):
            src, dst = ins[t], outs[t]
            own = pltpu.make_async_copy(src if t < ng else src.at[me], dst.at[me], local_sems.at[t])
            own.start()
            started.append(own)
            for k, pos, peer in _peers(x, y, c):
                cp = pltpu.make_async_remote_copy(
                    src_ref=src if t < ng else src.at[peer], dst_ref=dst.at[me],
                    send_sem=send_sems.at[t * (N_DEV - 1) + k - 1], recv_sem=recv_sems.at[t * (N_DEV - 1) + k - 1],
                    device_id=pos, device_id_type=MESH)
                cp.start()
                started.append(cp)
        for cp in started:
            cp.wait()

    out_shape = [jax.ShapeDtypeStruct((N_DEV,) + a.shape if t < ng else a.shape, a.dtype) for t, a in enumerate(arrays)]
    any_spec = pl.BlockSpec(memory_space=pl.ANY)
    return pl.pallas_call(
        body, name=name, out_shape=out_shape, in_specs=[any_spec] * n, out_specs=[any_spec] * n,
        scratch_shapes=[pltpu.SemaphoreType.DMA((n * (N_DEV - 1),)), pltpu.SemaphoreType.DMA((n * (N_DEV - 1),)),
                        pltpu.SemaphoreType.DMA((n,))],
        compiler_params=pltpu.CompilerParams(has_side_effects=True),
    )(*arrays)


def _gather_two_level(arrays, name, side=None):
    n = len(arrays)
    per = N_DEV - 1
    work, side_in, side_in_specs, side_out, side_out_specs, side_scratch = side or (None, [], [], [], [], [])
    n_in, n_out = len(side_in), len(side_out)

    def body(*refs):
        ins, outs = refs[:n], refs[n + n_in:2 * n + n_in]
        send_sems, recv_sems, local_sems = refs[2 * n + n_in + n_out:2 * n + n_in + n_out + 3]
        x, y, c = lax.axis_index("x"), lax.axis_index("y"), lax.axis_index("c")
        sibling = (x, y, 1 - c)
        chips = [(1 - x, y), (x, 1 - y), (1 - x, 1 - y)]
        index = lambda px, py, pc: 4 * px + 2 * py + pc

        def copy(t, k, block, to, from_input=False):
            slab = outs[t].at[index(*block)]
            return pltpu.make_async_remote_copy(
                src_ref=ins[t] if from_input else slab, dst_ref=slab, send_sem=send_sems.at[t * per + k],
                recv_sem=recv_sems.at[t * per + k], device_id=to, device_id_type=MESH)

        own, sent = [], []
        for t in range(n):
            own.append(pltpu.make_async_copy(ins[t], outs[t].at[index(x, y, c)], local_sems.at[t]))
            own[-1].start()
            first = [copy(t, 0, (x, y, c), sibling, True)]
            first += [copy(t, 1 + j, (x, y, c), (*chip, c), True) for j, chip in enumerate(chips)]
            for cp in first:
                cp.start()
            sent += first
        if work is not None:
            work(refs[n:n + n_in], refs[2 * n + n_in:2 * n + n_in + n_out], refs[2 * n + n_in + n_out + 3:])
        for t in range(n):
            for j, chip in enumerate(chips):
                copy(t, 1 + j, (*chip, c), (x, y, c)).wait_recv()
                sent.append(copy(t, 4 + j, (*chip, c), sibling))
                sent[-1].start()
        for t in range(n):
            copy(t, 0, sibling, (x, y, c)).wait_recv()
            for j, chip in enumerate(chips):
                copy(t, 4 + j, (*chip, 1 - c), (x, y, c)).wait_recv()
        for cp in sent:
            cp.wait_send()
        for cp in own:
            cp.wait()

    out_shape = [jax.ShapeDtypeStruct((N_DEV,) + a.shape, a.dtype) for a in arrays]
    any_spec = pl.BlockSpec(memory_space=pl.ANY)
    return pl.pallas_call(
        body, name=name, out_shape=out_shape + list(side_out), in_specs=[any_spec] * n + list(side_in_specs),
        out_specs=[any_spec] * n + list(side_out_specs),
        scratch_shapes=[pltpu.SemaphoreType.DMA((n * per,)), pltpu.SemaphoreType.DMA((n * per,)),
                        pltpu.SemaphoreType.DMA((n,))] + list(side_scratch),
        compiler_params=pltpu.CompilerParams(has_side_effects=True, vmem_limit_bytes=VMEM_LIMIT_BYTES),
    )(*arrays, *side_in)


def _peers(x, y, c):
    out = []
    for k in range(1, N_DEV):
        px = 1 - x if (k >> 2) & 1 else x
        py = 1 - y if (k >> 1) & 1 else y
        pc = 1 - c if k & 1 else c
        out.append((k, (px, py, pc), 4 * px + 2 * py + pc))
    return out


def _exchange_start(gathers, scatters, after, name):
    arrays = list(gathers) + list(scatters)
    n, ng = len(arrays), len(gathers)
    hbm = pl.BlockSpec(memory_space=pltpu.HBM)
    sem = pl.BlockSpec(memory_space=pltpu.SEMAPHORE)

    extra = [] if after is None else [after]
    ne = len(extra)

    def body(*refs):
        ins, lands = refs[:n], refs[n:2 * n]
        send_sems, recv_sems = refs[2 * n + ne], refs[2 * n + ne + 1]
        token = refs[4 * n + ne + 2]
        x, y, c = lax.axis_index("x"), lax.axis_index("y"), lax.axis_index("c")
        me = 4 * x + 2 * y + c
        for t in range(n):
            for k, pos, peer in _peers(x, y, c):
                pltpu.make_async_remote_copy(
                    src_ref=ins[t] if t < ng else ins[t].at[peer], dst_ref=lands[t].at[me],
                    send_sem=send_sems.at[t * (N_DEV - 1) + k - 1], recv_sem=recv_sems.at[t * (N_DEV - 1) + k - 1],
                    device_id=pos, device_id_type=MESH).start()
        token[...] = jnp.zeros_like(token)

    me = 4 * lax.axis_index("x") + 2 * lax.axis_index("y") + lax.axis_index("c")
    lands = [lax.dynamic_update_index_in_dim(lax.empty((N_DEV,) + a.shape if t < ng else a.shape, a.dtype),
                                             a if t < ng else lax.dynamic_index_in_dim(a, me, 0, keepdims=False), me, 0)
             for t, a in enumerate(arrays)]
    operands = [pltpu.with_memory_space_constraint(a, pltpu.HBM) for a in arrays + lands]
    sems = pltpu.SemaphoreType.DMA((n * (N_DEV - 1),))
    res = pl.pallas_call(
        body, name=name,
        out_shape=(sems, sems, *[pltpu.HBM(a.shape, a.dtype) for a in arrays + lands], jax.ShapeDtypeStruct((8, 128), F32)),
        in_specs=[hbm] * (2 * n) + [pl.BlockSpec(memory_space=pl.ANY)] * ne,
        out_specs=(sem, sem, *[hbm] * (2 * n), pl.BlockSpec(memory_space=pltpu.VMEM)),
        input_output_aliases={i: 2 + i for i in range(2 * n)},
        compiler_params=pltpu.CompilerParams(has_side_effects=pltpu.SideEffectType.DATAFLOW_SIDE_EFFECTING),
    )(*operands, *extra)
    return res[-1][0, 0], (ng, res[0], res[1], list(res[2:2 + n]), list(res[2 + n:2 + 2 * n]))


def _exchange_wait(state, after, name):
    ng, send_sems, recv_sems, sent, lands = state
    n = len(sent)
    hbm = pl.BlockSpec(memory_space=pltpu.HBM)
    sem = pl.BlockSpec(memory_space=pltpu.SEMAPHORE)

    def body(*refs):
        ins, land_refs = refs[:n], refs[n:2 * n]
        send_ref, recv_ref = refs[2 * n], refs[2 * n + 1]
        x, y, c = lax.axis_index("x"), lax.axis_index("y"), lax.axis_index("c")
        me = 4 * x + 2 * y + c
        for t in range(n):
            for k, pos, peer in _peers(x, y, c):
                cp = pltpu.make_async_remote_copy(
                    src_ref=ins[t] if t < ng else ins[t].at[peer], dst_ref=land_refs[t].at[me],
                    send_sem=send_ref.at[t * (N_DEV - 1) + k - 1], recv_sem=recv_ref.at[t * (N_DEV - 1) + k - 1],
                    device_id=pos, device_id_type=MESH)
                cp.wait_send()
                cp.wait_recv()

    res = pl.pallas_call(
        body, name=name, out_shape=tuple(pltpu.HBM(a.shape, a.dtype) for a in sent + lands),
        in_specs=[hbm] * (2 * n) + [sem, sem, pl.BlockSpec(memory_space=pl.ANY)], out_specs=tuple([hbm] * (2 * n)),
        input_output_aliases={i: i for i in range(2 * n)},
        compiler_params=pltpu.CompilerParams(has_side_effects=pltpu.SideEffectType.DATAFLOW_SIDE_EFFECTING),
    )(*sent, *lands, send_sems, recv_sems, after)
    return list(res[n:])


def _rms_scaled(h, g):
    return (h * lax.rsqrt(jnp.mean(h * h, axis=-1, keepdims=True) + EPS) * g).astype(BF16)


def _prenorm_tokens_side(x, g_pre, dm):
    bl, s, d = x.shape
    rows = _pick(s, 512, 16)
    tiles = [(b, j) for b in range(bl) for j in range(s // rows)]

    def work(ins, outs, scratch):
        (x_ref, g_ref), (u_ref,), (xbuf, ubuf, sem_in, sem_out) = ins, outs, scratch

        def load(t, slot):
            b, j = tiles[t]
            return pltpu.make_async_copy(x_ref.at[b, pl.ds(j * rows, rows), :], xbuf.at[slot], sem_in.at[slot])

        def store(t, slot):
            b, j = tiles[t]
            return pltpu.make_async_copy(ubuf.at[slot], u_ref.at[pl.ds(b * dm.LP + dm.TM + j * rows, rows), :], sem_out.at[slot])

        load(0, 0).start()
        for t in range(len(tiles)):
            slot = t % 2
            if t + 1 < len(tiles):
                load(t + 1, 1 - slot).start()
            load(t, slot).wait()
            if t >= 2:
                store(t - 2, slot).wait()
            ubuf[slot] = _rms_scaled(xbuf[slot], g_ref[...])
            store(t, slot).start()
        for t in range(max(len(tiles) - 2, 0), len(tiles)):
            store(t, t % 2).wait()

    any_spec = pl.BlockSpec(memory_space=pl.ANY)
    return (work, [x, g_pre], [any_spec, pl.BlockSpec(memory_space=pltpu.VMEM)],
            [jax.ShapeDtypeStruct((dm.T, d), BF16)], [any_spec],
            [pltpu.VMEM((2, rows, d), F32), pltpu.VMEM((2, rows, d), BF16), pltpu.SemaphoreType.DMA((2,)),
             pltpu.SemaphoreType.DMA((2,))])


def _prenorm_meta(u, metapad, g_pre, dm):
    tm, tps, d = dm.TM, dm.TPS, dm.D

    def body(u_in, mp_ref, g_ref, u_ref):
        u_ref[...] = _rms_scaled(mp_ref[...], g_ref[...])

    return pl.pallas_call(
        body, name="prenorm_meta", grid=(dm.Bl,),
        in_specs=[pl.BlockSpec(memory_space=pl.ANY), pl.BlockSpec((tm, d), lambda i: (0, 0)),
                  pl.BlockSpec((1, d), lambda i: (0, 0))],
        out_specs=pl.BlockSpec((tm, d), lambda i: (i * tps, 0)),
        out_shape=jax.ShapeDtypeStruct((dm.T, d), BF16), input_output_aliases={0: 0}, compiler_params=_cp(1),
    )(u, metapad, g_pre)


def _matmul_tn(a, b, out_dtype, name, tt=2304, tn=1024, tk=1024):
    t, k = a.shape
    n = b.shape[1]
    tt, tn, tk = _pick(t, tt, 16), _pick(n, tn, 128), _pick(k, tk, 128)
    nt = t // tt

    def body(a_ref, b_ref, o_ref, acc):
        p = _dot_tn(a_ref[...].astype(BF16), b_ref[...].astype(BF16))
        i = pl.program_id(2)

        @pl.when(i == 0)
        def _():
            acc[...] = p

        @pl.when(i > 0)
        def _():
            acc[...] += p

        @pl.when(i == nt - 1)
        def _():
            o_ref[...] = acc[...].astype(out_dtype)

    return pl.pallas_call(
        body, name=name, grid=(k // tk, n // tn, nt),
        in_specs=[pl.BlockSpec((tt, tk), lambda kk, j, i: (i, kk)), pl.BlockSpec((tt, tn), lambda kk, j, i: (i, j))],
        out_specs=pl.BlockSpec((tk, tn), lambda kk, j, i: (kk, j)),
        out_shape=jax.ShapeDtypeStruct((k, n), out_dtype), scratch_shapes=[pltpu.VMEM((tk, tn), F32)],
        compiler_params=_cp(3),
    )(a, b)


def _load_resident(hbm_refs, vmem_refs, sems):
    @pl.when(pl.program_id(0) == 0)
    def _():
        copies = [pltpu.make_async_copy(h, v, sems.at[i]) for i, (h, v) in enumerate(zip(hbm_refs, vmem_refs))]
        for cp in copies:
            cp.start()
        for cp in copies:
            cp.wait()


def _inproj(u, wts, dm):
    t, d = u.shape
    tm = _pick(t, 512, 16)
    np_ = len(wts)
    cn = 1024

    def body(*refs):
        u_ref, w_hbm, outs = refs[0], refs[1:1 + np_], refs[1 + np_:1 + 2 * np_]
        w_vmem, sems = refs[1 + 2 * np_:1 + 3 * np_], refs[1 + 3 * np_]
        _load_resident(w_hbm, w_vmem, sems)
        ut = u_ref[...]
        for w, o_ref in zip(w_vmem, outs):
            n = w.shape[0]
            step = cn if n % cn == 0 else n
            for j in range(0, n, step):
                o_ref[:, j:j + step] = _dot_nt(ut, w[j:j + step, :]).astype(BF16)

    return pl.pallas_call(
        body, name="inproj", grid=(t // tm,),
        in_specs=[pl.BlockSpec((tm, d), lambda i: (i, 0))] + [pl.BlockSpec(memory_space=pl.ANY)] * np_,
        out_specs=[pl.BlockSpec((tm, w.shape[0]), lambda i: (i, 0)) for w in wts],
        out_shape=[jax.ShapeDtypeStruct((t, w.shape[0]), BF16) for w in wts],
        scratch_shapes=[pltpu.VMEM(w.shape, BF16) for w in wts] + [pltpu.SemaphoreType.DMA((np_,))],
        compiler_params=_cp(1),
    )(u, *wts)


def _grad_u(d_parts, wts, dm):
    t = d_parts[0].shape[0]
    d = wts[0].shape[1]
    tm = _pick(t, 512, 16)
    np_ = len(wts)

    def body(*refs):
        d_refs, w_hbm, o_ref = refs[:np_], refs[np_:2 * np_], refs[2 * np_]
        w_vmem, sems, acc = refs[2 * np_ + 1:3 * np_ + 1], refs[3 * np_ + 1], refs[3 * np_ + 2]
        _load_resident(w_hbm, w_vmem, sems)
        acc[...] = _dot(d_refs[0][...].astype(BF16), w_vmem[0][...])
        for a_ref, w in zip(d_refs[1:], w_vmem[1:]):
            acc[...] += _dot(a_ref[...].astype(BF16), w[...])
        o_ref[...] = acc[...].astype(BF16)

    return pl.pallas_call(
        body, name="grad_u", grid=(t // tm,),
        in_specs=[pl.BlockSpec((tm, a.shape[1]), lambda i: (i, 0)) for a in d_parts] + [pl.BlockSpec(memory_space=pl.ANY)] * np_,
        out_specs=pl.BlockSpec((tm, d), lambda i: (i, 0)), out_shape=jax.ShapeDtypeStruct((t, d), BF16),
        scratch_shapes=[pltpu.VMEM(w.shape, BF16) for w in wts] + [pltpu.SemaphoreType.DMA((np_,)), pltpu.VMEM((tm, d), F32)],
        compiler_params=_cp(1),
    )(*d_parts, *wts)


def _conv_rows(dm):
    return _pick(dm.LP, 256, 16)


def _shifted(m, prev_row, next_row, rows):
    row = lax.broadcasted_iota(jnp.int32, m.shape, 0)
    m_prev = jnp.where(row == 0, prev_row, pltpu.roll(m, 1, 0))
    m_next = jnp.where(row == rows - 1, next_row, pltpu.roll(m, rows - 1, 0))
    return m_prev, m_next


def _conv_fwd(proj_a, conv_w, dm):
    lp, cw, rc = dm.LP, dm.CW, _conv_rows(dm)
    nchunk = lp // rc

    def body(p_ref, w_ref, y_ref):
        w0, w1, w2 = w_ref[0:1, :], w_ref[1:2, :], w_ref[2:3, :]

        def chunk(ci, carry):
            r0 = pl.multiple_of(ci * rc, rc)
            blk = p_ref[pl.ds(r0, rc), :].astype(F32)
            cb, cc, cx, cz = (blk[:, i * cw:(i + 1) * cw] for i in range(4))
            m = cc * cx
            rp = pl.multiple_of(jnp.maximum(r0 - 16, 0), 16)
            rn = pl.multiple_of(jnp.minimum(r0 + rc, lp - 16), 16)
            pv = p_ref[pl.ds(rp, 16), cw:3 * cw].astype(F32)
            nx = p_ref[pl.ds(rn, 16), cw:3 * cw].astype(F32)
            prev_row = jnp.where(ci > 0, pv[15:16, :cw] * pv[15:16, cw:], 0.0)
            next_row = jnp.where(ci < nchunk - 1, nx[0:1, :cw] * nx[0:1, cw:], 0.0)
            m_prev, m_next = _shifted(m, prev_row, next_row, rc)
            s = w0 * m_prev + w1 * m + w2 * m_next
            y_ref[pl.ds(r0, rc), :] = (cb * s * (cz * _sigmoid(cz))).astype(BF16)
            return carry

        lax.fori_loop(0, nchunk, chunk, 0)

    return pl.pallas_call(
        body, name="conv_fwd", grid=(dm.Bl, dm.NJ),
        in_specs=[pl.BlockSpec((lp, 4 * cw), lambda s, j: (s, j)), pl.BlockSpec((3, cw), lambda s, j: (0, j))],
        out_specs=pl.BlockSpec((lp, cw), lambda s, j: (s, j)),
        out_shape=jax.ShapeDtypeStruct((dm.T, dm.D), BF16), compiler_params=_cp(2),
    )(proj_a, conv_w)


def _conv_bwd(proj_a, dy_conv, conv_w, dm):
    lp, cw, rc = dm.LP, dm.CW, _conv_rows(dm)
    nchunk = lp // rc

    def body(p_ref, dy_ref, w_ref, d_ref, gw_ref):
        w0, w1, w2 = w_ref[0:1, :], w_ref[1:2, :], w_ref[2:3, :]

        def ds_of(p4, dy):
            cb, cz = p4[:, :cw], p4[:, 3 * cw:]
            return dy * cb * (cz * _sigmoid(cz))

        def chunk(ci, carry):
            g0, g1, g2 = carry
            r0 = pl.multiple_of(ci * rc, rc)
            blk = p_ref[pl.ds(r0, rc), :].astype(F32)
            dy = dy_ref[pl.ds(r0, rc), :].astype(F32)
            cb, cc, cx, cz = (blk[:, i * cw:(i + 1) * cw] for i in range(4))
            rp = pl.multiple_of(jnp.maximum(r0 - 16, 0), 16)
            rn = pl.multiple_of(jnp.minimum(r0 + rc, lp - 16), 16)
            pv = p_ref[pl.ds(rp, 16), :].astype(F32)[15:16]
            nx = p_ref[pl.ds(rn, 16), :].astype(F32)[0:1]
            dpv = dy_ref[pl.ds(rp, 16), :].astype(F32)[15:16]
            dnx = dy_ref[pl.ds(rn, 16), :].astype(F32)[0:1]
            has_prev, has_next = ci > 0, ci < nchunk - 1
            m = cc * cx
            m_prev, m_next = _shifted(m, jnp.where(has_prev, pv[:, cw:2 * cw] * pv[:, 2 * cw:3 * cw], 0.0),
                                      jnp.where(has_next, nx[:, cw:2 * cw] * nx[:, 2 * cw:3 * cw], 0.0), rc)
            s = w0 * m_prev + w1 * m + w2 * m_next
            sg = _sigmoid(cz)
            silu = cz * sg
            ds = dy * cb * silu
            ds_prev, ds_next = _shifted(ds, jnp.where(has_prev, ds_of(pv, dpv), 0.0),
                                        jnp.where(has_next, ds_of(nx, dnx), 0.0), rc)
            dm_ = w0 * ds_next + w1 * ds + w2 * ds_prev
            d_ref[pl.ds(r0, rc), 0:cw] = (dy * s * silu).astype(BF16)
            d_ref[pl.ds(r0, rc), cw:2 * cw] = (dm_ * cx).astype(BF16)
            d_ref[pl.ds(r0, rc), 2 * cw:3 * cw] = (dm_ * cc).astype(BF16)
            d_ref[pl.ds(r0, rc), 3 * cw:4 * cw] = (dy * cb * s * (sg * (1.0 + cz * (1.0 - sg)))).astype(BF16)
            return (g0 + jnp.sum(ds * m_prev, axis=0, keepdims=True), g1 + jnp.sum(ds * m, axis=0, keepdims=True),
                    g2 + jnp.sum(ds * m_next, axis=0, keepdims=True))

        z = jnp.zeros((1, cw), F32)
        g0, g1, g2 = lax.fori_loop(0, nchunk, chunk, (z, z, z))

        @pl.when(pl.program_id(1) == 0)
        def _():
            gw_ref[...] = jnp.zeros_like(gw_ref)

        gw_ref[0:1, :] += g0
        gw_ref[1:2, :] += g1
        gw_ref[2:3, :] += g2

    return pl.pallas_call(
        body, name="conv_bwd", grid=(dm.NJ, dm.Bl),
        in_specs=[pl.BlockSpec((lp, 4 * cw), lambda j, s: (s, j)), pl.BlockSpec((lp, cw), lambda j, s: (s, j)),
                  pl.BlockSpec((3, cw), lambda j, s: (0, j))],
        out_specs=[pl.BlockSpec((lp, 4 * cw), lambda j, s: (s, j)), pl.BlockSpec((8, cw), lambda j, s: (0, j))],
        out_shape=[jax.ShapeDtypeStruct((dm.T, 4 * dm.D), BF16), jax.ShapeDtypeStruct((8, dm.D), F32)],
        compiler_params=_cp(2),
    )(proj_a, dy_conv, conv_w)


def _interleave(gens):
    results = [None] * len(gens)
    live = list(range(len(gens)))
    while live:
        for idx in list(live):
            try:
                next(gens[idx])
            except StopIteration as done:
                results[idx] = done.value
                live.remove(idx)
    return results


def _group_chunks(dm):
    n = dm.NC - dm.C0
    return 3 if n % 3 == 0 else 1


def _group_masks(rows):
    ii = lax.broadcasted_iota(jnp.int32, (rows, rows), 0)
    jj = lax.broadcasted_iota(jnp.int32, (rows, rows), 1)
    same = jnp.right_shift(ii, CHUNK_SHIFT) == jnp.right_shift(jj, CHUNK_SHIFT)
    low, up = same & (jj <= ii), same & (jj >= ii)
    return low, same & (jj > ii), low.astype(BF16), up.astype(BF16)


def _first_row(chunk):
    return chunk * CHUNK if isinstance(chunk, int) else pl.multiple_of(chunk * CHUNK, CHUNK)


def _chunk_totals(b, fwd):
    hk = b.shape[1]
    rows = [b[c * CHUNK + CHUNK - 1:(c + 1) * CHUNK] if fwd else b[c * CHUNK:c * CHUNK + 1]
            for c in range(b.shape[0] // CHUNK)]
    return jnp.concatenate([jnp.broadcast_to(r, (CHUNK, hk)) for r in rows], axis=0)


def _log_gate(lr_rows, w_ref, b_ref, first_group, hk):
    z = _dot(lr_rows, w_ref[...]) + b_ref[...]
    e = jnp.exp(-jnp.abs(z))
    g = (jnp.minimum(z, 0.0) - jnp.log(1.0 + e)) * (1.0 / GATE_NORMALIZER)
    dg_dz = jnp.where(z >= 0.0, e, 1.0) / (1.0 + e) * (1.0 / GATE_NORMALIZER)
    row = lax.broadcasted_iota(jnp.int32, (lr_rows.shape[0], hk), 0)
    pad = first_group & (row < PAD_ROWS)
    return jnp.where(pad, 0.0, g), jnp.where(pad, 0.0, dg_dz)


def _gla_fwd(proj_b, lr, wg_f, bg_f, wg_b, bg_b, gla_g, dm):
    lp, hk, hv, nc, c0, hw = dm.LP, dm.HK, dm.HV, dm.NC, dm.C0, dm.HW
    scale = hk ** -0.5
    gc = _group_chunks(dm)
    gr, ng = gc * CHUNK, (nc - c0) // gc

    def body(p_ref, lr_ref, wf_ref, bf_ref, wb_ref, bb_ref, gg_ref, o_ref, y_ref, st_ref, b_out, gs_out, oacc_f, oacc_b):
        low_incl, up_strict, ones_low, ones_up = _group_masks(gr)
        if c0 > 0:
            zr = c0 * CHUNK
            o_ref[0:zr, :] = jnp.zeros((zr, hv), BF16)
            y_ref[0:zr, :] = jnp.zeros((zr, hv), BF16)
            b_out[:, 0:zr, :] = jnp.zeros((2, zr, hk), F32)
            gs_out[:, 0:zr, :] = jnp.zeros((2, zr, hk), F32)
            st_ref[0, 0, :, 0:c0] = jnp.zeros((2, c0, hv, hk), BF16)

        def decay(gi, fwd):
            w_ref, b_ref = (wf_ref, bf_ref) if fwd else (wb_ref, bb_ref)
            r0 = _first_row(c0 + gi * gc)
            yield
            g, dg_dz = _log_gate(lr_ref[pl.ds(r0, gr), :], w_ref, b_ref, gi == 0, hk)
            gs_out[0 if fwd else 1, pl.ds(r0, gr), :] = dg_dz
            yield
            b = _dot_exact01(ones_low if fwd else ones_up, g)
            b_out[0 if fwd else 1, pl.ds(r0, gr), :] = b
            return b

        def group(gi, st, b, fwd):
            oacc = oacc_f if fwd else oacc_b
            r0 = pl.multiple_of((c0 + gi * gc) * CHUNK, CHUNK)
            blk = p_ref[pl.ds(r0, gr), :]
            q = blk[:, :hk].astype(F32) * scale
            k = blk[:, hk:2 * hk].astype(F32)
            v = blk[:, 2 * hk:2 * hk + hv]
            btot = _chunk_totals(b, fwd)
            qi = (q * jnp.exp(b)).astype(BF16)
            ki = (k * jnp.exp(-b)).astype(BF16)
            kd = (k * jnp.exp(btot - b)).astype(BF16)
            dec = jnp.exp(btot)
            a = _dot_nt(qi, ki)
            yield
            o = _dot(jnp.where(low_incl if fwd else up_strict, a, 0.0).astype(BF16), v)
            chunk_rows = [slice(c * CHUNK, (c + 1) * CHUNK) for c in range(gc)]
            kv = [_dot_tn(v[rows], kd[rows]) for rows in chunk_rows]
            for c in (range(gc) if fwd else reversed(range(gc))):
                yield
                rows = chunk_rows[c]
                st_b = st.astype(BF16)
                st_ref[0, 0, 0 if fwd else 1, c0 + gi * gc + c] = st_b
                oacc[pl.ds(r0 + c * CHUNK, CHUNK), :] = o[rows] + _dot_nt(qi[rows], st_b)
                st = st * dec[c * CHUNK:c * CHUNK + 1] + kv[c]
            return st

        def step(i, carry):
            st_f, st_b, b_f, b_b = carry
            gf, gb = i, ng - 1 - i
            return tuple(_interleave([group(gf, st_f, b_f, True), group(gb, st_b, b_b, False),
                                      decay(jnp.minimum(gf + 1, ng - 1), True), decay(jnp.maximum(gb - 1, 0), False)]))

        zero = jnp.zeros((hv, hk), F32)
        lax.fori_loop(0, ng, step, (zero, zero, *_interleave([decay(0, True), decay(ng - 1, False)])))

        def finish(i, carry):
            r0 = pl.multiple_of((c0 + i * gc) * CHUNK, CHUNK)
            o = oacc_f[pl.ds(r0, gr), :] + oacc_b[pl.ds(r0, gr), :]
            r = p_ref[pl.ds(r0, gr), 2 * hk + hv:].astype(F32)
            on = o * lax.rsqrt(jnp.mean(o * o, axis=-1, keepdims=True) + EPS) * gg_ref[...]
            o_ref[pl.ds(r0, gr), :] = o.astype(BF16)
            y_ref[pl.ds(r0, gr), :] = (on * r * _sigmoid(r)).astype(BF16)
            return carry

        lax.fori_loop(0, ng, finish, 0)

    head = lambda s, h: (s, h)
    wspec = pl.BlockSpec((LR_LANES, hk), lambda s, h: (0, h))
    bspec = pl.BlockSpec((1, hk), lambda s, h: (0, h))
    return pl.pallas_call(
        body, name="gla_fwd", grid=(dm.Bl, HEADS),
        in_specs=[pl.BlockSpec((lp, hw), head), pl.BlockSpec((lp, LR_LANES), lambda s, h: (s, 0)),
                  wspec, bspec, wspec, bspec, pl.BlockSpec((1, hv), lambda s, h: (0, 0))],
        out_specs=[pl.BlockSpec((lp, hv), head), pl.BlockSpec((lp, hv), head),
                   pl.BlockSpec((1, 1, 2, nc, hv, hk), lambda s, h: (s, h, 0, 0, 0, 0)),
                   pl.BlockSpec((2, lp, hk), lambda s, h: (0, s, h)), pl.BlockSpec((2, lp, hk), lambda s, h: (0, s, h))],
        out_shape=[jax.ShapeDtypeStruct((dm.T, dm.DV), BF16), jax.ShapeDtypeStruct((dm.T, dm.DV), BF16),
                   jax.ShapeDtypeStruct((dm.Bl, HEADS, 2, nc, hv, hk), BF16),
                   jax.ShapeDtypeStruct((2, dm.T, dm.DK), F32), jax.ShapeDtypeStruct((2, dm.T, dm.DK), F32)],
        scratch_shapes=[pltpu.VMEM((lp, hv), F32), pltpu.VMEM((lp, hv), F32)],
        compiler_params=_cp(2),
    )(proj_b, lr, wg_f, bg_f, wg_b, bg_b, gla_g)


def _gla_bwd(proj_b, lr, o_all, dy_gla, states, decays, gate_slopes, wg_f, wg_b, gla_g, dm):
    lp, hk, hv, nc, c0, hw = dm.LP, dm.HK, dm.HV, dm.NC, dm.C0, dm.HW
    scale = hk ** -0.5
    gc = _group_chunks(dm)
    gr, ng = gc * CHUNK, (nc - c0) // gc

    def body(p_ref, lr_ref, o_ref, dy_ref, st_ref, b_ref, gs_ref, wf_ref, wb_ref, gg_ref,
             d_ref, dlr_ref, gwf_ref, gbf_ref, gwb_ref, gbb_ref, ggg_ref, do_s, dq_s, dk_s, dv_s, dlr_s):
        low_incl, up_strict, ones_low, ones_up = _group_masks(gr)
        h = pl.program_id(1)

        @pl.when(h == 0)
        def _():
            dlr_ref[...] = jnp.zeros_like(dlr_ref)

        if c0 > 0:
            zr = c0 * CHUNK
            d_ref[0:zr, :] = jnp.zeros((zr, hw), BF16)
        for acc in (dq_s, dk_s, dv_s, dlr_s):
            acc[...] = jnp.zeros_like(acc)

        def norm_bwd(i, ggg):
            r0 = pl.multiple_of((c0 + i * gc) * CHUNK, CHUNK)
            o = o_ref[pl.ds(r0, gr), :].astype(F32)
            dy = dy_ref[pl.ds(r0, gr), :].astype(F32)
            r = p_ref[pl.ds(r0, gr), 2 * hk + hv:].astype(F32)
            rstd = lax.rsqrt(jnp.mean(o * o, axis=-1, keepdims=True) + EPS)
            ohat = o * rstd
            sg = _sigmoid(r)
            d_on = dy * (r * sg)
            d_ref[pl.ds(r0, gr), 2 * hk + hv:] = (dy * ohat * gg_ref[...] * (sg * (1.0 + r * (1.0 - sg)))).astype(BF16)
            d_oh = d_on * gg_ref[...]
            do_s[pl.ds(r0, gr), :] = (rstd * (d_oh - ohat * jnp.mean(d_oh * ohat, axis=-1, keepdims=True))).astype(BF16)
            return ggg + jnp.sum(d_on * ohat, axis=0, keepdims=True)

        ggg = lax.fori_loop(0, ng, norm_bwd, jnp.zeros((1, hv), F32))

        @pl.when((pl.program_id(0) == 0) & (h == 0))
        def _():
            ggg_ref[...] = jnp.zeros_like(ggg_ref)

        ggg_ref[0:1, :] += ggg

        def load(gi):
            r0 = pl.multiple_of((c0 + gi * gc) * CHUNK, CHUNK)
            blk = p_ref[pl.ds(r0, gr), :]
            return r0, blk[:, :hk].astype(F32) * scale, blk[:, hk:2 * hk].astype(F32), blk[:, 2 * hk:2 * hk + hv]

        zero = jnp.zeros((hv, hk), F32)

        def grad(gi, carry, fwd):
            dst, gw, gb = carry
            w_ref, way = (wf_ref, 0) if fwd else (wb_ref, 1)
            mask = low_incl if fwd else up_strict
            r0, q, k, v = load(gi)
            b = b_ref[way, pl.ds(r0, gr), :]
            btot = _chunk_totals(b, fwd)
            eb, enb, edb, dec = jnp.exp(b), jnp.exp(-b), jnp.exp(btot - b), jnp.exp(btot)
            qi_f, ki_f, kd_f = q * eb, k * enb, k * edb
            qi, ki, kd = qi_f.astype(BF16), ki_f.astype(BF16), kd_f.astype(BF16)
            do = do_s[pl.ds(r0, gr), :]
            a = _dot_nt(qi, ki)
            da = _dot_nt(do, v)
            yield
            a = jnp.where(mask, a, 0.0).astype(BF16)
            da = jnp.where(mask, da, 0.0).astype(BF16)
            dv = _dot_tn(a, do)
            dqi = _dot(da, ki)
            dki = _dot_tn(da, qi)
            dv_c, dqi_c, dkd_c, extra_c = [None] * gc, [None] * gc, [None] * gc, [None] * gc
            chunk_rows = [slice(c * CHUNK, (c + 1) * CHUNK) for c in range(gc)]
            qdo = [_dot_tn(do[rows], qi[rows]) for rows in chunk_rows]
            for c in (reversed(range(gc)) if fwd else range(gc)):
                yield
                rows = chunk_rows[c]
                st = st_ref[0, 0, way, c0 + gi * gc + c]
                dsn_b = dst.astype(BF16)
                dec_c = dec[c * CHUNK:c * CHUNK + 1]
                dv_c[c] = dv[rows] + _dot_nt(kd[rows], dsn_b)
                dqi_c[c] = dqi[rows] + _dot(do[rows], st)
                dkd_c[c] = _dot(v[rows], dsn_b)
                ddec = jnp.sum(st.astype(F32) * dst, axis=0, keepdims=True)
                extra = jnp.sum(dkd_c[c] * kd_f[rows], axis=0, keepdims=True) + ddec * dec_c
                extra_c[c] = jnp.broadcast_to(extra, (CHUNK, hk))
                dst = dst * dec_c + qdo[c]
            yield
            dv, dqi = jnp.concatenate(dv_c, axis=0), jnp.concatenate(dqi_c, axis=0)
            dkd, extra = jnp.concatenate(dkd_c, axis=0), jnp.concatenate(extra_c, axis=0)
            dq_s[pl.ds(r0, gr), :] += dqi * eb * scale
            dk_s[pl.ds(r0, gr), :] += dki * enb + dkd * edb
            dv_s[pl.ds(r0, gr), :] += dv
            db = dqi * qi_f - dki * ki_f - dkd * kd_f
            dg = _dot_exact01(ones_up if fwd else ones_low, db) + extra
            yield
            dz = dg * gs_ref[way, pl.ds(r0, gr), :]
            dz_b = dz.astype(BF16)
            dlr_s[pl.ds(r0, gr), :] += _dot_nt(dz_b, w_ref[...])
            return dst, gw + _dot_tn(lr_ref[pl.ds(r0, gr), :], dz_b), gb + jnp.sum(dz, axis=0, keepdims=True)

        def grad_step(i, carry):
            return tuple(_interleave([grad(ng - 1 - i, carry[0], True), grad(i, carry[1], False)]))

        init = (zero, jnp.zeros((LR_LANES, hk), F32), jnp.zeros((1, hk), F32))
        (_, gw_f, gb_f), (_, gw_b, gb_b) = lax.fori_loop(0, ng, grad_step, (init, init))
        for gw_ref, gb_ref, gw, gb in ((gwf_ref, gbf_ref, gw_f, gb_f), (gwb_ref, gbb_ref, gw_b, gb_b)):
            gw_ref[0] = gw
            gb_ref[0] = jnp.zeros((8, hk), F32)
            gb_ref[0, 0:1, :] = gb

        def combine(i, carry):
            r0 = pl.multiple_of((c0 + i * gc) * CHUNK, CHUNK)
            d_ref[pl.ds(r0, gr), 0:hk] = dq_s[pl.ds(r0, gr), :].astype(BF16)
            d_ref[pl.ds(r0, gr), hk:2 * hk] = dk_s[pl.ds(r0, gr), :].astype(BF16)
            d_ref[pl.ds(r0, gr), 2 * hk:2 * hk + hv] = dv_s[pl.ds(r0, gr), :].astype(BF16)
            dlr_ref[pl.ds(r0, gr), :] += dlr_s[pl.ds(r0, gr), :]
            return carry

        lax.fori_loop(0, ng, combine, 0)

    head = lambda s, h: (s, h)
    wspec = pl.BlockSpec((LR_LANES, hk), lambda s, h: (0, h))
    gwspec = pl.BlockSpec((1, LR_LANES, hk), lambda s, h: (s, 0, h))
    gbspec = pl.BlockSpec((1, 8, hk), lambda s, h: (s, 0, h))
    gw_shape = jax.ShapeDtypeStruct((dm.Bl, LR_LANES, dm.DK), F32)
    gb_shape = jax.ShapeDtypeStruct((dm.Bl, 8, dm.DK), F32)
    both = pl.BlockSpec((2, lp, hk), lambda s, h: (0, s, h))
    return pl.pallas_call(
        body, name="gla_bwd", grid=(dm.Bl, HEADS),
        in_specs=[pl.BlockSpec((lp, hw), head), pl.BlockSpec((lp, LR_LANES), lambda s, h: (s, 0)),
                  pl.BlockSpec((lp, hv), head), pl.BlockSpec((lp, hv), head),
                  pl.BlockSpec((1, 1, 2, nc, hv, hk), lambda s, h: (s, h, 0, 0, 0, 0)), both, both,
                  wspec, wspec, pl.BlockSpec((1, hv), lambda s, h: (0, 0))],
        out_specs=[pl.BlockSpec((lp, hw), head), pl.BlockSpec((lp, LR_LANES), lambda s, h: (s, 0)),
                   gwspec, gbspec, gwspec, gbspec, pl.BlockSpec((8, hv), lambda s, h: (0, 0))],
        out_shape=[jax.ShapeDtypeStruct((dm.T, HEADS * hw), BF16), jax.ShapeDtypeStruct((dm.T, LR_LANES), F32),
                   gw_shape, gb_shape, gw_shape, gb_shape, jax.ShapeDtypeStruct((8, hv), F32)],
        scratch_shapes=[pltpu.VMEM((lp, hv), BF16), pltpu.VMEM((lp, hk), F32), pltpu.VMEM((lp, hk), F32),
                        pltpu.VMEM((lp, hv), F32), pltpu.VMEM((lp, LR_LANES), F32)],
        compiler_params=_cp(2),
    )(proj_b, lr, o_all, dy_gla, states, decays, gate_slopes, wg_f, wg_b, gla_g)


def _out_merge(y_conv, y_gla, proj_c, w_oc, w_og, dm):
    d = dm.D
    tm = _pick(dm.T, 512, 16)

    def body(yc_ref, yg_ref, c_ref, woc_ref, wog_ref, pc_ref, pg_ref, m_ref):
        pc = _dot(yc_ref[...], woc_ref[...])
        pg = _dot(yg_ref[...], wog_ref[...])
        pc_ref[...] = pc.astype(BF16)
        pg_ref[...] = pg.astype(BF16)
        ma = c_ref[:, :d].astype(F32)
        mb = c_ref[:, d:].astype(F32)
        m_ref[...] = (_sigmoid(ma) * pc + _sigmoid(mb) * pg).astype(BF16)

    row = pl.BlockSpec((tm, d), lambda i: (i, 0))
    full = pl.BlockSpec((d, d), lambda i: (0, 0))
    act = jax.ShapeDtypeStruct((dm.T, d), BF16)
    return pl.pallas_call(
        body, name="out_merge", grid=(dm.T // tm,),
        in_specs=[row, row, pl.BlockSpec((tm, 2 * d), lambda i: (i, 0)), full, full],
        out_specs=[row, row, row], out_shape=[act, act, act], compiler_params=_cp(1),
    )(y_conv, y_gla, proj_c, w_oc, w_og)


def _stream_tiles(n_tiles, loads, stores, compute):
    for cp in loads(0, 0):
        cp.start()

    def step(t, carry):
        slot = t % 2

        @pl.when(t + 1 < n_tiles)
        def _():
            for cp in loads(t + 1, 1 - slot):
                cp.start()

        for cp in loads(t, slot):
            cp.wait()

        @pl.when(t >= 2)
        def _():
            for cp in stores(t - 2, slot):
                cp.wait()

        compute(t, slot)
        for cp in stores(t, slot):
            cp.start()
        return carry

    lax.fori_loop(0, n_tiles, step, 0)
    for t in range(max(n_tiles - 2, 0), n_tiles):
        for cp in stores(t, t % 2):
            cp.wait()


def _token_tiles(dm):
    rows = _pick(dm.S, 512, 16)
    per_seq = dm.S // rows
    return rows, dm.Bl * per_seq, lambda t: pl.multiple_of((t // per_seq) * dm.LP + dm.TM + (t % per_seq) * rows, 16)


def _final_fwd(merged, w_out, x, target, g_post, dm):
    d, tm = dm.D, dm.TM
    rows, n_tiles, first_row = _token_tiles(dm)

    def body(m_hbm, w_ref, x_hbm, t_hbm, g_ref, dout_hbm, dy_hbm, st_ref, mbuf, xbuf, tbuf, obuf, ybuf, zbuf,
             sem_in, sem_out, sem_zero):
        def loads(t, slot):
            return [pltpu.make_async_copy(m_hbm.at[pl.ds(first_row(t), rows), :], mbuf.at[slot], sem_in.at[0, slot]),
                    pltpu.make_async_copy(x_hbm.at[pl.ds(t * rows, rows), :], xbuf.at[slot], sem_in.at[1, slot]),
                    pltpu.make_async_copy(t_hbm.at[pl.ds(t * rows, rows), :], tbuf.at[slot], sem_in.at[2, slot])]

        def stores(t, slot):
            return [pltpu.make_async_copy(obuf.at[slot], dout_hbm.at[pl.ds(first_row(t), rows), :], sem_out.at[0, slot]),
                    pltpu.make_async_copy(ybuf.at[slot], dy_hbm.at[pl.ds(first_row(t), rows), :], sem_out.at[1, slot])]

        def compute(t, slot):
            out = _dot(mbuf[slot], w_ref[...])
            rstd = lax.rsqrt(jnp.mean(out * out, axis=-1, keepdims=True) + EPS)
            ohat = out * rstd
            err = xbuf[slot] + ohat * g_ref[...] - tbuf[slot]
            dy = err * (1.0 / d)
            d_oh = dy * g_ref[...]
            obuf[slot] = (rstd * (d_oh - ohat * jnp.mean(d_oh * ohat, axis=-1, keepdims=True))).astype(BF16)
            ybuf[slot] = dy.astype(BF16)
            st_ref[0:1, :] += jnp.sum(dy * ohat, axis=0, keepdims=True)
            st_ref[1:2, :] += jnp.sum(err * err, axis=0, keepdims=True)

        st_ref[...] = jnp.zeros_like(st_ref)
        zbuf[...] = jnp.zeros_like(zbuf)
        zeros = [pltpu.make_async_copy(zbuf, out.at[pl.ds(b * dm.LP, tm), :], sem_zero.at[i, b])
                 for i, out in enumerate((dout_hbm, dy_hbm)) for b in range(dm.Bl)]
        for cp in zeros:
            cp.start()
        _stream_tiles(n_tiles, loads, stores, compute)
        for cp in zeros:
            cp.wait()

    any_spec, vmem = pl.BlockSpec(memory_space=pl.ANY), pl.BlockSpec(memory_space=pltpu.VMEM)
    act = jax.ShapeDtypeStruct((dm.T, d), BF16)
    return pl.pallas_call(
        body, name="final_fwd", in_specs=[any_spec, vmem, any_spec, any_spec, vmem], out_specs=[any_spec, any_spec, vmem],
        out_shape=[act, act, jax.ShapeDtypeStruct((8, d), F32)],
        scratch_shapes=[pltpu.VMEM((2, rows, d), BF16), pltpu.VMEM((2, rows, d), F32), pltpu.VMEM((2, rows, d), F32),
                        pltpu.VMEM((2, rows, d), BF16), pltpu.VMEM((2, rows, d), BF16), pltpu.VMEM((tm, d), BF16),
                        pltpu.SemaphoreType.DMA((3, 2)), pltpu.SemaphoreType.DMA((2, 2)), pltpu.SemaphoreType.DMA((2, dm.Bl))],
        compiler_params=pltpu.CompilerParams(vmem_limit_bytes=VMEM_LIMIT_BYTES),
    )(merged, w_out, x.reshape(dm.Bl * dm.S, d), target.reshape(dm.Bl * dm.S, d), g_post)


def _merge_bwd(d_out, proj_c, p_conv, p_gla, w_out, w_oc, w_og, dm):
    d = dm.D
    tm = _pick(dm.T, 512, 16)

    def body(do_ref, c_ref, pc_ref, pg_ref, wo_ref, woc_ref, wog_ref, dpc_ref, dpg_ref, dc_ref, dyc_ref, dyg_ref):
        dmg = _dot_nt(do_ref[...], wo_ref[...])
        sa = _sigmoid(c_ref[:, :d].astype(F32))
        sb = _sigmoid(c_ref[:, d:].astype(F32))
        dpc = (dmg * sa).astype(BF16)
        dpg = (dmg * sb).astype(BF16)
        dpc_ref[...] = dpc
        dpg_ref[...] = dpg
        dc_ref[:, :d] = (dmg * pc_ref[...].astype(F32) * sa * (1.0 - sa)).astype(BF16)
        dc_ref[:, d:] = (dmg * pg_ref[...].astype(F32) * sb * (1.0 - sb)).astype(BF16)
        dyc_ref[...] = _dot_nt(dpc, woc_ref[...]).astype(BF16)
        dyg_ref[...] = _dot_nt(dpg, wog_ref[...]).astype(BF16)

    row = pl.BlockSpec((tm, d), lambda i: (i, 0))
    row2 = pl.BlockSpec((tm, 2 * d), lambda i: (i, 0))
    full = pl.BlockSpec((d, d), lambda i: (0, 0))
    act = jax.ShapeDtypeStruct((dm.T, d), BF16)
    return pl.pallas_call(
        body, name="merge_bwd", grid=(dm.T // tm,),
        in_specs=[row, row2, row, row, full, full, full],
        out_specs=[row, row, row2, row, row],
        out_shape=[act, act, jax.ShapeDtypeStruct((dm.T, 2 * d), BF16), act, act],
        compiler_params=_cp(1),
    )(d_out, proj_c, p_conv, p_gla, w_out, w_oc, w_og)


def _prenorm_bwd(du, dy, x, metapad, g_pre, dm):
    d, tm = dm.D, dm.TM
    rows, n_tiles, first_row = _token_tiles(dm)

    def body(du_hbm, dy_hbm, x_hbm, mp_ref, g_ref, gx_hbm, dmeta_ref, gg_ref, dubuf, dybuf, xbuf, gbuf, mbuf,
             sem_in, sem_out, sem_meta):
        def norm_bwd(h, du, dy):
            rstd = lax.rsqrt(jnp.mean(h * h, axis=-1, keepdims=True) + EPS)
            hhat = h * rstd
            dug = du * g_ref[...]
            gg_ref[0:1, :] += jnp.sum(du * hhat, axis=0, keepdims=True)
            return dy + rstd * (dug - hhat * jnp.mean(dug * hhat, axis=-1, keepdims=True))

        def loads(t, slot):
            return [pltpu.make_async_copy(du_hbm.at[pl.ds(first_row(t), rows), :], dubuf.at[slot], sem_in.at[0, slot]),
                    pltpu.make_async_copy(dy_hbm.at[pl.ds(first_row(t), rows), :], dybuf.at[slot], sem_in.at[1, slot]),
                    pltpu.make_async_copy(x_hbm.at[pl.ds(t * rows, rows), :], xbuf.at[slot], sem_in.at[2, slot])]

        def stores(t, slot):
            return [pltpu.make_async_copy(gbuf.at[slot], gx_hbm.at[pl.ds(t * rows, rows), :], sem_out.at[slot])]

        def compute(t, slot):
            gbuf[slot] = norm_bwd(xbuf[slot], dubuf[slot].astype(F32), dybuf[slot].astype(F32))

        gg_ref[...] = jnp.zeros_like(gg_ref)
        meta = [pltpu.make_async_copy(du_hbm.at[pl.ds(b * dm.LP, tm), :], mbuf.at[b], sem_meta.at[b]) for b in range(dm.Bl)]
        for cp in meta:
            cp.start()
        _stream_tiles(n_tiles, loads, stores, compute)
        for b, cp in enumerate(meta):
            cp.wait()
            dmeta_ref[b] = norm_bwd(mp_ref[...], mbuf[b].astype(F32), 0.0)

    any_spec, vmem = pl.BlockSpec(memory_space=pl.ANY), pl.BlockSpec(memory_space=pltpu.VMEM)
    grad_x, d_meta, gg = pl.pallas_call(
        body, name="prenorm_bwd", in_specs=[any_spec, any_spec, any_spec, vmem, vmem], out_specs=[any_spec, vmem, vmem],
        out_shape=[jax.ShapeDtypeStruct((dm.Bl * dm.S, d), F32), jax.ShapeDtypeStruct((dm.Bl, tm, d), F32),
                   jax.ShapeDtypeStruct((8, d), F32)],
        scratch_shapes=[pltpu.VMEM((2, rows, d), BF16), pltpu.VMEM((2, rows, d), BF16), pltpu.VMEM((2, rows, d), F32),
                        pltpu.VMEM((2, rows, d), F32), pltpu.VMEM((dm.Bl, tm, d), BF16),
                        pltpu.SemaphoreType.DMA((3, 2)), pltpu.SemaphoreType.DMA((2,)), pltpu.SemaphoreType.DMA((dm.Bl,))],
        compiler_params=pltpu.CompilerParams(vmem_limit_bytes=VMEM_LIMIT_BYTES),
    )(du, dy, x.reshape(dm.Bl * dm.S, d), metapad, g_pre)
    return grad_x.reshape(dm.Bl, dm.S, d), d_meta, gg


def _adamw(partials, w, m, v, name, by_columns=False):
    r, c = w.shape
    n_parts = partials.shape[0]
    tr, tc = (r, _pick(c, 128, 128)) if by_columns else (_pick(r, 256, 16), c)

    def body(p_ref, w_ref, m_ref, v_ref, g_ref, d_ref, nm_ref, nv_ref):
        g = p_ref[0].astype(F32)
        for j in range(1, n_parts):
            g = g + p_ref[j].astype(F32)
        g_ref[...] = g
        d_ref[...], nm_ref[...], nv_ref[...] = _adam_step(g, w_ref[...], m_ref[...], v_ref[...])

    at = (lambda i: (0, i)) if by_columns else (lambda i: (i, 0))
    tile = pl.BlockSpec((tr, tc), at)
    out = jax.ShapeDtypeStruct((r, c), F32)
    return pl.pallas_call(
        body, name=name, grid=(c // tc if by_columns else r // tr,),
        in_specs=[pl.BlockSpec((n_parts, tr, tc), lambda i: (0,) + at(i)), tile, tile, tile],
        out_specs=[tile, tile, tile, tile], out_shape=[out, out, out, out], compiler_params=_cp(1),
    )(partials, w, m, v)


def _adam_step(g, w, m, v):
    m2 = ADAM_B1 * m + (1.0 - ADAM_B1) * g
    v2 = ADAM_B2 * v + (1.0 - ADAM_B2) * (g * g)
    m_hat = m2 / (1.0 - ADAM_B1 ** ADAM_STEP)
    v_hat = v2 / (1.0 - ADAM_B2 ** ADAM_STEP)
    return -ADAM_LR * (m_hat / (jnp.sqrt(v_hat) + ADAM_EPS) + ADAM_WD * w), m2, v2


def _adamw_small(items, name):
    n = len(items)

    def body(*refs):
        ins, outs = refs[:4 * n], refs[4 * n:]
        for i in range(n):
            p_ref, w_ref, m_ref, v_ref = ins[4 * i:4 * i + 4]
            g = p_ref[0]
            for j in range(1, p_ref.shape[0]):
                g = g + p_ref[j]
            delta, m2, v2 = _adam_step(g, w_ref[...], m_ref[...], v_ref[...])
            for o_ref, val in zip(outs[4 * i:4 * i + 4], (g, delta, m2, v2)):
                o_ref[...] = val

    vmem = pl.BlockSpec(memory_space=pltpu.VMEM)
    res = pl.pallas_call(
        body, name=name, in_specs=[vmem] * (4 * n), out_specs=[vmem] * (4 * n),
        out_shape=[jax.ShapeDtypeStruct(w.shape, F32) for _, w, _, _ in items for _ in range(4)],
    )(*[a for item in items for a in item])
    return [res[4 * i:4 * i + 4] for i in range(n)]


def _pack_rows(wt, dm):
    d, dk, hk, hv, cw, nj = dm.D, dm.DK, dm.HK, dm.HV, dm.CW, dm.NJ
    a = wt[:4 * d].reshape(4, nj, cw, d).transpose(1, 0, 2, 3).reshape(4 * d, d)
    b = jnp.concatenate([wt[4 * d:4 * d + dk].reshape(HEADS, hk, d), wt[4 * d + dk:5 * d].reshape(HEADS, hk, d),
                         wt[5 * d:6 * d].reshape(HEADS, hv, d), wt[6 * d:7 * d].reshape(HEADS, hv, d)],
                        axis=1).reshape(3 * d, d)
    c = wt[7 * d + 2 * RANK:]
    lr = jnp.pad(wt[7 * d:7 * d + 2 * RANK], ((0, LR_LANES - 2 * RANK), (0, 0)))
    return a, b, c, lr


def _unpack_rows(a, b, c, lr, dm):
    d, hk, hv, cw, nj, hw = dm.D, dm.HK, dm.HV, dm.CW, dm.NJ, dm.HW
    conv = a.reshape(nj, 4, cw, d).transpose(1, 0, 2, 3).reshape(4 * d, d)
    heads = b.reshape(HEADS, hw, d)
    q = heads[:, :hk].reshape(HEADS * hk, d)
    k = heads[:, hk:2 * hk].reshape(HEADS * hk, d)
    v = heads[:, 2 * hk:2 * hk + hv].reshape(HEADS * hv, d)
    r = heads[:, 2 * hk + hv:].reshape(HEADS * hv, d)
    return jnp.concatenate([conv, q, k, v, r, lr[:2 * RANK], c], axis=0)


def _column_shards(g, shard_shape):
    r, c = g.shape
    return g.reshape(r, N_DEV, c // N_DEV).transpose(1, 0, 2).reshape((N_DEV,) + tuple(shard_shape))


def _join_column_shards(parts):
    r, c = parts.shape[-2:]
    return parts.reshape(N_DEV, r, c).transpose(1, 0, 2).reshape(r, N_DEV * c)


def _local_step(x, target, meta, g_pre, u, wt_in, conv_w, wg_f, bg_f, wg_b, bg_b, gla_g, out_weights, g_post,
                on_matrix_grads=None):
    bl, s, d = x.shape
    dm = _Dims(bl, s, d)
    metapad = jnp.concatenate([jnp.zeros((dm.TM - N_META, d), F32), meta], axis=0)
    wta, wtb, wtc, wtlr = _pack_rows(wt_in, dm)
    wgp_f = jnp.pad(wg_f, ((0, LR_LANES - RANK), (0, 0))).astype(BF16)
    wgp_b = jnp.pad(wg_b, ((RANK, LR_LANES - 2 * RANK), (0, 0))).astype(BF16)

    u = _prenorm_meta(u, metapad, g_pre, dm)
    proj_a, proj_b, proj_c, lr = _inproj(u, [wta, wtb, wtc, wtlr], dm)
    y_conv = _conv_fwd(proj_a, conv_w, dm)
    o_all, y_gla, states, decays, gate_slopes = _gla_fwd(proj_b, lr, wgp_f, bg_f, wgp_b, bg_b, gla_g, dm)
    w_oc, w_og, w_out = out_weights(y_conv) if callable(out_weights) else out_weights
    p_conv, p_gla, merged = _out_merge(y_conv, y_gla, proj_c, w_oc, w_og, dm)
    d_out, dy, stats = _final_fwd(merged, w_out, x, target, g_post, dm)
    loss = 0.5 / d * jnp.sum(stats[1])

    d_pc, d_pg, d_c, dy_conv, dy_gla = _merge_bwd(d_out, proj_c, p_conv, p_gla, w_out, w_oc, w_og, dm)
    g_out = _matmul_tn(merged, d_out, BF16, "grad_w_out")
    g_oc = _matmul_tn(y_conv, d_pc, BF16, "grad_w_out_conv")
    g_og = _matmul_tn(y_gla, d_pg, BF16, "grad_w_out_gla")
    d_a, g_conv = _conv_bwd(proj_a, dy_conv, conv_w, dm)
    d_b, d_lr, gwp_f, gbp_f, gwp_b, gbp_b, g_gla = _gla_bwd(proj_b, lr, o_all, dy_gla, states, decays, gate_slopes, wgp_f, wgp_b, gla_g, dm)
    g_in = _unpack_rows(_matmul_tn(d_a, u, BF16, "grad_w_in_conv"), _matmul_tn(d_b, u, BF16, "grad_w_in_gla"),
                        _matmul_tn(d_c, u, BF16, "grad_w_in_merge"), _matmul_tn(d_lr, u, BF16, "grad_w_in_gate"), dm)
    if on_matrix_grads is not None:
        wtlr = wtlr + on_matrix_grads(dict(w_in=g_in, w_out_conv=g_oc, w_out_gla=g_og, w_merge_out=g_out)).astype(BF16)
    du = _grad_u([d_a, d_b, d_c, d_lr], [wta, wtb, wtc, wtlr], dm)
    grad_x, d_meta, g_pre_rows = _prenorm_bwd(du, dy, x, metapad, g_pre, dm)

    grads = dict(
        meta_tokens=jnp.sum(d_meta[:, dm.TM - N_META:, :], axis=0), norm_pre=g_pre_rows[0:1], w_in=g_in,
        conv_w=g_conv[0:3], w_gate_fwd=jnp.sum(gwp_f, axis=0)[:RANK], b_gate_fwd=jnp.sum(gbp_f, axis=0)[0:1],
        w_gate_bwd=jnp.sum(gwp_b, axis=0)[RANK:2 * RANK], b_gate_bwd=jnp.sum(gbp_b, axis=0)[0:1],
        gla_norm=g_gla[0:1], w_out_conv=g_oc, w_out_gla=g_og, w_merge_out=g_out, norm_post=stats[0:1])
    return loss, grad_x, grads


MATRICES = ("w_out_conv", "w_out_gla", "w_merge_out")
SMALL_SHARDED = ("meta_tokens", "conv_w", "w_gate_fwd", "w_gate_bwd")
REPLICATED = ("norm_pre", "b_gate_fwd", "b_gate_bwd", "gla_norm", "norm_post")
NAMES = ("meta_tokens", "norm_pre", "w_in", "conv_w", "w_gate_fwd", "b_gate_fwd", "w_gate_bwd", "b_gate_bwd", "gla_norm",
         "w_out_conv", "w_out_gla", "w_merge_out", "norm_post")


def kernel(x, meta_tokens, norm_pre, w_in, conv_w, w_gate_fwd, b_gate_fwd, w_gate_bwd, b_gate_bwd, gla_norm, w_out_conv, w_out_gla, w_merge_out, norm_post, loss_target, m_meta_tokens, m_norm_pre, m_w_in, m_conv_w, m_w_gate_fwd, m_b_gate_fwd, m_w_gate_bwd, m_b_gate_bwd, m_gla_norm, m_w_out_conv, m_w_out_gla, m_w_merge_out, m_norm_post, v_meta_tokens, v_norm_pre, v_w_in, v_conv_w, v_w_gate_fwd, v_b_gate_fwd, v_w_gate_bwd, v_b_gate_bwd, v_gla_norm, v_w_out_conv, v_w_out_gla, v_w_merge_out, v_norm_post):
    w = dict(meta_tokens=meta_tokens, norm_pre=norm_pre, w_in=w_in[0], conv_w=conv_w, w_gate_fwd=w_gate_fwd,
             b_gate_fwd=b_gate_fwd, w_gate_bwd=w_gate_bwd, b_gate_bwd=b_gate_bwd, gla_norm=gla_norm,
             w_out_conv=w_out_conv[0], w_out_gla=w_out_gla[0], w_merge_out=w_merge_out[0], norm_post=norm_post)
    m = dict(meta_tokens=m_meta_tokens, norm_pre=m_norm_pre, w_in=m_w_in[0], conv_w=m_conv_w, w_gate_fwd=m_w_gate_fwd,
             b_gate_fwd=m_b_gate_fwd, w_gate_bwd=m_w_gate_bwd, b_gate_bwd=m_b_gate_bwd, gla_norm=m_gla_norm,
             w_out_conv=m_w_out_conv[0], w_out_gla=m_w_out_gla[0], w_merge_out=m_w_merge_out[0], norm_post=m_norm_post)
    v = dict(meta_tokens=v_meta_tokens, norm_pre=v_norm_pre, w_in=v_w_in[0], conv_w=v_conv_w, w_gate_fwd=v_w_gate_fwd,
             b_gate_fwd=v_b_gate_fwd, w_gate_bwd=v_w_gate_bwd, b_gate_bwd=v_b_gate_bwd, gla_norm=v_gla_norm,
             w_out_conv=v_w_out_conv[0], w_out_gla=v_w_out_gla[0], w_merge_out=v_w_merge_out[0], norm_post=v_norm_post)
    d = x.shape[-1]

    dm = _Dims(*x.shape)
    wt_all, *small_all, u = _gather_two_level([w["w_in"].T.astype(BF16)] + [w[n] for n in SMALL_SHARDED], "gather_weights",
                                              _prenorm_tokens_side(x, norm_pre, dm))
    _, late_weights = _exchange_start([w[n].astype(BF16) for n in MATRICES], [], small_all[0], "gather_out_weights_start")
    wt_in = wt_all.reshape(-1, d)
    small = {n: _join_column_shards(p) for n, p in zip(SMALL_SHARDED, small_all)}

    def out_weights(after):
        return tuple(a.reshape(-1, d) for a in _exchange_wait(late_weights, after, "gather_out_weights_wait"))

    pending = []

    def on_matrix_grads(g):
        to_send = [g[n].astype(BF16).reshape(N_DEV, -1, d) for n in ("w_in",) + MATRICES]
        token, state = _exchange_start([], to_send, None, "exchange_grads_start")
        pending.append(state)
        return token

    loss, grad_x, grads = _local_step(
        x, loss_target, small["meta_tokens"], norm_pre, u, wt_in, small["conv_w"], small["w_gate_fwd"], b_gate_fwd,
        small["w_gate_bwd"], b_gate_bwd, gla_norm, out_weights, norm_post, on_matrix_grads)
    loss = lax.psum(loss, ("x", "y", "c"))
    received = _exchange_wait(pending[0], grad_x, "exchange_grads_wait")

    small_recv = _exchange([grads[n] for n in REPLICATED], [_column_shards(grads[n], w[n].shape) for n in SMALL_SHARDED],
                           "exchange_small_grads")

    results = {"w_in": [r.T[None] for r in _adamw(received[0], w["w_in"].T, m["w_in"].T, v["w_in"].T, "adamw_w_in", by_columns=True)]}
    for n, partials in zip(MATRICES, received[1:4]):
        results[n] = [r[None] for r in _adamw(partials, w[n], m[n], v[n], "adamw_" + n)]
    small_names = REPLICATED + SMALL_SHARDED
    results.update(zip(small_names, _adamw_small([(p, w[n], m[n], v[n]) for n, p in zip(small_names, small_recv)], "adamw_small")))
    return (loss, grad_x, *[results[n][i] for i in range(4) for n in NAMES])
```

```python
import jax
import jax.numpy as jnp
from jax import lax
from jax.experimental import pallas as pl
from jax.experimental.pallas import tpu as pltpu

F32 = jnp.float32
BF16 = jnp.bfloat16
MESH = pl.DeviceIdType.MESH

N_META = 16
CHUNK = 64
CHUNK_SHIFT = 6
HEADS = 4
RANK = 16
LR_LANES = 128
PAD_ROWS = CHUNK - N_META
EPS = 1e-6
GATE_NORMALIZER = 16.0
N_DEV = 8
ADAM_LR, ADAM_B1, ADAM_B2, ADAM_EPS, ADAM_WD, ADAM_STEP = 0.001, 0.9, 0.999, 1e-08, 0.01, 10
VMEM_LIMIT_BYTES = 56 * 1024 * 1024


class _Dims:
    def __init__(self, bl, s, d):
        self.Bl, self.S, self.D = bl, s, d
        self.TM = CHUNK
        self.LP = self.TM + s
        self.T = bl * self.LP
        self.TPS = self.LP // self.TM
        self.NC = self.LP // CHUNK
        self.C0 = (self.TM - CHUNK) // CHUNK
        self.DK, self.DV = d // 2, d
        self.HK, self.HV = self.DK // HEADS, self.DV // HEADS
        self.HW = 2 * self.HK + 2 * self.HV
        self.CW = 256 if d % 256 == 0 and d > 256 else d // 4
        self.NJ = d // self.CW


def _pick(n, target, mult):
    t = min(n, target)
    while t >= mult:
        if n % t == 0 and t % mult == 0:
            return t
        t -= mult
    return n


def _cp(n_axes):
    return pltpu.CompilerParams(dimension_semantics=("arbitrary",) * n_axes, vmem_limit_bytes=VMEM_LIMIT_BYTES)


def _sigmoid(x):
    return 1.0 / (1.0 + jnp.exp(-x))


def _dot(a, b):
    return jnp.dot(a, b, preferred_element_type=F32)


def _dot_nt(a, b):
    return lax.dot_general(a, b, (((1,), (1,)), ((), ())), preferred_element_type=F32)


def _dot_tn(a, b):
    return lax.dot_general(a, b, (((0,), (0,)), ((), ())), preferred_element_type=F32)


def _dot_exact01(m01, x):
    hi = x.astype(BF16)
    lo = (x - hi.astype(F32)).astype(BF16)
    return _dot(m01, hi) + _dot(m01, lo)


def _exchange(gathers, scatters, name):
    arrays = list(gathers) + list(scatters)
    n, ng = len(arrays), len(gathers)

    def body(*refs):
        ins, outs = refs[:n], refs[n:2 * n]
        send_sems, recv_sems, local_sems = refs[2 * n:]
        x, y, c = lax.axis_index("x"), lax.axis_index("y"), lax.axis_index("c")
        me = 4 * x + 2 * y + c
        started = []
        for t in range(n):
            src, dst = ins[t], outs[t]
            own = pltpu.make_async_copy(src if t < ng else src.at[me], dst.at[me], local_sems.at[t])
            own.start()
            started.append(own)
            for k, pos, peer in _peers(x, y, c):
                cp = pltpu.make_async_remote_copy(
                    src_ref=src if t < ng else src.at[peer], dst_ref=dst.at[me],
                    send_sem=send_sems.at[t * (N_DEV - 1) + k - 1], recv_sem=recv_sems.at[t * (N_DEV - 1) + k - 1],
                    device_id=pos, device_id_type=MESH)
                cp.start()
                started.append(cp)
        for cp in started:
            cp.wait()

    out_shape = [jax.ShapeDtypeStruct((N_DEV,) + a.shape if t < ng else a.shape, a.dtype) for t, a in enumerate(arrays)]
    any_spec = pl.BlockSpec(memory_space=pl.ANY)
    return pl.pallas_call(
        body, name=name, out_shape=out_shape, in_specs=[any_spec] * n, out_specs=[any_spec] * n,
        scratch_shapes=[pltpu.SemaphoreType.DMA((n * (N_DEV - 1),)), pltpu.SemaphoreType.DMA((n * (N_DEV - 1),)),
                        pltpu.SemaphoreType.DMA((n,))],
        compiler_params=pltpu.CompilerParams(has_side_effects=True),
    )(*arrays)


def _gather_two_level(arrays, name, side=None):
    n = len(arrays)
    per = N_DEV - 1
    work, side_in, side_in_specs, side_out, side_out_specs, side_scratch = side or (None, [], [], [], [], [])
    n_in, n_out = len(side_in), len(side_out)

    def body(*refs):
        ins, outs = refs[:n], refs[n + n_in:2 * n + n_in]
        send_sems, recv_sems, local_sems = refs[2 * n + n_in + n_out:2 * n + n_in + n_out + 3]
        x, y, c = lax.axis_index("x"), lax.axis_index("y"), lax.axis_index("c")
        sibling = (x, y, 1 - c)
        chips = [(1 - x, y), (x, 1 - y), (1 - x, 1 - y)]
        index = lambda px, py, pc: 4 * px + 2 * py + pc

        def copy(t, k, block, to, from_input=False):
            slab = outs[t].at[index(*block)]
            return pltpu.make_async_remote_copy(
                src_ref=ins[t] if from_input else slab, dst_ref=slab, send_sem=send_sems.at[t * per + k],
                recv_sem=recv_sems.at[t * per + k], device_id=to, device_id_type=MESH)

        own, sent = [], []
        for t in range(n):
            own.append(pltpu.make_async_copy(ins[t], outs[t].at[index(x, y, c)], local_sems.at[t]))
            own[-1].start()
            first = [copy(t, 0, (x, y, c), sibling, True)]
            first += [copy(t, 1 + j, (x, y, c), (*chip, c), True) for j, chip in enumerate(chips)]
            for cp in first:
                cp.start()
            sent += first
        if work is not None:
            work(refs[n:n + n_in], refs[2 * n + n_in:2 * n + n_in + n_out], refs[2 * n + n_in + n_out + 3:])
        for t in range(n):
            for j, chip in enumerate(chips):
                copy(t, 1 + j, (*chip, c), (x, y, c)).wait_recv()
                sent.append(copy(t, 4 + j, (*chip, c), sibling))
                sent[-1].start()
        for t in range(n):
            copy(t, 0, sibling, (x, y, c)).wait_recv()
            for j, chip in enumerate(chips):
                copy(t, 4 + j, (*chip, 1 - c), (x, y, c)).wait_recv()
        for cp in sent:
            cp.wait_send()
        for cp in own:
            cp.wait()

    out_shape = [jax.ShapeDtypeStruct((N_DEV,) + a.shape, a.dtype) for a in arrays]
    any_spec = pl.BlockSpec(memory_space=pl.ANY)
    return pl.pallas_call(
        body, name=name, out_shape=out_shape + list(side_out), in_specs=[any_spec] * n + list(side_in_specs),
        out_specs=[any_spec] * n + list(side_out_specs),
        scratch_shapes=[pltpu.SemaphoreType.DMA((n * per,)), pltpu.SemaphoreType.DMA((n * per,)),
                        pltpu.SemaphoreType.DMA((n,))] + list(side_scratch),
        compiler_params=pltpu.CompilerParams(has_side_effects=True, vmem_limit_bytes=VMEM_LIMIT_BYTES),
    )(*arrays, *side_in)


def _peers(x, y, c):
    out = []
    for k in range(1, N_DEV):
        px = 1 - x if (k >> 2) & 1 else x
        py = 1 - y if (k >> 1) & 1 else y
        pc = 1 - c if k & 1 else c
        out.append((k, (px, py, pc), 4 * px + 2 * py + pc))
    return out


def _exchange_start(gathers, scatters, after, name):
    arrays = list(gathers) + list(scatters)
    n, ng = len(arrays), len(gathers)
    hbm = pl.BlockSpec(memory_space=pltpu.HBM)
    sem = pl.BlockSpec(memory_space=pltpu.SEMAPHORE)

    extra = [] if after is None else [after]
    ne = len(extra)

    def body(*refs):
        ins, lands = refs[:n], refs[n:2 * n]
        send_sems, recv_sems = refs[2 * n + ne], refs[2 * n + ne + 1]
        token = refs[4 * n + ne + 2]
        x, y, c = lax.axis_index("x"), lax.axis_index("y"), lax.axis_index("c")
        me = 4 * x + 2 * y + c
        for t in range(n):
            for k, pos, peer in _peers(x, y, c):
                pltpu.make_async_remote_copy(
                    src_ref=ins[t] if t < ng else ins[t].at[peer], dst_ref=lands[t].at[me],
                    send_sem=send_sems.at[t * (N_DEV - 1) + k - 1], recv_sem=recv_sems.at[t * (N_DEV - 1) + k - 1],
                    device_id=pos, device_id_type=MESH).start()
        token[...] = jnp.zeros_like(token)

    me = 4 * lax.axis_index("x") + 2 * lax.axis_index("y") + lax.axis_index("c")
    lands = [lax.dynamic_update_index_in_dim(lax.empty((N_DEV,) + a.shape if t < ng else a.shape, a.dtype),
                                             a if t < ng else lax.dynamic_index_in_dim(a, me, 0, keepdims=False), me, 0)
             for t, a in enumerate(arrays)]
    operands = [pltpu.with_memory_space_constraint(a, pltpu.HBM) for a in arrays + lands]
    sems = pltpu.SemaphoreType.DMA((n * (N_DEV - 1),))
    res = pl.pallas_call(
        body, name=name,
        out_shape=(sems, sems, *[pltpu.HBM(a.shape, a.dtype) for a in arrays + lands], jax.ShapeDtypeStruct((8, 128), F32)),
        in_specs=[hbm] * (2 * n) + [pl.BlockSpec(memory_space=pl.ANY)] * ne,
        out_specs=(sem, sem, *[hbm] * (2 * n), pl.BlockSpec(memory_space=pltpu.VMEM)),
        input_output_aliases={i: 2 + i for i in range(2 * n)},
        compiler_params=pltpu.CompilerParams(has_side_effects=pltpu.SideEffectType.DATAFLOW_SIDE_EFFECTING),
    )(*operands, *extra)
    return res[-1][0, 0], (ng, res[0], res[1], list(res[2:2 + n]), list(res[2 + n:2 + 2 * n]))


def _exchange_wait(state, after, name):
    ng, send_sems, recv_sems, sent, lands = state
    n = len(sent)
    hbm = pl.BlockSpec(memory_space=pltpu.HBM)
    sem = pl.BlockSpec(memory_space=pltpu.SEMAPHORE)

    def body(*refs):
        ins, land_refs = refs[:n], refs[n:2 * n]
        send_ref, recv_ref = refs[2 * n], refs[2 * n + 1]
        x, y, c = lax.axis_index("x"), lax.axis_index("y"), lax.axis_index("c")
        me = 4 * x + 2 * y + c
        for t in range(n):
            for k, pos, peer in _peers(x, y, c):
                cp = pltpu.make_async_remote_copy(
                    src_ref=ins[t] if t < ng else ins[t].at[peer], dst_ref=land_refs[t].at[me],
                    send_sem=send_ref.at[t * (N_DEV - 1) + k - 1], recv_sem=recv_ref.at[t * (N_DEV - 1) + k - 1],
                    device_id=pos, device_id_type=MESH)
                cp.wait_send()
                cp.wait_recv()

    res = pl.pallas_call(
        body, name=name, out_shape=tuple(pltpu.HBM(a.shape, a.dtype) for a in sent + lands),
        in_specs=[hbm] * (2 * n) + [sem, sem, pl.BlockSpec(memory_space=pl.ANY)], out_specs=tuple([hbm] * (2 * n)),
        input_output_aliases={i: i for i in range(2 * n)},
        compiler_params=pltpu.CompilerParams(has_side_effects=pltpu.SideEffectType.DATAFLOW_SIDE_EFFECTING),
    )(*sent, *lands, send_sems, recv_sems, after)
    return list(res[n:])


def _rms_scaled(h, g):
    return (h * lax.rsqrt(jnp.mean(h * h, axis=-1, keepdims=True) + EPS) * g).astype(BF16)


def _prenorm_tokens_side(x, g_pre, dm):
    bl, s, d = x.shape
    rows = _pick(s, 512, 16)
    tiles = [(b, j) for b in range(bl) for j in range(s // rows)]

    def work(ins, outs, scratch):
        (x_ref, g_ref), (u_ref,), (xbuf, ubuf, sem_in, sem_out) = ins, outs, scratch

        def load(t, slot):
            b, j = tiles[t]
            return pltpu.make_async_copy(x_ref.at[b, pl.ds(j * rows, rows), :], xbuf.at[slot], sem_in.at[slot])

        def store(t, slot):
            b, j = tiles[t]
            return pltpu.make_async_copy(ubuf.at[slot], u_ref.at[pl.ds(b * dm.LP + dm.TM + j * rows, rows), :], sem_out.at[slot])

        load(0, 0).start()
        for t in range(len(tiles)):
            slot = t % 2
            if t + 1 < len(tiles):
                load(t + 1, 1 - slot).start()
            load(t, slot).wait()
            if t >= 2:
                store(t - 2, slot).wait()
            ubuf[slot] = _rms_scaled(xbuf[slot], g_ref[...])
            store(t, slot).start()
        for t in range(max(len(tiles) - 2, 0), len(tiles)):
            store(t, t % 2).wait()

    any_spec = pl.BlockSpec(memory_space=pl.ANY)
    return (work, [x, g_pre], [any_spec, pl.BlockSpec(memory_space=pltpu.VMEM)],
            [jax.ShapeDtypeStruct((dm.T, d), BF16)], [any_spec],
            [pltpu.VMEM((2, rows, d), F32), pltpu.VMEM((2, rows, d), BF16), pltpu.SemaphoreType.DMA((2,)),
             pltpu.SemaphoreType.DMA((2,))])


def _prenorm_meta(u, metapad, g_pre, dm):
    tm, tps, d = dm.TM, dm.TPS, dm.D

    def body(u_in, mp_ref, g_ref, u_ref):
        u_ref[...] = _rms_scaled(mp_ref[...], g_ref[...])

    return pl.pallas_call(
        body, name="prenorm_meta", grid=(dm.Bl,),
        in_specs=[pl.BlockSpec(memory_space=pl.ANY), pl.BlockSpec((tm, d), lambda i: (0, 0)),
                  pl.BlockSpec((1, d), lambda i: (0, 0))],
        out_specs=pl.BlockSpec((tm, d), lambda i: (i * tps, 0)),
        out_shape=jax.ShapeDtypeStruct((dm.T, d), BF16), input_output_aliases={0: 0}, compiler_params=_cp(1),
    )(u, metapad, g_pre)


def _matmul_tn(a, b, out_dtype, name, tt=2304, tn=1024, tk=1024):
    t, k = a.shape
    n = b.shape[1]
    tt, tn, tk = _pick(t, tt, 16), _pick(n, tn, 128), _pick(k, tk, 128)
    nt = t // tt

    def body(a_ref, b_ref, o_ref, acc):
        p = _dot_tn(a_ref[...].astype(BF16), b_ref[...].astype(BF16))
        i = pl.program_id(2)

        @pl.when(i == 0)
        def _():
            acc[...] = p

        @pl.when(i > 0)
        def _():
            acc[...] += p

        @pl.when(i == nt - 1)
        def _():
            o_ref[...] = acc[...].astype(out_dtype)

    return pl.pallas_call(
        body, name=name, grid=(k // tk, n // tn, nt),
        in_specs=[pl.BlockSpec((tt, tk), lambda kk, j, i: (i, kk)), pl.BlockSpec((tt, tn), lambda kk, j, i: (i, j))],
        out_specs=pl.BlockSpec((tk, tn), lambda kk, j, i: (kk, j)),
        out_shape=jax.ShapeDtypeStruct((k, n), out_dtype), scratch_shapes=[pltpu.VMEM((tk, tn), F32)],
        compiler_params=_cp(3),
    )(a, b)


def _load_resident(hbm_refs, vmem_refs, sems):
    @pl.when(pl.program_id(0) == 0)
    def _():
        copies = [pltpu.make_async_copy(h, v, sems.at[i]) for i, (h, v) in enumerate(zip(hbm_refs, vmem_refs))]
        for cp in copies:
            cp.start()
        for cp in copies:
            cp.wait()


def _inproj(u, wts, dm):
    t, d = u.shape
    tm = _pick(t, 512, 16)
    np_ = len(wts)
    cn = 1024

    def body(*refs):
        u_ref, w_hbm, outs = refs[0], refs[1:1 + np_], refs[1 + np_:1 + 2 * np_]
        w_vmem, sems = refs[1 + 2 * np_:1 + 3 * np_], refs[1 + 3 * np_]
        _load_resident(w_hbm, w_vmem, sems)
        ut = u_ref[...]
        for w, o_ref in zip(w_vmem, outs):
            n = w.shape[0]
            step = cn if n % cn == 0 else n
            for j in range(0, n, step):
                o_ref[:, j:j + step] = _dot_nt(ut, w[j:j + step, :]).astype(BF16)

    return pl.pallas_call(
        body, name="inproj", grid=(t // tm,),
        in_specs=[pl.BlockSpec((tm, d), lambda i: (i, 0))] + [pl.BlockSpec(memory_space=pl.ANY)] * np_,
        out_specs=[pl.BlockSpec((tm, w.shape[0]), lambda i: (i, 0)) for w in wts],
        out_shape=[jax.ShapeDtypeStruct((t, w.shape[0]), BF16) for w in wts],
        scratch_shapes=[pltpu.VMEM(w.shape, BF16) for w in wts] + [pltpu.SemaphoreType.DMA((np_,))],
        compiler_params=_cp(1),
    )(u, *wts)


def _grad_u(d_parts, wts, dm):
    t = d_parts[0].shape[0]
    d = wts[0].shape[1]
    tm = _pick(t, 512, 16)
    np_ = len(wts)

    def body(*refs):
        d_refs, w_hbm, o_ref = refs[:np_], refs[np_:2 * np_], refs[2 * np_]
        w_vmem, sems, acc = refs[2 * np_ + 1:3 * np_ + 1], refs[3 * np_ + 1], refs[3 * np_ + 2]
        _load_resident(w_hbm, w_vmem, sems)
        acc[...] = _dot(d_refs[0][...].astype(BF16), w_vmem[0][...])
        for a_ref, w in zip(d_refs[1:], w_vmem[1:]):
            acc[...] += _dot(a_ref[...].astype(BF16), w[...])
        o_ref[...] = acc[...].astype(BF16)

    return pl.pallas_call(
        body, name="grad_u", grid=(t // tm,),
        in_specs=[pl.BlockSpec((tm, a.shape[1]), lambda i: (i, 0)) for a in d_parts] + [pl.BlockSpec(memory_space=pl.ANY)] * np_,
        out_specs=pl.BlockSpec((tm, d), lambda i: (i, 0)), out_shape=jax.ShapeDtypeStruct((t, d), BF16),
        scratch_shapes=[pltpu.VMEM(w.shape, BF16) for w in wts] + [pltpu.SemaphoreType.DMA((np_,)), pltpu.VMEM((tm, d), F32)],
        compiler_params=_cp(1),
    )(*d_parts, *wts)


def _conv_rows(dm):
    return _pick(dm.LP, 256, 16)


def _shifted(m, prev_row, next_row, rows):
    row = lax.broadcasted_iota(jnp.int32, m.shape, 0)
    m_prev = jnp.where(row == 0, prev_row, pltpu.roll(m, 1, 0))
    m_next = jnp.where(row == rows - 1, next_row, pltpu.roll(m, rows - 1, 0))
    return m_prev, m_next


def _conv_fwd(proj_a, conv_w, dm):
    lp, cw, rc = dm.LP, dm.CW, _conv_rows(dm)
    nchunk = lp // rc

    def body(p_ref, w_ref, y_ref):
        w0, w1, w2 = w_ref[0:1, :], w_ref[1:2, :], w_ref[2:3, :]

        def chunk(ci, carry):
            r0 = pl.multiple_of(ci * rc, rc)
            blk = p_ref[pl.ds(r0, rc), :].astype(F32)
            cb, cc, cx, cz = (blk[:, i * cw:(i + 1) * cw] for i in range(4))
            m = cc * cx
            rp = pl.multiple_of(jnp.maximum(r0 - 16, 0), 16)
            rn = pl.multiple_of(jnp.minimum(r0 + rc, lp - 16), 16)
            pv = p_ref[pl.ds(rp, 16), cw:3 * cw].astype(F32)
            nx = p_ref[pl.ds(rn, 16), cw:3 * cw].astype(F32)
            prev_row = jnp.where(ci > 0, pv[15:16, :cw] * pv[15:16, cw:], 0.0)
            next_row = jnp.where(ci < nchunk - 1, nx[0:1, :cw] * nx[0:1, cw:], 0.0)
            m_prev, m_next = _shifted(m, prev_row, next_row, rc)
            s = w0 * m_prev + w1 * m + w2 * m_next
            y_ref[pl.ds(r0, rc), :] = (cb * s * (cz * _sigmoid(cz))).astype(BF16)
            return carry

        lax.fori_loop(0, nchunk, chunk, 0)

    return pl.pallas_call(
        body, name="conv_fwd", grid=(dm.Bl, dm.NJ),
        in_specs=[pl.BlockSpec((lp, 4 * cw), lambda s, j: (s, j)), pl.BlockSpec((3, cw), lambda s, j: (0, j))],
        out_specs=pl.BlockSpec((lp, cw), lambda s, j: (s, j)),
        out_shape=jax.ShapeDtypeStruct((dm.T, dm.D), BF16), compiler_params=_cp(2),
    )(proj_a, conv_w)


def _conv_bwd(proj_a, dy_conv, conv_w, dm):
    lp, cw, rc = dm.LP, dm.CW, _conv_rows(dm)
    nchunk = lp // rc

    def body(p_ref, dy_ref, w_ref, d_ref, gw_ref):
        w0, w1, w2 = w_ref[0:1, :], w_ref[1:2, :], w_ref[2:3, :]

        def ds_of(p4, dy):
            cb, cz = p4[:, :cw], p4[:, 3 * cw:]
            return dy * cb * (cz * _sigmoid(cz))

        def chunk(ci, carry):
            g0, g1, g2 = carry
            r0 = pl.multiple_of(ci * rc, rc)
            blk = p_ref[pl.ds(r0, rc), :].astype(F32)
            dy = dy_ref[pl.ds(r0, rc), :].astype(F32)
            cb, cc, cx, cz = (blk[:, i * cw:(i + 1) * cw] for i in range(4))
            rp = pl.multiple_of(jnp.maximum(r0 - 16, 0), 16)
            rn = pl.multiple_of(jnp.minimum(r0 + rc, lp - 16), 16)
            pv = p_ref[pl.ds(rp, 16), :].astype(F32)[15:16]
            nx = p_ref[pl.ds(rn, 16), :].astype(F32)[0:1]
            dpv = dy_ref[pl.ds(rp, 16), :].astype(F32)[15:16]
            dnx = dy_ref[pl.ds(rn, 16), :].astype(F32)[0:1]
            has_prev, has_next = ci > 0, ci < nchunk - 1
            m = cc * cx
            m_prev, m_next = _shifted(m, jnp.where(has_prev, pv[:, cw:2 * cw] * pv[:, 2 * cw:3 * cw], 0.0),
                                      jnp.where(has_next, nx[:, cw:2 * cw] * nx[:, 2 * cw:3 * cw], 0.0), rc)
            s = w0 * m_prev + w1 * m + w2 * m_next
            sg = _sigmoid(cz)
            silu = cz * sg
            ds = dy * cb * silu
            ds_prev, ds_next = _shifted(ds, jnp.where(has_prev, ds_of(pv, dpv), 0.0),
                                        jnp.where(has_next, ds_of(nx, dnx), 0.0), rc)
            dm_ = w0 * ds_next + w1 * ds + w2 * ds_prev
            d_ref[pl.ds(r0, rc), 0:cw] = (dy * s * silu).astype(BF16)
            d_ref[pl.ds(r0, rc), cw:2 * cw] = (dm_ * cx).astype(BF16)
            d_ref[pl.ds(r0, rc), 2 * cw:3 * cw] = (dm_ * cc).astype(BF16)
            d_ref[pl.ds(r0, rc), 3 * cw:4 * cw] = (dy * cb * s * (sg * (1.0 + cz * (1.0 - sg)))).astype(BF16)
            return (g0 + jnp.sum(ds * m_prev, axis=0, keepdims=True), g1 + jnp.sum(ds * m, axis=0, keepdims=True),
                    g2 + jnp.sum(ds * m_next, axis=0, keepdims=True))

        z = jnp.zeros((1, cw), F32)
        g0, g1, g2 = lax.fori_loop(0, nchunk, chunk, (z, z, z))

        @pl.when(pl.program_id(1) == 0)
        def _():
            gw_ref[...] = jnp.zeros_like(gw_ref)

        gw_ref[0:1, :] += g0
        gw_ref[1:2, :] += g1
        gw_ref[2:3, :] += g2

    return pl.pallas_call(
        body, name="conv_bwd", grid=(dm.NJ, dm.Bl),
        in_specs=[pl.BlockSpec((lp, 4 * cw), lambda j, s: (s, j)), pl.BlockSpec((lp, cw), lambda j, s: (s, j)),
                  pl.BlockSpec((3, cw), lambda j, s: (0, j))],
        out_specs=[pl.BlockSpec((lp, 4 * cw), lambda j, s: (s, j)), pl.BlockSpec((8, cw), lambda j, s: (0, j))],
        out_shape=[jax.ShapeDtypeStruct((dm.T, 4 * dm.D), BF16), jax.ShapeDtypeStruct((8, dm.D), F32)],
        compiler_params=_cp(2),
    )(proj_a, dy_conv, conv_w)


def _interleave(gens):
    results = [None] * len(gens)
    live = list(range(len(gens)))
    while live:
        for idx in list(live):
            try:
                next(gens[idx])
            except StopIteration as done:
                results[idx] = done.value
                live.remove(idx)
    return results


def _group_chunks(dm):
    n = dm.NC - dm.C0
    return 3 if n % 3 == 0 else 1


def _group_masks(rows):
    ii = lax.broadcasted_iota(jnp.int32, (rows, rows), 0)
    jj = lax.broadcasted_iota(jnp.int32, (rows, rows), 1)
    same = jnp.right_shift(ii, CHUNK_SHIFT) == jnp.right_shift(jj, CHUNK_SHIFT)
    low, up = same & (jj <= ii), same & (jj >= ii)
    return low, same & (jj > ii), low.astype(BF16), up.astype(BF16)


def _first_row(chunk):
    return chunk * CHUNK if isinstance(chunk, int) else pl.multiple_of(chunk * CHUNK, CHUNK)


def _chunk_totals(b, fwd):
    hk = b.shape[1]
    rows = [b[c * CHUNK + CHUNK - 1:(c + 1) * CHUNK] if fwd else b[c * CHUNK:c * CHUNK + 1]
            for c in range(b.shape[0] // CHUNK)]
    return jnp.concatenate([jnp.broadcast_to(r, (CHUNK, hk)) for r in rows], axis=0)


def _log_gate(lr_rows, w_ref, b_ref, first_group, hk):
    z = _dot(lr_rows, w_ref[...]) + b_ref[...]
    e = jnp.exp(-jnp.abs(z))
    g = (jnp.minimum(z, 0.0) - jnp.log(1.0 + e)) * (1.0 / GATE_NORMALIZER)
    dg_dz = jnp.where(z >= 0.0, e, 1.0) / (1.0 + e) * (1.0 / GATE_NORMALIZER)
    row = lax.broadcasted_iota(jnp.int32, (lr_rows.shape[0], hk), 0)
    pad = first_group & (row < PAD_ROWS)
    return jnp.where(pad, 0.0, g), jnp.where(pad, 0.0, dg_dz)


def _gla_fwd(proj_b, lr, wg_f, bg_f, wg_b, bg_b, gla_g, dm):
    lp, hk, hv, nc, c0, hw = dm.LP, dm.HK, dm.HV, dm.NC, dm.C0, dm.HW
    scale = hk ** -0.5
    gc = _group_chunks(dm)
    gr, ng = gc * CHUNK, (nc - c0) // gc

    def body(p_ref, lr_ref, wf_ref, bf_ref, wb_ref, bb_ref, gg_ref, o_ref, y_ref, st_ref, b_out, gs_out, oacc_f, oacc_b):
        low_incl, up_strict, ones_low, ones_up = _group_masks(gr)
        if c0 > 0:
            zr = c0 * CHUNK
            o_ref[0:zr, :] = jnp.zeros((zr, hv), BF16)
            y_ref[0:zr, :] = jnp.zeros((zr, hv), BF16)
            b_out[:, 0:zr, :] = jnp.zeros((2, zr, hk), F32)
            gs_out[:, 0:zr, :] = jnp.zeros((2, zr, hk), F32)
            st_ref[0, 0, :, 0:c0] = jnp.zeros((2, c0, hv, hk), BF16)

        def decay(gi, fwd):
            w_ref, b_ref = (wf_ref, bf_ref) if fwd else (wb_ref, bb_ref)
            r0 = _first_row(c0 + gi * gc)
            yield
            g, dg_dz = _log_gate(lr_ref[pl.ds(r0, gr), :], w_ref, b_ref, gi == 0, hk)
            gs_out[0 if fwd else 1, pl.ds(r0, gr), :] = dg_dz
            yield
            b = _dot_exact01(ones_low if fwd else ones_up, g)
            b_out[0 if fwd else 1, pl.ds(r0, gr), :] = b
            return b

        def group(gi, st, b, fwd):
            oacc = oacc_f if fwd else oacc_b
            r0 = pl.multiple_of((c0 + gi * gc) * CHUNK, CHUNK)
            blk = p_ref[pl.ds(r0, gr), :]
            q = blk[:, :hk].astype(F32) * scale
            k = blk[:, hk:2 * hk].astype(F32)
            v = blk[:, 2 * hk:2 * hk + hv]
            btot = _chunk_totals(b, fwd)
            qi = (q * jnp.exp(b)).astype(BF16)
            ki = (k * jnp.exp(-b)).astype(BF16)
            kd = (k * jnp.exp(btot - b)).astype(BF16)
            dec = jnp.exp(btot)
            a = _dot_nt(qi, ki)
            yield
            o = _dot(jnp.where(low_incl if fwd else up_strict, a, 0.0).astype(BF16), v)
            chunk_rows = [slice(c * CHUNK, (c + 1) * CHUNK) for c in range(gc)]
            kv = [_dot_tn(v[rows], kd[rows]) for rows in chunk_rows]
            for c in (range(gc) if fwd else reversed(range(gc))):
                yield
                rows = chunk_rows[c]
                st_b = st.astype(BF16)
                st_ref[0, 0, 0 if fwd else 1, c0 + gi * gc + c] = st_b
                oacc[pl.ds(r0 + c * CHUNK, CHUNK), :] = o[rows] + _dot_nt(qi[rows], st_b)
                st = st * dec[c * CHUNK:c * CHUNK + 1] + kv[c]
            return st

        def step(i, carry):
            st_f, st_b, b_f, b_b = carry
            gf, gb = i, ng - 1 - i
            return tuple(_interleave([group(gf, st_f, b_f, True), group(gb, st_b, b_b, False),
                                      decay(jnp.minimum(gf + 1, ng - 1), True), decay(jnp.maximum(gb - 1, 0), False)]))

        zero = jnp.zeros((hv, hk), F32)
        lax.fori_loop(0, ng, step, (zero, zero, *_interleave([decay(0, True), decay(ng - 1, False)])))

        def finish(i, carry):
            r0 = pl.multiple_of((c0 + i * gc) * CHUNK, CHUNK)
            o = oacc_f[pl.ds(r0, gr), :] + oacc_b[pl.ds(r0, gr), :]
            r = p_ref[pl.ds(r0, gr), 2 * hk + hv:].astype(F32)
            on = o * lax.rsqrt(jnp.mean(o * o, axis=-1, keepdims=True) + EPS) * gg_ref[...]
            o_ref[pl.ds(r0, gr), :] = o.astype(BF16)
            y_ref[pl.ds(r0, gr), :] = (on * r * _sigmoid(r)).astype(BF16)
            return carry

        lax.fori_loop(0, ng, finish, 0)

    head = lambda s, h: (s, h)
    wspec = pl.BlockSpec((LR_LANES, hk), lambda s, h: (0, h))
    bspec = pl.BlockSpec((1, hk), lambda s, h: (0, h))
    return pl.pallas_call(
        body, name="gla_fwd", grid=(dm.Bl, HEADS),
        in_specs=[pl.BlockSpec((lp, hw), head), pl.BlockSpec((lp, LR_LANES), lambda s, h: (s, 0)),
                  wspec, bspec, wspec, bspec, pl.BlockSpec((1, hv), lambda s, h: (0, 0))],
        out_specs=[pl.BlockSpec((lp, hv), head), pl.BlockSpec((lp, hv), head),
                   pl.BlockSpec((1, 1, 2, nc, hv, hk), lambda s, h: (s, h, 0, 0, 0, 0)),
                   pl.BlockSpec((2, lp, hk), lambda s, h: (0, s, h)), pl.BlockSpec((2, lp, hk), lambda s, h: (0, s, h))],
        out_shape=[jax.ShapeDtypeStruct((dm.T, dm.DV), BF16), jax.ShapeDtypeStruct((dm.T, dm.DV), BF16),
                   jax.ShapeDtypeStruct((dm.Bl, HEADS, 2, nc, hv, hk), BF16),
                   jax.ShapeDtypeStruct((2, dm.T, dm.DK), F32), jax.ShapeDtypeStruct((2, dm.T, dm.DK), F32)],
        scratch_shapes=[pltpu.VMEM((lp, hv), F32), pltpu.VMEM((lp, hv), F32)],
        compiler_params=_cp(2),
    )(proj_b, lr, wg_f, bg_f, wg_b, bg_b, gla_g)


def _gla_bwd(proj_b, lr, o_all, dy_gla, states, decays, gate_slopes, wg_f, wg_b, gla_g, dm):
    lp, hk, hv, nc, c0, hw = dm.LP, dm.HK, dm.HV, dm.NC, dm.C0, dm.HW
    scale = hk ** -0.5
    gc = _group_chunks(dm)
    gr, ng = gc * CHUNK, (nc - c0) // gc

    def body(p_ref, lr_ref, o_ref, dy_ref, st_ref, b_ref, gs_ref, wf_ref, wb_ref, gg_ref,
             d_ref, dlr_ref, gwf_ref, gbf_ref, gwb_ref, gbb_ref, ggg_ref, do_s, dq_s, dk_s, dv_s, dlr_s):
        low_incl, up_strict, ones_low, ones_up = _group_masks(gr)
        h = pl.program_id(1)

        @pl.when(h == 0)
        def _():
            dlr_ref[...] = jnp.zeros_like(dlr_ref)

        if c0 > 0:
            zr = c0 * CHUNK
            d_ref[0:zr, :] = jnp.zeros((zr, hw), BF16)
        for acc in (dq_s, dk_s, dv_s, dlr_s):
            acc[...] = jnp.zeros_like(acc)

        def norm_bwd(i, ggg):
            r0 = pl.multiple_of((c0 + i * gc) * CHUNK, CHUNK)
            o = o_ref[pl.ds(r0, gr), :].astype(F32)
            dy = dy_ref[pl.ds(r0, gr), :].astype(F32)
            r = p_ref[pl.ds(r0, gr), 2 * hk + hv:].astype(F32)
            rstd = lax.rsqrt(jnp.mean(o * o, axis=-1, keepdims=True) + EPS)
            ohat = o * rstd
            sg = _sigmoid(r)
            d_on = dy * (r * sg)
            d_ref[pl.ds(r0, gr), 2 * hk + hv:] = (dy * ohat * gg_ref[...] * (sg * (1.0 + r * (1.0 - sg)))).astype(BF16)
            d_oh = d_on * gg_ref[...]
            do_s[pl.ds(r0, gr), :] = (rstd * (d_oh - ohat * jnp.mean(d_oh * ohat, axis=-1, keepdims=True))).astype(BF16)
            return ggg + jnp.sum(d_on * ohat, axis=0, keepdims=True)

        ggg = lax.fori_loop(0, ng, norm_bwd, jnp.zeros((1, hv), F32))

        @pl.when((pl.program_id(0) == 0) & (h == 0))
        def _():
            ggg_ref[...] = jnp.zeros_like(ggg_ref)

        ggg_ref[0:1, :] += ggg

        def load(gi):
            r0 = pl.multiple_of((c0 + gi * gc) * CHUNK, CHUNK)
            blk = p_ref[pl.ds(r0, gr), :]
            return r0, blk[:, :hk].astype(F32) * scale, blk[:, hk:2 * hk].astype(F32), blk[:, 2 * hk:2 * hk + hv]

        zero = jnp.zeros((hv, hk), F32)

        def grad(gi, carry, fwd):
            dst, gw, gb = carry
            w_ref, way = (wf_ref, 0) if fwd else (wb_ref, 1)
            mask = low_incl if fwd else up_strict
            r0, q, k, v = load(gi)
            b = b_ref[way, pl.ds(r0, gr), :]
            btot = _chunk_totals(b, fwd)
            eb, enb, edb, dec = jnp.exp(b), jnp.exp(-b), jnp.exp(btot - b), jnp.exp(btot)
            qi_f, ki_f, kd_f = q * eb, k * enb, k * edb
            qi, ki, kd = qi_f.astype(BF16), ki_f.astype(BF16), kd_f.astype(BF16)
            do = do_s[pl.ds(r0, gr), :]
            a = _dot_nt(qi, ki)
            da = _dot_nt(do, v)
            yield
            a = jnp.where(mask, a, 0.0).astype(BF16)
            da = jnp.where(mask, da, 0.0).astype(BF16)
            dv = _dot_tn(a, do)
            dqi = _dot(da, ki)
            dki = _dot_tn(da, qi)
            dv_c, dqi_c, dkd_c, extra_c = [None] * gc, [None] * gc, [None] * gc, [None] * gc
            chunk_rows = [slice(c * CHUNK, (c + 1) * CHUNK) for c in range(gc)]
            qdo = [_dot_tn(do[rows], qi[rows]) for rows in chunk_rows]
            for c in (reversed(range(gc)) if fwd else range(gc)):
                yield
                rows = chunk_rows[c]
                st = st_ref[0, 0, way, c0 + gi * gc + c]
                dsn_b = dst.astype(BF16)
                dec_c = dec[c * CHUNK:c * CHUNK + 1]
                dv_c[c] = dv[rows] + _dot_nt(kd[rows], dsn_b)
                dqi_c[c] = dqi[rows] + _dot(do[rows], st)
                dkd_c[c] = _dot(v[rows], dsn_b)
                ddec = jnp.sum(st.astype(F32) * dst, axis=0, keepdims=True)
                extra = jnp.sum(dkd_c[c] * kd_f[rows], axis=0, keepdims=True) + ddec * dec_c
                extra_c[c] = jnp.broadcast_to(extra, (CHUNK, hk))
                dst = dst * dec_c + qdo[c]
            yield
            dv, dqi = jnp.concatenate(dv_c, axis=0), jnp.concatenate(dqi_c, axis=0)
            dkd, extra = jnp.concatenate(dkd_c, axis=0), jnp.concatenate(extra_c, axis=0)
            dq_s[pl.ds(r0, gr), :] += dqi * eb * scale
            dk_s[pl.ds(r0, gr), :] += dki * enb + dkd * edb
            dv_s[pl.ds(r0, gr), :] += dv
            db = dqi * qi_f - dki * ki_f - dkd * kd_f
            dg = _dot_exact01(ones_up if fwd else ones_low, db) + extra
            yield
            dz = dg * gs_ref[way, pl.ds(r0, gr), :]
            dz_b = dz.astype(BF16)
            dlr_s[pl.ds(r0, gr), :] += _dot_nt(dz_b, w_ref[...])
            return dst, gw + _dot_tn(lr_ref[pl.ds(r0, gr), :], dz_b), gb + jnp.sum(dz, axis=0, keepdims=True)

        def grad_step(i, carry):
            return tuple(_interleave([grad(ng - 1 - i, carry[0], True), grad(i, carry[1], False)]))

        init = (zero, jnp.zeros((LR_LANES, hk), F32), jnp.zeros((1, hk), F32))
        (_, gw_f, gb_f), (_, gw_b, gb_b) = lax.fori_loop(0, ng, grad_step, (init, init))
        for gw_ref, gb_ref, gw, gb in ((gwf_ref, gbf_ref, gw_f, gb_f), (gwb_ref, gbb_ref, gw_b, gb_b)):
            gw_ref[0] = gw
            gb_ref[0] = jnp.zeros((8, hk), F32)
            gb_ref[0, 0:1, :] = gb

        def combine(i, carry):
            r0 = pl.multiple_of((c0 + i * gc) * CHUNK, CHUNK)
            d_ref[pl.ds(r0, gr), 0:hk] = dq_s[pl.ds(r0, gr), :].astype(BF16)
            d_ref[pl.ds(r0, gr), hk:2 * hk] = dk_s[pl.ds(r0, gr), :].astype(BF16)
            d_ref[pl.ds(r0, gr), 2 * hk:2 * hk + hv] = dv_s[pl.ds(r0, gr), :].astype(BF16)
            dlr_ref[pl.ds(r0, gr), :] += dlr_s[pl.ds(r0, gr), :]
            return carry

        lax.fori_loop(0, ng, combine, 0)

    head = lambda s, h: (s, h)
    wspec = pl.BlockSpec((LR_LANES, hk), lambda s, h: (0, h))
    gwspec = pl.BlockSpec((1, LR_LANES, hk), lambda s, h: (s, 0, h))
    gbspec = pl.BlockSpec((1, 8, hk), lambda s, h: (s, 0, h))
    gw_shape = jax.ShapeDtypeStruct((dm.Bl, LR_LANES, dm.DK), F32)
    gb_shape = jax.ShapeDtypeStruct((dm.Bl, 8, dm.DK), F32)
    both = pl.BlockSpec((2, lp, hk), lambda s, h: (0, s, h))
    return pl.pallas_call(
        body, name="gla_bwd", grid=(dm.Bl, HEADS),
        in_specs=[pl.BlockSpec((lp, hw), head), pl.BlockSpec((lp, LR_LANES), lambda s, h: (s, 0)),
                  pl.BlockSpec((lp, hv), head), pl.BlockSpec((lp, hv), head),
                  pl.BlockSpec((1, 1, 2, nc, hv, hk), lambda s, h: (s, h, 0, 0, 0, 0)), both, both,
                  wspec, wspec, pl.BlockSpec((1, hv), lambda s, h: (0, 0))],
        out_specs=[pl.BlockSpec((lp, hw), head), pl.BlockSpec((lp, LR_LANES), lambda s, h: (s, 0)),
                   gwspec, gbspec, gwspec, gbspec, pl.BlockSpec((8, hv), lambda s, h: (0, 0))],
        out_shape=[jax.ShapeDtypeStruct((dm.T, HEADS * hw), BF16), jax.ShapeDtypeStruct((dm.T, LR_LANES), F32),
                   gw_shape, gb_shape, gw_shape, gb_shape, jax.ShapeDtypeStruct((8, hv), F32)],
        scratch_shapes=[pltpu.VMEM((lp, hv), BF16), pltpu.VMEM((lp, hk), F32), pltpu.VMEM((lp, hk), F32),
                        pltpu.VMEM((lp, hv), F32), pltpu.VMEM((lp, LR_LANES), F32)],
        compiler_params=_cp(2),
    )(proj_b, lr, o_all, dy_gla, states, decays, gate_slopes, wg_f, wg_b, gla_g)


def _out_merge(y_conv, y_gla, proj_c, w_oc, w_og, dm):
    d = dm.D
    tm = _pick(dm.T, 512, 16)

    def body(yc_ref, yg_ref, c_ref, woc_ref, wog_ref, pc_ref, pg_ref, m_ref):
        pc = _dot(yc_ref[...], woc_ref[...])
        pg = _dot(yg_ref[...], wog_ref[...])
        pc_ref[...] = pc.astype(BF16)
        pg_ref[...] = pg.astype(BF16)
        ma = c_ref[:, :d].astype(F32)
        mb = c_ref[:, d:].astype(F32)
        m_ref[...] = (_sigmoid(ma) * pc + _sigmoid(mb) * pg).astype(BF16)

    row = pl.BlockSpec((tm, d), lambda i: (i, 0))
    full = pl.BlockSpec((d, d), lambda i: (0, 0))
    act = jax.ShapeDtypeStruct((dm.T, d), BF16)
    return pl.pallas_call(
        body, name="out_merge", grid=(dm.T // tm,),
        in_specs=[row, row, pl.BlockSpec((tm, 2 * d), lambda i: (i, 0)), full, full],
        out_specs=[row, row, row], out_shape=[act, act, act], compiler_params=_cp(1),
    )(y_conv, y_gla, proj_c, w_oc, w_og)


def _stream_tiles(n_tiles, loads, stores, compute):
    for cp in loads(0, 0):
        cp.start()

    def step(t, carry):
        slot = t % 2

        @pl.when(t + 1 < n_tiles)
        def _():
            for cp in loads(t + 1, 1 - slot):
                cp.start()

        for cp in loads(t, slot):
            cp.wait()

        @pl.when(t >= 2)
        def _():
            for cp in stores(t - 2, slot):
                cp.wait()

        compute(t, slot)
        for cp in stores(t, slot):
            cp.start()
        return carry

    lax.fori_loop(0, n_tiles, step, 0)
    for t in range(max(n_tiles - 2, 0), n_tiles):
        for cp in stores(t, t % 2):
            cp.wait()


def _token_tiles(dm):
    rows = _pick(dm.S, 512, 16)
    per_seq = dm.S // rows
    return rows, dm.Bl * per_seq, lambda t: pl.multiple_of((t // per_seq) * dm.LP + dm.TM + (t % per_seq) * rows, 16)


def _final_fwd(merged, w_out, x, target, g_post, dm):
    d, tm = dm.D, dm.TM
    rows, n_tiles, first_row = _token_tiles(dm)

    def body(m_hbm, w_ref, x_hbm, t_hbm, g_ref, dout_hbm, dy_hbm, st_ref, mbuf, xbuf, tbuf, obuf, ybuf, zbuf,
             sem_in, sem_out, sem_zero):
        def loads(t, slot):
            return [pltpu.make_async_copy(m_hbm.at[pl.ds(first_row(t), rows), :], mbuf.at[slot], sem_in.at[0, slot]),
                    pltpu.make_async_copy(x_hbm.at[pl.ds(t * rows, rows), :], xbuf.at[slot], sem_in.at[1, slot]),
                    pltpu.make_async_copy(t_hbm.at[pl.ds(t * rows, rows), :], tbuf.at[slot], sem_in.at[2, slot])]

        def stores(t, slot):
            return [pltpu.make_async_copy(obuf.at[slot], dout_hbm.at[pl.ds(first_row(t), rows), :], sem_out.at[0, slot]),
                    pltpu.make_async_copy(ybuf.at[slot], dy_hbm.at[pl.ds(first_row(t), rows), :], sem_out.at[1, slot])]

        def compute(t, slot):
            out = _dot(mbuf[slot], w_ref[...])
            rstd = lax.rsqrt(jnp.mean(out * out, axis=-1, keepdims=True) + EPS)
            ohat = out * rstd
            err = xbuf[slot] + ohat * g_ref[...] - tbuf[slot]
            dy = err * (1.0 / d)
            d_oh = dy * g_ref[...]
            obuf[slot] = (rstd * (d_oh - ohat * jnp.mean(d_oh * ohat, axis=-1, keepdims=True))).astype(BF16)
            ybuf[slot] = dy.astype(BF16)
            st_ref[0:1, :] += jnp.sum(dy * ohat, axis=0, keepdims=True)
            st_ref[1:2, :] += jnp.sum(err * err, axis=0, keepdims=True)

        st_ref[...] = jnp.zeros_like(st_ref)
        zbuf[...] = jnp.zeros_like(zbuf)
        zeros = [pltpu.make_async_copy(zbuf, out.at[pl.ds(b * dm.LP, tm), :], sem_zero.at[i, b])
                 for i, out in enumerate((dout_hbm, dy_hbm)) for b in range(dm.Bl)]
        for cp in zeros:
            cp.start()
        _stream_tiles(n_tiles, loads, stores, compute)
        for cp in zeros:
            cp.wait()

    any_spec, vmem = pl.BlockSpec(memory_space=pl.ANY), pl.BlockSpec(memory_space=pltpu.VMEM)
    act = jax.ShapeDtypeStruct((dm.T, d), BF16)
    return pl.pallas_call(
        body, name="final_fwd", in_specs=[any_spec, vmem, any_spec, any_spec, vmem], out_specs=[any_spec, any_spec, vmem],
        out_shape=[act, act, jax.ShapeDtypeStruct((8, d), F32)],
        scratch_shapes=[pltpu.VMEM((2, rows, d), BF16), pltpu.VMEM((2, rows, d), F32), pltpu.VMEM((2, rows, d), F32),
                        pltpu.VMEM((2, rows, d), BF16), pltpu.VMEM((2, rows, d), BF16), pltpu.VMEM((tm, d), BF16),
                        pltpu.SemaphoreType.DMA((3, 2)), pltpu.SemaphoreType.DMA((2, 2)), pltpu.SemaphoreType.DMA((2, dm.Bl))],
        compiler_params=pltpu.CompilerParams(vmem_limit_bytes=VMEM_LIMIT_BYTES),
    )(merged, w_out, x.reshape(dm.Bl * dm.S, d), target.reshape(dm.Bl * dm.S, d), g_post)


def _merge_bwd(d_out, proj_c, p_conv, p_gla, w_out, w_oc, w_og, dm):
    d = dm.D
    tm = _pick(dm.T, 512, 16)

    def body(do_ref, c_ref, pc_ref, pg_ref, wo_ref, woc_ref, wog_ref, dpc_ref, dpg_ref, dc_ref, dyc_ref, dyg_ref):
        dmg = _dot_nt(do_ref[...], wo_ref[...])
        sa = _sigmoid(c_ref[:, :d].astype(F32))
        sb = _sigmoid(c_ref[:, d:].astype(F32))
        dpc = (dmg * sa).astype(BF16)
        dpg = (dmg * sb).astype(BF16)
        dpc_ref[...] = dpc
        dpg_ref[...] = dpg
        dc_ref[:, :d] = (dmg * pc_ref[...].astype(F32) * sa * (1.0 - sa)).astype(BF16)
        dc_ref[:, d:] = (dmg * pg_ref[...].astype(F32) * sb * (1.0 - sb)).astype(BF16)
        dyc_ref[...] = _dot_nt(dpc, woc_ref[...]).astype(BF16)
        dyg_ref[...] = _dot_nt(dpg, wog_ref[...]).astype(BF16)

    row = pl.BlockSpec((tm, d), lambda i: (i, 0))
    row2 = pl.BlockSpec((tm, 2 * d), lambda i: (i, 0))
    full = pl.BlockSpec((d, d), lambda i: (0, 0))
    act = jax.ShapeDtypeStruct((dm.T, d), BF16)
    return pl.pallas_call(
        body, name="merge_bwd", grid=(dm.T // tm,),
        in_specs=[row, row2, row, row, full, full, full],
        out_specs=[row, row, row2, row, row],
        out_shape=[act, act, jax.ShapeDtypeStruct((dm.T, 2 * d), BF16), act, act],
        compiler_params=_cp(1),
    )(d_out, proj_c, p_conv, p_gla, w_out, w_oc, w_og)


def _prenorm_bwd(du, dy, x, metapad, g_pre, dm):
    d, tm = dm.D, dm.TM
    rows, n_tiles, first_row = _token_tiles(dm)

    def body(du_hbm, dy_hbm, x_hbm, mp_ref, g_ref, gx_hbm, dmeta_ref, gg_ref, dubuf, dybuf, xbuf, gbuf, mbuf,
             sem_in, sem_out, sem_meta):
        def norm_bwd(h, du, dy):
            rstd = lax.rsqrt(jnp.mean(h * h, axis=-1, keepdims=True) + EPS)
            hhat = h * rstd
            dug = du * g_ref[...]
            gg_ref[0:1, :] += jnp.sum(du * hhat, axis=0, keepdims=True)
            return dy + rstd * (dug - hhat * jnp.mean(dug * hhat, axis=-1, keepdims=True))

        def loads(t, slot):
            return [pltpu.make_async_copy(du_hbm.at[pl.ds(first_row(t), rows), :], dubuf.at[slot], sem_in.at[0, slot]),
                    pltpu.make_async_copy(dy_hbm.at[pl.ds(first_row(t), rows), :], dybuf.at[slot], sem_in.at[1, slot]),
                    pltpu.make_async_copy(x_hbm.at[pl.ds(t * rows, rows), :], xbuf.at[slot], sem_in.at[2, slot])]

        def stores(t, slot):
            return [pltpu.make_async_copy(gbuf.at[slot], gx_hbm.at[pl.ds(t * rows, rows), :], sem_out.at[slot])]

        def compute(t, slot):
            gbuf[slot] = norm_bwd(xbuf[slot], dubuf[slot].astype(F32), dybuf[slot].astype(F32))

        gg_ref[...] = jnp.zeros_like(gg_ref)
        meta = [pltpu.make_async_copy(du_hbm.at[pl.ds(b * dm.LP, tm), :], mbuf.at[b], sem_meta.at[b]) for b in range(dm.Bl)]
        for cp in meta:
            cp.start()
        _stream_tiles(n_tiles, loads, stores, compute)
        for b, cp in enumerate(meta):
            cp.wait()
            dmeta_ref[b] = norm_bwd(mp_ref[...], mbuf[b].astype(F32), 0.0)

    any_spec, vmem = pl.BlockSpec(memory_space=pl.ANY), pl.BlockSpec(memory_space=pltpu.VMEM)
    grad_x, d_meta, gg = pl.pallas_call(
        body, name="prenorm_bwd", in_specs=[any_spec, any_spec, any_spec, vmem, vmem], out_specs=[any_spec, vmem, vmem],
        out_shape=[jax.ShapeDtypeStruct((dm.Bl * dm.S, d), F32), jax.ShapeDtypeStruct((dm.Bl, tm, d), F32),
                   jax.ShapeDtypeStruct((8, d), F32)],
        scratch_shapes=[pltpu.VMEM((2, rows, d), BF16), pltpu.VMEM((2, rows, d), BF16), pltpu.VMEM((2, rows, d), F32),
                        pltpu.VMEM((2, rows, d), F32), pltpu.VMEM((dm.Bl, tm, d), BF16),
                        pltpu.SemaphoreType.DMA((3, 2)), pltpu.SemaphoreType.DMA((2,)), pltpu.SemaphoreType.DMA((dm.Bl,))],
        compiler_params=pltpu.CompilerParams(vmem_limit_bytes=VMEM_LIMIT_BYTES),
    )(du, dy, x.reshape(dm.Bl * dm.S, d), metapad, g_pre)
    return grad_x.reshape(dm.Bl, dm.S, d), d_meta, gg


def _adamw(partials, w, m, v, name, by_columns=False):
    r, c = w.shape
    n_parts = partials.shape[0]
    tr, tc = (r, _pick(c, 128, 128)) if by_columns else (_pick(r, 256, 16), c)

    def body(p_ref, w_ref, m_ref, v_ref, g_ref, d_ref, nm_ref, nv_ref):
        g = p_ref[0].astype(F32)
        for j in range(1, n_parts):
            g = g + p_ref[j].astype(F32)
        g_ref[...] = g
        d_ref[...], nm_ref[...], nv_ref[...] = _adam_step(g, w_ref[...], m_ref[...], v_ref[...])

    at = (lambda i: (0, i)) if by_columns else (lambda i: (i, 0))
    tile = pl.BlockSpec((tr, tc), at)
    out = jax.ShapeDtypeStruct((r, c), F32)
    return pl.pallas_call(
        body, name=name, grid=(c // tc if by_columns else r // tr,),
        in_specs=[pl.BlockSpec((n_parts, tr, tc), lambda i: (0,) + at(i)), tile, tile, tile],
        out_specs=[tile, tile, tile, tile], out_shape=[out, out, out, out], compiler_params=_cp(1),
    )(partials, w, m, v)


def _adam_step(g, w, m, v):
    m2 = ADAM_B1 * m + (1.0 - ADAM_B1) * g
    v2 = ADAM_B2 * v + (1.0 - ADAM_B2) * (g * g)
    m_hat = m2 / (1.0 - ADAM_B1 ** ADAM_STEP)
    v_hat = v2 / (1.0 - ADAM_B2 ** ADAM_STEP)
    return -ADAM_LR * (m_hat / (jnp.sqrt(v_hat) + ADAM_EPS) + ADAM_WD * w), m2, v2


def _adamw_small(items, name):
    n = len(items)

    def body(*refs):
        ins, outs = refs[:4 * n], refs[4 * n:]
        for i in range(n):
            p_ref, w_ref, m_ref, v_ref = ins[4 * i:4 * i + 4]
            g = p_ref[0]
            for j in range(1, p_ref.shape[0]):
                g = g + p_ref[j]
            delta, m2, v2 = _adam_step(g, w_ref[...], m_ref[...], v_ref[...])
            for o_ref, val in zip(outs[4 * i:4 * i + 4], (g, delta, m2, v2)):
                o_ref[...] = val

    vmem = pl.BlockSpec(memory_space=pltpu.VMEM)
    res = pl.pallas_call(
        body, name=name, in_specs=[vmem] * (4 * n), out_specs=[vmem] * (4 * n),
        out_shape=[jax.ShapeDtypeStruct(w.shape, F32) for _, w, _, _ in items for _ in range(4)],
    )(*[a for item in items for a in item])
    return [res[4 * i:4 * i + 4] for i in range(n)]


def _pack_rows(wt, dm):
    d, dk, hk, hv, cw, nj = dm.D, dm.DK, dm.HK, dm.HV, dm.CW, dm.NJ
    a = wt[:4 * d].reshape(4, nj, cw, d).transpose(1, 0, 2, 3).reshape(4 * d, d)
    b = jnp.concatenate([wt[4 * d:4 * d + dk].reshape(HEADS, hk, d), wt[4 * d + dk:5 * d].reshape(HEADS, hk, d),
                         wt[5 * d:6 * d].reshape(HEADS, hv, d), wt[6 * d:7 * d].reshape(HEADS, hv, d)],
                        axis=1).reshape(3 * d, d)
    c = wt[7 * d + 2 * RANK:]
    lr = jnp.pad(wt[7 * d:7 * d + 2 * RANK], ((0, LR_LANES - 2 * RANK), (0, 0)))
    return a, b, c, lr


def _unpack_rows(a, b, c, lr, dm):
    d, hk, hv, cw, nj, hw = dm.D, dm.HK, dm.HV, dm.CW, dm.NJ, dm.HW
    conv = a.reshape(nj, 4, cw, d).transpose(1, 0, 2, 3).reshape(4 * d, d)
    heads = b.reshape(HEADS, hw, d)
    q = heads[:, :hk].reshape(HEADS * hk, d)
    k = heads[:, hk:2 * hk].reshape(HEADS * hk, d)
    v = heads[:, 2 * hk:2 * hk + hv].reshape(HEADS * hv, d)
    r = heads[:, 2 * hk + hv:].reshape(HEADS * hv, d)
    return jnp.concatenate([conv, q, k, v, r, lr[:2 * RANK], c], axis=0)


def _column_shards(g, shard_shape):
    r, c = g.shape
    return g.reshape(r, N_DEV, c // N_DEV).transpose(1, 0, 2).reshape((N_DEV,) + tuple(shard_shape))


def _join_column_shards(parts):
    r, c = parts.shape[-2:]
    return parts.reshape(N_DEV, r, c).transpose(1, 0, 2).reshape(r, N_DEV * c)


def _local_step(x, target, meta, g_pre, u, wt_in, conv_w, wg_f, bg_f, wg_b, bg_b, gla_g, out_weights, g_post,
                on_matrix_grads=None):
    bl, s, d = x.shape
    dm = _Dims(bl, s, d)
    metapad = jnp.concatenate([jnp.zeros((dm.TM - N_META, d), F32), meta], axis=0)
    wta, wtb, wtc, wtlr = _pack_rows(wt_in, dm)
    wgp_f = jnp.pad(wg_f, ((0, LR_LANES - RANK), (0, 0))).astype(BF16)
    wgp_b = jnp.pad(wg_b, ((RANK, LR_LANES - 2 * RANK), (0, 0))).astype(BF16)

    u = _prenorm_meta(u, metapad, g_pre, dm)
    proj_a, proj_b, proj_c, lr = _inproj(u, [wta, wtb, wtc, wtlr], dm)
    y_conv = _conv_fwd(proj_a, conv_w, dm)
    o_all, y_gla, states, decays, gate_slopes = _gla_fwd(proj_b, lr, wgp_f, bg_f, wgp_b, bg_b, gla_g, dm)
    w_oc, w_og, w_out = out_weights(y_conv) if callable(out_weights) else out_weights
    p_conv, p_gla, merged = _out_merge(y_conv, y_gla, proj_c, w_oc, w_og, dm)
    d_out, dy, stats = _final_fwd(merged, w_out, x, target, g_post, dm)
    loss = 0.5 / d * jnp.sum(stats[1])

    d_pc, d_pg, d_c, dy_conv, dy_gla = _merge_bwd(d_out, proj_c, p_conv, p_gla, w_out, w_oc, w_og, dm)
    g_out = _matmul_tn(merged, d_out, BF16, "grad_w_out")
    g_oc = _matmul_tn(y_conv, d_pc, BF16, "grad_w_out_conv")
    g_og = _matmul_tn(y_gla, d_pg, BF16, "grad_w_out_gla")
    if on_matrix_grads is not None:
        conv_w = conv_w + on_matrix_grads(dict(w_out_conv=g_oc, w_out_gla=g_og, w_merge_out=g_out))
    d_a, g_conv = _conv_bwd(proj_a, dy_conv, conv_w, dm)
    d_b, d_lr, gwp_f, gbp_f, gwp_b, gbp_b, g_gla = _gla_bwd(proj_b, lr, o_all, dy_gla, states, decays, gate_slopes, wgp_f, wgp_b, gla_g, dm)
    g_in = _unpack_rows(_matmul_tn(d_a, u, BF16, "grad_w_in_conv"), _matmul_tn(d_b, u, BF16, "grad_w_in_gla"),
                        _matmul_tn(d_c, u, BF16, "grad_w_in_merge"), _matmul_tn(d_lr, u, BF16, "grad_w_in_gate"), dm)
    if on_matrix_grads is not None:
        wtlr = wtlr + on_matrix_grads(dict(w_in=g_in)).astype(BF16)
    du = _grad_u([d_a, d_b, d_c, d_lr], [wta, wtb, wtc, wtlr], dm)
    grad_x, d_meta, g_pre_rows = _prenorm_bwd(du, dy, x, metapad, g_pre, dm)

    grads = dict(
        meta_tokens=jnp.sum(d_meta[:, dm.TM - N_META:, :], axis=0), norm_pre=g_pre_rows[0:1], w_in=g_in,
        conv_w=g_conv[0:3], w_gate_fwd=jnp.sum(gwp_f, axis=0)[:RANK], b_gate_fwd=jnp.sum(gbp_f, axis=0)[0:1],
        w_gate_bwd=jnp.sum(gwp_b, axis=0)[RANK:2 * RANK], b_gate_bwd=jnp.sum(gbp_b, axis=0)[0:1],
        gla_norm=g_gla[0:1], w_out_conv=g_oc, w_out_gla=g_og, w_merge_out=g_out, norm_post=stats[0:1])
    return loss, grad_x, grads


MATRICES = ("w_out_conv", "w_out_gla", "w_merge_out")
SMALL_SHARDED = ("meta_tokens", "conv_w", "w_gate_fwd", "w_gate_bwd")
REPLICATED = ("norm_pre", "b_gate_fwd", "b_gate_bwd", "gla_norm", "norm_post")
NAMES = ("meta_tokens", "norm_pre", "w_in", "conv_w", "w_gate_fwd", "b_gate_fwd", "w_gate_bwd", "b_gate_bwd", "gla_norm",
         "w_out_conv", "w_out_gla", "w_merge_out", "norm_post")


def kernel(x, meta_tokens, norm_pre, w_in, conv_w, w_gate_fwd, b_gate_fwd, w_gate_bwd, b_gate_bwd, gla_norm, w_out_conv, w_out_gla, w_merge_out, norm_post, loss_target, m_meta_tokens, m_norm_pre, m_w_in, m_conv_w, m_w_gate_fwd, m_b_gate_fwd, m_w_gate_bwd, m_b_gate_bwd, m_gla_norm, m_w_out_conv, m_w_out_gla, m_w_merge_out, m_norm_post, v_meta_tokens, v_norm_pre, v_w_in, v_conv_w, v_w_gate_fwd, v_b_gate_fwd, v_w_gate_bwd, v_b_gate_bwd, v_gla_norm, v_w_out_conv, v_w_out_gla, v_w_merge_out, v_norm_post):
    w = dict(meta_tokens=meta_tokens, norm_pre=norm_pre, w_in=w_in[0], conv_w=conv_w, w_gate_fwd=w_gate_fwd,
             b_gate_fwd=b_gate_fwd, w_gate_bwd=w_gate_bwd, b_gate_bwd=b_gate_bwd, gla_norm=gla_norm,
             w_out_conv=w_out_conv[0], w_out_gla=w_out_gla[0], w_merge_out=w_merge_out[0], norm_post=norm_post)
    m = dict(meta_tokens=m_meta_tokens, norm_pre=m_norm_pre, w_in=m_w_in[0], conv_w=m_conv_w, w_gate_fwd=m_w_gate_fwd,
             b_gate_fwd=m_b_gate_fwd, w_gate_bwd=m_w_gate_bwd, b_gate_bwd=m_b_gate_bwd, gla_norm=m_gla_norm,
             w_out_conv=m_w_out_conv[0], w_out_gla=m_w_out_gla[0], w_merge_out=m_w_merge_out[0], norm_post=m_norm_post)
    v = dict(meta_tokens=v_meta_tokens, norm_pre=v_norm_pre, w_in=v_w_in[0], conv_w=v_conv_w, w_gate_fwd=v_w_gate_fwd,
             b_gate_fwd=v_b_gate_fwd, w_gate_bwd=v_w_gate_bwd, b_gate_bwd=v_b_gate_bwd, gla_norm=v_gla_norm,
             w_out_conv=v_w_out_conv[0], w_out_gla=v_w_out_gla[0], w_merge_out=v_w_merge_out[0], norm_post=v_norm_post)
    d = x.shape[-1]

    dm = _Dims(*x.shape)
    wt_all, *small_all, u = _gather_two_level([w["w_in"].T.astype(BF16)] + [w[n] for n in SMALL_SHARDED], "gather_weights",
                                              _prenorm_tokens_side(x, norm_pre, dm))
    _, late_weights = _exchange_start([w[n].astype(BF16) for n in MATRICES], [], small_all[0], "gather_out_weights_start")
    wt_in = wt_all.reshape(-1, d)
    small = {n: _join_column_shards(p) for n, p in zip(SMALL_SHARDED, small_all)}

    def out_weights(after):
        return tuple(a.reshape(-1, d) for a in _exchange_wait(late_weights, after, "gather_out_weights_wait"))

    pending = []

    def on_matrix_grads(g):
        token, state = _exchange_start([], [t.astype(BF16).reshape(N_DEV, -1, d) for t in g.values()], None,
                                       "exchange_grads_start_" + "_".join(g))
        pending.append((tuple(g), state))
        return token

    loss, grad_x, grads = _local_step(
        x, loss_target, small["meta_tokens"], norm_pre, u, wt_in, small["conv_w"], small["w_gate_fwd"], b_gate_fwd,
        small["w_gate_bwd"], b_gate_bwd, gla_norm, out_weights, norm_post, on_matrix_grads)
    loss = lax.psum(loss, ("x", "y", "c"))
    received = {}
    for names, state in pending:
        received.update(zip(names, _exchange_wait(state, grad_x, "exchange_grads_wait_" + "_".join(names))))

    small_recv = _exchange([grads[n] for n in REPLICATED], [_column_shards(grads[n], w[n].shape) for n in SMALL_SHARDED],
                           "exchange_small_grads")

    results = {"w_in": [r.T[None] for r in _adamw(received["w_in"], w["w_in"].T, m["w_in"].T, v["w_in"].T, "adamw_w_in", by_columns=True)]}
    for n in MATRICES:
        results[n] = [r[None] for r in _adamw(received[n], w[n], m[n], v[n], "adamw_" + n)]
    small_names = REPLICATED + SMALL_SHARDED
    results.update(zip(small_names, _adamw_small([(p, w[n], m[n], v[n]) for n, p in zip(small_names, small_recv)], "adamw_small")))
    return (loss, grad_x, *[results[n][i] for i in range(4) for n in NAMES])
```

```python
import jax
import jax.numpy as jnp
from jax import lax
from jax.experimental import pallas as pl
from jax.experimental.pallas import tpu as pltpu

F32 = jnp.float32
BF16 = jnp.bfloat16
MESH = pl.DeviceIdType.MESH

N_META = 16
CHUNK = 64
CHUNK_SHIFT = 6
HEADS = 4
RANK = 16
LR_LANES = 128
PAD_ROWS = CHUNK - N_META
EPS = 1e-6
GATE_NORMALIZER = 16.0
N_DEV = 8
ADAM_LR, ADAM_B1, ADAM_B2, ADAM_EPS, ADAM_WD, ADAM_STEP = 0.001, 0.9, 0.999, 1e-08, 0.01, 10
VMEM_LIMIT_BYTES = 56 * 1024 * 1024


class _Dims:
    def __init__(self, bl, s, d):
        self.Bl, self.S, self.D = bl, s, d
        self.TM = CHUNK
        self.LP = self.TM + s
        self.T = bl * self.LP
        self.TPS = self.LP // self.TM
        self.NC = self.LP // CHUNK
        self.C0 = (self.TM - CHUNK) // CHUNK
        self.DK, self.DV = d // 2, d
        self.HK, self.HV = self.DK // HEADS, self.DV // HEADS
        self.HW = 2 * self.HK + 2 * self.HV
        self.CW = 256 if d % 256 == 0 and d > 256 else d // 4
        self.NJ = d // self.CW


def _pick(n, target, mult):
    t = min(n, target)
    while t >= mult:
        if n % t == 0 and t % mult == 0:
            return t
        t -= mult
    return n


def _cp(n_axes):
    return pltpu.CompilerParams(dimension_semantics=("arbitrary",) * n_axes, vmem_limit_bytes=VMEM_LIMIT_BYTES)


def _sigmoid(x):
    return 1.0 / (1.0 + jnp.exp(-x))


def _dot(a, b):
    return jnp.dot(a, b, preferred_element_type=F32)


def _dot_nt(a, b):
    return lax.dot_general(a, b, (((1,), (1,)), ((), ())), preferred_element_type=F32)


def _dot_tn(a, b):
    return lax.dot_general(a, b, (((0,), (0,)), ((), ())), preferred_element_type=F32)


def _dot_exact01(m01, x):
    hi = x.astype(BF16)
    lo = (x - hi.astype(F32)).astype(BF16)
    return _dot(m01, hi) + _dot(m01, lo)


def _exchange(gathers, scatters, name):
    arrays = list(gathers) + list(scatters)
    n, ng = len(arrays), len(gathers)

    def body(*refs):
        ins, outs = refs[:n], refs[n:2 * n]
        send_sems, recv_sems, local_sems = refs[2 * n:]
        x, y, c = lax.axis_index("x"), lax.axis_index("y"), lax.axis_index("c")
        me = 4 * x + 2 * y + c
        started = []
        for t in range(n):
            src, dst = ins[t], outs[t]
            own = pltpu.make_async_copy(src if t < ng else src.at[me], dst.at[me], local_sems.at[t])
            own.start()
            started.append(own)
            for k, pos, peer in _peers(x, y, c):
                cp = pltpu.make_async_remote_copy(
                    src_ref=src if t < ng else src.at[peer], dst_ref=dst.at[me],
                    send_sem=send_sems.at[t * (N_DEV - 1) + k - 1], recv_sem=recv_sems.at[t * (N_DEV - 1) + k - 1],
                    device_id=pos, device_id_type=MESH)
                cp.start()
                started.append(cp)
        for cp in started:
            cp.wait()

    out_shape = [jax.ShapeDtypeStruct((N_DEV,) + a.shape if t < ng else a.shape, a.dtype) for t, a in enumerate(arrays)]
    any_spec = pl.BlockSpec(memory_space=pl.ANY)
    return pl.pallas_call(
        body, name=name, out_shape=out_shape, in_specs=[any_spec] * n, out_specs=[any_spec] * n,
        scratch_shapes=[pltpu.SemaphoreType.DMA((n * (N_DEV - 1),)), pltpu.SemaphoreType.DMA((n * (N_DEV - 1),)),
                        pltpu.SemaphoreType.DMA((n,))],
        compiler_params=pltpu.CompilerParams(has_side_effects=True),
    )(*arrays)


def _gather_two_level(arrays, name, side=None):
    n = len(arrays)
    per = N_DEV - 1
    work, side_in, side_in_specs, side_out, side_out_specs, side_scratch = side or (None, [], [], [], [], [])
    n_in, n_out = len(side_in), len(side_out)

    def body(*refs):
        ins, outs = refs[:n], refs[n + n_in:2 * n + n_in]
        send_sems, recv_sems, local_sems = refs[2 * n + n_in + n_out:2 * n + n_in + n_out + 3]
        x, y, c = lax.axis_index("x"), lax.axis_index("y"), lax.axis_index("c")
        sibling = (x, y, 1 - c)
        chips = [(1 - x, y), (x, 1 - y), (1 - x, 1 - y)]
        index = lambda px, py, pc: 4 * px + 2 * py + pc

        def copy(t, k, block, to, from_input=False):
            slab = outs[t].at[index(*block)]
            return pltpu.make_async_remote_copy(
                src_ref=ins[t] if from_input else slab, dst_ref=slab, send_sem=send_sems.at[t * per + k],
                recv_sem=recv_sems.at[t * per + k], device_id=to, device_id_type=MESH)

        own, sent = [], []
        for t in range(n):
            own.append(pltpu.make_async_copy(ins[t], outs[t].at[index(x, y, c)], local_sems.at[t]))
            own[-1].start()
            first = [copy(t, 0, (x, y, c), sibling, True)]
            first += [copy(t, 1 + j, (x, y, c), (*chip, c), True) for j, chip in enumerate(chips)]
            for cp in first:
                cp.start()
            sent += first
        if work is not None:
            work(refs[n:n + n_in], refs[2 * n + n_in:2 * n + n_in + n_out], refs[2 * n + n_in + n_out + 3:])
        for t in range(n):
            for j, chip in enumerate(chips):
                copy(t, 1 + j, (*chip, c), (x, y, c)).wait_recv()
                sent.append(copy(t, 4 + j, (*chip, c), sibling))
                sent[-1].start()
        for t in range(n):
            copy(t, 0, sibling, (x, y, c)).wait_recv()
            for j, chip in enumerate(chips):
                copy(t, 4 + j, (*chip, 1 - c), (x, y, c)).wait_recv()
        for cp in sent:
            cp.wait_send()
        for cp in own:
            cp.wait()

    out_shape = [jax.ShapeDtypeStruct((N_DEV,) + a.shape, a.dtype) for a in arrays]
    any_spec = pl.BlockSpec(memory_space=pl.ANY)
    return pl.pallas_call(
        body, name=name, out_shape=out_shape + list(side_out), in_specs=[any_spec] * n + list(side_in_specs),
        out_specs=[any_spec] * n + list(side_out_specs),
        scratch_shapes=[pltpu.SemaphoreType.DMA((n * per,)), pltpu.SemaphoreType.DMA((n * per,)),
                        pltpu.SemaphoreType.DMA((n,))] + list(side_scratch),
        compiler_params=pltpu.CompilerParams(has_side_effects=True, vmem_limit_bytes=VMEM_LIMIT_BYTES),
    )(*arrays, *side_in)


def _peers(x, y, c):
    out = []
    for k in range(1, N_DEV):
        px = 1 - x if (k >> 2) & 1 else x
        py = 1 - y if (k >> 1) & 1 else y
        pc = 1 - c if k & 1 else c
        out.append((k, (px, py, pc), 4 * px + 2 * py + pc))
    return out


def _exchange_start(gathers, scatters, after, name):
    arrays = list(gathers) + list(scatters)
    n, ng = len(arrays), len(gathers)
    hbm = pl.BlockSpec(memory_space=pltpu.HBM)
    sem = pl.BlockSpec(memory_space=pltpu.SEMAPHORE)

    extra = [] if after is None else [after]
    ne = len(extra)

    def body(*refs):
        ins, lands = refs[:n], refs[n:2 * n]
        send_sems, recv_sems = refs[2 * n + ne], refs[2 * n + ne + 1]
        token = refs[4 * n + ne + 2]
        x, y, c = lax.axis_index("x"), lax.axis_index("y"), lax.axis_index("c")
        me = 4 * x + 2 * y + c
        for t in range(n):
            for k, pos, peer in _peers(x, y, c):
                pltpu.make_async_remote_copy(
                    src_ref=ins[t] if t < ng else ins[t].at[peer], dst_ref=lands[t].at[me],
                    send_sem=send_sems.at[t * (N_DEV - 1) + k - 1], recv_sem=recv_sems.at[t * (N_DEV - 1) + k - 1],
                    device_id=pos, device_id_type=MESH).start()
        token[...] = jnp.zeros_like(token)

    me = 4 * lax.axis_index("x") + 2 * lax.axis_index("y") + lax.axis_index("c")
    lands = [lax.dynamic_update_index_in_dim(lax.empty((N_DEV,) + a.shape if t < ng else a.shape, a.dtype),
                                             a if t < ng else lax.dynamic_index_in_dim(a, me, 0, keepdims=False), me, 0)
             for t, a in enumerate(arrays)]
    operands = [pltpu.with_memory_space_constraint(a, pltpu.HBM) for a in arrays + lands]
    sems = pltpu.SemaphoreType.DMA((n * (N_DEV - 1),))
    res = pl.pallas_call(
        body, name=name,
        out_shape=(sems, sems, *[pltpu.HBM(a.shape, a.dtype) for a in arrays + lands], jax.ShapeDtypeStruct((8, 128), F32)),
        in_specs=[hbm] * (2 * n) + [pl.BlockSpec(memory_space=pl.ANY)] * ne,
        out_specs=(sem, sem, *[hbm] * (2 * n), pl.BlockSpec(memory_space=pltpu.VMEM)),
        input_output_aliases={i: 2 + i for i in range(2 * n)},
        compiler_params=pltpu.CompilerParams(has_side_effects=pltpu.SideEffectType.DATAFLOW_SIDE_EFFECTING),
    )(*operands, *extra)
    return res[-1][0, 0], (ng, res[0], res[1], list(res[2:2 + n]), list(res[2 + n:2 + 2 * n]))


def _exchange_wait(state, after, name):
    ng, send_sems, recv_sems, sent, lands = state
    n = len(sent)
    hbm = pl.BlockSpec(memory_space=pltpu.HBM)
    sem = pl.BlockSpec(memory_space=pltpu.SEMAPHORE)

    def body(*refs):
        ins, land_refs = refs[:n], refs[n:2 * n]
        send_ref, recv_ref = refs[2 * n], refs[2 * n + 1]
        x, y, c = lax.axis_index("x"), lax.axis_index("y"), lax.axis_index("c")
        me = 4 * x + 2 * y + c
        for t in range(n):
            for k, pos, peer in _peers(x, y, c):
                cp = pltpu.make_async_remote_copy(
                    src_ref=ins[t] if t < ng else ins[t].at[peer], dst_ref=land_refs[t].at[me],
                    send_sem=send_ref.at[t * (N_DEV - 1) + k - 1], recv_sem=recv_ref.at[t * (N_DEV - 1) + k - 1],
                    device_id=pos, device_id_type=MESH)
                cp.wait_send()
                cp.wait_recv()

    res = pl.pallas_call(
        body, name=name, out_shape=tuple(pltpu.HBM(a.shape, a.dtype) for a in sent + lands),
        in_specs=[hbm] * (2 * n) + [sem, sem, pl.BlockSpec(memory_space=pl.ANY)], out_specs=tuple([hbm] * (2 * n)),
        input_output_aliases={i: i for i in range(2 * n)},
        compiler_params=pltpu.CompilerParams(has_side_effects=pltpu.SideEffectType.DATAFLOW_SIDE_EFFECTING),
    )(*sent, *lands, send_sems, recv_sems, after)
    return list(res[n:])


def _rms_scaled(h, g):
    return (h * lax.rsqrt(jnp.mean(h * h, axis=-1, keepdims=True) + EPS) * g).astype(BF16)


def _prenorm_tokens_side(x, g_pre, dm):
    bl, s, d = x.shape
    rows = _pick(s, 512, 16)
    tiles = [(b, j) for b in range(bl) for j in range(s // rows)]

    def work(ins, outs, scratch):
        (x_ref, g_ref), (u_ref,), (xbuf, ubuf, sem_in, sem_out) = ins, outs, scratch

        def load(t, slot):
            b, j = tiles[t]
            return pltpu.make_async_copy(x_ref.at[b, pl.ds(j * rows, rows), :], xbuf.at[slot], sem_in.at[slot])

        def store(t, slot):
            b, j = tiles[t]
            return pltpu.make_async_copy(ubuf.at[slot], u_ref.at[pl.ds(b * dm.LP + dm.TM + j * rows, rows), :], sem_out.at[slot])

        load(0, 0).start()
        for t in range(len(tiles)):
            slot = t % 2
            if t + 1 < len(tiles):
                load(t + 1, 1 - slot).start()
            load(t, slot).wait()
            if t >= 2:
                store(t - 2, slot).wait()
            ubuf[slot] = _rms_scaled(xbuf[slot], g_ref[...])
            store(t, slot).start()
        for t in range(max(len(tiles) - 2, 0), len(tiles)):
            store(t, t % 2).wait()

    any_spec = pl.BlockSpec(memory_space=pl.ANY)
    return (work, [x, g_pre], [any_spec, pl.BlockSpec(memory_space=pltpu.VMEM)],
            [jax.ShapeDtypeStruct((dm.T, d), BF16)], [any_spec],
            [pltpu.VMEM((2, rows, d), F32), pltpu.VMEM((2, rows, d), BF16), pltpu.SemaphoreType.DMA((2,)),
             pltpu.SemaphoreType.DMA((2,))])


def _prenorm_meta(u, metapad, g_pre, dm):
    tm, tps, d = dm.TM, dm.TPS, dm.D

    def body(u_in, mp_ref, g_ref, u_ref):
        u_ref[...] = _rms_scaled(mp_ref[...], g_ref[...])

    return pl.pallas_call(
        body, name="prenorm_meta", grid=(dm.Bl,),
        in_specs=[pl.BlockSpec(memory_space=pl.ANY), pl.BlockSpec((tm, d), lambda i: (0, 0)),
                  pl.BlockSpec((1, d), lambda i: (0, 0))],
        out_specs=pl.BlockSpec((tm, d), lambda i: (i * tps, 0)),
        out_shape=jax.ShapeDtypeStruct((dm.T, d), BF16), input_output_aliases={0: 0}, compiler_params=_cp(1),
    )(u, metapad, g_pre)


def _matmul_tn(a, b, out_dtype, name, tt=2304, tn=1024, tk=1024):
    t, k = a.shape
    n = b.shape[1]
    tt, tn, tk = _pick(t, tt, 16), _pick(n, tn, 128), _pick(k, tk, 128)
    nt = t // tt

    def body(a_ref, b_ref, o_ref, acc):
        p = _dot_tn(a_ref[...].astype(BF16), b_ref[...].astype(BF16))
        i = pl.program_id(2)

        @pl.when(i == 0)
        def _():
            acc[...] = p

        @pl.when(i > 0)
        def _():
            acc[...] += p

        @pl.when(i == nt - 1)
        def _():
            o_ref[...] = acc[...].astype(out_dtype)

    return pl.pallas_call(
        body, name=name, grid=(k // tk, n // tn, nt),
        in_specs=[pl.BlockSpec((tt, tk), lambda kk, j, i: (i, kk)), pl.BlockSpec((tt, tn), lambda kk, j, i: (i, j))],
        out_specs=pl.BlockSpec((tk, tn), lambda kk, j, i: (kk, j)),
        out_shape=jax.ShapeDtypeStruct((k, n), out_dtype), scratch_shapes=[pltpu.VMEM((tk, tn), F32)],
        compiler_params=_cp(3),
    )(a, b)


def _load_resident(hbm_refs, vmem_refs, sems):
    @pl.when(pl.program_id(0) == 0)
    def _():
        copies = [pltpu.make_async_copy(h, v, sems.at[i]) for i, (h, v) in enumerate(zip(hbm_refs, vmem_refs))]
        for cp in copies:
            cp.start()
        for cp in copies:
            cp.wait()


def _inproj(u, wts, dm):
    t, d = u.shape
    tm = _pick(t, 512, 16)
    np_ = len(wts)
    cn = 1024

    def body(*refs):
        u_ref, w_hbm, outs = refs[0], refs[1:1 + np_], refs[1 + np_:1 + 2 * np_]
        w_vmem, sems = refs[1 + 2 * np_:1 + 3 * np_], refs[1 + 3 * np_]
        _load_resident(w_hbm, w_vmem, sems)
        ut = u_ref[...]
        for w, o_ref in zip(w_vmem, outs):
            n = w.shape[0]
            step = cn if n % cn == 0 else n
            for j in range(0, n, step):
                o_ref[:, j:j + step] = _dot_nt(ut, w[j:j + step, :]).astype(BF16)

    return pl.pallas_call(
        body, name="inproj", grid=(t // tm,),
        in_specs=[pl.BlockSpec((tm, d), lambda i: (i, 0))] + [pl.BlockSpec(memory_space=pl.ANY)] * np_,
        out_specs=[pl.BlockSpec((tm, w.shape[0]), lambda i: (i, 0)) for w in wts],
        out_shape=[jax.ShapeDtypeStruct((t, w.shape[0]), BF16) for w in wts],
        scratch_shapes=[pltpu.VMEM(w.shape, BF16) for w in wts] + [pltpu.SemaphoreType.DMA((np_,))],
        compiler_params=_cp(1),
    )(u, *wts)


def _grad_u(d_parts, wts, dm):
    t = d_parts[0].shape[0]
    d = wts[0].shape[1]
    tm = _pick(t, 512, 16)
    np_ = len(wts)

    def body(*refs):
        d_refs, w_hbm, o_ref = refs[:np_], refs[np_:2 * np_], refs[2 * np_]
        w_vmem, sems, acc = refs[2 * np_ + 1:3 * np_ + 1], refs[3 * np_ + 1], refs[3 * np_ + 2]
        _load_resident(w_hbm, w_vmem, sems)
        acc[...] = _dot(d_refs[0][...].astype(BF16), w_vmem[0][...])
        for a_ref, w in zip(d_refs[1:], w_vmem[1:]):
            acc[...] += _dot(a_ref[...].astype(BF16), w[...])
        o_ref[...] = acc[...].astype(BF16)

    return pl.pallas_call(
        body, name="grad_u", grid=(t // tm,),
        in_specs=[pl.BlockSpec((tm, a.shape[1]), lambda i: (i, 0)) for a in d_parts] + [pl.BlockSpec(memory_space=pl.ANY)] * np_,
        out_specs=pl.BlockSpec((tm, d), lambda i: (i, 0)), out_shape=jax.ShapeDtypeStruct((t, d), BF16),
        scratch_shapes=[pltpu.VMEM(w.shape, BF16) for w in wts] + [pltpu.SemaphoreType.DMA((np_,)), pltpu.VMEM((tm, d), F32)],
        compiler_params=_cp(1),
    )(*d_parts, *wts)


def _conv_rows(dm):
    return _pick(dm.LP, 256, 16)


def _shifted(m, prev_row, next_row, rows):
    row = lax.broadcasted_iota(jnp.int32, m.shape, 0)
    m_prev = jnp.where(row == 0, prev_row, pltpu.roll(m, 1, 0))
    m_next = jnp.where(row == rows - 1, next_row, pltpu.roll(m, rows - 1, 0))
    return m_prev, m_next


def _conv_fwd(proj_a, conv_w, dm):
    lp, cw, rc = dm.LP, dm.CW, _conv_rows(dm)
    nchunk = lp // rc

    def body(p_ref, w_ref, y_ref):
        w0, w1, w2 = w_ref[0:1, :], w_ref[1:2, :], w_ref[2:3, :]

        def chunk(ci, carry):
            r0 = pl.multiple_of(ci * rc, rc)
            blk = p_ref[pl.ds(r0, rc), :].astype(F32)
            cb, cc, cx, cz = (blk[:, i * cw:(i + 1) * cw] for i in range(4))
            m = cc * cx
            rp = pl.multiple_of(jnp.maximum(r0 - 16, 0), 16)
            rn = pl.multiple_of(jnp.minimum(r0 + rc, lp - 16), 16)
            pv = p_ref[pl.ds(rp, 16), cw:3 * cw].astype(F32)
            nx = p_ref[pl.ds(rn, 16), cw:3 * cw].astype(F32)
            prev_row = jnp.where(ci > 0, pv[15:16, :cw] * pv[15:16, cw:], 0.0)
            next_row = jnp.where(ci < nchunk - 1, nx[0:1, :cw] * nx[0:1, cw:], 0.0)
            m_prev, m_next = _shifted(m, prev_row, next_row, rc)
            s = w0 * m_prev + w1 * m + w2 * m_next
            y_ref[pl.ds(r0, rc), :] = (cb * s * (cz * _sigmoid(cz))).astype(BF16)
            return carry

        lax.fori_loop(0, nchunk, chunk, 0)

    return pl.pallas_call(
        body, name="conv_fwd", grid=(dm.Bl, dm.NJ),
        in_specs=[pl.BlockSpec((lp, 4 * cw), lambda s, j: (s, j)), pl.BlockSpec((3, cw), lambda s, j: (0, j))],
        out_specs=pl.BlockSpec((lp, cw), lambda s, j: (s, j)),
        out_shape=jax.ShapeDtypeStruct((dm.T, dm.D), BF16), compiler_params=_cp(2),
    )(proj_a, conv_w)


def _conv_bwd(proj_a, dy_conv, conv_w, dm):
    lp, cw, rc = dm.LP, dm.CW, _conv_rows(dm)
    nchunk = lp // rc

    def body(p_ref, dy_ref, w_ref, d_ref, gw_ref):
        w0, w1, w2 = w_ref[0:1, :], w_ref[1:2, :], w_ref[2:3, :]

        def ds_of(p4, dy):
            cb, cz = p4[:, :cw], p4[:, 3 * cw:]
            return dy * cb * (cz * _sigmoid(cz))

        def chunk(ci, carry):
            g0, g1, g2 = carry
            r0 = pl.multiple_of(ci * rc, rc)
            blk = p_ref[pl.ds(r0, rc), :].astype(F32)
            dy = dy_ref[pl.ds(r0, rc), :].astype(F32)
            cb, cc, cx, cz = (blk[:, i * cw:(i + 1) * cw] for i in range(4))
            rp = pl.multiple_of(jnp.maximum(r0 - 16, 0), 16)
            rn = pl.multiple_of(jnp.minimum(r0 + rc, lp - 16), 16)
            pv = p_ref[pl.ds(rp, 16), :].astype(F32)[15:16]
            nx = p_ref[pl.ds(rn, 16), :].astype(F32)[0:1]
            dpv = dy_ref[pl.ds(rp, 16), :].astype(F32)[15:16]
            dnx = dy_ref[pl.ds(rn, 16), :].astype(F32)[0:1]
            has_prev, has_next = ci > 0, ci < nchunk - 1
            m = cc * cx
            m_prev, m_next = _shifted(m, jnp.where(has_prev, pv[:, cw:2 * cw] * pv[:, 2 * cw:3 * cw], 0.0),
                                      jnp.where(has_next, nx[:, cw:2 * cw] * nx[:, 2 * cw:3 * cw], 0.0), rc)
            s = w0 * m_prev + w1 * m + w2 * m_next
            sg = _sigmoid(cz)
            silu = cz * sg
            ds = dy * cb * silu
            ds_prev, ds_next = _shifted(ds, jnp.where(has_prev, ds_of(pv, dpv), 0.0),
                                        jnp.where(has_next, ds_of(nx, dnx), 0.0), rc)
            dm_ = w0 * ds_next + w1 * ds + w2 * ds_prev
            d_ref[pl.ds(r0, rc), 0:cw] = (dy * s * silu).astype(BF16)
            d_ref[pl.ds(r0, rc), cw:2 * cw] = (dm_ * cx).astype(BF16)
            d_ref[pl.ds(r0, rc), 2 * cw:3 * cw] = (dm_ * cc).astype(BF16)
            d_ref[pl.ds(r0, rc), 3 * cw:4 * cw] = (dy * cb * s * (sg * (1.0 + cz * (1.0 - sg)))).astype(BF16)
            return (g0 + jnp.sum(ds * m_prev, axis=0, keepdims=True), g1 + jnp.sum(ds * m, axis=0, keepdims=True),
                    g2 + jnp.sum(ds * m_next, axis=0, keepdims=True))

        z = jnp.zeros((1, cw), F32)
        g0, g1, g2 = lax.fori_loop(0, nchunk, chunk, (z, z, z))

        @pl.when(pl.program_id(1) == 0)
        def _():
            gw_ref[...] = jnp.zeros_like(gw_ref)

        gw_ref[0:1, :] += g0
        gw_ref[1:2, :] += g1
        gw_ref[2:3, :] += g2

    return pl.pallas_call(
        body, name="conv_bwd", grid=(dm.NJ, dm.Bl),
        in_specs=[pl.BlockSpec((lp, 4 * cw), lambda j, s: (s, j)), pl.BlockSpec((lp, cw), lambda j, s: (s, j)),
                  pl.BlockSpec((3, cw), lambda j, s: (0, j))],
        out_specs=[pl.BlockSpec((lp, 4 * cw), lambda j, s: (s, j)), pl.BlockSpec((8, cw), lambda j, s: (0, j))],
        out_shape=[jax.ShapeDtypeStruct((dm.T, 4 * dm.D), BF16), jax.ShapeDtypeStruct((8, dm.D), F32)],
        compiler_params=_cp(2),
    )(proj_a, dy_conv, conv_w)


def _interleave(gens):
    results = [None] * len(gens)
    live = list(range(len(gens)))
    while live:
        for idx in list(live):
            try:
                next(gens[idx])
            except StopIteration as done:
                results[idx] = done.value
                live.remove(idx)
    return results


def _group_chunks(dm):
    n = dm.NC - dm.C0
    return 3 if n % 3 == 0 else 1


def _group_masks(rows):
    ii = lax.broadcasted_iota(jnp.int32, (rows, rows), 0)
    jj = lax.broadcasted_iota(jnp.int32, (rows, rows), 1)
    same = jnp.right_shift(ii, CHUNK_SHIFT) == jnp.right_shift(jj, CHUNK_SHIFT)
    low, up = same & (jj <= ii), same & (jj >= ii)
    return low, same & (jj > ii), low.astype(BF16), up.astype(BF16)


def _first_row(chunk):
    return chunk * CHUNK if isinstance(chunk, int) else pl.multiple_of(chunk * CHUNK, CHUNK)


def _chunk_totals(b, fwd):
    hk = b.shape[1]
    rows = [b[c * CHUNK + CHUNK - 1:(c + 1) * CHUNK] if fwd else b[c * CHUNK:c * CHUNK + 1]
            for c in range(b.shape[0] // CHUNK)]
    return jnp.concatenate([jnp.broadcast_to(r, (CHUNK, hk)) for r in rows], axis=0)


def _log_gate(lr_rows, w_ref, b_ref, first_group, hk):
    z = _dot(lr_rows, w_ref[...]) + b_ref[...]
    e = jnp.exp(-jnp.abs(z))
    g = (jnp.minimum(z, 0.0) - jnp.log(1.0 + e)) * (1.0 / GATE_NORMALIZER)
    dg_dz = jnp.where(z >= 0.0, e, 1.0) / (1.0 + e) * (1.0 / GATE_NORMALIZER)
    row = lax.broadcasted_iota(jnp.int32, (lr_rows.shape[0], hk), 0)
    pad = first_group & (row < PAD_ROWS)
    return jnp.where(pad, 0.0, g), jnp.where(pad, 0.0, dg_dz)


def _gla_fwd(proj_b, lr, wg_f, bg_f, wg_b, bg_b, gla_g, dm):
    lp, hk, hv, nc, c0, hw = dm.LP, dm.HK, dm.HV, dm.NC, dm.C0, dm.HW
    scale = hk ** -0.5
    gc = _group_chunks(dm)
    gr, ng = gc * CHUNK, (nc - c0) // gc

    def body(p_ref, lr_ref, wf_ref, bf_ref, wb_ref, bb_ref, gg_ref, o_ref, y_ref, st_ref, b_out, gs_out, oacc_f, oacc_b):
        low_incl, up_strict, ones_low, ones_up = _group_masks(gr)
        if c0 > 0:
            zr = c0 * CHUNK
            o_ref[0:zr, :] = jnp.zeros((zr, hv), BF16)
            y_ref[0:zr, :] = jnp.zeros((zr, hv), BF16)
            b_out[:, 0:zr, :] = jnp.zeros((2, zr, hk), F32)
            gs_out[:, 0:zr, :] = jnp.zeros((2, zr, hk), F32)
            st_ref[0, 0, :, 0:c0] = jnp.zeros((2, c0, hv, hk), BF16)

        def decay(gi, fwd):
            w_ref, b_ref = (wf_ref, bf_ref) if fwd else (wb_ref, bb_ref)
            r0 = _first_row(c0 + gi * gc)
            yield
            g, dg_dz = _log_gate(lr_ref[pl.ds(r0, gr), :], w_ref, b_ref, gi == 0, hk)
            gs_out[0 if fwd else 1, pl.ds(r0, gr), :] = dg_dz
            yield
            b = _dot_exact01(ones_low if fwd else ones_up, g)
            b_out[0 if fwd else 1, pl.ds(r0, gr), :] = b
            return b

        def group(gi, st, b, fwd):
            oacc = oacc_f if fwd else oacc_b
            r0 = pl.multiple_of((c0 + gi * gc) * CHUNK, CHUNK)
            blk = p_ref[pl.ds(r0, gr), :]
            q = blk[:, :hk].astype(F32) * scale
            k = blk[:, hk:2 * hk].astype(F32)
            v = blk[:, 2 * hk:2 * hk + hv]
            btot = _chunk_totals(b, fwd)
            qi = (q * jnp.exp(b)).astype(BF16)
            ki = (k * jnp.exp(-b)).astype(BF16)
            kd = (k * jnp.exp(btot - b)).astype(BF16)
            dec = jnp.exp(btot)
            a = _dot_nt(qi, ki)
            yield
            o = _dot(jnp.where(low_incl if fwd else up_strict, a, 0.0).astype(BF16), v)
            chunk_rows = [slice(c * CHUNK, (c + 1) * CHUNK) for c in range(gc)]
            kv = [_dot_tn(v[rows], kd[rows]) for rows in chunk_rows]
            for c in (range(gc) if fwd else reversed(range(gc))):
                yield
                rows = chunk_rows[c]
                st_b = st.astype(BF16)
                st_ref[0, 0, 0 if fwd else 1, c0 + gi * gc + c] = st_b
                oacc[pl.ds(r0 + c * CHUNK, CHUNK), :] = o[rows] + _dot_nt(qi[rows], st_b)
                st = st * dec[c * CHUNK:c * CHUNK + 1] + kv[c]
            return st

        def step(i, carry):
            st_f, st_b, b_f, b_b = carry
            gf, gb = i, ng - 1 - i
            return tuple(_interleave([group(gf, st_f, b_f, True), group(gb, st_b, b_b, False),
                                      decay(jnp.minimum(gf + 1, ng - 1), True), decay(jnp.maximum(gb - 1, 0), False)]))

        zero = jnp.zeros((hv, hk), F32)
        lax.fori_loop(0, ng, step, (zero, zero, *_interleave([decay(0, True), decay(ng - 1, False)])))

        def finish(i, carry):
            r0 = pl.multiple_of((c0 + i * gc) * CHUNK, CHUNK)
            o = oacc_f[pl.ds(r0, gr), :] + oacc_b[pl.ds(r0, gr), :]
            r = p_ref[pl.ds(r0, gr), 2 * hk + hv:].astype(F32)
            on = o * lax.rsqrt(jnp.mean(o * o, axis=-1, keepdims=True) + EPS) * gg_ref[...]
            o_ref[pl.ds(r0, gr), :] = o.astype(BF16)
            y_ref[pl.ds(r0, gr), :] = (on * r * _sigmoid(r)).astype(BF16)
            return carry

        lax.fori_loop(0, ng, finish, 0)

    head = lambda s, h: (s, h)
    wspec = pl.BlockSpec((LR_LANES, hk), lambda s, h: (0, h))
    bspec = pl.BlockSpec((1, hk), lambda s, h: (0, h))
    return pl.pallas_call(
        body, name="gla_fwd", grid=(dm.Bl, HEADS),
        in_specs=[pl.BlockSpec((lp, hw), head), pl.BlockSpec((lp, LR_LANES), lambda s, h: (s, 0)),
                  wspec, bspec, wspec, bspec, pl.BlockSpec((1, hv), lambda s, h: (0, 0))],
        out_specs=[pl.BlockSpec((lp, hv), head), pl.BlockSpec((lp, hv), head),
                   pl.BlockSpec((1, 1, 2, nc, hv, hk), lambda s, h: (s, h, 0, 0, 0, 0)),
                   pl.BlockSpec((2, lp, hk), lambda s, h: (0, s, h)), pl.BlockSpec((2, lp, hk), lambda s, h: (0, s, h))],
        out_shape=[jax.ShapeDtypeStruct((dm.T, dm.DV), BF16), jax.ShapeDtypeStruct((dm.T, dm.DV), BF16),
                   jax.ShapeDtypeStruct((dm.Bl, HEADS, 2, nc, hv, hk), BF16),
                   jax.ShapeDtypeStruct((2, dm.T, dm.DK), F32), jax.ShapeDtypeStruct((2, dm.T, dm.DK), F32)],
        scratch_shapes=[pltpu.VMEM((lp, hv), F32), pltpu.VMEM((lp, hv), F32)],
        compiler_params=_cp(2),
    )(proj_b, lr, wg_f, bg_f, wg_b, bg_b, gla_g)


def _gla_bwd(proj_b, lr, o_all, dy_gla, states, decays, gate_slopes, wg_f, wg_b, gla_g, dm):
    lp, hk, hv, nc, c0, hw = dm.LP, dm.HK, dm.HV, dm.NC, dm.C0, dm.HW
    scale = hk ** -0.5
    gc = _group_chunks(dm)
    gr, ng = gc * CHUNK, (nc - c0) // gc

    def body(p_ref, lr_ref, o_ref, dy_ref, st_ref, b_ref, gs_ref, wf_ref, wb_ref, gg_ref,
             d_ref, dlr_ref, gwf_ref, gbf_ref, gwb_ref, gbb_ref, ggg_ref, do_s, dq_s, dk_s, dv_s, dlr_s):
        low_incl, up_strict, ones_low, ones_up = _group_masks(gr)
        h = pl.program_id(1)

        @pl.when(h == 0)
        def _():
            dlr_ref[...] = jnp.zeros_like(dlr_ref)

        if c0 > 0:
            zr = c0 * CHUNK
            d_ref[0:zr, :] = jnp.zeros((zr, hw), BF16)
        for acc in (dq_s, dk_s, dv_s, dlr_s):
            acc[...] = jnp.zeros_like(acc)

        def norm_bwd(i, ggg):
            r0 = pl.multiple_of((c0 + i * gc) * CHUNK, CHUNK)
            o = o_ref[pl.ds(r0, gr), :].astype(F32)
            dy = dy_ref[pl.ds(r0, gr), :].astype(F32)
            r = p_ref[pl.ds(r0, gr), 2 * hk + hv:].astype(F32)
            rstd = lax.rsqrt(jnp.mean(o * o, axis=-1, keepdims=True) + EPS)
            ohat = o * rstd
            sg = _sigmoid(r)
            d_on = dy * (r * sg)
            d_ref[pl.ds(r0, gr), 2 * hk + hv:] = (dy * ohat * gg_ref[...] * (sg * (1.0 + r * (1.0 - sg)))).astype(BF16)
            d_oh = d_on * gg_ref[...]
            do_s[pl.ds(r0, gr), :] = (rstd * (d_oh - ohat * jnp.mean(d_oh * ohat, axis=-1, keepdims=True))).astype(BF16)
            return ggg + jnp.sum(d_on * ohat, axis=0, keepdims=True)

        ggg = lax.fori_loop(0, ng, norm_bwd, jnp.zeros((1, hv), F32))

        @pl.when((pl.program_id(0) == 0) & (h == 0))
        def _():
            ggg_ref[...] = jnp.zeros_like(ggg_ref)

        ggg_ref[0:1, :] += ggg

        def load(gi):
            r0 = pl.multiple_of((c0 + gi * gc) * CHUNK, CHUNK)
            blk = p_ref[pl.ds(r0, gr), :]
            return r0, blk[:, :hk].astype(F32) * scale, blk[:, hk:2 * hk].astype(F32), blk[:, 2 * hk:2 * hk + hv]

        zero = jnp.zeros((hv, hk), F32)

        def grad(gi, carry, fwd):
            dst, gw, gb = carry
            w_ref, way = (wf_ref, 0) if fwd else (wb_ref, 1)
            mask = low_incl if fwd else up_strict
            r0, q, k, v = load(gi)
            b = b_ref[way, pl.ds(r0, gr), :]
            btot = _chunk_totals(b, fwd)
            eb, enb, edb, dec = jnp.exp(b), jnp.exp(-b), jnp.exp(btot - b), jnp.exp(btot)
            qi_f, ki_f, kd_f = q * eb, k * enb, k * edb
            qi, ki, kd = qi_f.astype(BF16), ki_f.astype(BF16), kd_f.astype(BF16)
            do = do_s[pl.ds(r0, gr), :]
            a = _dot_nt(qi, ki)
            da = _dot_nt(do, v)
            yield
            a = jnp.where(mask, a, 0.0).astype(BF16)
            da = jnp.where(mask, da, 0.0).astype(BF16)
            dv = _dot_tn(a, do)
            dqi = _dot(da, ki)
            dki = _dot_tn(da, qi)
            dv_c, dqi_c, dkd_c, extra_c = [None] * gc, [None] * gc, [None] * gc, [None] * gc
            chunk_rows = [slice(c * CHUNK, (c + 1) * CHUNK) for c in range(gc)]
            qdo = [_dot_tn(do[rows], qi[rows]) for rows in chunk_rows]
            for c in (reversed(range(gc)) if fwd else range(gc)):
                yield
                rows = chunk_rows[c]
                st = st_ref[0, 0, way, c0 + gi * gc + c]
                dsn_b = dst.astype(BF16)
                dec_c = dec[c * CHUNK:c * CHUNK + 1]
                dv_c[c] = dv[rows] + _dot_nt(kd[rows], dsn_b)
                dqi_c[c] = dqi[rows] + _dot(do[rows], st)
                dkd_c[c] = _dot(v[rows], dsn_b)
                ddec = jnp.sum(st.astype(F32) * dst, axis=0, keepdims=True)
                extra = jnp.sum(dkd_c[c] * kd_f[rows], axis=0, keepdims=True) + ddec * dec_c
                extra_c[c] = jnp.broadcast_to(extra, (CHUNK, hk))
                dst = dst * dec_c + qdo[c]
            yield
            dv, dqi = jnp.concatenate(dv_c, axis=0), jnp.concatenate(dqi_c, axis=0)
            dkd, extra = jnp.concatenate(dkd_c, axis=0), jnp.concatenate(extra_c, axis=0)
            dq_s[pl.ds(r0, gr), :] += dqi * eb * scale
            dk_s[pl.ds(r0, gr), :] += dki * enb + dkd * edb
            dv_s[pl.ds(r0, gr), :] += dv
            db = dqi * qi_f - dki * ki_f - dkd * kd_f
            dg = _dot_exact01(ones_up if fwd else ones_low, db) + extra
            yield
            dz = dg * gs_ref[way, pl.ds(r0, gr), :]
            dz_b = dz.astype(BF16)
            dlr_s[pl.ds(r0, gr), :] += _dot_nt(dz_b, w_ref[...])
            return dst, gw + _dot_tn(lr_ref[pl.ds(r0, gr), :], dz_b), gb + jnp.sum(dz, axis=0, keepdims=True)

        def grad_step(i, carry):
            return tuple(_interleave([grad(ng - 1 - i, carry[0], True), grad(i, carry[1], False)]))

        init = (zero, jnp.zeros((LR_LANES, hk), F32), jnp.zeros((1, hk), F32))
        (_, gw_f, gb_f), (_, gw_b, gb_b) = lax.fori_loop(0, ng, grad_step, (init, init))
        for gw_ref, gb_ref, gw, gb in ((gwf_ref, gbf_ref, gw_f, gb_f), (gwb_ref, gbb_ref, gw_b, gb_b)):
            gw_ref[0] = gw
            gb_ref[0] = jnp.zeros((8, hk), F32)
            gb_ref[0, 0:1, :] = gb

        def combine(i, carry):
            r0 = pl.multiple_of((c0 + i * gc) * CHUNK, CHUNK)
            d_ref[pl.ds(r0, gr), 0:hk] = dq_s[pl.ds(r0, gr), :].astype(BF16)
            d_ref[pl.ds(r0, gr), hk:2 * hk] = dk_s[pl.ds(r0, gr), :].astype(BF16)
            d_ref[pl.ds(r0, gr), 2 * hk:2 * hk + hv] = dv_s[pl.ds(r0, gr), :].astype(BF16)
            dlr_ref[pl.ds(r0, gr), :] += dlr_s[pl.ds(r0, gr), :]
            return carry

        lax.fori_loop(0, ng, combine, 0)

    head = lambda s, h: (s, h)
    wspec = pl.BlockSpec((LR_LANES, hk), lambda s, h: (0, h))
    gwspec = pl.BlockSpec((1, LR_LANES, hk), lambda s, h: (s, 0, h))
    gbspec = pl.BlockSpec((1, 8, hk), lambda s, h: (s, 0, h))
    gw_shape = jax.ShapeDtypeStruct((dm.Bl, LR_LANES, dm.DK), F32)
    gb_shape = jax.ShapeDtypeStruct((dm.Bl, 8, dm.DK), F32)
    both = pl.BlockSpec((2, lp, hk), lambda s, h: (0, s, h))
    return pl.pallas_call(
        body, name="gla_bwd", grid=(dm.Bl, HEADS),
        in_specs=[pl.BlockSpec((lp, hw), head), pl.BlockSpec((lp, LR_LANES), lambda s, h: (s, 0)),
                  pl.BlockSpec((lp, hv), head), pl.BlockSpec((lp, hv), head),
                  pl.BlockSpec((1, 1, 2, nc, hv, hk), lambda s, h: (s, h, 0, 0, 0, 0)), both, both,
                  wspec, wspec, pl.BlockSpec((1, hv), lambda s, h: (0, 0))],
        out_specs=[pl.BlockSpec((lp, hw), head), pl.BlockSpec((lp, LR_LANES), lambda s, h: (s, 0)),
                   gwspec, gbspec, gwspec, gbspec, pl.BlockSpec((8, hv), lambda s, h: (0, 0))],
        out_shape=[jax.ShapeDtypeStruct((dm.T, HEADS * hw), BF16), jax.ShapeDtypeStruct((dm.T, LR_LANES), F32),
                   gw_shape, gb_shape, gw_shape, gb_shape, jax.ShapeDtypeStruct((8, hv), F32)],
        scratch_shapes=[pltpu.VMEM((lp, hv), BF16), pltpu.VMEM((lp, hk), F32), pltpu.VMEM((lp, hk), F32),
                        pltpu.VMEM((lp, hv), F32), pltpu.VMEM((lp, LR_LANES), F32)],
        compiler_params=_cp(2),
    )(proj_b, lr, o_all, dy_gla, states, decays, gate_slopes, wg_f, wg_b, gla_g)


def _stream_tiles(n_tiles, loads, stores, compute):
    for cp in loads(0, 0):
        cp.start()

    def step(t, carry):
        slot = t % 2

        @pl.when(t + 1 < n_tiles)
        def _():
            for cp in loads(t + 1, 1 - slot):
                cp.start()

        for cp in loads(t, slot):
            cp.wait()

        @pl.when(t >= 2)
        def _():
            for cp in stores(t - 2, slot):
                cp.wait()

        compute(t, slot)
        for cp in stores(t, slot):
            cp.start()
        return carry

    lax.fori_loop(0, n_tiles, step, 0)
    for t in range(max(n_tiles - 2, 0), n_tiles):
        for cp in stores(t, t % 2):
            cp.wait()


def _token_tiles(dm, target_rows=512):
    rows = _pick(dm.S, target_rows, 16)
    per_seq = dm.S // rows
    return rows, dm.Bl * per_seq, lambda t: pl.multiple_of((t // per_seq) * dm.LP + dm.TM + (t % per_seq) * rows, 16)


def _head(y_conv, y_gla, proj_c, w_oc, w_og, w_out, x, target, g_post, dm):
    d, tm = dm.D, dm.TM
    rows, n_tiles, first_row = _token_tiles(dm, 256)
    n_out = 8

    def body(*refs):
        yc_hbm, yg_hbm, c_hbm, woc_ref, wog_ref, wo_ref, x_hbm, t_hbm, g_ref = refs[:9]
        outs, st_ref = refs[9:9 + n_out], refs[9 + n_out]
        ycbuf, ygbuf, cbuf, xbuf, tbuf = refs[10 + n_out:15 + n_out]
        obufs = refs[15 + n_out:15 + 2 * n_out]
        zbuf, zbuf2, sem_in, sem_out, sem_zero = refs[15 + 2 * n_out:]

        def loads(t, slot):
            padded = [(yc_hbm, ycbuf), (yg_hbm, ygbuf), (c_hbm, cbuf)]
            own = [(x_hbm, xbuf), (t_hbm, tbuf)]
            return ([pltpu.make_async_copy(h.at[pl.ds(first_row(t), rows), :], b.at[slot], sem_in.at[i, slot])
                     for i, (h, b) in enumerate(padded)] +
                    [pltpu.make_async_copy(h.at[pl.ds(t * rows, rows), :], b.at[slot], sem_in.at[3 + i, slot])
                     for i, (h, b) in enumerate(own)])

        def stores(t, slot):
            return [pltpu.make_async_copy(b.at[slot], h.at[pl.ds(first_row(t), rows), :], sem_out.at[i, slot])
                    for i, (h, b) in enumerate(zip(outs, obufs))]

        def compute(t, slot):
            mg_o, do_o, dy_o, dpc_o, dpg_o, dc_o, dyc_o, dyg_o = obufs
            pc = _dot(ycbuf[slot], woc_ref[...])
            pg = _dot(ygbuf[slot], wog_ref[...])
            sa = _sigmoid(cbuf[slot, :, :d].astype(F32))
            sb = _sigmoid(cbuf[slot, :, d:].astype(F32))
            merged = (sa * pc + sb * pg).astype(BF16)
            mg_o[slot] = merged
            out = _dot(merged, wo_ref[...])
            rstd = lax.rsqrt(jnp.mean(out * out, axis=-1, keepdims=True) + EPS)
            ohat = out * rstd
            err = xbuf[slot] + ohat * g_ref[...] - tbuf[slot]
            dy = err * (1.0 / d)
            d_oh = dy * g_ref[...]
            d_out = (rstd * (d_oh - ohat * jnp.mean(d_oh * ohat, axis=-1, keepdims=True))).astype(BF16)
            do_o[slot] = d_out
            dy_o[slot] = dy.astype(BF16)
            st_ref[0:1, :] += jnp.sum(dy * ohat, axis=0, keepdims=True)
            st_ref[1:2, :] += jnp.sum(err * err, axis=0, keepdims=True)
            dmg = _dot_nt(d_out, wo_ref[...])
            dpc = (dmg * sa).astype(BF16)
            dpg = (dmg * sb).astype(BF16)
            dpc_o[slot] = dpc
            dpg_o[slot] = dpg
            dc_o[slot, :, :d] = (dmg * pc * sa * (1.0 - sa)).astype(BF16)
            dc_o[slot, :, d:] = (dmg * pg * sb * (1.0 - sb)).astype(BF16)
            dyc_o[slot] = _dot_nt(dpc, woc_ref[...]).astype(BF16)
            dyg_o[slot] = _dot_nt(dpg, wog_ref[...]).astype(BF16)

        st_ref[...] = jnp.zeros_like(st_ref)
        zbuf[...] = jnp.zeros_like(zbuf)
        zbuf2[...] = jnp.zeros_like(zbuf2)
        zeros = [pltpu.make_async_copy(zbuf2 if out.shape[1] == 2 * d else zbuf, out.at[pl.ds(b * dm.LP, tm), :], sem_zero.at[i, b])
                 for i, out in enumerate(outs) for b in range(dm.Bl)]
        for cp in zeros:
            cp.start()
        _stream_tiles(n_tiles, loads, stores, compute)
        for cp in zeros:
            cp.wait()

    any_spec, vmem = pl.BlockSpec(memory_space=pl.ANY), pl.BlockSpec(memory_space=pltpu.VMEM)
    widths = [d, d, d, d, d, 2 * d, d, d]
    tile = lambda w, dt: pltpu.VMEM((2, rows, w), dt)
    return pl.pallas_call(
        body, name="head", in_specs=[any_spec] * 3 + [vmem] * 3 + [any_spec] * 2 + [vmem],
        out_specs=[any_spec] * n_out + [vmem],
        out_shape=[jax.ShapeDtypeStruct((dm.T, w), BF16) for w in widths] + [jax.ShapeDtypeStruct((8, d), F32)],
        scratch_shapes=[tile(d, BF16), tile(d, BF16), tile(2 * d, BF16), tile(d, F32), tile(d, F32)]
        + [tile(w, BF16) for w in widths]
        + [pltpu.VMEM((tm, d), BF16), pltpu.VMEM((tm, 2 * d), BF16), pltpu.SemaphoreType.DMA((5, 2)),
           pltpu.SemaphoreType.DMA((n_out, 2)), pltpu.SemaphoreType.DMA((n_out, dm.Bl))],
        compiler_params=pltpu.CompilerParams(vmem_limit_bytes=VMEM_LIMIT_BYTES),
    )(y_conv, y_gla, proj_c, w_oc, w_og, w_out, x.reshape(dm.Bl * dm.S, d), target.reshape(dm.Bl * dm.S, d), g_post)


def _prenorm_bwd(du, dy, x, metapad, g_pre, dm):
    d, tm = dm.D, dm.TM
    rows, n_tiles, first_row = _token_tiles(dm)

    def body(du_hbm, dy_hbm, x_hbm, mp_ref, g_ref, gx_hbm, dmeta_ref, gg_ref, dubuf, dybuf, xbuf, gbuf, mbuf,
             sem_in, sem_out, sem_meta):
        def norm_bwd(h, du, dy):
            rstd = lax.rsqrt(jnp.mean(h * h, axis=-1, keepdims=True) + EPS)
            hhat = h * rstd
            dug = du * g_ref[...]
            gg_ref[0:1, :] += jnp.sum(du * hhat, axis=0, keepdims=True)
            return dy + rstd * (dug - hhat * jnp.mean(dug * hhat, axis=-1, keepdims=True))

        def loads(t, slot):
            return [pltpu.make_async_copy(du_hbm.at[pl.ds(first_row(t), rows), :], dubuf.at[slot], sem_in.at[0, slot]),
                    pltpu.make_async_copy(dy_hbm.at[pl.ds(first_row(t), rows), :], dybuf.at[slot], sem_in.at[1, slot]),
                    pltpu.make_async_copy(x_hbm.at[pl.ds(t * rows, rows), :], xbuf.at[slot], sem_in.at[2, slot])]

        def stores(t, slot):
            return [pltpu.make_async_copy(gbuf.at[slot], gx_hbm.at[pl.ds(t * rows, rows), :], sem_out.at[slot])]

        def compute(t, slot):
            gbuf[slot] = norm_bwd(xbuf[slot], dubuf[slot].astype(F32), dybuf[slot].astype(F32))

        gg_ref[...] = jnp.zeros_like(gg_ref)
        meta = [pltpu.make_async_copy(du_hbm.at[pl.ds(b * dm.LP, tm), :], mbuf.at[b], sem_meta.at[b]) for b in range(dm.Bl)]
        for cp in meta:
            cp.start()
        _stream_tiles(n_tiles, loads, stores, compute)
        for b, cp in enumerate(meta):
            cp.wait()
            dmeta_ref[b] = norm_bwd(mp_ref[...], mbuf[b].astype(F32), 0.0)

    any_spec, vmem = pl.BlockSpec(memory_space=pl.ANY), pl.BlockSpec(memory_space=pltpu.VMEM)
    grad_x, d_meta, gg = pl.pallas_call(
        body, name="prenorm_bwd", in_specs=[any_spec, any_spec, any_spec, vmem, vmem], out_specs=[any_spec, vmem, vmem],
        out_shape=[jax.ShapeDtypeStruct((dm.Bl * dm.S, d), F32), jax.ShapeDtypeStruct((dm.Bl, tm, d), F32),
                   jax.ShapeDtypeStruct((8, d), F32)],
        scratch_shapes=[pltpu.VMEM((2, rows, d), BF16), pltpu.VMEM((2, rows, d), BF16), pltpu.VMEM((2, rows, d), F32),
                        pltpu.VMEM((2, rows, d), F32), pltpu.VMEM((dm.Bl, tm, d), BF16),
                        pltpu.SemaphoreType.DMA((3, 2)), pltpu.SemaphoreType.DMA((2,)), pltpu.SemaphoreType.DMA((dm.Bl,))],
        compiler_params=pltpu.CompilerParams(vmem_limit_bytes=VMEM_LIMIT_BYTES),
    )(du, dy, x.reshape(dm.Bl * dm.S, d), metapad, g_pre)
    return grad_x.reshape(dm.Bl, dm.S, d), d_meta, gg


def _adamw(partials, w, m, v, name, by_columns=False):
    r, c = w.shape
    n_parts = partials.shape[0]
    tr, tc = (r, _pick(c, 128, 128)) if by_columns else (_pick(r, 256, 16), c)

    def body(p_ref, w_ref, m_ref, v_ref, g_ref, d_ref, nm_ref, nv_ref):
        g = p_ref[0].astype(F32)
        for j in range(1, n_parts):
            g = g + p_ref[j].astype(F32)
        g_ref[...] = g
        d_ref[...], nm_ref[...], nv_ref[...] = _adam_step(g, w_ref[...], m_ref[...], v_ref[...])

    at = (lambda i: (0, i)) if by_columns else (lambda i: (i, 0))
    tile = pl.BlockSpec((tr, tc), at)
    out = jax.ShapeDtypeStruct((r, c), F32)
    return pl.pallas_call(
        body, name=name, grid=(c // tc if by_columns else r // tr,),
        in_specs=[pl.BlockSpec((n_parts, tr, tc), lambda i: (0,) + at(i)), tile, tile, tile],
        out_specs=[tile, tile, tile, tile], out_shape=[out, out, out, out], compiler_params=_cp(1),
    )(partials, w, m, v)


def _adam_step(g, w, m, v):
    m2 = ADAM_B1 * m + (1.0 - ADAM_B1) * g
    v2 = ADAM_B2 * v + (1.0 - ADAM_B2) * (g * g)
    m_hat = m2 / (1.0 - ADAM_B1 ** ADAM_STEP)
    v_hat = v2 / (1.0 - ADAM_B2 ** ADAM_STEP)
    return -ADAM_LR * (m_hat / (jnp.sqrt(v_hat) + ADAM_EPS) + ADAM_WD * w), m2, v2


def _adamw_small(items, name):
    n = len(items)

    def body(*refs):
        ins, outs = refs[:4 * n], refs[4 * n:]
        for i in range(n):
            p_ref, w_ref, m_ref, v_ref = ins[4 * i:4 * i + 4]
            g = p_ref[0]
            for j in range(1, p_ref.shape[0]):
                g = g + p_ref[j]
            delta, m2, v2 = _adam_step(g, w_ref[...], m_ref[...], v_ref[...])
            for o_ref, val in zip(outs[4 * i:4 * i + 4], (g, delta, m2, v2)):
                o_ref[...] = val

    vmem = pl.BlockSpec(memory_space=pltpu.VMEM)
    res = pl.pallas_call(
        body, name=name, in_specs=[vmem] * (4 * n), out_specs=[vmem] * (4 * n),
        out_shape=[jax.ShapeDtypeStruct(w.shape, F32) for _, w, _, _ in items for _ in range(4)],
    )(*[a for item in items for a in item])
    return [res[4 * i:4 * i + 4] for i in range(n)]


def _pack_rows(wt, dm):
    d, dk, hk, hv, cw, nj = dm.D, dm.DK, dm.HK, dm.HV, dm.CW, dm.NJ
    a = wt[:4 * d].reshape(4, nj, cw, d).transpose(1, 0, 2, 3).reshape(4 * d, d)
    b = jnp.concatenate([wt[4 * d:4 * d + dk].reshape(HEADS, hk, d), wt[4 * d + dk:5 * d].reshape(HEADS, hk, d),
                         wt[5 * d:6 * d].reshape(HEADS, hv, d), wt[6 * d:7 * d].reshape(HEADS, hv, d)],
                        axis=1).reshape(3 * d, d)
    c = wt[7 * d + 2 * RANK:]
    lr = jnp.pad(wt[7 * d:7 * d + 2 * RANK], ((0, LR_LANES - 2 * RANK), (0, 0)))
    return a, b, c, lr


def _unpack_rows(a, b, c, lr, dm):
    d, hk, hv, cw, nj, hw = dm.D, dm.HK, dm.HV, dm.CW, dm.NJ, dm.HW
    conv = a.reshape(nj, 4, cw, d).transpose(1, 0, 2, 3).reshape(4 * d, d)
    heads = b.reshape(HEADS, hw, d)
    q = heads[:, :hk].reshape(HEADS * hk, d)
    k = heads[:, hk:2 * hk].reshape(HEADS * hk, d)
    v = heads[:, 2 * hk:2 * hk + hv].reshape(HEADS * hv, d)
    r = heads[:, 2 * hk + hv:].reshape(HEADS * hv, d)
    return jnp.concatenate([conv, q, k, v, r, lr[:2 * RANK], c], axis=0)


def _column_shards(g, shard_shape):
    r, c = g.shape
    return g.reshape(r, N_DEV, c // N_DEV).transpose(1, 0, 2).reshape((N_DEV,) + tuple(shard_shape))


def _join_column_shards(parts):
    r, c = parts.shape[-2:]
    return parts.reshape(N_DEV, r, c).transpose(1, 0, 2).reshape(r, N_DEV * c)


def _local_step(x, target, meta, g_pre, u, wt_in, conv_w, wg_f, bg_f, wg_b, bg_b, gla_g, out_weights, g_post,
                on_matrix_grads=None):
    bl, s, d = x.shape
    dm = _Dims(bl, s, d)
    metapad = jnp.concatenate([jnp.zeros((dm.TM - N_META, d), F32), meta], axis=0)
    wta, wtb, wtc, wtlr = _pack_rows(wt_in, dm)
    wgp_f = jnp.pad(wg_f, ((0, LR_LANES - RANK), (0, 0))).astype(BF16)
    wgp_b = jnp.pad(wg_b, ((RANK, LR_LANES - 2 * RANK), (0, 0))).astype(BF16)

    u = _prenorm_meta(u, metapad, g_pre, dm)
    proj_a, proj_b, proj_c, lr = _inproj(u, [wta, wtb, wtc, wtlr], dm)
    y_conv = _conv_fwd(proj_a, conv_w, dm)
    o_all, y_gla, states, decays, gate_slopes = _gla_fwd(proj_b, lr, wgp_f, bg_f, wgp_b, bg_b, gla_g, dm)
    w_oc, w_og, w_out = out_weights(y_conv) if callable(out_weights) else out_weights
    merged, d_out, dy, d_pc, d_pg, d_c, dy_conv, dy_gla, stats = _head(y_conv, y_gla, proj_c, w_oc, w_og, w_out, x, target,
                                                                        g_post, dm)
    loss = 0.5 / d * jnp.sum(stats[1])

    g_out = _matmul_tn(merged, d_out, BF16, "grad_w_out")
    g_oc = _matmul_tn(y_conv, d_pc, BF16, "grad_w_out_conv")
    g_og = _matmul_tn(y_gla, d_pg, BF16, "grad_w_out_gla")
    if on_matrix_grads is not None:
        conv_w = conv_w + on_matrix_grads(dict(w_out_conv=g_oc, w_out_gla=g_og, w_merge_out=g_out))
    d_a, g_conv = _conv_bwd(proj_a, dy_conv, conv_w, dm)
    d_b, d_lr, gwp_f, gbp_f, gwp_b, gbp_b, g_gla = _gla_bwd(proj_b, lr, o_all, dy_gla, states, decays, gate_slopes, wgp_f, wgp_b, gla_g, dm)
    g_in = _unpack_rows(_matmul_tn(d_a, u, BF16, "grad_w_in_conv"), _matmul_tn(d_b, u, BF16, "grad_w_in_gla"),
                        _matmul_tn(d_c, u, BF16, "grad_w_in_merge"), _matmul_tn(d_lr, u, BF16, "grad_w_in_gate"), dm)
    if on_matrix_grads is not None:
        wtlr = wtlr + on_matrix_grads(dict(w_in=g_in)).astype(BF16)
    du = _grad_u([d_a, d_b, d_c, d_lr], [wta, wtb, wtc, wtlr], dm)
    grad_x, d_meta, g_pre_rows = _prenorm_bwd(du, dy, x, metapad, g_pre, dm)

    grads = dict(
        meta_tokens=jnp.sum(d_meta[:, dm.TM - N_META:, :], axis=0), norm_pre=g_pre_rows[0:1], w_in=g_in,
        conv_w=g_conv[0:3], w_gate_fwd=jnp.sum(gwp_f, axis=0)[:RANK], b_gate_fwd=jnp.sum(gbp_f, axis=0)[0:1],
        w_gate_bwd=jnp.sum(gwp_b, axis=0)[RANK:2 * RANK], b_gate_bwd=jnp.sum(gbp_b, axis=0)[0:1],
        gla_norm=g_gla[0:1], w_out_conv=g_oc, w_out_gla=g_og, w_merge_out=g_out, norm_post=stats[0:1])
    return loss, grad_x, grads


MATRICES = ("w_out_conv", "w_out_gla", "w_merge_out")
SMALL_SHARDED = ("meta_tokens", "conv_w", "w_gate_fwd", "w_gate_bwd")
REPLICATED = ("norm_pre", "b_gate_fwd", "b_gate_bwd", "gla_norm", "norm_post")
NAMES = ("meta_tokens", "norm_pre", "w_in", "conv_w", "w_gate_fwd", "b_gate_fwd", "w_gate_bwd", "b_gate_bwd", "gla_norm",
         "w_out_conv", "w_out_gla", "w_merge_out", "norm_post")


def kernel(x, meta_tokens, norm_pre, w_in, conv_w, w_gate_fwd, b_gate_fwd, w_gate_bwd, b_gate_bwd, gla_norm, w_out_conv, w_out_gla, w_merge_out, norm_post, loss_target, m_meta_tokens, m_norm_pre, m_w_in, m_conv_w, m_w_gate_fwd, m_b_gate_fwd, m_w_gate_bwd, m_b_gate_bwd, m_gla_norm, m_w_out_conv, m_w_out_gla, m_w_merge_out, m_norm_post, v_meta_tokens, v_norm_pre, v_w_in, v_conv_w, v_w_gate_fwd, v_b_gate_fwd, v_w_gate_bwd, v_b_gate_bwd, v_gla_norm, v_w_out_conv, v_w_out_gla, v_w_merge_out, v_norm_post):
    w = dict(meta_tokens=meta_tokens, norm_pre=norm_pre, w_in=w_in[0], conv_w=conv_w, w_gate_fwd=w_gate_fwd,
             b_gate_fwd=b_gate_fwd, w_gate_bwd=w_gate_bwd, b_gate_bwd=b_gate_bwd, gla_norm=gla_norm,
             w_out_conv=w_out_conv[0], w_out_gla=w_out_gla[0], w_merge_out=w_merge_out[0], norm_post=norm_post)
    m = dict(meta_tokens=m_meta_tokens, norm_pre=m_norm_pre, w_in=m_w_in[0], conv_w=m_conv_w, w_gate_fwd=m_w_gate_fwd,
             b_gate_fwd=m_b_gate_fwd, w_gate_bwd=m_w_gate_bwd, b_gate_bwd=m_b_gate_bwd, gla_norm=m_gla_norm,
             w_out_conv=m_w_out_conv[0], w_out_gla=m_w_out_gla[0], w_merge_out=m_w_merge_out[0], norm_post=m_norm_post)
    v = dict(meta_tokens=v_meta_tokens, norm_pre=v_norm_pre, w_in=v_w_in[0], conv_w=v_conv_w, w_gate_fwd=v_w_gate_fwd,
             b_gate_fwd=v_b_gate_fwd, w_gate_bwd=v_w_gate_bwd, b_gate_bwd=v_b_gate_bwd, gla_norm=v_gla_norm,
             w_out_conv=v_w_out_conv[0], w_out_gla=v_w_out_gla[0], w_merge_out=v_w_merge_out[0], norm_post=v_norm_post)
    d = x.shape[-1]

    dm = _Dims(*x.shape)
    wt_all, *small_all, u = _gather_two_level([w["w_in"].T.astype(BF16)] + [w[n] for n in SMALL_SHARDED], "gather_weights",
                                              _prenorm_tokens_side(x, norm_pre, dm))
    _, late_weights = _exchange_start([w[n].astype(BF16) for n in MATRICES], [], small_all[0], "gather_out_weights_start")
    wt_in = wt_all.reshape(-1, d)
    small = {n: _join_column_shards(p) for n, p in zip(SMALL_SHARDED, small_all)}

    def out_weights(after):
        return tuple(a.reshape(-1, d) for a in _exchange_wait(late_weights, after, "gather_out_weights_wait"))

    pending = []

    def on_matrix_grads(g):
        token, state = _exchange_start([], [t.astype(BF16).reshape(N_DEV, -1, d) for t in g.values()], None,
                                       "exchange_grads_start_" + "_".join(g))
        pending.append((tuple(g), state))
        return token

    loss, grad_x, grads = _local_step(
        x, loss_target, small["meta_tokens"], norm_pre, u, wt_in, small["conv_w"], small["w_gate_fwd"], b_gate_fwd,
        small["w_gate_bwd"], b_gate_bwd, gla_norm, out_weights, norm_post, on_matrix_grads)
    loss = lax.psum(loss, ("x", "y", "c"))
    received = {}
    for names, state in pending:
        received.update(zip(names, _exchange_wait(state, grad_x, "exchange_grads_wait_" + "_".join(names))))

    small_recv = _exchange([grads[n] for n in REPLICATED], [_column_shards(grads[n], w[n].shape) for n in SMALL_SHARDED],
                           "exchange_small_grads")

    results = {"w_in": [r.T[None] for r in _adamw(received["w_in"], w["w_in"].T, m["w_in"].T, v["w_in"].T, "adamw_w_in", by_columns=True)]}
    for n in MATRICES:
        results[n] = [r[None] for r in _adamw(received[n], w[n], m[n], v[n], "adamw_" + n)]
    small_names = REPLICATED + SMALL_SHARDED
    results.update(zip(small_names, _adamw_small([(p, w[n], m[n], v[n]) for n, p in zip(small_names, small_recv)], "adamw_small")))
    return (loss, grad_x, *[results[n][i] for i in range(4) for n in NAMES])
```

```python
import jax
import jax.numpy as jnp
from jax import lax
from jax.experimental import pallas as pl
from jax.experimental.pallas import tpu as pltpu

F32 = jnp.float32
BF16 = jnp.bfloat16
MESH = pl.DeviceIdType.MESH

N_META = 16
CHUNK = 64
CHUNK_SHIFT = 6
HEADS = 4
RANK = 16
LR_LANES = 128
PAD_ROWS = CHUNK - N_META
EPS = 1e-6
GATE_NORMALIZER = 16.0
N_DEV = 8
ADAM_LR, ADAM_B1, ADAM_B2, ADAM_EPS, ADAM_WD, ADAM_STEP = 0.001, 0.9, 0.999, 1e-08, 0.01, 10
VMEM_LIMIT_BYTES = 56 * 1024 * 1024


class _Dims:
    def __init__(self, bl, s, d):
        self.Bl, self.S, self.D = bl, s, d
        self.TM = CHUNK
        self.LP = self.TM + s
        self.T = bl * self.LP
        self.TPS = self.LP // self.TM
        self.NC = self.LP // CHUNK
        self.C0 = (self.TM - CHUNK) // CHUNK
        self.DK, self.DV = d // 2, d
        self.HK, self.HV = self.DK // HEADS, self.DV // HEADS
        self.HW = 2 * self.HK + 2 * self.HV
        self.CW = 256 if d % 256 == 0 and d > 256 else d // 4
        self.NJ = d // self.CW


def _pick(n, target, mult):
    t = min(n, target)
    while t >= mult:
        if n % t == 0 and t % mult == 0:
            return t
        t -= mult
    return n


def _cp(n_axes):
    return pltpu.CompilerParams(dimension_semantics=("arbitrary",) * n_axes, vmem_limit_bytes=VMEM_LIMIT_BYTES)


def _sigmoid(x):
    return 1.0 / (1.0 + jnp.exp(-x))


def _dot(a, b):
    return jnp.dot(a, b, preferred_element_type=F32)


def _dot_nt(a, b):
    return lax.dot_general(a, b, (((1,), (1,)), ((), ())), preferred_element_type=F32)


def _dot_tn(a, b):
    return lax.dot_general(a, b, (((0,), (0,)), ((), ())), preferred_element_type=F32)


def _dot_exact01(m01, x):
    hi = x.astype(BF16)
    lo = (x - hi.astype(F32)).astype(BF16)
    return _dot(m01, hi) + _dot(m01, lo)


def _exchange(gathers, scatters, name):
    arrays = list(gathers) + list(scatters)
    n, ng = len(arrays), len(gathers)

    def body(*refs):
        ins, outs = refs[:n], refs[n:2 * n]
        send_sems, recv_sems, local_sems = refs[2 * n:]
        x, y, c = lax.axis_index("x"), lax.axis_index("y"), lax.axis_index("c")
        me = 4 * x + 2 * y + c
        started = []
        for t in range(n):
            src, dst = ins[t], outs[t]
            own = pltpu.make_async_copy(src if t < ng else src.at[me], dst.at[me], local_sems.at[t])
            own.start()
            started.append(own)
            for k, pos, peer in _peers(x, y, c):
                cp = pltpu.make_async_remote_copy(
                    src_ref=src if t < ng else src.at[peer], dst_ref=dst.at[me],
                    send_sem=send_sems.at[t * (N_DEV - 1) + k - 1], recv_sem=recv_sems.at[t * (N_DEV - 1) + k - 1],
                    device_id=pos, device_id_type=MESH)
                cp.start()
                started.append(cp)
        for cp in started:
            cp.wait()

    out_shape = [jax.ShapeDtypeStruct((N_DEV,) + a.shape if t < ng else a.shape, a.dtype) for t, a in enumerate(arrays)]
    any_spec = pl.BlockSpec(memory_space=pl.ANY)
    return pl.pallas_call(
        body, name=name, out_shape=out_shape, in_specs=[any_spec] * n, out_specs=[any_spec] * n,
        scratch_shapes=[pltpu.SemaphoreType.DMA((n * (N_DEV - 1),)), pltpu.SemaphoreType.DMA((n * (N_DEV - 1),)),
                        pltpu.SemaphoreType.DMA((n,))],
        compiler_params=pltpu.CompilerParams(has_side_effects=True),
    )(*arrays)


def _gather_two_level(arrays, name, side=None):
    n = len(arrays)
    per = N_DEV - 1
    work, side_in, side_in_specs, side_out, side_out_specs, side_scratch = side or (None, [], [], [], [], [])
    n_in, n_out = len(side_in), len(side_out)

    def body(*refs):
        ins, outs = refs[:n], refs[n + n_in:2 * n + n_in]
        send_sems, recv_sems, local_sems = refs[2 * n + n_in + n_out:2 * n + n_in + n_out + 3]
        x, y, c = lax.axis_index("x"), lax.axis_index("y"), lax.axis_index("c")
        sibling = (x, y, 1 - c)
        chips = [(1 - x, y), (x, 1 - y), (1 - x, 1 - y)]
        index = lambda px, py, pc: 4 * px + 2 * py + pc

        def copy(t, k, block, to, from_input=False):
            slab = outs[t].at[index(*block)]
            return pltpu.make_async_remote_copy(
                src_ref=ins[t] if from_input else slab, dst_ref=slab, send_sem=send_sems.at[t * per + k],
                recv_sem=recv_sems.at[t * per + k], device_id=to, device_id_type=MESH)

        own, sent = [], []
        for t in range(n):
            own.append(pltpu.make_async_copy(ins[t], outs[t].at[index(x, y, c)], local_sems.at[t]))
            own[-1].start()
            first = [copy(t, 0, (x, y, c), sibling, True)]
            first += [copy(t, 1 + j, (x, y, c), (*chip, c), True) for j, chip in enumerate(chips)]
            for cp in first:
                cp.start()
            sent += first
        if work is not None:
            work(refs[n:n + n_in], refs[2 * n + n_in:2 * n + n_in + n_out], refs[2 * n + n_in + n_out + 3:])
        for t in range(n):
            for j, chip in enumerate(chips):
                copy(t, 1 + j, (*chip, c), (x, y, c)).wait_recv()
                sent.append(copy(t, 4 + j, (*chip, c), sibling))
                sent[-1].start()
        for t in range(n):
            copy(t, 0, sibling, (x, y, c)).wait_recv()
            for j, chip in enumerate(chips):
                copy(t, 4 + j, (*chip, 1 - c), (x, y, c)).wait_recv()
        for cp in sent:
            cp.wait_send()
        for cp in own:
            cp.wait()

    out_shape = [jax.ShapeDtypeStruct((N_DEV,) + a.shape, a.dtype) for a in arrays]
    any_spec = pl.BlockSpec(memory_space=pl.ANY)
    return pl.pallas_call(
        body, name=name, out_shape=out_shape + list(side_out), in_specs=[any_spec] * n + list(side_in_specs),
        out_specs=[any_spec] * n + list(side_out_specs),
        scratch_shapes=[pltpu.SemaphoreType.DMA((n * per,)), pltpu.SemaphoreType.DMA((n * per,)),
                        pltpu.SemaphoreType.DMA((n,))] + list(side_scratch),
        compiler_params=pltpu.CompilerParams(has_side_effects=True, vmem_limit_bytes=VMEM_LIMIT_BYTES),
    )(*arrays, *side_in)


def _peers(x, y, c):
    out = []
    for k in range(1, N_DEV):
        px = 1 - x if (k >> 2) & 1 else x
        py = 1 - y if (k >> 1) & 1 else y
        pc = 1 - c if k & 1 else c
        out.append((k, (px, py, pc), 4 * px + 2 * py + pc))
    return out


def _exchange_start(gathers, scatters, after, name):
    arrays = list(gathers) + list(scatters)
    n, ng = len(arrays), len(gathers)
    hbm = pl.BlockSpec(memory_space=pltpu.HBM)
    sem = pl.BlockSpec(memory_space=pltpu.SEMAPHORE)

    extra = [] if after is None else [after]
    ne = len(extra)

    def body(*refs):
        ins, lands = refs[:n], refs[n:2 * n]
        send_sems, recv_sems = refs[2 * n + ne], refs[2 * n + ne + 1]
        token = refs[4 * n + ne + 2]
        x, y, c = lax.axis_index("x"), lax.axis_index("y"), lax.axis_index("c")
        me = 4 * x + 2 * y + c
        for t in range(n):
            for k, pos, peer in _peers(x, y, c):
                pltpu.make_async_remote_copy(
                    src_ref=ins[t] if t < ng else ins[t].at[peer], dst_ref=lands[t].at[me],
                    send_sem=send_sems.at[t * (N_DEV - 1) + k - 1], recv_sem=recv_sems.at[t * (N_DEV - 1) + k - 1],
                    device_id=pos, device_id_type=MESH).start()
        token[...] = jnp.zeros_like(token)

    me = 4 * lax.axis_index("x") + 2 * lax.axis_index("y") + lax.axis_index("c")
    lands = [lax.dynamic_update_index_in_dim(lax.empty((N_DEV,) + a.shape if t < ng else a.shape, a.dtype),
                                             a if t < ng else lax.dynamic_index_in_dim(a, me, 0, keepdims=False), me, 0)
             for t, a in enumerate(arrays)]
    operands = [pltpu.with_memory_space_constraint(a, pltpu.HBM) for a in arrays + lands]
    sems = pltpu.SemaphoreType.DMA((n * (N_DEV - 1),))
    res = pl.pallas_call(
        body, name=name,
        out_shape=(sems, sems, *[pltpu.HBM(a.shape, a.dtype) for a in arrays + lands], jax.ShapeDtypeStruct((8, 128), F32)),
        in_specs=[hbm] * (2 * n) + [pl.BlockSpec(memory_space=pl.ANY)] * ne,
        out_specs=(sem, sem, *[hbm] * (2 * n), pl.BlockSpec(memory_space=pltpu.VMEM)),
        input_output_aliases={i: 2 + i for i in range(2 * n)},
        compiler_params=pltpu.CompilerParams(has_side_effects=pltpu.SideEffectType.DATAFLOW_SIDE_EFFECTING),
    )(*operands, *extra)
    return res[-1][0, 0], (ng, res[0], res[1], list(res[2:2 + n]), list(res[2 + n:2 + 2 * n]))


def _exchange_wait(state, after, name):
    ng, send_sems, recv_sems, sent, lands = state
    n = len(sent)
    hbm = pl.BlockSpec(memory_space=pltpu.HBM)
    sem = pl.BlockSpec(memory_space=pltpu.SEMAPHORE)

    def body(*refs):
        ins, land_refs = refs[:n], refs[n:2 * n]
        send_ref, recv_ref = refs[2 * n], refs[2 * n + 1]
        x, y, c = lax.axis_index("x"), lax.axis_index("y"), lax.axis_index("c")
        me = 4 * x + 2 * y + c
        for t in range(n):
            for k, pos, peer in _peers(x, y, c):
                cp = pltpu.make_async_remote_copy(
                    src_ref=ins[t] if t < ng else ins[t].at[peer], dst_ref=land_refs[t].at[me],
                    send_sem=send_ref.at[t * (N_DEV - 1) + k - 1], recv_sem=recv_ref.at[t * (N_DEV - 1) + k - 1],
                    device_id=pos, device_id_type=MESH)
                cp.wait_send()
                cp.wait_recv()

    res = pl.pallas_call(
        body, name=name, out_shape=tuple(pltpu.HBM(a.shape, a.dtype) for a in sent + lands),
        in_specs=[hbm] * (2 * n) + [sem, sem, pl.BlockSpec(memory_space=pl.ANY)], out_specs=tuple([hbm] * (2 * n)),
        input_output_aliases={i: i for i in range(2 * n)},
        compiler_params=pltpu.CompilerParams(has_side_effects=pltpu.SideEffectType.DATAFLOW_SIDE_EFFECTING),
    )(*sent, *lands, send_sems, recv_sems, after)
    return list(res[n:])


def _rms_scaled(h, g):
    return (h * lax.rsqrt(jnp.mean(h * h, axis=-1, keepdims=True) + EPS) * g).astype(BF16)


def _prenorm_tokens_side(x, g_pre, dm):
    bl, s, d = x.shape
    rows = _pick(s, 512, 16)
    tiles = [(b, j) for b in range(bl) for j in range(s // rows)]

    def work(ins, outs, scratch):
        (x_ref, g_ref), (u_ref,), (xbuf, ubuf, sem_in, sem_out) = ins, outs, scratch

        def load(t, slot):
            b, j = tiles[t]
            return pltpu.make_async_copy(x_ref.at[b, pl.ds(j * rows, rows), :], xbuf.at[slot], sem_in.at[slot])

        def store(t, slot):
            b, j = tiles[t]
            return pltpu.make_async_copy(ubuf.at[slot], u_ref.at[pl.ds(b * dm.LP + dm.TM + j * rows, rows), :], sem_out.at[slot])

        load(0, 0).start()
        for t in range(len(tiles)):
            slot = t % 2
            if t + 1 < len(tiles):
                load(t + 1, 1 - slot).start()
            load(t, slot).wait()
            if t >= 2:
                store(t - 2, slot).wait()
            ubuf[slot] = _rms_scaled(xbuf[slot], g_ref[...])
            store(t, slot).start()
        for t in range(max(len(tiles) - 2, 0), len(tiles)):
            store(t, t % 2).wait()

    any_spec = pl.BlockSpec(memory_space=pl.ANY)
    return (work, [x, g_pre], [any_spec, pl.BlockSpec(memory_space=pltpu.VMEM)],
            [jax.ShapeDtypeStruct((dm.T, d), BF16)], [any_spec],
            [pltpu.VMEM((2, rows, d), F32), pltpu.VMEM((2, rows, d), BF16), pltpu.SemaphoreType.DMA((2,)),
             pltpu.SemaphoreType.DMA((2,))])


def _prenorm_meta(u, metapad, g_pre, dm):
    tm, tps, d = dm.TM, dm.TPS, dm.D

    def body(u_in, mp_ref, g_ref, u_ref):
        u_ref[...] = _rms_scaled(mp_ref[...], g_ref[...])

    return pl.pallas_call(
        body, name="prenorm_meta", grid=(dm.Bl,),
        in_specs=[pl.BlockSpec(memory_space=pl.ANY), pl.BlockSpec((tm, d), lambda i: (0, 0)),
                  pl.BlockSpec((1, d), lambda i: (0, 0))],
        out_specs=pl.BlockSpec((tm, d), lambda i: (i * tps, 0)),
        out_shape=jax.ShapeDtypeStruct((dm.T, d), BF16), input_output_aliases={0: 0}, compiler_params=_cp(1),
    )(u, metapad, g_pre)


def _matmul_tn(a, b, out_dtype, name, tt=2304, tn=1024, tk=1024):
    t, k = a.shape
    n = b.shape[1]
    tt, tn, tk = _pick(t, tt, 16), _pick(n, tn, 128), _pick(k, tk, 128)
    nt = t // tt

    def body(a_ref, b_ref, o_ref, acc):
        p = _dot_tn(a_ref[...].astype(BF16), b_ref[...].astype(BF16))
        i = pl.program_id(2)

        @pl.when(i == 0)
        def _():
            acc[...] = p

        @pl.when(i > 0)
        def _():
            acc[...] += p

        @pl.when(i == nt - 1)
        def _():
            o_ref[...] = acc[...].astype(out_dtype)

    return pl.pallas_call(
        body, name=name, grid=(k // tk, n // tn, nt),
        in_specs=[pl.BlockSpec((tt, tk), lambda kk, j, i: (i, kk)), pl.BlockSpec((tt, tn), lambda kk, j, i: (i, j))],
        out_specs=pl.BlockSpec((tk, tn), lambda kk, j, i: (kk, j)),
        out_shape=jax.ShapeDtypeStruct((k, n), out_dtype), scratch_shapes=[pltpu.VMEM((tk, tn), F32)],
        compiler_params=_cp(3),
    )(a, b)


BF16_TILE_ROWS = 16


def _shard_offset(index, shard_rows):
    return (index * shard_rows) % BF16_TILE_ROWS


def _pad_shard(wt_shard, index):
    rows, d = wt_shard.shape
    padded = -(-(rows + max(_shard_offset(j, rows) for j in range(N_DEV))) // BF16_TILE_ROWS) * BF16_TILE_ROWS
    return lax.dynamic_update_slice(jnp.zeros((padded, d), wt_shard.dtype), wt_shard, (_shard_offset(index, rows), 0))


def _packed_parts(dm):
    d, dk, hk, hv, cw, nj, hw = dm.D, dm.DK, dm.HK, dm.HV, dm.CW, dm.NJ, dm.HW
    blocks = [(0, (j * 4 + p) * cw, p * d + j * cw, cw) for j in range(nj) for p in range(4)]
    for h in range(HEADS):
        blocks += [(1, h * hw, 4 * d + h * hk, hk), (1, h * hw + hk, 4 * d + dk + h * hk, hk),
                   (1, h * hw + 2 * hk, 5 * d + h * hv, hv), (1, h * hw + 2 * hk + hv, 6 * d + h * hv, hv)]
    blocks += [(2, 0, 7 * d + 2 * RANK, 2 * d), (3, 0, 7 * d, 2 * RANK)]
    return [4 * d, 3 * d, 2 * d, LR_LANES], blocks


def _pack_plan(dm):
    sh = (9 * dm.D + 2 * RANK) // N_DEV
    tile = BF16_TILE_ROWS
    copies, straddles = [], []
    for part, dst, r0, n in _packed_parts(dm)[1]:
        for j in range(N_DEV):
            a, b = max(r0, sh * j), min(r0 + n, sh * (j + 1))
            if a >= b:
                continue
            a_up, b_down = -(-a // tile) * tile, b // tile * tile
            if b_down > a_up:
                copies.append((j, a_up - sh * j + _shard_offset(j, sh), b_down - a_up, part, dst + a_up - r0))
            if a % tile:
                lo = a // tile * tile
                straddles.append((j, lo - sh * (j - 1) + _shard_offset(j - 1, sh), part, dst + lo - r0, a - lo))
    return copies, straddles


def _packed_scratch(dm):
    copies, straddles = _pack_plan(dm)
    return ([pltpu.VMEM((rows, dm.D), BF16) for rows in _packed_parts(dm)[0]]
            + [pltpu.VMEM((2 * max(len(straddles), 1), BF16_TILE_ROWS, dm.D), BF16),
               pltpu.SemaphoreType.DMA((len(copies) + 2 * len(straddles),))])


def _load_packed(g_ref, parts, edges, sems, dm):
    copies, straddles = _pack_plan(dm)
    tile = BF16_TILE_ROWS

    @pl.when(pl.program_id(0) == 0)
    def _():
        parts[3][2 * RANK:, :] = jnp.zeros((LR_LANES - 2 * RANK, dm.D), BF16)
        dmas = [pltpu.make_async_copy(g_ref.at[j, pl.ds(src, n), :], parts[p].at[pl.ds(dst, n), :], sems.at[i])
                for i, (j, src, n, p, dst) in enumerate(copies)]
        for i, (j, src, p, dst, split) in enumerate(straddles):
            k = len(copies) + 2 * i
            dmas.append(pltpu.make_async_copy(g_ref.at[j - 1, pl.ds(src, tile), :], edges.at[2 * i], sems.at[k]))
            dmas.append(pltpu.make_async_copy(g_ref.at[j, pl.ds(0, tile), :], edges.at[2 * i + 1], sems.at[k + 1]))
        for cp in dmas:
            cp.start()
        for cp in dmas:
            cp.wait()
        row = lax.broadcasted_iota(jnp.int32, (tile, dm.D), 0)
        for i, (j, src, p, dst, split) in enumerate(straddles):
            parts[p][dst:dst + tile, :] = jnp.where(row < split, edges[2 * i], edges[2 * i + 1])


def _inproj(u, gathered, dm):
    t, d = u.shape
    tm = _pick(t, 512, 16)
    widths = _packed_parts(dm)[0]
    cn = 1024

    def body(u_ref, g_ref, *rest):
        outs, parts, (edges, sems) = rest[:4], rest[4:8], rest[8:]
        _load_packed(g_ref, parts, edges, sems, dm)
        ut = u_ref[...]
        for w, o_ref in zip(parts, outs):
            n = w.shape[0]
            step = cn if n % cn == 0 else n
            for j in range(0, n, step):
                o_ref[:, j:j + step] = _dot_nt(ut, w[j:j + step, :]).astype(BF16)

    return pl.pallas_call(
        body, name="inproj", grid=(t // tm,),
        in_specs=[pl.BlockSpec((tm, d), lambda i: (i, 0)), pl.BlockSpec(memory_space=pl.ANY)],
        out_specs=[pl.BlockSpec((tm, w), lambda i: (i, 0)) for w in widths],
        out_shape=[jax.ShapeDtypeStruct((t, w), BF16) for w in widths],
        scratch_shapes=_packed_scratch(dm), compiler_params=_cp(1),
    )(u, gathered)


def _grad_u(d_parts, gathered, dm):
    t, d = d_parts[0].shape[0], dm.D
    tm = _pick(t, 512, 16)

    def body(*refs):
        d_refs, g_ref, o_ref = refs[:4], refs[4], refs[5]
        parts, edges, sems, acc = refs[6:10], refs[10], refs[11], refs[12]
        _load_packed(g_ref, parts, edges, sems, dm)
        acc[...] = _dot(d_refs[0][...].astype(BF16), parts[0][...])
        for a_ref, w in zip(d_refs[1:], parts[1:]):
            acc[...] += _dot(a_ref[...].astype(BF16), w[...])
        o_ref[...] = acc[...].astype(BF16)

    return pl.pallas_call(
        body, name="grad_u", grid=(t // tm,),
        in_specs=[pl.BlockSpec((tm, a.shape[1]), lambda i: (i, 0)) for a in d_parts] + [pl.BlockSpec(memory_space=pl.ANY)],
        out_specs=pl.BlockSpec((tm, d), lambda i: (i, 0)), out_shape=jax.ShapeDtypeStruct((t, d), BF16),
        scratch_shapes=_packed_scratch(dm) + [pltpu.VMEM((tm, d), F32)], compiler_params=_cp(1),
    )(*d_parts, gathered)


def _conv_rows(dm):
    return _pick(dm.LP, 256, 16)


def _shifted(m, prev_row, next_row, rows):
    row = lax.broadcasted_iota(jnp.int32, m.shape, 0)
    m_prev = jnp.where(row == 0, prev_row, pltpu.roll(m, 1, 0))
    m_next = jnp.where(row == rows - 1, next_row, pltpu.roll(m, rows - 1, 0))
    return m_prev, m_next


def _conv_fwd(proj_a, conv_w, dm):
    lp, cw, rc = dm.LP, dm.CW, _conv_rows(dm)
    nchunk = lp // rc

    def body(p_ref, w_ref, y_ref):
        w0, w1, w2 = w_ref[0:1, :], w_ref[1:2, :], w_ref[2:3, :]

        def chunk(ci, carry):
            r0 = pl.multiple_of(ci * rc, rc)
            blk = p_ref[pl.ds(r0, rc), :].astype(F32)
            cb, cc, cx, cz = (blk[:, i * cw:(i + 1) * cw] for i in range(4))
            m = cc * cx
            rp = pl.multiple_of(jnp.maximum(r0 - 16, 0), 16)
            rn = pl.multiple_of(jnp.minimum(r0 + rc, lp - 16), 16)
            pv = p_ref[pl.ds(rp, 16), cw:3 * cw].astype(F32)
            nx = p_ref[pl.ds(rn, 16), cw:3 * cw].astype(F32)
            prev_row = jnp.where(ci > 0, pv[15:16, :cw] * pv[15:16, cw:], 0.0)
            next_row = jnp.where(ci < nchunk - 1, nx[0:1, :cw] * nx[0:1, cw:], 0.0)
            m_prev, m_next = _shifted(m, prev_row, next_row, rc)
            s = w0 * m_prev + w1 * m + w2 * m_next
            y_ref[pl.ds(r0, rc), :] = (cb * s * (cz * _sigmoid(cz))).astype(BF16)
            return carry

        lax.fori_loop(0, nchunk, chunk, 0)

    return pl.pallas_call(
        body, name="conv_fwd", grid=(dm.Bl, dm.NJ),
        in_specs=[pl.BlockSpec((lp, 4 * cw), lambda s, j: (s, j)), pl.BlockSpec((3, cw), lambda s, j: (0, j))],
        out_specs=pl.BlockSpec((lp, cw), lambda s, j: (s, j)),
        out_shape=jax.ShapeDtypeStruct((dm.T, dm.D), BF16), compiler_params=_cp(2),
    )(proj_a, conv_w)


def _conv_bwd(proj_a, dy_conv, conv_w, dm):
    lp, cw, rc = dm.LP, dm.CW, _conv_rows(dm)
    nchunk = lp // rc

    def body(p_ref, dy_ref, w_ref, d_ref, gw_ref):
        w0, w1, w2 = w_ref[0:1, :], w_ref[1:2, :], w_ref[2:3, :]

        def ds_of(p4, dy):
            cb, cz = p4[:, :cw], p4[:, 3 * cw:]
            return dy * cb * (cz * _sigmoid(cz))

        def chunk(ci, carry):
            g0, g1, g2 = carry
            r0 = pl.multiple_of(ci * rc, rc)
            blk = p_ref[pl.ds(r0, rc), :].astype(F32)
            dy = dy_ref[pl.ds(r0, rc), :].astype(F32)
            cb, cc, cx, cz = (blk[:, i * cw:(i + 1) * cw] for i in range(4))
            rp = pl.multiple_of(jnp.maximum(r0 - 16, 0), 16)
            rn = pl.multiple_of(jnp.minimum(r0 + rc, lp - 16), 16)
            pv = p_ref[pl.ds(rp, 16), :].astype(F32)[15:16]
            nx = p_ref[pl.ds(rn, 16), :].astype(F32)[0:1]
            dpv = dy_ref[pl.ds(rp, 16), :].astype(F32)[15:16]
            dnx = dy_ref[pl.ds(rn, 16), :].astype(F32)[0:1]
            has_prev, has_next = ci > 0, ci < nchunk - 1
            m = cc * cx
            m_prev, m_next = _shifted(m, jnp.where(has_prev, pv[:, cw:2 * cw] * pv[:, 2 * cw:3 * cw], 0.0),
                                      jnp.where(has_next, nx[:, cw:2 * cw] * nx[:, 2 * cw:3 * cw], 0.0), rc)
            s = w0 * m_prev + w1 * m + w2 * m_next
            sg = _sigmoid(cz)
            silu = cz * sg
            ds = dy * cb * silu
            ds_prev, ds_next = _shifted(ds, jnp.where(has_prev, ds_of(pv, dpv), 0.0),
                                        jnp.where(has_next, ds_of(nx, dnx), 0.0), rc)
            dm_ = w0 * ds_next + w1 * ds + w2 * ds_prev
            d_ref[pl.ds(r0, rc), 0:cw] = (dy * s * silu).astype(BF16)
            d_ref[pl.ds(r0, rc), cw:2 * cw] = (dm_ * cx).astype(BF16)
            d_ref[pl.ds(r0, rc), 2 * cw:3 * cw] = (dm_ * cc).astype(BF16)
            d_ref[pl.ds(r0, rc), 3 * cw:4 * cw] = (dy * cb * s * (sg * (1.0 + cz * (1.0 - sg)))).astype(BF16)
            return (g0 + jnp.sum(ds * m_prev, axis=0, keepdims=True), g1 + jnp.sum(ds * m, axis=0, keepdims=True),
                    g2 + jnp.sum(ds * m_next, axis=0, keepdims=True))

        z = jnp.zeros((1, cw), F32)
        g0, g1, g2 = lax.fori_loop(0, nchunk, chunk, (z, z, z))

        @pl.when(pl.program_id(1) == 0)
        def _():
            gw_ref[...] = jnp.zeros_like(gw_ref)

        gw_ref[0:1, :] += g0
        gw_ref[1:2, :] += g1
        gw_ref[2:3, :] += g2

    return pl.pallas_call(
        body, name="conv_bwd", grid=(dm.NJ, dm.Bl),
        in_specs=[pl.BlockSpec((lp, 4 * cw), lambda j, s: (s, j)), pl.BlockSpec((lp, cw), lambda j, s: (s, j)),
                  pl.BlockSpec((3, cw), lambda j, s: (0, j))],
        out_specs=[pl.BlockSpec((lp, 4 * cw), lambda j, s: (s, j)), pl.BlockSpec((8, cw), lambda j, s: (0, j))],
        out_shape=[jax.ShapeDtypeStruct((dm.T, 4 * dm.D), BF16), jax.ShapeDtypeStruct((8, dm.D), F32)],
        compiler_params=_cp(2),
    )(proj_a, dy_conv, conv_w)


def _interleave(gens):
    results = [None] * len(gens)
    live = list(range(len(gens)))
    while live:
        for idx in list(live):
            try:
                next(gens[idx])
            except StopIteration as done:
                results[idx] = done.value
                live.remove(idx)
    return results


def _group_chunks(dm):
    n = dm.NC - dm.C0
    return 3 if n % 3 == 0 else 1


def _group_masks(rows):
    ii = lax.broadcasted_iota(jnp.int32, (rows, rows), 0)
    jj = lax.broadcasted_iota(jnp.int32, (rows, rows), 1)
    same = jnp.right_shift(ii, CHUNK_SHIFT) == jnp.right_shift(jj, CHUNK_SHIFT)
    low, up = same & (jj <= ii), same & (jj >= ii)
    return low, same & (jj > ii), low.astype(BF16), up.astype(BF16)


def _first_row(chunk):
    return chunk * CHUNK if isinstance(chunk, int) else pl.multiple_of(chunk * CHUNK, CHUNK)


def _chunk_totals(b, fwd):
    hk = b.shape[1]
    rows = [b[c * CHUNK + CHUNK - 1:(c + 1) * CHUNK] if fwd else b[c * CHUNK:c * CHUNK + 1]
            for c in range(b.shape[0] // CHUNK)]
    return jnp.concatenate([jnp.broadcast_to(r, (CHUNK, hk)) for r in rows], axis=0)


def _log_gate(lr_rows, w_ref, b_ref, first_group, hk):
    z = _dot(lr_rows, w_ref[...]) + b_ref[...]
    e = jnp.exp(-jnp.abs(z))
    g = (jnp.minimum(z, 0.0) - jnp.log(1.0 + e)) * (1.0 / GATE_NORMALIZER)
    dg_dz = jnp.where(z >= 0.0, e, 1.0) / (1.0 + e) * (1.0 / GATE_NORMALIZER)
    row = lax.broadcasted_iota(jnp.int32, (lr_rows.shape[0], hk), 0)
    pad = first_group & (row < PAD_ROWS)
    return jnp.where(pad, 0.0, g), jnp.where(pad, 0.0, dg_dz)


def _gla_fwd(proj_b, lr, wg_f, bg_f, wg_b, bg_b, gla_g, dm):
    lp, hk, hv, nc, c0, hw = dm.LP, dm.HK, dm.HV, dm.NC, dm.C0, dm.HW
    scale = hk ** -0.5
    gc = _group_chunks(dm)
    gr, ng = gc * CHUNK, (nc - c0) // gc

    def body(p_ref, lr_ref, wf_ref, bf_ref, wb_ref, bb_ref, gg_ref, o_ref, y_ref, st_ref, b_out, gs_out, oacc_f, oacc_b):
        low_incl, up_strict, ones_low, ones_up = _group_masks(gr)
        if c0 > 0:
            zr = c0 * CHUNK
            o_ref[0:zr, :] = jnp.zeros((zr, hv), BF16)
            y_ref[0:zr, :] = jnp.zeros((zr, hv), BF16)
            b_out[:, 0:zr, :] = jnp.zeros((2, zr, hk), F32)
            gs_out[:, 0:zr, :] = jnp.zeros((2, zr, hk), F32)
            st_ref[0, 0, :, 0:c0] = jnp.zeros((2, c0, hv, hk), BF16)

        def decay(gi, fwd):
            w_ref, b_ref = (wf_ref, bf_ref) if fwd else (wb_ref, bb_ref)
            r0 = _first_row(c0 + gi * gc)
            yield
            g, dg_dz = _log_gate(lr_ref[pl.ds(r0, gr), :], w_ref, b_ref, gi == 0, hk)
            gs_out[0 if fwd else 1, pl.ds(r0, gr), :] = dg_dz
            yield
            b = _dot_exact01(ones_low if fwd else ones_up, g)
            b_out[0 if fwd else 1, pl.ds(r0, gr), :] = b
            return b

        def group(gi, st, b, fwd):
            oacc = oacc_f if fwd else oacc_b
            r0 = pl.multiple_of((c0 + gi * gc) * CHUNK, CHUNK)
            blk = p_ref[pl.ds(r0, gr), :]
            q = blk[:, :hk].astype(F32) * scale
            k = blk[:, hk:2 * hk].astype(F32)
            v = blk[:, 2 * hk:2 * hk + hv]
            btot = _chunk_totals(b, fwd)
            qi = (q * jnp.exp(b)).astype(BF16)
            ki = (k * jnp.exp(-b)).astype(BF16)
            kd = (k * jnp.exp(btot - b)).astype(BF16)
            dec = jnp.exp(btot)
            a = _dot_nt(qi, ki)
            yield
            o = _dot(jnp.where(low_incl if fwd else up_strict, a, 0.0).astype(BF16), v)
            chunk_rows = [slice(c * CHUNK, (c + 1) * CHUNK) for c in range(gc)]
            kv = [_dot_tn(v[rows], kd[rows]) for rows in chunk_rows]
            for c in (range(gc) if fwd else reversed(range(gc))):
                yield
                rows = chunk_rows[c]
                st_b = st.astype(BF16)
                st_ref[0, 0, 0 if fwd else 1, c0 + gi * gc + c] = st_b
                oacc[pl.ds(r0 + c * CHUNK, CHUNK), :] = o[rows] + _dot_nt(qi[rows], st_b)
                st = st * dec[c * CHUNK:c * CHUNK + 1] + kv[c]
            return st

        def step(i, carry):
            st_f, st_b, b_f, b_b = carry
            gf, gb = i, ng - 1 - i
            return tuple(_interleave([group(gf, st_f, b_f, True), group(gb, st_b, b_b, False),
                                      decay(jnp.minimum(gf + 1, ng - 1), True), decay(jnp.maximum(gb - 1, 0), False)]))

        zero = jnp.zeros((hv, hk), F32)
        lax.fori_loop(0, ng, step, (zero, zero, *_interleave([decay(0, True), decay(ng - 1, False)])))

        def finish(i, carry):
            r0 = pl.multiple_of((c0 + i * gc) * CHUNK, CHUNK)
            o = oacc_f[pl.ds(r0, gr), :] + oacc_b[pl.ds(r0, gr), :]
            r = p_ref[pl.ds(r0, gr), 2 * hk + hv:].astype(F32)
            on = o * lax.rsqrt(jnp.mean(o * o, axis=-1, keepdims=True) + EPS) * gg_ref[...]
            o_ref[pl.ds(r0, gr), :] = o.astype(BF16)
            y_ref[pl.ds(r0, gr), :] = (on * r * _sigmoid(r)).astype(BF16)
            return carry

        lax.fori_loop(0, ng, finish, 0)

    head = lambda s, h: (s, h)
    wspec = pl.BlockSpec((LR_LANES, hk), lambda s, h: (0, h))
    bspec = pl.BlockSpec((1, hk), lambda s, h: (0, h))
    return pl.pallas_call(
        body, name="gla_fwd", grid=(dm.Bl, HEADS),
        in_specs=[pl.BlockSpec((lp, hw), head), pl.BlockSpec((lp, LR_LANES), lambda s, h: (s, 0)),
                  wspec, bspec, wspec, bspec, pl.BlockSpec((1, hv), lambda s, h: (0, 0))],
        out_specs=[pl.BlockSpec((lp, hv), head), pl.BlockSpec((lp, hv), head),
                   pl.BlockSpec((1, 1, 2, nc, hv, hk), lambda s, h: (s, h, 0, 0, 0, 0)),
                   pl.BlockSpec((2, lp, hk), lambda s, h: (0, s, h)), pl.BlockSpec((2, lp, hk), lambda s, h: (0, s, h))],
        out_shape=[jax.ShapeDtypeStruct((dm.T, dm.DV), BF16), jax.ShapeDtypeStruct((dm.T, dm.DV), BF16),
                   jax.ShapeDtypeStruct((dm.Bl, HEADS, 2, nc, hv, hk), BF16),
                   jax.ShapeDtypeStruct((2, dm.T, dm.DK), F32), jax.ShapeDtypeStruct((2, dm.T, dm.DK), F32)],
        scratch_shapes=[pltpu.VMEM((lp, hv), F32), pltpu.VMEM((lp, hv), F32)],
        compiler_params=_cp(2),
    )(proj_b, lr, wg_f, bg_f, wg_b, bg_b, gla_g)


def _gla_bwd(proj_b, lr, o_all, dy_gla, states, decays, gate_slopes, wg_f, wg_b, gla_g, dm):
    lp, hk, hv, nc, c0, hw = dm.LP, dm.HK, dm.HV, dm.NC, dm.C0, dm.HW
    scale = hk ** -0.5
    gc = _group_chunks(dm)
    gr, ng = gc * CHUNK, (nc - c0) // gc

    def body(p_ref, lr_ref, o_ref, dy_ref, st_ref, b_ref, gs_ref, wf_ref, wb_ref, gg_ref,
             d_ref, dlr_ref, gwf_ref, gbf_ref, gwb_ref, gbb_ref, ggg_ref, do_s, dq_s, dk_s, dv_s, dlr_s):
        low_incl, up_strict, ones_low, ones_up = _group_masks(gr)
        h = pl.program_id(1)

        @pl.when(h == 0)
        def _():
            dlr_ref[...] = jnp.zeros_like(dlr_ref)

        if c0 > 0:
            zr = c0 * CHUNK
            d_ref[0:zr, :] = jnp.zeros((zr, hw), BF16)
        for acc in (dq_s, dk_s, dv_s, dlr_s):
            acc[...] = jnp.zeros_like(acc)

        def norm_bwd(i, ggg):
            r0 = pl.multiple_of((c0 + i * gc) * CHUNK, CHUNK)
            o = o_ref[pl.ds(r0, gr), :].astype(F32)
            dy = dy_ref[pl.ds(r0, gr), :].astype(F32)
            r = p_ref[pl.ds(r0, gr), 2 * hk + hv:].astype(F32)
            rstd = lax.rsqrt(jnp.mean(o * o, axis=-1, keepdims=True) + EPS)
            ohat = o * rstd
            sg = _sigmoid(r)
            d_on = dy * (r * sg)
            d_ref[pl.ds(r0, gr), 2 * hk + hv:] = (dy * ohat * gg_ref[...] * (sg * (1.0 + r * (1.0 - sg)))).astype(BF16)
            d_oh = d_on * gg_ref[...]
            do_s[pl.ds(r0, gr), :] = (rstd * (d_oh - ohat * jnp.mean(d_oh * ohat, axis=-1, keepdims=True))).astype(BF16)
            return ggg + jnp.sum(d_on * ohat, axis=0, keepdims=True)

        ggg = lax.fori_loop(0, ng, norm_bwd, jnp.zeros((1, hv), F32))

        @pl.when((pl.program_id(0) == 0) & (h == 0))
        def _():
            ggg_ref[...] = jnp.zeros_like(ggg_ref)

        ggg_ref[0:1, :] += ggg

        def load(gi):
            r0 = pl.multiple_of((c0 + gi * gc) * CHUNK, CHUNK)
            blk = p_ref[pl.ds(r0, gr), :]
            return r0, blk[:, :hk].astype(F32) * scale, blk[:, hk:2 * hk].astype(F32), blk[:, 2 * hk:2 * hk + hv]

        zero = jnp.zeros((hv, hk), F32)

        def grad(gi, carry, fwd):
            dst, gw, gb = carry
            w_ref, way = (wf_ref, 0) if fwd else (wb_ref, 1)
            mask = low_incl if fwd else up_strict
            r0, q, k, v = load(gi)
            b = b_ref[way, pl.ds(r0, gr), :]
            btot = _chunk_totals(b, fwd)
            eb, enb, edb, dec = jnp.exp(b), jnp.exp(-b), jnp.exp(btot - b), jnp.exp(btot)
            qi_f, ki_f, kd_f = q * eb, k * enb, k * edb
            qi, ki, kd = qi_f.astype(BF16), ki_f.astype(BF16), kd_f.astype(BF16)
            do = do_s[pl.ds(r0, gr), :]
            a = _dot_nt(qi, ki)
            da = _dot_nt(do, v)
            yield
            a = jnp.where(mask, a, 0.0).astype(BF16)
            da = jnp.where(mask, da, 0.0).astype(BF16)
            dv = _dot_tn(a, do)
            dqi = _dot(da, ki)
            dki = _dot_tn(da, qi)
            dv_c, dqi_c, dkd_c, extra_c = [None] * gc, [None] * gc, [None] * gc, [None] * gc
            chunk_rows = [slice(c * CHUNK, (c + 1) * CHUNK) for c in range(gc)]
            qdo = [_dot_tn(do[rows], qi[rows]) for rows in chunk_rows]
            for c in (reversed(range(gc)) if fwd else range(gc)):
                yield
                rows = chunk_rows[c]
                st = st_ref[0, 0, way, c0 + gi * gc + c]
                dsn_b = dst.astype(BF16)
                dec_c = dec[c * CHUNK:c * CHUNK + 1]
                dv_c[c] = dv[rows] + _dot_nt(kd[rows], dsn_b)
                dqi_c[c] = dqi[rows] + _dot(do[rows], st)
                dkd_c[c] = _dot(v[rows], dsn_b)
                ddec = jnp.sum(st.astype(F32) * dst, axis=0, keepdims=True)
                extra = jnp.sum(dkd_c[c] * kd_f[rows], axis=0, keepdims=True) + ddec * dec_c
                extra_c[c] = jnp.broadcast_to(extra, (CHUNK, hk))
                dst = dst * dec_c + qdo[c]
            yield
            dv, dqi = jnp.concatenate(dv_c, axis=0), jnp.concatenate(dqi_c, axis=0)
            dkd, extra = jnp.concatenate(dkd_c, axis=0), jnp.concatenate(extra_c, axis=0)
            dq_s[pl.ds(r0, gr), :] += dqi * eb * scale
            dk_s[pl.ds(r0, gr), :] += dki * enb + dkd * edb
            dv_s[pl.ds(r0, gr), :] += dv
            db = dqi * qi_f - dki * ki_f - dkd * kd_f
            dg = _dot_exact01(ones_up if fwd else ones_low, db) + extra
            yield
            dz = dg * gs_ref[way, pl.ds(r0, gr), :]
            dz_b = dz.astype(BF16)
            dlr_s[pl.ds(r0, gr), :] += _dot_nt(dz_b, w_ref[...])
            return dst, gw + _dot_tn(lr_ref[pl.ds(r0, gr), :], dz_b), gb + jnp.sum(dz, axis=0, keepdims=True)

        def grad_step(i, carry):
            return tuple(_interleave([grad(ng - 1 - i, carry[0], True), grad(i, carry[1], False)]))

        init = (zero, jnp.zeros((LR_LANES, hk), F32), jnp.zeros((1, hk), F32))
        (_, gw_f, gb_f), (_, gw_b, gb_b) = lax.fori_loop(0, ng, grad_step, (init, init))
        for gw_ref, gb_ref, gw, gb in ((gwf_ref, gbf_ref, gw_f, gb_f), (gwb_ref, gbb_ref, gw_b, gb_b)):
            gw_ref[0] = gw
            gb_ref[0] = jnp.zeros((8, hk), F32)
            gb_ref[0, 0:1, :] = gb

        def combine(i, carry):
            r0 = pl.multiple_of((c0 + i * gc) * CHUNK, CHUNK)
            d_ref[pl.ds(r0, gr), 0:hk] = dq_s[pl.ds(r0, gr), :].astype(BF16)
            d_ref[pl.ds(r0, gr), hk:2 * hk] = dk_s[pl.ds(r0, gr), :].astype(BF16)
            d_ref[pl.ds(r0, gr), 2 * hk:2 * hk + hv] = dv_s[pl.ds(r0, gr), :].astype(BF16)
            dlr_ref[pl.ds(r0, gr), :] += dlr_s[pl.ds(r0, gr), :]
            return carry

        lax.fori_loop(0, ng, combine, 0)

    head = lambda s, h: (s, h)
    wspec = pl.BlockSpec((LR_LANES, hk), lambda s, h: (0, h))
    gwspec = pl.BlockSpec((1, LR_LANES, hk), lambda s, h: (s, 0, h))
    gbspec = pl.BlockSpec((1, 8, hk), lambda s, h: (s, 0, h))
    gw_shape = jax.ShapeDtypeStruct((dm.Bl, LR_LANES, dm.DK), F32)
    gb_shape = jax.ShapeDtypeStruct((dm.Bl, 8, dm.DK), F32)
    both = pl.BlockSpec((2, lp, hk), lambda s, h: (0, s, h))
    return pl.pallas_call(
        body, name="gla_bwd", grid=(dm.Bl, HEADS),
        in_specs=[pl.BlockSpec((lp, hw), head), pl.BlockSpec((lp, LR_LANES), lambda s, h: (s, 0)),
                  pl.BlockSpec((lp, hv), head), pl.BlockSpec((lp, hv), head),
                  pl.BlockSpec((1, 1, 2, nc, hv, hk), lambda s, h: (s, h, 0, 0, 0, 0)), both, both,
                  wspec, wspec, pl.BlockSpec((1, hv), lambda s, h: (0, 0))],
        out_specs=[pl.BlockSpec((lp, hw), head), pl.BlockSpec((lp, LR_LANES), lambda s, h: (s, 0)),
                   gwspec, gbspec, gwspec, gbspec, pl.BlockSpec((8, hv), lambda s, h: (0, 0))],
        out_shape=[jax.ShapeDtypeStruct((dm.T, HEADS * hw), BF16), jax.ShapeDtypeStruct((dm.T, LR_LANES), F32),
                   gw_shape, gb_shape, gw_shape, gb_shape, jax.ShapeDtypeStruct((8, hv), F32)],
        scratch_shapes=[pltpu.VMEM((lp, hv), BF16), pltpu.VMEM((lp, hk), F32), pltpu.VMEM((lp, hk), F32),
                        pltpu.VMEM((lp, hv), F32), pltpu.VMEM((lp, LR_LANES), F32)],
        compiler_params=_cp(2),
    )(proj_b, lr, o_all, dy_gla, states, decays, gate_slopes, wg_f, wg_b, gla_g)


def _stream_tiles(n_tiles, loads, stores, compute):
    for cp in loads(0, 0):
        cp.start()

    def step(t, carry):
        slot = t % 2

        @pl.when(t + 1 < n_tiles)
        def _():
            for cp in loads(t + 1, 1 - slot):
                cp.start()

        for cp in loads(t, slot):
            cp.wait()

        @pl.when(t >= 2)
        def _():
            for cp in stores(t - 2, slot):
                cp.wait()

        compute(t, slot)
        for cp in stores(t, slot):
            cp.start()
        return carry

    lax.fori_loop(0, n_tiles, step, 0)
    for t in range(max(n_tiles - 2, 0), n_tiles):
        for cp in stores(t, t % 2):
            cp.wait()


def _token_tiles(dm, target_rows=512):
    rows = _pick(dm.S, target_rows, 16)
    per_seq = dm.S // rows
    return rows, dm.Bl * per_seq, lambda t: pl.multiple_of((t // per_seq) * dm.LP + dm.TM + (t % per_seq) * rows, 16)


def _head(y_conv, y_gla, proj_c, w_oc, w_og, w_out, x, target, g_post, dm):
    d, tm = dm.D, dm.TM
    rows, n_tiles, first_row = _token_tiles(dm, 256)
    n_out = 8

    def body(*refs):
        yc_hbm, yg_hbm, c_hbm, woc_ref, wog_ref, wo_ref, x_hbm, t_hbm, g_ref = refs[:9]
        outs, st_ref = refs[9:9 + n_out], refs[9 + n_out]
        ycbuf, ygbuf, cbuf, xbuf, tbuf = refs[10 + n_out:15 + n_out]
        obufs = refs[15 + n_out:15 + 2 * n_out]
        zbuf, zbuf2, sem_in, sem_out, sem_zero = refs[15 + 2 * n_out:]

        def loads(t, slot):
            padded = [(yc_hbm, ycbuf), (yg_hbm, ygbuf), (c_hbm, cbuf)]
            own = [(x_hbm, xbuf), (t_hbm, tbuf)]
            return ([pltpu.make_async_copy(h.at[pl.ds(first_row(t), rows), :], b.at[slot], sem_in.at[i, slot])
                     for i, (h, b) in enumerate(padded)] +
                    [pltpu.make_async_copy(h.at[pl.ds(t * rows, rows), :], b.at[slot], sem_in.at[3 + i, slot])
                     for i, (h, b) in enumerate(own)])

        def stores(t, slot):
            return [pltpu.make_async_copy(b.at[slot], h.at[pl.ds(first_row(t), rows), :], sem_out.at[i, slot])
                    for i, (h, b) in enumerate(zip(outs, obufs))]

        def compute(t, slot):
            mg_o, do_o, dy_o, dpc_o, dpg_o, dc_o, dyc_o, dyg_o = obufs
            pc = _dot(ycbuf[slot], woc_ref[...])
            pg = _dot(ygbuf[slot], wog_ref[...])
            sa = _sigmoid(cbuf[slot, :, :d].astype(F32))
            sb = _sigmoid(cbuf[slot, :, d:].astype(F32))
            merged = (sa * pc + sb * pg).astype(BF16)
            mg_o[slot] = merged
            out = _dot(merged, wo_ref[...])
            rstd = lax.rsqrt(jnp.mean(out * out, axis=-1, keepdims=True) + EPS)
            ohat = out * rstd
            err = xbuf[slot] + ohat * g_ref[...] - tbuf[slot]
            dy = err * (1.0 / d)
            d_oh = dy * g_ref[...]
            d_out = (rstd * (d_oh - ohat * jnp.mean(d_oh * ohat, axis=-1, keepdims=True))).astype(BF16)
            do_o[slot] = d_out
            dy_o[slot] = dy.astype(BF16)
            st_ref[0:1, :] += jnp.sum(dy * ohat, axis=0, keepdims=True)
            st_ref[1:2, :] += jnp.sum(err * err, axis=0, keepdims=True)
            dmg = _dot_nt(d_out, wo_ref[...])
            dpc = (dmg * sa).astype(BF16)
            dpg = (dmg * sb).astype(BF16)
            dpc_o[slot] = dpc
            dpg_o[slot] = dpg
            dc_o[slot, :, :d] = (dmg * pc * sa * (1.0 - sa)).astype(BF16)
            dc_o[slot, :, d:] = (dmg * pg * sb * (1.0 - sb)).astype(BF16)
            dyc_o[slot] = _dot_nt(dpc, woc_ref[...]).astype(BF16)
            dyg_o[slot] = _dot_nt(dpg, wog_ref[...]).astype(BF16)

        st_ref[...] = jnp.zeros_like(st_ref)
        zbuf[...] = jnp.zeros_like(zbuf)
        zbuf2[...] = jnp.zeros_like(zbuf2)
        zeros = [pltpu.make_async_copy(zbuf2 if out.shape[1] == 2 * d else zbuf, out.at[pl.ds(b * dm.LP, tm), :], sem_zero.at[i, b])
                 for i, out in enumerate(outs) for b in range(dm.Bl)]
        for cp in zeros:
            cp.start()
        _stream_tiles(n_tiles, loads, stores, compute)
        for cp in zeros:
            cp.wait()

    any_spec, vmem = pl.BlockSpec(memory_space=pl.ANY), pl.BlockSpec(memory_space=pltpu.VMEM)
    widths = [d, d, d, d, d, 2 * d, d, d]
    tile = lambda w, dt: pltpu.VMEM((2, rows, w), dt)
    return pl.pallas_call(
        body, name="head", in_specs=[any_spec] * 3 + [vmem] * 3 + [any_spec] * 2 + [vmem],
        out_specs=[any_spec] * n_out + [vmem],
        out_shape=[jax.ShapeDtypeStruct((dm.T, w), BF16) for w in widths] + [jax.ShapeDtypeStruct((8, d), F32)],
        scratch_shapes=[tile(d, BF16), tile(d, BF16), tile(2 * d, BF16), tile(d, F32), tile(d, F32)]
        + [tile(w, BF16) for w in widths]
        + [pltpu.VMEM((tm, d), BF16), pltpu.VMEM((tm, 2 * d), BF16), pltpu.SemaphoreType.DMA((5, 2)),
           pltpu.SemaphoreType.DMA((n_out, 2)), pltpu.SemaphoreType.DMA((n_out, dm.Bl))],
        compiler_params=pltpu.CompilerParams(vmem_limit_bytes=VMEM_LIMIT_BYTES),
    )(y_conv, y_gla, proj_c, w_oc, w_og, w_out, x.reshape(dm.Bl * dm.S, d), target.reshape(dm.Bl * dm.S, d), g_post)


def _prenorm_bwd(du, dy, x, metapad, g_pre, dm):
    d, tm = dm.D, dm.TM
    rows, n_tiles, first_row = _token_tiles(dm)

    def body(du_hbm, dy_hbm, x_hbm, mp_ref, g_ref, gx_hbm, dmeta_ref, gg_ref, dubuf, dybuf, xbuf, gbuf, mbuf,
             sem_in, sem_out, sem_meta):
        def norm_bwd(h, du, dy):
            rstd = lax.rsqrt(jnp.mean(h * h, axis=-1, keepdims=True) + EPS)
            hhat = h * rstd
            dug = du * g_ref[...]
            gg_ref[0:1, :] += jnp.sum(du * hhat, axis=0, keepdims=True)
            return dy + rstd * (dug - hhat * jnp.mean(dug * hhat, axis=-1, keepdims=True))

        def loads(t, slot):
            return [pltpu.make_async_copy(du_hbm.at[pl.ds(first_row(t), rows), :], dubuf.at[slot], sem_in.at[0, slot]),
                    pltpu.make_async_copy(dy_hbm.at[pl.ds(first_row(t), rows), :], dybuf.at[slot], sem_in.at[1, slot]),
                    pltpu.make_async_copy(x_hbm.at[pl.ds(t * rows, rows), :], xbuf.at[slot], sem_in.at[2, slot])]

        def stores(t, slot):
            return [pltpu.make_async_copy(gbuf.at[slot], gx_hbm.at[pl.ds(t * rows, rows), :], sem_out.at[slot])]

        def compute(t, slot):
            gbuf[slot] = norm_bwd(xbuf[slot], dubuf[slot].astype(F32), dybuf[slot].astype(F32))

        gg_ref[...] = jnp.zeros_like(gg_ref)
        meta = [pltpu.make_async_copy(du_hbm.at[pl.ds(b * dm.LP, tm), :], mbuf.at[b], sem_meta.at[b]) for b in range(dm.Bl)]
        for cp in meta:
            cp.start()
        _stream_tiles(n_tiles, loads, stores, compute)
        for b, cp in enumerate(meta):
            cp.wait()
            dmeta_ref[b] = norm_bwd(mp_ref[...], mbuf[b].astype(F32), 0.0)

    any_spec, vmem = pl.BlockSpec(memory_space=pl.ANY), pl.BlockSpec(memory_space=pltpu.VMEM)
    grad_x, d_meta, gg = pl.pallas_call(
        body, name="prenorm_bwd", in_specs=[any_spec, any_spec, any_spec, vmem, vmem], out_specs=[any_spec, vmem, vmem],
        out_shape=[jax.ShapeDtypeStruct((dm.Bl * dm.S, d), F32), jax.ShapeDtypeStruct((dm.Bl, tm, d), F32),
                   jax.ShapeDtypeStruct((8, d), F32)],
        scratch_shapes=[pltpu.VMEM((2, rows, d), BF16), pltpu.VMEM((2, rows, d), BF16), pltpu.VMEM((2, rows, d), F32),
                        pltpu.VMEM((2, rows, d), F32), pltpu.VMEM((dm.Bl, tm, d), BF16),
                        pltpu.SemaphoreType.DMA((3, 2)), pltpu.SemaphoreType.DMA((2,)), pltpu.SemaphoreType.DMA((dm.Bl,))],
        compiler_params=pltpu.CompilerParams(vmem_limit_bytes=VMEM_LIMIT_BYTES),
    )(du, dy, x.reshape(dm.Bl * dm.S, d), metapad, g_pre)
    return grad_x.reshape(dm.Bl, dm.S, d), d_meta, gg


def _adamw(partials, w, m, v, name, by_columns=False):
    r, c = w.shape
    n_parts = partials.shape[0]
    tr, tc = (r, _pick(c, 128, 128)) if by_columns else (_pick(r, 256, 16), c)

    def body(p_ref, w_ref, m_ref, v_ref, g_ref, d_ref, nm_ref, nv_ref):
        g = p_ref[0].astype(F32)
        for j in range(1, n_parts):
            g = g + p_ref[j].astype(F32)
        g_ref[...] = g
        d_ref[...], nm_ref[...], nv_ref[...] = _adam_step(g, w_ref[...], m_ref[...], v_ref[...])

    at = (lambda i: (0, i)) if by_columns else (lambda i: (i, 0))
    tile = pl.BlockSpec((tr, tc), at)
    out = jax.ShapeDtypeStruct((r, c), F32)
    return pl.pallas_call(
        body, name=name, grid=(c // tc if by_columns else r // tr,),
        in_specs=[pl.BlockSpec((n_parts, tr, tc), lambda i: (0,) + at(i)), tile, tile, tile],
        out_specs=[tile, tile, tile, tile], out_shape=[out, out, out, out], compiler_params=_cp(1),
    )(partials, w, m, v)


def _adam_step(g, w, m, v):
    m2 = ADAM_B1 * m + (1.0 - ADAM_B1) * g
    v2 = ADAM_B2 * v + (1.0 - ADAM_B2) * (g * g)
    m_hat = m2 / (1.0 - ADAM_B1 ** ADAM_STEP)
    v_hat = v2 / (1.0 - ADAM_B2 ** ADAM_STEP)
    return -ADAM_LR * (m_hat / (jnp.sqrt(v_hat) + ADAM_EPS) + ADAM_WD * w), m2, v2


def _adamw_small(items, name):
    n = len(items)

    def body(*refs):
        ins, outs = refs[:4 * n], refs[4 * n:]
        for i in range(n):
            p_ref, w_ref, m_ref, v_ref = ins[4 * i:4 * i + 4]
            g = p_ref[0]
            for j in range(1, p_ref.shape[0]):
                g = g + p_ref[j]
            delta, m2, v2 = _adam_step(g, w_ref[...], m_ref[...], v_ref[...])
            for o_ref, val in zip(outs[4 * i:4 * i + 4], (g, delta, m2, v2)):
                o_ref[...] = val

    vmem = pl.BlockSpec(memory_space=pltpu.VMEM)
    res = pl.pallas_call(
        body, name=name, in_specs=[vmem] * (4 * n), out_specs=[vmem] * (4 * n),
        out_shape=[jax.ShapeDtypeStruct(w.shape, F32) for _, w, _, _ in items for _ in range(4)],
    )(*[a for item in items for a in item])
    return [res[4 * i:4 * i + 4] for i in range(n)]


def _unpack_rows(a, b, c, lr, dm):
    d, hk, hv, cw, nj, hw = dm.D, dm.HK, dm.HV, dm.CW, dm.NJ, dm.HW
    conv = a.reshape(nj, 4, cw, d).transpose(1, 0, 2, 3).reshape(4 * d, d)
    heads = b.reshape(HEADS, hw, d)
    q = heads[:, :hk].reshape(HEADS * hk, d)
    k = heads[:, hk:2 * hk].reshape(HEADS * hk, d)
    v = heads[:, 2 * hk:2 * hk + hv].reshape(HEADS * hv, d)
    r = heads[:, 2 * hk + hv:].reshape(HEADS * hv, d)
    return jnp.concatenate([conv, q, k, v, r, lr[:2 * RANK], c], axis=0)


def _column_shards(g, shard_shape):
    r, c = g.shape
    return g.reshape(r, N_DEV, c // N_DEV).transpose(1, 0, 2).reshape((N_DEV,) + tuple(shard_shape))


def _join_column_shards(parts):
    r, c = parts.shape[-2:]
    return parts.reshape(N_DEV, r, c).transpose(1, 0, 2).reshape(r, N_DEV * c)


def _local_step(x, target, meta, g_pre, u, wt_shards, conv_w, wg_f, bg_f, wg_b, bg_b, gla_g, out_weights, g_post,
                on_matrix_grads=None):
    bl, s, d = x.shape
    dm = _Dims(bl, s, d)
    metapad = jnp.concatenate([jnp.zeros((dm.TM - N_META, d), F32), meta], axis=0)
    wgp_f = jnp.pad(wg_f, ((0, LR_LANES - RANK), (0, 0))).astype(BF16)
    wgp_b = jnp.pad(wg_b, ((RANK, LR_LANES - 2 * RANK), (0, 0))).astype(BF16)

    u = _prenorm_meta(u, metapad, g_pre, dm)
    proj_a, proj_b, proj_c, lr = _inproj(u, wt_shards, dm)
    y_conv = _conv_fwd(proj_a, conv_w, dm)
    o_all, y_gla, states, decays, gate_slopes = _gla_fwd(proj_b, lr, wgp_f, bg_f, wgp_b, bg_b, gla_g, dm)
    w_oc, w_og, w_out = out_weights(y_conv) if callable(out_weights) else out_weights
    merged, d_out, dy, d_pc, d_pg, d_c, dy_conv, dy_gla, stats = _head(y_conv, y_gla, proj_c, w_oc, w_og, w_out, x, target,
                                                                        g_post, dm)
    loss = 0.5 / d * jnp.sum(stats[1])

    g_out = _matmul_tn(merged, d_out, BF16, "grad_w_out")
    g_oc = _matmul_tn(y_conv, d_pc, BF16, "grad_w_out_conv")
    g_og = _matmul_tn(y_gla, d_pg, BF16, "grad_w_out_gla")
    if on_matrix_grads is not None:
        conv_w = conv_w + on_matrix_grads(dict(w_out_conv=g_oc, w_out_gla=g_og, w_merge_out=g_out))
    d_a, g_conv = _conv_bwd(proj_a, dy_conv, conv_w, dm)
    d_b, d_lr, gwp_f, gbp_f, gwp_b, gbp_b, g_gla = _gla_bwd(proj_b, lr, o_all, dy_gla, states, decays, gate_slopes, wgp_f, wgp_b, gla_g, dm)
    g_in = _unpack_rows(_matmul_tn(d_a, u, BF16, "grad_w_in_conv"), _matmul_tn(d_b, u, BF16, "grad_w_in_gla"),
                        _matmul_tn(d_c, u, BF16, "grad_w_in_merge"), _matmul_tn(d_lr, u, BF16, "grad_w_in_gate"), dm)
    if on_matrix_grads is not None:
        d_lr = d_lr + on_matrix_grads(dict(w_in=g_in))
    du = _grad_u([d_a, d_b, d_c, d_lr], wt_shards, dm)
    grad_x, d_meta, g_pre_rows = _prenorm_bwd(du, dy, x, metapad, g_pre, dm)

    grads = dict(
        meta_tokens=jnp.sum(d_meta[:, dm.TM - N_META:, :], axis=0), norm_pre=g_pre_rows[0:1], w_in=g_in,
        conv_w=g_conv[0:3], w_gate_fwd=jnp.sum(gwp_f, axis=0)[:RANK], b_gate_fwd=jnp.sum(gbp_f, axis=0)[0:1],
        w_gate_bwd=jnp.sum(gwp_b, axis=0)[RANK:2 * RANK], b_gate_bwd=jnp.sum(gbp_b, axis=0)[0:1],
        gla_norm=g_gla[0:1], w_out_conv=g_oc, w_out_gla=g_og, w_merge_out=g_out, norm_post=stats[0:1])
    return loss, grad_x, grads


MATRICES = ("w_out_conv", "w_out_gla", "w_merge_out")
SMALL_SHARDED = ("meta_tokens", "conv_w", "w_gate_fwd", "w_gate_bwd")
REPLICATED = ("norm_pre", "b_gate_fwd", "b_gate_bwd", "gla_norm", "norm_post")
NAMES = ("meta_tokens", "norm_pre", "w_in", "conv_w", "w_gate_fwd", "b_gate_fwd", "w_gate_bwd", "b_gate_bwd", "gla_norm",
         "w_out_conv", "w_out_gla", "w_merge_out", "norm_post")


def kernel(x, meta_tokens, norm_pre, w_in, conv_w, w_gate_fwd, b_gate_fwd, w_gate_bwd, b_gate_bwd, gla_norm, w_out_conv, w_out_gla, w_merge_out, norm_post, loss_target, m_meta_tokens, m_norm_pre, m_w_in, m_conv_w, m_w_gate_fwd, m_b_gate_fwd, m_w_gate_bwd, m_b_gate_bwd, m_gla_norm, m_w_out_conv, m_w_out_gla, m_w_merge_out, m_norm_post, v_meta_tokens, v_norm_pre, v_w_in, v_conv_w, v_w_gate_fwd, v_b_gate_fwd, v_w_gate_bwd, v_b_gate_bwd, v_gla_norm, v_w_out_conv, v_w_out_gla, v_w_merge_out, v_norm_post):
    w = dict(meta_tokens=meta_tokens, norm_pre=norm_pre, w_in=w_in[0], conv_w=conv_w, w_gate_fwd=w_gate_fwd,
             b_gate_fwd=b_gate_fwd, w_gate_bwd=w_gate_bwd, b_gate_bwd=b_gate_bwd, gla_norm=gla_norm,
             w_out_conv=w_out_conv[0], w_out_gla=w_out_gla[0], w_merge_out=w_merge_out[0], norm_post=norm_post)
    m = dict(meta_tokens=m_meta_tokens, norm_pre=m_norm_pre, w_in=m_w_in[0], conv_w=m_conv_w, w_gate_fwd=m_w_gate_fwd,
             b_gate_fwd=m_b_gate_fwd, w_gate_bwd=m_w_gate_bwd, b_gate_bwd=m_b_gate_bwd, gla_norm=m_gla_norm,
             w_out_conv=m_w_out_conv[0], w_out_gla=m_w_out_gla[0], w_merge_out=m_w_merge_out[0], norm_post=m_norm_post)
    v = dict(meta_tokens=v_meta_tokens, norm_pre=v_norm_pre, w_in=v_w_in[0], conv_w=v_conv_w, w_gate_fwd=v_w_gate_fwd,
             b_gate_fwd=v_b_gate_fwd, w_gate_bwd=v_w_gate_bwd, b_gate_bwd=v_b_gate_bwd, gla_norm=v_gla_norm,
             w_out_conv=v_w_out_conv[0], w_out_gla=v_w_out_gla[0], w_merge_out=v_w_merge_out[0], norm_post=v_norm_post)
    d = x.shape[-1]

    dm = _Dims(*x.shape)
    me = 4 * lax.axis_index("x") + 2 * lax.axis_index("y") + lax.axis_index("c")
    wt_shards, *small_all, u = _gather_two_level(
        [_pad_shard(w["w_in"].T.astype(BF16), me)] + [w[n] for n in SMALL_SHARDED], "gather_weights",
        _prenorm_tokens_side(x, norm_pre, dm))
    _, late_weights = _exchange_start([w[n].astype(BF16) for n in MATRICES], [], small_all[0], "gather_out_weights_start")
    small = {n: _join_column_shards(p) for n, p in zip(SMALL_SHARDED, small_all)}

    def out_weights(after):
        return tuple(a.reshape(-1, d) for a in _exchange_wait(late_weights, after, "gather_out_weights_wait"))

    pending = []

    def on_matrix_grads(g):
        token, state = _exchange_start([], [t.astype(BF16).reshape(N_DEV, -1, d) for t in g.values()], None,
                                       "exchange_grads_start_" + "_".join(g))
        pending.append((tuple(g), state))
        return token

    loss, grad_x, grads = _local_step(
        x, loss_target, small["meta_tokens"], norm_pre, u, wt_shards, small["conv_w"], small["w_gate_fwd"], b_gate_fwd,
        small["w_gate_bwd"], b_gate_bwd, gla_norm, out_weights, norm_post, on_matrix_grads)
    loss = lax.psum(loss, ("x", "y", "c"))
    received = {}
    for names, state in pending:
        received.update(zip(names, _exchange_wait(state, grad_x, "exchange_grads_wait_" + "_".join(names))))

    small_recv = _exchange([grads[n] for n in REPLICATED], [_column_shards(grads[n], w[n].shape) for n in SMALL_SHARDED],
                           "exchange_small_grads")

    results = {"w_in": [r.T[None] for r in _adamw(received["w_in"], w["w_in"].T, m["w_in"].T, v["w_in"].T, "adamw_w_in", by_columns=True)]}
    for n in MATRICES:
        results[n] = [r[None] for r in _adamw(received[n], w[n], m[n], v[n], "adamw_" + n)]
    small_names = REPLICATED + SMALL_SHARDED
    results.update(zip(small_names, _adamw_small([(p, w[n], m[n], v[n]) for n, p in zip(small_names, small_recv)], "adamw_small")))
    return (loss, grad_x, *[results[n][i] for i in range(4) for n in NAMES])
```

```python
import jax
import jax.numpy as jnp
from jax import lax
from jax.experimental import pallas as pl
from jax.experimental.pallas import tpu as pltpu

F32 = jnp.float32
BF16 = jnp.bfloat16
MESH = pl.DeviceIdType.MESH

N_META = 16
CHUNK = 64
CHUNK_SHIFT = 6
HEADS = 4
RANK = 16
LR_LANES = 128
PAD_ROWS = CHUNK - N_META
EPS = 1e-6
GATE_NORMALIZER = 16.0
N_DEV = 8
ADAM_LR, ADAM_B1, ADAM_B2, ADAM_EPS, ADAM_WD, ADAM_STEP = 0.001, 0.9, 0.999, 1e-08, 0.01, 10
VMEM_LIMIT_BYTES = 56 * 1024 * 1024


class _Dims:
    def __init__(self, bl, s, d):
        self.Bl, self.S, self.D = bl, s, d
        self.TM = CHUNK
        self.LP = self.TM + s
        self.T = bl * self.LP
        self.TPS = self.LP // self.TM
        self.NC = self.LP // CHUNK
        self.C0 = (self.TM - CHUNK) // CHUNK
        self.DK, self.DV = d // 2, d
        self.HK, self.HV = self.DK // HEADS, self.DV // HEADS
        self.HW = 2 * self.HK + 2 * self.HV
        self.CW = 256 if d % 256 == 0 and d > 256 else d // 4
        self.NJ = d // self.CW


def _pick(n, target, mult):
    t = min(n, target)
    while t >= mult:
        if n % t == 0 and t % mult == 0:
            return t
        t -= mult
    return n


def _cp(n_axes):
    return pltpu.CompilerParams(dimension_semantics=("arbitrary",) * n_axes, vmem_limit_bytes=VMEM_LIMIT_BYTES)


def _sigmoid(x):
    return 1.0 / (1.0 + jnp.exp(-x))


def _dot(a, b):
    return jnp.dot(a, b, preferred_element_type=F32)


def _dot_nt(a, b):
    return lax.dot_general(a, b, (((1,), (1,)), ((), ())), preferred_element_type=F32)


def _dot_tn(a, b):
    return lax.dot_general(a, b, (((0,), (0,)), ((), ())), preferred_element_type=F32)


def _dot_exact01(m01, x):
    hi = x.astype(BF16)
    lo = (x - hi.astype(F32)).astype(BF16)
    return _dot(m01, hi) + _dot(m01, lo)


def _exchange(gathers, scatters, name):
    arrays = list(gathers) + list(scatters)
    n, ng = len(arrays), len(gathers)

    def body(*refs):
        ins, outs = refs[:n], refs[n:2 * n]
        send_sems, recv_sems, local_sems = refs[2 * n:]
        x, y, c = lax.axis_index("x"), lax.axis_index("y"), lax.axis_index("c")
        me = 4 * x + 2 * y + c
        started = []
        for t in range(n):
            src, dst = ins[t], outs[t]
            own = pltpu.make_async_copy(src if t < ng else src.at[me], dst.at[me], local_sems.at[t])
            own.start()
            started.append(own)
            for k, pos, peer in _peers(x, y, c):
                cp = pltpu.make_async_remote_copy(
                    src_ref=src if t < ng else src.at[peer], dst_ref=dst.at[me],
                    send_sem=send_sems.at[t * (N_DEV - 1) + k - 1], recv_sem=recv_sems.at[t * (N_DEV - 1) + k - 1],
                    device_id=pos, device_id_type=MESH)
                cp.start()
                started.append(cp)
        for cp in started:
            cp.wait()

    out_shape = [jax.ShapeDtypeStruct((N_DEV,) + a.shape if t < ng else a.shape, a.dtype) for t, a in enumerate(arrays)]
    any_spec = pl.BlockSpec(memory_space=pl.ANY)
    return pl.pallas_call(
        body, name=name, out_shape=out_shape, in_specs=[any_spec] * n, out_specs=[any_spec] * n,
        scratch_shapes=[pltpu.SemaphoreType.DMA((n * (N_DEV - 1),)), pltpu.SemaphoreType.DMA((n * (N_DEV - 1),)),
                        pltpu.SemaphoreType.DMA((n,))],
        compiler_params=pltpu.CompilerParams(has_side_effects=True),
    )(*arrays)


def _gather_two_level(arrays, name, side=None):
    n = len(arrays)
    per = N_DEV - 1
    work, side_in, side_in_specs, side_out, side_out_specs, side_scratch = side or (None, [], [], [], [], [])
    n_in, n_out = len(side_in), len(side_out)

    def body(*refs):
        ins, outs = refs[:n], refs[n + n_in:2 * n + n_in]
        send_sems, recv_sems, local_sems = refs[2 * n + n_in + n_out:2 * n + n_in + n_out + 3]
        x, y, c = lax.axis_index("x"), lax.axis_index("y"), lax.axis_index("c")
        sibling = (x, y, 1 - c)
        chips = [(1 - x, y), (x, 1 - y), (1 - x, 1 - y)]
        index = lambda px, py, pc: 4 * px + 2 * py + pc

        def copy(t, k, block, to, from_input=False):
            slab = outs[t].at[index(*block)]
            return pltpu.make_async_remote_copy(
                src_ref=ins[t] if from_input else slab, dst_ref=slab, send_sem=send_sems.at[t * per + k],
                recv_sem=recv_sems.at[t * per + k], device_id=to, device_id_type=MESH)

        own, sent = [], []
        for t in range(n):
            own.append(pltpu.make_async_copy(ins[t], outs[t].at[index(x, y, c)], local_sems.at[t]))
            own[-1].start()
            first = [copy(t, 0, (x, y, c), sibling, True)]
            first += [copy(t, 1 + j, (x, y, c), (*chip, c), True) for j, chip in enumerate(chips)]
            for cp in first:
                cp.start()
            sent += first
        if work is not None:
            work(refs[n:n + n_in], refs[2 * n + n_in:2 * n + n_in + n_out], refs[2 * n + n_in + n_out + 3:])
        for t in range(n):
            for j, chip in enumerate(chips):
                copy(t, 1 + j, (*chip, c), (x, y, c)).wait_recv()
                sent.append(copy(t, 4 + j, (*chip, c), sibling))
                sent[-1].start()
        for t in range(n):
            copy(t, 0, sibling, (x, y, c)).wait_recv()
            for j, chip in enumerate(chips):
                copy(t, 4 + j, (*chip, 1 - c), (x, y, c)).wait_recv()
        for cp in sent:
            cp.wait_send()
        for cp in own:
            cp.wait()

    out_shape = [jax.ShapeDtypeStruct((N_DEV,) + a.shape, a.dtype) for a in arrays]
    any_spec = pl.BlockSpec(memory_space=pl.ANY)
    return pl.pallas_call(
        body, name=name, out_shape=out_shape + list(side_out), in_specs=[any_spec] * n + list(side_in_specs),
        out_specs=[any_spec] * n + list(side_out_specs),
        scratch_shapes=[pltpu.SemaphoreType.DMA((n * per,)), pltpu.SemaphoreType.DMA((n * per,)),
                        pltpu.SemaphoreType.DMA((n,))] + list(side_scratch),
        compiler_params=pltpu.CompilerParams(has_side_effects=True, vmem_limit_bytes=VMEM_LIMIT_BYTES),
    )(*arrays, *side_in)


def _peers(x, y, c):
    out = []
    for k in range(1, N_DEV):
        px = 1 - x if (k >> 2) & 1 else x
        py = 1 - y if (k >> 1) & 1 else y
        pc = 1 - c if k & 1 else c
        out.append((k, (px, py, pc), 4 * px + 2 * py + pc))
    return out


def _exchange_start(gathers, scatters, after, name):
    arrays = list(gathers) + list(scatters)
    n, ng = len(arrays), len(gathers)
    hbm = pl.BlockSpec(memory_space=pltpu.HBM)
    sem = pl.BlockSpec(memory_space=pltpu.SEMAPHORE)

    extra = [] if after is None else [after]
    ne = len(extra)

    def body(*refs):
        ins, lands = refs[:n], refs[n:2 * n]
        send_sems, recv_sems = refs[2 * n + ne], refs[2 * n + ne + 1]
        token = refs[4 * n + ne + 2]
        x, y, c = lax.axis_index("x"), lax.axis_index("y"), lax.axis_index("c")
        me = 4 * x + 2 * y + c
        for t in range(n):
            for k, pos, peer in _peers(x, y, c):
                pltpu.make_async_remote_copy(
                    src_ref=ins[t] if t < ng else ins[t].at[peer], dst_ref=lands[t].at[me],
                    send_sem=send_sems.at[t * (N_DEV - 1) + k - 1], recv_sem=recv_sems.at[t * (N_DEV - 1) + k - 1],
                    device_id=pos, device_id_type=MESH).start()
        token[...] = jnp.zeros_like(token)

    me = 4 * lax.axis_index("x") + 2 * lax.axis_index("y") + lax.axis_index("c")
    lands = [lax.dynamic_update_index_in_dim(lax.empty((N_DEV,) + a.shape if t < ng else a.shape, a.dtype),
                                             a if t < ng else lax.dynamic_index_in_dim(a, me, 0, keepdims=False), me, 0)
             for t, a in enumerate(arrays)]
    operands = [pltpu.with_memory_space_constraint(a, pltpu.HBM) for a in arrays + lands]
    sems = pltpu.SemaphoreType.DMA((n * (N_DEV - 1),))
    res = pl.pallas_call(
        body, name=name,
        out_shape=(sems, sems, *[pltpu.HBM(a.shape, a.dtype) for a in arrays + lands], jax.ShapeDtypeStruct((8, 128), F32)),
        in_specs=[hbm] * (2 * n) + [pl.BlockSpec(memory_space=pl.ANY)] * ne,
        out_specs=(sem, sem, *[hbm] * (2 * n), pl.BlockSpec(memory_space=pltpu.VMEM)),
        input_output_aliases={i: 2 + i for i in range(2 * n)},
        compiler_params=pltpu.CompilerParams(has_side_effects=pltpu.SideEffectType.DATAFLOW_SIDE_EFFECTING),
    )(*operands, *extra)
    return res[-1][0, 0], (ng, res[0], res[1], list(res[2:2 + n]), list(res[2 + n:2 + 2 * n]))


def _exchange_wait(state, after, name):
    ng, send_sems, recv_sems, sent, lands = state
    n = len(sent)
    hbm = pl.BlockSpec(memory_space=pltpu.HBM)
    sem = pl.BlockSpec(memory_space=pltpu.SEMAPHORE)

    def body(*refs):
        ins, land_refs = refs[:n], refs[n:2 * n]
        send_ref, recv_ref = refs[2 * n], refs[2 * n + 1]
        x, y, c = lax.axis_index("x"), lax.axis_index("y"), lax.axis_index("c")
        me = 4 * x + 2 * y + c
        for t in range(n):
            for k, pos, peer in _peers(x, y, c):
                cp = pltpu.make_async_remote_copy(
                    src_ref=ins[t] if t < ng else ins[t].at[peer], dst_ref=land_refs[t].at[me],
                    send_sem=send_ref.at[t * (N_DEV - 1) + k - 1], recv_sem=recv_ref.at[t * (N_DEV - 1) + k - 1],
                    device_id=pos, device_id_type=MESH)
                cp.wait_send()
                cp.wait_recv()

    res = pl.pallas_call(
        body, name=name, out_shape=tuple(pltpu.HBM(a.shape, a.dtype) for a in sent + lands),
        in_specs=[hbm] * (2 * n) + [sem, sem, pl.BlockSpec(memory_space=pl.ANY)], out_specs=tuple([hbm] * (2 * n)),
        input_output_aliases={i: i for i in range(2 * n)},
        compiler_params=pltpu.CompilerParams(has_side_effects=pltpu.SideEffectType.DATAFLOW_SIDE_EFFECTING),
    )(*sent, *lands, send_sems, recv_sems, after)
    return list(res[n:])


def _rms_scaled(h, g):
    return (h * lax.rsqrt(jnp.mean(h * h, axis=-1, keepdims=True) + EPS) * g).astype(BF16)


def _prenorm_tokens_side(x, g_pre, dm):
    bl, s, d = x.shape
    rows = _pick(s, 512, 16)
    tiles = [(b, j) for b in range(bl) for j in range(s // rows)]

    def work(ins, outs, scratch):
        (x_ref, g_ref), (u_ref,), (xbuf, ubuf, sem_in, sem_out) = ins, outs, scratch

        def load(t, slot):
            b, j = tiles[t]
            return pltpu.make_async_copy(x_ref.at[b, pl.ds(j * rows, rows), :], xbuf.at[slot], sem_in.at[slot])

        def store(t, slot):
            b, j = tiles[t]
            return pltpu.make_async_copy(ubuf.at[slot], u_ref.at[pl.ds(b * dm.LP + dm.TM + j * rows, rows), :], sem_out.at[slot])

        load(0, 0).start()
        for t in range(len(tiles)):
            slot = t % 2
            if t + 1 < len(tiles):
                load(t + 1, 1 - slot).start()
            load(t, slot).wait()
            if t >= 2:
                store(t - 2, slot).wait()
            ubuf[slot] = _rms_scaled(xbuf[slot], g_ref[...])
            store(t, slot).start()
        for t in range(max(len(tiles) - 2, 0), len(tiles)):
            store(t, t % 2).wait()

    any_spec = pl.BlockSpec(memory_space=pl.ANY)
    return (work, [x, g_pre], [any_spec, pl.BlockSpec(memory_space=pltpu.VMEM)],
            [jax.ShapeDtypeStruct((dm.T, d), BF16)], [any_spec],
            [pltpu.VMEM((2, rows, d), F32), pltpu.VMEM((2, rows, d), BF16), pltpu.SemaphoreType.DMA((2,)),
             pltpu.SemaphoreType.DMA((2,))])


def _prenorm_meta(u, metapad, g_pre, dm):
    tm, tps, d = dm.TM, dm.TPS, dm.D

    def body(u_in, mp_ref, g_ref, u_ref):
        u_ref[...] = _rms_scaled(mp_ref[...], g_ref[...])

    return pl.pallas_call(
        body, name="prenorm_meta", grid=(dm.Bl,),
        in_specs=[pl.BlockSpec(memory_space=pl.ANY), pl.BlockSpec((tm, d), lambda i: (0, 0)),
                  pl.BlockSpec((1, d), lambda i: (0, 0))],
        out_specs=pl.BlockSpec((tm, d), lambda i: (i * tps, 0)),
        out_shape=jax.ShapeDtypeStruct((dm.T, d), BF16), input_output_aliases={0: 0}, compiler_params=_cp(1),
    )(u, metapad, g_pre)


def _matmul_tn(a, b, out_dtype, name, tt=2304, tn=1024, tk=1024):
    t, k = a.shape
    n = b.shape[1]
    tt, tn, tk = _pick(t, tt, 16), _pick(n, tn, 128), _pick(k, tk, 128)
    nt = t // tt

    def body(a_ref, b_ref, o_ref, acc):
        p = _dot_tn(a_ref[...].astype(BF16), b_ref[...].astype(BF16))
        i = pl.program_id(2)

        @pl.when(i == 0)
        def _():
            acc[...] = p

        @pl.when(i > 0)
        def _():
            acc[...] += p

        @pl.when(i == nt - 1)
        def _():
            o_ref[...] = acc[...].astype(out_dtype)

    return pl.pallas_call(
        body, name=name, grid=(k // tk, n // tn, nt),
        in_specs=[pl.BlockSpec((tt, tk), lambda kk, j, i: (i, kk)), pl.BlockSpec((tt, tn), lambda kk, j, i: (i, j))],
        out_specs=pl.BlockSpec((tk, tn), lambda kk, j, i: (kk, j)),
        out_shape=jax.ShapeDtypeStruct((k, n), out_dtype), scratch_shapes=[pltpu.VMEM((tk, tn), F32)],
        compiler_params=_cp(3),
    )(a, b)


BF16_TILE_ROWS = 16


def _shard_offset(index, shard_rows):
    return (index * shard_rows) % BF16_TILE_ROWS


def _pad_shard(wt_shard, index):
    rows, d = wt_shard.shape
    padded = -(-(rows + max(_shard_offset(j, rows) for j in range(N_DEV))) // BF16_TILE_ROWS) * BF16_TILE_ROWS
    return lax.dynamic_update_slice(jnp.zeros((padded, d), wt_shard.dtype), wt_shard, (_shard_offset(index, rows), 0))


def _packed_parts(dm):
    d, dk, hk, hv, cw, nj, hw = dm.D, dm.DK, dm.HK, dm.HV, dm.CW, dm.NJ, dm.HW
    blocks = [(0, (j * 4 + p) * cw, p * d + j * cw, cw) for j in range(nj) for p in range(4)]
    for h in range(HEADS):
        blocks += [(1, h * hw, 4 * d + h * hk, hk), (1, h * hw + hk, 4 * d + dk + h * hk, hk),
                   (1, h * hw + 2 * hk, 5 * d + h * hv, hv), (1, h * hw + 2 * hk + hv, 6 * d + h * hv, hv)]
    blocks += [(2, 0, 7 * d + 2 * RANK, 2 * d), (3, 0, 7 * d, 2 * RANK)]
    return [4 * d, 3 * d, 2 * d, LR_LANES], blocks


def _pack_plan(dm):
    sh = (9 * dm.D + 2 * RANK) // N_DEV
    tile = BF16_TILE_ROWS
    copies, straddles = [], []
    for part, dst, r0, n in _packed_parts(dm)[1]:
        for j in range(N_DEV):
            a, b = max(r0, sh * j), min(r0 + n, sh * (j + 1))
            if a >= b:
                continue
            a_up, b_down = -(-a // tile) * tile, b // tile * tile
            if b_down > a_up:
                copies.append((j, a_up - sh * j + _shard_offset(j, sh), b_down - a_up, part, dst + a_up - r0))
            if a % tile:
                lo = a // tile * tile
                straddles.append((j, lo - sh * (j - 1) + _shard_offset(j - 1, sh), part, dst + lo - r0, a - lo))
    return copies, straddles


def _packed_scratch(dm):
    copies, straddles = _pack_plan(dm)
    return ([pltpu.VMEM((rows, dm.D), BF16) for rows in _packed_parts(dm)[0]]
            + [pltpu.VMEM((2 * max(len(straddles), 1), BF16_TILE_ROWS, dm.D), BF16),
               pltpu.SemaphoreType.DMA((len(copies) + 2 * len(straddles),))])


def _load_packed(g_ref, parts, edges, sems, dm):
    copies, straddles = _pack_plan(dm)
    tile = BF16_TILE_ROWS
    parts[3][2 * RANK:, :] = jnp.zeros((LR_LANES - 2 * RANK, dm.D), BF16)
    dmas = [pltpu.make_async_copy(g_ref.at[j, pl.ds(src, n), :], parts[p].at[pl.ds(dst, n), :], sems.at[i])
            for i, (j, src, n, p, dst) in enumerate(copies)]
    for i, (j, src, p, dst, split) in enumerate(straddles):
        k = len(copies) + 2 * i
        dmas.append(pltpu.make_async_copy(g_ref.at[j - 1, pl.ds(src, tile), :], edges.at[2 * i], sems.at[k]))
        dmas.append(pltpu.make_async_copy(g_ref.at[j, pl.ds(0, tile), :], edges.at[2 * i + 1], sems.at[k + 1]))
    for cp in dmas:
        cp.start()
    for cp in dmas:
        cp.wait()
    row = lax.broadcasted_iota(jnp.int32, (tile, dm.D), 0)
    for i, (j, src, p, dst, split) in enumerate(straddles):
        parts[p][dst:dst + tile, :] = jnp.where(row < split, edges[2 * i], edges[2 * i + 1])


def _inproj(u, gathered, dm):
    t, d = u.shape
    tm = _pick(t, 512, 16)
    widths = _packed_parts(dm)[0]
    cn = 1024

    def body(u_ref, g_ref, *rest):
        outs, parts, (edges, sems) = rest[:4], rest[4:8], rest[8:]

        @pl.when(pl.program_id(0) == 0)
        def _():
            _load_packed(g_ref, parts, edges, sems, dm)

        ut = u_ref[...]
        for w, o_ref in zip(parts, outs):
            n = w.shape[0]
            step = cn if n % cn == 0 else n
            for j in range(0, n, step):
                o_ref[:, j:j + step] = _dot_nt(ut, w[j:j + step, :]).astype(BF16)

    return pl.pallas_call(
        body, name="inproj", grid=(t // tm,),
        in_specs=[pl.BlockSpec((tm, d), lambda i: (i, 0)), pl.BlockSpec(memory_space=pl.ANY)],
        out_specs=[pl.BlockSpec((tm, w), lambda i: (i, 0)) for w in widths],
        out_shape=[jax.ShapeDtypeStruct((t, w), BF16) for w in widths],
        scratch_shapes=_packed_scratch(dm), compiler_params=_cp(1),
    )(u, gathered)


def _conv_rows(dm):
    return _pick(dm.LP, 256, 16)


def _shifted(m, prev_row, next_row, rows):
    row = lax.broadcasted_iota(jnp.int32, m.shape, 0)
    m_prev = jnp.where(row == 0, prev_row, pltpu.roll(m, 1, 0))
    m_next = jnp.where(row == rows - 1, next_row, pltpu.roll(m, rows - 1, 0))
    return m_prev, m_next


def _conv_fwd(proj_a, conv_w, dm):
    lp, cw, rc = dm.LP, dm.CW, _conv_rows(dm)
    nchunk = lp // rc

    def body(p_ref, w_ref, y_ref):
        w0, w1, w2 = w_ref[0:1, :], w_ref[1:2, :], w_ref[2:3, :]

        def chunk(ci, carry):
            r0 = pl.multiple_of(ci * rc, rc)
            blk = p_ref[pl.ds(r0, rc), :].astype(F32)
            cb, cc, cx, cz = (blk[:, i * cw:(i + 1) * cw] for i in range(4))
            m = cc * cx
            rp = pl.multiple_of(jnp.maximum(r0 - 16, 0), 16)
            rn = pl.multiple_of(jnp.minimum(r0 + rc, lp - 16), 16)
            pv = p_ref[pl.ds(rp, 16), cw:3 * cw].astype(F32)
            nx = p_ref[pl.ds(rn, 16), cw:3 * cw].astype(F32)
            prev_row = jnp.where(ci > 0, pv[15:16, :cw] * pv[15:16, cw:], 0.0)
            next_row = jnp.where(ci < nchunk - 1, nx[0:1, :cw] * nx[0:1, cw:], 0.0)
            m_prev, m_next = _shifted(m, prev_row, next_row, rc)
            s = w0 * m_prev + w1 * m + w2 * m_next
            y_ref[pl.ds(r0, rc), :] = (cb * s * (cz * _sigmoid(cz))).astype(BF16)
            return carry

        lax.fori_loop(0, nchunk, chunk, 0)

    return pl.pallas_call(
        body, name="conv_fwd", grid=(dm.Bl, dm.NJ),
        in_specs=[pl.BlockSpec((lp, 4 * cw), lambda s, j: (s, j)), pl.BlockSpec((3, cw), lambda s, j: (0, j))],
        out_specs=pl.BlockSpec((lp, cw), lambda s, j: (s, j)),
        out_shape=jax.ShapeDtypeStruct((dm.T, dm.D), BF16), compiler_params=_cp(2),
    )(proj_a, conv_w)


def _conv_bwd(proj_a, dy_conv, conv_w, dm):
    lp, cw, rc = dm.LP, dm.CW, _conv_rows(dm)
    nchunk = lp // rc

    def body(p_ref, dy_ref, w_ref, d_ref, gw_ref):
        w0, w1, w2 = w_ref[0:1, :], w_ref[1:2, :], w_ref[2:3, :]

        def ds_of(p4, dy):
            cb, cz = p4[:, :cw], p4[:, 3 * cw:]
            return dy * cb * (cz * _sigmoid(cz))

        def chunk(ci, carry):
            g0, g1, g2 = carry
            r0 = pl.multiple_of(ci * rc, rc)
            blk = p_ref[pl.ds(r0, rc), :].astype(F32)
            dy = dy_ref[pl.ds(r0, rc), :].astype(F32)
            cb, cc, cx, cz = (blk[:, i * cw:(i + 1) * cw] for i in range(4))
            rp = pl.multiple_of(jnp.maximum(r0 - 16, 0), 16)
            rn = pl.multiple_of(jnp.minimum(r0 + rc, lp - 16), 16)
            pv = p_ref[pl.ds(rp, 16), :].astype(F32)[15:16]
            nx = p_ref[pl.ds(rn, 16), :].astype(F32)[0:1]
            dpv = dy_ref[pl.ds(rp, 16), :].astype(F32)[15:16]
            dnx = dy_ref[pl.ds(rn, 16), :].astype(F32)[0:1]
            has_prev, has_next = ci > 0, ci < nchunk - 1
            m = cc * cx
            m_prev, m_next = _shifted(m, jnp.where(has_prev, pv[:, cw:2 * cw] * pv[:, 2 * cw:3 * cw], 0.0),
                                      jnp.where(has_next, nx[:, cw:2 * cw] * nx[:, 2 * cw:3 * cw], 0.0), rc)
            s = w0 * m_prev + w1 * m + w2 * m_next
            sg = _sigmoid(cz)
            silu = cz * sg
            ds = dy * cb * silu
            ds_prev, ds_next = _shifted(ds, jnp.where(has_prev, ds_of(pv, dpv), 0.0),
                                        jnp.where(has_next, ds_of(nx, dnx), 0.0), rc)
            dm_ = w0 * ds_next + w1 * ds + w2 * ds_prev
            d_ref[pl.ds(r0, rc), 0:cw] = (dy * s * silu).astype(BF16)
            d_ref[pl.ds(r0, rc), cw:2 * cw] = (dm_ * cx).astype(BF16)
            d_ref[pl.ds(r0, rc), 2 * cw:3 * cw] = (dm_ * cc).astype(BF16)
            d_ref[pl.ds(r0, rc), 3 * cw:4 * cw] = (dy * cb * s * (sg * (1.0 + cz * (1.0 - sg)))).astype(BF16)
            return (g0 + jnp.sum(ds * m_prev, axis=0, keepdims=True), g1 + jnp.sum(ds * m, axis=0, keepdims=True),
                    g2 + jnp.sum(ds * m_next, axis=0, keepdims=True))

        z = jnp.zeros((1, cw), F32)
        g0, g1, g2 = lax.fori_loop(0, nchunk, chunk, (z, z, z))

        @pl.when(pl.program_id(1) == 0)
        def _():
            gw_ref[...] = jnp.zeros_like(gw_ref)

        gw_ref[0:1, :] += g0
        gw_ref[1:2, :] += g1
        gw_ref[2:3, :] += g2

    return pl.pallas_call(
        body, name="conv_bwd", grid=(dm.NJ, dm.Bl),
        in_specs=[pl.BlockSpec((lp, 4 * cw), lambda j, s: (s, j)), pl.BlockSpec((lp, cw), lambda j, s: (s, j)),
                  pl.BlockSpec((3, cw), lambda j, s: (0, j))],
        out_specs=[pl.BlockSpec((lp, 4 * cw), lambda j, s: (s, j)), pl.BlockSpec((8, cw), lambda j, s: (0, j))],
        out_shape=[jax.ShapeDtypeStruct((dm.T, 4 * dm.D), BF16), jax.ShapeDtypeStruct((8, dm.D), F32)],
        compiler_params=_cp(2),
    )(proj_a, dy_conv, conv_w)


def _interleave(gens):
    results = [None] * len(gens)
    live = list(range(len(gens)))
    while live:
        for idx in list(live):
            try:
                next(gens[idx])
            except StopIteration as done:
                results[idx] = done.value
                live.remove(idx)
    return results


def _group_chunks(dm):
    n = dm.NC - dm.C0
    return 3 if n % 3 == 0 else 1


def _group_masks(rows):
    ii = lax.broadcasted_iota(jnp.int32, (rows, rows), 0)
    jj = lax.broadcasted_iota(jnp.int32, (rows, rows), 1)
    same = jnp.right_shift(ii, CHUNK_SHIFT) == jnp.right_shift(jj, CHUNK_SHIFT)
    low, up = same & (jj <= ii), same & (jj >= ii)
    return low, same & (jj > ii), low.astype(BF16), up.astype(BF16)


def _first_row(chunk):
    return chunk * CHUNK if isinstance(chunk, int) else pl.multiple_of(chunk * CHUNK, CHUNK)


def _chunk_totals(b, fwd):
    hk = b.shape[1]
    rows = [b[c * CHUNK + CHUNK - 1:(c + 1) * CHUNK] if fwd else b[c * CHUNK:c * CHUNK + 1]
            for c in range(b.shape[0] // CHUNK)]
    return jnp.concatenate([jnp.broadcast_to(r, (CHUNK, hk)) for r in rows], axis=0)


def _log_gate(lr_rows, w_ref, b_ref, first_group, hk):
    z = _dot(lr_rows, w_ref[...]) + b_ref[...]
    e = jnp.exp(-jnp.abs(z))
    g = (jnp.minimum(z, 0.0) - jnp.log(1.0 + e)) * (1.0 / GATE_NORMALIZER)
    dg_dz = jnp.where(z >= 0.0, e, 1.0) / (1.0 + e) * (1.0 / GATE_NORMALIZER)
    row = lax.broadcasted_iota(jnp.int32, (lr_rows.shape[0], hk), 0)
    pad = first_group & (row < PAD_ROWS)
    return jnp.where(pad, 0.0, g), jnp.where(pad, 0.0, dg_dz)


def _gla_fwd(proj_b, lr, wg_f, bg_f, wg_b, bg_b, gla_g, dm):
    lp, hk, hv, nc, c0, hw = dm.LP, dm.HK, dm.HV, dm.NC, dm.C0, dm.HW
    scale = hk ** -0.5
    gc = _group_chunks(dm)
    gr, ng = gc * CHUNK, (nc - c0) // gc

    def body(p_ref, lr_ref, wf_ref, bf_ref, wb_ref, bb_ref, gg_ref, o_ref, y_ref, st_ref, b_out, gs_out, oacc_f, oacc_b):
        low_incl, up_strict, ones_low, ones_up = _group_masks(gr)
        if c0 > 0:
            zr = c0 * CHUNK
            o_ref[0:zr, :] = jnp.zeros((zr, hv), BF16)
            y_ref[0:zr, :] = jnp.zeros((zr, hv), BF16)
            b_out[:, 0:zr, :] = jnp.zeros((2, zr, hk), F32)
            gs_out[:, 0:zr, :] = jnp.zeros((2, zr, hk), F32)
            st_ref[0, 0, :, 0:c0] = jnp.zeros((2, c0, hv, hk), BF16)

        def decay(gi, fwd):
            w_ref, b_ref = (wf_ref, bf_ref) if fwd else (wb_ref, bb_ref)
            r0 = _first_row(c0 + gi * gc)
            yield
            g, dg_dz = _log_gate(lr_ref[pl.ds(r0, gr), :], w_ref, b_ref, gi == 0, hk)
            gs_out[0 if fwd else 1, pl.ds(r0, gr), :] = dg_dz
            yield
            b = _dot_exact01(ones_low if fwd else ones_up, g)
            b_out[0 if fwd else 1, pl.ds(r0, gr), :] = b
            return b

        def group(gi, st, b, fwd):
            oacc = oacc_f if fwd else oacc_b
            r0 = pl.multiple_of((c0 + gi * gc) * CHUNK, CHUNK)
            blk = p_ref[pl.ds(r0, gr), :]
            q = blk[:, :hk].astype(F32) * scale
            k = blk[:, hk:2 * hk].astype(F32)
            v = blk[:, 2 * hk:2 * hk + hv]
            btot = _chunk_totals(b, fwd)
            qi = (q * jnp.exp(b)).astype(BF16)
            ki = (k * jnp.exp(-b)).astype(BF16)
            kd = (k * jnp.exp(btot - b)).astype(BF16)
            dec = jnp.exp(btot)
            a = _dot_nt(qi, ki)
            yield
            o = _dot(jnp.where(low_incl if fwd else up_strict, a, 0.0).astype(BF16), v)
            chunk_rows = [slice(c * CHUNK, (c + 1) * CHUNK) for c in range(gc)]
            kv = [_dot_tn(v[rows], kd[rows]) for rows in chunk_rows]
            for c in (range(gc) if fwd else reversed(range(gc))):
                yield
                rows = chunk_rows[c]
                st_b = st.astype(BF16)
                st_ref[0, 0, 0 if fwd else 1, c0 + gi * gc + c] = st_b
                oacc[pl.ds(r0 + c * CHUNK, CHUNK), :] = o[rows] + _dot_nt(qi[rows], st_b)
                st = st * dec[c * CHUNK:c * CHUNK + 1] + kv[c]
            return st

        def step(i, carry):
            st_f, st_b, b_f, b_b = carry
            gf, gb = i, ng - 1 - i
            return tuple(_interleave([group(gf, st_f, b_f, True), group(gb, st_b, b_b, False),
                                      decay(jnp.minimum(gf + 1, ng - 1), True), decay(jnp.maximum(gb - 1, 0), False)]))

        zero = jnp.zeros((hv, hk), F32)
        lax.fori_loop(0, ng, step, (zero, zero, *_interleave([decay(0, True), decay(ng - 1, False)])))

        def finish(i, carry):
            r0 = pl.multiple_of((c0 + i * gc) * CHUNK, CHUNK)
            o = oacc_f[pl.ds(r0, gr), :] + oacc_b[pl.ds(r0, gr), :]
            r = p_ref[pl.ds(r0, gr), 2 * hk + hv:].astype(F32)
            on = o * lax.rsqrt(jnp.mean(o * o, axis=-1, keepdims=True) + EPS) * gg_ref[...]
            o_ref[pl.ds(r0, gr), :] = o.astype(BF16)
            y_ref[pl.ds(r0, gr), :] = (on * r * _sigmoid(r)).astype(BF16)
            return carry

        lax.fori_loop(0, ng, finish, 0)

    head = lambda s, h: (s, h)
    wspec = pl.BlockSpec((LR_LANES, hk), lambda s, h: (0, h))
    bspec = pl.BlockSpec((1, hk), lambda s, h: (0, h))
    return pl.pallas_call(
        body, name="gla_fwd", grid=(dm.Bl, HEADS),
        in_specs=[pl.BlockSpec((lp, hw), head), pl.BlockSpec((lp, LR_LANES), lambda s, h: (s, 0)),
                  wspec, bspec, wspec, bspec, pl.BlockSpec((1, hv), lambda s, h: (0, 0))],
        out_specs=[pl.BlockSpec((lp, hv), head), pl.BlockSpec((lp, hv), head),
                   pl.BlockSpec((1, 1, 2, nc, hv, hk), lambda s, h: (s, h, 0, 0, 0, 0)),
                   pl.BlockSpec((2, lp, hk), lambda s, h: (0, s, h)), pl.BlockSpec((2, lp, hk), lambda s, h: (0, s, h))],
        out_shape=[jax.ShapeDtypeStruct((dm.T, dm.DV), BF16), jax.ShapeDtypeStruct((dm.T, dm.DV), BF16),
                   jax.ShapeDtypeStruct((dm.Bl, HEADS, 2, nc, hv, hk), BF16),
                   jax.ShapeDtypeStruct((2, dm.T, dm.DK), F32), jax.ShapeDtypeStruct((2, dm.T, dm.DK), F32)],
        scratch_shapes=[pltpu.VMEM((lp, hv), F32), pltpu.VMEM((lp, hv), F32)],
        compiler_params=_cp(2),
    )(proj_b, lr, wg_f, bg_f, wg_b, bg_b, gla_g)


def _gla_bwd(proj_b, lr, o_all, dy_gla, states, decays, gate_slopes, wg_f, wg_b, gla_g, dm):
    lp, hk, hv, nc, c0, hw = dm.LP, dm.HK, dm.HV, dm.NC, dm.C0, dm.HW
    scale = hk ** -0.5
    gc = _group_chunks(dm)
    gr, ng = gc * CHUNK, (nc - c0) // gc

    def body(p_ref, lr_ref, o_ref, dy_ref, st_ref, b_ref, gs_ref, wf_ref, wb_ref, gg_ref,
             d_ref, dlr_ref, gwf_ref, gbf_ref, gwb_ref, gbb_ref, ggg_ref, do_s, dq_s, dk_s, dv_s, dlr_s):
        low_incl, up_strict, ones_low, ones_up = _group_masks(gr)
        h = pl.program_id(1)

        @pl.when(h == 0)
        def _():
            dlr_ref[...] = jnp.zeros_like(dlr_ref)

        if c0 > 0:
            zr = c0 * CHUNK
            d_ref[0:zr, :] = jnp.zeros((zr, hw), BF16)
        for acc in (dq_s, dk_s, dv_s, dlr_s):
            acc[...] = jnp.zeros_like(acc)

        def norm_bwd(i, ggg):
            r0 = pl.multiple_of((c0 + i * gc) * CHUNK, CHUNK)
            o = o_ref[pl.ds(r0, gr), :].astype(F32)
            dy = dy_ref[pl.ds(r0, gr), :].astype(F32)
            r = p_ref[pl.ds(r0, gr), 2 * hk + hv:].astype(F32)
            rstd = lax.rsqrt(jnp.mean(o * o, axis=-1, keepdims=True) + EPS)
            ohat = o * rstd
            sg = _sigmoid(r)
            d_on = dy * (r * sg)
            d_ref[pl.ds(r0, gr), 2 * hk + hv:] = (dy * ohat * gg_ref[...] * (sg * (1.0 + r * (1.0 - sg)))).astype(BF16)
            d_oh = d_on * gg_ref[...]
            do_s[pl.ds(r0, gr), :] = (rstd * (d_oh - ohat * jnp.mean(d_oh * ohat, axis=-1, keepdims=True))).astype(BF16)
            return ggg + jnp.sum(d_on * ohat, axis=0, keepdims=True)

        ggg = lax.fori_loop(0, ng, norm_bwd, jnp.zeros((1, hv), F32))

        @pl.when((pl.program_id(0) == 0) & (h == 0))
        def _():
            ggg_ref[...] = jnp.zeros_like(ggg_ref)

        ggg_ref[0:1, :] += ggg

        def load(gi):
            r0 = pl.multiple_of((c0 + gi * gc) * CHUNK, CHUNK)
            blk = p_ref[pl.ds(r0, gr), :]
            return r0, blk[:, :hk].astype(F32) * scale, blk[:, hk:2 * hk].astype(F32), blk[:, 2 * hk:2 * hk + hv]

        zero = jnp.zeros((hv, hk), F32)

        def grad(gi, carry, fwd):
            dst, gw, gb = carry
            w_ref, way = (wf_ref, 0) if fwd else (wb_ref, 1)
            mask = low_incl if fwd else up_strict
            r0, q, k, v = load(gi)
            b = b_ref[way, pl.ds(r0, gr), :]
            btot = _chunk_totals(b, fwd)
            eb, enb, edb, dec = jnp.exp(b), jnp.exp(-b), jnp.exp(btot - b), jnp.exp(btot)
            qi_f, ki_f, kd_f = q * eb, k * enb, k * edb
            qi, ki, kd = qi_f.astype(BF16), ki_f.astype(BF16), kd_f.astype(BF16)
            do = do_s[pl.ds(r0, gr), :]
            a = _dot_nt(qi, ki)
            da = _dot_nt(do, v)
            yield
            a = jnp.where(mask, a, 0.0).astype(BF16)
            da = jnp.where(mask, da, 0.0).astype(BF16)
            dv = _dot_tn(a, do)
            dqi = _dot(da, ki)
            dki = _dot_tn(da, qi)
            dv_c, dqi_c, dkd_c, extra_c = [None] * gc, [None] * gc, [None] * gc, [None] * gc
            chunk_rows = [slice(c * CHUNK, (c + 1) * CHUNK) for c in range(gc)]
            qdo = [_dot_tn(do[rows], qi[rows]) for rows in chunk_rows]
            for c in (reversed(range(gc)) if fwd else range(gc)):
                yield
                rows = chunk_rows[c]
                st = st_ref[0, 0, way, c0 + gi * gc + c]
                dsn_b = dst.astype(BF16)
                dec_c = dec[c * CHUNK:c * CHUNK + 1]
                dv_c[c] = dv[rows] + _dot_nt(kd[rows], dsn_b)
                dqi_c[c] = dqi[rows] + _dot(do[rows], st)
                dkd_c[c] = _dot(v[rows], dsn_b)
                ddec = jnp.sum(st.astype(F32) * dst, axis=0, keepdims=True)
                extra = jnp.sum(dkd_c[c] * kd_f[rows], axis=0, keepdims=True) + ddec * dec_c
                extra_c[c] = jnp.broadcast_to(extra, (CHUNK, hk))
                dst = dst * dec_c + qdo[c]
            yield
            dv, dqi = jnp.concatenate(dv_c, axis=0), jnp.concatenate(dqi_c, axis=0)
            dkd, extra = jnp.concatenate(dkd_c, axis=0), jnp.concatenate(extra_c, axis=0)
            dq_s[pl.ds(r0, gr), :] += dqi * eb * scale
            dk_s[pl.ds(r0, gr), :] += dki * enb + dkd * edb
            dv_s[pl.ds(r0, gr), :] += dv
            db = dqi * qi_f - dki * ki_f - dkd * kd_f
            dg = _dot_exact01(ones_up if fwd else ones_low, db) + extra
            yield
            dz = dg * gs_ref[way, pl.ds(r0, gr), :]
            dz_b = dz.astype(BF16)
            dlr_s[pl.ds(r0, gr), :] += _dot_nt(dz_b, w_ref[...])
            return dst, gw + _dot_tn(lr_ref[pl.ds(r0, gr), :], dz_b), gb + jnp.sum(dz, axis=0, keepdims=True)

        def grad_step(i, carry):
            return tuple(_interleave([grad(ng - 1 - i, carry[0], True), grad(i, carry[1], False)]))

        init = (zero, jnp.zeros((LR_LANES, hk), F32), jnp.zeros((1, hk), F32))
        (_, gw_f, gb_f), (_, gw_b, gb_b) = lax.fori_loop(0, ng, grad_step, (init, init))
        for gw_ref, gb_ref, gw, gb in ((gwf_ref, gbf_ref, gw_f, gb_f), (gwb_ref, gbb_ref, gw_b, gb_b)):
            gw_ref[0] = gw
            gb_ref[0] = jnp.zeros((8, hk), F32)
            gb_ref[0, 0:1, :] = gb

        def combine(i, carry):
            r0 = pl.multiple_of((c0 + i * gc) * CHUNK, CHUNK)
            d_ref[pl.ds(r0, gr), 0:hk] = dq_s[pl.ds(r0, gr), :].astype(BF16)
            d_ref[pl.ds(r0, gr), hk:2 * hk] = dk_s[pl.ds(r0, gr), :].astype(BF16)
            d_ref[pl.ds(r0, gr), 2 * hk:2 * hk + hv] = dv_s[pl.ds(r0, gr), :].astype(BF16)
            dlr_ref[pl.ds(r0, gr), :] += dlr_s[pl.ds(r0, gr), :]
            return carry

        lax.fori_loop(0, ng, combine, 0)

    head = lambda s, h: (s, h)
    wspec = pl.BlockSpec((LR_LANES, hk), lambda s, h: (0, h))
    gwspec = pl.BlockSpec((1, LR_LANES, hk), lambda s, h: (s, 0, h))
    gbspec = pl.BlockSpec((1, 8, hk), lambda s, h: (s, 0, h))
    gw_shape = jax.ShapeDtypeStruct((dm.Bl, LR_LANES, dm.DK), F32)
    gb_shape = jax.ShapeDtypeStruct((dm.Bl, 8, dm.DK), F32)
    both = pl.BlockSpec((2, lp, hk), lambda s, h: (0, s, h))
    return pl.pallas_call(
        body, name="gla_bwd", grid=(dm.Bl, HEADS),
        in_specs=[pl.BlockSpec((lp, hw), head), pl.BlockSpec((lp, LR_LANES), lambda s, h: (s, 0)),
                  pl.BlockSpec((lp, hv), head), pl.BlockSpec((lp, hv), head),
                  pl.BlockSpec((1, 1, 2, nc, hv, hk), lambda s, h: (s, h, 0, 0, 0, 0)), both, both,
                  wspec, wspec, pl.BlockSpec((1, hv), lambda s, h: (0, 0))],
        out_specs=[pl.BlockSpec((lp, hw), head), pl.BlockSpec((lp, LR_LANES), lambda s, h: (s, 0)),
                   gwspec, gbspec, gwspec, gbspec, pl.BlockSpec((8, hv), lambda s, h: (0, 0))],
        out_shape=[jax.ShapeDtypeStruct((dm.T, HEADS * hw), BF16), jax.ShapeDtypeStruct((dm.T, LR_LANES), F32),
                   gw_shape, gb_shape, gw_shape, gb_shape, jax.ShapeDtypeStruct((8, hv), F32)],
        scratch_shapes=[pltpu.VMEM((lp, hv), BF16), pltpu.VMEM((lp, hk), F32), pltpu.VMEM((lp, hk), F32),
                        pltpu.VMEM((lp, hv), F32), pltpu.VMEM((lp, LR_LANES), F32)],
        compiler_params=_cp(2),
    )(proj_b, lr, o_all, dy_gla, states, decays, gate_slopes, wg_f, wg_b, gla_g)


def _stream_tiles(n_tiles, loads, stores, compute):
    for cp in loads(0, 0):
        cp.start()

    def step(t, carry):
        slot = t % 2

        @pl.when(t + 1 < n_tiles)
        def _():
            for cp in loads(t + 1, 1 - slot):
                cp.start()

        for cp in loads(t, slot):
            cp.wait()

        @pl.when(t >= 2)
        def _():
            for cp in stores(t - 2, slot):
                cp.wait()

        compute(t, slot)
        for cp in stores(t, slot):
            cp.start()
        return carry

    lax.fori_loop(0, n_tiles, step, 0)
    for t in range(max(n_tiles - 2, 0), n_tiles):
        for cp in stores(t, t % 2):
            cp.wait()


def _token_tiles(dm, target_rows=512):
    rows = _pick(dm.S, target_rows, 16)
    per_seq = dm.S // rows
    return rows, dm.Bl * per_seq, lambda t: pl.multiple_of((t // per_seq) * dm.LP + dm.TM + (t % per_seq) * rows, 16)


def _head(y_conv, y_gla, proj_c, w_oc, w_og, w_out, x, target, g_post, dm):
    d, tm = dm.D, dm.TM
    rows, n_tiles, first_row = _token_tiles(dm, 256)
    n_out = 8

    def body(*refs):
        yc_hbm, yg_hbm, c_hbm, woc_ref, wog_ref, wo_ref, x_hbm, t_hbm, g_ref = refs[:9]
        outs, st_ref = refs[9:9 + n_out], refs[9 + n_out]
        ycbuf, ygbuf, cbuf, xbuf, tbuf = refs[10 + n_out:15 + n_out]
        obufs = refs[15 + n_out:15 + 2 * n_out]
        zbuf, zbuf2, sem_in, sem_out, sem_zero = refs[15 + 2 * n_out:]

        def loads(t, slot):
            padded = [(yc_hbm, ycbuf), (yg_hbm, ygbuf), (c_hbm, cbuf)]
            own = [(x_hbm, xbuf), (t_hbm, tbuf)]
            return ([pltpu.make_async_copy(h.at[pl.ds(first_row(t), rows), :], b.at[slot], sem_in.at[i, slot])
                     for i, (h, b) in enumerate(padded)] +
                    [pltpu.make_async_copy(h.at[pl.ds(t * rows, rows), :], b.at[slot], sem_in.at[3 + i, slot])
                     for i, (h, b) in enumerate(own)])

        def stores(t, slot):
            return [pltpu.make_async_copy(b.at[slot], h.at[pl.ds(first_row(t), rows), :], sem_out.at[i, slot])
                    for i, (h, b) in enumerate(zip(outs, obufs))]

        def compute(t, slot):
            mg_o, do_o, dy_o, dpc_o, dpg_o, dc_o, dyc_o, dyg_o = obufs
            pc = _dot(ycbuf[slot], woc_ref[...])
            pg = _dot(ygbuf[slot], wog_ref[...])
            sa = _sigmoid(cbuf[slot, :, :d].astype(F32))
            sb = _sigmoid(cbuf[slot, :, d:].astype(F32))
            merged = (sa * pc + sb * pg).astype(BF16)
            mg_o[slot] = merged
            out = _dot(merged, wo_ref[...])
            rstd = lax.rsqrt(jnp.mean(out * out, axis=-1, keepdims=True) + EPS)
            ohat = out * rstd
            err = xbuf[slot] + ohat * g_ref[...] - tbuf[slot]
            dy = err * (1.0 / d)
            d_oh = dy * g_ref[...]
            d_out = (rstd * (d_oh - ohat * jnp.mean(d_oh * ohat, axis=-1, keepdims=True))).astype(BF16)
            do_o[slot] = d_out
            dy_o[slot] = dy.astype(BF16)
            st_ref[0:1, :] += jnp.sum(dy * ohat, axis=0, keepdims=True)
            st_ref[1:2, :] += jnp.sum(err * err, axis=0, keepdims=True)
            dmg = _dot_nt(d_out, wo_ref[...])
            dpc = (dmg * sa).astype(BF16)
            dpg = (dmg * sb).astype(BF16)
            dpc_o[slot] = dpc
            dpg_o[slot] = dpg
            dc_o[slot, :, :d] = (dmg * pc * sa * (1.0 - sa)).astype(BF16)
            dc_o[slot, :, d:] = (dmg * pg * sb * (1.0 - sb)).astype(BF16)
            dyc_o[slot] = _dot_nt(dpc, woc_ref[...]).astype(BF16)
            dyg_o[slot] = _dot_nt(dpg, wog_ref[...]).astype(BF16)

        st_ref[...] = jnp.zeros_like(st_ref)
        zbuf[...] = jnp.zeros_like(zbuf)
        zbuf2[...] = jnp.zeros_like(zbuf2)
        zeros = [pltpu.make_async_copy(zbuf2 if out.shape[1] == 2 * d else zbuf, out.at[pl.ds(b * dm.LP, tm), :], sem_zero.at[i, b])
                 for i, out in enumerate(outs) for b in range(dm.Bl)]
        for cp in zeros:
            cp.start()
        _stream_tiles(n_tiles, loads, stores, compute)
        for cp in zeros:
            cp.wait()

    any_spec, vmem = pl.BlockSpec(memory_space=pl.ANY), pl.BlockSpec(memory_space=pltpu.VMEM)
    widths = [d, d, d, d, d, 2 * d, d, d]
    tile = lambda w, dt: pltpu.VMEM((2, rows, w), dt)
    return pl.pallas_call(
        body, name="head", in_specs=[any_spec] * 3 + [vmem] * 3 + [any_spec] * 2 + [vmem],
        out_specs=[any_spec] * n_out + [vmem],
        out_shape=[jax.ShapeDtypeStruct((dm.T, w), BF16) for w in widths] + [jax.ShapeDtypeStruct((8, d), F32)],
        scratch_shapes=[tile(d, BF16), tile(d, BF16), tile(2 * d, BF16), tile(d, F32), tile(d, F32)]
        + [tile(w, BF16) for w in widths]
        + [pltpu.VMEM((tm, d), BF16), pltpu.VMEM((tm, 2 * d), BF16), pltpu.SemaphoreType.DMA((5, 2)),
           pltpu.SemaphoreType.DMA((n_out, 2)), pltpu.SemaphoreType.DMA((n_out, dm.Bl))],
        compiler_params=pltpu.CompilerParams(vmem_limit_bytes=VMEM_LIMIT_BYTES),
    )(y_conv, y_gla, proj_c, w_oc, w_og, w_out, x.reshape(dm.Bl * dm.S, d), target.reshape(dm.Bl * dm.S, d), g_post)


def _grad_h(d_parts, gathered, dy, x, metapad, g_pre, dm):
    d, tm = dm.D, dm.TM
    rows, n_tiles, first_row = _token_tiles(dm, 256)
    widths = [a.shape[1] for a in d_parts]
    np_ = len(d_parts)

    def body(*refs):
        d_hbm, g_hbm, dy_hbm, x_hbm, mp_ref, g_ref = refs[:np_], refs[np_], refs[np_ + 1], refs[np_ + 2], refs[np_ + 3], refs[np_ + 4]
        gx_hbm, dmeta_ref, gg_ref = refs[np_ + 5:np_ + 8]
        parts, edges, sems = refs[np_ + 8:np_ + 12], refs[np_ + 12], refs[np_ + 13]
        dbufs = refs[np_ + 14:2 * np_ + 14]
        dybuf, xbuf, gbuf = refs[2 * np_ + 14:2 * np_ + 17]
        mbufs = refs[2 * np_ + 17:3 * np_ + 17]
        sem_in, sem_out, sem_meta = refs[3 * np_ + 17:]

        def grad_u(tiles):
            du = _dot(tiles[0].astype(BF16), parts[0][...])
            for a, w in zip(tiles[1:], parts[1:]):
                du = du + _dot(a.astype(BF16), w[...])
            return du

        def norm_bwd(h, du, dy):
            rstd = lax.rsqrt(jnp.mean(h * h, axis=-1, keepdims=True) + EPS)
            hhat = h * rstd
            dug = du * g_ref[...]
            gg_ref[0:1, :] += jnp.sum(du * hhat, axis=0, keepdims=True)
            return dy + rstd * (dug - hhat * jnp.mean(dug * hhat, axis=-1, keepdims=True))

        def loads(t, slot):
            padded = list(zip(d_hbm, dbufs)) + [(dy_hbm, dybuf)]
            return ([pltpu.make_async_copy(h.at[pl.ds(first_row(t), rows), :], b.at[slot], sem_in.at[i, slot])
                     for i, (h, b) in enumerate(padded)] +
                    [pltpu.make_async_copy(x_hbm.at[pl.ds(t * rows, rows), :], xbuf.at[slot], sem_in.at[np_ + 1, slot])])

        def stores(t, slot):
            return [pltpu.make_async_copy(gbuf.at[slot], gx_hbm.at[pl.ds(t * rows, rows), :], sem_out.at[slot])]

        def compute(t, slot):
            gbuf[slot] = norm_bwd(xbuf[slot], grad_u([b[slot] for b in dbufs]), dybuf[slot].astype(F32))

        gg_ref[...] = jnp.zeros_like(gg_ref)
        meta = [pltpu.make_async_copy(h.at[pl.ds(b * dm.LP, tm), :], buf.at[b], sem_meta.at[i, b])
                for i, (h, buf) in enumerate(zip(d_hbm, mbufs)) for b in range(dm.Bl)]
        for cp in meta:
            cp.start()
        _load_packed(g_hbm, parts, edges, sems, dm)
        _stream_tiles(n_tiles, loads, stores, compute)
        for cp in meta:
            cp.wait()
        for b in range(dm.Bl):
            dmeta_ref[b] = norm_bwd(mp_ref[...], grad_u([buf[b] for buf in mbufs]), 0.0)

    any_spec, vmem = pl.BlockSpec(memory_space=pl.ANY), pl.BlockSpec(memory_space=pltpu.VMEM)
    grad_x, d_meta, gg = pl.pallas_call(
        body, name="grad_h", in_specs=[any_spec] * (np_ + 3) + [vmem, vmem], out_specs=[any_spec, vmem, vmem],
        out_shape=[jax.ShapeDtypeStruct((dm.Bl * dm.S, d), F32), jax.ShapeDtypeStruct((dm.Bl, tm, d), F32),
                   jax.ShapeDtypeStruct((8, d), F32)],
        scratch_shapes=_packed_scratch(dm)
        + [pltpu.VMEM((2, rows, w), a.dtype) for w, a in zip(widths, d_parts)]
        + [pltpu.VMEM((2, rows, d), BF16), pltpu.VMEM((2, rows, d), F32), pltpu.VMEM((2, rows, d), F32)]
        + [pltpu.VMEM((dm.Bl, tm, w), a.dtype) for w, a in zip(widths, d_parts)]
        + [pltpu.SemaphoreType.DMA((np_ + 2, 2)), pltpu.SemaphoreType.DMA((2,)), pltpu.SemaphoreType.DMA((np_, dm.Bl))],
        compiler_params=pltpu.CompilerParams(vmem_limit_bytes=VMEM_LIMIT_BYTES),
    )(*d_parts, gathered, dy, x.reshape(dm.Bl * dm.S, d), metapad, g_pre)
    return grad_x.reshape(dm.Bl, dm.S, d), d_meta, gg


def _adamw(partials, w, m, v, name, by_columns=False):
    r, c = w.shape
    n_parts = partials.shape[0]
    tr, tc = (r, _pick(c, 128, 128)) if by_columns else (_pick(r, 256, 16), c)

    def body(p_ref, w_ref, m_ref, v_ref, g_ref, d_ref, nm_ref, nv_ref):
        g = p_ref[0].astype(F32)
        for j in range(1, n_parts):
            g = g + p_ref[j].astype(F32)
        g_ref[...] = g
        d_ref[...], nm_ref[...], nv_ref[...] = _adam_step(g, w_ref[...], m_ref[...], v_ref[...])

    at = (lambda i: (0, i)) if by_columns else (lambda i: (i, 0))
    tile = pl.BlockSpec((tr, tc), at)
    out = jax.ShapeDtypeStruct((r, c), F32)
    return pl.pallas_call(
        body, name=name, grid=(c // tc if by_columns else r // tr,),
        in_specs=[pl.BlockSpec((n_parts, tr, tc), lambda i: (0,) + at(i)), tile, tile, tile],
        out_specs=[tile, tile, tile, tile], out_shape=[out, out, out, out], compiler_params=_cp(1),
    )(partials, w, m, v)


def _adam_step(g, w, m, v):
    m2 = ADAM_B1 * m + (1.0 - ADAM_B1) * g
    v2 = ADAM_B2 * v + (1.0 - ADAM_B2) * (g * g)
    m_hat = m2 / (1.0 - ADAM_B1 ** ADAM_STEP)
    v_hat = v2 / (1.0 - ADAM_B2 ** ADAM_STEP)
    return -ADAM_LR * (m_hat / (jnp.sqrt(v_hat) + ADAM_EPS) + ADAM_WD * w), m2, v2


def _adamw_small(items, name):
    n = len(items)

    def body(*refs):
        ins, outs = refs[:4 * n], refs[4 * n:]
        for i in range(n):
            p_ref, w_ref, m_ref, v_ref = ins[4 * i:4 * i + 4]
            g = p_ref[0]
            for j in range(1, p_ref.shape[0]):
                g = g + p_ref[j]
            delta, m2, v2 = _adam_step(g, w_ref[...], m_ref[...], v_ref[...])
            for o_ref, val in zip(outs[4 * i:4 * i + 4], (g, delta, m2, v2)):
                o_ref[...] = val

    vmem = pl.BlockSpec(memory_space=pltpu.VMEM)
    res = pl.pallas_call(
        body, name=name, in_specs=[vmem] * (4 * n), out_specs=[vmem] * (4 * n),
        out_shape=[jax.ShapeDtypeStruct(w.shape, F32) for _, w, _, _ in items for _ in range(4)],
    )(*[a for item in items for a in item])
    return [res[4 * i:4 * i + 4] for i in range(n)]


def _unpack_rows(a, b, c, lr, dm):
    d, hk, hv, cw, nj, hw = dm.D, dm.HK, dm.HV, dm.CW, dm.NJ, dm.HW
    conv = a.reshape(nj, 4, cw, d).transpose(1, 0, 2, 3).reshape(4 * d, d)
    heads = b.reshape(HEADS, hw, d)
    q = heads[:, :hk].reshape(HEADS * hk, d)
    k = heads[:, hk:2 * hk].reshape(HEADS * hk, d)
    v = heads[:, 2 * hk:2 * hk + hv].reshape(HEADS * hv, d)
    r = heads[:, 2 * hk + hv:].reshape(HEADS * hv, d)
    return jnp.concatenate([conv, q, k, v, r, lr[:2 * RANK], c], axis=0)


def _column_shards(g, shard_shape):
    r, c = g.shape
    return g.reshape(r, N_DEV, c // N_DEV).transpose(1, 0, 2).reshape((N_DEV,) + tuple(shard_shape))


def _join_column_shards(parts):
    r, c = parts.shape[-2:]
    return parts.reshape(N_DEV, r, c).transpose(1, 0, 2).reshape(r, N_DEV * c)


def _local_step(x, target, meta, g_pre, u, wt_shards, conv_w, wg_f, bg_f, wg_b, bg_b, gla_g, out_weights, g_post,
                on_matrix_grads=None):
    bl, s, d = x.shape
    dm = _Dims(bl, s, d)
    metapad = jnp.concatenate([jnp.zeros((dm.TM - N_META, d), F32), meta], axis=0)
    wgp_f = jnp.pad(wg_f, ((0, LR_LANES - RANK), (0, 0))).astype(BF16)
    wgp_b = jnp.pad(wg_b, ((RANK, LR_LANES - 2 * RANK), (0, 0))).astype(BF16)

    u = _prenorm_meta(u, metapad, g_pre, dm)
    proj_a, proj_b, proj_c, lr = _inproj(u, wt_shards, dm)
    y_conv = _conv_fwd(proj_a, conv_w, dm)
    o_all, y_gla, states, decays, gate_slopes = _gla_fwd(proj_b, lr, wgp_f, bg_f, wgp_b, bg_b, gla_g, dm)
    w_oc, w_og, w_out = out_weights(y_conv) if callable(out_weights) else out_weights
    merged, d_out, dy, d_pc, d_pg, d_c, dy_conv, dy_gla, stats = _head(y_conv, y_gla, proj_c, w_oc, w_og, w_out, x, target,
                                                                        g_post, dm)
    loss = 0.5 / d * jnp.sum(stats[1])

    g_out = _matmul_tn(merged, d_out, BF16, "grad_w_out")
    g_oc = _matmul_tn(y_conv, d_pc, BF16, "grad_w_out_conv")
    g_og = _matmul_tn(y_gla, d_pg, BF16, "grad_w_out_gla")
    if on_matrix_grads is not None:
        conv_w = conv_w + on_matrix_grads(dict(w_out_conv=g_oc, w_out_gla=g_og, w_merge_out=g_out))
    d_a, g_conv = _conv_bwd(proj_a, dy_conv, conv_w, dm)
    d_b, d_lr, gwp_f, gbp_f, gwp_b, gbp_b, g_gla = _gla_bwd(proj_b, lr, o_all, dy_gla, states, decays, gate_slopes, wgp_f, wgp_b, gla_g, dm)
    g_in = _unpack_rows(_matmul_tn(d_a, u, BF16, "grad_w_in_conv"), _matmul_tn(d_b, u, BF16, "grad_w_in_gla"),
                        _matmul_tn(d_c, u, BF16, "grad_w_in_merge"), _matmul_tn(d_lr, u, BF16, "grad_w_in_gate"), dm)
    if on_matrix_grads is not None:
        d_lr = d_lr + on_matrix_grads(dict(w_in=g_in))
    grad_x, d_meta, g_pre_rows = _grad_h([d_a, d_b, d_c, d_lr], wt_shards, dy, x, metapad, g_pre, dm)

    grads = dict(
        meta_tokens=jnp.sum(d_meta[:, dm.TM - N_META:, :], axis=0), norm_pre=g_pre_rows[0:1], w_in=g_in,
        conv_w=g_conv[0:3], w_gate_fwd=jnp.sum(gwp_f, axis=0)[:RANK], b_gate_fwd=jnp.sum(gbp_f, axis=0)[0:1],
        w_gate_bwd=jnp.sum(gwp_b, axis=0)[RANK:2 * RANK], b_gate_bwd=jnp.sum(gbp_b, axis=0)[0:1],
        gla_norm=g_gla[0:1], w_out_conv=g_oc, w_out_gla=g_og, w_merge_out=g_out, norm_post=stats[0:1])
    return loss, grad_x, grads


MATRICES = ("w_out_conv", "w_out_gla", "w_merge_out")
SMALL_SHARDED = ("meta_tokens", "conv_w", "w_gate_fwd", "w_gate_bwd")
REPLICATED = ("norm_pre", "b_gate_fwd", "b_gate_bwd", "gla_norm", "norm_post")
NAMES = ("meta_tokens", "norm_pre", "w_in", "conv_w", "w_gate_fwd", "b_gate_fwd", "w_gate_bwd", "b_gate_bwd", "gla_norm",
         "w_out_conv", "w_out_gla", "w_merge_out", "norm_post")


def kernel(x, meta_tokens, norm_pre, w_in, conv_w, w_gate_fwd, b_gate_fwd, w_gate_bwd, b_gate_bwd, gla_norm, w_out_conv, w_out_gla, w_merge_out, norm_post, loss_target, m_meta_tokens, m_norm_pre, m_w_in, m_conv_w, m_w_gate_fwd, m_b_gate_fwd, m_w_gate_bwd, m_b_gate_bwd, m_gla_norm, m_w_out_conv, m_w_out_gla, m_w_merge_out, m_norm_post, v_meta_tokens, v_norm_pre, v_w_in, v_conv_w, v_w_gate_fwd, v_b_gate_fwd, v_w_gate_bwd, v_b_gate_bwd, v_gla_norm, v_w_out_conv, v_w_out_gla, v_w_merge_out, v_norm_post):
    w = dict(meta_tokens=meta_tokens, norm_pre=norm_pre, w_in=w_in[0], conv_w=conv_w, w_gate_fwd=w_gate_fwd,
             b_gate_fwd=b_gate_fwd, w_gate_bwd=w_gate_bwd, b_gate_bwd=b_gate_bwd, gla_norm=gla_norm,
             w_out_conv=w_out_conv[0], w_out_gla=w_out_gla[0], w_merge_out=w_merge_out[0], norm_post=norm_post)
    m = dict(meta_tokens=m_meta_tokens, norm_pre=m_norm_pre, w_in=m_w_in[0], conv_w=m_conv_w, w_gate_fwd=m_w_gate_fwd,
             b_gate_fwd=m_b_gate_fwd, w_gate_bwd=m_w_gate_bwd, b_gate_bwd=m_b_gate_bwd, gla_norm=m_gla_norm,
             w_out_conv=m_w_out_conv[0], w_out_gla=m_w_out_gla[0], w_merge_out=m_w_merge_out[0], norm_post=m_norm_post)
    v = dict(meta_tokens=v_meta_tokens, norm_pre=v_norm_pre, w_in=v_w_in[0], conv_w=v_conv_w, w_gate_fwd=v_w_gate_fwd,
             b_gate_fwd=v_b_gate_fwd, w_gate_bwd=v_w_gate_bwd, b_gate_bwd=v_b_gate_bwd, gla_norm=v_gla_norm,
             w_out_conv=v_w_out_conv[0], w_out_gla=v_w_out_gla[0], w_merge_out=v_w_merge_out[0], norm_post=v_norm_post)
    d = x.shape[-1]

    dm = _Dims(*x.shape)
    me = 4 * lax.axis_index("x") + 2 * lax.axis_index("y") + lax.axis_index("c")
    wt_shards, *small_all, u = _gather_two_level(
        [_pad_shard(w["w_in"].T.astype(BF16), me)] + [w[n] for n in SMALL_SHARDED], "gather_weights",
        _prenorm_tokens_side(x, norm_pre, dm))
    _, late_weights = _exchange_start([w[n].astype(BF16) for n in MATRICES], [], small_all[0], "gather_out_weights_start")
    small = {n: _join_column_shards(p) for n, p in zip(SMALL_SHARDED, small_all)}

    def out_weights(after):
        return tuple(a.reshape(-1, d) for a in _exchange_wait(late_weights, after, "gather_out_weights_wait"))

    pending = []

    def on_matrix_grads(g):
        token, state = _exchange_start([], [t.reshape(N_DEV, -1, d) for t in g.values()], None,
                                       "exchange_grads_start_" + "_".join(g))
        pending.append((tuple(g), state))
        return token

    loss, grad_x, grads = _local_step(
        x, loss_target, small["meta_tokens"], norm_pre, u, wt_shards, small["conv_w"], small["w_gate_fwd"], b_gate_fwd,
        small["w_gate_bwd"], b_gate_bwd, gla_norm, out_weights, norm_post, on_matrix_grads)
    loss = lax.psum(loss, ("x", "y", "c"))
    received = {}
    for names, state in pending:
        received.update(zip(names, _exchange_wait(state, grad_x, "exchange_grads_wait_" + "_".join(names))))

    small_recv = _exchange([grads[n] for n in REPLICATED], [_column_shards(grads[n], w[n].shape) for n in SMALL_SHARDED],
                           "exchange_small_grads")

    results = {"w_in": [r.T[None] for r in _adamw(received["w_in"], w["w_in"].T, m["w_in"].T, v["w_in"].T, "adamw_w_in", by_columns=True)]}
    for n in MATRICES:
        results[n] = [r[None] for r in _adamw(received[n], w[n], m[n], v[n], "adamw_" + n)]
    small_names = REPLICATED + SMALL_SHARDED
    results.update(zip(small_names, _adamw_small([(p, w[n], m[n], v[n]) for n, p in zip(small_names, small_recv)], "adamw_small")))
    return (loss, grad_x, *[results[n][i] for i in range(4) for n in NAMES])
```

```python
import jax
import jax.numpy as jnp
from jax import lax
from jax.experimental import pallas as pl
from jax.experimental.pallas import tpu as pltpu

F32 = jnp.float32
BF16 = jnp.bfloat16
MESH = pl.DeviceIdType.MESH

N_META = 16
CHUNK = 64
CHUNK_SHIFT = 6
HEADS = 4
RANK = 16
LR_LANES = 128
PAD_ROWS = CHUNK - N_META
EPS = 1e-6
GATE_NORMALIZER = 16.0
N_DEV = 8
ADAM_LR, ADAM_B1, ADAM_B2, ADAM_EPS, ADAM_WD, ADAM_STEP = 0.001, 0.9, 0.999, 1e-08, 0.01, 10
VMEM_LIMIT_BYTES = 56 * 1024 * 1024


class _Dims:
    def __init__(self, bl, s, d):
        self.Bl, self.S, self.D = bl, s, d
        self.TM = CHUNK
        self.LP = self.TM + s
        self.T = bl * self.LP
        self.TPS = self.LP // self.TM
        self.NC = self.LP // CHUNK
        self.C0 = (self.TM - CHUNK) // CHUNK
        self.DK, self.DV = d // 2, d
        self.HK, self.HV = self.DK // HEADS, self.DV // HEADS
        self.HW = 2 * self.HK + 2 * self.HV
        self.CW = 256 if d % 256 == 0 and d > 256 else d // 4
        self.NJ = d // self.CW


def _pick(n, target, mult):
    t = min(n, target)
    while t >= mult:
        if n % t == 0 and t % mult == 0:
            return t
        t -= mult
    return n


def _cp(n_axes):
    return pltpu.CompilerParams(dimension_semantics=("arbitrary",) * n_axes, vmem_limit_bytes=VMEM_LIMIT_BYTES)


def _sigmoid(x):
    return 1.0 / (1.0 + jnp.exp(-x))


def _dot(a, b):
    return jnp.dot(a, b, preferred_element_type=F32)


def _dot_nt(a, b):
    return lax.dot_general(a, b, (((1,), (1,)), ((), ())), preferred_element_type=F32)


def _dot_tn(a, b):
    return lax.dot_general(a, b, (((0,), (0,)), ((), ())), preferred_element_type=F32)


def _dot_exact01(m01, x):
    hi = x.astype(BF16)
    lo = (x - hi.astype(F32)).astype(BF16)
    return _dot(m01, hi) + _dot(m01, lo)


def _exchange(gathers, scatters, name):
    arrays = list(gathers) + list(scatters)
    n, ng = len(arrays), len(gathers)

    def body(*refs):
        ins, outs = refs[:n], refs[n:2 * n]
        send_sems, recv_sems, local_sems = refs[2 * n:]
        x, y, c = lax.axis_index("x"), lax.axis_index("y"), lax.axis_index("c")
        me = 4 * x + 2 * y + c
        started = []
        for t in range(n):
            src, dst = ins[t], outs[t]
            own = pltpu.make_async_copy(src if t < ng else src.at[me], dst.at[me], local_sems.at[t])
            own.start()
            started.append(own)
            for k, pos, peer in _peers(x, y, c):
                cp = pltpu.make_async_remote_copy(
                    src_ref=src if t < ng else src.at[peer], dst_ref=dst.at[me],
                    send_sem=send_sems.at[t * (N_DEV - 1) + k - 1], recv_sem=recv_sems.at[t * (N_DEV - 1) + k - 1],
                    device_id=pos, device_id_type=MESH)
                cp.start()
                started.append(cp)
        for cp in started:
            cp.wait()

    out_shape = [jax.ShapeDtypeStruct((N_DEV,) + a.shape if t < ng else a.shape, a.dtype) for t, a in enumerate(arrays)]
    any_spec = pl.BlockSpec(memory_space=pl.ANY)
    return pl.pallas_call(
        body, name=name, out_shape=out_shape, in_specs=[any_spec] * n, out_specs=[any_spec] * n,
        scratch_shapes=[pltpu.SemaphoreType.DMA((n * (N_DEV - 1),)), pltpu.SemaphoreType.DMA((n * (N_DEV - 1),)),
                        pltpu.SemaphoreType.DMA((n,))],
        compiler_params=pltpu.CompilerParams(has_side_effects=True),
    )(*arrays)


def _gather_two_level(arrays, name, side=None):
    n = len(arrays)
    per = N_DEV - 1
    work, side_in, side_in_specs, side_out, side_out_specs, side_scratch = side or (None, [], [], [], [], [])
    n_in, n_out = len(side_in), len(side_out)

    def body(*refs):
        ins, outs = refs[:n], refs[n + n_in:2 * n + n_in]
        send_sems, recv_sems, local_sems = refs[2 * n + n_in + n_out:2 * n + n_in + n_out + 3]
        x, y, c = lax.axis_index("x"), lax.axis_index("y"), lax.axis_index("c")
        sibling = (x, y, 1 - c)
        chips = [(1 - x, y), (x, 1 - y), (1 - x, 1 - y)]
        index = lambda px, py, pc: 4 * px + 2 * py + pc

        def copy(t, k, block, to, from_input=False):
            slab = outs[t].at[index(*block)]
            return pltpu.make_async_remote_copy(
                src_ref=ins[t] if from_input else slab, dst_ref=slab, send_sem=send_sems.at[t * per + k],
                recv_sem=recv_sems.at[t * per + k], device_id=to, device_id_type=MESH)

        own, sent = [], []
        for t in range(n):
            own.append(pltpu.make_async_copy(ins[t], outs[t].at[index(x, y, c)], local_sems.at[t]))
            own[-1].start()
            first = [copy(t, 0, (x, y, c), sibling, True)]
            first += [copy(t, 1 + j, (x, y, c), (*chip, c), True) for j, chip in enumerate(chips)]
            for cp in first:
                cp.start()
            sent += first
        if work is not None:
            work(refs[n:n + n_in], refs[2 * n + n_in:2 * n + n_in + n_out], refs[2 * n + n_in + n_out + 3:])
        for t in range(n):
            for j, chip in enumerate(chips):
                copy(t, 1 + j, (*chip, c), (x, y, c)).wait_recv()
                sent.append(copy(t, 4 + j, (*chip, c), sibling))
                sent[-1].start()
        for t in range(n):
            copy(t, 0, sibling, (x, y, c)).wait_recv()
            for j, chip in enumerate(chips):
                copy(t, 4 + j, (*chip, 1 - c), (x, y, c)).wait_recv()
        for cp in sent:
            cp.wait_send()
        for cp in own:
            cp.wait()

    out_shape = [jax.ShapeDtypeStruct((N_DEV,) + a.shape, a.dtype) for a in arrays]
    any_spec = pl.BlockSpec(memory_space=pl.ANY)
    return pl.pallas_call(
        body, name=name, out_shape=out_shape + list(side_out), in_specs=[any_spec] * n + list(side_in_specs),
        out_specs=[any_spec] * n + list(side_out_specs),
        scratch_shapes=[pltpu.SemaphoreType.DMA((n * per,)), pltpu.SemaphoreType.DMA((n * per,)),
                        pltpu.SemaphoreType.DMA((n,))] + list(side_scratch),
        compiler_params=pltpu.CompilerParams(has_side_effects=True, vmem_limit_bytes=VMEM_LIMIT_BYTES),
    )(*arrays, *side_in)


def _peers(x, y, c):
    out = []
    for k in range(1, N_DEV):
        px = 1 - x if (k >> 2) & 1 else x
        py = 1 - y if (k >> 1) & 1 else y
        pc = 1 - c if k & 1 else c
        out.append((k, (px, py, pc), 4 * px + 2 * py + pc))
    return out


def _exchange_start(gathers, scatters, after, name):
    arrays = list(gathers) + list(scatters)
    n, ng = len(arrays), len(gathers)
    hbm = pl.BlockSpec(memory_space=pltpu.HBM)
    sem = pl.BlockSpec(memory_space=pltpu.SEMAPHORE)

    extra = [] if after is None else [after]
    ne = len(extra)

    def body(*refs):
        ins, lands = refs[:n], refs[n:2 * n]
        send_sems, recv_sems = refs[2 * n + ne], refs[2 * n + ne + 1]
        token = refs[4 * n + ne + 2]
        x, y, c = lax.axis_index("x"), lax.axis_index("y"), lax.axis_index("c")
        me = 4 * x + 2 * y + c
        for t in range(n):
            for k, pos, peer in _peers(x, y, c):
                pltpu.make_async_remote_copy(
                    src_ref=ins[t] if t < ng else ins[t].at[peer], dst_ref=lands[t].at[me],
                    send_sem=send_sems.at[t * (N_DEV - 1) + k - 1], recv_sem=recv_sems.at[t * (N_DEV - 1) + k - 1],
                    device_id=pos, device_id_type=MESH).start()
        token[...] = jnp.zeros_like(token)

    me = 4 * lax.axis_index("x") + 2 * lax.axis_index("y") + lax.axis_index("c")
    lands = [lax.dynamic_update_index_in_dim(lax.empty((N_DEV,) + a.shape if t < ng else a.shape, a.dtype),
                                             a if t < ng else lax.dynamic_index_in_dim(a, me, 0, keepdims=False), me, 0)
             for t, a in enumerate(arrays)]
    operands = [pltpu.with_memory_space_constraint(a, pltpu.HBM) for a in arrays + lands]
    sems = pltpu.SemaphoreType.DMA((n * (N_DEV - 1),))
    res = pl.pallas_call(
        body, name=name,
        out_shape=(sems, sems, *[pltpu.HBM(a.shape, a.dtype) for a in arrays + lands], jax.ShapeDtypeStruct((8, 128), F32)),
        in_specs=[hbm] * (2 * n) + [pl.BlockSpec(memory_space=pl.ANY)] * ne,
        out_specs=(sem, sem, *[hbm] * (2 * n), pl.BlockSpec(memory_space=pltpu.VMEM)),
        input_output_aliases={i: 2 + i for i in range(2 * n)},
        compiler_params=pltpu.CompilerParams(has_side_effects=pltpu.SideEffectType.DATAFLOW_SIDE_EFFECTING),
    )(*operands, *extra)
    return res[-1][0, 0], (ng, res[0], res[1], list(res[2:2 + n]), list(res[2 + n:2 + 2 * n]))


def _exchange_wait(state, after, name):
    ng, send_sems, recv_sems, sent, lands = state
    n = len(sent)
    hbm = pl.BlockSpec(memory_space=pltpu.HBM)
    sem = pl.BlockSpec(memory_space=pltpu.SEMAPHORE)

    def body(*refs):
        ins, land_refs = refs[:n], refs[n:2 * n]
        send_ref, recv_ref = refs[2 * n], refs[2 * n + 1]
        x, y, c = lax.axis_index("x"), lax.axis_index("y"), lax.axis_index("c")
        me = 4 * x + 2 * y + c
        for t in range(n):
            for k, pos, peer in _peers(x, y, c):
                cp = pltpu.make_async_remote_copy(
                    src_ref=ins[t] if t < ng else ins[t].at[peer], dst_ref=land_refs[t].at[me],
                    send_sem=send_ref.at[t * (N_DEV - 1) + k - 1], recv_sem=recv_ref.at[t * (N_DEV - 1) + k - 1],
                    device_id=pos, device_id_type=MESH)
                cp.wait_send()
                cp.wait_recv()

    res = pl.pallas_call(
        body, name=name, out_shape=tuple(pltpu.HBM(a.shape, a.dtype) for a in sent + lands),
        in_specs=[hbm] * (2 * n) + [sem, sem, pl.BlockSpec(memory_space=pl.ANY)], out_specs=tuple([hbm] * (2 * n)),
        input_output_aliases={i: i for i in range(2 * n)},
        compiler_params=pltpu.CompilerParams(has_side_effects=pltpu.SideEffectType.DATAFLOW_SIDE_EFFECTING),
    )(*sent, *lands, send_sems, recv_sems, after)
    return list(res[n:])


def _rms_scaled(h, g):
    return (h * lax.rsqrt(jnp.mean(h * h, axis=-1, keepdims=True) + EPS) * g).astype(BF16)


def _prenorm_tokens_side(x, g_pre, dm):
    bl, s, d = x.shape
    rows = _pick(s, 512, 16)
    tiles = [(b, j) for b in range(bl) for j in range(s // rows)]

    def work(ins, outs, scratch):
        (x_ref, g_ref), (u_ref,), (xbuf, ubuf, sem_in, sem_out) = ins, outs, scratch

        def load(t, slot):
            b, j = tiles[t]
            return pltpu.make_async_copy(x_ref.at[b, pl.ds(j * rows, rows), :], xbuf.at[slot], sem_in.at[slot])

        def store(t, slot):
            b, j = tiles[t]
            return pltpu.make_async_copy(ubuf.at[slot], u_ref.at[pl.ds(b * dm.LP + dm.TM + j * rows, rows), :], sem_out.at[slot])

        load(0, 0).start()
        for t in range(len(tiles)):
            slot = t % 2
            if t + 1 < len(tiles):
                load(t + 1, 1 - slot).start()
            load(t, slot).wait()
            if t >= 2:
                store(t - 2, slot).wait()
            ubuf[slot] = _rms_scaled(xbuf[slot], g_ref[...])
            store(t, slot).start()
        for t in range(max(len(tiles) - 2, 0), len(tiles)):
            store(t, t % 2).wait()

    any_spec = pl.BlockSpec(memory_space=pl.ANY)
    return (work, [x, g_pre], [any_spec, pl.BlockSpec(memory_space=pltpu.VMEM)],
            [jax.ShapeDtypeStruct((dm.T, d), BF16)], [any_spec],
            [pltpu.VMEM((2, rows, d), F32), pltpu.VMEM((2, rows, d), BF16), pltpu.SemaphoreType.DMA((2,)),
             pltpu.SemaphoreType.DMA((2,))])


def _prenorm_meta(u, metapad, g_pre, dm):
    tm, tps, d = dm.TM, dm.TPS, dm.D

    def body(u_in, mp_ref, g_ref, u_ref):
        u_ref[...] = _rms_scaled(mp_ref[...], g_ref[...])

    return pl.pallas_call(
        body, name="prenorm_meta", grid=(dm.Bl,),
        in_specs=[pl.BlockSpec(memory_space=pl.ANY), pl.BlockSpec((tm, d), lambda i: (0, 0)),
                  pl.BlockSpec((1, d), lambda i: (0, 0))],
        out_specs=pl.BlockSpec((tm, d), lambda i: (i * tps, 0)),
        out_shape=jax.ShapeDtypeStruct((dm.T, d), BF16), input_output_aliases={0: 0}, compiler_params=_cp(1),
    )(u, metapad, g_pre)


def _matmul_tn(a, b, out_dtype, name, tt=2304, tn=1024, tk=1024):
    t, k = a.shape
    n = b.shape[1]
    tt, tn, tk = _pick(t, tt, 16), _pick(n, tn, 128), _pick(k, tk, 128)
    nt = t // tt

    def body(a_ref, b_ref, o_ref, acc):
        p = _dot_tn(a_ref[...].astype(BF16), b_ref[...].astype(BF16))
        i = pl.program_id(2)

        @pl.when(i == 0)
        def _():
            acc[...] = p

        @pl.when(i > 0)
        def _():
            acc[...] += p

        @pl.when(i == nt - 1)
        def _():
            o_ref[...] = acc[...].astype(out_dtype)

    return pl.pallas_call(
        body, name=name, grid=(k // tk, n // tn, nt),
        in_specs=[pl.BlockSpec((tt, tk), lambda kk, j, i: (i, kk)), pl.BlockSpec((tt, tn), lambda kk, j, i: (i, j))],
        out_specs=pl.BlockSpec((tk, tn), lambda kk, j, i: (kk, j)),
        out_shape=jax.ShapeDtypeStruct((k, n), out_dtype), scratch_shapes=[pltpu.VMEM((tk, tn), F32)],
        compiler_params=_cp(3),
    )(a, b)


def _matmul_tn_group(a_list, b, name, tt=2304, tile=1024):
    t, n = b.shape
    tt = _pick(t, tt, 16)
    nt = t // tt
    counts = [a.shape[1] // tile for a in a_list]
    starts = [sum(counts[:m]) for m in range(len(a_list))]
    items = sum(counts)

    def active(p, m):
        return (p >= starts[m]) & (p < starts[m] + counts[m])

    def body(*refs):
        a_refs, b_ref = refs[:len(a_list)], refs[len(a_list)]
        o_ref, acc = refs[-2], refs[-1]
        p, i = pl.program_id(0), pl.program_id(1)
        for m, a_ref in enumerate(a_refs):
            @pl.when(active(p, m))
            def _(a_ref=a_ref):
                prod = _dot_tn(a_ref[...].astype(BF16), b_ref[...].astype(BF16))

                @pl.when(i == 0)
                def _():
                    acc[...] = prod

                @pl.when(i > 0)
                def _():
                    acc[...] += prod

        @pl.when(i == nt - 1)
        def _():
            o_ref[0] = acc[...].astype(BF16)

    a_specs = [pl.BlockSpec((tt, tile), lambda p, i, m=m: (jnp.where(active(p, m), i, 0), jnp.where(active(p, m), p - starts[m], 0)))
               for m in range(len(a_list))]
    return pl.pallas_call(
        body, name=name, grid=(items, nt), in_specs=a_specs + [pl.BlockSpec((tt, n), lambda p, i: (i, 0))],
        out_specs=pl.BlockSpec((1, tile, n), lambda p, i: (p, 0, 0)),
        out_shape=jax.ShapeDtypeStruct((items, tile, n), BF16), scratch_shapes=[pltpu.VMEM((tile, n), F32)],
        compiler_params=_cp(2),
    )(*a_list, b)


BF16_TILE_ROWS = 16


def _shard_offset(index, shard_rows):
    return (index * shard_rows) % BF16_TILE_ROWS


def _pad_shard(wt_shard, index):
    rows, d = wt_shard.shape
    padded = -(-(rows + max(_shard_offset(j, rows) for j in range(N_DEV))) // BF16_TILE_ROWS) * BF16_TILE_ROWS
    return lax.dynamic_update_slice(jnp.zeros((padded, d), wt_shard.dtype), wt_shard, (_shard_offset(index, rows), 0))


def _packed_parts(dm):
    d, dk, hk, hv, cw, nj, hw = dm.D, dm.DK, dm.HK, dm.HV, dm.CW, dm.NJ, dm.HW
    blocks = [(0, (j * 4 + p) * cw, p * d + j * cw, cw) for j in range(nj) for p in range(4)]
    for h in range(HEADS):
        blocks += [(1, h * hw, 4 * d + h * hk, hk), (1, h * hw + hk, 4 * d + dk + h * hk, hk),
                   (1, h * hw + 2 * hk, 5 * d + h * hv, hv), (1, h * hw + 2 * hk + hv, 6 * d + h * hv, hv)]
    blocks += [(2, 0, 7 * d + 2 * RANK, 2 * d), (3, 0, 7 * d, 2 * RANK)]
    return [4 * d, 3 * d, 2 * d, LR_LANES], blocks


def _pack_plan(dm):
    sh = (9 * dm.D + 2 * RANK) // N_DEV
    tile = BF16_TILE_ROWS
    copies, straddles = [], []
    for part, dst, r0, n in _packed_parts(dm)[1]:
        for j in range(N_DEV):
            a, b = max(r0, sh * j), min(r0 + n, sh * (j + 1))
            if a >= b:
                continue
            a_up, b_down = -(-a // tile) * tile, b // tile * tile
            if b_down > a_up:
                copies.append((j, a_up - sh * j + _shard_offset(j, sh), b_down - a_up, part, dst + a_up - r0))
            if a % tile:
                lo = a // tile * tile
                straddles.append((j, lo - sh * (j - 1) + _shard_offset(j - 1, sh), part, dst + lo - r0, a - lo))
    return copies, straddles


def _packed_scratch(dm):
    copies, straddles = _pack_plan(dm)
    return ([pltpu.VMEM((rows, dm.D), BF16) for rows in _packed_parts(dm)[0]]
            + [pltpu.VMEM((2 * max(len(straddles), 1), BF16_TILE_ROWS, dm.D), BF16),
               pltpu.SemaphoreType.DMA((len(copies) + 2 * len(straddles),))])


def _load_packed(g_ref, parts, edges, sems, dm):
    copies, straddles = _pack_plan(dm)
    tile = BF16_TILE_ROWS
    parts[3][2 * RANK:, :] = jnp.zeros((LR_LANES - 2 * RANK, dm.D), BF16)
    dmas = [pltpu.make_async_copy(g_ref.at[j, pl.ds(src, n), :], parts[p].at[pl.ds(dst, n), :], sems.at[i])
            for i, (j, src, n, p, dst) in enumerate(copies)]
    for i, (j, src, p, dst, split) in enumerate(straddles):
        k = len(copies) + 2 * i
        dmas.append(pltpu.make_async_copy(g_ref.at[j - 1, pl.ds(src, tile), :], edges.at[2 * i], sems.at[k]))
        dmas.append(pltpu.make_async_copy(g_ref.at[j, pl.ds(0, tile), :], edges.at[2 * i + 1], sems.at[k + 1]))
    for cp in dmas:
        cp.start()
    for cp in dmas:
        cp.wait()
    row = lax.broadcasted_iota(jnp.int32, (tile, dm.D), 0)
    for i, (j, src, p, dst, split) in enumerate(straddles):
        parts[p][dst:dst + tile, :] = jnp.where(row < split, edges[2 * i], edges[2 * i + 1])


def _inproj(u, gathered, dm):
    t, d = u.shape
    tm = _pick(t, 512, 16)
    widths = _packed_parts(dm)[0]
    cn = 1024

    def body(u_ref, g_ref, *rest):
        outs, parts, (edges, sems) = rest[:4], rest[4:8], rest[8:]

        @pl.when(pl.program_id(0) == 0)
        def _():
            _load_packed(g_ref, parts, edges, sems, dm)

        ut = u_ref[...]
        for w, o_ref in zip(parts, outs):
            n = w.shape[0]
            step = cn if n % cn == 0 else n
            for j in range(0, n, step):
                o_ref[:, j:j + step] = _dot_nt(ut, w[j:j + step, :]).astype(BF16)

    return pl.pallas_call(
        body, name="inproj", grid=(t // tm,),
        in_specs=[pl.BlockSpec((tm, d), lambda i: (i, 0)), pl.BlockSpec(memory_space=pl.ANY)],
        out_specs=[pl.BlockSpec((tm, w), lambda i: (i, 0)) for w in widths],
        out_shape=[jax.ShapeDtypeStruct((t, w), BF16) for w in widths],
        scratch_shapes=_packed_scratch(dm), compiler_params=_cp(1),
    )(u, gathered)


def _conv_rows(dm):
    return _pick(dm.LP, 256, 16)


def _shifted(m, prev_row, next_row, rows):
    row = lax.broadcasted_iota(jnp.int32, m.shape, 0)
    m_prev = jnp.where(row == 0, prev_row, pltpu.roll(m, 1, 0))
    m_next = jnp.where(row == rows - 1, next_row, pltpu.roll(m, rows - 1, 0))
    return m_prev, m_next


def _conv_fwd(proj_a, conv_w, dm):
    lp, cw, rc = dm.LP, dm.CW, _conv_rows(dm)
    nchunk = lp // rc

    def body(p_ref, w_ref, y_ref):
        w0, w1, w2 = w_ref[0:1, :], w_ref[1:2, :], w_ref[2:3, :]

        def chunk(ci, carry):
            r0 = pl.multiple_of(ci * rc, rc)
            blk = p_ref[pl.ds(r0, rc), :].astype(F32)
            cb, cc, cx, cz = (blk[:, i * cw:(i + 1) * cw] for i in range(4))
            m = cc * cx
            rp = pl.multiple_of(jnp.maximum(r0 - 16, 0), 16)
            rn = pl.multiple_of(jnp.minimum(r0 + rc, lp - 16), 16)
            pv = p_ref[pl.ds(rp, 16), cw:3 * cw].astype(F32)
            nx = p_ref[pl.ds(rn, 16), cw:3 * cw].astype(F32)
            prev_row = jnp.where(ci > 0, pv[15:16, :cw] * pv[15:16, cw:], 0.0)
            next_row = jnp.where(ci < nchunk - 1, nx[0:1, :cw] * nx[0:1, cw:], 0.0)
            m_prev, m_next = _shifted(m, prev_row, next_row, rc)
            s = w0 * m_prev + w1 * m + w2 * m_next
            y_ref[pl.ds(r0, rc), :] = (cb * s * (cz * _sigmoid(cz))).astype(BF16)
            return carry

        lax.fori_loop(0, nchunk, chunk, 0)

    return pl.pallas_call(
        body, name="conv_fwd", grid=(dm.Bl, dm.NJ),
        in_specs=[pl.BlockSpec((lp, 4 * cw), lambda s, j: (s, j)), pl.BlockSpec((3, cw), lambda s, j: (0, j))],
        out_specs=pl.BlockSpec((lp, cw), lambda s, j: (s, j)),
        out_shape=jax.ShapeDtypeStruct((dm.T, dm.D), BF16), compiler_params=_cp(2),
    )(proj_a, conv_w)


def _conv_bwd(proj_a, dy_conv, conv_w, dm):
    lp, cw, rc = dm.LP, dm.CW, _conv_rows(dm)
    nchunk = lp // rc

    def body(p_ref, dy_ref, w_ref, d_ref, gw_ref):
        w0, w1, w2 = w_ref[0:1, :], w_ref[1:2, :], w_ref[2:3, :]

        def ds_of(p4, dy):
            cb, cz = p4[:, :cw], p4[:, 3 * cw:]
            return dy * cb * (cz * _sigmoid(cz))

        def chunk(ci, carry):
            g0, g1, g2 = carry
            r0 = pl.multiple_of(ci * rc, rc)
            blk = p_ref[pl.ds(r0, rc), :].astype(F32)
            dy = dy_ref[pl.ds(r0, rc), :].astype(F32)
            cb, cc, cx, cz = (blk[:, i * cw:(i + 1) * cw] for i in range(4))
            rp = pl.multiple_of(jnp.maximum(r0 - 16, 0), 16)
            rn = pl.multiple_of(jnp.minimum(r0 + rc, lp - 16), 16)
            pv = p_ref[pl.ds(rp, 16), :].astype(F32)[15:16]
            nx = p_ref[pl.ds(rn, 16), :].astype(F32)[0:1]
            dpv = dy_ref[pl.ds(rp, 16), :].astype(F32)[15:16]
            dnx = dy_ref[pl.ds(rn, 16), :].astype(F32)[0:1]
            has_prev, has_next = ci > 0, ci < nchunk - 1
            m = cc * cx
            m_prev, m_next = _shifted(m, jnp.where(has_prev, pv[:, cw:2 * cw] * pv[:, 2 * cw:3 * cw], 0.0),
                                      jnp.where(has_next, nx[:, cw:2 * cw] * nx[:, 2 * cw:3 * cw], 0.0), rc)
            s = w0 * m_prev + w1 * m + w2 * m_next
            sg = _sigmoid(cz)
            silu = cz * sg
            ds = dy * cb * silu
            ds_prev, ds_next = _shifted(ds, jnp.where(has_prev, ds_of(pv, dpv), 0.0),
                                        jnp.where(has_next, ds_of(nx, dnx), 0.0), rc)
            dm_ = w0 * ds_next + w1 * ds + w2 * ds_prev
            d_ref[pl.ds(r0, rc), 0:cw] = (dy * s * silu).astype(BF16)
            d_ref[pl.ds(r0, rc), cw:2 * cw] = (dm_ * cx).astype(BF16)
            d_ref[pl.ds(r0, rc), 2 * cw:3 * cw] = (dm_ * cc).astype(BF16)
            d_ref[pl.ds(r0, rc), 3 * cw:4 * cw] = (dy * cb * s * (sg * (1.0 + cz * (1.0 - sg)))).astype(BF16)
            return (g0 + jnp.sum(ds * m_prev, axis=0, keepdims=True), g1 + jnp.sum(ds * m, axis=0, keepdims=True),
                    g2 + jnp.sum(ds * m_next, axis=0, keepdims=True))

        z = jnp.zeros((1, cw), F32)
        g0, g1, g2 = lax.fori_loop(0, nchunk, chunk, (z, z, z))

        @pl.when(pl.program_id(1) == 0)
        def _():
            gw_ref[...] = jnp.zeros_like(gw_ref)

        gw_ref[0:1, :] += g0
        gw_ref[1:2, :] += g1
        gw_ref[2:3, :] += g2

    return pl.pallas_call(
        body, name="conv_bwd", grid=(dm.NJ, dm.Bl),
        in_specs=[pl.BlockSpec((lp, 4 * cw), lambda j, s: (s, j)), pl.BlockSpec((lp, cw), lambda j, s: (s, j)),
                  pl.BlockSpec((3, cw), lambda j, s: (0, j))],
        out_specs=[pl.BlockSpec((lp, 4 * cw), lambda j, s: (s, j)), pl.BlockSpec((8, cw), lambda j, s: (0, j))],
        out_shape=[jax.ShapeDtypeStruct((dm.T, 4 * dm.D), BF16), jax.ShapeDtypeStruct((8, dm.D), F32)],
        compiler_params=_cp(2),
    )(proj_a, dy_conv, conv_w)


def _interleave(gens):
    results = [None] * len(gens)
    live = list(range(len(gens)))
    while live:
        for idx in list(live):
            try:
                next(gens[idx])
            except StopIteration as done:
                results[idx] = done.value
                live.remove(idx)
    return results


def _group_chunks(dm):
    n = dm.NC - dm.C0
    return 3 if n % 3 == 0 else 1


def _group_masks(rows):
    ii = lax.broadcasted_iota(jnp.int32, (rows, rows), 0)
    jj = lax.broadcasted_iota(jnp.int32, (rows, rows), 1)
    same = jnp.right_shift(ii, CHUNK_SHIFT) == jnp.right_shift(jj, CHUNK_SHIFT)
    low, up = same & (jj <= ii), same & (jj >= ii)
    return low, same & (jj > ii), low.astype(BF16), up.astype(BF16)


def _first_row(chunk):
    return chunk * CHUNK if isinstance(chunk, int) else pl.multiple_of(chunk * CHUNK, CHUNK)


def _chunk_totals(b, fwd):
    hk = b.shape[1]
    rows = [b[c * CHUNK + CHUNK - 1:(c + 1) * CHUNK] if fwd else b[c * CHUNK:c * CHUNK + 1]
            for c in range(b.shape[0] // CHUNK)]
    return jnp.concatenate([jnp.broadcast_to(r, (CHUNK, hk)) for r in rows], axis=0)


def _log_gate(lr_rows, w_ref, b_ref, first_group, hk):
    z = _dot(lr_rows, w_ref[...]) + b_ref[...]
    e = jnp.exp(-jnp.abs(z))
    g = (jnp.minimum(z, 0.0) - jnp.log(1.0 + e)) * (1.0 / GATE_NORMALIZER)
    dg_dz = jnp.where(z >= 0.0, e, 1.0) / (1.0 + e) * (1.0 / GATE_NORMALIZER)
    row = lax.broadcasted_iota(jnp.int32, (lr_rows.shape[0], hk), 0)
    pad = first_group & (row < PAD_ROWS)
    return jnp.where(pad, 0.0, g), jnp.where(pad, 0.0, dg_dz)


def _gla_fwd(proj_b, lr, wg_f, bg_f, wg_b, bg_b, gla_g, dm):
    lp, hk, hv, nc, c0, hw = dm.LP, dm.HK, dm.HV, dm.NC, dm.C0, dm.HW
    scale = hk ** -0.5
    gc = _group_chunks(dm)
    gr, ng = gc * CHUNK, (nc - c0) // gc

    def body(p_ref, lr_ref, wf_ref, bf_ref, wb_ref, bb_ref, gg_ref, o_ref, y_ref, st_ref, b_out, gs_out, oacc_f, oacc_b):
        low_incl, up_strict, ones_low, ones_up = _group_masks(gr)
        if c0 > 0:
            zr = c0 * CHUNK
            o_ref[0:zr, :] = jnp.zeros((zr, hv), BF16)
            y_ref[0:zr, :] = jnp.zeros((zr, hv), BF16)
            b_out[:, 0:zr, :] = jnp.zeros((2, zr, hk), F32)
            gs_out[:, 0:zr, :] = jnp.zeros((2, zr, hk), F32)
            st_ref[0, 0, :, 0:c0] = jnp.zeros((2, c0, hv, hk), BF16)

        def decay(gi, fwd):
            w_ref, b_ref = (wf_ref, bf_ref) if fwd else (wb_ref, bb_ref)
            r0 = _first_row(c0 + gi * gc)
            yield
            g, dg_dz = _log_gate(lr_ref[pl.ds(r0, gr), :], w_ref, b_ref, gi == 0, hk)
            gs_out[0 if fwd else 1, pl.ds(r0, gr), :] = dg_dz
            yield
            b = _dot_exact01(ones_low if fwd else ones_up, g)
            b_out[0 if fwd else 1, pl.ds(r0, gr), :] = b
            return b

        def group(gi, st, b, fwd):
            oacc = oacc_f if fwd else oacc_b
            r0 = pl.multiple_of((c0 + gi * gc) * CHUNK, CHUNK)
            blk = p_ref[pl.ds(r0, gr), :]
            q = blk[:, :hk].astype(F32) * scale
            k = blk[:, hk:2 * hk].astype(F32)
            v = blk[:, 2 * hk:2 * hk + hv]
            btot = _chunk_totals(b, fwd)
            qi = (q * jnp.exp(b)).astype(BF16)
            ki = (k * jnp.exp(-b)).astype(BF16)
            kd = (k * jnp.exp(btot - b)).astype(BF16)
            dec = jnp.exp(btot)
            a = _dot_nt(qi, ki)
            yield
            o = _dot(jnp.where(low_incl if fwd else up_strict, a, 0.0).astype(BF16), v)
            chunk_rows = [slice(c * CHUNK, (c + 1) * CHUNK) for c in range(gc)]
            kv = [_dot_tn(v[rows], kd[rows]) for rows in chunk_rows]
            for c in (range(gc) if fwd else reversed(range(gc))):
                yield
                rows = chunk_rows[c]
                st_b = st.astype(BF16)
                st_ref[0, 0, 0 if fwd else 1, c0 + gi * gc + c] = st_b
                oacc[pl.ds(r0 + c * CHUNK, CHUNK), :] = o[rows] + _dot_nt(qi[rows], st_b)
                st = st * dec[c * CHUNK:c * CHUNK + 1] + kv[c]
            return st

        def step(i, carry):
            st_f, st_b, b_f, b_b = carry
            gf, gb = i, ng - 1 - i
            return tuple(_interleave([group(gf, st_f, b_f, True), group(gb, st_b, b_b, False),
                                      decay(jnp.minimum(gf + 1, ng - 1), True), decay(jnp.maximum(gb - 1, 0), False)]))

        zero = jnp.zeros((hv, hk), F32)
        lax.fori_loop(0, ng, step, (zero, zero, *_interleave([decay(0, True), decay(ng - 1, False)])))

        def finish(i, carry):
            r0 = pl.multiple_of((c0 + i * gc) * CHUNK, CHUNK)
            o = oacc_f[pl.ds(r0, gr), :] + oacc_b[pl.ds(r0, gr), :]
            r = p_ref[pl.ds(r0, gr), 2 * hk + hv:].astype(F32)
            on = o * lax.rsqrt(jnp.mean(o * o, axis=-1, keepdims=True) + EPS) * gg_ref[...]
            o_ref[pl.ds(r0, gr), :] = o.astype(BF16)
            y_ref[pl.ds(r0, gr), :] = (on * r * _sigmoid(r)).astype(BF16)
            return carry

        lax.fori_loop(0, ng, finish, 0)

    head = lambda s, h: (s, h)
    wspec = pl.BlockSpec((LR_LANES, hk), lambda s, h: (0, h))
    bspec = pl.BlockSpec((1, hk), lambda s, h: (0, h))
    return pl.pallas_call(
        body, name="gla_fwd", grid=(dm.Bl, HEADS),
        in_specs=[pl.BlockSpec((lp, hw), head), pl.BlockSpec((lp, LR_LANES), lambda s, h: (s, 0)),
                  wspec, bspec, wspec, bspec, pl.BlockSpec((1, hv), lambda s, h: (0, 0))],
        out_specs=[pl.BlockSpec((lp, hv), head), pl.BlockSpec((lp, hv), head),
                   pl.BlockSpec((1, 1, 2, nc, hv, hk), lambda s, h: (s, h, 0, 0, 0, 0)),
                   pl.BlockSpec((2, lp, hk), lambda s, h: (0, s, h)), pl.BlockSpec((2, lp, hk), lambda s, h: (0, s, h))],
        out_shape=[jax.ShapeDtypeStruct((dm.T, dm.DV), BF16), jax.ShapeDtypeStruct((dm.T, dm.DV), BF16),
                   jax.ShapeDtypeStruct((dm.Bl, HEADS, 2, nc, hv, hk), BF16),
                   jax.ShapeDtypeStruct((2, dm.T, dm.DK), F32), jax.ShapeDtypeStruct((2, dm.T, dm.DK), F32)],
        scratch_shapes=[pltpu.VMEM((lp, hv), F32), pltpu.VMEM((lp, hv), F32)],
        compiler_params=_cp(2),
    )(proj_b, lr, wg_f, bg_f, wg_b, bg_b, gla_g)


def _gla_bwd(proj_b, lr, o_all, dy_gla, states, decays, gate_slopes, wg_f, wg_b, gla_g, dm):
    lp, hk, hv, nc, c0, hw = dm.LP, dm.HK, dm.HV, dm.NC, dm.C0, dm.HW
    scale = hk ** -0.5
    gc = _group_chunks(dm)
    gr, ng = gc * CHUNK, (nc - c0) // gc

    def body(p_ref, lr_ref, o_ref, dy_ref, st_ref, b_ref, gs_ref, wf_ref, wb_ref, gg_ref,
             d_ref, dlr_ref, gwf_ref, gbf_ref, gwb_ref, gbb_ref, ggg_ref, do_s, dq_s, dk_s, dv_s, dlr_s):
        low_incl, up_strict, ones_low, ones_up = _group_masks(gr)
        h = pl.program_id(1)

        @pl.when(h == 0)
        def _():
            dlr_ref[...] = jnp.zeros_like(dlr_ref)

        if c0 > 0:
            zr = c0 * CHUNK
            d_ref[0:zr, :] = jnp.zeros((zr, hw), BF16)
        for acc in (dq_s, dk_s, dv_s, dlr_s):
            acc[...] = jnp.zeros_like(acc)

        def norm_bwd(i, ggg):
            r0 = pl.multiple_of((c0 + i * gc) * CHUNK, CHUNK)
            o = o_ref[pl.ds(r0, gr), :].astype(F32)
            dy = dy_ref[pl.ds(r0, gr), :].astype(F32)
            r = p_ref[pl.ds(r0, gr), 2 * hk + hv:].astype(F32)
            rstd = lax.rsqrt(jnp.mean(o * o, axis=-1, keepdims=True) + EPS)
            ohat = o * rstd
            sg = _sigmoid(r)
            d_on = dy * (r * sg)
            d_ref[pl.ds(r0, gr), 2 * hk + hv:] = (dy * ohat * gg_ref[...] * (sg * (1.0 + r * (1.0 - sg)))).astype(BF16)
            d_oh = d_on * gg_ref[...]
            do_s[pl.ds(r0, gr), :] = (rstd * (d_oh - ohat * jnp.mean(d_oh * ohat, axis=-1, keepdims=True))).astype(BF16)
            return ggg + jnp.sum(d_on * ohat, axis=0, keepdims=True)

        ggg = lax.fori_loop(0, ng, norm_bwd, jnp.zeros((1, hv), F32))

        @pl.when((pl.program_id(0) == 0) & (h == 0))
        def _():
            ggg_ref[...] = jnp.zeros_like(ggg_ref)

        ggg_ref[0:1, :] += ggg

        def load(gi):
            r0 = pl.multiple_of((c0 + gi * gc) * CHUNK, CHUNK)
            blk = p_ref[pl.ds(r0, gr), :]
            return r0, blk[:, :hk].astype(F32) * scale, blk[:, hk:2 * hk].astype(F32), blk[:, 2 * hk:2 * hk + hv]

        zero = jnp.zeros((hv, hk), F32)

        def grad(gi, carry, fwd):
            dst, gw, gb = carry
            w_ref, way = (wf_ref, 0) if fwd else (wb_ref, 1)
            mask = low_incl if fwd else up_strict
            r0, q, k, v = load(gi)
            b = b_ref[way, pl.ds(r0, gr), :]
            btot = _chunk_totals(b, fwd)
            eb, enb, edb, dec = jnp.exp(b), jnp.exp(-b), jnp.exp(btot - b), jnp.exp(btot)
            qi_f, ki_f, kd_f = q * eb, k * enb, k * edb
            qi, ki, kd = qi_f.astype(BF16), ki_f.astype(BF16), kd_f.astype(BF16)
            do = do_s[pl.ds(r0, gr), :]
            a = _dot_nt(qi, ki)
            da = _dot_nt(do, v)
            yield
            a = jnp.where(mask, a, 0.0).astype(BF16)
            da = jnp.where(mask, da, 0.0).astype(BF16)
            dv = _dot_tn(a, do)
            dqi = _dot(da, ki)
            dki = _dot_tn(da, qi)
            dv_c, dqi_c, dkd_c, extra_c = [None] * gc, [None] * gc, [None] * gc, [None] * gc
            chunk_rows = [slice(c * CHUNK, (c + 1) * CHUNK) for c in range(gc)]
            qdo = [_dot_tn(do[rows], qi[rows]) for rows in chunk_rows]
            for c in (reversed(range(gc)) if fwd else range(gc)):
                yield
                rows = chunk_rows[c]
                st = st_ref[0, 0, way, c0 + gi * gc + c]
                dsn_b = dst.astype(BF16)
                dec_c = dec[c * CHUNK:c * CHUNK + 1]
                dv_c[c] = dv[rows] + _dot_nt(kd[rows], dsn_b)
                dqi_c[c] = dqi[rows] + _dot(do[rows], st)
                dkd_c[c] = _dot(v[rows], dsn_b)
                ddec = jnp.sum(st.astype(F32) * dst, axis=0, keepdims=True)
                extra = jnp.sum(dkd_c[c] * kd_f[rows], axis=0, keepdims=True) + ddec * dec_c
                extra_c[c] = jnp.broadcast_to(extra, (CHUNK, hk))
                dst = dst * dec_c + qdo[c]
            yield
            dv, dqi = jnp.concatenate(dv_c, axis=0), jnp.concatenate(dqi_c, axis=0)
            dkd, extra = jnp.concatenate(dkd_c, axis=0), jnp.concatenate(extra_c, axis=0)
            dq_s[pl.ds(r0, gr), :] += dqi * eb * scale
            dk_s[pl.ds(r0, gr), :] += dki * enb + dkd * edb
            dv_s[pl.ds(r0, gr), :] += dv
            db = dqi * qi_f - dki * ki_f - dkd * kd_f
            dg = _dot_exact01(ones_up if fwd else ones_low, db) + extra
            yield
            dz = dg * gs_ref[way, pl.ds(r0, gr), :]
            dz_b = dz.astype(BF16)
            dlr_s[pl.ds(r0, gr), :] += _dot_nt(dz_b, w_ref[...])
            return dst, gw + _dot_tn(lr_ref[pl.ds(r0, gr), :], dz_b), gb + jnp.sum(dz, axis=0, keepdims=True)

        def grad_step(i, carry):
            return tuple(_interleave([grad(ng - 1 - i, carry[0], True), grad(i, carry[1], False)]))

        init = (zero, jnp.zeros((LR_LANES, hk), F32), jnp.zeros((1, hk), F32))
        (_, gw_f, gb_f), (_, gw_b, gb_b) = lax.fori_loop(0, ng, grad_step, (init, init))
        for gw_ref, gb_ref, gw, gb in ((gwf_ref, gbf_ref, gw_f, gb_f), (gwb_ref, gbb_ref, gw_b, gb_b)):
            gw_ref[0] = gw
            gb_ref[0] = jnp.zeros((8, hk), F32)
            gb_ref[0, 0:1, :] = gb

        def combine(i, carry):
            r0 = pl.multiple_of((c0 + i * gc) * CHUNK, CHUNK)
            d_ref[pl.ds(r0, gr), 0:hk] = dq_s[pl.ds(r0, gr), :].astype(BF16)
            d_ref[pl.ds(r0, gr), hk:2 * hk] = dk_s[pl.ds(r0, gr), :].astype(BF16)
            d_ref[pl.ds(r0, gr), 2 * hk:2 * hk + hv] = dv_s[pl.ds(r0, gr), :].astype(BF16)
            dlr_ref[pl.ds(r0, gr), :] += dlr_s[pl.ds(r0, gr), :]
            return carry

        lax.fori_loop(0, ng, combine, 0)

    head = lambda s, h: (s, h)
    wspec = pl.BlockSpec((LR_LANES, hk), lambda s, h: (0, h))
    gwspec = pl.BlockSpec((1, LR_LANES, hk), lambda s, h: (s, 0, h))
    gbspec = pl.BlockSpec((1, 8, hk), lambda s, h: (s, 0, h))
    gw_shape = jax.ShapeDtypeStruct((dm.Bl, LR_LANES, dm.DK), F32)
    gb_shape = jax.ShapeDtypeStruct((dm.Bl, 8, dm.DK), F32)
    both = pl.BlockSpec((2, lp, hk), lambda s, h: (0, s, h))
    return pl.pallas_call(
        body, name="gla_bwd", grid=(dm.Bl, HEADS),
        in_specs=[pl.BlockSpec((lp, hw), head), pl.BlockSpec((lp, LR_LANES), lambda s, h: (s, 0)),
                  pl.BlockSpec((lp, hv), head), pl.BlockSpec((lp, hv), head),
                  pl.BlockSpec((1, 1, 2, nc, hv, hk), lambda s, h: (s, h, 0, 0, 0, 0)), both, both,
                  wspec, wspec, pl.BlockSpec((1, hv), lambda s, h: (0, 0))],
        out_specs=[pl.BlockSpec((lp, hw), head), pl.BlockSpec((lp, LR_LANES), lambda s, h: (s, 0)),
                   gwspec, gbspec, gwspec, gbspec, pl.BlockSpec((8, hv), lambda s, h: (0, 0))],
        out_shape=[jax.ShapeDtypeStruct((dm.T, HEADS * hw), BF16), jax.ShapeDtypeStruct((dm.T, LR_LANES), F32),
                   gw_shape, gb_shape, gw_shape, gb_shape, jax.ShapeDtypeStruct((8, hv), F32)],
        scratch_shapes=[pltpu.VMEM((lp, hv), BF16), pltpu.VMEM((lp, hk), F32), pltpu.VMEM((lp, hk), F32),
                        pltpu.VMEM((lp, hv), F32), pltpu.VMEM((lp, LR_LANES), F32)],
        compiler_params=_cp(2),
    )(proj_b, lr, o_all, dy_gla, states, decays, gate_slopes, wg_f, wg_b, gla_g)


def _stream_tiles(n_tiles, loads, stores, compute):
    for cp in loads(0, 0):
        cp.start()

    def step(t, carry):
        slot = t % 2

        @pl.when(t + 1 < n_tiles)
        def _():
            for cp in loads(t + 1, 1 - slot):
                cp.start()

        for cp in loads(t, slot):
            cp.wait()

        @pl.when(t >= 2)
        def _():
            for cp in stores(t - 2, slot):
                cp.wait()

        compute(t, slot)
        for cp in stores(t, slot):
            cp.start()
        return carry

    lax.fori_loop(0, n_tiles, step, 0)
    for t in range(max(n_tiles - 2, 0), n_tiles):
        for cp in stores(t, t % 2):
            cp.wait()


def _token_tiles(dm, target_rows=512):
    rows = _pick(dm.S, target_rows, 16)
    per_seq = dm.S // rows
    return rows, dm.Bl * per_seq, lambda t: pl.multiple_of((t // per_seq) * dm.LP + dm.TM + (t % per_seq) * rows, 16)


def _head(y_conv, y_gla, proj_c, w_oc, w_og, w_out, x, target, g_post, dm):
    d, tm = dm.D, dm.TM
    rows, n_tiles, first_row = _token_tiles(dm, 256)
    n_out = 8

    def body(*refs):
        yc_hbm, yg_hbm, c_hbm, woc_ref, wog_ref, wo_ref, x_hbm, t_hbm, g_ref = refs[:9]
        outs, st_ref = refs[9:9 + n_out], refs[9 + n_out]
        ycbuf, ygbuf, cbuf, xbuf, tbuf = refs[10 + n_out:15 + n_out]
        obufs = refs[15 + n_out:15 + 2 * n_out]
        zbuf, zbuf2, sem_in, sem_out, sem_zero = refs[15 + 2 * n_out:]

        def loads(t, slot):
            padded = [(yc_hbm, ycbuf), (yg_hbm, ygbuf), (c_hbm, cbuf)]
            own = [(x_hbm, xbuf), (t_hbm, tbuf)]
            return ([pltpu.make_async_copy(h.at[pl.ds(first_row(t), rows), :], b.at[slot], sem_in.at[i, slot])
                     for i, (h, b) in enumerate(padded)] +
                    [pltpu.make_async_copy(h.at[pl.ds(t * rows, rows), :], b.at[slot], sem_in.at[3 + i, slot])
                     for i, (h, b) in enumerate(own)])

        def stores(t, slot):
            return [pltpu.make_async_copy(b.at[slot], h.at[pl.ds(first_row(t), rows), :], sem_out.at[i, slot])
                    for i, (h, b) in enumerate(zip(outs, obufs))]

        def compute(t, slot):
            mg_o, do_o, dy_o, dpc_o, dpg_o, dc_o, dyc_o, dyg_o = obufs
            pc = _dot(ycbuf[slot], woc_ref[...])
            pg = _dot(ygbuf[slot], wog_ref[...])
            sa = _sigmoid(cbuf[slot, :, :d].astype(F32))
            sb = _sigmoid(cbuf[slot, :, d:].astype(F32))
            merged = (sa * pc + sb * pg).astype(BF16)
            mg_o[slot] = merged
            out = _dot(merged, wo_ref[...])
            rstd = lax.rsqrt(jnp.mean(out * out, axis=-1, keepdims=True) + EPS)
            ohat = out * rstd
            err = xbuf[slot] + ohat * g_ref[...] - tbuf[slot]
            dy = err * (1.0 / d)
            d_oh = dy * g_ref[...]
            d_out = (rstd * (d_oh - ohat * jnp.mean(d_oh * ohat, axis=-1, keepdims=True))).astype(BF16)
            do_o[slot] = d_out
            dy_o[slot] = dy.astype(BF16)
            st_ref[0:1, :] += jnp.sum(dy * ohat, axis=0, keepdims=True)
            st_ref[1:2, :] += jnp.sum(err * err, axis=0, keepdims=True)
            dmg = _dot_nt(d_out, wo_ref[...])
            dpc = (dmg * sa).astype(BF16)
            dpg = (dmg * sb).astype(BF16)
            dpc_o[slot] = dpc
            dpg_o[slot] = dpg
            dc_o[slot, :, :d] = (dmg * pc * sa * (1.0 - sa)).astype(BF16)
            dc_o[slot, :, d:] = (dmg * pg * sb * (1.0 - sb)).astype(BF16)
            dyc_o[slot] = _dot_nt(dpc, woc_ref[...]).astype(BF16)
            dyg_o[slot] = _dot_nt(dpg, wog_ref[...]).astype(BF16)

        st_ref[...] = jnp.zeros_like(st_ref)
        zbuf[...] = jnp.zeros_like(zbuf)
        zbuf2[...] = jnp.zeros_like(zbuf2)
        zeros = [pltpu.make_async_copy(zbuf2 if out.shape[1] == 2 * d else zbuf, out.at[pl.ds(b * dm.LP, tm), :], sem_zero.at[i, b])
                 for i, out in enumerate(outs) for b in range(dm.Bl)]
        for cp in zeros:
            cp.start()
        _stream_tiles(n_tiles, loads, stores, compute)
        for cp in zeros:
            cp.wait()

    any_spec, vmem = pl.BlockSpec(memory_space=pl.ANY), pl.BlockSpec(memory_space=pltpu.VMEM)
    widths = [d, d, d, d, d, 2 * d, d, d]
    tile = lambda w, dt: pltpu.VMEM((2, rows, w), dt)
    return pl.pallas_call(
        body, name="head", in_specs=[any_spec] * 3 + [vmem] * 3 + [any_spec] * 2 + [vmem],
        out_specs=[any_spec] * n_out + [vmem],
        out_shape=[jax.ShapeDtypeStruct((dm.T, w), BF16) for w in widths] + [jax.ShapeDtypeStruct((8, d), F32)],
        scratch_shapes=[tile(d, BF16), tile(d, BF16), tile(2 * d, BF16), tile(d, F32), tile(d, F32)]
        + [tile(w, BF16) for w in widths]
        + [pltpu.VMEM((tm, d), BF16), pltpu.VMEM((tm, 2 * d), BF16), pltpu.SemaphoreType.DMA((5, 2)),
           pltpu.SemaphoreType.DMA((n_out, 2)), pltpu.SemaphoreType.DMA((n_out, dm.Bl))],
        compiler_params=pltpu.CompilerParams(vmem_limit_bytes=VMEM_LIMIT_BYTES),
    )(y_conv, y_gla, proj_c, w_oc, w_og, w_out, x.reshape(dm.Bl * dm.S, d), target.reshape(dm.Bl * dm.S, d), g_post)


def _grad_h(d_parts, gathered, dy, x, metapad, g_pre, dm):
    d, tm = dm.D, dm.TM
    rows, n_tiles, first_row = _token_tiles(dm, 256)
    widths = [a.shape[1] for a in d_parts]
    np_ = len(d_parts)

    def body(*refs):
        d_hbm, g_hbm, dy_hbm, x_hbm, mp_ref, g_ref = refs[:np_], refs[np_], refs[np_ + 1], refs[np_ + 2], refs[np_ + 3], refs[np_ + 4]
        gx_hbm, dmeta_ref, gg_ref = refs[np_ + 5:np_ + 8]
        parts, edges, sems = refs[np_ + 8:np_ + 12], refs[np_ + 12], refs[np_ + 13]
        dbufs = refs[np_ + 14:2 * np_ + 14]
        dybuf, xbuf, gbuf = refs[2 * np_ + 14:2 * np_ + 17]
        mbufs = refs[2 * np_ + 17:3 * np_ + 17]
        sem_in, sem_out, sem_meta = refs[3 * np_ + 17:]

        def grad_u(tiles):
            du = _dot(tiles[0].astype(BF16), parts[0][...])
            for a, w in zip(tiles[1:], parts[1:]):
                du = du + _dot(a.astype(BF16), w[...])
            return du

        def norm_bwd(h, du, dy):
            rstd = lax.rsqrt(jnp.mean(h * h, axis=-1, keepdims=True) + EPS)
            hhat = h * rstd
            dug = du * g_ref[...]
            gg_ref[0:1, :] += jnp.sum(du * hhat, axis=0, keepdims=True)
            return dy + rstd * (dug - hhat * jnp.mean(dug * hhat, axis=-1, keepdims=True))

        def loads(t, slot):
            padded = list(zip(d_hbm, dbufs)) + [(dy_hbm, dybuf)]
            return ([pltpu.make_async_copy(h.at[pl.ds(first_row(t), rows), :], b.at[slot], sem_in.at[i, slot])
                     for i, (h, b) in enumerate(padded)] +
                    [pltpu.make_async_copy(x_hbm.at[pl.ds(t * rows, rows), :], xbuf.at[slot], sem_in.at[np_ + 1, slot])])

        def stores(t, slot):
            return [pltpu.make_async_copy(gbuf.at[slot], gx_hbm.at[pl.ds(t * rows, rows), :], sem_out.at[slot])]

        def compute(t, slot):
            gbuf[slot] = norm_bwd(xbuf[slot], grad_u([b[slot] for b in dbufs]), dybuf[slot].astype(F32))

        gg_ref[...] = jnp.zeros_like(gg_ref)
        meta = [pltpu.make_async_copy(h.at[pl.ds(b * dm.LP, tm), :], buf.at[b], sem_meta.at[i, b])
                for i, (h, buf) in enumerate(zip(d_hbm, mbufs)) for b in range(dm.Bl)]
        for cp in meta:
            cp.start()
        _load_packed(g_hbm, parts, edges, sems, dm)
        _stream_tiles(n_tiles, loads, stores, compute)
        for cp in meta:
            cp.wait()
        for b in range(dm.Bl):
            dmeta_ref[b] = norm_bwd(mp_ref[...], grad_u([buf[b] for buf in mbufs]), 0.0)

    any_spec, vmem = pl.BlockSpec(memory_space=pl.ANY), pl.BlockSpec(memory_space=pltpu.VMEM)
    grad_x, d_meta, gg = pl.pallas_call(
        body, name="grad_h", in_specs=[any_spec] * (np_ + 3) + [vmem, vmem], out_specs=[any_spec, vmem, vmem],
        out_shape=[jax.ShapeDtypeStruct((dm.Bl * dm.S, d), F32), jax.ShapeDtypeStruct((dm.Bl, tm, d), F32),
                   jax.ShapeDtypeStruct((8, d), F32)],
        scratch_shapes=_packed_scratch(dm)
        + [pltpu.VMEM((2, rows, w), a.dtype) for w, a in zip(widths, d_parts)]
        + [pltpu.VMEM((2, rows, d), BF16), pltpu.VMEM((2, rows, d), F32), pltpu.VMEM((2, rows, d), F32)]
        + [pltpu.VMEM((dm.Bl, tm, w), a.dtype) for w, a in zip(widths, d_parts)]
        + [pltpu.SemaphoreType.DMA((np_ + 2, 2)), pltpu.SemaphoreType.DMA((2,)), pltpu.SemaphoreType.DMA((np_, dm.Bl))],
        compiler_params=pltpu.CompilerParams(vmem_limit_bytes=VMEM_LIMIT_BYTES),
    )(*d_parts, gathered, dy, x.reshape(dm.Bl * dm.S, d), metapad, g_pre)
    return grad_x.reshape(dm.Bl, dm.S, d), d_meta, gg


def _adamw(partials, w, m, v, name, by_columns=False):
    r, c = w.shape
    n_parts = partials.shape[0]
    tr, tc = (r, _pick(c, 128, 128)) if by_columns else (_pick(r, 256, 16), c)

    def body(p_ref, w_ref, m_ref, v_ref, g_ref, d_ref, nm_ref, nv_ref):
        g = p_ref[0].astype(F32)
        for j in range(1, n_parts):
            g = g + p_ref[j].astype(F32)
        g_ref[...] = g
        d_ref[...], nm_ref[...], nv_ref[...] = _adam_step(g, w_ref[...], m_ref[...], v_ref[...])

    at = (lambda i: (0, i)) if by_columns else (lambda i: (i, 0))
    tile = pl.BlockSpec((tr, tc), at)
    out = jax.ShapeDtypeStruct((r, c), F32)
    return pl.pallas_call(
        body, name=name, grid=(c // tc if by_columns else r // tr,),
        in_specs=[pl.BlockSpec((n_parts, tr, tc), lambda i: (0,) + at(i)), tile, tile, tile],
        out_specs=[tile, tile, tile, tile], out_shape=[out, out, out, out], compiler_params=_cp(1),
    )(partials, w, m, v)


def _adam_step(g, w, m, v):
    m2 = ADAM_B1 * m + (1.0 - ADAM_B1) * g
    v2 = ADAM_B2 * v + (1.0 - ADAM_B2) * (g * g)
    m_hat = m2 / (1.0 - ADAM_B1 ** ADAM_STEP)
    v_hat = v2 / (1.0 - ADAM_B2 ** ADAM_STEP)
    return -ADAM_LR * (m_hat / (jnp.sqrt(v_hat) + ADAM_EPS) + ADAM_WD * w), m2, v2


def _adamw_small(items, name):
    n = len(items)

    def body(*refs):
        ins, outs = refs[:4 * n], refs[4 * n:]
        for i in range(n):
            p_ref, w_ref, m_ref, v_ref = ins[4 * i:4 * i + 4]
            g = p_ref[0]
            for j in range(1, p_ref.shape[0]):
                g = g + p_ref[j]
            delta, m2, v2 = _adam_step(g, w_ref[...], m_ref[...], v_ref[...])
            for o_ref, val in zip(outs[4 * i:4 * i + 4], (g, delta, m2, v2)):
                o_ref[...] = val

    vmem = pl.BlockSpec(memory_space=pltpu.VMEM)
    res = pl.pallas_call(
        body, name=name, in_specs=[vmem] * (4 * n), out_specs=[vmem] * (4 * n),
        out_shape=[jax.ShapeDtypeStruct(w.shape, F32) for _, w, _, _ in items for _ in range(4)],
    )(*[a for item in items for a in item])
    return [res[4 * i:4 * i + 4] for i in range(n)]


def _unpack_rows(a, b, c, lr, dm):
    d, hk, hv, cw, nj, hw = dm.D, dm.HK, dm.HV, dm.CW, dm.NJ, dm.HW
    conv = a.reshape(nj, 4, cw, d).transpose(1, 0, 2, 3).reshape(4 * d, d)
    heads = b.reshape(HEADS, hw, d)
    q = heads[:, :hk].reshape(HEADS * hk, d)
    k = heads[:, hk:2 * hk].reshape(HEADS * hk, d)
    v = heads[:, 2 * hk:2 * hk + hv].reshape(HEADS * hv, d)
    r = heads[:, 2 * hk + hv:].reshape(HEADS * hv, d)
    return jnp.concatenate([conv, q, k, v, r, lr[:2 * RANK], c], axis=0)


def _column_shards(g, shard_shape):
    r, c = g.shape
    return g.reshape(r, N_DEV, c // N_DEV).transpose(1, 0, 2).reshape((N_DEV,) + tuple(shard_shape))


def _join_column_shards(parts):
    r, c = parts.shape[-2:]
    return parts.reshape(N_DEV, r, c).transpose(1, 0, 2).reshape(r, N_DEV * c)


def _local_step(x, target, meta, g_pre, u, wt_shards, conv_w, wg_f, bg_f, wg_b, bg_b, gla_g, out_weights, g_post,
                on_matrix_grads=None):
    bl, s, d = x.shape
    dm = _Dims(bl, s, d)
    metapad = jnp.concatenate([jnp.zeros((dm.TM - N_META, d), F32), meta], axis=0)
    wgp_f = jnp.pad(wg_f, ((0, LR_LANES - RANK), (0, 0))).astype(BF16)
    wgp_b = jnp.pad(wg_b, ((RANK, LR_LANES - 2 * RANK), (0, 0))).astype(BF16)

    u = _prenorm_meta(u, metapad, g_pre, dm)
    proj_a, proj_b, proj_c, lr = _inproj(u, wt_shards, dm)
    y_conv = _conv_fwd(proj_a, conv_w, dm)
    o_all, y_gla, states, decays, gate_slopes = _gla_fwd(proj_b, lr, wgp_f, bg_f, wgp_b, bg_b, gla_g, dm)
    w_oc, w_og, w_out = out_weights(y_conv) if callable(out_weights) else out_weights
    merged, d_out, dy, d_pc, d_pg, d_c, dy_conv, dy_gla, stats = _head(y_conv, y_gla, proj_c, w_oc, w_og, w_out, x, target,
                                                                        g_post, dm)

    g_out = _matmul_tn(merged, d_out, BF16, "grad_w_out")
    g_oc = _matmul_tn(y_conv, d_pc, BF16, "grad_w_out_conv")
    g_og = _matmul_tn(y_gla, d_pg, BF16, "grad_w_out_gla")
    if on_matrix_grads is not None:
        conv_w = conv_w + on_matrix_grads(dict(w_out_conv=g_oc, w_out_gla=g_og, w_merge_out=g_out))
    d_a, g_conv = _conv_bwd(proj_a, dy_conv, conv_w, dm)
    d_b, d_lr, gwp_f, gbp_f, gwp_b, gbp_b, g_gla = _gla_bwd(proj_b, lr, o_all, dy_gla, states, decays, gate_slopes, wgp_f, wgp_b, gla_g, dm)
    g_abc = _matmul_tn_group([d_a, d_b, d_c], u, "grad_w_in", tile=d).reshape(9 * d, d)
    g_in = _unpack_rows(g_abc[:4 * d], g_abc[4 * d:7 * d], g_abc[7 * d:], _matmul_tn(d_lr, u, BF16, "grad_w_in_gate"), dm)
    if on_matrix_grads is not None:
        d_lr = d_lr + on_matrix_grads(dict(w_in=g_in))
    grad_x, d_meta, g_pre_rows = _grad_h([d_a, d_b, d_c, d_lr], wt_shards, dy, x, metapad, g_pre, dm)

    grads = dict(
        meta_tokens=jnp.sum(d_meta[:, dm.TM - N_META:, :], axis=0), norm_pre=g_pre_rows[0:1], w_in=g_in,
        conv_w=g_conv[0:3], w_gate_fwd=jnp.sum(gwp_f, axis=0)[:RANK], b_gate_fwd=jnp.sum(gbp_f, axis=0)[0:1],
        w_gate_bwd=jnp.sum(gwp_b, axis=0)[RANK:2 * RANK], b_gate_bwd=jnp.sum(gbp_b, axis=0)[0:1],
        gla_norm=g_gla[0:1], w_out_conv=g_oc, w_out_gla=g_og, w_merge_out=g_out, norm_post=stats[0:1])
    return stats[1:2], grad_x, grads


MATRICES = ("w_out_conv", "w_out_gla", "w_merge_out")
SMALL_SHARDED = ("meta_tokens", "conv_w", "w_gate_fwd", "w_gate_bwd")
REPLICATED = ("norm_pre", "b_gate_fwd", "b_gate_bwd", "gla_norm", "norm_post")
NAMES = ("meta_tokens", "norm_pre", "w_in", "conv_w", "w_gate_fwd", "b_gate_fwd", "w_gate_bwd", "b_gate_bwd", "gla_norm",
         "w_out_conv", "w_out_gla", "w_merge_out", "norm_post")


def kernel(x, meta_tokens, norm_pre, w_in, conv_w, w_gate_fwd, b_gate_fwd, w_gate_bwd, b_gate_bwd, gla_norm, w_out_conv, w_out_gla, w_merge_out, norm_post, loss_target, m_meta_tokens, m_norm_pre, m_w_in, m_conv_w, m_w_gate_fwd, m_b_gate_fwd, m_w_gate_bwd, m_b_gate_bwd, m_gla_norm, m_w_out_conv, m_w_out_gla, m_w_merge_out, m_norm_post, v_meta_tokens, v_norm_pre, v_w_in, v_conv_w, v_w_gate_fwd, v_b_gate_fwd, v_w_gate_bwd, v_b_gate_bwd, v_gla_norm, v_w_out_conv, v_w_out_gla, v_w_merge_out, v_norm_post):
    w = dict(meta_tokens=meta_tokens, norm_pre=norm_pre, w_in=w_in[0], conv_w=conv_w, w_gate_fwd=w_gate_fwd,
             b_gate_fwd=b_gate_fwd, w_gate_bwd=w_gate_bwd, b_gate_bwd=b_gate_bwd, gla_norm=gla_norm,
             w_out_conv=w_out_conv[0], w_out_gla=w_out_gla[0], w_merge_out=w_merge_out[0], norm_post=norm_post)
    m = dict(meta_tokens=m_meta_tokens, norm_pre=m_norm_pre, w_in=m_w_in[0], conv_w=m_conv_w, w_gate_fwd=m_w_gate_fwd,
             b_gate_fwd=m_b_gate_fwd, w_gate_bwd=m_w_gate_bwd, b_gate_bwd=m_b_gate_bwd, gla_norm=m_gla_norm,
             w_out_conv=m_w_out_conv[0], w_out_gla=m_w_out_gla[0], w_merge_out=m_w_merge_out[0], norm_post=m_norm_post)
    v = dict(meta_tokens=v_meta_tokens, norm_pre=v_norm_pre, w_in=v_w_in[0], conv_w=v_conv_w, w_gate_fwd=v_w_gate_fwd,
             b_gate_fwd=v_b_gate_fwd, w_gate_bwd=v_w_gate_bwd, b_gate_bwd=v_b_gate_bwd, gla_norm=v_gla_norm,
             w_out_conv=v_w_out_conv[0], w_out_gla=v_w_out_gla[0], w_merge_out=v_w_merge_out[0], norm_post=v_norm_post)
    d = x.shape[-1]

    dm = _Dims(*x.shape)
    me = 4 * lax.axis_index("x") + 2 * lax.axis_index("y") + lax.axis_index("c")
    wt_shards, *small_all, u = _gather_two_level(
        [_pad_shard(w["w_in"].T.astype(BF16), me)] + [w[n] for n in SMALL_SHARDED], "gather_weights",
        _prenorm_tokens_side(x, norm_pre, dm))
    _, late_weights = _exchange_start([w[n].astype(BF16) for n in MATRICES], [], small_all[0], "gather_out_weights_start")
    small = {n: _join_column_shards(p) for n, p in zip(SMALL_SHARDED, small_all)}

    def out_weights(after):
        return tuple(a.reshape(-1, d) for a in _exchange_wait(late_weights, after, "gather_out_weights_wait"))

    pending = []

    def on_matrix_grads(g):
        token, state = _exchange_start([], [t.reshape(N_DEV, -1, d) for t in g.values()], None,
                                       "exchange_grads_start_" + "_".join(g))
        pending.append((tuple(g), state))
        return token

    sq_err_cols, grad_x, grads = _local_step(
        x, loss_target, small["meta_tokens"], norm_pre, u, wt_shards, small["conv_w"], small["w_gate_fwd"], b_gate_fwd,
        small["w_gate_bwd"], b_gate_bwd, gla_norm, out_weights, norm_post, on_matrix_grads)
    received = {}
    for names, state in pending:
        received.update(zip(names, _exchange_wait(state, grad_x, "exchange_grads_wait_" + "_".join(names))))

    exchanged = _exchange([grads[n] for n in REPLICATED] + [sq_err_cols],
                          [_column_shards(grads[n], w[n].shape) for n in SMALL_SHARDED], "exchange_small_grads")
    small_recv = exchanged[:len(REPLICATED)] + exchanged[len(REPLICATED) + 1:]
    loss = 0.5 / d * jnp.sum(exchanged[len(REPLICATED)])

    results = {"w_in": [r.T[None] for r in _adamw(received["w_in"], w["w_in"].T, m["w_in"].T, v["w_in"].T, "adamw_w_in", by_columns=True)]}
    for n in MATRICES:
        results[n] = [r[None] for r in _adamw(received[n], w[n], m[n], v[n], "adamw_" + n)]
    small_names = REPLICATED + SMALL_SHARDED
    results.update(zip(small_names, _adamw_small([(p, w[n], m[n], v[n]) for n, p in zip(small_names, small_recv)], "adamw_small")))
    return (loss, grad_x, *[results[n][i] for i in range(4) for n in NAMES])
```

```python
import jax
import jax.numpy as jnp
from jax import lax
from jax.experimental import pallas as pl
from jax.experimental.pallas import tpu as pltpu

F32 = jnp.float32
BF16 = jnp.bfloat16
MESH = pl.DeviceIdType.MESH

N_META = 16
CHUNK = 64
CHUNK_SHIFT = 6
HEADS = 4
RANK = 16
LR_LANES = 128
PAD_ROWS = CHUNK - N_META
EPS = 1e-6
GATE_NORMALIZER = 16.0
N_DEV = 8
ADAM_LR, ADAM_B1, ADAM_B2, ADAM_EPS, ADAM_WD, ADAM_STEP = 0.001, 0.9, 0.999, 1e-08, 0.01, 10
VMEM_LIMIT_BYTES = 56 * 1024 * 1024


class _Dims:
    def __init__(self, bl, s, d):
        self.Bl, self.S, self.D = bl, s, d
        self.TM = CHUNK
        self.LP = self.TM + s
        self.T = bl * self.LP
        self.TPS = self.LP // self.TM
        self.NC = self.LP // CHUNK
        self.C0 = (self.TM - CHUNK) // CHUNK
        self.DK, self.DV = d // 2, d
        self.HK, self.HV = self.DK // HEADS, self.DV // HEADS
        self.HW = 2 * self.HK + 2 * self.HV
        self.CW = 256 if d % 256 == 0 and d > 256 else d // 4
        self.NJ = d // self.CW


def _pick(n, target, mult):
    t = min(n, target)
    while t >= mult:
        if n % t == 0 and t % mult == 0:
            return t
        t -= mult
    return n


def _cp(n_axes):
    return pltpu.CompilerParams(dimension_semantics=("arbitrary",) * n_axes, vmem_limit_bytes=VMEM_LIMIT_BYTES)


def _sigmoid(x):
    return 1.0 / (1.0 + jnp.exp(-x))


def _dot(a, b):
    return jnp.dot(a, b, preferred_element_type=F32)


def _dot_nt(a, b):
    return lax.dot_general(a, b, (((1,), (1,)), ((), ())), preferred_element_type=F32)


def _dot_tn(a, b):
    return lax.dot_general(a, b, (((0,), (0,)), ((), ())), preferred_element_type=F32)


def _chunk_cumsum(x, reverse):
    rows = x.shape[0]
    r = lax.broadcasted_iota(jnp.int32, x.shape, 0) & (CHUNK - 1)
    step = 1
    while step < CHUNK:
        if reverse:
            x = x + jnp.where(r < CHUNK - step, pltpu.roll(x, rows - step, 0), 0.0)
        else:
            x = x + jnp.where(r >= step, pltpu.roll(x, step, 0), 0.0)
        step *= 2
    return x


def _exchange(gathers, scatters, name):
    arrays = list(gathers) + list(scatters)
    n, ng = len(arrays), len(gathers)

    def body(*refs):
        ins, outs = refs[:n], refs[n:2 * n]
        send_sems, recv_sems, local_sems = refs[2 * n:]
        x, y, c = lax.axis_index("x"), lax.axis_index("y"), lax.axis_index("c")
        me = 4 * x + 2 * y + c
        started = []
        for t in range(n):
            src, dst = ins[t], outs[t]
            own = pltpu.make_async_copy(src if t < ng else src.at[me], dst.at[me], local_sems.at[t])
            own.start()
            started.append(own)
            for k, pos, peer in _peers(x, y, c):
                cp = pltpu.make_async_remote_copy(
                    src_ref=src if t < ng else src.at[peer], dst_ref=dst.at[me],
                    send_sem=send_sems.at[t * (N_DEV - 1) + k - 1], recv_sem=recv_sems.at[t * (N_DEV - 1) + k - 1],
                    device_id=pos, device_id_type=MESH)
                cp.start()
                started.append(cp)
        for cp in started:
            cp.wait()

    out_shape = [jax.ShapeDtypeStruct((N_DEV,) + a.shape if t < ng else a.shape, a.dtype) for t, a in enumerate(arrays)]
    any_spec = pl.BlockSpec(memory_space=pl.ANY)
    return pl.pallas_call(
        body, name=name, out_shape=out_shape, in_specs=[any_spec] * n, out_specs=[any_spec] * n,
        scratch_shapes=[pltpu.SemaphoreType.DMA((n * (N_DEV - 1),)), pltpu.SemaphoreType.DMA((n * (N_DEV - 1),)),
                        pltpu.SemaphoreType.DMA((n,))],
        compiler_params=pltpu.CompilerParams(has_side_effects=True),
    )(*arrays)


def _gather_two_level(arrays, name, side=None):
    n = len(arrays)
    per = N_DEV - 1
    work, side_in, side_in_specs, side_out, side_out_specs, side_scratch = side or (None, [], [], [], [], [])
    n_in, n_out = len(side_in), len(side_out)

    def body(*refs):
        ins, outs = refs[:n], refs[n + n_in:2 * n + n_in]
        send_sems, recv_sems, local_sems = refs[2 * n + n_in + n_out:2 * n + n_in + n_out + 3]
        x, y, c = lax.axis_index("x"), lax.axis_index("y"), lax.axis_index("c")
        sibling = (x, y, 1 - c)
        chips = [(1 - x, y), (x, 1 - y), (1 - x, 1 - y)]
        index = lambda px, py, pc: 4 * px + 2 * py + pc

        def copy(t, k, block, to, from_input=False):
            slab = outs[t].at[index(*block)]
            return pltpu.make_async_remote_copy(
                src_ref=ins[t] if from_input else slab, dst_ref=slab, send_sem=send_sems.at[t * per + k],
                recv_sem=recv_sems.at[t * per + k], device_id=to, device_id_type=MESH)

        own, sent = [], []
        for t in range(n):
            own.append(pltpu.make_async_copy(ins[t], outs[t].at[index(x, y, c)], local_sems.at[t]))
            own[-1].start()
            first = [copy(t, 0, (x, y, c), sibling, True)]
            first += [copy(t, 1 + j, (x, y, c), (*chip, c), True) for j, chip in enumerate(chips)]
            for cp in first:
                cp.start()
            sent += first
        if work is not None:
            work(refs[n:n + n_in], refs[2 * n + n_in:2 * n + n_in + n_out], refs[2 * n + n_in + n_out + 3:])
        for t in range(n):
            for j, chip in enumerate(chips):
                copy(t, 1 + j, (*chip, c), (x, y, c)).wait_recv()
                sent.append(copy(t, 4 + j, (*chip, c), sibling))
                sent[-1].start()
        for t in range(n):
            copy(t, 0, sibling, (x, y, c)).wait_recv()
            for j, chip in enumerate(chips):
                copy(t, 4 + j, (*chip, 1 - c), (x, y, c)).wait_recv()
        for cp in sent:
            cp.wait_send()
        for cp in own:
            cp.wait()

    out_shape = [jax.ShapeDtypeStruct((N_DEV,) + a.shape, a.dtype) for a in arrays]
    any_spec = pl.BlockSpec(memory_space=pl.ANY)
    return pl.pallas_call(
        body, name=name, out_shape=out_shape + list(side_out), in_specs=[any_spec] * n + list(side_in_specs),
        out_specs=[any_spec] * n + list(side_out_specs),
        scratch_shapes=[pltpu.SemaphoreType.DMA((n * per,)), pltpu.SemaphoreType.DMA((n * per,)),
                        pltpu.SemaphoreType.DMA((n,))] + list(side_scratch),
        compiler_params=pltpu.CompilerParams(has_side_effects=True, vmem_limit_bytes=VMEM_LIMIT_BYTES),
    )(*arrays, *side_in)


def _peers(x, y, c):
    out = []
    for k in range(1, N_DEV):
        px = 1 - x if (k >> 2) & 1 else x
        py = 1 - y if (k >> 1) & 1 else y
        pc = 1 - c if k & 1 else c
        out.append((k, (px, py, pc), 4 * px + 2 * py + pc))
    return out


def _exchange_start(gathers, scatters, after, name):
    arrays = list(gathers) + list(scatters)
    n, ng = len(arrays), len(gathers)
    hbm = pl.BlockSpec(memory_space=pltpu.HBM)
    sem = pl.BlockSpec(memory_space=pltpu.SEMAPHORE)

    extra = [] if after is None else [after]
    ne = len(extra)

    def body(*refs):
        ins, lands = refs[:n], refs[n:2 * n]
        send_sems, recv_sems = refs[2 * n + ne], refs[2 * n + ne + 1]
        token = refs[4 * n + ne + 2]
        x, y, c = lax.axis_index("x"), lax.axis_index("y"), lax.axis_index("c")
        me = 4 * x + 2 * y + c
        for t in range(n):
            for k, pos, peer in _peers(x, y, c):
                pltpu.make_async_remote_copy(
                    src_ref=ins[t] if t < ng else ins[t].at[peer], dst_ref=lands[t].at[me],
                    send_sem=send_sems.at[t * (N_DEV - 1) + k - 1], recv_sem=recv_sems.at[t * (N_DEV - 1) + k - 1],
                    device_id=pos, device_id_type=MESH).start()
        token[...] = jnp.zeros_like(token)

    me = 4 * lax.axis_index("x") + 2 * lax.axis_index("y") + lax.axis_index("c")
    lands = [lax.dynamic_update_index_in_dim(lax.empty((N_DEV,) + a.shape if t < ng else a.shape, a.dtype),
                                             a if t < ng else lax.dynamic_index_in_dim(a, me, 0, keepdims=False), me, 0)
             for t, a in enumerate(arrays)]
    operands = [pltpu.with_memory_space_constraint(a, pltpu.HBM) for a in arrays + lands]
    sems = pltpu.SemaphoreType.DMA((n * (N_DEV - 1),))
    res = pl.pallas_call(
        body, name=name,
        out_shape=(sems, sems, *[pltpu.HBM(a.shape, a.dtype) for a in arrays + lands], jax.ShapeDtypeStruct((8, 128), F32)),
        in_specs=[hbm] * (2 * n) + [pl.BlockSpec(memory_space=pl.ANY)] * ne,
        out_specs=(sem, sem, *[hbm] * (2 * n), pl.BlockSpec(memory_space=pltpu.VMEM)),
        input_output_aliases={i: 2 + i for i in range(2 * n)},
        compiler_params=pltpu.CompilerParams(has_side_effects=pltpu.SideEffectType.DATAFLOW_SIDE_EFFECTING),
    )(*operands, *extra)
    return res[-1][0, 0], (ng, res[0], res[1], list(res[2:2 + n]), list(res[2 + n:2 + 2 * n]))


def _exchange_wait(state, after, name):
    ng, send_sems, recv_sems, sent, lands = state
    n = len(sent)
    hbm = pl.BlockSpec(memory_space=pltpu.HBM)
    sem = pl.BlockSpec(memory_space=pltpu.SEMAPHORE)

    def body(*refs):
        ins, land_refs = refs[:n], refs[n:2 * n]
        send_ref, recv_ref = refs[2 * n], refs[2 * n + 1]
        x, y, c = lax.axis_index("x"), lax.axis_index("y"), lax.axis_index("c")
        me = 4 * x + 2 * y + c
        for t in range(n):
            for k, pos, peer in _peers(x, y, c):
                cp = pltpu.make_async_remote_copy(
                    src_ref=ins[t] if t < ng else ins[t].at[peer], dst_ref=land_refs[t].at[me],
                    send_sem=send_ref.at[t * (N_DEV - 1) + k - 1], recv_sem=recv_ref.at[t * (N_DEV - 1) + k - 1],
                    device_id=pos, device_id_type=MESH)
                cp.wait_send()
                cp.wait_recv()

    res = pl.pallas_call(
        body, name=name, out_shape=tuple(pltpu.HBM(a.shape, a.dtype) for a in sent + lands),
        in_specs=[hbm] * (2 * n) + [sem, sem, pl.BlockSpec(memory_space=pl.ANY)], out_specs=tuple([hbm] * (2 * n)),
        input_output_aliases={i: i for i in range(2 * n)},
        compiler_params=pltpu.CompilerParams(has_side_effects=pltpu.SideEffectType.DATAFLOW_SIDE_EFFECTING),
    )(*sent, *lands, send_sems, recv_sems, after)
    return list(res[n:])


def _rms_scaled(h, g):
    return (h * lax.rsqrt(jnp.mean(h * h, axis=-1, keepdims=True) + EPS) * g).astype(BF16)


def _prenorm_tokens_side(x, g_pre, dm):
    bl, s, d = x.shape
    rows = _pick(s, 512, 16)
    tiles = [(b, j) for b in range(bl) for j in range(s // rows)]

    def work(ins, outs, scratch):
        (x_ref, g_ref), (u_ref,), (xbuf, ubuf, sem_in, sem_out) = ins, outs, scratch

        def load(t, slot):
            b, j = tiles[t]
            return pltpu.make_async_copy(x_ref.at[b, pl.ds(j * rows, rows), :], xbuf.at[slot], sem_in.at[slot])

        def store(t, slot):
            b, j = tiles[t]
            return pltpu.make_async_copy(ubuf.at[slot], u_ref.at[pl.ds(b * dm.LP + dm.TM + j * rows, rows), :], sem_out.at[slot])

        load(0, 0).start()
        for t in range(len(tiles)):
            slot = t % 2
            if t + 1 < len(tiles):
                load(t + 1, 1 - slot).start()
            load(t, slot).wait()
            if t >= 2:
                store(t - 2, slot).wait()
            ubuf[slot] = _rms_scaled(xbuf[slot], g_ref[...])
            store(t, slot).start()
        for t in range(max(len(tiles) - 2, 0), len(tiles)):
            store(t, t % 2).wait()

    any_spec = pl.BlockSpec(memory_space=pl.ANY)
    return (work, [x, g_pre], [any_spec, pl.BlockSpec(memory_space=pltpu.VMEM)],
            [jax.ShapeDtypeStruct((dm.T, d), BF16)], [any_spec],
            [pltpu.VMEM((2, rows, d), F32), pltpu.VMEM((2, rows, d), BF16), pltpu.SemaphoreType.DMA((2,)),
             pltpu.SemaphoreType.DMA((2,))])


def _prenorm_meta(u, metapad, g_pre, dm):
    tm, tps, d = dm.TM, dm.TPS, dm.D

    def body(u_in, mp_ref, g_ref, u_ref):
        u_ref[...] = _rms_scaled(mp_ref[...], g_ref[...])

    return pl.pallas_call(
        body, name="prenorm_meta", grid=(dm.Bl,),
        in_specs=[pl.BlockSpec(memory_space=pl.ANY), pl.BlockSpec((tm, d), lambda i: (0, 0)),
                  pl.BlockSpec((1, d), lambda i: (0, 0))],
        out_specs=pl.BlockSpec((tm, d), lambda i: (i * tps, 0)),
        out_shape=jax.ShapeDtypeStruct((dm.T, d), BF16), input_output_aliases={0: 0}, compiler_params=_cp(1),
    )(u, metapad, g_pre)


def _matmul_tn(a, b, out_dtype, name, tt=2304, tn=1024, tk=1024):
    t, k = a.shape
    n = b.shape[1]
    tt, tn, tk = _pick(t, tt, 16), _pick(n, tn, 128), _pick(k, tk, 128)
    nt = t // tt

    def body(a_ref, b_ref, o_ref, acc):
        p = _dot_tn(a_ref[...].astype(BF16), b_ref[...].astype(BF16))
        i = pl.program_id(2)

        @pl.when(i == 0)
        def _():
            acc[...] = p

        @pl.when(i > 0)
        def _():
            acc[...] += p

        @pl.when(i == nt - 1)
        def _():
            o_ref[...] = acc[...].astype(out_dtype)

    return pl.pallas_call(
        body, name=name, grid=(k // tk, n // tn, nt),
        in_specs=[pl.BlockSpec((tt, tk), lambda kk, j, i: (i, kk)), pl.BlockSpec((tt, tn), lambda kk, j, i: (i, j))],
        out_specs=pl.BlockSpec((tk, tn), lambda kk, j, i: (kk, j)),
        out_shape=jax.ShapeDtypeStruct((k, n), out_dtype), scratch_shapes=[pltpu.VMEM((tk, tn), F32)],
        compiler_params=_cp(3),
    )(a, b)


def _matmul_tn_group(a_list, b, name, tt=2304, tile=1024):
    t, n = b.shape
    tt = _pick(t, tt, 16)
    nt = t // tt
    counts = [a.shape[1] // tile for a in a_list]
    starts = [sum(counts[:m]) for m in range(len(a_list))]
    items = sum(counts)

    def active(p, m):
        return (p >= starts[m]) & (p < starts[m] + counts[m])

    def body(*refs):
        a_refs, b_ref = refs[:len(a_list)], refs[len(a_list)]
        o_ref, acc = refs[-2], refs[-1]
        p, i = pl.program_id(0), pl.program_id(1)
        for m, a_ref in enumerate(a_refs):
            @pl.when(active(p, m))
            def _(a_ref=a_ref):
                prod = _dot_tn(a_ref[...].astype(BF16), b_ref[...].astype(BF16))

                @pl.when(i == 0)
                def _():
                    acc[...] = prod

                @pl.when(i > 0)
                def _():
                    acc[...] += prod

        @pl.when(i == nt - 1)
        def _():
            o_ref[0] = acc[...].astype(BF16)

    a_specs = [pl.BlockSpec((tt, tile), lambda p, i, m=m: (jnp.where(active(p, m), i, 0), jnp.where(active(p, m), p - starts[m], 0)))
               for m in range(len(a_list))]
    return pl.pallas_call(
        body, name=name, grid=(items, nt), in_specs=a_specs + [pl.BlockSpec((tt, n), lambda p, i: (i, 0))],
        out_specs=pl.BlockSpec((1, tile, n), lambda p, i: (p, 0, 0)),
        out_shape=jax.ShapeDtypeStruct((items, tile, n), BF16), scratch_shapes=[pltpu.VMEM((tile, n), F32)],
        compiler_params=_cp(2),
    )(*a_list, b)


BF16_TILE_ROWS = 16


def _shard_offset(index, shard_rows):
    return (index * shard_rows) % BF16_TILE_ROWS


def _pad_shard(wt_shard, index):
    rows, d = wt_shard.shape
    padded = -(-(rows + max(_shard_offset(j, rows) for j in range(N_DEV))) // BF16_TILE_ROWS) * BF16_TILE_ROWS
    return lax.dynamic_update_slice(jnp.zeros((padded, d), wt_shard.dtype), wt_shard, (_shard_offset(index, rows), 0))


def _packed_parts(dm):
    d, dk, hk, hv, cw, nj, hw = dm.D, dm.DK, dm.HK, dm.HV, dm.CW, dm.NJ, dm.HW
    blocks = [(0, (j * 4 + p) * cw, p * d + j * cw, cw) for j in range(nj) for p in range(4)]
    for h in range(HEADS):
        blocks += [(1, h * hw, 4 * d + h * hk, hk), (1, h * hw + hk, 4 * d + dk + h * hk, hk),
                   (1, h * hw + 2 * hk, 5 * d + h * hv, hv), (1, h * hw + 2 * hk + hv, 6 * d + h * hv, hv)]
    blocks += [(2, 0, 7 * d + 2 * RANK, 2 * d), (3, 0, 7 * d, 2 * RANK)]
    return [4 * d, 3 * d, 2 * d, LR_LANES], blocks


def _pack_plan(dm):
    sh = (9 * dm.D + 2 * RANK) // N_DEV
    tile = BF16_TILE_ROWS
    copies, straddles = [], []
    for part, dst, r0, n in _packed_parts(dm)[1]:
        for j in range(N_DEV):
            a, b = max(r0, sh * j), min(r0 + n, sh * (j + 1))
            if a >= b:
                continue
            a_up, b_down = -(-a // tile) * tile, b // tile * tile
            if b_down > a_up:
                copies.append((j, a_up - sh * j + _shard_offset(j, sh), b_down - a_up, part, dst + a_up - r0))
            if a % tile:
                lo = a // tile * tile
                straddles.append((j, lo - sh * (j - 1) + _shard_offset(j - 1, sh), part, dst + lo - r0, a - lo))
    return copies, straddles


def _packed_scratch(dm):
    copies, straddles = _pack_plan(dm)
    return ([pltpu.VMEM((rows, dm.D), BF16) for rows in _packed_parts(dm)[0]]
            + [pltpu.VMEM((2 * max(len(straddles), 1), BF16_TILE_ROWS, dm.D), BF16),
               pltpu.SemaphoreType.DMA((len(copies) + 2 * len(straddles),))])


def _load_packed(g_ref, parts, edges, sems, dm):
    copies, straddles = _pack_plan(dm)
    tile = BF16_TILE_ROWS
    parts[3][2 * RANK:, :] = jnp.zeros((LR_LANES - 2 * RANK, dm.D), BF16)
    dmas = [pltpu.make_async_copy(g_ref.at[j, pl.ds(src, n), :], parts[p].at[pl.ds(dst, n), :], sems.at[i])
            for i, (j, src, n, p, dst) in enumerate(copies)]
    for i, (j, src, p, dst, split) in enumerate(straddles):
        k = len(copies) + 2 * i
        dmas.append(pltpu.make_async_copy(g_ref.at[j - 1, pl.ds(src, tile), :], edges.at[2 * i], sems.at[k]))
        dmas.append(pltpu.make_async_copy(g_ref.at[j, pl.ds(0, tile), :], edges.at[2 * i + 1], sems.at[k + 1]))
    for cp in dmas:
        cp.start()
    for cp in dmas:
        cp.wait()
    row = lax.broadcasted_iota(jnp.int32, (tile, dm.D), 0)
    for i, (j, src, p, dst, split) in enumerate(straddles):
        parts[p][dst:dst + tile, :] = jnp.where(row < split, edges[2 * i], edges[2 * i + 1])


def _inproj(u, gathered, dm):
    t, d = u.shape
    tm = _pick(t, 512, 16)
    widths = _packed_parts(dm)[0]
    cn = 1024

    def body(u_ref, g_ref, *rest):
        outs, parts, (edges, sems) = rest[:4], rest[4:8], rest[8:]

        @pl.when(pl.program_id(0) == 0)
        def _():
            _load_packed(g_ref, parts, edges, sems, dm)

        ut = u_ref[...]
        for w, o_ref in zip(parts, outs):
            n = w.shape[0]
            step = cn if n % cn == 0 else n
            for j in range(0, n, step):
                o_ref[:, j:j + step] = _dot_nt(ut, w[j:j + step, :]).astype(BF16)

    return pl.pallas_call(
        body, name="inproj", grid=(t // tm,),
        in_specs=[pl.BlockSpec((tm, d), lambda i: (i, 0)), pl.BlockSpec(memory_space=pl.ANY)],
        out_specs=[pl.BlockSpec((tm, w), lambda i: (i, 0)) for w in widths],
        out_shape=[jax.ShapeDtypeStruct((t, w), BF16) for w in widths],
        scratch_shapes=_packed_scratch(dm), compiler_params=_cp(1),
    )(u, gathered)


def _conv_rows(dm):
    return _pick(dm.LP, 256, 16)


def _shifted(m, prev_row, next_row, rows):
    row = lax.broadcasted_iota(jnp.int32, m.shape, 0)
    m_prev = jnp.where(row == 0, prev_row, pltpu.roll(m, 1, 0))
    m_next = jnp.where(row == rows - 1, next_row, pltpu.roll(m, rows - 1, 0))
    return m_prev, m_next


def _conv_fwd(proj_a, conv_w, dm):
    lp, cw, rc = dm.LP, dm.CW, _conv_rows(dm)
    nchunk = lp // rc

    def body(p_ref, w_ref, y_ref):
        w0, w1, w2 = w_ref[0:1, :], w_ref[1:2, :], w_ref[2:3, :]

        def chunk(ci, carry):
            r0 = pl.multiple_of(ci * rc, rc)
            blk = p_ref[pl.ds(r0, rc), :].astype(F32)
            cb, cc, cx, cz = (blk[:, i * cw:(i + 1) * cw] for i in range(4))
            m = cc * cx
            rp = pl.multiple_of(jnp.maximum(r0 - 16, 0), 16)
            rn = pl.multiple_of(jnp.minimum(r0 + rc, lp - 16), 16)
            pv = p_ref[pl.ds(rp, 16), cw:3 * cw].astype(F32)
            nx = p_ref[pl.ds(rn, 16), cw:3 * cw].astype(F32)
            prev_row = jnp.where(ci > 0, pv[15:16, :cw] * pv[15:16, cw:], 0.0)
            next_row = jnp.where(ci < nchunk - 1, nx[0:1, :cw] * nx[0:1, cw:], 0.0)
            m_prev, m_next = _shifted(m, prev_row, next_row, rc)
            s = w0 * m_prev + w1 * m + w2 * m_next
            y_ref[pl.ds(r0, rc), :] = (cb * s * (cz * _sigmoid(cz))).astype(BF16)
            return carry

        lax.fori_loop(0, nchunk, chunk, 0)

    return pl.pallas_call(
        body, name="conv_fwd", grid=(dm.Bl, dm.NJ),
        in_specs=[pl.BlockSpec((lp, 4 * cw), lambda s, j: (s, j)), pl.BlockSpec((3, cw), lambda s, j: (0, j))],
        out_specs=pl.BlockSpec((lp, cw), lambda s, j: (s, j)),
        out_shape=jax.ShapeDtypeStruct((dm.T, dm.D), BF16), compiler_params=_cp(2),
    )(proj_a, conv_w)


def _conv_bwd(proj_a, dy_conv, conv_w, dm):
    lp, cw, rc = dm.LP, dm.CW, _conv_rows(dm)
    nchunk = lp // rc

    def body(p_ref, dy_ref, w_ref, d_ref, gw_ref):
        w0, w1, w2 = w_ref[0:1, :], w_ref[1:2, :], w_ref[2:3, :]

        def ds_of(p4, dy):
            cb, cz = p4[:, :cw], p4[:, 3 * cw:]
            return dy * cb * (cz * _sigmoid(cz))

        def chunk(ci, carry):
            g0, g1, g2 = carry
            r0 = pl.multiple_of(ci * rc, rc)
            blk = p_ref[pl.ds(r0, rc), :].astype(F32)
            dy = dy_ref[pl.ds(r0, rc), :].astype(F32)
            cb, cc, cx, cz = (blk[:, i * cw:(i + 1) * cw] for i in range(4))
            rp = pl.multiple_of(jnp.maximum(r0 - 16, 0), 16)
            rn = pl.multiple_of(jnp.minimum(r0 + rc, lp - 16), 16)
            pv = p_ref[pl.ds(rp, 16), :].astype(F32)[15:16]
            nx = p_ref[pl.ds(rn, 16), :].astype(F32)[0:1]
            dpv = dy_ref[pl.ds(rp, 16), :].astype(F32)[15:16]
            dnx = dy_ref[pl.ds(rn, 16), :].astype(F32)[0:1]
            has_prev, has_next = ci > 0, ci < nchunk - 1
            m = cc * cx
            m_prev, m_next = _shifted(m, jnp.where(has_prev, pv[:, cw:2 * cw] * pv[:, 2 * cw:3 * cw], 0.0),
                                      jnp.where(has_next, nx[:, cw:2 * cw] * nx[:, 2 * cw:3 * cw], 0.0), rc)
            s = w0 * m_prev + w1 * m + w2 * m_next
            sg = _sigmoid(cz)
            silu = cz * sg
            ds = dy * cb * silu
            ds_prev, ds_next = _shifted(ds, jnp.where(has_prev, ds_of(pv, dpv), 0.0),
                                        jnp.where(has_next, ds_of(nx, dnx), 0.0), rc)
            dm_ = w0 * ds_next + w1 * ds + w2 * ds_prev
            d_ref[pl.ds(r0, rc), 0:cw] = (dy * s * silu).astype(BF16)
            d_ref[pl.ds(r0, rc), cw:2 * cw] = (dm_ * cx).astype(BF16)
            d_ref[pl.ds(r0, rc), 2 * cw:3 * cw] = (dm_ * cc).astype(BF16)
            d_ref[pl.ds(r0, rc), 3 * cw:4 * cw] = (dy * cb * s * (sg * (1.0 + cz * (1.0 - sg)))).astype(BF16)
            return (g0 + jnp.sum(ds * m_prev, axis=0, keepdims=True), g1 + jnp.sum(ds * m, axis=0, keepdims=True),
                    g2 + jnp.sum(ds * m_next, axis=0, keepdims=True))

        z = jnp.zeros((1, cw), F32)
        g0, g1, g2 = lax.fori_loop(0, nchunk, chunk, (z, z, z))

        @pl.when(pl.program_id(1) == 0)
        def _():
            gw_ref[...] = jnp.zeros_like(gw_ref)

        gw_ref[0:1, :] += g0
        gw_ref[1:2, :] += g1
        gw_ref[2:3, :] += g2

    return pl.pallas_call(
        body, name="conv_bwd", grid=(dm.NJ, dm.Bl),
        in_specs=[pl.BlockSpec((lp, 4 * cw), lambda j, s: (s, j)), pl.BlockSpec((lp, cw), lambda j, s: (s, j)),
                  pl.BlockSpec((3, cw), lambda j, s: (0, j))],
        out_specs=[pl.BlockSpec((lp, 4 * cw), lambda j, s: (s, j)), pl.BlockSpec((8, cw), lambda j, s: (0, j))],
        out_shape=[jax.ShapeDtypeStruct((dm.T, 4 * dm.D), BF16), jax.ShapeDtypeStruct((8, dm.D), F32)],
        compiler_params=_cp(2),
    )(proj_a, dy_conv, conv_w)


def _interleave(gens):
    results = [None] * len(gens)
    live = list(range(len(gens)))
    while live:
        for idx in list(live):
            try:
                next(gens[idx])
            except StopIteration as done:
                results[idx] = done.value
                live.remove(idx)
    return results


def _group_chunks(dm):
    n = dm.NC - dm.C0
    return 3 if n % 3 == 0 else 1


def _group_masks(rows):
    ii = lax.broadcasted_iota(jnp.int32, (rows, rows), 0)
    jj = lax.broadcasted_iota(jnp.int32, (rows, rows), 1)
    same = jnp.right_shift(ii, CHUNK_SHIFT) == jnp.right_shift(jj, CHUNK_SHIFT)
    return same & (jj <= ii), same & (jj > ii)


def _first_row(chunk):
    return chunk * CHUNK if isinstance(chunk, int) else pl.multiple_of(chunk * CHUNK, CHUNK)


def _chunk_totals(b, fwd):
    hk = b.shape[1]
    rows = [b[c * CHUNK + CHUNK - 1:(c + 1) * CHUNK] if fwd else b[c * CHUNK:c * CHUNK + 1]
            for c in range(b.shape[0] // CHUNK)]
    return jnp.concatenate([jnp.broadcast_to(r, (CHUNK, hk)) for r in rows], axis=0)


def _log_gate(lr_rows, w_ref, b_ref, first_group, hk):
    z = _dot(lr_rows, w_ref[...]) + b_ref[...]
    e = jnp.exp(-jnp.abs(z))
    g = (jnp.minimum(z, 0.0) - jnp.log(1.0 + e)) * (1.0 / GATE_NORMALIZER)
    dg_dz = jnp.where(z >= 0.0, e, 1.0) / (1.0 + e) * (1.0 / GATE_NORMALIZER)
    row = lax.broadcasted_iota(jnp.int32, (lr_rows.shape[0], hk), 0)
    pad = first_group & (row < PAD_ROWS)
    return jnp.where(pad, 0.0, g), jnp.where(pad, 0.0, dg_dz)


def _gla_fwd(proj_b, lr, wg_f, bg_f, wg_b, bg_b, gla_g, dm):
    lp, hk, hv, nc, c0, hw = dm.LP, dm.HK, dm.HV, dm.NC, dm.C0, dm.HW
    scale = hk ** -0.5
    gc = _group_chunks(dm)
    gr, ng = gc * CHUNK, (nc - c0) // gc

    def body(p_ref, lr_ref, wf_ref, bf_ref, wb_ref, bb_ref, gg_ref, o_ref, y_ref, st_ref, b_out, gs_out, oacc_f, oacc_b):
        low_incl, up_strict = _group_masks(gr)
        if c0 > 0:
            zr = c0 * CHUNK
            o_ref[0:zr, :] = jnp.zeros((zr, hv), BF16)
            y_ref[0:zr, :] = jnp.zeros((zr, hv), BF16)
            b_out[:, 0:zr, :] = jnp.zeros((2, zr, hk), F32)
            gs_out[:, 0:zr, :] = jnp.zeros((2, zr, hk), F32)
            st_ref[0, 0, :, 0:c0] = jnp.zeros((2, c0, hv, hk), BF16)

        def decay(gi, fwd):
            w_ref, b_ref = (wf_ref, bf_ref) if fwd else (wb_ref, bb_ref)
            r0 = _first_row(c0 + gi * gc)
            yield
            g, dg_dz = _log_gate(lr_ref[pl.ds(r0, gr), :], w_ref, b_ref, gi == 0, hk)
            gs_out[0 if fwd else 1, pl.ds(r0, gr), :] = dg_dz
            yield
            b = _chunk_cumsum(g, not fwd)
            b_out[0 if fwd else 1, pl.ds(r0, gr), :] = b
            return b

        def group(gi, st, b, fwd):
            oacc = oacc_f if fwd else oacc_b
            r0 = pl.multiple_of((c0 + gi * gc) * CHUNK, CHUNK)
            blk = p_ref[pl.ds(r0, gr), :]
            q = blk[:, :hk].astype(F32) * scale
            k = blk[:, hk:2 * hk].astype(F32)
            v = blk[:, 2 * hk:2 * hk + hv]
            btot = _chunk_totals(b, fwd)
            qi = (q * jnp.exp(b)).astype(BF16)
            ki = (k * jnp.exp(-b)).astype(BF16)
            kd = (k * jnp.exp(btot - b)).astype(BF16)
            dec = jnp.exp(btot)
            a = _dot_nt(qi, ki)
            yield
            o = _dot(jnp.where(low_incl if fwd else up_strict, a, 0.0).astype(BF16), v)
            chunk_rows = [slice(c * CHUNK, (c + 1) * CHUNK) for c in range(gc)]
            kv = [_dot_tn(v[rows], kd[rows]) for rows in chunk_rows]
            for c in (range(gc) if fwd else reversed(range(gc))):
                yield
                rows = chunk_rows[c]
                st_b = st.astype(BF16)
                st_ref[0, 0, 0 if fwd else 1, c0 + gi * gc + c] = st_b
                oacc[pl.ds(r0 + c * CHUNK, CHUNK), :] = o[rows] + _dot_nt(qi[rows], st_b)
                st = st * dec[c * CHUNK:c * CHUNK + 1] + kv[c]
            return st

        def step(i, carry):
            st_f, st_b, b_f, b_b = carry
            gf, gb = i, ng - 1 - i
            return tuple(_interleave([group(gf, st_f, b_f, True), group(gb, st_b, b_b, False),
                                      decay(jnp.minimum(gf + 1, ng - 1), True), decay(jnp.maximum(gb - 1, 0), False)]))

        zero = jnp.zeros((hv, hk), F32)
        lax.fori_loop(0, ng, step, (zero, zero, *_interleave([decay(0, True), decay(ng - 1, False)])))

        def finish(i, carry):
            r0 = pl.multiple_of((c0 + i * gc) * CHUNK, CHUNK)
            o = oacc_f[pl.ds(r0, gr), :] + oacc_b[pl.ds(r0, gr), :]
            r = p_ref[pl.ds(r0, gr), 2 * hk + hv:].astype(F32)
            on = o * lax.rsqrt(jnp.mean(o * o, axis=-1, keepdims=True) + EPS) * gg_ref[...]
            o_ref[pl.ds(r0, gr), :] = o.astype(BF16)
            y_ref[pl.ds(r0, gr), :] = (on * r * _sigmoid(r)).astype(BF16)
            return carry

        lax.fori_loop(0, ng, finish, 0)

    head = lambda s, h: (s, h)
    wspec = pl.BlockSpec((LR_LANES, hk), lambda s, h: (0, h))
    bspec = pl.BlockSpec((1, hk), lambda s, h: (0, h))
    return pl.pallas_call(
        body, name="gla_fwd", grid=(dm.Bl, HEADS),
        in_specs=[pl.BlockSpec((lp, hw), head), pl.BlockSpec((lp, LR_LANES), lambda s, h: (s, 0)),
                  wspec, bspec, wspec, bspec, pl.BlockSpec((1, hv), lambda s, h: (0, 0))],
        out_specs=[pl.BlockSpec((lp, hv), head), pl.BlockSpec((lp, hv), head),
                   pl.BlockSpec((1, 1, 2, nc, hv, hk), lambda s, h: (s, h, 0, 0, 0, 0)),
                   pl.BlockSpec((2, lp, hk), lambda s, h: (0, s, h)), pl.BlockSpec((2, lp, hk), lambda s, h: (0, s, h))],
        out_shape=[jax.ShapeDtypeStruct((dm.T, dm.DV), BF16), jax.ShapeDtypeStruct((dm.T, dm.DV), BF16),
                   jax.ShapeDtypeStruct((dm.Bl, HEADS, 2, nc, hv, hk), BF16),
                   jax.ShapeDtypeStruct((2, dm.T, dm.DK), F32), jax.ShapeDtypeStruct((2, dm.T, dm.DK), F32)],
        scratch_shapes=[pltpu.VMEM((lp, hv), F32), pltpu.VMEM((lp, hv), F32)],
        compiler_params=_cp(2),
    )(proj_b, lr, wg_f, bg_f, wg_b, bg_b, gla_g)


def _gla_bwd(proj_b, lr, o_all, dy_gla, states, decays, gate_slopes, wg_f, wg_b, gla_g, dm):
    lp, hk, hv, nc, c0, hw = dm.LP, dm.HK, dm.HV, dm.NC, dm.C0, dm.HW
    scale = hk ** -0.5
    gc = _group_chunks(dm)
    gr, ng = gc * CHUNK, (nc - c0) // gc

    def body(p_ref, lr_ref, o_ref, dy_ref, st_ref, b_ref, gs_ref, wf_ref, wb_ref, gg_ref,
             d_ref, dlr_ref, gwf_ref, gbf_ref, gwb_ref, gbb_ref, ggg_ref, do_s, dq_s, dk_s, dv_s, dlr_s):
        low_incl, up_strict = _group_masks(gr)
        h = pl.program_id(1)

        @pl.when(h == 0)
        def _():
            dlr_ref[...] = jnp.zeros_like(dlr_ref)

        if c0 > 0:
            zr = c0 * CHUNK
            d_ref[0:zr, :] = jnp.zeros((zr, hw), BF16)
        for acc in (dq_s, dk_s, dv_s, dlr_s):
            acc[...] = jnp.zeros_like(acc)

        def norm_bwd(i, ggg):
            r0 = pl.multiple_of((c0 + i * gc) * CHUNK, CHUNK)
            o = o_ref[pl.ds(r0, gr), :].astype(F32)
            dy = dy_ref[pl.ds(r0, gr), :].astype(F32)
            r = p_ref[pl.ds(r0, gr), 2 * hk + hv:].astype(F32)
            rstd = lax.rsqrt(jnp.mean(o * o, axis=-1, keepdims=True) + EPS)
            ohat = o * rstd
            sg = _sigmoid(r)
            d_on = dy * (r * sg)
            d_ref[pl.ds(r0, gr), 2 * hk + hv:] = (dy * ohat * gg_ref[...] * (sg * (1.0 + r * (1.0 - sg)))).astype(BF16)
            d_oh = d_on * gg_ref[...]
            do_s[pl.ds(r0, gr), :] = (rstd * (d_oh - ohat * jnp.mean(d_oh * ohat, axis=-1, keepdims=True))).astype(BF16)
            return ggg + jnp.sum(d_on * ohat, axis=0, keepdims=True)

        ggg = lax.fori_loop(0, ng, norm_bwd, jnp.zeros((1, hv), F32))

        @pl.when((pl.program_id(0) == 0) & (h == 0))
        def _():
            ggg_ref[...] = jnp.zeros_like(ggg_ref)

        ggg_ref[0:1, :] += ggg

        def load(gi):
            r0 = pl.multiple_of((c0 + gi * gc) * CHUNK, CHUNK)
            blk = p_ref[pl.ds(r0, gr), :]
            return r0, blk[:, :hk].astype(F32) * scale, blk[:, hk:2 * hk].astype(F32), blk[:, 2 * hk:2 * hk + hv]

        zero = jnp.zeros((hv, hk), F32)

        def grad(gi, carry, fwd):
            dst, gw, gb = carry
            w_ref, way = (wf_ref, 0) if fwd else (wb_ref, 1)
            mask = low_incl if fwd else up_strict
            r0, q, k, v = load(gi)
            b = b_ref[way, pl.ds(r0, gr), :]
            btot = _chunk_totals(b, fwd)
            eb, enb, edb, dec = jnp.exp(b), jnp.exp(-b), jnp.exp(btot - b), jnp.exp(btot)
            qi_f, ki_f, kd_f = q * eb, k * enb, k * edb
            qi, ki, kd = qi_f.astype(BF16), ki_f.astype(BF16), kd_f.astype(BF16)
            do = do_s[pl.ds(r0, gr), :]
            a = _dot_nt(qi, ki)
            da = _dot_nt(do, v)
            yield
            a = jnp.where(mask, a, 0.0).astype(BF16)
            da = jnp.where(mask, da, 0.0).astype(BF16)
            dv = _dot_tn(a, do)
            dqi = _dot(da, ki)
            dki = _dot_tn(da, qi)
            dv_c, dqi_c, dkd_c, extra_c = [None] * gc, [None] * gc, [None] * gc, [None] * gc
            chunk_rows = [slice(c * CHUNK, (c + 1) * CHUNK) for c in range(gc)]
            qdo = [_dot_tn(do[rows], qi[rows]) for rows in chunk_rows]
            for c in (reversed(range(gc)) if fwd else range(gc)):
                yield
                rows = chunk_rows[c]
                st = st_ref[0, 0, way, c0 + gi * gc + c]
                dsn_b = dst.astype(BF16)
                dec_c = dec[c * CHUNK:c * CHUNK + 1]
                dv_c[c] = dv[rows] + _dot_nt(kd[rows], dsn_b)
                dqi_c[c] = dqi[rows] + _dot(do[rows], st)
                dkd_c[c] = _dot(v[rows], dsn_b)
                ddec = jnp.sum(st.astype(F32) * dst, axis=0, keepdims=True)
                extra = jnp.sum(dkd_c[c] * kd_f[rows], axis=0, keepdims=True) + ddec * dec_c
                extra_c[c] = jnp.broadcast_to(extra, (CHUNK, hk))
                dst = dst * dec_c + qdo[c]
            yield
            dv, dqi = jnp.concatenate(dv_c, axis=0), jnp.concatenate(dqi_c, axis=0)
            dkd, extra = jnp.concatenate(dkd_c, axis=0), jnp.concatenate(extra_c, axis=0)
            dq_s[pl.ds(r0, gr), :] += dqi * eb * scale
            dk_s[pl.ds(r0, gr), :] += dki * enb + dkd * edb
            dv_s[pl.ds(r0, gr), :] += dv
            db = dqi * qi_f - dki * ki_f - dkd * kd_f
            dg = _chunk_cumsum(db, fwd) + extra
            yield
            dz = dg * gs_ref[way, pl.ds(r0, gr), :]
            dz_b = dz.astype(BF16)
            dlr_s[pl.ds(r0, gr), :] += _dot_nt(dz_b, w_ref[...])
            return dst, gw + _dot_tn(lr_ref[pl.ds(r0, gr), :], dz_b), gb + jnp.sum(dz, axis=0, keepdims=True)

        def grad_step(i, carry):
            return tuple(_interleave([grad(ng - 1 - i, carry[0], True), grad(i, carry[1], False)]))

        init = (zero, jnp.zeros((LR_LANES, hk), F32), jnp.zeros((1, hk), F32))
        (_, gw_f, gb_f), (_, gw_b, gb_b) = lax.fori_loop(0, ng, grad_step, (init, init))
        for gw_ref, gb_ref, gw, gb in ((gwf_ref, gbf_ref, gw_f, gb_f), (gwb_ref, gbb_ref, gw_b, gb_b)):
            gw_ref[0] = gw
            gb_ref[0] = jnp.zeros((8, hk), F32)
            gb_ref[0, 0:1, :] = gb

        def combine(i, carry):
            r0 = pl.multiple_of((c0 + i * gc) * CHUNK, CHUNK)
            d_ref[pl.ds(r0, gr), 0:hk] = dq_s[pl.ds(r0, gr), :].astype(BF16)
            d_ref[pl.ds(r0, gr), hk:2 * hk] = dk_s[pl.ds(r0, gr), :].astype(BF16)
            d_ref[pl.ds(r0, gr), 2 * hk:2 * hk + hv] = dv_s[pl.ds(r0, gr), :].astype(BF16)
            dlr_ref[pl.ds(r0, gr), :] += dlr_s[pl.ds(r0, gr), :]
            return carry

        lax.fori_loop(0, ng, combine, 0)

    head = lambda s, h: (s, h)
    wspec = pl.BlockSpec((LR_LANES, hk), lambda s, h: (0, h))
    gwspec = pl.BlockSpec((1, LR_LANES, hk), lambda s, h: (s, 0, h))
    gbspec = pl.BlockSpec((1, 8, hk), lambda s, h: (s, 0, h))
    gw_shape = jax.ShapeDtypeStruct((dm.Bl, LR_LANES, dm.DK), F32)
    gb_shape = jax.ShapeDtypeStruct((dm.Bl, 8, dm.DK), F32)
    both = pl.BlockSpec((2, lp, hk), lambda s, h: (0, s, h))
    return pl.pallas_call(
        body, name="gla_bwd", grid=(dm.Bl, HEADS),
        in_specs=[pl.BlockSpec((lp, hw), head), pl.BlockSpec((lp, LR_LANES), lambda s, h: (s, 0)),
                  pl.BlockSpec((lp, hv), head), pl.BlockSpec((lp, hv), head),
                  pl.BlockSpec((1, 1, 2, nc, hv, hk), lambda s, h: (s, h, 0, 0, 0, 0)), both, both,
                  wspec, wspec, pl.BlockSpec((1, hv), lambda s, h: (0, 0))],
        out_specs=[pl.BlockSpec((lp, hw), head), pl.BlockSpec((lp, LR_LANES), lambda s, h: (s, 0)),
                   gwspec, gbspec, gwspec, gbspec, pl.BlockSpec((8, hv), lambda s, h: (0, 0))],
        out_shape=[jax.ShapeDtypeStruct((dm.T, HEADS * hw), BF16), jax.ShapeDtypeStruct((dm.T, LR_LANES), F32),
                   gw_shape, gb_shape, gw_shape, gb_shape, jax.ShapeDtypeStruct((8, hv), F32)],
        scratch_shapes=[pltpu.VMEM((lp, hv), BF16), pltpu.VMEM((lp, hk), F32), pltpu.VMEM((lp, hk), F32),
                        pltpu.VMEM((lp, hv), F32), pltpu.VMEM((lp, LR_LANES), F32)],
        compiler_params=_cp(2),
    )(proj_b, lr, o_all, dy_gla, states, decays, gate_slopes, wg_f, wg_b, gla_g)


def _stream_tiles(n_tiles, loads, stores, compute):
    for cp in loads(0, 0):
        cp.start()

    def step(t, carry):
        slot = t % 2

        @pl.when(t + 1 < n_tiles)
        def _():
            for cp in loads(t + 1, 1 - slot):
                cp.start()

        for cp in loads(t, slot):
            cp.wait()

        @pl.when(t >= 2)
        def _():
            for cp in stores(t - 2, slot):
                cp.wait()

        compute(t, slot)
        for cp in stores(t, slot):
            cp.start()
        return carry

    lax.fori_loop(0, n_tiles, step, 0)
    for t in range(max(n_tiles - 2, 0), n_tiles):
        for cp in stores(t, t % 2):
            cp.wait()


def _token_tiles(dm, target_rows=512):
    rows = _pick(dm.S, target_rows, 16)
    per_seq = dm.S // rows
    return rows, dm.Bl * per_seq, lambda t: pl.multiple_of((t // per_seq) * dm.LP + dm.TM + (t % per_seq) * rows, 16)


def _head(y_conv, y_gla, proj_c, w_oc, w_og, w_out, x, target, g_post, dm):
    d, tm = dm.D, dm.TM
    rows, n_tiles, first_row = _token_tiles(dm, 256)
    n_out = 8

    def body(*refs):
        yc_hbm, yg_hbm, c_hbm, woc_ref, wog_ref, wo_ref, x_hbm, t_hbm, g_ref = refs[:9]
        outs, st_ref = refs[9:9 + n_out], refs[9 + n_out]
        ycbuf, ygbuf, cbuf, xbuf, tbuf = refs[10 + n_out:15 + n_out]
        obufs = refs[15 + n_out:15 + 2 * n_out]
        zbuf, zbuf2, sem_in, sem_out, sem_zero = refs[15 + 2 * n_out:]

        def loads(t, slot):
            padded = [(yc_hbm, ycbuf), (yg_hbm, ygbuf), (c_hbm, cbuf)]
            own = [(x_hbm, xbuf), (t_hbm, tbuf)]
            return ([pltpu.make_async_copy(h.at[pl.ds(first_row(t), rows), :], b.at[slot], sem_in.at[i, slot])
                     for i, (h, b) in enumerate(padded)] +
                    [pltpu.make_async_copy(h.at[pl.ds(t * rows, rows), :], b.at[slot], sem_in.at[3 + i, slot])
                     for i, (h, b) in enumerate(own)])

        def stores(t, slot):
            return [pltpu.make_async_copy(b.at[slot], h.at[pl.ds(first_row(t), rows), :], sem_out.at[i, slot])
                    for i, (h, b) in enumerate(zip(outs, obufs))]

        def compute(t, slot):
            mg_o, do_o, dy_o, dpc_o, dpg_o, dc_o, dyc_o, dyg_o = obufs
            pc = _dot(ycbuf[slot], woc_ref[...])
            pg = _dot(ygbuf[slot], wog_ref[...])
            sa = _sigmoid(cbuf[slot, :, :d].astype(F32))
            sb = _sigmoid(cbuf[slot, :, d:].astype(F32))
            merged = (sa * pc + sb * pg).astype(BF16)
            mg_o[slot] = merged
            out = _dot(merged, wo_ref[...])
            rstd = lax.rsqrt(jnp.mean(out * out, axis=-1, keepdims=True) + EPS)
            ohat = out * rstd
            err = xbuf[slot] + ohat * g_ref[...] - tbuf[slot]
            dy = err * (1.0 / d)
            d_oh = dy * g_ref[...]
            d_out = (rstd * (d_oh - ohat * jnp.mean(d_oh * ohat, axis=-1, keepdims=True))).astype(BF16)
            do_o[slot] = d_out
            dy_o[slot] = dy.astype(BF16)
            st_ref[0:1, :] += jnp.sum(dy * ohat, axis=0, keepdims=True)
            st_ref[1:2, :] += jnp.sum(err * err, axis=0, keepdims=True)
            dmg = _dot_nt(d_out, wo_ref[...])
            dpc = (dmg * sa).astype(BF16)
            dpg = (dmg * sb).astype(BF16)
            dpc_o[slot] = dpc
            dpg_o[slot] = dpg
            dc_o[slot, :, :d] = (dmg * pc * sa * (1.0 - sa)).astype(BF16)
            dc_o[slot, :, d:] = (dmg * pg * sb * (1.0 - sb)).astype(BF16)
            dyc_o[slot] = _dot_nt(dpc, woc_ref[...]).astype(BF16)
            dyg_o[slot] = _dot_nt(dpg, wog_ref[...]).astype(BF16)

        st_ref[...] = jnp.zeros_like(st_ref)
        zbuf[...] = jnp.zeros_like(zbuf)
        zbuf2[...] = jnp.zeros_like(zbuf2)
        zeros = [pltpu.make_async_copy(zbuf2 if out.shape[1] == 2 * d else zbuf, out.at[pl.ds(b * dm.LP, tm), :], sem_zero.at[i, b])
                 for i, out in enumerate(outs) for b in range(dm.Bl)]
        for cp in zeros:
            cp.start()
        _stream_tiles(n_tiles, loads, stores, compute)
        for cp in zeros:
            cp.wait()

    any_spec, vmem = pl.BlockSpec(memory_space=pl.ANY), pl.BlockSpec(memory_space=pltpu.VMEM)
    widths = [d, d, d, d, d, 2 * d, d, d]
    tile = lambda w, dt: pltpu.VMEM((2, rows, w), dt)
    return pl.pallas_call(
        body, name="head", in_specs=[any_spec] * 3 + [vmem] * 3 + [any_spec] * 2 + [vmem],
        out_specs=[any_spec] * n_out + [vmem],
        out_shape=[jax.ShapeDtypeStruct((dm.T, w), BF16) for w in widths] + [jax.ShapeDtypeStruct((8, d), F32)],
        scratch_shapes=[tile(d, BF16), tile(d, BF16), tile(2 * d, BF16), tile(d, F32), tile(d, F32)]
        + [tile(w, BF16) for w in widths]
        + [pltpu.VMEM((tm, d), BF16), pltpu.VMEM((tm, 2 * d), BF16), pltpu.SemaphoreType.DMA((5, 2)),
           pltpu.SemaphoreType.DMA((n_out, 2)), pltpu.SemaphoreType.DMA((n_out, dm.Bl))],
        compiler_params=pltpu.CompilerParams(vmem_limit_bytes=VMEM_LIMIT_BYTES),
    )(y_conv, y_gla, proj_c, w_oc, w_og, w_out, x.reshape(dm.Bl * dm.S, d), target.reshape(dm.Bl * dm.S, d), g_post)


def _grad_h(d_parts, gathered, dy, x, metapad, g_pre, dm):
    d, tm = dm.D, dm.TM
    rows, n_tiles, first_row = _token_tiles(dm, 256)
    widths = [a.shape[1] for a in d_parts]
    np_ = len(d_parts)

    def body(*refs):
        d_hbm, g_hbm, dy_hbm, x_hbm, mp_ref, g_ref = refs[:np_], refs[np_], refs[np_ + 1], refs[np_ + 2], refs[np_ + 3], refs[np_ + 4]
        gx_hbm, dmeta_ref, gg_ref = refs[np_ + 5:np_ + 8]
        parts, edges, sems = refs[np_ + 8:np_ + 12], refs[np_ + 12], refs[np_ + 13]
        dbufs = refs[np_ + 14:2 * np_ + 14]
        dybuf, xbuf, gbuf = refs[2 * np_ + 14:2 * np_ + 17]
        mbufs = refs[2 * np_ + 17:3 * np_ + 17]
        sem_in, sem_out, sem_meta = refs[3 * np_ + 17:]

        def grad_u(tiles):
            du = _dot(tiles[0].astype(BF16), parts[0][...])
            for a, w in zip(tiles[1:], parts[1:]):
                du = du + _dot(a.astype(BF16), w[...])
            return du

        def norm_bwd(h, du, dy):
            rstd = lax.rsqrt(jnp.mean(h * h, axis=-1, keepdims=True) + EPS)
            hhat = h * rstd
            dug = du * g_ref[...]
            gg_ref[0:1, :] += jnp.sum(du * hhat, axis=0, keepdims=True)
            return dy + rstd * (dug - hhat * jnp.mean(dug * hhat, axis=-1, keepdims=True))

        def loads(t, slot):
            padded = list(zip(d_hbm, dbufs)) + [(dy_hbm, dybuf)]
            return ([pltpu.make_async_copy(h.at[pl.ds(first_row(t), rows), :], b.at[slot], sem_in.at[i, slot])
                     for i, (h, b) in enumerate(padded)] +
                    [pltpu.make_async_copy(x_hbm.at[pl.ds(t * rows, rows), :], xbuf.at[slot], sem_in.at[np_ + 1, slot])])

        def stores(t, slot):
            return [pltpu.make_async_copy(gbuf.at[slot], gx_hbm.at[pl.ds(t * rows, rows), :], sem_out.at[slot])]

        def compute(t, slot):
            gbuf[slot] = norm_bwd(xbuf[slot], grad_u([b[slot] for b in dbufs]), dybuf[slot].astype(F32))

        gg_ref[...] = jnp.zeros_like(gg_ref)
        meta = [pltpu.make_async_copy(h.at[pl.ds(b * dm.LP, tm), :], buf.at[b], sem_meta.at[i, b])
                for i, (h, buf) in enumerate(zip(d_hbm, mbufs)) for b in range(dm.Bl)]
        for cp in meta:
            cp.start()
        _load_packed(g_hbm, parts, edges, sems, dm)
        _stream_tiles(n_tiles, loads, stores, compute)
        for cp in meta:
            cp.wait()
        for b in range(dm.Bl):
            dmeta_ref[b] = norm_bwd(mp_ref[...], grad_u([buf[b] for buf in mbufs]), 0.0)

    any_spec, vmem = pl.BlockSpec(memory_space=pl.ANY), pl.BlockSpec(memory_space=pltpu.VMEM)
    grad_x, d_meta, gg = pl.pallas_call(
        body, name="grad_h", in_specs=[any_spec] * (np_ + 3) + [vmem, vmem], out_specs=[any_spec, vmem, vmem],
        out_shape=[jax.ShapeDtypeStruct((dm.Bl * dm.S, d), F32), jax.ShapeDtypeStruct((dm.Bl, tm, d), F32),
                   jax.ShapeDtypeStruct((8, d), F32)],
        scratch_shapes=_packed_scratch(dm)
        + [pltpu.VMEM((2, rows, w), a.dtype) for w, a in zip(widths, d_parts)]
        + [pltpu.VMEM((2, rows, d), BF16), pltpu.VMEM((2, rows, d), F32), pltpu.VMEM((2, rows, d), F32)]
        + [pltpu.VMEM((dm.Bl, tm, w), a.dtype) for w, a in zip(widths, d_parts)]
        + [pltpu.SemaphoreType.DMA((np_ + 2, 2)), pltpu.SemaphoreType.DMA((2,)), pltpu.SemaphoreType.DMA((np_, dm.Bl))],
        compiler_params=pltpu.CompilerParams(vmem_limit_bytes=VMEM_LIMIT_BYTES),
    )(*d_parts, gathered, dy, x.reshape(dm.Bl * dm.S, d), metapad, g_pre)
    return grad_x.reshape(dm.Bl, dm.S, d), d_meta, gg


def _adamw(partials, w, m, v, name, by_columns=False):
    r, c = w.shape
    n_parts = partials.shape[0]
    tr, tc = (r, _pick(c, 128, 128)) if by_columns else (_pick(r, 256, 16), c)

    def body(p_ref, w_ref, m_ref, v_ref, g_ref, d_ref, nm_ref, nv_ref):
        g = p_ref[0].astype(F32)
        for j in range(1, n_parts):
            g = g + p_ref[j].astype(F32)
        g_ref[...] = g
        d_ref[...], nm_ref[...], nv_ref[...] = _adam_step(g, w_ref[...], m_ref[...], v_ref[...])

    at = (lambda i: (0, i)) if by_columns else (lambda i: (i, 0))
    tile = pl.BlockSpec((tr, tc), at)
    out = jax.ShapeDtypeStruct((r, c), F32)
    return pl.pallas_call(
        body, name=name, grid=(c // tc if by_columns else r // tr,),
        in_specs=[pl.BlockSpec((n_parts, tr, tc), lambda i: (0,) + at(i)), tile, tile, tile],
        out_specs=[tile, tile, tile, tile], out_shape=[out, out, out, out], compiler_params=_cp(1),
    )(partials, w, m, v)


def _adam_step(g, w, m, v):
    m2 = ADAM_B1 * m + (1.0 - ADAM_B1) * g
    v2 = ADAM_B2 * v + (1.0 - ADAM_B2) * (g * g)
    m_hat = m2 / (1.0 - ADAM_B1 ** ADAM_STEP)
    v_hat = v2 / (1.0 - ADAM_B2 ** ADAM_STEP)
    return -ADAM_LR * (m_hat / (jnp.sqrt(v_hat) + ADAM_EPS) + ADAM_WD * w), m2, v2


def _adamw_small(items, name):
    n = len(items)

    def body(*refs):
        ins, outs = refs[:4 * n], refs[4 * n:]
        for i in range(n):
            p_ref, w_ref, m_ref, v_ref = ins[4 * i:4 * i + 4]
            g = p_ref[0]
            for j in range(1, p_ref.shape[0]):
                g = g + p_ref[j]
            delta, m2, v2 = _adam_step(g, w_ref[...], m_ref[...], v_ref[...])
            for o_ref, val in zip(outs[4 * i:4 * i + 4], (g, delta, m2, v2)):
                o_ref[...] = val

    vmem = pl.BlockSpec(memory_space=pltpu.VMEM)
    res = pl.pallas_call(
        body, name=name, in_specs=[vmem] * (4 * n), out_specs=[vmem] * (4 * n),
        out_shape=[jax.ShapeDtypeStruct(w.shape, F32) for _, w, _, _ in items for _ in range(4)],
    )(*[a for item in items for a in item])
    return [res[4 * i:4 * i + 4] for i in range(n)]


def _unpack_rows(a, b, c, lr, dm):
    d, hk, hv, cw, nj, hw = dm.D, dm.HK, dm.HV, dm.CW, dm.NJ, dm.HW
    conv = a.reshape(nj, 4, cw, d).transpose(1, 0, 2, 3).reshape(4 * d, d)
    heads = b.reshape(HEADS, hw, d)
    q = heads[:, :hk].reshape(HEADS * hk, d)
    k = heads[:, hk:2 * hk].reshape(HEADS * hk, d)
    v = heads[:, 2 * hk:2 * hk + hv].reshape(HEADS * hv, d)
    r = heads[:, 2 * hk + hv:].reshape(HEADS * hv, d)
    return jnp.concatenate([conv, q, k, v, r, lr[:2 * RANK], c], axis=0)


def _column_shards(g, shard_shape):
    r, c = g.shape
    return g.reshape(r, N_DEV, c // N_DEV).transpose(1, 0, 2).reshape((N_DEV,) + tuple(shard_shape))


def _join_column_shards(parts):
    r, c = parts.shape[-2:]
    return parts.reshape(N_DEV, r, c).transpose(1, 0, 2).reshape(r, N_DEV * c)


def _local_step(x, target, meta, g_pre, u, wt_shards, conv_w, wg_f, bg_f, wg_b, bg_b, gla_g, out_weights, g_post,
                on_matrix_grads=None):
    bl, s, d = x.shape
    dm = _Dims(bl, s, d)
    metapad = jnp.concatenate([jnp.zeros((dm.TM - N_META, d), F32), meta], axis=0)
    wgp_f = jnp.pad(wg_f, ((0, LR_LANES - RANK), (0, 0))).astype(BF16)
    wgp_b = jnp.pad(wg_b, ((RANK, LR_LANES - 2 * RANK), (0, 0))).astype(BF16)

    u = _prenorm_meta(u, metapad, g_pre, dm)
    proj_a, proj_b, proj_c, lr = _inproj(u, wt_shards, dm)
    y_conv = _conv_fwd(proj_a, conv_w, dm)
    o_all, y_gla, states, decays, gate_slopes = _gla_fwd(proj_b, lr, wgp_f, bg_f, wgp_b, bg_b, gla_g, dm)
    w_oc, w_og, w_out = out_weights(y_conv) if callable(out_weights) else out_weights
    merged, d_out, dy, d_pc, d_pg, d_c, dy_conv, dy_gla, stats = _head(y_conv, y_gla, proj_c, w_oc, w_og, w_out, x, target,
                                                                        g_post, dm)

    g_out = _matmul_tn(merged, d_out, BF16, "grad_w_out")
    g_oc = _matmul_tn(y_conv, d_pc, BF16, "grad_w_out_conv")
    g_og = _matmul_tn(y_gla, d_pg, BF16, "grad_w_out_gla")
    if on_matrix_grads is not None:
        conv_w = conv_w + on_matrix_grads(dict(w_out_conv=g_oc, w_out_gla=g_og, w_merge_out=g_out))
    d_a, g_conv = _conv_bwd(proj_a, dy_conv, conv_w, dm)
    d_b, d_lr, gwp_f, gbp_f, gwp_b, gbp_b, g_gla = _gla_bwd(proj_b, lr, o_all, dy_gla, states, decays, gate_slopes, wgp_f, wgp_b, gla_g, dm)
    g_abc = _matmul_tn_group([d_a, d_b, d_c], u, "grad_w_in", tile=d).reshape(9 * d, d)
    g_in = _unpack_rows(g_abc[:4 * d], g_abc[4 * d:7 * d], g_abc[7 * d:], _matmul_tn(d_lr, u, BF16, "grad_w_in_gate"), dm)
    if on_matrix_grads is not None:
        d_lr = d_lr + on_matrix_grads(dict(w_in=g_in))
    grad_x, d_meta, g_pre_rows = _grad_h([d_a, d_b, d_c, d_lr], wt_shards, dy, x, metapad, g_pre, dm)

    grads = dict(
        meta_tokens=jnp.sum(d_meta[:, dm.TM - N_META:, :], axis=0), norm_pre=g_pre_rows[0:1], w_in=g_in,
        conv_w=g_conv[0:3], w_gate_fwd=jnp.sum(gwp_f, axis=0)[:RANK], b_gate_fwd=jnp.sum(gbp_f, axis=0)[0:1],
        w_gate_bwd=jnp.sum(gwp_b, axis=0)[RANK:2 * RANK], b_gate_bwd=jnp.sum(gbp_b, axis=0)[0:1],
        gla_norm=g_gla[0:1], w_out_conv=g_oc, w_out_gla=g_og, w_merge_out=g_out, norm_post=stats[0:1])
    return stats[1:2], grad_x, grads


MATRICES = ("w_out_conv", "w_out_gla", "w_merge_out")
SMALL_SHARDED = ("meta_tokens", "conv_w", "w_gate_fwd", "w_gate_bwd")
REPLICATED = ("norm_pre", "b_gate_fwd", "b_gate_bwd", "gla_norm", "norm_post")
NAMES = ("meta_tokens", "norm_pre", "w_in", "conv_w", "w_gate_fwd", "b_gate_fwd", "w_gate_bwd", "b_gate_bwd", "gla_norm",
         "w_out_conv", "w_out_gla", "w_merge_out", "norm_post")


def kernel(x, meta_tokens, norm_pre, w_in, conv_w, w_gate_fwd, b_gate_fwd, w_gate_bwd, b_gate_bwd, gla_norm, w_out_conv, w_out_gla, w_merge_out, norm_post, loss_target, m_meta_tokens, m_norm_pre, m_w_in, m_conv_w, m_w_gate_fwd, m_b_gate_fwd, m_w_gate_bwd, m_b_gate_bwd, m_gla_norm, m_w_out_conv, m_w_out_gla, m_w_merge_out, m_norm_post, v_meta_tokens, v_norm_pre, v_w_in, v_conv_w, v_w_gate_fwd, v_b_gate_fwd, v_w_gate_bwd, v_b_gate_bwd, v_gla_norm, v_w_out_conv, v_w_out_gla, v_w_merge_out, v_norm_post):
    w = dict(meta_tokens=meta_tokens, norm_pre=norm_pre, w_in=w_in[0], conv_w=conv_w, w_gate_fwd=w_gate_fwd,
             b_gate_fwd=b_gate_fwd, w_gate_bwd=w_gate_bwd, b_gate_bwd=b_gate_bwd, gla_norm=gla_norm,
             w_out_conv=w_out_conv[0], w_out_gla=w_out_gla[0], w_merge_out=w_merge_out[0], norm_post=norm_post)
    m = dict(meta_tokens=m_meta_tokens, norm_pre=m_norm_pre, w_in=m_w_in[0], conv_w=m_conv_w, w_gate_fwd=m_w_gate_fwd,
             b_gate_fwd=m_b_gate_fwd, w_gate_bwd=m_w_gate_bwd, b_gate_bwd=m_b_gate_bwd, gla_norm=m_gla_norm,
             w_out_conv=m_w_out_conv[0], w_out_gla=m_w_out_gla[0], w_merge_out=m_w_merge_out[0], norm_post=m_norm_post)
    v = dict(meta_tokens=v_meta_tokens, norm_pre=v_norm_pre, w_in=v_w_in[0], conv_w=v_conv_w, w_gate_fwd=v_w_gate_fwd,
             b_gate_fwd=v_b_gate_fwd, w_gate_bwd=v_w_gate_bwd, b_gate_bwd=v_b_gate_bwd, gla_norm=v_gla_norm,
             w_out_conv=v_w_out_conv[0], w_out_gla=v_w_out_gla[0], w_merge_out=v_w_merge_out[0], norm_post=v_norm_post)
    d = x.shape[-1]

    dm = _Dims(*x.shape)
    me = 4 * lax.axis_index("x") + 2 * lax.axis_index("y") + lax.axis_index("c")
    wt_shards, *small_all, u = _gather_two_level(
        [_pad_shard(w["w_in"].T.astype(BF16), me)] + [w[n] for n in SMALL_SHARDED], "gather_weights",
        _prenorm_tokens_side(x, norm_pre, dm))
    _, late_weights = _exchange_start([w[n].astype(BF16) for n in MATRICES], [], small_all[0], "gather_out_weights_start")
    small = {n: _join_column_shards(p) for n, p in zip(SMALL_SHARDED, small_all)}

    def out_weights(after):
        return tuple(a.reshape(-1, d) for a in _exchange_wait(late_weights, after, "gather_out_weights_wait"))

    pending = []

    def on_matrix_grads(g):
        token, state = _exchange_start([], [t.reshape(N_DEV, -1, d) for t in g.values()], None,
                                       "exchange_grads_start_" + "_".join(g))
        pending.append((tuple(g), state))
        return token

    sq_err_cols, grad_x, grads = _local_step(
        x, loss_target, small["meta_tokens"], norm_pre, u, wt_shards, small["conv_w"], small["w_gate_fwd"], b_gate_fwd,
        small["w_gate_bwd"], b_gate_bwd, gla_norm, out_weights, norm_post, on_matrix_grads)
    received = {}
    for names, state in pending:
        received.update(zip(names, _exchange_wait(state, grad_x, "exchange_grads_wait_" + "_".join(names))))

    exchanged = _exchange([grads[n] for n in REPLICATED] + [sq_err_cols],
                          [_column_shards(grads[n], w[n].shape) for n in SMALL_SHARDED], "exchange_small_grads")
    small_recv = exchanged[:len(REPLICATED)] + exchanged[len(REPLICATED) + 1:]
    loss = 0.5 / d * jnp.sum(exchanged[len(REPLICATED)])

    results = {"w_in": [r.T[None] for r in _adamw(received["w_in"], w["w_in"].T, m["w_in"].T, v["w_in"].T, "adamw_w_in", by_columns=True)]}
    for n in MATRICES:
        results[n] = [r[None] for r in _adamw(received[n], w[n], m[n], v[n], "adamw_" + n)]
    small_names = REPLICATED + SMALL_SHARDED
    results.update(zip(small_names, _adamw_small([(p, w[n], m[n], v[n]) for n, p in zip(small_names, small_recv)], "adamw_small")))
    return (loss, grad_x, *[results[n][i] for i in range(4) for n in NAMES])
```

```python
import jax
import jax.numpy as jnp
from jax import lax
from jax.experimental import pallas as pl
from jax.experimental.pallas import tpu as pltpu

F32 = jnp.float32
BF16 = jnp.bfloat16
MESH = pl.DeviceIdType.MESH

N_META = 16
CHUNK = 64
CHUNK_SHIFT = 6
HEADS = 4
RANK = 16
LR_LANES = 128
PAD_ROWS = CHUNK - N_META
EPS = 1e-6
GATE_NORMALIZER = 16.0
N_DEV = 8
ADAM_LR, ADAM_B1, ADAM_B2, ADAM_EPS, ADAM_WD, ADAM_STEP = 0.001, 0.9, 0.999, 1e-08, 0.01, 10
VMEM_LIMIT_BYTES = 56 * 1024 * 1024


class _Dims:
    def __init__(self, bl, s, d):
        self.Bl, self.S, self.D = bl, s, d
        self.TM = CHUNK
        self.LP = self.TM + s
        self.T = bl * self.LP
        self.TPS = self.LP // self.TM
        self.NC = self.LP // CHUNK
        self.C0 = (self.TM - CHUNK) // CHUNK
        self.DK, self.DV = d // 2, d
        self.HK, self.HV = self.DK // HEADS, self.DV // HEADS
        self.HW = 2 * self.HK + 2 * self.HV
        self.CW = 256 if d % 256 == 0 and d > 256 else d // 4
        self.NJ = d // self.CW


def _pick(n, target, mult):
    t = min(n, target)
    while t >= mult:
        if n % t == 0 and t % mult == 0:
            return t
        t -= mult
    return n


def _cp(n_axes):
    return pltpu.CompilerParams(dimension_semantics=("arbitrary",) * n_axes, vmem_limit_bytes=VMEM_LIMIT_BYTES)


def _sigmoid(x):
    return 1.0 / (1.0 + jnp.exp(-x))


def _dot(a, b):
    return jnp.dot(a, b, preferred_element_type=F32)


def _dot_nt(a, b):
    return lax.dot_general(a, b, (((1,), (1,)), ((), ())), preferred_element_type=F32)


def _dot_tn(a, b):
    return lax.dot_general(a, b, (((0,), (0,)), ((), ())), preferred_element_type=F32)


def _chunk_cumsum(x, reverse):
    rows = x.shape[0]
    r = lax.broadcasted_iota(jnp.int32, x.shape, 0) & (CHUNK - 1)
    step = 1
    while step < CHUNK:
        if reverse:
            x = x + jnp.where(r < CHUNK - step, pltpu.roll(x, rows - step, 0), 0.0)
        else:
            x = x + jnp.where(r >= step, pltpu.roll(x, step, 0), 0.0)
        step *= 2
    return x


def _exchange(gathers, scatters, name):
    arrays = list(gathers) + list(scatters)
    n, ng = len(arrays), len(gathers)

    def body(*refs):
        ins, outs = refs[:n], refs[n:2 * n]
        send_sems, recv_sems, local_sems = refs[2 * n:]
        x, y, c = lax.axis_index("x"), lax.axis_index("y"), lax.axis_index("c")
        me = 4 * x + 2 * y + c
        started = []
        for t in range(n):
            src, dst = ins[t], outs[t]
            own = pltpu.make_async_copy(src if t < ng else src.at[me], dst.at[me], local_sems.at[t])
            own.start()
            started.append(own)
            for k, pos, peer in _peers(x, y, c):
                cp = pltpu.make_async_remote_copy(
                    src_ref=src if t < ng else src.at[peer], dst_ref=dst.at[me],
                    send_sem=send_sems.at[t * (N_DEV - 1) + k - 1], recv_sem=recv_sems.at[t * (N_DEV - 1) + k - 1],
                    device_id=pos, device_id_type=MESH)
                cp.start()
                started.append(cp)
        for cp in started:
            cp.wait()

    out_shape = [jax.ShapeDtypeStruct((N_DEV,) + a.shape if t < ng else a.shape, a.dtype) for t, a in enumerate(arrays)]
    any_spec = pl.BlockSpec(memory_space=pl.ANY)
    return pl.pallas_call(
        body, name=name, out_shape=out_shape, in_specs=[any_spec] * n, out_specs=[any_spec] * n,
        scratch_shapes=[pltpu.SemaphoreType.DMA((n * (N_DEV - 1),)), pltpu.SemaphoreType.DMA((n * (N_DEV - 1),)),
                        pltpu.SemaphoreType.DMA((n,))],
        compiler_params=pltpu.CompilerParams(has_side_effects=True),
    )(*arrays)


def _gather_two_level(arrays, name, side=None):
    n = len(arrays)
    per = N_DEV - 1
    work, side_in, side_in_specs, side_out, side_out_specs, side_scratch = side or (None, [], [], [], [], [])
    n_in, n_out = len(side_in), len(side_out)

    def body(*refs):
        ins, outs = refs[:n], refs[n + n_in:2 * n + n_in]
        send_sems, recv_sems, local_sems = refs[2 * n + n_in + n_out:2 * n + n_in + n_out + 3]
        x, y, c = lax.axis_index("x"), lax.axis_index("y"), lax.axis_index("c")
        sibling = (x, y, 1 - c)
        chips = [(1 - x, y), (x, 1 - y), (1 - x, 1 - y)]
        index = lambda px, py, pc: 4 * px + 2 * py + pc

        def copy(t, k, block, to, from_input=False):
            slab = outs[t].at[index(*block)]
            return pltpu.make_async_remote_copy(
                src_ref=ins[t] if from_input else slab, dst_ref=slab, send_sem=send_sems.at[t * per + k],
                recv_sem=recv_sems.at[t * per + k], device_id=to, device_id_type=MESH)

        own, sent = [], []
        for t in range(n):
            own.append(pltpu.make_async_copy(ins[t], outs[t].at[index(x, y, c)], local_sems.at[t]))
            own[-1].start()
            first = [copy(t, 0, (x, y, c), sibling, True)]
            first += [copy(t, 1 + j, (x, y, c), (*chip, c), True) for j, chip in enumerate(chips)]
            for cp in first:
                cp.start()
            sent += first
        if work is not None:
            work(refs[n:n + n_in], refs[2 * n + n_in:2 * n + n_in + n_out], refs[2 * n + n_in + n_out + 3:])
        for t in range(n):
            for j, chip in enumerate(chips):
                copy(t, 1 + j, (*chip, c), (x, y, c)).wait_recv()
                sent.append(copy(t, 4 + j, (*chip, c), sibling))
                sent[-1].start()
        for t in range(n):
            copy(t, 0, sibling, (x, y, c)).wait_recv()
            for j, chip in enumerate(chips):
                copy(t, 4 + j, (*chip, 1 - c), (x, y, c)).wait_recv()
        for cp in sent:
            cp.wait_send()
        for cp in own:
            cp.wait()

    out_shape = [jax.ShapeDtypeStruct((N_DEV,) + a.shape, a.dtype) for a in arrays]
    any_spec = pl.BlockSpec(memory_space=pl.ANY)
    return pl.pallas_call(
        body, name=name, out_shape=out_shape + list(side_out), in_specs=[any_spec] * n + list(side_in_specs),
        out_specs=[any_spec] * n + list(side_out_specs),
        scratch_shapes=[pltpu.SemaphoreType.DMA((n * per,)), pltpu.SemaphoreType.DMA((n * per,)),
                        pltpu.SemaphoreType.DMA((n,))] + list(side_scratch),
        compiler_params=pltpu.CompilerParams(has_side_effects=True, vmem_limit_bytes=VMEM_LIMIT_BYTES),
    )(*arrays, *side_in)


def _peers(x, y, c):
    out = []
    for k in range(1, N_DEV):
        px = 1 - x if (k >> 2) & 1 else x
        py = 1 - y if (k >> 1) & 1 else y
        pc = 1 - c if k & 1 else c
        out.append((k, (px, py, pc), 4 * px + 2 * py + pc))
    return out


def _exchange_start(gathers, scatters, after, name):
    arrays = list(gathers) + list(scatters)
    n, ng = len(arrays), len(gathers)
    hbm = pl.BlockSpec(memory_space=pltpu.HBM)
    sem = pl.BlockSpec(memory_space=pltpu.SEMAPHORE)

    extra = [] if after is None else [after]
    ne = len(extra)

    def body(*refs):
        ins, lands = refs[:n], refs[n:2 * n]
        send_sems, recv_sems = refs[2 * n + ne], refs[2 * n + ne + 1]
        token = refs[4 * n + ne + 2]
        x, y, c = lax.axis_index("x"), lax.axis_index("y"), lax.axis_index("c")
        me = 4 * x + 2 * y + c
        for t in range(n):
            for k, pos, peer in _peers(x, y, c):
                pltpu.make_async_remote_copy(
                    src_ref=ins[t] if t < ng else ins[t].at[peer], dst_ref=lands[t].at[me],
                    send_sem=send_sems.at[t * (N_DEV - 1) + k - 1], recv_sem=recv_sems.at[t * (N_DEV - 1) + k - 1],
                    device_id=pos, device_id_type=MESH).start()
        token[...] = jnp.zeros_like(token)

    me = 4 * lax.axis_index("x") + 2 * lax.axis_index("y") + lax.axis_index("c")
    lands = [lax.dynamic_update_index_in_dim(lax.empty((N_DEV,) + a.shape if t < ng else a.shape, a.dtype),
                                             a if t < ng else lax.dynamic_index_in_dim(a, me, 0, keepdims=False), me, 0)
             for t, a in enumerate(arrays)]
    operands = [pltpu.with_memory_space_constraint(a, pltpu.HBM) for a in arrays + lands]
    sems = pltpu.SemaphoreType.DMA((n * (N_DEV - 1),))
    res = pl.pallas_call(
        body, name=name,
        out_shape=(sems, sems, *[pltpu.HBM(a.shape, a.dtype) for a in arrays + lands], jax.ShapeDtypeStruct((8, 128), F32)),
        in_specs=[hbm] * (2 * n) + [pl.BlockSpec(memory_space=pl.ANY)] * ne,
        out_specs=(sem, sem, *[hbm] * (2 * n), pl.BlockSpec(memory_space=pltpu.VMEM)),
        input_output_aliases={i: 2 + i for i in range(2 * n)},
        compiler_params=pltpu.CompilerParams(has_side_effects=pltpu.SideEffectType.DATAFLOW_SIDE_EFFECTING),
    )(*operands, *extra)
    return res[-1][0, 0], (ng, res[0], res[1], list(res[2:2 + n]), list(res[2 + n:2 + 2 * n]))


def _exchange_wait(state, after, name):
    ng, send_sems, recv_sems, sent, lands = state
    n = len(sent)
    hbm = pl.BlockSpec(memory_space=pltpu.HBM)
    sem = pl.BlockSpec(memory_space=pltpu.SEMAPHORE)

    def body(*refs):
        ins, land_refs = refs[:n], refs[n:2 * n]
        send_ref, recv_ref = refs[2 * n], refs[2 * n + 1]
        x, y, c = lax.axis_index("x"), lax.axis_index("y"), lax.axis_index("c")
        me = 4 * x + 2 * y + c
        for t in range(n):
            for k, pos, peer in _peers(x, y, c):
                cp = pltpu.make_async_remote_copy(
                    src_ref=ins[t] if t < ng else ins[t].at[peer], dst_ref=land_refs[t].at[me],
                    send_sem=send_ref.at[t * (N_DEV - 1) + k - 1], recv_sem=recv_ref.at[t * (N_DEV - 1) + k - 1],
                    device_id=pos, device_id_type=MESH)
                cp.wait_send()
                cp.wait_recv()

    res = pl.pallas_call(
        body, name=name, out_shape=tuple(pltpu.HBM(a.shape, a.dtype) for a in sent + lands),
        in_specs=[hbm] * (2 * n) + [sem, sem, pl.BlockSpec(memory_space=pl.ANY)], out_specs=tuple([hbm] * (2 * n)),
        input_output_aliases={i: i for i in range(2 * n)},
        compiler_params=pltpu.CompilerParams(has_side_effects=pltpu.SideEffectType.DATAFLOW_SIDE_EFFECTING),
    )(*sent, *lands, send_sems, recv_sems, after)
    return list(res[n:])


def _rms_scaled(h, g):
    return (h * lax.rsqrt(jnp.mean(h * h, axis=-1, keepdims=True) + EPS) * g).astype(BF16)


def _prenorm_tokens_side(x, g_pre, dm):
    bl, s, d = x.shape
    rows = _pick(s, 512, 16)
    tiles = [(b, j) for b in range(bl) for j in range(s // rows)]

    def work(ins, outs, scratch):
        (x_ref, g_ref), (u_ref,), (xbuf, ubuf, sem_in, sem_out) = ins, outs, scratch

        def load(t, slot):
            b, j = tiles[t]
            return pltpu.make_async_copy(x_ref.at[b, pl.ds(j * rows, rows), :], xbuf.at[slot], sem_in.at[slot])

        def store(t, slot):
            b, j = tiles[t]
            return pltpu.make_async_copy(ubuf.at[slot], u_ref.at[pl.ds(b * dm.LP + dm.TM + j * rows, rows), :], sem_out.at[slot])

        load(0, 0).start()
        for t in range(len(tiles)):
            slot = t % 2
            if t + 1 < len(tiles):
                load(t + 1, 1 - slot).start()
            load(t, slot).wait()
            if t >= 2:
                store(t - 2, slot).wait()
            ubuf[slot] = _rms_scaled(xbuf[slot], g_ref[...])
            store(t, slot).start()
        for t in range(max(len(tiles) - 2, 0), len(tiles)):
            store(t, t % 2).wait()

    any_spec = pl.BlockSpec(memory_space=pl.ANY)
    return (work, [x, g_pre], [any_spec, pl.BlockSpec(memory_space=pltpu.VMEM)],
            [jax.ShapeDtypeStruct((dm.T, d), BF16)], [any_spec],
            [pltpu.VMEM((2, rows, d), F32), pltpu.VMEM((2, rows, d), BF16), pltpu.SemaphoreType.DMA((2,)),
             pltpu.SemaphoreType.DMA((2,))])


def _prenorm_meta(u, metapad, g_pre, dm):
    tm, tps, d = dm.TM, dm.TPS, dm.D

    def body(u_in, mp_ref, g_ref, u_ref):
        u_ref[...] = _rms_scaled(mp_ref[...], g_ref[...])

    return pl.pallas_call(
        body, name="prenorm_meta", grid=(dm.Bl,),
        in_specs=[pl.BlockSpec(memory_space=pl.ANY), pl.BlockSpec((tm, d), lambda i: (0, 0)),
                  pl.BlockSpec((1, d), lambda i: (0, 0))],
        out_specs=pl.BlockSpec((tm, d), lambda i: (i * tps, 0)),
        out_shape=jax.ShapeDtypeStruct((dm.T, d), BF16), input_output_aliases={0: 0}, compiler_params=_cp(1),
    )(u, metapad, g_pre)


def _matmul_tn(a, b, out_dtype, name, tt=2304, tn=1024, tk=1024):
    t, k = a.shape
    n = b.shape[1]
    tt, tn, tk = _pick(t, tt, 16), _pick(n, tn, 128), _pick(k, tk, 128)
    nt = t // tt

    def body(a_ref, b_ref, o_ref, acc):
        p = _dot_tn(a_ref[...].astype(BF16), b_ref[...].astype(BF16))
        i = pl.program_id(2)

        @pl.when(i == 0)
        def _():
            acc[...] = p

        @pl.when(i > 0)
        def _():
            acc[...] += p

        @pl.when(i == nt - 1)
        def _():
            o_ref[...] = acc[...].astype(out_dtype)

    return pl.pallas_call(
        body, name=name, grid=(k // tk, n // tn, nt),
        in_specs=[pl.BlockSpec((tt, tk), lambda kk, j, i: (i, kk)), pl.BlockSpec((tt, tn), lambda kk, j, i: (i, j))],
        out_specs=pl.BlockSpec((tk, tn), lambda kk, j, i: (kk, j)),
        out_shape=jax.ShapeDtypeStruct((k, n), out_dtype), scratch_shapes=[pltpu.VMEM((tk, tn), F32)],
        compiler_params=_cp(3),
    )(a, b)


def _matmul_tn_group(a_list, b, name, tt=2304, tile=1024):
    t, n = b.shape
    tt = _pick(t, tt, 16)
    nt = t // tt
    counts = [a.shape[1] // tile for a in a_list]
    starts = [sum(counts[:m]) for m in range(len(a_list))]
    items = sum(counts)

    def active(p, m):
        return (p >= starts[m]) & (p < starts[m] + counts[m])

    def body(*refs):
        a_refs, b_ref = refs[:len(a_list)], refs[len(a_list)]
        o_ref, acc = refs[-2], refs[-1]
        p, i = pl.program_id(0), pl.program_id(1)
        for m, a_ref in enumerate(a_refs):
            @pl.when(active(p, m))
            def _(a_ref=a_ref):
                prod = _dot_tn(a_ref[...].astype(BF16), b_ref[...].astype(BF16))

                @pl.when(i == 0)
                def _():
                    acc[...] = prod

                @pl.when(i > 0)
                def _():
                    acc[...] += prod

        @pl.when(i == nt - 1)
        def _():
            o_ref[0] = acc[...].astype(BF16)

    a_specs = [pl.BlockSpec((tt, tile), lambda p, i, m=m: (jnp.where(active(p, m), i, 0), jnp.where(active(p, m), p - starts[m], 0)))
               for m in range(len(a_list))]
    return pl.pallas_call(
        body, name=name, grid=(items, nt), in_specs=a_specs + [pl.BlockSpec((tt, n), lambda p, i: (i, 0))],
        out_specs=pl.BlockSpec((1, tile, n), lambda p, i: (p, 0, 0)),
        out_shape=jax.ShapeDtypeStruct((items, tile, n), BF16), scratch_shapes=[pltpu.VMEM((tile, n), F32)],
        compiler_params=_cp(2),
    )(*a_list, b)


BF16_TILE_ROWS = 16


def _shard_offset(index, shard_rows):
    return (index * shard_rows) % BF16_TILE_ROWS


def _pad_shard(wt_shard, index):
    rows, d = wt_shard.shape
    padded = -(-(rows + max(_shard_offset(j, rows) for j in range(N_DEV))) // BF16_TILE_ROWS) * BF16_TILE_ROWS
    return lax.dynamic_update_slice(jnp.zeros((padded, d), wt_shard.dtype), wt_shard, (_shard_offset(index, rows), 0))


def _packed_parts(dm):
    d, dk, hk, hv, cw, nj, hw = dm.D, dm.DK, dm.HK, dm.HV, dm.CW, dm.NJ, dm.HW
    blocks = [(0, (j * 4 + p) * cw, p * d + j * cw, cw) for j in range(nj) for p in range(4)]
    for h in range(HEADS):
        blocks += [(1, h * hw, 4 * d + h * hk, hk), (1, h * hw + hk, 4 * d + dk + h * hk, hk),
                   (1, h * hw + 2 * hk, 5 * d + h * hv, hv), (1, h * hw + 2 * hk + hv, 6 * d + h * hv, hv)]
    blocks += [(2, 0, 7 * d + 2 * RANK, 2 * d), (3, 0, 7 * d, 2 * RANK)]
    return [4 * d, 3 * d, 2 * d, LR_LANES], blocks


def _pack_plan(dm):
    sh = (9 * dm.D + 2 * RANK) // N_DEV
    tile = BF16_TILE_ROWS
    copies, straddles = [], []
    for part, dst, r0, n in _packed_parts(dm)[1]:
        for j in range(N_DEV):
            a, b = max(r0, sh * j), min(r0 + n, sh * (j + 1))
            if a >= b:
                continue
            a_up, b_down = -(-a // tile) * tile, b // tile * tile
            if b_down > a_up:
                copies.append((j, a_up - sh * j + _shard_offset(j, sh), b_down - a_up, part, dst + a_up - r0))
            if a % tile:
                lo = a // tile * tile
                straddles.append((j, lo - sh * (j - 1) + _shard_offset(j - 1, sh), part, dst + lo - r0, a - lo))
    return copies, straddles


def _packed_scratch(dm):
    copies, straddles = _pack_plan(dm)
    return ([pltpu.VMEM((rows, dm.D), BF16) for rows in _packed_parts(dm)[0]]
            + [pltpu.VMEM((2 * max(len(straddles), 1), BF16_TILE_ROWS, dm.D), BF16),
               pltpu.SemaphoreType.DMA((len(copies) + 2 * len(straddles),))])


def _load_packed(g_ref, parts, edges, sems, dm):
    copies, straddles = _pack_plan(dm)
    tile = BF16_TILE_ROWS
    parts[3][2 * RANK:, :] = jnp.zeros((LR_LANES - 2 * RANK, dm.D), BF16)
    dmas = [pltpu.make_async_copy(g_ref.at[j, pl.ds(src, n), :], parts[p].at[pl.ds(dst, n), :], sems.at[i])
            for i, (j, src, n, p, dst) in enumerate(copies)]
    for i, (j, src, p, dst, split) in enumerate(straddles):
        k = len(copies) + 2 * i
        dmas.append(pltpu.make_async_copy(g_ref.at[j - 1, pl.ds(src, tile), :], edges.at[2 * i], sems.at[k]))
        dmas.append(pltpu.make_async_copy(g_ref.at[j, pl.ds(0, tile), :], edges.at[2 * i + 1], sems.at[k + 1]))
    for cp in dmas:
        cp.start()
    for cp in dmas:
        cp.wait()
    row = lax.broadcasted_iota(jnp.int32, (tile, dm.D), 0)
    for i, (j, src, p, dst, split) in enumerate(straddles):
        parts[p][dst:dst + tile, :] = jnp.where(row < split, edges[2 * i], edges[2 * i + 1])


def _inproj(u, gathered, dm):
    t, d = u.shape
    tm = _pick(t, 512, 16)
    widths = _packed_parts(dm)[0]
    cn = 1024

    def body(u_ref, g_ref, *rest):
        outs, parts, (edges, sems) = rest[:4], rest[4:8], rest[8:]

        @pl.when(pl.program_id(0) == 0)
        def _():
            _load_packed(g_ref, parts, edges, sems, dm)

        ut = u_ref[...]
        for w, o_ref in zip(parts, outs):
            n = w.shape[0]
            step = cn if n % cn == 0 else n
            for j in range(0, n, step):
                o_ref[:, j:j + step] = _dot_nt(ut, w[j:j + step, :]).astype(BF16)

    return pl.pallas_call(
        body, name="inproj", grid=(t // tm,),
        in_specs=[pl.BlockSpec((tm, d), lambda i: (i, 0)), pl.BlockSpec(memory_space=pl.ANY)],
        out_specs=[pl.BlockSpec((tm, w), lambda i: (i, 0)) for w in widths],
        out_shape=[jax.ShapeDtypeStruct((t, w), BF16) for w in widths],
        scratch_shapes=_packed_scratch(dm), compiler_params=_cp(1),
    )(u, gathered)


def _conv_rows(dm):
    return _pick(dm.LP, 256, 16)


def _shifted(m, prev_row, next_row, rows):
    row = lax.broadcasted_iota(jnp.int32, m.shape, 0)
    m_prev = jnp.where(row == 0, prev_row, pltpu.roll(m, 1, 0))
    m_next = jnp.where(row == rows - 1, next_row, pltpu.roll(m, rows - 1, 0))
    return m_prev, m_next


def _conv_fwd(proj_a, conv_w, dm):
    lp, cw, rc = dm.LP, dm.CW, _conv_rows(dm)
    nchunk = lp // rc

    def body(p_ref, w_ref, y_ref):
        w0, w1, w2 = w_ref[0:1, :], w_ref[1:2, :], w_ref[2:3, :]

        def chunk(ci, carry):
            r0 = pl.multiple_of(ci * rc, rc)
            blk = p_ref[pl.ds(r0, rc), :].astype(F32)
            cb, cc, cx, cz = (blk[:, i * cw:(i + 1) * cw] for i in range(4))
            m = cc * cx
            rp = pl.multiple_of(jnp.maximum(r0 - 16, 0), 16)
            rn = pl.multiple_of(jnp.minimum(r0 + rc, lp - 16), 16)
            pv = p_ref[pl.ds(rp, 16), cw:3 * cw].astype(F32)
            nx = p_ref[pl.ds(rn, 16), cw:3 * cw].astype(F32)
            prev_row = jnp.where(ci > 0, pv[15:16, :cw] * pv[15:16, cw:], 0.0)
            next_row = jnp.where(ci < nchunk - 1, nx[0:1, :cw] * nx[0:1, cw:], 0.0)
            m_prev, m_next = _shifted(m, prev_row, next_row, rc)
            s = w0 * m_prev + w1 * m + w2 * m_next
            y_ref[pl.ds(r0, rc), :] = (cb * s * (cz * _sigmoid(cz))).astype(BF16)
            return carry

        lax.fori_loop(0, nchunk, chunk, 0)

    return pl.pallas_call(
        body, name="conv_fwd", grid=(dm.Bl, dm.NJ),
        in_specs=[pl.BlockSpec((lp, 4 * cw), lambda s, j: (s, j)), pl.BlockSpec((3, cw), lambda s, j: (0, j))],
        out_specs=pl.BlockSpec((lp, cw), lambda s, j: (s, j)),
        out_shape=jax.ShapeDtypeStruct((dm.T, dm.D), BF16), compiler_params=_cp(2),
    )(proj_a, conv_w)


def _conv_bwd(proj_a, dy_conv, conv_w, dm):
    lp, cw, rc = dm.LP, dm.CW, _conv_rows(dm)
    nchunk = lp // rc

    def body(p_ref, dy_ref, w_ref, d_ref, gw_ref):
        w0, w1, w2 = w_ref[0:1, :], w_ref[1:2, :], w_ref[2:3, :]

        def ds_of(p4, dy):
            cb, cz = p4[:, :cw], p4[:, 3 * cw:]
            return dy * cb * (cz * _sigmoid(cz))

        def chunk(ci, carry):
            g0, g1, g2 = carry
            r0 = pl.multiple_of(ci * rc, rc)
            blk = p_ref[pl.ds(r0, rc), :].astype(F32)
            dy = dy_ref[pl.ds(r0, rc), :].astype(F32)
            cb, cc, cx, cz = (blk[:, i * cw:(i + 1) * cw] for i in range(4))
            rp = pl.multiple_of(jnp.maximum(r0 - 16, 0), 16)
            rn = pl.multiple_of(jnp.minimum(r0 + rc, lp - 16), 16)
            pv = p_ref[pl.ds(rp, 16), :].astype(F32)[15:16]
            nx = p_ref[pl.ds(rn, 16), :].astype(F32)[0:1]
            dpv = dy_ref[pl.ds(rp, 16), :].astype(F32)[15:16]
            dnx = dy_ref[pl.ds(rn, 16), :].astype(F32)[0:1]
            has_prev, has_next = ci > 0, ci < nchunk - 1
            m = cc * cx
            m_prev, m_next = _shifted(m, jnp.where(has_prev, pv[:, cw:2 * cw] * pv[:, 2 * cw:3 * cw], 0.0),
                                      jnp.where(has_next, nx[:, cw:2 * cw] * nx[:, 2 * cw:3 * cw], 0.0), rc)
            s = w0 * m_prev + w1 * m + w2 * m_next
            sg = _sigmoid(cz)
            silu = cz * sg
            ds = dy * cb * silu
            ds_prev, ds_next = _shifted(ds, jnp.where(has_prev, ds_of(pv, dpv), 0.0),
                                        jnp.where(has_next, ds_of(nx, dnx), 0.0), rc)
            dm_ = w0 * ds_next + w1 * ds + w2 * ds_prev
            d_ref[pl.ds(r0, rc), 0:cw] = (dy * s * silu).astype(BF16)
            d_ref[pl.ds(r0, rc), cw:2 * cw] = (dm_ * cx).astype(BF16)
            d_ref[pl.ds(r0, rc), 2 * cw:3 * cw] = (dm_ * cc).astype(BF16)
            d_ref[pl.ds(r0, rc), 3 * cw:4 * cw] = (dy * cb * s * (sg * (1.0 + cz * (1.0 - sg)))).astype(BF16)
            return (g0 + jnp.sum(ds * m_prev, axis=0, keepdims=True), g1 + jnp.sum(ds * m, axis=0, keepdims=True),
                    g2 + jnp.sum(ds * m_next, axis=0, keepdims=True))

        z = jnp.zeros((1, cw), F32)
        g0, g1, g2 = lax.fori_loop(0, nchunk, chunk, (z, z, z))

        @pl.when(pl.program_id(1) == 0)
        def _():
            gw_ref[...] = jnp.zeros_like(gw_ref)

        gw_ref[0:1, :] += g0
        gw_ref[1:2, :] += g1
        gw_ref[2:3, :] += g2

    return pl.pallas_call(
        body, name="conv_bwd", grid=(dm.NJ, dm.Bl),
        in_specs=[pl.BlockSpec((lp, 4 * cw), lambda j, s: (s, j)), pl.BlockSpec((lp, cw), lambda j, s: (s, j)),
                  pl.BlockSpec((3, cw), lambda j, s: (0, j))],
        out_specs=[pl.BlockSpec((lp, 4 * cw), lambda j, s: (s, j)), pl.BlockSpec((8, cw), lambda j, s: (0, j))],
        out_shape=[jax.ShapeDtypeStruct((dm.T, 4 * dm.D), BF16), jax.ShapeDtypeStruct((8, dm.D), F32)],
        compiler_params=_cp(2),
    )(proj_a, dy_conv, conv_w)


def _interleave(gens):
    results = [None] * len(gens)
    live = list(range(len(gens)))
    while live:
        for idx in list(live):
            try:
                next(gens[idx])
            except StopIteration as done:
                results[idx] = done.value
                live.remove(idx)
    return results


def _group_chunks(dm):
    n = dm.NC - dm.C0
    return 3 if n % 3 == 0 else 1


def _group_masks(rows):
    ii = lax.broadcasted_iota(jnp.int32, (rows, rows), 0)
    jj = lax.broadcasted_iota(jnp.int32, (rows, rows), 1)
    same = jnp.right_shift(ii, CHUNK_SHIFT) == jnp.right_shift(jj, CHUNK_SHIFT)
    return same & (jj <= ii), same & (jj > ii)


def _first_row(chunk):
    return chunk * CHUNK if isinstance(chunk, int) else pl.multiple_of(chunk * CHUNK, CHUNK)


def _chunk_totals(b, fwd):
    hk = b.shape[1]
    rows = [b[c * CHUNK + CHUNK - 1:(c + 1) * CHUNK] if fwd else b[c * CHUNK:c * CHUNK + 1]
            for c in range(b.shape[0] // CHUNK)]
    return jnp.concatenate([jnp.broadcast_to(r, (CHUNK, hk)) for r in rows], axis=0)


def _log_gate(lr_rows, w_ref, b_ref, first_group, hk):
    z = _dot(lr_rows, w_ref[...]) + b_ref[...]
    e = jnp.exp(-jnp.abs(z))
    g = (jnp.minimum(z, 0.0) - jnp.log(1.0 + e)) * (1.0 / GATE_NORMALIZER)
    dg_dz = jnp.where(z >= 0.0, e, 1.0) / (1.0 + e) * (1.0 / GATE_NORMALIZER)
    row = lax.broadcasted_iota(jnp.int32, (lr_rows.shape[0], hk), 0)
    pad = first_group & (row < PAD_ROWS)
    return jnp.where(pad, 0.0, g), jnp.where(pad, 0.0, dg_dz)


def _gla_fwd(proj_b, lr, wg_f, bg_f, wg_b, bg_b, gla_g, dm):
    lp, hk, hv, nc, c0, hw = dm.LP, dm.HK, dm.HV, dm.NC, dm.C0, dm.HW
    scale = hk ** -0.5
    gc = _group_chunks(dm)
    gr, ng = gc * CHUNK, (nc - c0) // gc

    def body(p_ref, lr_ref, wf_ref, bf_ref, wb_ref, bb_ref, gg_ref, o_ref, y_ref, st_ref, b_out, gs_out, oacc_f, oacc_b):
        low_incl, up_strict = _group_masks(gr)
        if c0 > 0:
            zr = c0 * CHUNK
            o_ref[0:zr, :] = jnp.zeros((zr, hv), BF16)
            y_ref[0:zr, :] = jnp.zeros((zr, hv), BF16)
            b_out[:, 0:zr, :] = jnp.zeros((2, zr, hk), F32)
            gs_out[:, 0:zr, :] = jnp.zeros((2, zr, hk), F32)
            st_ref[0, 0, :, 0:c0] = jnp.zeros((2, c0, hv, hk), BF16)

        def decay(gi, fwd):
            w_ref, b_ref = (wf_ref, bf_ref) if fwd else (wb_ref, bb_ref)
            r0 = _first_row(c0 + gi * gc)
            yield
            g, dg_dz = _log_gate(lr_ref[pl.ds(r0, gr), :], w_ref, b_ref, gi == 0, hk)
            gs_out[0 if fwd else 1, pl.ds(r0, gr), :] = dg_dz
            yield
            b = _chunk_cumsum(g, not fwd)
            b_out[0 if fwd else 1, pl.ds(r0, gr), :] = b
            return b

        def group(gi, st, b, fwd):
            oacc = oacc_f if fwd else oacc_b
            r0 = pl.multiple_of((c0 + gi * gc) * CHUNK, CHUNK)
            blk = p_ref[pl.ds(r0, gr), :]
            q = blk[:, :hk].astype(F32) * scale
            k = blk[:, hk:2 * hk].astype(F32)
            v = blk[:, 2 * hk:2 * hk + hv]
            btot = _chunk_totals(b, fwd)
            qi = (q * jnp.exp(b)).astype(BF16)
            ki = (k * jnp.exp(-b)).astype(BF16)
            kd = (k * jnp.exp(btot - b)).astype(BF16)
            dec = jnp.exp(btot)
            a = _dot_nt(qi, ki)
            yield
            o = _dot(jnp.where(low_incl if fwd else up_strict, a, 0.0).astype(BF16), v)
            chunk_rows = [slice(c * CHUNK, (c + 1) * CHUNK) for c in range(gc)]
            kv = [_dot_tn(v[rows], kd[rows]) for rows in chunk_rows]
            for c in (range(gc) if fwd else reversed(range(gc))):
                yield
                rows = chunk_rows[c]
                st_b = st.astype(BF16)
                st_ref[0, 0, 0 if fwd else 1, c0 + gi * gc + c] = st_b
                oacc[pl.ds(r0 + c * CHUNK, CHUNK), :] = o[rows] + _dot_nt(qi[rows], st_b)
                st = st * dec[c * CHUNK:c * CHUNK + 1] + kv[c]
            return st

        def step(i, carry):
            st_f, st_b, b_f, b_b = carry
            gf, gb = i, ng - 1 - i
            return tuple(_interleave([group(gf, st_f, b_f, True), group(gb, st_b, b_b, False),
                                      decay(jnp.minimum(gf + 1, ng - 1), True), decay(jnp.maximum(gb - 1, 0), False)]))

        zero = jnp.zeros((hv, hk), F32)
        lax.fori_loop(0, ng, step, (zero, zero, *_interleave([decay(0, True), decay(ng - 1, False)])))

        def finish(i, carry):
            r0 = pl.multiple_of((c0 + i * gc) * CHUNK, CHUNK)
            o = oacc_f[pl.ds(r0, gr), :] + oacc_b[pl.ds(r0, gr), :]
            r = p_ref[pl.ds(r0, gr), 2 * hk + hv:].astype(F32)
            on = o * lax.rsqrt(jnp.mean(o * o, axis=-1, keepdims=True) + EPS) * gg_ref[...]
            o_ref[pl.ds(r0, gr), :] = o.astype(BF16)
            y_ref[pl.ds(r0, gr), :] = (on * r * _sigmoid(r)).astype(BF16)
            return carry

        lax.fori_loop(0, ng, finish, 0)

    head = lambda s, h: (s, h)
    wspec = pl.BlockSpec((LR_LANES, hk), lambda s, h: (0, h))
    bspec = pl.BlockSpec((1, hk), lambda s, h: (0, h))
    return pl.pallas_call(
        body, name="gla_fwd", grid=(dm.Bl, HEADS),
        in_specs=[pl.BlockSpec((lp, hw), head), pl.BlockSpec((lp, LR_LANES), lambda s, h: (s, 0)),
                  wspec, bspec, wspec, bspec, pl.BlockSpec((1, hv), lambda s, h: (0, 0))],
        out_specs=[pl.BlockSpec((lp, hv), head), pl.BlockSpec((lp, hv), head),
                   pl.BlockSpec((1, 1, 2, nc, hv, hk), lambda s, h: (s, h, 0, 0, 0, 0)),
                   pl.BlockSpec((2, lp, hk), lambda s, h: (0, s, h)), pl.BlockSpec((2, lp, hk), lambda s, h: (0, s, h))],
        out_shape=[jax.ShapeDtypeStruct((dm.T, dm.DV), BF16), jax.ShapeDtypeStruct((dm.T, dm.DV), BF16),
                   jax.ShapeDtypeStruct((dm.Bl, HEADS, 2, nc, hv, hk), BF16),
                   jax.ShapeDtypeStruct((2, dm.T, dm.DK), F32), jax.ShapeDtypeStruct((2, dm.T, dm.DK), F32)],
        scratch_shapes=[pltpu.VMEM((lp, hv), F32), pltpu.VMEM((lp, hv), F32)],
        compiler_params=_cp(2),
    )(proj_b, lr, wg_f, bg_f, wg_b, bg_b, gla_g)


def _gla_bwd(proj_b, lr, o_all, dy_gla, states, decays, gate_slopes, wg_f, wg_b, gla_g, dm):
    lp, hk, hv, nc, c0, hw = dm.LP, dm.HK, dm.HV, dm.NC, dm.C0, dm.HW
    scale = hk ** -0.5
    gc = _group_chunks(dm)
    gr, ng = gc * CHUNK, (nc - c0) // gc

    def body(p_ref, lr_ref, o_ref, dy_ref, st_ref, b_ref, gs_ref, wf_ref, wb_ref, gg_ref,
             d_ref, dlr_ref, gwf_ref, gbf_ref, gwb_ref, gbb_ref, ggg_ref, do_s, dq_s, dk_s, dv_s, dlr_s):
        low_incl, up_strict = _group_masks(gr)
        h = pl.program_id(1)

        @pl.when(h == 0)
        def _():
            dlr_ref[...] = jnp.zeros_like(dlr_ref)

        if c0 > 0:
            zr = c0 * CHUNK
            d_ref[0:zr, :] = jnp.zeros((zr, hw), BF16)
        for acc in (dq_s, dk_s, dv_s, dlr_s):
            acc[...] = jnp.zeros_like(acc)

        def norm_bwd(i, ggg):
            r0 = pl.multiple_of((c0 + i * gc) * CHUNK, CHUNK)
            o = o_ref[pl.ds(r0, gr), :].astype(F32)
            dy = dy_ref[pl.ds(r0, gr), :].astype(F32)
            r = p_ref[pl.ds(r0, gr), 2 * hk + hv:].astype(F32)
            rstd = lax.rsqrt(jnp.mean(o * o, axis=-1, keepdims=True) + EPS)
            ohat = o * rstd
            sg = _sigmoid(r)
            d_on = dy * (r * sg)
            d_ref[pl.ds(r0, gr), 2 * hk + hv:] = (dy * ohat * gg_ref[...] * (sg * (1.0 + r * (1.0 - sg)))).astype(BF16)
            d_oh = d_on * gg_ref[...]
            do_s[pl.ds(r0, gr), :] = (rstd * (d_oh - ohat * jnp.mean(d_oh * ohat, axis=-1, keepdims=True))).astype(BF16)
            return ggg + jnp.sum(d_on * ohat, axis=0, keepdims=True)

        ggg = lax.fori_loop(0, ng, norm_bwd, jnp.zeros((1, hv), F32))

        @pl.when((pl.program_id(0) == 0) & (h == 0))
        def _():
            ggg_ref[...] = jnp.zeros_like(ggg_ref)

        ggg_ref[0:1, :] += ggg

        def load(gi):
            r0 = pl.multiple_of((c0 + gi * gc) * CHUNK, CHUNK)
            blk = p_ref[pl.ds(r0, gr), :]
            return r0, blk[:, :hk].astype(F32) * scale, blk[:, hk:2 * hk].astype(F32), blk[:, 2 * hk:2 * hk + hv]

        zero = jnp.zeros((hv, hk), F32)

        def grad(gi, carry, fwd):
            dst, gw, gb = carry
            w_ref, way = (wf_ref, 0) if fwd else (wb_ref, 1)
            mask = low_incl if fwd else up_strict
            r0, q, k, v = load(gi)
            b = b_ref[way, pl.ds(r0, gr), :]
            btot = _chunk_totals(b, fwd)
            eb, enb, edb, dec = jnp.exp(b), jnp.exp(-b), jnp.exp(btot - b), jnp.exp(btot)
            qi_f, ki_f, kd_f = q * eb, k * enb, k * edb
            qi, ki, kd = qi_f.astype(BF16), ki_f.astype(BF16), kd_f.astype(BF16)
            do = do_s[pl.ds(r0, gr), :]
            a = _dot_nt(qi, ki)
            da = _dot_nt(do, v)
            yield
            a = jnp.where(mask, a, 0.0).astype(BF16)
            da = jnp.where(mask, da, 0.0).astype(BF16)
            dv = _dot_tn(a, do)
            dqi = _dot(da, ki)
            dki = _dot_tn(da, qi)
            dv_c, dqi_c, dkd_c, extra_c = [None] * gc, [None] * gc, [None] * gc, [None] * gc
            chunk_rows = [slice(c * CHUNK, (c + 1) * CHUNK) for c in range(gc)]
            qdo = [_dot_tn(do[rows], qi[rows]) for rows in chunk_rows]
            for c in (reversed(range(gc)) if fwd else range(gc)):
                yield
                rows = chunk_rows[c]
                st = st_ref[0, 0, way, c0 + gi * gc + c]
                dsn_b = dst.astype(BF16)
                dec_c = dec[c * CHUNK:c * CHUNK + 1]
                dv_c[c] = dv[rows] + _dot_nt(kd[rows], dsn_b)
                dqi_c[c] = dqi[rows] + _dot(do[rows], st)
                dkd_c[c] = _dot(v[rows], dsn_b)
                ddec = jnp.sum(st.astype(F32) * dst, axis=0, keepdims=True)
                extra = jnp.sum(dkd_c[c] * kd_f[rows], axis=0, keepdims=True) + ddec * dec_c
                extra_c[c] = jnp.broadcast_to(extra, (CHUNK, hk))
                dst = dst * dec_c + qdo[c]
            yield
            dv, dqi = jnp.concatenate(dv_c, axis=0), jnp.concatenate(dqi_c, axis=0)
            dkd, extra = jnp.concatenate(dkd_c, axis=0), jnp.concatenate(extra_c, axis=0)
            dq_s[pl.ds(r0, gr), :] += dqi * eb * scale
            dk_s[pl.ds(r0, gr), :] += dki * enb + dkd * edb
            dv_s[pl.ds(r0, gr), :] += dv
            db = dqi * qi_f - dki * ki_f - dkd * kd_f
            dg = _chunk_cumsum(db, fwd) + extra
            yield
            dz = dg * gs_ref[way, pl.ds(r0, gr), :]
            dz_b = dz.astype(BF16)
            dlr_s[pl.ds(r0, gr), :] += _dot_nt(dz_b, w_ref[...])
            return dst, gw + _dot_tn(lr_ref[pl.ds(r0, gr), :], dz_b), gb + jnp.sum(dz, axis=0, keepdims=True)

        def grad_step(i, carry):
            return tuple(_interleave([grad(ng - 1 - i, carry[0], True), grad(i, carry[1], False)]))

        init = (zero, jnp.zeros((LR_LANES, hk), F32), jnp.zeros((1, hk), F32))
        (_, gw_f, gb_f), (_, gw_b, gb_b) = lax.fori_loop(0, ng, grad_step, (init, init))
        for gw_ref, gb_ref, gw, gb in ((gwf_ref, gbf_ref, gw_f, gb_f), (gwb_ref, gbb_ref, gw_b, gb_b)):
            gw_ref[0] = gw
            gb_ref[0] = jnp.zeros((8, hk), F32)
            gb_ref[0, 0:1, :] = gb

        def combine(i, carry):
            r0 = pl.multiple_of((c0 + i * gc) * CHUNK, CHUNK)
            d_ref[pl.ds(r0, gr), 0:hk] = dq_s[pl.ds(r0, gr), :].astype(BF16)
            d_ref[pl.ds(r0, gr), hk:2 * hk] = dk_s[pl.ds(r0, gr), :].astype(BF16)
            d_ref[pl.ds(r0, gr), 2 * hk:2 * hk + hv] = dv_s[pl.ds(r0, gr), :].astype(BF16)
            dlr_ref[pl.ds(r0, gr), :] += dlr_s[pl.ds(r0, gr), :]
            return carry

        lax.fori_loop(0, ng, combine, 0)

    head = lambda s, h: (s, h)
    wspec = pl.BlockSpec((LR_LANES, hk), lambda s, h: (0, h))
    gwspec = pl.BlockSpec((1, LR_LANES, hk), lambda s, h: (s, 0, h))
    gbspec = pl.BlockSpec((1, 8, hk), lambda s, h: (s, 0, h))
    gw_shape = jax.ShapeDtypeStruct((dm.Bl, LR_LANES, dm.DK), F32)
    gb_shape = jax.ShapeDtypeStruct((dm.Bl, 8, dm.DK), F32)
    both = pl.BlockSpec((2, lp, hk), lambda s, h: (0, s, h))
    return pl.pallas_call(
        body, name="gla_bwd", grid=(dm.Bl, HEADS),
        in_specs=[pl.BlockSpec((lp, hw), head), pl.BlockSpec((lp, LR_LANES), lambda s, h: (s, 0)),
                  pl.BlockSpec((lp, hv), head), pl.BlockSpec((lp, hv), head),
                  pl.BlockSpec((1, 1, 2, nc, hv, hk), lambda s, h: (s, h, 0, 0, 0, 0)), both, both,
                  wspec, wspec, pl.BlockSpec((1, hv), lambda s, h: (0, 0))],
        out_specs=[pl.BlockSpec((lp, hw), head), pl.BlockSpec((lp, LR_LANES), lambda s, h: (s, 0)),
                   gwspec, gbspec, gwspec, gbspec, pl.BlockSpec((8, hv), lambda s, h: (0, 0))],
        out_shape=[jax.ShapeDtypeStruct((dm.T, HEADS * hw), BF16), jax.ShapeDtypeStruct((dm.T, LR_LANES), F32),
                   gw_shape, gb_shape, gw_shape, gb_shape, jax.ShapeDtypeStruct((8, hv), F32)],
        scratch_shapes=[pltpu.VMEM((lp, hv), BF16), pltpu.VMEM((lp, hk), F32), pltpu.VMEM((lp, hk), F32),
                        pltpu.VMEM((lp, hv), F32), pltpu.VMEM((lp, LR_LANES), F32)],
        compiler_params=_cp(2),
    )(proj_b, lr, o_all, dy_gla, states, decays, gate_slopes, wg_f, wg_b, gla_g)


def _stream_tiles(n_tiles, loads, stores, compute):
    for cp in loads(0, 0):
        cp.start()

    def step(t, carry):
        slot = t % 2

        @pl.when(t + 1 < n_tiles)
        def _():
            for cp in loads(t + 1, 1 - slot):
                cp.start()

        for cp in loads(t, slot):
            cp.wait()

        @pl.when(t >= 2)
        def _():
            for cp in stores(t - 2, slot):
                cp.wait()

        compute(t, slot)
        for cp in stores(t, slot):
            cp.start()
        return carry

    lax.fori_loop(0, n_tiles, step, 0)
    for t in range(max(n_tiles - 2, 0), n_tiles):
        for cp in stores(t, t % 2):
            cp.wait()


def _token_tiles(dm, target_rows=512):
    rows = _pick(dm.S, target_rows, 16)
    per_seq = dm.S // rows
    return rows, dm.Bl * per_seq, lambda t: pl.multiple_of((t // per_seq) * dm.LP + dm.TM + (t % per_seq) * rows, 16)


def _head(y_conv, y_gla, proj_c, w_oc, w_og, w_out, x, target, g_post, dm):
    d, tm = dm.D, dm.TM
    rows, n_tiles, first_row = _token_tiles(dm, 256)
    n_out = 8

    def body(*refs):
        yc_hbm, yg_hbm, c_hbm, woc_ref, wog_ref, wo_ref, x_hbm, t_hbm, g_ref = refs[:9]
        outs, st_ref = refs[9:9 + n_out], refs[9 + n_out]
        ycbuf, ygbuf, cbuf, xbuf, tbuf = refs[10 + n_out:15 + n_out]
        obufs = refs[15 + n_out:15 + 2 * n_out]
        zbuf, zbuf2, sem_in, sem_out, sem_zero = refs[15 + 2 * n_out:]

        def loads(t, slot):
            padded = [(yc_hbm, ycbuf), (yg_hbm, ygbuf), (c_hbm, cbuf)]
            own = [(x_hbm, xbuf), (t_hbm, tbuf)]
            return ([pltpu.make_async_copy(h.at[pl.ds(first_row(t), rows), :], b.at[slot], sem_in.at[i, slot])
                     for i, (h, b) in enumerate(padded)] +
                    [pltpu.make_async_copy(h.at[pl.ds(t * rows, rows), :], b.at[slot], sem_in.at[3 + i, slot])
                     for i, (h, b) in enumerate(own)])

        def stores(t, slot):
            return [pltpu.make_async_copy(b.at[slot], h.at[pl.ds(first_row(t), rows), :], sem_out.at[i, slot])
                    for i, (h, b) in enumerate(zip(outs, obufs))]

        def compute(t, slot):
            mg_o, do_o, dy_o, dpc_o, dpg_o, dc_o, dyc_o, dyg_o = obufs
            pc = _dot(ycbuf[slot], woc_ref[...])
            pg = _dot(ygbuf[slot], wog_ref[...])
            sa = _sigmoid(cbuf[slot, :, :d].astype(F32))
            sb = _sigmoid(cbuf[slot, :, d:].astype(F32))
            merged = (sa * pc + sb * pg).astype(BF16)
            mg_o[slot] = merged
            out = _dot(merged, wo_ref[...])
            rstd = lax.rsqrt(jnp.mean(out * out, axis=-1, keepdims=True) + EPS)
            ohat = out * rstd
            err = xbuf[slot] + ohat * g_ref[...] - tbuf[slot]
            dy = err * (1.0 / d)
            d_oh = dy * g_ref[...]
            d_out = (rstd * (d_oh - ohat * jnp.mean(d_oh * ohat, axis=-1, keepdims=True))).astype(BF16)
            do_o[slot] = d_out
            dy_o[slot] = dy.astype(BF16)
            st_ref[0:1, :] += jnp.sum(dy * ohat, axis=0, keepdims=True)
            st_ref[1:2, :] += jnp.sum(err * err, axis=0, keepdims=True)
            dmg = _dot_nt(d_out, wo_ref[...])
            dpc = (dmg * sa).astype(BF16)
            dpg = (dmg * sb).astype(BF16)
            dpc_o[slot] = dpc
            dpg_o[slot] = dpg
            dc_o[slot, :, :d] = (dmg * pc * sa * (1.0 - sa)).astype(BF16)
            dc_o[slot, :, d:] = (dmg * pg * sb * (1.0 - sb)).astype(BF16)
            dyc_o[slot] = _dot_nt(dpc, woc_ref[...]).astype(BF16)
            dyg_o[slot] = _dot_nt(dpg, wog_ref[...]).astype(BF16)

        st_ref[...] = jnp.zeros_like(st_ref)
        zbuf[...] = jnp.zeros_like(zbuf)
        zbuf2[...] = jnp.zeros_like(zbuf2)
        zeros = [pltpu.make_async_copy(zbuf2 if out.shape[1] == 2 * d else zbuf, out.at[pl.ds(b * dm.LP, tm), :], sem_zero.at[i, b])
                 for i, out in enumerate(outs) for b in range(dm.Bl)]
        for cp in zeros:
            cp.start()
        _stream_tiles(n_tiles, loads, stores, compute)
        for cp in zeros:
            cp.wait()

    any_spec, vmem = pl.BlockSpec(memory_space=pl.ANY), pl.BlockSpec(memory_space=pltpu.VMEM)
    widths = [d, d, d, d, d, 2 * d, d, d]
    tile = lambda w, dt: pltpu.VMEM((2, rows, w), dt)
    return pl.pallas_call(
        body, name="head", in_specs=[any_spec] * 3 + [vmem] * 3 + [any_spec] * 2 + [vmem],
        out_specs=[any_spec] * n_out + [vmem],
        out_shape=[jax.ShapeDtypeStruct((dm.T, w), BF16) for w in widths] + [jax.ShapeDtypeStruct((8, d), F32)],
        scratch_shapes=[tile(d, BF16), tile(d, BF16), tile(2 * d, BF16), tile(d, F32), tile(d, F32)]
        + [tile(w, BF16) for w in widths]
        + [pltpu.VMEM((tm, d), BF16), pltpu.VMEM((tm, 2 * d), BF16), pltpu.SemaphoreType.DMA((5, 2)),
           pltpu.SemaphoreType.DMA((n_out, 2)), pltpu.SemaphoreType.DMA((n_out, dm.Bl))],
        compiler_params=pltpu.CompilerParams(vmem_limit_bytes=VMEM_LIMIT_BYTES),
    )(y_conv, y_gla, proj_c, w_oc, w_og, w_out, x.reshape(dm.Bl * dm.S, d), target.reshape(dm.Bl * dm.S, d), g_post)


def _grad_h(d_parts, gathered, dy, x, metapad, g_pre, dm):
    d, tm = dm.D, dm.TM
    rows, n_tiles, first_row = _token_tiles(dm, 256)
    widths = [a.shape[1] for a in d_parts]
    np_ = len(d_parts)

    def body(*refs):
        d_hbm, g_hbm, dy_hbm, x_hbm, mp_ref, g_ref = refs[:np_], refs[np_], refs[np_ + 1], refs[np_ + 2], refs[np_ + 3], refs[np_ + 4]
        gx_hbm, dmeta_ref, gg_ref = refs[np_ + 5:np_ + 8]
        parts, edges, sems = refs[np_ + 8:np_ + 12], refs[np_ + 12], refs[np_ + 13]
        dbufs = refs[np_ + 14:2 * np_ + 14]
        dybuf, xbuf, gbuf = refs[2 * np_ + 14:2 * np_ + 17]
        mbufs = refs[2 * np_ + 17:3 * np_ + 17]
        sem_in, sem_out, sem_meta = refs[3 * np_ + 17:]

        def grad_u(tiles):
            du = _dot(tiles[0].astype(BF16), parts[0][...])
            for a, w in zip(tiles[1:], parts[1:]):
                du = du + _dot(a.astype(BF16), w[...])
            return du

        def norm_bwd(h, du, dy):
            rstd = lax.rsqrt(jnp.mean(h * h, axis=-1, keepdims=True) + EPS)
            hhat = h * rstd
            dug = du * g_ref[...]
            gg_ref[0:1, :] += jnp.sum(du * hhat, axis=0, keepdims=True)
            return dy + rstd * (dug - hhat * jnp.mean(dug * hhat, axis=-1, keepdims=True))

        def loads(t, slot):
            padded = list(zip(d_hbm, dbufs)) + [(dy_hbm, dybuf)]
            return ([pltpu.make_async_copy(h.at[pl.ds(first_row(t), rows), :], b.at[slot], sem_in.at[i, slot])
                     for i, (h, b) in enumerate(padded)] +
                    [pltpu.make_async_copy(x_hbm.at[pl.ds(t * rows, rows), :], xbuf.at[slot], sem_in.at[np_ + 1, slot])])

        def stores(t, slot):
            return [pltpu.make_async_copy(gbuf.at[slot], gx_hbm.at[pl.ds(t * rows, rows), :], sem_out.at[slot])]

        def compute(t, slot):
            gbuf[slot] = norm_bwd(xbuf[slot], grad_u([b[slot] for b in dbufs]), dybuf[slot].astype(F32))

        gg_ref[...] = jnp.zeros_like(gg_ref)
        meta = [pltpu.make_async_copy(h.at[pl.ds(b * dm.LP, tm), :], buf.at[pl.ds(b * tm, tm), :], sem_meta.at[i, b])
                for i, (h, buf) in enumerate(zip(d_hbm, mbufs)) for b in range(dm.Bl)]
        for cp in meta:
            cp.start()
        _load_packed(g_hbm, parts, edges, sems, dm)
        _stream_tiles(n_tiles, loads, stores, compute)
        for cp in meta:
            cp.wait()
        dmeta_ref[...] = norm_bwd(jnp.concatenate([mp_ref[...]] * dm.Bl, axis=0), grad_u([buf[...] for buf in mbufs]), 0.0)

    any_spec, vmem = pl.BlockSpec(memory_space=pl.ANY), pl.BlockSpec(memory_space=pltpu.VMEM)
    grad_x, d_meta, gg = pl.pallas_call(
        body, name="grad_h", in_specs=[any_spec] * (np_ + 3) + [vmem, vmem], out_specs=[any_spec, vmem, vmem],
        out_shape=[jax.ShapeDtypeStruct((dm.Bl * dm.S, d), F32), jax.ShapeDtypeStruct((dm.Bl * tm, d), F32),
                   jax.ShapeDtypeStruct((8, d), F32)],
        scratch_shapes=_packed_scratch(dm)
        + [pltpu.VMEM((2, rows, w), a.dtype) for w, a in zip(widths, d_parts)]
        + [pltpu.VMEM((2, rows, d), BF16), pltpu.VMEM((2, rows, d), F32), pltpu.VMEM((2, rows, d), F32)]
        + [pltpu.VMEM((dm.Bl * tm, w), a.dtype) for w, a in zip(widths, d_parts)]
        + [pltpu.SemaphoreType.DMA((np_ + 2, 2)), pltpu.SemaphoreType.DMA((2,)), pltpu.SemaphoreType.DMA((np_, dm.Bl))],
        compiler_params=pltpu.CompilerParams(vmem_limit_bytes=VMEM_LIMIT_BYTES),
    )(*d_parts, gathered, dy, x.reshape(dm.Bl * dm.S, d), metapad, g_pre)
    return grad_x.reshape(dm.Bl, dm.S, d), d_meta.reshape(dm.Bl, tm, d), gg


def _adamw(partials, w, m, v, name, by_columns=False):
    r, c = w.shape
    n_parts = partials.shape[0]
    tr, tc = (r, _pick(c, 128, 128)) if by_columns else (_pick(r, 256, 16), c)

    def body(p_ref, w_ref, m_ref, v_ref, g_ref, d_ref, nm_ref, nv_ref):
        g = p_ref[0].astype(F32)
        for j in range(1, n_parts):
            g = g + p_ref[j].astype(F32)
        g_ref[...] = g
        d_ref[...], nm_ref[...], nv_ref[...] = _adam_step(g, w_ref[...], m_ref[...], v_ref[...])

    at = (lambda i: (0, i)) if by_columns else (lambda i: (i, 0))
    tile = pl.BlockSpec((tr, tc), at)
    out = jax.ShapeDtypeStruct((r, c), F32)
    return pl.pallas_call(
        body, name=name, grid=(c // tc if by_columns else r // tr,),
        in_specs=[pl.BlockSpec((n_parts, tr, tc), lambda i: (0,) + at(i)), tile, tile, tile],
        out_specs=[tile, tile, tile, tile], out_shape=[out, out, out, out], compiler_params=_cp(1),
    )(partials, w, m, v)


def _adam_step(g, w, m, v):
    m2 = ADAM_B1 * m + (1.0 - ADAM_B1) * g
    v2 = ADAM_B2 * v + (1.0 - ADAM_B2) * (g * g)
    m_hat = m2 / (1.0 - ADAM_B1 ** ADAM_STEP)
    v_hat = v2 / (1.0 - ADAM_B2 ** ADAM_STEP)
    return -ADAM_LR * (m_hat / (jnp.sqrt(v_hat) + ADAM_EPS) + ADAM_WD * w), m2, v2


def _adamw_small(items, name):
    n = len(items)

    def body(*refs):
        ins, outs = refs[:4 * n], refs[4 * n:]
        for i in range(n):
            p_ref, w_ref, m_ref, v_ref = ins[4 * i:4 * i + 4]
            g = p_ref[0]
            for j in range(1, p_ref.shape[0]):
                g = g + p_ref[j]
            delta, m2, v2 = _adam_step(g, w_ref[...], m_ref[...], v_ref[...])
            for o_ref, val in zip(outs[4 * i:4 * i + 4], (g, delta, m2, v2)):
                o_ref[...] = val

    vmem = pl.BlockSpec(memory_space=pltpu.VMEM)
    res = pl.pallas_call(
        body, name=name, in_specs=[vmem] * (4 * n), out_specs=[vmem] * (4 * n),
        out_shape=[jax.ShapeDtypeStruct(w.shape, F32) for _, w, _, _ in items for _ in range(4)],
    )(*[a for item in items for a in item])
    return [res[4 * i:4 * i + 4] for i in range(n)]


def _unpack_rows(a, b, c, lr, dm):
    d, hk, hv, cw, nj, hw = dm.D, dm.HK, dm.HV, dm.CW, dm.NJ, dm.HW
    conv = a.reshape(nj, 4, cw, d).transpose(1, 0, 2, 3).reshape(4 * d, d)
    heads = b.reshape(HEADS, hw, d)
    q = heads[:, :hk].reshape(HEADS * hk, d)
    k = heads[:, hk:2 * hk].reshape(HEADS * hk, d)
    v = heads[:, 2 * hk:2 * hk + hv].reshape(HEADS * hv, d)
    r = heads[:, 2 * hk + hv:].reshape(HEADS * hv, d)
    return jnp.concatenate([conv, q, k, v, r, lr[:2 * RANK], c], axis=0)


def _column_shards(g, shard_shape):
    r, c = g.shape
    return g.reshape(r, N_DEV, c // N_DEV).transpose(1, 0, 2).reshape((N_DEV,) + tuple(shard_shape))


def _join_column_shards(parts):
    r, c = parts.shape[-2:]
    return parts.reshape(N_DEV, r, c).transpose(1, 0, 2).reshape(r, N_DEV * c)


def _local_step(x, target, meta, g_pre, u, wt_shards, conv_w, wg_f, bg_f, wg_b, bg_b, gla_g, out_weights, g_post,
                on_matrix_grads=None):
    bl, s, d = x.shape
    dm = _Dims(bl, s, d)
    metapad = jnp.concatenate([jnp.zeros((dm.TM - N_META, d), F32), meta], axis=0)
    wgp_f = jnp.pad(wg_f, ((0, LR_LANES - RANK), (0, 0))).astype(BF16)
    wgp_b = jnp.pad(wg_b, ((RANK, LR_LANES - 2 * RANK), (0, 0))).astype(BF16)

    u = _prenorm_meta(u, metapad, g_pre, dm)
    proj_a, proj_b, proj_c, lr = _inproj(u, wt_shards, dm)
    y_conv = _conv_fwd(proj_a, conv_w, dm)
    o_all, y_gla, states, decays, gate_slopes = _gla_fwd(proj_b, lr, wgp_f, bg_f, wgp_b, bg_b, gla_g, dm)
    w_oc, w_og, w_out = out_weights(y_conv) if callable(out_weights) else out_weights
    merged, d_out, dy, d_pc, d_pg, d_c, dy_conv, dy_gla, stats = _head(y_conv, y_gla, proj_c, w_oc, w_og, w_out, x, target,
                                                                        g_post, dm)

    g_out = _matmul_tn(merged, d_out, BF16, "grad_w_out")
    g_oc = _matmul_tn(y_conv, d_pc, BF16, "grad_w_out_conv")
    g_og = _matmul_tn(y_gla, d_pg, BF16, "grad_w_out_gla")
    if on_matrix_grads is not None:
        conv_w = conv_w + on_matrix_grads(dict(w_out_conv=g_oc, w_out_gla=g_og, w_merge_out=g_out))
    d_a, g_conv = _conv_bwd(proj_a, dy_conv, conv_w, dm)
    d_b, d_lr, gwp_f, gbp_f, gwp_b, gbp_b, g_gla = _gla_bwd(proj_b, lr, o_all, dy_gla, states, decays, gate_slopes, wgp_f, wgp_b, gla_g, dm)
    g_abc = _matmul_tn_group([d_a, d_b, d_c], u, "grad_w_in", tile=d).reshape(9 * d, d)
    g_in = _unpack_rows(g_abc[:4 * d], g_abc[4 * d:7 * d], g_abc[7 * d:], _matmul_tn(d_lr, u, BF16, "grad_w_in_gate"), dm)
    if on_matrix_grads is not None:
        d_lr = d_lr + on_matrix_grads(dict(w_in=g_in))
    grad_x, d_meta, g_pre_rows = _grad_h([d_a, d_b, d_c, d_lr], wt_shards, dy, x, metapad, g_pre, dm)

    grads = dict(
        meta_tokens=jnp.sum(d_meta[:, dm.TM - N_META:, :], axis=0), norm_pre=g_pre_rows[0:1], w_in=g_in,
        conv_w=g_conv[0:3], w_gate_fwd=jnp.sum(gwp_f, axis=0)[:RANK], b_gate_fwd=jnp.sum(gbp_f, axis=0)[0:1],
        w_gate_bwd=jnp.sum(gwp_b, axis=0)[RANK:2 * RANK], b_gate_bwd=jnp.sum(gbp_b, axis=0)[0:1],
        gla_norm=g_gla[0:1], w_out_conv=g_oc, w_out_gla=g_og, w_merge_out=g_out, norm_post=stats[0:1])
    return stats[1:2], grad_x, grads


MATRICES = ("w_out_conv", "w_out_gla", "w_merge_out")
SMALL_SHARDED = ("meta_tokens", "conv_w", "w_gate_fwd", "w_gate_bwd")
REPLICATED = ("norm_pre", "b_gate_fwd", "b_gate_bwd", "gla_norm", "norm_post")
NAMES = ("meta_tokens", "norm_pre", "w_in", "conv_w", "w_gate_fwd", "b_gate_fwd", "w_gate_bwd", "b_gate_bwd", "gla_norm",
         "w_out_conv", "w_out_gla", "w_merge_out", "norm_post")


def kernel(x, meta_tokens, norm_pre, w_in, conv_w, w_gate_fwd, b_gate_fwd, w_gate_bwd, b_gate_bwd, gla_norm, w_out_conv, w_out_gla, w_merge_out, norm_post, loss_target, m_meta_tokens, m_norm_pre, m_w_in, m_conv_w, m_w_gate_fwd, m_b_gate_fwd, m_w_gate_bwd, m_b_gate_bwd, m_gla_norm, m_w_out_conv, m_w_out_gla, m_w_merge_out, m_norm_post, v_meta_tokens, v_norm_pre, v_w_in, v_conv_w, v_w_gate_fwd, v_b_gate_fwd, v_w_gate_bwd, v_b_gate_bwd, v_gla_norm, v_w_out_conv, v_w_out_gla, v_w_merge_out, v_norm_post):
    w = dict(meta_tokens=meta_tokens, norm_pre=norm_pre, w_in=w_in[0], conv_w=conv_w, w_gate_fwd=w_gate_fwd,
             b_gate_fwd=b_gate_fwd, w_gate_bwd=w_gate_bwd, b_gate_bwd=b_gate_bwd, gla_norm=gla_norm,
             w_out_conv=w_out_conv[0], w_out_gla=w_out_gla[0], w_merge_out=w_merge_out[0], norm_post=norm_post)
    m = dict(meta_tokens=m_meta_tokens, norm_pre=m_norm_pre, w_in=m_w_in[0], conv_w=m_conv_w, w_gate_fwd=m_w_gate_fwd,
             b_gate_fwd=m_b_gate_fwd, w_gate_bwd=m_w_gate_bwd, b_gate_bwd=m_b_gate_bwd, gla_norm=m_gla_norm,
             w_out_conv=m_w_out_conv[0], w_out_gla=m_w_out_gla[0], w_merge_out=m_w_merge_out[0], norm_post=m_norm_post)
    v = dict(meta_tokens=v_meta_tokens, norm_pre=v_norm_pre, w_in=v_w_in[0], conv_w=v_conv_w, w_gate_fwd=v_w_gate_fwd,
             b_gate_fwd=v_b_gate_fwd, w_gate_bwd=v_w_gate_bwd, b_gate_bwd=v_b_gate_bwd, gla_norm=v_gla_norm,
             w_out_conv=v_w_out_conv[0], w_out_gla=v_w_out_gla[0], w_merge_out=v_w_merge_out[0], norm_post=v_norm_post)
    d = x.shape[-1]

    dm = _Dims(*x.shape)
    me = 4 * lax.axis_index("x") + 2 * lax.axis_index("y") + lax.axis_index("c")
    wt_shards, *small_all, u = _gather_two_level(
        [_pad_shard(w["w_in"].T.astype(BF16), me)] + [w[n] for n in SMALL_SHARDED], "gather_weights",
        _prenorm_tokens_side(x, norm_pre, dm))
    _, late_weights = _exchange_start([w[n].astype(BF16) for n in MATRICES], [], small_all[0], "gather_out_weights_start")
    small = {n: _join_column_shards(p) for n, p in zip(SMALL_SHARDED, small_all)}

    def out_weights(after):
        return tuple(a.reshape(-1, d) for a in _exchange_wait(late_weights, after, "gather_out_weights_wait"))

    pending = []

    def on_matrix_grads(g):
        token, state = _exchange_start([], [t.reshape(N_DEV, -1, d) for t in g.values()], None,
                                       "exchange_grads_start_" + "_".join(g))
        pending.append((tuple(g), state))
        return token

    sq_err_cols, grad_x, grads = _local_step(
        x, loss_target, small["meta_tokens"], norm_pre, u, wt_shards, small["conv_w"], small["w_gate_fwd"], b_gate_fwd,
        small["w_gate_bwd"], b_gate_bwd, gla_norm, out_weights, norm_post, on_matrix_grads)
    received = {}
    for names, state in pending:
        received.update(zip(names, _exchange_wait(state, grad_x, "exchange_grads_wait_" + "_".join(names))))

    exchanged = _exchange([grads[n] for n in REPLICATED] + [sq_err_cols],
                          [_column_shards(grads[n], w[n].shape) for n in SMALL_SHARDED], "exchange_small_grads")
    small_recv = exchanged[:len(REPLICATED)] + exchanged[len(REPLICATED) + 1:]
    loss = 0.5 / d * jnp.sum(exchanged[len(REPLICATED)])

    results = {"w_in": [r.T[None] for r in _adamw(received["w_in"], w["w_in"].T, m["w_in"].T, v["w_in"].T, "adamw_w_in", by_columns=True)]}
    for n in MATRICES:
        results[n] = [r[None] for r in _adamw(received[n], w[n], m[n], v[n], "adamw_" + n)]
    small_names = REPLICATED + SMALL_SHARDED
    results.update(zip(small_names, _adamw_small([(p, w[n], m[n], v[n]) for n, p in zip(small_names, small_recv)], "adamw_small")))
    return (loss, grad_x, *[results[n][i] for i in range(4) for n in NAMES])
```

```python
import jax
import jax.numpy as jnp
from jax import lax
from jax.experimental import pallas as pl
from jax.experimental.pallas import tpu as pltpu

F32 = jnp.float32
BF16 = jnp.bfloat16
MESH = pl.DeviceIdType.MESH

N_META = 16
CHUNK = 64
CHUNK_SHIFT = 6
HEADS = 4
RANK = 16
LR_LANES = 128
PAD_ROWS = CHUNK - N_META
EPS = 1e-6
GATE_NORMALIZER = 16.0
N_DEV = 8
ADAM_LR, ADAM_B1, ADAM_B2, ADAM_EPS, ADAM_WD, ADAM_STEP = 0.001, 0.9, 0.999, 1e-08, 0.01, 10
VMEM_LIMIT_BYTES = 56 * 1024 * 1024


class _Dims:
    def __init__(self, bl, s, d):
        self.Bl, self.S, self.D = bl, s, d
        self.TM = CHUNK
        self.LP = self.TM + s
        self.T = bl * self.LP
        self.TPS = self.LP // self.TM
        self.NC = self.LP // CHUNK
        self.C0 = (self.TM - CHUNK) // CHUNK
        self.DK, self.DV = d // 2, d
        self.HK, self.HV = self.DK // HEADS, self.DV // HEADS
        self.HW = 2 * self.HK + 2 * self.HV
        self.CW = 256 if d % 256 == 0 and d > 256 else d // 4
        self.NJ = d // self.CW


def _pick(n, target, mult):
    t = min(n, target)
    while t >= mult:
        if n % t == 0 and t % mult == 0:
            return t
        t -= mult
    return n


def _cp(n_axes):
    return pltpu.CompilerParams(dimension_semantics=("arbitrary",) * n_axes, vmem_limit_bytes=VMEM_LIMIT_BYTES)


def _sigmoid(x):
    return 1.0 / (1.0 + jnp.exp(-x))


def _dot(a, b):
    return jnp.dot(a, b, preferred_element_type=F32)


def _dot_nt(a, b):
    return lax.dot_general(a, b, (((1,), (1,)), ((), ())), preferred_element_type=F32)


def _dot_tn(a, b):
    return lax.dot_general(a, b, (((0,), (0,)), ((), ())), preferred_element_type=F32)


def _chunk_cumsum(x, reverse):
    rows = x.shape[0]
    r = lax.broadcasted_iota(jnp.int32, x.shape, 0) & (CHUNK - 1)
    step = 1
    while step < CHUNK:
        if reverse:
            x = x + jnp.where(r < CHUNK - step, pltpu.roll(x, rows - step, 0), 0.0)
        else:
            x = x + jnp.where(r >= step, pltpu.roll(x, step, 0), 0.0)
        step *= 2
    return x


def _exchange(gathers, scatters, name):
    arrays = list(gathers) + list(scatters)
    n, ng = len(arrays), len(gathers)

    def body(*refs):
        ins, outs = refs[:n], refs[n:2 * n]
        send_sems, recv_sems, local_sems = refs[2 * n:]
        x, y, c = lax.axis_index("x"), lax.axis_index("y"), lax.axis_index("c")
        me = 4 * x + 2 * y + c
        started = []
        for t in range(n):
            src, dst = ins[t], outs[t]
            own = pltpu.make_async_copy(src if t < ng else src.at[me], dst.at[me], local_sems.at[t])
            own.start()
            started.append(own)
            for k, pos, peer in _peers(x, y, c):
                cp = pltpu.make_async_remote_copy(
                    src_ref=src if t < ng else src.at[peer], dst_ref=dst.at[me],
                    send_sem=send_sems.at[t * (N_DEV - 1) + k - 1], recv_sem=recv_sems.at[t * (N_DEV - 1) + k - 1],
                    device_id=pos, device_id_type=MESH)
                cp.start()
                started.append(cp)
        for cp in started:
            cp.wait()

    out_shape = [jax.ShapeDtypeStruct((N_DEV,) + a.shape if t < ng else a.shape, a.dtype) for t, a in enumerate(arrays)]
    any_spec = pl.BlockSpec(memory_space=pl.ANY)
    return pl.pallas_call(
        body, name=name, out_shape=out_shape, in_specs=[any_spec] * n, out_specs=[any_spec] * n,
        scratch_shapes=[pltpu.SemaphoreType.DMA((n * (N_DEV - 1),)), pltpu.SemaphoreType.DMA((n * (N_DEV - 1),)),
                        pltpu.SemaphoreType.DMA((n,))],
        compiler_params=pltpu.CompilerParams(has_side_effects=True),
    )(*arrays)


def _gather_two_level(arrays, name, side=None):
    n = len(arrays)
    per = N_DEV - 1
    work, side_in, side_in_specs, side_out, side_out_specs, side_scratch = side or (None, [], [], [], [], [])
    n_in, n_out = len(side_in), len(side_out)

    def body(*refs):
        ins, outs = refs[:n], refs[n + n_in:2 * n + n_in]
        send_sems, recv_sems, local_sems = refs[2 * n + n_in + n_out:2 * n + n_in + n_out + 3]
        x, y, c = lax.axis_index("x"), lax.axis_index("y"), lax.axis_index("c")
        sibling = (x, y, 1 - c)
        chips = [(1 - x, y), (x, 1 - y), (1 - x, 1 - y)]
        index = lambda px, py, pc: 4 * px + 2 * py + pc

        def copy(t, k, block, to, from_input=False):
            slab = outs[t].at[index(*block)]
            return pltpu.make_async_remote_copy(
                src_ref=ins[t] if from_input else slab, dst_ref=slab, send_sem=send_sems.at[t * per + k],
                recv_sem=recv_sems.at[t * per + k], device_id=to, device_id_type=MESH)

        own, sent = [], []
        for t in range(n):
            own.append(pltpu.make_async_copy(ins[t], outs[t].at[index(x, y, c)], local_sems.at[t]))
            own[-1].start()
            first = [copy(t, 0, (x, y, c), sibling, True)]
            first += [copy(t, 1 + j, (x, y, c), (*chip, c), True) for j, chip in enumerate(chips)]
            for cp in first:
                cp.start()
            sent += first
        if work is not None:
            work(refs[n:n + n_in], refs[2 * n + n_in:2 * n + n_in + n_out], refs[2 * n + n_in + n_out + 3:])
        for t in range(n):
            for j, chip in enumerate(chips):
                copy(t, 1 + j, (*chip, c), (x, y, c)).wait_recv()
                sent.append(copy(t, 4 + j, (*chip, c), sibling))
                sent[-1].start()
        for t in range(n):
            copy(t, 0, sibling, (x, y, c)).wait_recv()
            for j, chip in enumerate(chips):
                copy(t, 4 + j, (*chip, 1 - c), (x, y, c)).wait_recv()
        for cp in sent:
            cp.wait_send()
        for cp in own:
            cp.wait()

    out_shape = [jax.ShapeDtypeStruct((N_DEV,) + a.shape, a.dtype) for a in arrays]
    any_spec = pl.BlockSpec(memory_space=pl.ANY)
    return pl.pallas_call(
        body, name=name, out_shape=out_shape + list(side_out), in_specs=[any_spec] * n + list(side_in_specs),
        out_specs=[any_spec] * n + list(side_out_specs),
        scratch_shapes=[pltpu.SemaphoreType.DMA((n * per,)), pltpu.SemaphoreType.DMA((n * per,)),
                        pltpu.SemaphoreType.DMA((n,))] + list(side_scratch),
        compiler_params=pltpu.CompilerParams(has_side_effects=True, vmem_limit_bytes=VMEM_LIMIT_BYTES),
    )(*arrays, *side_in)


def _peers(x, y, c):
    out = []
    for k in range(1, N_DEV):
        px = 1 - x if (k >> 2) & 1 else x
        py = 1 - y if (k >> 1) & 1 else y
        pc = 1 - c if k & 1 else c
        out.append((k, (px, py, pc), 4 * px + 2 * py + pc))
    return out


def _exchange_start(gathers, scatters, after, name):
    arrays = list(gathers) + list(scatters)
    n, ng = len(arrays), len(gathers)
    hbm = pl.BlockSpec(memory_space=pltpu.HBM)
    sem = pl.BlockSpec(memory_space=pltpu.SEMAPHORE)

    extra = [] if after is None else [after]
    ne = len(extra)

    def body(*refs):
        ins, lands = refs[:n], refs[n:2 * n]
        send_sems, recv_sems = refs[2 * n + ne], refs[2 * n + ne + 1]
        token = refs[4 * n + ne + 2]
        x, y, c = lax.axis_index("x"), lax.axis_index("y"), lax.axis_index("c")
        me = 4 * x + 2 * y + c
        for t in range(n):
            for k, pos, peer in _peers(x, y, c):
                pltpu.make_async_remote_copy(
                    src_ref=ins[t] if t < ng else ins[t].at[peer], dst_ref=lands[t].at[me],
                    send_sem=send_sems.at[t * (N_DEV - 1) + k - 1], recv_sem=recv_sems.at[t * (N_DEV - 1) + k - 1],
                    device_id=pos, device_id_type=MESH).start()
        token[...] = jnp.zeros_like(token)

    me = 4 * lax.axis_index("x") + 2 * lax.axis_index("y") + lax.axis_index("c")
    lands = [lax.dynamic_update_index_in_dim(lax.empty((N_DEV,) + a.shape if t < ng else a.shape, a.dtype),
                                             a if t < ng else lax.dynamic_index_in_dim(a, me, 0, keepdims=False), me, 0)
             for t, a in enumerate(arrays)]
    operands = [pltpu.with_memory_space_constraint(a, pltpu.HBM) for a in arrays + lands]
    sems = pltpu.SemaphoreType.DMA((n * (N_DEV - 1),))
    res = pl.pallas_call(
        body, name=name,
        out_shape=(sems, sems, *[pltpu.HBM(a.shape, a.dtype) for a in arrays + lands], jax.ShapeDtypeStruct((8, 128), F32)),
        in_specs=[hbm] * (2 * n) + [pl.BlockSpec(memory_space=pl.ANY)] * ne,
        out_specs=(sem, sem, *[hbm] * (2 * n), pl.BlockSpec(memory_space=pltpu.VMEM)),
        input_output_aliases={i: 2 + i for i in range(2 * n)},
        compiler_params=pltpu.CompilerParams(has_side_effects=pltpu.SideEffectType.DATAFLOW_SIDE_EFFECTING),
    )(*operands, *extra)
    return res[-1][0, 0], (ng, res[0], res[1], list(res[2:2 + n]), list(res[2 + n:2 + 2 * n]))


def _exchange_wait(state, after, name):
    ng, send_sems, recv_sems, sent, lands = state
    n = len(sent)
    hbm = pl.BlockSpec(memory_space=pltpu.HBM)
    sem = pl.BlockSpec(memory_space=pltpu.SEMAPHORE)

    def body(*refs):
        ins, land_refs = refs[:n], refs[n:2 * n]
        send_ref, recv_ref = refs[2 * n], refs[2 * n + 1]
        x, y, c = lax.axis_index("x"), lax.axis_index("y"), lax.axis_index("c")
        me = 4 * x + 2 * y + c
        for t in range(n):
            for k, pos, peer in _peers(x, y, c):
                cp = pltpu.make_async_remote_copy(
                    src_ref=ins[t] if t < ng else ins[t].at[peer], dst_ref=land_refs[t].at[me],
                    send_sem=send_ref.at[t * (N_DEV - 1) + k - 1], recv_sem=recv_ref.at[t * (N_DEV - 1) + k - 1],
                    device_id=pos, device_id_type=MESH)
                cp.wait_send()
                cp.wait_recv()

    res = pl.pallas_call(
        body, name=name, out_shape=tuple(pltpu.HBM(a.shape, a.dtype) for a in sent + lands),
        in_specs=[hbm] * (2 * n) + [sem, sem, pl.BlockSpec(memory_space=pl.ANY)], out_specs=tuple([hbm] * (2 * n)),
        input_output_aliases={i: i for i in range(2 * n)},
        compiler_params=pltpu.CompilerParams(has_side_effects=pltpu.SideEffectType.DATAFLOW_SIDE_EFFECTING),
    )(*sent, *lands, send_sems, recv_sems, after)
    return list(res[n:])


def _rms_scaled(h, g):
    return (h * lax.rsqrt(jnp.mean(h * h, axis=-1, keepdims=True) + EPS) * g).astype(BF16)


def _prenorm_tokens_side(x, g_pre, dm):
    bl, s, d = x.shape
    rows = _pick(s, 512, 16)
    tiles = [(b, j) for b in range(bl) for j in range(s // rows)]

    def work(ins, outs, scratch):
        (x_ref, g_ref), (u_ref,), (xbuf, ubuf, sem_in, sem_out) = ins, outs, scratch

        def load(t, slot):
            b, j = tiles[t]
            return pltpu.make_async_copy(x_ref.at[b, pl.ds(j * rows, rows), :], xbuf.at[slot], sem_in.at[slot])

        def store(t, slot):
            b, j = tiles[t]
            return pltpu.make_async_copy(ubuf.at[slot], u_ref.at[pl.ds(b * dm.LP + dm.TM + j * rows, rows), :], sem_out.at[slot])

        load(0, 0).start()
        for t in range(len(tiles)):
            slot = t % 2
            if t + 1 < len(tiles):
                load(t + 1, 1 - slot).start()
            load(t, slot).wait()
            if t >= 2:
                store(t - 2, slot).wait()
            ubuf[slot] = _rms_scaled(xbuf[slot], g_ref[...])
            store(t, slot).start()
        for t in range(max(len(tiles) - 2, 0), len(tiles)):
            store(t, t % 2).wait()

    any_spec = pl.BlockSpec(memory_space=pl.ANY)
    return (work, [x, g_pre], [any_spec, pl.BlockSpec(memory_space=pltpu.VMEM)],
            [jax.ShapeDtypeStruct((dm.T, d), BF16)], [any_spec],
            [pltpu.VMEM((2, rows, d), F32), pltpu.VMEM((2, rows, d), BF16), pltpu.SemaphoreType.DMA((2,)),
             pltpu.SemaphoreType.DMA((2,))])


def _prenorm_meta(u, metapad, g_pre, dm):
    tm, tps, d = dm.TM, dm.TPS, dm.D

    def body(u_in, mp_ref, g_ref, u_ref):
        u_ref[...] = _rms_scaled(mp_ref[...], g_ref[...])

    return pl.pallas_call(
        body, name="prenorm_meta", grid=(dm.Bl,),
        in_specs=[pl.BlockSpec(memory_space=pl.ANY), pl.BlockSpec((tm, d), lambda i: (0, 0)),
                  pl.BlockSpec((1, d), lambda i: (0, 0))],
        out_specs=pl.BlockSpec((tm, d), lambda i: (i * tps, 0)),
        out_shape=jax.ShapeDtypeStruct((dm.T, d), BF16), input_output_aliases={0: 0}, compiler_params=_cp(1),
    )(u, metapad, g_pre)


def _matmul_tn(a, b, out_dtype, name, tt=2304, tn=1024, tk=1024):
    t, k = a.shape
    n = b.shape[1]
    tt, tn, tk = _pick(t, tt, 16), _pick(n, tn, 128), _pick(k, tk, 128)
    nt = t // tt

    def body(a_ref, b_ref, o_ref, acc):
        p = _dot_tn(a_ref[...].astype(BF16), b_ref[...].astype(BF16))
        i = pl.program_id(2)

        @pl.when(i == 0)
        def _():
            acc[...] = p

        @pl.when(i > 0)
        def _():
            acc[...] += p

        @pl.when(i == nt - 1)
        def _():
            o_ref[...] = acc[...].astype(out_dtype)

    return pl.pallas_call(
        body, name=name, grid=(k // tk, n // tn, nt),
        in_specs=[pl.BlockSpec((tt, tk), lambda kk, j, i: (i, kk)), pl.BlockSpec((tt, tn), lambda kk, j, i: (i, j))],
        out_specs=pl.BlockSpec((tk, tn), lambda kk, j, i: (kk, j)),
        out_shape=jax.ShapeDtypeStruct((k, n), out_dtype), scratch_shapes=[pltpu.VMEM((tk, tn), F32)],
        compiler_params=_cp(3),
    )(a, b)


def _matmul_tn_group(a_list, b, name, tt=2304, tile=1024):
    t, n = b.shape
    tt = _pick(t, tt, 16)
    nt = t // tt
    counts = [a.shape[1] // tile for a in a_list]
    starts = [sum(counts[:m]) for m in range(len(a_list))]
    items = sum(counts)

    def active(p, m):
        return (p >= starts[m]) & (p < starts[m] + counts[m])

    def body(*refs):
        a_refs, b_ref = refs[:len(a_list)], refs[len(a_list)]
        o_ref, acc = refs[-2], refs[-1]
        p, i = pl.program_id(0), pl.program_id(1)
        for m, a_ref in enumerate(a_refs):
            @pl.when(active(p, m))
            def _(a_ref=a_ref):
                prod = _dot_tn(a_ref[...].astype(BF16), b_ref[...].astype(BF16))

                @pl.when(i == 0)
                def _():
                    acc[...] = prod

                @pl.when(i > 0)
                def _():
                    acc[...] += prod

        @pl.when(i == nt - 1)
        def _():
            o_ref[0] = acc[...].astype(BF16)

    a_specs = [pl.BlockSpec((tt, tile), lambda p, i, m=m: (jnp.where(active(p, m), i, 0), jnp.where(active(p, m), p - starts[m], 0)))
               for m in range(len(a_list))]
    return pl.pallas_call(
        body, name=name, grid=(items, nt), in_specs=a_specs + [pl.BlockSpec((tt, n), lambda p, i: (i, 0))],
        out_specs=pl.BlockSpec((1, tile, n), lambda p, i: (p, 0, 0)),
        out_shape=jax.ShapeDtypeStruct((items, tile, n), BF16), scratch_shapes=[pltpu.VMEM((tile, n), F32)],
        compiler_params=_cp(2),
    )(*a_list, b)


BF16_TILE_ROWS = 16


def _shard_offset(index, shard_rows):
    return (index * shard_rows) % BF16_TILE_ROWS


def _pad_shard(wt_shard, index):
    rows, d = wt_shard.shape
    padded = -(-(rows + max(_shard_offset(j, rows) for j in range(N_DEV))) // BF16_TILE_ROWS) * BF16_TILE_ROWS
    return lax.dynamic_update_slice(jnp.zeros((padded, d), wt_shard.dtype), wt_shard, (_shard_offset(index, rows), 0))


def _packed_parts(dm):
    d, dk, hk, hv, cw, nj, hw = dm.D, dm.DK, dm.HK, dm.HV, dm.CW, dm.NJ, dm.HW
    blocks = [(0, (j * 4 + p) * cw, p * d + j * cw, cw) for j in range(nj) for p in range(4)]
    for h in range(HEADS):
        blocks += [(1, h * hw, 4 * d + h * hk, hk), (1, h * hw + hk, 4 * d + dk + h * hk, hk),
                   (1, h * hw + 2 * hk, 5 * d + h * hv, hv), (1, h * hw + 2 * hk + hv, 6 * d + h * hv, hv)]
    blocks += [(2, 0, 7 * d + 2 * RANK, 2 * d), (3, 0, 7 * d, 2 * RANK)]
    return [4 * d, 3 * d, 2 * d, LR_LANES], blocks


def _pack_plan(dm):
    sh = (9 * dm.D + 2 * RANK) // N_DEV
    tile = BF16_TILE_ROWS
    copies, straddles = [], []
    for part, dst, r0, n in _packed_parts(dm)[1]:
        for j in range(N_DEV):
            a, b = max(r0, sh * j), min(r0 + n, sh * (j + 1))
            if a >= b:
                continue
            a_up, b_down = -(-a // tile) * tile, b // tile * tile
            if b_down > a_up:
                copies.append((j, a_up - sh * j + _shard_offset(j, sh), b_down - a_up, part, dst + a_up - r0))
            if a % tile:
                lo = a // tile * tile
                straddles.append((j, lo - sh * (j - 1) + _shard_offset(j - 1, sh), part, dst + lo - r0, a - lo))
    return copies, straddles


def _packed_scratch(dm):
    copies, straddles = _pack_plan(dm)
    return ([pltpu.VMEM((rows, dm.D), BF16) for rows in _packed_parts(dm)[0]]
            + [pltpu.VMEM((2 * max(len(straddles), 1), BF16_TILE_ROWS, dm.D), BF16),
               pltpu.SemaphoreType.DMA((len(copies) + 2 * len(straddles),))])


def _load_packed(g_ref, parts, edges, sems, dm):
    copies, straddles = _pack_plan(dm)
    tile = BF16_TILE_ROWS
    parts[3][2 * RANK:, :] = jnp.zeros((LR_LANES - 2 * RANK, dm.D), BF16)
    dmas = [pltpu.make_async_copy(g_ref.at[j, pl.ds(src, n), :], parts[p].at[pl.ds(dst, n), :], sems.at[i])
            for i, (j, src, n, p, dst) in enumerate(copies)]
    for i, (j, src, p, dst, split) in enumerate(straddles):
        k = len(copies) + 2 * i
        dmas.append(pltpu.make_async_copy(g_ref.at[j - 1, pl.ds(src, tile), :], edges.at[2 * i], sems.at[k]))
        dmas.append(pltpu.make_async_copy(g_ref.at[j, pl.ds(0, tile), :], edges.at[2 * i + 1], sems.at[k + 1]))
    for cp in dmas:
        cp.start()
    for cp in dmas:
        cp.wait()
    row = lax.broadcasted_iota(jnp.int32, (tile, dm.D), 0)
    for i, (j, src, p, dst, split) in enumerate(straddles):
        parts[p][dst:dst + tile, :] = jnp.where(row < split, edges[2 * i], edges[2 * i + 1])


def _inproj(u, gathered, dm):
    t, d = u.shape
    tm = _pick(t, 512, 16)
    widths = _packed_parts(dm)[0]
    cn = 1024

    def body(u_ref, g_ref, *rest):
        outs, parts, (edges, sems) = rest[:4], rest[4:8], rest[8:]

        @pl.when(pl.program_id(0) == 0)
        def _():
            _load_packed(g_ref, parts, edges, sems, dm)

        ut = u_ref[...]
        for w, o_ref in zip(parts, outs):
            n = w.shape[0]
            step = cn if n % cn == 0 else n
            for j in range(0, n, step):
                o_ref[:, j:j + step] = _dot_nt(ut, w[j:j + step, :]).astype(BF16)

    return pl.pallas_call(
        body, name="inproj", grid=(t // tm,),
        in_specs=[pl.BlockSpec((tm, d), lambda i: (i, 0)), pl.BlockSpec(memory_space=pl.ANY)],
        out_specs=[pl.BlockSpec((tm, w), lambda i: (i, 0)) for w in widths],
        out_shape=[jax.ShapeDtypeStruct((t, w), BF16) for w in widths],
        scratch_shapes=_packed_scratch(dm), compiler_params=_cp(1),
    )(u, gathered)


def _conv_rows(dm):
    return _pick(dm.LP, 256, 16)


def _shifted(m, prev_row, next_row, rows):
    row = lax.broadcasted_iota(jnp.int32, m.shape, 0)
    m_prev = jnp.where(row == 0, prev_row, pltpu.roll(m, 1, 0))
    m_next = jnp.where(row == rows - 1, next_row, pltpu.roll(m, rows - 1, 0))
    return m_prev, m_next


def _conv_fwd(proj_a, conv_w, dm):
    lp, cw, rc = dm.LP, dm.CW, _conv_rows(dm)
    nchunk = lp // rc

    def body(p_ref, w_ref, y_ref):
        w0, w1, w2 = w_ref[0:1, :], w_ref[1:2, :], w_ref[2:3, :]

        def chunk(ci, carry):
            r0 = pl.multiple_of(ci * rc, rc)
            blk = p_ref[pl.ds(r0, rc), :].astype(F32)
            cb, cc, cx, cz = (blk[:, i * cw:(i + 1) * cw] for i in range(4))
            m = cc * cx
            rp = pl.multiple_of(jnp.maximum(r0 - 16, 0), 16)
            rn = pl.multiple_of(jnp.minimum(r0 + rc, lp - 16), 16)
            pv = p_ref[pl.ds(rp, 16), cw:3 * cw].astype(F32)
            nx = p_ref[pl.ds(rn, 16), cw:3 * cw].astype(F32)
            prev_row = jnp.where(ci > 0, pv[15:16, :cw] * pv[15:16, cw:], 0.0)
            next_row = jnp.where(ci < nchunk - 1, nx[0:1, :cw] * nx[0:1, cw:], 0.0)
            m_prev, m_next = _shifted(m, prev_row, next_row, rc)
            s = w0 * m_prev + w1 * m + w2 * m_next
            y_ref[pl.ds(r0, rc), :] = (cb * s * (cz * _sigmoid(cz))).astype(BF16)
            return carry

        lax.fori_loop(0, nchunk, chunk, 0)

    return pl.pallas_call(
        body, name="conv_fwd", grid=(dm.Bl, dm.NJ),
        in_specs=[pl.BlockSpec((lp, 4 * cw), lambda s, j: (s, j)), pl.BlockSpec((3, cw), lambda s, j: (0, j))],
        out_specs=pl.BlockSpec((lp, cw), lambda s, j: (s, j)),
        out_shape=jax.ShapeDtypeStruct((dm.T, dm.D), BF16), compiler_params=_cp(2),
    )(proj_a, conv_w)


def _conv_bwd(proj_a, dy_conv, conv_w, dm):
    lp, cw, rc = dm.LP, dm.CW, _conv_rows(dm)
    nchunk = lp // rc

    def body(p_ref, dy_ref, w_ref, d_ref, gw_ref):
        w0, w1, w2 = w_ref[0:1, :], w_ref[1:2, :], w_ref[2:3, :]

        def ds_of(p4, dy):
            cb, cz = p4[:, :cw], p4[:, 3 * cw:]
            return dy * cb * (cz * _sigmoid(cz))

        def chunk(ci, carry):
            g0, g1, g2 = carry
            r0 = pl.multiple_of(ci * rc, rc)
            blk = p_ref[pl.ds(r0, rc), :].astype(F32)
            dy = dy_ref[pl.ds(r0, rc), :].astype(F32)
            cb, cc, cx, cz = (blk[:, i * cw:(i + 1) * cw] for i in range(4))
            rp = pl.multiple_of(jnp.maximum(r0 - 16, 0), 16)
            rn = pl.multiple_of(jnp.minimum(r0 + rc, lp - 16), 16)
            pv = p_ref[pl.ds(rp, 16), :].astype(F32)[15:16]
            nx = p_ref[pl.ds(rn, 16), :].astype(F32)[0:1]
            dpv = dy_ref[pl.ds(rp, 16), :].astype(F32)[15:16]
            dnx = dy_ref[pl.ds(rn, 16), :].astype(F32)[0:1]
            has_prev, has_next = ci > 0, ci < nchunk - 1
            m = cc * cx
            m_prev, m_next = _shifted(m, jnp.where(has_prev, pv[:, cw:2 * cw] * pv[:, 2 * cw:3 * cw], 0.0),
                                      jnp.where(has_next, nx[:, cw:2 * cw] * nx[:, 2 * cw:3 * cw], 0.0), rc)
            s = w0 * m_prev + w1 * m + w2 * m_next
            sg = _sigmoid(cz)
            silu = cz * sg
            ds = dy * cb * silu
            ds_prev, ds_next = _shifted(ds, jnp.where(has_prev, ds_of(pv, dpv), 0.0),
                                        jnp.where(has_next, ds_of(nx, dnx), 0.0), rc)
            dm_ = w0 * ds_next + w1 * ds + w2 * ds_prev
            d_ref[pl.ds(r0, rc), 0:cw] = (dy * s * silu).astype(BF16)
            d_ref[pl.ds(r0, rc), cw:2 * cw] = (dm_ * cx).astype(BF16)
            d_ref[pl.ds(r0, rc), 2 * cw:3 * cw] = (dm_ * cc).astype(BF16)
            d_ref[pl.ds(r0, rc), 3 * cw:4 * cw] = (dy * cb * s * (sg * (1.0 + cz * (1.0 - sg)))).astype(BF16)
            return (g0 + jnp.sum(ds * m_prev, axis=0, keepdims=True), g1 + jnp.sum(ds * m, axis=0, keepdims=True),
                    g2 + jnp.sum(ds * m_next, axis=0, keepdims=True))

        z = jnp.zeros((1, cw), F32)
        g0, g1, g2 = lax.fori_loop(0, nchunk, chunk, (z, z, z))

        @pl.when(pl.program_id(1) == 0)
        def _():
            gw_ref[...] = jnp.zeros_like(gw_ref)

        gw_ref[0:1, :] += g0
        gw_ref[1:2, :] += g1
        gw_ref[2:3, :] += g2

    return pl.pallas_call(
        body, name="conv_bwd", grid=(dm.NJ, dm.Bl),
        in_specs=[pl.BlockSpec((lp, 4 * cw), lambda j, s: (s, j)), pl.BlockSpec((lp, cw), lambda j, s: (s, j)),
                  pl.BlockSpec((3, cw), lambda j, s: (0, j))],
        out_specs=[pl.BlockSpec((lp, 4 * cw), lambda j, s: (s, j)), pl.BlockSpec((8, cw), lambda j, s: (0, j))],
        out_shape=[jax.ShapeDtypeStruct((dm.T, 4 * dm.D), BF16), jax.ShapeDtypeStruct((8, dm.D), F32)],
        compiler_params=_cp(2),
    )(proj_a, dy_conv, conv_w)


def _interleave(gens):
    results = [None] * len(gens)
    live = list(range(len(gens)))
    while live:
        for idx in list(live):
            try:
                next(gens[idx])
            except StopIteration as done:
                results[idx] = done.value
                live.remove(idx)
    return results


def _group_chunks(dm):
    n = dm.NC - dm.C0
    return 3 if n % 3 == 0 else 1


def _group_masks(rows):
    ii = lax.broadcasted_iota(jnp.int32, (rows, rows), 0)
    jj = lax.broadcasted_iota(jnp.int32, (rows, rows), 1)
    same = jnp.right_shift(ii, CHUNK_SHIFT) == jnp.right_shift(jj, CHUNK_SHIFT)
    return same & (jj <= ii), same & (jj > ii)


def _first_row(chunk):
    return chunk * CHUNK if isinstance(chunk, int) else pl.multiple_of(chunk * CHUNK, CHUNK)


def _chunk_totals(b, fwd):
    hk = b.shape[1]
    rows = [b[c * CHUNK + CHUNK - 1:(c + 1) * CHUNK] if fwd else b[c * CHUNK:c * CHUNK + 1]
            for c in range(b.shape[0] // CHUNK)]
    return jnp.concatenate([jnp.broadcast_to(r, (CHUNK, hk)) for r in rows], axis=0)


def _log_gate(lr_rows, w_ref, b_ref, first_group, hk):
    z = _dot(lr_rows, w_ref[...]) + b_ref[...]
    e = jnp.exp(-jnp.abs(z))
    g = (jnp.minimum(z, 0.0) - jnp.log(1.0 + e)) * (1.0 / GATE_NORMALIZER)
    dg_dz = jnp.where(z >= 0.0, e, 1.0) / (1.0 + e) * (1.0 / GATE_NORMALIZER)
    row = lax.broadcasted_iota(jnp.int32, (lr_rows.shape[0], hk), 0)
    pad = first_group & (row < PAD_ROWS)
    return jnp.where(pad, 0.0, g), jnp.where(pad, 0.0, dg_dz)


def _gla_fwd(proj_b, lr, wg_f, bg_f, wg_b, bg_b, gla_g, dm):
    lp, hk, hv, nc, c0, hw = dm.LP, dm.HK, dm.HV, dm.NC, dm.C0, dm.HW
    scale = hk ** -0.5
    gc = _group_chunks(dm)
    gr, ng = gc * CHUNK, (nc - c0) // gc

    def body(p_ref, lr_ref, wf_ref, bf_ref, wb_ref, bb_ref, gg_ref, o_ref, y_ref, st_ref, b_out, gs_out, oacc_f, oacc_b):
        low_incl, up_strict = _group_masks(gr)
        if c0 > 0:
            zr = c0 * CHUNK
            o_ref[0:zr, :] = jnp.zeros((zr, hv), BF16)
            y_ref[0:zr, :] = jnp.zeros((zr, hv), BF16)
            b_out[:, 0:zr, :] = jnp.zeros((2, zr, hk), F32)
            gs_out[:, 0:zr, :] = jnp.zeros((2, zr, hk), F32)
            st_ref[0, 0, :, 0:c0] = jnp.zeros((2, c0, hv, hk), BF16)

        def decay(gi, fwd):
            w_ref, b_ref = (wf_ref, bf_ref) if fwd else (wb_ref, bb_ref)
            r0 = _first_row(c0 + gi * gc)
            yield
            g, dg_dz = _log_gate(lr_ref[pl.ds(r0, gr), :], w_ref, b_ref, gi == 0, hk)
            gs_out[0 if fwd else 1, pl.ds(r0, gr), :] = dg_dz
            yield
            b = _chunk_cumsum(g, not fwd)
            b_out[0 if fwd else 1, pl.ds(r0, gr), :] = b
            return b

        def group(gi, st, b, fwd):
            oacc = oacc_f if fwd else oacc_b
            r0 = pl.multiple_of((c0 + gi * gc) * CHUNK, CHUNK)
            blk = p_ref[pl.ds(r0, gr), :]
            q = blk[:, :hk].astype(F32) * scale
            k = blk[:, hk:2 * hk].astype(F32)
            v = blk[:, 2 * hk:2 * hk + hv]
            btot = _chunk_totals(b, fwd)
            qi = (q * jnp.exp(b)).astype(BF16)
            ki = (k * jnp.exp(-b)).astype(BF16)
            kd = (k * jnp.exp(btot - b)).astype(BF16)
            dec = jnp.exp(btot)
            a = _dot_nt(qi, ki)
            yield
            o = _dot(jnp.where(low_incl if fwd else up_strict, a, 0.0).astype(BF16), v)
            chunk_rows = [slice(c * CHUNK, (c + 1) * CHUNK) for c in range(gc)]
            kv = [_dot_tn(v[rows], kd[rows]) for rows in chunk_rows]
            for c in (range(gc) if fwd else reversed(range(gc))):
                yield
                rows = chunk_rows[c]
                st_b = st.astype(BF16)
                st_ref[0, 0, 0 if fwd else 1, c0 + gi * gc + c] = st_b
                oacc[pl.ds(r0 + c * CHUNK, CHUNK), :] = o[rows] + _dot_nt(qi[rows], st_b)
                st = st * dec[c * CHUNK:c * CHUNK + 1] + kv[c]
            return st

        def step(i, carry):
            st_f, st_b, b_f, b_b = carry
            gf, gb = i, ng - 1 - i
            return tuple(_interleave([group(gf, st_f, b_f, True), group(gb, st_b, b_b, False),
                                      decay(jnp.minimum(gf + 1, ng - 1), True), decay(jnp.maximum(gb - 1, 0), False)]))

        zero = jnp.zeros((hv, hk), F32)
        lax.fori_loop(0, ng, step, (zero, zero, *_interleave([decay(0, True), decay(ng - 1, False)])))

        def finish(i, carry):
            r0 = pl.multiple_of((c0 + i * gc) * CHUNK, CHUNK)
            o = oacc_f[pl.ds(r0, gr), :] + oacc_b[pl.ds(r0, gr), :]
            r = p_ref[pl.ds(r0, gr), 2 * hk + hv:].astype(F32)
            on = o * lax.rsqrt(jnp.mean(o * o, axis=-1, keepdims=True) + EPS) * gg_ref[...]
            o_ref[pl.ds(r0, gr), :] = o.astype(BF16)
            y_ref[pl.ds(r0, gr), :] = (on * r * _sigmoid(r)).astype(BF16)
            return carry

        lax.fori_loop(0, ng, finish, 0)

    head = lambda s, h: (s, h)
    wspec = pl.BlockSpec((LR_LANES, hk), lambda s, h: (0, h))
    bspec = pl.BlockSpec((1, hk), lambda s, h: (0, h))
    return pl.pallas_call(
        body, name="gla_fwd", grid=(dm.Bl, HEADS),
        in_specs=[pl.BlockSpec((lp, hw), head), pl.BlockSpec((lp, LR_LANES), lambda s, h: (s, 0)),
                  wspec, bspec, wspec, bspec, pl.BlockSpec((1, hv), lambda s, h: (0, 0))],
        out_specs=[pl.BlockSpec((lp, hv), head), pl.BlockSpec((lp, hv), head),
                   pl.BlockSpec((1, 1, 2, nc, hv, hk), lambda s, h: (s, h, 0, 0, 0, 0)),
                   pl.BlockSpec((2, lp, hk), lambda s, h: (0, s, h)), pl.BlockSpec((2, lp, hk), lambda s, h: (0, s, h))],
        out_shape=[jax.ShapeDtypeStruct((dm.T, dm.DV), BF16), jax.ShapeDtypeStruct((dm.T, dm.DV), BF16),
                   jax.ShapeDtypeStruct((dm.Bl, HEADS, 2, nc, hv, hk), BF16),
                   jax.ShapeDtypeStruct((2, dm.T, dm.DK), F32), jax.ShapeDtypeStruct((2, dm.T, dm.DK), F32)],
        scratch_shapes=[pltpu.VMEM((lp, hv), F32), pltpu.VMEM((lp, hv), F32)],
        compiler_params=_cp(2),
    )(proj_b, lr, wg_f, bg_f, wg_b, bg_b, gla_g)


def _gla_bwd(proj_b, lr, o_all, dy_gla, states, decays, gate_slopes, wg_f, wg_b, gla_g, dm):
    lp, hk, hv, nc, c0, hw = dm.LP, dm.HK, dm.HV, dm.NC, dm.C0, dm.HW
    scale = hk ** -0.5
    gc = _group_chunks(dm)
    gr, ng = gc * CHUNK, (nc - c0) // gc

    def body(p_ref, lr_ref, o_ref, dy_ref, st_ref, b_ref, gs_ref, wf_ref, wb_ref, gg_ref,
             d_ref, dlr_ref, gwf_ref, gbf_ref, gwb_ref, gbb_ref, ggg_ref, do_s, dq_s, dk_s, dv_s, dz_s):
        low_incl, up_strict = _group_masks(gr)
        h = pl.program_id(1)

        @pl.when(h == 0)
        def _():
            dlr_ref[...] = jnp.zeros_like(dlr_ref)

        if c0 > 0:
            zr = c0 * CHUNK
            d_ref[0:zr, :] = jnp.zeros((zr, hw), BF16)
        for acc in (dq_s, dk_s, dv_s):
            acc[...] = jnp.zeros_like(acc)

        def norm_bwd(i, ggg):
            r0 = pl.multiple_of((c0 + i * gc) * CHUNK, CHUNK)
            o = o_ref[pl.ds(r0, gr), :].astype(F32)
            dy = dy_ref[pl.ds(r0, gr), :].astype(F32)
            r = p_ref[pl.ds(r0, gr), 2 * hk + hv:].astype(F32)
            rstd = lax.rsqrt(jnp.mean(o * o, axis=-1, keepdims=True) + EPS)
            ohat = o * rstd
            sg = _sigmoid(r)
            d_on = dy * (r * sg)
            d_ref[pl.ds(r0, gr), 2 * hk + hv:] = (dy * ohat * gg_ref[...] * (sg * (1.0 + r * (1.0 - sg)))).astype(BF16)
            d_oh = d_on * gg_ref[...]
            do_s[pl.ds(r0, gr), :] = (rstd * (d_oh - ohat * jnp.mean(d_oh * ohat, axis=-1, keepdims=True))).astype(BF16)
            return ggg + jnp.sum(d_on * ohat, axis=0, keepdims=True)

        ggg = lax.fori_loop(0, ng, norm_bwd, jnp.zeros((1, hv), F32))

        @pl.when((pl.program_id(0) == 0) & (h == 0))
        def _():
            ggg_ref[...] = jnp.zeros_like(ggg_ref)

        ggg_ref[0:1, :] += ggg

        def load(gi):
            r0 = pl.multiple_of((c0 + gi * gc) * CHUNK, CHUNK)
            blk = p_ref[pl.ds(r0, gr), :]
            return r0, blk[:, :hk].astype(F32) * scale, blk[:, hk:2 * hk].astype(F32), blk[:, 2 * hk:2 * hk + hv]

        zero = jnp.zeros((hv, hk), F32)

        def grad(gi, carry, fwd):
            dst, gb = carry
            way = 0 if fwd else 1
            mask = low_incl if fwd else up_strict
            r0, q, k, v = load(gi)
            b = b_ref[way, pl.ds(r0, gr), :]
            btot = _chunk_totals(b, fwd)
            eb, enb, edb, dec = jnp.exp(b), jnp.exp(-b), jnp.exp(btot - b), jnp.exp(btot)
            qi_f, ki_f, kd_f = q * eb, k * enb, k * edb
            qi, ki, kd = qi_f.astype(BF16), ki_f.astype(BF16), kd_f.astype(BF16)
            do = do_s[pl.ds(r0, gr), :]
            a = _dot_nt(qi, ki)
            da = _dot_nt(do, v)
            yield
            a = jnp.where(mask, a, 0.0).astype(BF16)
            da = jnp.where(mask, da, 0.0).astype(BF16)
            dv = _dot_tn(a, do)
            dqi = _dot(da, ki)
            dki = _dot_tn(da, qi)
            dv_c, dqi_c, dkd_c, extra_c = [None] * gc, [None] * gc, [None] * gc, [None] * gc
            chunk_rows = [slice(c * CHUNK, (c + 1) * CHUNK) for c in range(gc)]
            qdo = [_dot_tn(do[rows], qi[rows]) for rows in chunk_rows]
            for c in (reversed(range(gc)) if fwd else range(gc)):
                yield
                rows = chunk_rows[c]
                st = st_ref[0, 0, way, c0 + gi * gc + c]
                dsn_b = dst.astype(BF16)
                dec_c = dec[c * CHUNK:c * CHUNK + 1]
                dv_c[c] = dv[rows] + _dot_nt(kd[rows], dsn_b)
                dqi_c[c] = dqi[rows] + _dot(do[rows], st)
                dkd_c[c] = _dot(v[rows], dsn_b)
                ddec = jnp.sum(st.astype(F32) * dst, axis=0, keepdims=True)
                extra = jnp.sum(dkd_c[c] * kd_f[rows], axis=0, keepdims=True) + ddec * dec_c
                extra_c[c] = jnp.broadcast_to(extra, (CHUNK, hk))
                dst = dst * dec_c + qdo[c]
            yield
            dv, dqi = jnp.concatenate(dv_c, axis=0), jnp.concatenate(dqi_c, axis=0)
            dkd, extra = jnp.concatenate(dkd_c, axis=0), jnp.concatenate(extra_c, axis=0)
            dq_s[pl.ds(r0, gr), :] += dqi * eb * scale
            dk_s[pl.ds(r0, gr), :] += dki * enb + dkd * edb
            dv_s[pl.ds(r0, gr), :] += dv
            db = dqi * qi_f - dki * ki_f - dkd * kd_f
            dg = _chunk_cumsum(db, fwd) + extra
            yield
            dz = dg * gs_ref[way, pl.ds(r0, gr), :]
            dz_s[way, pl.ds(r0, gr), :] = dz.astype(BF16)
            return dst, gb + jnp.sum(dz, axis=0, keepdims=True)

        def grad_step(i, carry):
            return tuple(_interleave([grad(ng - 1 - i, carry[0], True), grad(i, carry[1], False)]))

        init = (zero, jnp.zeros((1, hk), F32))
        (_, gb_f), (_, gb_b) = lax.fori_loop(0, ng, grad_step, (init, init))
        used = slice(c0 * CHUNK, lp)
        for way, (w_ref, gw_ref, gb_ref, gb) in enumerate(((wf_ref, gwf_ref, gbf_ref, gb_f), (wb_ref, gwb_ref, gbb_ref, gb_b))):
            dlr_ref[used, :] += _dot_nt(dz_s[way, used, :], w_ref[...])
            gw_ref[0] = _dot_tn(lr_ref[used, :], dz_s[way, used, :])
            gb_ref[0] = jnp.zeros((8, hk), F32)
            gb_ref[0, 0:1, :] = gb

        def combine(i, carry):
            r0 = pl.multiple_of((c0 + i * gc) * CHUNK, CHUNK)
            d_ref[pl.ds(r0, gr), 0:hk] = dq_s[pl.ds(r0, gr), :].astype(BF16)
            d_ref[pl.ds(r0, gr), hk:2 * hk] = dk_s[pl.ds(r0, gr), :].astype(BF16)
            d_ref[pl.ds(r0, gr), 2 * hk:2 * hk + hv] = dv_s[pl.ds(r0, gr), :].astype(BF16)
            return carry

        lax.fori_loop(0, ng, combine, 0)

    head = lambda s, h: (s, h)
    wspec = pl.BlockSpec((LR_LANES, hk), lambda s, h: (0, h))
    gwspec = pl.BlockSpec((1, LR_LANES, hk), lambda s, h: (s, 0, h))
    gbspec = pl.BlockSpec((1, 8, hk), lambda s, h: (s, 0, h))
    gw_shape = jax.ShapeDtypeStruct((dm.Bl, LR_LANES, dm.DK), F32)
    gb_shape = jax.ShapeDtypeStruct((dm.Bl, 8, dm.DK), F32)
    both = pl.BlockSpec((2, lp, hk), lambda s, h: (0, s, h))
    return pl.pallas_call(
        body, name="gla_bwd", grid=(dm.Bl, HEADS),
        in_specs=[pl.BlockSpec((lp, hw), head), pl.BlockSpec((lp, LR_LANES), lambda s, h: (s, 0)),
                  pl.BlockSpec((lp, hv), head), pl.BlockSpec((lp, hv), head),
                  pl.BlockSpec((1, 1, 2, nc, hv, hk), lambda s, h: (s, h, 0, 0, 0, 0)), both, both,
                  wspec, wspec, pl.BlockSpec((1, hv), lambda s, h: (0, 0))],
        out_specs=[pl.BlockSpec((lp, hw), head), pl.BlockSpec((lp, LR_LANES), lambda s, h: (s, 0)),
                   gwspec, gbspec, gwspec, gbspec, pl.BlockSpec((8, hv), lambda s, h: (0, 0))],
        out_shape=[jax.ShapeDtypeStruct((dm.T, HEADS * hw), BF16), jax.ShapeDtypeStruct((dm.T, LR_LANES), F32),
                   gw_shape, gb_shape, gw_shape, gb_shape, jax.ShapeDtypeStruct((8, hv), F32)],
        scratch_shapes=[pltpu.VMEM((lp, hv), BF16), pltpu.VMEM((lp, hk), F32), pltpu.VMEM((lp, hk), F32),
                        pltpu.VMEM((lp, hv), F32), pltpu.VMEM((2, lp, hk), BF16)],
        compiler_params=_cp(2),
    )(proj_b, lr, o_all, dy_gla, states, decays, gate_slopes, wg_f, wg_b, gla_g)


def _stream_tiles(n_tiles, loads, stores, compute):
    for cp in loads(0, 0):
        cp.start()

    def step(t, carry):
        slot = t % 2

        @pl.when(t + 1 < n_tiles)
        def _():
            for cp in loads(t + 1, 1 - slot):
                cp.start()

        for cp in loads(t, slot):
            cp.wait()

        @pl.when(t >= 2)
        def _():
            for cp in stores(t - 2, slot):
                cp.wait()

        compute(t, slot)
        for cp in stores(t, slot):
            cp.start()
        return carry

    lax.fori_loop(0, n_tiles, step, 0)
    for t in range(max(n_tiles - 2, 0), n_tiles):
        for cp in stores(t, t % 2):
            cp.wait()


def _token_tiles(dm, target_rows=512):
    rows = _pick(dm.S, target_rows, 16)
    per_seq = dm.S // rows
    return rows, dm.Bl * per_seq, lambda t: pl.multiple_of((t // per_seq) * dm.LP + dm.TM + (t % per_seq) * rows, 16)


def _head(y_conv, y_gla, proj_c, w_oc, w_og, w_out, x, target, g_post, dm):
    d, tm = dm.D, dm.TM
    rows, n_tiles, first_row = _token_tiles(dm, 256)
    n_out = 8

    def body(*refs):
        yc_hbm, yg_hbm, c_hbm, woc_ref, wog_ref, wo_ref, x_hbm, t_hbm, g_ref = refs[:9]
        outs, st_ref = refs[9:9 + n_out], refs[9 + n_out]
        ycbuf, ygbuf, cbuf, xbuf, tbuf = refs[10 + n_out:15 + n_out]
        obufs = refs[15 + n_out:15 + 2 * n_out]
        zbuf, zbuf2, sem_in, sem_out, sem_zero = refs[15 + 2 * n_out:]

        def loads(t, slot):
            padded = [(yc_hbm, ycbuf), (yg_hbm, ygbuf), (c_hbm, cbuf)]
            own = [(x_hbm, xbuf), (t_hbm, tbuf)]
            return ([pltpu.make_async_copy(h.at[pl.ds(first_row(t), rows), :], b.at[slot], sem_in.at[i, slot])
                     for i, (h, b) in enumerate(padded)] +
                    [pltpu.make_async_copy(h.at[pl.ds(t * rows, rows), :], b.at[slot], sem_in.at[3 + i, slot])
                     for i, (h, b) in enumerate(own)])

        def stores(t, slot):
            return [pltpu.make_async_copy(b.at[slot], h.at[pl.ds(first_row(t), rows), :], sem_out.at[i, slot])
                    for i, (h, b) in enumerate(zip(outs, obufs))]

        def compute(t, slot):
            mg_o, do_o, dy_o, dpc_o, dpg_o, dc_o, dyc_o, dyg_o = obufs
            pc = _dot(ycbuf[slot], woc_ref[...])
            pg = _dot(ygbuf[slot], wog_ref[...])
            sa = _sigmoid(cbuf[slot, :, :d].astype(F32))
            sb = _sigmoid(cbuf[slot, :, d:].astype(F32))
            merged = (sa * pc + sb * pg).astype(BF16)
            mg_o[slot] = merged
            out = _dot(merged, wo_ref[...])
            rstd = lax.rsqrt(jnp.mean(out * out, axis=-1, keepdims=True) + EPS)
            ohat = out * rstd
            err = xbuf[slot] + ohat * g_ref[...] - tbuf[slot]
            dy = err * (1.0 / d)
            d_oh = dy * g_ref[...]
            d_out = (rstd * (d_oh - ohat * jnp.mean(d_oh * ohat, axis=-1, keepdims=True))).astype(BF16)
            do_o[slot] = d_out
            dy_o[slot] = dy.astype(BF16)
            st_ref[0:1, :] += jnp.sum(dy * ohat, axis=0, keepdims=True)
            st_ref[1:2, :] += jnp.sum(err * err, axis=0, keepdims=True)
            dmg = _dot_nt(d_out, wo_ref[...])
            dpc = (dmg * sa).astype(BF16)
            dpg = (dmg * sb).astype(BF16)
            dpc_o[slot] = dpc
            dpg_o[slot] = dpg
            dc_o[slot, :, :d] = (dmg * pc * sa * (1.0 - sa)).astype(BF16)
            dc_o[slot, :, d:] = (dmg * pg * sb * (1.0 - sb)).astype(BF16)
            dyc_o[slot] = _dot_nt(dpc, woc_ref[...]).astype(BF16)
            dyg_o[slot] = _dot_nt(dpg, wog_ref[...]).astype(BF16)

        st_ref[...] = jnp.zeros_like(st_ref)
        zbuf[...] = jnp.zeros_like(zbuf)
        zbuf2[...] = jnp.zeros_like(zbuf2)
        zeros = [pltpu.make_async_copy(zbuf2 if out.shape[1] == 2 * d else zbuf, out.at[pl.ds(b * dm.LP, tm), :], sem_zero.at[i, b])
                 for i, out in enumerate(outs) for b in range(dm.Bl)]
        for cp in zeros:
            cp.start()
        _stream_tiles(n_tiles, loads, stores, compute)
        for cp in zeros:
            cp.wait()

    any_spec, vmem = pl.BlockSpec(memory_space=pl.ANY), pl.BlockSpec(memory_space=pltpu.VMEM)
    widths = [d, d, d, d, d, 2 * d, d, d]
    tile = lambda w, dt: pltpu.VMEM((2, rows, w), dt)
    return pl.pallas_call(
        body, name="head", in_specs=[any_spec] * 3 + [vmem] * 3 + [any_spec] * 2 + [vmem],
        out_specs=[any_spec] * n_out + [vmem],
        out_shape=[jax.ShapeDtypeStruct((dm.T, w), BF16) for w in widths] + [jax.ShapeDtypeStruct((8, d), F32)],
        scratch_shapes=[tile(d, BF16), tile(d, BF16), tile(2 * d, BF16), tile(d, F32), tile(d, F32)]
        + [tile(w, BF16) for w in widths]
        + [pltpu.VMEM((tm, d), BF16), pltpu.VMEM((tm, 2 * d), BF16), pltpu.SemaphoreType.DMA((5, 2)),
           pltpu.SemaphoreType.DMA((n_out, 2)), pltpu.SemaphoreType.DMA((n_out, dm.Bl))],
        compiler_params=pltpu.CompilerParams(vmem_limit_bytes=VMEM_LIMIT_BYTES),
    )(y_conv, y_gla, proj_c, w_oc, w_og, w_out, x.reshape(dm.Bl * dm.S, d), target.reshape(dm.Bl * dm.S, d), g_post)


def _grad_h(d_parts, gathered, dy, x, metapad, g_pre, dm):
    d, tm = dm.D, dm.TM
    rows, n_tiles, first_row = _token_tiles(dm, 256)
    widths = [a.shape[1] for a in d_parts]
    np_ = len(d_parts)

    def body(*refs):
        d_hbm, g_hbm, dy_hbm, x_hbm, mp_ref, g_ref = refs[:np_], refs[np_], refs[np_ + 1], refs[np_ + 2], refs[np_ + 3], refs[np_ + 4]
        gx_hbm, dmeta_ref, gg_ref = refs[np_ + 5:np_ + 8]
        parts, edges, sems = refs[np_ + 8:np_ + 12], refs[np_ + 12], refs[np_ + 13]
        dbufs = refs[np_ + 14:2 * np_ + 14]
        dybuf, xbuf, gbuf = refs[2 * np_ + 14:2 * np_ + 17]
        mbufs = refs[2 * np_ + 17:3 * np_ + 17]
        sem_in, sem_out, sem_meta = refs[3 * np_ + 17:]

        def grad_u(tiles):
            du = _dot(tiles[0].astype(BF16), parts[0][...])
            for a, w in zip(tiles[1:], parts[1:]):
                du = du + _dot(a.astype(BF16), w[...])
            return du

        def norm_bwd(h, du, dy):
            rstd = lax.rsqrt(jnp.mean(h * h, axis=-1, keepdims=True) + EPS)
            hhat = h * rstd
            dug = du * g_ref[...]
            gg_ref[0:1, :] += jnp.sum(du * hhat, axis=0, keepdims=True)
            return dy + rstd * (dug - hhat * jnp.mean(dug * hhat, axis=-1, keepdims=True))

        def loads(t, slot):
            padded = list(zip(d_hbm, dbufs)) + [(dy_hbm, dybuf)]
            return ([pltpu.make_async_copy(h.at[pl.ds(first_row(t), rows), :], b.at[slot], sem_in.at[i, slot])
                     for i, (h, b) in enumerate(padded)] +
                    [pltpu.make_async_copy(x_hbm.at[pl.ds(t * rows, rows), :], xbuf.at[slot], sem_in.at[np_ + 1, slot])])

        def stores(t, slot):
            return [pltpu.make_async_copy(gbuf.at[slot], gx_hbm.at[pl.ds(t * rows, rows), :], sem_out.at[slot])]

        def compute(t, slot):
            gbuf[slot] = norm_bwd(xbuf[slot], grad_u([b[slot] for b in dbufs]), dybuf[slot].astype(F32))

        gg_ref[...] = jnp.zeros_like(gg_ref)
        meta = [pltpu.make_async_copy(h.at[pl.ds(b * dm.LP, tm), :], buf.at[pl.ds(b * tm, tm), :], sem_meta.at[i, b])
                for i, (h, buf) in enumerate(zip(d_hbm, mbufs)) for b in range(dm.Bl)]
        for cp in meta:
            cp.start()
        _load_packed(g_hbm, parts, edges, sems, dm)
        _stream_tiles(n_tiles, loads, stores, compute)
        for cp in meta:
            cp.wait()
        dmeta_ref[...] = norm_bwd(jnp.concatenate([mp_ref[...]] * dm.Bl, axis=0), grad_u([buf[...] for buf in mbufs]), 0.0)

    any_spec, vmem = pl.BlockSpec(memory_space=pl.ANY), pl.BlockSpec(memory_space=pltpu.VMEM)
    grad_x, d_meta, gg = pl.pallas_call(
        body, name="grad_h", in_specs=[any_spec] * (np_ + 3) + [vmem, vmem], out_specs=[any_spec, vmem, vmem],
        out_shape=[jax.ShapeDtypeStruct((dm.Bl * dm.S, d), F32), jax.ShapeDtypeStruct((dm.Bl * tm, d), F32),
                   jax.ShapeDtypeStruct((8, d), F32)],
        scratch_shapes=_packed_scratch(dm)
        + [pltpu.VMEM((2, rows, w), a.dtype) for w, a in zip(widths, d_parts)]
        + [pltpu.VMEM((2, rows, d), BF16), pltpu.VMEM((2, rows, d), F32), pltpu.VMEM((2, rows, d), F32)]
        + [pltpu.VMEM((dm.Bl * tm, w), a.dtype) for w, a in zip(widths, d_parts)]
        + [pltpu.SemaphoreType.DMA((np_ + 2, 2)), pltpu.SemaphoreType.DMA((2,)), pltpu.SemaphoreType.DMA((np_, dm.Bl))],
        compiler_params=pltpu.CompilerParams(vmem_limit_bytes=VMEM_LIMIT_BYTES),
    )(*d_parts, gathered, dy, x.reshape(dm.Bl * dm.S, d), metapad, g_pre)
    return grad_x.reshape(dm.Bl, dm.S, d), d_meta.reshape(dm.Bl, tm, d), gg


def _adamw(partials, w, m, v, name, by_columns=False):
    r, c = w.shape
    n_parts = partials.shape[0]
    tr, tc = (r, _pick(c, 128, 128)) if by_columns else (_pick(r, 256, 16), c)

    def body(p_ref, w_ref, m_ref, v_ref, g_ref, d_ref, nm_ref, nv_ref):
        g = p_ref[0].astype(F32)
        for j in range(1, n_parts):
            g = g + p_ref[j].astype(F32)
        g_ref[...] = g
        d_ref[...], nm_ref[...], nv_ref[...] = _adam_step(g, w_ref[...], m_ref[...], v_ref[...])

    at = (lambda i: (0, i)) if by_columns else (lambda i: (i, 0))
    tile = pl.BlockSpec((tr, tc), at)
    out = jax.ShapeDtypeStruct((r, c), F32)
    return pl.pallas_call(
        body, name=name, grid=(c // tc if by_columns else r // tr,),
        in_specs=[pl.BlockSpec((n_parts, tr, tc), lambda i: (0,) + at(i)), tile, tile, tile],
        out_specs=[tile, tile, tile, tile], out_shape=[out, out, out, out], compiler_params=_cp(1),
    )(partials, w, m, v)


def _adam_step(g, w, m, v):
    m2 = ADAM_B1 * m + (1.0 - ADAM_B1) * g
    v2 = ADAM_B2 * v + (1.0 - ADAM_B2) * (g * g)
    m_hat = m2 / (1.0 - ADAM_B1 ** ADAM_STEP)
    v_hat = v2 / (1.0 - ADAM_B2 ** ADAM_STEP)
    return -ADAM_LR * (m_hat / (jnp.sqrt(v_hat) + ADAM_EPS) + ADAM_WD * w), m2, v2


def _adamw_small(items, name):
    n = len(items)

    def body(*refs):
        ins, outs = refs[:4 * n], refs[4 * n:]
        for i in range(n):
            p_ref, w_ref, m_ref, v_ref = ins[4 * i:4 * i + 4]
            g = p_ref[0]
            for j in range(1, p_ref.shape[0]):
                g = g + p_ref[j]
            delta, m2, v2 = _adam_step(g, w_ref[...], m_ref[...], v_ref[...])
            for o_ref, val in zip(outs[4 * i:4 * i + 4], (g, delta, m2, v2)):
                o_ref[...] = val

    vmem = pl.BlockSpec(memory_space=pltpu.VMEM)
    res = pl.pallas_call(
        body, name=name, in_specs=[vmem] * (4 * n), out_specs=[vmem] * (4 * n),
        out_shape=[jax.ShapeDtypeStruct(w.shape, F32) for _, w, _, _ in items for _ in range(4)],
    )(*[a for item in items for a in item])
    return [res[4 * i:4 * i + 4] for i in range(n)]


def _unpack_rows(a, b, c, lr, dm):
    d, hk, hv, cw, nj, hw = dm.D, dm.HK, dm.HV, dm.CW, dm.NJ, dm.HW
    conv = a.reshape(nj, 4, cw, d).transpose(1, 0, 2, 3).reshape(4 * d, d)
    heads = b.reshape(HEADS, hw, d)
    q = heads[:, :hk].reshape(HEADS * hk, d)
    k = heads[:, hk:2 * hk].reshape(HEADS * hk, d)
    v = heads[:, 2 * hk:2 * hk + hv].reshape(HEADS * hv, d)
    r = heads[:, 2 * hk + hv:].reshape(HEADS * hv, d)
    return jnp.concatenate([conv, q, k, v, r, lr[:2 * RANK], c], axis=0)


def _column_shards(g, shard_shape):
    r, c = g.shape
    return g.reshape(r, N_DEV, c // N_DEV).transpose(1, 0, 2).reshape((N_DEV,) + tuple(shard_shape))


def _join_column_shards(parts):
    r, c = parts.shape[-2:]
    return parts.reshape(N_DEV, r, c).transpose(1, 0, 2).reshape(r, N_DEV * c)


def _local_step(x, target, meta, g_pre, u, wt_shards, conv_w, wg_f, bg_f, wg_b, bg_b, gla_g, out_weights, g_post,
                on_matrix_grads=None):
    bl, s, d = x.shape
    dm = _Dims(bl, s, d)
    metapad = jnp.concatenate([jnp.zeros((dm.TM - N_META, d), F32), meta], axis=0)
    wgp_f = jnp.pad(wg_f, ((0, LR_LANES - RANK), (0, 0))).astype(BF16)
    wgp_b = jnp.pad(wg_b, ((RANK, LR_LANES - 2 * RANK), (0, 0))).astype(BF16)

    u = _prenorm_meta(u, metapad, g_pre, dm)
    proj_a, proj_b, proj_c, lr = _inproj(u, wt_shards, dm)
    y_conv = _conv_fwd(proj_a, conv_w, dm)
    o_all, y_gla, states, decays, gate_slopes = _gla_fwd(proj_b, lr, wgp_f, bg_f, wgp_b, bg_b, gla_g, dm)
    w_oc, w_og, w_out = out_weights(y_conv) if callable(out_weights) else out_weights
    merged, d_out, dy, d_pc, d_pg, d_c, dy_conv, dy_gla, stats = _head(y_conv, y_gla, proj_c, w_oc, w_og, w_out, x, target,
                                                                        g_post, dm)

    g_out = _matmul_tn(merged, d_out, BF16, "grad_w_out")
    g_oc = _matmul_tn(y_conv, d_pc, BF16, "grad_w_out_conv")
    g_og = _matmul_tn(y_gla, d_pg, BF16, "grad_w_out_gla")
    if on_matrix_grads is not None:
        conv_w = conv_w + on_matrix_grads(dict(w_out_conv=g_oc, w_out_gla=g_og, w_merge_out=g_out))
    d_a, g_conv = _conv_bwd(proj_a, dy_conv, conv_w, dm)
    d_b, d_lr, gwp_f, gbp_f, gwp_b, gbp_b, g_gla = _gla_bwd(proj_b, lr, o_all, dy_gla, states, decays, gate_slopes, wgp_f, wgp_b, gla_g, dm)
    g_abc = _matmul_tn_group([d_a, d_b, d_c], u, "grad_w_in", tile=d).reshape(9 * d, d)
    g_in = _unpack_rows(g_abc[:4 * d], g_abc[4 * d:7 * d], g_abc[7 * d:], _matmul_tn(d_lr, u, BF16, "grad_w_in_gate"), dm)
    if on_matrix_grads is not None:
        d_lr = d_lr + on_matrix_grads(dict(w_in=g_in))
    grad_x, d_meta, g_pre_rows = _grad_h([d_a, d_b, d_c, d_lr], wt_shards, dy, x, metapad, g_pre, dm)

    grads = dict(
        meta_tokens=jnp.sum(d_meta[:, dm.TM - N_META:, :], axis=0), norm_pre=g_pre_rows[0:1], w_in=g_in,
        conv_w=g_conv[0:3], w_gate_fwd=jnp.sum(gwp_f, axis=0)[:RANK], b_gate_fwd=jnp.sum(gbp_f, axis=0)[0:1],
        w_gate_bwd=jnp.sum(gwp_b, axis=0)[RANK:2 * RANK], b_gate_bwd=jnp.sum(gbp_b, axis=0)[0:1],
        gla_norm=g_gla[0:1], w_out_conv=g_oc, w_out_gla=g_og, w_merge_out=g_out, norm_post=stats[0:1])
    return stats[1:2], grad_x, grads


MATRICES = ("w_out_conv", "w_out_gla", "w_merge_out")
SMALL_SHARDED = ("meta_tokens", "conv_w", "w_gate_fwd", "w_gate_bwd")
REPLICATED = ("norm_pre", "b_gate_fwd", "b_gate_bwd", "gla_norm", "norm_post")
NAMES = ("meta_tokens", "norm_pre", "w_in", "conv_w", "w_gate_fwd", "b_gate_fwd", "w_gate_bwd", "b_gate_bwd", "gla_norm",
         "w_out_conv", "w_out_gla", "w_merge_out", "norm_post")


def kernel(x, meta_tokens, norm_pre, w_in, conv_w, w_gate_fwd, b_gate_fwd, w_gate_bwd, b_gate_bwd, gla_norm, w_out_conv, w_out_gla, w_merge_out, norm_post, loss_target, m_meta_tokens, m_norm_pre, m_w_in, m_conv_w, m_w_gate_fwd, m_b_gate_fwd, m_w_gate_bwd, m_b_gate_bwd, m_gla_norm, m_w_out_conv, m_w_out_gla, m_w_merge_out, m_norm_post, v_meta_tokens, v_norm_pre, v_w_in, v_conv_w, v_w_gate_fwd, v_b_gate_fwd, v_w_gate_bwd, v_b_gate_bwd, v_gla_norm, v_w_out_conv, v_w_out_gla, v_w_merge_out, v_norm_post):
    w = dict(meta_tokens=meta_tokens, norm_pre=norm_pre, w_in=w_in[0], conv_w=conv_w, w_gate_fwd=w_gate_fwd,
             b_gate_fwd=b_gate_fwd, w_gate_bwd=w_gate_bwd, b_gate_bwd=b_gate_bwd, gla_norm=gla_norm,
             w_out_conv=w_out_conv[0], w_out_gla=w_out_gla[0], w_merge_out=w_merge_out[0], norm_post=norm_post)
    m = dict(meta_tokens=m_meta_tokens, norm_pre=m_norm_pre, w_in=m_w_in[0], conv_w=m_conv_w, w_gate_fwd=m_w_gate_fwd,
             b_gate_fwd=m_b_gate_fwd, w_gate_bwd=m_w_gate_bwd, b_gate_bwd=m_b_gate_bwd, gla_norm=m_gla_norm,
             w_out_conv=m_w_out_conv[0], w_out_gla=m_w_out_gla[0], w_merge_out=m_w_merge_out[0], norm_post=m_norm_post)
    v = dict(meta_tokens=v_meta_tokens, norm_pre=v_norm_pre, w_in=v_w_in[0], conv_w=v_conv_w, w_gate_fwd=v_w_gate_fwd,
             b_gate_fwd=v_b_gate_fwd, w_gate_bwd=v_w_gate_bwd, b_gate_bwd=v_b_gate_bwd, gla_norm=v_gla_norm,
             w_out_conv=v_w_out_conv[0], w_out_gla=v_w_out_gla[0], w_merge_out=v_w_merge_out[0], norm_post=v_norm_post)
    d = x.shape[-1]

    dm = _Dims(*x.shape)
    me = 4 * lax.axis_index("x") + 2 * lax.axis_index("y") + lax.axis_index("c")
    wt_shards, *small_all, u = _gather_two_level(
        [_pad_shard(w["w_in"].T.astype(BF16), me)] + [w[n] for n in SMALL_SHARDED], "gather_weights",
        _prenorm_tokens_side(x, norm_pre, dm))
    _, late_weights = _exchange_start([w[n].astype(BF16) for n in MATRICES], [], small_all[0], "gather_out_weights_start")
    small = {n: _join_column_shards(p) for n, p in zip(SMALL_SHARDED, small_all)}

    def out_weights(after):
        return tuple(a.reshape(-1, d) for a in _exchange_wait(late_weights, after, "gather_out_weights_wait"))

    pending = []

    def on_matrix_grads(g):
        token, state = _exchange_start([], [t.reshape(N_DEV, -1, d) for t in g.values()], None,
                                       "exchange_grads_start_" + "_".join(g))
        pending.append((tuple(g), state))
        return token

    sq_err_cols, grad_x, grads = _local_step(
        x, loss_target, small["meta_tokens"], norm_pre, u, wt_shards, small["conv_w"], small["w_gate_fwd"], b_gate_fwd,
        small["w_gate_bwd"], b_gate_bwd, gla_norm, out_weights, norm_post, on_matrix_grads)
    received = {}
    for names, state in pending:
        received.update(zip(names, _exchange_wait(state, grad_x, "exchange_grads_wait_" + "_".join(names))))

    exchanged = _exchange([grads[n] for n in REPLICATED] + [sq_err_cols],
                          [_column_shards(grads[n], w[n].shape) for n in SMALL_SHARDED], "exchange_small_grads")
    small_recv = exchanged[:len(REPLICATED)] + exchanged[len(REPLICATED) + 1:]
    loss = 0.5 / d * jnp.sum(exchanged[len(REPLICATED)])

    results = {"w_in": [r.T[None] for r in _adamw(received["w_in"], w["w_in"].T, m["w_in"].T, v["w_in"].T, "adamw_w_in", by_columns=True)]}
    for n in MATRICES:
        results[n] = [r[None] for r in _adamw(received[n], w[n], m[n], v[n], "adamw_" + n)]
    small_names = REPLICATED + SMALL_SHARDED
    results.update(zip(small_names, _adamw_small([(p, w[n], m[n], v[n]) for n, p in zip(small_names, small_recv)], "adamw_small")))
    return (loss, grad_x, *[results[n][i] for i in range(4) for n in NAMES])
```

```python
import jax
import jax.numpy as jnp
from jax import lax
from jax.experimental import pallas as pl
from jax.experimental.pallas import tpu as pltpu

F32 = jnp.float32
BF16 = jnp.bfloat16
MESH = pl.DeviceIdType.MESH

N_META = 16
CHUNK = 64
CHUNK_SHIFT = 6
HEADS = 4
RANK = 16
LR_LANES = 128
PAD_ROWS = CHUNK - N_META
EPS = 1e-6
GATE_NORMALIZER = 16.0
N_DEV = 8
ADAM_LR, ADAM_B1, ADAM_B2, ADAM_EPS, ADAM_WD, ADAM_STEP = 0.001, 0.9, 0.999, 1e-08, 0.01, 10
VMEM_LIMIT_BYTES = 56 * 1024 * 1024


class _Dims:
    def __init__(self, bl, s, d):
        self.Bl, self.S, self.D = bl, s, d
        self.TM = CHUNK
        self.LP = self.TM + s
        self.T = bl * self.LP
        self.TPS = self.LP // self.TM
        self.NC = self.LP // CHUNK
        self.C0 = (self.TM - CHUNK) // CHUNK
        self.DK, self.DV = d // 2, d
        self.HK, self.HV = self.DK // HEADS, self.DV // HEADS
        self.HW = 2 * self.HK + 2 * self.HV
        self.CW = 256 if d % 256 == 0 and d > 256 else d // 4
        self.NJ = d // self.CW


def _pick(n, target, mult):
    t = min(n, target)
    while t >= mult:
        if n % t == 0 and t % mult == 0:
            return t
        t -= mult
    return n


def _cp(n_axes):
    return pltpu.CompilerParams(dimension_semantics=("arbitrary",) * n_axes, vmem_limit_bytes=VMEM_LIMIT_BYTES)


def _sigmoid(x):
    return 1.0 / (1.0 + jnp.exp(-x))


def _dot(a, b):
    return jnp.dot(a, b, preferred_element_type=F32)


def _dot_nt(a, b):
    return lax.dot_general(a, b, (((1,), (1,)), ((), ())), preferred_element_type=F32)


def _dot_tn(a, b):
    return lax.dot_general(a, b, (((0,), (0,)), ((), ())), preferred_element_type=F32)


def _chunk_cumsum(x, reverse):
    rows = x.shape[0]
    r = lax.broadcasted_iota(jnp.int32, x.shape, 0) & (CHUNK - 1)
    step = 1
    while step < CHUNK:
        if reverse:
            x = x + jnp.where(r < CHUNK - step, pltpu.roll(x, rows - step, 0), 0.0)
        else:
            x = x + jnp.where(r >= step, pltpu.roll(x, step, 0), 0.0)
        step *= 2
    return x


def _exchange(gathers, scatters, name):
    arrays = list(gathers) + list(scatters)
    n, ng = len(arrays), len(gathers)

    def body(*refs):
        ins, outs = refs[:n], refs[n:2 * n]
        send_sems, recv_sems, local_sems = refs[2 * n:]
        x, y, c = lax.axis_index("x"), lax.axis_index("y"), lax.axis_index("c")
        me = 4 * x + 2 * y + c
        started = []
        for t in range(n):
            src, dst = ins[t], outs[t]
            own = pltpu.make_async_copy(src if t < ng else src.at[me], dst.at[me], local_sems.at[t])
            own.start()
            started.append(own)
            for k, pos, peer in _peers(x, y, c):
                cp = pltpu.make_async_remote_copy(
                    src_ref=src if t < ng else src.at[peer], dst_ref=dst.at[me],
                    send_sem=send_sems.at[t * (N_DEV - 1) + k - 1], recv_sem=recv_sems.at[t * (N_DEV - 1) + k - 1],
                    device_id=pos, device_id_type=MESH)
                cp.start()
                started.append(cp)
        for cp in started:
            cp.wait()

    out_shape = [jax.ShapeDtypeStruct((N_DEV,) + a.shape if t < ng else a.shape, a.dtype) for t, a in enumerate(arrays)]
    any_spec = pl.BlockSpec(memory_space=pl.ANY)
    return pl.pallas_call(
        body, name=name, out_shape=out_shape, in_specs=[any_spec] * n, out_specs=[any_spec] * n,
        scratch_shapes=[pltpu.SemaphoreType.DMA((n * (N_DEV - 1),)), pltpu.SemaphoreType.DMA((n * (N_DEV - 1),)),
                        pltpu.SemaphoreType.DMA((n,))],
        compiler_params=pltpu.CompilerParams(has_side_effects=True),
    )(*arrays)


def _gather_two_level(arrays, name, side=None):
    n = len(arrays)
    per = N_DEV - 1
    work, side_in, side_in_specs, side_out, side_out_specs, side_scratch = side or (None, [], [], [], [], [])
    n_in, n_out = len(side_in), len(side_out)

    def body(*refs):
        ins, outs = refs[:n], refs[n + n_in:2 * n + n_in]
        send_sems, recv_sems, local_sems = refs[2 * n + n_in + n_out:2 * n + n_in + n_out + 3]
        x, y, c = lax.axis_index("x"), lax.axis_index("y"), lax.axis_index("c")
        sibling = (x, y, 1 - c)
        chips = [(1 - x, y), (x, 1 - y), (1 - x, 1 - y)]
        index = lambda px, py, pc: 4 * px + 2 * py + pc

        def copy(t, k, block, to, from_input=False):
            slab = outs[t].at[index(*block)]
            return pltpu.make_async_remote_copy(
                src_ref=ins[t] if from_input else slab, dst_ref=slab, send_sem=send_sems.at[t * per + k],
                recv_sem=recv_sems.at[t * per + k], device_id=to, device_id_type=MESH)

        own, sent = [], []
        for t in range(n):
            own.append(pltpu.make_async_copy(ins[t], outs[t].at[index(x, y, c)], local_sems.at[t]))
            own[-1].start()
            first = [copy(t, 0, (x, y, c), sibling, True)]
            first += [copy(t, 1 + j, (x, y, c), (*chip, c), True) for j, chip in enumerate(chips)]
            for cp in first:
                cp.start()
            sent += first
        if work is not None:
            work(refs[n:n + n_in], refs[2 * n + n_in:2 * n + n_in + n_out], refs[2 * n + n_in + n_out + 3:])
        for t in range(n):
            for j, chip in enumerate(chips):
                copy(t, 1 + j, (*chip, c), (x, y, c)).wait_recv()
                sent.append(copy(t, 4 + j, (*chip, c), sibling))
                sent[-1].start()
        for t in range(n):
            copy(t, 0, sibling, (x, y, c)).wait_recv()
            for j, chip in enumerate(chips):
                copy(t, 4 + j, (*chip, 1 - c), (x, y, c)).wait_recv()
        for cp in sent:
            cp.wait_send()
        for cp in own:
            cp.wait()

    out_shape = [jax.ShapeDtypeStruct((N_DEV,) + a.shape, a.dtype) for a in arrays]
    any_spec = pl.BlockSpec(memory_space=pl.ANY)
    return pl.pallas_call(
        body, name=name, out_shape=out_shape + list(side_out), in_specs=[any_spec] * n + list(side_in_specs),
        out_specs=[any_spec] * n + list(side_out_specs),
        scratch_shapes=[pltpu.SemaphoreType.DMA((n * per,)), pltpu.SemaphoreType.DMA((n * per,)),
                        pltpu.SemaphoreType.DMA((n,))] + list(side_scratch),
        compiler_params=pltpu.CompilerParams(has_side_effects=True, vmem_limit_bytes=VMEM_LIMIT_BYTES),
    )(*arrays, *side_in)


def _peers(x, y, c):
    out = []
    for k in range(1, N_DEV):
        px = 1 - x if (k >> 2) & 1 else x
        py = 1 - y if (k >> 1) & 1 else y
        pc = 1 - c if k & 1 else c
        out.append((k, (px, py, pc), 4 * px + 2 * py + pc))
    return out


def _exchange_start(gathers, scatters, after, name):
    shard_rows = [None] * len(gathers) + [s[1] if isinstance(s, tuple) else None for s in scatters]
    arrays = list(gathers) + [s[0] if isinstance(s, tuple) else s for s in scatters]
    n, ng = len(arrays), len(gathers)
    hbm = pl.BlockSpec(memory_space=pltpu.HBM)
    sem = pl.BlockSpec(memory_space=pltpu.SEMAPHORE)

    extra = [] if after is None else [after]
    ne = len(extra)

    def body(*refs):
        ins, lands = refs[:n], refs[n:2 * n]
        send_sems, recv_sems = refs[2 * n + ne], refs[2 * n + ne + 1]
        token = refs[4 * n + ne + 2]
        x, y, c = lax.axis_index("x"), lax.axis_index("y"), lax.axis_index("c")
        me = 4 * x + 2 * y + c
        for t in range(n):
            for k, pos, peer in _peers(x, y, c):
                pltpu.make_async_remote_copy(
                    src_ref=_block_for(ins[t], peer, t < ng, shard_rows[t]), dst_ref=lands[t].at[me],
                    send_sem=send_sems.at[t * (N_DEV - 1) + k - 1], recv_sem=recv_sems.at[t * (N_DEV - 1) + k - 1],
                    device_id=pos, device_id_type=MESH).start()
        token[...] = jnp.zeros_like(token)

    me = 4 * lax.axis_index("x") + 2 * lax.axis_index("y") + lax.axis_index("c")

    def own_block(t, a):
        if t < ng:
            return a
        if shard_rows[t] is None:
            return lax.dynamic_index_in_dim(a, me, 0, keepdims=False)
        assert all(_shard_window(j, shard_rows[t]) + _padded_shard_rows(shard_rows[t]) <= a.shape[0] for j in range(N_DEV))
        return lax.dynamic_slice_in_dim(a, _shard_window(me, shard_rows[t]), _padded_shard_rows(shard_rows[t]), 0)

    blocks = [own_block(t, a) for t, a in enumerate(arrays)]
    lands = [lax.dynamic_update_index_in_dim(lax.empty((N_DEV,) + b.shape if t < ng or shard_rows[t] else a.shape, a.dtype), b, me, 0)
             for t, (a, b) in enumerate(zip(arrays, blocks))]
    operands = [pltpu.with_memory_space_constraint(a, pltpu.HBM) for a in arrays + lands]
    sems = pltpu.SemaphoreType.DMA((n * (N_DEV - 1),))
    res = pl.pallas_call(
        body, name=name,
        out_shape=(sems, sems, *[pltpu.HBM(a.shape, a.dtype) for a in arrays + lands], jax.ShapeDtypeStruct((8, 128), F32)),
        in_specs=[hbm] * (2 * n) + [pl.BlockSpec(memory_space=pl.ANY)] * ne,
        out_specs=(sem, sem, *[hbm] * (2 * n), pl.BlockSpec(memory_space=pltpu.VMEM)),
        input_output_aliases={i: 2 + i for i in range(2 * n)},
        compiler_params=pltpu.CompilerParams(has_side_effects=pltpu.SideEffectType.DATAFLOW_SIDE_EFFECTING),
    )(*operands, *extra)
    return res[-1][0, 0], (ng, shard_rows, res[0], res[1], list(res[2:2 + n]), list(res[2 + n:2 + 2 * n]))


def _block_for(ref, peer, whole, shard_rows):
    if whole:
        return ref
    if shard_rows is None:
        return ref.at[peer]
    return ref.at[pl.ds(pl.multiple_of(_shard_window(peer, shard_rows), BF16_TILE_ROWS), _padded_shard_rows(shard_rows))]


def _exchange_wait(state, after, name):
    ng, shard_rows, send_sems, recv_sems, sent, lands = state
    n = len(sent)
    hbm = pl.BlockSpec(memory_space=pltpu.HBM)
    sem = pl.BlockSpec(memory_space=pltpu.SEMAPHORE)

    def body(*refs):
        ins, land_refs = refs[:n], refs[n:2 * n]
        send_ref, recv_ref = refs[2 * n], refs[2 * n + 1]
        x, y, c = lax.axis_index("x"), lax.axis_index("y"), lax.axis_index("c")
        me = 4 * x + 2 * y + c
        for t in range(n):
            for k, pos, peer in _peers(x, y, c):
                cp = pltpu.make_async_remote_copy(
                    src_ref=_block_for(ins[t], peer, t < ng, shard_rows[t]), dst_ref=land_refs[t].at[me],
                    send_sem=send_ref.at[t * (N_DEV - 1) + k - 1], recv_sem=recv_ref.at[t * (N_DEV - 1) + k - 1],
                    device_id=pos, device_id_type=MESH)
                cp.wait_send()
                cp.wait_recv()

    res = pl.pallas_call(
        body, name=name, out_shape=tuple(pltpu.HBM(a.shape, a.dtype) for a in sent + lands),
        in_specs=[hbm] * (2 * n) + [sem, sem, pl.BlockSpec(memory_space=pl.ANY)], out_specs=tuple([hbm] * (2 * n)),
        input_output_aliases={i: i for i in range(2 * n)},
        compiler_params=pltpu.CompilerParams(has_side_effects=pltpu.SideEffectType.DATAFLOW_SIDE_EFFECTING),
    )(*sent, *lands, send_sems, recv_sems, after)
    return list(res[n:])


def _rms_scaled(h, g):
    return (h * lax.rsqrt(jnp.mean(h * h, axis=-1, keepdims=True) + EPS) * g).astype(BF16)


def _prenorm_tokens_side(x, g_pre, dm):
    bl, s, d = x.shape
    rows = _pick(s, 512, 16)
    tiles = [(b, j) for b in range(bl) for j in range(s // rows)]

    def work(ins, outs, scratch):
        (x_ref, g_ref), (u_ref,), (xbuf, ubuf, sem_in, sem_out) = ins, outs, scratch

        def load(t, slot):
            b, j = tiles[t]
            return pltpu.make_async_copy(x_ref.at[b, pl.ds(j * rows, rows), :], xbuf.at[slot], sem_in.at[slot])

        def store(t, slot):
            b, j = tiles[t]
            return pltpu.make_async_copy(ubuf.at[slot], u_ref.at[pl.ds(b * dm.LP + dm.TM + j * rows, rows), :], sem_out.at[slot])

        load(0, 0).start()
        for t in range(len(tiles)):
            slot = t % 2
            if t + 1 < len(tiles):
                load(t + 1, 1 - slot).start()
            load(t, slot).wait()
            if t >= 2:
                store(t - 2, slot).wait()
            ubuf[slot] = _rms_scaled(xbuf[slot], g_ref[...])
            store(t, slot).start()
        for t in range(max(len(tiles) - 2, 0), len(tiles)):
            store(t, t % 2).wait()

    any_spec = pl.BlockSpec(memory_space=pl.ANY)
    return (work, [x, g_pre], [any_spec, pl.BlockSpec(memory_space=pltpu.VMEM)],
            [jax.ShapeDtypeStruct((dm.T, d), BF16)], [any_spec],
            [pltpu.VMEM((2, rows, d), F32), pltpu.VMEM((2, rows, d), BF16), pltpu.SemaphoreType.DMA((2,)),
             pltpu.SemaphoreType.DMA((2,))])


def _prenorm_meta(u, metapad, g_pre, dm):
    tm, tps, d = dm.TM, dm.TPS, dm.D

    def body(u_in, mp_ref, g_ref, u_ref):
        u_ref[...] = _rms_scaled(mp_ref[...], g_ref[...])

    return pl.pallas_call(
        body, name="prenorm_meta", grid=(dm.Bl,),
        in_specs=[pl.BlockSpec(memory_space=pl.ANY), pl.BlockSpec((tm, d), lambda i: (0, 0)),
                  pl.BlockSpec((1, d), lambda i: (0, 0))],
        out_specs=pl.BlockSpec((tm, d), lambda i: (i * tps, 0)),
        out_shape=jax.ShapeDtypeStruct((dm.T, d), BF16), input_output_aliases={0: 0}, compiler_params=_cp(1),
    )(u, metapad, g_pre)


def _matmul_tn(a, b, out_dtype, name, tt=2304, tn=1024, tk=1024):
    t, k = a.shape
    n = b.shape[1]
    tt, tn, tk = _pick(t, tt, 16), _pick(n, tn, 128), _pick(k, tk, 128)
    nt = t // tt

    def body(a_ref, b_ref, o_ref, acc):
        p = _dot_tn(a_ref[...].astype(BF16), b_ref[...].astype(BF16))
        i = pl.program_id(2)

        @pl.when(i == 0)
        def _():
            acc[...] = p

        @pl.when(i > 0)
        def _():
            acc[...] += p

        @pl.when(i == nt - 1)
        def _():
            o_ref[...] = acc[...].astype(out_dtype)

    return pl.pallas_call(
        body, name=name, grid=(k // tk, n // tn, nt),
        in_specs=[pl.BlockSpec((tt, tk), lambda kk, j, i: (i, kk)), pl.BlockSpec((tt, tn), lambda kk, j, i: (i, j))],
        out_specs=pl.BlockSpec((tk, tn), lambda kk, j, i: (kk, j)),
        out_shape=jax.ShapeDtypeStruct((k, n), out_dtype), scratch_shapes=[pltpu.VMEM((tk, tn), F32)],
        compiler_params=_cp(3),
    )(a, b)


def _matmul_tn_group(a_list, b, moves, out_rows, extra, name, tt=2304, tile=1024):
    t, n = b.shape
    tt = _pick(t, tt, 16)
    nt = t // tt
    counts = [a.shape[1] // tile for a in a_list]
    starts = [sum(counts[:m]) for m in range(len(a_list))]
    items = sum(counts)
    extra_rows, extra_at = extra
    cuts = [[] for _ in range(items)]
    for row, rows, at in moves:
        while rows > 0:
            p, r = divmod(row, tile)
            take = min(rows, tile - r)
            cuts[p].append((r, take, at))
            row, rows, at = row + take, rows - take, at + take
    assert all(v % BF16_TILE_ROWS == 0 for cut in cuts for move in cut for v in move)
    assert sum(rows for _, rows, _ in moves) + extra_rows.shape[0] == out_rows

    def active(p, m):
        return (p >= starts[m]) & (p < starts[m] + counts[m])

    def body(*refs):
        a_refs, b_ref, x_ref = refs[:len(a_list)], refs[len(a_list)], refs[len(a_list) + 1]
        o_ref, acc, stage, sems, x_sem = refs[-5:]
        p, i = pl.program_id(0), pl.program_id(1)

        def writes(item):
            return [pltpu.make_async_copy(stage.at[pl.ds(r, rows), :], o_ref.at[pl.ds(at, rows), :], sems.at[s])
                    for s, (r, rows, at) in enumerate(cuts[item])]

        extra_copy = pltpu.make_async_copy(x_ref, o_ref.at[pl.ds(extra_at, extra_rows.shape[0]), :], x_sem.at[0])

        @pl.when((p == 0) & (i == 0))
        def _():
            extra_copy.start()

        for m, a_ref in enumerate(a_refs):
            @pl.when(active(p, m))
            def _(a_ref=a_ref):
                prod = _dot_tn(a_ref[...].astype(BF16), b_ref[...].astype(BF16))

                @pl.when(i == 0)
                def _():
                    acc[...] = prod

                @pl.when(i > 0)
                def _():
                    acc[...] += prod

        for item in range(items):
            @pl.when((p == item) & (i == nt - 1))
            def _(item=item):
                if item > 0:
                    for cp in writes(item - 1):
                        cp.wait()
                stage[...] = acc[...].astype(BF16)
                for cp in writes(item):
                    cp.start()
                if item == items - 1:
                    for cp in writes(item):
                        cp.wait()
                    extra_copy.wait()

    a_specs = [pl.BlockSpec((tt, tile), lambda p, i, m=m: (jnp.where(active(p, m), i, 0), jnp.where(active(p, m), p - starts[m], 0)))
               for m in range(len(a_list))]
    return pl.pallas_call(
        body, name=name, grid=(items, nt),
        in_specs=a_specs + [pl.BlockSpec((tt, n), lambda p, i: (i, 0)), pl.BlockSpec(memory_space=pltpu.VMEM)],
        out_specs=pl.BlockSpec(memory_space=pl.ANY), out_shape=jax.ShapeDtypeStruct((out_rows, n), BF16),
        scratch_shapes=[pltpu.VMEM((tile, n), F32), pltpu.VMEM((tile, n), BF16),
                        pltpu.SemaphoreType.DMA((max(len(cut) for cut in cuts),)), pltpu.SemaphoreType.DMA((1,))],
        compiler_params=_cp(2),
    )(*a_list, b, extra_rows)


BF16_TILE_ROWS = 16


def _shard_offset(index, shard_rows):
    return (index * shard_rows) % BF16_TILE_ROWS


def _padded_shard_rows(shard_rows):
    return -(-(shard_rows + max(_shard_offset(j, shard_rows) for j in range(N_DEV))) // BF16_TILE_ROWS) * BF16_TILE_ROWS


def _pad_shard(wt_shard, index):
    rows, d = wt_shard.shape
    return lax.dynamic_update_slice(jnp.zeros((_padded_shard_rows(rows), d), wt_shard.dtype), wt_shard,
                                    (_shard_offset(index, rows), 0))


def _shard_window(index, shard_rows):
    return index * shard_rows - _shard_offset(index, shard_rows)


def _packed_parts(dm):
    d, dk, hk, hv, cw, nj, hw = dm.D, dm.DK, dm.HK, dm.HV, dm.CW, dm.NJ, dm.HW
    blocks = [(0, (j * 4 + p) * cw, p * d + j * cw, cw) for j in range(nj) for p in range(4)]
    for h in range(HEADS):
        blocks += [(1, h * hw, 4 * d + h * hk, hk), (1, h * hw + hk, 4 * d + dk + h * hk, hk),
                   (1, h * hw + 2 * hk, 5 * d + h * hv, hv), (1, h * hw + 2 * hk + hv, 6 * d + h * hv, hv)]
    blocks += [(2, 0, 7 * d + 2 * RANK, 2 * d), (3, 0, 7 * d, 2 * RANK)]
    return [4 * d, 3 * d, 2 * d, LR_LANES], blocks


def _pack_plan(dm):
    sh = (9 * dm.D + 2 * RANK) // N_DEV
    tile = BF16_TILE_ROWS
    copies, straddles = [], []
    for part, dst, r0, n in _packed_parts(dm)[1]:
        for j in range(N_DEV):
            a, b = max(r0, sh * j), min(r0 + n, sh * (j + 1))
            if a >= b:
                continue
            a_up, b_down = -(-a // tile) * tile, b // tile * tile
            if b_down > a_up:
                copies.append((j, a_up - sh * j + _shard_offset(j, sh), b_down - a_up, part, dst + a_up - r0))
            if a % tile:
                lo = a // tile * tile
                straddles.append((j, lo - sh * (j - 1) + _shard_offset(j - 1, sh), part, dst + lo - r0, a - lo))
    return copies, straddles


def _packed_scratch(dm):
    copies, straddles = _pack_plan(dm)
    return ([pltpu.VMEM((rows, dm.D), BF16) for rows in _packed_parts(dm)[0]]
            + [pltpu.VMEM((2 * max(len(straddles), 1), BF16_TILE_ROWS, dm.D), BF16),
               pltpu.SemaphoreType.DMA((len(copies) + 2 * len(straddles),))])


def _load_packed(g_ref, parts, edges, sems, dm):
    copies, straddles = _pack_plan(dm)
    tile = BF16_TILE_ROWS
    parts[3][2 * RANK:, :] = jnp.zeros((LR_LANES - 2 * RANK, dm.D), BF16)
    dmas = [pltpu.make_async_copy(g_ref.at[j, pl.ds(src, n), :], parts[p].at[pl.ds(dst, n), :], sems.at[i])
            for i, (j, src, n, p, dst) in enumerate(copies)]
    for i, (j, src, p, dst, split) in enumerate(straddles):
        k = len(copies) + 2 * i
        dmas.append(pltpu.make_async_copy(g_ref.at[j - 1, pl.ds(src, tile), :], edges.at[2 * i], sems.at[k]))
        dmas.append(pltpu.make_async_copy(g_ref.at[j, pl.ds(0, tile), :], edges.at[2 * i + 1], sems.at[k + 1]))
    for cp in dmas:
        cp.start()
    for cp in dmas:
        cp.wait()
    row = lax.broadcasted_iota(jnp.int32, (tile, dm.D), 0)
    for i, (j, src, p, dst, split) in enumerate(straddles):
        parts[p][dst:dst + tile, :] = jnp.where(row < split, edges[2 * i], edges[2 * i + 1])


def _inproj(u, gathered, dm):
    t, d = u.shape
    tm = _pick(t, 512, 16)
    widths = _packed_parts(dm)[0]
    cn = 1024

    def body(u_ref, g_ref, *rest):
        outs, parts, (edges, sems) = rest[:4], rest[4:8], rest[8:]

        @pl.when(pl.program_id(0) == 0)
        def _():
            _load_packed(g_ref, parts, edges, sems, dm)

        ut = u_ref[...]
        for w, o_ref in zip(parts, outs):
            n = w.shape[0]
            step = cn if n % cn == 0 else n
            for j in range(0, n, step):
                o_ref[:, j:j + step] = _dot_nt(ut, w[j:j + step, :]).astype(BF16)

    return pl.pallas_call(
        body, name="inproj", grid=(t // tm,),
        in_specs=[pl.BlockSpec((tm, d), lambda i: (i, 0)), pl.BlockSpec(memory_space=pl.ANY)],
        out_specs=[pl.BlockSpec((tm, w), lambda i: (i, 0)) for w in widths],
        out_shape=[jax.ShapeDtypeStruct((t, w), BF16) for w in widths],
        scratch_shapes=_packed_scratch(dm), compiler_params=_cp(1),
    )(u, gathered)


def _conv_rows(dm):
    return _pick(dm.LP, 256, 16)


def _shifted(m, prev_row, next_row, rows):
    row = lax.broadcasted_iota(jnp.int32, m.shape, 0)
    m_prev = jnp.where(row == 0, prev_row, pltpu.roll(m, 1, 0))
    m_next = jnp.where(row == rows - 1, next_row, pltpu.roll(m, rows - 1, 0))
    return m_prev, m_next


def _conv_fwd(proj_a, conv_w, dm):
    lp, cw, rc = dm.LP, dm.CW, _conv_rows(dm)
    nchunk = lp // rc

    def body(p_ref, w_ref, y_ref):
        w0, w1, w2 = w_ref[0:1, :], w_ref[1:2, :], w_ref[2:3, :]

        def chunk(ci, carry):
            r0 = pl.multiple_of(ci * rc, rc)
            blk = p_ref[pl.ds(r0, rc), :].astype(F32)
            cb, cc, cx, cz = (blk[:, i * cw:(i + 1) * cw] for i in range(4))
            m = cc * cx
            rp = pl.multiple_of(jnp.maximum(r0 - 16, 0), 16)
            rn = pl.multiple_of(jnp.minimum(r0 + rc, lp - 16), 16)
            pv = p_ref[pl.ds(rp, 16), cw:3 * cw].astype(F32)
            nx = p_ref[pl.ds(rn, 16), cw:3 * cw].astype(F32)
            prev_row = jnp.where(ci > 0, pv[15:16, :cw] * pv[15:16, cw:], 0.0)
            next_row = jnp.where(ci < nchunk - 1, nx[0:1, :cw] * nx[0:1, cw:], 0.0)
            m_prev, m_next = _shifted(m, prev_row, next_row, rc)
            s = w0 * m_prev + w1 * m + w2 * m_next
            y_ref[pl.ds(r0, rc), :] = (cb * s * (cz * _sigmoid(cz))).astype(BF16)
            return carry

        lax.fori_loop(0, nchunk, chunk, 0)

    return pl.pallas_call(
        body, name="conv_fwd", grid=(dm.Bl, dm.NJ),
        in_specs=[pl.BlockSpec((lp, 4 * cw), lambda s, j: (s, j)), pl.BlockSpec((3, cw), lambda s, j: (0, j))],
        out_specs=pl.BlockSpec((lp, cw), lambda s, j: (s, j)),
        out_shape=jax.ShapeDtypeStruct((dm.T, dm.D), BF16), compiler_params=_cp(2),
    )(proj_a, conv_w)


def _conv_bwd(proj_a, dy_conv, conv_w, dm):
    lp, cw, rc = dm.LP, dm.CW, _conv_rows(dm)
    nchunk = lp // rc

    def body(p_ref, dy_ref, w_ref, d_ref, gw_ref):
        w0, w1, w2 = w_ref[0:1, :], w_ref[1:2, :], w_ref[2:3, :]

        def ds_of(p4, dy):
            cb, cz = p4[:, :cw], p4[:, 3 * cw:]
            return dy * cb * (cz * _sigmoid(cz))

        def chunk(ci, carry):
            g0, g1, g2 = carry
            r0 = pl.multiple_of(ci * rc, rc)
            blk = p_ref[pl.ds(r0, rc), :].astype(F32)
            dy = dy_ref[pl.ds(r0, rc), :].astype(F32)
            cb, cc, cx, cz = (blk[:, i * cw:(i + 1) * cw] for i in range(4))
            rp = pl.multiple_of(jnp.maximum(r0 - 16, 0), 16)
            rn = pl.multiple_of(jnp.minimum(r0 + rc, lp - 16), 16)
            pv = p_ref[pl.ds(rp, 16), :].astype(F32)[15:16]
            nx = p_ref[pl.ds(rn, 16), :].astype(F32)[0:1]
            dpv = dy_ref[pl.ds(rp, 16), :].astype(F32)[15:16]
            dnx = dy_ref[pl.ds(rn, 16), :].astype(F32)[0:1]
            has_prev, has_next = ci > 0, ci < nchunk - 1
            m = cc * cx
            m_prev, m_next = _shifted(m, jnp.where(has_prev, pv[:, cw:2 * cw] * pv[:, 2 * cw:3 * cw], 0.0),
                                      jnp.where(has_next, nx[:, cw:2 * cw] * nx[:, 2 * cw:3 * cw], 0.0), rc)
            s = w0 * m_prev + w1 * m + w2 * m_next
            sg = _sigmoid(cz)
            silu = cz * sg
            ds = dy * cb * silu
            ds_prev, ds_next = _shifted(ds, jnp.where(has_prev, ds_of(pv, dpv), 0.0),
                                        jnp.where(has_next, ds_of(nx, dnx), 0.0), rc)
            dm_ = w0 * ds_next + w1 * ds + w2 * ds_prev
            d_ref[pl.ds(r0, rc), 0:cw] = (dy * s * silu).astype(BF16)
            d_ref[pl.ds(r0, rc), cw:2 * cw] = (dm_ * cx).astype(BF16)
            d_ref[pl.ds(r0, rc), 2 * cw:3 * cw] = (dm_ * cc).astype(BF16)
            d_ref[pl.ds(r0, rc), 3 * cw:4 * cw] = (dy * cb * s * (sg * (1.0 + cz * (1.0 - sg)))).astype(BF16)
            return (g0 + jnp.sum(ds * m_prev, axis=0, keepdims=True), g1 + jnp.sum(ds * m, axis=0, keepdims=True),
                    g2 + jnp.sum(ds * m_next, axis=0, keepdims=True))

        z = jnp.zeros((1, cw), F32)
        g0, g1, g2 = lax.fori_loop(0, nchunk, chunk, (z, z, z))

        @pl.when(pl.program_id(1) == 0)
        def _():
            gw_ref[...] = jnp.zeros_like(gw_ref)

        gw_ref[0:1, :] += g0
        gw_ref[1:2, :] += g1
        gw_ref[2:3, :] += g2

    return pl.pallas_call(
        body, name="conv_bwd", grid=(dm.NJ, dm.Bl),
        in_specs=[pl.BlockSpec((lp, 4 * cw), lambda j, s: (s, j)), pl.BlockSpec((lp, cw), lambda j, s: (s, j)),
                  pl.BlockSpec((3, cw), lambda j, s: (0, j))],
        out_specs=[pl.BlockSpec((lp, 4 * cw), lambda j, s: (s, j)), pl.BlockSpec((8, cw), lambda j, s: (0, j))],
        out_shape=[jax.ShapeDtypeStruct((dm.T, 4 * dm.D), BF16), jax.ShapeDtypeStruct((8, dm.D), F32)],
        compiler_params=_cp(2),
    )(proj_a, dy_conv, conv_w)


def _interleave(gens):
    results = [None] * len(gens)
    live = list(range(len(gens)))
    while live:
        for idx in list(live):
            try:
                next(gens[idx])
            except StopIteration as done:
                results[idx] = done.value
                live.remove(idx)
    return results


def _group_chunks(dm):
    n = dm.NC - dm.C0
    return 3 if n % 3 == 0 else 1


def _group_masks(rows):
    ii = lax.broadcasted_iota(jnp.int32, (rows, rows), 0)
    jj = lax.broadcasted_iota(jnp.int32, (rows, rows), 1)
    same = jnp.right_shift(ii, CHUNK_SHIFT) == jnp.right_shift(jj, CHUNK_SHIFT)
    return same & (jj <= ii), same & (jj > ii)


def _first_row(chunk):
    return chunk * CHUNK if isinstance(chunk, int) else pl.multiple_of(chunk * CHUNK, CHUNK)


def _chunk_totals(b, fwd):
    hk = b.shape[1]
    rows = [b[c * CHUNK + CHUNK - 1:(c + 1) * CHUNK] if fwd else b[c * CHUNK:c * CHUNK + 1]
            for c in range(b.shape[0] // CHUNK)]
    return jnp.concatenate([jnp.broadcast_to(r, (CHUNK, hk)) for r in rows], axis=0)


def _log_gate(lr_rows, w_ref, b_ref, first_group, hk):
    z = _dot(lr_rows, w_ref[...]) + b_ref[...]
    e = jnp.exp(-jnp.abs(z))
    g = (jnp.minimum(z, 0.0) - jnp.log(1.0 + e)) * (1.0 / GATE_NORMALIZER)
    dg_dz = jnp.where(z >= 0.0, e, 1.0) / (1.0 + e) * (1.0 / GATE_NORMALIZER)
    row = lax.broadcasted_iota(jnp.int32, (lr_rows.shape[0], hk), 0)
    pad = first_group & (row < PAD_ROWS)
    return jnp.where(pad, 0.0, g), jnp.where(pad, 0.0, dg_dz)


def _gla_fwd(proj_b, lr, wg_f, bg_f, wg_b, bg_b, gla_g, dm):
    lp, hk, hv, nc, c0, hw = dm.LP, dm.HK, dm.HV, dm.NC, dm.C0, dm.HW
    scale = hk ** -0.5
    gc = _group_chunks(dm)
    gr, ng = gc * CHUNK, (nc - c0) // gc

    def body(p_ref, lr_ref, wf_ref, bf_ref, wb_ref, bb_ref, gg_ref, o_ref, y_ref, st_ref, b_out, gs_out, oacc_f, oacc_b):
        low_incl, up_strict = _group_masks(gr)
        if c0 > 0:
            zr = c0 * CHUNK
            o_ref[0:zr, :] = jnp.zeros((zr, hv), BF16)
            y_ref[0:zr, :] = jnp.zeros((zr, hv), BF16)
            b_out[:, 0:zr, :] = jnp.zeros((2, zr, hk), F32)
            gs_out[:, 0:zr, :] = jnp.zeros((2, zr, hk), F32)
            st_ref[0, 0, :, 0:c0] = jnp.zeros((2, c0, hv, hk), BF16)

        def decay(gi, fwd):
            w_ref, b_ref = (wf_ref, bf_ref) if fwd else (wb_ref, bb_ref)
            r0 = _first_row(c0 + gi * gc)
            yield
            g, dg_dz = _log_gate(lr_ref[pl.ds(r0, gr), :], w_ref, b_ref, gi == 0, hk)
            gs_out[0 if fwd else 1, pl.ds(r0, gr), :] = dg_dz
            yield
            b = _chunk_cumsum(g, not fwd)
            b_out[0 if fwd else 1, pl.ds(r0, gr), :] = b
            return b

        def group(gi, st, b, fwd):
            oacc = oacc_f if fwd else oacc_b
            r0 = pl.multiple_of((c0 + gi * gc) * CHUNK, CHUNK)
            blk = p_ref[pl.ds(r0, gr), :]
            q = blk[:, :hk].astype(F32) * scale
            k = blk[:, hk:2 * hk].astype(F32)
            v = blk[:, 2 * hk:2 * hk + hv]
            btot = _chunk_totals(b, fwd)
            qi = (q * jnp.exp(b)).astype(BF16)
            ki = (k * jnp.exp(-b)).astype(BF16)
            kd = (k * jnp.exp(btot - b)).astype(BF16)
            dec = jnp.exp(btot)
            a = _dot_nt(qi, ki)
            yield
            o = _dot(jnp.where(low_incl if fwd else up_strict, a, 0.0).astype(BF16), v)
            chunk_rows = [slice(c * CHUNK, (c + 1) * CHUNK) for c in range(gc)]
            kv = [_dot_tn(v[rows], kd[rows]) for rows in chunk_rows]
            for c in (range(gc) if fwd else reversed(range(gc))):
                yield
                rows = chunk_rows[c]
                st_b = st.astype(BF16)
                st_ref[0, 0, 0 if fwd else 1, c0 + gi * gc + c] = st_b
                oacc[pl.ds(r0 + c * CHUNK, CHUNK), :] = o[rows] + _dot_nt(qi[rows], st_b)
                st = st * dec[c * CHUNK:c * CHUNK + 1] + kv[c]
            return st

        def step(i, carry):
            st_f, st_b, b_f, b_b = carry
            gf, gb = i, ng - 1 - i
            return tuple(_interleave([group(gf, st_f, b_f, True), group(gb, st_b, b_b, False),
                                      decay(jnp.minimum(gf + 1, ng - 1), True), decay(jnp.maximum(gb - 1, 0), False)]))

        zero = jnp.zeros((hv, hk), F32)
        lax.fori_loop(0, ng, step, (zero, zero, *_interleave([decay(0, True), decay(ng - 1, False)])))

        def finish(i, carry):
            r0 = pl.multiple_of((c0 + i * gc) * CHUNK, CHUNK)
            o = oacc_f[pl.ds(r0, gr), :] + oacc_b[pl.ds(r0, gr), :]
            r = p_ref[pl.ds(r0, gr), 2 * hk + hv:].astype(F32)
            on = o * lax.rsqrt(jnp.mean(o * o, axis=-1, keepdims=True) + EPS) * gg_ref[...]
            o_ref[pl.ds(r0, gr), :] = o.astype(BF16)
            y_ref[pl.ds(r0, gr), :] = (on * r * _sigmoid(r)).astype(BF16)
            return carry

        lax.fori_loop(0, ng, finish, 0)

    head = lambda s, h: (s, h)
    wspec = pl.BlockSpec((LR_LANES, hk), lambda s, h: (0, h))
    bspec = pl.BlockSpec((1, hk), lambda s, h: (0, h))
    return pl.pallas_call(
        body, name="gla_fwd", grid=(dm.Bl, HEADS),
        in_specs=[pl.BlockSpec((lp, hw), head), pl.BlockSpec((lp, LR_LANES), lambda s, h: (s, 0)),
                  wspec, bspec, wspec, bspec, pl.BlockSpec((1, hv), lambda s, h: (0, 0))],
        out_specs=[pl.BlockSpec((lp, hv), head), pl.BlockSpec((lp, hv), head),
                   pl.BlockSpec((1, 1, 2, nc, hv, hk), lambda s, h: (s, h, 0, 0, 0, 0)),
                   pl.BlockSpec((2, lp, hk), lambda s, h: (0, s, h)), pl.BlockSpec((2, lp, hk), lambda s, h: (0, s, h))],
        out_shape=[jax.ShapeDtypeStruct((dm.T, dm.DV), BF16), jax.ShapeDtypeStruct((dm.T, dm.DV), BF16),
                   jax.ShapeDtypeStruct((dm.Bl, HEADS, 2, nc, hv, hk), BF16),
                   jax.ShapeDtypeStruct((2, dm.T, dm.DK), F32), jax.ShapeDtypeStruct((2, dm.T, dm.DK), F32)],
        scratch_shapes=[pltpu.VMEM((lp, hv), F32), pltpu.VMEM((lp, hv), F32)],
        compiler_params=_cp(2),
    )(proj_b, lr, wg_f, bg_f, wg_b, bg_b, gla_g)


def _gla_bwd(proj_b, lr, o_all, dy_gla, states, decays, gate_slopes, wg_f, wg_b, gla_g, dm):
    lp, hk, hv, nc, c0, hw = dm.LP, dm.HK, dm.HV, dm.NC, dm.C0, dm.HW
    scale = hk ** -0.5
    gc = _group_chunks(dm)
    gr, ng = gc * CHUNK, (nc - c0) // gc

    def body(p_ref, lr_ref, o_ref, dy_ref, st_ref, b_ref, gs_ref, wf_ref, wb_ref, gg_ref,
             d_ref, dlr_ref, gwf_ref, gbf_ref, gwb_ref, gbb_ref, ggg_ref, do_s, dq_s, dk_s, dv_s, dz_s):
        low_incl, up_strict = _group_masks(gr)
        h = pl.program_id(1)

        @pl.when(h == 0)
        def _():
            dlr_ref[...] = jnp.zeros_like(dlr_ref)

        if c0 > 0:
            zr = c0 * CHUNK
            d_ref[0:zr, :] = jnp.zeros((zr, hw), BF16)
        for acc in (dq_s, dk_s, dv_s):
            acc[...] = jnp.zeros_like(acc)

        def norm_bwd(i, ggg):
            r0 = pl.multiple_of((c0 + i * gc) * CHUNK, CHUNK)
            o = o_ref[pl.ds(r0, gr), :].astype(F32)
            dy = dy_ref[pl.ds(r0, gr), :].astype(F32)
            r = p_ref[pl.ds(r0, gr), 2 * hk + hv:].astype(F32)
            rstd = lax.rsqrt(jnp.mean(o * o, axis=-1, keepdims=True) + EPS)
            ohat = o * rstd
            sg = _sigmoid(r)
            d_on = dy * (r * sg)
            d_ref[pl.ds(r0, gr), 2 * hk + hv:] = (dy * ohat * gg_ref[...] * (sg * (1.0 + r * (1.0 - sg)))).astype(BF16)
            d_oh = d_on * gg_ref[...]
            do_s[pl.ds(r0, gr), :] = (rstd * (d_oh - ohat * jnp.mean(d_oh * ohat, axis=-1, keepdims=True))).astype(BF16)
            return ggg + jnp.sum(d_on * ohat, axis=0, keepdims=True)

        ggg = lax.fori_loop(0, ng, norm_bwd, jnp.zeros((1, hv), F32))

        @pl.when((pl.program_id(0) == 0) & (h == 0))
        def _():
            ggg_ref[...] = jnp.zeros_like(ggg_ref)

        ggg_ref[0:1, :] += ggg

        def load(gi):
            r0 = pl.multiple_of((c0 + gi * gc) * CHUNK, CHUNK)
            blk = p_ref[pl.ds(r0, gr), :]
            return r0, blk[:, :hk].astype(F32) * scale, blk[:, hk:2 * hk].astype(F32), blk[:, 2 * hk:2 * hk + hv]

        zero = jnp.zeros((hv, hk), F32)

        def grad(gi, carry, fwd):
            dst, gb = carry
            way = 0 if fwd else 1
            mask = low_incl if fwd else up_strict
            r0, q, k, v = load(gi)
            b = b_ref[way, pl.ds(r0, gr), :]
            btot = _chunk_totals(b, fwd)
            eb, enb, edb, dec = jnp.exp(b), jnp.exp(-b), jnp.exp(btot - b), jnp.exp(btot)
            qi_f, ki_f, kd_f = q * eb, k * enb, k * edb
            qi, ki, kd = qi_f.astype(BF16), ki_f.astype(BF16), kd_f.astype(BF16)
            do = do_s[pl.ds(r0, gr), :]
            a = _dot_nt(qi, ki)
            da = _dot_nt(do, v)
            yield
            a = jnp.where(mask, a, 0.0).astype(BF16)
            da = jnp.where(mask, da, 0.0).astype(BF16)
            dv = _dot_tn(a, do)
            dqi = _dot(da, ki)
            dki = _dot_tn(da, qi)
            dv_c, dqi_c, dkd_c, extra_c = [None] * gc, [None] * gc, [None] * gc, [None] * gc
            chunk_rows = [slice(c * CHUNK, (c + 1) * CHUNK) for c in range(gc)]
            qdo = [_dot_tn(do[rows], qi[rows]) for rows in chunk_rows]
            for c in (reversed(range(gc)) if fwd else range(gc)):
                yield
                rows = chunk_rows[c]
                st = st_ref[0, 0, way, c0 + gi * gc + c]
                dsn_b = dst.astype(BF16)
                dec_c = dec[c * CHUNK:c * CHUNK + 1]
                dv_c[c] = dv[rows] + _dot_nt(kd[rows], dsn_b)
                dqi_c[c] = dqi[rows] + _dot(do[rows], st)
                dkd_c[c] = _dot(v[rows], dsn_b)
                ddec = jnp.sum(st.astype(F32) * dst, axis=0, keepdims=True)
                extra = jnp.sum(dkd_c[c] * kd_f[rows], axis=0, keepdims=True) + ddec * dec_c
                extra_c[c] = jnp.broadcast_to(extra, (CHUNK, hk))
                dst = dst * dec_c + qdo[c]
            yield
            dv, dqi = jnp.concatenate(dv_c, axis=0), jnp.concatenate(dqi_c, axis=0)
            dkd, extra = jnp.concatenate(dkd_c, axis=0), jnp.concatenate(extra_c, axis=0)
            dq_s[pl.ds(r0, gr), :] += dqi * eb * scale
            dk_s[pl.ds(r0, gr), :] += dki * enb + dkd * edb
            dv_s[pl.ds(r0, gr), :] += dv
            db = dqi * qi_f - dki * ki_f - dkd * kd_f
            dg = _chunk_cumsum(db, fwd) + extra
            yield
            dz = dg * gs_ref[way, pl.ds(r0, gr), :]
            dz_s[way, pl.ds(r0, gr), :] = dz.astype(BF16)
            return dst, gb + jnp.sum(dz, axis=0, keepdims=True)

        def grad_step(i, carry):
            return tuple(_interleave([grad(ng - 1 - i, carry[0], True), grad(i, carry[1], False)]))

        init = (zero, jnp.zeros((1, hk), F32))
        (_, gb_f), (_, gb_b) = lax.fori_loop(0, ng, grad_step, (init, init))
        used = slice(c0 * CHUNK, lp)
        for way, (w_ref, gw_ref, gb_ref, gb) in enumerate(((wf_ref, gwf_ref, gbf_ref, gb_f), (wb_ref, gwb_ref, gbb_ref, gb_b))):
            dlr_ref[used, :] += _dot_nt(dz_s[way, used, :], w_ref[...])
            gw_ref[0] = _dot_tn(lr_ref[used, :], dz_s[way, used, :])
            gb_ref[0] = jnp.zeros((8, hk), F32)
            gb_ref[0, 0:1, :] = gb

        def combine(i, carry):
            r0 = pl.multiple_of((c0 + i * gc) * CHUNK, CHUNK)
            d_ref[pl.ds(r0, gr), 0:hk] = dq_s[pl.ds(r0, gr), :].astype(BF16)
            d_ref[pl.ds(r0, gr), hk:2 * hk] = dk_s[pl.ds(r0, gr), :].astype(BF16)
            d_ref[pl.ds(r0, gr), 2 * hk:2 * hk + hv] = dv_s[pl.ds(r0, gr), :].astype(BF16)
            return carry

        lax.fori_loop(0, ng, combine, 0)

    head = lambda s, h: (s, h)
    wspec = pl.BlockSpec((LR_LANES, hk), lambda s, h: (0, h))
    gwspec = pl.BlockSpec((1, LR_LANES, hk), lambda s, h: (s, 0, h))
    gbspec = pl.BlockSpec((1, 8, hk), lambda s, h: (s, 0, h))
    gw_shape = jax.ShapeDtypeStruct((dm.Bl, LR_LANES, dm.DK), F32)
    gb_shape = jax.ShapeDtypeStruct((dm.Bl, 8, dm.DK), F32)
    both = pl.BlockSpec((2, lp, hk), lambda s, h: (0, s, h))
    return pl.pallas_call(
        body, name="gla_bwd", grid=(dm.Bl, HEADS),
        in_specs=[pl.BlockSpec((lp, hw), head), pl.BlockSpec((lp, LR_LANES), lambda s, h: (s, 0)),
                  pl.BlockSpec((lp, hv), head), pl.BlockSpec((lp, hv), head),
                  pl.BlockSpec((1, 1, 2, nc, hv, hk), lambda s, h: (s, h, 0, 0, 0, 0)), both, both,
                  wspec, wspec, pl.BlockSpec((1, hv), lambda s, h: (0, 0))],
        out_specs=[pl.BlockSpec((lp, hw), head), pl.BlockSpec((lp, LR_LANES), lambda s, h: (s, 0)),
                   gwspec, gbspec, gwspec, gbspec, pl.BlockSpec((8, hv), lambda s, h: (0, 0))],
        out_shape=[jax.ShapeDtypeStruct((dm.T, HEADS * hw), BF16), jax.ShapeDtypeStruct((dm.T, LR_LANES), F32),
                   gw_shape, gb_shape, gw_shape, gb_shape, jax.ShapeDtypeStruct((8, hv), F32)],
        scratch_shapes=[pltpu.VMEM((lp, hv), BF16), pltpu.VMEM((lp, hk), F32), pltpu.VMEM((lp, hk), F32),
                        pltpu.VMEM((lp, hv), F32), pltpu.VMEM((2, lp, hk), BF16)],
        compiler_params=_cp(2),
    )(proj_b, lr, o_all, dy_gla, states, decays, gate_slopes, wg_f, wg_b, gla_g)


def _stream_tiles(n_tiles, loads, stores, compute):
    for cp in loads(0, 0):
        cp.start()

    def step(t, carry):
        slot = t % 2

        @pl.when(t + 1 < n_tiles)
        def _():
            for cp in loads(t + 1, 1 - slot):
                cp.start()

        for cp in loads(t, slot):
            cp.wait()

        @pl.when(t >= 2)
        def _():
            for cp in stores(t - 2, slot):
                cp.wait()

        compute(t, slot)
        for cp in stores(t, slot):
            cp.start()
        return carry

    lax.fori_loop(0, n_tiles, step, 0)
    for t in range(max(n_tiles - 2, 0), n_tiles):
        for cp in stores(t, t % 2):
            cp.wait()


def _token_tiles(dm, target_rows=512):
    rows = _pick(dm.S, target_rows, 16)
    per_seq = dm.S // rows
    return rows, dm.Bl * per_seq, lambda t: pl.multiple_of((t // per_seq) * dm.LP + dm.TM + (t % per_seq) * rows, 16)


def _head(y_conv, y_gla, proj_c, w_oc, w_og, w_out, x, target, g_post, dm):
    d, tm = dm.D, dm.TM
    rows, n_tiles, first_row = _token_tiles(dm, 256)
    n_out = 8

    def body(*refs):
        yc_hbm, yg_hbm, c_hbm, woc_ref, wog_ref, wo_ref, x_hbm, t_hbm, g_ref = refs[:9]
        outs, st_ref = refs[9:9 + n_out], refs[9 + n_out]
        ycbuf, ygbuf, cbuf, xbuf, tbuf = refs[10 + n_out:15 + n_out]
        obufs = refs[15 + n_out:15 + 2 * n_out]
        zbuf, zbuf2, sem_in, sem_out, sem_zero = refs[15 + 2 * n_out:]

        def loads(t, slot):
            padded = [(yc_hbm, ycbuf), (yg_hbm, ygbuf), (c_hbm, cbuf)]
            own = [(x_hbm, xbuf), (t_hbm, tbuf)]
            return ([pltpu.make_async_copy(h.at[pl.ds(first_row(t), rows), :], b.at[slot], sem_in.at[i, slot])
                     for i, (h, b) in enumerate(padded)] +
                    [pltpu.make_async_copy(h.at[pl.ds(t * rows, rows), :], b.at[slot], sem_in.at[3 + i, slot])
                     for i, (h, b) in enumerate(own)])

        def stores(t, slot):
            return [pltpu.make_async_copy(b.at[slot], h.at[pl.ds(first_row(t), rows), :], sem_out.at[i, slot])
                    for i, (h, b) in enumerate(zip(outs, obufs))]

        def compute(t, slot):
            mg_o, do_o, dy_o, dpc_o, dpg_o, dc_o, dyc_o, dyg_o = obufs
            pc = _dot(ycbuf[slot], woc_ref[...])
            pg = _dot(ygbuf[slot], wog_ref[...])
            sa = _sigmoid(cbuf[slot, :, :d].astype(F32))
            sb = _sigmoid(cbuf[slot, :, d:].astype(F32))
            merged = (sa * pc + sb * pg).astype(BF16)
            mg_o[slot] = merged
            out = _dot(merged, wo_ref[...])
            rstd = lax.rsqrt(jnp.mean(out * out, axis=-1, keepdims=True) + EPS)
            ohat = out * rstd
            err = xbuf[slot] + ohat * g_ref[...] - tbuf[slot]
            dy = err * (1.0 / d)
            d_oh = dy * g_ref[...]
            d_out = (rstd * (d_oh - ohat * jnp.mean(d_oh * ohat, axis=-1, keepdims=True))).astype(BF16)
            do_o[slot] = d_out
            dy_o[slot] = dy.astype(BF16)
            st_ref[0:1, :] += jnp.sum(dy * ohat, axis=0, keepdims=True)
            st_ref[1:2, :] += jnp.sum(err * err, axis=0, keepdims=True)
            dmg = _dot_nt(d_out, wo_ref[...])
            dpc = (dmg * sa).astype(BF16)
            dpg = (dmg * sb).astype(BF16)
            dpc_o[slot] = dpc
            dpg_o[slot] = dpg
            dc_o[slot, :, :d] = (dmg * pc * sa * (1.0 - sa)).astype(BF16)
            dc_o[slot, :, d:] = (dmg * pg * sb * (1.0 - sb)).astype(BF16)
            dyc_o[slot] = _dot_nt(dpc, woc_ref[...]).astype(BF16)
            dyg_o[slot] = _dot_nt(dpg, wog_ref[...]).astype(BF16)

        st_ref[...] = jnp.zeros_like(st_ref)
        zbuf[...] = jnp.zeros_like(zbuf)
        zbuf2[...] = jnp.zeros_like(zbuf2)
        zeros = [pltpu.make_async_copy(zbuf2 if out.shape[1] == 2 * d else zbuf, out.at[pl.ds(b * dm.LP, tm), :], sem_zero.at[i, b])
                 for i, out in enumerate(outs) for b in range(dm.Bl)]
        for cp in zeros:
            cp.start()
        _stream_tiles(n_tiles, loads, stores, compute)
        for cp in zeros:
            cp.wait()

    any_spec, vmem = pl.BlockSpec(memory_space=pl.ANY), pl.BlockSpec(memory_space=pltpu.VMEM)
    widths = [d, d, d, d, d, 2 * d, d, d]
    tile = lambda w, dt: pltpu.VMEM((2, rows, w), dt)
    return pl.pallas_call(
        body, name="head", in_specs=[any_spec] * 3 + [vmem] * 3 + [any_spec] * 2 + [vmem],
        out_specs=[any_spec] * n_out + [vmem],
        out_shape=[jax.ShapeDtypeStruct((dm.T, w), BF16) for w in widths] + [jax.ShapeDtypeStruct((8, d), F32)],
        scratch_shapes=[tile(d, BF16), tile(d, BF16), tile(2 * d, BF16), tile(d, F32), tile(d, F32)]
        + [tile(w, BF16) for w in widths]
        + [pltpu.VMEM((tm, d), BF16), pltpu.VMEM((tm, 2 * d), BF16), pltpu.SemaphoreType.DMA((5, 2)),
           pltpu.SemaphoreType.DMA((n_out, 2)), pltpu.SemaphoreType.DMA((n_out, dm.Bl))],
        compiler_params=pltpu.CompilerParams(vmem_limit_bytes=VMEM_LIMIT_BYTES),
    )(y_conv, y_gla, proj_c, w_oc, w_og, w_out, x.reshape(dm.Bl * dm.S, d), target.reshape(dm.Bl * dm.S, d), g_post)


def _grad_h(d_parts, gathered, dy, x, metapad, g_pre, dm):
    d, tm = dm.D, dm.TM
    rows, n_tiles, first_row = _token_tiles(dm, 256)
    widths = [a.shape[1] for a in d_parts]
    np_ = len(d_parts)

    def body(*refs):
        d_hbm, g_hbm, dy_hbm, x_hbm, mp_ref, g_ref = refs[:np_], refs[np_], refs[np_ + 1], refs[np_ + 2], refs[np_ + 3], refs[np_ + 4]
        gx_hbm, dmeta_ref, gg_ref = refs[np_ + 5:np_ + 8]
        parts, edges, sems = refs[np_ + 8:np_ + 12], refs[np_ + 12], refs[np_ + 13]
        dbufs = refs[np_ + 14:2 * np_ + 14]
        dybuf, xbuf, gbuf = refs[2 * np_ + 14:2 * np_ + 17]
        mbufs = refs[2 * np_ + 17:3 * np_ + 17]
        sem_in, sem_out, sem_meta = refs[3 * np_ + 17:]

        def grad_u(tiles):
            du = _dot(tiles[0].astype(BF16), parts[0][...])
            for a, w in zip(tiles[1:], parts[1:]):
                du = du + _dot(a.astype(BF16), w[...])
            return du

        def norm_bwd(h, du, dy):
            rstd = lax.rsqrt(jnp.mean(h * h, axis=-1, keepdims=True) + EPS)
            hhat = h * rstd
            dug = du * g_ref[...]
            gg_ref[0:1, :] += jnp.sum(du * hhat, axis=0, keepdims=True)
            return dy + rstd * (dug - hhat * jnp.mean(dug * hhat, axis=-1, keepdims=True))

        def loads(t, slot):
            padded = list(zip(d_hbm, dbufs)) + [(dy_hbm, dybuf)]
            return ([pltpu.make_async_copy(h.at[pl.ds(first_row(t), rows), :], b.at[slot], sem_in.at[i, slot])
                     for i, (h, b) in enumerate(padded)] +
                    [pltpu.make_async_copy(x_hbm.at[pl.ds(t * rows, rows), :], xbuf.at[slot], sem_in.at[np_ + 1, slot])])

        def stores(t, slot):
            return [pltpu.make_async_copy(gbuf.at[slot], gx_hbm.at[pl.ds(t * rows, rows), :], sem_out.at[slot])]

        def compute(t, slot):
            gbuf[slot] = norm_bwd(xbuf[slot], grad_u([b[slot] for b in dbufs]), dybuf[slot].astype(F32))

        gg_ref[...] = jnp.zeros_like(gg_ref)
        meta = [pltpu.make_async_copy(h.at[pl.ds(b * dm.LP, tm), :], buf.at[pl.ds(b * tm, tm), :], sem_meta.at[i, b])
                for i, (h, buf) in enumerate(zip(d_hbm, mbufs)) for b in range(dm.Bl)]
        for cp in meta:
            cp.start()
        _load_packed(g_hbm, parts, edges, sems, dm)
        _stream_tiles(n_tiles, loads, stores, compute)
        for cp in meta:
            cp.wait()
        dmeta_ref[...] = norm_bwd(jnp.concatenate([mp_ref[...]] * dm.Bl, axis=0), grad_u([buf[...] for buf in mbufs]), 0.0)

    any_spec, vmem = pl.BlockSpec(memory_space=pl.ANY), pl.BlockSpec(memory_space=pltpu.VMEM)
    grad_x, d_meta, gg = pl.pallas_call(
        body, name="grad_h", in_specs=[any_spec] * (np_ + 3) + [vmem, vmem], out_specs=[any_spec, vmem, vmem],
        out_shape=[jax.ShapeDtypeStruct((dm.Bl * dm.S, d), F32), jax.ShapeDtypeStruct((dm.Bl * tm, d), F32),
                   jax.ShapeDtypeStruct((8, d), F32)],
        scratch_shapes=_packed_scratch(dm)
        + [pltpu.VMEM((2, rows, w), a.dtype) for w, a in zip(widths, d_parts)]
        + [pltpu.VMEM((2, rows, d), BF16), pltpu.VMEM((2, rows, d), F32), pltpu.VMEM((2, rows, d), F32)]
        + [pltpu.VMEM((dm.Bl * tm, w), a.dtype) for w, a in zip(widths, d_parts)]
        + [pltpu.SemaphoreType.DMA((np_ + 2, 2)), pltpu.SemaphoreType.DMA((2,)), pltpu.SemaphoreType.DMA((np_, dm.Bl))],
        compiler_params=pltpu.CompilerParams(vmem_limit_bytes=VMEM_LIMIT_BYTES),
    )(*d_parts, gathered, dy, x.reshape(dm.Bl * dm.S, d), metapad, g_pre)
    return grad_x.reshape(dm.Bl, dm.S, d), d_meta.reshape(dm.Bl, tm, d), gg


def _adamw(partials, w, m, v, name, by_columns=False):
    r, c = w.shape
    n_parts, pr = partials.shape[:2]
    assert pr == r or (by_columns and pr == _padded_shard_rows(r))
    tr, tc = (r, _pick(c, 128, 128)) if by_columns else (_pick(r, 256, 16), c)

    def body(p_ref, w_ref, m_ref, v_ref, g_ref, d_ref, nm_ref, nv_ref):
        g = p_ref[0].astype(F32)
        for j in range(1, n_parts):
            g = g + p_ref[j].astype(F32)

        def step(g):
            g_ref[...] = g
            d_ref[...], nm_ref[...], nv_ref[...] = _adam_step(g, w_ref[...], m_ref[...], v_ref[...])

        if pr == r:
            step(g)
        else:
            me = 4 * lax.axis_index("x") + 2 * lax.axis_index("y") + lax.axis_index("c")
            for offset in sorted({_shard_offset(j, r) for j in range(N_DEV)}):
                @pl.when(_shard_offset(me, r) == offset)
                def _(offset=offset):
                    step(g[offset:offset + r])

    at = (lambda i: (0, i)) if by_columns else (lambda i: (i, 0))
    tile = pl.BlockSpec((tr, tc), at)
    out = jax.ShapeDtypeStruct((r, c), F32)
    return pl.pallas_call(
        body, name=name, grid=(c // tc if by_columns else r // tr,),
        in_specs=[pl.BlockSpec((n_parts, pr if by_columns else tr, tc), lambda i: (0,) + at(i)), tile, tile, tile],
        out_specs=[tile, tile, tile, tile], out_shape=[out, out, out, out], compiler_params=_cp(1),
    )(partials, w, m, v)


def _adam_step(g, w, m, v):
    m2 = ADAM_B1 * m + (1.0 - ADAM_B1) * g
    v2 = ADAM_B2 * v + (1.0 - ADAM_B2) * (g * g)
    m_hat = m2 / (1.0 - ADAM_B1 ** ADAM_STEP)
    v_hat = v2 / (1.0 - ADAM_B2 ** ADAM_STEP)
    return -ADAM_LR * (m_hat / (jnp.sqrt(v_hat) + ADAM_EPS) + ADAM_WD * w), m2, v2


def _adamw_small(items, name):
    n = len(items)

    def body(*refs):
        ins, outs = refs[:4 * n], refs[4 * n:]
        for i in range(n):
            p_ref, w_ref, m_ref, v_ref = ins[4 * i:4 * i + 4]
            g = p_ref[0]
            for j in range(1, p_ref.shape[0]):
                g = g + p_ref[j]
            delta, m2, v2 = _adam_step(g, w_ref[...], m_ref[...], v_ref[...])
            for o_ref, val in zip(outs[4 * i:4 * i + 4], (g, delta, m2, v2)):
                o_ref[...] = val

    vmem = pl.BlockSpec(memory_space=pltpu.VMEM)
    res = pl.pallas_call(
        body, name=name, in_specs=[vmem] * (4 * n), out_specs=[vmem] * (4 * n),
        out_shape=[jax.ShapeDtypeStruct(w.shape, F32) for _, w, _, _ in items for _ in range(4)],
    )(*[a for item in items for a in item])
    return [res[4 * i:4 * i + 4] for i in range(n)]


def _unpack_moves(dm):
    d, hk, hv, cw, nj, hw = dm.D, dm.HK, dm.HV, dm.CW, dm.NJ, dm.HW
    moves = [((4 * j + part) * cw, cw, part * d + j * cw) for j in range(nj) for part in range(4)]
    q0 = 4 * d
    k0, v0 = q0 + HEADS * hk, q0 + 2 * HEADS * hk
    r0 = v0 + HEADS * hv
    lr0 = r0 + HEADS * hv
    for h in range(HEADS):
        b0 = 4 * d + h * hw
        moves += [(b0, hk, q0 + h * hk), (b0 + hk, hk, k0 + h * hk), (b0 + 2 * hk, hv, v0 + h * hv),
                  (b0 + 2 * hk + hv, hv, r0 + h * hv)]
    moves.append((4 * d + HEADS * hw, 2 * d, lr0 + 2 * RANK))
    return moves, lr0


def _column_shards(g, shard_shape):
    r, c = g.shape
    return g.reshape(r, N_DEV, c // N_DEV).transpose(1, 0, 2).reshape((N_DEV,) + tuple(shard_shape))


def _join_column_shards(parts):
    r, c = parts.shape[-2:]
    return parts.reshape(N_DEV, r, c).transpose(1, 0, 2).reshape(r, N_DEV * c)


def _local_step(x, target, meta, g_pre, u, wt_shards, conv_w, wg_f, bg_f, wg_b, bg_b, gla_g, out_weights, g_post,
                on_matrix_grads=None):
    bl, s, d = x.shape
    dm = _Dims(bl, s, d)
    metapad = jnp.concatenate([jnp.zeros((dm.TM - N_META, d), F32), meta], axis=0)
    wgp_f = jnp.pad(wg_f, ((0, LR_LANES - RANK), (0, 0))).astype(BF16)
    wgp_b = jnp.pad(wg_b, ((RANK, LR_LANES - 2 * RANK), (0, 0))).astype(BF16)

    u = _prenorm_meta(u, metapad, g_pre, dm)
    proj_a, proj_b, proj_c, lr = _inproj(u, wt_shards, dm)
    y_conv = _conv_fwd(proj_a, conv_w, dm)
    o_all, y_gla, states, decays, gate_slopes = _gla_fwd(proj_b, lr, wgp_f, bg_f, wgp_b, bg_b, gla_g, dm)
    w_oc, w_og, w_out = out_weights(y_conv) if callable(out_weights) else out_weights
    merged, d_out, dy, d_pc, d_pg, d_c, dy_conv, dy_gla, stats = _head(y_conv, y_gla, proj_c, w_oc, w_og, w_out, x, target,
                                                                        g_post, dm)

    g_out = _matmul_tn(merged, d_out, BF16, "grad_w_out")
    g_oc = _matmul_tn(y_conv, d_pc, BF16, "grad_w_out_conv")
    g_og = _matmul_tn(y_gla, d_pg, BF16, "grad_w_out_gla")
    if on_matrix_grads is not None:
        conv_w = conv_w + on_matrix_grads(dict(w_out_conv=g_oc, w_out_gla=g_og, w_merge_out=g_out))
    d_a, g_conv = _conv_bwd(proj_a, dy_conv, conv_w, dm)
    d_b, d_lr, gwp_f, gbp_f, gwp_b, gbp_b, g_gla = _gla_bwd(proj_b, lr, o_all, dy_gla, states, decays, gate_slopes, wgp_f, wgp_b, gla_g, dm)
    moves, lr_at = _unpack_moves(dm)
    g_lr = _matmul_tn(d_lr, u, BF16, "grad_w_in_gate")[:2 * RANK]
    g_in = _matmul_tn_group([d_a, d_b, d_c], u, moves, 9 * d + 2 * RANK, (g_lr, lr_at), "grad_w_in", tile=d)
    if on_matrix_grads is not None:
        d_lr = d_lr + on_matrix_grads(dict(w_in=g_in))
    grad_x, d_meta, g_pre_rows = _grad_h([d_a, d_b, d_c, d_lr], wt_shards, dy, x, metapad, g_pre, dm)

    grads = dict(
        meta_tokens=jnp.sum(d_meta[:, dm.TM - N_META:, :], axis=0), norm_pre=g_pre_rows[0:1], w_in=g_in,
        conv_w=g_conv[0:3], w_gate_fwd=jnp.sum(gwp_f, axis=0)[:RANK], b_gate_fwd=jnp.sum(gbp_f, axis=0)[0:1],
        w_gate_bwd=jnp.sum(gwp_b, axis=0)[RANK:2 * RANK], b_gate_bwd=jnp.sum(gbp_b, axis=0)[0:1],
        gla_norm=g_gla[0:1], w_out_conv=g_oc, w_out_gla=g_og, w_merge_out=g_out, norm_post=stats[0:1])
    return stats[1:2], grad_x, grads


MATRICES = ("w_out_conv", "w_out_gla", "w_merge_out")
SMALL_SHARDED = ("meta_tokens", "conv_w", "w_gate_fwd", "w_gate_bwd")
REPLICATED = ("norm_pre", "b_gate_fwd", "b_gate_bwd", "gla_norm", "norm_post")
NAMES = ("meta_tokens", "norm_pre", "w_in", "conv_w", "w_gate_fwd", "b_gate_fwd", "w_gate_bwd", "b_gate_bwd", "gla_norm",
         "w_out_conv", "w_out_gla", "w_merge_out", "norm_post")


def kernel(x, meta_tokens, norm_pre, w_in, conv_w, w_gate_fwd, b_gate_fwd, w_gate_bwd, b_gate_bwd, gla_norm, w_out_conv, w_out_gla, w_merge_out, norm_post, loss_target, m_meta_tokens, m_norm_pre, m_w_in, m_conv_w, m_w_gate_fwd, m_b_gate_fwd, m_w_gate_bwd, m_b_gate_bwd, m_gla_norm, m_w_out_conv, m_w_out_gla, m_w_merge_out, m_norm_post, v_meta_tokens, v_norm_pre, v_w_in, v_conv_w, v_w_gate_fwd, v_b_gate_fwd, v_w_gate_bwd, v_b_gate_bwd, v_gla_norm, v_w_out_conv, v_w_out_gla, v_w_merge_out, v_norm_post):
    w = dict(meta_tokens=meta_tokens, norm_pre=norm_pre, w_in=w_in[0], conv_w=conv_w, w_gate_fwd=w_gate_fwd,
             b_gate_fwd=b_gate_fwd, w_gate_bwd=w_gate_bwd, b_gate_bwd=b_gate_bwd, gla_norm=gla_norm,
             w_out_conv=w_out_conv[0], w_out_gla=w_out_gla[0], w_merge_out=w_merge_out[0], norm_post=norm_post)
    m = dict(meta_tokens=m_meta_tokens, norm_pre=m_norm_pre, w_in=m_w_in[0], conv_w=m_conv_w, w_gate_fwd=m_w_gate_fwd,
             b_gate_fwd=m_b_gate_fwd, w_gate_bwd=m_w_gate_bwd, b_gate_bwd=m_b_gate_bwd, gla_norm=m_gla_norm,
             w_out_conv=m_w_out_conv[0], w_out_gla=m_w_out_gla[0], w_merge_out=m_w_merge_out[0], norm_post=m_norm_post)
    v = dict(meta_tokens=v_meta_tokens, norm_pre=v_norm_pre, w_in=v_w_in[0], conv_w=v_conv_w, w_gate_fwd=v_w_gate_fwd,
             b_gate_fwd=v_b_gate_fwd, w_gate_bwd=v_w_gate_bwd, b_gate_bwd=v_b_gate_bwd, gla_norm=v_gla_norm,
             w_out_conv=v_w_out_conv[0], w_out_gla=v_w_out_gla[0], w_merge_out=v_w_merge_out[0], norm_post=v_norm_post)
    d = x.shape[-1]

    dm = _Dims(*x.shape)
    me = 4 * lax.axis_index("x") + 2 * lax.axis_index("y") + lax.axis_index("c")
    wt_shards, *small_all, u = _gather_two_level(
        [_pad_shard(w["w_in"].T.astype(BF16), me)] + [w[n] for n in SMALL_SHARDED], "gather_weights",
        _prenorm_tokens_side(x, norm_pre, dm))
    _, late_weights = _exchange_start([w[n].astype(BF16) for n in MATRICES], [], small_all[0], "gather_out_weights_start")
    small = {n: _join_column_shards(p) for n, p in zip(SMALL_SHARDED, small_all)}

    def out_weights(after):
        return tuple(a.reshape(-1, d) for a in _exchange_wait(late_weights, after, "gather_out_weights_wait"))

    pending = []

    def on_matrix_grads(g):
        blocks = [t.reshape(N_DEV, -1, d) if t.shape[0] % (N_DEV * BF16_TILE_ROWS) == 0 else (t, t.shape[0] // N_DEV)
                  for t in g.values()]
        token, state = _exchange_start([], blocks, None, "exchange_grads_start_" + "_".join(g))
        pending.append((tuple(g), state))
        return token

    sq_err_cols, grad_x, grads = _local_step(
        x, loss_target, small["meta_tokens"], norm_pre, u, wt_shards, small["conv_w"], small["w_gate_fwd"], b_gate_fwd,
        small["w_gate_bwd"], b_gate_bwd, gla_norm, out_weights, norm_post, on_matrix_grads)
    received = {}
    for names, state in pending:
        received.update(zip(names, _exchange_wait(state, grad_x, "exchange_grads_wait_" + "_".join(names))))

    exchanged = _exchange([grads[n] for n in REPLICATED] + [sq_err_cols],
                          [_column_shards(grads[n], w[n].shape) for n in SMALL_SHARDED], "exchange_small_grads")
    small_recv = exchanged[:len(REPLICATED)] + exchanged[len(REPLICATED) + 1:]
    loss = 0.5 / d * jnp.sum(exchanged[len(REPLICATED)])

    results = {"w_in": [r.T[None] for r in _adamw(received["w_in"], w["w_in"].T, m["w_in"].T, v["w_in"].T, "adamw_w_in", by_columns=True)]}
    for n in MATRICES:
        results[n] = [r[None] for r in _adamw(received[n], w[n], m[n], v[n], "adamw_" + n)]
    small_names = REPLICATED + SMALL_SHARDED
    results.update(zip(small_names, _adamw_small([(p, w[n], m[n], v[n]) for n, p in zip(small_names, small_recv)], "adamw_small")))
    return (loss, grad_x, *[results[n][i] for i in range(4) for n in NAMES])
```

```python
import jax
import jax.numpy as jnp
from jax import lax
from jax.experimental import pallas as pl
from jax.experimental.pallas import tpu as pltpu

F32 = jnp.float32
BF16 = jnp.bfloat16
MESH = pl.DeviceIdType.MESH

N_META = 16
CHUNK = 64
CHUNK_SHIFT = 6
HEADS = 4
RANK = 16
LR_LANES = 128
PAD_ROWS = CHUNK - N_META
EPS = 1e-6
GATE_NORMALIZER = 16.0
N_DEV = 8
ADAM_LR, ADAM_B1, ADAM_B2, ADAM_EPS, ADAM_WD, ADAM_STEP = 0.001, 0.9, 0.999, 1e-08, 0.01, 10
VMEM_LIMIT_BYTES = 56 * 1024 * 1024


class _Dims:
    def __init__(self, bl, s, d):
        self.Bl, self.S, self.D = bl, s, d
        self.TM = CHUNK
        self.LP = self.TM + s
        self.T = bl * self.LP
        self.TPS = self.LP // self.TM
        self.NC = self.LP // CHUNK
        self.C0 = (self.TM - CHUNK) // CHUNK
        self.DK, self.DV = d // 2, d
        self.HK, self.HV = self.DK // HEADS, self.DV // HEADS
        self.HW = 2 * self.HK + 2 * self.HV
        self.CW = 256 if d % 256 == 0 and d > 256 else d // 4
        self.NJ = d // self.CW


def _pick(n, target, mult):
    t = min(n, target)
    while t >= mult:
        if n % t == 0 and t % mult == 0:
            return t
        t -= mult
    return n


def _cp(n_axes):
    return pltpu.CompilerParams(dimension_semantics=("arbitrary",) * n_axes, vmem_limit_bytes=VMEM_LIMIT_BYTES)


def _sigmoid(x):
    return 1.0 / (1.0 + jnp.exp(-x))


def _dot(a, b):
    return jnp.dot(a, b, preferred_element_type=F32)


def _dot_nt(a, b):
    return lax.dot_general(a, b, (((1,), (1,)), ((), ())), preferred_element_type=F32)


def _dot_tn(a, b):
    return lax.dot_general(a, b, (((0,), (0,)), ((), ())), preferred_element_type=F32)


def _chunk_cumsum(x, reverse):
    rows = x.shape[0]
    r = lax.broadcasted_iota(jnp.int32, x.shape, 0) & (CHUNK - 1)
    step = 1
    while step < CHUNK:
        if reverse:
            x = x + jnp.where(r < CHUNK - step, pltpu.roll(x, rows - step, 0), 0.0)
        else:
            x = x + jnp.where(r >= step, pltpu.roll(x, step, 0), 0.0)
        step *= 2
    return x


def _exchange(gathers, scatters, name):
    arrays = list(gathers) + list(scatters)
    n, ng = len(arrays), len(gathers)

    def body(*refs):
        ins, outs = refs[:n], refs[n:2 * n]
        send_sems, recv_sems, local_sems = refs[2 * n:]
        x, y, c = lax.axis_index("x"), lax.axis_index("y"), lax.axis_index("c")
        me = 4 * x + 2 * y + c
        started = []
        for t in range(n):
            src, dst = ins[t], outs[t]
            own = pltpu.make_async_copy(src if t < ng else src.at[me], dst.at[me], local_sems.at[t])
            own.start()
            started.append(own)
            for k, pos, peer in _peers(x, y, c):
                cp = pltpu.make_async_remote_copy(
                    src_ref=src if t < ng else src.at[peer], dst_ref=dst.at[me],
                    send_sem=send_sems.at[t * (N_DEV - 1) + k - 1], recv_sem=recv_sems.at[t * (N_DEV - 1) + k - 1],
                    device_id=pos, device_id_type=MESH)
                cp.start()
                started.append(cp)
        for cp in started:
            cp.wait()

    out_shape = [jax.ShapeDtypeStruct((N_DEV,) + a.shape if t < ng else a.shape, a.dtype) for t, a in enumerate(arrays)]
    any_spec = pl.BlockSpec(memory_space=pl.ANY)
    return pl.pallas_call(
        body, name=name, out_shape=out_shape, in_specs=[any_spec] * n, out_specs=[any_spec] * n,
        scratch_shapes=[pltpu.SemaphoreType.DMA((n * (N_DEV - 1),)), pltpu.SemaphoreType.DMA((n * (N_DEV - 1),)),
                        pltpu.SemaphoreType.DMA((n,))],
        compiler_params=pltpu.CompilerParams(has_side_effects=True),
    )(*arrays)


def _gather_two_level(arrays, name, side=None):
    n = len(arrays)
    per = N_DEV - 1
    work, side_in, side_in_specs, side_out, side_out_specs, side_scratch = side or (None, [], [], [], [], [])
    n_in, n_out = len(side_in), len(side_out)

    def body(*refs):
        ins, outs = refs[:n], refs[n + n_in:2 * n + n_in]
        send_sems, recv_sems, local_sems = refs[2 * n + n_in + n_out:2 * n + n_in + n_out + 3]
        x, y, c = lax.axis_index("x"), lax.axis_index("y"), lax.axis_index("c")
        sibling = (x, y, 1 - c)
        chips = [(1 - x, y), (x, 1 - y), (1 - x, 1 - y)]
        index = lambda px, py, pc: 4 * px + 2 * py + pc

        def copy(t, k, block, to, from_input=False):
            slab = outs[t].at[index(*block)]
            return pltpu.make_async_remote_copy(
                src_ref=ins[t] if from_input else slab, dst_ref=slab, send_sem=send_sems.at[t * per + k],
                recv_sem=recv_sems.at[t * per + k], device_id=to, device_id_type=MESH)

        own, sent = [], []
        for t in range(n):
            own.append(pltpu.make_async_copy(ins[t], outs[t].at[index(x, y, c)], local_sems.at[t]))
            own[-1].start()
            first = [copy(t, 0, (x, y, c), sibling, True)]
            first += [copy(t, 1 + j, (x, y, c), (*chip, c), True) for j, chip in enumerate(chips)]
            for cp in first:
                cp.start()
            sent += first
        if work is not None:
            work(refs[n:n + n_in], refs[2 * n + n_in:2 * n + n_in + n_out], refs[2 * n + n_in + n_out + 3:])
        for t in range(n):
            for j, chip in enumerate(chips):
                copy(t, 1 + j, (*chip, c), (x, y, c)).wait_recv()
                sent.append(copy(t, 4 + j, (*chip, c), sibling))
                sent[-1].start()
        for t in range(n):
            copy(t, 0, sibling, (x, y, c)).wait_recv()
            for j, chip in enumerate(chips):
                copy(t, 4 + j, (*chip, 1 - c), (x, y, c)).wait_recv()
        for cp in sent:
            cp.wait_send()
        for cp in own:
            cp.wait()

    out_shape = [jax.ShapeDtypeStruct((N_DEV,) + a.shape, a.dtype) for a in arrays]
    any_spec = pl.BlockSpec(memory_space=pl.ANY)
    return pl.pallas_call(
        body, name=name, out_shape=out_shape + list(side_out), in_specs=[any_spec] * n + list(side_in_specs),
        out_specs=[any_spec] * n + list(side_out_specs),
        scratch_shapes=[pltpu.SemaphoreType.DMA((n * per,)), pltpu.SemaphoreType.DMA((n * per,)),
                        pltpu.SemaphoreType.DMA((n,))] + list(side_scratch),
        compiler_params=pltpu.CompilerParams(has_side_effects=True, vmem_limit_bytes=VMEM_LIMIT_BYTES),
    )(*arrays, *side_in)


def _peers(x, y, c):
    out = []
    for k in range(1, N_DEV):
        px = 1 - x if (k >> 2) & 1 else x
        py = 1 - y if (k >> 1) & 1 else y
        pc = 1 - c if k & 1 else c
        out.append((k, (px, py, pc), 4 * px + 2 * py + pc))
    return out


def _exchange_start(gathers, scatters, after, name):
    shard_rows = [None] * len(gathers) + [s[1] if isinstance(s, tuple) else None for s in scatters]
    arrays = list(gathers) + [s[0] if isinstance(s, tuple) else s for s in scatters]
    n, ng = len(arrays), len(gathers)
    hbm = pl.BlockSpec(memory_space=pltpu.HBM)
    sem = pl.BlockSpec(memory_space=pltpu.SEMAPHORE)

    extra = [] if after is None else [after]
    ne = len(extra)

    def body(*refs):
        ins, lands = refs[:n], refs[n:2 * n]
        send_sems, recv_sems = refs[2 * n + ne], refs[2 * n + ne + 1]
        token = refs[4 * n + ne + 2]
        x, y, c = lax.axis_index("x"), lax.axis_index("y"), lax.axis_index("c")
        me = 4 * x + 2 * y + c
        for t in range(n):
            for k, pos, peer in _peers(x, y, c):
                pltpu.make_async_remote_copy(
                    src_ref=_block_for(ins[t], peer, t < ng, shard_rows[t]), dst_ref=lands[t].at[me],
                    send_sem=send_sems.at[t * (N_DEV - 1) + k - 1], recv_sem=recv_sems.at[t * (N_DEV - 1) + k - 1],
                    device_id=pos, device_id_type=MESH).start()
        token[...] = jnp.zeros_like(token)

    me = 4 * lax.axis_index("x") + 2 * lax.axis_index("y") + lax.axis_index("c")

    def own_block(t, a):
        if t < ng:
            return a
        if shard_rows[t] is None:
            return lax.dynamic_index_in_dim(a, me, 0, keepdims=False)
        assert all(_shard_window(j, shard_rows[t]) + _padded_shard_rows(shard_rows[t]) <= a.shape[0] for j in range(N_DEV))
        return lax.dynamic_slice_in_dim(a, _shard_window(me, shard_rows[t]), _padded_shard_rows(shard_rows[t]), 0)

    blocks = [own_block(t, a) for t, a in enumerate(arrays)]
    lands = [lax.dynamic_update_index_in_dim(lax.empty((N_DEV,) + b.shape if t < ng or shard_rows[t] else a.shape, a.dtype), b, me, 0)
             for t, (a, b) in enumerate(zip(arrays, blocks))]
    operands = [pltpu.with_memory_space_constraint(a, pltpu.HBM) for a in arrays + lands]
    sems = pltpu.SemaphoreType.DMA((n * (N_DEV - 1),))
    res = pl.pallas_call(
        body, name=name,
        out_shape=(sems, sems, *[pltpu.HBM(a.shape, a.dtype) for a in arrays + lands], jax.ShapeDtypeStruct((8, 128), F32)),
        in_specs=[hbm] * (2 * n) + [pl.BlockSpec(memory_space=pl.ANY)] * ne,
        out_specs=(sem, sem, *[hbm] * (2 * n), pl.BlockSpec(memory_space=pltpu.VMEM)),
        input_output_aliases={i: 2 + i for i in range(2 * n)},
        compiler_params=pltpu.CompilerParams(has_side_effects=pltpu.SideEffectType.DATAFLOW_SIDE_EFFECTING),
    )(*operands, *extra)
    return res[-1][0, 0], (ng, shard_rows, res[0], res[1], list(res[2:2 + n]), list(res[2 + n:2 + 2 * n]))


def _block_for(ref, peer, whole, shard_rows):
    if whole:
        return ref
    if shard_rows is None:
        return ref.at[peer]
    return ref.at[pl.ds(pl.multiple_of(_shard_window(peer, shard_rows), BF16_TILE_ROWS), _padded_shard_rows(shard_rows))]


def _exchange_wait(state, after, name):
    ng, shard_rows, send_sems, recv_sems, sent, lands = state
    n = len(sent)
    hbm = pl.BlockSpec(memory_space=pltpu.HBM)
    sem = pl.BlockSpec(memory_space=pltpu.SEMAPHORE)

    def body(*refs):
        ins, land_refs = refs[:n], refs[n:2 * n]
        send_ref, recv_ref = refs[2 * n], refs[2 * n + 1]
        x, y, c = lax.axis_index("x"), lax.axis_index("y"), lax.axis_index("c")
        me = 4 * x + 2 * y + c
        for t in range(n):
            for k, pos, peer in _peers(x, y, c):
                cp = pltpu.make_async_remote_copy(
                    src_ref=_block_for(ins[t], peer, t < ng, shard_rows[t]), dst_ref=land_refs[t].at[me],
                    send_sem=send_ref.at[t * (N_DEV - 1) + k - 1], recv_sem=recv_ref.at[t * (N_DEV - 1) + k - 1],
                    device_id=pos, device_id_type=MESH)
                cp.wait_send()
                cp.wait_recv()

    res = pl.pallas_call(
        body, name=name, out_shape=tuple(pltpu.HBM(a.shape, a.dtype) for a in sent + lands),
        in_specs=[hbm] * (2 * n) + [sem, sem, pl.BlockSpec(memory_space=pl.ANY)], out_specs=tuple([hbm] * (2 * n)),
        input_output_aliases={i: i for i in range(2 * n)},
        compiler_params=pltpu.CompilerParams(has_side_effects=pltpu.SideEffectType.DATAFLOW_SIDE_EFFECTING),
    )(*sent, *lands, send_sems, recv_sems, after)
    return list(res[n:])


def _rms_scaled(h, g):
    return (h * lax.rsqrt(jnp.mean(h * h, axis=-1, keepdims=True) + EPS) * g).astype(BF16)


def _prenorm_tokens_side(x, g_pre, dm):
    bl, s, d = x.shape
    rows = _pick(s, 512, 16)
    tiles = [(b, j) for b in range(bl) for j in range(s // rows)]

    def work(ins, outs, scratch):
        (x_ref, g_ref), (u_ref,), (xbuf, ubuf, sem_in, sem_out) = ins, outs, scratch

        def load(t, slot):
            b, j = tiles[t]
            return pltpu.make_async_copy(x_ref.at[b, pl.ds(j * rows, rows), :], xbuf.at[slot], sem_in.at[slot])

        def store(t, slot):
            b, j = tiles[t]
            return pltpu.make_async_copy(ubuf.at[slot], u_ref.at[pl.ds(b * dm.LP + dm.TM + j * rows, rows), :], sem_out.at[slot])

        load(0, 0).start()
        for t in range(len(tiles)):
            slot = t % 2
            if t + 1 < len(tiles):
                load(t + 1, 1 - slot).start()
            load(t, slot).wait()
            if t >= 2:
                store(t - 2, slot).wait()
            ubuf[slot] = _rms_scaled(xbuf[slot], g_ref[...])
            store(t, slot).start()
        for t in range(max(len(tiles) - 2, 0), len(tiles)):
            store(t, t % 2).wait()

    any_spec = pl.BlockSpec(memory_space=pl.ANY)
    return (work, [x, g_pre], [any_spec, pl.BlockSpec(memory_space=pltpu.VMEM)],
            [jax.ShapeDtypeStruct((dm.T, d), BF16)], [any_spec],
            [pltpu.VMEM((2, rows, d), F32), pltpu.VMEM((2, rows, d), BF16), pltpu.SemaphoreType.DMA((2,)),
             pltpu.SemaphoreType.DMA((2,))])


def _prenorm_meta(u, metapad, g_pre, dm):
    tm, tps, d = dm.TM, dm.TPS, dm.D

    def body(u_in, mp_ref, g_ref, u_ref):
        u_ref[...] = _rms_scaled(mp_ref[...], g_ref[...])

    return pl.pallas_call(
        body, name="prenorm_meta", grid=(dm.Bl,),
        in_specs=[pl.BlockSpec(memory_space=pl.ANY), pl.BlockSpec((tm, d), lambda i: (0, 0)),
                  pl.BlockSpec((1, d), lambda i: (0, 0))],
        out_specs=pl.BlockSpec((tm, d), lambda i: (i * tps, 0)),
        out_shape=jax.ShapeDtypeStruct((dm.T, d), BF16), input_output_aliases={0: 0}, compiler_params=_cp(1),
    )(u, metapad, g_pre)


def _matmul_tn(a, b, out_dtype, name, tt=2304, tn=1024, tk=1024):
    t, k = a.shape
    n = b.shape[1]
    tt, tn, tk = _pick(t, tt, 16), _pick(n, tn, 128), _pick(k, tk, 128)
    nt = t // tt

    def body(a_ref, b_ref, o_ref, acc):
        p = _dot_tn(a_ref[...].astype(BF16), b_ref[...].astype(BF16))
        i = pl.program_id(2)

        @pl.when(i == 0)
        def _():
            acc[...] = p

        @pl.when(i > 0)
        def _():
            acc[...] += p

        @pl.when(i == nt - 1)
        def _():
            o_ref[...] = acc[...].astype(out_dtype)

    return pl.pallas_call(
        body, name=name, grid=(k // tk, n // tn, nt),
        in_specs=[pl.BlockSpec((tt, tk), lambda kk, j, i: (i, kk)), pl.BlockSpec((tt, tn), lambda kk, j, i: (i, j))],
        out_specs=pl.BlockSpec((tk, tn), lambda kk, j, i: (kk, j)),
        out_shape=jax.ShapeDtypeStruct((k, n), out_dtype), scratch_shapes=[pltpu.VMEM((tk, tn), F32)],
        compiler_params=_cp(3),
    )(a, b)


def _matmul_tn_group(a_list, b, moves, out_rows, extra, name, tt=2304, tile=1024):
    t, n = b.shape
    tt = _pick(t, tt, 16)
    nt = t // tt
    counts = [a.shape[1] // tile for a in a_list]
    starts = [sum(counts[:m]) for m in range(len(a_list))]
    items = sum(counts)
    extra_rows, extra_at = extra
    cuts = [[] for _ in range(items)]
    for row, rows, at in moves:
        while rows > 0:
            p, r = divmod(row, tile)
            take = min(rows, tile - r)
            cuts[p].append((r, take, at))
            row, rows, at = row + take, rows - take, at + take
    assert all(v % BF16_TILE_ROWS == 0 for cut in cuts for move in cut for v in move)
    assert sum(rows for _, rows, _ in moves) + extra_rows.shape[0] == out_rows

    def active(p, m):
        return (p >= starts[m]) & (p < starts[m] + counts[m])

    def body(*refs):
        a_refs, b_ref, x_ref = refs[:len(a_list)], refs[len(a_list)], refs[len(a_list) + 1]
        o_ref, acc, stage, sems, x_sem = refs[-5:]
        p, i = pl.program_id(0), pl.program_id(1)

        def writes(item):
            return [pltpu.make_async_copy(stage.at[pl.ds(r, rows), :], o_ref.at[pl.ds(at, rows), :], sems.at[s])
                    for s, (r, rows, at) in enumerate(cuts[item])]

        extra_copy = pltpu.make_async_copy(x_ref, o_ref.at[pl.ds(extra_at, extra_rows.shape[0]), :], x_sem.at[0])

        @pl.when((p == 0) & (i == 0))
        def _():
            extra_copy.start()

        for m, a_ref in enumerate(a_refs):
            @pl.when(active(p, m))
            def _(a_ref=a_ref):
                prod = _dot_tn(a_ref[...].astype(BF16), b_ref[...].astype(BF16))

                @pl.when(i == 0)
                def _():
                    acc[...] = prod

                @pl.when(i > 0)
                def _():
                    acc[...] += prod

        for item in range(items):
            @pl.when((p == item) & (i == nt - 1))
            def _(item=item):
                if item > 0:
                    for cp in writes(item - 1):
                        cp.wait()
                stage[...] = acc[...].astype(BF16)
                for cp in writes(item):
                    cp.start()
                if item == items - 1:
                    for cp in writes(item):
                        cp.wait()
                    extra_copy.wait()

    a_specs = [pl.BlockSpec((tt, tile), lambda p, i, m=m: (jnp.where(active(p, m), i, 0), jnp.where(active(p, m), p - starts[m], 0)))
               for m in range(len(a_list))]
    return pl.pallas_call(
        body, name=name, grid=(items, nt),
        in_specs=a_specs + [pl.BlockSpec((tt, n), lambda p, i: (i, 0)), pl.BlockSpec(memory_space=pltpu.VMEM)],
        out_specs=pl.BlockSpec(memory_space=pl.ANY), out_shape=jax.ShapeDtypeStruct((out_rows, n), BF16),
        scratch_shapes=[pltpu.VMEM((tile, n), F32), pltpu.VMEM((tile, n), BF16),
                        pltpu.SemaphoreType.DMA((max(len(cut) for cut in cuts),)), pltpu.SemaphoreType.DMA((1,))],
        compiler_params=_cp(2),
    )(*a_list, b, extra_rows)


BF16_TILE_ROWS = 16


def _shard_offset(index, shard_rows):
    return (index * shard_rows) % BF16_TILE_ROWS


def _padded_shard_rows(shard_rows):
    return -(-(shard_rows + max(_shard_offset(j, shard_rows) for j in range(N_DEV))) // BF16_TILE_ROWS) * BF16_TILE_ROWS


def _pad_shard(wt_shard, index):
    rows, d = wt_shard.shape
    return lax.dynamic_update_slice(jnp.zeros((_padded_shard_rows(rows), d), wt_shard.dtype), wt_shard,
                                    (_shard_offset(index, rows), 0))


def _shard_window(index, shard_rows):
    return index * shard_rows - _shard_offset(index, shard_rows)


def _packed_parts(dm):
    d, dk, hk, hv, cw, nj, hw = dm.D, dm.DK, dm.HK, dm.HV, dm.CW, dm.NJ, dm.HW
    blocks = [(0, (j * 4 + p) * cw, p * d + j * cw, cw) for j in range(nj) for p in range(4)]
    for h in range(HEADS):
        blocks += [(1, h * hw, 4 * d + h * hk, hk), (1, h * hw + hk, 4 * d + dk + h * hk, hk),
                   (1, h * hw + 2 * hk, 5 * d + h * hv, hv), (1, h * hw + 2 * hk + hv, 6 * d + h * hv, hv)]
    blocks += [(2, 0, 7 * d + 2 * RANK, 2 * d), (3, 0, 7 * d, 2 * RANK)]
    return [4 * d, 3 * d, 2 * d, LR_LANES], blocks


def _pack_plan(dm):
    sh = (9 * dm.D + 2 * RANK) // N_DEV
    tile = BF16_TILE_ROWS
    copies, straddles = [], []
    for part, dst, r0, n in _packed_parts(dm)[1]:
        for j in range(N_DEV):
            a, b = max(r0, sh * j), min(r0 + n, sh * (j + 1))
            if a >= b:
                continue
            a_up, b_down = -(-a // tile) * tile, b // tile * tile
            if b_down > a_up:
                copies.append((j, a_up - sh * j + _shard_offset(j, sh), b_down - a_up, part, dst + a_up - r0))
            if a % tile:
                lo = a // tile * tile
                straddles.append((j, lo - sh * (j - 1) + _shard_offset(j - 1, sh), part, dst + lo - r0, a - lo))
    return copies, straddles


def _packed_scratch(dm):
    copies, straddles = _pack_plan(dm)
    return ([pltpu.VMEM((rows, dm.D), BF16) for rows in _packed_parts(dm)[0]]
            + [pltpu.VMEM((2 * max(len(straddles), 1), BF16_TILE_ROWS, dm.D), BF16),
               pltpu.SemaphoreType.DMA((len(copies) + 2 * len(straddles),))])


def _load_packed(g_ref, parts, edges, sems, dm):
    copies, straddles = _pack_plan(dm)
    tile = BF16_TILE_ROWS
    parts[3][2 * RANK:, :] = jnp.zeros((LR_LANES - 2 * RANK, dm.D), BF16)
    dmas = [pltpu.make_async_copy(g_ref.at[j, pl.ds(src, n), :], parts[p].at[pl.ds(dst, n), :], sems.at[i])
            for i, (j, src, n, p, dst) in enumerate(copies)]
    for i, (j, src, p, dst, split) in enumerate(straddles):
        k = len(copies) + 2 * i
        dmas.append(pltpu.make_async_copy(g_ref.at[j - 1, pl.ds(src, tile), :], edges.at[2 * i], sems.at[k]))
        dmas.append(pltpu.make_async_copy(g_ref.at[j, pl.ds(0, tile), :], edges.at[2 * i + 1], sems.at[k + 1]))
    for cp in dmas:
        cp.start()
    for cp in dmas:
        cp.wait()
    row = lax.broadcasted_iota(jnp.int32, (tile, dm.D), 0)
    for i, (j, src, p, dst, split) in enumerate(straddles):
        parts[p][dst:dst + tile, :] = jnp.where(row < split, edges[2 * i], edges[2 * i + 1])


def _inproj(u, gathered, dm):
    t, d = u.shape
    tm = _pick(t, 512, 16)
    widths = _packed_parts(dm)[0]
    cn = 1024

    def body(u_ref, g_ref, *rest):
        outs, parts, (edges, sems) = rest[:4], rest[4:8], rest[8:]

        @pl.when(pl.program_id(0) == 0)
        def _():
            _load_packed(g_ref, parts, edges, sems, dm)

        ut = u_ref[...]
        for w, o_ref in zip(parts, outs):
            n = w.shape[0]
            step = cn if n % cn == 0 else n
            for j in range(0, n, step):
                o_ref[:, j:j + step] = _dot_nt(ut, w[j:j + step, :]).astype(BF16)

    return pl.pallas_call(
        body, name="inproj", grid=(t // tm,),
        in_specs=[pl.BlockSpec((tm, d), lambda i: (i, 0)), pl.BlockSpec(memory_space=pl.ANY)],
        out_specs=[pl.BlockSpec((tm, w), lambda i: (i, 0)) for w in widths],
        out_shape=[jax.ShapeDtypeStruct((t, w), BF16) for w in widths],
        scratch_shapes=_packed_scratch(dm), compiler_params=_cp(1),
    )(u, gathered)


def _conv_rows(dm):
    return _pick(dm.LP, 256, 16)


def _shifted(m, prev_row, next_row, rows):
    row = lax.broadcasted_iota(jnp.int32, m.shape, 0)
    m_prev = jnp.where(row == 0, prev_row, pltpu.roll(m, 1, 0))
    m_next = jnp.where(row == rows - 1, next_row, pltpu.roll(m, rows - 1, 0))
    return m_prev, m_next


def _conv_fwd(proj_a, conv_w, dm):
    lp, cw, rc = dm.LP, dm.CW, _conv_rows(dm)
    nchunk = lp // rc

    def body(p_ref, w_ref, y_ref):
        w0, w1, w2 = w_ref[0:1, :], w_ref[1:2, :], w_ref[2:3, :]

        def chunk(ci, carry):
            r0 = pl.multiple_of(ci * rc, rc)
            blk = p_ref[pl.ds(r0, rc), :].astype(F32)
            cb, cc, cx, cz = (blk[:, i * cw:(i + 1) * cw] for i in range(4))
            m = cc * cx
            rp = pl.multiple_of(jnp.maximum(r0 - 16, 0), 16)
            rn = pl.multiple_of(jnp.minimum(r0 + rc, lp - 16), 16)
            pv = p_ref[pl.ds(rp, 16), cw:3 * cw].astype(F32)
            nx = p_ref[pl.ds(rn, 16), cw:3 * cw].astype(F32)
            prev_row = jnp.where(ci > 0, pv[15:16, :cw] * pv[15:16, cw:], 0.0)
            next_row = jnp.where(ci < nchunk - 1, nx[0:1, :cw] * nx[0:1, cw:], 0.0)
            m_prev, m_next = _shifted(m, prev_row, next_row, rc)
            s = w0 * m_prev + w1 * m + w2 * m_next
            y_ref[pl.ds(r0, rc), :] = (cb * s * (cz * _sigmoid(cz))).astype(BF16)
            return carry

        lax.fori_loop(0, nchunk, chunk, 0)

    return pl.pallas_call(
        body, name="conv_fwd", grid=(dm.Bl, dm.NJ),
        in_specs=[pl.BlockSpec((lp, 4 * cw), lambda s, j: (s, j)), pl.BlockSpec((3, cw), lambda s, j: (0, j))],
        out_specs=pl.BlockSpec((lp, cw), lambda s, j: (s, j)),
        out_shape=jax.ShapeDtypeStruct((dm.T, dm.D), BF16), compiler_params=_cp(2),
    )(proj_a, conv_w)


def _conv_bwd(proj_a, dy_conv, conv_w, dm):
    lp, cw, rc = dm.LP, dm.CW, _conv_rows(dm)
    nchunk = lp // rc

    def body(p_ref, dy_ref, w_ref, d_ref, gw_ref):
        w0, w1, w2 = w_ref[0:1, :], w_ref[1:2, :], w_ref[2:3, :]

        def ds_of(p4, dy):
            cb, cz = p4[:, :cw], p4[:, 3 * cw:]
            return dy * cb * (cz * _sigmoid(cz))

        def chunk(ci, carry):
            g0, g1, g2 = carry
            r0 = pl.multiple_of(ci * rc, rc)
            blk = p_ref[pl.ds(r0, rc), :].astype(F32)
            dy = dy_ref[pl.ds(r0, rc), :].astype(F32)
            cb, cc, cx, cz = (blk[:, i * cw:(i + 1) * cw] for i in range(4))
            rp = pl.multiple_of(jnp.maximum(r0 - 16, 0), 16)
            rn = pl.multiple_of(jnp.minimum(r0 + rc, lp - 16), 16)
            pv = p_ref[pl.ds(rp, 16), :].astype(F32)[15:16]
            nx = p_ref[pl.ds(rn, 16), :].astype(F32)[0:1]
            dpv = dy_ref[pl.ds(rp, 16), :].astype(F32)[15:16]
            dnx = dy_ref[pl.ds(rn, 16), :].astype(F32)[0:1]
            has_prev, has_next = ci > 0, ci < nchunk - 1
            m = cc * cx
            m_prev, m_next = _shifted(m, jnp.where(has_prev, pv[:, cw:2 * cw] * pv[:, 2 * cw:3 * cw], 0.0),
                                      jnp.where(has_next, nx[:, cw:2 * cw] * nx[:, 2 * cw:3 * cw], 0.0), rc)
            s = w0 * m_prev + w1 * m + w2 * m_next
            sg = _sigmoid(cz)
            silu = cz * sg
            ds = dy * cb * silu
            ds_prev, ds_next = _shifted(ds, jnp.where(has_prev, ds_of(pv, dpv), 0.0),
                                        jnp.where(has_next, ds_of(nx, dnx), 0.0), rc)
            dm_ = w0 * ds_next + w1 * ds + w2 * ds_prev
            d_ref[pl.ds(r0, rc), 0:cw] = (dy * s * silu).astype(BF16)
            d_ref[pl.ds(r0, rc), cw:2 * cw] = (dm_ * cx).astype(BF16)
            d_ref[pl.ds(r0, rc), 2 * cw:3 * cw] = (dm_ * cc).astype(BF16)
            d_ref[pl.ds(r0, rc), 3 * cw:4 * cw] = (dy * cb * s * (sg * (1.0 + cz * (1.0 - sg)))).astype(BF16)
            return (g0 + jnp.sum(ds * m_prev, axis=0, keepdims=True), g1 + jnp.sum(ds * m, axis=0, keepdims=True),
                    g2 + jnp.sum(ds * m_next, axis=0, keepdims=True))

        z = jnp.zeros((1, cw), F32)
        g0, g1, g2 = lax.fori_loop(0, nchunk, chunk, (z, z, z))

        @pl.when(pl.program_id(1) == 0)
        def _():
            gw_ref[...] = jnp.zeros_like(gw_ref)

        gw_ref[0:1, :] += g0
        gw_ref[1:2, :] += g1
        gw_ref[2:3, :] += g2

    return pl.pallas_call(
        body, name="conv_bwd", grid=(dm.NJ, dm.Bl),
        in_specs=[pl.BlockSpec((lp, 4 * cw), lambda j, s: (s, j)), pl.BlockSpec((lp, cw), lambda j, s: (s, j)),
                  pl.BlockSpec((3, cw), lambda j, s: (0, j))],
        out_specs=[pl.BlockSpec((lp, 4 * cw), lambda j, s: (s, j)), pl.BlockSpec((8, cw), lambda j, s: (0, j))],
        out_shape=[jax.ShapeDtypeStruct((dm.T, 4 * dm.D), BF16), jax.ShapeDtypeStruct((8, dm.D), F32)],
        compiler_params=_cp(2),
    )(proj_a, dy_conv, conv_w)


def _interleave(gens):
    results = [None] * len(gens)
    live = list(range(len(gens)))
    while live:
        for idx in list(live):
            try:
                next(gens[idx])
            except StopIteration as done:
                results[idx] = done.value
                live.remove(idx)
    return results


def _group_chunks(dm):
    n = dm.NC - dm.C0
    return 3 if n % 3 == 0 else 1


def _group_masks(rows):
    ii = lax.broadcasted_iota(jnp.int32, (rows, rows), 0)
    jj = lax.broadcasted_iota(jnp.int32, (rows, rows), 1)
    same = jnp.right_shift(ii, CHUNK_SHIFT) == jnp.right_shift(jj, CHUNK_SHIFT)
    return same & (jj <= ii), same & (jj > ii)


def _first_row(chunk):
    return chunk * CHUNK if isinstance(chunk, int) else pl.multiple_of(chunk * CHUNK, CHUNK)


def _chunk_totals(b, fwd):
    hk = b.shape[1]
    rows = [b[c * CHUNK + CHUNK - 1:(c + 1) * CHUNK] if fwd else b[c * CHUNK:c * CHUNK + 1]
            for c in range(b.shape[0] // CHUNK)]
    return jnp.concatenate([jnp.broadcast_to(r, (CHUNK, hk)) for r in rows], axis=0)


def _log_gate(lr_rows, w_ref, b_ref, first_group, hk):
    z = _dot(lr_rows, w_ref[...]) + b_ref[...]
    e = jnp.exp(-jnp.abs(z))
    g = (jnp.minimum(z, 0.0) - jnp.log(1.0 + e)) * (1.0 / GATE_NORMALIZER)
    dg_dz = jnp.where(z >= 0.0, e, 1.0) / (1.0 + e) * (1.0 / GATE_NORMALIZER)
    row = lax.broadcasted_iota(jnp.int32, (lr_rows.shape[0], hk), 0)
    pad = first_group & (row < PAD_ROWS)
    return jnp.where(pad, 0.0, g), jnp.where(pad, 0.0, dg_dz)


def _gla_fwd(proj_b, lr, wg_f, bg_f, wg_b, bg_b, gla_g, dm):
    lp, hk, hv, nc, c0, hw = dm.LP, dm.HK, dm.HV, dm.NC, dm.C0, dm.HW
    scale = hk ** -0.5
    gc = _group_chunks(dm)
    gr, ng = gc * CHUNK, (nc - c0) // gc

    def body(p_ref, lr_ref, wf_ref, bf_ref, wb_ref, bb_ref, gg_ref, o_ref, y_ref, st_ref, b_out, gs_out, oacc_f, oacc_b):
        low_incl, up_strict = _group_masks(gr)
        if c0 > 0:
            zr = c0 * CHUNK
            o_ref[0:zr, :] = jnp.zeros((zr, hv), BF16)
            y_ref[0:zr, :] = jnp.zeros((zr, hv), BF16)
            b_out[:, 0:zr, :] = jnp.zeros((2, zr, hk), F32)
            gs_out[:, 0:zr, :] = jnp.zeros((2, zr, hk), F32)
            st_ref[0, 0, :, 0:c0] = jnp.zeros((2, c0, hv, hk), BF16)

        def decay(gi, fwd):
            w_ref, b_ref = (wf_ref, bf_ref) if fwd else (wb_ref, bb_ref)
            r0 = _first_row(c0 + gi * gc)
            yield
            g, dg_dz = _log_gate(lr_ref[pl.ds(r0, gr), :], w_ref, b_ref, gi == 0, hk)
            gs_out[0 if fwd else 1, pl.ds(r0, gr), :] = dg_dz
            yield
            b = _chunk_cumsum(g, not fwd)
            b_out[0 if fwd else 1, pl.ds(r0, gr), :] = b
            return b

        def group(gi, st, b, fwd):
            oacc = oacc_f if fwd else oacc_b
            r0 = pl.multiple_of((c0 + gi * gc) * CHUNK, CHUNK)
            blk = p_ref[pl.ds(r0, gr), :]
            q = blk[:, :hk].astype(F32) * scale
            k = blk[:, hk:2 * hk].astype(F32)
            v = blk[:, 2 * hk:2 * hk + hv]
            btot = _chunk_totals(b, fwd)
            qi = (q * jnp.exp(b)).astype(BF16)
            ki = (k * jnp.exp(-b)).astype(BF16)
            kd = (k * jnp.exp(btot - b)).astype(BF16)
            dec = jnp.exp(btot)
            a = _dot_nt(qi, ki)
            yield
            o = _dot(jnp.where(low_incl if fwd else up_strict, a, 0.0).astype(BF16), v)
            chunk_rows = [slice(c * CHUNK, (c + 1) * CHUNK) for c in range(gc)]
            kv = [_dot_tn(v[rows], kd[rows]) for rows in chunk_rows]
            for c in (range(gc) if fwd else reversed(range(gc))):
                yield
                rows = chunk_rows[c]
                st_b = st.astype(BF16)
                st_ref[0, 0, 0 if fwd else 1, c0 + gi * gc + c] = st_b
                oacc[pl.ds(r0 + c * CHUNK, CHUNK), :] = o[rows] + _dot_nt(qi[rows], st_b)
                st = st * dec[c * CHUNK:c * CHUNK + 1] + kv[c]
            return st

        def step(i, carry):
            st_f, st_b, b_f, b_b = carry
            gf, gb = i, ng - 1 - i
            return tuple(_interleave([group(gf, st_f, b_f, True), group(gb, st_b, b_b, False),
                                      decay(jnp.minimum(gf + 1, ng - 1), True), decay(jnp.maximum(gb - 1, 0), False)]))

        zero = jnp.zeros((hv, hk), F32)
        lax.fori_loop(0, ng, step, (zero, zero, *_interleave([decay(0, True), decay(ng - 1, False)])))

        def finish(i, carry):
            r0 = pl.multiple_of((c0 + i * gc) * CHUNK, CHUNK)
            o = oacc_f[pl.ds(r0, gr), :] + oacc_b[pl.ds(r0, gr), :]
            r = p_ref[pl.ds(r0, gr), 2 * hk + hv:].astype(F32)
            on = o * lax.rsqrt(jnp.mean(o * o, axis=-1, keepdims=True) + EPS) * gg_ref[...]
            o_ref[pl.ds(r0, gr), :] = o.astype(BF16)
            y_ref[pl.ds(r0, gr), :] = (on * r * _sigmoid(r)).astype(BF16)
            return carry

        lax.fori_loop(0, ng, finish, 0)

    head = lambda s, h: (s, h)
    wspec = pl.BlockSpec((LR_LANES, hk), lambda s, h: (0, h))
    bspec = pl.BlockSpec((1, hk), lambda s, h: (0, h))
    return pl.pallas_call(
        body, name="gla_fwd", grid=(dm.Bl, HEADS),
        in_specs=[pl.BlockSpec((lp, hw), head), pl.BlockSpec((lp, LR_LANES), lambda s, h: (s, 0)),
                  wspec, bspec, wspec, bspec, pl.BlockSpec((1, hv), lambda s, h: (0, 0))],
        out_specs=[pl.BlockSpec((lp, hv), head), pl.BlockSpec((lp, hv), head),
                   pl.BlockSpec((1, 1, 2, nc, hv, hk), lambda s, h: (s, h, 0, 0, 0, 0)),
                   pl.BlockSpec((2, lp, hk), lambda s, h: (0, s, h)), pl.BlockSpec((2, lp, hk), lambda s, h: (0, s, h))],
        out_shape=[jax.ShapeDtypeStruct((dm.T, dm.DV), BF16), jax.ShapeDtypeStruct((dm.T, dm.DV), BF16),
                   jax.ShapeDtypeStruct((dm.Bl, HEADS, 2, nc, hv, hk), BF16),
                   jax.ShapeDtypeStruct((2, dm.T, dm.DK), F32), jax.ShapeDtypeStruct((2, dm.T, dm.DK), F32)],
        scratch_shapes=[pltpu.VMEM((lp, hv), F32), pltpu.VMEM((lp, hv), F32)],
        compiler_params=_cp(2),
    )(proj_b, lr, wg_f, bg_f, wg_b, bg_b, gla_g)


def _gla_bwd(proj_b, lr, o_all, dy_gla, states, decays, gate_slopes, wg_f, wg_b, gla_g, dm):
    lp, hk, hv, nc, c0, hw = dm.LP, dm.HK, dm.HV, dm.NC, dm.C0, dm.HW
    scale = hk ** -0.5
    gc = _group_chunks(dm)
    gr, ng = gc * CHUNK, (nc - c0) // gc

    def body(p_ref, lr_ref, o_ref, dy_ref, st_ref, b_ref, gs_ref, wf_ref, wb_ref, gg_ref,
             d_ref, dlr_ref, gwf_ref, gbf_ref, gwb_ref, gbb_ref, ggg_ref, do_s, dq_s, dk_s, dv_s, dz_s):
        low_incl, up_strict = _group_masks(gr)
        h = pl.program_id(1)

        @pl.when(h == 0)
        def _():
            dlr_ref[...] = jnp.zeros_like(dlr_ref)

        if c0 > 0:
            zr = c0 * CHUNK
            d_ref[0:zr, :] = jnp.zeros((zr, hw), BF16)
        for acc in (dq_s, dk_s, dv_s):
            acc[...] = jnp.zeros_like(acc)

        def norm_bwd(i, ggg):
            r0 = pl.multiple_of((c0 + i * gc) * CHUNK, CHUNK)
            o = o_ref[pl.ds(r0, gr), :].astype(F32)
            dy = dy_ref[pl.ds(r0, gr), :].astype(F32)
            r = p_ref[pl.ds(r0, gr), 2 * hk + hv:].astype(F32)
            rstd = lax.rsqrt(jnp.mean(o * o, axis=-1, keepdims=True) + EPS)
            ohat = o * rstd
            sg = _sigmoid(r)
            d_on = dy * (r * sg)
            d_ref[pl.ds(r0, gr), 2 * hk + hv:] = (dy * ohat * gg_ref[...] * (sg * (1.0 + r * (1.0 - sg)))).astype(BF16)
            d_oh = d_on * gg_ref[...]
            do_s[pl.ds(r0, gr), :] = (rstd * (d_oh - ohat * jnp.mean(d_oh * ohat, axis=-1, keepdims=True))).astype(BF16)
            return ggg + jnp.sum(d_on * ohat, axis=0, keepdims=True)

        ggg = lax.fori_loop(0, ng, norm_bwd, jnp.zeros((1, hv), F32))

        @pl.when((pl.program_id(0) == 0) & (h == 0))
        def _():
            ggg_ref[...] = jnp.zeros_like(ggg_ref)

        ggg_ref[0:1, :] += ggg

        def load(gi):
            r0 = pl.multiple_of((c0 + gi * gc) * CHUNK, CHUNK)
            blk = p_ref[pl.ds(r0, gr), :]
            return r0, blk[:, :hk].astype(F32) * scale, blk[:, hk:2 * hk].astype(F32), blk[:, 2 * hk:2 * hk + hv]

        zero = jnp.zeros((hv, hk), F32)

        def grad(gi, carry, fwd):
            dst, gb = carry
            way = 0 if fwd else 1
            mask = low_incl if fwd else up_strict
            r0, q, k, v = load(gi)
            b = b_ref[way, pl.ds(r0, gr), :]
            btot = _chunk_totals(b, fwd)
            eb, enb, edb, dec = jnp.exp(b), jnp.exp(-b), jnp.exp(btot - b), jnp.exp(btot)
            qi_f, ki_f, kd_f = q * eb, k * enb, k * edb
            qi, ki, kd = qi_f.astype(BF16), ki_f.astype(BF16), kd_f.astype(BF16)
            do = do_s[pl.ds(r0, gr), :]
            a = _dot_nt(qi, ki)
            da = _dot_nt(do, v)
            yield
            a = jnp.where(mask, a, 0.0).astype(BF16)
            da = jnp.where(mask, da, 0.0).astype(BF16)
            dv = _dot_tn(a, do)
            dqi = _dot(da, ki)
            dki = _dot_tn(da, qi)
            dv_c, dqi_c, dkd_c, extra_c = [None] * gc, [None] * gc, [None] * gc, [None] * gc
            chunk_rows = [slice(c * CHUNK, (c + 1) * CHUNK) for c in range(gc)]
            qdo = [_dot_tn(do[rows], qi[rows]) for rows in chunk_rows]
            for c in (reversed(range(gc)) if fwd else range(gc)):
                yield
                rows = chunk_rows[c]
                st = st_ref[0, 0, way, c0 + gi * gc + c]
                dsn_b = dst.astype(BF16)
                dec_c = dec[c * CHUNK:c * CHUNK + 1]
                dv_c[c] = dv[rows] + _dot_nt(kd[rows], dsn_b)
                dqi_c[c] = dqi[rows] + _dot(do[rows], st)
                dkd_c[c] = _dot(v[rows], dsn_b)
                ddec = jnp.sum(st.astype(F32) * dst, axis=0, keepdims=True)
                extra = jnp.sum(dkd_c[c] * kd_f[rows], axis=0, keepdims=True) + ddec * dec_c
                extra_c[c] = jnp.broadcast_to(extra, (CHUNK, hk))
                dst = dst * dec_c + qdo[c]
            yield
            dv, dqi = jnp.concatenate(dv_c, axis=0), jnp.concatenate(dqi_c, axis=0)
            dkd, extra = jnp.concatenate(dkd_c, axis=0), jnp.concatenate(extra_c, axis=0)
            dq_s[pl.ds(r0, gr), :] += dqi * eb * scale
            dk_s[pl.ds(r0, gr), :] += dki * enb + dkd * edb
            dv_s[pl.ds(r0, gr), :] += dv
            db = dqi * qi_f - dki * ki_f - dkd * kd_f
            dg = _chunk_cumsum(db, fwd) + extra
            yield
            dz = dg * gs_ref[way, pl.ds(r0, gr), :]
            dz_s[way, pl.ds(r0, gr), :] = dz.astype(BF16)
            return dst, gb + jnp.sum(dz, axis=0, keepdims=True)

        def grad_step(i, carry):
            return tuple(_interleave([grad(ng - 1 - i, carry[0], True), grad(i, carry[1], False)]))

        init = (zero, jnp.zeros((1, hk), F32))
        (_, gb_f), (_, gb_b) = lax.fori_loop(0, ng, grad_step, (init, init))
        used = slice(c0 * CHUNK, lp)
        for way, (w_ref, gw_ref, gb_ref, gb) in enumerate(((wf_ref, gwf_ref, gbf_ref, gb_f), (wb_ref, gwb_ref, gbb_ref, gb_b))):
            dlr_ref[used, :] += _dot_nt(dz_s[way, used, :], w_ref[...])
            gw_ref[0] = _dot_tn(lr_ref[used, :], dz_s[way, used, :])
            gb_ref[0] = jnp.zeros((8, hk), F32)
            gb_ref[0, 0:1, :] = gb

        def combine(i, carry):
            r0 = pl.multiple_of((c0 + i * gc) * CHUNK, CHUNK)
            d_ref[pl.ds(r0, gr), 0:hk] = dq_s[pl.ds(r0, gr), :].astype(BF16)
            d_ref[pl.ds(r0, gr), hk:2 * hk] = dk_s[pl.ds(r0, gr), :].astype(BF16)
            d_ref[pl.ds(r0, gr), 2 * hk:2 * hk + hv] = dv_s[pl.ds(r0, gr), :].astype(BF16)
            return carry

        lax.fori_loop(0, ng, combine, 0)

    head = lambda s, h: (s, h)
    wspec = pl.BlockSpec((LR_LANES, hk), lambda s, h: (0, h))
    gwspec = pl.BlockSpec((1, LR_LANES, hk), lambda s, h: (s, 0, h))
    gbspec = pl.BlockSpec((1, 8, hk), lambda s, h: (s, 0, h))
    gw_shape = jax.ShapeDtypeStruct((dm.Bl, LR_LANES, dm.DK), F32)
    gb_shape = jax.ShapeDtypeStruct((dm.Bl, 8, dm.DK), F32)
    both = pl.BlockSpec((2, lp, hk), lambda s, h: (0, s, h))
    return pl.pallas_call(
        body, name="gla_bwd", grid=(dm.Bl, HEADS),
        in_specs=[pl.BlockSpec((lp, hw), head), pl.BlockSpec((lp, LR_LANES), lambda s, h: (s, 0)),
                  pl.BlockSpec((lp, hv), head), pl.BlockSpec((lp, hv), head),
                  pl.BlockSpec((1, 1, 2, nc, hv, hk), lambda s, h: (s, h, 0, 0, 0, 0)), both, both,
                  wspec, wspec, pl.BlockSpec((1, hv), lambda s, h: (0, 0))],
        out_specs=[pl.BlockSpec((lp, hw), head), pl.BlockSpec((lp, LR_LANES), lambda s, h: (s, 0)),
                   gwspec, gbspec, gwspec, gbspec, pl.BlockSpec((8, hv), lambda s, h: (0, 0))],
        out_shape=[jax.ShapeDtypeStruct((dm.T, HEADS * hw), BF16), jax.ShapeDtypeStruct((dm.T, LR_LANES), F32),
                   gw_shape, gb_shape, gw_shape, gb_shape, jax.ShapeDtypeStruct((8, hv), F32)],
        scratch_shapes=[pltpu.VMEM((lp, hv), BF16), pltpu.VMEM((lp, hk), F32), pltpu.VMEM((lp, hk), F32),
                        pltpu.VMEM((lp, hv), F32), pltpu.VMEM((2, lp, hk), BF16)],
        compiler_params=_cp(2),
    )(proj_b, lr, o_all, dy_gla, states, decays, gate_slopes, wg_f, wg_b, gla_g)


def _stream_tiles(n_tiles, loads, stores, compute):
    for cp in loads(0, 0):
        cp.start()

    def step(t, carry):
        slot = t % 2

        @pl.when(t + 1 < n_tiles)
        def _():
            for cp in loads(t + 1, 1 - slot):
                cp.start()

        for cp in loads(t, slot):
            cp.wait()

        @pl.when(t >= 2)
        def _():
            for cp in stores(t - 2, slot):
                cp.wait()

        compute(t, slot)
        for cp in stores(t, slot):
            cp.start()
        return carry

    lax.fori_loop(0, n_tiles, step, 0)
    for t in range(max(n_tiles - 2, 0), n_tiles):
        for cp in stores(t, t % 2):
            cp.wait()


def _token_tiles(dm, target_rows=512):
    rows = _pick(dm.S, target_rows, 16)
    per_seq = dm.S // rows
    return rows, dm.Bl * per_seq, lambda t: pl.multiple_of((t // per_seq) * dm.LP + dm.TM + (t % per_seq) * rows, 16)


def _head(y_conv, y_gla, proj_c, w_oc, w_og, w_out, x, target, g_post, dm):
    d, tm = dm.D, dm.TM
    rows, n_tiles, first_row = _token_tiles(dm, 256)
    n_out = 8

    def body(*refs):
        yc_hbm, yg_hbm, c_hbm, woc_ref, wog_ref, wo_ref, x_hbm, t_hbm, g_ref = refs[:9]
        outs, st_ref = refs[9:9 + n_out], refs[9 + n_out]
        ycbuf, ygbuf, cbuf, xbuf, tbuf = refs[10 + n_out:15 + n_out]
        obufs = refs[15 + n_out:15 + 2 * n_out]
        zbuf, zbuf2, sem_in, sem_out, sem_zero = refs[15 + 2 * n_out:]

        def loads(t, slot):
            padded = [(yc_hbm, ycbuf), (yg_hbm, ygbuf), (c_hbm, cbuf)]
            own = [(x_hbm, xbuf), (t_hbm, tbuf)]
            return ([pltpu.make_async_copy(h.at[pl.ds(first_row(t), rows), :], b.at[slot], sem_in.at[i, slot])
                     for i, (h, b) in enumerate(padded)] +
                    [pltpu.make_async_copy(h.at[pl.ds(t * rows, rows), :], b.at[slot], sem_in.at[3 + i, slot])
                     for i, (h, b) in enumerate(own)])

        def stores(t, slot):
            return [pltpu.make_async_copy(b.at[slot], h.at[pl.ds(first_row(t), rows), :], sem_out.at[i, slot])
                    for i, (h, b) in enumerate(zip(outs, obufs))]

        def compute(t, slot):
            mg_o, do_o, dy_o, dpc_o, dpg_o, dc_o, dyc_o, dyg_o = obufs
            pc = _dot(ycbuf[slot], woc_ref[...])
            pg = _dot(ygbuf[slot], wog_ref[...])
            sa = _sigmoid(cbuf[slot, :, :d].astype(F32))
            sb = _sigmoid(cbuf[slot, :, d:].astype(F32))
            merged = (sa * pc + sb * pg).astype(BF16)
            mg_o[slot] = merged
            out = _dot(merged, wo_ref[...])
            rstd = lax.rsqrt(jnp.mean(out * out, axis=-1, keepdims=True) + EPS)
            ohat = out * rstd
            err = xbuf[slot] + ohat * g_ref[...] - tbuf[slot]
            dy = err * (1.0 / d)
            d_oh = dy * g_ref[...]
            d_out = (rstd * (d_oh - ohat * jnp.mean(d_oh * ohat, axis=-1, keepdims=True))).astype(BF16)
            do_o[slot] = d_out
            dy_o[slot] = dy.astype(BF16)
            st_ref[0:1, :] += jnp.sum(dy * ohat, axis=0, keepdims=True)
            st_ref[1:2, :] += jnp.sum(err * err, axis=0, keepdims=True)
            dmg = _dot_nt(d_out, wo_ref[...])
            dpc = (dmg * sa).astype(BF16)
            dpg = (dmg * sb).astype(BF16)
            dpc_o[slot] = dpc
            dpg_o[slot] = dpg
            dc_o[slot, :, :d] = (dmg * pc * sa * (1.0 - sa)).astype(BF16)
            dc_o[slot, :, d:] = (dmg * pg * sb * (1.0 - sb)).astype(BF16)
            dyc_o[slot] = _dot_nt(dpc, woc_ref[...]).astype(BF16)
            dyg_o[slot] = _dot_nt(dpg, wog_ref[...]).astype(BF16)

        st_ref[...] = jnp.zeros_like(st_ref)
        zbuf[...] = jnp.zeros_like(zbuf)
        zbuf2[...] = jnp.zeros_like(zbuf2)
        zeros = [pltpu.make_async_copy(zbuf2 if out.shape[1] == 2 * d else zbuf, out.at[pl.ds(b * dm.LP, tm), :], sem_zero.at[i, b])
                 for i, out in enumerate(outs) for b in range(dm.Bl)]
        for cp in zeros:
            cp.start()
        _stream_tiles(n_tiles, loads, stores, compute)
        for cp in zeros:
            cp.wait()

    any_spec, vmem = pl.BlockSpec(memory_space=pl.ANY), pl.BlockSpec(memory_space=pltpu.VMEM)
    widths = [d, d, d, d, d, 2 * d, d, d]
    tile = lambda w, dt: pltpu.VMEM((2, rows, w), dt)
    return pl.pallas_call(
        body, name="head", in_specs=[any_spec] * 3 + [vmem] * 3 + [any_spec] * 2 + [vmem],
        out_specs=[any_spec] * n_out + [vmem],
        out_shape=[jax.ShapeDtypeStruct((dm.T, w), BF16) for w in widths] + [jax.ShapeDtypeStruct((8, d), F32)],
        scratch_shapes=[tile(d, BF16), tile(d, BF16), tile(2 * d, BF16), tile(d, F32), tile(d, F32)]
        + [tile(w, BF16) for w in widths]
        + [pltpu.VMEM((tm, d), BF16), pltpu.VMEM((tm, 2 * d), BF16), pltpu.SemaphoreType.DMA((5, 2)),
           pltpu.SemaphoreType.DMA((n_out, 2)), pltpu.SemaphoreType.DMA((n_out, dm.Bl))],
        compiler_params=pltpu.CompilerParams(vmem_limit_bytes=VMEM_LIMIT_BYTES),
    )(y_conv, y_gla, proj_c, w_oc, w_og, w_out, x.reshape(dm.Bl * dm.S, d), target.reshape(dm.Bl * dm.S, d), g_post)


def _grad_h(d_parts, gathered, dy, x, metapad, g_pre, dm):
    d, tm = dm.D, dm.TM
    rows, n_tiles, first_row = _token_tiles(dm, 256)
    widths = [a.shape[1] for a in d_parts]
    np_ = len(d_parts)

    def body(*refs):
        d_hbm, g_hbm, dy_hbm, x_hbm, mp_ref, g_ref = refs[:np_], refs[np_], refs[np_ + 1], refs[np_ + 2], refs[np_ + 3], refs[np_ + 4]
        gx_hbm, dmeta_ref, gg_ref = refs[np_ + 5:np_ + 8]
        parts, edges, sems = refs[np_ + 8:np_ + 12], refs[np_ + 12], refs[np_ + 13]
        dbufs = refs[np_ + 14:2 * np_ + 14]
        dybuf, xbuf, gbuf = refs[2 * np_ + 14:2 * np_ + 17]
        mbufs = refs[2 * np_ + 17:3 * np_ + 17]
        sem_in, sem_out, sem_meta = refs[3 * np_ + 17:]

        def grad_u(tiles):
            du = _dot(tiles[0].astype(BF16), parts[0][...])
            for a, w in zip(tiles[1:], parts[1:]):
                du = du + _dot(a.astype(BF16), w[...])
            return du

        def norm_bwd(h, du, dy):
            rstd = lax.rsqrt(jnp.mean(h * h, axis=-1, keepdims=True) + EPS)
            hhat = h * rstd
            dug = du * g_ref[...]
            gg_ref[0:1, :] += jnp.sum(du * hhat, axis=0, keepdims=True)
            return dy + rstd * (dug - hhat * jnp.mean(dug * hhat, axis=-1, keepdims=True))

        def loads(t, slot):
            padded = list(zip(d_hbm, dbufs)) + [(dy_hbm, dybuf)]
            return ([pltpu.make_async_copy(h.at[pl.ds(first_row(t), rows), :], b.at[slot], sem_in.at[i, slot])
                     for i, (h, b) in enumerate(padded)] +
                    [pltpu.make_async_copy(x_hbm.at[pl.ds(t * rows, rows), :], xbuf.at[slot], sem_in.at[np_ + 1, slot])])

        def stores(t, slot):
            return [pltpu.make_async_copy(gbuf.at[slot], gx_hbm.at[pl.ds(t * rows, rows), :], sem_out.at[slot])]

        def compute(t, slot):
            gbuf[slot] = norm_bwd(xbuf[slot], grad_u([b[slot] for b in dbufs]), dybuf[slot].astype(F32))

        gg_ref[...] = jnp.zeros_like(gg_ref)
        meta = [pltpu.make_async_copy(h.at[pl.ds(b * dm.LP, tm), :], buf.at[pl.ds(b * tm, tm), :], sem_meta.at[i, b])
                for i, (h, buf) in enumerate(zip(d_hbm, mbufs)) for b in range(dm.Bl)]
        for cp in meta:
            cp.start()
        _load_packed(g_hbm, parts, edges, sems, dm)
        _stream_tiles(n_tiles, loads, stores, compute)
        for cp in meta:
            cp.wait()
        dmeta_ref[...] = norm_bwd(jnp.concatenate([mp_ref[...]] * dm.Bl, axis=0), grad_u([buf[...] for buf in mbufs]), 0.0)

    any_spec, vmem = pl.BlockSpec(memory_space=pl.ANY), pl.BlockSpec(memory_space=pltpu.VMEM)
    grad_x, d_meta, gg = pl.pallas_call(
        body, name="grad_h", in_specs=[any_spec] * (np_ + 3) + [vmem, vmem], out_specs=[any_spec, vmem, vmem],
        out_shape=[jax.ShapeDtypeStruct((dm.Bl * dm.S, d), F32), jax.ShapeDtypeStruct((dm.Bl * tm, d), F32),
                   jax.ShapeDtypeStruct((8, d), F32)],
        scratch_shapes=_packed_scratch(dm)
        + [pltpu.VMEM((2, rows, w), a.dtype) for w, a in zip(widths, d_parts)]
        + [pltpu.VMEM((2, rows, d), BF16), pltpu.VMEM((2, rows, d), F32), pltpu.VMEM((2, rows, d), F32)]
        + [pltpu.VMEM((dm.Bl * tm, w), a.dtype) for w, a in zip(widths, d_parts)]
        + [pltpu.SemaphoreType.DMA((np_ + 2, 2)), pltpu.SemaphoreType.DMA((2,)), pltpu.SemaphoreType.DMA((np_, dm.Bl))],
        compiler_params=pltpu.CompilerParams(vmem_limit_bytes=VMEM_LIMIT_BYTES),
    )(*d_parts, gathered, dy, x.reshape(dm.Bl * dm.S, d), metapad, g_pre)
    return grad_x.reshape(dm.Bl, dm.S, d), d_meta.reshape(dm.Bl, tm, d), gg


def _adamw(partials, w, m, v, name, by_columns=False):
    r, c = w.shape
    n_parts, pr = partials.shape[:2]
    assert pr == r or (by_columns and pr == _padded_shard_rows(r))
    tr, tc = (r, _pick(c, 128, 128)) if by_columns else (_pick(r, 256, 16), c)

    def body(p_ref, w_ref, m_ref, v_ref, g_ref, d_ref, nm_ref, nv_ref):
        g = p_ref[0].astype(F32)
        for j in range(1, n_parts):
            g = g + p_ref[j].astype(F32)

        def step(g):
            g_ref[...] = g
            d_ref[...], nm_ref[...], nv_ref[...] = _adam_step(g, w_ref[...], m_ref[...], v_ref[...])

        if pr == r:
            step(g)
        else:
            me = 4 * lax.axis_index("x") + 2 * lax.axis_index("y") + lax.axis_index("c")
            for offset in sorted({_shard_offset(j, r) for j in range(N_DEV)}):
                @pl.when(_shard_offset(me, r) == offset)
                def _(offset=offset):
                    step(g[offset:offset + r])

    at = (lambda i: (0, i)) if by_columns else (lambda i: (i, 0))
    tile = pl.BlockSpec((tr, tc), at)
    out = jax.ShapeDtypeStruct((r, c), F32)
    return pl.pallas_call(
        body, name=name, grid=(c // tc if by_columns else r // tr,),
        in_specs=[pl.BlockSpec((n_parts, pr if by_columns else tr, tc), lambda i: (0,) + at(i)), tile, tile, tile],
        out_specs=[tile, tile, tile, tile], out_shape=[out, out, out, out], compiler_params=_cp(1),
    )(partials, w, m, v)


def _adam_step(g, w, m, v):
    m2 = ADAM_B1 * m + (1.0 - ADAM_B1) * g
    v2 = ADAM_B2 * v + (1.0 - ADAM_B2) * (g * g)
    m_hat = m2 / (1.0 - ADAM_B1 ** ADAM_STEP)
    v_hat = v2 / (1.0 - ADAM_B2 ** ADAM_STEP)
    return -ADAM_LR * (m_hat / (jnp.sqrt(v_hat) + ADAM_EPS) + ADAM_WD * w), m2, v2


def _adamw_small(items, name):
    n = len(items)

    def body(*refs):
        ins, outs = refs[:4 * n], refs[4 * n:]
        for i in range(n):
            p_ref, w_ref, m_ref, v_ref = ins[4 * i:4 * i + 4]
            g = p_ref[0]
            for j in range(1, p_ref.shape[0]):
                g = g + p_ref[j]
            delta, m2, v2 = _adam_step(g, w_ref[...], m_ref[...], v_ref[...])
            for o_ref, val in zip(outs[4 * i:4 * i + 4], (g, delta, m2, v2)):
                o_ref[...] = val

    vmem = pl.BlockSpec(memory_space=pltpu.VMEM)
    res = pl.pallas_call(
        body, name=name, in_specs=[vmem] * (4 * n), out_specs=[vmem] * (4 * n),
        out_shape=[jax.ShapeDtypeStruct(w.shape, F32) for _, w, _, _ in items for _ in range(4)],
    )(*[a for item in items for a in item])
    return [res[4 * i:4 * i + 4] for i in range(n)]


def _unpack_moves(dm):
    d, hk, hv, cw, nj, hw = dm.D, dm.HK, dm.HV, dm.CW, dm.NJ, dm.HW
    moves = [((4 * j + part) * cw, cw, part * d + j * cw) for j in range(nj) for part in range(4)]
    q0 = 4 * d
    k0, v0 = q0 + HEADS * hk, q0 + 2 * HEADS * hk
    r0 = v0 + HEADS * hv
    lr0 = r0 + HEADS * hv
    for h in range(HEADS):
        b0 = 4 * d + h * hw
        moves += [(b0, hk, q0 + h * hk), (b0 + hk, hk, k0 + h * hk), (b0 + 2 * hk, hv, v0 + h * hv),
                  (b0 + 2 * hk + hv, hv, r0 + h * hv)]
    moves.append((4 * d + HEADS * hw, 2 * d, lr0 + 2 * RANK))
    return moves, lr0


def _column_shards(g, shard_shape):
    r, c = g.shape
    return g.reshape(r, N_DEV, c // N_DEV).transpose(1, 0, 2).reshape((N_DEV,) + tuple(shard_shape))


def _join_column_shards(parts):
    r, c = parts.shape[-2:]
    return parts.reshape(N_DEV, r, c).transpose(1, 0, 2).reshape(r, N_DEV * c)


def _local_step(x, target, meta, g_pre, u, wt_shards, conv_w, wg_f, bg_f, wg_b, bg_b, gla_g, out_weights, g_post,
                on_matrix_grads=None):
    bl, s, d = x.shape
    dm = _Dims(bl, s, d)
    metapad = jnp.concatenate([jnp.zeros((dm.TM - N_META, d), F32), meta], axis=0)
    wgp_f = jnp.pad(wg_f, ((0, LR_LANES - RANK), (0, 0))).astype(BF16)
    wgp_b = jnp.pad(wg_b, ((RANK, LR_LANES - 2 * RANK), (0, 0))).astype(BF16)

    u = _prenorm_meta(u, metapad, g_pre, dm)
    proj_a, proj_b, proj_c, lr = _inproj(u, wt_shards, dm)
    y_conv = _conv_fwd(proj_a, conv_w, dm)
    o_all, y_gla, states, decays, gate_slopes = _gla_fwd(proj_b, lr, wgp_f, bg_f, wgp_b, bg_b, gla_g, dm)
    w_oc, w_og, w_out = out_weights(y_conv) if callable(out_weights) else out_weights
    merged, d_out, dy, d_pc, d_pg, d_c, dy_conv, dy_gla, stats = _head(y_conv, y_gla, proj_c, w_oc, w_og, w_out, x, target,
                                                                        g_post, dm)

    g_out = _matmul_tn(merged, d_out, BF16, "grad_w_out")
    g_oc = _matmul_tn(y_conv, d_pc, BF16, "grad_w_out_conv")
    g_og = _matmul_tn(y_gla, d_pg, BF16, "grad_w_out_gla")
    if on_matrix_grads is not None:
        conv_w = conv_w + on_matrix_grads(dict(w_out_conv=g_oc, w_out_gla=g_og, w_merge_out=g_out))
    d_a, g_conv = _conv_bwd(proj_a, dy_conv, conv_w, dm)
    d_b, d_lr, gwp_f, gbp_f, gwp_b, gbp_b, g_gla = _gla_bwd(proj_b, lr, o_all, dy_gla, states, decays, gate_slopes, wgp_f, wgp_b, gla_g, dm)
    moves, lr_at = _unpack_moves(dm)
    g_lr = _matmul_tn(d_lr, u, BF16, "grad_w_in_gate")[:2 * RANK]
    g_in = _matmul_tn_group([d_a, d_b, d_c], u, moves, 9 * d + 2 * RANK, (g_lr, lr_at), "grad_w_in", tile=d)
    if on_matrix_grads is not None:
        d_lr = d_lr + on_matrix_grads(dict(w_in=g_in))
    grad_x, d_meta, g_pre_rows = _grad_h([d_a, d_b, d_c, d_lr], wt_shards, dy, x, metapad, g_pre, dm)

    grads = dict(
        meta_tokens=jnp.sum(d_meta[:, dm.TM - N_META:, :], axis=0), norm_pre=g_pre_rows[0:1], w_in=g_in,
        conv_w=g_conv[0:3], w_gate_fwd=jnp.sum(gwp_f, axis=0)[:RANK], b_gate_fwd=jnp.sum(gbp_f, axis=0)[0:1],
        w_gate_bwd=jnp.sum(gwp_b, axis=0)[RANK:2 * RANK], b_gate_bwd=jnp.sum(gbp_b, axis=0)[0:1],
        gla_norm=g_gla[0:1], w_out_conv=g_oc, w_out_gla=g_og, w_merge_out=g_out, norm_post=stats[0:1])
    return stats[1:2], grad_x, grads


MATRICES = ("w_out_conv", "w_out_gla", "w_merge_out")
SMALL_SHARDED = ("meta_tokens", "conv_w", "w_gate_fwd", "w_gate_bwd")
REPLICATED = ("norm_pre", "b_gate_fwd", "b_gate_bwd", "gla_norm", "norm_post")
NAMES = ("meta_tokens", "norm_pre", "w_in", "conv_w", "w_gate_fwd", "b_gate_fwd", "w_gate_bwd", "b_gate_bwd", "gla_norm",
         "w_out_conv", "w_out_gla", "w_merge_out", "norm_post")


def kernel(x, meta_tokens, norm_pre, w_in, conv_w, w_gate_fwd, b_gate_fwd, w_gate_bwd, b_gate_bwd, gla_norm, w_out_conv, w_out_gla, w_merge_out, norm_post, loss_target, m_meta_tokens, m_norm_pre, m_w_in, m_conv_w, m_w_gate_fwd, m_b_gate_fwd, m_w_gate_bwd, m_b_gate_bwd, m_gla_norm, m_w_out_conv, m_w_out_gla, m_w_merge_out, m_norm_post, v_meta_tokens, v_norm_pre, v_w_in, v_conv_w, v_w_gate_fwd, v_b_gate_fwd, v_w_gate_bwd, v_b_gate_bwd, v_gla_norm, v_w_out_conv, v_w_out_gla, v_w_merge_out, v_norm_post):
    w = dict(meta_tokens=meta_tokens, norm_pre=norm_pre, w_in=w_in[0], conv_w=conv_w, w_gate_fwd=w_gate_fwd,
             b_gate_fwd=b_gate_fwd, w_gate_bwd=w_gate_bwd, b_gate_bwd=b_gate_bwd, gla_norm=gla_norm,
             w_out_conv=w_out_conv[0], w_out_gla=w_out_gla[0], w_merge_out=w_merge_out[0], norm_post=norm_post)
    m = dict(meta_tokens=m_meta_tokens, norm_pre=m_norm_pre, w_in=m_w_in[0], conv_w=m_conv_w, w_gate_fwd=m_w_gate_fwd,
             b_gate_fwd=m_b_gate_fwd, w_gate_bwd=m_w_gate_bwd, b_gate_bwd=m_b_gate_bwd, gla_norm=m_gla_norm,
             w_out_conv=m_w_out_conv[0], w_out_gla=m_w_out_gla[0], w_merge_out=m_w_merge_out[0], norm_post=m_norm_post)
    v = dict(meta_tokens=v_meta_tokens, norm_pre=v_norm_pre, w_in=v_w_in[0], conv_w=v_conv_w, w_gate_fwd=v_w_gate_fwd,
             b_gate_fwd=v_b_gate_fwd, w_gate_bwd=v_w_gate_bwd, b_gate_bwd=v_b_gate_bwd, gla_norm=v_gla_norm,
             w_out_conv=v_w_out_conv[0], w_out_gla=v_w_out_gla[0], w_merge_out=v_w_merge_out[0], norm_post=v_norm_post)
    d = x.shape[-1]

    dm = _Dims(*x.shape)
    me = 4 * lax.axis_index("x") + 2 * lax.axis_index("y") + lax.axis_index("c")
    wt_shards, *small_all, u = _gather_two_level(
        [_pad_shard(w["w_in"].T.astype(BF16), me)] + [w[n] for n in SMALL_SHARDED], "gather_weights",
        _prenorm_tokens_side(x, norm_pre, dm))
    started, late_weights = _exchange_start([w[n].astype(BF16) for n in MATRICES], [], small_all[0], "gather_out_weights_start")
    small = {n: _join_column_shards(p) for n, p in zip(SMALL_SHARDED, small_all)}
    small["meta_tokens"] = small["meta_tokens"] + started

    def out_weights(after):
        return tuple(a.reshape(-1, d) for a in _exchange_wait(late_weights, after, "gather_out_weights_wait"))

    pending = []

    def on_matrix_grads(g):
        blocks = [t.reshape(N_DEV, -1, d) if t.shape[0] % (N_DEV * BF16_TILE_ROWS) == 0 else (t, t.shape[0] // N_DEV)
                  for t in g.values()]
        token, state = _exchange_start([], blocks, None, "exchange_grads_start_" + "_".join(g))
        pending.append((tuple(g), state))
        return token

    sq_err_cols, grad_x, grads = _local_step(
        x, loss_target, small["meta_tokens"], norm_pre, u, wt_shards, small["conv_w"], small["w_gate_fwd"], b_gate_fwd,
        small["w_gate_bwd"], b_gate_bwd, gla_norm, out_weights, norm_post, on_matrix_grads)
    received = {}
    for names, state in pending:
        received.update(zip(names, _exchange_wait(state, grad_x, "exchange_grads_wait_" + "_".join(names))))

    exchanged = _exchange([grads[n] for n in REPLICATED] + [sq_err_cols],
                          [_column_shards(grads[n], w[n].shape) for n in SMALL_SHARDED], "exchange_small_grads")
    small_recv = exchanged[:len(REPLICATED)] + exchanged[len(REPLICATED) + 1:]
    loss = 0.5 / d * jnp.sum(exchanged[len(REPLICATED)])

    results = {"w_in": [r.T[None] for r in _adamw(received["w_in"], w["w_in"].T, m["w_in"].T, v["w_in"].T, "adamw_w_in", by_columns=True)]}
    for n in MATRICES:
        results[n] = [r[None] for r in _adamw(received[n], w[n], m[n], v[n], "adamw_" + n)]
    small_names = REPLICATED + SMALL_SHARDED
    results.update(zip(small_names, _adamw_small([(p, w[n], m[n], v[n]) for n, p in zip(small_names, small_recv)], "adamw_small")))
    return (loss, grad_x, *[results[n][i] for i in range(4) for n in NAMES])
```

```python
import jax
import jax.numpy as jnp
from jax import lax
from jax.experimental import pallas as pl
from jax.experimental.pallas import tpu as pltpu

F32 = jnp.float32
BF16 = jnp.bfloat16
MESH = pl.DeviceIdType.MESH

N_META = 16
CHUNK = 64
CHUNK_SHIFT = 6
HEADS = 4
RANK = 16
LR_LANES = 128
PAD_ROWS = CHUNK - N_META
EPS = 1e-6
GATE_NORMALIZER = 16.0
N_DEV = 8
ADAM_LR, ADAM_B1, ADAM_B2, ADAM_EPS, ADAM_WD, ADAM_STEP = 0.001, 0.9, 0.999, 1e-08, 0.01, 10
VMEM_LIMIT_BYTES = 56 * 1024 * 1024


class _Dims:
    def __init__(self, bl, s, d):
        self.Bl, self.S, self.D = bl, s, d
        self.TM = CHUNK
        self.LP = self.TM + s
        self.T = bl * self.LP
        self.TPS = self.LP // self.TM
        self.NC = self.LP // CHUNK
        self.C0 = (self.TM - CHUNK) // CHUNK
        self.DK, self.DV = d // 2, d
        self.HK, self.HV = self.DK // HEADS, self.DV // HEADS
        self.HW = 2 * self.HK + 2 * self.HV
        self.CW = 256 if d % 256 == 0 and d > 256 else d // 4
        self.NJ = d // self.CW


def _pick(n, target, mult):
    t = min(n, target)
    while t >= mult:
        if n % t == 0 and t % mult == 0:
            return t
        t -= mult
    return n


def _cp(n_axes):
    return pltpu.CompilerParams(dimension_semantics=("arbitrary",) * n_axes, vmem_limit_bytes=VMEM_LIMIT_BYTES)


def _sigmoid(x):
    return 1.0 / (1.0 + jnp.exp(-x))


def _dot(a, b):
    return jnp.dot(a, b, preferred_element_type=F32)


def _dot_nt(a, b):
    return lax.dot_general(a, b, (((1,), (1,)), ((), ())), preferred_element_type=F32)


def _dot_tn(a, b):
    return lax.dot_general(a, b, (((0,), (0,)), ((), ())), preferred_element_type=F32)


def _chunk_cumsum(x, reverse):
    rows = x.shape[0]
    r = lax.broadcasted_iota(jnp.int32, x.shape, 0) & (CHUNK - 1)
    step = 1
    while step < CHUNK:
        if reverse:
            x = x + jnp.where(r < CHUNK - step, pltpu.roll(x, rows - step, 0), 0.0)
        else:
            x = x + jnp.where(r >= step, pltpu.roll(x, step, 0), 0.0)
        step *= 2
    return x


def _exchange(gathers, scatters, name):
    arrays = list(gathers) + list(scatters)
    n, ng = len(arrays), len(gathers)

    def body(*refs):
        ins, outs = refs[:n], refs[n:2 * n]
        send_sems, recv_sems, local_sems = refs[2 * n:]
        x, y, c = lax.axis_index("x"), lax.axis_index("y"), lax.axis_index("c")
        me = 4 * x + 2 * y + c
        started = []
        for t in range(n):
            src, dst = ins[t], outs[t]
            own = pltpu.make_async_copy(src if t < ng else src.at[me], dst.at[me], local_sems.at[t])
            own.start()
            started.append(own)
            for k, pos, peer in _peers(x, y, c):
                cp = pltpu.make_async_remote_copy(
                    src_ref=src if t < ng else src.at[peer], dst_ref=dst.at[me],
                    send_sem=send_sems.at[t * (N_DEV - 1) + k - 1], recv_sem=recv_sems.at[t * (N_DEV - 1) + k - 1],
                    device_id=pos, device_id_type=MESH)
                cp.start()
                started.append(cp)
        for cp in started:
            cp.wait()

    out_shape = [jax.ShapeDtypeStruct((N_DEV,) + a.shape if t < ng else a.shape, a.dtype) for t, a in enumerate(arrays)]
    any_spec = pl.BlockSpec(memory_space=pl.ANY)
    return pl.pallas_call(
        body, name=name, out_shape=out_shape, in_specs=[any_spec] * n, out_specs=[any_spec] * n,
        scratch_shapes=[pltpu.SemaphoreType.DMA((n * (N_DEV - 1),)), pltpu.SemaphoreType.DMA((n * (N_DEV - 1),)),
                        pltpu.SemaphoreType.DMA((n,))],
        compiler_params=pltpu.CompilerParams(has_side_effects=True),
    )(*arrays)


def _gather_two_level(arrays, name, side=None):
    n = len(arrays)
    per = N_DEV - 1
    work, side_in, side_in_specs, side_out, side_out_specs, side_scratch = side or (None, [], [], [], [], [])
    n_in, n_out = len(side_in), len(side_out)

    def body(*refs):
        ins, outs = refs[:n], refs[n + n_in:2 * n + n_in]
        send_sems, recv_sems, local_sems = refs[2 * n + n_in + n_out:2 * n + n_in + n_out + 3]
        x, y, c = lax.axis_index("x"), lax.axis_index("y"), lax.axis_index("c")
        sibling = (x, y, 1 - c)
        chips = [(1 - x, y), (x, 1 - y), (1 - x, 1 - y)]
        index = lambda px, py, pc: 4 * px + 2 * py + pc

        def copy(t, k, block, to, from_input=False):
            slab = outs[t].at[index(*block)]
            return pltpu.make_async_remote_copy(
                src_ref=ins[t] if from_input else slab, dst_ref=slab, send_sem=send_sems.at[t * per + k],
                recv_sem=recv_sems.at[t * per + k], device_id=to, device_id_type=MESH)

        own, sent = [], []
        for t in range(n):
            own.append(pltpu.make_async_copy(ins[t], outs[t].at[index(x, y, c)], local_sems.at[t]))
            own[-1].start()
            first = [copy(t, 0, (x, y, c), sibling, True)]
            first += [copy(t, 1 + j, (x, y, c), (*chip, c), True) for j, chip in enumerate(chips)]
            for cp in first:
                cp.start()
            sent += first
        if work is not None:
            work(refs[n:n + n_in], refs[2 * n + n_in:2 * n + n_in + n_out], refs[2 * n + n_in + n_out + 3:])
        for t in range(n):
            for j, chip in enumerate(chips):
                copy(t, 1 + j, (*chip, c), (x, y, c)).wait_recv()
                sent.append(copy(t, 4 + j, (*chip, c), sibling))
                sent[-1].start()
        for t in range(n):
            copy(t, 0, sibling, (x, y, c)).wait_recv()
            for j, chip in enumerate(chips):
                copy(t, 4 + j, (*chip, 1 - c), (x, y, c)).wait_recv()
        for cp in sent:
            cp.wait_send()
        for cp in own:
            cp.wait()

    out_shape = [jax.ShapeDtypeStruct((N_DEV,) + a.shape, a.dtype) for a in arrays]
    any_spec = pl.BlockSpec(memory_space=pl.ANY)
    return pl.pallas_call(
        body, name=name, out_shape=out_shape + list(side_out), in_specs=[any_spec] * n + list(side_in_specs),
        out_specs=[any_spec] * n + list(side_out_specs),
        scratch_shapes=[pltpu.SemaphoreType.DMA((n * per,)), pltpu.SemaphoreType.DMA((n * per,)),
                        pltpu.SemaphoreType.DMA((n,))] + list(side_scratch),
        compiler_params=pltpu.CompilerParams(has_side_effects=True, vmem_limit_bytes=VMEM_LIMIT_BYTES),
    )(*arrays, *side_in)


def _peers(x, y, c):
    out = []
    for k in range(1, N_DEV):
        px = 1 - x if (k >> 2) & 1 else x
        py = 1 - y if (k >> 1) & 1 else y
        pc = 1 - c if k & 1 else c
        out.append((k, (px, py, pc), 4 * px + 2 * py + pc))
    return out


def _exchange_start(gathers, scatters, after, name):
    shard_rows = [None] * len(gathers) + [s[1] if isinstance(s, tuple) else None for s in scatters]
    arrays = list(gathers) + [s[0] if isinstance(s, tuple) else s for s in scatters]
    n, ng = len(arrays), len(gathers)
    hbm = pl.BlockSpec(memory_space=pltpu.HBM)
    sem = pl.BlockSpec(memory_space=pltpu.SEMAPHORE)

    extra = [] if after is None else [after]
    ne = len(extra)

    def body(*refs):
        ins, lands = refs[:n], refs[n:2 * n]
        send_sems, recv_sems = refs[2 * n + ne], refs[2 * n + ne + 1]
        token = refs[4 * n + ne + 2]
        x, y, c = lax.axis_index("x"), lax.axis_index("y"), lax.axis_index("c")
        me = 4 * x + 2 * y + c
        for t in range(n):
            for k, pos, peer in _peers(x, y, c):
                pltpu.make_async_remote_copy(
                    src_ref=_block_for(ins[t], peer, t < ng, shard_rows[t]), dst_ref=lands[t].at[me],
                    send_sem=send_sems.at[t * (N_DEV - 1) + k - 1], recv_sem=recv_sems.at[t * (N_DEV - 1) + k - 1],
                    device_id=pos, device_id_type=MESH).start()
        token[...] = jnp.zeros_like(token)

    me = 4 * lax.axis_index("x") + 2 * lax.axis_index("y") + lax.axis_index("c")

    def own_block(t, a):
        if t < ng:
            return a
        if shard_rows[t] is None:
            return lax.dynamic_index_in_dim(a, me, 0, keepdims=False)
        assert all(_shard_window(j, shard_rows[t]) + _padded_shard_rows(shard_rows[t]) <= a.shape[0] for j in range(N_DEV))
        return lax.dynamic_slice_in_dim(a, _shard_window(me, shard_rows[t]), _padded_shard_rows(shard_rows[t]), 0)

    blocks = [own_block(t, a) for t, a in enumerate(arrays)]
    lands = [lax.dynamic_update_index_in_dim(lax.empty((N_DEV,) + b.shape if t < ng or shard_rows[t] else a.shape, a.dtype), b, me, 0)
             for t, (a, b) in enumerate(zip(arrays, blocks))]
    operands = [pltpu.with_memory_space_constraint(a, pltpu.HBM) for a in arrays + lands]
    sems = pltpu.SemaphoreType.DMA((n * (N_DEV - 1),))
    res = pl.pallas_call(
        body, name=name,
        out_shape=(sems, sems, *[pltpu.HBM(a.shape, a.dtype) for a in arrays + lands], jax.ShapeDtypeStruct((8, 128), F32)),
        in_specs=[hbm] * (2 * n) + [pl.BlockSpec(memory_space=pl.ANY)] * ne,
        out_specs=(sem, sem, *[hbm] * (2 * n), pl.BlockSpec(memory_space=pltpu.VMEM)),
        input_output_aliases={i: 2 + i for i in range(2 * n)},
        compiler_params=pltpu.CompilerParams(has_side_effects=pltpu.SideEffectType.DATAFLOW_SIDE_EFFECTING),
    )(*operands, *extra)
    return res[-1][0, 0], (ng, shard_rows, res[0], res[1], list(res[2:2 + n]), list(res[2 + n:2 + 2 * n]))


def _block_for(ref, peer, whole, shard_rows):
    if whole:
        return ref
    if shard_rows is None:
        return ref.at[peer]
    return ref.at[pl.ds(pl.multiple_of(_shard_window(peer, shard_rows), BF16_TILE_ROWS), _padded_shard_rows(shard_rows))]


def _exchange_wait(state, after, name):
    ng, shard_rows, send_sems, recv_sems, sent, lands = state
    n = len(sent)
    hbm = pl.BlockSpec(memory_space=pltpu.HBM)
    sem = pl.BlockSpec(memory_space=pltpu.SEMAPHORE)

    def body(*refs):
        ins, land_refs = refs[:n], refs[n:2 * n]
        send_ref, recv_ref = refs[2 * n], refs[2 * n + 1]
        x, y, c = lax.axis_index("x"), lax.axis_index("y"), lax.axis_index("c")
        me = 4 * x + 2 * y + c
        for t in range(n):
            for k, pos, peer in _peers(x, y, c):
                cp = pltpu.make_async_remote_copy(
                    src_ref=_block_for(ins[t], peer, t < ng, shard_rows[t]), dst_ref=land_refs[t].at[me],
                    send_sem=send_ref.at[t * (N_DEV - 1) + k - 1], recv_sem=recv_ref.at[t * (N_DEV - 1) + k - 1],
                    device_id=pos, device_id_type=MESH)
                cp.wait_send()
                cp.wait_recv()

    res = pl.pallas_call(
        body, name=name, out_shape=tuple(pltpu.HBM(a.shape, a.dtype) for a in sent + lands),
        in_specs=[hbm] * (2 * n) + [sem, sem, pl.BlockSpec(memory_space=pl.ANY)], out_specs=tuple([hbm] * (2 * n)),
        input_output_aliases={i: i for i in range(2 * n)},
        compiler_params=pltpu.CompilerParams(has_side_effects=pltpu.SideEffectType.DATAFLOW_SIDE_EFFECTING),
    )(*sent, *lands, send_sems, recv_sems, after)
    return list(res[n:])


def _rms_scaled(h, g):
    return (h * lax.rsqrt(jnp.mean(h * h, axis=-1, keepdims=True) + EPS) * g).astype(BF16)


def _prenorm_tokens_side(x, g_pre, dm):
    bl, s, d = x.shape
    rows = _pick(s, 512, 16)
    tiles = [(b, j) for b in range(bl) for j in range(s // rows)]

    def work(ins, outs, scratch):
        (x_ref, g_ref), (u_ref,), (xbuf, ubuf, sem_in, sem_out) = ins, outs, scratch

        def load(t, slot):
            b, j = tiles[t]
            return pltpu.make_async_copy(x_ref.at[b, pl.ds(j * rows, rows), :], xbuf.at[slot], sem_in.at[slot])

        def store(t, slot):
            b, j = tiles[t]
            return pltpu.make_async_copy(ubuf.at[slot], u_ref.at[pl.ds(b * dm.LP + dm.TM + j * rows, rows), :], sem_out.at[slot])

        load(0, 0).start()
        for t in range(len(tiles)):
            slot = t % 2
            if t + 1 < len(tiles):
                load(t + 1, 1 - slot).start()
            load(t, slot).wait()
            if t >= 2:
                store(t - 2, slot).wait()
            ubuf[slot] = _rms_scaled(xbuf[slot], g_ref[...])
            store(t, slot).start()
        for t in range(max(len(tiles) - 2, 0), len(tiles)):
            store(t, t % 2).wait()

    any_spec = pl.BlockSpec(memory_space=pl.ANY)
    return (work, [x, g_pre], [any_spec, pl.BlockSpec(memory_space=pltpu.VMEM)],
            [jax.ShapeDtypeStruct((dm.T, d), BF16)], [any_spec],
            [pltpu.VMEM((2, rows, d), F32), pltpu.VMEM((2, rows, d), BF16), pltpu.SemaphoreType.DMA((2,)),
             pltpu.SemaphoreType.DMA((2,))])


def _prenorm_meta(u, metapad, g_pre, dm):
    tm, tps, d = dm.TM, dm.TPS, dm.D

    def body(u_in, mp_ref, g_ref, u_ref):
        u_ref[...] = _rms_scaled(mp_ref[...], g_ref[...])

    return pl.pallas_call(
        body, name="prenorm_meta", grid=(dm.Bl,),
        in_specs=[pl.BlockSpec(memory_space=pl.ANY), pl.BlockSpec((tm, d), lambda i: (0, 0)),
                  pl.BlockSpec((1, d), lambda i: (0, 0))],
        out_specs=pl.BlockSpec((tm, d), lambda i: (i * tps, 0)),
        out_shape=jax.ShapeDtypeStruct((dm.T, d), BF16), input_output_aliases={0: 0}, compiler_params=_cp(1),
    )(u, metapad, g_pre)


def _matmul_tn(a, b, out_dtype, name, tt=2304, tn=1024, tk=1024):
    t, k = a.shape
    n = b.shape[1]
    tt, tn, tk = _pick(t, tt, 16), _pick(n, tn, 128), _pick(k, tk, 128)
    nt = t // tt

    def body(a_ref, b_ref, o_ref, acc):
        p = _dot_tn(a_ref[...].astype(BF16), b_ref[...].astype(BF16))
        i = pl.program_id(2)

        @pl.when(i == 0)
        def _():
            acc[...] = p

        @pl.when(i > 0)
        def _():
            acc[...] += p

        @pl.when(i == nt - 1)
        def _():
            o_ref[...] = acc[...].astype(out_dtype)

    return pl.pallas_call(
        body, name=name, grid=(k // tk, n // tn, nt),
        in_specs=[pl.BlockSpec((tt, tk), lambda kk, j, i: (i, kk)), pl.BlockSpec((tt, tn), lambda kk, j, i: (i, j))],
        out_specs=pl.BlockSpec((tk, tn), lambda kk, j, i: (kk, j)),
        out_shape=jax.ShapeDtypeStruct((k, n), out_dtype), scratch_shapes=[pltpu.VMEM((tk, tn), F32)],
        compiler_params=_cp(3),
    )(a, b)


def _matmul_tn_group(a_list, b, moves, out_rows, extra, name, tt=2816, tile=1024):
    t, n = b.shape
    tt = _pick(t, tt, 16)
    nt = t // tt
    counts = [a.shape[1] // tile for a in a_list]
    starts = [sum(counts[:m]) for m in range(len(a_list))]
    items = sum(counts)
    extra_rows, extra_at = extra
    cuts = [[] for _ in range(items)]
    for row, rows, at in moves:
        while rows > 0:
            p, r = divmod(row, tile)
            take = min(rows, tile - r)
            cuts[p].append((r, take, at))
            row, rows, at = row + take, rows - take, at + take
    assert all(v % BF16_TILE_ROWS == 0 for cut in cuts for move in cut for v in move)
    assert sum(rows for _, rows, _ in moves) + extra_rows.shape[0] == out_rows

    def active(p, m):
        return (p >= starts[m]) & (p < starts[m] + counts[m])

    def body(*refs):
        a_refs, b_ref, x_ref = refs[:len(a_list)], refs[len(a_list)], refs[len(a_list) + 1]
        o_ref, acc, stage, sems, x_sem, abuf, a_sems = refs[-7:]
        p, i = pl.program_id(0), pl.program_id(1)

        def fetch(p, i, slot, m):
            cols = pl.ds(pl.multiple_of((p - starts[m]) * tile, tile), tile)
            return pltpu.make_async_copy(a_refs[m].at[pl.ds(pl.multiple_of(i * tt, tt), tt), cols], abuf.at[slot], a_sems.at[slot])

        def start_fetch(p, i, slot):
            for m in range(len(a_list)):
                @pl.when(active(p, m))
                def _(m=m):
                    fetch(p, i, slot, m).start()

        step = p * nt + i
        slot = step % 2

        @pl.when(step == 0)
        def _():
            start_fetch(p, i, slot)

        @pl.when(step + 1 < items * nt)
        def _():
            last = i == nt - 1
            start_fetch(jnp.where(last, p + 1, p), jnp.where(last, 0, i + 1), 1 - slot)

        pltpu.make_async_copy(a_refs[0].at[pl.ds(0, tt), pl.ds(0, tile)], abuf.at[slot], a_sems.at[slot]).wait()

        def writes(item):
            return [pltpu.make_async_copy(stage.at[pl.ds(r, rows), :], o_ref.at[pl.ds(at, rows), :], sems.at[s])
                    for s, (r, rows, at) in enumerate(cuts[item])]

        extra_copy = pltpu.make_async_copy(x_ref, o_ref.at[pl.ds(extra_at, extra_rows.shape[0]), :], x_sem.at[0])

        @pl.when((p == 0) & (i == 0))
        def _():
            extra_copy.start()

        prod = _dot_tn(abuf[slot], b_ref[...])

        @pl.when(i == 0)
        def _():
            acc[...] = prod

        @pl.when(i > 0)
        def _():
            acc[...] += prod

        for item in range(items):
            @pl.when((p == item) & (i == nt - 1))
            def _(item=item):
                if item > 0:
                    for cp in writes(item - 1):
                        cp.wait()
                stage[...] = acc[...].astype(BF16)
                for cp in writes(item):
                    cp.start()
                if item == items - 1:
                    for cp in writes(item):
                        cp.wait()
                    extra_copy.wait()

    assert all(a.dtype == BF16 for a in a_list) and b.dtype == BF16
    any_spec = pl.BlockSpec(memory_space=pl.ANY)
    return pl.pallas_call(
        body, name=name, grid=(items, nt),
        in_specs=[any_spec] * len(a_list) + [pl.BlockSpec((tt, n), lambda p, i: (i, 0)), pl.BlockSpec(memory_space=pltpu.VMEM)],
        out_specs=any_spec, out_shape=jax.ShapeDtypeStruct((out_rows, n), BF16),
        scratch_shapes=[pltpu.VMEM((tile, n), F32), pltpu.VMEM((tile, n), BF16),
                        pltpu.SemaphoreType.DMA((max(len(cut) for cut in cuts),)), pltpu.SemaphoreType.DMA((1,)),
                        pltpu.VMEM((2, tt, tile), BF16), pltpu.SemaphoreType.DMA((2,))],
        compiler_params=_cp(2),
    )(*a_list, b, extra_rows)


BF16_TILE_ROWS = 16


def _shard_offset(index, shard_rows):
    return (index * shard_rows) % BF16_TILE_ROWS


def _padded_shard_rows(shard_rows):
    return -(-(shard_rows + max(_shard_offset(j, shard_rows) for j in range(N_DEV))) // BF16_TILE_ROWS) * BF16_TILE_ROWS


def _pad_shard(wt_shard, index):
    rows, d = wt_shard.shape
    return lax.dynamic_update_slice(jnp.zeros((_padded_shard_rows(rows), d), wt_shard.dtype), wt_shard,
                                    (_shard_offset(index, rows), 0))


def _shard_window(index, shard_rows):
    return index * shard_rows - _shard_offset(index, shard_rows)


def _packed_parts(dm):
    d, dk, hk, hv, cw, nj, hw = dm.D, dm.DK, dm.HK, dm.HV, dm.CW, dm.NJ, dm.HW
    blocks = [(0, (j * 4 + p) * cw, p * d + j * cw, cw) for j in range(nj) for p in range(4)]
    for h in range(HEADS):
        blocks += [(1, h * hw, 4 * d + h * hk, hk), (1, h * hw + hk, 4 * d + dk + h * hk, hk),
                   (1, h * hw + 2 * hk, 5 * d + h * hv, hv), (1, h * hw + 2 * hk + hv, 6 * d + h * hv, hv)]
    blocks += [(2, 0, 7 * d + 2 * RANK, 2 * d), (3, 0, 7 * d, 2 * RANK)]
    return [4 * d, 3 * d, 2 * d, LR_LANES], blocks


def _pack_plan(dm):
    sh = (9 * dm.D + 2 * RANK) // N_DEV
    tile = BF16_TILE_ROWS
    copies, straddles = [], []
    for part, dst, r0, n in _packed_parts(dm)[1]:
        for j in range(N_DEV):
            a, b = max(r0, sh * j), min(r0 + n, sh * (j + 1))
            if a >= b:
                continue
            a_up, b_down = -(-a // tile) * tile, b // tile * tile
            if b_down > a_up:
                copies.append((j, a_up - sh * j + _shard_offset(j, sh), b_down - a_up, part, dst + a_up - r0))
            if a % tile:
                lo = a // tile * tile
                straddles.append((j, lo - sh * (j - 1) + _shard_offset(j - 1, sh), part, dst + lo - r0, a - lo))
    return copies, straddles


def _packed_scratch(dm):
    copies, straddles = _pack_plan(dm)
    return ([pltpu.VMEM((rows, dm.D), BF16) for rows in _packed_parts(dm)[0]]
            + [pltpu.VMEM((2 * max(len(straddles), 1), BF16_TILE_ROWS, dm.D), BF16),
               pltpu.SemaphoreType.DMA((len(copies) + 2 * len(straddles),))])


def _load_packed(g_ref, parts, edges, sems, dm):
    copies, straddles = _pack_plan(dm)
    tile = BF16_TILE_ROWS
    parts[3][2 * RANK:, :] = jnp.zeros((LR_LANES - 2 * RANK, dm.D), BF16)
    dmas = [pltpu.make_async_copy(g_ref.at[j, pl.ds(src, n), :], parts[p].at[pl.ds(dst, n), :], sems.at[i])
            for i, (j, src, n, p, dst) in enumerate(copies)]
    for i, (j, src, p, dst, split) in enumerate(straddles):
        k = len(copies) + 2 * i
        dmas.append(pltpu.make_async_copy(g_ref.at[j - 1, pl.ds(src, tile), :], edges.at[2 * i], sems.at[k]))
        dmas.append(pltpu.make_async_copy(g_ref.at[j, pl.ds(0, tile), :], edges.at[2 * i + 1], sems.at[k + 1]))
    for cp in dmas:
        cp.start()
    for cp in dmas:
        cp.wait()
    row = lax.broadcasted_iota(jnp.int32, (tile, dm.D), 0)
    for i, (j, src, p, dst, split) in enumerate(straddles):
        parts[p][dst:dst + tile, :] = jnp.where(row < split, edges[2 * i], edges[2 * i + 1])


def _inproj(u, gathered, dm):
    t, d = u.shape
    tm = _pick(t, 512, 16)
    widths = _packed_parts(dm)[0]
    cn = 1024

    def body(u_ref, g_ref, *rest):
        outs, parts, (edges, sems) = rest[:4], rest[4:8], rest[8:]

        @pl.when(pl.program_id(0) == 0)
        def _():
            _load_packed(g_ref, parts, edges, sems, dm)

        ut = u_ref[...]
        for w, o_ref in zip(parts, outs):
            n = w.shape[0]
            step = cn if n % cn == 0 else n
            for j in range(0, n, step):
                o_ref[:, j:j + step] = _dot_nt(ut, w[j:j + step, :]).astype(BF16)

    return pl.pallas_call(
        body, name="inproj", grid=(t // tm,),
        in_specs=[pl.BlockSpec((tm, d), lambda i: (i, 0)), pl.BlockSpec(memory_space=pl.ANY)],
        out_specs=[pl.BlockSpec((tm, w), lambda i: (i, 0)) for w in widths],
        out_shape=[jax.ShapeDtypeStruct((t, w), BF16) for w in widths],
        scratch_shapes=_packed_scratch(dm), compiler_params=_cp(1),
    )(u, gathered)


def _conv_rows(dm):
    return _pick(dm.LP, 256, 16)


def _shifted(m, prev_row, next_row, rows):
    row = lax.broadcasted_iota(jnp.int32, m.shape, 0)
    m_prev = jnp.where(row == 0, prev_row, pltpu.roll(m, 1, 0))
    m_next = jnp.where(row == rows - 1, next_row, pltpu.roll(m, rows - 1, 0))
    return m_prev, m_next


def _conv_fwd(proj_a, conv_w, dm):
    lp, cw, rc = dm.LP, dm.CW, _conv_rows(dm)
    nchunk = lp // rc

    def body(p_ref, w_ref, y_ref):
        w0, w1, w2 = w_ref[0:1, :], w_ref[1:2, :], w_ref[2:3, :]

        def chunk(ci, carry):
            r0 = pl.multiple_of(ci * rc, rc)
            blk = p_ref[pl.ds(r0, rc), :].astype(F32)
            cb, cc, cx, cz = (blk[:, i * cw:(i + 1) * cw] for i in range(4))
            m = cc * cx
            rp = pl.multiple_of(jnp.maximum(r0 - 16, 0), 16)
            rn = pl.multiple_of(jnp.minimum(r0 + rc, lp - 16), 16)
            pv = p_ref[pl.ds(rp, 16), cw:3 * cw].astype(F32)
            nx = p_ref[pl.ds(rn, 16), cw:3 * cw].astype(F32)
            prev_row = jnp.where(ci > 0, pv[15:16, :cw] * pv[15:16, cw:], 0.0)
            next_row = jnp.where(ci < nchunk - 1, nx[0:1, :cw] * nx[0:1, cw:], 0.0)
            m_prev, m_next = _shifted(m, prev_row, next_row, rc)
            s = w0 * m_prev + w1 * m + w2 * m_next
            y_ref[pl.ds(r0, rc), :] = (cb * s * (cz * _sigmoid(cz))).astype(BF16)
            return carry

        lax.fori_loop(0, nchunk, chunk, 0)

    return pl.pallas_call(
        body, name="conv_fwd", grid=(dm.Bl, dm.NJ),
        in_specs=[pl.BlockSpec((lp, 4 * cw), lambda s, j: (s, j)), pl.BlockSpec((3, cw), lambda s, j: (0, j))],
        out_specs=pl.BlockSpec((lp, cw), lambda s, j: (s, j)),
        out_shape=jax.ShapeDtypeStruct((dm.T, dm.D), BF16), compiler_params=_cp(2),
    )(proj_a, conv_w)


def _conv_bwd(proj_a, dy_conv, conv_w, dm):
    lp, cw, rc = dm.LP, dm.CW, _conv_rows(dm)
    nchunk = lp // rc

    def body(p_ref, dy_ref, w_ref, d_ref, gw_ref):
        w0, w1, w2 = w_ref[0:1, :], w_ref[1:2, :], w_ref[2:3, :]

        def ds_of(p4, dy):
            cb, cz = p4[:, :cw], p4[:, 3 * cw:]
            return dy * cb * (cz * _sigmoid(cz))

        def chunk(ci, carry):
            g0, g1, g2 = carry
            r0 = pl.multiple_of(ci * rc, rc)
            blk = p_ref[pl.ds(r0, rc), :].astype(F32)
            dy = dy_ref[pl.ds(r0, rc), :].astype(F32)
            cb, cc, cx, cz = (blk[:, i * cw:(i + 1) * cw] for i in range(4))
            rp = pl.multiple_of(jnp.maximum(r0 - 16, 0), 16)
            rn = pl.multiple_of(jnp.minimum(r0 + rc, lp - 16), 16)
            pv = p_ref[pl.ds(rp, 16), :].astype(F32)[15:16]
            nx = p_ref[pl.ds(rn, 16), :].astype(F32)[0:1]
            dpv = dy_ref[pl.ds(rp, 16), :].astype(F32)[15:16]
            dnx = dy_ref[pl.ds(rn, 16), :].astype(F32)[0:1]
            has_prev, has_next = ci > 0, ci < nchunk - 1
            m = cc * cx
            m_prev, m_next = _shifted(m, jnp.where(has_prev, pv[:, cw:2 * cw] * pv[:, 2 * cw:3 * cw], 0.0),
                                      jnp.where(has_next, nx[:, cw:2 * cw] * nx[:, 2 * cw:3 * cw], 0.0), rc)
            s = w0 * m_prev + w1 * m + w2 * m_next
            sg = _sigmoid(cz)
            silu = cz * sg
            ds = dy * cb * silu
            ds_prev, ds_next = _shifted(ds, jnp.where(has_prev, ds_of(pv, dpv), 0.0),
                                        jnp.where(has_next, ds_of(nx, dnx), 0.0), rc)
            dm_ = w0 * ds_next + w1 * ds + w2 * ds_prev
            d_ref[pl.ds(r0, rc), 0:cw] = (dy * s * silu).astype(BF16)
            d_ref[pl.ds(r0, rc), cw:2 * cw] = (dm_ * cx).astype(BF16)
            d_ref[pl.ds(r0, rc), 2 * cw:3 * cw] = (dm_ * cc).astype(BF16)
            d_ref[pl.ds(r0, rc), 3 * cw:4 * cw] = (dy * cb * s * (sg * (1.0 + cz * (1.0 - sg)))).astype(BF16)
            return (g0 + jnp.sum(ds * m_prev, axis=0, keepdims=True), g1 + jnp.sum(ds * m, axis=0, keepdims=True),
                    g2 + jnp.sum(ds * m_next, axis=0, keepdims=True))

        z = jnp.zeros((1, cw), F32)
        g0, g1, g2 = lax.fori_loop(0, nchunk, chunk, (z, z, z))

        @pl.when(pl.program_id(1) == 0)
        def _():
            gw_ref[...] = jnp.zeros_like(gw_ref)

        gw_ref[0:1, :] += g0
        gw_ref[1:2, :] += g1
        gw_ref[2:3, :] += g2

    return pl.pallas_call(
        body, name="conv_bwd", grid=(dm.NJ, dm.Bl),
        in_specs=[pl.BlockSpec((lp, 4 * cw), lambda j, s: (s, j)), pl.BlockSpec((lp, cw), lambda j, s: (s, j)),
                  pl.BlockSpec((3, cw), lambda j, s: (0, j))],
        out_specs=[pl.BlockSpec((lp, 4 * cw), lambda j, s: (s, j)), pl.BlockSpec((8, cw), lambda j, s: (0, j))],
        out_shape=[jax.ShapeDtypeStruct((dm.T, 4 * dm.D), BF16), jax.ShapeDtypeStruct((8, dm.D), F32)],
        compiler_params=_cp(2),
    )(proj_a, dy_conv, conv_w)


def _interleave(gens):
    results = [None] * len(gens)
    live = list(range(len(gens)))
    while live:
        for idx in list(live):
            try:
                next(gens[idx])
            except StopIteration as done:
                results[idx] = done.value
                live.remove(idx)
    return results


def _group_chunks(dm):
    n = dm.NC - dm.C0
    return 3 if n % 3 == 0 else 1


def _group_masks(rows):
    ii = lax.broadcasted_iota(jnp.int32, (rows, rows), 0)
    jj = lax.broadcasted_iota(jnp.int32, (rows, rows), 1)
    same = jnp.right_shift(ii, CHUNK_SHIFT) == jnp.right_shift(jj, CHUNK_SHIFT)
    return same & (jj <= ii), same & (jj > ii)


def _first_row(chunk):
    return chunk * CHUNK if isinstance(chunk, int) else pl.multiple_of(chunk * CHUNK, CHUNK)


def _chunk_totals(b, fwd):
    hk = b.shape[1]
    rows = [b[c * CHUNK + CHUNK - 1:(c + 1) * CHUNK] if fwd else b[c * CHUNK:c * CHUNK + 1]
            for c in range(b.shape[0] // CHUNK)]
    return jnp.concatenate([jnp.broadcast_to(r, (CHUNK, hk)) for r in rows], axis=0)


def _log_gate(lr_rows, w_ref, b_ref, first_group, hk):
    z = _dot(lr_rows, w_ref[...]) + b_ref[...]
    e = jnp.exp(-jnp.abs(z))
    g = (jnp.minimum(z, 0.0) - jnp.log(1.0 + e)) * (1.0 / GATE_NORMALIZER)
    dg_dz = jnp.where(z >= 0.0, e, 1.0) / (1.0 + e) * (1.0 / GATE_NORMALIZER)
    row = lax.broadcasted_iota(jnp.int32, (lr_rows.shape[0], hk), 0)
    pad = first_group & (row < PAD_ROWS)
    return jnp.where(pad, 0.0, g), jnp.where(pad, 0.0, dg_dz)


def _gla_fwd(proj_b, lr, wg_f, bg_f, wg_b, bg_b, gla_g, dm):
    lp, hk, hv, nc, c0, hw = dm.LP, dm.HK, dm.HV, dm.NC, dm.C0, dm.HW
    scale = hk ** -0.5
    gc = _group_chunks(dm)
    gr, ng = gc * CHUNK, (nc - c0) // gc

    def body(p_ref, lr_ref, wf_ref, bf_ref, wb_ref, bb_ref, gg_ref, o_ref, y_ref, st_ref, b_out, gs_out, oacc_f, oacc_b):
        low_incl, up_strict = _group_masks(gr)
        if c0 > 0:
            zr = c0 * CHUNK
            o_ref[0:zr, :] = jnp.zeros((zr, hv), BF16)
            y_ref[0:zr, :] = jnp.zeros((zr, hv), BF16)
            b_out[:, 0:zr, :] = jnp.zeros((2, zr, hk), F32)
            gs_out[:, 0:zr, :] = jnp.zeros((2, zr, hk), F32)
            st_ref[0, 0, :, 0:c0] = jnp.zeros((2, c0, hv, hk), BF16)

        def decay(gi, fwd):
            w_ref, b_ref = (wf_ref, bf_ref) if fwd else (wb_ref, bb_ref)
            r0 = _first_row(c0 + gi * gc)
            yield
            g, dg_dz = _log_gate(lr_ref[pl.ds(r0, gr), :], w_ref, b_ref, gi == 0, hk)
            gs_out[0 if fwd else 1, pl.ds(r0, gr), :] = dg_dz
            yield
            b = _chunk_cumsum(g, not fwd)
            b_out[0 if fwd else 1, pl.ds(r0, gr), :] = b
            return b

        def group(gi, st, b, fwd):
            oacc = oacc_f if fwd else oacc_b
            r0 = pl.multiple_of((c0 + gi * gc) * CHUNK, CHUNK)
            blk = p_ref[pl.ds(r0, gr), :]
            q = blk[:, :hk].astype(F32) * scale
            k = blk[:, hk:2 * hk].astype(F32)
            v = blk[:, 2 * hk:2 * hk + hv]
            btot = _chunk_totals(b, fwd)
            qi = (q * jnp.exp(b)).astype(BF16)
            ki = (k * jnp.exp(-b)).astype(BF16)
            kd = (k * jnp.exp(btot - b)).astype(BF16)
            dec = jnp.exp(btot)
            a = _dot_nt(qi, ki)
            yield
            o = _dot(jnp.where(low_incl if fwd else up_strict, a, 0.0).astype(BF16), v)
            chunk_rows = [slice(c * CHUNK, (c + 1) * CHUNK) for c in range(gc)]
            kv = [_dot_tn(v[rows], kd[rows]) for rows in chunk_rows]
            for c in (range(gc) if fwd else reversed(range(gc))):
                yield
                rows = chunk_rows[c]
                st_b = st.astype(BF16)
                st_ref[0, 0, 0 if fwd else 1, c0 + gi * gc + c] = st_b
                oacc[pl.ds(r0 + c * CHUNK, CHUNK), :] = o[rows] + _dot_nt(qi[rows], st_b)
                st = st * dec[c * CHUNK:c * CHUNK + 1] + kv[c]
            return st

        def step(i, carry):
            st_f, st_b, b_f, b_b = carry
            gf, gb = i, ng - 1 - i
            return tuple(_interleave([group(gf, st_f, b_f, True), group(gb, st_b, b_b, False),
                                      decay(jnp.minimum(gf + 1, ng - 1), True), decay(jnp.maximum(gb - 1, 0), False)]))

        zero = jnp.zeros((hv, hk), F32)
        lax.fori_loop(0, ng, step, (zero, zero, *_interleave([decay(0, True), decay(ng - 1, False)])))

        def finish(i, carry):
            r0 = pl.multiple_of((c0 + i * gc) * CHUNK, CHUNK)
            o = oacc_f[pl.ds(r0, gr), :] + oacc_b[pl.ds(r0, gr), :]
            r = p_ref[pl.ds(r0, gr), 2 * hk + hv:].astype(F32)
            on = o * lax.rsqrt(jnp.mean(o * o, axis=-1, keepdims=True) + EPS) * gg_ref[...]
            o_ref[pl.ds(r0, gr), :] = o.astype(BF16)
            y_ref[pl.ds(r0, gr), :] = (on * r * _sigmoid(r)).astype(BF16)
            return carry

        lax.fori_loop(0, ng, finish, 0)

    head = lambda s, h: (s, h)
    wspec = pl.BlockSpec((LR_LANES, hk), lambda s, h: (0, h))
    bspec = pl.BlockSpec((1, hk), lambda s, h: (0, h))
    return pl.pallas_call(
        body, name="gla_fwd", grid=(dm.Bl, HEADS),
        in_specs=[pl.BlockSpec((lp, hw), head), pl.BlockSpec((lp, LR_LANES), lambda s, h: (s, 0)),
                  wspec, bspec, wspec, bspec, pl.BlockSpec((1, hv), lambda s, h: (0, 0))],
        out_specs=[pl.BlockSpec((lp, hv), head), pl.BlockSpec((lp, hv), head),
                   pl.BlockSpec((1, 1, 2, nc, hv, hk), lambda s, h: (s, h, 0, 0, 0, 0)),
                   pl.BlockSpec((2, lp, hk), lambda s, h: (0, s, h)), pl.BlockSpec((2, lp, hk), lambda s, h: (0, s, h))],
        out_shape=[jax.ShapeDtypeStruct((dm.T, dm.DV), BF16), jax.ShapeDtypeStruct((dm.T, dm.DV), BF16),
                   jax.ShapeDtypeStruct((dm.Bl, HEADS, 2, nc, hv, hk), BF16),
                   jax.ShapeDtypeStruct((2, dm.T, dm.DK), F32), jax.ShapeDtypeStruct((2, dm.T, dm.DK), F32)],
        scratch_shapes=[pltpu.VMEM((lp, hv), F32), pltpu.VMEM((lp, hv), F32)],
        compiler_params=_cp(2),
    )(proj_b, lr, wg_f, bg_f, wg_b, bg_b, gla_g)


def _gla_bwd(proj_b, lr, o_all, dy_gla, states, decays, gate_slopes, wg_f, wg_b, gla_g, dm):
    lp, hk, hv, nc, c0, hw = dm.LP, dm.HK, dm.HV, dm.NC, dm.C0, dm.HW
    scale = hk ** -0.5
    gc = _group_chunks(dm)
    gr, ng = gc * CHUNK, (nc - c0) // gc

    def body(p_ref, lr_ref, o_ref, dy_ref, st_ref, b_ref, gs_ref, wf_ref, wb_ref, gg_ref,
             d_ref, dlr_ref, gwf_ref, gbf_ref, gwb_ref, gbb_ref, ggg_ref, do_s, dq_s, dk_s, dv_s, dz_s):
        low_incl, up_strict = _group_masks(gr)
        h = pl.program_id(1)

        @pl.when(h == 0)
        def _():
            dlr_ref[...] = jnp.zeros_like(dlr_ref)

        if c0 > 0:
            zr = c0 * CHUNK
            d_ref[0:zr, :] = jnp.zeros((zr, hw), BF16)
        for acc in (dq_s, dk_s, dv_s):
            acc[...] = jnp.zeros_like(acc)

        def norm_bwd(i, ggg):
            r0 = pl.multiple_of((c0 + i * gc) * CHUNK, CHUNK)
            o = o_ref[pl.ds(r0, gr), :].astype(F32)
            dy = dy_ref[pl.ds(r0, gr), :].astype(F32)
            r = p_ref[pl.ds(r0, gr), 2 * hk + hv:].astype(F32)
            rstd = lax.rsqrt(jnp.mean(o * o, axis=-1, keepdims=True) + EPS)
            ohat = o * rstd
            sg = _sigmoid(r)
            d_on = dy * (r * sg)
            d_ref[pl.ds(r0, gr), 2 * hk + hv:] = (dy * ohat * gg_ref[...] * (sg * (1.0 + r * (1.0 - sg)))).astype(BF16)
            d_oh = d_on * gg_ref[...]
            do_s[pl.ds(r0, gr), :] = (rstd * (d_oh - ohat * jnp.mean(d_oh * ohat, axis=-1, keepdims=True))).astype(BF16)
            return ggg + jnp.sum(d_on * ohat, axis=0, keepdims=True)

        ggg = lax.fori_loop(0, ng, norm_bwd, jnp.zeros((1, hv), F32))

        @pl.when((pl.program_id(0) == 0) & (h == 0))
        def _():
            ggg_ref[...] = jnp.zeros_like(ggg_ref)

        ggg_ref[0:1, :] += ggg

        def load(gi):
            r0 = pl.multiple_of((c0 + gi * gc) * CHUNK, CHUNK)
            blk = p_ref[pl.ds(r0, gr), :]
            return r0, blk[:, :hk].astype(F32) * scale, blk[:, hk:2 * hk].astype(F32), blk[:, 2 * hk:2 * hk + hv]

        zero = jnp.zeros((hv, hk), F32)

        def grad(gi, carry, fwd):
            dst, gb = carry
            way = 0 if fwd else 1
            mask = low_incl if fwd else up_strict
            r0, q, k, v = load(gi)
            b = b_ref[way, pl.ds(r0, gr), :]
            btot = _chunk_totals(b, fwd)
            eb, enb, edb, dec = jnp.exp(b), jnp.exp(-b), jnp.exp(btot - b), jnp.exp(btot)
            qi_f, ki_f, kd_f = q * eb, k * enb, k * edb
            qi, ki, kd = qi_f.astype(BF16), ki_f.astype(BF16), kd_f.astype(BF16)
            do = do_s[pl.ds(r0, gr), :]
            a = _dot_nt(qi, ki)
            da = _dot_nt(do, v)
            yield
            a = jnp.where(mask, a, 0.0).astype(BF16)
            da = jnp.where(mask, da, 0.0).astype(BF16)
            dv = _dot_tn(a, do)
            dqi = _dot(da, ki)
            dki = _dot_tn(da, qi)
            dv_c, dqi_c, dkd_c, extra_c = [None] * gc, [None] * gc, [None] * gc, [None] * gc
            chunk_rows = [slice(c * CHUNK, (c + 1) * CHUNK) for c in range(gc)]
            qdo = [_dot_tn(do[rows], qi[rows]) for rows in chunk_rows]
            for c in (reversed(range(gc)) if fwd else range(gc)):
                yield
                rows = chunk_rows[c]
                st = st_ref[0, 0, way, c0 + gi * gc + c]
                dsn_b = dst.astype(BF16)
                dec_c = dec[c * CHUNK:c * CHUNK + 1]
                dv_c[c] = dv[rows] + _dot_nt(kd[rows], dsn_b)
                dqi_c[c] = dqi[rows] + _dot(do[rows], st)
                dkd_c[c] = _dot(v[rows], dsn_b)
                ddec = jnp.sum(st.astype(F32) * dst, axis=0, keepdims=True)
                extra = jnp.sum(dkd_c[c] * kd_f[rows], axis=0, keepdims=True) + ddec * dec_c
                extra_c[c] = jnp.broadcast_to(extra, (CHUNK, hk))
                dst = dst * dec_c + qdo[c]
            yield
            dv, dqi = jnp.concatenate(dv_c, axis=0), jnp.concatenate(dqi_c, axis=0)
            dkd, extra = jnp.concatenate(dkd_c, axis=0), jnp.concatenate(extra_c, axis=0)
            dq_s[pl.ds(r0, gr), :] += dqi * eb * scale
            dk_s[pl.ds(r0, gr), :] += dki * enb + dkd * edb
            dv_s[pl.ds(r0, gr), :] += dv
            db = dqi * qi_f - dki * ki_f - dkd * kd_f
            dg = _chunk_cumsum(db, fwd) + extra
            yield
            dz = dg * gs_ref[way, pl.ds(r0, gr), :]
            dz_s[way, pl.ds(r0, gr), :] = dz.astype(BF16)
            return dst, gb + jnp.sum(dz, axis=0, keepdims=True)

        def grad_step(i, carry):
            return tuple(_interleave([grad(ng - 1 - i, carry[0], True), grad(i, carry[1], False)]))

        init = (zero, jnp.zeros((1, hk), F32))
        (_, gb_f), (_, gb_b) = lax.fori_loop(0, ng, grad_step, (init, init))
        used = slice(c0 * CHUNK, lp)
        for way, (w_ref, gw_ref, gb_ref, gb) in enumerate(((wf_ref, gwf_ref, gbf_ref, gb_f), (wb_ref, gwb_ref, gbb_ref, gb_b))):
            dlr_ref[used, :] += _dot_nt(dz_s[way, used, :], w_ref[...])
            gw_ref[0] = _dot_tn(lr_ref[used, :], dz_s[way, used, :])
            gb_ref[0] = jnp.zeros((8, hk), F32)
            gb_ref[0, 0:1, :] = gb

        def combine(i, carry):
            r0 = pl.multiple_of((c0 + i * gc) * CHUNK, CHUNK)
            d_ref[pl.ds(r0, gr), 0:hk] = dq_s[pl.ds(r0, gr), :].astype(BF16)
            d_ref[pl.ds(r0, gr), hk:2 * hk] = dk_s[pl.ds(r0, gr), :].astype(BF16)
            d_ref[pl.ds(r0, gr), 2 * hk:2 * hk + hv] = dv_s[pl.ds(r0, gr), :].astype(BF16)
            return carry

        lax.fori_loop(0, ng, combine, 0)

    head = lambda s, h: (s, h)
    wspec = pl.BlockSpec((LR_LANES, hk), lambda s, h: (0, h))
    gwspec = pl.BlockSpec((1, LR_LANES, hk), lambda s, h: (s, 0, h))
    gbspec = pl.BlockSpec((1, 8, hk), lambda s, h: (s, 0, h))
    gw_shape = jax.ShapeDtypeStruct((dm.Bl, LR_LANES, dm.DK), F32)
    gb_shape = jax.ShapeDtypeStruct((dm.Bl, 8, dm.DK), F32)
    both = pl.BlockSpec((2, lp, hk), lambda s, h: (0, s, h))
    return pl.pallas_call(
        body, name="gla_bwd", grid=(dm.Bl, HEADS),
        in_specs=[pl.BlockSpec((lp, hw), head), pl.BlockSpec((lp, LR_LANES), lambda s, h: (s, 0)),
                  pl.BlockSpec((lp, hv), head), pl.BlockSpec((lp, hv), head),
                  pl.BlockSpec((1, 1, 2, nc, hv, hk), lambda s, h: (s, h, 0, 0, 0, 0)), both, both,
                  wspec, wspec, pl.BlockSpec((1, hv), lambda s, h: (0, 0))],
        out_specs=[pl.BlockSpec((lp, hw), head), pl.BlockSpec((lp, LR_LANES), lambda s, h: (s, 0)),
                   gwspec, gbspec, gwspec, gbspec, pl.BlockSpec((8, hv), lambda s, h: (0, 0))],
        out_shape=[jax.ShapeDtypeStruct((dm.T, HEADS * hw), BF16), jax.ShapeDtypeStruct((dm.T, LR_LANES), F32),
                   gw_shape, gb_shape, gw_shape, gb_shape, jax.ShapeDtypeStruct((8, hv), F32)],
        scratch_shapes=[pltpu.VMEM((lp, hv), BF16), pltpu.VMEM((lp, hk), F32), pltpu.VMEM((lp, hk), F32),
                        pltpu.VMEM((lp, hv), F32), pltpu.VMEM((2, lp, hk), BF16)],
        compiler_params=_cp(2),
    )(proj_b, lr, o_all, dy_gla, states, decays, gate_slopes, wg_f, wg_b, gla_g)


def _stream_tiles(n_tiles, loads, stores, compute):
    for cp in loads(0, 0):
        cp.start()

    def step(t, carry):
        slot = t % 2

        @pl.when(t + 1 < n_tiles)
        def _():
            for cp in loads(t + 1, 1 - slot):
                cp.start()

        for cp in loads(t, slot):
            cp.wait()

        @pl.when(t >= 2)
        def _():
            for cp in stores(t - 2, slot):
                cp.wait()

        compute(t, slot)
        for cp in stores(t, slot):
            cp.start()
        return carry

    lax.fori_loop(0, n_tiles, step, 0)
    for t in range(max(n_tiles - 2, 0), n_tiles):
        for cp in stores(t, t % 2):
            cp.wait()


def _token_tiles(dm, target_rows=512):
    rows = _pick(dm.S, target_rows, 16)
    per_seq = dm.S // rows
    return rows, dm.Bl * per_seq, lambda t: pl.multiple_of((t // per_seq) * dm.LP + dm.TM + (t % per_seq) * rows, 16)


def _head(y_conv, y_gla, proj_c, w_oc, w_og, w_out, x, target, g_post, dm):
    d, tm = dm.D, dm.TM
    rows, n_tiles, first_row = _token_tiles(dm, 512)
    parts = 2 if rows % (2 * BF16_TILE_ROWS) == 0 else 1
    n_out = 8

    def body(*refs):
        yc_hbm, yg_hbm, c_hbm, woc_ref, wog_ref, wo_ref, x_hbm, t_hbm, g_ref = refs[:9]
        outs, st_ref = refs[9:9 + n_out], refs[9 + n_out]
        ycbuf, ygbuf, cbuf, xbuf, tbuf = refs[10 + n_out:15 + n_out]
        obufs = refs[15 + n_out:15 + 2 * n_out]
        zbuf, zbuf2, sem_in, sem_out, sem_zero = refs[15 + 2 * n_out:]

        def loads(t, slot):
            padded = [(yc_hbm, ycbuf), (yg_hbm, ygbuf), (c_hbm, cbuf)]
            own = [(x_hbm, xbuf), (t_hbm, tbuf)]
            return ([pltpu.make_async_copy(h.at[pl.ds(first_row(t), rows), :], b.at[slot], sem_in.at[i, slot])
                     for i, (h, b) in enumerate(padded)] +
                    [pltpu.make_async_copy(h.at[pl.ds(t * rows, rows), :], b.at[slot], sem_in.at[3 + i, slot])
                     for i, (h, b) in enumerate(own)])

        def stores(t, slot):
            return [pltpu.make_async_copy(b.at[slot], h.at[pl.ds(first_row(t), rows), :], sem_out.at[i, slot])
                    for i, (h, b) in enumerate(zip(outs, obufs))]

        def chain(slot, part):
            mg_o, do_o, dy_o, dpc_o, dpg_o, dc_o, dyc_o, dyg_o = obufs
            pc = _dot(ycbuf[slot, part], woc_ref[...])
            pg = _dot(ygbuf[slot, part], wog_ref[...])
            yield
            sa = _sigmoid(cbuf[slot, part, :d].astype(F32))
            sb = _sigmoid(cbuf[slot, part, d:].astype(F32))
            merged = (sa * pc + sb * pg).astype(BF16)
            mg_o[slot, part] = merged
            out = _dot(merged, wo_ref[...])
            yield
            rstd = lax.rsqrt(jnp.mean(out * out, axis=-1, keepdims=True) + EPS)
            ohat = out * rstd
            err = xbuf[slot, part] + ohat * g_ref[...] - tbuf[slot, part]
            dy = err * (1.0 / d)
            d_oh = dy * g_ref[...]
            d_out = (rstd * (d_oh - ohat * jnp.mean(d_oh * ohat, axis=-1, keepdims=True))).astype(BF16)
            do_o[slot, part] = d_out
            dy_o[slot, part] = dy.astype(BF16)
            st_ref[0:1, :] += jnp.sum(dy * ohat, axis=0, keepdims=True)
            st_ref[1:2, :] += jnp.sum(err * err, axis=0, keepdims=True)
            dmg = _dot_nt(d_out, wo_ref[...])
            yield
            dpc = (dmg * sa).astype(BF16)
            dpg = (dmg * sb).astype(BF16)
            dpc_o[slot, part] = dpc
            dpg_o[slot, part] = dpg
            dc_o[slot, part, :d] = (dmg * pc * sa * (1.0 - sa)).astype(BF16)
            dc_o[slot, part, d:] = (dmg * pg * sb * (1.0 - sb)).astype(BF16)
            dyc_o[slot, part] = _dot_nt(dpc, woc_ref[...]).astype(BF16)
            dyg_o[slot, part] = _dot_nt(dpg, wog_ref[...]).astype(BF16)

        def compute(t, slot):
            _interleave([chain(slot, pl.ds(i * (rows // parts), rows // parts)) for i in range(parts)])

        st_ref[...] = jnp.zeros_like(st_ref)
        zbuf[...] = jnp.zeros_like(zbuf)
        zbuf2[...] = jnp.zeros_like(zbuf2)
        zeros = [pltpu.make_async_copy(zbuf2 if out.shape[1] == 2 * d else zbuf, out.at[pl.ds(b * dm.LP, tm), :], sem_zero.at[i, b])
                 for i, out in enumerate(outs) for b in range(dm.Bl)]
        for cp in zeros:
            cp.start()
        _stream_tiles(n_tiles, loads, stores, compute)
        for cp in zeros:
            cp.wait()

    any_spec, vmem = pl.BlockSpec(memory_space=pl.ANY), pl.BlockSpec(memory_space=pltpu.VMEM)
    widths = [d, d, d, d, d, 2 * d, d, d]
    tile = lambda w, dt: pltpu.VMEM((2, rows, w), dt)
    return pl.pallas_call(
        body, name="head", in_specs=[any_spec] * 3 + [vmem] * 3 + [any_spec] * 2 + [vmem],
        out_specs=[any_spec] * n_out + [vmem],
        out_shape=[jax.ShapeDtypeStruct((dm.T, w), BF16) for w in widths] + [jax.ShapeDtypeStruct((8, d), F32)],
        scratch_shapes=[tile(d, BF16), tile(d, BF16), tile(2 * d, BF16), tile(d, F32), tile(d, F32)]
        + [tile(w, BF16) for w in widths]
        + [pltpu.VMEM((tm, d), BF16), pltpu.VMEM((tm, 2 * d), BF16), pltpu.SemaphoreType.DMA((5, 2)),
           pltpu.SemaphoreType.DMA((n_out, 2)), pltpu.SemaphoreType.DMA((n_out, dm.Bl))],
        compiler_params=pltpu.CompilerParams(vmem_limit_bytes=VMEM_LIMIT_BYTES),
    )(y_conv, y_gla, proj_c, w_oc, w_og, w_out, x.reshape(dm.Bl * dm.S, d), target.reshape(dm.Bl * dm.S, d), g_post)


def _grad_h(d_parts, gathered, dy, x, metapad, g_pre, dm):
    d, tm = dm.D, dm.TM
    rows, n_tiles, first_row = _token_tiles(dm, 256)
    widths = [a.shape[1] for a in d_parts]
    np_ = len(d_parts)

    def body(*refs):
        d_hbm, g_hbm, dy_hbm, x_hbm, mp_ref, g_ref = refs[:np_], refs[np_], refs[np_ + 1], refs[np_ + 2], refs[np_ + 3], refs[np_ + 4]
        gx_hbm, dmeta_ref, gg_ref = refs[np_ + 5:np_ + 8]
        parts, edges, sems = refs[np_ + 8:np_ + 12], refs[np_ + 12], refs[np_ + 13]
        dbufs = refs[np_ + 14:2 * np_ + 14]
        dybuf, xbuf, gbuf = refs[2 * np_ + 14:2 * np_ + 17]
        mbufs = refs[2 * np_ + 17:3 * np_ + 17]
        sem_in, sem_out, sem_meta = refs[3 * np_ + 17:]

        def grad_u(tiles):
            du = _dot(tiles[0].astype(BF16), parts[0][...])
            for a, w in zip(tiles[1:], parts[1:]):
                du = du + _dot(a.astype(BF16), w[...])
            return du

        def norm_bwd(h, du, dy):
            rstd = lax.rsqrt(jnp.mean(h * h, axis=-1, keepdims=True) + EPS)
            hhat = h * rstd
            dug = du * g_ref[...]
            gg_ref[0:1, :] += jnp.sum(du * hhat, axis=0, keepdims=True)
            return dy + rstd * (dug - hhat * jnp.mean(dug * hhat, axis=-1, keepdims=True))

        def loads(t, slot):
            padded = list(zip(d_hbm, dbufs)) + [(dy_hbm, dybuf)]
            return ([pltpu.make_async_copy(h.at[pl.ds(first_row(t), rows), :], b.at[slot], sem_in.at[i, slot])
                     for i, (h, b) in enumerate(padded)] +
                    [pltpu.make_async_copy(x_hbm.at[pl.ds(t * rows, rows), :], xbuf.at[slot], sem_in.at[np_ + 1, slot])])

        def stores(t, slot):
            return [pltpu.make_async_copy(gbuf.at[slot], gx_hbm.at[pl.ds(t * rows, rows), :], sem_out.at[slot])]

        def compute(t, slot):
            gbuf[slot] = norm_bwd(xbuf[slot], grad_u([b[slot] for b in dbufs]), dybuf[slot].astype(F32))

        gg_ref[...] = jnp.zeros_like(gg_ref)
        meta = [pltpu.make_async_copy(h.at[pl.ds(b * dm.LP, tm), :], buf.at[pl.ds(b * tm, tm), :], sem_meta.at[i, b])
                for i, (h, buf) in enumerate(zip(d_hbm, mbufs)) for b in range(dm.Bl)]
        for cp in meta:
            cp.start()
        _load_packed(g_hbm, parts, edges, sems, dm)
        _stream_tiles(n_tiles, loads, stores, compute)
        for cp in meta:
            cp.wait()
        dmeta_ref[...] = norm_bwd(jnp.concatenate([mp_ref[...]] * dm.Bl, axis=0), grad_u([buf[...] for buf in mbufs]), 0.0)

    any_spec, vmem = pl.BlockSpec(memory_space=pl.ANY), pl.BlockSpec(memory_space=pltpu.VMEM)
    grad_x, d_meta, gg = pl.pallas_call(
        body, name="grad_h", in_specs=[any_spec] * (np_ + 3) + [vmem, vmem], out_specs=[any_spec, vmem, vmem],
        out_shape=[jax.ShapeDtypeStruct((dm.Bl * dm.S, d), F32), jax.ShapeDtypeStruct((dm.Bl * tm, d), F32),
                   jax.ShapeDtypeStruct((8, d), F32)],
        scratch_shapes=_packed_scratch(dm)
        + [pltpu.VMEM((2, rows, w), a.dtype) for w, a in zip(widths, d_parts)]
        + [pltpu.VMEM((2, rows, d), BF16), pltpu.VMEM((2, rows, d), F32), pltpu.VMEM((2, rows, d), F32)]
        + [pltpu.VMEM((dm.Bl * tm, w), a.dtype) for w, a in zip(widths, d_parts)]
        + [pltpu.SemaphoreType.DMA((np_ + 2, 2)), pltpu.SemaphoreType.DMA((2,)), pltpu.SemaphoreType.DMA((np_, dm.Bl))],
        compiler_params=pltpu.CompilerParams(vmem_limit_bytes=VMEM_LIMIT_BYTES),
    )(*d_parts, gathered, dy, x.reshape(dm.Bl * dm.S, d), metapad, g_pre)
    return grad_x.reshape(dm.Bl, dm.S, d), d_meta.reshape(dm.Bl, tm, d), gg


def _adamw(partials, w, m, v, name, by_columns=False):
    r, c = w.shape
    n_parts, pr = partials.shape[:2]
    assert pr == r or (by_columns and pr == _padded_shard_rows(r))
    tr, tc = (r, _pick(c, 128, 128)) if by_columns else (_pick(r, 256, 16), c)

    def body(p_ref, w_ref, m_ref, v_ref, g_ref, d_ref, nm_ref, nv_ref):
        g = p_ref[0].astype(F32)
        for j in range(1, n_parts):
            g = g + p_ref[j].astype(F32)

        def step(g):
            g_ref[...] = g
            d_ref[...], nm_ref[...], nv_ref[...] = _adam_step(g, w_ref[...], m_ref[...], v_ref[...])

        if pr == r:
            step(g)
        else:
            me = 4 * lax.axis_index("x") + 2 * lax.axis_index("y") + lax.axis_index("c")
            for offset in sorted({_shard_offset(j, r) for j in range(N_DEV)}):
                @pl.when(_shard_offset(me, r) == offset)
                def _(offset=offset):
                    step(g[offset:offset + r])

    at = (lambda i: (0, i)) if by_columns else (lambda i: (i, 0))
    tile = pl.BlockSpec((tr, tc), at)
    out = jax.ShapeDtypeStruct((r, c), F32)
    return pl.pallas_call(
        body, name=name, grid=(c // tc if by_columns else r // tr,),
        in_specs=[pl.BlockSpec((n_parts, pr if by_columns else tr, tc), lambda i: (0,) + at(i)), tile, tile, tile],
        out_specs=[tile, tile, tile, tile], out_shape=[out, out, out, out], compiler_params=_cp(1),
    )(partials, w, m, v)


def _adam_step(g, w, m, v):
    m2 = ADAM_B1 * m + (1.0 - ADAM_B1) * g
    v2 = ADAM_B2 * v + (1.0 - ADAM_B2) * (g * g)
    m_hat = m2 / (1.0 - ADAM_B1 ** ADAM_STEP)
    v_hat = v2 / (1.0 - ADAM_B2 ** ADAM_STEP)
    return -ADAM_LR * (m_hat / (jnp.sqrt(v_hat) + ADAM_EPS) + ADAM_WD * w), m2, v2


def _adamw_small(items, name):
    n = len(items)

    def body(*refs):
        ins, outs = refs[:4 * n], refs[4 * n:]
        for i in range(n):
            p_ref, w_ref, m_ref, v_ref = ins[4 * i:4 * i + 4]
            g = p_ref[0]
            for j in range(1, p_ref.shape[0]):
                g = g + p_ref[j]
            delta, m2, v2 = _adam_step(g, w_ref[...], m_ref[...], v_ref[...])
            for o_ref, val in zip(outs[4 * i:4 * i + 4], (g, delta, m2, v2)):
                o_ref[...] = val

    vmem = pl.BlockSpec(memory_space=pltpu.VMEM)
    res = pl.pallas_call(
        body, name=name, in_specs=[vmem] * (4 * n), out_specs=[vmem] * (4 * n),
        out_shape=[jax.ShapeDtypeStruct(w.shape, F32) for _, w, _, _ in items for _ in range(4)],
    )(*[a for item in items for a in item])
    return [res[4 * i:4 * i + 4] for i in range(n)]


def _unpack_moves(dm):
    d, hk, hv, cw, nj, hw = dm.D, dm.HK, dm.HV, dm.CW, dm.NJ, dm.HW
    moves = [((4 * j + part) * cw, cw, part * d + j * cw) for j in range(nj) for part in range(4)]
    q0 = 4 * d
    k0, v0 = q0 + HEADS * hk, q0 + 2 * HEADS * hk
    r0 = v0 + HEADS * hv
    lr0 = r0 + HEADS * hv
    for h in range(HEADS):
        b0 = 4 * d + h * hw
        moves += [(b0, hk, q0 + h * hk), (b0 + hk, hk, k0 + h * hk), (b0 + 2 * hk, hv, v0 + h * hv),
                  (b0 + 2 * hk + hv, hv, r0 + h * hv)]
    moves.append((4 * d + HEADS * hw, 2 * d, lr0 + 2 * RANK))
    return moves, lr0


def _column_shards(g, shard_shape):
    r, c = g.shape
    return g.reshape(r, N_DEV, c // N_DEV).transpose(1, 0, 2).reshape((N_DEV,) + tuple(shard_shape))


def _join_column_shards(parts):
    r, c = parts.shape[-2:]
    return parts.reshape(N_DEV, r, c).transpose(1, 0, 2).reshape(r, N_DEV * c)


def _local_step(x, target, meta, g_pre, u, wt_shards, conv_w, wg_f, bg_f, wg_b, bg_b, gla_g, out_weights, g_post,
                on_matrix_grads=None):
    bl, s, d = x.shape
    dm = _Dims(bl, s, d)
    metapad = jnp.concatenate([jnp.zeros((dm.TM - N_META, d), F32), meta], axis=0)
    wgp_f = jnp.pad(wg_f, ((0, LR_LANES - RANK), (0, 0))).astype(BF16)
    wgp_b = jnp.pad(wg_b, ((RANK, LR_LANES - 2 * RANK), (0, 0))).astype(BF16)

    u = _prenorm_meta(u, metapad, g_pre, dm)
    proj_a, proj_b, proj_c, lr = _inproj(u, wt_shards, dm)
    y_conv = _conv_fwd(proj_a, conv_w, dm)
    o_all, y_gla, states, decays, gate_slopes = _gla_fwd(proj_b, lr, wgp_f, bg_f, wgp_b, bg_b, gla_g, dm)
    w_oc, w_og, w_out = out_weights(y_conv) if callable(out_weights) else out_weights
    merged, d_out, dy, d_pc, d_pg, d_c, dy_conv, dy_gla, stats = _head(y_conv, y_gla, proj_c, w_oc, w_og, w_out, x, target,
                                                                        g_post, dm)

    g_out = _matmul_tn(merged, d_out, BF16, "grad_w_out")
    g_oc = _matmul_tn(y_conv, d_pc, BF16, "grad_w_out_conv")
    g_og = _matmul_tn(y_gla, d_pg, BF16, "grad_w_out_gla")
    if on_matrix_grads is not None:
        conv_w = conv_w + on_matrix_grads(dict(w_out_conv=g_oc, w_out_gla=g_og, w_merge_out=g_out))
    d_a, g_conv = _conv_bwd(proj_a, dy_conv, conv_w, dm)
    d_b, d_lr, gwp_f, gbp_f, gwp_b, gbp_b, g_gla = _gla_bwd(proj_b, lr, o_all, dy_gla, states, decays, gate_slopes, wgp_f, wgp_b, gla_g, dm)
    moves, lr_at = _unpack_moves(dm)
    g_lr = _matmul_tn(d_lr, u, BF16, "grad_w_in_gate")[:2 * RANK]
    g_in = _matmul_tn_group([d_a, d_b, d_c], u, moves, 9 * d + 2 * RANK, (g_lr, lr_at), "grad_w_in", tile=d)
    if on_matrix_grads is not None:
        d_lr = d_lr + on_matrix_grads(dict(w_in=g_in))
    grad_x, d_meta, g_pre_rows = _grad_h([d_a, d_b, d_c, d_lr], wt_shards, dy, x, metapad, g_pre, dm)

    grads = dict(
        meta_tokens=jnp.sum(d_meta[:, dm.TM - N_META:, :], axis=0), norm_pre=g_pre_rows[0:1], w_in=g_in,
        conv_w=g_conv[0:3], w_gate_fwd=jnp.sum(gwp_f, axis=0)[:RANK], b_gate_fwd=jnp.sum(gbp_f, axis=0)[0:1],
        w_gate_bwd=jnp.sum(gwp_b, axis=0)[RANK:2 * RANK], b_gate_bwd=jnp.sum(gbp_b, axis=0)[0:1],
        gla_norm=g_gla[0:1], w_out_conv=g_oc, w_out_gla=g_og, w_merge_out=g_out, norm_post=stats[0:1])
    return stats[1:2], grad_x, grads


MATRICES = ("w_out_conv", "w_out_gla", "w_merge_out")
SMALL_SHARDED = ("meta_tokens", "conv_w", "w_gate_fwd", "w_gate_bwd")
REPLICATED = ("norm_pre", "b_gate_fwd", "b_gate_bwd", "gla_norm", "norm_post")
NAMES = ("meta_tokens", "norm_pre", "w_in", "conv_w", "w_gate_fwd", "b_gate_fwd", "w_gate_bwd", "b_gate_bwd", "gla_norm",
         "w_out_conv", "w_out_gla", "w_merge_out", "norm_post")


def kernel(x, meta_tokens, norm_pre, w_in, conv_w, w_gate_fwd, b_gate_fwd, w_gate_bwd, b_gate_bwd, gla_norm, w_out_conv, w_out_gla, w_merge_out, norm_post, loss_target, m_meta_tokens, m_norm_pre, m_w_in, m_conv_w, m_w_gate_fwd, m_b_gate_fwd, m_w_gate_bwd, m_b_gate_bwd, m_gla_norm, m_w_out_conv, m_w_out_gla, m_w_merge_out, m_norm_post, v_meta_tokens, v_norm_pre, v_w_in, v_conv_w, v_w_gate_fwd, v_b_gate_fwd, v_w_gate_bwd, v_b_gate_bwd, v_gla_norm, v_w_out_conv, v_w_out_gla, v_w_merge_out, v_norm_post):
    w = dict(meta_tokens=meta_tokens, norm_pre=norm_pre, w_in=w_in[0], conv_w=conv_w, w_gate_fwd=w_gate_fwd,
             b_gate_fwd=b_gate_fwd, w_gate_bwd=w_gate_bwd, b_gate_bwd=b_gate_bwd, gla_norm=gla_norm,
             w_out_conv=w_out_conv[0], w_out_gla=w_out_gla[0], w_merge_out=w_merge_out[0], norm_post=norm_post)
    m = dict(meta_tokens=m_meta_tokens, norm_pre=m_norm_pre, w_in=m_w_in[0], conv_w=m_conv_w, w_gate_fwd=m_w_gate_fwd,
             b_gate_fwd=m_b_gate_fwd, w_gate_bwd=m_w_gate_bwd, b_gate_bwd=m_b_gate_bwd, gla_norm=m_gla_norm,
             w_out_conv=m_w_out_conv[0], w_out_gla=m_w_out_gla[0], w_merge_out=m_w_merge_out[0], norm_post=m_norm_post)
    v = dict(meta_tokens=v_meta_tokens, norm_pre=v_norm_pre, w_in=v_w_in[0], conv_w=v_conv_w, w_gate_fwd=v_w_gate_fwd,
             b_gate_fwd=v_b_gate_fwd, w_gate_bwd=v_w_gate_bwd, b_gate_bwd=v_b_gate_bwd, gla_norm=v_gla_norm,
             w_out_conv=v_w_out_conv[0], w_out_gla=v_w_out_gla[0], w_merge_out=v_w_merge_out[0], norm_post=v_norm_post)
    d = x.shape[-1]

    dm = _Dims(*x.shape)
    me = 4 * lax.axis_index("x") + 2 * lax.axis_index("y") + lax.axis_index("c")
    wt_shards, *small_all, u = _gather_two_level(
        [_pad_shard(w["w_in"].T.astype(BF16), me)] + [w[n] for n in SMALL_SHARDED], "gather_weights",
        _prenorm_tokens_side(x, norm_pre, dm))
    started, late_weights = _exchange_start([w[n].astype(BF16) for n in MATRICES], [], small_all[0], "gather_out_weights_start")
    small = {n: _join_column_shards(p) for n, p in zip(SMALL_SHARDED, small_all)}
    small["meta_tokens"] = small["meta_tokens"] + started

    def out_weights(after):
        return tuple(a.reshape(-1, d) for a in _exchange_wait(late_weights, after, "gather_out_weights_wait"))

    pending = []

    def on_matrix_grads(g):
        blocks = [t.reshape(N_DEV, -1, d) if t.shape[0] % (N_DEV * BF16_TILE_ROWS) == 0 else (t, t.shape[0] // N_DEV)
                  for t in g.values()]
        token, state = _exchange_start([], blocks, None, "exchange_grads_start_" + "_".join(g))
        pending.append((tuple(g), state))
        return token

    sq_err_cols, grad_x, grads = _local_step(
        x, loss_target, small["meta_tokens"], norm_pre, u, wt_shards, small["conv_w"], small["w_gate_fwd"], b_gate_fwd,
        small["w_gate_bwd"], b_gate_bwd, gla_norm, out_weights, norm_post, on_matrix_grads)
    received = {}
    for names, state in pending:
        received.update(zip(names, _exchange_wait(state, grad_x, "exchange_grads_wait_" + "_".join(names))))

    exchanged = _exchange([grads[n] for n in REPLICATED] + [sq_err_cols],
                          [_column_shards(grads[n], w[n].shape) for n in SMALL_SHARDED], "exchange_small_grads")
    small_recv = exchanged[:len(REPLICATED)] + exchanged[len(REPLICATED) + 1:]
    loss = 0.5 / d * jnp.sum(exchanged[len(REPLICATED)])

    results = {"w_in": [r.T[None] for r in _adamw(received["w_in"], w["w_in"].T, m["w_in"].T, v["w_in"].T, "adamw_w_in", by_columns=True)]}
    for n in MATRICES:
        results[n] = [r[None] for r in _adamw(received[n], w[n], m[n], v[n], "adamw_" + n)]
    small_names = REPLICATED + SMALL_SHARDED
    results.update(zip(small_names, _adamw_small([(p, w[n], m[n], v[n]) for n, p in zip(small_names, small_recv)], "adamw_small")))
    return (loss, grad_x, *[results[n][i] for i in range(4) for n in NAMES])
```

```python
import jax
import jax.numpy as jnp
from jax import lax
from jax.experimental import pallas as pl
from jax.experimental.pallas import tpu as pltpu

F32 = jnp.float32
BF16 = jnp.bfloat16
MESH = pl.DeviceIdType.MESH

N_META = 16
CHUNK = 64
CHUNK_SHIFT = 6
HEADS = 4
RANK = 16
LR_LANES = 128
PAD_ROWS = CHUNK - N_META
EPS = 1e-6
GATE_NORMALIZER = 16.0
N_DEV = 8
ADAM_LR, ADAM_B1, ADAM_B2, ADAM_EPS, ADAM_WD, ADAM_STEP = 0.001, 0.9, 0.999, 1e-08, 0.01, 10
VMEM_LIMIT_BYTES = 56 * 1024 * 1024


class _Dims:
    def __init__(self, bl, s, d):
        self.Bl, self.S, self.D = bl, s, d
        self.TM = CHUNK
        self.LP = self.TM + s
        self.T = bl * self.LP
        self.TPS = self.LP // self.TM
        self.NC = self.LP // CHUNK
        self.C0 = (self.TM - CHUNK) // CHUNK
        self.DK, self.DV = d // 2, d
        self.HK, self.HV = self.DK // HEADS, self.DV // HEADS
        self.HW = 2 * self.HK + 2 * self.HV
        self.CW = 256 if d % 256 == 0 and d > 256 else d // 4
        self.NJ = d // self.CW


def _pick(n, target, mult):
    t = min(n, target)
    while t >= mult:
        if n % t == 0 and t % mult == 0:
            return t
        t -= mult
    return n


def _cp(n_axes):
    return pltpu.CompilerParams(dimension_semantics=("arbitrary",) * n_axes, vmem_limit_bytes=VMEM_LIMIT_BYTES)


def _sigmoid(x):
    return 1.0 / (1.0 + jnp.exp(-x))


def _dot(a, b):
    return jnp.dot(a, b, preferred_element_type=F32)


def _dot_nt(a, b):
    return lax.dot_general(a, b, (((1,), (1,)), ((), ())), preferred_element_type=F32)


def _dot_tn(a, b):
    return lax.dot_general(a, b, (((0,), (0,)), ((), ())), preferred_element_type=F32)


def _chunk_cumsum(x, reverse):
    rows = x.shape[0]
    r = lax.broadcasted_iota(jnp.int32, x.shape, 0) & (CHUNK - 1)
    step = 1
    while step < CHUNK:
        if reverse:
            x = x + jnp.where(r < CHUNK - step, pltpu.roll(x, rows - step, 0), 0.0)
        else:
            x = x + jnp.where(r >= step, pltpu.roll(x, step, 0), 0.0)
        step *= 2
    return x


def _exchange(gathers, scatters, name):
    arrays = list(gathers) + list(scatters)
    n, ng = len(arrays), len(gathers)

    def body(*refs):
        ins, outs = refs[:n], refs[n:2 * n]
        send_sems, recv_sems, local_sems = refs[2 * n:]
        x, y, c = lax.axis_index("x"), lax.axis_index("y"), lax.axis_index("c")
        me = 4 * x + 2 * y + c
        started = []
        for t in range(n):
            src, dst = ins[t], outs[t]
            own = pltpu.make_async_copy(src if t < ng else src.at[me], dst.at[me], local_sems.at[t])
            own.start()
            started.append(own)
            for k, pos, peer in _peers(x, y, c):
                cp = pltpu.make_async_remote_copy(
                    src_ref=src if t < ng else src.at[peer], dst_ref=dst.at[me],
                    send_sem=send_sems.at[t * (N_DEV - 1) + k - 1], recv_sem=recv_sems.at[t * (N_DEV - 1) + k - 1],
                    device_id=pos, device_id_type=MESH)
                cp.start()
                started.append(cp)
        for cp in started:
            cp.wait()

    out_shape = [jax.ShapeDtypeStruct((N_DEV,) + a.shape if t < ng else a.shape, a.dtype) for t, a in enumerate(arrays)]
    any_spec = pl.BlockSpec(memory_space=pl.ANY)
    return pl.pallas_call(
        body, name=name, out_shape=out_shape, in_specs=[any_spec] * n, out_specs=[any_spec] * n,
        scratch_shapes=[pltpu.SemaphoreType.DMA((n * (N_DEV - 1),)), pltpu.SemaphoreType.DMA((n * (N_DEV - 1),)),
                        pltpu.SemaphoreType.DMA((n,))],
        compiler_params=pltpu.CompilerParams(has_side_effects=True),
    )(*arrays)


def _gather_two_level(arrays, name, side=None):
    n = len(arrays)
    per = N_DEV - 1
    work, side_in, side_in_specs, side_out, side_out_specs, side_scratch = side or (None, [], [], [], [], [])
    n_in, n_out = len(side_in), len(side_out)

    def body(*refs):
        ins, outs = refs[:n], refs[n + n_in:2 * n + n_in]
        send_sems, recv_sems, local_sems = refs[2 * n + n_in + n_out:2 * n + n_in + n_out + 3]
        x, y, c = lax.axis_index("x"), lax.axis_index("y"), lax.axis_index("c")
        sibling = (x, y, 1 - c)
        chips = [(1 - x, y), (x, 1 - y), (1 - x, 1 - y)]
        index = lambda px, py, pc: 4 * px + 2 * py + pc

        def copy(t, k, block, to, from_input=False):
            slab = outs[t].at[index(*block)]
            return pltpu.make_async_remote_copy(
                src_ref=ins[t] if from_input else slab, dst_ref=slab, send_sem=send_sems.at[t * per + k],
                recv_sem=recv_sems.at[t * per + k], device_id=to, device_id_type=MESH)

        own, sent = [], []
        for t in range(n):
            own.append(pltpu.make_async_copy(ins[t], outs[t].at[index(x, y, c)], local_sems.at[t]))
            own[-1].start()
            first = [copy(t, 0, (x, y, c), sibling, True)]
            first += [copy(t, 1 + j, (x, y, c), (*chip, c), True) for j, chip in enumerate(chips)]
            for cp in first:
                cp.start()
            sent += first
        if work is not None:
            work(refs[n:n + n_in], refs[2 * n + n_in:2 * n + n_in + n_out], refs[2 * n + n_in + n_out + 3:])
        for t in range(n):
            for j, chip in enumerate(chips):
                copy(t, 1 + j, (*chip, c), (x, y, c)).wait_recv()
                sent.append(copy(t, 4 + j, (*chip, c), sibling))
                sent[-1].start()
        for t in range(n):
            copy(t, 0, sibling, (x, y, c)).wait_recv()
            for j, chip in enumerate(chips):
                copy(t, 4 + j, (*chip, 1 - c), (x, y, c)).wait_recv()
        for cp in sent:
            cp.wait_send()
        for cp in own:
            cp.wait()

    out_shape = [jax.ShapeDtypeStruct((N_DEV,) + a.shape, a.dtype) for a in arrays]
    any_spec = pl.BlockSpec(memory_space=pl.ANY)
    return pl.pallas_call(
        body, name=name, out_shape=out_shape + list(side_out), in_specs=[any_spec] * n + list(side_in_specs),
        out_specs=[any_spec] * n + list(side_out_specs),
        scratch_shapes=[pltpu.SemaphoreType.DMA((n * per,)), pltpu.SemaphoreType.DMA((n * per,)),
                        pltpu.SemaphoreType.DMA((n,))] + list(side_scratch),
        compiler_params=pltpu.CompilerParams(has_side_effects=True, vmem_limit_bytes=VMEM_LIMIT_BYTES),
    )(*arrays, *side_in)


def _peers(x, y, c):
    out = []
    for k in range(1, N_DEV):
        px = 1 - x if (k >> 2) & 1 else x
        py = 1 - y if (k >> 1) & 1 else y
        pc = 1 - c if k & 1 else c
        out.append((k, (px, py, pc), 4 * px + 2 * py + pc))
    return out


def _exchange_start(gathers, scatters, after, name):
    shard_rows = [None] * len(gathers) + [s[1] if isinstance(s, tuple) else None for s in scatters]
    arrays = list(gathers) + [s[0] if isinstance(s, tuple) else s for s in scatters]
    n, ng = len(arrays), len(gathers)
    hbm = pl.BlockSpec(memory_space=pltpu.HBM)
    sem = pl.BlockSpec(memory_space=pltpu.SEMAPHORE)

    extra = [] if after is None else [after]
    ne = len(extra)

    def body(*refs):
        ins, lands = refs[:n], refs[n:2 * n]
        send_sems, recv_sems = refs[2 * n + ne], refs[2 * n + ne + 1]
        token = refs[4 * n + ne + 2]
        x, y, c = lax.axis_index("x"), lax.axis_index("y"), lax.axis_index("c")
        me = 4 * x + 2 * y + c
        for t in range(n):
            for k, pos, peer in _peers(x, y, c):
                pltpu.make_async_remote_copy(
                    src_ref=_block_for(ins[t], peer, t < ng, shard_rows[t]), dst_ref=lands[t].at[me],
                    send_sem=send_sems.at[t * (N_DEV - 1) + k - 1], recv_sem=recv_sems.at[t * (N_DEV - 1) + k - 1],
                    device_id=pos, device_id_type=MESH).start()
        token[...] = jnp.zeros_like(token)

    me = 4 * lax.axis_index("x") + 2 * lax.axis_index("y") + lax.axis_index("c")

    def own_block(t, a):
        if t < ng:
            return a
        if shard_rows[t] is None:
            return lax.dynamic_index_in_dim(a, me, 0, keepdims=False)
        assert all(_shard_window(j, shard_rows[t]) + _padded_shard_rows(shard_rows[t]) <= a.shape[0] for j in range(N_DEV))
        return lax.dynamic_slice_in_dim(a, _shard_window(me, shard_rows[t]), _padded_shard_rows(shard_rows[t]), 0)

    blocks = [own_block(t, a) for t, a in enumerate(arrays)]
    lands = [lax.dynamic_update_index_in_dim(lax.empty((N_DEV,) + b.shape if t < ng or shard_rows[t] else a.shape, a.dtype), b, me, 0)
             for t, (a, b) in enumerate(zip(arrays, blocks))]
    operands = [pltpu.with_memory_space_constraint(a, pltpu.HBM) for a in arrays + lands]
    sems = pltpu.SemaphoreType.DMA((n * (N_DEV - 1),))
    res = pl.pallas_call(
        body, name=name,
        out_shape=(sems, sems, *[pltpu.HBM(a.shape, a.dtype) for a in arrays + lands], jax.ShapeDtypeStruct((8, 128), F32)),
        in_specs=[hbm] * (2 * n) + [pl.BlockSpec(memory_space=pl.ANY)] * ne,
        out_specs=(sem, sem, *[hbm] * (2 * n), pl.BlockSpec(memory_space=pltpu.VMEM)),
        input_output_aliases={i: 2 + i for i in range(2 * n)},
        compiler_params=pltpu.CompilerParams(has_side_effects=pltpu.SideEffectType.DATAFLOW_SIDE_EFFECTING),
    )(*operands, *extra)
    return res[-1][0, 0], (ng, shard_rows, res[0], res[1], list(res[2:2 + n]), list(res[2 + n:2 + 2 * n]))


def _block_for(ref, peer, whole, shard_rows):
    if whole:
        return ref
    if shard_rows is None:
        return ref.at[peer]
    return ref.at[pl.ds(pl.multiple_of(_shard_window(peer, shard_rows), BF16_TILE_ROWS), _padded_shard_rows(shard_rows))]


def _exchange_wait(state, after, name):
    ng, shard_rows, send_sems, recv_sems, sent, lands = state
    n = len(sent)
    hbm = pl.BlockSpec(memory_space=pltpu.HBM)
    sem = pl.BlockSpec(memory_space=pltpu.SEMAPHORE)

    def body(*refs):
        ins, land_refs = refs[:n], refs[n:2 * n]
        send_ref, recv_ref = refs[2 * n], refs[2 * n + 1]
        x, y, c = lax.axis_index("x"), lax.axis_index("y"), lax.axis_index("c")
        me = 4 * x + 2 * y + c
        for t in range(n):
            for k, pos, peer in _peers(x, y, c):
                cp = pltpu.make_async_remote_copy(
                    src_ref=_block_for(ins[t], peer, t < ng, shard_rows[t]), dst_ref=land_refs[t].at[me],
                    send_sem=send_ref.at[t * (N_DEV - 1) + k - 1], recv_sem=recv_ref.at[t * (N_DEV - 1) + k - 1],
                    device_id=pos, device_id_type=MESH)
                cp.wait_send()
                cp.wait_recv()

    res = pl.pallas_call(
        body, name=name, out_shape=tuple(pltpu.HBM(a.shape, a.dtype) for a in sent + lands),
        in_specs=[hbm] * (2 * n) + [sem, sem, pl.BlockSpec(memory_space=pl.ANY)], out_specs=tuple([hbm] * (2 * n)),
        input_output_aliases={i: i for i in range(2 * n)},
        compiler_params=pltpu.CompilerParams(has_side_effects=pltpu.SideEffectType.DATAFLOW_SIDE_EFFECTING),
    )(*sent, *lands, send_sems, recv_sems, after)
    return list(res[n:])


def _rms_scaled(h, g):
    return (h * lax.rsqrt(jnp.mean(h * h, axis=-1, keepdims=True) + EPS) * g).astype(BF16)


def _prenorm_tokens_side(x, g_pre, dm):
    bl, s, d = x.shape
    rows = _pick(s, 512, 16)
    tiles = [(b, j) for b in range(bl) for j in range(s // rows)]

    def work(ins, outs, scratch):
        (x_ref, g_ref), (u_ref,), (xbuf, ubuf, sem_in, sem_out) = ins, outs, scratch

        def load(t, slot):
            b, j = tiles[t]
            return pltpu.make_async_copy(x_ref.at[b, pl.ds(j * rows, rows), :], xbuf.at[slot], sem_in.at[slot])

        def store(t, slot):
            b, j = tiles[t]
            return pltpu.make_async_copy(ubuf.at[slot], u_ref.at[pl.ds(b * dm.LP + dm.TM + j * rows, rows), :], sem_out.at[slot])

        load(0, 0).start()
        for t in range(len(tiles)):
            slot = t % 2
            if t + 1 < len(tiles):
                load(t + 1, 1 - slot).start()
            load(t, slot).wait()
            if t >= 2:
                store(t - 2, slot).wait()
            ubuf[slot] = _rms_scaled(xbuf[slot], g_ref[...])
            store(t, slot).start()
        for t in range(max(len(tiles) - 2, 0), len(tiles)):
            store(t, t % 2).wait()

    any_spec = pl.BlockSpec(memory_space=pl.ANY)
    return (work, [x, g_pre], [any_spec, pl.BlockSpec(memory_space=pltpu.VMEM)],
            [jax.ShapeDtypeStruct((dm.T, d), BF16)], [any_spec],
            [pltpu.VMEM((2, rows, d), F32), pltpu.VMEM((2, rows, d), BF16), pltpu.SemaphoreType.DMA((2,)),
             pltpu.SemaphoreType.DMA((2,))])


def _prenorm_meta(u, metapad, g_pre, dm):
    tm, tps, d = dm.TM, dm.TPS, dm.D

    def body(u_in, mp_ref, g_ref, u_ref):
        u_ref[...] = _rms_scaled(mp_ref[...], g_ref[...])

    return pl.pallas_call(
        body, name="prenorm_meta", grid=(dm.Bl,),
        in_specs=[pl.BlockSpec(memory_space=pl.ANY), pl.BlockSpec((tm, d), lambda i: (0, 0)),
                  pl.BlockSpec((1, d), lambda i: (0, 0))],
        out_specs=pl.BlockSpec((tm, d), lambda i: (i * tps, 0)),
        out_shape=jax.ShapeDtypeStruct((dm.T, d), BF16), input_output_aliases={0: 0}, compiler_params=_cp(1),
    )(u, metapad, g_pre)


def _matmul_tn(a, b, out_dtype, name, tt=2816, tn=1024, tk=1024):
    t, k = a.shape
    n = b.shape[1]
    tt, tn, tk = _pick(t, tt, 16), _pick(n, tn, 128), _pick(k, tk, 128)
    nt = t // tt

    def body(a_ref, b_ref, o_ref, acc):
        p = _dot_tn(a_ref[...].astype(BF16), b_ref[...].astype(BF16))
        i = pl.program_id(2)

        @pl.when(i == 0)
        def _():
            acc[...] = p

        @pl.when(i > 0)
        def _():
            acc[...] += p

        @pl.when(i == nt - 1)
        def _():
            o_ref[...] = acc[...].astype(out_dtype)

    return pl.pallas_call(
        body, name=name, grid=(k // tk, n // tn, nt),
        in_specs=[pl.BlockSpec((tt, tk), lambda kk, j, i: (i, kk)), pl.BlockSpec((tt, tn), lambda kk, j, i: (i, j))],
        out_specs=pl.BlockSpec((tk, tn), lambda kk, j, i: (kk, j)),
        out_shape=jax.ShapeDtypeStruct((k, n), out_dtype), scratch_shapes=[pltpu.VMEM((tk, tn), F32)],
        compiler_params=_cp(3),
    )(a, b)


def _matmul_tn_group(a_list, b, moves, out_rows, extra, name, tt=2816, tile=1024):
    t, n = b.shape
    tt = _pick(t, tt, 16)
    nt = t // tt
    counts = [a.shape[1] // tile for a in a_list]
    starts = [sum(counts[:m]) for m in range(len(a_list))]
    items = sum(counts)
    extra_rows, extra_at = extra
    cuts = [[] for _ in range(items)]
    for row, rows, at in moves:
        while rows > 0:
            p, r = divmod(row, tile)
            take = min(rows, tile - r)
            cuts[p].append((r, take, at))
            row, rows, at = row + take, rows - take, at + take
    assert all(v % BF16_TILE_ROWS == 0 for cut in cuts for move in cut for v in move)
    assert sum(rows for _, rows, _ in moves) + extra_rows.shape[0] == out_rows

    def active(p, m):
        return (p >= starts[m]) & (p < starts[m] + counts[m])

    def body(*refs):
        a_refs, b_ref, x_ref = refs[:len(a_list)], refs[len(a_list)], refs[len(a_list) + 1]
        o_ref, acc, stage, sems, x_sem, abuf, a_sems = refs[-7:]
        p, i = pl.program_id(0), pl.program_id(1)

        def fetch(p, i, slot, m):
            cols = pl.ds(pl.multiple_of((p - starts[m]) * tile, tile), tile)
            return pltpu.make_async_copy(a_refs[m].at[pl.ds(pl.multiple_of(i * tt, tt), tt), cols], abuf.at[slot], a_sems.at[slot])

        def start_fetch(p, i, slot):
            for m in range(len(a_list)):
                @pl.when(active(p, m))
                def _(m=m):
                    fetch(p, i, slot, m).start()

        step = p * nt + i
        slot = step % 2

        @pl.when(step == 0)
        def _():
            start_fetch(p, i, slot)

        @pl.when(step + 1 < items * nt)
        def _():
            last = i == nt - 1
            start_fetch(jnp.where(last, p + 1, p), jnp.where(last, 0, i + 1), 1 - slot)

        pltpu.make_async_copy(a_refs[0].at[pl.ds(0, tt), pl.ds(0, tile)], abuf.at[slot], a_sems.at[slot]).wait()

        def writes(item):
            return [pltpu.make_async_copy(stage.at[pl.ds(r, rows), :], o_ref.at[pl.ds(at, rows), :], sems.at[s])
                    for s, (r, rows, at) in enumerate(cuts[item])]

        extra_copy = pltpu.make_async_copy(x_ref, o_ref.at[pl.ds(extra_at, extra_rows.shape[0]), :], x_sem.at[0])

        @pl.when((p == 0) & (i == 0))
        def _():
            extra_copy.start()

        prod = _dot_tn(abuf[slot], b_ref[...])

        @pl.when(i == 0)
        def _():
            acc[...] = prod

        @pl.when(i > 0)
        def _():
            acc[...] += prod

        for item in range(items):
            @pl.when((p == item) & (i == nt - 1))
            def _(item=item):
                if item > 0:
                    for cp in writes(item - 1):
                        cp.wait()
                stage[...] = acc[...].astype(BF16)
                for cp in writes(item):
                    cp.start()
                if item == items - 1:
                    for cp in writes(item):
                        cp.wait()
                    extra_copy.wait()

    assert all(a.dtype == BF16 for a in a_list) and b.dtype == BF16
    any_spec = pl.BlockSpec(memory_space=pl.ANY)
    return pl.pallas_call(
        body, name=name, grid=(items, nt),
        in_specs=[any_spec] * len(a_list) + [pl.BlockSpec((tt, n), lambda p, i: (i, 0)), pl.BlockSpec(memory_space=pltpu.VMEM)],
        out_specs=any_spec, out_shape=jax.ShapeDtypeStruct((out_rows, n), BF16),
        scratch_shapes=[pltpu.VMEM((tile, n), F32), pltpu.VMEM((tile, n), BF16),
                        pltpu.SemaphoreType.DMA((max(len(cut) for cut in cuts),)), pltpu.SemaphoreType.DMA((1,)),
                        pltpu.VMEM((2, tt, tile), BF16), pltpu.SemaphoreType.DMA((2,))],
        compiler_params=_cp(2),
    )(*a_list, b, extra_rows)


BF16_TILE_ROWS = 16


def _shard_offset(index, shard_rows):
    return (index * shard_rows) % BF16_TILE_ROWS


def _padded_shard_rows(shard_rows):
    return -(-(shard_rows + max(_shard_offset(j, shard_rows) for j in range(N_DEV))) // BF16_TILE_ROWS) * BF16_TILE_ROWS


def _pad_shard(wt_shard, index):
    rows, d = wt_shard.shape
    return lax.dynamic_update_slice(jnp.zeros((_padded_shard_rows(rows), d), wt_shard.dtype), wt_shard,
                                    (_shard_offset(index, rows), 0))


def _shard_window(index, shard_rows):
    return index * shard_rows - _shard_offset(index, shard_rows)


def _packed_parts(dm):
    d, dk, hk, hv, cw, nj, hw = dm.D, dm.DK, dm.HK, dm.HV, dm.CW, dm.NJ, dm.HW
    blocks = [(0, (j * 4 + p) * cw, p * d + j * cw, cw) for j in range(nj) for p in range(4)]
    for h in range(HEADS):
        blocks += [(1, h * hw, 4 * d + h * hk, hk), (1, h * hw + hk, 4 * d + dk + h * hk, hk),
                   (1, h * hw + 2 * hk, 5 * d + h * hv, hv), (1, h * hw + 2 * hk + hv, 6 * d + h * hv, hv)]
    blocks += [(2, 0, 7 * d + 2 * RANK, 2 * d), (3, 0, 7 * d, 2 * RANK)]
    return [4 * d, 3 * d, 2 * d, LR_LANES], blocks


def _pack_plan(dm):
    sh = (9 * dm.D + 2 * RANK) // N_DEV
    tile = BF16_TILE_ROWS
    copies, straddles = [], []
    for part, dst, r0, n in _packed_parts(dm)[1]:
        for j in range(N_DEV):
            a, b = max(r0, sh * j), min(r0 + n, sh * (j + 1))
            if a >= b:
                continue
            a_up, b_down = -(-a // tile) * tile, b // tile * tile
            if b_down > a_up:
                copies.append((j, a_up - sh * j + _shard_offset(j, sh), b_down - a_up, part, dst + a_up - r0))
            if a % tile:
                lo = a // tile * tile
                straddles.append((j, lo - sh * (j - 1) + _shard_offset(j - 1, sh), part, dst + lo - r0, a - lo))
    return copies, straddles


def _packed_scratch(dm):
    copies, straddles = _pack_plan(dm)
    return ([pltpu.VMEM((rows, dm.D), BF16) for rows in _packed_parts(dm)[0]]
            + [pltpu.VMEM((2 * max(len(straddles), 1), BF16_TILE_ROWS, dm.D), BF16),
               pltpu.SemaphoreType.DMA((len(copies) + 2 * len(straddles),))])


def _load_packed(g_ref, parts, edges, sems, dm):
    copies, straddles = _pack_plan(dm)
    tile = BF16_TILE_ROWS
    parts[3][2 * RANK:, :] = jnp.zeros((LR_LANES - 2 * RANK, dm.D), BF16)
    dmas = [pltpu.make_async_copy(g_ref.at[j, pl.ds(src, n), :], parts[p].at[pl.ds(dst, n), :], sems.at[i])
            for i, (j, src, n, p, dst) in enumerate(copies)]
    for i, (j, src, p, dst, split) in enumerate(straddles):
        k = len(copies) + 2 * i
        dmas.append(pltpu.make_async_copy(g_ref.at[j - 1, pl.ds(src, tile), :], edges.at[2 * i], sems.at[k]))
        dmas.append(pltpu.make_async_copy(g_ref.at[j, pl.ds(0, tile), :], edges.at[2 * i + 1], sems.at[k + 1]))
    for cp in dmas:
        cp.start()
    for cp in dmas:
        cp.wait()
    row = lax.broadcasted_iota(jnp.int32, (tile, dm.D), 0)
    for i, (j, src, p, dst, split) in enumerate(straddles):
        parts[p][dst:dst + tile, :] = jnp.where(row < split, edges[2 * i], edges[2 * i + 1])


def _inproj(u, gathered, dm):
    t, d = u.shape
    tm = _pick(t, 512, 16)
    widths = _packed_parts(dm)[0]
    cn = 1024

    def body(u_ref, g_ref, *rest):
        outs, parts, (edges, sems) = rest[:4], rest[4:8], rest[8:]

        @pl.when(pl.program_id(0) == 0)
        def _():
            _load_packed(g_ref, parts, edges, sems, dm)

        ut = u_ref[...]
        for w, o_ref in zip(parts, outs):
            n = w.shape[0]
            step = cn if n % cn == 0 else n
            for j in range(0, n, step):
                o_ref[:, j:j + step] = _dot_nt(ut, w[j:j + step, :]).astype(BF16)

    return pl.pallas_call(
        body, name="inproj", grid=(t // tm,),
        in_specs=[pl.BlockSpec((tm, d), lambda i: (i, 0)), pl.BlockSpec(memory_space=pl.ANY)],
        out_specs=[pl.BlockSpec((tm, w), lambda i: (i, 0)) for w in widths],
        out_shape=[jax.ShapeDtypeStruct((t, w), BF16) for w in widths],
        scratch_shapes=_packed_scratch(dm), compiler_params=_cp(1),
    )(u, gathered)


def _conv_rows(dm):
    return _pick(dm.LP, 256, 16)


def _shifted(m, prev_row, next_row, rows):
    row = lax.broadcasted_iota(jnp.int32, m.shape, 0)
    m_prev = jnp.where(row == 0, prev_row, pltpu.roll(m, 1, 0))
    m_next = jnp.where(row == rows - 1, next_row, pltpu.roll(m, rows - 1, 0))
    return m_prev, m_next


def _conv_fwd(proj_a, conv_w, dm):
    lp, cw, rc = dm.LP, dm.CW, _conv_rows(dm)
    nchunk = lp // rc

    def body(p_ref, w_ref, y_ref):
        w0, w1, w2 = w_ref[0:1, :], w_ref[1:2, :], w_ref[2:3, :]

        def chunk(ci, carry):
            r0 = pl.multiple_of(ci * rc, rc)
            blk = p_ref[pl.ds(r0, rc), :].astype(F32)
            cb, cc, cx, cz = (blk[:, i * cw:(i + 1) * cw] for i in range(4))
            m = cc * cx
            rp = pl.multiple_of(jnp.maximum(r0 - 16, 0), 16)
            rn = pl.multiple_of(jnp.minimum(r0 + rc, lp - 16), 16)
            pv = p_ref[pl.ds(rp, 16), cw:3 * cw].astype(F32)
            nx = p_ref[pl.ds(rn, 16), cw:3 * cw].astype(F32)
            prev_row = jnp.where(ci > 0, pv[15:16, :cw] * pv[15:16, cw:], 0.0)
            next_row = jnp.where(ci < nchunk - 1, nx[0:1, :cw] * nx[0:1, cw:], 0.0)
            m_prev, m_next = _shifted(m, prev_row, next_row, rc)
            s = w0 * m_prev + w1 * m + w2 * m_next
            y_ref[pl.ds(r0, rc), :] = (cb * s * (cz * _sigmoid(cz))).astype(BF16)
            return carry

        lax.fori_loop(0, nchunk, chunk, 0)

    return pl.pallas_call(
        body, name="conv_fwd", grid=(dm.Bl, dm.NJ),
        in_specs=[pl.BlockSpec((lp, 4 * cw), lambda s, j: (s, j)), pl.BlockSpec((3, cw), lambda s, j: (0, j))],
        out_specs=pl.BlockSpec((lp, cw), lambda s, j: (s, j)),
        out_shape=jax.ShapeDtypeStruct((dm.T, dm.D), BF16), compiler_params=_cp(2),
    )(proj_a, conv_w)


def _conv_bwd(proj_a, dy_conv, conv_w, dm):
    lp, cw, rc = dm.LP, dm.CW, _conv_rows(dm)
    nchunk = lp // rc

    def body(p_ref, dy_ref, w_ref, d_ref, gw_ref):
        w0, w1, w2 = w_ref[0:1, :], w_ref[1:2, :], w_ref[2:3, :]

        def ds_of(p4, dy):
            cb, cz = p4[:, :cw], p4[:, 3 * cw:]
            return dy * cb * (cz * _sigmoid(cz))

        def chunk(ci, carry):
            g0, g1, g2 = carry
            r0 = pl.multiple_of(ci * rc, rc)
            blk = p_ref[pl.ds(r0, rc), :].astype(F32)
            dy = dy_ref[pl.ds(r0, rc), :].astype(F32)
            cb, cc, cx, cz = (blk[:, i * cw:(i + 1) * cw] for i in range(4))
            rp = pl.multiple_of(jnp.maximum(r0 - 16, 0), 16)
            rn = pl.multiple_of(jnp.minimum(r0 + rc, lp - 16), 16)
            pv = p_ref[pl.ds(rp, 16), :].astype(F32)[15:16]
            nx = p_ref[pl.ds(rn, 16), :].astype(F32)[0:1]
            dpv = dy_ref[pl.ds(rp, 16), :].astype(F32)[15:16]
            dnx = dy_ref[pl.ds(rn, 16), :].astype(F32)[0:1]
            has_prev, has_next = ci > 0, ci < nchunk - 1
            m = cc * cx
            m_prev, m_next = _shifted(m, jnp.where(has_prev, pv[:, cw:2 * cw] * pv[:, 2 * cw:3 * cw], 0.0),
                                      jnp.where(has_next, nx[:, cw:2 * cw] * nx[:, 2 * cw:3 * cw], 0.0), rc)
            s = w0 * m_prev + w1 * m + w2 * m_next
            sg = _sigmoid(cz)
            silu = cz * sg
            ds = dy * cb * silu
            ds_prev, ds_next = _shifted(ds, jnp.where(has_prev, ds_of(pv, dpv), 0.0),
                                        jnp.where(has_next, ds_of(nx, dnx), 0.0), rc)
            dm_ = w0 * ds_next + w1 * ds + w2 * ds_prev
            d_ref[pl.ds(r0, rc), 0:cw] = (dy * s * silu).astype(BF16)
            d_ref[pl.ds(r0, rc), cw:2 * cw] = (dm_ * cx).astype(BF16)
            d_ref[pl.ds(r0, rc), 2 * cw:3 * cw] = (dm_ * cc).astype(BF16)
            d_ref[pl.ds(r0, rc), 3 * cw:4 * cw] = (dy * cb * s * (sg * (1.0 + cz * (1.0 - sg)))).astype(BF16)
            return (g0 + jnp.sum(ds * m_prev, axis=0, keepdims=True), g1 + jnp.sum(ds * m, axis=0, keepdims=True),
                    g2 + jnp.sum(ds * m_next, axis=0, keepdims=True))

        z = jnp.zeros((1, cw), F32)
        g0, g1, g2 = lax.fori_loop(0, nchunk, chunk, (z, z, z))

        @pl.when(pl.program_id(1) == 0)
        def _():
            gw_ref[...] = jnp.zeros_like(gw_ref)

        gw_ref[0:1, :] += g0
        gw_ref[1:2, :] += g1
        gw_ref[2:3, :] += g2

    return pl.pallas_call(
        body, name="conv_bwd", grid=(dm.NJ, dm.Bl),
        in_specs=[pl.BlockSpec((lp, 4 * cw), lambda j, s: (s, j)), pl.BlockSpec((lp, cw), lambda j, s: (s, j)),
                  pl.BlockSpec((3, cw), lambda j, s: (0, j))],
        out_specs=[pl.BlockSpec((lp, 4 * cw), lambda j, s: (s, j)), pl.BlockSpec((8, cw), lambda j, s: (0, j))],
        out_shape=[jax.ShapeDtypeStruct((dm.T, 4 * dm.D), BF16), jax.ShapeDtypeStruct((8, dm.D), F32)],
        compiler_params=_cp(2),
    )(proj_a, dy_conv, conv_w)


def _interleave(gens):
    results = [None] * len(gens)
    live = list(range(len(gens)))
    while live:
        for idx in list(live):
            try:
                next(gens[idx])
            except StopIteration as done:
                results[idx] = done.value
                live.remove(idx)
    return results


def _group_chunks(dm):
    n = dm.NC - dm.C0
    return 3 if n % 3 == 0 else 1


def _group_masks(rows):
    ii = lax.broadcasted_iota(jnp.int32, (rows, rows), 0)
    jj = lax.broadcasted_iota(jnp.int32, (rows, rows), 1)
    same = jnp.right_shift(ii, CHUNK_SHIFT) == jnp.right_shift(jj, CHUNK_SHIFT)
    return same & (jj <= ii), same & (jj > ii)


def _first_row(chunk):
    return chunk * CHUNK if isinstance(chunk, int) else pl.multiple_of(chunk * CHUNK, CHUNK)


def _chunk_totals(b, fwd):
    hk = b.shape[1]
    rows = [b[c * CHUNK + CHUNK - 1:(c + 1) * CHUNK] if fwd else b[c * CHUNK:c * CHUNK + 1]
            for c in range(b.shape[0] // CHUNK)]
    return jnp.concatenate([jnp.broadcast_to(r, (CHUNK, hk)) for r in rows], axis=0)


def _log_gate(lr_rows, w_ref, b_ref, first_group, hk):
    z = _dot(lr_rows, w_ref[...]) + b_ref[...]
    e = jnp.exp(-jnp.abs(z))
    g = (jnp.minimum(z, 0.0) - jnp.log(1.0 + e)) * (1.0 / GATE_NORMALIZER)
    dg_dz = jnp.where(z >= 0.0, e, 1.0) / (1.0 + e) * (1.0 / GATE_NORMALIZER)
    row = lax.broadcasted_iota(jnp.int32, (lr_rows.shape[0], hk), 0)
    pad = first_group & (row < PAD_ROWS)
    return jnp.where(pad, 0.0, g), jnp.where(pad, 0.0, dg_dz)


def _gla_fwd(proj_b, lr, wg_f, bg_f, wg_b, bg_b, gla_g, dm):
    lp, hk, hv, nc, c0, hw = dm.LP, dm.HK, dm.HV, dm.NC, dm.C0, dm.HW
    scale = hk ** -0.5
    gc = _group_chunks(dm)
    gr, ng = gc * CHUNK, (nc - c0) // gc

    def body(p_ref, lr_ref, wf_ref, bf_ref, wb_ref, bb_ref, gg_ref, o_ref, y_ref, st_ref, b_out, gs_out, oacc_f, oacc_b):
        low_incl, up_strict = _group_masks(gr)
        if c0 > 0:
            zr = c0 * CHUNK
            o_ref[0:zr, :] = jnp.zeros((zr, hv), BF16)
            y_ref[0:zr, :] = jnp.zeros((zr, hv), BF16)
            b_out[:, 0:zr, :] = jnp.zeros((2, zr, hk), F32)
            gs_out[:, 0:zr, :] = jnp.zeros((2, zr, hk), F32)
            st_ref[0, 0, :, 0:c0] = jnp.zeros((2, c0, hv, hk), BF16)

        def decay(gi, fwd):
            w_ref, b_ref = (wf_ref, bf_ref) if fwd else (wb_ref, bb_ref)
            r0 = _first_row(c0 + gi * gc)
            yield
            g, dg_dz = _log_gate(lr_ref[pl.ds(r0, gr), :], w_ref, b_ref, gi == 0, hk)
            gs_out[0 if fwd else 1, pl.ds(r0, gr), :] = dg_dz
            yield
            b = _chunk_cumsum(g, not fwd)
            b_out[0 if fwd else 1, pl.ds(r0, gr), :] = b
            return b

        def group(gi, st, b, fwd):
            oacc = oacc_f if fwd else oacc_b
            r0 = pl.multiple_of((c0 + gi * gc) * CHUNK, CHUNK)
            blk = p_ref[pl.ds(r0, gr), :]
            q = blk[:, :hk].astype(F32) * scale
            k = blk[:, hk:2 * hk].astype(F32)
            v = blk[:, 2 * hk:2 * hk + hv]
            btot = _chunk_totals(b, fwd)
            qi = (q * jnp.exp(b)).astype(BF16)
            ki = (k * jnp.exp(-b)).astype(BF16)
            kd = (k * jnp.exp(btot - b)).astype(BF16)
            dec = jnp.exp(btot)
            a = _dot_nt(qi, ki)
            yield
            o = _dot(jnp.where(low_incl if fwd else up_strict, a, 0.0).astype(BF16), v)
            chunk_rows = [slice(c * CHUNK, (c + 1) * CHUNK) for c in range(gc)]
            kv = [_dot_tn(v[rows], kd[rows]) for rows in chunk_rows]
            for c in (range(gc) if fwd else reversed(range(gc))):
                yield
                rows = chunk_rows[c]
                st_b = st.astype(BF16)
                st_ref[0, 0, 0 if fwd else 1, c0 + gi * gc + c] = st_b
                oacc[pl.ds(r0 + c * CHUNK, CHUNK), :] = o[rows] + _dot_nt(qi[rows], st_b)
                st = st * dec[c * CHUNK:c * CHUNK + 1] + kv[c]
            return st

        def step(i, carry):
            st_f, st_b, b_f, b_b = carry
            gf, gb = i, ng - 1 - i
            return tuple(_interleave([group(gf, st_f, b_f, True), group(gb, st_b, b_b, False),
                                      decay(jnp.minimum(gf + 1, ng - 1), True), decay(jnp.maximum(gb - 1, 0), False)]))

        zero = jnp.zeros((hv, hk), F32)
        lax.fori_loop(0, ng, step, (zero, zero, *_interleave([decay(0, True), decay(ng - 1, False)])))

        def finish(i, carry):
            r0 = pl.multiple_of((c0 + i * gc) * CHUNK, CHUNK)
            o = oacc_f[pl.ds(r0, gr), :] + oacc_b[pl.ds(r0, gr), :]
            r = p_ref[pl.ds(r0, gr), 2 * hk + hv:].astype(F32)
            on = o * lax.rsqrt(jnp.mean(o * o, axis=-1, keepdims=True) + EPS) * gg_ref[...]
            o_ref[pl.ds(r0, gr), :] = o.astype(BF16)
            y_ref[pl.ds(r0, gr), :] = (on * r * _sigmoid(r)).astype(BF16)
            return carry

        lax.fori_loop(0, ng, finish, 0)

    head = lambda s, h: (s, h)
    wspec = pl.BlockSpec((LR_LANES, hk), lambda s, h: (0, h))
    bspec = pl.BlockSpec((1, hk), lambda s, h: (0, h))
    return pl.pallas_call(
        body, name="gla_fwd", grid=(dm.Bl, HEADS),
        in_specs=[pl.BlockSpec((lp, hw), head), pl.BlockSpec((lp, LR_LANES), lambda s, h: (s, 0)),
                  wspec, bspec, wspec, bspec, pl.BlockSpec((1, hv), lambda s, h: (0, 0))],
        out_specs=[pl.BlockSpec((lp, hv), head), pl.BlockSpec((lp, hv), head),
                   pl.BlockSpec((1, 1, 2, nc, hv, hk), lambda s, h: (s, h, 0, 0, 0, 0)),
                   pl.BlockSpec((2, lp, hk), lambda s, h: (0, s, h)), pl.BlockSpec((2, lp, hk), lambda s, h: (0, s, h))],
        out_shape=[jax.ShapeDtypeStruct((dm.T, dm.DV), BF16), jax.ShapeDtypeStruct((dm.T, dm.DV), BF16),
                   jax.ShapeDtypeStruct((dm.Bl, HEADS, 2, nc, hv, hk), BF16),
                   jax.ShapeDtypeStruct((2, dm.T, dm.DK), F32), jax.ShapeDtypeStruct((2, dm.T, dm.DK), F32)],
        scratch_shapes=[pltpu.VMEM((lp, hv), F32), pltpu.VMEM((lp, hv), F32)],
        compiler_params=_cp(2),
    )(proj_b, lr, wg_f, bg_f, wg_b, bg_b, gla_g)


def _gla_bwd(proj_b, lr, o_all, dy_gla, states, decays, gate_slopes, wg_f, wg_b, gla_g, dm):
    lp, hk, hv, nc, c0, hw = dm.LP, dm.HK, dm.HV, dm.NC, dm.C0, dm.HW
    scale = hk ** -0.5
    gc = _group_chunks(dm)
    gr, ng = gc * CHUNK, (nc - c0) // gc

    def body(p_ref, lr_ref, o_ref, dy_ref, st_ref, b_ref, gs_ref, wf_ref, wb_ref, gg_ref,
             d_ref, dlr_ref, gwf_ref, gbf_ref, gwb_ref, gbb_ref, ggg_ref, do_s, dq_s, dk_s, dv_s, dz_s):
        low_incl, up_strict = _group_masks(gr)
        h = pl.program_id(1)

        @pl.when(h == 0)
        def _():
            dlr_ref[...] = jnp.zeros_like(dlr_ref)

        if c0 > 0:
            zr = c0 * CHUNK
            d_ref[0:zr, :] = jnp.zeros((zr, hw), BF16)
        for acc in (dq_s, dk_s, dv_s):
            acc[...] = jnp.zeros_like(acc)

        def norm_bwd(i, ggg):
            r0 = pl.multiple_of((c0 + i * gc) * CHUNK, CHUNK)
            o = o_ref[pl.ds(r0, gr), :].astype(F32)
            dy = dy_ref[pl.ds(r0, gr), :].astype(F32)
            r = p_ref[pl.ds(r0, gr), 2 * hk + hv:].astype(F32)
            rstd = lax.rsqrt(jnp.mean(o * o, axis=-1, keepdims=True) + EPS)
            ohat = o * rstd
            sg = _sigmoid(r)
            d_on = dy * (r * sg)
            d_ref[pl.ds(r0, gr), 2 * hk + hv:] = (dy * ohat * gg_ref[...] * (sg * (1.0 + r * (1.0 - sg)))).astype(BF16)
            d_oh = d_on * gg_ref[...]
            do_s[pl.ds(r0, gr), :] = (rstd * (d_oh - ohat * jnp.mean(d_oh * ohat, axis=-1, keepdims=True))).astype(BF16)
            return ggg + jnp.sum(d_on * ohat, axis=0, keepdims=True)

        ggg = lax.fori_loop(0, ng, norm_bwd, jnp.zeros((1, hv), F32))

        @pl.when((pl.program_id(0) == 0) & (h == 0))
        def _():
            ggg_ref[...] = jnp.zeros_like(ggg_ref)

        ggg_ref[0:1, :] += ggg

        def load(gi):
            r0 = pl.multiple_of((c0 + gi * gc) * CHUNK, CHUNK)
            blk = p_ref[pl.ds(r0, gr), :]
            return r0, blk[:, :hk].astype(F32) * scale, blk[:, hk:2 * hk].astype(F32), blk[:, 2 * hk:2 * hk + hv]

        zero = jnp.zeros((hv, hk), F32)

        def grad(gi, carry, fwd):
            dst, gb = carry
            way = 0 if fwd else 1
            mask = low_incl if fwd else up_strict
            r0, q, k, v = load(gi)
            b = b_ref[way, pl.ds(r0, gr), :]
            btot = _chunk_totals(b, fwd)
            eb, enb, edb, dec = jnp.exp(b), jnp.exp(-b), jnp.exp(btot - b), jnp.exp(btot)
            qi_f, ki_f, kd_f = q * eb, k * enb, k * edb
            qi, ki, kd = qi_f.astype(BF16), ki_f.astype(BF16), kd_f.astype(BF16)
            do = do_s[pl.ds(r0, gr), :]
            a = _dot_nt(qi, ki)
            da = _dot_nt(do, v)
            yield
            a = jnp.where(mask, a, 0.0).astype(BF16)
            da = jnp.where(mask, da, 0.0).astype(BF16)
            dv = _dot_tn(a, do)
            dqi = _dot(da, ki)
            dki = _dot_tn(da, qi)
            dv_c, dqi_c, dkd_c, extra_c = [None] * gc, [None] * gc, [None] * gc, [None] * gc
            chunk_rows = [slice(c * CHUNK, (c + 1) * CHUNK) for c in range(gc)]
            qdo = [_dot_tn(do[rows], qi[rows]) for rows in chunk_rows]
            for c in (reversed(range(gc)) if fwd else range(gc)):
                yield
                rows = chunk_rows[c]
                st = st_ref[0, 0, way, c0 + gi * gc + c]
                dsn_b = dst.astype(BF16)
                dec_c = dec[c * CHUNK:c * CHUNK + 1]
                dv_c[c] = dv[rows] + _dot_nt(kd[rows], dsn_b)
                dqi_c[c] = dqi[rows] + _dot(do[rows], st)
                dkd_c[c] = _dot(v[rows], dsn_b)
                ddec = jnp.sum(st.astype(F32) * dst, axis=0, keepdims=True)
                extra = jnp.sum(dkd_c[c] * kd_f[rows], axis=0, keepdims=True) + ddec * dec_c
                extra_c[c] = jnp.broadcast_to(extra, (CHUNK, hk))
                dst = dst * dec_c + qdo[c]
            yield
            dv, dqi = jnp.concatenate(dv_c, axis=0), jnp.concatenate(dqi_c, axis=0)
            dkd, extra = jnp.concatenate(dkd_c, axis=0), jnp.concatenate(extra_c, axis=0)
            dq_s[pl.ds(r0, gr), :] += dqi * eb * scale
            dk_s[pl.ds(r0, gr), :] += dki * enb + dkd * edb
            dv_s[pl.ds(r0, gr), :] += dv
            db = dqi * qi_f - dki * ki_f - dkd * kd_f
            dg = _chunk_cumsum(db, fwd) + extra
            yield
            dz = dg * gs_ref[way, pl.ds(r0, gr), :]
            dz_s[way, pl.ds(r0, gr), :] = dz.astype(BF16)
            return dst, gb + jnp.sum(dz, axis=0, keepdims=True)

        def grad_step(i, carry):
            return tuple(_interleave([grad(ng - 1 - i, carry[0], True), grad(i, carry[1], False)]))

        init = (zero, jnp.zeros((1, hk), F32))
        (_, gb_f), (_, gb_b) = lax.fori_loop(0, ng, grad_step, (init, init))
        used = slice(c0 * CHUNK, lp)
        for way, (w_ref, gw_ref, gb_ref, gb) in enumerate(((wf_ref, gwf_ref, gbf_ref, gb_f), (wb_ref, gwb_ref, gbb_ref, gb_b))):
            dlr_ref[used, :] += _dot_nt(dz_s[way, used, :], w_ref[...])
            gw_ref[0] = _dot_tn(lr_ref[used, :], dz_s[way, used, :])
            gb_ref[0] = jnp.zeros((8, hk), F32)
            gb_ref[0, 0:1, :] = gb

        def combine(i, carry):
            r0 = pl.multiple_of((c0 + i * gc) * CHUNK, CHUNK)
            d_ref[pl.ds(r0, gr), 0:hk] = dq_s[pl.ds(r0, gr), :].astype(BF16)
            d_ref[pl.ds(r0, gr), hk:2 * hk] = dk_s[pl.ds(r0, gr), :].astype(BF16)
            d_ref[pl.ds(r0, gr), 2 * hk:2 * hk + hv] = dv_s[pl.ds(r0, gr), :].astype(BF16)
            return carry

        lax.fori_loop(0, ng, combine, 0)

    head = lambda s, h: (s, h)
    wspec = pl.BlockSpec((LR_LANES, hk), lambda s, h: (0, h))
    gwspec = pl.BlockSpec((1, LR_LANES, hk), lambda s, h: (s, 0, h))
    gbspec = pl.BlockSpec((1, 8, hk), lambda s, h: (s, 0, h))
    gw_shape = jax.ShapeDtypeStruct((dm.Bl, LR_LANES, dm.DK), F32)
    gb_shape = jax.ShapeDtypeStruct((dm.Bl, 8, dm.DK), F32)
    both = pl.BlockSpec((2, lp, hk), lambda s, h: (0, s, h))
    return pl.pallas_call(
        body, name="gla_bwd", grid=(dm.Bl, HEADS),
        in_specs=[pl.BlockSpec((lp, hw), head), pl.BlockSpec((lp, LR_LANES), lambda s, h: (s, 0)),
                  pl.BlockSpec((lp, hv), head), pl.BlockSpec((lp, hv), head),
                  pl.BlockSpec((1, 1, 2, nc, hv, hk), lambda s, h: (s, h, 0, 0, 0, 0)), both, both,
                  wspec, wspec, pl.BlockSpec((1, hv), lambda s, h: (0, 0))],
        out_specs=[pl.BlockSpec((lp, hw), head), pl.BlockSpec((lp, LR_LANES), lambda s, h: (s, 0)),
                   gwspec, gbspec, gwspec, gbspec, pl.BlockSpec((8, hv), lambda s, h: (0, 0))],
        out_shape=[jax.ShapeDtypeStruct((dm.T, HEADS * hw), BF16), jax.ShapeDtypeStruct((dm.T, LR_LANES), F32),
                   gw_shape, gb_shape, gw_shape, gb_shape, jax.ShapeDtypeStruct((8, hv), F32)],
        scratch_shapes=[pltpu.VMEM((lp, hv), BF16), pltpu.VMEM((lp, hk), F32), pltpu.VMEM((lp, hk), F32),
                        pltpu.VMEM((lp, hv), F32), pltpu.VMEM((2, lp, hk), BF16)],
        compiler_params=_cp(2),
    )(proj_b, lr, o_all, dy_gla, states, decays, gate_slopes, wg_f, wg_b, gla_g)


def _stream_tiles(n_tiles, loads, stores, compute):
    for cp in loads(0, 0):
        cp.start()

    def step(t, carry):
        slot = t % 2

        @pl.when(t + 1 < n_tiles)
        def _():
            for cp in loads(t + 1, 1 - slot):
                cp.start()

        for cp in loads(t, slot):
            cp.wait()

        @pl.when(t >= 2)
        def _():
            for cp in stores(t - 2, slot):
                cp.wait()

        compute(t, slot)
        for cp in stores(t, slot):
            cp.start()
        return carry

    lax.fori_loop(0, n_tiles, step, 0)
    for t in range(max(n_tiles - 2, 0), n_tiles):
        for cp in stores(t, t % 2):
            cp.wait()


def _token_tiles(dm, target_rows=512):
    rows = _pick(dm.S, target_rows, 16)
    per_seq = dm.S // rows
    return rows, dm.Bl * per_seq, lambda t: pl.multiple_of((t // per_seq) * dm.LP + dm.TM + (t % per_seq) * rows, 16)


def _head(y_conv, y_gla, proj_c, w_oc, w_og, w_out, x, target, g_post, dm):
    d, tm = dm.D, dm.TM
    rows, n_tiles, first_row = _token_tiles(dm, 512)
    parts = 2 if rows % (2 * BF16_TILE_ROWS) == 0 else 1
    n_out = 8

    def body(*refs):
        yc_hbm, yg_hbm, c_hbm, woc_ref, wog_ref, wo_ref, x_hbm, t_hbm, g_ref = refs[:9]
        outs, st_ref = refs[9:9 + n_out], refs[9 + n_out]
        ycbuf, ygbuf, cbuf, xbuf, tbuf = refs[10 + n_out:15 + n_out]
        obufs = refs[15 + n_out:15 + 2 * n_out]
        zbuf, zbuf2, sem_in, sem_out, sem_zero = refs[15 + 2 * n_out:]

        def loads(t, slot):
            padded = [(yc_hbm, ycbuf), (yg_hbm, ygbuf), (c_hbm, cbuf)]
            own = [(x_hbm, xbuf), (t_hbm, tbuf)]
            return ([pltpu.make_async_copy(h.at[pl.ds(first_row(t), rows), :], b.at[slot], sem_in.at[i, slot])
                     for i, (h, b) in enumerate(padded)] +
                    [pltpu.make_async_copy(h.at[pl.ds(t * rows, rows), :], b.at[slot], sem_in.at[3 + i, slot])
                     for i, (h, b) in enumerate(own)])

        def stores(t, slot):
            return [pltpu.make_async_copy(b.at[slot], h.at[pl.ds(first_row(t), rows), :], sem_out.at[i, slot])
                    for i, (h, b) in enumerate(zip(outs, obufs))]

        def chain(slot, part):
            mg_o, do_o, dy_o, dpc_o, dpg_o, dc_o, dyc_o, dyg_o = obufs
            pc = _dot(ycbuf[slot, part], woc_ref[...])
            pg = _dot(ygbuf[slot, part], wog_ref[...])
            yield
            sa = _sigmoid(cbuf[slot, part, :d].astype(F32))
            sb = _sigmoid(cbuf[slot, part, d:].astype(F32))
            merged = (sa * pc + sb * pg).astype(BF16)
            mg_o[slot, part] = merged
            out = _dot(merged, wo_ref[...])
            yield
            rstd = lax.rsqrt(jnp.mean(out * out, axis=-1, keepdims=True) + EPS)
            ohat = out * rstd
            err = xbuf[slot, part] + ohat * g_ref[...] - tbuf[slot, part]
            dy = err * (1.0 / d)
            d_oh = dy * g_ref[...]
            d_out = (rstd * (d_oh - ohat * jnp.mean(d_oh * ohat, axis=-1, keepdims=True))).astype(BF16)
            do_o[slot, part] = d_out
            dy_o[slot, part] = dy.astype(BF16)
            st_ref[0:1, :] += jnp.sum(dy * ohat, axis=0, keepdims=True)
            st_ref[1:2, :] += jnp.sum(err * err, axis=0, keepdims=True)
            dmg = _dot_nt(d_out, wo_ref[...])
            yield
            dpc = (dmg * sa).astype(BF16)
            dpg = (dmg * sb).astype(BF16)
            dpc_o[slot, part] = dpc
            dpg_o[slot, part] = dpg
            dc_o[slot, part, :d] = (dmg * pc * sa * (1.0 - sa)).astype(BF16)
            dc_o[slot, part, d:] = (dmg * pg * sb * (1.0 - sb)).astype(BF16)
            dyc_o[slot, part] = _dot_nt(dpc, woc_ref[...]).astype(BF16)
            dyg_o[slot, part] = _dot_nt(dpg, wog_ref[...]).astype(BF16)

        def compute(t, slot):
            _interleave([chain(slot, pl.ds(i * (rows // parts), rows // parts)) for i in range(parts)])

        st_ref[...] = jnp.zeros_like(st_ref)
        zbuf[...] = jnp.zeros_like(zbuf)
        zbuf2[...] = jnp.zeros_like(zbuf2)
        zeros = [pltpu.make_async_copy(zbuf2 if out.shape[1] == 2 * d else zbuf, out.at[pl.ds(b * dm.LP, tm), :], sem_zero.at[i, b])
                 for i, out in enumerate(outs) for b in range(dm.Bl)]
        for cp in zeros:
            cp.start()
        _stream_tiles(n_tiles, loads, stores, compute)
        for cp in zeros:
            cp.wait()

    any_spec, vmem = pl.BlockSpec(memory_space=pl.ANY), pl.BlockSpec(memory_space=pltpu.VMEM)
    widths = [d, d, d, d, d, 2 * d, d, d]
    tile = lambda w, dt: pltpu.VMEM((2, rows, w), dt)
    return pl.pallas_call(
        body, name="head", in_specs=[any_spec] * 3 + [vmem] * 3 + [any_spec] * 2 + [vmem],
        out_specs=[any_spec] * n_out + [vmem],
        out_shape=[jax.ShapeDtypeStruct((dm.T, w), BF16) for w in widths] + [jax.ShapeDtypeStruct((8, d), F32)],
        scratch_shapes=[tile(d, BF16), tile(d, BF16), tile(2 * d, BF16), tile(d, F32), tile(d, F32)]
        + [tile(w, BF16) for w in widths]
        + [pltpu.VMEM((tm, d), BF16), pltpu.VMEM((tm, 2 * d), BF16), pltpu.SemaphoreType.DMA((5, 2)),
           pltpu.SemaphoreType.DMA((n_out, 2)), pltpu.SemaphoreType.DMA((n_out, dm.Bl))],
        compiler_params=pltpu.CompilerParams(vmem_limit_bytes=VMEM_LIMIT_BYTES),
    )(y_conv, y_gla, proj_c, w_oc, w_og, w_out, x.reshape(dm.Bl * dm.S, d), target.reshape(dm.Bl * dm.S, d), g_post)


def _grad_h(d_parts, gathered, dy, x, metapad, g_pre, dm):
    d, tm = dm.D, dm.TM
    rows, n_tiles, first_row = _token_tiles(dm, 256)
    widths = [a.shape[1] for a in d_parts]
    np_ = len(d_parts)

    def body(*refs):
        d_hbm, g_hbm, dy_hbm, x_hbm, mp_ref, g_ref = refs[:np_], refs[np_], refs[np_ + 1], refs[np_ + 2], refs[np_ + 3], refs[np_ + 4]
        gx_hbm, dmeta_ref, gg_ref = refs[np_ + 5:np_ + 8]
        parts, edges, sems = refs[np_ + 8:np_ + 12], refs[np_ + 12], refs[np_ + 13]
        dbufs = refs[np_ + 14:2 * np_ + 14]
        dybuf, xbuf, gbuf = refs[2 * np_ + 14:2 * np_ + 17]
        mbufs = refs[2 * np_ + 17:3 * np_ + 17]
        sem_in, sem_out, sem_meta = refs[3 * np_ + 17:]

        def grad_u(tiles):
            du = _dot(tiles[0].astype(BF16), parts[0][...])
            for a, w in zip(tiles[1:], parts[1:]):
                du = du + _dot(a.astype(BF16), w[...])
            return du

        def norm_bwd(h, du, dy):
            rstd = lax.rsqrt(jnp.mean(h * h, axis=-1, keepdims=True) + EPS)
            hhat = h * rstd
            dug = du * g_ref[...]
            gg_ref[0:1, :] += jnp.sum(du * hhat, axis=0, keepdims=True)
            return dy + rstd * (dug - hhat * jnp.mean(dug * hhat, axis=-1, keepdims=True))

        def loads(t, slot):
            padded = list(zip(d_hbm, dbufs)) + [(dy_hbm, dybuf)]
            return ([pltpu.make_async_copy(h.at[pl.ds(first_row(t), rows), :], b.at[slot], sem_in.at[i, slot])
                     for i, (h, b) in enumerate(padded)] +
                    [pltpu.make_async_copy(x_hbm.at[pl.ds(t * rows, rows), :], xbuf.at[slot], sem_in.at[np_ + 1, slot])])

        def stores(t, slot):
            return [pltpu.make_async_copy(gbuf.at[slot], gx_hbm.at[pl.ds(t * rows, rows), :], sem_out.at[slot])]

        def compute(t, slot):
            gbuf[slot] = norm_bwd(xbuf[slot], grad_u([b[slot] for b in dbufs]), dybuf[slot].astype(F32))

        gg_ref[...] = jnp.zeros_like(gg_ref)
        meta = [pltpu.make_async_copy(h.at[pl.ds(b * dm.LP, tm), :], buf.at[pl.ds(b * tm, tm), :], sem_meta.at[i, b])
                for i, (h, buf) in enumerate(zip(d_hbm, mbufs)) for b in range(dm.Bl)]
        for cp in meta:
            cp.start()
        _load_packed(g_hbm, parts, edges, sems, dm)
        _stream_tiles(n_tiles, loads, stores, compute)
        for cp in meta:
            cp.wait()
        dmeta_ref[...] = norm_bwd(jnp.concatenate([mp_ref[...]] * dm.Bl, axis=0), grad_u([buf[...] for buf in mbufs]), 0.0)

    any_spec, vmem = pl.BlockSpec(memory_space=pl.ANY), pl.BlockSpec(memory_space=pltpu.VMEM)
    grad_x, d_meta, gg = pl.pallas_call(
        body, name="grad_h", in_specs=[any_spec] * (np_ + 3) + [vmem, vmem], out_specs=[any_spec, vmem, vmem],
        out_shape=[jax.ShapeDtypeStruct((dm.Bl * dm.S, d), F32), jax.ShapeDtypeStruct((dm.Bl * tm, d), F32),
                   jax.ShapeDtypeStruct((8, d), F32)],
        scratch_shapes=_packed_scratch(dm)
        + [pltpu.VMEM((2, rows, w), a.dtype) for w, a in zip(widths, d_parts)]
        + [pltpu.VMEM((2, rows, d), BF16), pltpu.VMEM((2, rows, d), F32), pltpu.VMEM((2, rows, d), F32)]
        + [pltpu.VMEM((dm.Bl * tm, w), a.dtype) for w, a in zip(widths, d_parts)]
        + [pltpu.SemaphoreType.DMA((np_ + 2, 2)), pltpu.SemaphoreType.DMA((2,)), pltpu.SemaphoreType.DMA((np_, dm.Bl))],
        compiler_params=pltpu.CompilerParams(vmem_limit_bytes=VMEM_LIMIT_BYTES),
    )(*d_parts, gathered, dy, x.reshape(dm.Bl * dm.S, d), metapad, g_pre)
    return grad_x.reshape(dm.Bl, dm.S, d), d_meta.reshape(dm.Bl, tm, d), gg


def _adamw(partials, w, m, v, name, by_columns=False):
    r, c = w.shape
    n_parts, pr = partials.shape[:2]
    assert pr == r or (by_columns and pr == _padded_shard_rows(r))
    tr, tc = (r, _pick(c, 128, 128)) if by_columns else (_pick(r, 256, 16), c)

    def body(p_ref, w_ref, m_ref, v_ref, g_ref, d_ref, nm_ref, nv_ref):
        g = p_ref[0].astype(F32)
        for j in range(1, n_parts):
            g = g + p_ref[j].astype(F32)

        def step(g):
            g_ref[...] = g
            d_ref[...], nm_ref[...], nv_ref[...] = _adam_step(g, w_ref[...], m_ref[...], v_ref[...])

        if pr == r:
            step(g)
        else:
            me = 4 * lax.axis_index("x") + 2 * lax.axis_index("y") + lax.axis_index("c")
            for offset in sorted({_shard_offset(j, r) for j in range(N_DEV)}):
                @pl.when(_shard_offset(me, r) == offset)
                def _(offset=offset):
                    step(g[offset:offset + r])

    at = (lambda i: (0, i)) if by_columns else (lambda i: (i, 0))
    tile = pl.BlockSpec((tr, tc), at)
    out = jax.ShapeDtypeStruct((r, c), F32)
    return pl.pallas_call(
        body, name=name, grid=(c // tc if by_columns else r // tr,),
        in_specs=[pl.BlockSpec((n_parts, pr if by_columns else tr, tc), lambda i: (0,) + at(i)), tile, tile, tile],
        out_specs=[tile, tile, tile, tile], out_shape=[out, out, out, out], compiler_params=_cp(1),
    )(partials, w, m, v)


def _adam_step(g, w, m, v):
    m2 = ADAM_B1 * m + (1.0 - ADAM_B1) * g
    v2 = ADAM_B2 * v + (1.0 - ADAM_B2) * (g * g)
    m_hat = m2 / (1.0 - ADAM_B1 ** ADAM_STEP)
    v_hat = v2 / (1.0 - ADAM_B2 ** ADAM_STEP)
    return -ADAM_LR * (m_hat / (jnp.sqrt(v_hat) + ADAM_EPS) + ADAM_WD * w), m2, v2


def _adamw_small(items, name):
    n = len(items)

    def body(*refs):
        ins, outs = refs[:4 * n], refs[4 * n:]
        for i in range(n):
            p_ref, w_ref, m_ref, v_ref = ins[4 * i:4 * i + 4]
            g = p_ref[0]
            for j in range(1, p_ref.shape[0]):
                g = g + p_ref[j]
            delta, m2, v2 = _adam_step(g, w_ref[...], m_ref[...], v_ref[...])
            for o_ref, val in zip(outs[4 * i:4 * i + 4], (g, delta, m2, v2)):
                o_ref[...] = val

    vmem = pl.BlockSpec(memory_space=pltpu.VMEM)
    res = pl.pallas_call(
        body, name=name, in_specs=[vmem] * (4 * n), out_specs=[vmem] * (4 * n),
        out_shape=[jax.ShapeDtypeStruct(w.shape, F32) for _, w, _, _ in items for _ in range(4)],
    )(*[a for item in items for a in item])
    return [res[4 * i:4 * i + 4] for i in range(n)]


def _unpack_moves(dm):
    d, hk, hv, cw, nj, hw = dm.D, dm.HK, dm.HV, dm.CW, dm.NJ, dm.HW
    moves = [((4 * j + part) * cw, cw, part * d + j * cw) for j in range(nj) for part in range(4)]
    q0 = 4 * d
    k0, v0 = q0 + HEADS * hk, q0 + 2 * HEADS * hk
    r0 = v0 + HEADS * hv
    lr0 = r0 + HEADS * hv
    for h in range(HEADS):
        b0 = 4 * d + h * hw
        moves += [(b0, hk, q0 + h * hk), (b0 + hk, hk, k0 + h * hk), (b0 + 2 * hk, hv, v0 + h * hv),
                  (b0 + 2 * hk + hv, hv, r0 + h * hv)]
    moves.append((4 * d + HEADS * hw, 2 * d, lr0 + 2 * RANK))
    return moves, lr0


def _column_shards(g, shard_shape):
    r, c = g.shape
    return g.reshape(r, N_DEV, c // N_DEV).transpose(1, 0, 2).reshape((N_DEV,) + tuple(shard_shape))


def _join_column_shards(parts):
    r, c = parts.shape[-2:]
    return parts.reshape(N_DEV, r, c).transpose(1, 0, 2).reshape(r, N_DEV * c)


def _local_step(x, target, meta, g_pre, u, wt_shards, conv_w, wg_f, bg_f, wg_b, bg_b, gla_g, out_weights, g_post,
                on_matrix_grads=None):
    bl, s, d = x.shape
    dm = _Dims(bl, s, d)
    metapad = jnp.concatenate([jnp.zeros((dm.TM - N_META, d), F32), meta], axis=0)
    wgp_f = jnp.pad(wg_f, ((0, LR_LANES - RANK), (0, 0))).astype(BF16)
    wgp_b = jnp.pad(wg_b, ((RANK, LR_LANES - 2 * RANK), (0, 0))).astype(BF16)

    u = _prenorm_meta(u, metapad, g_pre, dm)
    proj_a, proj_b, proj_c, lr = _inproj(u, wt_shards, dm)
    y_conv = _conv_fwd(proj_a, conv_w, dm)
    o_all, y_gla, states, decays, gate_slopes = _gla_fwd(proj_b, lr, wgp_f, bg_f, wgp_b, bg_b, gla_g, dm)
    w_oc, w_og, w_out = out_weights(y_conv) if callable(out_weights) else out_weights
    merged, d_out, dy, d_pc, d_pg, d_c, dy_conv, dy_gla, stats = _head(y_conv, y_gla, proj_c, w_oc, w_og, w_out, x, target,
                                                                        g_post, dm)

    g_out = _matmul_tn(merged, d_out, BF16, "grad_w_out")
    g_oc = _matmul_tn(y_conv, d_pc, BF16, "grad_w_out_conv")
    g_og = _matmul_tn(y_gla, d_pg, BF16, "grad_w_out_gla")
    if on_matrix_grads is not None:
        conv_w = conv_w + on_matrix_grads(dict(w_out_conv=g_oc, w_out_gla=g_og, w_merge_out=g_out))
    d_a, g_conv = _conv_bwd(proj_a, dy_conv, conv_w, dm)
    d_b, d_lr, gwp_f, gbp_f, gwp_b, gbp_b, g_gla = _gla_bwd(proj_b, lr, o_all, dy_gla, states, decays, gate_slopes, wgp_f, wgp_b, gla_g, dm)
    moves, lr_at = _unpack_moves(dm)
    g_lr = _matmul_tn(d_lr, u, BF16, "grad_w_in_gate")[:2 * RANK]
    g_in = _matmul_tn_group([d_a, d_b, d_c], u, moves, 9 * d + 2 * RANK, (g_lr, lr_at), "grad_w_in", tile=d)
    if on_matrix_grads is not None:
        g_pre = g_pre + on_matrix_grads(dict(w_in=g_in))
    grad_x, d_meta, g_pre_rows = _grad_h([d_a, d_b, d_c, d_lr], wt_shards, dy, x, metapad, g_pre, dm)

    grads = dict(
        meta_tokens=jnp.sum(d_meta[:, dm.TM - N_META:, :], axis=0), norm_pre=g_pre_rows[0:1], w_in=g_in,
        conv_w=g_conv[0:3], w_gate_fwd=jnp.sum(gwp_f, axis=0)[:RANK], b_gate_fwd=jnp.sum(gbp_f, axis=0)[0:1],
        w_gate_bwd=jnp.sum(gwp_b, axis=0)[RANK:2 * RANK], b_gate_bwd=jnp.sum(gbp_b, axis=0)[0:1],
        gla_norm=g_gla[0:1], w_out_conv=g_oc, w_out_gla=g_og, w_merge_out=g_out, norm_post=stats[0:1])
    return stats[1:2], grad_x, grads


MATRICES = ("w_out_conv", "w_out_gla", "w_merge_out")
SMALL_SHARDED = ("meta_tokens", "conv_w", "w_gate_fwd", "w_gate_bwd")
REPLICATED = ("norm_pre", "b_gate_fwd", "b_gate_bwd", "gla_norm", "norm_post")
NAMES = ("meta_tokens", "norm_pre", "w_in", "conv_w", "w_gate_fwd", "b_gate_fwd", "w_gate_bwd", "b_gate_bwd", "gla_norm",
         "w_out_conv", "w_out_gla", "w_merge_out", "norm_post")


def kernel(x, meta_tokens, norm_pre, w_in, conv_w, w_gate_fwd, b_gate_fwd, w_gate_bwd, b_gate_bwd, gla_norm, w_out_conv, w_out_gla, w_merge_out, norm_post, loss_target, m_meta_tokens, m_norm_pre, m_w_in, m_conv_w, m_w_gate_fwd, m_b_gate_fwd, m_w_gate_bwd, m_b_gate_bwd, m_gla_norm, m_w_out_conv, m_w_out_gla, m_w_merge_out, m_norm_post, v_meta_tokens, v_norm_pre, v_w_in, v_conv_w, v_w_gate_fwd, v_b_gate_fwd, v_w_gate_bwd, v_b_gate_bwd, v_gla_norm, v_w_out_conv, v_w_out_gla, v_w_merge_out, v_norm_post):
    w = dict(meta_tokens=meta_tokens, norm_pre=norm_pre, w_in=w_in[0], conv_w=conv_w, w_gate_fwd=w_gate_fwd,
             b_gate_fwd=b_gate_fwd, w_gate_bwd=w_gate_bwd, b_gate_bwd=b_gate_bwd, gla_norm=gla_norm,
             w_out_conv=w_out_conv[0], w_out_gla=w_out_gla[0], w_merge_out=w_merge_out[0], norm_post=norm_post)
    m = dict(meta_tokens=m_meta_tokens, norm_pre=m_norm_pre, w_in=m_w_in[0], conv_w=m_conv_w, w_gate_fwd=m_w_gate_fwd,
             b_gate_fwd=m_b_gate_fwd, w_gate_bwd=m_w_gate_bwd, b_gate_bwd=m_b_gate_bwd, gla_norm=m_gla_norm,
             w_out_conv=m_w_out_conv[0], w_out_gla=m_w_out_gla[0], w_merge_out=m_w_merge_out[0], norm_post=m_norm_post)
    v = dict(meta_tokens=v_meta_tokens, norm_pre=v_norm_pre, w_in=v_w_in[0], conv_w=v_conv_w, w_gate_fwd=v_w_gate_fwd,
             b_gate_fwd=v_b_gate_fwd, w_gate_bwd=v_w_gate_bwd, b_gate_bwd=v_b_gate_bwd, gla_norm=v_gla_norm,
             w_out_conv=v_w_out_conv[0], w_out_gla=v_w_out_gla[0], w_merge_out=v_w_merge_out[0], norm_post=v_norm_post)
    d = x.shape[-1]

    dm = _Dims(*x.shape)
    me = 4 * lax.axis_index("x") + 2 * lax.axis_index("y") + lax.axis_index("c")
    wt_shards, *small_all, u = _gather_two_level(
        [_pad_shard(w["w_in"].T.astype(BF16), me)] + [w[n] for n in SMALL_SHARDED], "gather_weights",
        _prenorm_tokens_side(x, norm_pre, dm))
    started, late_weights = _exchange_start([w[n].astype(BF16) for n in MATRICES], [], small_all[0], "gather_out_weights_start")
    small = {n: _join_column_shards(p) for n, p in zip(SMALL_SHARDED, small_all)}
    small["meta_tokens"] = small["meta_tokens"] + started

    def out_weights(after):
        return tuple(a.reshape(-1, d) for a in _exchange_wait(late_weights, after, "gather_out_weights_wait"))

    pending = []

    def on_matrix_grads(g):
        blocks = [t.reshape(N_DEV, -1, d) if t.shape[0] % (N_DEV * BF16_TILE_ROWS) == 0 else (t, t.shape[0] // N_DEV)
                  for t in g.values()]
        token, state = _exchange_start([], blocks, None, "exchange_grads_start_" + "_".join(g))
        pending.append((tuple(g), state))
        return token

    sq_err_cols, grad_x, grads = _local_step(
        x, loss_target, small["meta_tokens"], norm_pre, u, wt_shards, small["conv_w"], small["w_gate_fwd"], b_gate_fwd,
        small["w_gate_bwd"], b_gate_bwd, gla_norm, out_weights, norm_post, on_matrix_grads)
    received = {}
    for names, state in pending:
        received.update(zip(names, _exchange_wait(state, grad_x, "exchange_grads_wait_" + "_".join(names))))

    exchanged = _exchange([grads[n] for n in REPLICATED] + [sq_err_cols],
                          [_column_shards(grads[n], w[n].shape) for n in SMALL_SHARDED], "exchange_small_grads")
    small_recv = exchanged[:len(REPLICATED)] + exchanged[len(REPLICATED) + 1:]
    loss = 0.5 / d * jnp.sum(exchanged[len(REPLICATED)])

    results = {"w_in": [r.T[None] for r in _adamw(received["w_in"], w["w_in"].T, m["w_in"].T, v["w_in"].T, "adamw_w_in", by_columns=True)]}
    for n in MATRICES:
        results[n] = [r[None] for r in _adamw(received[n], w[n], m[n], v[n], "adamw_" + n)]
    small_names = REPLICATED + SMALL_SHARDED
    results.update(zip(small_names, _adamw_small([(p, w[n], m[n], v[n]) for n, p in zip(small_names, small_recv)], "adamw_small")))
    return (loss, grad_x, *[results[n][i] for i in range(4) for n in NAMES])
```

```python
import jax
import jax.numpy as jnp
from jax import lax
from jax.experimental import pallas as pl
from jax.experimental.pallas import tpu as pltpu

F32 = jnp.float32
BF16 = jnp.bfloat16
MESH = pl.DeviceIdType.MESH

N_META = 16
CHUNK = 64
CHUNK_SHIFT = 6
HEADS = 4
RANK = 16
LR_LANES = 128
PAD_ROWS = CHUNK - N_META
EPS = 1e-6
GATE_NORMALIZER = 16.0
N_DEV = 8
ADAM_LR, ADAM_B1, ADAM_B2, ADAM_EPS, ADAM_WD, ADAM_STEP = 0.001, 0.9, 0.999, 1e-08, 0.01, 10
VMEM_LIMIT_BYTES = 56 * 1024 * 1024


class _Dims:
    def __init__(self, bl, s, d):
        self.Bl, self.S, self.D = bl, s, d
        self.TM = CHUNK
        self.LP = self.TM + s
        self.T = bl * self.LP
        self.TPS = self.LP // self.TM
        self.NC = self.LP // CHUNK
        self.C0 = (self.TM - CHUNK) // CHUNK
        self.DK, self.DV = d // 2, d
        self.HK, self.HV = self.DK // HEADS, self.DV // HEADS
        self.HW = 2 * self.HK + 2 * self.HV
        self.CW = 256 if d % 256 == 0 and d > 256 else d // 4
        self.NJ = d // self.CW


def _pick(n, target, mult):
    t = min(n, target)
    while t >= mult:
        if n % t == 0 and t % mult == 0:
            return t
        t -= mult
    return n


def _cp(n_axes):
    return pltpu.CompilerParams(dimension_semantics=("arbitrary",) * n_axes, vmem_limit_bytes=VMEM_LIMIT_BYTES)


def _sigmoid(x):
    return 1.0 / (1.0 + jnp.exp(-x))


def _dot(a, b):
    return jnp.dot(a, b, preferred_element_type=F32)


def _dot_nt(a, b):
    return lax.dot_general(a, b, (((1,), (1,)), ((), ())), preferred_element_type=F32)


def _dot_tn(a, b):
    return lax.dot_general(a, b, (((0,), (0,)), ((), ())), preferred_element_type=F32)


def _chunk_cumsum(x, reverse):
    rows = x.shape[0]
    r = lax.broadcasted_iota(jnp.int32, x.shape, 0) & (CHUNK - 1)
    step = 1
    while step < CHUNK:
        if reverse:
            x = x + jnp.where(r < CHUNK - step, pltpu.roll(x, rows - step, 0), 0.0)
        else:
            x = x + jnp.where(r >= step, pltpu.roll(x, step, 0), 0.0)
        step *= 2
    return x


def _exchange(gathers, scatters, name):
    arrays = list(gathers) + list(scatters)
    n, ng = len(arrays), len(gathers)

    def body(*refs):
        ins, outs = refs[:n], refs[n:2 * n]
        send_sems, recv_sems, local_sems = refs[2 * n:]
        x, y, c = lax.axis_index("x"), lax.axis_index("y"), lax.axis_index("c")
        me = 4 * x + 2 * y + c
        started = []
        for t in range(n):
            src, dst = ins[t], outs[t]
            own = pltpu.make_async_copy(src if t < ng else src.at[me], dst.at[me], local_sems.at[t])
            own.start()
            started.append(own)
            for k, pos, peer in _peers(x, y, c):
                cp = pltpu.make_async_remote_copy(
                    src_ref=src if t < ng else src.at[peer], dst_ref=dst.at[me],
                    send_sem=send_sems.at[t * (N_DEV - 1) + k - 1], recv_sem=recv_sems.at[t * (N_DEV - 1) + k - 1],
                    device_id=pos, device_id_type=MESH)
                cp.start()
                started.append(cp)
        for cp in started:
            cp.wait()

    out_shape = [jax.ShapeDtypeStruct((N_DEV,) + a.shape if t < ng else a.shape, a.dtype) for t, a in enumerate(arrays)]
    any_spec = pl.BlockSpec(memory_space=pl.ANY)
    return pl.pallas_call(
        body, name=name, out_shape=out_shape, in_specs=[any_spec] * n, out_specs=[any_spec] * n,
        scratch_shapes=[pltpu.SemaphoreType.DMA((n * (N_DEV - 1),)), pltpu.SemaphoreType.DMA((n * (N_DEV - 1),)),
                        pltpu.SemaphoreType.DMA((n,))],
        compiler_params=pltpu.CompilerParams(has_side_effects=True),
    )(*arrays)


def _gather_two_level(arrays, name, side=None):
    n = len(arrays)
    per = N_DEV - 1
    work, side_in, side_in_specs, side_out, side_out_specs, side_scratch = side or (None, [], [], [], [], [])
    n_in, n_out = len(side_in), len(side_out)

    def body(*refs):
        ins, outs = refs[:n], refs[n + n_in:2 * n + n_in]
        send_sems, recv_sems, local_sems = refs[2 * n + n_in + n_out:2 * n + n_in + n_out + 3]
        x, y, c = lax.axis_index("x"), lax.axis_index("y"), lax.axis_index("c")
        sibling = (x, y, 1 - c)
        chips = [(1 - x, y), (x, 1 - y), (1 - x, 1 - y)]
        index = lambda px, py, pc: 4 * px + 2 * py + pc

        def copy(t, k, block, to, from_input=False):
            slab = outs[t].at[index(*block)]
            return pltpu.make_async_remote_copy(
                src_ref=ins[t] if from_input else slab, dst_ref=slab, send_sem=send_sems.at[t * per + k],
                recv_sem=recv_sems.at[t * per + k], device_id=to, device_id_type=MESH)

        own, sent = [], []
        for t in range(n):
            own.append(pltpu.make_async_copy(ins[t], outs[t].at[index(x, y, c)], local_sems.at[t]))
            own[-1].start()
            first = [copy(t, 0, (x, y, c), sibling, True)]
            first += [copy(t, 1 + j, (x, y, c), (*chip, c), True) for j, chip in enumerate(chips)]
            for cp in first:
                cp.start()
            sent += first
        if work is not None:
            work(refs[n:n + n_in], refs[2 * n + n_in:2 * n + n_in + n_out], refs[2 * n + n_in + n_out + 3:])
        for t in range(n):
            for j, chip in enumerate(chips):
                copy(t, 1 + j, (*chip, c), (x, y, c)).wait_recv()
                sent.append(copy(t, 4 + j, (*chip, c), sibling))
                sent[-1].start()
        for t in range(n):
            copy(t, 0, sibling, (x, y, c)).wait_recv()
            for j, chip in enumerate(chips):
                copy(t, 4 + j, (*chip, 1 - c), (x, y, c)).wait_recv()
        for cp in sent:
            cp.wait_send()
        for cp in own:
            cp.wait()

    out_shape = [jax.ShapeDtypeStruct((N_DEV,) + a.shape, a.dtype) for a in arrays]
    any_spec = pl.BlockSpec(memory_space=pl.ANY)
    return pl.pallas_call(
        body, name=name, out_shape=out_shape + list(side_out), in_specs=[any_spec] * n + list(side_in_specs),
        out_specs=[any_spec] * n + list(side_out_specs),
        scratch_shapes=[pltpu.SemaphoreType.DMA((n * per,)), pltpu.SemaphoreType.DMA((n * per,)),
                        pltpu.SemaphoreType.DMA((n,))] + list(side_scratch),
        compiler_params=pltpu.CompilerParams(has_side_effects=True, vmem_limit_bytes=VMEM_LIMIT_BYTES),
    )(*arrays, *side_in)


def _peers(x, y, c):
    out = []
    for k in range(1, N_DEV):
        px = 1 - x if (k >> 2) & 1 else x
        py = 1 - y if (k >> 1) & 1 else y
        pc = 1 - c if k & 1 else c
        out.append((k, (px, py, pc), 4 * px + 2 * py + pc))
    return out


def _exchange_start(gathers, scatters, after, name):
    shard_rows = [None] * len(gathers) + [s[1] if isinstance(s, tuple) else None for s in scatters]
    arrays = list(gathers) + [s[0] if isinstance(s, tuple) else s for s in scatters]
    n, ng = len(arrays), len(gathers)
    hbm = pl.BlockSpec(memory_space=pltpu.HBM)
    sem = pl.BlockSpec(memory_space=pltpu.SEMAPHORE)

    extra = [] if after is None else [after]
    ne = len(extra)

    def body(*refs):
        ins, lands = refs[:n], refs[n:2 * n]
        send_sems, recv_sems = refs[2 * n + ne], refs[2 * n + ne + 1]
        token = refs[4 * n + ne + 2]
        x, y, c = lax.axis_index("x"), lax.axis_index("y"), lax.axis_index("c")
        me = 4 * x + 2 * y + c
        for t in range(n):
            for k, pos, peer in _peers(x, y, c):
                pltpu.make_async_remote_copy(
                    src_ref=_block_for(ins[t], peer, t < ng, shard_rows[t]), dst_ref=lands[t].at[me],
                    send_sem=send_sems.at[t * (N_DEV - 1) + k - 1], recv_sem=recv_sems.at[t * (N_DEV - 1) + k - 1],
                    device_id=pos, device_id_type=MESH).start()
        token[...] = jnp.zeros_like(token)

    me = 4 * lax.axis_index("x") + 2 * lax.axis_index("y") + lax.axis_index("c")

    def own_block(t, a):
        if t < ng:
            return a
        if shard_rows[t] is None:
            return lax.dynamic_index_in_dim(a, me, 0, keepdims=False)
        assert all(_shard_window(j, shard_rows[t]) + _padded_shard_rows(shard_rows[t]) <= a.shape[0] for j in range(N_DEV))
        return lax.dynamic_slice_in_dim(a, _shard_window(me, shard_rows[t]), _padded_shard_rows(shard_rows[t]), 0)

    blocks = [own_block(t, a) for t, a in enumerate(arrays)]
    lands = [lax.dynamic_update_index_in_dim(lax.empty((N_DEV,) + b.shape if t < ng or shard_rows[t] else a.shape, a.dtype), b, me, 0)
             for t, (a, b) in enumerate(zip(arrays, blocks))]
    operands = [pltpu.with_memory_space_constraint(a, pltpu.HBM) for a in arrays + lands]
    sems = pltpu.SemaphoreType.DMA((n * (N_DEV - 1),))
    res = pl.pallas_call(
        body, name=name,
        out_shape=(sems, sems, *[pltpu.HBM(a.shape, a.dtype) for a in arrays + lands], jax.ShapeDtypeStruct((8, 128), F32)),
        in_specs=[hbm] * (2 * n) + [pl.BlockSpec(memory_space=pl.ANY)] * ne,
        out_specs=(sem, sem, *[hbm] * (2 * n), pl.BlockSpec(memory_space=pltpu.VMEM)),
        input_output_aliases={i: 2 + i for i in range(2 * n)},
        compiler_params=pltpu.CompilerParams(has_side_effects=pltpu.SideEffectType.DATAFLOW_SIDE_EFFECTING),
    )(*operands, *extra)
    return res[-1][0, 0], (ng, shard_rows, res[0], res[1], list(res[2:2 + n]), list(res[2 + n:2 + 2 * n]))


def _block_for(ref, peer, whole, shard_rows):
    if whole:
        return ref
    if shard_rows is None:
        return ref.at[peer]
    return ref.at[pl.ds(pl.multiple_of(_shard_window(peer, shard_rows), BF16_TILE_ROWS), _padded_shard_rows(shard_rows))]


def _exchange_wait(state, after, name):
    ng, shard_rows, send_sems, recv_sems, sent, lands = state
    n = len(sent)
    hbm = pl.BlockSpec(memory_space=pltpu.HBM)
    sem = pl.BlockSpec(memory_space=pltpu.SEMAPHORE)

    def body(*refs):
        ins, land_refs = refs[:n], refs[n:2 * n]
        send_ref, recv_ref = refs[2 * n], refs[2 * n + 1]
        x, y, c = lax.axis_index("x"), lax.axis_index("y"), lax.axis_index("c")
        me = 4 * x + 2 * y + c
        for t in range(n):
            for k, pos, peer in _peers(x, y, c):
                cp = pltpu.make_async_remote_copy(
                    src_ref=_block_for(ins[t], peer, t < ng, shard_rows[t]), dst_ref=land_refs[t].at[me],
                    send_sem=send_ref.at[t * (N_DEV - 1) + k - 1], recv_sem=recv_ref.at[t * (N_DEV - 1) + k - 1],
                    device_id=pos, device_id_type=MESH)
                cp.wait_send()
                cp.wait_recv()

    res = pl.pallas_call(
        body, name=name, out_shape=tuple(pltpu.HBM(a.shape, a.dtype) for a in sent + lands),
        in_specs=[hbm] * (2 * n) + [sem, sem, pl.BlockSpec(memory_space=pl.ANY)], out_specs=tuple([hbm] * (2 * n)),
        input_output_aliases={i: i for i in range(2 * n)},
        compiler_params=pltpu.CompilerParams(has_side_effects=pltpu.SideEffectType.DATAFLOW_SIDE_EFFECTING),
    )(*sent, *lands, send_sems, recv_sems, after)
    return list(res[n:])


def _rms_scaled(h, g):
    return (h * lax.rsqrt(jnp.mean(h * h, axis=-1, keepdims=True) + EPS) * g).astype(BF16)


def _prenorm_tokens_side(x, g_pre, dm):
    bl, s, d = x.shape
    rows = _pick(s, 512, 16)
    tiles = [(b, j) for b in range(bl) for j in range(s // rows)]

    def work(ins, outs, scratch):
        (x_ref, g_ref), (u_ref,), (xbuf, ubuf, sem_in, sem_out) = ins, outs, scratch

        def load(t, slot):
            b, j = tiles[t]
            return pltpu.make_async_copy(x_ref.at[b, pl.ds(j * rows, rows), :], xbuf.at[slot], sem_in.at[slot])

        def store(t, slot):
            b, j = tiles[t]
            return pltpu.make_async_copy(ubuf.at[slot], u_ref.at[pl.ds(b * dm.LP + dm.TM + j * rows, rows), :], sem_out.at[slot])

        load(0, 0).start()
        for t in range(len(tiles)):
            slot = t % 2
            if t + 1 < len(tiles):
                load(t + 1, 1 - slot).start()
            load(t, slot).wait()
            if t >= 2:
                store(t - 2, slot).wait()
            ubuf[slot] = _rms_scaled(xbuf[slot], g_ref[...])
            store(t, slot).start()
        for t in range(max(len(tiles) - 2, 0), len(tiles)):
            store(t, t % 2).wait()

    any_spec = pl.BlockSpec(memory_space=pl.ANY)
    return (work, [x, g_pre], [any_spec, pl.BlockSpec(memory_space=pltpu.VMEM)],
            [jax.ShapeDtypeStruct((dm.T, d), BF16)], [any_spec],
            [pltpu.VMEM((2, rows, d), F32), pltpu.VMEM((2, rows, d), BF16), pltpu.SemaphoreType.DMA((2,)),
             pltpu.SemaphoreType.DMA((2,))])


def _prenorm_meta(u, metapad, g_pre, dm):
    tm, tps, d = dm.TM, dm.TPS, dm.D

    def body(u_in, mp_ref, g_ref, u_ref):
        u_ref[...] = _rms_scaled(mp_ref[...], g_ref[...])

    return pl.pallas_call(
        body, name="prenorm_meta", grid=(dm.Bl,),
        in_specs=[pl.BlockSpec(memory_space=pl.ANY), pl.BlockSpec((tm, d), lambda i: (0, 0)),
                  pl.BlockSpec((1, d), lambda i: (0, 0))],
        out_specs=pl.BlockSpec((tm, d), lambda i: (i * tps, 0)),
        out_shape=jax.ShapeDtypeStruct((dm.T, d), BF16), input_output_aliases={0: 0}, compiler_params=_cp(1),
    )(u, metapad, g_pre)


def _matmul_tn(a, b, out_dtype, name, tt=2816, tn=1024, tk=1024):
    t, k = a.shape
    n = b.shape[1]
    tt, tn, tk = _pick(t, tt, 16), _pick(n, tn, 128), _pick(k, tk, 128)
    nt = t // tt

    def body(a_ref, b_ref, o_ref, acc):
        p = _dot_tn(a_ref[...].astype(BF16), b_ref[...].astype(BF16))
        i = pl.program_id(2)

        @pl.when(i == 0)
        def _():
            acc[...] = p

        @pl.when(i > 0)
        def _():
            acc[...] += p

        @pl.when(i == nt - 1)
        def _():
            o_ref[...] = acc[...].astype(out_dtype)

    return pl.pallas_call(
        body, name=name, grid=(k // tk, n // tn, nt),
        in_specs=[pl.BlockSpec((tt, tk), lambda kk, j, i: (i, kk)), pl.BlockSpec((tt, tn), lambda kk, j, i: (i, j))],
        out_specs=pl.BlockSpec((tk, tn), lambda kk, j, i: (kk, j)),
        out_shape=jax.ShapeDtypeStruct((k, n), out_dtype), scratch_shapes=[pltpu.VMEM((tk, tn), F32)],
        compiler_params=_cp(3),
    )(a, b)


def _matmul_tn_group(a_list, b, moves, out_rows, extra, name, tt=2816, tile=1024):
    t, n = b.shape
    tt = _pick(t, tt, 16)
    nt = t // tt
    counts = [a.shape[1] // tile for a in a_list]
    starts = [sum(counts[:m]) for m in range(len(a_list))]
    items = sum(counts)
    extra_rows, extra_at = extra
    cuts = [[] for _ in range(items)]
    for row, rows, at in moves:
        while rows > 0:
            p, r = divmod(row, tile)
            take = min(rows, tile - r)
            cuts[p].append((r, take, at))
            row, rows, at = row + take, rows - take, at + take
    assert all(v % BF16_TILE_ROWS == 0 for cut in cuts for move in cut for v in move)
    assert sum(rows for _, rows, _ in moves) + extra_rows.shape[0] == out_rows

    def active(p, m):
        return (p >= starts[m]) & (p < starts[m] + counts[m])

    def body(*refs):
        a_refs, b_ref, x_ref = refs[:len(a_list)], refs[len(a_list)], refs[len(a_list) + 1]
        o_ref, acc, stage, sems, x_sem, abuf, a_sems = refs[-7:]
        p, i = pl.program_id(0), pl.program_id(1)

        def fetch(p, i, slot, m):
            cols = pl.ds(pl.multiple_of((p - starts[m]) * tile, tile), tile)
            return pltpu.make_async_copy(a_refs[m].at[pl.ds(pl.multiple_of(i * tt, tt), tt), cols], abuf.at[slot], a_sems.at[slot])

        def start_fetch(p, i, slot):
            for m in range(len(a_list)):
                @pl.when(active(p, m))
                def _(m=m):
                    fetch(p, i, slot, m).start()

        step = p * nt + i
        slot = step % 2

        @pl.when(step == 0)
        def _():
            start_fetch(p, i, slot)

        @pl.when(step + 1 < items * nt)
        def _():
            last = i == nt - 1
            start_fetch(jnp.where(last, p + 1, p), jnp.where(last, 0, i + 1), 1 - slot)

        pltpu.make_async_copy(a_refs[0].at[pl.ds(0, tt), pl.ds(0, tile)], abuf.at[slot], a_sems.at[slot]).wait()

        def writes(item):
            return [pltpu.make_async_copy(stage.at[pl.ds(r, rows), :], o_ref.at[pl.ds(at, rows), :], sems.at[s])
                    for s, (r, rows, at) in enumerate(cuts[item])]

        extra_copy = pltpu.make_async_copy(x_ref, o_ref.at[pl.ds(extra_at, extra_rows.shape[0]), :], x_sem.at[0])

        @pl.when((p == 0) & (i == 0))
        def _():
            extra_copy.start()

        prod = _dot_tn(abuf[slot], b_ref[...])

        @pl.when(i == 0)
        def _():
            acc[...] = prod

        @pl.when(i > 0)
        def _():
            acc[...] += prod

        for item in range(items):
            @pl.when((p == item) & (i == nt - 1))
            def _(item=item):
                if item > 0:
                    for cp in writes(item - 1):
                        cp.wait()
                stage[...] = acc[...].astype(BF16)
                for cp in writes(item):
                    cp.start()
                if item == items - 1:
                    for cp in writes(item):
                        cp.wait()
                    extra_copy.wait()

    assert all(a.dtype == BF16 for a in a_list) and b.dtype == BF16
    any_spec = pl.BlockSpec(memory_space=pl.ANY)
    return pl.pallas_call(
        body, name=name, grid=(items, nt),
        in_specs=[any_spec] * len(a_list) + [pl.BlockSpec((tt, n), lambda p, i: (i, 0)), pl.BlockSpec(memory_space=pltpu.VMEM)],
        out_specs=any_spec, out_shape=jax.ShapeDtypeStruct((out_rows, n), BF16),
        scratch_shapes=[pltpu.VMEM((tile, n), F32), pltpu.VMEM((tile, n), BF16),
                        pltpu.SemaphoreType.DMA((max(len(cut) for cut in cuts),)), pltpu.SemaphoreType.DMA((1,)),
                        pltpu.VMEM((2, tt, tile), BF16), pltpu.SemaphoreType.DMA((2,))],
        compiler_params=_cp(2),
    )(*a_list, b, extra_rows)


BF16_TILE_ROWS = 16


def _shard_offset(index, shard_rows):
    return (index * shard_rows) % BF16_TILE_ROWS


def _padded_shard_rows(shard_rows):
    return -(-(shard_rows + max(_shard_offset(j, shard_rows) for j in range(N_DEV))) // BF16_TILE_ROWS) * BF16_TILE_ROWS


def _pad_shard(wt_shard, index):
    rows, d = wt_shard.shape
    return lax.dynamic_update_slice(jnp.zeros((_padded_shard_rows(rows), d), wt_shard.dtype), wt_shard,
                                    (_shard_offset(index, rows), 0))


def _shard_window(index, shard_rows):
    return index * shard_rows - _shard_offset(index, shard_rows)


def _packed_parts(dm):
    d, dk, hk, hv, cw, nj, hw = dm.D, dm.DK, dm.HK, dm.HV, dm.CW, dm.NJ, dm.HW
    blocks = [(0, (j * 4 + p) * cw, p * d + j * cw, cw) for j in range(nj) for p in range(4)]
    for h in range(HEADS):
        blocks += [(1, h * hw, 4 * d + h * hk, hk), (1, h * hw + hk, 4 * d + dk + h * hk, hk),
                   (1, h * hw + 2 * hk, 5 * d + h * hv, hv), (1, h * hw + 2 * hk + hv, 6 * d + h * hv, hv)]
    blocks += [(2, 0, 7 * d + 2 * RANK, 2 * d), (3, 0, 7 * d, 2 * RANK)]
    return [4 * d, 3 * d, 2 * d, LR_LANES], blocks


def _pack_plan(dm):
    sh = (9 * dm.D + 2 * RANK) // N_DEV
    tile = BF16_TILE_ROWS
    copies, straddles = [], []
    for part, dst, r0, n in _packed_parts(dm)[1]:
        for j in range(N_DEV):
            a, b = max(r0, sh * j), min(r0 + n, sh * (j + 1))
            if a >= b:
                continue
            a_up, b_down = -(-a // tile) * tile, b // tile * tile
            if b_down > a_up:
                copies.append((j, a_up - sh * j + _shard_offset(j, sh), b_down - a_up, part, dst + a_up - r0))
            if a % tile:
                lo = a // tile * tile
                straddles.append((j, lo - sh * (j - 1) + _shard_offset(j - 1, sh), part, dst + lo - r0, a - lo))
    return copies, straddles


def _packed_scratch(dm):
    copies, straddles = _pack_plan(dm)
    return ([pltpu.VMEM((rows, dm.D), BF16) for rows in _packed_parts(dm)[0]]
            + [pltpu.VMEM((2 * max(len(straddles), 1), BF16_TILE_ROWS, dm.D), BF16),
               pltpu.SemaphoreType.DMA((len(copies) + 2 * len(straddles),))])


def _load_packed(g_ref, parts, edges, sems, dm):
    copies, straddles = _pack_plan(dm)
    tile = BF16_TILE_ROWS
    parts[3][2 * RANK:, :] = jnp.zeros((LR_LANES - 2 * RANK, dm.D), BF16)
    dmas = [pltpu.make_async_copy(g_ref.at[j, pl.ds(src, n), :], parts[p].at[pl.ds(dst, n), :], sems.at[i])
            for i, (j, src, n, p, dst) in enumerate(copies)]
    for i, (j, src, p, dst, split) in enumerate(straddles):
        k = len(copies) + 2 * i
        dmas.append(pltpu.make_async_copy(g_ref.at[j - 1, pl.ds(src, tile), :], edges.at[2 * i], sems.at[k]))
        dmas.append(pltpu.make_async_copy(g_ref.at[j, pl.ds(0, tile), :], edges.at[2 * i + 1], sems.at[k + 1]))
    for cp in dmas:
        cp.start()
    for cp in dmas:
        cp.wait()
    row = lax.broadcasted_iota(jnp.int32, (tile, dm.D), 0)
    for i, (j, src, p, dst, split) in enumerate(straddles):
        parts[p][dst:dst + tile, :] = jnp.where(row < split, edges[2 * i], edges[2 * i + 1])


def _inproj(u, gathered, dm):
    t, d = u.shape
    tm = _pick(t, 512, 16)
    widths = _packed_parts(dm)[0]
    cn = 1024

    def body(u_ref, g_ref, *rest):
        outs, parts, (edges, sems) = rest[:4], rest[4:8], rest[8:]

        @pl.when(pl.program_id(0) == 0)
        def _():
            _load_packed(g_ref, parts, edges, sems, dm)

        ut = u_ref[...]
        for w, o_ref in zip(parts, outs):
            n = w.shape[0]
            step = cn if n % cn == 0 else n
            for j in range(0, n, step):
                o_ref[:, j:j + step] = _dot_nt(ut, w[j:j + step, :]).astype(BF16)

    return pl.pallas_call(
        body, name="inproj", grid=(t // tm,),
        in_specs=[pl.BlockSpec((tm, d), lambda i: (i, 0)), pl.BlockSpec(memory_space=pl.ANY)],
        out_specs=[pl.BlockSpec((tm, w), lambda i: (i, 0)) for w in widths],
        out_shape=[jax.ShapeDtypeStruct((t, w), BF16) for w in widths],
        scratch_shapes=_packed_scratch(dm), compiler_params=_cp(1),
    )(u, gathered)


def _conv_rows(dm):
    return _pick(dm.LP, 256, 16)


def _shifted(m, prev_row, next_row, rows):
    row = lax.broadcasted_iota(jnp.int32, m.shape, 0)
    m_prev = jnp.where(row == 0, prev_row, pltpu.roll(m, 1, 0))
    m_next = jnp.where(row == rows - 1, next_row, pltpu.roll(m, rows - 1, 0))
    return m_prev, m_next


def _conv_fwd(proj_a, conv_w, dm):
    lp, cw, rc = dm.LP, dm.CW, _conv_rows(dm)
    nchunk = lp // rc

    def body(p_ref, w_ref, y_ref):
        w0, w1, w2 = w_ref[0:1, :], w_ref[1:2, :], w_ref[2:3, :]

        def chunk(ci, carry):
            r0 = pl.multiple_of(ci * rc, rc)
            blk = p_ref[pl.ds(r0, rc), :].astype(F32)
            cb, cc, cx, cz = (blk[:, i * cw:(i + 1) * cw] for i in range(4))
            m = cc * cx
            rp = pl.multiple_of(jnp.maximum(r0 - 16, 0), 16)
            rn = pl.multiple_of(jnp.minimum(r0 + rc, lp - 16), 16)
            pv = p_ref[pl.ds(rp, 16), cw:3 * cw].astype(F32)
            nx = p_ref[pl.ds(rn, 16), cw:3 * cw].astype(F32)
            prev_row = jnp.where(ci > 0, pv[15:16, :cw] * pv[15:16, cw:], 0.0)
            next_row = jnp.where(ci < nchunk - 1, nx[0:1, :cw] * nx[0:1, cw:], 0.0)
            m_prev, m_next = _shifted(m, prev_row, next_row, rc)
            s = w0 * m_prev + w1 * m + w2 * m_next
            y_ref[pl.ds(r0, rc), :] = (cb * s * (cz * _sigmoid(cz))).astype(BF16)
            return carry

        lax.fori_loop(0, nchunk, chunk, 0)

    return pl.pallas_call(
        body, name="conv_fwd", grid=(dm.Bl, dm.NJ),
        in_specs=[pl.BlockSpec((lp, 4 * cw), lambda s, j: (s, j)), pl.BlockSpec((3, cw), lambda s, j: (0, j))],
        out_specs=pl.BlockSpec((lp, cw), lambda s, j: (s, j)),
        out_shape=jax.ShapeDtypeStruct((dm.T, dm.D), BF16), compiler_params=_cp(2),
    )(proj_a, conv_w)


def _conv_bwd(proj_a, dy_conv, conv_w, dm):
    lp, cw, rc = dm.LP, dm.CW, _conv_rows(dm)
    nchunk = lp // rc

    def body(p_ref, dy_ref, w_ref, d_ref, gw_ref):
        w0, w1, w2 = w_ref[0:1, :], w_ref[1:2, :], w_ref[2:3, :]

        def ds_of(p4, dy):
            cb, cz = p4[:, :cw], p4[:, 3 * cw:]
            return dy * cb * (cz * _sigmoid(cz))

        def chunk(ci, carry):
            g0, g1, g2 = carry
            r0 = pl.multiple_of(ci * rc, rc)
            blk = p_ref[pl.ds(r0, rc), :].astype(F32)
            dy = dy_ref[pl.ds(r0, rc), :].astype(F32)
            cb, cc, cx, cz = (blk[:, i * cw:(i + 1) * cw] for i in range(4))
            rp = pl.multiple_of(jnp.maximum(r0 - 16, 0), 16)
            rn = pl.multiple_of(jnp.minimum(r0 + rc, lp - 16), 16)
            pv = p_ref[pl.ds(rp, 16), :].astype(F32)[15:16]
            nx = p_ref[pl.ds(rn, 16), :].astype(F32)[0:1]
            dpv = dy_ref[pl.ds(rp, 16), :].astype(F32)[15:16]
            dnx = dy_ref[pl.ds(rn, 16), :].astype(F32)[0:1]
            has_prev, has_next = ci > 0, ci < nchunk - 1
            m = cc * cx
            m_prev, m_next = _shifted(m, jnp.where(has_prev, pv[:, cw:2 * cw] * pv[:, 2 * cw:3 * cw], 0.0),
                                      jnp.where(has_next, nx[:, cw:2 * cw] * nx[:, 2 * cw:3 * cw], 0.0), rc)
            s = w0 * m_prev + w1 * m + w2 * m_next
            sg = _sigmoid(cz)
            silu = cz * sg
            ds = dy * cb * silu
            ds_prev, ds_next = _shifted(ds, jnp.where(has_prev, ds_of(pv, dpv), 0.0),
                                        jnp.where(has_next, ds_of(nx, dnx), 0.0), rc)
            dm_ = w0 * ds_next + w1 * ds + w2 * ds_prev
            d_ref[pl.ds(r0, rc), 0:cw] = (dy * s * silu).astype(BF16)
            d_ref[pl.ds(r0, rc), cw:2 * cw] = (dm_ * cx).astype(BF16)
            d_ref[pl.ds(r0, rc), 2 * cw:3 * cw] = (dm_ * cc).astype(BF16)
            d_ref[pl.ds(r0, rc), 3 * cw:4 * cw] = (dy * cb * s * (sg * (1.0 + cz * (1.0 - sg)))).astype(BF16)
            return (g0 + jnp.sum(ds * m_prev, axis=0, keepdims=True), g1 + jnp.sum(ds * m, axis=0, keepdims=True),
                    g2 + jnp.sum(ds * m_next, axis=0, keepdims=True))

        z = jnp.zeros((1, cw), F32)
        g0, g1, g2 = lax.fori_loop(0, nchunk, chunk, (z, z, z))

        @pl.when(pl.program_id(1) == 0)
        def _():
            gw_ref[...] = jnp.zeros_like(gw_ref)

        gw_ref[0:1, :] += g0
        gw_ref[1:2, :] += g1
        gw_ref[2:3, :] += g2

    return pl.pallas_call(
        body, name="conv_bwd", grid=(dm.NJ, dm.Bl),
        in_specs=[pl.BlockSpec((lp, 4 * cw), lambda j, s: (s, j)), pl.BlockSpec((lp, cw), lambda j, s: (s, j)),
                  pl.BlockSpec((3, cw), lambda j, s: (0, j))],
        out_specs=[pl.BlockSpec((lp, 4 * cw), lambda j, s: (s, j)), pl.BlockSpec((8, cw), lambda j, s: (0, j))],
        out_shape=[jax.ShapeDtypeStruct((dm.T, 4 * dm.D), BF16), jax.ShapeDtypeStruct((8, dm.D), F32)],
        compiler_params=_cp(2),
    )(proj_a, dy_conv, conv_w)


def _interleave(gens):
    results = [None] * len(gens)
    live = list(range(len(gens)))
    while live:
        for idx in list(live):
            try:
                next(gens[idx])
            except StopIteration as done:
                results[idx] = done.value
                live.remove(idx)
    return results


def _group_chunks(dm):
    n = dm.NC - dm.C0
    return 3 if n % 3 == 0 else 1


def _group_masks(rows):
    ii = lax.broadcasted_iota(jnp.int32, (rows, rows), 0)
    jj = lax.broadcasted_iota(jnp.int32, (rows, rows), 1)
    same = jnp.right_shift(ii, CHUNK_SHIFT) == jnp.right_shift(jj, CHUNK_SHIFT)
    return same & (jj <= ii), same & (jj > ii)


def _first_row(chunk):
    return chunk * CHUNK if isinstance(chunk, int) else pl.multiple_of(chunk * CHUNK, CHUNK)


def _chunk_totals(b, fwd):
    hk = b.shape[1]
    rows = [b[c * CHUNK + CHUNK - 1:(c + 1) * CHUNK] if fwd else b[c * CHUNK:c * CHUNK + 1]
            for c in range(b.shape[0] // CHUNK)]
    return jnp.concatenate([jnp.broadcast_to(r, (CHUNK, hk)) for r in rows], axis=0)


def _log_gate(lr_rows, w_ref, b_ref, first_group, hk):
    z = _dot(lr_rows, w_ref[...]) + b_ref[...]
    e = jnp.exp(-jnp.abs(z))
    g = (jnp.minimum(z, 0.0) - jnp.log(1.0 + e)) * (1.0 / GATE_NORMALIZER)
    dg_dz = jnp.where(z >= 0.0, e, 1.0) / (1.0 + e) * (1.0 / GATE_NORMALIZER)
    row = lax.broadcasted_iota(jnp.int32, (lr_rows.shape[0], hk), 0)
    pad = first_group & (row < PAD_ROWS)
    return jnp.where(pad, 0.0, g), jnp.where(pad, 0.0, dg_dz)


def _gla_fwd(proj_b, lr, wg_f, bg_f, wg_b, bg_b, gla_g, dm):
    lp, hk, hv, nc, c0, hw = dm.LP, dm.HK, dm.HV, dm.NC, dm.C0, dm.HW
    scale = hk ** -0.5
    gc = _group_chunks(dm)
    gr, ng = gc * CHUNK, (nc - c0) // gc

    def body(p_ref, lr_ref, wf_ref, bf_ref, wb_ref, bb_ref, gg_ref, o_ref, y_ref, st_ref, b_out, gs_out, oacc_f, oacc_b):
        low_incl, up_strict = _group_masks(gr)
        if c0 > 0:
            zr = c0 * CHUNK
            o_ref[0:zr, :] = jnp.zeros((zr, hv), BF16)
            y_ref[0:zr, :] = jnp.zeros((zr, hv), BF16)
            b_out[:, 0:zr, :] = jnp.zeros((2, zr, hk), F32)
            gs_out[:, 0:zr, :] = jnp.zeros((2, zr, hk), F32)
            st_ref[0, 0, :, 0:c0] = jnp.zeros((2, c0, hv, hk), BF16)

        def decay(gi, fwd):
            w_ref, b_ref = (wf_ref, bf_ref) if fwd else (wb_ref, bb_ref)
            r0 = _first_row(c0 + gi * gc)
            yield
            g, dg_dz = _log_gate(lr_ref[pl.ds(r0, gr), :], w_ref, b_ref, gi == 0, hk)
            gs_out[0 if fwd else 1, pl.ds(r0, gr), :] = dg_dz
            yield
            b = _chunk_cumsum(g, not fwd)
            b_out[0 if fwd else 1, pl.ds(r0, gr), :] = b
            return b

        def group(gi, st, b, fwd):
            oacc = oacc_f if fwd else oacc_b
            r0 = pl.multiple_of((c0 + gi * gc) * CHUNK, CHUNK)
            blk = p_ref[pl.ds(r0, gr), :]
            q = blk[:, :hk].astype(F32) * scale
            k = blk[:, hk:2 * hk].astype(F32)
            v = blk[:, 2 * hk:2 * hk + hv]
            btot = _chunk_totals(b, fwd)
            qi = (q * jnp.exp(b)).astype(BF16)
            ki = (k * jnp.exp(-b)).astype(BF16)
            kd = (k * jnp.exp(btot - b)).astype(BF16)
            dec = jnp.exp(btot)
            a = _dot_nt(qi, ki)
            yield
            o = _dot(jnp.where(low_incl if fwd else up_strict, a, 0.0).astype(BF16), v)
            chunk_rows = [slice(c * CHUNK, (c + 1) * CHUNK) for c in range(gc)]
            kv = [_dot_tn(v[rows], kd[rows]) for rows in chunk_rows]
            for c in (range(gc) if fwd else reversed(range(gc))):
                yield
                rows = chunk_rows[c]
                st_b = st.astype(BF16)
                st_ref[0, 0, 0 if fwd else 1, c0 + gi * gc + c] = st_b
                oacc[pl.ds(r0 + c * CHUNK, CHUNK), :] = o[rows] + _dot_nt(qi[rows], st_b)
                st = st * dec[c * CHUNK:c * CHUNK + 1] + kv[c]
            return st

        def step(i, carry):
            st_f, st_b, b_f, b_b = carry
            gf, gb = i, ng - 1 - i
            return tuple(_interleave([group(gf, st_f, b_f, True), group(gb, st_b, b_b, False),
                                      decay(jnp.minimum(gf + 1, ng - 1), True), decay(jnp.maximum(gb - 1, 0), False)]))

        zero = jnp.zeros((hv, hk), F32)
        lax.fori_loop(0, ng, step, (zero, zero, *_interleave([decay(0, True), decay(ng - 1, False)])))

        def finish(i, carry):
            r0 = pl.multiple_of((c0 + i * gc) * CHUNK, CHUNK)
            o = oacc_f[pl.ds(r0, gr), :] + oacc_b[pl.ds(r0, gr), :]
            r = p_ref[pl.ds(r0, gr), 2 * hk + hv:].astype(F32)
            on = o * lax.rsqrt(jnp.mean(o * o, axis=-1, keepdims=True) + EPS) * gg_ref[...]
            o_ref[pl.ds(r0, gr), :] = o.astype(BF16)
            y_ref[pl.ds(r0, gr), :] = (on * r * _sigmoid(r)).astype(BF16)
            return carry

        lax.fori_loop(0, ng, finish, 0)

    head = lambda s, h: (s, h)
    wspec = pl.BlockSpec((LR_LANES, hk), lambda s, h: (0, h))
    bspec = pl.BlockSpec((1, hk), lambda s, h: (0, h))
    return pl.pallas_call(
        body, name="gla_fwd", grid=(dm.Bl, HEADS),
        in_specs=[pl.BlockSpec((lp, hw), head), pl.BlockSpec((lp, LR_LANES), lambda s, h: (s, 0)),
                  wspec, bspec, wspec, bspec, pl.BlockSpec((1, hv), lambda s, h: (0, 0))],
        out_specs=[pl.BlockSpec((lp, hv), head), pl.BlockSpec((lp, hv), head),
                   pl.BlockSpec((1, 1, 2, nc, hv, hk), lambda s, h: (s, h, 0, 0, 0, 0)),
                   pl.BlockSpec((2, lp, hk), lambda s, h: (0, s, h)), pl.BlockSpec((2, lp, hk), lambda s, h: (0, s, h))],
        out_shape=[jax.ShapeDtypeStruct((dm.T, dm.DV), BF16), jax.ShapeDtypeStruct((dm.T, dm.DV), BF16),
                   jax.ShapeDtypeStruct((dm.Bl, HEADS, 2, nc, hv, hk), BF16),
                   jax.ShapeDtypeStruct((2, dm.T, dm.DK), F32), jax.ShapeDtypeStruct((2, dm.T, dm.DK), F32)],
        scratch_shapes=[pltpu.VMEM((lp, hv), F32), pltpu.VMEM((lp, hv), F32)],
        compiler_params=_cp(2),
    )(proj_b, lr, wg_f, bg_f, wg_b, bg_b, gla_g)


def _gla_bwd(proj_b, lr, o_all, dy_gla, states, decays, gate_slopes, wg_f, wg_b, gla_g, dm):
    lp, hk, hv, nc, c0, hw = dm.LP, dm.HK, dm.HV, dm.NC, dm.C0, dm.HW
    scale = hk ** -0.5
    gc = _group_chunks(dm)
    gr, ng = gc * CHUNK, (nc - c0) // gc

    def body(p_ref, lr_ref, o_ref, dy_ref, st_ref, b_ref, gs_ref, wf_ref, wb_ref, gg_ref,
             d_ref, dlr_ref, gwf_ref, gbf_ref, gwb_ref, gbb_ref, ggg_ref, do_s, dq_s, dk_s, dv_s, dz_s):
        low_incl, up_strict = _group_masks(gr)
        h = pl.program_id(1)

        @pl.when(h == 0)
        def _():
            dlr_ref[...] = jnp.zeros_like(dlr_ref)

        if c0 > 0:
            zr = c0 * CHUNK
            d_ref[0:zr, :] = jnp.zeros((zr, hw), BF16)
        for acc in (dq_s, dk_s, dv_s):
            acc[...] = jnp.zeros_like(acc)

        def norm_bwd(i, ggg):
            r0 = pl.multiple_of((c0 + i * gc) * CHUNK, CHUNK)
            o = o_ref[pl.ds(r0, gr), :].astype(F32)
            dy = dy_ref[pl.ds(r0, gr), :].astype(F32)
            r = p_ref[pl.ds(r0, gr), 2 * hk + hv:].astype(F32)
            rstd = lax.rsqrt(jnp.mean(o * o, axis=-1, keepdims=True) + EPS)
            ohat = o * rstd
            sg = _sigmoid(r)
            d_on = dy * (r * sg)
            d_ref[pl.ds(r0, gr), 2 * hk + hv:] = (dy * ohat * gg_ref[...] * (sg * (1.0 + r * (1.0 - sg)))).astype(BF16)
            d_oh = d_on * gg_ref[...]
            do_s[pl.ds(r0, gr), :] = (rstd * (d_oh - ohat * jnp.mean(d_oh * ohat, axis=-1, keepdims=True))).astype(BF16)
            return ggg + jnp.sum(d_on * ohat, axis=0, keepdims=True)

        ggg = lax.fori_loop(0, ng, norm_bwd, jnp.zeros((1, hv), F32))

        @pl.when((pl.program_id(0) == 0) & (h == 0))
        def _():
            ggg_ref[...] = jnp.zeros_like(ggg_ref)

        ggg_ref[0:1, :] += ggg

        def load(gi):
            r0 = pl.multiple_of((c0 + gi * gc) * CHUNK, CHUNK)
            blk = p_ref[pl.ds(r0, gr), :]
            return r0, blk[:, :hk].astype(F32) * scale, blk[:, hk:2 * hk].astype(F32), blk[:, 2 * hk:2 * hk + hv]

        zero = jnp.zeros((hv, hk), F32)

        def grad(gi, carry, fwd):
            dst, gb = carry
            way = 0 if fwd else 1
            mask = low_incl if fwd else up_strict
            r0, q, k, v = load(gi)
            b = b_ref[way, pl.ds(r0, gr), :]
            btot = _chunk_totals(b, fwd)
            eb, enb, edb, dec = jnp.exp(b), jnp.exp(-b), jnp.exp(btot - b), jnp.exp(btot)
            qi_f, ki_f, kd_f = q * eb, k * enb, k * edb
            qi, ki, kd = qi_f.astype(BF16), ki_f.astype(BF16), kd_f.astype(BF16)
            do = do_s[pl.ds(r0, gr), :]
            a = _dot_nt(qi, ki)
            da = _dot_nt(do, v)
            yield
            a = jnp.where(mask, a, 0.0).astype(BF16)
            da = jnp.where(mask, da, 0.0).astype(BF16)
            dv = _dot_tn(a, do)
            dqi = _dot(da, ki)
            dki = _dot_tn(da, qi)
            dv_c, dqi_c, dkd_c, extra_c = [None] * gc, [None] * gc, [None] * gc, [None] * gc
            chunk_rows = [slice(c * CHUNK, (c + 1) * CHUNK) for c in range(gc)]
            qdo = [_dot_tn(do[rows], qi[rows]) for rows in chunk_rows]
            for c in (reversed(range(gc)) if fwd else range(gc)):
                yield
                rows = chunk_rows[c]
                st = st_ref[0, 0, way, c0 + gi * gc + c]
                dsn_b = dst.astype(BF16)
                dec_c = dec[c * CHUNK:c * CHUNK + 1]
                dv_c[c] = dv[rows] + _dot_nt(kd[rows], dsn_b)
                dqi_c[c] = dqi[rows] + _dot(do[rows], st)
                dkd_c[c] = _dot(v[rows], dsn_b)
                ddec = jnp.sum(st.astype(F32) * dst, axis=0, keepdims=True)
                extra = jnp.sum(dkd_c[c] * kd_f[rows], axis=0, keepdims=True) + ddec * dec_c
                extra_c[c] = jnp.broadcast_to(extra, (CHUNK, hk))
                dst = dst * dec_c + qdo[c]
            yield
            dv, dqi = jnp.concatenate(dv_c, axis=0), jnp.concatenate(dqi_c, axis=0)
            dkd, extra = jnp.concatenate(dkd_c, axis=0), jnp.concatenate(extra_c, axis=0)
            dq_s[pl.ds(r0, gr), :] += dqi * eb * scale
            dk_s[pl.ds(r0, gr), :] += dki * enb + dkd * edb
            dv_s[pl.ds(r0, gr), :] += dv
            db = dqi * qi_f - dki * ki_f - dkd * kd_f
            dg = _chunk_cumsum(db, fwd) + extra
            yield
            dz = dg * gs_ref[way, pl.ds(r0, gr), :]
            dz_s[way, pl.ds(r0, gr), :] = dz.astype(BF16)
            return dst, gb + jnp.sum(dz, axis=0, keepdims=True)

        def grad_step(i, carry):
            return tuple(_interleave([grad(ng - 1 - i, carry[0], True), grad(i, carry[1], False)]))

        init = (zero, jnp.zeros((1, hk), F32))
        (_, gb_f), (_, gb_b) = lax.fori_loop(0, ng, grad_step, (init, init))
        used = slice(c0 * CHUNK, lp)
        for way, (w_ref, gw_ref, gb_ref, gb) in enumerate(((wf_ref, gwf_ref, gbf_ref, gb_f), (wb_ref, gwb_ref, gbb_ref, gb_b))):
            dlr_ref[used, :] += _dot_nt(dz_s[way, used, :], w_ref[...])
            gw_ref[0] = _dot_tn(lr_ref[used, :], dz_s[way, used, :])
            gb_ref[0] = jnp.zeros((8, hk), F32)
            gb_ref[0, 0:1, :] = gb

        def combine(i, carry):
            r0 = pl.multiple_of((c0 + i * gc) * CHUNK, CHUNK)
            d_ref[pl.ds(r0, gr), 0:hk] = dq_s[pl.ds(r0, gr), :].astype(BF16)
            d_ref[pl.ds(r0, gr), hk:2 * hk] = dk_s[pl.ds(r0, gr), :].astype(BF16)
            d_ref[pl.ds(r0, gr), 2 * hk:2 * hk + hv] = dv_s[pl.ds(r0, gr), :].astype(BF16)
            return carry

        lax.fori_loop(0, ng, combine, 0)

    head = lambda s, h: (s, h)
    wspec = pl.BlockSpec((LR_LANES, hk), lambda s, h: (0, h))
    gwspec = pl.BlockSpec((1, LR_LANES, hk), lambda s, h: (s, 0, h))
    gbspec = pl.BlockSpec((1, 8, hk), lambda s, h: (s, 0, h))
    gw_shape = jax.ShapeDtypeStruct((dm.Bl, LR_LANES, dm.DK), F32)
    gb_shape = jax.ShapeDtypeStruct((dm.Bl, 8, dm.DK), F32)
    both = pl.BlockSpec((2, lp, hk), lambda s, h: (0, s, h))
    return pl.pallas_call(
        body, name="gla_bwd", grid=(dm.Bl, HEADS),
        in_specs=[pl.BlockSpec((lp, hw), head), pl.BlockSpec((lp, LR_LANES), lambda s, h: (s, 0)),
                  pl.BlockSpec((lp, hv), head), pl.BlockSpec((lp, hv), head),
                  pl.BlockSpec((1, 1, 2, nc, hv, hk), lambda s, h: (s, h, 0, 0, 0, 0)), both, both,
                  wspec, wspec, pl.BlockSpec((1, hv), lambda s, h: (0, 0))],
        out_specs=[pl.BlockSpec((lp, hw), head), pl.BlockSpec((lp, LR_LANES), lambda s, h: (s, 0)),
                   gwspec, gbspec, gwspec, gbspec, pl.BlockSpec((8, hv), lambda s, h: (0, 0))],
        out_shape=[jax.ShapeDtypeStruct((dm.T, HEADS * hw), BF16), jax.ShapeDtypeStruct((dm.T, LR_LANES), F32),
                   gw_shape, gb_shape, gw_shape, gb_shape, jax.ShapeDtypeStruct((8, hv), F32)],
        scratch_shapes=[pltpu.VMEM((lp, hv), BF16), pltpu.VMEM((lp, hk), F32), pltpu.VMEM((lp, hk), F32),
                        pltpu.VMEM((lp, hv), F32), pltpu.VMEM((2, lp, hk), BF16)],
        compiler_params=_cp(2),
    )(proj_b, lr, o_all, dy_gla, states, decays, gate_slopes, wg_f, wg_b, gla_g)


def _stream_tiles(n_tiles, loads, stores, compute):
    for cp in loads(0, 0):
        cp.start()

    def step(t, carry):
        slot = t % 2

        @pl.when(t + 1 < n_tiles)
        def _():
            for cp in loads(t + 1, 1 - slot):
                cp.start()

        for cp in loads(t, slot):
            cp.wait()

        @pl.when(t >= 2)
        def _():
            for cp in stores(t - 2, slot):
                cp.wait()

        compute(t, slot)
        for cp in stores(t, slot):
            cp.start()
        return carry

    lax.fori_loop(0, n_tiles, step, 0)
    for t in range(max(n_tiles - 2, 0), n_tiles):
        for cp in stores(t, t % 2):
            cp.wait()


def _token_tiles(dm, target_rows=512):
    rows = _pick(dm.S, target_rows, 16)
    per_seq = dm.S // rows
    return rows, dm.Bl * per_seq, lambda t: pl.multiple_of((t // per_seq) * dm.LP + dm.TM + (t % per_seq) * rows, 16)


def _head(y_conv, y_gla, proj_c, w_oc, w_og, w_out, x, target, g_post, dm):
    d, tm = dm.D, dm.TM
    rows, n_tiles, first_row = _token_tiles(dm, 512)
    parts = 2 if rows % (2 * BF16_TILE_ROWS) == 0 else 1
    n_out = 8

    def body(*refs):
        yc_hbm, yg_hbm, c_hbm, woc_ref, wog_ref, wo_ref, x_hbm, t_hbm, g_ref = refs[:9]
        outs, st_ref = refs[9:9 + n_out], refs[9 + n_out]
        ycbuf, ygbuf, cbuf, xbuf, tbuf = refs[10 + n_out:15 + n_out]
        obufs = refs[15 + n_out:15 + 2 * n_out]
        zbuf, zbuf2, sem_in, sem_out, sem_zero = refs[15 + 2 * n_out:]

        def loads(t, slot):
            padded = [(yc_hbm, ycbuf), (yg_hbm, ygbuf), (c_hbm, cbuf)]
            own = [(x_hbm, xbuf), (t_hbm, tbuf)]
            return ([pltpu.make_async_copy(h.at[pl.ds(first_row(t), rows), :], b.at[slot], sem_in.at[i, slot])
                     for i, (h, b) in enumerate(padded)] +
                    [pltpu.make_async_copy(h.at[pl.ds(t * rows, rows), :], b.at[slot], sem_in.at[3 + i, slot])
                     for i, (h, b) in enumerate(own)])

        def stores(t, slot):
            return [pltpu.make_async_copy(b.at[slot], h.at[pl.ds(first_row(t), rows), :], sem_out.at[i, slot])
                    for i, (h, b) in enumerate(zip(outs, obufs))]

        def chain(slot, part):
            mg_o, do_o, dy_o, dpc_o, dpg_o, dc_o, dyc_o, dyg_o = obufs
            pc = _dot(ycbuf[slot, part], woc_ref[...])
            pg = _dot(ygbuf[slot, part], wog_ref[...])
            yield
            sa = _sigmoid(cbuf[slot, part, :d].astype(F32))
            sb = _sigmoid(cbuf[slot, part, d:].astype(F32))
            merged = (sa * pc + sb * pg).astype(BF16)
            mg_o[slot, part] = merged
            out = _dot(merged, wo_ref[...])
            yield
            rstd = lax.rsqrt(jnp.mean(out * out, axis=-1, keepdims=True) + EPS)
            ohat = out * rstd
            err = xbuf[slot, part] + ohat * g_ref[...] - tbuf[slot, part]
            dy = err * (1.0 / d)
            d_oh = dy * g_ref[...]
            d_out = (rstd * (d_oh - ohat * jnp.mean(d_oh * ohat, axis=-1, keepdims=True))).astype(BF16)
            do_o[slot, part] = d_out
            dy_o[slot, part] = dy.astype(BF16)
            st_ref[0:1, :] += jnp.sum(dy * ohat, axis=0, keepdims=True)
            st_ref[1:2, :] += jnp.sum(err * err, axis=0, keepdims=True)
            dmg = _dot_nt(d_out, wo_ref[...])
            yield
            dpc = (dmg * sa).astype(BF16)
            dpg = (dmg * sb).astype(BF16)
            dpc_o[slot, part] = dpc
            dpg_o[slot, part] = dpg
            dc_o[slot, part, :d] = (dmg * pc * sa * (1.0 - sa)).astype(BF16)
            dc_o[slot, part, d:] = (dmg * pg * sb * (1.0 - sb)).astype(BF16)
            dyc_o[slot, part] = _dot_nt(dpc, woc_ref[...]).astype(BF16)
            dyg_o[slot, part] = _dot_nt(dpg, wog_ref[...]).astype(BF16)

        def compute(t, slot):
            _interleave([chain(slot, pl.ds(i * (rows // parts), rows // parts)) for i in range(parts)])

        st_ref[...] = jnp.zeros_like(st_ref)
        zbuf[...] = jnp.zeros_like(zbuf)
        zbuf2[...] = jnp.zeros_like(zbuf2)
        zeros = [pltpu.make_async_copy(zbuf2 if out.shape[1] == 2 * d else zbuf, out.at[pl.ds(b * dm.LP, tm), :], sem_zero.at[i, b])
                 for i, out in enumerate(outs) for b in range(dm.Bl)]
        for cp in zeros:
            cp.start()
        _stream_tiles(n_tiles, loads, stores, compute)
        for cp in zeros:
            cp.wait()

    any_spec, vmem = pl.BlockSpec(memory_space=pl.ANY), pl.BlockSpec(memory_space=pltpu.VMEM)
    widths = [d, d, d, d, d, 2 * d, d, d]
    tile = lambda w, dt: pltpu.VMEM((2, rows, w), dt)
    return pl.pallas_call(
        body, name="head", in_specs=[any_spec] * 3 + [vmem] * 3 + [any_spec] * 2 + [vmem],
        out_specs=[any_spec] * n_out + [vmem],
        out_shape=[jax.ShapeDtypeStruct((dm.T, w), BF16) for w in widths] + [jax.ShapeDtypeStruct((8, d), F32)],
        scratch_shapes=[tile(d, BF16), tile(d, BF16), tile(2 * d, BF16), tile(d, F32), tile(d, F32)]
        + [tile(w, BF16) for w in widths]
        + [pltpu.VMEM((tm, d), BF16), pltpu.VMEM((tm, 2 * d), BF16), pltpu.SemaphoreType.DMA((5, 2)),
           pltpu.SemaphoreType.DMA((n_out, 2)), pltpu.SemaphoreType.DMA((n_out, dm.Bl))],
        compiler_params=pltpu.CompilerParams(vmem_limit_bytes=VMEM_LIMIT_BYTES),
    )(y_conv, y_gla, proj_c, w_oc, w_og, w_out, x.reshape(dm.Bl * dm.S, d), target.reshape(dm.Bl * dm.S, d), g_post)


def _grad_h(d_parts, gathered, dy, x, metapad, g_pre, dm):
    d, tm = dm.D, dm.TM
    rows, n_tiles, first_row = _token_tiles(dm, 256)
    widths = [a.shape[1] for a in d_parts]
    np_ = len(d_parts)

    def body(*refs):
        d_hbm, g_hbm, dy_hbm, x_hbm, mp_ref, g_ref = refs[:np_], refs[np_], refs[np_ + 1], refs[np_ + 2], refs[np_ + 3], refs[np_ + 4]
        gx_hbm, dmeta_ref, gg_ref = refs[np_ + 5:np_ + 8]
        parts, edges, sems = refs[np_ + 8:np_ + 12], refs[np_ + 12], refs[np_ + 13]
        dbufs = refs[np_ + 14:2 * np_ + 14]
        dybuf, xbuf, gbuf = refs[2 * np_ + 14:2 * np_ + 17]
        mbufs = refs[2 * np_ + 17:3 * np_ + 17]
        sem_in, sem_out, sem_meta = refs[3 * np_ + 17:]

        def grad_u(tiles):
            du = _dot(tiles[0].astype(BF16), parts[0][...])
            for a, w in zip(tiles[1:], parts[1:]):
                du = du + _dot(a.astype(BF16), w[...])
            return du

        def norm_bwd(h, du, dy):
            rstd = lax.rsqrt(jnp.mean(h * h, axis=-1, keepdims=True) + EPS)
            hhat = h * rstd
            dug = du * g_ref[...]
            gg_ref[0:1, :] += jnp.sum(du * hhat, axis=0, keepdims=True)
            return dy + rstd * (dug - hhat * jnp.mean(dug * hhat, axis=-1, keepdims=True))

        def loads(t, slot):
            padded = list(zip(d_hbm, dbufs)) + [(dy_hbm, dybuf)]
            return ([pltpu.make_async_copy(h.at[pl.ds(first_row(t), rows), :], b.at[slot], sem_in.at[i, slot])
                     for i, (h, b) in enumerate(padded)] +
                    [pltpu.make_async_copy(x_hbm.at[pl.ds(t * rows, rows), :], xbuf.at[slot], sem_in.at[np_ + 1, slot])])

        def stores(t, slot):
            return [pltpu.make_async_copy(gbuf.at[slot], gx_hbm.at[pl.ds(t * rows, rows), :], sem_out.at[slot])]

        def compute(t, slot):
            gbuf[slot] = norm_bwd(xbuf[slot], grad_u([b[slot] for b in dbufs]), dybuf[slot].astype(F32))

        gg_ref[...] = jnp.zeros_like(gg_ref)
        meta = [pltpu.make_async_copy(h.at[pl.ds(b * dm.LP, tm), :], buf.at[pl.ds(b * tm, tm), :], sem_meta.at[i, b])
                for i, (h, buf) in enumerate(zip(d_hbm, mbufs)) for b in range(dm.Bl)]
        for cp in meta:
            cp.start()
        _load_packed(g_hbm, parts, edges, sems, dm)
        _stream_tiles(n_tiles, loads, stores, compute)
        for cp in meta:
            cp.wait()
        dmeta_ref[...] = norm_bwd(jnp.concatenate([mp_ref[...]] * dm.Bl, axis=0), grad_u([buf[...] for buf in mbufs]), 0.0)

    any_spec, vmem = pl.BlockSpec(memory_space=pl.ANY), pl.BlockSpec(memory_space=pltpu.VMEM)
    grad_x, d_meta, gg = pl.pallas_call(
        body, name="grad_h", in_specs=[any_spec] * (np_ + 3) + [vmem, vmem], out_specs=[any_spec, vmem, vmem],
        out_shape=[jax.ShapeDtypeStruct((dm.Bl * dm.S, d), F32), jax.ShapeDtypeStruct((dm.Bl * tm, d), F32),
                   jax.ShapeDtypeStruct((8, d), F32)],
        scratch_shapes=_packed_scratch(dm)
        + [pltpu.VMEM((2, rows, w), a.dtype) for w, a in zip(widths, d_parts)]
        + [pltpu.VMEM((2, rows, d), BF16), pltpu.VMEM((2, rows, d), F32), pltpu.VMEM((2, rows, d), F32)]
        + [pltpu.VMEM((dm.Bl * tm, w), a.dtype) for w, a in zip(widths, d_parts)]
        + [pltpu.SemaphoreType.DMA((np_ + 2, 2)), pltpu.SemaphoreType.DMA((2,)), pltpu.SemaphoreType.DMA((np_, dm.Bl))],
        compiler_params=pltpu.CompilerParams(vmem_limit_bytes=VMEM_LIMIT_BYTES),
    )(*d_parts, gathered, dy, x.reshape(dm.Bl * dm.S, d), metapad, g_pre)
    return grad_x.reshape(dm.Bl, dm.S, d), d_meta.reshape(dm.Bl, tm, d), gg


def _adamw(partials, w, m, v, name, by_columns=False):
    rows_apart = w.ndim == 3
    r, c = w.shape[0], w.shape[-1]
    n_parts, pr = partials.shape[:2]
    assert pr == r or (by_columns and pr == _padded_shard_rows(r))
    assert by_columns or not rows_apart
    tr, tc = (r, _pick(c, 128, 128)) if by_columns else (_pick(r, 256, 16), c)

    def body(p_ref, w_ref, m_ref, v_ref, g_ref, d_ref, nm_ref, nv_ref):
        g = p_ref[0].astype(F32)
        for j in range(1, n_parts):
            g = g + p_ref[j].astype(F32)

        def step(g):
            results = (g,) + _adam_step(g, *[ref[...].reshape(tr, tc) for ref in (w_ref, m_ref, v_ref)])
            for ref, val in zip((g_ref, d_ref, nm_ref, nv_ref), results):
                ref[...] = val.reshape(ref.shape)

        if pr == r:
            step(g)
        else:
            me = 4 * lax.axis_index("x") + 2 * lax.axis_index("y") + lax.axis_index("c")
            for offset in sorted({_shard_offset(j, r) for j in range(N_DEV)}):
                @pl.when(_shard_offset(me, r) == offset)
                def _(offset=offset):
                    step(g[offset:offset + r])

    at = (lambda i: (0, i)) if by_columns else (lambda i: (i, 0))
    tile = pl.BlockSpec((tr, 1, tc), lambda i: (0, 0, i)) if rows_apart else pl.BlockSpec((tr, tc), at)
    out = jax.ShapeDtypeStruct(w.shape, F32)
    return pl.pallas_call(
        body, name=name, grid=(c // tc if by_columns else r // tr,),
        in_specs=[pl.BlockSpec((n_parts, pr if by_columns else tr, tc), lambda i: (0,) + at(i)), tile, tile, tile],
        out_specs=[tile, tile, tile, tile], out_shape=[out, out, out, out], compiler_params=_cp(1),
    )(partials, w, m, v)


def _adam_step(g, w, m, v):
    m2 = ADAM_B1 * m + (1.0 - ADAM_B1) * g
    v2 = ADAM_B2 * v + (1.0 - ADAM_B2) * (g * g)
    m_hat = m2 / (1.0 - ADAM_B1 ** ADAM_STEP)
    v_hat = v2 / (1.0 - ADAM_B2 ** ADAM_STEP)
    return -ADAM_LR * (m_hat / (jnp.sqrt(v_hat) + ADAM_EPS) + ADAM_WD * w), m2, v2


def _adamw_small(items, name):
    n = len(items)

    def body(*refs):
        ins, outs = refs[:4 * n], refs[4 * n:]
        for i in range(n):
            p_ref, w_ref, m_ref, v_ref = ins[4 * i:4 * i + 4]
            g = p_ref[0]
            for j in range(1, p_ref.shape[0]):
                g = g + p_ref[j]
            delta, m2, v2 = _adam_step(g, w_ref[...], m_ref[...], v_ref[...])
            for o_ref, val in zip(outs[4 * i:4 * i + 4], (g, delta, m2, v2)):
                o_ref[...] = val

    vmem = pl.BlockSpec(memory_space=pltpu.VMEM)
    res = pl.pallas_call(
        body, name=name, in_specs=[vmem] * (4 * n), out_specs=[vmem] * (4 * n),
        out_shape=[jax.ShapeDtypeStruct(w.shape, F32) for _, w, _, _ in items for _ in range(4)],
    )(*[a for item in items for a in item])
    return [res[4 * i:4 * i + 4] for i in range(n)]


def _unpack_moves(dm):
    d, hk, hv, cw, nj, hw = dm.D, dm.HK, dm.HV, dm.CW, dm.NJ, dm.HW
    moves = [((4 * j + part) * cw, cw, part * d + j * cw) for j in range(nj) for part in range(4)]
    q0 = 4 * d
    k0, v0 = q0 + HEADS * hk, q0 + 2 * HEADS * hk
    r0 = v0 + HEADS * hv
    lr0 = r0 + HEADS * hv
    for h in range(HEADS):
        b0 = 4 * d + h * hw
        moves += [(b0, hk, q0 + h * hk), (b0 + hk, hk, k0 + h * hk), (b0 + 2 * hk, hv, v0 + h * hv),
                  (b0 + 2 * hk + hv, hv, r0 + h * hv)]
    moves.append((4 * d + HEADS * hw, 2 * d, lr0 + 2 * RANK))
    return moves, lr0


def _column_shards(g, shard_shape):
    r, c = g.shape
    return g.reshape(r, N_DEV, c // N_DEV).transpose(1, 0, 2).reshape((N_DEV,) + tuple(shard_shape))


def _join_column_shards(parts):
    r, c = parts.shape[-2:]
    return parts.reshape(N_DEV, r, c).transpose(1, 0, 2).reshape(r, N_DEV * c)


def _local_step(x, target, meta, g_pre, u, wt_shards, conv_w, wg_f, bg_f, wg_b, bg_b, gla_g, out_weights, g_post,
                on_matrix_grads=None):
    bl, s, d = x.shape
    dm = _Dims(bl, s, d)
    metapad = jnp.concatenate([jnp.zeros((dm.TM - N_META, d), F32), meta], axis=0)
    wgp_f = jnp.pad(wg_f, ((0, LR_LANES - RANK), (0, 0))).astype(BF16)
    wgp_b = jnp.pad(wg_b, ((RANK, LR_LANES - 2 * RANK), (0, 0))).astype(BF16)

    u = _prenorm_meta(u, metapad, g_pre, dm)
    proj_a, proj_b, proj_c, lr = _inproj(u, wt_shards, dm)
    y_conv = _conv_fwd(proj_a, conv_w, dm)
    o_all, y_gla, states, decays, gate_slopes = _gla_fwd(proj_b, lr, wgp_f, bg_f, wgp_b, bg_b, gla_g, dm)
    w_oc, w_og, w_out = out_weights(y_conv) if callable(out_weights) else out_weights
    merged, d_out, dy, d_pc, d_pg, d_c, dy_conv, dy_gla, stats = _head(y_conv, y_gla, proj_c, w_oc, w_og, w_out, x, target,
                                                                        g_post, dm)

    g_out = _matmul_tn(merged, d_out, BF16, "grad_w_out")
    g_oc = _matmul_tn(y_conv, d_pc, BF16, "grad_w_out_conv")
    g_og = _matmul_tn(y_gla, d_pg, BF16, "grad_w_out_gla")
    if on_matrix_grads is not None:
        conv_w = conv_w + on_matrix_grads(dict(w_out_conv=g_oc, w_out_gla=g_og, w_merge_out=g_out))
    d_a, g_conv = _conv_bwd(proj_a, dy_conv, conv_w, dm)
    d_b, d_lr, gwp_f, gbp_f, gwp_b, gbp_b, g_gla = _gla_bwd(proj_b, lr, o_all, dy_gla, states, decays, gate_slopes, wgp_f, wgp_b, gla_g, dm)
    moves, lr_at = _unpack_moves(dm)
    g_lr = _matmul_tn(d_lr, u, BF16, "grad_w_in_gate")[:2 * RANK]
    g_in = _matmul_tn_group([d_a, d_b, d_c], u, moves, 9 * d + 2 * RANK, (g_lr, lr_at), "grad_w_in", tile=d)
    if on_matrix_grads is not None:
        g_pre = g_pre + on_matrix_grads(dict(w_in=g_in))
    grad_x, d_meta, g_pre_rows = _grad_h([d_a, d_b, d_c, d_lr], wt_shards, dy, x, metapad, g_pre, dm)

    grads = dict(
        meta_tokens=jnp.sum(d_meta[:, dm.TM - N_META:, :], axis=0), norm_pre=g_pre_rows[0:1], w_in=g_in,
        conv_w=g_conv[0:3], w_gate_fwd=jnp.sum(gwp_f, axis=0)[:RANK], b_gate_fwd=jnp.sum(gbp_f, axis=0)[0:1],
        w_gate_bwd=jnp.sum(gwp_b, axis=0)[RANK:2 * RANK], b_gate_bwd=jnp.sum(gbp_b, axis=0)[0:1],
        gla_norm=g_gla[0:1], w_out_conv=g_oc, w_out_gla=g_og, w_merge_out=g_out, norm_post=stats[0:1])
    return stats[1:2], grad_x, grads


MATRICES = ("w_out_conv", "w_out_gla", "w_merge_out")
SMALL_SHARDED = ("meta_tokens", "conv_w", "w_gate_fwd", "w_gate_bwd")
REPLICATED = ("norm_pre", "b_gate_fwd", "b_gate_bwd", "gla_norm", "norm_post")
NAMES = ("meta_tokens", "norm_pre", "w_in", "conv_w", "w_gate_fwd", "b_gate_fwd", "w_gate_bwd", "b_gate_bwd", "gla_norm",
         "w_out_conv", "w_out_gla", "w_merge_out", "norm_post")


def kernel(x, meta_tokens, norm_pre, w_in, conv_w, w_gate_fwd, b_gate_fwd, w_gate_bwd, b_gate_bwd, gla_norm, w_out_conv, w_out_gla, w_merge_out, norm_post, loss_target, m_meta_tokens, m_norm_pre, m_w_in, m_conv_w, m_w_gate_fwd, m_b_gate_fwd, m_w_gate_bwd, m_b_gate_bwd, m_gla_norm, m_w_out_conv, m_w_out_gla, m_w_merge_out, m_norm_post, v_meta_tokens, v_norm_pre, v_w_in, v_conv_w, v_w_gate_fwd, v_b_gate_fwd, v_w_gate_bwd, v_b_gate_bwd, v_gla_norm, v_w_out_conv, v_w_out_gla, v_w_merge_out, v_norm_post):
    w = dict(meta_tokens=meta_tokens, norm_pre=norm_pre, w_in=w_in[0], conv_w=conv_w, w_gate_fwd=w_gate_fwd,
             b_gate_fwd=b_gate_fwd, w_gate_bwd=w_gate_bwd, b_gate_bwd=b_gate_bwd, gla_norm=gla_norm,
             w_out_conv=w_out_conv[0], w_out_gla=w_out_gla[0], w_merge_out=w_merge_out[0], norm_post=norm_post)
    m = dict(meta_tokens=m_meta_tokens, norm_pre=m_norm_pre, w_in=m_w_in[0], conv_w=m_conv_w, w_gate_fwd=m_w_gate_fwd,
             b_gate_fwd=m_b_gate_fwd, w_gate_bwd=m_w_gate_bwd, b_gate_bwd=m_b_gate_bwd, gla_norm=m_gla_norm,
             w_out_conv=m_w_out_conv[0], w_out_gla=m_w_out_gla[0], w_merge_out=m_w_merge_out[0], norm_post=m_norm_post)
    v = dict(meta_tokens=v_meta_tokens, norm_pre=v_norm_pre, w_in=v_w_in[0], conv_w=v_conv_w, w_gate_fwd=v_w_gate_fwd,
             b_gate_fwd=v_b_gate_fwd, w_gate_bwd=v_w_gate_bwd, b_gate_bwd=v_b_gate_bwd, gla_norm=v_gla_norm,
             w_out_conv=v_w_out_conv[0], w_out_gla=v_w_out_gla[0], w_merge_out=v_w_merge_out[0], norm_post=v_norm_post)
    d = x.shape[-1]

    dm = _Dims(*x.shape)
    me = 4 * lax.axis_index("x") + 2 * lax.axis_index("y") + lax.axis_index("c")
    wt_shards, *small_all, u = _gather_two_level(
        [_pad_shard(w["w_in"].T.astype(BF16), me)] + [w[n] for n in SMALL_SHARDED], "gather_weights",
        _prenorm_tokens_side(x, norm_pre, dm))
    started, late_weights = _exchange_start([w[n].astype(BF16) for n in MATRICES], [], small_all[0], "gather_out_weights_start")
    small = {n: _join_column_shards(p) for n, p in zip(SMALL_SHARDED, small_all)}
    small["meta_tokens"] = small["meta_tokens"] + started

    def out_weights(after):
        return tuple(a.reshape(-1, d) for a in _exchange_wait(late_weights, after, "gather_out_weights_wait"))

    pending = []

    def on_matrix_grads(g):
        blocks = [t.reshape(N_DEV, -1, d) if t.shape[0] % (N_DEV * BF16_TILE_ROWS) == 0 else (t, t.shape[0] // N_DEV)
                  for t in g.values()]
        token, state = _exchange_start([], blocks, None, "exchange_grads_start_" + "_".join(g))
        pending.append((tuple(g), state))
        return token

    sq_err_cols, grad_x, grads = _local_step(
        x, loss_target, small["meta_tokens"], norm_pre, u, wt_shards, small["conv_w"], small["w_gate_fwd"], b_gate_fwd,
        small["w_gate_bwd"], b_gate_bwd, gla_norm, out_weights, norm_post, on_matrix_grads)
    received = {}
    for names, state in pending:
        received.update(zip(names, _exchange_wait(state, grad_x, "exchange_grads_wait_" + "_".join(names))))

    exchanged = _exchange([grads[n] for n in REPLICATED] + [sq_err_cols],
                          [_column_shards(grads[n], w[n].shape) for n in SMALL_SHARDED], "exchange_small_grads")
    small_recv = exchanged[:len(REPLICATED)] + exchanged[len(REPLICATED) + 1:]
    loss = 0.5 / d * jnp.sum(exchanged[len(REPLICATED)])

    rows_apart = lambda a: jnp.transpose(a, (2, 0, 1))
    results = {"w_in": [jnp.transpose(r, (1, 2, 0)) for r in _adamw(received["w_in"], rows_apart(w_in), rows_apart(m_w_in),
                                                                     rows_apart(v_w_in), "adamw_w_in", by_columns=True)]}
    for n in MATRICES:
        results[n] = [r[None] for r in _adamw(received[n], w[n], m[n], v[n], "adamw_" + n)]
    small_names = REPLICATED + SMALL_SHARDED
    results.update(zip(small_names, _adamw_small([(p, w[n], m[n], v[n]) for n, p in zip(small_names, small_recv)], "adamw_small")))
    return (loss, grad_x, *[results[n][i] for i in range(4) for n in NAMES])
```

```python
import jax
import jax.numpy as jnp
from jax import lax
from jax.experimental import pallas as pl
from jax.experimental.pallas import tpu as pltpu

F32 = jnp.float32
BF16 = jnp.bfloat16
MESH = pl.DeviceIdType.MESH

N_META = 16
CHUNK = 64
CHUNK_SHIFT = 6
HEADS = 4
RANK = 16
LR_LANES = 128
PAD_ROWS = CHUNK - N_META
EPS = 1e-6
GATE_NORMALIZER = 16.0
N_DEV = 8
ADAM_LR, ADAM_B1, ADAM_B2, ADAM_EPS, ADAM_WD, ADAM_STEP = 0.001, 0.9, 0.999, 1e-08, 0.01, 10
VMEM_LIMIT_BYTES = 56 * 1024 * 1024


class _Dims:
    def __init__(self, bl, s, d):
        self.Bl, self.S, self.D = bl, s, d
        self.TM = CHUNK
        self.LP = self.TM + s
        self.T = bl * self.LP
        self.TPS = self.LP // self.TM
        self.NC = self.LP // CHUNK
        self.C0 = (self.TM - CHUNK) // CHUNK
        self.DK, self.DV = d // 2, d
        self.HK, self.HV = self.DK // HEADS, self.DV // HEADS
        self.HW = 2 * self.HK + 2 * self.HV
        self.CW = 256 if d % 256 == 0 and d > 256 else d // 4
        self.NJ = d // self.CW


def _pick(n, target, mult):
    t = min(n, target)
    while t >= mult:
        if n % t == 0 and t % mult == 0:
            return t
        t -= mult
    return n


def _cp(n_axes):
    return pltpu.CompilerParams(dimension_semantics=("arbitrary",) * n_axes, vmem_limit_bytes=VMEM_LIMIT_BYTES)


def _sigmoid(x):
    return 1.0 / (1.0 + jnp.exp(-x))


def _dot(a, b):
    return jnp.dot(a, b, preferred_element_type=F32)


def _dot_nt(a, b):
    return lax.dot_general(a, b, (((1,), (1,)), ((), ())), preferred_element_type=F32)


def _dot_tn(a, b):
    return lax.dot_general(a, b, (((0,), (0,)), ((), ())), preferred_element_type=F32)


def _chunk_cumsum(x, reverse):
    rows = x.shape[0]
    r = lax.broadcasted_iota(jnp.int32, x.shape, 0) & (CHUNK - 1)
    step = 1
    while step < CHUNK:
        if reverse:
            x = x + jnp.where(r < CHUNK - step, pltpu.roll(x, rows - step, 0), 0.0)
        else:
            x = x + jnp.where(r >= step, pltpu.roll(x, step, 0), 0.0)
        step *= 2
    return x


def _exchange(gathers, scatters, name):
    arrays = list(gathers) + list(scatters)
    n, ng = len(arrays), len(gathers)

    def body(*refs):
        ins, outs = refs[:n], refs[n:2 * n]
        send_sems, recv_sems, local_sems = refs[2 * n:]
        x, y, c = lax.axis_index("x"), lax.axis_index("y"), lax.axis_index("c")
        me = 4 * x + 2 * y + c
        started = []
        for t in range(n):
            src, dst = ins[t], outs[t]
            own = pltpu.make_async_copy(src if t < ng else src.at[me], dst.at[me], local_sems.at[t])
            own.start()
            started.append(own)
            for k, pos, peer in _peers(x, y, c):
                cp = pltpu.make_async_remote_copy(
                    src_ref=src if t < ng else src.at[peer], dst_ref=dst.at[me],
                    send_sem=send_sems.at[t * (N_DEV - 1) + k - 1], recv_sem=recv_sems.at[t * (N_DEV - 1) + k - 1],
                    device_id=pos, device_id_type=MESH)
                cp.start()
                started.append(cp)
        for cp in started:
            cp.wait()

    out_shape = [jax.ShapeDtypeStruct((N_DEV,) + a.shape if t < ng else a.shape, a.dtype) for t, a in enumerate(arrays)]
    any_spec = pl.BlockSpec(memory_space=pl.ANY)
    return pl.pallas_call(
        body, name=name, out_shape=out_shape, in_specs=[any_spec] * n, out_specs=[any_spec] * n,
        scratch_shapes=[pltpu.SemaphoreType.DMA((n * (N_DEV - 1),)), pltpu.SemaphoreType.DMA((n * (N_DEV - 1),)),
                        pltpu.SemaphoreType.DMA((n,))],
        compiler_params=pltpu.CompilerParams(has_side_effects=True),
    )(*arrays)


def _gather_two_level(arrays, name, side=None):
    n = len(arrays)
    per = N_DEV - 1
    work, side_in, side_in_specs, side_out, side_out_specs, side_scratch = side or (None, [], [], [], [], [])
    n_in, n_out = len(side_in), len(side_out)

    def body(*refs):
        ins, outs = refs[:n], refs[n + n_in:2 * n + n_in]
        send_sems, recv_sems, local_sems = refs[2 * n + n_in + n_out:2 * n + n_in + n_out + 3]
        x, y, c = lax.axis_index("x"), lax.axis_index("y"), lax.axis_index("c")
        sibling = (x, y, 1 - c)
        chips = [(1 - x, y), (x, 1 - y), (1 - x, 1 - y)]
        index = lambda px, py, pc: 4 * px + 2 * py + pc

        def copy(t, k, block, to, from_input=False):
            slab = outs[t].at[index(*block)]
            return pltpu.make_async_remote_copy(
                src_ref=ins[t] if from_input else slab, dst_ref=slab, send_sem=send_sems.at[t * per + k],
                recv_sem=recv_sems.at[t * per + k], device_id=to, device_id_type=MESH)

        own, sent = [], []
        for t in range(n):
            own.append(pltpu.make_async_copy(ins[t], outs[t].at[index(x, y, c)], local_sems.at[t]))
            own[-1].start()
            first = [copy(t, 0, (x, y, c), sibling, True)]
            first += [copy(t, 1 + j, (x, y, c), (*chip, c), True) for j, chip in enumerate(chips)]
            for cp in first:
                cp.start()
            sent += first
        if work is not None:
            work(refs[n:n + n_in], refs[2 * n + n_in:2 * n + n_in + n_out], refs[2 * n + n_in + n_out + 3:])
        for t in range(n):
            for j, chip in enumerate(chips):
                copy(t, 1 + j, (*chip, c), (x, y, c)).wait_recv()
                sent.append(copy(t, 4 + j, (*chip, c), sibling))
                sent[-1].start()
        for t in range(n):
            copy(t, 0, sibling, (x, y, c)).wait_recv()
            for j, chip in enumerate(chips):
                copy(t, 4 + j, (*chip, 1 - c), (x, y, c)).wait_recv()
        for cp in sent:
            cp.wait_send()
        for cp in own:
            cp.wait()

    out_shape = [jax.ShapeDtypeStruct((N_DEV,) + a.shape, a.dtype) for a in arrays]
    any_spec = pl.BlockSpec(memory_space=pl.ANY)
    return pl.pallas_call(
        body, name=name, out_shape=out_shape + list(side_out), in_specs=[any_spec] * n + list(side_in_specs),
        out_specs=[any_spec] * n + list(side_out_specs),
        scratch_shapes=[pltpu.SemaphoreType.DMA((n * per,)), pltpu.SemaphoreType.DMA((n * per,)),
                        pltpu.SemaphoreType.DMA((n,))] + list(side_scratch),
        compiler_params=pltpu.CompilerParams(has_side_effects=True, vmem_limit_bytes=VMEM_LIMIT_BYTES),
    )(*arrays, *side_in)


def _peers(x, y, c):
    out = []
    for k in range(1, N_DEV):
        px = 1 - x if (k >> 2) & 1 else x
        py = 1 - y if (k >> 1) & 1 else y
        pc = 1 - c if k & 1 else c
        out.append((k, (px, py, pc), 4 * px + 2 * py + pc))
    return out


def _exchange_start(gathers, scatters, after, name):
    shard_rows = [None] * len(gathers) + [s[1] if isinstance(s, tuple) else None for s in scatters]
    arrays = list(gathers) + [s[0] if isinstance(s, tuple) else s for s in scatters]
    n, ng = len(arrays), len(gathers)
    hbm = pl.BlockSpec(memory_space=pltpu.HBM)
    sem = pl.BlockSpec(memory_space=pltpu.SEMAPHORE)

    extra = [] if after is None else [after]
    ne = len(extra)

    def body(*refs):
        ins, lands = refs[:n], refs[n:2 * n]
        send_sems, recv_sems = refs[2 * n + ne], refs[2 * n + ne + 1]
        token = refs[4 * n + ne + 2]
        x, y, c = lax.axis_index("x"), lax.axis_index("y"), lax.axis_index("c")
        me = 4 * x + 2 * y + c
        for t in range(n):
            for k, pos, peer in _peers(x, y, c):
                pltpu.make_async_remote_copy(
                    src_ref=_block_for(ins[t], peer, t < ng, shard_rows[t]), dst_ref=lands[t].at[me],
                    send_sem=send_sems.at[t * (N_DEV - 1) + k - 1], recv_sem=recv_sems.at[t * (N_DEV - 1) + k - 1],
                    device_id=pos, device_id_type=MESH).start()
        token[...] = jnp.zeros_like(token)

    me = 4 * lax.axis_index("x") + 2 * lax.axis_index("y") + lax.axis_index("c")

    def own_block(t, a):
        if t < ng:
            return a
        if shard_rows[t] is None:
            return lax.dynamic_index_in_dim(a, me, 0, keepdims=False)
        assert all(_shard_window(j, shard_rows[t]) + _padded_shard_rows(shard_rows[t]) <= a.shape[0] for j in range(N_DEV))
        return lax.dynamic_slice_in_dim(a, _shard_window(me, shard_rows[t]), _padded_shard_rows(shard_rows[t]), 0)

    blocks = [own_block(t, a) for t, a in enumerate(arrays)]
    lands = [lax.dynamic_update_index_in_dim(lax.empty((N_DEV,) + b.shape if t < ng or shard_rows[t] else a.shape, a.dtype), b, me, 0)
             for t, (a, b) in enumerate(zip(arrays, blocks))]
    operands = [pltpu.with_memory_space_constraint(a, pltpu.HBM) for a in arrays + lands]
    sems = pltpu.SemaphoreType.DMA((n * (N_DEV - 1),))
    res = pl.pallas_call(
        body, name=name,
        out_shape=(sems, sems, *[pltpu.HBM(a.shape, a.dtype) for a in arrays + lands], jax.ShapeDtypeStruct((8, 128), F32)),
        in_specs=[hbm] * (2 * n) + [pl.BlockSpec(memory_space=pl.ANY)] * ne,
        out_specs=(sem, sem, *[hbm] * (2 * n), pl.BlockSpec(memory_space=pltpu.VMEM)),
        input_output_aliases={i: 2 + i for i in range(2 * n)},
        compiler_params=pltpu.CompilerParams(has_side_effects=pltpu.SideEffectType.DATAFLOW_SIDE_EFFECTING),
    )(*operands, *extra)
    return res[-1][0, 0], (ng, shard_rows, res[0], res[1], list(res[2:2 + n]), list(res[2 + n:2 + 2 * n]))


def _block_for(ref, peer, whole, shard_rows):
    if whole:
        return ref
    if shard_rows is None:
        return ref.at[peer]
    return ref.at[pl.ds(pl.multiple_of(_shard_window(peer, shard_rows), BF16_TILE_ROWS), _padded_shard_rows(shard_rows))]


def _exchange_wait(state, after, name):
    ng, shard_rows, send_sems, recv_sems, sent, lands = state
    n = len(sent)
    hbm = pl.BlockSpec(memory_space=pltpu.HBM)
    sem = pl.BlockSpec(memory_space=pltpu.SEMAPHORE)

    def body(*refs):
        ins, land_refs = refs[:n], refs[n:2 * n]
        send_ref, recv_ref = refs[2 * n], refs[2 * n + 1]
        x, y, c = lax.axis_index("x"), lax.axis_index("y"), lax.axis_index("c")
        me = 4 * x + 2 * y + c
        for t in range(n):
            for k, pos, peer in _peers(x, y, c):
                cp = pltpu.make_async_remote_copy(
                    src_ref=_block_for(ins[t], peer, t < ng, shard_rows[t]), dst_ref=land_refs[t].at[me],
                    send_sem=send_ref.at[t * (N_DEV - 1) + k - 1], recv_sem=recv_ref.at[t * (N_DEV - 1) + k - 1],
                    device_id=pos, device_id_type=MESH)
                cp.wait_send()
                cp.wait_recv()

    res = pl.pallas_call(
        body, name=name, out_shape=tuple(pltpu.HBM(a.shape, a.dtype) for a in sent + lands),
        in_specs=[hbm] * (2 * n) + [sem, sem, pl.BlockSpec(memory_space=pl.ANY)], out_specs=tuple([hbm] * (2 * n)),
        input_output_aliases={i: i for i in range(2 * n)},
        compiler_params=pltpu.CompilerParams(has_side_effects=pltpu.SideEffectType.DATAFLOW_SIDE_EFFECTING),
    )(*sent, *lands, send_sems, recv_sems, after)
    return list(res[n:])


def _rms_scaled(h, g):
    return (h * lax.rsqrt(jnp.mean(h * h, axis=-1, keepdims=True) + EPS) * g).astype(BF16)


def _prenorm_tokens_side(x, g_pre, dm):
    bl, s, d = x.shape
    rows = _pick(s, 512, 16)
    tiles = [(b, j) for b in range(bl) for j in range(s // rows)]

    def work(ins, outs, scratch):
        (x_ref, g_ref), (u_ref,), (xbuf, ubuf, sem_in, sem_out) = ins, outs, scratch

        def load(t, slot):
            b, j = tiles[t]
            return pltpu.make_async_copy(x_ref.at[b, pl.ds(j * rows, rows), :], xbuf.at[slot], sem_in.at[slot])

        def store(t, slot):
            b, j = tiles[t]
            return pltpu.make_async_copy(ubuf.at[slot], u_ref.at[pl.ds(b * dm.LP + dm.TM + j * rows, rows), :], sem_out.at[slot])

        load(0, 0).start()
        for t in range(len(tiles)):
            slot = t % 2
            if t + 1 < len(tiles):
                load(t + 1, 1 - slot).start()
            load(t, slot).wait()
            if t >= 2:
                store(t - 2, slot).wait()
            ubuf[slot] = _rms_scaled(xbuf[slot], g_ref[...])
            store(t, slot).start()
        for t in range(max(len(tiles) - 2, 0), len(tiles)):
            store(t, t % 2).wait()

    any_spec = pl.BlockSpec(memory_space=pl.ANY)
    return (work, [x, g_pre], [any_spec, pl.BlockSpec(memory_space=pltpu.VMEM)],
            [jax.ShapeDtypeStruct((dm.T, d), BF16)], [any_spec],
            [pltpu.VMEM((2, rows, d), F32), pltpu.VMEM((2, rows, d), BF16), pltpu.SemaphoreType.DMA((2,)),
             pltpu.SemaphoreType.DMA((2,))])


def _prenorm_meta(u, metapad, g_pre, dm):
    tm, tps, d = dm.TM, dm.TPS, dm.D

    def body(u_in, mp_ref, g_ref, u_ref):
        u_ref[...] = _rms_scaled(mp_ref[...], g_ref[...])

    return pl.pallas_call(
        body, name="prenorm_meta", grid=(dm.Bl,),
        in_specs=[pl.BlockSpec(memory_space=pl.ANY), pl.BlockSpec((tm, d), lambda i: (0, 0)),
                  pl.BlockSpec((1, d), lambda i: (0, 0))],
        out_specs=pl.BlockSpec((tm, d), lambda i: (i * tps, 0)),
        out_shape=jax.ShapeDtypeStruct((dm.T, d), BF16), input_output_aliases={0: 0}, compiler_params=_cp(1),
    )(u, metapad, g_pre)


def _matmul_tn(a, b, out_dtype, name, tt=2816, tn=1024, tk=1024):
    t, k = a.shape
    n = b.shape[1]
    tt, tn, tk = _pick(t, tt, 16), _pick(n, tn, 128), _pick(k, tk, 128)
    nt = t // tt

    def body(a_ref, b_ref, o_ref, acc):
        p = _dot_tn(a_ref[...].astype(BF16), b_ref[...].astype(BF16))
        i = pl.program_id(2)

        @pl.when(i == 0)
        def _():
            acc[...] = p

        @pl.when(i > 0)
        def _():
            acc[...] += p

        @pl.when(i == nt - 1)
        def _():
            o_ref[...] = acc[...].astype(out_dtype)

    return pl.pallas_call(
        body, name=name, grid=(k // tk, n // tn, nt),
        in_specs=[pl.BlockSpec((tt, tk), lambda kk, j, i: (i, kk)), pl.BlockSpec((tt, tn), lambda kk, j, i: (i, j))],
        out_specs=pl.BlockSpec((tk, tn), lambda kk, j, i: (kk, j)),
        out_shape=jax.ShapeDtypeStruct((k, n), out_dtype), scratch_shapes=[pltpu.VMEM((tk, tn), F32)],
        compiler_params=_cp(3),
    )(a, b)


def _matmul_tn_group(a_list, b, moves, out_rows, extra, name, tt=2816, tile=1024):
    t, n = b.shape
    tt = _pick(t, tt, 16)
    nt = t // tt
    counts = [a.shape[1] // tile for a in a_list]
    starts = [sum(counts[:m]) for m in range(len(a_list))]
    items = sum(counts)
    extra_rows, extra_at = extra
    cuts = [[] for _ in range(items)]
    for row, rows, at in moves:
        while rows > 0:
            p, r = divmod(row, tile)
            take = min(rows, tile - r)
            cuts[p].append((r, take, at))
            row, rows, at = row + take, rows - take, at + take
    assert all(v % BF16_TILE_ROWS == 0 for cut in cuts for move in cut for v in move)
    assert sum(rows for _, rows, _ in moves) + extra_rows.shape[0] == out_rows

    def active(p, m):
        return (p >= starts[m]) & (p < starts[m] + counts[m])

    def body(*refs):
        a_refs, b_ref, x_ref = refs[:len(a_list)], refs[len(a_list)], refs[len(a_list) + 1]
        o_ref, acc, stage, sems, x_sem, abuf, a_sems = refs[-7:]
        p, i = pl.program_id(0), pl.program_id(1)

        def fetch(p, i, slot, m):
            cols = pl.ds(pl.multiple_of((p - starts[m]) * tile, tile), tile)
            return pltpu.make_async_copy(a_refs[m].at[pl.ds(pl.multiple_of(i * tt, tt), tt), cols], abuf.at[slot], a_sems.at[slot])

        def start_fetch(p, i, slot):
            for m in range(len(a_list)):
                @pl.when(active(p, m))
                def _(m=m):
                    fetch(p, i, slot, m).start()

        step = p * nt + i
        slot = step % 2

        @pl.when(step == 0)
        def _():
            start_fetch(p, i, slot)

        @pl.when(step + 1 < items * nt)
        def _():
            last = i == nt - 1
            start_fetch(jnp.where(last, p + 1, p), jnp.where(last, 0, i + 1), 1 - slot)

        pltpu.make_async_copy(a_refs[0].at[pl.ds(0, tt), pl.ds(0, tile)], abuf.at[slot], a_sems.at[slot]).wait()

        def writes(item):
            return [pltpu.make_async_copy(stage.at[pl.ds(r, rows), :], o_ref.at[pl.ds(at, rows), :], sems.at[s])
                    for s, (r, rows, at) in enumerate(cuts[item])]

        extra_copy = pltpu.make_async_copy(x_ref, o_ref.at[pl.ds(extra_at, extra_rows.shape[0]), :], x_sem.at[0])

        @pl.when((p == 0) & (i == 0))
        def _():
            extra_copy.start()

        prod = _dot_tn(abuf[slot], b_ref[...])

        @pl.when(i == 0)
        def _():
            acc[...] = prod

        @pl.when(i > 0)
        def _():
            acc[...] += prod

        for item in range(items):
            @pl.when((p == item) & (i == nt - 1))
            def _(item=item):
                if item > 0:
                    for cp in writes(item - 1):
                        cp.wait()
                stage[...] = acc[...].astype(BF16)
                for cp in writes(item):
                    cp.start()
                if item == items - 1:
                    for cp in writes(item):
                        cp.wait()
                    extra_copy.wait()

    assert all(a.dtype == BF16 for a in a_list) and b.dtype == BF16
    any_spec = pl.BlockSpec(memory_space=pl.ANY)
    return pl.pallas_call(
        body, name=name, grid=(items, nt),
        in_specs=[any_spec] * len(a_list) + [pl.BlockSpec((tt, n), lambda p, i: (i, 0)), pl.BlockSpec(memory_space=pltpu.VMEM)],
        out_specs=any_spec, out_shape=jax.ShapeDtypeStruct((out_rows, n), BF16),
        scratch_shapes=[pltpu.VMEM((tile, n), F32), pltpu.VMEM((tile, n), BF16),
                        pltpu.SemaphoreType.DMA((max(len(cut) for cut in cuts),)), pltpu.SemaphoreType.DMA((1,)),
                        pltpu.VMEM((2, tt, tile), BF16), pltpu.SemaphoreType.DMA((2,))],
        compiler_params=_cp(2),
    )(*a_list, b, extra_rows)


BF16_TILE_ROWS = 16


def _shard_offset(index, shard_rows):
    return (index * shard_rows) % BF16_TILE_ROWS


def _padded_shard_rows(shard_rows):
    return -(-(shard_rows + max(_shard_offset(j, shard_rows) for j in range(N_DEV))) // BF16_TILE_ROWS) * BF16_TILE_ROWS


def _pad_shard(wt_shard, index):
    rows, d = wt_shard.shape
    return lax.dynamic_update_slice(jnp.zeros((_padded_shard_rows(rows), d), wt_shard.dtype), wt_shard,
                                    (_shard_offset(index, rows), 0))


def _shard_window(index, shard_rows):
    return index * shard_rows - _shard_offset(index, shard_rows)


def _packed_parts(dm):
    d, dk, hk, hv, cw, nj, hw = dm.D, dm.DK, dm.HK, dm.HV, dm.CW, dm.NJ, dm.HW
    blocks = [(0, (j * 4 + p) * cw, p * d + j * cw, cw) for j in range(nj) for p in range(4)]
    for h in range(HEADS):
        blocks += [(1, h * hw, 4 * d + h * hk, hk), (1, h * hw + hk, 4 * d + dk + h * hk, hk),
                   (1, h * hw + 2 * hk, 5 * d + h * hv, hv), (1, h * hw + 2 * hk + hv, 6 * d + h * hv, hv)]
    blocks += [(2, 0, 7 * d + 2 * RANK, 2 * d), (3, 0, 7 * d, 2 * RANK)]
    return [4 * d, 3 * d, 2 * d, LR_LANES], blocks


def _pack_plan(dm):
    sh = (9 * dm.D + 2 * RANK) // N_DEV
    tile = BF16_TILE_ROWS
    copies, straddles = [], []
    for part, dst, r0, n in _packed_parts(dm)[1]:
        for j in range(N_DEV):
            a, b = max(r0, sh * j), min(r0 + n, sh * (j + 1))
            if a >= b:
                continue
            a_up, b_down = -(-a // tile) * tile, b // tile * tile
            if b_down > a_up:
                copies.append((j, a_up - sh * j + _shard_offset(j, sh), b_down - a_up, part, dst + a_up - r0))
            if a % tile:
                lo = a // tile * tile
                straddles.append((j, lo - sh * (j - 1) + _shard_offset(j - 1, sh), part, dst + lo - r0, a - lo))
    return copies, straddles


def _packed_scratch(dm):
    copies, straddles = _pack_plan(dm)
    return ([pltpu.VMEM((rows, dm.D), BF16) for rows in _packed_parts(dm)[0]]
            + [pltpu.VMEM((2 * max(len(straddles), 1), BF16_TILE_ROWS, dm.D), BF16),
               pltpu.SemaphoreType.DMA((len(copies) + 2 * len(straddles),))])


def _load_packed(g_ref, parts, edges, sems, dm):
    copies, straddles = _pack_plan(dm)
    tile = BF16_TILE_ROWS
    parts[3][2 * RANK:, :] = jnp.zeros((LR_LANES - 2 * RANK, dm.D), BF16)
    dmas = [pltpu.make_async_copy(g_ref.at[j, pl.ds(src, n), :], parts[p].at[pl.ds(dst, n), :], sems.at[i])
            for i, (j, src, n, p, dst) in enumerate(copies)]
    for i, (j, src, p, dst, split) in enumerate(straddles):
        k = len(copies) + 2 * i
        dmas.append(pltpu.make_async_copy(g_ref.at[j - 1, pl.ds(src, tile), :], edges.at[2 * i], sems.at[k]))
        dmas.append(pltpu.make_async_copy(g_ref.at[j, pl.ds(0, tile), :], edges.at[2 * i + 1], sems.at[k + 1]))
    for cp in dmas:
        cp.start()
    for cp in dmas:
        cp.wait()
    row = lax.broadcasted_iota(jnp.int32, (tile, dm.D), 0)
    for i, (j, src, p, dst, split) in enumerate(straddles):
        parts[p][dst:dst + tile, :] = jnp.where(row < split, edges[2 * i], edges[2 * i + 1])


def _inproj(u, gathered, dm):
    t, d = u.shape
    tm = _pick(t, 512, 16)
    widths = _packed_parts(dm)[0]
    cn = 1024

    def body(u_ref, g_ref, *rest):
        outs, parts, (edges, sems) = rest[:4], rest[4:8], rest[8:]

        @pl.when(pl.program_id(0) == 0)
        def _():
            _load_packed(g_ref, parts, edges, sems, dm)

        ut = u_ref[...]
        for w, o_ref in zip(parts, outs):
            n = w.shape[0]
            step = cn if n % cn == 0 else n
            for j in range(0, n, step):
                o_ref[:, j:j + step] = _dot_nt(ut, w[j:j + step, :]).astype(BF16)

    return pl.pallas_call(
        body, name="inproj", grid=(t // tm,),
        in_specs=[pl.BlockSpec((tm, d), lambda i: (i, 0)), pl.BlockSpec(memory_space=pl.ANY)],
        out_specs=[pl.BlockSpec((tm, w), lambda i: (i, 0)) for w in widths],
        out_shape=[jax.ShapeDtypeStruct((t, w), BF16) for w in widths],
        scratch_shapes=_packed_scratch(dm), compiler_params=_cp(1),
    )(u, gathered)


def _conv_rows(dm):
    return _pick(dm.LP, 256, 16)


def _shifted(m, prev_row, next_row, rows):
    row = lax.broadcasted_iota(jnp.int32, m.shape, 0)
    m_prev = jnp.where(row == 0, prev_row, pltpu.roll(m, 1, 0))
    m_next = jnp.where(row == rows - 1, next_row, pltpu.roll(m, rows - 1, 0))
    return m_prev, m_next


def _conv_fwd(proj_a, conv_w, dm):
    lp, cw, rc = dm.LP, dm.CW, _conv_rows(dm)
    nchunk = lp // rc

    def body(p_ref, w_ref, y_ref):
        w0, w1, w2 = w_ref[0:1, :], w_ref[1:2, :], w_ref[2:3, :]

        def chunk(ci, carry):
            r0 = pl.multiple_of(ci * rc, rc)
            blk = p_ref[pl.ds(r0, rc), :].astype(F32)
            cb, cc, cx, cz = (blk[:, i * cw:(i + 1) * cw] for i in range(4))
            m = cc * cx
            rp = pl.multiple_of(jnp.maximum(r0 - 16, 0), 16)
            rn = pl.multiple_of(jnp.minimum(r0 + rc, lp - 16), 16)
            pv = p_ref[pl.ds(rp, 16), cw:3 * cw].astype(F32)
            nx = p_ref[pl.ds(rn, 16), cw:3 * cw].astype(F32)
            prev_row = jnp.where(ci > 0, pv[15:16, :cw] * pv[15:16, cw:], 0.0)
            next_row = jnp.where(ci < nchunk - 1, nx[0:1, :cw] * nx[0:1, cw:], 0.0)
            m_prev, m_next = _shifted(m, prev_row, next_row, rc)
            s = w0 * m_prev + w1 * m + w2 * m_next
            y_ref[pl.ds(r0, rc), :] = (cb * s * (cz * _sigmoid(cz))).astype(BF16)
            return carry

        lax.fori_loop(0, nchunk, chunk, 0)

    return pl.pallas_call(
        body, name="conv_fwd", grid=(dm.Bl, dm.NJ),
        in_specs=[pl.BlockSpec((lp, 4 * cw), lambda s, j: (s, j)), pl.BlockSpec((3, cw), lambda s, j: (0, j))],
        out_specs=pl.BlockSpec((lp, cw), lambda s, j: (s, j)),
        out_shape=jax.ShapeDtypeStruct((dm.T, dm.D), BF16), compiler_params=_cp(2),
    )(proj_a, conv_w)


def _conv_bwd(proj_a, dy_conv, conv_w, dm):
    lp, cw, rc = dm.LP, dm.CW, _conv_rows(dm)
    nchunk = lp // rc

    def body(p_ref, dy_ref, w_ref, d_ref, gw_ref):
        w0, w1, w2 = w_ref[0:1, :], w_ref[1:2, :], w_ref[2:3, :]

        def ds_of(p4, dy):
            cb, cz = p4[:, :cw], p4[:, 3 * cw:]
            return dy * cb * (cz * _sigmoid(cz))

        def chunk(ci, carry):
            g0, g1, g2 = carry
            r0 = pl.multiple_of(ci * rc, rc)
            blk = p_ref[pl.ds(r0, rc), :].astype(F32)
            dy = dy_ref[pl.ds(r0, rc), :].astype(F32)
            cb, cc, cx, cz = (blk[:, i * cw:(i + 1) * cw] for i in range(4))
            rp = pl.multiple_of(jnp.maximum(r0 - 16, 0), 16)
            rn = pl.multiple_of(jnp.minimum(r0 + rc, lp - 16), 16)
            pv = p_ref[pl.ds(rp, 16), :].astype(F32)[15:16]
            nx = p_ref[pl.ds(rn, 16), :].astype(F32)[0:1]
            dpv = dy_ref[pl.ds(rp, 16), :].astype(F32)[15:16]
            dnx = dy_ref[pl.ds(rn, 16), :].astype(F32)[0:1]
            has_prev, has_next = ci > 0, ci < nchunk - 1
            m = cc * cx
            m_prev, m_next = _shifted(m, jnp.where(has_prev, pv[:, cw:2 * cw] * pv[:, 2 * cw:3 * cw], 0.0),
                                      jnp.where(has_next, nx[:, cw:2 * cw] * nx[:, 2 * cw:3 * cw], 0.0), rc)
            s = w0 * m_prev + w1 * m + w2 * m_next
            sg = _sigmoid(cz)
            silu = cz * sg
            ds = dy * cb * silu
            ds_prev, ds_next = _shifted(ds, jnp.where(has_prev, ds_of(pv, dpv), 0.0),
                                        jnp.where(has_next, ds_of(nx, dnx), 0.0), rc)
            dm_ = w0 * ds_next + w1 * ds + w2 * ds_prev
            d_ref[pl.ds(r0, rc), 0:cw] = (dy * s * silu).astype(BF16)
            d_ref[pl.ds(r0, rc), cw:2 * cw] = (dm_ * cx).astype(BF16)
            d_ref[pl.ds(r0, rc), 2 * cw:3 * cw] = (dm_ * cc).astype(BF16)
            d_ref[pl.ds(r0, rc), 3 * cw:4 * cw] = (dy * cb * s * (sg * (1.0 + cz * (1.0 - sg)))).astype(BF16)
            return (g0 + jnp.sum(ds * m_prev, axis=0, keepdims=True), g1 + jnp.sum(ds * m, axis=0, keepdims=True),
                    g2 + jnp.sum(ds * m_next, axis=0, keepdims=True))

        z = jnp.zeros((1, cw), F32)
        g0, g1, g2 = lax.fori_loop(0, nchunk, chunk, (z, z, z))

        @pl.when(pl.program_id(1) == 0)
        def _():
            gw_ref[...] = jnp.zeros_like(gw_ref)

        gw_ref[0:1, :] += g0
        gw_ref[1:2, :] += g1
        gw_ref[2:3, :] += g2

    return pl.pallas_call(
        body, name="conv_bwd", grid=(dm.NJ, dm.Bl),
        in_specs=[pl.BlockSpec((lp, 4 * cw), lambda j, s: (s, j)), pl.BlockSpec((lp, cw), lambda j, s: (s, j)),
                  pl.BlockSpec((3, cw), lambda j, s: (0, j))],
        out_specs=[pl.BlockSpec((lp, 4 * cw), lambda j, s: (s, j)), pl.BlockSpec((8, cw), lambda j, s: (0, j))],
        out_shape=[jax.ShapeDtypeStruct((dm.T, 4 * dm.D), BF16), jax.ShapeDtypeStruct((8, dm.D), F32)],
        compiler_params=_cp(2),
    )(proj_a, dy_conv, conv_w)


def _interleave(gens):
    results = [None] * len(gens)
    live = list(range(len(gens)))
    while live:
        for idx in list(live):
            try:
                next(gens[idx])
            except StopIteration as done:
                results[idx] = done.value
                live.remove(idx)
    return results


def _group_chunks(dm):
    n = dm.NC - dm.C0
    return 3 if n % 3 == 0 else 1


def _group_masks(rows):
    ii = lax.broadcasted_iota(jnp.int32, (rows, rows), 0)
    jj = lax.broadcasted_iota(jnp.int32, (rows, rows), 1)
    same = jnp.right_shift(ii, CHUNK_SHIFT) == jnp.right_shift(jj, CHUNK_SHIFT)
    return same & (jj <= ii), same & (jj > ii)


def _first_row(chunk):
    return chunk * CHUNK if isinstance(chunk, int) else pl.multiple_of(chunk * CHUNK, CHUNK)


def _chunk_totals(b, fwd):
    hk = b.shape[1]
    rows = [b[c * CHUNK + CHUNK - 1:(c + 1) * CHUNK] if fwd else b[c * CHUNK:c * CHUNK + 1]
            for c in range(b.shape[0] // CHUNK)]
    return jnp.concatenate([jnp.broadcast_to(r, (CHUNK, hk)) for r in rows], axis=0)


def _log_gate(lr_rows, w_ref, b_ref, first_group, hk):
    z = _dot(lr_rows, w_ref[...]) + b_ref[...]
    e = jnp.exp(-jnp.abs(z))
    g = (jnp.minimum(z, 0.0) - jnp.log(1.0 + e)) * (1.0 / GATE_NORMALIZER)
    dg_dz = jnp.where(z >= 0.0, e, 1.0) / (1.0 + e) * (1.0 / GATE_NORMALIZER)
    row = lax.broadcasted_iota(jnp.int32, (lr_rows.shape[0], hk), 0)
    pad = first_group & (row < PAD_ROWS)
    return jnp.where(pad, 0.0, g), jnp.where(pad, 0.0, dg_dz)


def _gla_fwd(proj_b, lr, wg_f, bg_f, wg_b, bg_b, gla_g, dm):
    lp, hk, hv, nc, c0, hw = dm.LP, dm.HK, dm.HV, dm.NC, dm.C0, dm.HW
    scale = hk ** -0.5
    gc = _group_chunks(dm)
    gr, ng = gc * CHUNK, (nc - c0) // gc

    def body(p_ref, lr_ref, wf_ref, bf_ref, wb_ref, bb_ref, gg_ref, o_ref, y_ref, st_ref, b_out, gs_out, oacc_f, oacc_b):
        low_incl, up_strict = _group_masks(gr)
        if c0 > 0:
            zr = c0 * CHUNK
            o_ref[0:zr, :] = jnp.zeros((zr, hv), BF16)
            y_ref[0:zr, :] = jnp.zeros((zr, hv), BF16)
            b_out[:, 0:zr, :] = jnp.zeros((2, zr, hk), F32)
            gs_out[:, 0:zr, :] = jnp.zeros((2, zr, hk), F32)
            st_ref[0, 0, :, 0:c0] = jnp.zeros((2, c0, hv, hk), BF16)

        def decay(gi, fwd):
            w_ref, b_ref = (wf_ref, bf_ref) if fwd else (wb_ref, bb_ref)
            r0 = _first_row(c0 + gi * gc)
            yield
            g, dg_dz = _log_gate(lr_ref[pl.ds(r0, gr), :], w_ref, b_ref, gi == 0, hk)
            gs_out[0 if fwd else 1, pl.ds(r0, gr), :] = dg_dz
            yield
            b = _chunk_cumsum(g, not fwd)
            b_out[0 if fwd else 1, pl.ds(r0, gr), :] = b
            return b

        def group(gi, st, b, fwd):
            oacc = oacc_f if fwd else oacc_b
            r0 = pl.multiple_of((c0 + gi * gc) * CHUNK, CHUNK)
            blk = p_ref[pl.ds(r0, gr), :]
            q = blk[:, :hk].astype(F32) * scale
            k = blk[:, hk:2 * hk].astype(F32)
            v = blk[:, 2 * hk:2 * hk + hv]
            btot = _chunk_totals(b, fwd)
            qi = (q * jnp.exp(b)).astype(BF16)
            ki = (k * jnp.exp(-b)).astype(BF16)
            kd = (k * jnp.exp(btot - b)).astype(BF16)
            dec = jnp.exp(btot)
            a = _dot_nt(qi, ki)
            yield
            o = _dot(jnp.where(low_incl if fwd else up_strict, a, 0.0).astype(BF16), v)
            chunk_rows = [slice(c * CHUNK, (c + 1) * CHUNK) for c in range(gc)]
            kv = [_dot_tn(v[rows], kd[rows]) for rows in chunk_rows]
            for c in (range(gc) if fwd else reversed(range(gc))):
                yield
                rows = chunk_rows[c]
                st_b = st.astype(BF16)
                st_ref[0, 0, 0 if fwd else 1, c0 + gi * gc + c] = st_b
                oacc[pl.ds(r0 + c * CHUNK, CHUNK), :] = o[rows] + _dot_nt(qi[rows], st_b)
                st = st * dec[c * CHUNK:c * CHUNK + 1] + kv[c]
            return st

        def step(i, carry):
            st_f, st_b, b_f, b_b = carry
            gf, gb = i, ng - 1 - i
            return tuple(_interleave([group(gf, st_f, b_f, True), group(gb, st_b, b_b, False),
                                      decay(jnp.minimum(gf + 1, ng - 1), True), decay(jnp.maximum(gb - 1, 0), False)]))

        zero = jnp.zeros((hv, hk), F32)
        lax.fori_loop(0, ng, step, (zero, zero, *_interleave([decay(0, True), decay(ng - 1, False)])))

        def finish(i, carry):
            r0 = pl.multiple_of((c0 + i * gc) * CHUNK, CHUNK)
            o = oacc_f[pl.ds(r0, gr), :] + oacc_b[pl.ds(r0, gr), :]
            r = p_ref[pl.ds(r0, gr), 2 * hk + hv:].astype(F32)
            on = o * lax.rsqrt(jnp.mean(o * o, axis=-1, keepdims=True) + EPS) * gg_ref[...]
            o_ref[pl.ds(r0, gr), :] = o.astype(BF16)
            y_ref[pl.ds(r0, gr), :] = (on * r * _sigmoid(r)).astype(BF16)
            return carry

        lax.fori_loop(0, ng, finish, 0)

    head = lambda s, h: (s, h)
    wspec = pl.BlockSpec((LR_LANES, hk), lambda s, h: (0, h))
    bspec = pl.BlockSpec((1, hk), lambda s, h: (0, h))
    return pl.pallas_call(
        body, name="gla_fwd", grid=(dm.Bl, HEADS),
        in_specs=[pl.BlockSpec((lp, hw), head), pl.BlockSpec((lp, LR_LANES), lambda s, h: (s, 0)),
                  wspec, bspec, wspec, bspec, pl.BlockSpec((1, hv), lambda s, h: (0, 0))],
        out_specs=[pl.BlockSpec((lp, hv), head), pl.BlockSpec((lp, hv), head),
                   pl.BlockSpec((1, 1, 2, nc, hv, hk), lambda s, h: (s, h, 0, 0, 0, 0)),
                   pl.BlockSpec((2, lp, hk), lambda s, h: (0, s, h)), pl.BlockSpec((2, lp, hk), lambda s, h: (0, s, h))],
        out_shape=[jax.ShapeDtypeStruct((dm.T, dm.DV), BF16), jax.ShapeDtypeStruct((dm.T, dm.DV), BF16),
                   jax.ShapeDtypeStruct((dm.Bl, HEADS, 2, nc, hv, hk), BF16),
                   jax.ShapeDtypeStruct((2, dm.T, dm.DK), F32), jax.ShapeDtypeStruct((2, dm.T, dm.DK), F32)],
        scratch_shapes=[pltpu.VMEM((lp, hv), F32), pltpu.VMEM((lp, hv), F32)],
        compiler_params=_cp(2),
    )(proj_b, lr, wg_f, bg_f, wg_b, bg_b, gla_g)


def _gla_bwd(proj_b, lr, o_all, dy_gla, states, decays, gate_slopes, wg_f, wg_b, gla_g, dm):
    lp, hk, hv, nc, c0, hw = dm.LP, dm.HK, dm.HV, dm.NC, dm.C0, dm.HW
    scale = hk ** -0.5
    gc = _group_chunks(dm)
    gr, ng = gc * CHUNK, (nc - c0) // gc

    def body(p_ref, lr_ref, o_ref, dy_ref, st_ref, b_ref, gs_ref, wf_ref, wb_ref, gg_ref,
             d_ref, dlr_ref, gwf_ref, gbf_ref, gwb_ref, gbb_ref, ggg_ref, do_s, dq_s, dk_s, dv_s, dz_s):
        low_incl, up_strict = _group_masks(gr)
        h = pl.program_id(1)

        @pl.when(h == 0)
        def _():
            dlr_ref[...] = jnp.zeros_like(dlr_ref)

        if c0 > 0:
            zr = c0 * CHUNK
            d_ref[0:zr, :] = jnp.zeros((zr, hw), BF16)
        for acc in (dq_s, dk_s, dv_s):
            acc[...] = jnp.zeros_like(acc)

        def norm_bwd(i, ggg):
            r0 = pl.multiple_of((c0 + i * gc) * CHUNK, CHUNK)
            o = o_ref[pl.ds(r0, gr), :].astype(F32)
            dy = dy_ref[pl.ds(r0, gr), :].astype(F32)
            r = p_ref[pl.ds(r0, gr), 2 * hk + hv:].astype(F32)
            rstd = lax.rsqrt(jnp.mean(o * o, axis=-1, keepdims=True) + EPS)
            ohat = o * rstd
            sg = _sigmoid(r)
            d_on = dy * (r * sg)
            d_ref[pl.ds(r0, gr), 2 * hk + hv:] = (dy * ohat * gg_ref[...] * (sg * (1.0 + r * (1.0 - sg)))).astype(BF16)
            d_oh = d_on * gg_ref[...]
            do_s[pl.ds(r0, gr), :] = (rstd * (d_oh - ohat * jnp.mean(d_oh * ohat, axis=-1, keepdims=True))).astype(BF16)
            return ggg + jnp.sum(d_on * ohat, axis=0, keepdims=True)

        ggg = lax.fori_loop(0, ng, norm_bwd, jnp.zeros((1, hv), F32))

        @pl.when((pl.program_id(0) == 0) & (h == 0))
        def _():
            ggg_ref[...] = jnp.zeros_like(ggg_ref)

        ggg_ref[0:1, :] += ggg

        def load(gi):
            r0 = pl.multiple_of((c0 + gi * gc) * CHUNK, CHUNK)
            blk = p_ref[pl.ds(r0, gr), :]
            return r0, blk[:, :hk].astype(F32) * scale, blk[:, hk:2 * hk].astype(F32), blk[:, 2 * hk:2 * hk + hv]

        zero = jnp.zeros((hv, hk), F32)

        def grad(gi, carry, fwd):
            dst, gb = carry
            way = 0 if fwd else 1
            mask = low_incl if fwd else up_strict
            r0, q, k, v = load(gi)
            b = b_ref[way, pl.ds(r0, gr), :]
            btot = _chunk_totals(b, fwd)
            eb, enb, edb, dec = jnp.exp(b), jnp.exp(-b), jnp.exp(btot - b), jnp.exp(btot)
            qi_f, ki_f, kd_f = q * eb, k * enb, k * edb
            qi, ki, kd = qi_f.astype(BF16), ki_f.astype(BF16), kd_f.astype(BF16)
            do = do_s[pl.ds(r0, gr), :]
            a = _dot_nt(qi, ki)
            da = _dot_nt(do, v)
            yield
            a = jnp.where(mask, a, 0.0).astype(BF16)
            da = jnp.where(mask, da, 0.0).astype(BF16)
            dv = _dot_tn(a, do)
            dqi = _dot(da, ki)
            dki = _dot_tn(da, qi)
            dv_c, dqi_c, dkd_c, extra_c = [None] * gc, [None] * gc, [None] * gc, [None] * gc
            chunk_rows = [slice(c * CHUNK, (c + 1) * CHUNK) for c in range(gc)]
            qdo = [_dot_tn(do[rows], qi[rows]) for rows in chunk_rows]
            for c in (reversed(range(gc)) if fwd else range(gc)):
                yield
                rows = chunk_rows[c]
                st = st_ref[0, 0, way, c0 + gi * gc + c]
                dsn_b = dst.astype(BF16)
                dec_c = dec[c * CHUNK:c * CHUNK + 1]
                dv_c[c] = dv[rows] + _dot_nt(kd[rows], dsn_b)
                dqi_c[c] = dqi[rows] + _dot(do[rows], st)
                dkd_c[c] = _dot(v[rows], dsn_b)
                ddec = jnp.sum(st.astype(F32) * dst, axis=0, keepdims=True)
                extra = jnp.sum(dkd_c[c] * kd_f[rows], axis=0, keepdims=True) + ddec * dec_c
                extra_c[c] = jnp.broadcast_to(extra, (CHUNK, hk))
                dst = dst * dec_c + qdo[c]
            yield
            dv, dqi = jnp.concatenate(dv_c, axis=0), jnp.concatenate(dqi_c, axis=0)
            dkd, extra = jnp.concatenate(dkd_c, axis=0), jnp.concatenate(extra_c, axis=0)
            dq_s[pl.ds(r0, gr), :] += dqi * eb * scale
            dk_s[pl.ds(r0, gr), :] += dki * enb + dkd * edb
            dv_s[pl.ds(r0, gr), :] += dv
            db = dqi * qi_f - dki * ki_f - dkd * kd_f
            dg = _chunk_cumsum(db, fwd) + extra
            yield
            dz = dg * gs_ref[way, pl.ds(r0, gr), :]
            dz_s[way, pl.ds(r0, gr), :] = dz.astype(BF16)
            return dst, gb + jnp.sum(dz, axis=0, keepdims=True)

        def grad_step(i, carry):
            return tuple(_interleave([grad(ng - 1 - i, carry[0], True), grad(i, carry[1], False)]))

        init = (zero, jnp.zeros((1, hk), F32))
        (_, gb_f), (_, gb_b) = lax.fori_loop(0, ng, grad_step, (init, init))
        used = slice(c0 * CHUNK, lp)
        for way, (w_ref, gw_ref, gb_ref, gb) in enumerate(((wf_ref, gwf_ref, gbf_ref, gb_f), (wb_ref, gwb_ref, gbb_ref, gb_b))):
            dlr_ref[used, :] += _dot_nt(dz_s[way, used, :], w_ref[...])
            gw_ref[0] = _dot_tn(lr_ref[used, :], dz_s[way, used, :])
            gb_ref[0] = jnp.zeros((8, hk), F32)
            gb_ref[0, 0:1, :] = gb

        def combine(i, carry):
            r0 = pl.multiple_of((c0 + i * gc) * CHUNK, CHUNK)
            d_ref[pl.ds(r0, gr), 0:hk] = dq_s[pl.ds(r0, gr), :].astype(BF16)
            d_ref[pl.ds(r0, gr), hk:2 * hk] = dk_s[pl.ds(r0, gr), :].astype(BF16)
            d_ref[pl.ds(r0, gr), 2 * hk:2 * hk + hv] = dv_s[pl.ds(r0, gr), :].astype(BF16)
            return carry

        lax.fori_loop(0, ng, combine, 0)

    head = lambda s, h: (s, h)
    wspec = pl.BlockSpec((LR_LANES, hk), lambda s, h: (0, h))
    gwspec = pl.BlockSpec((1, LR_LANES, hk), lambda s, h: (s, 0, h))
    gbspec = pl.BlockSpec((1, 8, hk), lambda s, h: (s, 0, h))
    gw_shape = jax.ShapeDtypeStruct((dm.Bl, LR_LANES, dm.DK), F32)
    gb_shape = jax.ShapeDtypeStruct((dm.Bl, 8, dm.DK), F32)
    both = pl.BlockSpec((2, lp, hk), lambda s, h: (0, s, h))
    return pl.pallas_call(
        body, name="gla_bwd", grid=(dm.Bl, HEADS),
        in_specs=[pl.BlockSpec((lp, hw), head), pl.BlockSpec((lp, LR_LANES), lambda s, h: (s, 0)),
                  pl.BlockSpec((lp, hv), head), pl.BlockSpec((lp, hv), head),
                  pl.BlockSpec((1, 1, 2, nc, hv, hk), lambda s, h: (s, h, 0, 0, 0, 0)), both, both,
                  wspec, wspec, pl.BlockSpec((1, hv), lambda s, h: (0, 0))],
        out_specs=[pl.BlockSpec((lp, hw), head), pl.BlockSpec((lp, LR_LANES), lambda s, h: (s, 0)),
                   gwspec, gbspec, gwspec, gbspec, pl.BlockSpec((8, hv), lambda s, h: (0, 0))],
        out_shape=[jax.ShapeDtypeStruct((dm.T, HEADS * hw), BF16), jax.ShapeDtypeStruct((dm.T, LR_LANES), F32),
                   gw_shape, gb_shape, gw_shape, gb_shape, jax.ShapeDtypeStruct((8, hv), F32)],
        scratch_shapes=[pltpu.VMEM((lp, hv), BF16), pltpu.VMEM((lp, hk), F32), pltpu.VMEM((lp, hk), F32),
                        pltpu.VMEM((lp, hv), F32), pltpu.VMEM((2, lp, hk), BF16)],
        compiler_params=_cp(2),
    )(proj_b, lr, o_all, dy_gla, states, decays, gate_slopes, wg_f, wg_b, gla_g)


def _stream_tiles(n_tiles, loads, stores, compute):
    for cp in loads(0, 0):
        cp.start()

    def step(t, carry):
        slot = t % 2

        @pl.when(t + 1 < n_tiles)
        def _():
            for cp in loads(t + 1, 1 - slot):
                cp.start()

        for cp in loads(t, slot):
            cp.wait()

        @pl.when(t >= 2)
        def _():
            for cp in stores(t - 2, slot):
                cp.wait()

        compute(t, slot)
        for cp in stores(t, slot):
            cp.start()
        return carry

    lax.fori_loop(0, n_tiles, step, 0)
    for t in range(max(n_tiles - 2, 0), n_tiles):
        for cp in stores(t, t % 2):
            cp.wait()


def _token_tiles(dm, target_rows=512):
    rows = _pick(dm.S, target_rows, 16)
    per_seq = dm.S // rows
    return rows, dm.Bl * per_seq, lambda t: pl.multiple_of((t // per_seq) * dm.LP + dm.TM + (t % per_seq) * rows, 16)


def _head(y_conv, y_gla, proj_c, w_oc, w_og, w_out, x, target, g_post, dm):
    d, tm = dm.D, dm.TM
    rows, n_tiles, first_row = _token_tiles(dm, 512)
    parts = 2 if rows % (2 * BF16_TILE_ROWS) == 0 else 1
    n_out = 8

    def body(*refs):
        yc_hbm, yg_hbm, c_hbm, woc_ref, wog_ref, wo_ref, x_hbm, t_hbm, g_ref = refs[:9]
        outs, st_ref = refs[9:9 + n_out], refs[9 + n_out]
        ycbuf, ygbuf, cbuf, xbuf, tbuf = refs[10 + n_out:15 + n_out]
        obufs = refs[15 + n_out:15 + 2 * n_out]
        zbuf, zbuf2, sem_in, sem_out, sem_zero = refs[15 + 2 * n_out:]

        def loads(t, slot):
            padded = [(yc_hbm, ycbuf), (yg_hbm, ygbuf), (c_hbm, cbuf)]
            own = [(x_hbm, xbuf), (t_hbm, tbuf)]
            return ([pltpu.make_async_copy(h.at[pl.ds(first_row(t), rows), :], b.at[slot], sem_in.at[i, slot])
                     for i, (h, b) in enumerate(padded)] +
                    [pltpu.make_async_copy(h.at[pl.ds(t * rows, rows), :], b.at[slot], sem_in.at[3 + i, slot])
                     for i, (h, b) in enumerate(own)])

        def stores(t, slot):
            return [pltpu.make_async_copy(b.at[slot], h.at[pl.ds(first_row(t), rows), :], sem_out.at[i, slot])
                    for i, (h, b) in enumerate(zip(outs, obufs))]

        def chain(slot, part):
            mg_o, do_o, dy_o, dpc_o, dpg_o, dc_o, dyc_o, dyg_o = obufs
            pc = _dot(ycbuf[slot, part], woc_ref[...])
            pg = _dot(ygbuf[slot, part], wog_ref[...])
            yield
            sa = _sigmoid(cbuf[slot, part, :d].astype(F32))
            sb = _sigmoid(cbuf[slot, part, d:].astype(F32))
            merged = (sa * pc + sb * pg).astype(BF16)
            mg_o[slot, part] = merged
            out = _dot(merged, wo_ref[...])
            yield
            rstd = lax.rsqrt(jnp.mean(out * out, axis=-1, keepdims=True) + EPS)
            ohat = out * rstd
            err = xbuf[slot, part] + ohat * g_ref[...] - tbuf[slot, part]
            dy = err * (1.0 / d)
            d_oh = dy * g_ref[...]
            d_out = (rstd * (d_oh - ohat * jnp.mean(d_oh * ohat, axis=-1, keepdims=True))).astype(BF16)
            do_o[slot, part] = d_out
            dy_o[slot, part] = dy.astype(BF16)
            st_ref[0:1, :] += jnp.sum(dy * ohat, axis=0, keepdims=True)
            st_ref[1:2, :] += jnp.sum(err * err, axis=0, keepdims=True)
            dmg = _dot_nt(d_out, wo_ref[...])
            yield
            dpc = (dmg * sa).astype(BF16)
            dpg = (dmg * sb).astype(BF16)
            dpc_o[slot, part] = dpc
            dpg_o[slot, part] = dpg
            dc_o[slot, part, :d] = (dmg * pc * sa * (1.0 - sa)).astype(BF16)
            dc_o[slot, part, d:] = (dmg * pg * sb * (1.0 - sb)).astype(BF16)
            dyc_o[slot, part] = _dot_nt(dpc, woc_ref[...]).astype(BF16)
            dyg_o[slot, part] = _dot_nt(dpg, wog_ref[...]).astype(BF16)

        def compute(t, slot):
            _interleave([chain(slot, pl.ds(i * (rows // parts), rows // parts)) for i in range(parts)])

        st_ref[...] = jnp.zeros_like(st_ref)
        zbuf[...] = jnp.zeros_like(zbuf)
        zbuf2[...] = jnp.zeros_like(zbuf2)
        zeros = [pltpu.make_async_copy(zbuf2 if out.shape[1] == 2 * d else zbuf, out.at[pl.ds(b * dm.LP, tm), :], sem_zero.at[i, b])
                 for i, out in enumerate(outs) for b in range(dm.Bl)]
        for cp in zeros:
            cp.start()
        _stream_tiles(n_tiles, loads, stores, compute)
        for cp in zeros:
            cp.wait()

    any_spec, vmem = pl.BlockSpec(memory_space=pl.ANY), pl.BlockSpec(memory_space=pltpu.VMEM)
    widths = [d, d, d, d, d, 2 * d, d, d]
    tile = lambda w, dt: pltpu.VMEM((2, rows, w), dt)
    return pl.pallas_call(
        body, name="head", in_specs=[any_spec] * 3 + [vmem] * 3 + [any_spec] * 2 + [vmem],
        out_specs=[any_spec] * n_out + [vmem],
        out_shape=[jax.ShapeDtypeStruct((dm.T, w), BF16) for w in widths] + [jax.ShapeDtypeStruct((8, d), F32)],
        scratch_shapes=[tile(d, BF16), tile(d, BF16), tile(2 * d, BF16), tile(d, F32), tile(d, F32)]
        + [tile(w, BF16) for w in widths]
        + [pltpu.VMEM((tm, d), BF16), pltpu.VMEM((tm, 2 * d), BF16), pltpu.SemaphoreType.DMA((5, 2)),
           pltpu.SemaphoreType.DMA((n_out, 2)), pltpu.SemaphoreType.DMA((n_out, dm.Bl))],
        compiler_params=pltpu.CompilerParams(vmem_limit_bytes=VMEM_LIMIT_BYTES),
    )(y_conv, y_gla, proj_c, w_oc, w_og, w_out, x.reshape(dm.Bl * dm.S, d), target.reshape(dm.Bl * dm.S, d), g_post)


def _grad_h(d_parts, gathered, dy, x, metapad, g_pre, dm):
    d, tm = dm.D, dm.TM
    rows, n_tiles, first_row = _token_tiles(dm, 256)
    widths = [a.shape[1] for a in d_parts]
    np_ = len(d_parts)

    def body(*refs):
        d_hbm, g_hbm, dy_hbm, x_hbm, mp_ref, g_ref = refs[:np_], refs[np_], refs[np_ + 1], refs[np_ + 2], refs[np_ + 3], refs[np_ + 4]
        gx_hbm, dmeta_ref, gg_ref = refs[np_ + 5:np_ + 8]
        parts, edges, sems = refs[np_ + 8:np_ + 12], refs[np_ + 12], refs[np_ + 13]
        dbufs = refs[np_ + 14:2 * np_ + 14]
        dybuf, xbuf, gbuf = refs[2 * np_ + 14:2 * np_ + 17]
        mbufs = refs[2 * np_ + 17:3 * np_ + 17]
        sem_in, sem_out, sem_meta = refs[3 * np_ + 17:]

        def grad_u(tiles):
            du = _dot(tiles[0].astype(BF16), parts[0][...])
            for a, w in zip(tiles[1:], parts[1:]):
                du = du + _dot(a.astype(BF16), w[...])
            return du

        def norm_bwd(h, du, dy):
            rstd = lax.rsqrt(jnp.mean(h * h, axis=-1, keepdims=True) + EPS)
            hhat = h * rstd
            dug = du * g_ref[...]
            gg_ref[0:1, :] += jnp.sum(du * hhat, axis=0, keepdims=True)
            return dy + rstd * (dug - hhat * jnp.mean(dug * hhat, axis=-1, keepdims=True))

        def loads(t, slot):
            padded = list(zip(d_hbm, dbufs)) + [(dy_hbm, dybuf)]
            return ([pltpu.make_async_copy(h.at[pl.ds(first_row(t), rows), :], b.at[slot], sem_in.at[i, slot])
                     for i, (h, b) in enumerate(padded)] +
                    [pltpu.make_async_copy(x_hbm.at[pl.ds(t * rows, rows), :], xbuf.at[slot], sem_in.at[np_ + 1, slot])])

        def stores(t, slot):
            return [pltpu.make_async_copy(gbuf.at[slot], gx_hbm.at[pl.ds(t * rows, rows), :], sem_out.at[slot])]

        def compute(t, slot):
            gbuf[slot] = norm_bwd(xbuf[slot], grad_u([b[slot] for b in dbufs]), dybuf[slot].astype(F32))

        gg_ref[...] = jnp.zeros_like(gg_ref)
        meta = [pltpu.make_async_copy(h.at[pl.ds(b * dm.LP, tm), :], buf.at[pl.ds(b * tm, tm), :], sem_meta.at[i, b])
                for i, (h, buf) in enumerate(zip(d_hbm, mbufs)) for b in range(dm.Bl)]
        for cp in meta:
            cp.start()
        _load_packed(g_hbm, parts, edges, sems, dm)
        _stream_tiles(n_tiles, loads, stores, compute)
        for cp in meta:
            cp.wait()
        dmeta_ref[...] = norm_bwd(jnp.concatenate([mp_ref[...]] * dm.Bl, axis=0), grad_u([buf[...] for buf in mbufs]), 0.0)

    any_spec, vmem = pl.BlockSpec(memory_space=pl.ANY), pl.BlockSpec(memory_space=pltpu.VMEM)
    grad_x, d_meta, gg = pl.pallas_call(
        body, name="grad_h", in_specs=[any_spec] * (np_ + 3) + [vmem, vmem], out_specs=[any_spec, vmem, vmem],
        out_shape=[jax.ShapeDtypeStruct((dm.Bl * dm.S, d), F32), jax.ShapeDtypeStruct((dm.Bl * tm, d), F32),
                   jax.ShapeDtypeStruct((8, d), F32)],
        scratch_shapes=_packed_scratch(dm)
        + [pltpu.VMEM((2, rows, w), a.dtype) for w, a in zip(widths, d_parts)]
        + [pltpu.VMEM((2, rows, d), BF16), pltpu.VMEM((2, rows, d), F32), pltpu.VMEM((2, rows, d), F32)]
        + [pltpu.VMEM((dm.Bl * tm, w), a.dtype) for w, a in zip(widths, d_parts)]
        + [pltpu.SemaphoreType.DMA((np_ + 2, 2)), pltpu.SemaphoreType.DMA((2,)), pltpu.SemaphoreType.DMA((np_, dm.Bl))],
        compiler_params=pltpu.CompilerParams(vmem_limit_bytes=VMEM_LIMIT_BYTES),
    )(*d_parts, gathered, dy, x.reshape(dm.Bl * dm.S, d), metapad, g_pre)
    return grad_x.reshape(dm.Bl, dm.S, d), d_meta.reshape(dm.Bl, tm, d), gg


def _adamw(partials, w, m, v, name, by_columns=False):
    rows_apart = w.ndim == 3
    r, c = w.shape[0], w.shape[-1]
    n_parts, pr = partials.shape[:2]
    assert pr == r or (by_columns and pr == _padded_shard_rows(r))
    assert by_columns or not rows_apart
    tr, tc = (r, _pick(c, 128, 128)) if by_columns else (_pick(r, 256, 16), c)

    def body(p_ref, w_ref, m_ref, v_ref, g_ref, d_ref, nm_ref, nv_ref):
        g = p_ref[0].astype(F32)
        for j in range(1, n_parts):
            g = g + p_ref[j].astype(F32)

        flat = lambda ref: ref.reshape(tr, tc) if rows_apart else ref

        def step(g):
            results = (g,) + _adam_step(g, *[flat(ref)[...] for ref in (w_ref, m_ref, v_ref)])
            for ref, val in zip((g_ref, d_ref, nm_ref, nv_ref), results):
                flat(ref)[...] = val

        if pr == r:
            step(g)
        else:
            me = 4 * lax.axis_index("x") + 2 * lax.axis_index("y") + lax.axis_index("c")
            for offset in sorted({_shard_offset(j, r) for j in range(N_DEV)}):
                @pl.when(_shard_offset(me, r) == offset)
                def _(offset=offset):
                    step(g[offset:offset + r])

    at = (lambda i: (0, i)) if by_columns else (lambda i: (i, 0))
    tile = pl.BlockSpec((tr, 1, tc), lambda i: (0, 0, i)) if rows_apart else pl.BlockSpec((tr, tc), at)
    out = jax.ShapeDtypeStruct(w.shape, F32)
    return pl.pallas_call(
        body, name=name, grid=(c // tc if by_columns else r // tr,),
        in_specs=[pl.BlockSpec((n_parts, pr if by_columns else tr, tc), lambda i: (0,) + at(i)), tile, tile, tile],
        out_specs=[tile, tile, tile, tile], out_shape=[out, out, out, out], compiler_params=_cp(1),
    )(partials, w, m, v)


def _adam_step(g, w, m, v):
    m2 = ADAM_B1 * m + (1.0 - ADAM_B1) * g
    v2 = ADAM_B2 * v + (1.0 - ADAM_B2) * (g * g)
    m_hat = m2 / (1.0 - ADAM_B1 ** ADAM_STEP)
    v_hat = v2 / (1.0 - ADAM_B2 ** ADAM_STEP)
    return -ADAM_LR * (m_hat / (jnp.sqrt(v_hat) + ADAM_EPS) + ADAM_WD * w), m2, v2


def _adamw_small(items, name):
    n = len(items)

    def body(*refs):
        ins, outs = refs[:4 * n], refs[4 * n:]
        for i in range(n):
            p_ref, w_ref, m_ref, v_ref = ins[4 * i:4 * i + 4]
            g = p_ref[0]
            for j in range(1, p_ref.shape[0]):
                g = g + p_ref[j]
            delta, m2, v2 = _adam_step(g, w_ref[...], m_ref[...], v_ref[...])
            for o_ref, val in zip(outs[4 * i:4 * i + 4], (g, delta, m2, v2)):
                o_ref[...] = val

    vmem = pl.BlockSpec(memory_space=pltpu.VMEM)
    res = pl.pallas_call(
        body, name=name, in_specs=[vmem] * (4 * n), out_specs=[vmem] * (4 * n),
        out_shape=[jax.ShapeDtypeStruct(w.shape, F32) for _, w, _, _ in items for _ in range(4)],
    )(*[a for item in items for a in item])
    return [res[4 * i:4 * i + 4] for i in range(n)]


def _unpack_moves(dm):
    d, hk, hv, cw, nj, hw = dm.D, dm.HK, dm.HV, dm.CW, dm.NJ, dm.HW
    moves = [((4 * j + part) * cw, cw, part * d + j * cw) for j in range(nj) for part in range(4)]
    q0 = 4 * d
    k0, v0 = q0 + HEADS * hk, q0 + 2 * HEADS * hk
    r0 = v0 + HEADS * hv
    lr0 = r0 + HEADS * hv
    for h in range(HEADS):
        b0 = 4 * d + h * hw
        moves += [(b0, hk, q0 + h * hk), (b0 + hk, hk, k0 + h * hk), (b0 + 2 * hk, hv, v0 + h * hv),
                  (b0 + 2 * hk + hv, hv, r0 + h * hv)]
    moves.append((4 * d + HEADS * hw, 2 * d, lr0 + 2 * RANK))
    return moves, lr0


def _column_shards(g, shard_shape):
    r, c = g.shape
    return g.reshape(r, N_DEV, c // N_DEV).transpose(1, 0, 2).reshape((N_DEV,) + tuple(shard_shape))


def _join_column_shards(parts):
    r, c = parts.shape[-2:]
    return parts.reshape(N_DEV, r, c).transpose(1, 0, 2).reshape(r, N_DEV * c)


def _local_step(x, target, meta, g_pre, u, wt_shards, conv_w, wg_f, bg_f, wg_b, bg_b, gla_g, out_weights, g_post,
                on_matrix_grads=None):
    bl, s, d = x.shape
    dm = _Dims(bl, s, d)
    metapad = jnp.concatenate([jnp.zeros((dm.TM - N_META, d), F32), meta], axis=0)
    wgp_f = jnp.pad(wg_f, ((0, LR_LANES - RANK), (0, 0))).astype(BF16)
    wgp_b = jnp.pad(wg_b, ((RANK, LR_LANES - 2 * RANK), (0, 0))).astype(BF16)

    u = _prenorm_meta(u, metapad, g_pre, dm)
    proj_a, proj_b, proj_c, lr = _inproj(u, wt_shards, dm)
    y_conv = _conv_fwd(proj_a, conv_w, dm)
    o_all, y_gla, states, decays, gate_slopes = _gla_fwd(proj_b, lr, wgp_f, bg_f, wgp_b, bg_b, gla_g, dm)
    w_oc, w_og, w_out = out_weights(y_conv) if callable(out_weights) else out_weights
    merged, d_out, dy, d_pc, d_pg, d_c, dy_conv, dy_gla, stats = _head(y_conv, y_gla, proj_c, w_oc, w_og, w_out, x, target,
                                                                        g_post, dm)

    g_out = _matmul_tn(merged, d_out, BF16, "grad_w_out")
    g_oc = _matmul_tn(y_conv, d_pc, BF16, "grad_w_out_conv")
    g_og = _matmul_tn(y_gla, d_pg, BF16, "grad_w_out_gla")
    if on_matrix_grads is not None:
        conv_w = conv_w + on_matrix_grads(dict(w_out_conv=g_oc, w_out_gla=g_og, w_merge_out=g_out))
    d_a, g_conv = _conv_bwd(proj_a, dy_conv, conv_w, dm)
    d_b, d_lr, gwp_f, gbp_f, gwp_b, gbp_b, g_gla = _gla_bwd(proj_b, lr, o_all, dy_gla, states, decays, gate_slopes, wgp_f, wgp_b, gla_g, dm)
    moves, lr_at = _unpack_moves(dm)
    g_lr = _matmul_tn(d_lr, u, BF16, "grad_w_in_gate")[:2 * RANK]
    g_in = _matmul_tn_group([d_a, d_b, d_c], u, moves, 9 * d + 2 * RANK, (g_lr, lr_at), "grad_w_in", tile=d)
    if on_matrix_grads is not None:
        g_pre = g_pre + on_matrix_grads(dict(w_in=g_in))
    grad_x, d_meta, g_pre_rows = _grad_h([d_a, d_b, d_c, d_lr], wt_shards, dy, x, metapad, g_pre, dm)

    grads = dict(
        meta_tokens=jnp.sum(d_meta[:, dm.TM - N_META:, :], axis=0), norm_pre=g_pre_rows[0:1], w_in=g_in,
        conv_w=g_conv[0:3], w_gate_fwd=jnp.sum(gwp_f, axis=0)[:RANK], b_gate_fwd=jnp.sum(gbp_f, axis=0)[0:1],
        w_gate_bwd=jnp.sum(gwp_b, axis=0)[RANK:2 * RANK], b_gate_bwd=jnp.sum(gbp_b, axis=0)[0:1],
        gla_norm=g_gla[0:1], w_out_conv=g_oc, w_out_gla=g_og, w_merge_out=g_out, norm_post=stats[0:1])
    return stats[1:2], grad_x, grads


MATRICES = ("w_out_conv", "w_out_gla", "w_merge_out")
SMALL_SHARDED = ("meta_tokens", "conv_w", "w_gate_fwd", "w_gate_bwd")
REPLICATED = ("norm_pre", "b_gate_fwd", "b_gate_bwd", "gla_norm", "norm_post")
NAMES = ("meta_tokens", "norm_pre", "w_in", "conv_w", "w_gate_fwd", "b_gate_fwd", "w_gate_bwd", "b_gate_bwd", "gla_norm",
         "w_out_conv", "w_out_gla", "w_merge_out", "norm_post")


def kernel(x, meta_tokens, norm_pre, w_in, conv_w, w_gate_fwd, b_gate_fwd, w_gate_bwd, b_gate_bwd, gla_norm, w_out_conv, w_out_gla, w_merge_out, norm_post, loss_target, m_meta_tokens, m_norm_pre, m_w_in, m_conv_w, m_w_gate_fwd, m_b_gate_fwd, m_w_gate_bwd, m_b_gate_bwd, m_gla_norm, m_w_out_conv, m_w_out_gla, m_w_merge_out, m_norm_post, v_meta_tokens, v_norm_pre, v_w_in, v_conv_w, v_w_gate_fwd, v_b_gate_fwd, v_w_gate_bwd, v_b_gate_bwd, v_gla_norm, v_w_out_conv, v_w_out_gla, v_w_merge_out, v_norm_post):
    w = dict(meta_tokens=meta_tokens, norm_pre=norm_pre, w_in=w_in[0], conv_w=conv_w, w_gate_fwd=w_gate_fwd,
             b_gate_fwd=b_gate_fwd, w_gate_bwd=w_gate_bwd, b_gate_bwd=b_gate_bwd, gla_norm=gla_norm,
             w_out_conv=w_out_conv[0], w_out_gla=w_out_gla[0], w_merge_out=w_merge_out[0], norm_post=norm_post)
    m = dict(meta_tokens=m_meta_tokens, norm_pre=m_norm_pre, w_in=m_w_in[0], conv_w=m_conv_w, w_gate_fwd=m_w_gate_fwd,
             b_gate_fwd=m_b_gate_fwd, w_gate_bwd=m_w_gate_bwd, b_gate_bwd=m_b_gate_bwd, gla_norm=m_gla_norm,
             w_out_conv=m_w_out_conv[0], w_out_gla=m_w_out_gla[0], w_merge_out=m_w_merge_out[0], norm_post=m_norm_post)
    v = dict(meta_tokens=v_meta_tokens, norm_pre=v_norm_pre, w_in=v_w_in[0], conv_w=v_conv_w, w_gate_fwd=v_w_gate_fwd,
             b_gate_fwd=v_b_gate_fwd, w_gate_bwd=v_w_gate_bwd, b_gate_bwd=v_b_gate_bwd, gla_norm=v_gla_norm,
             w_out_conv=v_w_out_conv[0], w_out_gla=v_w_out_gla[0], w_merge_out=v_w_merge_out[0], norm_post=v_norm_post)
    d = x.shape[-1]

    dm = _Dims(*x.shape)
    me = 4 * lax.axis_index("x") + 2 * lax.axis_index("y") + lax.axis_index("c")
    wt_shards, *small_all, u = _gather_two_level(
        [_pad_shard(w["w_in"].T.astype(BF16), me)] + [w[n] for n in SMALL_SHARDED], "gather_weights",
        _prenorm_tokens_side(x, norm_pre, dm))
    started, late_weights = _exchange_start([w[n].astype(BF16) for n in MATRICES], [], small_all[0], "gather_out_weights_start")
    small = {n: _join_column_shards(p) for n, p in zip(SMALL_SHARDED, small_all)}
    small["meta_tokens"] = small["meta_tokens"] + started

    def out_weights(after):
        return tuple(a.reshape(-1, d) for a in _exchange_wait(late_weights, after, "gather_out_weights_wait"))

    pending = []

    def on_matrix_grads(g):
        blocks = [t.reshape(N_DEV, -1, d) if t.shape[0] % (N_DEV * BF16_TILE_ROWS) == 0 else (t, t.shape[0] // N_DEV)
                  for t in g.values()]
        token, state = _exchange_start([], blocks, None, "exchange_grads_start_" + "_".join(g))
        pending.append((tuple(g), state))
        return token

    sq_err_cols, grad_x, grads = _local_step(
        x, loss_target, small["meta_tokens"], norm_pre, u, wt_shards, small["conv_w"], small["w_gate_fwd"], b_gate_fwd,
        small["w_gate_bwd"], b_gate_bwd, gla_norm, out_weights, norm_post, on_matrix_grads)
    received = {}
    for names, state in pending:
        received.update(zip(names, _exchange_wait(state, grad_x, "exchange_grads_wait_" + "_".join(names))))

    exchanged = _exchange([grads[n] for n in REPLICATED] + [sq_err_cols],
                          [_column_shards(grads[n], w[n].shape) for n in SMALL_SHARDED], "exchange_small_grads")
    small_recv = exchanged[:len(REPLICATED)] + exchanged[len(REPLICATED) + 1:]
    loss = 0.5 / d * jnp.sum(exchanged[len(REPLICATED)])

    rows_apart = lambda a: jnp.transpose(a, (2, 0, 1))
    results = {"w_in": [jnp.transpose(r, (1, 2, 0)) for r in _adamw(received["w_in"], rows_apart(w_in), rows_apart(m_w_in),
                                                                     rows_apart(v_w_in), "adamw_w_in", by_columns=True)]}
    for n in MATRICES:
        results[n] = [r[None] for r in _adamw(received[n], w[n], m[n], v[n], "adamw_" + n)]
    small_names = REPLICATED + SMALL_SHARDED
    results.update(zip(small_names, _adamw_small([(p, w[n], m[n], v[n]) for n, p in zip(small_names, small_recv)], "adamw_small")))
    return (loss, grad_x, *[results[n][i] for i in range(4) for n in NAMES])
```

```python
import jax
import jax.numpy as jnp
from jax import lax
from jax.experimental import pallas as pl
from jax.experimental.pallas import tpu as pltpu

F32 = jnp.float32
BF16 = jnp.bfloat16
MESH = pl.DeviceIdType.MESH

N_META = 16
CHUNK = 64
CHUNK_SHIFT = 6
HEADS = 4
RANK = 16
LR_LANES = 128
PAD_ROWS = CHUNK - N_META
EPS = 1e-6
GATE_NORMALIZER = 16.0
N_DEV = 8
ADAM_LR, ADAM_B1, ADAM_B2, ADAM_EPS, ADAM_WD, ADAM_STEP = 0.001, 0.9, 0.999, 1e-08, 0.01, 10
VMEM_LIMIT_BYTES = 56 * 1024 * 1024


class _Dims:
    def __init__(self, bl, s, d):
        self.Bl, self.S, self.D = bl, s, d
        self.TM = CHUNK
        self.LP = self.TM + s
        self.T = bl * self.LP
        self.TPS = self.LP // self.TM
        self.NC = self.LP // CHUNK
        self.C0 = (self.TM - CHUNK) // CHUNK
        self.DK, self.DV = d // 2, d
        self.HK, self.HV = self.DK // HEADS, self.DV // HEADS
        self.HW = 2 * self.HK + 2 * self.HV
        self.CW = 256 if d % 256 == 0 and d > 256 else d // 4
        self.NJ = d // self.CW


def _pick(n, target, mult):
    t = min(n, target)
    while t >= mult:
        if n % t == 0 and t % mult == 0:
            return t
        t -= mult
    return n


def _cp(n_axes):
    return pltpu.CompilerParams(dimension_semantics=("arbitrary",) * n_axes, vmem_limit_bytes=VMEM_LIMIT_BYTES)


def _sigmoid(x):
    return 1.0 / (1.0 + jnp.exp(-x))


def _dot(a, b):
    return jnp.dot(a, b, preferred_element_type=F32)


def _dot_nt(a, b):
    return lax.dot_general(a, b, (((1,), (1,)), ((), ())), preferred_element_type=F32)


def _dot_tn(a, b):
    return lax.dot_general(a, b, (((0,), (0,)), ((), ())), preferred_element_type=F32)


def _chunk_cumsum(x, reverse):
    rows = x.shape[0]
    r = lax.broadcasted_iota(jnp.int32, x.shape, 0) & (CHUNK - 1)
    step = 1
    while step < CHUNK:
        if reverse:
            x = x + jnp.where(r < CHUNK - step, pltpu.roll(x, rows - step, 0), 0.0)
        else:
            x = x + jnp.where(r >= step, pltpu.roll(x, step, 0), 0.0)
        step *= 2
    return x


def _exchange(gathers, scatters, name):
    arrays = list(gathers) + list(scatters)
    n, ng = len(arrays), len(gathers)

    def body(*refs):
        ins, outs = refs[:n], refs[n:2 * n]
        send_sems, recv_sems, local_sems = refs[2 * n:]
        x, y, c = lax.axis_index("x"), lax.axis_index("y"), lax.axis_index("c")
        me = 4 * x + 2 * y + c
        started = []
        for t in range(n):
            src, dst = ins[t], outs[t]
            own = pltpu.make_async_copy(src if t < ng else src.at[me], dst.at[me], local_sems.at[t])
            own.start()
            started.append(own)
            for k, pos, peer in _peers(x, y, c):
                cp = pltpu.make_async_remote_copy(
                    src_ref=src if t < ng else src.at[peer], dst_ref=dst.at[me],
                    send_sem=send_sems.at[t * (N_DEV - 1) + k - 1], recv_sem=recv_sems.at[t * (N_DEV - 1) + k - 1],
                    device_id=pos, device_id_type=MESH)
                cp.start()
                started.append(cp)
        for cp in started:
            cp.wait()

    out_shape = [jax.ShapeDtypeStruct((N_DEV,) + a.shape if t < ng else a.shape, a.dtype) for t, a in enumerate(arrays)]
    any_spec = pl.BlockSpec(memory_space=pl.ANY)
    return pl.pallas_call(
        body, name=name, out_shape=out_shape, in_specs=[any_spec] * n, out_specs=[any_spec] * n,
        scratch_shapes=[pltpu.SemaphoreType.DMA((n * (N_DEV - 1),)), pltpu.SemaphoreType.DMA((n * (N_DEV - 1),)),
                        pltpu.SemaphoreType.DMA((n,))],
        compiler_params=pltpu.CompilerParams(has_side_effects=True),
    )(*arrays)


def _gather_two_level(arrays, name, side=None):
    n = len(arrays)
    per = N_DEV - 1
    work, side_in, side_in_specs, side_out, side_out_specs, side_scratch = side or (None, [], [], [], [], [])
    n_in, n_out = len(side_in), len(side_out)

    def body(*refs):
        ins, outs = refs[:n], refs[n + n_in:2 * n + n_in]
        send_sems, recv_sems, local_sems = refs[2 * n + n_in + n_out:2 * n + n_in + n_out + 3]
        x, y, c = lax.axis_index("x"), lax.axis_index("y"), lax.axis_index("c")
        sibling = (x, y, 1 - c)
        chips = [(1 - x, y), (x, 1 - y), (1 - x, 1 - y)]
        index = lambda px, py, pc: 4 * px + 2 * py + pc

        def copy(t, k, block, to, from_input=False):
            slab = outs[t].at[index(*block)]
            return pltpu.make_async_remote_copy(
                src_ref=ins[t] if from_input else slab, dst_ref=slab, send_sem=send_sems.at[t * per + k],
                recv_sem=recv_sems.at[t * per + k], device_id=to, device_id_type=MESH)

        own, sent = [], []
        for t in range(n):
            own.append(pltpu.make_async_copy(ins[t], outs[t].at[index(x, y, c)], local_sems.at[t]))
            own[-1].start()
            first = [copy(t, 0, (x, y, c), sibling, True)]
            first += [copy(t, 1 + j, (x, y, c), (*chip, c), True) for j, chip in enumerate(chips)]
            for cp in first:
                cp.start()
            sent += first
        if work is not None:
            work(refs[n:n + n_in], refs[2 * n + n_in:2 * n + n_in + n_out], refs[2 * n + n_in + n_out + 3:])
        for t in range(n):
            for j, chip in enumerate(chips):
                copy(t, 1 + j, (*chip, c), (x, y, c)).wait_recv()
                sent.append(copy(t, 4 + j, (*chip, c), sibling))
                sent[-1].start()
        for t in range(n):
            copy(t, 0, sibling, (x, y, c)).wait_recv()
            for j, chip in enumerate(chips):
                copy(t, 4 + j, (*chip, 1 - c), (x, y, c)).wait_recv()
        for cp in sent:
            cp.wait_send()
        for cp in own:
            cp.wait()

    out_shape = [jax.ShapeDtypeStruct((N_DEV,) + a.shape, a.dtype) for a in arrays]
    any_spec = pl.BlockSpec(memory_space=pl.ANY)
    return pl.pallas_call(
        body, name=name, out_shape=out_shape + list(side_out), in_specs=[any_spec] * n + list(side_in_specs),
        out_specs=[any_spec] * n + list(side_out_specs),
        scratch_shapes=[pltpu.SemaphoreType.DMA((n * per,)), pltpu.SemaphoreType.DMA((n * per,)),
                        pltpu.SemaphoreType.DMA((n,))] + list(side_scratch),
        compiler_params=pltpu.CompilerParams(has_side_effects=True, vmem_limit_bytes=VMEM_LIMIT_BYTES),
    )(*arrays, *side_in)


def _peers(x, y, c):
    out = []
    for k in range(1, N_DEV):
        px = 1 - x if (k >> 2) & 1 else x
        py = 1 - y if (k >> 1) & 1 else y
        pc = 1 - c if k & 1 else c
        out.append((k, (px, py, pc), 4 * px + 2 * py + pc))
    return out


def _exchange_start(gathers, scatters, after, name):
    shard_rows = [None] * len(gathers) + [s[1] if isinstance(s, tuple) else None for s in scatters]
    arrays = list(gathers) + [s[0] if isinstance(s, tuple) else s for s in scatters]
    n, ng = len(arrays), len(gathers)
    hbm = pl.BlockSpec(memory_space=pltpu.HBM)
    sem = pl.BlockSpec(memory_space=pltpu.SEMAPHORE)

    extra = [] if after is None else [after]
    ne = len(extra)

    def body(*refs):
        ins, lands = refs[:n], refs[n:2 * n]
        send_sems, recv_sems = refs[2 * n + ne], refs[2 * n + ne + 1]
        token = refs[4 * n + ne + 2]
        x, y, c = lax.axis_index("x"), lax.axis_index("y"), lax.axis_index("c")
        me = 4 * x + 2 * y + c
        for t in range(n):
            for k, pos, peer in _peers(x, y, c):
                pltpu.make_async_remote_copy(
                    src_ref=_block_for(ins[t], peer, t < ng, shard_rows[t]), dst_ref=lands[t].at[me],
                    send_sem=send_sems.at[t * (N_DEV - 1) + k - 1], recv_sem=recv_sems.at[t * (N_DEV - 1) + k - 1],
                    device_id=pos, device_id_type=MESH).start()
        token[...] = jnp.zeros_like(token)

    me = 4 * lax.axis_index("x") + 2 * lax.axis_index("y") + lax.axis_index("c")

    def own_block(t, a):
        if t < ng:
            return a
        if shard_rows[t] is None:
            return lax.dynamic_index_in_dim(a, me, 0, keepdims=False)
        assert all(_shard_window(j, shard_rows[t]) + _padded_shard_rows(shard_rows[t]) <= a.shape[0] for j in range(N_DEV))
        return lax.dynamic_slice_in_dim(a, _shard_window(me, shard_rows[t]), _padded_shard_rows(shard_rows[t]), 0)

    blocks = [own_block(t, a) for t, a in enumerate(arrays)]
    lands = [lax.dynamic_update_index_in_dim(lax.empty((N_DEV,) + b.shape if t < ng or shard_rows[t] else a.shape, a.dtype), b, me, 0)
             for t, (a, b) in enumerate(zip(arrays, blocks))]
    operands = [pltpu.with_memory_space_constraint(a, pltpu.HBM) for a in arrays + lands]
    sems = pltpu.SemaphoreType.DMA((n * (N_DEV - 1),))
    res = pl.pallas_call(
        body, name=name,
        out_shape=(sems, sems, *[pltpu.HBM(a.shape, a.dtype) for a in arrays + lands], jax.ShapeDtypeStruct((8, 128), F32)),
        in_specs=[hbm] * (2 * n) + [pl.BlockSpec(memory_space=pl.ANY)] * ne,
        out_specs=(sem, sem, *[hbm] * (2 * n), pl.BlockSpec(memory_space=pltpu.VMEM)),
        input_output_aliases={i: 2 + i for i in range(2 * n)},
        compiler_params=pltpu.CompilerParams(has_side_effects=pltpu.SideEffectType.DATAFLOW_SIDE_EFFECTING),
    )(*operands, *extra)
    return res[-1][0, 0], (ng, shard_rows, res[0], res[1], list(res[2:2 + n]), list(res[2 + n:2 + 2 * n]))


def _block_for(ref, peer, whole, shard_rows):
    if whole:
        return ref
    if shard_rows is None:
        return ref.at[peer]
    return ref.at[pl.ds(pl.multiple_of(_shard_window(peer, shard_rows), BF16_TILE_ROWS), _padded_shard_rows(shard_rows))]


def _exchange_wait(state, after, name):
    ng, shard_rows, send_sems, recv_sems, sent, lands = state
    n = len(sent)
    hbm = pl.BlockSpec(memory_space=pltpu.HBM)
    sem = pl.BlockSpec(memory_space=pltpu.SEMAPHORE)

    def body(*refs):
        ins, land_refs = refs[:n], refs[n:2 * n]
        send_ref, recv_ref = refs[2 * n], refs[2 * n + 1]
        x, y, c = lax.axis_index("x"), lax.axis_index("y"), lax.axis_index("c")
        me = 4 * x + 2 * y + c
        for t in range(n):
            for k, pos, peer in _peers(x, y, c):
                cp = pltpu.make_async_remote_copy(
                    src_ref=_block_for(ins[t], peer, t < ng, shard_rows[t]), dst_ref=land_refs[t].at[me],
                    send_sem=send_ref.at[t * (N_DEV - 1) + k - 1], recv_sem=recv_ref.at[t * (N_DEV - 1) + k - 1],
                    device_id=pos, device_id_type=MESH)
                cp.wait_send()
                cp.wait_recv()

    res = pl.pallas_call(
        body, name=name, out_shape=tuple(pltpu.HBM(a.shape, a.dtype) for a in sent + lands),
        in_specs=[hbm] * (2 * n) + [sem, sem, pl.BlockSpec(memory_space=pl.ANY)], out_specs=tuple([hbm] * (2 * n)),
        input_output_aliases={i: i for i in range(2 * n)},
        compiler_params=pltpu.CompilerParams(has_side_effects=pltpu.SideEffectType.DATAFLOW_SIDE_EFFECTING),
    )(*sent, *lands, send_sems, recv_sems, after)
    return list(res[n:])


def _rms_scaled(h, g):
    return (h * lax.rsqrt(jnp.mean(h * h, axis=-1, keepdims=True) + EPS) * g).astype(BF16)


def _prenorm_tokens_side(x, g_pre, dm):
    bl, s, d = x.shape
    rows = _pick(s, 512, 16)
    tiles = [(b, j) for b in range(bl) for j in range(s // rows)]

    def work(ins, outs, scratch):
        (x_ref, g_ref), (u_ref,), (xbuf, ubuf, sem_in, sem_out) = ins, outs, scratch

        def load(t, slot):
            b, j = tiles[t]
            return pltpu.make_async_copy(x_ref.at[b, pl.ds(j * rows, rows), :], xbuf.at[slot], sem_in.at[slot])

        def store(t, slot):
            b, j = tiles[t]
            return pltpu.make_async_copy(ubuf.at[slot], u_ref.at[pl.ds(b * dm.LP + dm.TM + j * rows, rows), :], sem_out.at[slot])

        load(0, 0).start()
        for t in range(len(tiles)):
            slot = t % 2
            if t + 1 < len(tiles):
                load(t + 1, 1 - slot).start()
            load(t, slot).wait()
            if t >= 2:
                store(t - 2, slot).wait()
            ubuf[slot] = _rms_scaled(xbuf[slot], g_ref[...])
            store(t, slot).start()
        for t in range(max(len(tiles) - 2, 0), len(tiles)):
            store(t, t % 2).wait()

    any_spec = pl.BlockSpec(memory_space=pl.ANY)
    return (work, [x, g_pre], [any_spec, pl.BlockSpec(memory_space=pltpu.VMEM)],
            [jax.ShapeDtypeStruct((dm.T, d), BF16)], [any_spec],
            [pltpu.VMEM((2, rows, d), F32), pltpu.VMEM((2, rows, d), BF16), pltpu.SemaphoreType.DMA((2,)),
             pltpu.SemaphoreType.DMA((2,))])


def _prenorm_meta(u, metapad, g_pre, dm):
    tm, tps, d = dm.TM, dm.TPS, dm.D

    def body(u_in, mp_ref, g_ref, u_ref):
        u_ref[...] = _rms_scaled(mp_ref[...], g_ref[...])

    return pl.pallas_call(
        body, name="prenorm_meta", grid=(dm.Bl,),
        in_specs=[pl.BlockSpec(memory_space=pl.ANY), pl.BlockSpec((tm, d), lambda i: (0, 0)),
                  pl.BlockSpec((1, d), lambda i: (0, 0))],
        out_specs=pl.BlockSpec((tm, d), lambda i: (i * tps, 0)),
        out_shape=jax.ShapeDtypeStruct((dm.T, d), BF16), input_output_aliases={0: 0}, compiler_params=_cp(1),
    )(u, metapad, g_pre)


def _matmul_tn(a, b, out_dtype, name, tt=2816, tn=1024, tk=1024):
    t, k = a.shape
    n = b.shape[1]
    tt, tn, tk = _pick(t, tt, 16), _pick(n, tn, 128), _pick(k, tk, 128)
    nt = t // tt

    def body(a_ref, b_ref, o_ref, acc):
        p = _dot_tn(a_ref[...].astype(BF16), b_ref[...].astype(BF16))
        i = pl.program_id(2)

        @pl.when(i == 0)
        def _():
            acc[...] = p

        @pl.when(i > 0)
        def _():
            acc[...] += p

        @pl.when(i == nt - 1)
        def _():
            o_ref[...] = acc[...].astype(out_dtype)

    return pl.pallas_call(
        body, name=name, grid=(k // tk, n // tn, nt),
        in_specs=[pl.BlockSpec((tt, tk), lambda kk, j, i: (i, kk)), pl.BlockSpec((tt, tn), lambda kk, j, i: (i, j))],
        out_specs=pl.BlockSpec((tk, tn), lambda kk, j, i: (kk, j)),
        out_shape=jax.ShapeDtypeStruct((k, n), out_dtype), scratch_shapes=[pltpu.VMEM((tk, tn), F32)],
        compiler_params=_cp(3),
    )(a, b)


def _matmul_tn_group(a_list, b, moves, out_rows, extra, name, tt=2816, tile=1024):
    t, n = b.shape
    tt = _pick(t, tt, 16)
    nt = t // tt
    counts = [a.shape[1] // tile for a in a_list]
    starts = [sum(counts[:m]) for m in range(len(a_list))]
    items = sum(counts)
    extra_a, extra_rows, extra_at = extra
    extra_w = extra_a.shape[1]
    cuts = [[] for _ in range(items)]
    for row, rows, at in moves:
        while rows > 0:
            p, r = divmod(row, tile)
            take = min(rows, tile - r)
            cuts[p].append((r, take, at))
            row, rows, at = row + take, rows - take, at + take
    assert all(v % BF16_TILE_ROWS == 0 for cut in cuts for move in cut for v in move)
    assert sum(rows for _, rows, _ in moves) + extra_rows == out_rows and extra_rows % BF16_TILE_ROWS == 0

    def active(p, m):
        return (p >= starts[m]) & (p < starts[m] + counts[m])

    def body(*refs):
        a_refs, b_ref, x_ref = refs[:len(a_list)], refs[len(a_list)], refs[len(a_list) + 1]
        o_ref, acc, stage, sems, x_sem, abuf, a_sems, x_acc, x_stage = refs[-9:]
        p, i = pl.program_id(0), pl.program_id(1)

        def fetch(p, i, slot, m):
            cols = pl.ds(pl.multiple_of((p - starts[m]) * tile, tile), tile)
            return pltpu.make_async_copy(a_refs[m].at[pl.ds(pl.multiple_of(i * tt, tt), tt), cols], abuf.at[slot], a_sems.at[slot])

        def start_fetch(p, i, slot):
            for m in range(len(a_list)):
                @pl.when(active(p, m))
                def _(m=m):
                    fetch(p, i, slot, m).start()

        step = p * nt + i
        slot = step % 2

        @pl.when(step == 0)
        def _():
            start_fetch(p, i, slot)

        @pl.when(step + 1 < items * nt)
        def _():
            last = i == nt - 1
            start_fetch(jnp.where(last, p + 1, p), jnp.where(last, 0, i + 1), 1 - slot)

        pltpu.make_async_copy(a_refs[0].at[pl.ds(0, tt), pl.ds(0, tile)], abuf.at[slot], a_sems.at[slot]).wait()

        def writes(item):
            return [pltpu.make_async_copy(stage.at[pl.ds(r, rows), :], o_ref.at[pl.ds(at, rows), :], sems.at[s])
                    for s, (r, rows, at) in enumerate(cuts[item])]

        extra_copy = pltpu.make_async_copy(x_stage, o_ref.at[pl.ds(extra_at, extra_rows), :], x_sem.at[0])

        @pl.when(p == 0)
        def _():
            x_prod = _dot_tn(x_ref[...].astype(BF16), b_ref[...])

            @pl.when(i == 0)
            def _():
                x_acc[...] = x_prod

            @pl.when(i > 0)
            def _():
                x_acc[...] += x_prod

            @pl.when(i == nt - 1)
            def _():
                x_stage[...] = x_acc[...].astype(BF16)[:extra_rows]
                extra_copy.start()

        prod = _dot_tn(abuf[slot], b_ref[...])

        @pl.when(i == 0)
        def _():
            acc[...] = prod

        @pl.when(i > 0)
        def _():
            acc[...] += prod

        for item in range(items):
            @pl.when((p == item) & (i == nt - 1))
            def _(item=item):
                if item > 0:
                    for cp in writes(item - 1):
                        cp.wait()
                stage[...] = acc[...].astype(BF16)
                for cp in writes(item):
                    cp.start()
                if item == items - 1:
                    for cp in writes(item):
                        cp.wait()
                    extra_copy.wait()

    assert all(a.dtype == BF16 for a in a_list) and b.dtype == BF16
    any_spec = pl.BlockSpec(memory_space=pl.ANY)
    return pl.pallas_call(
        body, name=name, grid=(items, nt),
        in_specs=[any_spec] * len(a_list) + [pl.BlockSpec((tt, n), lambda p, i: (i, 0)),
                                             pl.BlockSpec((tt, extra_w), lambda p, i: (jnp.where(p == 0, i, nt - 1), 0))],
        out_specs=any_spec, out_shape=jax.ShapeDtypeStruct((out_rows, n), BF16),
        scratch_shapes=[pltpu.VMEM((tile, n), F32), pltpu.VMEM((tile, n), BF16),
                        pltpu.SemaphoreType.DMA((max(len(cut) for cut in cuts),)), pltpu.SemaphoreType.DMA((1,)),
                        pltpu.VMEM((2, tt, tile), BF16), pltpu.SemaphoreType.DMA((2,)),
                        pltpu.VMEM((extra_w, n), F32), pltpu.VMEM((extra_rows, n), BF16)],
        compiler_params=_cp(2),
    )(*a_list, b, extra_a)


BF16_TILE_ROWS = 16


def _shard_offset(index, shard_rows):
    return (index * shard_rows) % BF16_TILE_ROWS


def _padded_shard_rows(shard_rows):
    return -(-(shard_rows + max(_shard_offset(j, shard_rows) for j in range(N_DEV))) // BF16_TILE_ROWS) * BF16_TILE_ROWS


def _pad_shard(wt_shard, index):
    rows, d = wt_shard.shape
    return lax.dynamic_update_slice(jnp.zeros((_padded_shard_rows(rows), d), wt_shard.dtype), wt_shard,
                                    (_shard_offset(index, rows), 0))


def _shard_window(index, shard_rows):
    return index * shard_rows - _shard_offset(index, shard_rows)


def _packed_parts(dm):
    d, dk, hk, hv, cw, nj, hw = dm.D, dm.DK, dm.HK, dm.HV, dm.CW, dm.NJ, dm.HW
    blocks = [(0, (j * 4 + p) * cw, p * d + j * cw, cw) for j in range(nj) for p in range(4)]
    for h in range(HEADS):
        blocks += [(1, h * hw, 4 * d + h * hk, hk), (1, h * hw + hk, 4 * d + dk + h * hk, hk),
                   (1, h * hw + 2 * hk, 5 * d + h * hv, hv), (1, h * hw + 2 * hk + hv, 6 * d + h * hv, hv)]
    blocks += [(2, 0, 7 * d + 2 * RANK, 2 * d), (3, 0, 7 * d, 2 * RANK)]
    return [4 * d, 3 * d, 2 * d, LR_LANES], blocks


def _pack_plan(dm):
    sh = (9 * dm.D + 2 * RANK) // N_DEV
    tile = BF16_TILE_ROWS
    copies, straddles = [], []
    for part, dst, r0, n in _packed_parts(dm)[1]:
        for j in range(N_DEV):
            a, b = max(r0, sh * j), min(r0 + n, sh * (j + 1))
            if a >= b:
                continue
            a_up, b_down = -(-a // tile) * tile, b // tile * tile
            if b_down > a_up:
                copies.append((j, a_up - sh * j + _shard_offset(j, sh), b_down - a_up, part, dst + a_up - r0))
            if a % tile:
                lo = a // tile * tile
                straddles.append((j, lo - sh * (j - 1) + _shard_offset(j - 1, sh), part, dst + lo - r0, a - lo))
    return copies, straddles


def _packed_scratch(dm):
    copies, straddles = _pack_plan(dm)
    return ([pltpu.VMEM((rows, dm.D), BF16) for rows in _packed_parts(dm)[0]]
            + [pltpu.VMEM((2 * max(len(straddles), 1), BF16_TILE_ROWS, dm.D), BF16),
               pltpu.SemaphoreType.DMA((len(copies) + 2 * len(straddles),))])


def _load_packed(g_ref, parts, edges, sems, dm):
    copies, straddles = _pack_plan(dm)
    tile = BF16_TILE_ROWS
    parts[3][2 * RANK:, :] = jnp.zeros((LR_LANES - 2 * RANK, dm.D), BF16)
    dmas = [pltpu.make_async_copy(g_ref.at[j, pl.ds(src, n), :], parts[p].at[pl.ds(dst, n), :], sems.at[i])
            for i, (j, src, n, p, dst) in enumerate(copies)]
    for i, (j, src, p, dst, split) in enumerate(straddles):
        k = len(copies) + 2 * i
        dmas.append(pltpu.make_async_copy(g_ref.at[j - 1, pl.ds(src, tile), :], edges.at[2 * i], sems.at[k]))
        dmas.append(pltpu.make_async_copy(g_ref.at[j, pl.ds(0, tile), :], edges.at[2 * i + 1], sems.at[k + 1]))
    for cp in dmas:
        cp.start()
    for cp in dmas:
        cp.wait()
    row = lax.broadcasted_iota(jnp.int32, (tile, dm.D), 0)
    for i, (j, src, p, dst, split) in enumerate(straddles):
        parts[p][dst:dst + tile, :] = jnp.where(row < split, edges[2 * i], edges[2 * i + 1])


def _inproj(u, gathered, dm):
    t, d = u.shape
    tm = _pick(t, 512, 16)
    widths = _packed_parts(dm)[0]
    cn = 1024

    def body(u_ref, g_ref, *rest):
        outs, parts, (edges, sems) = rest[:4], rest[4:8], rest[8:]

        @pl.when(pl.program_id(0) == 0)
        def _():
            _load_packed(g_ref, parts, edges, sems, dm)

        ut = u_ref[...]
        for w, o_ref in zip(parts, outs):
            n = w.shape[0]
            step = cn if n % cn == 0 else n
            for j in range(0, n, step):
                o_ref[:, j:j + step] = _dot_nt(ut, w[j:j + step, :]).astype(BF16)

    return pl.pallas_call(
        body, name="inproj", grid=(t // tm,),
        in_specs=[pl.BlockSpec((tm, d), lambda i: (i, 0)), pl.BlockSpec(memory_space=pl.ANY)],
        out_specs=[pl.BlockSpec((tm, w), lambda i: (i, 0)) for w in widths],
        out_shape=[jax.ShapeDtypeStruct((t, w), BF16) for w in widths],
        scratch_shapes=_packed_scratch(dm), compiler_params=_cp(1),
    )(u, gathered)


def _conv_rows(dm):
    return _pick(dm.LP, 256, 16)


def _shifted(m, prev_row, next_row, rows):
    row = lax.broadcasted_iota(jnp.int32, m.shape, 0)
    m_prev = jnp.where(row == 0, prev_row, pltpu.roll(m, 1, 0))
    m_next = jnp.where(row == rows - 1, next_row, pltpu.roll(m, rows - 1, 0))
    return m_prev, m_next


def _conv_fwd(proj_a, conv_w, dm):
    lp, cw, rc = dm.LP, dm.CW, _conv_rows(dm)
    nchunk = lp // rc

    def body(p_ref, w_ref, y_ref):
        w0, w1, w2 = w_ref[0:1, :], w_ref[1:2, :], w_ref[2:3, :]

        def chunk(ci, carry):
            r0 = pl.multiple_of(ci * rc, rc)
            blk = p_ref[pl.ds(r0, rc), :].astype(F32)
            cb, cc, cx, cz = (blk[:, i * cw:(i + 1) * cw] for i in range(4))
            m = cc * cx
            rp = pl.multiple_of(jnp.maximum(r0 - 16, 0), 16)
            rn = pl.multiple_of(jnp.minimum(r0 + rc, lp - 16), 16)
            pv = p_ref[pl.ds(rp, 16), cw:3 * cw].astype(F32)
            nx = p_ref[pl.ds(rn, 16), cw:3 * cw].astype(F32)
            prev_row = jnp.where(ci > 0, pv[15:16, :cw] * pv[15:16, cw:], 0.0)
            next_row = jnp.where(ci < nchunk - 1, nx[0:1, :cw] * nx[0:1, cw:], 0.0)
            m_prev, m_next = _shifted(m, prev_row, next_row, rc)
            s = w0 * m_prev + w1 * m + w2 * m_next
            y_ref[pl.ds(r0, rc), :] = (cb * s * (cz * _sigmoid(cz))).astype(BF16)
            return carry

        lax.fori_loop(0, nchunk, chunk, 0)

    return pl.pallas_call(
        body, name="conv_fwd", grid=(dm.Bl, dm.NJ),
        in_specs=[pl.BlockSpec((lp, 4 * cw), lambda s, j: (s, j)), pl.BlockSpec((3, cw), lambda s, j: (0, j))],
        out_specs=pl.BlockSpec((lp, cw), lambda s, j: (s, j)),
        out_shape=jax.ShapeDtypeStruct((dm.T, dm.D), BF16), compiler_params=_cp(2),
    )(proj_a, conv_w)


def _conv_bwd(proj_a, dy_conv, conv_w, dm):
    lp, cw, rc = dm.LP, dm.CW, _conv_rows(dm)
    nchunk = lp // rc

    def body(p_ref, dy_ref, w_ref, d_ref, gw_ref):
        w0, w1, w2 = w_ref[0:1, :], w_ref[1:2, :], w_ref[2:3, :]

        def ds_of(p4, dy):
            cb, cz = p4[:, :cw], p4[:, 3 * cw:]
            return dy * cb * (cz * _sigmoid(cz))

        def chunk(ci, carry):
            g0, g1, g2 = carry
            r0 = pl.multiple_of(ci * rc, rc)
            blk = p_ref[pl.ds(r0, rc), :].astype(F32)
            dy = dy_ref[pl.ds(r0, rc), :].astype(F32)
            cb, cc, cx, cz = (blk[:, i * cw:(i + 1) * cw] for i in range(4))
            rp = pl.multiple_of(jnp.maximum(r0 - 16, 0), 16)
            rn = pl.multiple_of(jnp.minimum(r0 + rc, lp - 16), 16)
            pv = p_ref[pl.ds(rp, 16), :].astype(F32)[15:16]
            nx = p_ref[pl.ds(rn, 16), :].astype(F32)[0:1]
            dpv = dy_ref[pl.ds(rp, 16), :].astype(F32)[15:16]
            dnx = dy_ref[pl.ds(rn, 16), :].astype(F32)[0:1]
            has_prev, has_next = ci > 0, ci < nchunk - 1
            m = cc * cx
            m_prev, m_next = _shifted(m, jnp.where(has_prev, pv[:, cw:2 * cw] * pv[:, 2 * cw:3 * cw], 0.0),
                                      jnp.where(has_next, nx[:, cw:2 * cw] * nx[:, 2 * cw:3 * cw], 0.0), rc)
            s = w0 * m_prev + w1 * m + w2 * m_next
            sg = _sigmoid(cz)
            silu = cz * sg
            ds = dy * cb * silu
            ds_prev, ds_next = _shifted(ds, jnp.where(has_prev, ds_of(pv, dpv), 0.0),
                                        jnp.where(has_next, ds_of(nx, dnx), 0.0), rc)
            dm_ = w0 * ds_next + w1 * ds + w2 * ds_prev
            d_ref[pl.ds(r0, rc), 0:cw] = (dy * s * silu).astype(BF16)
            d_ref[pl.ds(r0, rc), cw:2 * cw] = (dm_ * cx).astype(BF16)
            d_ref[pl.ds(r0, rc), 2 * cw:3 * cw] = (dm_ * cc).astype(BF16)
            d_ref[pl.ds(r0, rc), 3 * cw:4 * cw] = (dy * cb * s * (sg * (1.0 + cz * (1.0 - sg)))).astype(BF16)
            return (g0 + jnp.sum(ds * m_prev, axis=0, keepdims=True), g1 + jnp.sum(ds * m, axis=0, keepdims=True),
                    g2 + jnp.sum(ds * m_next, axis=0, keepdims=True))

        z = jnp.zeros((1, cw), F32)
        g0, g1, g2 = lax.fori_loop(0, nchunk, chunk, (z, z, z))

        @pl.when(pl.program_id(1) == 0)
        def _():
            gw_ref[...] = jnp.zeros_like(gw_ref)

        gw_ref[0:1, :] += g0
        gw_ref[1:2, :] += g1
        gw_ref[2:3, :] += g2

    return pl.pallas_call(
        body, name="conv_bwd", grid=(dm.NJ, dm.Bl),
        in_specs=[pl.BlockSpec((lp, 4 * cw), lambda j, s: (s, j)), pl.BlockSpec((lp, cw), lambda j, s: (s, j)),
                  pl.BlockSpec((3, cw), lambda j, s: (0, j))],
        out_specs=[pl.BlockSpec((lp, 4 * cw), lambda j, s: (s, j)), pl.BlockSpec((8, cw), lambda j, s: (0, j))],
        out_shape=[jax.ShapeDtypeStruct((dm.T, 4 * dm.D), BF16), jax.ShapeDtypeStruct((8, dm.D), F32)],
        compiler_params=_cp(2),
    )(proj_a, dy_conv, conv_w)


def _interleave(gens):
    results = [None] * len(gens)
    live = list(range(len(gens)))
    while live:
        for idx in list(live):
            try:
                next(gens[idx])
            except StopIteration as done:
                results[idx] = done.value
                live.remove(idx)
    return results


def _group_chunks(dm):
    n = dm.NC - dm.C0
    return 3 if n % 3 == 0 else 1


def _group_masks(rows):
    ii = lax.broadcasted_iota(jnp.int32, (rows, rows), 0)
    jj = lax.broadcasted_iota(jnp.int32, (rows, rows), 1)
    same = jnp.right_shift(ii, CHUNK_SHIFT) == jnp.right_shift(jj, CHUNK_SHIFT)
    return same & (jj <= ii), same & (jj > ii)


def _first_row(chunk):
    return chunk * CHUNK if isinstance(chunk, int) else pl.multiple_of(chunk * CHUNK, CHUNK)


def _chunk_totals(b, fwd):
    hk = b.shape[1]
    rows = [b[c * CHUNK + CHUNK - 1:(c + 1) * CHUNK] if fwd else b[c * CHUNK:c * CHUNK + 1]
            for c in range(b.shape[0] // CHUNK)]
    return jnp.concatenate([jnp.broadcast_to(r, (CHUNK, hk)) for r in rows], axis=0)


def _log_gate(lr_rows, w_ref, b_ref, first_group, hk):
    z = _dot(lr_rows, w_ref[...]) + b_ref[...]
    e = jnp.exp(-jnp.abs(z))
    g = (jnp.minimum(z, 0.0) - jnp.log(1.0 + e)) * (1.0 / GATE_NORMALIZER)
    dg_dz = jnp.where(z >= 0.0, e, 1.0) / (1.0 + e) * (1.0 / GATE_NORMALIZER)
    row = lax.broadcasted_iota(jnp.int32, (lr_rows.shape[0], hk), 0)
    pad = first_group & (row < PAD_ROWS)
    return jnp.where(pad, 0.0, g), jnp.where(pad, 0.0, dg_dz)


def _gla_fwd(proj_b, lr, wg_f, bg_f, wg_b, bg_b, gla_g, dm):
    lp, hk, hv, nc, c0, hw = dm.LP, dm.HK, dm.HV, dm.NC, dm.C0, dm.HW
    scale = hk ** -0.5
    gc = _group_chunks(dm)
    gr, ng = gc * CHUNK, (nc - c0) // gc

    def body(p_ref, lr_ref, wf_ref, bf_ref, wb_ref, bb_ref, gg_ref, o_ref, y_ref, st_ref, b_out, gs_out, oacc_f, oacc_b):
        low_incl, up_strict = _group_masks(gr)
        if c0 > 0:
            zr = c0 * CHUNK
            o_ref[0:zr, :] = jnp.zeros((zr, hv), BF16)
            y_ref[0:zr, :] = jnp.zeros((zr, hv), BF16)
            b_out[:, 0:zr, :] = jnp.zeros((2, zr, hk), F32)
            gs_out[:, 0:zr, :] = jnp.zeros((2, zr, hk), F32)
            st_ref[0, 0, :, 0:c0] = jnp.zeros((2, c0, hv, hk), BF16)

        def decay(gi, fwd):
            w_ref, b_ref = (wf_ref, bf_ref) if fwd else (wb_ref, bb_ref)
            r0 = _first_row(c0 + gi * gc)
            yield
            g, dg_dz = _log_gate(lr_ref[pl.ds(r0, gr), :], w_ref, b_ref, gi == 0, hk)
            gs_out[0 if fwd else 1, pl.ds(r0, gr), :] = dg_dz
            yield
            b = _chunk_cumsum(g, not fwd)
            b_out[0 if fwd else 1, pl.ds(r0, gr), :] = b
            return b

        def group(gi, st, b, fwd):
            oacc = oacc_f if fwd else oacc_b
            r0 = pl.multiple_of((c0 + gi * gc) * CHUNK, CHUNK)
            blk = p_ref[pl.ds(r0, gr), :]
            q = blk[:, :hk].astype(F32) * scale
            k = blk[:, hk:2 * hk].astype(F32)
            v = blk[:, 2 * hk:2 * hk + hv]
            btot = _chunk_totals(b, fwd)
            qi = (q * jnp.exp(b)).astype(BF16)
            ki = (k * jnp.exp(-b)).astype(BF16)
            kd = (k * jnp.exp(btot - b)).astype(BF16)
            dec = jnp.exp(btot)
            a = _dot_nt(qi, ki)
            yield
            o = _dot(jnp.where(low_incl if fwd else up_strict, a, 0.0).astype(BF16), v)
            chunk_rows = [slice(c * CHUNK, (c + 1) * CHUNK) for c in range(gc)]
            kv = [_dot_tn(v[rows], kd[rows]) for rows in chunk_rows]
            for c in (range(gc) if fwd else reversed(range(gc))):
                yield
                rows = chunk_rows[c]
                st_b = st.astype(BF16)
                st_ref[0, 0, 0 if fwd else 1, c0 + gi * gc + c] = st_b
                oacc[pl.ds(r0 + c * CHUNK, CHUNK), :] = o[rows] + _dot_nt(qi[rows], st_b)
                st = st * dec[c * CHUNK:c * CHUNK + 1] + kv[c]
            return st

        def step(i, carry):
            st_f, st_b, b_f, b_b = carry
            gf, gb = i, ng - 1 - i
            return tuple(_interleave([group(gf, st_f, b_f, True), group(gb, st_b, b_b, False),
                                      decay(jnp.minimum(gf + 1, ng - 1), True), decay(jnp.maximum(gb - 1, 0), False)]))

        zero = jnp.zeros((hv, hk), F32)
        lax.fori_loop(0, ng, step, (zero, zero, *_interleave([decay(0, True), decay(ng - 1, False)])))

        def finish(i, carry):
            r0 = pl.multiple_of((c0 + i * gc) * CHUNK, CHUNK)
            o = oacc_f[pl.ds(r0, gr), :] + oacc_b[pl.ds(r0, gr), :]
            r = p_ref[pl.ds(r0, gr), 2 * hk + hv:].astype(F32)
            on = o * lax.rsqrt(jnp.mean(o * o, axis=-1, keepdims=True) + EPS) * gg_ref[...]
            o_ref[pl.ds(r0, gr), :] = o.astype(BF16)
            y_ref[pl.ds(r0, gr), :] = (on * r * _sigmoid(r)).astype(BF16)
            return carry

        lax.fori_loop(0, ng, finish, 0)

    head = lambda s, h: (s, h)
    wspec = pl.BlockSpec((LR_LANES, hk), lambda s, h: (0, h))
    bspec = pl.BlockSpec((1, hk), lambda s, h: (0, h))
    return pl.pallas_call(
        body, name="gla_fwd", grid=(dm.Bl, HEADS),
        in_specs=[pl.BlockSpec((lp, hw), head), pl.BlockSpec((lp, LR_LANES), lambda s, h: (s, 0)),
                  wspec, bspec, wspec, bspec, pl.BlockSpec((1, hv), lambda s, h: (0, 0))],
        out_specs=[pl.BlockSpec((lp, hv), head), pl.BlockSpec((lp, hv), head),
                   pl.BlockSpec((1, 1, 2, nc, hv, hk), lambda s, h: (s, h, 0, 0, 0, 0)),
                   pl.BlockSpec((2, lp, hk), lambda s, h: (0, s, h)), pl.BlockSpec((2, lp, hk), lambda s, h: (0, s, h))],
        out_shape=[jax.ShapeDtypeStruct((dm.T, dm.DV), BF16), jax.ShapeDtypeStruct((dm.T, dm.DV), BF16),
                   jax.ShapeDtypeStruct((dm.Bl, HEADS, 2, nc, hv, hk), BF16),
                   jax.ShapeDtypeStruct((2, dm.T, dm.DK), F32), jax.ShapeDtypeStruct((2, dm.T, dm.DK), F32)],
        scratch_shapes=[pltpu.VMEM((lp, hv), F32), pltpu.VMEM((lp, hv), F32)],
        compiler_params=_cp(2),
    )(proj_b, lr, wg_f, bg_f, wg_b, bg_b, gla_g)


def _gla_bwd(proj_b, lr, o_all, dy_gla, states, decays, gate_slopes, wg_f, wg_b, gla_g, dm):
    lp, hk, hv, nc, c0, hw = dm.LP, dm.HK, dm.HV, dm.NC, dm.C0, dm.HW
    scale = hk ** -0.5
    gc = _group_chunks(dm)
    gr, ng = gc * CHUNK, (nc - c0) // gc

    def body(p_ref, lr_ref, o_ref, dy_ref, st_ref, b_ref, gs_ref, wf_ref, wb_ref, gg_ref,
             d_ref, dlr_ref, gwf_ref, gbf_ref, gwb_ref, gbb_ref, ggg_ref, do_s, dq_s, dk_s, dv_s, dz_s):
        low_incl, up_strict = _group_masks(gr)
        h = pl.program_id(1)

        @pl.when(h == 0)
        def _():
            dlr_ref[...] = jnp.zeros_like(dlr_ref)

        if c0 > 0:
            zr = c0 * CHUNK
            d_ref[0:zr, :] = jnp.zeros((zr, hw), BF16)
        def norm_bwd(i, ggg):
            r0 = pl.multiple_of((c0 + i * gc) * CHUNK, CHUNK)
            for acc in (dq_s, dk_s, dv_s):
                acc[pl.ds(r0, gr), :] = jnp.zeros((gr, acc.shape[1]), F32)
            o = o_ref[pl.ds(r0, gr), :].astype(F32)
            dy = dy_ref[pl.ds(r0, gr), :].astype(F32)
            r = p_ref[pl.ds(r0, gr), 2 * hk + hv:].astype(F32)
            rstd = lax.rsqrt(jnp.mean(o * o, axis=-1, keepdims=True) + EPS)
            ohat = o * rstd
            sg = _sigmoid(r)
            d_on = dy * (r * sg)
            d_ref[pl.ds(r0, gr), 2 * hk + hv:] = (dy * ohat * gg_ref[...] * (sg * (1.0 + r * (1.0 - sg)))).astype(BF16)
            d_oh = d_on * gg_ref[...]
            do_s[pl.ds(r0, gr), :] = (rstd * (d_oh - ohat * jnp.mean(d_oh * ohat, axis=-1, keepdims=True))).astype(BF16)
            return ggg + jnp.sum(d_on * ohat, axis=0, keepdims=True)

        ggg = lax.fori_loop(0, ng, norm_bwd, jnp.zeros((1, hv), F32))

        @pl.when((pl.program_id(0) == 0) & (h == 0))
        def _():
            ggg_ref[...] = jnp.zeros_like(ggg_ref)

        ggg_ref[0:1, :] += ggg

        def load(gi):
            r0 = pl.multiple_of((c0 + gi * gc) * CHUNK, CHUNK)
            blk = p_ref[pl.ds(r0, gr), :]
            return r0, blk[:, :hk].astype(F32) * scale, blk[:, hk:2 * hk].astype(F32), blk[:, 2 * hk:2 * hk + hv]

        zero = jnp.zeros((hv, hk), F32)

        def grad(gi, carry, fwd):
            dst, gb = carry
            way = 0 if fwd else 1
            mask = low_incl if fwd else up_strict
            r0, q, k, v = load(gi)
            b = b_ref[way, pl.ds(r0, gr), :]
            btot = _chunk_totals(b, fwd)
            eb, enb, edb, dec = jnp.exp(b), jnp.exp(-b), jnp.exp(btot - b), jnp.exp(btot)
            qi_f, ki_f, kd_f = q * eb, k * enb, k * edb
            qi, ki, kd = qi_f.astype(BF16), ki_f.astype(BF16), kd_f.astype(BF16)
            do = do_s[pl.ds(r0, gr), :]
            a = _dot_nt(qi, ki)
            da = _dot_nt(do, v)
            yield
            a = jnp.where(mask, a, 0.0).astype(BF16)
            da = jnp.where(mask, da, 0.0).astype(BF16)
            dv = _dot_tn(a, do)
            dqi = _dot(da, ki)
            dki = _dot_tn(da, qi)
            dv_c, dqi_c, dkd_c, extra_c = [None] * gc, [None] * gc, [None] * gc, [None] * gc
            chunk_rows = [slice(c * CHUNK, (c + 1) * CHUNK) for c in range(gc)]
            qdo = [_dot_tn(do[rows], qi[rows]) for rows in chunk_rows]
            for c in (reversed(range(gc)) if fwd else range(gc)):
                yield
                rows = chunk_rows[c]
                st = st_ref[0, 0, way, c0 + gi * gc + c]
                dsn_b = dst.astype(BF16)
                dec_c = dec[c * CHUNK:c * CHUNK + 1]
                dv_c[c] = dv[rows] + _dot_nt(kd[rows], dsn_b)
                dqi_c[c] = dqi[rows] + _dot(do[rows], st)
                dkd_c[c] = _dot(v[rows], dsn_b)
                ddec = jnp.sum(st.astype(F32) * dst, axis=0, keepdims=True)
                extra = jnp.sum(dkd_c[c] * kd_f[rows], axis=0, keepdims=True) + ddec * dec_c
                extra_c[c] = jnp.broadcast_to(extra, (CHUNK, hk))
                dst = dst * dec_c + qdo[c]
            yield
            dv, dqi = jnp.concatenate(dv_c, axis=0), jnp.concatenate(dqi_c, axis=0)
            dkd, extra = jnp.concatenate(dkd_c, axis=0), jnp.concatenate(extra_c, axis=0)
            dq_s[pl.ds(r0, gr), :] += dqi * eb * scale
            dk_s[pl.ds(r0, gr), :] += dki * enb + dkd * edb
            dv_s[pl.ds(r0, gr), :] += dv
            db = dqi * qi_f - dki * ki_f - dkd * kd_f
            dg = _chunk_cumsum(db, fwd) + extra
            yield
            dz = dg * gs_ref[way, pl.ds(r0, gr), :]
            dz_s[way, pl.ds(r0, gr), :] = dz.astype(BF16)
            return dst, gb + jnp.sum(dz, axis=0, keepdims=True)

        def grad_step(i, carry):
            return tuple(_interleave([grad(ng - 1 - i, carry[0], True), grad(i, carry[1], False)]))

        init = (zero, jnp.zeros((1, hk), F32))
        (_, gb_f), (_, gb_b) = lax.fori_loop(0, ng, grad_step, (init, init))
        used = slice(c0 * CHUNK, lp)
        for way, (w_ref, gw_ref, gb_ref, gb) in enumerate(((wf_ref, gwf_ref, gbf_ref, gb_f), (wb_ref, gwb_ref, gbb_ref, gb_b))):
            dlr_ref[used, :] += _dot_nt(dz_s[way, used, :], w_ref[...])
            gw_ref[0] = _dot_tn(lr_ref[used, :], dz_s[way, used, :])
            gb_ref[0] = jnp.zeros((8, hk), F32)
            gb_ref[0, 0:1, :] = gb

        def combine(i, carry):
            r0 = pl.multiple_of((c0 + i * gc) * CHUNK, CHUNK)
            d_ref[pl.ds(r0, gr), 0:hk] = dq_s[pl.ds(r0, gr), :].astype(BF16)
            d_ref[pl.ds(r0, gr), hk:2 * hk] = dk_s[pl.ds(r0, gr), :].astype(BF16)
            d_ref[pl.ds(r0, gr), 2 * hk:2 * hk + hv] = dv_s[pl.ds(r0, gr), :].astype(BF16)
            return carry

        lax.fori_loop(0, ng, combine, 0)

    head = lambda s, h: (s, h)
    wspec = pl.BlockSpec((LR_LANES, hk), lambda s, h: (0, h))
    gwspec = pl.BlockSpec((1, LR_LANES, hk), lambda s, h: (s, 0, h))
    gbspec = pl.BlockSpec((1, 8, hk), lambda s, h: (s, 0, h))
    gw_shape = jax.ShapeDtypeStruct((dm.Bl, LR_LANES, dm.DK), F32)
    gb_shape = jax.ShapeDtypeStruct((dm.Bl, 8, dm.DK), F32)
    both = pl.BlockSpec((2, lp, hk), lambda s, h: (0, s, h))
    return pl.pallas_call(
        body, name="gla_bwd", grid=(dm.Bl, HEADS),
        in_specs=[pl.BlockSpec((lp, hw), head), pl.BlockSpec((lp, LR_LANES), lambda s, h: (s, 0)),
                  pl.BlockSpec((lp, hv), head), pl.BlockSpec((lp, hv), head),
                  pl.BlockSpec((1, 1, 2, nc, hv, hk), lambda s, h: (s, h, 0, 0, 0, 0)), both, both,
                  wspec, wspec, pl.BlockSpec((1, hv), lambda s, h: (0, 0))],
        out_specs=[pl.BlockSpec((lp, hw), head), pl.BlockSpec((lp, LR_LANES), lambda s, h: (s, 0)),
                   gwspec, gbspec, gwspec, gbspec, pl.BlockSpec((8, hv), lambda s, h: (0, 0))],
        out_shape=[jax.ShapeDtypeStruct((dm.T, HEADS * hw), BF16), jax.ShapeDtypeStruct((dm.T, LR_LANES), F32),
                   gw_shape, gb_shape, gw_shape, gb_shape, jax.ShapeDtypeStruct((8, hv), F32)],
        scratch_shapes=[pltpu.VMEM((lp, hv), BF16), pltpu.VMEM((lp, hk), F32), pltpu.VMEM((lp, hk), F32),
                        pltpu.VMEM((lp, hv), F32), pltpu.VMEM((2, lp, hk), BF16)],
        compiler_params=_cp(2),
    )(proj_b, lr, o_all, dy_gla, states, decays, gate_slopes, wg_f, wg_b, gla_g)


def _stream_tiles(n_tiles, loads, stores, compute):
    for cp in loads(0, 0):
        cp.start()

    def step(t, carry):
        slot = t % 2

        @pl.when(t + 1 < n_tiles)
        def _():
            for cp in loads(t + 1, 1 - slot):
                cp.start()

        for cp in loads(t, slot):
            cp.wait()

        @pl.when(t >= 2)
        def _():
            for cp in stores(t - 2, slot):
                cp.wait()

        compute(t, slot)
        for cp in stores(t, slot):
            cp.start()
        return carry

    lax.fori_loop(0, n_tiles, step, 0)
    for t in range(max(n_tiles - 2, 0), n_tiles):
        for cp in stores(t, t % 2):
            cp.wait()


def _token_tiles(dm, target_rows=512):
    rows = _pick(dm.S, target_rows, 16)
    per_seq = dm.S // rows
    return rows, dm.Bl * per_seq, lambda t: pl.multiple_of((t // per_seq) * dm.LP + dm.TM + (t % per_seq) * rows, 16)


def _head(y_conv, y_gla, proj_c, w_oc, w_og, w_out, x, target, g_post, dm):
    d, tm = dm.D, dm.TM
    rows, n_tiles, first_row = _token_tiles(dm, 512)
    parts = 2 if rows % (2 * BF16_TILE_ROWS) == 0 else 1
    n_out = 8

    def body(*refs):
        yc_hbm, yg_hbm, c_hbm, woc_ref, wog_ref, wo_ref, x_hbm, t_hbm, g_ref = refs[:9]
        outs, st_ref = refs[9:9 + n_out], refs[9 + n_out]
        ycbuf, ygbuf, cbuf, xbuf, tbuf = refs[10 + n_out:15 + n_out]
        obufs = refs[15 + n_out:15 + 2 * n_out]
        zbuf, zbuf2, sem_in, sem_out, sem_zero = refs[15 + 2 * n_out:]

        def loads(t, slot):
            padded = [(yc_hbm, ycbuf), (yg_hbm, ygbuf), (c_hbm, cbuf)]
            own = [(x_hbm, xbuf), (t_hbm, tbuf)]
            return ([pltpu.make_async_copy(h.at[pl.ds(first_row(t), rows), :], b.at[slot], sem_in.at[i, slot])
                     for i, (h, b) in enumerate(padded)] +
                    [pltpu.make_async_copy(h.at[pl.ds(t * rows, rows), :], b.at[slot], sem_in.at[3 + i, slot])
                     for i, (h, b) in enumerate(own)])

        def stores(t, slot):
            return [pltpu.make_async_copy(b.at[slot], h.at[pl.ds(first_row(t), rows), :], sem_out.at[i, slot])
                    for i, (h, b) in enumerate(zip(outs, obufs))]

        def chain(slot, part):
            mg_o, do_o, dy_o, dpc_o, dpg_o, dc_o, dyc_o, dyg_o = obufs
            pc = _dot(ycbuf[slot, part], woc_ref[...])
            pg = _dot(ygbuf[slot, part], wog_ref[...])
            yield
            sa = _sigmoid(cbuf[slot, part, :d].astype(F32))
            sb = _sigmoid(cbuf[slot, part, d:].astype(F32))
            merged = (sa * pc + sb * pg).astype(BF16)
            mg_o[slot, part] = merged
            out = _dot(merged, wo_ref[...])
            yield
            rstd = lax.rsqrt(jnp.mean(out * out, axis=-1, keepdims=True) + EPS)
            ohat = out * rstd
            err = xbuf[slot, part] + ohat * g_ref[...] - tbuf[slot, part]
            dy = err * (1.0 / d)
            d_oh = dy * g_ref[...]
            d_out = (rstd * (d_oh - ohat * jnp.mean(d_oh * ohat, axis=-1, keepdims=True))).astype(BF16)
            do_o[slot, part] = d_out
            dy_o[slot, part] = dy.astype(BF16)
            st_ref[0:1, :] += jnp.sum(dy * ohat, axis=0, keepdims=True)
            st_ref[1:2, :] += jnp.sum(err * err, axis=0, keepdims=True)
            dmg = _dot_nt(d_out, wo_ref[...])
            yield
            dpc = (dmg * sa).astype(BF16)
            dpg = (dmg * sb).astype(BF16)
            dpc_o[slot, part] = dpc
            dpg_o[slot, part] = dpg
            dc_o[slot, part, :d] = (dmg * pc * sa * (1.0 - sa)).astype(BF16)
            dc_o[slot, part, d:] = (dmg * pg * sb * (1.0 - sb)).astype(BF16)
            dyc_o[slot, part] = _dot_nt(dpc, woc_ref[...]).astype(BF16)
            dyg_o[slot, part] = _dot_nt(dpg, wog_ref[...]).astype(BF16)

        def compute(t, slot):
            _interleave([chain(slot, pl.ds(i * (rows // parts), rows // parts)) for i in range(parts)])

        st_ref[...] = jnp.zeros_like(st_ref)
        zbuf[...] = jnp.zeros_like(zbuf)
        zbuf2[...] = jnp.zeros_like(zbuf2)
        zeros = [pltpu.make_async_copy(zbuf2 if out.shape[1] == 2 * d else zbuf, out.at[pl.ds(b * dm.LP, tm), :], sem_zero.at[i, b])
                 for i, out in enumerate(outs) for b in range(dm.Bl)]
        for cp in zeros:
            cp.start()
        _stream_tiles(n_tiles, loads, stores, compute)
        for cp in zeros:
            cp.wait()

    any_spec, vmem = pl.BlockSpec(memory_space=pl.ANY), pl.BlockSpec(memory_space=pltpu.VMEM)
    widths = [d, d, d, d, d, 2 * d, d, d]
    tile = lambda w, dt: pltpu.VMEM((2, rows, w), dt)
    return pl.pallas_call(
        body, name="head", in_specs=[any_spec] * 3 + [vmem] * 3 + [any_spec] * 2 + [vmem],
        out_specs=[any_spec] * n_out + [vmem],
        out_shape=[jax.ShapeDtypeStruct((dm.T, w), BF16) for w in widths] + [jax.ShapeDtypeStruct((8, d), F32)],
        scratch_shapes=[tile(d, BF16), tile(d, BF16), tile(2 * d, BF16), tile(d, F32), tile(d, F32)]
        + [tile(w, BF16) for w in widths]
        + [pltpu.VMEM((tm, d), BF16), pltpu.VMEM((tm, 2 * d), BF16), pltpu.SemaphoreType.DMA((5, 2)),
           pltpu.SemaphoreType.DMA((n_out, 2)), pltpu.SemaphoreType.DMA((n_out, dm.Bl))],
        compiler_params=pltpu.CompilerParams(vmem_limit_bytes=VMEM_LIMIT_BYTES),
    )(y_conv, y_gla, proj_c, w_oc, w_og, w_out, x.reshape(dm.Bl * dm.S, d), target.reshape(dm.Bl * dm.S, d), g_post)


def _grad_h(d_parts, gathered, dy, x, metapad, g_pre, dm):
    d, tm = dm.D, dm.TM
    rows, n_tiles, first_row = _token_tiles(dm, 256)
    widths = [a.shape[1] for a in d_parts]
    np_ = len(d_parts)

    def body(*refs):
        d_hbm, g_hbm, dy_hbm, x_hbm, mp_ref, g_ref = refs[:np_], refs[np_], refs[np_ + 1], refs[np_ + 2], refs[np_ + 3], refs[np_ + 4]
        gx_hbm, dmeta_ref, gg_ref = refs[np_ + 5:np_ + 8]
        parts, edges, sems = refs[np_ + 8:np_ + 12], refs[np_ + 12], refs[np_ + 13]
        dbufs = refs[np_ + 14:2 * np_ + 14]
        dybuf, xbuf, gbuf = refs[2 * np_ + 14:2 * np_ + 17]
        mbufs = refs[2 * np_ + 17:3 * np_ + 17]
        sem_in, sem_out, sem_meta = refs[3 * np_ + 17:]

        def grad_u(tiles):
            du = _dot(tiles[0].astype(BF16), parts[0][...])
            for a, w in zip(tiles[1:], parts[1:]):
                du = du + _dot(a.astype(BF16), w[...])
            return du

        def norm_bwd(h, du, dy):
            rstd = lax.rsqrt(jnp.mean(h * h, axis=-1, keepdims=True) + EPS)
            hhat = h * rstd
            dug = du * g_ref[...]
            gg_ref[0:1, :] += jnp.sum(du * hhat, axis=0, keepdims=True)
            return dy + rstd * (dug - hhat * jnp.mean(dug * hhat, axis=-1, keepdims=True))

        def loads(t, slot):
            padded = list(zip(d_hbm, dbufs)) + [(dy_hbm, dybuf)]
            return ([pltpu.make_async_copy(h.at[pl.ds(first_row(t), rows), :], b.at[slot], sem_in.at[i, slot])
                     for i, (h, b) in enumerate(padded)] +
                    [pltpu.make_async_copy(x_hbm.at[pl.ds(t * rows, rows), :], xbuf.at[slot], sem_in.at[np_ + 1, slot])])

        def stores(t, slot):
            return [pltpu.make_async_copy(gbuf.at[slot], gx_hbm.at[pl.ds(t * rows, rows), :], sem_out.at[slot])]

        def compute(t, slot):
            gbuf[slot] = norm_bwd(xbuf[slot], grad_u([b[slot] for b in dbufs]), dybuf[slot].astype(F32))

        gg_ref[...] = jnp.zeros_like(gg_ref)
        meta = [pltpu.make_async_copy(h.at[pl.ds(b * dm.LP, tm), :], buf.at[pl.ds(b * tm, tm), :], sem_meta.at[i, b])
                for i, (h, buf) in enumerate(zip(d_hbm, mbufs)) for b in range(dm.Bl)]
        for cp in meta:
            cp.start()
        _load_packed(g_hbm, parts, edges, sems, dm)
        _stream_tiles(n_tiles, loads, stores, compute)
        for cp in meta:
            cp.wait()
        dmeta_ref[...] = norm_bwd(jnp.concatenate([mp_ref[...]] * dm.Bl, axis=0), grad_u([buf[...] for buf in mbufs]), 0.0)

    any_spec, vmem = pl.BlockSpec(memory_space=pl.ANY), pl.BlockSpec(memory_space=pltpu.VMEM)
    grad_x, d_meta, gg = pl.pallas_call(
        body, name="grad_h", in_specs=[any_spec] * (np_ + 3) + [vmem, vmem], out_specs=[any_spec, vmem, vmem],
        out_shape=[jax.ShapeDtypeStruct((dm.Bl * dm.S, d), F32), jax.ShapeDtypeStruct((dm.Bl * tm, d), F32),
                   jax.ShapeDtypeStruct((8, d), F32)],
        scratch_shapes=_packed_scratch(dm)
        + [pltpu.VMEM((2, rows, w), a.dtype) for w, a in zip(widths, d_parts)]
        + [pltpu.VMEM((2, rows, d), BF16), pltpu.VMEM((2, rows, d), F32), pltpu.VMEM((2, rows, d), F32)]
        + [pltpu.VMEM((dm.Bl * tm, w), a.dtype) for w, a in zip(widths, d_parts)]
        + [pltpu.SemaphoreType.DMA((np_ + 2, 2)), pltpu.SemaphoreType.DMA((2,)), pltpu.SemaphoreType.DMA((np_, dm.Bl))],
        compiler_params=pltpu.CompilerParams(vmem_limit_bytes=VMEM_LIMIT_BYTES),
    )(*d_parts, gathered, dy, x.reshape(dm.Bl * dm.S, d), metapad, g_pre)
    return grad_x.reshape(dm.Bl, dm.S, d), d_meta.reshape(dm.Bl, tm, d), gg


def _adamw(partials, w, m, v, name, by_columns=False):
    rows_apart = w.ndim == 3
    r, c = w.shape[0], w.shape[-1]
    n_parts, pr = partials.shape[:2]
    assert pr == r or (by_columns and pr == _padded_shard_rows(r))
    assert by_columns or not rows_apart
    tr, tc = (r, _pick(c, 128, 128)) if by_columns else (_pick(r, 256, 16), c)

    def body(p_ref, w_ref, m_ref, v_ref, g_ref, d_ref, nm_ref, nv_ref):
        g = p_ref[0].astype(F32)
        for j in range(1, n_parts):
            g = g + p_ref[j].astype(F32)

        flat = lambda ref: ref.reshape(tr, tc) if rows_apart else ref

        def step(g):
            results = (g,) + _adam_step(g, *[flat(ref)[...] for ref in (w_ref, m_ref, v_ref)])
            for ref, val in zip((g_ref, d_ref, nm_ref, nv_ref), results):
                flat(ref)[...] = val

        if pr == r:
            step(g)
        else:
            me = 4 * lax.axis_index("x") + 2 * lax.axis_index("y") + lax.axis_index("c")
            for offset in sorted({_shard_offset(j, r) for j in range(N_DEV)}):
                @pl.when(_shard_offset(me, r) == offset)
                def _(offset=offset):
                    step(g[offset:offset + r])

    at = (lambda i: (0, i)) if by_columns else (lambda i: (i, 0))
    tile = pl.BlockSpec((tr, 1, tc), lambda i: (0, 0, i)) if rows_apart else pl.BlockSpec((tr, tc), at)
    out = jax.ShapeDtypeStruct(w.shape, F32)
    return pl.pallas_call(
        body, name=name, grid=(c // tc if by_columns else r // tr,),
        in_specs=[pl.BlockSpec((n_parts, pr if by_columns else tr, tc), lambda i: (0,) + at(i)), tile, tile, tile],
        out_specs=[tile, tile, tile, tile], out_shape=[out, out, out, out], compiler_params=_cp(1),
    )(partials, w, m, v)


def _adam_step(g, w, m, v):
    m2 = ADAM_B1 * m + (1.0 - ADAM_B1) * g
    v2 = ADAM_B2 * v + (1.0 - ADAM_B2) * (g * g)
    m_hat = m2 / (1.0 - ADAM_B1 ** ADAM_STEP)
    v_hat = v2 / (1.0 - ADAM_B2 ** ADAM_STEP)
    return -ADAM_LR * (m_hat / (jnp.sqrt(v_hat) + ADAM_EPS) + ADAM_WD * w), m2, v2


def _adamw_small(items, name):
    n = len(items)

    def body(*refs):
        ins, outs = refs[:4 * n], refs[4 * n:]
        for i in range(n):
            p_ref, w_ref, m_ref, v_ref = ins[4 * i:4 * i + 4]
            g = p_ref[0]
            for j in range(1, p_ref.shape[0]):
                g = g + p_ref[j]
            delta, m2, v2 = _adam_step(g, w_ref[...], m_ref[...], v_ref[...])
            for o_ref, val in zip(outs[4 * i:4 * i + 4], (g, delta, m2, v2)):
                o_ref[...] = val

    vmem = pl.BlockSpec(memory_space=pltpu.VMEM)
    res = pl.pallas_call(
        body, name=name, in_specs=[vmem] * (4 * n), out_specs=[vmem] * (4 * n),
        out_shape=[jax.ShapeDtypeStruct(w.shape, F32) for _, w, _, _ in items for _ in range(4)],
    )(*[a for item in items for a in item])
    return [res[4 * i:4 * i + 4] for i in range(n)]


def _unpack_moves(dm):
    d, hk, hv, cw, nj, hw = dm.D, dm.HK, dm.HV, dm.CW, dm.NJ, dm.HW
    moves = [((4 * j + part) * cw, cw, part * d + j * cw) for j in range(nj) for part in range(4)]
    q0 = 4 * d
    k0, v0 = q0 + HEADS * hk, q0 + 2 * HEADS * hk
    r0 = v0 + HEADS * hv
    lr0 = r0 + HEADS * hv
    for h in range(HEADS):
        b0 = 4 * d + h * hw
        moves += [(b0, hk, q0 + h * hk), (b0 + hk, hk, k0 + h * hk), (b0 + 2 * hk, hv, v0 + h * hv),
                  (b0 + 2 * hk + hv, hv, r0 + h * hv)]
    moves.append((4 * d + HEADS * hw, 2 * d, lr0 + 2 * RANK))
    return moves, lr0


def _column_shards(g, shard_shape):
    r, c = g.shape
    return g.reshape(r, N_DEV, c // N_DEV).transpose(1, 0, 2).reshape((N_DEV,) + tuple(shard_shape))


def _join_column_shards(parts):
    r, c = parts.shape[-2:]
    return parts.reshape(N_DEV, r, c).transpose(1, 0, 2).reshape(r, N_DEV * c)


def _local_step(x, target, meta, g_pre, u, wt_shards, conv_w, wg_f, bg_f, wg_b, bg_b, gla_g, out_weights, g_post,
                on_matrix_grads=None):
    bl, s, d = x.shape
    dm = _Dims(bl, s, d)
    metapad = jnp.concatenate([jnp.zeros((dm.TM - N_META, d), F32), meta], axis=0)
    wgp_f = jnp.pad(wg_f, ((0, LR_LANES - RANK), (0, 0))).astype(BF16)
    wgp_b = jnp.pad(wg_b, ((RANK, LR_LANES - 2 * RANK), (0, 0))).astype(BF16)

    u = _prenorm_meta(u, metapad, g_pre, dm)
    proj_a, proj_b, proj_c, lr = _inproj(u, wt_shards, dm)
    y_conv = _conv_fwd(proj_a, conv_w, dm)
    o_all, y_gla, states, decays, gate_slopes = _gla_fwd(proj_b, lr, wgp_f, bg_f, wgp_b, bg_b, gla_g, dm)
    w_oc, w_og, w_out = out_weights(y_conv) if callable(out_weights) else out_weights
    merged, d_out, dy, d_pc, d_pg, d_c, dy_conv, dy_gla, stats = _head(y_conv, y_gla, proj_c, w_oc, w_og, w_out, x, target,
                                                                        g_post, dm)

    g_out = _matmul_tn(merged, d_out, BF16, "grad_w_out")
    g_oc = _matmul_tn(y_conv, d_pc, BF16, "grad_w_out_conv")
    g_og = _matmul_tn(y_gla, d_pg, BF16, "grad_w_out_gla")
    if on_matrix_grads is not None:
        conv_w = conv_w + on_matrix_grads(dict(w_out_conv=g_oc, w_out_gla=g_og, w_merge_out=g_out))
    d_a, g_conv = _conv_bwd(proj_a, dy_conv, conv_w, dm)
    d_b, d_lr, gwp_f, gbp_f, gwp_b, gbp_b, g_gla = _gla_bwd(proj_b, lr, o_all, dy_gla, states, decays, gate_slopes, wgp_f, wgp_b, gla_g, dm)
    moves, lr_at = _unpack_moves(dm)
    g_in = _matmul_tn_group([d_a, d_b, d_c], u, moves, 9 * d + 2 * RANK, (d_lr, 2 * RANK, lr_at), "grad_w_in", tile=d)
    if on_matrix_grads is not None:
        g_pre = g_pre + on_matrix_grads(dict(w_in=g_in))
    grad_x, d_meta, g_pre_rows = _grad_h([d_a, d_b, d_c, d_lr], wt_shards, dy, x, metapad, g_pre, dm)

    grads = dict(
        meta_tokens=jnp.sum(d_meta[:, dm.TM - N_META:, :], axis=0), norm_pre=g_pre_rows[0:1], w_in=g_in,
        conv_w=g_conv[0:3], w_gate_fwd=jnp.sum(gwp_f, axis=0)[:RANK], b_gate_fwd=jnp.sum(gbp_f, axis=0)[0:1],
        w_gate_bwd=jnp.sum(gwp_b, axis=0)[RANK:2 * RANK], b_gate_bwd=jnp.sum(gbp_b, axis=0)[0:1],
        gla_norm=g_gla[0:1], w_out_conv=g_oc, w_out_gla=g_og, w_merge_out=g_out, norm_post=stats[0:1])
    return stats[1:2], grad_x, grads


MATRICES = ("w_out_conv", "w_out_gla", "w_merge_out")
SMALL_SHARDED = ("meta_tokens", "conv_w", "w_gate_fwd", "w_gate_bwd")
REPLICATED = ("norm_pre", "b_gate_fwd", "b_gate_bwd", "gla_norm", "norm_post")
NAMES = ("meta_tokens", "norm_pre", "w_in", "conv_w", "w_gate_fwd", "b_gate_fwd", "w_gate_bwd", "b_gate_bwd", "gla_norm",
         "w_out_conv", "w_out_gla", "w_merge_out", "norm_post")


def kernel(x, meta_tokens, norm_pre, w_in, conv_w, w_gate_fwd, b_gate_fwd, w_gate_bwd, b_gate_bwd, gla_norm, w_out_conv, w_out_gla, w_merge_out, norm_post, loss_target, m_meta_tokens, m_norm_pre, m_w_in, m_conv_w, m_w_gate_fwd, m_b_gate_fwd, m_w_gate_bwd, m_b_gate_bwd, m_gla_norm, m_w_out_conv, m_w_out_gla, m_w_merge_out, m_norm_post, v_meta_tokens, v_norm_pre, v_w_in, v_conv_w, v_w_gate_fwd, v_b_gate_fwd, v_w_gate_bwd, v_b_gate_bwd, v_gla_norm, v_w_out_conv, v_w_out_gla, v_w_merge_out, v_norm_post):
    w = dict(meta_tokens=meta_tokens, norm_pre=norm_pre, w_in=w_in[0], conv_w=conv_w, w_gate_fwd=w_gate_fwd,
             b_gate_fwd=b_gate_fwd, w_gate_bwd=w_gate_bwd, b_gate_bwd=b_gate_bwd, gla_norm=gla_norm,
             w_out_conv=w_out_conv[0], w_out_gla=w_out_gla[0], w_merge_out=w_merge_out[0], norm_post=norm_post)
    m = dict(meta_tokens=m_meta_tokens, norm_pre=m_norm_pre, w_in=m_w_in[0], conv_w=m_conv_w, w_gate_fwd=m_w_gate_fwd,
             b_gate_fwd=m_b_gate_fwd, w_gate_bwd=m_w_gate_bwd, b_gate_bwd=m_b_gate_bwd, gla_norm=m_gla_norm,
             w_out_conv=m_w_out_conv[0], w_out_gla=m_w_out_gla[0], w_merge_out=m_w_merge_out[0], norm_post=m_norm_post)
    v = dict(meta_tokens=v_meta_tokens, norm_pre=v_norm_pre, w_in=v_w_in[0], conv_w=v_conv_w, w_gate_fwd=v_w_gate_fwd,
             b_gate_fwd=v_b_gate_fwd, w_gate_bwd=v_w_gate_bwd, b_gate_bwd=v_b_gate_bwd, gla_norm=v_gla_norm,
             w_out_conv=v_w_out_conv[0], w_out_gla=v_w_out_gla[0], w_merge_out=v_w_merge_out[0], norm_post=v_norm_post)
    d = x.shape[-1]

    dm = _Dims(*x.shape)
    me = 4 * lax.axis_index("x") + 2 * lax.axis_index("y") + lax.axis_index("c")
    wt_shards, *small_all, u = _gather_two_level(
        [_pad_shard(w["w_in"].T.astype(BF16), me)] + [w[n] for n in SMALL_SHARDED], "gather_weights",
        _prenorm_tokens_side(x, norm_pre, dm))
    started, late_weights = _exchange_start([w[n].astype(BF16) for n in MATRICES], [], small_all[0], "gather_out_weights_start")
    small = {n: _join_column_shards(p) for n, p in zip(SMALL_SHARDED, small_all)}
    small["meta_tokens"] = small["meta_tokens"] + started

    def out_weights(after):
        return tuple(a.reshape(-1, d) for a in _exchange_wait(late_weights, after, "gather_out_weights_wait"))

    pending = []

    def on_matrix_grads(g):
        blocks = [t.reshape(N_DEV, -1, d) if t.shape[0] % (N_DEV * BF16_TILE_ROWS) == 0 else (t, t.shape[0] // N_DEV)
                  for t in g.values()]
        token, state = _exchange_start([], blocks, None, "exchange_grads_start_" + "_".join(g))
        pending.append((tuple(g), state))
        return token

    sq_err_cols, grad_x, grads = _local_step(
        x, loss_target, small["meta_tokens"], norm_pre, u, wt_shards, small["conv_w"], small["w_gate_fwd"], b_gate_fwd,
        small["w_gate_bwd"], b_gate_bwd, gla_norm, out_weights, norm_post, on_matrix_grads)
    received = {}
    for names, state in pending:
        received.update(zip(names, _exchange_wait(state, grad_x, "exchange_grads_wait_" + "_".join(names))))

    exchanged = _exchange([grads[n] for n in REPLICATED] + [sq_err_cols],
                          [_column_shards(grads[n], w[n].shape) for n in SMALL_SHARDED], "exchange_small_grads")
    small_recv = exchanged[:len(REPLICATED)] + exchanged[len(REPLICATED) + 1:]
    loss = 0.5 / d * jnp.sum(exchanged[len(REPLICATED)])

    rows_apart = lambda a: jnp.transpose(a, (2, 0, 1))
    results = {"w_in": [jnp.transpose(r, (1, 2, 0)) for r in _adamw(received["w_in"], rows_apart(w_in), rows_apart(m_w_in),
                                                                     rows_apart(v_w_in), "adamw_w_in", by_columns=True)]}
    for n in MATRICES:
        results[n] = [r[None] for r in _adamw(received[n], w[n], m[n], v[n], "adamw_" + n)]
    small_names = REPLICATED + SMALL_SHARDED
    results.update(zip(small_names, _adamw_small([(p, w[n], m[n], v[n]) for n, p in zip(small_names, small_recv)], "adamw_small")))
    return (loss, grad_x, *[results[n][i] for i in range(4) for n in NAMES])
```

```python
import jax
import jax.numpy as jnp
from jax import lax
from jax.experimental import pallas as pl
from jax.experimental.pallas import tpu as pltpu

F32 = jnp.float32
BF16 = jnp.bfloat16
MESH = pl.DeviceIdType.MESH

N_META = 16
CHUNK = 64
CHUNK_SHIFT = 6
HEADS = 4
RANK = 16
LR_LANES = 128
PAD_ROWS = CHUNK - N_META
EPS = 1e-6
GATE_NORMALIZER = 16.0
N_DEV = 8
ADAM_LR, ADAM_B1, ADAM_B2, ADAM_EPS, ADAM_WD, ADAM_STEP = 0.001, 0.9, 0.999, 1e-08, 0.01, 10
VMEM_LIMIT_BYTES = 56 * 1024 * 1024


class _Dims:
    def __init__(self, bl, s, d):
        self.Bl, self.S, self.D = bl, s, d
        self.TM = CHUNK
        self.LP = self.TM + s
        self.T = bl * self.LP
        self.TPS = self.LP // self.TM
        self.NC = self.LP // CHUNK
        self.C0 = (self.TM - CHUNK) // CHUNK
        self.DK, self.DV = d // 2, d
        self.HK, self.HV = self.DK // HEADS, self.DV // HEADS
        self.HW = 2 * self.HK + 2 * self.HV
        self.CW = 256 if d % 256 == 0 and d > 256 else d // 4
        self.NJ = d // self.CW


def _pick(n, target, mult):
    t = min(n, target)
    while t >= mult:
        if n % t == 0 and t % mult == 0:
            return t
        t -= mult
    return n


def _cp(n_axes):
    return pltpu.CompilerParams(dimension_semantics=("arbitrary",) * n_axes, vmem_limit_bytes=VMEM_LIMIT_BYTES)


def _sigmoid(x):
    return 1.0 / (1.0 + jnp.exp(-x))


def _dot(a, b):
    return jnp.dot(a, b, preferred_element_type=F32)


def _dot_nt(a, b):
    return lax.dot_general(a, b, (((1,), (1,)), ((), ())), preferred_element_type=F32)


def _dot_tn(a, b):
    return lax.dot_general(a, b, (((0,), (0,)), ((), ())), preferred_element_type=F32)


def _chunk_cumsum(x, reverse):
    rows = x.shape[0]
    r = lax.broadcasted_iota(jnp.int32, x.shape, 0) & (CHUNK - 1)
    step = 1
    while step < CHUNK:
        if reverse:
            x = x + jnp.where(r < CHUNK - step, pltpu.roll(x, rows - step, 0), 0.0)
        else:
            x = x + jnp.where(r >= step, pltpu.roll(x, step, 0), 0.0)
        step *= 2
    return x


def _exchange(gathers, scatters, name):
    arrays = list(gathers) + list(scatters)
    n, ng = len(arrays), len(gathers)

    def body(*refs):
        ins, outs = refs[:n], refs[n:2 * n]
        send_sems, recv_sems, local_sems = refs[2 * n:]
        x, y, c = lax.axis_index("x"), lax.axis_index("y"), lax.axis_index("c")
        me = 4 * x + 2 * y + c
        started = []
        for t in range(n):
            src, dst = ins[t], outs[t]
            own = pltpu.make_async_copy(src if t < ng else src.at[me], dst.at[me], local_sems.at[t])
            own.start()
            started.append(own)
            for k, pos, peer in _peers(x, y, c):
                cp = pltpu.make_async_remote_copy(
                    src_ref=src if t < ng else src.at[peer], dst_ref=dst.at[me],
                    send_sem=send_sems.at[t * (N_DEV - 1) + k - 1], recv_sem=recv_sems.at[t * (N_DEV - 1) + k - 1],
                    device_id=pos, device_id_type=MESH)
                cp.start()
                started.append(cp)
        for cp in started:
            cp.wait()

    out_shape = [jax.ShapeDtypeStruct((N_DEV,) + a.shape if t < ng else a.shape, a.dtype) for t, a in enumerate(arrays)]
    any_spec = pl.BlockSpec(memory_space=pl.ANY)
    return pl.pallas_call(
        body, name=name, out_shape=out_shape, in_specs=[any_spec] * n, out_specs=[any_spec] * n,
        scratch_shapes=[pltpu.SemaphoreType.DMA((n * (N_DEV - 1),)), pltpu.SemaphoreType.DMA((n * (N_DEV - 1),)),
                        pltpu.SemaphoreType.DMA((n,))],
        compiler_params=pltpu.CompilerParams(has_side_effects=True),
    )(*arrays)


def _gather_two_level(arrays, name, side=None):
    n = len(arrays)
    per = N_DEV - 1
    work, side_in, side_in_specs, side_out, side_out_specs, side_scratch = side or (None, [], [], [], [], [])
    n_in, n_out = len(side_in), len(side_out)

    def body(*refs):
        ins, outs = refs[:n], refs[n + n_in:2 * n + n_in]
        send_sems, recv_sems, local_sems = refs[2 * n + n_in + n_out:2 * n + n_in + n_out + 3]
        x, y, c = lax.axis_index("x"), lax.axis_index("y"), lax.axis_index("c")
        sibling = (x, y, 1 - c)
        chips = [(1 - x, y), (x, 1 - y), (1 - x, 1 - y)]
        index = lambda px, py, pc: 4 * px + 2 * py + pc

        def copy(t, k, block, to, from_input=False):
            slab = outs[t].at[index(*block)]
            return pltpu.make_async_remote_copy(
                src_ref=ins[t] if from_input else slab, dst_ref=slab, send_sem=send_sems.at[t * per + k],
                recv_sem=recv_sems.at[t * per + k], device_id=to, device_id_type=MESH)

        own, sent = [], []
        for t in range(n):
            own.append(pltpu.make_async_copy(ins[t], outs[t].at[index(x, y, c)], local_sems.at[t]))
            own[-1].start()
            first = [copy(t, 0, (x, y, c), sibling, True)]
            first += [copy(t, 1 + j, (x, y, c), (*chip, c), True) for j, chip in enumerate(chips)]
            for cp in first:
                cp.start()
            sent += first
        if work is not None:
            work(refs[n:n + n_in], refs[2 * n + n_in:2 * n + n_in + n_out], refs[2 * n + n_in + n_out + 3:])
        for t in range(n):
            for j, chip in enumerate(chips):
                copy(t, 1 + j, (*chip, c), (x, y, c)).wait_recv()
                sent.append(copy(t, 4 + j, (*chip, c), sibling))
                sent[-1].start()
        for t in range(n):
            copy(t, 0, sibling, (x, y, c)).wait_recv()
            for j, chip in enumerate(chips):
                copy(t, 4 + j, (*chip, 1 - c), (x, y, c)).wait_recv()
        for cp in sent:
            cp.wait_send()
        for cp in own:
            cp.wait()

    out_shape = [jax.ShapeDtypeStruct((N_DEV,) + a.shape, a.dtype) for a in arrays]
    any_spec = pl.BlockSpec(memory_space=pl.ANY)
    return pl.pallas_call(
        body, name=name, out_shape=out_shape + list(side_out), in_specs=[any_spec] * n + list(side_in_specs),
        out_specs=[any_spec] * n + list(side_out_specs),
        scratch_shapes=[pltpu.SemaphoreType.DMA((n * per,)), pltpu.SemaphoreType.DMA((n * per,)),
                        pltpu.SemaphoreType.DMA((n,))] + list(side_scratch),
        compiler_params=pltpu.CompilerParams(has_side_effects=True, vmem_limit_bytes=VMEM_LIMIT_BYTES),
    )(*arrays, *side_in)


def _peers(x, y, c):
    out = []
    for k in range(1, N_DEV):
        px = 1 - x if (k >> 2) & 1 else x
        py = 1 - y if (k >> 1) & 1 else y
        pc = 1 - c if k & 1 else c
        out.append((k, (px, py, pc), 4 * px + 2 * py + pc))
    return out


def _exchange_start(gathers, scatters, after, name):
    shard_rows = [None] * len(gathers) + [s[1] if isinstance(s, tuple) else None for s in scatters]
    arrays = list(gathers) + [s[0] if isinstance(s, tuple) else s for s in scatters]
    n, ng = len(arrays), len(gathers)
    hbm = pl.BlockSpec(memory_space=pltpu.HBM)
    sem = pl.BlockSpec(memory_space=pltpu.SEMAPHORE)

    extra = [] if after is None else [after]
    ne = len(extra)

    def body(*refs):
        ins, lands = refs[:n], refs[n:2 * n]
        send_sems, recv_sems = refs[2 * n + ne], refs[2 * n + ne + 1]
        token = refs[4 * n + ne + 2]
        x, y, c = lax.axis_index("x"), lax.axis_index("y"), lax.axis_index("c")
        me = 4 * x + 2 * y + c
        for t in range(n):
            for k, pos, peer in _peers(x, y, c):
                pltpu.make_async_remote_copy(
                    src_ref=_block_for(ins[t], peer, t < ng, shard_rows[t]), dst_ref=lands[t].at[me],
                    send_sem=send_sems.at[t * (N_DEV - 1) + k - 1], recv_sem=recv_sems.at[t * (N_DEV - 1) + k - 1],
                    device_id=pos, device_id_type=MESH).start()
        token[...] = jnp.zeros_like(token)

    me = 4 * lax.axis_index("x") + 2 * lax.axis_index("y") + lax.axis_index("c")

    def own_block(t, a):
        if t < ng:
            return a
        if shard_rows[t] is None:
            return lax.dynamic_index_in_dim(a, me, 0, keepdims=False)
        assert all(_shard_window(j, shard_rows[t]) + _padded_shard_rows(shard_rows[t]) <= a.shape[0] for j in range(N_DEV))
        return lax.dynamic_slice_in_dim(a, _shard_window(me, shard_rows[t]), _padded_shard_rows(shard_rows[t]), 0)

    blocks = [own_block(t, a) for t, a in enumerate(arrays)]
    lands = [lax.dynamic_update_index_in_dim(lax.empty((N_DEV,) + b.shape if t < ng or shard_rows[t] else a.shape, a.dtype), b, me, 0)
             for t, (a, b) in enumerate(zip(arrays, blocks))]
    operands = [pltpu.with_memory_space_constraint(a, pltpu.HBM) for a in arrays + lands]
    sems = pltpu.SemaphoreType.DMA((n * (N_DEV - 1),))
    res = pl.pallas_call(
        body, name=name,
        out_shape=(sems, sems, *[pltpu.HBM(a.shape, a.dtype) for a in arrays + lands], jax.ShapeDtypeStruct((8, 128), F32)),
        in_specs=[hbm] * (2 * n) + [pl.BlockSpec(memory_space=pl.ANY)] * ne,
        out_specs=(sem, sem, *[hbm] * (2 * n), pl.BlockSpec(memory_space=pltpu.VMEM)),
        input_output_aliases={i: 2 + i for i in range(2 * n)},
        compiler_params=pltpu.CompilerParams(has_side_effects=pltpu.SideEffectType.DATAFLOW_SIDE_EFFECTING),
    )(*operands, *extra)
    return res[-1][0, 0], (ng, shard_rows, res[0], res[1], list(res[2:2 + n]), list(res[2 + n:2 + 2 * n]))


def _block_for(ref, peer, whole, shard_rows):
    if whole:
        return ref
    if shard_rows is None:
        return ref.at[peer]
    return ref.at[pl.ds(pl.multiple_of(_shard_window(peer, shard_rows), BF16_TILE_ROWS), _padded_shard_rows(shard_rows))]


def _exchange_wait(state, after, name):
    ng, shard_rows, send_sems, recv_sems, sent, lands = state
    n = len(sent)
    hbm = pl.BlockSpec(memory_space=pltpu.HBM)
    sem = pl.BlockSpec(memory_space=pltpu.SEMAPHORE)

    def body(*refs):
        ins, land_refs = refs[:n], refs[n:2 * n]
        send_ref, recv_ref = refs[2 * n], refs[2 * n + 1]
        x, y, c = lax.axis_index("x"), lax.axis_index("y"), lax.axis_index("c")
        me = 4 * x + 2 * y + c
        for t in range(n):
            for k, pos, peer in _peers(x, y, c):
                cp = pltpu.make_async_remote_copy(
                    src_ref=_block_for(ins[t], peer, t < ng, shard_rows[t]), dst_ref=land_refs[t].at[me],
                    send_sem=send_ref.at[t * (N_DEV - 1) + k - 1], recv_sem=recv_ref.at[t * (N_DEV - 1) + k - 1],
                    device_id=pos, device_id_type=MESH)
                cp.wait_send()
                cp.wait_recv()

    res = pl.pallas_call(
        body, name=name, out_shape=tuple(pltpu.HBM(a.shape, a.dtype) for a in sent + lands),
        in_specs=[hbm] * (2 * n) + [sem, sem, pl.BlockSpec(memory_space=pl.ANY)], out_specs=tuple([hbm] * (2 * n)),
        input_output_aliases={i: i for i in range(2 * n)},
        compiler_params=pltpu.CompilerParams(has_side_effects=pltpu.SideEffectType.DATAFLOW_SIDE_EFFECTING),
    )(*sent, *lands, send_sems, recv_sems, after)
    return list(res[n:])


def _rms_scaled(h, g):
    return (h * lax.rsqrt(jnp.mean(h * h, axis=-1, keepdims=True) + EPS) * g).astype(BF16)


def _prenorm_tokens_side(x, g_pre, dm):
    bl, s, d = x.shape
    rows = _pick(s, 512, 16)
    tiles = [(b, j) for b in range(bl) for j in range(s // rows)]

    def work(ins, outs, scratch):
        (x_ref, g_ref), (u_ref,), (xbuf, ubuf, sem_in, sem_out) = ins, outs, scratch

        def load(t, slot):
            b, j = tiles[t]
            return pltpu.make_async_copy(x_ref.at[b, pl.ds(j * rows, rows), :], xbuf.at[slot], sem_in.at[slot])

        def store(t, slot):
            b, j = tiles[t]
            return pltpu.make_async_copy(ubuf.at[slot], u_ref.at[pl.ds(b * dm.LP + dm.TM + j * rows, rows), :], sem_out.at[slot])

        load(0, 0).start()
        for t in range(len(tiles)):
            slot = t % 2
            if t + 1 < len(tiles):
                load(t + 1, 1 - slot).start()
            load(t, slot).wait()
            if t >= 2:
                store(t - 2, slot).wait()
            ubuf[slot] = _rms_scaled(xbuf[slot], g_ref[...])
            store(t, slot).start()
        for t in range(max(len(tiles) - 2, 0), len(tiles)):
            store(t, t % 2).wait()

    any_spec = pl.BlockSpec(memory_space=pl.ANY)
    return (work, [x, g_pre], [any_spec, pl.BlockSpec(memory_space=pltpu.VMEM)],
            [jax.ShapeDtypeStruct((dm.T, d), BF16)], [any_spec],
            [pltpu.VMEM((2, rows, d), F32), pltpu.VMEM((2, rows, d), BF16), pltpu.SemaphoreType.DMA((2,)),
             pltpu.SemaphoreType.DMA((2,))])


def _prenorm_meta(u, metapad, g_pre, dm):
    tm, tps, d = dm.TM, dm.TPS, dm.D

    def body(u_in, mp_ref, g_ref, u_ref):
        u_ref[...] = _rms_scaled(mp_ref[...], g_ref[...])

    return pl.pallas_call(
        body, name="prenorm_meta", grid=(dm.Bl,),
        in_specs=[pl.BlockSpec(memory_space=pl.ANY), pl.BlockSpec((tm, d), lambda i: (0, 0)),
                  pl.BlockSpec((1, d), lambda i: (0, 0))],
        out_specs=pl.BlockSpec((tm, d), lambda i: (i * tps, 0)),
        out_shape=jax.ShapeDtypeStruct((dm.T, d), BF16), input_output_aliases={0: 0}, compiler_params=_cp(1),
    )(u, metapad, g_pre)


def _matmul_tn(a, b, out_dtype, name, tt=2816, tn=1024, tk=1024):
    t, k = a.shape
    n = b.shape[1]
    tt, tn, tk = _pick(t, tt, 16), _pick(n, tn, 128), _pick(k, tk, 128)
    nt = t // tt

    def body(a_ref, b_ref, o_ref, acc):
        p = _dot_tn(a_ref[...].astype(BF16), b_ref[...].astype(BF16))
        i = pl.program_id(2)

        @pl.when(i == 0)
        def _():
            acc[...] = p

        @pl.when(i > 0)
        def _():
            acc[...] += p

        @pl.when(i == nt - 1)
        def _():
            o_ref[...] = acc[...].astype(out_dtype)

    return pl.pallas_call(
        body, name=name, grid=(k // tk, n // tn, nt),
        in_specs=[pl.BlockSpec((tt, tk), lambda kk, j, i: (i, kk)), pl.BlockSpec((tt, tn), lambda kk, j, i: (i, j))],
        out_specs=pl.BlockSpec((tk, tn), lambda kk, j, i: (kk, j)),
        out_shape=jax.ShapeDtypeStruct((k, n), out_dtype), scratch_shapes=[pltpu.VMEM((tk, tn), F32)],
        compiler_params=_cp(3),
    )(a, b)


def _matmul_tn_group(a_list, b, moves, out_rows, extra, name, tt=2816, tile=1024):
    t, n = b.shape
    tt = _pick(t, tt, 16)
    nt = t // tt
    counts = [a.shape[1] // tile for a in a_list]
    starts = [sum(counts[:m]) for m in range(len(a_list))]
    items = sum(counts)
    extra_a, extra_rows, extra_at = extra
    extra_w = extra_a.shape[1]
    cuts = [[] for _ in range(items)]
    for row, rows, at in moves:
        while rows > 0:
            p, r = divmod(row, tile)
            take = min(rows, tile - r)
            cuts[p].append((r, take, at))
            row, rows, at = row + take, rows - take, at + take
    assert all(v % BF16_TILE_ROWS == 0 for cut in cuts for move in cut for v in move)
    assert sum(rows for _, rows, _ in moves) + extra_rows == out_rows and extra_rows % BF16_TILE_ROWS == 0

    def active(p, m):
        return (p >= starts[m]) & (p < starts[m] + counts[m])

    def body(*refs):
        a_refs, b_ref, x_ref = refs[:len(a_list)], refs[len(a_list)], refs[len(a_list) + 1]
        o_ref, acc, stage, sems, x_sem, abuf, a_sems, x_acc, x_stage = refs[-9:]
        p, i = pl.program_id(0), pl.program_id(1)

        def fetch(p, i, slot, m):
            cols = pl.ds(pl.multiple_of((p - starts[m]) * tile, tile), tile)
            return pltpu.make_async_copy(a_refs[m].at[pl.ds(pl.multiple_of(i * tt, tt), tt), cols], abuf.at[slot], a_sems.at[slot])

        def start_fetch(p, i, slot):
            for m in range(len(a_list)):
                @pl.when(active(p, m))
                def _(m=m):
                    fetch(p, i, slot, m).start()

        step = p * nt + i
        slot = step % 2

        @pl.when(step == 0)
        def _():
            start_fetch(p, i, slot)

        @pl.when(step + 1 < items * nt)
        def _():
            last = i == nt - 1
            start_fetch(jnp.where(last, p + 1, p), jnp.where(last, 0, i + 1), 1 - slot)

        pltpu.make_async_copy(a_refs[0].at[pl.ds(0, tt), pl.ds(0, tile)], abuf.at[slot], a_sems.at[slot]).wait()

        def writes(item):
            return [pltpu.make_async_copy(stage.at[pl.ds(r, rows), :], o_ref.at[pl.ds(at, rows), :], sems.at[s])
                    for s, (r, rows, at) in enumerate(cuts[item])]

        extra_copy = pltpu.make_async_copy(x_stage, o_ref.at[pl.ds(extra_at, extra_rows), :], x_sem.at[0])

        @pl.when(p == 0)
        def _():
            x_prod = _dot_tn(x_ref[...].astype(BF16), b_ref[...])

            @pl.when(i == 0)
            def _():
                x_acc[...] = x_prod

            @pl.when(i > 0)
            def _():
                x_acc[...] += x_prod

            @pl.when(i == nt - 1)
            def _():
                x_stage[...] = x_acc[...].astype(BF16)[:extra_rows]
                extra_copy.start()

        prod = _dot_tn(abuf[slot], b_ref[...])

        @pl.when(i == 0)
        def _():
            acc[...] = prod

        @pl.when(i > 0)
        def _():
            acc[...] += prod

        for item in range(items):
            @pl.when((p == item) & (i == nt - 1))
            def _(item=item):
                if item > 0:
                    for cp in writes(item - 1):
                        cp.wait()
                stage[...] = acc[...].astype(BF16)
                for cp in writes(item):
                    cp.start()
                if item == items - 1:
                    for cp in writes(item):
                        cp.wait()
                    extra_copy.wait()

    assert all(a.dtype == BF16 for a in a_list) and b.dtype == BF16
    any_spec = pl.BlockSpec(memory_space=pl.ANY)
    return pl.pallas_call(
        body, name=name, grid=(items, nt),
        in_specs=[any_spec] * len(a_list) + [pl.BlockSpec((tt, n), lambda p, i: (i, 0)),
                                             pl.BlockSpec((tt, extra_w), lambda p, i: (jnp.where(p == 0, i, nt - 1), 0))],
        out_specs=any_spec, out_shape=jax.ShapeDtypeStruct((out_rows, n), BF16),
        scratch_shapes=[pltpu.VMEM((tile, n), F32), pltpu.VMEM((tile, n), BF16),
                        pltpu.SemaphoreType.DMA((max(len(cut) for cut in cuts),)), pltpu.SemaphoreType.DMA((1,)),
                        pltpu.VMEM((2, tt, tile), BF16), pltpu.SemaphoreType.DMA((2,)),
                        pltpu.VMEM((extra_w, n), F32), pltpu.VMEM((extra_rows, n), BF16)],
        compiler_params=_cp(2),
    )(*a_list, b, extra_a)


BF16_TILE_ROWS = 16


def _shard_offset(index, shard_rows):
    return (index * shard_rows) % BF16_TILE_ROWS


def _padded_shard_rows(shard_rows):
    return -(-(shard_rows + max(_shard_offset(j, shard_rows) for j in range(N_DEV))) // BF16_TILE_ROWS) * BF16_TILE_ROWS


def _pad_shard(wt_shard, index):
    rows, d = wt_shard.shape
    return lax.dynamic_update_slice(jnp.zeros((_padded_shard_rows(rows), d), wt_shard.dtype), wt_shard,
                                    (_shard_offset(index, rows), 0))


def _shard_window(index, shard_rows):
    return index * shard_rows - _shard_offset(index, shard_rows)


def _packed_parts(dm):
    d, dk, hk, hv, cw, nj, hw = dm.D, dm.DK, dm.HK, dm.HV, dm.CW, dm.NJ, dm.HW
    blocks = [(0, (j * 4 + p) * cw, p * d + j * cw, cw) for j in range(nj) for p in range(4)]
    for h in range(HEADS):
        blocks += [(1, h * hw, 4 * d + h * hk, hk), (1, h * hw + hk, 4 * d + dk + h * hk, hk),
                   (1, h * hw + 2 * hk, 5 * d + h * hv, hv), (1, h * hw + 2 * hk + hv, 6 * d + h * hv, hv)]
    blocks += [(2, 0, 7 * d + 2 * RANK, 2 * d), (3, 0, 7 * d, 2 * RANK)]
    return [4 * d, 3 * d, 2 * d, LR_LANES], blocks


def _pack_plan(dm):
    sh = (9 * dm.D + 2 * RANK) // N_DEV
    tile = BF16_TILE_ROWS
    copies, straddles = [], []
    for part, dst, r0, n in _packed_parts(dm)[1]:
        for j in range(N_DEV):
            a, b = max(r0, sh * j), min(r0 + n, sh * (j + 1))
            if a >= b:
                continue
            a_up, b_down = -(-a // tile) * tile, b // tile * tile
            if b_down > a_up:
                copies.append((j, a_up - sh * j + _shard_offset(j, sh), b_down - a_up, part, dst + a_up - r0))
            if a % tile:
                lo = a // tile * tile
                straddles.append((j, lo - sh * (j - 1) + _shard_offset(j - 1, sh), part, dst + lo - r0, a - lo))
    return copies, straddles


def _packed_scratch(dm):
    copies, straddles = _pack_plan(dm)
    return ([pltpu.VMEM((rows, dm.D), BF16) for rows in _packed_parts(dm)[0]]
            + [pltpu.VMEM((2 * max(len(straddles), 1), BF16_TILE_ROWS, dm.D), BF16),
               pltpu.SemaphoreType.DMA((len(copies) + 2 * len(straddles),))])


def _load_packed(g_ref, parts, edges, sems, dm):
    copies, straddles = _pack_plan(dm)
    tile = BF16_TILE_ROWS
    parts[3][2 * RANK:, :] = jnp.zeros((LR_LANES - 2 * RANK, dm.D), BF16)
    dmas = [pltpu.make_async_copy(g_ref.at[j, pl.ds(src, n), :], parts[p].at[pl.ds(dst, n), :], sems.at[i])
            for i, (j, src, n, p, dst) in enumerate(copies)]
    for i, (j, src, p, dst, split) in enumerate(straddles):
        k = len(copies) + 2 * i
        dmas.append(pltpu.make_async_copy(g_ref.at[j - 1, pl.ds(src, tile), :], edges.at[2 * i], sems.at[k]))
        dmas.append(pltpu.make_async_copy(g_ref.at[j, pl.ds(0, tile), :], edges.at[2 * i + 1], sems.at[k + 1]))
    for cp in dmas:
        cp.start()
    for cp in dmas:
        cp.wait()
    row = lax.broadcasted_iota(jnp.int32, (tile, dm.D), 0)
    for i, (j, src, p, dst, split) in enumerate(straddles):
        parts[p][dst:dst + tile, :] = jnp.where(row < split, edges[2 * i], edges[2 * i + 1])


def _inproj(u, gathered, dm):
    t, d = u.shape
    tm = _pick(t, 512, 16)
    widths = _packed_parts(dm)[0]
    cn = 1024

    def body(u_ref, g_ref, *rest):
        outs, parts, (edges, sems) = rest[:4], rest[4:8], rest[8:]

        @pl.when(pl.program_id(0) == 0)
        def _():
            _load_packed(g_ref, parts, edges, sems, dm)

        ut = u_ref[...]
        for w, o_ref in zip(parts, outs):
            n = w.shape[0]
            step = cn if n % cn == 0 else n
            for j in range(0, n, step):
                o_ref[:, j:j + step] = _dot_nt(ut, w[j:j + step, :]).astype(BF16)

    return pl.pallas_call(
        body, name="inproj", grid=(t // tm,),
        in_specs=[pl.BlockSpec((tm, d), lambda i: (i, 0)), pl.BlockSpec(memory_space=pl.ANY)],
        out_specs=[pl.BlockSpec((tm, w), lambda i: (i, 0)) for w in widths],
        out_shape=[jax.ShapeDtypeStruct((t, w), BF16) for w in widths],
        scratch_shapes=_packed_scratch(dm), compiler_params=_cp(1),
    )(u, gathered)


def _conv_rows(dm):
    return _pick(dm.LP, 256, 16)


def _shifted(m, prev_row, next_row, rows):
    row = lax.broadcasted_iota(jnp.int32, m.shape, 0)
    m_prev = jnp.where(row == 0, prev_row, pltpu.roll(m, 1, 0))
    m_next = jnp.where(row == rows - 1, next_row, pltpu.roll(m, rows - 1, 0))
    return m_prev, m_next


def _conv_fwd(proj_a, conv_w, dm):
    lp, cw, rc = dm.LP, dm.CW, _conv_rows(dm)
    nchunk = lp // rc

    def body(p_ref, w_ref, y_ref):
        w0, w1, w2 = w_ref[0:1, :], w_ref[1:2, :], w_ref[2:3, :]

        def chunk(ci, carry):
            r0 = pl.multiple_of(ci * rc, rc)
            blk = p_ref[pl.ds(r0, rc), :].astype(F32)
            cb, cc, cx, cz = (blk[:, i * cw:(i + 1) * cw] for i in range(4))
            m = cc * cx
            rp = pl.multiple_of(jnp.maximum(r0 - 16, 0), 16)
            rn = pl.multiple_of(jnp.minimum(r0 + rc, lp - 16), 16)
            pv = p_ref[pl.ds(rp, 16), cw:3 * cw].astype(F32)
            nx = p_ref[pl.ds(rn, 16), cw:3 * cw].astype(F32)
            prev_row = jnp.where(ci > 0, pv[15:16, :cw] * pv[15:16, cw:], 0.0)
            next_row = jnp.where(ci < nchunk - 1, nx[0:1, :cw] * nx[0:1, cw:], 0.0)
            m_prev, m_next = _shifted(m, prev_row, next_row, rc)
            s = w0 * m_prev + w1 * m + w2 * m_next
            y_ref[pl.ds(r0, rc), :] = (cb * s * (cz * _sigmoid(cz))).astype(BF16)
            return carry

        lax.fori_loop(0, nchunk, chunk, 0)

    return pl.pallas_call(
        body, name="conv_fwd", grid=(dm.Bl, dm.NJ),
        in_specs=[pl.BlockSpec((lp, 4 * cw), lambda s, j: (s, j)), pl.BlockSpec((3, cw), lambda s, j: (0, j))],
        out_specs=pl.BlockSpec((lp, cw), lambda s, j: (s, j)),
        out_shape=jax.ShapeDtypeStruct((dm.T, dm.D), BF16), compiler_params=_cp(2),
    )(proj_a, conv_w)


def _conv_bwd(proj_a, dy_conv, conv_w, dm):
    lp, cw, rc = dm.LP, dm.CW, _conv_rows(dm)
    nchunk = lp // rc

    def body(p_ref, dy_ref, w_ref, d_ref, gw_ref):
        w0, w1, w2 = w_ref[0:1, :], w_ref[1:2, :], w_ref[2:3, :]

        def ds_of(p4, dy):
            cb, cz = p4[:, :cw], p4[:, 3 * cw:]
            return dy * cb * (cz * _sigmoid(cz))

        def chunk(ci, carry):
            g0, g1, g2 = carry
            r0 = pl.multiple_of(ci * rc, rc)
            blk = p_ref[pl.ds(r0, rc), :].astype(F32)
            dy = dy_ref[pl.ds(r0, rc), :].astype(F32)
            cb, cc, cx, cz = (blk[:, i * cw:(i + 1) * cw] for i in range(4))
            rp = pl.multiple_of(jnp.maximum(r0 - 16, 0), 16)
            rn = pl.multiple_of(jnp.minimum(r0 + rc, lp - 16), 16)
            pv = p_ref[pl.ds(rp, 16), :].astype(F32)[15:16]
            nx = p_ref[pl.ds(rn, 16), :].astype(F32)[0:1]
            dpv = dy_ref[pl.ds(rp, 16), :].astype(F32)[15:16]
            dnx = dy_ref[pl.ds(rn, 16), :].astype(F32)[0:1]
            has_prev, has_next = ci > 0, ci < nchunk - 1
            m = cc * cx
            m_prev, m_next = _shifted(m, jnp.where(has_prev, pv[:, cw:2 * cw] * pv[:, 2 * cw:3 * cw], 0.0),
                                      jnp.where(has_next, nx[:, cw:2 * cw] * nx[:, 2 * cw:3 * cw], 0.0), rc)
            s = w0 * m_prev + w1 * m + w2 * m_next
            sg = _sigmoid(cz)
            silu = cz * sg
            ds = dy * cb * silu
            ds_prev, ds_next = _shifted(ds, jnp.where(has_prev, ds_of(pv, dpv), 0.0),
                                        jnp.where(has_next, ds_of(nx, dnx), 0.0), rc)
            dm_ = w0 * ds_next + w1 * ds + w2 * ds_prev
            d_ref[pl.ds(r0, rc), 0:cw] = (dy * s * silu).astype(BF16)
            d_ref[pl.ds(r0, rc), cw:2 * cw] = (dm_ * cx).astype(BF16)
            d_ref[pl.ds(r0, rc), 2 * cw:3 * cw] = (dm_ * cc).astype(BF16)
            d_ref[pl.ds(r0, rc), 3 * cw:4 * cw] = (dy * cb * s * (sg * (1.0 + cz * (1.0 - sg)))).astype(BF16)
            return (g0 + jnp.sum(ds * m_prev, axis=0, keepdims=True), g1 + jnp.sum(ds * m, axis=0, keepdims=True),
                    g2 + jnp.sum(ds * m_next, axis=0, keepdims=True))

        z = jnp.zeros((1, cw), F32)
        g0, g1, g2 = lax.fori_loop(0, nchunk, chunk, (z, z, z))

        @pl.when(pl.program_id(1) == 0)
        def _():
            gw_ref[...] = jnp.zeros_like(gw_ref)

        gw_ref[0:1, :] += g0
        gw_ref[1:2, :] += g1
        gw_ref[2:3, :] += g2

    return pl.pallas_call(
        body, name="conv_bwd", grid=(dm.NJ, dm.Bl),
        in_specs=[pl.BlockSpec((lp, 4 * cw), lambda j, s: (s, j)), pl.BlockSpec((lp, cw), lambda j, s: (s, j)),
                  pl.BlockSpec((3, cw), lambda j, s: (0, j))],
        out_specs=[pl.BlockSpec((lp, 4 * cw), lambda j, s: (s, j)), pl.BlockSpec((8, cw), lambda j, s: (0, j))],
        out_shape=[jax.ShapeDtypeStruct((dm.T, 4 * dm.D), BF16), jax.ShapeDtypeStruct((8, dm.D), F32)],
        compiler_params=_cp(2),
    )(proj_a, dy_conv, conv_w)


def _interleave(gens):
    results = [None] * len(gens)
    live = list(range(len(gens)))
    while live:
        for idx in list(live):
            try:
                next(gens[idx])
            except StopIteration as done:
                results[idx] = done.value
                live.remove(idx)
    return results


def _group_chunks(dm):
    n = dm.NC - dm.C0
    return 3 if n % 3 == 0 else 1


def _group_masks(rows):
    ii = lax.broadcasted_iota(jnp.int32, (rows, rows), 0)
    jj = lax.broadcasted_iota(jnp.int32, (rows, rows), 1)
    same = jnp.right_shift(ii, CHUNK_SHIFT) == jnp.right_shift(jj, CHUNK_SHIFT)
    return same & (jj <= ii), same & (jj > ii)


def _first_row(chunk):
    return chunk * CHUNK if isinstance(chunk, int) else pl.multiple_of(chunk * CHUNK, CHUNK)


def _chunk_totals(b, fwd):
    hk = b.shape[1]
    rows = [b[c * CHUNK + CHUNK - 1:(c + 1) * CHUNK] if fwd else b[c * CHUNK:c * CHUNK + 1]
            for c in range(b.shape[0] // CHUNK)]
    return jnp.concatenate([jnp.broadcast_to(r, (CHUNK, hk)) for r in rows], axis=0)


def _log_gate(lr_rows, w_ref, b_ref, first_group, hk):
    z = _dot(lr_rows, w_ref[...]) + b_ref[...]
    e = jnp.exp(-jnp.abs(z))
    g = (jnp.minimum(z, 0.0) - jnp.log(1.0 + e)) * (1.0 / GATE_NORMALIZER)
    dg_dz = jnp.where(z >= 0.0, e, 1.0) / (1.0 + e) * (1.0 / GATE_NORMALIZER)
    row = lax.broadcasted_iota(jnp.int32, (lr_rows.shape[0], hk), 0)
    pad = first_group & (row < PAD_ROWS)
    return jnp.where(pad, 0.0, g), jnp.where(pad, 0.0, dg_dz)


def _gla_fwd(proj_b, lr, wg_f, bg_f, wg_b, bg_b, gla_g, dm):
    lp, hk, hv, nc, c0, hw = dm.LP, dm.HK, dm.HV, dm.NC, dm.C0, dm.HW
    scale = hk ** -0.5
    gc = _group_chunks(dm)
    gr, ng = gc * CHUNK, (nc - c0) // gc

    def body(p_ref, lr_ref, wf_ref, bf_ref, wb_ref, bb_ref, gg_ref, o_ref, y_ref, st_ref, b_out, gs_out, oacc_f, oacc_b):
        low_incl, up_strict = _group_masks(gr)
        if c0 > 0:
            zr = c0 * CHUNK
            o_ref[0:zr, :] = jnp.zeros((zr, hv), BF16)
            y_ref[0:zr, :] = jnp.zeros((zr, hv), BF16)
            b_out[:, 0:zr, :] = jnp.zeros((2, zr, hk), F32)
            gs_out[:, 0:zr, :] = jnp.zeros((2, zr, hk), F32)
            st_ref[0, 0, :, 0:c0] = jnp.zeros((2, c0, hv, hk), BF16)

        def decay(gi, fwd):
            w_ref, b_ref = (wf_ref, bf_ref) if fwd else (wb_ref, bb_ref)
            r0 = _first_row(c0 + gi * gc)
            yield
            g, dg_dz = _log_gate(lr_ref[pl.ds(r0, gr), :], w_ref, b_ref, gi == 0, hk)
            gs_out[0 if fwd else 1, pl.ds(r0, gr), :] = dg_dz
            yield
            b = _chunk_cumsum(g, not fwd)
            b_out[0 if fwd else 1, pl.ds(r0, gr), :] = b
            return b

        def group(gi, st, b, fwd):
            oacc = oacc_f if fwd else oacc_b
            r0 = pl.multiple_of((c0 + gi * gc) * CHUNK, CHUNK)
            blk = p_ref[pl.ds(r0, gr), :]
            q = blk[:, :hk].astype(F32) * scale
            k = blk[:, hk:2 * hk].astype(F32)
            v = blk[:, 2 * hk:2 * hk + hv]
            btot = _chunk_totals(b, fwd)
            qi = (q * jnp.exp(b)).astype(BF16)
            ki = (k * jnp.exp(-b)).astype(BF16)
            kd = (k * jnp.exp(btot - b)).astype(BF16)
            dec = jnp.exp(btot)
            a = _dot_nt(qi, ki)
            yield
            o = _dot(jnp.where(low_incl if fwd else up_strict, a, 0.0).astype(BF16), v)
            chunk_rows = [slice(c * CHUNK, (c + 1) * CHUNK) for c in range(gc)]
            kv = [_dot_tn(v[rows], kd[rows]) for rows in chunk_rows]
            for c in (range(gc) if fwd else reversed(range(gc))):
                yield
                rows = chunk_rows[c]
                st_b = st.astype(BF16)
                st_ref[0, 0, 0 if fwd else 1, c0 + gi * gc + c] = st_b
                oacc[pl.ds(r0 + c * CHUNK, CHUNK), :] = o[rows] + _dot_nt(qi[rows], st_b)
                st = st * dec[c * CHUNK:c * CHUNK + 1] + kv[c]
            return st

        def step(i, carry):
            st_f, st_b, b_f, b_b = carry
            gf, gb = i, ng - 1 - i
            return tuple(_interleave([group(gf, st_f, b_f, True), group(gb, st_b, b_b, False),
                                      decay(jnp.minimum(gf + 1, ng - 1), True), decay(jnp.maximum(gb - 1, 0), False)]))

        zero = jnp.zeros((hv, hk), F32)
        lax.fori_loop(0, ng, step, (zero, zero, *_interleave([decay(0, True), decay(ng - 1, False)])))

        def finish(i, carry):
            r0 = pl.multiple_of((c0 + i * gc) * CHUNK, CHUNK)
            o = oacc_f[pl.ds(r0, gr), :] + oacc_b[pl.ds(r0, gr), :]
            r = p_ref[pl.ds(r0, gr), 2 * hk + hv:].astype(F32)
            on = o * lax.rsqrt(jnp.mean(o * o, axis=-1, keepdims=True) + EPS) * gg_ref[...]
            o_ref[pl.ds(r0, gr), :] = o.astype(BF16)
            y_ref[pl.ds(r0, gr), :] = (on * r * _sigmoid(r)).astype(BF16)
            return carry

        lax.fori_loop(0, ng, finish, 0)

    head = lambda s, h: (s, h)
    wspec = pl.BlockSpec((LR_LANES, hk), lambda s, h: (0, h))
    bspec = pl.BlockSpec((1, hk), lambda s, h: (0, h))
    return pl.pallas_call(
        body, name="gla_fwd", grid=(dm.Bl, HEADS),
        in_specs=[pl.BlockSpec((lp, hw), head), pl.BlockSpec((lp, LR_LANES), lambda s, h: (s, 0)),
                  wspec, bspec, wspec, bspec, pl.BlockSpec((1, hv), lambda s, h: (0, 0))],
        out_specs=[pl.BlockSpec((lp, hv), head), pl.BlockSpec((lp, hv), head),
                   pl.BlockSpec((1, 1, 2, nc, hv, hk), lambda s, h: (s, h, 0, 0, 0, 0)),
                   pl.BlockSpec((2, lp, hk), lambda s, h: (0, s, h)), pl.BlockSpec((2, lp, hk), lambda s, h: (0, s, h))],
        out_shape=[jax.ShapeDtypeStruct((dm.T, dm.DV), BF16), jax.ShapeDtypeStruct((dm.T, dm.DV), BF16),
                   jax.ShapeDtypeStruct((dm.Bl, HEADS, 2, nc, hv, hk), BF16),
                   jax.ShapeDtypeStruct((2, dm.T, dm.DK), F32), jax.ShapeDtypeStruct((2, dm.T, dm.DK), F32)],
        scratch_shapes=[pltpu.VMEM((lp, hv), F32), pltpu.VMEM((lp, hv), F32)],
        compiler_params=_cp(2),
    )(proj_b, lr, wg_f, bg_f, wg_b, bg_b, gla_g)


def _gla_bwd(proj_b, lr, o_all, dy_gla, states, decays, gate_slopes, wg_f, wg_b, gla_g, dm):
    lp, hk, hv, nc, c0, hw = dm.LP, dm.HK, dm.HV, dm.NC, dm.C0, dm.HW
    scale = hk ** -0.5
    gc = _group_chunks(dm)
    gr, ng = gc * CHUNK, (nc - c0) // gc

    def body(p_ref, lr_ref, o_ref, dy_ref, st_ref, b_ref, gs_ref, wf_ref, wb_ref, gg_ref,
             d_ref, dlr_ref, gwf_ref, gbf_ref, gwb_ref, gbb_ref, ggg_ref, do_s, dq_s, dk_s, dv_s, dz_s):
        low_incl, up_strict = _group_masks(gr)
        h = pl.program_id(1)

        @pl.when(h == 0)
        def _():
            dlr_ref[...] = jnp.zeros_like(dlr_ref)

        if c0 > 0:
            zr = c0 * CHUNK
            d_ref[0:zr, :] = jnp.zeros((zr, hw), BF16)
        def norm_bwd(i, ggg):
            r0 = pl.multiple_of((c0 + i * gc) * CHUNK, CHUNK)
            for acc in (dq_s, dk_s, dv_s):
                acc[pl.ds(r0, gr), :] = jnp.zeros((gr, acc.shape[1]), F32)
            o = o_ref[pl.ds(r0, gr), :].astype(F32)
            dy = dy_ref[pl.ds(r0, gr), :].astype(F32)
            r = p_ref[pl.ds(r0, gr), 2 * hk + hv:].astype(F32)
            rstd = lax.rsqrt(jnp.mean(o * o, axis=-1, keepdims=True) + EPS)
            ohat = o * rstd
            sg = _sigmoid(r)
            d_on = dy * (r * sg)
            d_ref[pl.ds(r0, gr), 2 * hk + hv:] = (dy * ohat * gg_ref[...] * (sg * (1.0 + r * (1.0 - sg)))).astype(BF16)
            d_oh = d_on * gg_ref[...]
            do_s[pl.ds(r0, gr), :] = (rstd * (d_oh - ohat * jnp.mean(d_oh * ohat, axis=-1, keepdims=True))).astype(BF16)
            return ggg + jnp.sum(d_on * ohat, axis=0, keepdims=True)

        ggg = lax.fori_loop(0, ng, norm_bwd, jnp.zeros((1, hv), F32))

        @pl.when((pl.program_id(0) == 0) & (h == 0))
        def _():
            ggg_ref[...] = jnp.zeros_like(ggg_ref)

        ggg_ref[0:1, :] += ggg

        def load(gi):
            r0 = pl.multiple_of((c0 + gi * gc) * CHUNK, CHUNK)
            blk = p_ref[pl.ds(r0, gr), :]
            return r0, blk[:, :hk].astype(F32) * scale, blk[:, hk:2 * hk].astype(F32), blk[:, 2 * hk:2 * hk + hv]

        zero = jnp.zeros((hv, hk), F32)

        def grad(gi, carry, fwd):
            dst, gb = carry
            way = 0 if fwd else 1
            mask = low_incl if fwd else up_strict
            r0, q, k, v = load(gi)
            b = b_ref[way, pl.ds(r0, gr), :]
            btot = _chunk_totals(b, fwd)
            eb, enb, edb, dec = jnp.exp(b), jnp.exp(-b), jnp.exp(btot - b), jnp.exp(btot)
            qi_f, ki_f, kd_f = q * eb, k * enb, k * edb
            qi, ki, kd = qi_f.astype(BF16), ki_f.astype(BF16), kd_f.astype(BF16)
            do = do_s[pl.ds(r0, gr), :]
            a = _dot_nt(qi, ki)
            da = _dot_nt(do, v)
            yield
            a = jnp.where(mask, a, 0.0).astype(BF16)
            da = jnp.where(mask, da, 0.0).astype(BF16)
            dv = _dot_tn(a, do)
            dqi = _dot(da, ki)
            dki = _dot_tn(da, qi)
            dv_c, dqi_c, dkd_c, extra_c = [None] * gc, [None] * gc, [None] * gc, [None] * gc
            chunk_rows = [slice(c * CHUNK, (c + 1) * CHUNK) for c in range(gc)]
            qdo = [_dot_tn(do[rows], qi[rows]) for rows in chunk_rows]
            for c in (reversed(range(gc)) if fwd else range(gc)):
                yield
                rows = chunk_rows[c]
                st = st_ref[0, 0, way, c0 + gi * gc + c]
                dsn_b = dst.astype(BF16)
                dec_c = dec[c * CHUNK:c * CHUNK + 1]
                dv_c[c] = dv[rows] + _dot_nt(kd[rows], dsn_b)
                dqi_c[c] = dqi[rows] + _dot(do[rows], st)
                dkd_c[c] = _dot(v[rows], dsn_b)
                ddec = jnp.sum(st.astype(F32) * dst, axis=0, keepdims=True)
                extra = jnp.sum(dkd_c[c] * kd_f[rows], axis=0, keepdims=True) + ddec * dec_c
                extra_c[c] = jnp.broadcast_to(extra, (CHUNK, hk))
                dst = dst * dec_c + qdo[c]
            yield
            dv, dqi = jnp.concatenate(dv_c, axis=0), jnp.concatenate(dqi_c, axis=0)
            dkd, extra = jnp.concatenate(dkd_c, axis=0), jnp.concatenate(extra_c, axis=0)
            dq_s[pl.ds(r0, gr), :] += dqi * eb * scale
            dk_s[pl.ds(r0, gr), :] += dki * enb + dkd * edb
            dv_s[pl.ds(r0, gr), :] += dv
            db = dqi * qi_f - dki * ki_f - dkd * kd_f
            dg = _chunk_cumsum(db, fwd) + extra
            yield
            dz = dg * gs_ref[way, pl.ds(r0, gr), :]
            dz_s[way, pl.ds(r0, gr), :] = dz.astype(BF16)
            return dst, gb + jnp.sum(dz, axis=0, keepdims=True)

        def grad_step(i, carry):
            return tuple(_interleave([grad(ng - 1 - i, carry[0], True), grad(i, carry[1], False)]))

        init = (zero, jnp.zeros((1, hk), F32))
        (_, gb_f), (_, gb_b) = lax.fori_loop(0, ng, grad_step, (init, init))
        used = slice(c0 * CHUNK, lp)
        for way, (w_ref, gw_ref, gb_ref, gb) in enumerate(((wf_ref, gwf_ref, gbf_ref, gb_f), (wb_ref, gwb_ref, gbb_ref, gb_b))):
            dlr_ref[used, :] += _dot_nt(dz_s[way, used, :], w_ref[...])
            gw_ref[0] = _dot_tn(lr_ref[used, :], dz_s[way, used, :])
            gb_ref[0] = jnp.zeros((8, hk), F32)
            gb_ref[0, 0:1, :] = gb

        def combine(i, carry):
            r0 = pl.multiple_of((c0 + i * gc) * CHUNK, CHUNK)
            d_ref[pl.ds(r0, gr), 0:hk] = dq_s[pl.ds(r0, gr), :].astype(BF16)
            d_ref[pl.ds(r0, gr), hk:2 * hk] = dk_s[pl.ds(r0, gr), :].astype(BF16)
            d_ref[pl.ds(r0, gr), 2 * hk:2 * hk + hv] = dv_s[pl.ds(r0, gr), :].astype(BF16)
            return carry

        lax.fori_loop(0, ng, combine, 0)

    head = lambda s, h: (s, h)
    wspec = pl.BlockSpec((LR_LANES, hk), lambda s, h: (0, h))
    gwspec = pl.BlockSpec((1, LR_LANES, hk), lambda s, h: (s, 0, h))
    gbspec = pl.BlockSpec((1, 8, hk), lambda s, h: (s, 0, h))
    gw_shape = jax.ShapeDtypeStruct((dm.Bl, LR_LANES, dm.DK), F32)
    gb_shape = jax.ShapeDtypeStruct((dm.Bl, 8, dm.DK), F32)
    both = pl.BlockSpec((2, lp, hk), lambda s, h: (0, s, h))
    return pl.pallas_call(
        body, name="gla_bwd", grid=(dm.Bl, HEADS),
        in_specs=[pl.BlockSpec((lp, hw), head), pl.BlockSpec((lp, LR_LANES), lambda s, h: (s, 0)),
                  pl.BlockSpec((lp, hv), head), pl.BlockSpec((lp, hv), head),
                  pl.BlockSpec((1, 1, 2, nc, hv, hk), lambda s, h: (s, h, 0, 0, 0, 0)), both, both,
                  wspec, wspec, pl.BlockSpec((1, hv), lambda s, h: (0, 0))],
        out_specs=[pl.BlockSpec((lp, hw), head), pl.BlockSpec((lp, LR_LANES), lambda s, h: (s, 0)),
                   gwspec, gbspec, gwspec, gbspec, pl.BlockSpec((8, hv), lambda s, h: (0, 0))],
        out_shape=[jax.ShapeDtypeStruct((dm.T, HEADS * hw), BF16), jax.ShapeDtypeStruct((dm.T, LR_LANES), F32),
                   gw_shape, gb_shape, gw_shape, gb_shape, jax.ShapeDtypeStruct((8, hv), F32)],
        scratch_shapes=[pltpu.VMEM((lp, hv), BF16), pltpu.VMEM((lp, hk), F32), pltpu.VMEM((lp, hk), F32),
                        pltpu.VMEM((lp, hv), F32), pltpu.VMEM((2, lp, hk), BF16)],
        compiler_params=_cp(2),
    )(proj_b, lr, o_all, dy_gla, states, decays, gate_slopes, wg_f, wg_b, gla_g)


def _stream_tiles(n_tiles, loads, stores, compute):
    for cp in loads(0, 0):
        cp.start()

    def step(t, carry):
        slot = t % 2

        @pl.when(t + 1 < n_tiles)
        def _():
            for cp in loads(t + 1, 1 - slot):
                cp.start()

        for cp in loads(t, slot):
            cp.wait()

        @pl.when(t >= 2)
        def _():
            for cp in stores(t - 2, slot):
                cp.wait()

        compute(t, slot)
        for cp in stores(t, slot):
            cp.start()
        return carry

    lax.fori_loop(0, n_tiles, step, 0)
    for t in range(max(n_tiles - 2, 0), n_tiles):
        for cp in stores(t, t % 2):
            cp.wait()


def _token_tiles(dm, target_rows=512):
    rows = _pick(dm.S, target_rows, 16)
    per_seq = dm.S // rows
    return rows, dm.Bl * per_seq, lambda t: pl.multiple_of((t // per_seq) * dm.LP + dm.TM + (t % per_seq) * rows, 16)


def _head(y_conv, y_gla, proj_c, w_oc, w_og, w_out, x, target, g_post, dm):
    d, tm = dm.D, dm.TM
    rows, n_tiles, first_row = _token_tiles(dm, 512)
    parts = 2 if rows % (2 * BF16_TILE_ROWS) == 0 else 1
    n_out = 8

    def body(*refs):
        yc_hbm, yg_hbm, c_hbm, woc_ref, wog_ref, wo_ref, x_hbm, t_hbm, g_ref = refs[:9]
        outs, st_ref = refs[9:9 + n_out], refs[9 + n_out]
        ycbuf, ygbuf, cbuf, xbuf, tbuf = refs[10 + n_out:15 + n_out]
        obufs = refs[15 + n_out:15 + 2 * n_out]
        zbuf, zbuf2, sem_in, sem_out, sem_zero = refs[15 + 2 * n_out:]

        def loads(t, slot):
            padded = [(yc_hbm, ycbuf), (yg_hbm, ygbuf), (c_hbm, cbuf)]
            own = [(x_hbm, xbuf), (t_hbm, tbuf)]
            return ([pltpu.make_async_copy(h.at[pl.ds(first_row(t), rows), :], b.at[slot], sem_in.at[i, slot])
                     for i, (h, b) in enumerate(padded)] +
                    [pltpu.make_async_copy(h.at[pl.ds(t * rows, rows), :], b.at[slot], sem_in.at[3 + i, slot])
                     for i, (h, b) in enumerate(own)])

        def stores(t, slot):
            return [pltpu.make_async_copy(b.at[slot], h.at[pl.ds(first_row(t), rows), :], sem_out.at[i, slot])
                    for i, (h, b) in enumerate(zip(outs, obufs))]

        def chain(slot, part):
            mg_o, do_o, dy_o, dpc_o, dpg_o, dc_o, dyc_o, dyg_o = obufs
            pc = _dot(ycbuf[slot, part], woc_ref[...])
            pg = _dot(ygbuf[slot, part], wog_ref[...])
            yield
            sa = _sigmoid(cbuf[slot, part, :d].astype(F32))
            sb = _sigmoid(cbuf[slot, part, d:].astype(F32))
            merged = (sa * pc + sb * pg).astype(BF16)
            mg_o[slot, part] = merged
            out = _dot(merged, wo_ref[...])
            yield
            rstd = lax.rsqrt(jnp.mean(out * out, axis=-1, keepdims=True) + EPS)
            ohat = out * rstd
            err = xbuf[slot, part] + ohat * g_ref[...] - tbuf[slot, part]
            dy = err * (1.0 / d)
            d_oh = dy * g_ref[...]
            d_out = (rstd * (d_oh - ohat * jnp.mean(d_oh * ohat, axis=-1, keepdims=True))).astype(BF16)
            do_o[slot, part] = d_out
            dy_o[slot, part] = dy.astype(BF16)
            st_ref[0:1, :] += jnp.sum(dy * ohat, axis=0, keepdims=True)
            st_ref[1:2, :] += jnp.sum(err * err, axis=0, keepdims=True)
            dmg = _dot_nt(d_out, wo_ref[...])
            yield
            dpc = (dmg * sa).astype(BF16)
            dpg = (dmg * sb).astype(BF16)
            dpc_o[slot, part] = dpc
            dpg_o[slot, part] = dpg
            dc_o[slot, part, :d] = (dmg * pc * sa * (1.0 - sa)).astype(BF16)
            dc_o[slot, part, d:] = (dmg * pg * sb * (1.0 - sb)).astype(BF16)
            dyc_o[slot, part] = _dot_nt(dpc, woc_ref[...]).astype(BF16)
            dyg_o[slot, part] = _dot_nt(dpg, wog_ref[...]).astype(BF16)

        def compute(t, slot):
            _interleave([chain(slot, pl.ds(i * (rows // parts), rows // parts)) for i in range(parts)])

        st_ref[...] = jnp.zeros_like(st_ref)
        zbuf[...] = jnp.zeros_like(zbuf)
        zbuf2[...] = jnp.zeros_like(zbuf2)
        zeros = [pltpu.make_async_copy(zbuf2 if out.shape[1] == 2 * d else zbuf, out.at[pl.ds(b * dm.LP, tm), :], sem_zero.at[i, b])
                 for i, out in enumerate(outs) for b in range(dm.Bl)]
        for cp in zeros:
            cp.start()
        _stream_tiles(n_tiles, loads, stores, compute)
        for cp in zeros:
            cp.wait()

    any_spec, vmem = pl.BlockSpec(memory_space=pl.ANY), pl.BlockSpec(memory_space=pltpu.VMEM)
    widths = [d, d, d, d, d, 2 * d, d, d]
    tile = lambda w, dt: pltpu.VMEM((2, rows, w), dt)
    return pl.pallas_call(
        body, name="head", in_specs=[any_spec] * 3 + [vmem] * 3 + [any_spec] * 2 + [vmem],
        out_specs=[any_spec] * n_out + [vmem],
        out_shape=[jax.ShapeDtypeStruct((dm.T, w), BF16) for w in widths] + [jax.ShapeDtypeStruct((8, d), F32)],
        scratch_shapes=[tile(d, BF16), tile(d, BF16), tile(2 * d, BF16), tile(d, F32), tile(d, F32)]
        + [tile(w, BF16) for w in widths]
        + [pltpu.VMEM((tm, d), BF16), pltpu.VMEM((tm, 2 * d), BF16), pltpu.SemaphoreType.DMA((5, 2)),
           pltpu.SemaphoreType.DMA((n_out, 2)), pltpu.SemaphoreType.DMA((n_out, dm.Bl))],
        compiler_params=pltpu.CompilerParams(vmem_limit_bytes=VMEM_LIMIT_BYTES),
    )(y_conv, y_gla, proj_c, w_oc, w_og, w_out, x.reshape(dm.Bl * dm.S, d), target.reshape(dm.Bl * dm.S, d), g_post)


def _grad_h(d_parts, gathered, dy, x, metapad, g_pre, dm):
    d, tm = dm.D, dm.TM
    rows, n_tiles, first_row = _token_tiles(dm, 256)
    widths = [a.shape[1] for a in d_parts]
    np_ = len(d_parts)

    def body(*refs):
        d_hbm, g_hbm, dy_hbm, x_hbm, mp_ref, g_ref = refs[:np_], refs[np_], refs[np_ + 1], refs[np_ + 2], refs[np_ + 3], refs[np_ + 4]
        gx_hbm, dmeta_ref, gg_ref = refs[np_ + 5:np_ + 8]
        parts, edges, sems = refs[np_ + 8:np_ + 12], refs[np_ + 12], refs[np_ + 13]
        dbufs = refs[np_ + 14:2 * np_ + 14]
        dybuf, xbuf, gbuf = refs[2 * np_ + 14:2 * np_ + 17]
        mbufs = refs[2 * np_ + 17:3 * np_ + 17]
        sem_in, sem_out, sem_meta = refs[3 * np_ + 17:]

        def grad_u(tiles):
            du = _dot(tiles[0].astype(BF16), parts[0][...])
            for a, w in zip(tiles[1:], parts[1:]):
                du = du + _dot(a.astype(BF16), w[...])
            return du

        def norm_bwd(h, du, dy):
            rstd = lax.rsqrt(jnp.mean(h * h, axis=-1, keepdims=True) + EPS)
            hhat = h * rstd
            dug = du * g_ref[...]
            gg_ref[0:1, :] += jnp.sum(du * hhat, axis=0, keepdims=True)
            return dy + rstd * (dug - hhat * jnp.mean(dug * hhat, axis=-1, keepdims=True))

        def loads(t, slot):
            padded = list(zip(d_hbm, dbufs)) + [(dy_hbm, dybuf)]
            return ([pltpu.make_async_copy(h.at[pl.ds(first_row(t), rows), :], b.at[slot], sem_in.at[i, slot])
                     for i, (h, b) in enumerate(padded)] +
                    [pltpu.make_async_copy(x_hbm.at[pl.ds(t * rows, rows), :], xbuf.at[slot], sem_in.at[np_ + 1, slot])])

        def stores(t, slot):
            return [pltpu.make_async_copy(gbuf.at[slot], gx_hbm.at[pl.ds(t * rows, rows), :], sem_out.at[slot])]

        def compute(t, slot):
            gbuf[slot] = norm_bwd(xbuf[slot], grad_u([b[slot] for b in dbufs]), dybuf[slot].astype(F32))

        gg_ref[...] = jnp.zeros_like(gg_ref)
        meta = [pltpu.make_async_copy(h.at[pl.ds(b * dm.LP, tm), :], buf.at[pl.ds(b * tm, tm), :], sem_meta.at[i, b])
                for i, (h, buf) in enumerate(zip(d_hbm, mbufs)) for b in range(dm.Bl)]
        for cp in meta:
            cp.start()
        _load_packed(g_hbm, parts, edges, sems, dm)
        _stream_tiles(n_tiles, loads, stores, compute)
        for cp in meta:
            cp.wait()
        dmeta_ref[...] = norm_bwd(jnp.concatenate([mp_ref[...]] * dm.Bl, axis=0), grad_u([buf[...] for buf in mbufs]), 0.0)

    any_spec, vmem = pl.BlockSpec(memory_space=pl.ANY), pl.BlockSpec(memory_space=pltpu.VMEM)
    grad_x, d_meta, gg = pl.pallas_call(
        body, name="grad_h", in_specs=[any_spec] * (np_ + 3) + [vmem, vmem], out_specs=[any_spec, vmem, vmem],
        out_shape=[jax.ShapeDtypeStruct((dm.Bl * dm.S, d), F32), jax.ShapeDtypeStruct((dm.Bl * tm, d), F32),
                   jax.ShapeDtypeStruct((8, d), F32)],
        scratch_shapes=_packed_scratch(dm)
        + [pltpu.VMEM((2, rows, w), a.dtype) for w, a in zip(widths, d_parts)]
        + [pltpu.VMEM((2, rows, d), BF16), pltpu.VMEM((2, rows, d), F32), pltpu.VMEM((2, rows, d), F32)]
        + [pltpu.VMEM((dm.Bl * tm, w), a.dtype) for w, a in zip(widths, d_parts)]
        + [pltpu.SemaphoreType.DMA((np_ + 2, 2)), pltpu.SemaphoreType.DMA((2,)), pltpu.SemaphoreType.DMA((np_, dm.Bl))],
        compiler_params=pltpu.CompilerParams(vmem_limit_bytes=VMEM_LIMIT_BYTES),
    )(*d_parts, gathered, dy, x.reshape(dm.Bl * dm.S, d), metapad, g_pre)
    return grad_x.reshape(dm.Bl, dm.S, d), d_meta.reshape(dm.Bl, tm, d), gg


def _adamw(partials, w, m, v, name, by_columns=False):
    rows_apart = w.ndim == 3
    r, c = w.shape[0], w.shape[-1]
    n_parts, pr = partials.shape[:2]
    assert pr == r or (by_columns and pr == _padded_shard_rows(r))
    assert by_columns or not rows_apart
    tr, tc = (r, _pick(c, 128, 128)) if by_columns else (_pick(r, 256, 16), c)

    def body(p_ref, w_ref, m_ref, v_ref, g_ref, d_ref, nm_ref, nv_ref):
        g = p_ref[0].astype(F32)
        for j in range(1, n_parts):
            g = g + p_ref[j].astype(F32)

        flat = lambda ref: ref.reshape(tr, tc) if rows_apart else ref

        def step(g):
            results = (g,) + _adam_step(g, *[flat(ref)[...] for ref in (w_ref, m_ref, v_ref)])
            for ref, val in zip((g_ref, d_ref, nm_ref, nv_ref), results):
                flat(ref)[...] = val

        if pr == r:
            step(g)
        else:
            me = 4 * lax.axis_index("x") + 2 * lax.axis_index("y") + lax.axis_index("c")
            for offset in sorted({_shard_offset(j, r) for j in range(N_DEV)}):
                @pl.when(_shard_offset(me, r) == offset)
                def _(offset=offset):
                    step(g[offset:offset + r])

    at = (lambda i: (0, i)) if by_columns else (lambda i: (i, 0))
    tile = pl.BlockSpec((tr, 1, tc), lambda i: (0, 0, i)) if rows_apart else pl.BlockSpec((tr, tc), at)
    out = jax.ShapeDtypeStruct(w.shape, F32)
    return pl.pallas_call(
        body, name=name, grid=(c // tc if by_columns else r // tr,),
        in_specs=[pl.BlockSpec((n_parts, pr if by_columns else tr, tc), lambda i: (0,) + at(i)), tile, tile, tile],
        out_specs=[tile, tile, tile, tile], out_shape=[out, out, out, out], compiler_params=_cp(1),
    )(partials, w, m, v)


def _adam_step(g, w, m, v):
    m2 = ADAM_B1 * m + (1.0 - ADAM_B1) * g
    v2 = ADAM_B2 * v + (1.0 - ADAM_B2) * (g * g)
    m_hat = m2 / (1.0 - ADAM_B1 ** ADAM_STEP)
    v_hat = v2 / (1.0 - ADAM_B2 ** ADAM_STEP)
    return -ADAM_LR * (m_hat / (jnp.sqrt(v_hat) + ADAM_EPS) + ADAM_WD * w), m2, v2


def _adamw_small(items, name):
    n = len(items)

    def body(*refs):
        ins, outs = refs[:4 * n], refs[4 * n:]
        for i in range(n):
            p_ref, w_ref, m_ref, v_ref = ins[4 * i:4 * i + 4]
            g = p_ref[0].astype(F32)
            for j in range(1, p_ref.shape[0]):
                g = g + p_ref[j].astype(F32)
            delta, m2, v2 = _adam_step(g, w_ref[...], m_ref[...], v_ref[...])
            for o_ref, val in zip(outs[4 * i:4 * i + 4], (g, delta, m2, v2)):
                o_ref[...] = val

    vmem = pl.BlockSpec(memory_space=pltpu.VMEM)
    res = pl.pallas_call(
        body, name=name, in_specs=[vmem] * (4 * n), out_specs=[vmem] * (4 * n),
        out_shape=[jax.ShapeDtypeStruct(w.shape, F32) for _, w, _, _ in items for _ in range(4)],
        compiler_params=pltpu.CompilerParams(vmem_limit_bytes=VMEM_LIMIT_BYTES),
    )(*[a for item in items for a in item])
    return [res[4 * i:4 * i + 4] for i in range(n)]


def _unpack_moves(dm):
    d, hk, hv, cw, nj, hw = dm.D, dm.HK, dm.HV, dm.CW, dm.NJ, dm.HW
    moves = [((4 * j + part) * cw, cw, part * d + j * cw) for j in range(nj) for part in range(4)]
    q0 = 4 * d
    k0, v0 = q0 + HEADS * hk, q0 + 2 * HEADS * hk
    r0 = v0 + HEADS * hv
    lr0 = r0 + HEADS * hv
    for h in range(HEADS):
        b0 = 4 * d + h * hw
        moves += [(b0, hk, q0 + h * hk), (b0 + hk, hk, k0 + h * hk), (b0 + 2 * hk, hv, v0 + h * hv),
                  (b0 + 2 * hk + hv, hv, r0 + h * hv)]
    moves.append((4 * d + HEADS * hw, 2 * d, lr0 + 2 * RANK))
    return moves, lr0


def _column_shards(g, shard_shape):
    r, c = g.shape
    return g.reshape(r, N_DEV, c // N_DEV).transpose(1, 0, 2).reshape((N_DEV,) + tuple(shard_shape))


def _join_column_shards(parts):
    r, c = parts.shape[-2:]
    return parts.reshape(N_DEV, r, c).transpose(1, 0, 2).reshape(r, N_DEV * c)


def _local_step(x, target, meta, g_pre, u, wt_shards, conv_w, wg_f, bg_f, wg_b, bg_b, gla_g, out_weights, g_post,
                on_matrix_grads=None):
    bl, s, d = x.shape
    dm = _Dims(bl, s, d)
    metapad = jnp.concatenate([jnp.zeros((dm.TM - N_META, d), F32), meta], axis=0)
    wgp_f = jnp.pad(wg_f, ((0, LR_LANES - RANK), (0, 0))).astype(BF16)
    wgp_b = jnp.pad(wg_b, ((RANK, LR_LANES - 2 * RANK), (0, 0))).astype(BF16)

    u = _prenorm_meta(u, metapad, g_pre, dm)
    proj_a, proj_b, proj_c, lr = _inproj(u, wt_shards, dm)
    y_conv = _conv_fwd(proj_a, conv_w, dm)
    o_all, y_gla, states, decays, gate_slopes = _gla_fwd(proj_b, lr, wgp_f, bg_f, wgp_b, bg_b, gla_g, dm)
    w_oc, w_og, w_out = out_weights(y_conv) if callable(out_weights) else out_weights
    merged, d_out, dy, d_pc, d_pg, d_c, dy_conv, dy_gla, stats = _head(y_conv, y_gla, proj_c, w_oc, w_og, w_out, x, target,
                                                                        g_post, dm)

    g_out = _matmul_tn(merged, d_out, BF16, "grad_w_out")
    g_oc = _matmul_tn(y_conv, d_pc, BF16, "grad_w_out_conv")
    g_og = _matmul_tn(y_gla, d_pg, BF16, "grad_w_out_gla")
    if on_matrix_grads is not None:
        conv_w = conv_w + on_matrix_grads(dict(w_out_conv=g_oc, w_out_gla=g_og, w_merge_out=g_out))
    d_a, g_conv = _conv_bwd(proj_a, dy_conv, conv_w, dm)
    d_b, d_lr, gwp_f, gbp_f, gwp_b, gbp_b, g_gla = _gla_bwd(proj_b, lr, o_all, dy_gla, states, decays, gate_slopes, wgp_f, wgp_b, gla_g, dm)
    moves, lr_at = _unpack_moves(dm)
    g_in = _matmul_tn_group([d_a, d_b, d_c], u, moves, 9 * d + 2 * RANK, (d_lr, 2 * RANK, lr_at), "grad_w_in", tile=d)
    if on_matrix_grads is not None:
        g_pre = g_pre + on_matrix_grads(dict(w_in=g_in))
    grad_x, d_meta, g_pre_rows = _grad_h([d_a, d_b, d_c, d_lr], wt_shards, dy, x, metapad, g_pre, dm)

    grads = dict(
        meta_tokens=jnp.sum(d_meta[:, dm.TM - N_META:, :], axis=0), norm_pre=g_pre_rows[0:1], w_in=g_in,
        conv_w=g_conv[0:3], w_gate_fwd=jnp.sum(gwp_f, axis=0)[:RANK], b_gate_fwd=jnp.sum(gbp_f, axis=0)[0:1],
        w_gate_bwd=jnp.sum(gwp_b, axis=0)[RANK:2 * RANK], b_gate_bwd=jnp.sum(gbp_b, axis=0)[0:1],
        gla_norm=g_gla[0:1], w_out_conv=g_oc, w_out_gla=g_og, w_merge_out=g_out, norm_post=stats[0:1])
    return stats[1:2], grad_x, grads


MATRICES = ("w_out_conv", "w_out_gla", "w_merge_out")
SMALL_SHARDED = ("meta_tokens", "conv_w", "w_gate_fwd", "w_gate_bwd")
REPLICATED = ("norm_pre", "b_gate_fwd", "b_gate_bwd", "gla_norm", "norm_post")
NAMES = ("meta_tokens", "norm_pre", "w_in", "conv_w", "w_gate_fwd", "b_gate_fwd", "w_gate_bwd", "b_gate_bwd", "gla_norm",
         "w_out_conv", "w_out_gla", "w_merge_out", "norm_post")


def kernel(x, meta_tokens, norm_pre, w_in, conv_w, w_gate_fwd, b_gate_fwd, w_gate_bwd, b_gate_bwd, gla_norm, w_out_conv, w_out_gla, w_merge_out, norm_post, loss_target, m_meta_tokens, m_norm_pre, m_w_in, m_conv_w, m_w_gate_fwd, m_b_gate_fwd, m_w_gate_bwd, m_b_gate_bwd, m_gla_norm, m_w_out_conv, m_w_out_gla, m_w_merge_out, m_norm_post, v_meta_tokens, v_norm_pre, v_w_in, v_conv_w, v_w_gate_fwd, v_b_gate_fwd, v_w_gate_bwd, v_b_gate_bwd, v_gla_norm, v_w_out_conv, v_w_out_gla, v_w_merge_out, v_norm_post):
    w = dict(meta_tokens=meta_tokens, norm_pre=norm_pre, w_in=w_in[0], conv_w=conv_w, w_gate_fwd=w_gate_fwd,
             b_gate_fwd=b_gate_fwd, w_gate_bwd=w_gate_bwd, b_gate_bwd=b_gate_bwd, gla_norm=gla_norm,
             w_out_conv=w_out_conv[0], w_out_gla=w_out_gla[0], w_merge_out=w_merge_out[0], norm_post=norm_post)
    m = dict(meta_tokens=m_meta_tokens, norm_pre=m_norm_pre, w_in=m_w_in[0], conv_w=m_conv_w, w_gate_fwd=m_w_gate_fwd,
             b_gate_fwd=m_b_gate_fwd, w_gate_bwd=m_w_gate_bwd, b_gate_bwd=m_b_gate_bwd, gla_norm=m_gla_norm,
             w_out_conv=m_w_out_conv[0], w_out_gla=m_w_out_gla[0], w_merge_out=m_w_merge_out[0], norm_post=m_norm_post)
    v = dict(meta_tokens=v_meta_tokens, norm_pre=v_norm_pre, w_in=v_w_in[0], conv_w=v_conv_w, w_gate_fwd=v_w_gate_fwd,
             b_gate_fwd=v_b_gate_fwd, w_gate_bwd=v_w_gate_bwd, b_gate_bwd=v_b_gate_bwd, gla_norm=v_gla_norm,
             w_out_conv=v_w_out_conv[0], w_out_gla=v_w_out_gla[0], w_merge_out=v_w_merge_out[0], norm_post=v_norm_post)
    d = x.shape[-1]

    dm = _Dims(*x.shape)
    me = 4 * lax.axis_index("x") + 2 * lax.axis_index("y") + lax.axis_index("c")
    wt_shards, *small_all, u = _gather_two_level(
        [_pad_shard(w["w_in"].T.astype(BF16), me)] + [w[n] for n in SMALL_SHARDED], "gather_weights",
        _prenorm_tokens_side(x, norm_pre, dm))
    started, late_weights = _exchange_start([w[n].astype(BF16) for n in MATRICES], [], small_all[0], "gather_out_weights_start")
    small = {n: _join_column_shards(p) for n, p in zip(SMALL_SHARDED, small_all)}
    small["meta_tokens"] = small["meta_tokens"] + started

    def out_weights(after):
        return tuple(a.reshape(-1, d) for a in _exchange_wait(late_weights, after, "gather_out_weights_wait"))

    pending = []

    def on_matrix_grads(g):
        blocks = [t.reshape(N_DEV, -1, d) if t.shape[0] % (N_DEV * BF16_TILE_ROWS) == 0 else (t, t.shape[0] // N_DEV)
                  for t in g.values()]
        token, state = _exchange_start([], blocks, None, "exchange_grads_start_" + "_".join(g))
        pending.append((tuple(g), state))
        return token

    sq_err_cols, grad_x, grads = _local_step(
        x, loss_target, small["meta_tokens"], norm_pre, u, wt_shards, small["conv_w"], small["w_gate_fwd"], b_gate_fwd,
        small["w_gate_bwd"], b_gate_bwd, gla_norm, out_weights, norm_post, on_matrix_grads)
    received = {}
    for names, state in pending:
        received.update(zip(names, _exchange_wait(state, grad_x, "exchange_grads_wait_" + "_".join(names))))

    exchanged = _exchange([grads[n] for n in REPLICATED] + [sq_err_cols],
                          [_column_shards(grads[n], w[n].shape) for n in SMALL_SHARDED], "exchange_small_grads")
    small_recv = exchanged[:len(REPLICATED)] + exchanged[len(REPLICATED) + 1:]
    loss = 0.5 / d * jnp.sum(exchanged[len(REPLICATED)])

    rows_apart = lambda a: jnp.transpose(a, (2, 0, 1))
    results = {"w_in": [jnp.transpose(r, (1, 2, 0)) for r in _adamw(received["w_in"], rows_apart(w_in), rows_apart(m_w_in),
                                                                     rows_apart(v_w_in), "adamw_w_in", by_columns=True)]}
    small_names = REPLICATED + SMALL_SHARDED + MATRICES
    small_recv = list(small_recv) + [received[n] for n in MATRICES]
    results.update(zip(small_names, _adamw_small([(p, w[n], m[n], v[n]) for n, p in zip(small_names, small_recv)], "adamw_small")))
    for n in MATRICES:
        results[n] = [r[None] for r in results[n]]
    return (loss, grad_x, *[results[n][i] for i in range(4) for n in NAMES])
```
